```python
import math
import jax, jax.numpy as jnp
from jax import lax
import numpy as np

D_MODEL = 1024
BATCH = 8
SEQ = 2048
DEPTH = 4

HEAD_DIM = 64
ATT_HEADS = D_MODEL // HEAD_DIM
ATT_WIDTH = ATT_HEADS * HEAD_DIM
ROT_DIM = HEAD_DIM // 4
ROPE_THETA = 500000.0
DILATED_PATTERNS = ((128, 1), (512, 4), (2048, 16))
MAX_REACH = 2048
Q_BLOCK = 128
SSM_WIDTH = D_MODEL // 2
SSM_GROUP = 16
SSM_GROUPS = SSM_WIDTH // SSM_GROUP
SSM_STATE = 64
POOL_WIDTH = 2 * D_MODEL
POOL_WINDOWS = (2, 4, 8, 16)
POOL_GROUP = POOL_WIDTH // len(POOL_WINDOWS)
EVEN_IN = 4 * ATT_WIDTH + 2 * SSM_WIDTH
EVEN_OUT = ATT_WIDTH + SSM_WIDTH
ODD_IN = 2 * POOL_WIDTH
N_EVEN = (DEPTH + 1) // 2
N_ODD = DEPTH // 2
RMS_EPS = 1e-6

kernel_name = "hybrid_dilated_attn_s5_pool_block"


def rmsnorm(x, gain):
    xf = x.astype(jnp.float32)
    y = xf * lax.rsqrt(jnp.mean(xf * xf, axis=-1, keepdims=True) + RMS_EPS)
    return (y * gain.astype(jnp.float32)).astype(x.dtype)


def apply_partial_rope(x, pos):
    half = ROT_DIM // 2
    inv_freq = ROPE_THETA ** (-jnp.arange(0, ROT_DIM, 2, dtype=jnp.float32) / ROT_DIM)
    ang = pos.astype(jnp.float32)[:, None] * inv_freq[None, :]
    cos = jnp.cos(ang)[None, :, None, :]
    sin = jnp.sin(ang)[None, :, None, :]
    xr = x[..., :ROT_DIM].astype(jnp.float32)
    x1, x2 = xr[..., :half], xr[..., half:]
    rot = jnp.concatenate([x1 * cos - x2 * sin, x2 * cos + x1 * sin], axis=-1)
    return jnp.concatenate([rot.astype(x.dtype), x[..., ROT_DIM:]], axis=-1)


def dilated_offsets():
    return np.concatenate([np.arange(w // d + 1) * d for w, d in DILATED_PATTERNS]).astype(np.int32)


def dilated_attention(q, k, v):
    b, s, h, dh = q.shape
    offsets = jnp.asarray(dilated_offsets())
    qh = q.transpose(0, 2, 1, 3)
    pad = ((0, 0), (0, 0), (MAX_REACH, 0), (0, 0))
    k_pad = jnp.pad(k.transpose(0, 2, 1, 3), pad)
    v_pad = jnp.pad(v.transpose(0, 2, 1, 3), pad)

    def block(bi):
        start = bi * Q_BLOCK
        q_blk = lax.dynamic_slice_in_dim(qh, start, Q_BLOCK, axis=2)
        rel = start + jnp.arange(Q_BLOCK, dtype=jnp.int32)[:, None] - offsets[None, :]
        k_g = k_pad[:, :, rel + MAX_REACH]
        v_g = v_pad[:, :, rel + MAX_REACH]
        scores = jnp.einsum('bhqd,bhqkd->bhqk', q_blk, k_g).astype(jnp.float32)
        scores = jnp.where(rel >= 0, scores, -jnp.inf)
        p = jax.nn.softmax(scores, axis=-1)
        return jnp.einsum('bhqk,bhqkd->bhqd', p.astype(v.dtype), v_g)

    out = lax.map(block, jnp.arange(s // Q_BLOCK, dtype=jnp.int32))
    return out.transpose(1, 0, 3, 2, 4).reshape(b, s, h * dh)


def s5_ssm(u, a_re, a_im, log_dt, b_re, b_im, c_re, c_im, d_skip, glu_w, glu_b):
    bsz, s, _ = u.shape
    f32 = jnp.float32
    uf = u.astype(f32)
    ug = uf.reshape(bsz, s, SSM_GROUPS, SSM_GROUP)
    lam = lax.complex(a_re.astype(f32), a_im.astype(f32))
    dt = jnp.exp(log_dt.astype(f32))[:, None]
    lam_bar = jnp.exp(lam * dt)
    b_mat = lax.complex(b_re.astype(f32), b_im.astype(f32))
    b_bar = ((lam_bar - 1.0) / lam)[..., None] * b_mat
    bu = jnp.einsum('bsgh,gph->bsgp', ug.astype(jnp.complex64), b_bar)
    a_seq = jnp.broadcast_to(lam_bar, bu.shape)

    def combine(left, right):
        a_l, s_l = left
        a_r, s_r = right
        return a_r * a_l, a_r * s_l + s_r

    _, states = lax.associative_scan(combine, (a_seq, bu), axis=1)
    c_mat = lax.complex(c_re.astype(f32), c_im.astype(f32))
    y = jnp.real(jnp.einsum('bsgp,ghp->bsgh', states, c_mat)).reshape(bsz, s, SSM_WIDTH)
    y = jax.nn.gelu(y + d_skip.astype(f32) * uf)
    y = y * jax.nn.sigmoid(y @ glu_w.astype(f32) + glu_b.astype(f32))
    return y.astype(u.dtype)


def multiscale_pool(u, pool_w, pool_scale):
    bsz, s, _ = u.shape
    ug = u.astype(jnp.float32).reshape(bsz, s, len(POOL_WINDOWS), POOL_GROUP)
    count_base = jnp.arange(1, s + 1, dtype=jnp.float32)[None, :, None]
    outs = []
    for g, w in enumerate(POOL_WINDOWS):
        ch = ug[:, :, g]
        cs = jnp.cumsum(ch, axis=1)
        lagged = jnp.pad(cs, ((0, 0), (w, 0), (0, 0)))[:, :s]
        mean = (cs - lagged) / jnp.minimum(count_base, float(w))
        outs.append(mean - ch)
    mixed = jnp.stack(outs, axis=2)
    y = jnp.einsum('bsgc,gcd->bsgd', mixed, pool_w.astype(jnp.float32)).reshape(bsz, s, POOL_WIDTH)
    return (y * pool_scale.astype(jnp.float32)).astype(u.dtype)


def even_mixer(h, w_in, w_out, a_re, a_im, log_dt, b_re, b_im, c_re, c_im, d_skip, glu_w, glu_b):
    bsz, s, _ = h.shape
    proj = h @ w_in
    cuts = [ATT_WIDTH, 2 * ATT_WIDTH, 3 * ATT_WIDTH, 4 * ATT_WIDTH, 4 * ATT_WIDTH + SSM_WIDTH]
    q, k, v, g_att, u_ssm, g_ssm = jnp.split(proj, cuts, axis=-1)
    pos = jnp.arange(s, dtype=jnp.int32)
    q = apply_partial_rope(q.reshape(bsz, s, ATT_HEADS, HEAD_DIM), pos) * (HEAD_DIM ** -0.5)
    k = apply_partial_rope(k.reshape(bsz, s, ATT_HEADS, HEAD_DIM), pos)
    v = v.reshape(bsz, s, ATT_HEADS, HEAD_DIM)
    att = dilated_attention(q, k, v)
    ssm = s5_ssm(u_ssm, a_re, a_im, log_dt, b_re, b_im, c_re, c_im, d_skip, glu_w, glu_b)
    merged = jnp.concatenate([att * jax.nn.silu(g_att), ssm * jax.nn.silu(g_ssm)], axis=-1)
    return merged @ w_out


def odd_mixer(h, w_in, pool_w, pool_scale, w_out):
    u, gate = jnp.split(h @ w_in, [POOL_WIDTH], axis=-1)
    y = multiscale_pool(u, pool_w, pool_scale)
    return (y * jax.nn.silu(gate)) @ w_out


def _fwd_setup_inputs(seed: int = 0) -> dict:
    key = jax.random.key(seed)
    ks = jax.random.split(key, 24)
    f32 = jnp.float32

    def nrm(k, shape, scale):
        return jax.random.normal(k, shape, f32) * scale

    n_idx = jnp.arange(SSM_STATE, dtype=f32)
    return {
        "x": jax.random.normal(ks[0], (BATCH, SEQ, D_MODEL), f32),
        "pre_norm": 1.0 + nrm(ks[1], (DEPTH, D_MODEL), 0.1),
        "post_norm": 1.0 + nrm(ks[2], (DEPTH, D_MODEL), 0.1),
        "even_w_in": nrm(ks[3], (N_EVEN, D_MODEL, EVEN_IN), D_MODEL ** -0.5),
        "even_w_out": nrm(ks[4], (N_EVEN, EVEN_OUT, D_MODEL), EVEN_OUT ** -0.5),
        "ssm_a_re": -0.5 + nrm(ks[5], (N_EVEN, SSM_GROUPS, SSM_STATE), 0.01),
        "ssm_a_im": math.pi * n_idx + nrm(ks[6], (N_EVEN, SSM_GROUPS, SSM_STATE), 0.01),
        "ssm_log_dt": jax.random.uniform(ks[7], (N_EVEN, SSM_GROUPS), f32, math.log(1e-3), math.log(1e-1)),
        "ssm_b_re": nrm(ks[8], (N_EVEN, SSM_GROUPS, SSM_STATE, SSM_GROUP), (2 * SSM_GROUP) ** -0.5),
        "ssm_b_im": nrm(ks[9], (N_EVEN, SSM_GROUPS, SSM_STATE, SSM_GROUP), (2 * SSM_GROUP) ** -0.5),
        "ssm_c_re": nrm(ks[10], (N_EVEN, SSM_GROUPS, SSM_GROUP, SSM_STATE), (2 * SSM_STATE) ** -0.5),
        "ssm_c_im": nrm(ks[11], (N_EVEN, SSM_GROUPS, SSM_GROUP, SSM_STATE), (2 * SSM_STATE) ** -0.5),
        "ssm_d": nrm(ks[12], (N_EVEN, SSM_WIDTH), 1.0),
        "ssm_glu_w": nrm(ks[13], (N_EVEN, SSM_WIDTH, SSM_WIDTH), SSM_WIDTH ** -0.5),
        "ssm_glu_b": nrm(ks[14], (N_EVEN, SSM_WIDTH), 0.02),
        "odd_w_in": nrm(ks[15], (N_ODD, D_MODEL, ODD_IN), D_MODEL ** -0.5),
        "pool_w": nrm(ks[16], (N_ODD, len(POOL_WINDOWS), POOL_GROUP, POOL_GROUP), POOL_GROUP ** -0.5),
        "pool_scale": 1.0 + nrm(ks[17], (N_ODD, POOL_WIDTH), 0.1),
        "odd_w_out": nrm(ks[18], (N_ODD, POOL_WIDTH, D_MODEL), POOL_WIDTH ** -0.5),
    }


def _fwd_reference(x, pre_norm, post_norm, even_w_in, even_w_out, ssm_a_re, ssm_a_im, ssm_log_dt,
              ssm_b_re, ssm_b_im, ssm_c_re, ssm_c_im, ssm_d, ssm_glu_w, ssm_glu_b,
              odd_w_in, pool_w, pool_scale, odd_w_out):
    for layer in range(DEPTH):
        h = rmsnorm(x, pre_norm[layer])
        i = layer // 2
        if layer % 2 == 0:
            y = even_mixer(h, even_w_in[i], even_w_out[i], ssm_a_re[i], ssm_a_im[i], ssm_log_dt[i],
                           ssm_b_re[i], ssm_b_im[i], ssm_c_re[i], ssm_c_im[i], ssm_d[i],
                           ssm_glu_w[i], ssm_glu_b[i])
        else:
            y = odd_mixer(h, odd_w_in[i], pool_w[i], pool_scale[i], odd_w_out[i])
        x = x + rmsnorm(y, post_norm[layer])
    return x


import jax as _jax
import jax.numpy as _jnp

TWIN_FORMAT = 'train_step'
FWD_PARAMS = ['x', 'pre_norm', 'post_norm', 'even_w_in', 'even_w_out', 'ssm_a_re', 'ssm_a_im', 'ssm_log_dt', 'ssm_b_re', 'ssm_b_im', 'ssm_c_re', 'ssm_c_im', 'ssm_d', 'ssm_glu_w', 'ssm_glu_b', 'odd_w_in', 'pool_w', 'pool_scale', 'odd_w_out']
TWIN_WEIGHTS = ['pre_norm', 'post_norm', 'even_w_in', 'even_w_out', 'ssm_a_re', 'ssm_a_im', 'ssm_log_dt', 'ssm_b_re', 'ssm_b_im', 'ssm_c_re', 'ssm_c_im', 'ssm_d', 'ssm_glu_w', 'ssm_glu_b', 'odd_w_in', 'pool_w', 'pool_scale', 'odd_w_out']
TWIN_DIFF_INPUT = 'x'
TWIN_INPUTS = ['x', 'pre_norm', 'post_norm', 'even_w_in', 'even_w_out', 'ssm_a_re', 'ssm_a_im', 'ssm_log_dt', 'ssm_b_re', 'ssm_b_im', 'ssm_c_re', 'ssm_c_im', 'ssm_d', 'ssm_glu_w', 'ssm_glu_b', 'odd_w_in', 'pool_w', 'pool_scale', 'odd_w_out', 'loss_target', 'm_pre_norm', 'm_post_norm', 'm_even_w_in', 'm_even_w_out', 'm_ssm_a_re', 'm_ssm_a_im', 'm_ssm_log_dt', 'm_ssm_b_re', 'm_ssm_b_im', 'm_ssm_c_re', 'm_ssm_c_im', 'm_ssm_d', 'm_ssm_glu_w', 'm_ssm_glu_b', 'm_odd_w_in', 'm_pool_w', 'm_pool_scale', 'm_odd_w_out', 'v_pre_norm', 'v_post_norm', 'v_even_w_in', 'v_even_w_out', 'v_ssm_a_re', 'v_ssm_a_im', 'v_ssm_log_dt', 'v_ssm_b_re', 'v_ssm_b_im', 'v_ssm_c_re', 'v_ssm_c_im', 'v_ssm_d', 'v_ssm_glu_w', 'v_ssm_glu_b', 'v_odd_w_in', 'v_pool_w', 'v_pool_scale', 'v_odd_w_out']
TWIN_OUTPUTS = ['loss', 'grad_x', 'grad_pre_norm', 'grad_post_norm', 'grad_even_w_in', 'grad_even_w_out', 'grad_ssm_a_re', 'grad_ssm_a_im', 'grad_ssm_log_dt', 'grad_ssm_b_re', 'grad_ssm_b_im', 'grad_ssm_c_re', 'grad_ssm_c_im', 'grad_ssm_d', 'grad_ssm_glu_w', 'grad_ssm_glu_b', 'grad_odd_w_in', 'grad_pool_w', 'grad_pool_scale', 'grad_odd_w_out', 'delta_pre_norm', 'delta_post_norm', 'delta_even_w_in', 'delta_even_w_out', 'delta_ssm_a_re', 'delta_ssm_a_im', 'delta_ssm_log_dt', 'delta_ssm_b_re', 'delta_ssm_b_im', 'delta_ssm_c_re', 'delta_ssm_c_im', 'delta_ssm_d', 'delta_ssm_glu_w', 'delta_ssm_glu_b', 'delta_odd_w_in', 'delta_pool_w', 'delta_pool_scale', 'delta_odd_w_out', 'new_m_pre_norm', 'new_m_post_norm', 'new_m_even_w_in', 'new_m_even_w_out', 'new_m_ssm_a_re', 'new_m_ssm_a_im', 'new_m_ssm_log_dt', 'new_m_ssm_b_re', 'new_m_ssm_b_im', 'new_m_ssm_c_re', 'new_m_ssm_c_im', 'new_m_ssm_d', 'new_m_ssm_glu_w', 'new_m_ssm_glu_b', 'new_m_odd_w_in', 'new_m_pool_w', 'new_m_pool_scale', 'new_m_odd_w_out', 'new_v_pre_norm', 'new_v_post_norm', 'new_v_even_w_in', 'new_v_even_w_out', 'new_v_ssm_a_re', 'new_v_ssm_a_im', 'new_v_ssm_log_dt', 'new_v_ssm_b_re', 'new_v_ssm_b_im', 'new_v_ssm_c_re', 'new_v_ssm_c_im', 'new_v_ssm_d', 'new_v_ssm_glu_w', 'new_v_ssm_glu_b', 'new_v_odd_w_in', 'new_v_pool_w', 'new_v_pool_scale', 'new_v_odd_w_out']
TWIN_LEAF_KINDS = {'loss': 'loss', 'grad_x': 'grad_x', 'grad_pre_norm': 'grad_w', 'grad_post_norm': 'grad_w', 'grad_even_w_in': 'grad_w', 'grad_even_w_out': 'grad_w', 'grad_ssm_a_re': 'grad_w', 'grad_ssm_a_im': 'grad_w', 'grad_ssm_log_dt': 'grad_w', 'grad_ssm_b_re': 'grad_w', 'grad_ssm_b_im': 'grad_w', 'grad_ssm_c_re': 'grad_w', 'grad_ssm_c_im': 'grad_w', 'grad_ssm_d': 'grad_w', 'grad_ssm_glu_w': 'grad_w', 'grad_ssm_glu_b': 'grad_w', 'grad_odd_w_in': 'grad_w', 'grad_pool_w': 'grad_w', 'grad_pool_scale': 'grad_w', 'grad_odd_w_out': 'grad_w', 'delta_pre_norm': 'delta_w', 'delta_post_norm': 'delta_w', 'delta_even_w_in': 'delta_w', 'delta_even_w_out': 'delta_w', 'delta_ssm_a_re': 'delta_w', 'delta_ssm_a_im': 'delta_w', 'delta_ssm_log_dt': 'delta_w', 'delta_ssm_b_re': 'delta_w', 'delta_ssm_b_im': 'delta_w', 'delta_ssm_c_re': 'delta_w', 'delta_ssm_c_im': 'delta_w', 'delta_ssm_d': 'delta_w', 'delta_ssm_glu_w': 'delta_w', 'delta_ssm_glu_b': 'delta_w', 'delta_odd_w_in': 'delta_w', 'delta_pool_w': 'delta_w', 'delta_pool_scale': 'delta_w', 'delta_odd_w_out': 'delta_w', 'new_m_pre_norm': 'new_m', 'new_m_post_norm': 'new_m', 'new_m_even_w_in': 'new_m', 'new_m_even_w_out': 'new_m', 'new_m_ssm_a_re': 'new_m', 'new_m_ssm_a_im': 'new_m', 'new_m_ssm_log_dt': 'new_m', 'new_m_ssm_b_re': 'new_m', 'new_m_ssm_b_im': 'new_m', 'new_m_ssm_c_re': 'new_m', 'new_m_ssm_c_im': 'new_m', 'new_m_ssm_d': 'new_m', 'new_m_ssm_glu_w': 'new_m', 'new_m_ssm_glu_b': 'new_m', 'new_m_odd_w_in': 'new_m', 'new_m_pool_w': 'new_m', 'new_m_pool_scale': 'new_m', 'new_m_odd_w_out': 'new_m', 'new_v_pre_norm': 'new_v', 'new_v_post_norm': 'new_v', 'new_v_even_w_in': 'new_v', 'new_v_even_w_out': 'new_v', 'new_v_ssm_a_re': 'new_v', 'new_v_ssm_a_im': 'new_v', 'new_v_ssm_log_dt': 'new_v', 'new_v_ssm_b_re': 'new_v', 'new_v_ssm_b_im': 'new_v', 'new_v_ssm_c_re': 'new_v', 'new_v_ssm_c_im': 'new_v', 'new_v_ssm_d': 'new_v', 'new_v_ssm_glu_w': 'new_v', 'new_v_ssm_glu_b': 'new_v', 'new_v_odd_w_in': 'new_v', 'new_v_pool_w': 'new_v', 'new_v_pool_scale': 'new_v', 'new_v_odd_w_out': 'new_v'}


def _forward(args):
    return _fwd_reference(*[args[k] for k in FWD_PARAMS])


def _output_shape():
    out = _jax.eval_shape(lambda: _forward(_fwd_setup_inputs(0)))
    return out.shape, out.dtype

N_MICROBATCH = 1
ADAM_LR = 0.001
ADAM_B1 = 0.9
ADAM_B2 = 0.999
ADAM_EPS = 1e-08
ADAM_WD = 0.01
ADAM_STEP = 10
PER_EXAMPLE_BATCH_AXIS = {'x': 0, 'loss_target': 0}
SHARED_INPUTS = []
_WEIGHT_DTYPES = {'pre_norm': _jnp.float32, 'post_norm': _jnp.float32, 'even_w_in': _jnp.float32, 'even_w_out': _jnp.float32, 'ssm_a_re': _jnp.float32, 'ssm_a_im': _jnp.float32, 'ssm_log_dt': _jnp.float32, 'ssm_b_re': _jnp.float32, 'ssm_b_im': _jnp.float32, 'ssm_c_re': _jnp.float32, 'ssm_c_im': _jnp.float32, 'ssm_d': _jnp.float32, 'ssm_glu_w': _jnp.float32, 'ssm_glu_b': _jnp.float32, 'odd_w_in': _jnp.float32, 'pool_w': _jnp.float32, 'pool_scale': _jnp.float32, 'odd_w_out': _jnp.float32}
MOMENT_SCALE = {'pre_norm': 8.317401e-01, 'post_norm': 1.609392e+01, 'even_w_in': 4.204185e-01, 'even_w_out': 7.925968e-01, 'ssm_a_re': 3.904293e-02, 'ssm_a_im': 4.303031e-02, 'ssm_log_dt': 4.081416e+01, 'ssm_b_re': 2.560700e-02, 'ssm_b_im': 2.661979e-02, 'ssm_c_re': 5.153774e-02, 'ssm_c_im': 5.163506e-02, 'ssm_d': 1.283932e+00, 'ssm_glu_w': 2.372153e-01, 'ssm_glu_b': 5.471952e-01, 'odd_w_in': 3.221141e-01, 'pool_w': 3.420976e-01, 'pool_scale': 3.533536e-01, 'odd_w_out': 4.953858e-01}


def _to_microbatches(a, axis):
    t = _jnp.moveaxis(a, axis, 0)
    t = t.reshape((N_MICROBATCH, t.shape[0] // N_MICROBATCH) + t.shape[1:])
    return _jnp.moveaxis(t, 1, axis + 1)


def setup_inputs(seed: int = 0) -> dict:
    inp = _fwd_setup_inputs(seed)
    key = _jax.random.fold_in(_jax.random.key(seed), 7919)
    shape, _ = _output_shape()
    out = dict(inp)
    out["loss_target"] = _jax.random.normal(_jax.random.fold_in(key, 0), shape, _jnp.float32)
    for i, name in enumerate(TWIN_WEIGHTS):
        w = inp[name].astype(_jnp.float32)
        if MOMENT_SCALE is None:
            s = _jnp.sqrt(_jnp.mean(_jnp.square(w)) + 1e-30)
        else:
            s = MOMENT_SCALE[name]
        km, kv = _jax.random.split(_jax.random.fold_in(key, i + 1))
        out[name] = w
        out["m_" + name] = s * _jax.random.normal(km, w.shape, _jnp.float32)
        out["v_" + name] = (s * s) * _jax.random.uniform(kv, w.shape, _jnp.float32, 0.5, 1.5)
    if N_MICROBATCH > 1:
        for name, axis in PER_EXAMPLE_BATCH_AXIS.items():
            out[name] = _to_microbatches(out[name], axis)
    return {'x': out['x'], 'pre_norm': out['pre_norm'], 'post_norm': out['post_norm'], 'even_w_in': out['even_w_in'], 'even_w_out': out['even_w_out'], 'ssm_a_re': out['ssm_a_re'], 'ssm_a_im': out['ssm_a_im'], 'ssm_log_dt': out['ssm_log_dt'], 'ssm_b_re': out['ssm_b_re'], 'ssm_b_im': out['ssm_b_im'], 'ssm_c_re': out['ssm_c_re'], 'ssm_c_im': out['ssm_c_im'], 'ssm_d': out['ssm_d'], 'ssm_glu_w': out['ssm_glu_w'], 'ssm_glu_b': out['ssm_glu_b'], 'odd_w_in': out['odd_w_in'], 'pool_w': out['pool_w'], 'pool_scale': out['pool_scale'], 'odd_w_out': out['odd_w_out'], 'loss_target': out['loss_target'], 'm_pre_norm': out['m_pre_norm'], 'm_post_norm': out['m_post_norm'], 'm_even_w_in': out['m_even_w_in'], 'm_even_w_out': out['m_even_w_out'], 'm_ssm_a_re': out['m_ssm_a_re'], 'm_ssm_a_im': out['m_ssm_a_im'], 'm_ssm_log_dt': out['m_ssm_log_dt'], 'm_ssm_b_re': out['m_ssm_b_re'], 'm_ssm_b_im': out['m_ssm_b_im'], 'm_ssm_c_re': out['m_ssm_c_re'], 'm_ssm_c_im': out['m_ssm_c_im'], 'm_ssm_d': out['m_ssm_d'], 'm_ssm_glu_w': out['m_ssm_glu_w'], 'm_ssm_glu_b': out['m_ssm_glu_b'], 'm_odd_w_in': out['m_odd_w_in'], 'm_pool_w': out['m_pool_w'], 'm_pool_scale': out['m_pool_scale'], 'm_odd_w_out': out['m_odd_w_out'], 'v_pre_norm': out['v_pre_norm'], 'v_post_norm': out['v_post_norm'], 'v_even_w_in': out['v_even_w_in'], 'v_even_w_out': out['v_even_w_out'], 'v_ssm_a_re': out['v_ssm_a_re'], 'v_ssm_a_im': out['v_ssm_a_im'], 'v_ssm_log_dt': out['v_ssm_log_dt'], 'v_ssm_b_re': out['v_ssm_b_re'], 'v_ssm_b_im': out['v_ssm_b_im'], 'v_ssm_c_re': out['v_ssm_c_re'], 'v_ssm_c_im': out['v_ssm_c_im'], 'v_ssm_d': out['v_ssm_d'], 'v_ssm_glu_w': out['v_ssm_glu_w'], 'v_ssm_glu_b': out['v_ssm_glu_b'], 'v_odd_w_in': out['v_odd_w_in'], 'v_pool_w': out['v_pool_w'], 'v_pool_scale': out['v_pool_scale'], 'v_odd_w_out': out['v_odd_w_out']}


def _loss(weights, diff, rest, loss_target):
    with _jax.named_scope("forward"):
        args = {**rest, TWIN_DIFF_INPUT: diff, **{k: w.astype(_WEIGHT_DTYPES[k]) for k, w in weights.items()}}
        y = _forward(args)
    with _jax.named_scope("loss_head"):
        err = _jnp.square(y.astype(_jnp.float32) - loss_target)
        return 0.5 * _jnp.sum(_jnp.mean(err, axis=-1)) if err.ndim else 0.5 * err


def _adamw(w, g, m, v):
    m = ADAM_B1 * m + (1.0 - ADAM_B1) * g
    v = ADAM_B2 * v + (1.0 - ADAM_B2) * _jnp.square(g)
    m_hat = m / (1.0 - ADAM_B1 ** ADAM_STEP)
    v_hat = v / (1.0 - ADAM_B2 ** ADAM_STEP)
    delta = -ADAM_LR * (m_hat / (_jnp.sqrt(v_hat) + ADAM_EPS) + ADAM_WD * w)
    return delta, m, v


def reference(x, pre_norm, post_norm, even_w_in, even_w_out, ssm_a_re, ssm_a_im, ssm_log_dt, ssm_b_re, ssm_b_im, ssm_c_re, ssm_c_im, ssm_d, ssm_glu_w, ssm_glu_b, odd_w_in, pool_w, pool_scale, odd_w_out, loss_target, m_pre_norm, m_post_norm, m_even_w_in, m_even_w_out, m_ssm_a_re, m_ssm_a_im, m_ssm_log_dt, m_ssm_b_re, m_ssm_b_im, m_ssm_c_re, m_ssm_c_im, m_ssm_d, m_ssm_glu_w, m_ssm_glu_b, m_odd_w_in, m_pool_w, m_pool_scale, m_odd_w_out, v_pre_norm, v_post_norm, v_even_w_in, v_even_w_out, v_ssm_a_re, v_ssm_a_im, v_ssm_log_dt, v_ssm_b_re, v_ssm_b_im, v_ssm_c_re, v_ssm_c_im, v_ssm_d, v_ssm_glu_w, v_ssm_glu_b, v_odd_w_in, v_pool_w, v_pool_scale, v_odd_w_out):
    given = dict(x=x, pre_norm=pre_norm, post_norm=post_norm, even_w_in=even_w_in, even_w_out=even_w_out, ssm_a_re=ssm_a_re, ssm_a_im=ssm_a_im, ssm_log_dt=ssm_log_dt, ssm_b_re=ssm_b_re, ssm_b_im=ssm_b_im, ssm_c_re=ssm_c_re, ssm_c_im=ssm_c_im, ssm_d=ssm_d, ssm_glu_w=ssm_glu_w, ssm_glu_b=ssm_glu_b, odd_w_in=odd_w_in, pool_w=pool_w, pool_scale=pool_scale, odd_w_out=odd_w_out, loss_target=loss_target, m_pre_norm=m_pre_norm, m_post_norm=m_post_norm, m_even_w_in=m_even_w_in, m_even_w_out=m_even_w_out, m_ssm_a_re=m_ssm_a_re, m_ssm_a_im=m_ssm_a_im, m_ssm_log_dt=m_ssm_log_dt, m_ssm_b_re=m_ssm_b_re, m_ssm_b_im=m_ssm_b_im, m_ssm_c_re=m_ssm_c_re, m_ssm_c_im=m_ssm_c_im, m_ssm_d=m_ssm_d, m_ssm_glu_w=m_ssm_glu_w, m_ssm_glu_b=m_ssm_glu_b, m_odd_w_in=m_odd_w_in, m_pool_w=m_pool_w, m_pool_scale=m_pool_scale, m_odd_w_out=m_odd_w_out, v_pre_norm=v_pre_norm, v_post_norm=v_post_norm, v_even_w_in=v_even_w_in, v_even_w_out=v_even_w_out, v_ssm_a_re=v_ssm_a_re, v_ssm_a_im=v_ssm_a_im, v_ssm_log_dt=v_ssm_log_dt, v_ssm_b_re=v_ssm_b_re, v_ssm_b_im=v_ssm_b_im, v_ssm_c_re=v_ssm_c_re, v_ssm_c_im=v_ssm_c_im, v_ssm_d=v_ssm_d, v_ssm_glu_w=v_ssm_glu_w, v_ssm_glu_b=v_ssm_glu_b, v_odd_w_in=v_odd_w_in, v_pool_w=v_pool_w, v_pool_scale=v_pool_scale, v_odd_w_out=v_odd_w_out)
    weights = {n: given[n] for n in TWIN_WEIGHTS}
    shared = {n: given[n] for n in SHARED_INPUTS}
    per_example = {n: given[n] for n in ['x']}
    grad_fn = _jax.value_and_grad(_loss, argnums=(0, 1))

    def one_microbatch(ex, loss_target):
        ex = dict(ex)
        diff = ex.pop(TWIN_DIFF_INPUT)
        return grad_fn(weights, diff, {**shared, **ex}, loss_target)

    if N_MICROBATCH == 1:
        loss, (grad_w, grad_x) = one_microbatch(per_example, given["loss_target"])
    else:
        def body(carry, xs):
            loss_sum, grad_sum = carry
            l_k, (gw_k, gx_k) = one_microbatch(xs[0], xs[1])
            with _jax.named_scope("update"):
                return (loss_sum + l_k, _jax.tree.map(_jnp.add, grad_sum, gw_k)), gx_k

        init = (_jnp.zeros((), _jnp.float32), _jax.tree.map(_jnp.zeros_like, weights))
        (loss, grad_w), grad_x = _jax.lax.scan(body, init, (per_example, given["loss_target"]))
    with _jax.named_scope("update"):
        delta_w, new_m, new_v = {}, {}, {}
        for n in TWIN_WEIGHTS:
            delta_w[n], new_m[n], new_v[n] = _adamw(weights[n], grad_w[n], given["m_" + n], given["v_" + n])
    return (loss, grad_x, *[grad_w[n] for n in TWIN_WEIGHTS], *[delta_w[n] for n in TWIN_WEIGHTS],
            *[new_m[n] for n in TWIN_WEIGHTS], *[new_v[n] for n in TWIN_WEIGHTS])
```

```python
import functools
import math

import jax
import jax.numpy as jnp
from jax import lax
from jax.experimental import pallas as pl
from jax.experimental.pallas import tpu as pltpu

F32 = jnp.float32
BF16 = jnp.bfloat16
SDS = jax.ShapeDtypeStruct

N_DEV = 8
S = 2048
D = 1024
HEAD_DIM = 64
ROT_DIM = 16
ROPE_THETA = 500000.0
ATT_W = 1024
SSM_W = 512
SSM_GROUPS = 32
SSM_GROUP = 16
SSM_STATE = 64
N_CPLX = SSM_GROUPS * SSM_STATE
POOL_W = 2048
POOL_GROUP = 512
EVEN_IN = 5120
EVEN_OUT = 1536
ODD_IN = 4096
RMS_EPS = 1e-6
LANES = 128
VMEM_LIMIT = 48 * 1024 * 1024

ADAM_LR = 0.001
ADAM_B1 = 0.9
ADAM_B2 = 0.999
ADAM_EPS = 1e-08
ADAM_WD = 0.01
ADAM_STEP = 10

MESH_ID = pl.DeviceIdType.MESH
NN = (((1,), (0,)), ((), ()))
NT = (((1,), (1,)), ((), ()))
TN = (((0,), (0,)), ((), ()))
_DN = {"nn": NN, "nt": NT, "tn": TN}


def _cparams(sem):
    return pltpu.CompilerParams(dimension_semantics=sem, vmem_limit_bytes=VMEM_LIMIT)


def _mm(a, b, mode, out_dtype, tm=512, tn=512, tk=512):
    if mode == "nn":
        (m, k), n = a.shape, b.shape[1]
    elif mode == "nt":
        (m, k), n = a.shape, b.shape[0]
    else:
        (k, m), n = a.shape, b.shape[1]
    tm, tn, tk = min(tm, m), min(tn, n), min(tk, k)
    assert m % tm == 0 and n % tn == 0 and k % tk == 0, (a.shape, b.shape, mode)
    nk = k // tk

    def body(a_ref, b_ref, o_ref, acc_ref):
        kk = pl.program_id(2)

        @pl.when(kk == 0)
        def _():
            acc_ref[...] = jnp.zeros_like(acc_ref)

        acc_ref[...] += lax.dot_general(a_ref[...].astype(BF16), b_ref[...].astype(BF16), _DN[mode],
                                        preferred_element_type=F32)

        @pl.when(kk == nk - 1)
        def _():
            o_ref[...] = acc_ref[...].astype(o_ref.dtype)

    if mode == "nn":
        a_spec = pl.BlockSpec((tm, tk), lambda i, j, kk: (i, kk))
        b_spec = pl.BlockSpec((tk, tn), lambda i, j, kk: (kk, j))
    elif mode == "nt":
        a_spec = pl.BlockSpec((tm, tk), lambda i, j, kk: (i, kk))
        b_spec = pl.BlockSpec((tn, tk), lambda i, j, kk: (j, kk))
    else:
        a_spec = pl.BlockSpec((tk, tm), lambda i, j, kk: (kk, i))
        b_spec = pl.BlockSpec((tk, tn), lambda i, j, kk: (kk, j))
    return pl.pallas_call(
        body, name=f"mm_{mode}_{m}x{k}x{n}",
        grid=(m // tm, n // tn, nk),
        in_specs=[a_spec, b_spec],
        out_specs=pl.BlockSpec((tm, tn), lambda i, j, kk: (i, j)),
        out_shape=SDS((m, n), out_dtype),
        scratch_shapes=[pltpu.VMEM((tm, tn), F32)],
        compiler_params=_cparams(("parallel", "parallel", "arbitrary")),
    )(a, b)


def _gmm(a, b, mode, out_dtype, tm=512):
    ng, gw = POOL_W // POOL_GROUP, POOL_GROUP
    ns = S // tm
    if mode in ("nn", "nt"):
        def body(a_ref, b_ref, o_ref):
            o_ref[...] = lax.dot_general(a_ref[...].astype(BF16), b_ref[...].astype(BF16), _DN[mode],
                                         preferred_element_type=F32).astype(o_ref.dtype)

        return pl.pallas_call(
            body, name=f"gmm_{mode}", grid=(ng, ns),
            in_specs=[pl.BlockSpec((tm, gw), lambda g, i: (i, g)),
                      pl.BlockSpec((None, gw, gw), lambda g, i: (g, 0, 0))],
            out_specs=pl.BlockSpec((tm, gw), lambda g, i: (i, g)),
            out_shape=SDS((S, POOL_W), out_dtype),
            compiler_params=_cparams(("parallel", "parallel")),
        )(a, b)

    def body_tn(a_ref, b_ref, o_ref, acc_ref):
        i = pl.program_id(1)

        @pl.when(i == 0)
        def _():
            acc_ref[...] = jnp.zeros_like(acc_ref)

        acc_ref[...] += lax.dot_general(a_ref[...].astype(BF16), b_ref[...].astype(BF16), TN,
                                        preferred_element_type=F32)

        @pl.when(i == ns - 1)
        def _():
            o_ref[...] = acc_ref[...].astype(o_ref.dtype)

    return pl.pallas_call(
        body_tn, name="gmm_tn", grid=(ng, ns),
        in_specs=[pl.BlockSpec((tm, gw), lambda g, i: (i, g)),
                  pl.BlockSpec((tm, gw), lambda g, i: (i, g))],
        out_specs=pl.BlockSpec((None, gw, gw), lambda g, i: (g, 0, 0)),
        out_shape=SDS((ng, gw, gw), out_dtype),
        scratch_shapes=[pltpu.VMEM((gw, gw), F32)],
        compiler_params=_cparams(("parallel", "arbitrary")),
    )(a, b)


def _rowwise(fn, inputs, out_defs, acc_defs=(), tm=256, name=None):
    n_in, n_out, n_acc = len(inputs), len(out_defs), len(acc_defs)
    in_specs, args = [], []
    for arr, width, cb in inputs:
        if arr.shape[0] == 1:
            in_specs.append(pl.BlockSpec((1, width), lambda i, cb=cb: (0, cb)))
        else:
            in_specs.append(pl.BlockSpec((tm, width), lambda i, cb=cb: (i, cb)))
        args.append(arr)
    out_shape = [SDS((S, w), dt) for w, dt in out_defs] + [SDS((1, w), F32) for w in acc_defs]
    out_specs = ([pl.BlockSpec((tm, w), lambda i: (i, 0)) for w, _ in out_defs]
                 + [pl.BlockSpec((1, w), lambda i: (0, 0)) for w in acc_defs])

    def kern(*refs):
        vals = [r[...] for r in refs[:n_in]]
        outs, accs = fn(*vals)
        for r, v in zip(refs[n_in:n_in + n_out], outs):
            r[...] = v.astype(r.dtype)
        if n_acc:
            acc_refs = refs[n_in + n_out:]

            @pl.when(pl.program_id(0) == 0)
            def _():
                for r in acc_refs:
                    r[...] = jnp.zeros_like(r)

            for r, v in zip(acc_refs, accs):
                r[...] += jnp.sum(v, axis=0, keepdims=True)

    res = pl.pallas_call(
        kern, name=name, grid=(S // tm,), in_specs=in_specs, out_specs=out_specs, out_shape=out_shape,
        compiler_params=_cparams(("arbitrary",)),
    )(*args)
    return res


def _sigmoid(x):
    return 1.0 / (1.0 + jnp.exp(-x))


def _silu_and_grad(x):
    s = _sigmoid(x)
    return x * s, s * (1.0 + x * (1.0 - s))


_GELU_K = math.sqrt(2.0 / math.pi)
_GELU_C = 0.044715


def _gelu_and_grad(x):
    t = jnp.tanh(_GELU_K * (x + _GELU_C * (x * x * x)))
    cdf = 0.5 * (1.0 + t)
    grad = cdf + 0.5 * x * (1.0 - t * t) * (_GELU_K * (1.0 + 3.0 * _GELU_C * x * x))
    return x * cdf, grad


def _rms(xv, gain):
    r = lax.rsqrt(jnp.mean(xv * xv, axis=-1, keepdims=True) + RMS_EPS)
    return xv * r * gain


def _rms_bwd(dout, xv, gain):
    r = lax.rsqrt(jnp.mean(xv * xv, axis=-1, keepdims=True) + RMS_EPS)
    xhat = xv * r
    dxhat = dout * gain
    dx = r * (dxhat - xhat * jnp.mean(dxhat * xhat, axis=-1, keepdims=True))
    return dx, dout * xhat


def _norm_fwd(x, gain):
    (h,) = _rowwise(lambda xv, g: ((_rms(xv, g),), ()), [(x, D, 0), (gain, D, 0)], [(D, BF16)], name="norm_fwd")
    return h


def _post_fwd(x, y, gain):
    (o,) = _rowwise(lambda xv, yv, g: ((xv + _rms(yv, g),), ()), [(x, D, 0), (y, D, 0), (gain, D, 0)],
                    [(D, F32)], name="post_fwd")
    return o


def _post_bwd(g, y, gain):
    def fn(gv, yv, gn):
        dx, dg = _rms_bwd(gv, yv, gn)
        return (dx,), (dg,)

    return _rowwise(fn, [(g, D, 0), (y, D, 0), (gain, D, 0)], [(D, BF16)], [D], name="post_bwd")


def _pre_bwd(g, dh, x, gain):
    def fn(gv, dhv, xv, gn):
        dx, dg = _rms_bwd(dhv, xv, gn)
        return (gv + dx,), (dg,)

    return _rowwise(fn, [(g, D, 0), (dh, D, 0), (x, D, 0), (gain, D, 0)], [(D, F32)], [D], name="pre_bwd")


def _loss_grad(xo, tgt):
    def fn(xv, tv):
        e = xv - tv
        return (e * (1.0 / D),), (e * e,)

    return _rowwise(fn, [(xo, D, 0), (tgt, D, 0)], [(D, F32)], [D], name="loss_grad")


def _pool(u_arr, col_block, transpose, out_dtype, tc=256):
    n_t = POOL_W // tc
    per_group = POOL_GROUP // tc

    def body(u_ref, o_ref):
        c = pl.program_id(0)
        grp = c // per_group
        xv = u_ref[...]
        t = lax.broadcasted_iota(jnp.int32, (S, 1), 0)
        win = jnp.left_shift(2, grp)
        cnt = jnp.minimum(t + 1, win).astype(F32)
        cur = xv / cnt if transpose else xv
        sums = []
        for k in (1, 2, 4, 8):
            if transpose:
                sh = jnp.where(t < S - k, pltpu.roll(cur, S - k, 0), 0.0)
            else:
                sh = jnp.where(t >= k, pltpu.roll(cur, k, 0), 0.0)
            cur = cur + sh
            sums.append(cur)
        tot = jnp.where(grp == 0, sums[0], jnp.where(grp == 1, sums[1], jnp.where(grp == 2, sums[2], sums[3])))
        res = tot - xv if transpose else tot / cnt - xv
        o_ref[...] = res.astype(o_ref.dtype)

    return pl.pallas_call(
        body, name="pool_bwd" if transpose else "pool_fwd", grid=(n_t,),
        in_specs=[pl.BlockSpec((S, tc), lambda c: (0, col_block * n_t + c))],
        out_specs=pl.BlockSpec((S, tc), lambda c: (0, c)),
        out_shape=SDS((S, POOL_W), out_dtype),
        compiler_params=_cparams(("parallel",)),
    )(u_arr)


def _rope_tables():
    pos = jnp.arange(S, dtype=jnp.int32).astype(F32)
    inv_freq = ROPE_THETA ** (-jnp.arange(0, ROT_DIM, 2, dtype=F32) / ROT_DIM)
    ang = pos[:, None] * inv_freq[None, :]
    cos8, sin8 = jnp.cos(ang), jnp.sin(ang)
    half = ROT_DIM // 2
    zeros = jnp.zeros((S, HEAD_DIM - ROT_DIM), F32)
    cos = jnp.concatenate([cos8, cos8, jnp.ones((S, HEAD_DIM - ROT_DIM), F32)], axis=1)
    lo = jnp.concatenate([-sin8, jnp.zeros((S, half), F32), zeros], axis=1)
    hi = jnp.concatenate([jnp.zeros((S, half), F32), sin8, zeros], axis=1)
    rep = LANES // HEAD_DIM
    return jnp.tile(cos, (1, rep)), jnp.tile(lo, (1, rep)), jnp.tile(hi, (1, rep))


def _rope(x_arr, col_tile0, tables, scale, transpose, out_dtype):
    cos, lo, hi = tables
    sign = -1.0 if transpose else 1.0
    half = ROT_DIM // 2

    def body(x_ref, c_ref, lo_ref, hi_ref, o_ref):
        xv = x_ref[...].astype(F32)
        up = pltpu.roll(xv, LANES - half, 1)
        dn = pltpu.roll(xv, half, 1)
        res = xv * c_ref[...] + sign * (up * lo_ref[...] + dn * hi_ref[...])
        o_ref[...] = (res * scale).astype(o_ref.dtype)

    n_t = ATT_W // LANES
    tab = pl.BlockSpec((S, LANES), lambda c: (0, 0))
    return pl.pallas_call(
        body, name="rope_bwd" if transpose else "rope_fwd", grid=(n_t,),
        in_specs=[pl.BlockSpec((S, LANES), lambda c: (0, col_tile0 + c)), tab, tab, tab],
        out_specs=pl.BlockSpec((S, LANES), lambda c: (0, c)),
        out_shape=SDS((S, ATT_W), out_dtype),
        compiler_params=_cparams(("parallel",)),
    )(x_arr, cos, lo, hi)


ATT_T = 256


def _multiplicity(delta):
    ok = delta >= 0
    near = jnp.where(ok & (delta <= 128), 1.0, 0.0)
    mid = jnp.where(ok & (delta <= 512) & ((delta & 3) == 0), 1.0, 0.0)
    far = jnp.where(ok & ((delta & 15) == 0), 1.0, 0.0)
    return near + mid + far


def _head_split(v, first):
    zero = jnp.zeros_like(v)
    return [jnp.where(first, v, zero), jnp.where(first, zero, v)]


def _flash_fwd(q, k, v):
    t = ATT_T
    n_hp = ATT_W // LANES

    def body(q_ref, k_ref, v_ref, o_ref, lse_ref):
        i = pl.program_id(1)
        first = lax.broadcasted_iota(jnp.int32, (1, LANES), 1) < HEAD_DIM
        qs = _head_split(q_ref[...], first)
        base = (lax.broadcasted_iota(jnp.int32, (t, t), 0) - lax.broadcasted_iota(jnp.int32, (t, t), 1))

        def kv_step(j, carry):
            m0, l0, m1, l1, acc = carry
            off = pl.multiple_of(j * t, t)
            kb = k_ref[pl.ds(off, t), :]
            vs = _head_split(v_ref[pl.ds(off, t), :], first)
            mult = _multiplicity(base + (i - j) * t)
            valid = mult > 0.0
            new = []
            pv = None
            for h, (m_prev, l_prev) in enumerate(((m0, l0), (m1, l1))):
                s = lax.dot_general(qs[h], kb, NT, preferred_element_type=F32)
                s = jnp.where(valid, s, -1e30)
                m_new = jnp.maximum(m_prev, jnp.max(s, axis=1, keepdims=True))
                p = jnp.exp(s - m_new) * mult
                alpha = jnp.exp(m_prev - m_new)
                l_new = alpha * l_prev + jnp.sum(p, axis=1, keepdims=True)
                d = lax.dot_general(p.astype(BF16), vs[h], NN, preferred_element_type=F32)
                pv = d if pv is None else pv + d
                new.append((m_new, l_new, alpha))
            acc = acc * jnp.where(first, new[0][2], new[1][2]) + pv
            return new[0][0], new[0][1], new[1][0], new[1][1], acc

        neg = jnp.full((t, 1), -1e30, F32)
        zero = jnp.zeros((t, 1), F32)
        m0, l0, m1, l1, acc = lax.fori_loop(0, i + 1, kv_step, (neg, zero, neg, zero, jnp.zeros((t, LANES), F32)))
        o_ref[...] = acc * jnp.where(first, 1.0 / l0, 1.0 / l1)
        lse_ref[...] = jnp.where(first, m0 + jnp.log(l0), m1 + jnp.log(l1))

    blk = pl.BlockSpec((t, LANES), lambda hp, i: (i, hp))
    full = pl.BlockSpec((S, LANES), lambda hp, i: (0, hp))
    return pl.pallas_call(
        body, name="flash_fwd", grid=(n_hp, S // t),
        in_specs=[blk, full, full], out_specs=[blk, blk],
        out_shape=[SDS((S, ATT_W), F32), SDS((S, ATT_W), F32)],
        compiler_params=_cparams(("parallel", "arbitrary")),
    )(q, k, v)


def _flash_bwd(q, k, v, o, do, lse):
    t = ATT_T
    n_hp = ATT_W // LANES
    n_t = S // t

    def body(q_ref, k_ref, v_ref, o_ref, do_ref, lse_ref, dq_ref, dk_ref, dv_ref):
        j = pl.program_id(1)
        first = lax.broadcasted_iota(jnp.int32, (1, LANES), 1) < HEAD_DIM

        @pl.when(j == 0)
        def _():
            dq_ref[...] = jnp.zeros_like(dq_ref)

        kb = k_ref[...]
        vb = v_ref[...]
        ks = _head_split(kb, first)
        base = (lax.broadcasted_iota(jnp.int32, (t, t), 0) - lax.broadcasted_iota(jnp.int32, (t, t), 1))

        def q_step(i, carry):
            dk_acc, dv_acc = carry
            rows = pl.ds(pl.multiple_of(i * t, t), t)
            qs = _head_split(q_ref[rows, :], first)
            dob = do_ref[rows, :]
            prod = dob * o_ref[rows, :]
            d_all = jnp.sum(prod, axis=1, keepdims=True)
            d0 = jnp.sum(jnp.where(first, prod, 0.0), axis=1, keepdims=True)
            lse_b = lse_ref[rows, :]
            lse0 = jnp.max(jnp.where(first, lse_b, -jnp.inf), axis=1, keepdims=True)
            lse1 = jnp.max(jnp.where(first, -jnp.inf, lse_b), axis=1, keepdims=True)
            dos = _head_split(dob.astype(BF16), first)
            mult = _multiplicity(base + (i - j) * t)
            valid = mult > 0.0
            dq_t = jnp.zeros((t, LANES), F32)
            for h, (lse_h, d_h) in enumerate(((lse0, d0), (lse1, d_all - d0))):
                s = lax.dot_general(qs[h], kb, NT, preferred_element_type=F32)
                p = jnp.exp(jnp.where(valid, s, -1e30) - lse_h) * mult
                dp = lax.dot_general(dos[h], vb, NT, preferred_element_type=F32)
                ds = (p * (dp - d_h)).astype(BF16)
                dv_acc = dv_acc + lax.dot_general(p.astype(BF16), dos[h], TN, preferred_element_type=F32)
                dk_acc = dk_acc + lax.dot_general(ds, qs[h], TN, preferred_element_type=F32)
                dq_t = dq_t + lax.dot_general(ds, ks[h], NN, preferred_element_type=F32)
            dq_ref[rows, :] += dq_t
            return dk_acc, dv_acc

        zero = jnp.zeros((t, LANES), F32)
        dk_acc, dv_acc = lax.fori_loop(j, n_t, q_step, (zero, zero))
        dk_ref[...] = dk_acc
        dv_ref[...] = dv_acc

    blk = pl.BlockSpec((t, LANES), lambda hp, j: (j, hp))
    full = pl.BlockSpec((S, LANES), lambda hp, j: (0, hp))
    return pl.pallas_call(
        body, name="flash_bwd", grid=(n_hp, n_t),
        in_specs=[full, blk, blk, full, full, full], out_specs=[full, blk, blk],
        out_shape=[SDS((S, ATT_W), F32)] * 3,
        compiler_params=_cparams(("parallel", "arbitrary")),
    )(q, k, v, o, do, lse)


SCAN_T = 256
ST_ROWS = 2 * N_CPLX // LANES
HALF = ST_ROWS // 2


def _scan_fwd(lam, bu):
    def body(lam_ref, bu_ref, st_ref, carry):
        @pl.when(pl.program_id(0) == 0)
        def _():
            carry[...] = jnp.zeros_like(carry)

        ar, ai = lam_ref[0:HALF, :], lam_ref[HALF:ST_ROWS, :]

        def step(t, c):
            sr, si = c
            b = bu_ref[t]
            nr = ar * sr - ai * si + b[0:HALF]
            ni = ar * si + ai * sr + b[HALF:ST_ROWS]
            st_ref[t, 0:HALF, :] = nr
            st_ref[t, HALF:ST_ROWS, :] = ni
            return nr, ni

        sr, si = lax.fori_loop(0, SCAN_T, step, (carry[0:HALF, :], carry[HALF:ST_ROWS, :]), unroll=8)
        carry[0:HALF, :] = sr
        carry[HALF:ST_ROWS, :] = si

    blk = pl.BlockSpec((SCAN_T, ST_ROWS, LANES), lambda i: (i, 0, 0))
    return pl.pallas_call(
        body, name="scan_fwd", grid=(S // SCAN_T,),
        in_specs=[pl.BlockSpec((ST_ROWS, LANES), lambda i: (0, 0)), blk], out_specs=blk,
        out_shape=SDS((S, ST_ROWS, LANES), F32),
        scratch_shapes=[pltpu.VMEM((ST_ROWS, LANES), F32)],
        compiler_params=_cparams(("arbitrary",)),
    )(lam, bu)


def _scan_bwd(lam, dst, states):
    n_blk = S // SCAN_T

    def body(lam_ref, d_ref, st_ref, g_ref, dlam_ref, carry):
        @pl.when(pl.program_id(0) == 0)
        def _():
            carry[...] = jnp.zeros_like(carry)
            dlam_ref[...] = jnp.zeros_like(dlam_ref)

        ar, ai = lam_ref[0:HALF, :], lam_ref[HALF:ST_ROWS, :]

        def step(kk, c):
            t = SCAN_T - 1 - kk
            gr, gi, dar, dai = c
            x = st_ref[t]
            xr, xi = x[0:HALF], x[HALF:ST_ROWS]
            dar = dar + gr * xr + gi * xi
            dai = dai + gi * xr - gr * xi
            d = d_ref[t]
            ngr = d[0:HALF] + ar * gr + ai * gi
            ngi = d[HALF:ST_ROWS] + ar * gi - ai * gr
            g_ref[t, 0:HALF, :] = ngr
            g_ref[t, HALF:ST_ROWS, :] = ngi
            return ngr, ngi, dar, dai

        zero = jnp.zeros((HALF, LANES), F32)
        gr, gi, dar, dai = lax.fori_loop(0, SCAN_T, step, (carry[0:HALF, :], carry[HALF:ST_ROWS, :], zero, zero),
                                         unroll=8)
        carry[0:HALF, :] = gr
        carry[HALF:ST_ROWS, :] = gi
        dlam_ref[0:HALF, :] += dar
        dlam_ref[HALF:ST_ROWS, :] += dai

    blk = pl.BlockSpec((SCAN_T, ST_ROWS, LANES), lambda i: (n_blk - 1 - i, 0, 0))
    small = pl.BlockSpec((ST_ROWS, LANES), lambda i: (0, 0))
    return pl.pallas_call(
        body, name="scan_bwd", grid=(n_blk,),
        in_specs=[small, blk, blk], out_specs=[blk, small],
        out_shape=[SDS((S, ST_ROWS, LANES), F32), SDS((ST_ROWS, LANES), F32)],
        scratch_shapes=[pltpu.VMEM((ST_ROWS, LANES), F32)],
        compiler_params=_cparams(("arbitrary",)),
    )(lam, dst, states)


def _ssm_prep(a_re, a_im, log_dt, b_re, b_im, c_re, c_im):
    lam = lax.complex(a_re, a_im)
    dt = jnp.exp(log_dt)[:, None]
    lam_bar = jnp.exp(lam * dt)
    b_bar = ((lam_bar - 1.0) / lam)[..., None] * lax.complex(b_re, b_im)
    lam_t = jnp.concatenate([jnp.real(lam_bar).reshape(HALF, LANES), jnp.imag(lam_bar).reshape(HALF, LANES)], axis=0)
    eye = jnp.eye(SSM_GROUPS, dtype=F32)

    def in_map(m):
        return jnp.einsum("gph,gk->ghkp", m, eye).reshape(SSM_W, N_CPLX)

    def out_map(m):
        return jnp.einsum("ghp,gk->gpkh", m, eye).reshape(N_CPLX, SSM_W)

    w_b = jnp.concatenate([in_map(jnp.real(b_bar)), in_map(jnp.imag(b_bar))], axis=1)
    w_c = jnp.concatenate([out_map(c_re), -out_map(c_im)], axis=0)
    return lam_t, w_b, w_c


def _row(v):
    return v.reshape(1, -1)


def _even_fwd(x, pre, post, w_in, w_out, glu_w, glu_b, ssm_d, prep, tables):
    lam_t, w_b, w_c = prep
    h = _norm_fwd(x, pre)
    proj = _mm(h, w_in, "nn", F32)
    q = _rope(proj, 0, tables, HEAD_DIM ** -0.5, False, BF16)
    k = _rope(proj, ATT_W // LANES, tables, 1.0, False, BF16)
    v = proj[:, 2 * ATT_W:3 * ATT_W].astype(BF16)
    att, lse = _flash_fwd(q, k, v)
    u_ssm = proj[:, 4 * ATT_W:4 * ATT_W + SSM_W]
    bu = _mm(u_ssm, w_b, "nn", F32)
    states = _scan_fwd(lam_t, bu.reshape(S, ST_ROWS, LANES))
    y = _mm(states.reshape(S, 2 * N_CPLX), w_c, "nn", F32)

    def act1(yv, uv, dv):
        return (_gelu_and_grad(yv + dv * uv)[0],), ()

    (z1,) = _rowwise(act1, [(y, SSM_W, 0), (proj, SSM_W, 8), (ssm_d, SSM_W, 0)], [(SSM_W, F32)], name="ssm_act_fwd")
    lin = _mm(z1, glu_w, "nn", F32)

    def gate(att_v, ga, gs, z1v, linv, bv):
        ssm_out = z1v * _sigmoid(linv + bv)
        return (jnp.concatenate([att_v * _silu_and_grad(ga)[0], ssm_out * _silu_and_grad(gs)[0]], axis=1),), ()

    (merged,) = _rowwise(gate, [(att, ATT_W, 0), (proj, ATT_W, 3), (proj, SSM_W, 9), (z1, SSM_W, 0),
                                (lin, SSM_W, 0), (glu_b, SSM_W, 0)], [(EVEN_OUT, BF16)], name="even_gate_fwd")
    yout = _mm(merged, w_out, "nn", F32)
    x_next = _post_fwd(x, yout, post)
    saved = (x, h, proj, q, k, v, att, lse, states, y, z1, lin, merged, yout)
    return x_next, saved


def _even_bwd(g, saved, pre, post, w_in, w_out, glu_w, glu_b, ssm_d, prep, tables):
    x, h, proj, q, k, v, att, lse, states, y, z1, lin, merged, yout = saved
    lam_t, w_b, w_c = prep
    dyout, dpost = _post_bwd(g, yout, post)
    dmerged = _mm(dyout, w_out, "nt", F32)
    dw_out = _mm(merged, dyout, "tn", BF16)

    def gate_bwd(dm_a, dm_s, att_v, ga, gs, z1v, linv, bv):
        sa, dsa = _silu_and_grad(ga)
        ss, dss = _silu_and_grad(gs)
        sig = _sigmoid(linv + bv)
        ssm_out = z1v * sig
        dssm = dm_s * ss
        dlin = dssm * z1v * sig * (1.0 - sig)
        return (dm_a * sa, dm_a * att_v * dsa, dm_s * ssm_out * dss, dssm * sig, dlin), (dlin,)

    datt, dg_att, dg_ssm, dz1a, dlin, dglu_b = _rowwise(
        gate_bwd, [(dmerged, ATT_W, 0), (dmerged, SSM_W, 2), (att, ATT_W, 0), (proj, ATT_W, 3), (proj, SSM_W, 9),
                   (z1, SSM_W, 0), (lin, SSM_W, 0), (glu_b, SSM_W, 0)],
        [(ATT_W, F32), (ATT_W, BF16), (SSM_W, BF16), (SSM_W, F32), (SSM_W, BF16)], [SSM_W], name="even_gate_bwd")
    dz1b = _mm(dlin, glu_w, "nt", F32)
    dglu_w = _mm(z1, dlin, "tn", BF16)

    def act1_bwd(da, db, yv, uv, dv):
        dpre = (da + db) * _gelu_and_grad(yv + dv * uv)[1]
        return (dpre, dpre * dv), (dpre * uv,)

    dy, du_direct, dd = _rowwise(act1_bwd, [(dz1a, SSM_W, 0), (dz1b, SSM_W, 0), (y, SSM_W, 0), (proj, SSM_W, 8),
                                            (ssm_d, SSM_W, 0)], [(SSM_W, BF16), (SSM_W, F32)], [SSM_W],
                                 name="ssm_act_bwd")
    dst = _mm(dy, w_c, "nt", F32)
    dw_c = _mm(states.reshape(S, 2 * N_CPLX), dy, "tn", F32)
    gst, dlam = _scan_bwd(lam_t, dst.reshape(S, ST_ROWS, LANES), states)
    dbu = gst.reshape(S, 2 * N_CPLX)
    u_ssm = proj[:, 4 * ATT_W:4 * ATT_W + SSM_W]
    du_ssm = _mm(dbu, w_b, "nt", F32) + du_direct
    dw_b = _mm(u_ssm, dbu, "tn", F32)
    dq, dk, dv = _flash_bwd(q, k, v, att, datt, lse)
    dq = _rope(dq, 0, tables, HEAD_DIM ** -0.5, True, BF16)
    dk = _rope(dk, 0, tables, 1.0, True, BF16)
    dproj = jnp.concatenate([dq, dk, dv.astype(BF16), dg_att, du_ssm.astype(BF16), dg_ssm], axis=1)
    dw_in = _mm(h, dproj, "tn", BF16)
    dh = _mm(dproj, w_in, "nt", F32)
    g_prev, dpre = _pre_bwd(g, dh, x, pre)
    return g_prev, dict(pre=dpre, post=dpost, w_in=dw_in, w_out=dw_out, glu_w=dglu_w, glu_b=dglu_b, ssm_d=dd,
                        prep=(dlam, dw_b, dw_c))


def _odd_fwd(x, pre, post, w_in, pool_w, pool_scale, w_out):
    h = _norm_fwd(x, pre)
    proj = _mm(h, w_in, "nn", F32)
    mixed = _pool(proj, 0, False, BF16)
    ylin = _gmm(mixed, pool_w, "nn", F32)

    def gate(yl, gt, sc):
        return (yl * sc * _silu_and_grad(gt)[0],), ()

    (z,) = _rowwise(gate, [(ylin, POOL_W, 0), (proj, POOL_W, 1), (pool_scale, POOL_W, 0)], [(POOL_W, BF16)],
                    name="odd_gate_fwd")
    yout = _mm(z, w_out, "nn", F32)
    x_next = _post_fwd(x, yout, post)
    return x_next, (x, h, proj, mixed, ylin, z, yout)


def _odd_bwd(g, saved, pre, post, w_in, pool_w, pool_scale, w_out):
    x, h, proj, mixed, ylin, z, yout = saved
    dyout, dpost = _post_bwd(g, yout, post)
    dz = _mm(dyout, w_out, "nt", F32)
    dw_out = _mm(z, dyout, "tn", BF16)

    def gate_bwd(dzv, yl, gt, sc):
        sg, dsg = _silu_and_grad(gt)
        tt = dzv * sg
        return (tt * sc, dzv * yl * sc * dsg), (tt * yl,)

    dylin, dgate, dscale = _rowwise(gate_bwd, [(dz, POOL_W, 0), (ylin, POOL_W, 0), (proj, POOL_W, 1),
                                               (pool_scale, POOL_W, 0)], [(POOL_W, BF16), (POOL_W, BF16)], [POOL_W],
                                    name="odd_gate_bwd")
    dmixed = _gmm(dylin, pool_w, "nt", F32)
    dpool_w = _gmm(mixed, dylin, "tn", BF16)
    du = _pool(dmixed, 0, True, BF16)
    dproj = jnp.concatenate([du, dgate], axis=1)
    dw_in = _mm(h, dproj, "tn", BF16)
    dh = _mm(dproj, w_in, "nt", F32)
    g_prev, dpre = _pre_bwd(g, dh, x, pre)
    return g_prev, dict(pre=dpre, post=dpost, w_in=dw_in, w_out=dw_out, pool_w=dpool_w, pool_scale=dscale)


def _my_index():
    return 4 * lax.axis_index("x") + 2 * lax.axis_index("y") + lax.axis_index("c")


def _exchange(arrs, gather, name):
    n = len(arrs)
    out_shape = [SDS((N_DEV,) + a.shape, a.dtype) if gather else SDS(a.shape, a.dtype) for a in arrs]

    def body(*refs):
        ins, outs = refs[:n], refs[n:2 * n]
        send_sems, recv_sems, local_sems = refs[2 * n:]
        me = _my_index()

        def src(i, j):
            return ins[i] if gather else ins[i].at[j]

        def remote(i, j, src_slot, dst_slot, recv_slot):
            return pltpu.make_async_remote_copy(
                src_ref=src(i, src_slot), dst_ref=outs[i].at[dst_slot], send_sem=send_sems.at[i, j],
                recv_sem=recv_sems.at[i, recv_slot], device_id=(j // 4, (j // 2) % 2, j % 2), device_id_type=MESH_ID)

        def local(i):
            return pltpu.make_async_copy(src(i, me), outs[i].at[me], local_sems.at[i])

        for i in range(n):
            local(i).start()
        for j in range(N_DEV):
            @pl.when(me != j)
            def _(j=j):
                for i in range(n):
                    remote(i, j, j, me, me).start()
        for j in range(N_DEV):
            @pl.when(me != j)
            def _(j=j):
                for i in range(n):
                    remote(i, j, j, me, me).wait_send()
                    remote(i, j, j, j, j).wait_recv()
        for i in range(n):
            local(i).wait()

    any_spec = pl.BlockSpec(memory_space=pl.ANY)
    return pl.pallas_call(
        body, name=name, in_specs=[any_spec] * n, out_specs=[any_spec] * n, out_shape=out_shape,
        scratch_shapes=[pltpu.SemaphoreType.DMA((n, N_DEV)), pltpu.SemaphoreType.DMA((n, N_DEV)),
                        pltpu.SemaphoreType.DMA((n,))],
    )(*arrs)


def _adam(w, gslots, m, v, name):
    r, c = w.shape
    ns = gslots.shape[0]
    tr = r
    while tr * c * 4 > (1 << 20) and tr % 16 == 0:
        tr //= 2
    assert r % tr == 0

    def body(w_ref, g_ref, m_ref, v_ref, go_ref, d_ref, mo_ref, vo_ref):
        g = g_ref[0].astype(F32)
        for s in range(1, ns):
            g = g + g_ref[s].astype(F32)
        wv = w_ref[...]
        mn = ADAM_B1 * m_ref[...] + (1.0 - ADAM_B1) * g
        vn = ADAM_B2 * v_ref[...] + (1.0 - ADAM_B2) * (g * g)
        m_hat = mn / (1.0 - ADAM_B1 ** ADAM_STEP)
        v_hat = vn / (1.0 - ADAM_B2 ** ADAM_STEP)
        go_ref[...] = g
        d_ref[...] = -ADAM_LR * (m_hat / (jnp.sqrt(v_hat) + ADAM_EPS) + ADAM_WD * wv)
        mo_ref[...] = mn
        vo_ref[...] = vn

    blk = pl.BlockSpec((tr, c), lambda i: (i, 0))
    return pl.pallas_call(
        body, name=name, grid=(r // tr,),
        in_specs=[blk, pl.BlockSpec((ns, tr, c), lambda i: (0, i, 0)), blk, blk],
        out_specs=[blk] * 4, out_shape=[SDS((r, c), F32)] * 4,
        compiler_params=_cparams(("parallel",)),
    )(w, gslots, m, v)


SMALL_NAMES = ("pre_norm", "post_norm", "ssm_a_re", "ssm_a_im", "ssm_log_dt", "ssm_b_re", "ssm_b_im", "ssm_c_re",
               "ssm_c_im", "ssm_d", "ssm_glu_b")
SHARDED_NAMES = ("even_w_in", "even_w_out", "ssm_glu_w", "odd_w_in", "pool_w", "odd_w_out")
WEIGHT_ORDER = ("pre_norm", "post_norm", "even_w_in", "even_w_out", "ssm_a_re", "ssm_a_im", "ssm_log_dt", "ssm_b_re",
                "ssm_b_im", "ssm_c_re", "ssm_c_im", "ssm_d", "ssm_glu_w", "ssm_glu_b", "odd_w_in", "pool_w",
                "pool_scale", "odd_w_out")
PACK_ROWS_ALIGN = 8


def _pack(parts):
    flat = jnp.concatenate([p.reshape(-1).astype(F32) for p in parts])
    rows = -(-flat.shape[0] // (LANES * PACK_ROWS_ALIGN)) * PACK_ROWS_ALIGN
    return jnp.pad(flat, (0, rows * LANES - flat.shape[0])).reshape(rows, LANES)


def _unpack(packed, shapes):
    flat = packed.reshape(-1)
    out, off = [], 0
    for shp in shapes:
        size = math.prod(shp)
        out.append(flat[off:off + size].reshape(shp))
        off += size
    return out


def _local_step(x, tgt, small, full):
    tables = _rope_tables()
    preps, prep_vjps = [], []
    for i in range(2):
        out, vjp = jax.vjp(_ssm_prep, small["ssm_a_re"][i], small["ssm_a_im"][i], small["ssm_log_dt"][i],
                           small["ssm_b_re"][i], small["ssm_b_im"][i], small["ssm_c_re"][i], small["ssm_c_im"][i])
        preps.append(out)
        prep_vjps.append(vjp)

    def even_args(i):
        layer = 2 * i
        return (_row(small["pre_norm"][layer]), _row(small["post_norm"][layer]), full["even_w_in"][i],
                full["even_w_out"][i], full["ssm_glu_w"][i], _row(small["ssm_glu_b"][i]), _row(small["ssm_d"][i]),
                preps[i], tables)

    def odd_args(i):
        layer = 2 * i + 1
        return (_row(small["pre_norm"][layer]), _row(small["post_norm"][layer]), full["odd_w_in"][i],
                full["pool_w"][i], _row(full["pool_scale"][i]), full["odd_w_out"][i])

    saved = []
    cur = x
    for layer in range(4):
        if layer % 2 == 0:
            cur, sv = _even_fwd(cur, *even_args(layer // 2))
        else:
            cur, sv = _odd_fwd(cur, *odd_args(layer // 2))
        saved.append(sv)
    g, sq = _loss_grad(cur, tgt)
    loss = 0.5 * jnp.sum(sq) / D

    lg = [None] * 4
    for layer in reversed(range(4)):
        if layer % 2 == 0:
            g, lg[layer] = _even_bwd(g, saved[layer], *even_args(layer // 2))
        else:
            g, lg[layer] = _odd_bwd(g, saved[layer], *odd_args(layer // 2))

    ssm_g = [prep_vjps[i](lg[2 * i]["prep"]) for i in range(2)]
    grads = {
        "pre_norm": jnp.concatenate([lg[l]["pre"] for l in range(4)], axis=0),
        "post_norm": jnp.concatenate([lg[l]["post"] for l in range(4)], axis=0),
        "ssm_d": jnp.concatenate([lg[0]["ssm_d"], lg[2]["ssm_d"]], axis=0),
        "ssm_glu_b": jnp.concatenate([lg[0]["glu_b"], lg[2]["glu_b"]], axis=0),
        "pool_scale": jnp.concatenate([lg[1]["pool_scale"], lg[3]["pool_scale"]], axis=0),
        "even_w_in": jnp.stack([lg[0]["w_in"], lg[2]["w_in"]]),
        "even_w_out": jnp.stack([lg[0]["w_out"], lg[2]["w_out"]]),
        "ssm_glu_w": jnp.stack([lg[0]["glu_w"], lg[2]["glu_w"]]),
        "odd_w_in": jnp.stack([lg[1]["w_in"], lg[3]["w_in"]]),
        "pool_w": jnp.stack([lg[1]["pool_w"], lg[3]["pool_w"]]),
        "odd_w_out": jnp.stack([lg[1]["w_out"], lg[3]["w_out"]]),
    }
    for idx, nm in enumerate(("ssm_a_re", "ssm_a_im", "ssm_log_dt", "ssm_b_re", "ssm_b_im", "ssm_c_re", "ssm_c_im")):
        grads[nm] = jnp.stack([ssm_g[0][idx], ssm_g[1][idx]])
    return loss, g, grads


def _to_slots(name, gfull):
    if name in ("even_w_in", "odd_w_in"):
        two, kk, nn = gfull.shape
        return gfull.reshape(two, kk, N_DEV, nn // N_DEV).transpose(2, 0, 1, 3)
    if name in ("even_w_out", "odd_w_out", "ssm_glu_w"):
        two, rr, nn = gfull.shape
        return gfull.reshape(two, N_DEV, rr // N_DEV, nn).transpose(1, 0, 2, 3)
    assert name == "pool_w"
    two, gg, rr, nn = gfull.shape
    return gfull.reshape(two, gg, N_DEV, rr // N_DEV, nn).transpose(2, 0, 1, 3, 4)


def _from_gathered(name, gat):
    if name in ("even_w_in", "odd_w_in"):
        _, two, kk, nn = gat.shape
        return gat.transpose(1, 2, 0, 3).reshape(two, kk, N_DEV * nn)
    if name in ("even_w_out", "odd_w_out", "ssm_glu_w"):
        _, two, rr, nn = gat.shape
        return gat.transpose(1, 0, 2, 3).reshape(two, N_DEV * rr, nn)
    assert name == "pool_w"
    _, two, gg, rr, nn = gat.shape
    return gat.transpose(1, 2, 0, 3, 4).reshape(two, gg, N_DEV * rr, nn)


def kernel(x, pre_norm, post_norm, even_w_in, even_w_out, ssm_a_re, ssm_a_im, ssm_log_dt, ssm_b_re, ssm_b_im, ssm_c_re, ssm_c_im, ssm_d, ssm_glu_w, ssm_glu_b, odd_w_in, pool_w, pool_scale, odd_w_out, loss_target, m_pre_norm, m_post_norm, m_even_w_in, m_even_w_out, m_ssm_a_re, m_ssm_a_im, m_ssm_log_dt, m_ssm_b_re, m_ssm_b_im, m_ssm_c_re, m_ssm_c_im, m_ssm_d, m_ssm_glu_w, m_ssm_glu_b, m_odd_w_in, m_pool_w, m_pool_scale, m_odd_w_out, v_pre_norm, v_post_norm, v_even_w_in, v_even_w_out, v_ssm_a_re, v_ssm_a_im, v_ssm_log_dt, v_ssm_b_re, v_ssm_b_im, v_ssm_c_re, v_ssm_c_im, v_ssm_d, v_ssm_glu_w, v_ssm_glu_b, v_odd_w_in, v_pool_w, v_pool_scale, v_odd_w_out):
    w = dict(pre_norm=pre_norm, post_norm=post_norm, even_w_in=even_w_in, even_w_out=even_w_out, ssm_a_re=ssm_a_re,
             ssm_a_im=ssm_a_im, ssm_log_dt=ssm_log_dt, ssm_b_re=ssm_b_re, ssm_b_im=ssm_b_im, ssm_c_re=ssm_c_re,
             ssm_c_im=ssm_c_im, ssm_d=ssm_d, ssm_glu_w=ssm_glu_w, ssm_glu_b=ssm_glu_b, odd_w_in=odd_w_in,
             pool_w=pool_w, pool_scale=pool_scale, odd_w_out=odd_w_out)
    mom = dict(pre_norm=m_pre_norm, post_norm=m_post_norm, even_w_in=m_even_w_in, even_w_out=m_even_w_out,
               ssm_a_re=m_ssm_a_re, ssm_a_im=m_ssm_a_im, ssm_log_dt=m_ssm_log_dt, ssm_b_re=m_ssm_b_re,
               ssm_b_im=m_ssm_b_im, ssm_c_re=m_ssm_c_re, ssm_c_im=m_ssm_c_im, ssm_d=m_ssm_d, ssm_glu_w=m_ssm_glu_w,
               ssm_glu_b=m_ssm_glu_b, odd_w_in=m_odd_w_in, pool_w=m_pool_w, pool_scale=m_pool_scale,
               odd_w_out=m_odd_w_out)
    var = dict(pre_norm=v_pre_norm, post_norm=v_post_norm, even_w_in=v_even_w_in, even_w_out=v_even_w_out,
               ssm_a_re=v_ssm_a_re, ssm_a_im=v_ssm_a_im, ssm_log_dt=v_ssm_log_dt, ssm_b_re=v_ssm_b_re,
               ssm_b_im=v_ssm_b_im, ssm_c_re=v_ssm_c_re, ssm_c_im=v_ssm_c_im, ssm_d=v_ssm_d, ssm_glu_w=v_ssm_glu_w,
               ssm_glu_b=v_ssm_glu_b, odd_w_in=v_odd_w_in, pool_w=v_pool_w, pool_scale=v_pool_scale,
               odd_w_out=v_odd_w_out)
    me = _my_index()
    scale_cols = pool_scale.shape[1]

    scale_pad = jnp.pad(pool_scale, ((0, PACK_ROWS_ALIGN - pool_scale.shape[0]), (0, 0)))
    gathered = _exchange([w[nm].astype(BF16) for nm in SHARDED_NAMES] + [scale_pad], True, "gather_weights")
    full = {nm: _from_gathered(nm, gat) for nm, gat in zip(SHARDED_NAMES, gathered[:-1])}
    full["pool_scale"] = gathered[-1][:, :2, :].transpose(1, 0, 2).reshape(2, N_DEV * scale_cols)
    small = {nm: w[nm] for nm in SMALL_NAMES}

    loss_local, grad_x, grads = _local_step(x[0], loss_target[0], small, full)
    loss = lax.psum(loss_local, ("x", "y", "c"))

    small_parts = [grads[nm] for nm in SMALL_NAMES] + [grads["pool_scale"]]
    small_shapes = [w[nm].shape for nm in SMALL_NAMES] + [(2, N_DEV * scale_cols)]
    packed = _pack(small_parts)
    slots = _exchange([_to_slots(nm, grads[nm]) for nm in SHARDED_NAMES], False, "scatter_grads")
    (small_slots,) = _exchange([packed], True, "gather_small_grads")

    res = {}
    for nm, sl in zip(SHARDED_NAMES, slots):
        shp = w[nm].shape
        cols = shp[-1]
        outs = _adam(w[nm].reshape(-1, cols), sl.reshape(N_DEV, -1, cols), mom[nm].reshape(-1, cols),
                     var[nm].reshape(-1, cols), name=f"adam_{nm}")
        res[nm] = [o.reshape(shp) for o in outs]
    w_pack = _pack([w[nm] for nm in SMALL_NAMES] + [jnp.zeros((2, N_DEV * scale_cols), F32)])
    m_pack = _pack([mom[nm] for nm in SMALL_NAMES] + [jnp.zeros((2, N_DEV * scale_cols), F32)])
    v_pack = _pack([var[nm] for nm in SMALL_NAMES] + [jnp.zeros((2, N_DEV * scale_cols), F32)])
    outs = _adam(w_pack, small_slots, m_pack, v_pack, name="adam_small")
    unpacked = [_unpack(o, small_shapes) for o in outs]
    for idx, nm in enumerate(SMALL_NAMES):
        res[nm] = [unpacked[kind][idx] for kind in range(4)]
    g_scale = lax.dynamic_slice_in_dim(unpacked[0][-1], me * scale_cols, scale_cols, axis=1)
    pad = ((0, PACK_ROWS_ALIGN - 2), (0, 0))
    outs = _adam(jnp.pad(pool_scale, pad), jnp.pad(g_scale, pad)[None], jnp.pad(m_pool_scale, pad),
                 jnp.pad(v_pool_scale, pad), name="adam_pool_scale")
    res["pool_scale"] = [o[:2] for o in outs]

    out = [loss, grad_x[None]]
    for kind in range(4):
        out += [res[nm][kind] for nm in WEIGHT_ORDER]
    return tuple(out)
```

```python
import functools
import math

import jax
import jax.numpy as jnp
from jax import lax
from jax.experimental import pallas as pl
from jax.experimental.pallas import tpu as pltpu

F32 = jnp.float32
BF16 = jnp.bfloat16
SDS = jax.ShapeDtypeStruct

N_DEV = 8
S = 2048
D = 1024
HEAD_DIM = 64
ROT_DIM = 16
ROPE_THETA = 500000.0
ATT_W = 1024
SSM_W = 512
SSM_GROUPS = 32
SSM_GROUP = 16
SSM_STATE = 64
N_CPLX = SSM_GROUPS * SSM_STATE
POOL_W = 2048
POOL_GROUP = 512
EVEN_IN = 5120
EVEN_OUT = 1536
ODD_IN = 4096
RMS_EPS = 1e-6
LANES = 128
VMEM_LIMIT = 48 * 1024 * 1024

ADAM_LR = 0.001
ADAM_B1 = 0.9
ADAM_B2 = 0.999
ADAM_EPS = 1e-08
ADAM_WD = 0.01
ADAM_STEP = 10

MESH_ID = pl.DeviceIdType.MESH
NN = (((1,), (0,)), ((), ()))
NT = (((1,), (1,)), ((), ()))
TN = (((0,), (0,)), ((), ()))
_DN = {"nn": NN, "nt": NT, "tn": TN}


def _cparams(sem):
    return pltpu.CompilerParams(dimension_semantics=sem, vmem_limit_bytes=VMEM_LIMIT)


MM_TILES = (1024, 768, 512)


def _tile(dim):
    return next((t for t in MM_TILES if dim % t == 0), dim)


def _mm(a, b, mode, out_dtype):
    if mode == "nn":
        (m, k), n = a.shape, b.shape[1]
    elif mode == "nt":
        (m, k), n = a.shape, b.shape[0]
    else:
        (k, m), n = a.shape, b.shape[1]
    tm, tn, tk = _tile(m), _tile(n), _tile(k)
    nk = k // tk

    def body(a_ref, b_ref, o_ref, acc_ref):
        kk = pl.program_id(2)
        part = lax.dot_general(a_ref[...].astype(BF16), b_ref[...].astype(BF16), _DN[mode],
                               preferred_element_type=F32)
        if nk == 1:
            o_ref[...] = part.astype(o_ref.dtype)
            return

        @pl.when(kk == 0)
        def _():
            acc_ref[...] = part

        @pl.when((kk > 0) & (kk < nk - 1))
        def _():
            acc_ref[...] += part

        @pl.when(kk == nk - 1)
        def _():
            o_ref[...] = (acc_ref[...] + part).astype(o_ref.dtype)

    if mode == "nn":
        a_spec = pl.BlockSpec((tm, tk), lambda i, j, kk: (i, kk))
        b_spec = pl.BlockSpec((tk, tn), lambda i, j, kk: (kk, j))
    elif mode == "nt":
        a_spec = pl.BlockSpec((tm, tk), lambda i, j, kk: (i, kk))
        b_spec = pl.BlockSpec((tn, tk), lambda i, j, kk: (j, kk))
    else:
        a_spec = pl.BlockSpec((tk, tm), lambda i, j, kk: (kk, i))
        b_spec = pl.BlockSpec((tk, tn), lambda i, j, kk: (kk, j))
    return pl.pallas_call(
        body, name=f"mm_{mode}_{m}x{k}x{n}",
        grid=(m // tm, n // tn, nk),
        in_specs=[a_spec, b_spec],
        out_specs=pl.BlockSpec((tm, tn), lambda i, j, kk: (i, j)),
        out_shape=SDS((m, n), out_dtype),
        scratch_shapes=[pltpu.VMEM((tm, tn) if nk > 1 else (8, LANES), F32)],
        compiler_params=_cparams(("parallel", "parallel", "arbitrary")),
    )(a, b)


def _gmm(a, b, mode, out_dtype, tm=512):
    ng, gw = POOL_W // POOL_GROUP, POOL_GROUP
    ns = S // tm
    if mode in ("nn", "nt"):
        def body(a_ref, b_ref, o_ref):
            o_ref[...] = lax.dot_general(a_ref[...].astype(BF16), b_ref[...].astype(BF16), _DN[mode],
                                         preferred_element_type=F32).astype(o_ref.dtype)

        return pl.pallas_call(
            body, name=f"gmm_{mode}", grid=(ng, ns),
            in_specs=[pl.BlockSpec((tm, gw), lambda g, i: (i, g)),
                      pl.BlockSpec((None, gw, gw), lambda g, i: (g, 0, 0))],
            out_specs=pl.BlockSpec((tm, gw), lambda g, i: (i, g)),
            out_shape=SDS((S, POOL_W), out_dtype),
            compiler_params=_cparams(("parallel", "parallel")),
        )(a, b)

    def body_tn(a_ref, b_ref, o_ref, acc_ref):
        i = pl.program_id(1)

        @pl.when(i == 0)
        def _():
            acc_ref[...] = jnp.zeros_like(acc_ref)

        acc_ref[...] += lax.dot_general(a_ref[...].astype(BF16), b_ref[...].astype(BF16), TN,
                                        preferred_element_type=F32)

        @pl.when(i == ns - 1)
        def _():
            o_ref[...] = acc_ref[...].astype(o_ref.dtype)

    return pl.pallas_call(
        body_tn, name="gmm_tn", grid=(ng, ns),
        in_specs=[pl.BlockSpec((tm, gw), lambda g, i: (i, g)),
                  pl.BlockSpec((tm, gw), lambda g, i: (i, g))],
        out_specs=pl.BlockSpec((None, gw, gw), lambda g, i: (g, 0, 0)),
        out_shape=SDS((ng, gw, gw), out_dtype),
        scratch_shapes=[pltpu.VMEM((gw, gw), F32)],
        compiler_params=_cparams(("parallel", "arbitrary")),
    )(a, b)


def _rowwise(fn, inputs, out_defs, acc_defs=(), tm=256, name=None):
    n_in, n_out, n_acc = len(inputs), len(out_defs), len(acc_defs)
    in_specs, args = [], []
    for arr, width, cb in inputs:
        if arr.shape[0] == 1:
            in_specs.append(pl.BlockSpec((1, width), lambda i, cb=cb: (0, cb)))
        else:
            in_specs.append(pl.BlockSpec((tm, width), lambda i, cb=cb: (i, cb)))
        args.append(arr)
    out_shape = [SDS((S, w), dt) for w, dt in out_defs] + [SDS((1, w), F32) for w in acc_defs]
    out_specs = ([pl.BlockSpec((tm, w), lambda i: (i, 0)) for w, _ in out_defs]
                 + [pl.BlockSpec((1, w), lambda i: (0, 0)) for w in acc_defs])

    def kern(*refs):
        vals = [r[...] for r in refs[:n_in]]
        outs, accs = fn(*vals)
        for r, v in zip(refs[n_in:n_in + n_out], outs):
            r[...] = v.astype(r.dtype)
        if n_acc:
            acc_refs = refs[n_in + n_out:]

            @pl.when(pl.program_id(0) == 0)
            def _():
                for r in acc_refs:
                    r[...] = jnp.zeros_like(r)

            for r, v in zip(acc_refs, accs):
                r[...] += jnp.sum(v, axis=0, keepdims=True)

    res = pl.pallas_call(
        kern, name=name, grid=(S // tm,), in_specs=in_specs, out_specs=out_specs, out_shape=out_shape,
        compiler_params=_cparams(("arbitrary",)),
    )(*args)
    return res


def _sigmoid(x):
    return 1.0 / (1.0 + jnp.exp(-x))


def _silu_and_grad(x):
    s = _sigmoid(x)
    return x * s, s * (1.0 + x * (1.0 - s))


_GELU_K = math.sqrt(2.0 / math.pi)
_GELU_C = 0.044715


def _gelu_and_grad(x):
    t = jnp.tanh(_GELU_K * (x + _GELU_C * (x * x * x)))
    cdf = 0.5 * (1.0 + t)
    grad = cdf + 0.5 * x * (1.0 - t * t) * (_GELU_K * (1.0 + 3.0 * _GELU_C * x * x))
    return x * cdf, grad


def _rms(xv, gain):
    r = lax.rsqrt(jnp.mean(xv * xv, axis=-1, keepdims=True) + RMS_EPS)
    return xv * r * gain


def _rms_bwd(dout, xv, gain):
    r = lax.rsqrt(jnp.mean(xv * xv, axis=-1, keepdims=True) + RMS_EPS)
    xhat = xv * r
    dxhat = dout * gain
    dx = r * (dxhat - xhat * jnp.mean(dxhat * xhat, axis=-1, keepdims=True))
    return dx, dout * xhat


def _norm_fwd(x, gain):
    (h,) = _rowwise(lambda xv, g: ((_rms(xv, g),), ()), [(x, D, 0), (gain, D, 0)], [(D, BF16)], name="norm_fwd")
    return h


def _post_fwd(x, y, gain):
    (o,) = _rowwise(lambda xv, yv, g: ((xv + _rms(yv, g),), ()), [(x, D, 0), (y, D, 0), (gain, D, 0)],
                    [(D, F32)], name="post_fwd")
    return o


def _post_bwd(g, y, gain):
    def fn(gv, yv, gn):
        dx, dg = _rms_bwd(gv, yv, gn)
        return (dx,), (dg,)

    return _rowwise(fn, [(g, D, 0), (y, D, 0), (gain, D, 0)], [(D, BF16)], [D], name="post_bwd")


def _pre_bwd(g, dh, x, gain):
    def fn(gv, dhv, xv, gn):
        dx, dg = _rms_bwd(dhv, xv, gn)
        return (gv + dx,), (dg,)

    return _rowwise(fn, [(g, D, 0), (dh, D, 0), (x, D, 0), (gain, D, 0)], [(D, F32)], [D], name="pre_bwd")


def _loss_grad(xo, tgt):
    def fn(xv, tv):
        e = xv - tv
        return (e * (1.0 / D),), (e * e,)

    return _rowwise(fn, [(xo, D, 0), (tgt, D, 0)], [(D, F32)], [D], name="loss_grad")


def _pool(u_arr, col_block, transpose, out_dtype, tc=256):
    n_t = POOL_W // tc
    per_group = POOL_GROUP // tc

    def body(u_ref, o_ref):
        c = pl.program_id(0)
        grp = c // per_group
        xv = u_ref[...]
        t = lax.broadcasted_iota(jnp.int32, (S, 1), 0)
        win = jnp.left_shift(2, grp)
        cnt = jnp.minimum(t + 1, win).astype(F32)
        cur = xv / cnt if transpose else xv
        sums = []
        for k in (1, 2, 4, 8):
            if transpose:
                sh = jnp.where(t < S - k, pltpu.roll(cur, S - k, 0), 0.0)
            else:
                sh = jnp.where(t >= k, pltpu.roll(cur, k, 0), 0.0)
            cur = cur + sh
            sums.append(cur)
        tot = jnp.where(grp == 0, sums[0], jnp.where(grp == 1, sums[1], jnp.where(grp == 2, sums[2], sums[3])))
        res = tot - xv if transpose else tot / cnt - xv
        o_ref[...] = res.astype(o_ref.dtype)

    return pl.pallas_call(
        body, name="pool_bwd" if transpose else "pool_fwd", grid=(n_t,),
        in_specs=[pl.BlockSpec((S, tc), lambda c: (0, col_block * n_t + c))],
        out_specs=pl.BlockSpec((S, tc), lambda c: (0, c)),
        out_shape=SDS((S, POOL_W), out_dtype),
        compiler_params=_cparams(("parallel",)),
    )(u_arr)


def _rope_tables():
    pos = jnp.arange(S, dtype=jnp.int32).astype(F32)
    inv_freq = ROPE_THETA ** (-jnp.arange(0, ROT_DIM, 2, dtype=F32) / ROT_DIM)
    ang = pos[:, None] * inv_freq[None, :]
    cos8, sin8 = jnp.cos(ang), jnp.sin(ang)
    half = ROT_DIM // 2
    zeros = jnp.zeros((S, HEAD_DIM - ROT_DIM), F32)
    cos = jnp.concatenate([cos8, cos8, jnp.ones((S, HEAD_DIM - ROT_DIM), F32)], axis=1)
    lo = jnp.concatenate([-sin8, jnp.zeros((S, half), F32), zeros], axis=1)
    hi = jnp.concatenate([jnp.zeros((S, half), F32), sin8, zeros], axis=1)
    rep = LANES // HEAD_DIM
    return jnp.tile(cos, (1, rep)), jnp.tile(lo, (1, rep)), jnp.tile(hi, (1, rep))


def _rope(x_arr, col_tile0, tables, scale, transpose, out_dtype):
    cos, lo, hi = tables
    sign = -1.0 if transpose else 1.0
    half = ROT_DIM // 2

    def body(x_ref, c_ref, lo_ref, hi_ref, o_ref):
        xv = x_ref[...].astype(F32)
        up = pltpu.roll(xv, LANES - half, 1)
        dn = pltpu.roll(xv, half, 1)
        res = xv * c_ref[...] + sign * (up * lo_ref[...] + dn * hi_ref[...])
        o_ref[...] = (res * scale).astype(o_ref.dtype)

    n_t = ATT_W // LANES
    tab = pl.BlockSpec((S, LANES), lambda c: (0, 0))
    return pl.pallas_call(
        body, name="rope_bwd" if transpose else "rope_fwd", grid=(n_t,),
        in_specs=[pl.BlockSpec((S, LANES), lambda c: (0, col_tile0 + c)), tab, tab, tab],
        out_specs=pl.BlockSpec((S, LANES), lambda c: (0, c)),
        out_shape=SDS((S, ATT_W), out_dtype),
        compiler_params=_cparams(("parallel",)),
    )(x_arr, cos, lo, hi)


ATT_T = 256


def _multiplicity(delta):
    ok = delta >= 0
    near = jnp.where(ok & (delta <= 128), 1.0, 0.0)
    mid = jnp.where(ok & (delta <= 512) & ((delta & 3) == 0), 1.0, 0.0)
    far = jnp.where(ok & ((delta & 15) == 0), 1.0, 0.0)
    return near + mid + far


def _head_split(v, first):
    zero = jnp.zeros_like(v)
    return [jnp.where(first, v, zero), jnp.where(first, zero, v)]


def _flash_fwd(q, k, v):
    t = ATT_T
    n_hp = ATT_W // LANES

    def body(q_ref, k_ref, v_ref, o_ref, lse_ref):
        i = pl.program_id(1)
        first = lax.broadcasted_iota(jnp.int32, (1, LANES), 1) < HEAD_DIM
        qs = _head_split(q_ref[...], first)
        base = (lax.broadcasted_iota(jnp.int32, (t, t), 0) - lax.broadcasted_iota(jnp.int32, (t, t), 1))

        def kv_step(j, carry):
            m0, l0, m1, l1, acc = carry
            off = pl.multiple_of(j * t, t)
            kb = k_ref[pl.ds(off, t), :]
            vs = _head_split(v_ref[pl.ds(off, t), :], first)
            mult = _multiplicity(base + (i - j) * t)
            valid = mult > 0.0
            new = []
            pv = None
            for h, (m_prev, l_prev) in enumerate(((m0, l0), (m1, l1))):
                s = lax.dot_general(qs[h], kb, NT, preferred_element_type=F32)
                s = jnp.where(valid, s, -1e30)
                m_new = jnp.maximum(m_prev, jnp.max(s, axis=1, keepdims=True))
                p = jnp.exp(s - m_new) * mult
                alpha = jnp.exp(m_prev - m_new)
                l_new = alpha * l_prev + jnp.sum(p, axis=1, keepdims=True)
                d = lax.dot_general(p.astype(BF16), vs[h], NN, preferred_element_type=F32)
                pv = d if pv is None else pv + d
                new.append((m_new, l_new, alpha))
            acc = acc * jnp.where(first, new[0][2], new[1][2]) + pv
            return new[0][0], new[0][1], new[1][0], new[1][1], acc

        neg = jnp.full((t, 1), -1e30, F32)
        zero = jnp.zeros((t, 1), F32)
        m0, l0, m1, l1, acc = lax.fori_loop(0, i + 1, kv_step, (neg, zero, neg, zero, jnp.zeros((t, LANES), F32)))
        o_ref[...] = acc * jnp.where(first, 1.0 / l0, 1.0 / l1)
        lse_ref[...] = jnp.where(first, m0 + jnp.log(l0), m1 + jnp.log(l1))

    blk = pl.BlockSpec((t, LANES), lambda hp, i: (i, hp))
    full = pl.BlockSpec((S, LANES), lambda hp, i: (0, hp))
    return pl.pallas_call(
        body, name="flash_fwd", grid=(n_hp, S // t),
        in_specs=[blk, full, full], out_specs=[blk, blk],
        out_shape=[SDS((S, ATT_W), F32), SDS((S, ATT_W), F32)],
        compiler_params=_cparams(("parallel", "arbitrary")),
    )(q, k, v)


def _flash_bwd(q, k, v, o, do, lse):
    t = ATT_T
    n_hp = ATT_W // LANES
    n_t = S // t

    def body(q_ref, k_ref, v_ref, o_ref, do_ref, lse_ref, dq_ref, dk_ref, dv_ref):
        j = pl.program_id(1)
        first = lax.broadcasted_iota(jnp.int32, (1, LANES), 1) < HEAD_DIM

        @pl.when(j == 0)
        def _():
            dq_ref[...] = jnp.zeros_like(dq_ref)

        kb = k_ref[...]
        vb = v_ref[...]
        ks = _head_split(kb, first)
        base = (lax.broadcasted_iota(jnp.int32, (t, t), 0) - lax.broadcasted_iota(jnp.int32, (t, t), 1))

        def q_step(i, carry):
            dk_acc, dv_acc = carry
            rows = pl.ds(pl.multiple_of(i * t, t), t)
            qs = _head_split(q_ref[rows, :], first)
            dob = do_ref[rows, :]
            prod = dob * o_ref[rows, :]
            d_all = jnp.sum(prod, axis=1, keepdims=True)
            d0 = jnp.sum(jnp.where(first, prod, 0.0), axis=1, keepdims=True)
            lse_b = lse_ref[rows, :]
            lse0 = jnp.max(jnp.where(first, lse_b, -jnp.inf), axis=1, keepdims=True)
            lse1 = jnp.max(jnp.where(first, -jnp.inf, lse_b), axis=1, keepdims=True)
            dos = _head_split(dob.astype(BF16), first)
            mult = _multiplicity(base + (i - j) * t)
            valid = mult > 0.0
            dq_t = jnp.zeros((t, LANES), F32)
            for h, (lse_h, d_h) in enumerate(((lse0, d0), (lse1, d_all - d0))):
                s = lax.dot_general(qs[h], kb, NT, preferred_element_type=F32)
                p = jnp.exp(jnp.where(valid, s, -1e30) - lse_h) * mult
                dp = lax.dot_general(dos[h], vb, NT, preferred_element_type=F32)
                ds = (p * (dp - d_h)).astype(BF16)
                dv_acc = dv_acc + lax.dot_general(p.astype(BF16), dos[h], TN, preferred_element_type=F32)
                dk_acc = dk_acc + lax.dot_general(ds, qs[h], TN, preferred_element_type=F32)
                dq_t = dq_t + lax.dot_general(ds, ks[h], NN, preferred_element_type=F32)
            dq_ref[rows, :] += dq_t
            return dk_acc, dv_acc

        zero = jnp.zeros((t, LANES), F32)
        dk_acc, dv_acc = lax.fori_loop(j, n_t, q_step, (zero, zero))
        dk_ref[...] = dk_acc
        dv_ref[...] = dv_acc

    blk = pl.BlockSpec((t, LANES), lambda hp, j: (j, hp))
    full = pl.BlockSpec((S, LANES), lambda hp, j: (0, hp))
    return pl.pallas_call(
        body, name="flash_bwd", grid=(n_hp, n_t),
        in_specs=[full, blk, blk, full, full, full], out_specs=[full, blk, blk],
        out_shape=[SDS((S, ATT_W), F32)] * 3,
        compiler_params=_cparams(("parallel", "arbitrary")),
    )(q, k, v, o, do, lse)


SCAN_T = 256
ST_ROWS = 2 * N_CPLX // LANES
HALF = ST_ROWS // 2


def _scan_fwd(lam, bu):
    def body(lam_ref, bu_ref, st_ref, carry):
        @pl.when(pl.program_id(0) == 0)
        def _():
            carry[...] = jnp.zeros_like(carry)

        ar, ai = lam_ref[0:HALF, :], lam_ref[HALF:ST_ROWS, :]

        def step(t, c):
            sr, si = c
            b = bu_ref[t]
            nr = ar * sr - ai * si + b[0:HALF]
            ni = ar * si + ai * sr + b[HALF:ST_ROWS]
            st_ref[t, 0:HALF, :] = nr
            st_ref[t, HALF:ST_ROWS, :] = ni
            return nr, ni

        sr, si = lax.fori_loop(0, SCAN_T, step, (carry[0:HALF, :], carry[HALF:ST_ROWS, :]), unroll=8)
        carry[0:HALF, :] = sr
        carry[HALF:ST_ROWS, :] = si

    blk = pl.BlockSpec((SCAN_T, ST_ROWS, LANES), lambda i: (i, 0, 0))
    return pl.pallas_call(
        body, name="scan_fwd", grid=(S // SCAN_T,),
        in_specs=[pl.BlockSpec((ST_ROWS, LANES), lambda i: (0, 0)), blk], out_specs=blk,
        out_shape=SDS((S, ST_ROWS, LANES), F32),
        scratch_shapes=[pltpu.VMEM((ST_ROWS, LANES), F32)],
        compiler_params=_cparams(("arbitrary",)),
    )(lam, bu)


def _scan_bwd(lam, dst, states):
    n_blk = S // SCAN_T

    def body(lam_ref, d_ref, st_ref, g_ref, dlam_ref, carry):
        @pl.when(pl.program_id(0) == 0)
        def _():
            carry[...] = jnp.zeros_like(carry)
            dlam_ref[...] = jnp.zeros_like(dlam_ref)

        ar, ai = lam_ref[0:HALF, :], lam_ref[HALF:ST_ROWS, :]

        def step(kk, c):
            t = SCAN_T - 1 - kk
            gr, gi, dar, dai = c
            x = st_ref[t]
            xr, xi = x[0:HALF], x[HALF:ST_ROWS]
            dar = dar + gr * xr + gi * xi
            dai = dai + gi * xr - gr * xi
            d = d_ref[t]
            ngr = d[0:HALF] + ar * gr + ai * gi
            ngi = d[HALF:ST_ROWS] + ar * gi - ai * gr
            g_ref[t, 0:HALF, :] = ngr
            g_ref[t, HALF:ST_ROWS, :] = ngi
            return ngr, ngi, dar, dai

        zero = jnp.zeros((HALF, LANES), F32)
        gr, gi, dar, dai = lax.fori_loop(0, SCAN_T, step, (carry[0:HALF, :], carry[HALF:ST_ROWS, :], zero, zero),
                                         unroll=8)
        carry[0:HALF, :] = gr
        carry[HALF:ST_ROWS, :] = gi
        dlam_ref[0:HALF, :] += dar
        dlam_ref[HALF:ST_ROWS, :] += dai

    blk = pl.BlockSpec((SCAN_T, ST_ROWS, LANES), lambda i: (n_blk - 1 - i, 0, 0))
    small = pl.BlockSpec((ST_ROWS, LANES), lambda i: (0, 0))
    return pl.pallas_call(
        body, name="scan_bwd", grid=(n_blk,),
        in_specs=[small, blk, blk], out_specs=[blk, small],
        out_shape=[SDS((S, ST_ROWS, LANES), F32), SDS((ST_ROWS, LANES), F32)],
        scratch_shapes=[pltpu.VMEM((ST_ROWS, LANES), F32)],
        compiler_params=_cparams(("arbitrary",)),
    )(lam, dst, states)


def _ssm_prep(a_re, a_im, log_dt, b_re, b_im, c_re, c_im):
    lam = lax.complex(a_re, a_im)
    dt = jnp.exp(log_dt)[:, None]
    lam_bar = jnp.exp(lam * dt)
    b_bar = ((lam_bar - 1.0) / lam)[..., None] * lax.complex(b_re, b_im)
    lam_t = jnp.concatenate([jnp.real(lam_bar).reshape(HALF, LANES), jnp.imag(lam_bar).reshape(HALF, LANES)], axis=0)
    eye = jnp.eye(SSM_GROUPS, dtype=F32)

    def in_map(m):
        return jnp.einsum("gph,gk->ghkp", m, eye).reshape(SSM_W, N_CPLX)

    def out_map(m):
        return jnp.einsum("ghp,gk->gpkh", m, eye).reshape(N_CPLX, SSM_W)

    w_b = jnp.concatenate([in_map(jnp.real(b_bar)), in_map(jnp.imag(b_bar))], axis=1)
    w_c = jnp.concatenate([out_map(c_re), -out_map(c_im)], axis=0)
    return lam_t, w_b, w_c


def _row(v):
    return v.reshape(1, -1)


def _even_fwd(x, pre, post, w_in, w_out, glu_w, glu_b, ssm_d, prep, tables):
    lam_t, w_b, w_c = prep
    h = _norm_fwd(x, pre)
    proj = _mm(h, w_in, "nn", F32)
    q = _rope(proj, 0, tables, HEAD_DIM ** -0.5, False, BF16)
    k = _rope(proj, ATT_W // LANES, tables, 1.0, False, BF16)
    v = proj[:, 2 * ATT_W:3 * ATT_W].astype(BF16)
    att, lse = _flash_fwd(q, k, v)
    u_ssm = proj[:, 4 * ATT_W:4 * ATT_W + SSM_W]
    bu = _mm(u_ssm, w_b, "nn", F32)
    states = _scan_fwd(lam_t, bu.reshape(S, ST_ROWS, LANES))
    y = _mm(states.reshape(S, 2 * N_CPLX), w_c, "nn", F32)

    def act1(yv, uv, dv):
        return (_gelu_and_grad(yv + dv * uv)[0],), ()

    (z1,) = _rowwise(act1, [(y, SSM_W, 0), (proj, SSM_W, 8), (ssm_d, SSM_W, 0)], [(SSM_W, F32)], name="ssm_act_fwd")
    lin = _mm(z1, glu_w, "nn", F32)

    def gate(att_v, ga, gs, z1v, linv, bv):
        ssm_out = z1v * _sigmoid(linv + bv)
        return (jnp.concatenate([att_v * _silu_and_grad(ga)[0], ssm_out * _silu_and_grad(gs)[0]], axis=1),), ()

    (merged,) = _rowwise(gate, [(att, ATT_W, 0), (proj, ATT_W, 3), (proj, SSM_W, 9), (z1, SSM_W, 0),
                                (lin, SSM_W, 0), (glu_b, SSM_W, 0)], [(EVEN_OUT, BF16)], name="even_gate_fwd")
    yout = _mm(merged, w_out, "nn", F32)
    x_next = _post_fwd(x, yout, post)
    saved = (x, h, proj, q, k, v, att, lse, states, y, z1, lin, merged, yout)
    return x_next, saved


def _even_bwd(g, saved, pre, post, w_in, w_out, glu_w, glu_b, ssm_d, prep, tables):
    x, h, proj, q, k, v, att, lse, states, y, z1, lin, merged, yout = saved
    lam_t, w_b, w_c = prep
    dyout, dpost = _post_bwd(g, yout, post)
    dmerged = _mm(dyout, w_out, "nt", F32)
    dw_out = _mm(merged, dyout, "tn", BF16)

    def gate_bwd(dm_a, dm_s, att_v, ga, gs, z1v, linv, bv):
        sa, dsa = _silu_and_grad(ga)
        ss, dss = _silu_and_grad(gs)
        sig = _sigmoid(linv + bv)
        ssm_out = z1v * sig
        dssm = dm_s * ss
        dlin = dssm * z1v * sig * (1.0 - sig)
        return (dm_a * sa, dm_a * att_v * dsa, dm_s * ssm_out * dss, dssm * sig, dlin), (dlin,)

    datt, dg_att, dg_ssm, dz1a, dlin, dglu_b = _rowwise(
        gate_bwd, [(dmerged, ATT_W, 0), (dmerged, SSM_W, 2), (att, ATT_W, 0), (proj, ATT_W, 3), (proj, SSM_W, 9),
                   (z1, SSM_W, 0), (lin, SSM_W, 0), (glu_b, SSM_W, 0)],
        [(ATT_W, F32), (ATT_W, BF16), (SSM_W, BF16), (SSM_W, F32), (SSM_W, BF16)], [SSM_W], name="even_gate_bwd")
    dz1b = _mm(dlin, glu_w, "nt", F32)
    dglu_w = _mm(z1, dlin, "tn", BF16)

    def act1_bwd(da, db, yv, uv, dv):
        dpre = (da + db) * _gelu_and_grad(yv + dv * uv)[1]
        return (dpre, dpre * dv), (dpre * uv,)

    dy, du_direct, dd = _rowwise(act1_bwd, [(dz1a, SSM_W, 0), (dz1b, SSM_W, 0), (y, SSM_W, 0), (proj, SSM_W, 8),
                                            (ssm_d, SSM_W, 0)], [(SSM_W, BF16), (SSM_W, F32)], [SSM_W],
                                 name="ssm_act_bwd")
    dst = _mm(dy, w_c, "nt", F32)
    dw_c = _mm(states.reshape(S, 2 * N_CPLX), dy, "tn", F32)
    gst, dlam = _scan_bwd(lam_t, dst.reshape(S, ST_ROWS, LANES), states)
    dbu = gst.reshape(S, 2 * N_CPLX)
    u_ssm = proj[:, 4 * ATT_W:4 * ATT_W + SSM_W]
    du_ssm = _mm(dbu, w_b, "nt", F32) + du_direct
    dw_b = _mm(u_ssm, dbu, "tn", F32)
    dq, dk, dv = _flash_bwd(q, k, v, att, datt, lse)
    dq = _rope(dq, 0, tables, HEAD_DIM ** -0.5, True, BF16)
    dk = _rope(dk, 0, tables, 1.0, True, BF16)
    dproj = jnp.concatenate([dq, dk, dv.astype(BF16), dg_att, du_ssm.astype(BF16), dg_ssm], axis=1)
    dw_in = _mm(h, dproj, "tn", BF16)
    dh = _mm(dproj, w_in, "nt", F32)
    g_prev, dpre = _pre_bwd(g, dh, x, pre)
    return g_prev, dict(pre=dpre, post=dpost, w_in=dw_in, w_out=dw_out, glu_w=dglu_w, glu_b=dglu_b, ssm_d=dd,
                        prep=(dlam, dw_b, dw_c))


def _odd_fwd(x, pre, post, w_in, pool_w, pool_scale, w_out):
    h = _norm_fwd(x, pre)
    proj = _mm(h, w_in, "nn", F32)
    mixed = _pool(proj, 0, False, BF16)
    ylin = _gmm(mixed, pool_w, "nn", F32)

    def gate(yl, gt, sc):
        return (yl * sc * _silu_and_grad(gt)[0],), ()

    (z,) = _rowwise(gate, [(ylin, POOL_W, 0), (proj, POOL_W, 1), (pool_scale, POOL_W, 0)], [(POOL_W, BF16)],
                    name="odd_gate_fwd")
    yout = _mm(z, w_out, "nn", F32)
    x_next = _post_fwd(x, yout, post)
    return x_next, (x, h, proj, mixed, ylin, z, yout)


def _odd_bwd(g, saved, pre, post, w_in, pool_w, pool_scale, w_out):
    x, h, proj, mixed, ylin, z, yout = saved
    dyout, dpost = _post_bwd(g, yout, post)
    dz = _mm(dyout, w_out, "nt", F32)
    dw_out = _mm(z, dyout, "tn", BF16)

    def gate_bwd(dzv, yl, gt, sc):
        sg, dsg = _silu_and_grad(gt)
        tt = dzv * sg
        return (tt * sc, dzv * yl * sc * dsg), (tt * yl,)

    dylin, dgate, dscale = _rowwise(gate_bwd, [(dz, POOL_W, 0), (ylin, POOL_W, 0), (proj, POOL_W, 1),
                                               (pool_scale, POOL_W, 0)], [(POOL_W, BF16), (POOL_W, BF16)], [POOL_W],
                                    name="odd_gate_bwd")
    dmixed = _gmm(dylin, pool_w, "nt", F32)
    dpool_w = _gmm(mixed, dylin, "tn", BF16)
    du = _pool(dmixed, 0, True, BF16)
    dproj = jnp.concatenate([du, dgate], axis=1)
    dw_in = _mm(h, dproj, "tn", BF16)
    dh = _mm(dproj, w_in, "nt", F32)
    g_prev, dpre = _pre_bwd(g, dh, x, pre)
    return g_prev, dict(pre=dpre, post=dpost, w_in=dw_in, w_out=dw_out, pool_w=dpool_w, pool_scale=dscale)


def _my_index():
    return 4 * lax.axis_index("x") + 2 * lax.axis_index("y") + lax.axis_index("c")


def _exchange(arrs, gather, name):
    n = len(arrs)
    out_shape = [SDS((N_DEV,) + a.shape, a.dtype) if gather else SDS(a.shape, a.dtype) for a in arrs]

    def body(*refs):
        ins, outs = refs[:n], refs[n:2 * n]
        send_sems, recv_sems, local_sems = refs[2 * n:]
        me = _my_index()

        def src(i, j):
            return ins[i] if gather else ins[i].at[j]

        def remote(i, j, src_slot, dst_slot, recv_slot):
            return pltpu.make_async_remote_copy(
                src_ref=src(i, src_slot), dst_ref=outs[i].at[dst_slot], send_sem=send_sems.at[i, j],
                recv_sem=recv_sems.at[i, recv_slot], device_id=(j // 4, (j // 2) % 2, j % 2), device_id_type=MESH_ID)

        def local(i):
            return pltpu.make_async_copy(src(i, me), outs[i].at[me], local_sems.at[i])

        for i in range(n):
            local(i).start()
        for j in range(N_DEV):
            @pl.when(me != j)
            def _(j=j):
                for i in range(n):
                    remote(i, j, j, me, me).start()
        for j in range(N_DEV):
            @pl.when(me != j)
            def _(j=j):
                for i in range(n):
                    remote(i, j, j, me, me).wait_send()
                    remote(i, j, j, j, j).wait_recv()
        for i in range(n):
            local(i).wait()

    any_spec = pl.BlockSpec(memory_space=pl.ANY)
    return pl.pallas_call(
        body, name=name, in_specs=[any_spec] * n, out_specs=[any_spec] * n, out_shape=out_shape,
        scratch_shapes=[pltpu.SemaphoreType.DMA((n, N_DEV)), pltpu.SemaphoreType.DMA((n, N_DEV)),
                        pltpu.SemaphoreType.DMA((n,))],
    )(*arrs)


def _adam(w, gslots, m, v, name):
    r, c = w.shape
    ns = gslots.shape[0]
    tr = r
    while tr * c * 4 > (1 << 20) and tr % 16 == 0:
        tr //= 2
    assert r % tr == 0

    def body(w_ref, g_ref, m_ref, v_ref, go_ref, d_ref, mo_ref, vo_ref):
        g = g_ref[0].astype(F32)
        for s in range(1, ns):
            g = g + g_ref[s].astype(F32)
        wv = w_ref[...]
        mn = ADAM_B1 * m_ref[...] + (1.0 - ADAM_B1) * g
        vn = ADAM_B2 * v_ref[...] + (1.0 - ADAM_B2) * (g * g)
        m_hat = mn / (1.0 - ADAM_B1 ** ADAM_STEP)
        v_hat = vn / (1.0 - ADAM_B2 ** ADAM_STEP)
        go_ref[...] = g
        d_ref[...] = -ADAM_LR * (m_hat / (jnp.sqrt(v_hat) + ADAM_EPS) + ADAM_WD * wv)
        mo_ref[...] = mn
        vo_ref[...] = vn

    blk = pl.BlockSpec((tr, c), lambda i: (i, 0))
    return pl.pallas_call(
        body, name=name, grid=(r // tr,),
        in_specs=[blk, pl.BlockSpec((ns, tr, c), lambda i: (0, i, 0)), blk, blk],
        out_specs=[blk] * 4, out_shape=[SDS((r, c), F32)] * 4,
        compiler_params=_cparams(("parallel",)),
    )(w, gslots, m, v)


SMALL_NAMES = ("pre_norm", "post_norm", "ssm_a_re", "ssm_a_im", "ssm_log_dt", "ssm_b_re", "ssm_b_im", "ssm_c_re",
               "ssm_c_im", "ssm_d", "ssm_glu_b")
SHARDED_NAMES = ("even_w_in", "even_w_out", "ssm_glu_w", "odd_w_in", "pool_w", "odd_w_out")
WEIGHT_ORDER = ("pre_norm", "post_norm", "even_w_in", "even_w_out", "ssm_a_re", "ssm_a_im", "ssm_log_dt", "ssm_b_re",
                "ssm_b_im", "ssm_c_re", "ssm_c_im", "ssm_d", "ssm_glu_w", "ssm_glu_b", "odd_w_in", "pool_w",
                "pool_scale", "odd_w_out")
PACK_ROWS_ALIGN = 8


def _pack(parts):
    flat = jnp.concatenate([p.reshape(-1).astype(F32) for p in parts])
    rows = -(-flat.shape[0] // (LANES * PACK_ROWS_ALIGN)) * PACK_ROWS_ALIGN
    return jnp.pad(flat, (0, rows * LANES - flat.shape[0])).reshape(rows, LANES)


def _unpack(packed, shapes):
    flat = packed.reshape(-1)
    out, off = [], 0
    for shp in shapes:
        size = math.prod(shp)
        out.append(flat[off:off + size].reshape(shp))
        off += size
    return out


def _local_step(x, tgt, small, full):
    tables = _rope_tables()
    preps, prep_vjps = [], []
    for i in range(2):
        out, vjp = jax.vjp(_ssm_prep, small["ssm_a_re"][i], small["ssm_a_im"][i], small["ssm_log_dt"][i],
                           small["ssm_b_re"][i], small["ssm_b_im"][i], small["ssm_c_re"][i], small["ssm_c_im"][i])
        preps.append(out)
        prep_vjps.append(vjp)

    def even_args(i):
        layer = 2 * i
        return (_row(small["pre_norm"][layer]), _row(small["post_norm"][layer]), full["even_w_in"][i],
                full["even_w_out"][i], full["ssm_glu_w"][i], _row(small["ssm_glu_b"][i]), _row(small["ssm_d"][i]),
                preps[i], tables)

    def odd_args(i):
        layer = 2 * i + 1
        return (_row(small["pre_norm"][layer]), _row(small["post_norm"][layer]), full["odd_w_in"][i],
                full["pool_w"][i], _row(full["pool_scale"][i]), full["odd_w_out"][i])

    saved = []
    cur = x
    for layer in range(4):
        if layer % 2 == 0:
            cur, sv = _even_fwd(cur, *even_args(layer // 2))
        else:
            cur, sv = _odd_fwd(cur, *odd_args(layer // 2))
        saved.append(sv)
    g, sq = _loss_grad(cur, tgt)
    loss = 0.5 * jnp.sum(sq) / D

    lg = [None] * 4
    for layer in reversed(range(4)):
        if layer % 2 == 0:
            g, lg[layer] = _even_bwd(g, saved[layer], *even_args(layer // 2))
        else:
            g, lg[layer] = _odd_bwd(g, saved[layer], *odd_args(layer // 2))

    ssm_g = [prep_vjps[i](lg[2 * i]["prep"]) for i in range(2)]
    grads = {
        "pre_norm": jnp.concatenate([lg[l]["pre"] for l in range(4)], axis=0),
        "post_norm": jnp.concatenate([lg[l]["post"] for l in range(4)], axis=0),
        "ssm_d": jnp.concatenate([lg[0]["ssm_d"], lg[2]["ssm_d"]], axis=0),
        "ssm_glu_b": jnp.concatenate([lg[0]["glu_b"], lg[2]["glu_b"]], axis=0),
        "pool_scale": jnp.concatenate([lg[1]["pool_scale"], lg[3]["pool_scale"]], axis=0),
        "even_w_in": jnp.stack([lg[0]["w_in"], lg[2]["w_in"]]),
        "even_w_out": jnp.stack([lg[0]["w_out"], lg[2]["w_out"]]),
        "ssm_glu_w": jnp.stack([lg[0]["glu_w"], lg[2]["glu_w"]]),
        "odd_w_in": jnp.stack([lg[1]["w_in"], lg[3]["w_in"]]),
        "pool_w": jnp.stack([lg[1]["pool_w"], lg[3]["pool_w"]]),
        "odd_w_out": jnp.stack([lg[1]["w_out"], lg[3]["w_out"]]),
    }
    for idx, nm in enumerate(("ssm_a_re", "ssm_a_im", "ssm_log_dt", "ssm_b_re", "ssm_b_im", "ssm_c_re", "ssm_c_im")):
        grads[nm] = jnp.stack([ssm_g[0][idx], ssm_g[1][idx]])
    return loss, g, grads


def _to_slots(name, gfull):
    if name in ("even_w_in", "odd_w_in"):
        two, kk, nn = gfull.shape
        return gfull.reshape(two, kk, N_DEV, nn // N_DEV).transpose(2, 0, 1, 3)
    if name in ("even_w_out", "odd_w_out", "ssm_glu_w"):
        two, rr, nn = gfull.shape
        return gfull.reshape(two, N_DEV, rr // N_DEV, nn).transpose(1, 0, 2, 3)
    assert name == "pool_w"
    two, gg, rr, nn = gfull.shape
    return gfull.reshape(two, gg, N_DEV, rr // N_DEV, nn).transpose(2, 0, 1, 3, 4)


def _from_gathered(name, gat):
    if name in ("even_w_in", "odd_w_in"):
        _, two, kk, nn = gat.shape
        return gat.transpose(1, 2, 0, 3).reshape(two, kk, N_DEV * nn)
    if name in ("even_w_out", "odd_w_out", "ssm_glu_w"):
        _, two, rr, nn = gat.shape
        return gat.transpose(1, 0, 2, 3).reshape(two, N_DEV * rr, nn)
    assert name == "pool_w"
    _, two, gg, rr, nn = gat.shape
    return gat.transpose(1, 2, 0, 3, 4).reshape(two, gg, N_DEV * rr, nn)


def kernel(x, pre_norm, post_norm, even_w_in, even_w_out, ssm_a_re, ssm_a_im, ssm_log_dt, ssm_b_re, ssm_b_im, ssm_c_re, ssm_c_im, ssm_d, ssm_glu_w, ssm_glu_b, odd_w_in, pool_w, pool_scale, odd_w_out, loss_target, m_pre_norm, m_post_norm, m_even_w_in, m_even_w_out, m_ssm_a_re, m_ssm_a_im, m_ssm_log_dt, m_ssm_b_re, m_ssm_b_im, m_ssm_c_re, m_ssm_c_im, m_ssm_d, m_ssm_glu_w, m_ssm_glu_b, m_odd_w_in, m_pool_w, m_pool_scale, m_odd_w_out, v_pre_norm, v_post_norm, v_even_w_in, v_even_w_out, v_ssm_a_re, v_ssm_a_im, v_ssm_log_dt, v_ssm_b_re, v_ssm_b_im, v_ssm_c_re, v_ssm_c_im, v_ssm_d, v_ssm_glu_w, v_ssm_glu_b, v_odd_w_in, v_pool_w, v_pool_scale, v_odd_w_out):
    w = dict(pre_norm=pre_norm, post_norm=post_norm, even_w_in=even_w_in, even_w_out=even_w_out, ssm_a_re=ssm_a_re,
             ssm_a_im=ssm_a_im, ssm_log_dt=ssm_log_dt, ssm_b_re=ssm_b_re, ssm_b_im=ssm_b_im, ssm_c_re=ssm_c_re,
             ssm_c_im=ssm_c_im, ssm_d=ssm_d, ssm_glu_w=ssm_glu_w, ssm_glu_b=ssm_glu_b, odd_w_in=odd_w_in,
             pool_w=pool_w, pool_scale=pool_scale, odd_w_out=odd_w_out)
    mom = dict(pre_norm=m_pre_norm, post_norm=m_post_norm, even_w_in=m_even_w_in, even_w_out=m_even_w_out,
               ssm_a_re=m_ssm_a_re, ssm_a_im=m_ssm_a_im, ssm_log_dt=m_ssm_log_dt, ssm_b_re=m_ssm_b_re,
               ssm_b_im=m_ssm_b_im, ssm_c_re=m_ssm_c_re, ssm_c_im=m_ssm_c_im, ssm_d=m_ssm_d, ssm_glu_w=m_ssm_glu_w,
               ssm_glu_b=m_ssm_glu_b, odd_w_in=m_odd_w_in, pool_w=m_pool_w, pool_scale=m_pool_scale,
               odd_w_out=m_odd_w_out)
    var = dict(pre_norm=v_pre_norm, post_norm=v_post_norm, even_w_in=v_even_w_in, even_w_out=v_even_w_out,
               ssm_a_re=v_ssm_a_re, ssm_a_im=v_ssm_a_im, ssm_log_dt=v_ssm_log_dt, ssm_b_re=v_ssm_b_re,
               ssm_b_im=v_ssm_b_im, ssm_c_re=v_ssm_c_re, ssm_c_im=v_ssm_c_im, ssm_d=v_ssm_d, ssm_glu_w=v_ssm_glu_w,
               ssm_glu_b=v_ssm_glu_b, odd_w_in=v_odd_w_in, pool_w=v_pool_w, pool_scale=v_pool_scale,
               odd_w_out=v_odd_w_out)
    me = _my_index()
    scale_cols = pool_scale.shape[1]

    scale_pad = jnp.pad(pool_scale, ((0, PACK_ROWS_ALIGN - pool_scale.shape[0]), (0, 0)))
    gathered = _exchange([w[nm].astype(BF16) for nm in SHARDED_NAMES] + [scale_pad], True, "gather_weights")
    full = {nm: _from_gathered(nm, gat) for nm, gat in zip(SHARDED_NAMES, gathered[:-1])}
    full["pool_scale"] = gathered[-1][:, :2, :].transpose(1, 0, 2).reshape(2, N_DEV * scale_cols)
    small = {nm: w[nm] for nm in SMALL_NAMES}

    loss_local, grad_x, grads = _local_step(x[0], loss_target[0], small, full)
    loss = lax.psum(loss_local, ("x", "y", "c"))

    small_parts = [grads[nm] for nm in SMALL_NAMES] + [grads["pool_scale"]]
    small_shapes = [w[nm].shape for nm in SMALL_NAMES] + [(2, N_DEV * scale_cols)]
    packed = _pack(small_parts)
    slots = _exchange([_to_slots(nm, grads[nm]) for nm in SHARDED_NAMES], False, "scatter_grads")
    (small_slots,) = _exchange([packed], True, "gather_small_grads")

    res = {}
    for nm, sl in zip(SHARDED_NAMES, slots):
        shp = w[nm].shape
        cols = shp[-1]
        outs = _adam(w[nm].reshape(-1, cols), sl.reshape(N_DEV, -1, cols), mom[nm].reshape(-1, cols),
                     var[nm].reshape(-1, cols), name=f"adam_{nm}")
        res[nm] = [o.reshape(shp) for o in outs]
    w_pack = _pack([w[nm] for nm in SMALL_NAMES] + [jnp.zeros((2, N_DEV * scale_cols), F32)])
    m_pack = _pack([mom[nm] for nm in SMALL_NAMES] + [jnp.zeros((2, N_DEV * scale_cols), F32)])
    v_pack = _pack([var[nm] for nm in SMALL_NAMES] + [jnp.zeros((2, N_DEV * scale_cols), F32)])
    outs = _adam(w_pack, small_slots, m_pack, v_pack, name="adam_small")
    unpacked = [_unpack(o, small_shapes) for o in outs]
    for idx, nm in enumerate(SMALL_NAMES):
        res[nm] = [unpacked[kind][idx] for kind in range(4)]
    g_scale = lax.dynamic_slice_in_dim(unpacked[0][-1], me * scale_cols, scale_cols, axis=1)
    pad = ((0, PACK_ROWS_ALIGN - 2), (0, 0))
    outs = _adam(jnp.pad(pool_scale, pad), jnp.pad(g_scale, pad)[None], jnp.pad(m_pool_scale, pad),
                 jnp.pad(v_pool_scale, pad), name="adam_pool_scale")
    res["pool_scale"] = [o[:2] for o in outs]

    out = [loss, grad_x[None]]
    for kind in range(4):
        out += [res[nm][kind] for nm in WEIGHT_ORDER]
    return tuple(out)
```

```python
import functools
import math

import jax
import jax.numpy as jnp
from jax import lax
from jax.experimental import pallas as pl
from jax.experimental.pallas import tpu as pltpu

F32 = jnp.float32
BF16 = jnp.bfloat16
SDS = jax.ShapeDtypeStruct

N_DEV = 8
S = 2048
D = 1024
HEAD_DIM = 64
ROT_DIM = 16
ROPE_THETA = 500000.0
ATT_W = 1024
SSM_W = 512
SSM_GROUPS = 32
SSM_GROUP = 16
SSM_STATE = 64
N_CPLX = SSM_GROUPS * SSM_STATE
POOL_W = 2048
POOL_GROUP = 512
EVEN_IN = 5120
EVEN_OUT = 1536
ODD_IN = 4096
RMS_EPS = 1e-6
LANES = 128
VMEM_LIMIT = 48 * 1024 * 1024

ADAM_LR = 0.001
ADAM_B1 = 0.9
ADAM_B2 = 0.999
ADAM_EPS = 1e-08
ADAM_WD = 0.01
ADAM_STEP = 10

MESH_ID = pl.DeviceIdType.MESH
NN = (((1,), (0,)), ((), ()))
NT = (((1,), (1,)), ((), ()))
TN = (((0,), (0,)), ((), ()))
_DN = {"nn": NN, "nt": NT, "tn": TN}


def _cparams(sem):
    return pltpu.CompilerParams(dimension_semantics=sem, vmem_limit_bytes=VMEM_LIMIT)


MM_TILES = (1024, 768, 512)


def _tile(dim):
    return next((t for t in MM_TILES if dim % t == 0), dim)


def _mm(a, b, mode, out_dtype):
    if mode == "nn":
        (m, k), n = a.shape, b.shape[1]
    elif mode == "nt":
        (m, k), n = a.shape, b.shape[0]
    else:
        (k, m), n = a.shape, b.shape[1]
    tm, tn, tk = _tile(m), _tile(n), _tile(k)
    nk = k // tk

    def body(a_ref, b_ref, o_ref, acc_ref):
        kk = pl.program_id(2)
        part = lax.dot_general(a_ref[...].astype(BF16), b_ref[...].astype(BF16), _DN[mode],
                               preferred_element_type=F32)
        if nk == 1:
            o_ref[...] = part.astype(o_ref.dtype)
            return

        @pl.when(kk == 0)
        def _():
            acc_ref[...] = part

        @pl.when((kk > 0) & (kk < nk - 1))
        def _():
            acc_ref[...] += part

        @pl.when(kk == nk - 1)
        def _():
            o_ref[...] = (acc_ref[...] + part).astype(o_ref.dtype)

    if mode == "nn":
        a_spec = pl.BlockSpec((tm, tk), lambda i, j, kk: (i, kk))
        b_spec = pl.BlockSpec((tk, tn), lambda i, j, kk: (kk, j))
    elif mode == "nt":
        a_spec = pl.BlockSpec((tm, tk), lambda i, j, kk: (i, kk))
        b_spec = pl.BlockSpec((tn, tk), lambda i, j, kk: (j, kk))
    else:
        a_spec = pl.BlockSpec((tk, tm), lambda i, j, kk: (kk, i))
        b_spec = pl.BlockSpec((tk, tn), lambda i, j, kk: (kk, j))
    return pl.pallas_call(
        body, name=f"mm_{mode}_{m}x{k}x{n}",
        grid=(m // tm, n // tn, nk),
        in_specs=[a_spec, b_spec],
        out_specs=pl.BlockSpec((tm, tn), lambda i, j, kk: (i, j)),
        out_shape=SDS((m, n), out_dtype),
        scratch_shapes=[pltpu.VMEM((tm, tn) if nk > 1 else (8, LANES), F32)],
        compiler_params=_cparams(("parallel", "parallel", "arbitrary")),
    )(a, b)


def _gmm(a, b, mode, out_dtype, tm=512):
    ng, gw = POOL_W // POOL_GROUP, POOL_GROUP
    ns = S // tm
    if mode in ("nn", "nt"):
        def body(a_ref, b_ref, o_ref):
            o_ref[...] = lax.dot_general(a_ref[...].astype(BF16), b_ref[...].astype(BF16), _DN[mode],
                                         preferred_element_type=F32).astype(o_ref.dtype)

        return pl.pallas_call(
            body, name=f"gmm_{mode}", grid=(ng, ns),
            in_specs=[pl.BlockSpec((tm, gw), lambda g, i: (i, g)),
                      pl.BlockSpec((None, gw, gw), lambda g, i: (g, 0, 0))],
            out_specs=pl.BlockSpec((tm, gw), lambda g, i: (i, g)),
            out_shape=SDS((S, POOL_W), out_dtype),
            compiler_params=_cparams(("parallel", "parallel")),
        )(a, b)

    def body_tn(a_ref, b_ref, o_ref, acc_ref):
        i = pl.program_id(1)

        @pl.when(i == 0)
        def _():
            acc_ref[...] = jnp.zeros_like(acc_ref)

        acc_ref[...] += lax.dot_general(a_ref[...].astype(BF16), b_ref[...].astype(BF16), TN,
                                        preferred_element_type=F32)

        @pl.when(i == ns - 1)
        def _():
            o_ref[...] = acc_ref[...].astype(o_ref.dtype)

    return pl.pallas_call(
        body_tn, name="gmm_tn", grid=(ng, ns),
        in_specs=[pl.BlockSpec((tm, gw), lambda g, i: (i, g)),
                  pl.BlockSpec((tm, gw), lambda g, i: (i, g))],
        out_specs=pl.BlockSpec((None, gw, gw), lambda g, i: (g, 0, 0)),
        out_shape=SDS((ng, gw, gw), out_dtype),
        scratch_shapes=[pltpu.VMEM((gw, gw), F32)],
        compiler_params=_cparams(("parallel", "arbitrary")),
    )(a, b)


def _rowwise(fn, inputs, out_defs, acc_defs=(), tm=256, name=None):
    n_in, n_out, n_acc = len(inputs), len(out_defs), len(acc_defs)
    in_specs, args = [], []
    for arr, width, cb in inputs:
        if arr.shape[0] == 1:
            in_specs.append(pl.BlockSpec((1, width), lambda i, cb=cb: (0, cb)))
        else:
            in_specs.append(pl.BlockSpec((tm, width), lambda i, cb=cb: (i, cb)))
        args.append(arr)
    out_shape = [SDS((S, w), dt) for w, dt in out_defs] + [SDS((1, w), F32) for w in acc_defs]
    out_specs = ([pl.BlockSpec((tm, w), lambda i: (i, 0)) for w, _ in out_defs]
                 + [pl.BlockSpec((1, w), lambda i: (0, 0)) for w in acc_defs])

    def kern(*refs):
        vals = [r[...] for r in refs[:n_in]]
        outs, accs = fn(*vals)
        for r, v in zip(refs[n_in:n_in + n_out], outs):
            r[...] = v.astype(r.dtype)
        if n_acc:
            acc_refs = refs[n_in + n_out:]

            @pl.when(pl.program_id(0) == 0)
            def _():
                for r in acc_refs:
                    r[...] = jnp.zeros_like(r)

            for r, v in zip(acc_refs, accs):
                r[...] += jnp.sum(v, axis=0, keepdims=True)

    res = pl.pallas_call(
        kern, name=name, grid=(S // tm,), in_specs=in_specs, out_specs=out_specs, out_shape=out_shape,
        compiler_params=_cparams(("arbitrary",)),
    )(*args)
    return res


def _sigmoid(x):
    return 1.0 / (1.0 + jnp.exp(-x))


def _silu_and_grad(x):
    s = _sigmoid(x)
    return x * s, s * (1.0 + x * (1.0 - s))


_GELU_K = math.sqrt(2.0 / math.pi)
_GELU_C = 0.044715


def _gelu_and_grad(x):
    t = jnp.tanh(_GELU_K * (x + _GELU_C * (x * x * x)))
    cdf = 0.5 * (1.0 + t)
    grad = cdf + 0.5 * x * (1.0 - t * t) * (_GELU_K * (1.0 + 3.0 * _GELU_C * x * x))
    return x * cdf, grad


def _rms(xv, gain):
    r = lax.rsqrt(jnp.mean(xv * xv, axis=-1, keepdims=True) + RMS_EPS)
    return xv * r * gain


def _rms_bwd(dout, xv, gain):
    r = lax.rsqrt(jnp.mean(xv * xv, axis=-1, keepdims=True) + RMS_EPS)
    xhat = xv * r
    dxhat = dout * gain
    dx = r * (dxhat - xhat * jnp.mean(dxhat * xhat, axis=-1, keepdims=True))
    return dx, dout * xhat


def _norm_fwd(x, gain):
    (h,) = _rowwise(lambda xv, g: ((_rms(xv, g),), ()), [(x, D, 0), (gain, D, 0)], [(D, BF16)], name="norm_fwd")
    return h


def _post_fwd(x, y, gain):
    (o,) = _rowwise(lambda xv, yv, g: ((xv + _rms(yv, g),), ()), [(x, D, 0), (y, D, 0), (gain, D, 0)],
                    [(D, F32)], name="post_fwd")
    return o


def _post_bwd(g, y, gain):
    def fn(gv, yv, gn):
        dx, dg = _rms_bwd(gv, yv, gn)
        return (dx,), (dg,)

    return _rowwise(fn, [(g, D, 0), (y, D, 0), (gain, D, 0)], [(D, BF16)], [D], name="post_bwd")


def _pre_bwd(g, dh, x, gain):
    def fn(gv, dhv, xv, gn):
        dx, dg = _rms_bwd(dhv, xv, gn)
        return (gv + dx,), (dg,)

    return _rowwise(fn, [(g, D, 0), (dh, D, 0), (x, D, 0), (gain, D, 0)], [(D, F32)], [D], name="pre_bwd")


def _loss_grad(xo, tgt):
    def fn(xv, tv):
        e = xv - tv
        return (e * (1.0 / D),), (e * e,)

    return _rowwise(fn, [(xo, D, 0), (tgt, D, 0)], [(D, F32)], [D], name="loss_grad")


def _pool(u_arr, col_block, transpose, out_dtype, tc=256):
    n_t = POOL_W // tc
    per_group = POOL_GROUP // tc

    def body(u_ref, o_ref):
        c = pl.program_id(0)
        grp = c // per_group
        xv = u_ref[...]
        t = lax.broadcasted_iota(jnp.int32, (S, 1), 0)
        win = jnp.left_shift(2, grp)
        cnt = jnp.minimum(t + 1, win).astype(F32)
        cur = xv / cnt if transpose else xv
        sums = []
        for k in (1, 2, 4, 8):
            if transpose:
                sh = jnp.where(t < S - k, pltpu.roll(cur, S - k, 0), 0.0)
            else:
                sh = jnp.where(t >= k, pltpu.roll(cur, k, 0), 0.0)
            cur = cur + sh
            sums.append(cur)
        tot = jnp.where(grp == 0, sums[0], jnp.where(grp == 1, sums[1], jnp.where(grp == 2, sums[2], sums[3])))
        res = tot - xv if transpose else tot / cnt - xv
        o_ref[...] = res.astype(o_ref.dtype)

    return pl.pallas_call(
        body, name="pool_bwd" if transpose else "pool_fwd", grid=(n_t,),
        in_specs=[pl.BlockSpec((S, tc), lambda c: (0, col_block * n_t + c))],
        out_specs=pl.BlockSpec((S, tc), lambda c: (0, c)),
        out_shape=SDS((S, POOL_W), out_dtype),
        compiler_params=_cparams(("parallel",)),
    )(u_arr)


def _rope_tables():
    pos = jnp.arange(S, dtype=jnp.int32).astype(F32)
    inv_freq = ROPE_THETA ** (-jnp.arange(0, ROT_DIM, 2, dtype=F32) / ROT_DIM)
    ang = pos[:, None] * inv_freq[None, :]
    cos8, sin8 = jnp.cos(ang), jnp.sin(ang)
    half = ROT_DIM // 2
    zeros = jnp.zeros((S, HEAD_DIM - ROT_DIM), F32)
    cos = jnp.concatenate([cos8, cos8, jnp.ones((S, HEAD_DIM - ROT_DIM), F32)], axis=1)
    lo = jnp.concatenate([-sin8, jnp.zeros((S, half), F32), zeros], axis=1)
    hi = jnp.concatenate([jnp.zeros((S, half), F32), sin8, zeros], axis=1)
    rep = LANES // HEAD_DIM
    return jnp.tile(cos, (1, rep)), jnp.tile(lo, (1, rep)), jnp.tile(hi, (1, rep))


def _rope(x_arr, col_tile0, tables, scale, transpose, out_dtype):
    cos, lo, hi = tables
    sign = -1.0 if transpose else 1.0
    half = ROT_DIM // 2

    def body(x_ref, c_ref, lo_ref, hi_ref, o_ref):
        xv = x_ref[...].astype(F32)
        up = pltpu.roll(xv, LANES - half, 1)
        dn = pltpu.roll(xv, half, 1)
        res = xv * c_ref[...] + sign * (up * lo_ref[...] + dn * hi_ref[...])
        o_ref[...] = (res * scale).astype(o_ref.dtype)

    n_t = ATT_W // LANES
    tab = pl.BlockSpec((S, LANES), lambda c: (0, 0))
    return pl.pallas_call(
        body, name="rope_bwd" if transpose else "rope_fwd", grid=(n_t,),
        in_specs=[pl.BlockSpec((S, LANES), lambda c: (0, col_tile0 + c)), tab, tab, tab],
        out_specs=pl.BlockSpec((S, LANES), lambda c: (0, c)),
        out_shape=SDS((S, ATT_W), out_dtype),
        compiler_params=_cparams(("parallel",)),
    )(x_arr, cos, lo, hi)


ATT_T = 256


def _multiplicity(delta):
    ok = delta >= 0
    near = jnp.where(ok & (delta <= 128), 1.0, 0.0)
    mid = jnp.where(ok & (delta <= 512) & ((delta & 3) == 0), 1.0, 0.0)
    far = jnp.where(ok & ((delta & 15) == 0), 1.0, 0.0)
    return near + mid + far


def _head_split(v, first):
    zero = jnp.zeros_like(v)
    return [jnp.where(first, v, zero), jnp.where(first, zero, v)]


def _flash_fwd(q, k, v):
    t = ATT_T
    n_hp = ATT_W // LANES

    def body(q_ref, k_ref, v_ref, o_ref, lse_ref):
        i = pl.program_id(1)
        first = lax.broadcasted_iota(jnp.int32, (1, LANES), 1) < HEAD_DIM
        qs = _head_split(q_ref[...], first)
        base = (lax.broadcasted_iota(jnp.int32, (t, t), 0) - lax.broadcasted_iota(jnp.int32, (t, t), 1))

        def kv_step(j, carry):
            m0, l0, m1, l1, acc = carry
            off = pl.multiple_of(j * t, t)
            kb = k_ref[pl.ds(off, t), :]
            vs = _head_split(v_ref[pl.ds(off, t), :], first)
            mult = _multiplicity(base + (i - j) * t)
            valid = mult > 0.0
            new = []
            pv = None
            for h, (m_prev, l_prev) in enumerate(((m0, l0), (m1, l1))):
                s = lax.dot_general(qs[h], kb, NT, preferred_element_type=F32)
                s = jnp.where(valid, s, -1e30)
                m_new = jnp.maximum(m_prev, jnp.max(s, axis=1, keepdims=True))
                p = jnp.exp(s - m_new) * mult
                alpha = jnp.exp(m_prev - m_new)
                l_new = alpha * l_prev + jnp.sum(p, axis=1, keepdims=True)
                d = lax.dot_general(p.astype(BF16), vs[h], NN, preferred_element_type=F32)
                pv = d if pv is None else pv + d
                new.append((m_new, l_new, alpha))
            acc = acc * jnp.where(first, new[0][2], new[1][2]) + pv
            return new[0][0], new[0][1], new[1][0], new[1][1], acc

        neg = jnp.full((t, 1), -1e30, F32)
        zero = jnp.zeros((t, 1), F32)
        m0, l0, m1, l1, acc = lax.fori_loop(0, i + 1, kv_step, (neg, zero, neg, zero, jnp.zeros((t, LANES), F32)))
        o_ref[...] = acc * jnp.where(first, 1.0 / l0, 1.0 / l1)
        lse_ref[...] = jnp.where(first, m0 + jnp.log(l0), m1 + jnp.log(l1))

    blk = pl.BlockSpec((t, LANES), lambda hp, i: (i, hp))
    full = pl.BlockSpec((S, LANES), lambda hp, i: (0, hp))
    return pl.pallas_call(
        body, name="flash_fwd", grid=(n_hp, S // t),
        in_specs=[blk, full, full], out_specs=[blk, blk],
        out_shape=[SDS((S, ATT_W), F32), SDS((S, ATT_W), F32)],
        compiler_params=_cparams(("parallel", "arbitrary")),
    )(q, k, v)


def _flash_bwd(q, k, v, o, do, lse):
    t = ATT_T
    n_hp = ATT_W // LANES
    n_t = S // t

    def body(q_ref, k_ref, v_ref, o_ref, do_ref, lse_ref, dq_ref, dk_ref, dv_ref):
        j = pl.program_id(1)
        first = lax.broadcasted_iota(jnp.int32, (1, LANES), 1) < HEAD_DIM

        @pl.when(j == 0)
        def _():
            dq_ref[...] = jnp.zeros_like(dq_ref)

        kb = k_ref[...]
        vb = v_ref[...]
        ks = _head_split(kb, first)
        base = (lax.broadcasted_iota(jnp.int32, (t, t), 0) - lax.broadcasted_iota(jnp.int32, (t, t), 1))

        def q_step(i, carry):
            dk_acc, dv_acc = carry
            rows = pl.ds(pl.multiple_of(i * t, t), t)
            qs = _head_split(q_ref[rows, :], first)
            dob = do_ref[rows, :]
            prod = dob * o_ref[rows, :]
            d_all = jnp.sum(prod, axis=1, keepdims=True)
            d0 = jnp.sum(jnp.where(first, prod, 0.0), axis=1, keepdims=True)
            lse_b = lse_ref[rows, :]
            lse0 = jnp.max(jnp.where(first, lse_b, -jnp.inf), axis=1, keepdims=True)
            lse1 = jnp.max(jnp.where(first, -jnp.inf, lse_b), axis=1, keepdims=True)
            dos = _head_split(dob.astype(BF16), first)
            mult = _multiplicity(base + (i - j) * t)
            valid = mult > 0.0
            dq_t = jnp.zeros((t, LANES), F32)
            for h, (lse_h, d_h) in enumerate(((lse0, d0), (lse1, d_all - d0))):
                s = lax.dot_general(qs[h], kb, NT, preferred_element_type=F32)
                p = jnp.exp(jnp.where(valid, s, -1e30) - lse_h) * mult
                dp = lax.dot_general(dos[h], vb, NT, preferred_element_type=F32)
                ds = (p * (dp - d_h)).astype(BF16)
                dv_acc = dv_acc + lax.dot_general(p.astype(BF16), dos[h], TN, preferred_element_type=F32)
                dk_acc = dk_acc + lax.dot_general(ds, qs[h], TN, preferred_element_type=F32)
                dq_t = dq_t + lax.dot_general(ds, ks[h], NN, preferred_element_type=F32)
            dq_ref[rows, :] += dq_t
            return dk_acc, dv_acc

        zero = jnp.zeros((t, LANES), F32)
        dk_acc, dv_acc = lax.fori_loop(j, n_t, q_step, (zero, zero))
        dk_ref[...] = dk_acc
        dv_ref[...] = dv_acc

    blk = pl.BlockSpec((t, LANES), lambda hp, j: (j, hp))
    full = pl.BlockSpec((S, LANES), lambda hp, j: (0, hp))
    return pl.pallas_call(
        body, name="flash_bwd", grid=(n_hp, n_t),
        in_specs=[full, blk, blk, full, full, full], out_specs=[full, blk, blk],
        out_shape=[SDS((S, ATT_W), F32)] * 3,
        compiler_params=_cparams(("parallel", "arbitrary")),
    )(q, k, v, o, do, lse)


SCAN_T = 256
ST_ROWS = 2 * N_CPLX // LANES
HALF = ST_ROWS // 2


def _scan_fwd(lam, bu):
    def body(lam_ref, bu_ref, st_ref, carry):
        @pl.when(pl.program_id(0) == 0)
        def _():
            carry[...] = jnp.zeros_like(carry)

        ar, ai = lam_ref[0:HALF, :], lam_ref[HALF:ST_ROWS, :]

        def step(t, c):
            sr, si = c
            b = bu_ref[t]
            nr = ar * sr - ai * si + b[0:HALF]
            ni = ar * si + ai * sr + b[HALF:ST_ROWS]
            st_ref[t, 0:HALF, :] = nr
            st_ref[t, HALF:ST_ROWS, :] = ni
            return nr, ni

        sr, si = lax.fori_loop(0, SCAN_T, step, (carry[0:HALF, :], carry[HALF:ST_ROWS, :]), unroll=8)
        carry[0:HALF, :] = sr
        carry[HALF:ST_ROWS, :] = si

    blk = pl.BlockSpec((SCAN_T, ST_ROWS, LANES), lambda i: (i, 0, 0))
    return pl.pallas_call(
        body, name="scan_fwd", grid=(S // SCAN_T,),
        in_specs=[pl.BlockSpec((ST_ROWS, LANES), lambda i: (0, 0)), blk], out_specs=blk,
        out_shape=SDS((S, ST_ROWS, LANES), F32),
        scratch_shapes=[pltpu.VMEM((ST_ROWS, LANES), F32)],
        compiler_params=_cparams(("arbitrary",)),
    )(lam, bu)


def _scan_bwd(lam, dst, states):
    n_blk = S // SCAN_T

    def body(lam_ref, d_ref, st_ref, g_ref, dlam_ref, carry):
        @pl.when(pl.program_id(0) == 0)
        def _():
            carry[...] = jnp.zeros_like(carry)
            dlam_ref[...] = jnp.zeros_like(dlam_ref)

        ar, ai = lam_ref[0:HALF, :], lam_ref[HALF:ST_ROWS, :]

        def step(kk, c):
            t = SCAN_T - 1 - kk
            gr, gi, dar, dai = c
            x = st_ref[t]
            xr, xi = x[0:HALF], x[HALF:ST_ROWS]
            dar = dar + gr * xr + gi * xi
            dai = dai + gi * xr - gr * xi
            d = d_ref[t]
            ngr = d[0:HALF] + ar * gr + ai * gi
            ngi = d[HALF:ST_ROWS] + ar * gi - ai * gr
            g_ref[t, 0:HALF, :] = ngr
            g_ref[t, HALF:ST_ROWS, :] = ngi
            return ngr, ngi, dar, dai

        zero = jnp.zeros((HALF, LANES), F32)
        gr, gi, dar, dai = lax.fori_loop(0, SCAN_T, step, (carry[0:HALF, :], carry[HALF:ST_ROWS, :], zero, zero),
                                         unroll=8)
        carry[0:HALF, :] = gr
        carry[HALF:ST_ROWS, :] = gi
        dlam_ref[0:HALF, :] += dar
        dlam_ref[HALF:ST_ROWS, :] += dai

    blk = pl.BlockSpec((SCAN_T, ST_ROWS, LANES), lambda i: (n_blk - 1 - i, 0, 0))
    small = pl.BlockSpec((ST_ROWS, LANES), lambda i: (0, 0))
    return pl.pallas_call(
        body, name="scan_bwd", grid=(n_blk,),
        in_specs=[small, blk, blk], out_specs=[blk, small],
        out_shape=[SDS((S, ST_ROWS, LANES), F32), SDS((ST_ROWS, LANES), F32)],
        scratch_shapes=[pltpu.VMEM((ST_ROWS, LANES), F32)],
        compiler_params=_cparams(("arbitrary",)),
    )(lam, dst, states)


def _ssm_prep(a_re, a_im, log_dt, b_re, b_im, c_re, c_im):
    lam = lax.complex(a_re, a_im)
    dt = jnp.exp(log_dt)[:, None]
    lam_bar = jnp.exp(lam * dt)
    b_bar = ((lam_bar - 1.0) / lam)[..., None] * lax.complex(b_re, b_im)
    lam_t = jnp.concatenate([jnp.real(lam_bar).reshape(HALF, LANES), jnp.imag(lam_bar).reshape(HALF, LANES)], axis=0)
    eye = jnp.eye(SSM_GROUPS, dtype=F32)

    def in_map(m):
        return jnp.einsum("gph,gk->ghkp", m, eye).reshape(SSM_W, N_CPLX)

    def out_map(m):
        return jnp.einsum("ghp,gk->gpkh", m, eye).reshape(N_CPLX, SSM_W)

    w_b = jnp.concatenate([in_map(jnp.real(b_bar)), in_map(jnp.imag(b_bar))], axis=1)
    w_c = jnp.concatenate([out_map(c_re), -out_map(c_im)], axis=0)
    return lam_t, w_b, w_c


def _row(v):
    return v.reshape(1, -1)


def _even_fwd(x, pre, post, w_in, w_out, glu_w, glu_b, ssm_d, prep, tables):
    lam_t, w_b, w_c = prep
    h = _norm_fwd(x, pre)
    proj = _mm(h, w_in, "nn", F32)
    q = _rope(proj, 0, tables, HEAD_DIM ** -0.5, False, BF16)
    k = _rope(proj, ATT_W // LANES, tables, 1.0, False, BF16)
    v = proj[:, 2 * ATT_W:3 * ATT_W].astype(BF16)
    att, lse = _flash_fwd(q, k, v)
    u_ssm = proj[:, 4 * ATT_W:4 * ATT_W + SSM_W]
    bu = _mm(u_ssm, w_b, "nn", F32)
    states = _scan_fwd(lam_t, bu.reshape(S, ST_ROWS, LANES))
    y = _mm(states.reshape(S, 2 * N_CPLX), w_c, "nn", F32)

    def act1(yv, uv, dv):
        return (_gelu_and_grad(yv + dv * uv)[0],), ()

    (z1,) = _rowwise(act1, [(y, SSM_W, 0), (proj, SSM_W, 8), (ssm_d, SSM_W, 0)], [(SSM_W, F32)], name="ssm_act_fwd")
    lin = _mm(z1, glu_w, "nn", F32)

    def gate(att_v, ga, gs, z1v, linv, bv):
        ssm_out = z1v * _sigmoid(linv + bv)
        return (jnp.concatenate([att_v * _silu_and_grad(ga)[0], ssm_out * _silu_and_grad(gs)[0]], axis=1),), ()

    (merged,) = _rowwise(gate, [(att, ATT_W, 0), (proj, ATT_W, 3), (proj, SSM_W, 9), (z1, SSM_W, 0),
                                (lin, SSM_W, 0), (glu_b, SSM_W, 0)], [(EVEN_OUT, BF16)], name="even_gate_fwd")
    yout = _mm(merged, w_out, "nn", F32)
    x_next = _post_fwd(x, yout, post)
    saved = (x, h, proj, q, k, v, att, lse, states, y, z1, lin, merged, yout)
    return x_next, saved


def _even_bwd(g, saved, pre, post, w_in, w_out, glu_w, glu_b, ssm_d, prep, tables):
    x, h, proj, q, k, v, att, lse, states, y, z1, lin, merged, yout = saved
    lam_t, w_b, w_c = prep
    dyout, dpost = _post_bwd(g, yout, post)
    dmerged = _mm(dyout, w_out, "nt", F32)
    dw_out = _mm(merged, dyout, "tn", BF16)

    def gate_bwd(dm_a, dm_s, att_v, ga, gs, z1v, linv, bv):
        sa, dsa = _silu_and_grad(ga)
        ss, dss = _silu_and_grad(gs)
        sig = _sigmoid(linv + bv)
        ssm_out = z1v * sig
        dssm = dm_s * ss
        dlin = dssm * z1v * sig * (1.0 - sig)
        return (dm_a * sa, dm_a * att_v * dsa, dm_s * ssm_out * dss, dssm * sig, dlin), (dlin,)

    datt, dg_att, dg_ssm, dz1a, dlin, dglu_b = _rowwise(
        gate_bwd, [(dmerged, ATT_W, 0), (dmerged, SSM_W, 2), (att, ATT_W, 0), (proj, ATT_W, 3), (proj, SSM_W, 9),
                   (z1, SSM_W, 0), (lin, SSM_W, 0), (glu_b, SSM_W, 0)],
        [(ATT_W, F32), (ATT_W, BF16), (SSM_W, BF16), (SSM_W, F32), (SSM_W, BF16)], [SSM_W], name="even_gate_bwd")
    dz1b = _mm(dlin, glu_w, "nt", F32)
    dglu_w = _mm(z1, dlin, "tn", BF16)

    def act1_bwd(da, db, yv, uv, dv):
        dpre = (da + db) * _gelu_and_grad(yv + dv * uv)[1]
        return (dpre, dpre * dv), (dpre * uv,)

    dy, du_direct, dd = _rowwise(act1_bwd, [(dz1a, SSM_W, 0), (dz1b, SSM_W, 0), (y, SSM_W, 0), (proj, SSM_W, 8),
                                            (ssm_d, SSM_W, 0)], [(SSM_W, BF16), (SSM_W, F32)], [SSM_W],
                                 name="ssm_act_bwd")
    dst = _mm(dy, w_c, "nt", F32)
    dw_c = _mm(states.reshape(S, 2 * N_CPLX), dy, "tn", F32)
    gst, dlam = _scan_bwd(lam_t, dst.reshape(S, ST_ROWS, LANES), states)
    dbu = gst.reshape(S, 2 * N_CPLX)
    u_ssm = proj[:, 4 * ATT_W:4 * ATT_W + SSM_W]
    du_ssm = _mm(dbu, w_b, "nt", F32) + du_direct
    dw_b = _mm(u_ssm, dbu, "tn", F32)
    dq, dk, dv = _flash_bwd(q, k, v, att, datt, lse)
    dq = _rope(dq, 0, tables, HEAD_DIM ** -0.5, True, BF16)
    dk = _rope(dk, 0, tables, 1.0, True, BF16)
    dproj = jnp.concatenate([dq, dk, dv.astype(BF16), dg_att, du_ssm.astype(BF16), dg_ssm], axis=1)
    dw_in = _mm(h, dproj, "tn", BF16)
    dh = _mm(dproj, w_in, "nt", F32)
    g_prev, dpre = _pre_bwd(g, dh, x, pre)
    return g_prev, dict(pre=dpre, post=dpost, w_in=dw_in, w_out=dw_out, glu_w=dglu_w, glu_b=dglu_b, ssm_d=dd,
                        prep=(dlam, dw_b, dw_c))


def _odd_fwd(x, pre, post, w_in, pool_w, pool_scale, w_out):
    h = _norm_fwd(x, pre)
    proj = _mm(h, w_in, "nn", F32)
    mixed = _pool(proj, 0, False, BF16)
    ylin = _gmm(mixed, pool_w, "nn", F32)

    def gate(yl, gt, sc):
        return (yl * sc * _silu_and_grad(gt)[0],), ()

    (z,) = _rowwise(gate, [(ylin, POOL_W, 0), (proj, POOL_W, 1), (pool_scale, POOL_W, 0)], [(POOL_W, BF16)],
                    name="odd_gate_fwd")
    yout = _mm(z, w_out, "nn", F32)
    x_next = _post_fwd(x, yout, post)
    return x_next, (x, h, proj, mixed, ylin, z, yout)


def _odd_bwd(g, saved, pre, post, w_in, pool_w, pool_scale, w_out):
    x, h, proj, mixed, ylin, z, yout = saved
    dyout, dpost = _post_bwd(g, yout, post)
    dz = _mm(dyout, w_out, "nt", F32)
    dw_out = _mm(z, dyout, "tn", BF16)

    def gate_bwd(dzv, yl, gt, sc):
        sg, dsg = _silu_and_grad(gt)
        tt = dzv * sg
        return (tt * sc, dzv * yl * sc * dsg), (tt * yl,)

    dylin, dgate, dscale = _rowwise(gate_bwd, [(dz, POOL_W, 0), (ylin, POOL_W, 0), (proj, POOL_W, 1),
                                               (pool_scale, POOL_W, 0)], [(POOL_W, BF16), (POOL_W, BF16)], [POOL_W],
                                    name="odd_gate_bwd")
    dmixed = _gmm(dylin, pool_w, "nt", F32)
    dpool_w = _gmm(mixed, dylin, "tn", BF16)
    du = _pool(dmixed, 0, True, BF16)
    dproj = jnp.concatenate([du, dgate], axis=1)
    dw_in = _mm(h, dproj, "tn", BF16)
    dh = _mm(dproj, w_in, "nt", F32)
    g_prev, dpre = _pre_bwd(g, dh, x, pre)
    return g_prev, dict(pre=dpre, post=dpost, w_in=dw_in, w_out=dw_out, pool_w=dpool_w, pool_scale=dscale)


def _my_index():
    return 4 * lax.axis_index("x") + 2 * lax.axis_index("y") + lax.axis_index("c")


def _exchange(arrs, gather, name):
    n = len(arrs)
    out_shape = [SDS((N_DEV,) + a.shape, a.dtype) if gather else SDS(a.shape, a.dtype) for a in arrs]

    def body(*refs):
        ins, outs = refs[:n], refs[n:2 * n]
        send_sems, recv_sems, local_sems = refs[2 * n:]
        me = _my_index()

        def src(i, j):
            return ins[i] if gather else ins[i].at[j]

        def remote(i, j, src_slot, dst_slot, recv_slot):
            return pltpu.make_async_remote_copy(
                src_ref=src(i, src_slot), dst_ref=outs[i].at[dst_slot], send_sem=send_sems.at[i, j],
                recv_sem=recv_sems.at[i, recv_slot], device_id=(j // 4, (j // 2) % 2, j % 2), device_id_type=MESH_ID)

        def local(i):
            return pltpu.make_async_copy(src(i, me), outs[i].at[me], local_sems.at[i])

        for i in range(n):
            local(i).start()
        for j in range(N_DEV):
            @pl.when(me != j)
            def _(j=j):
                for i in range(n):
                    remote(i, j, j, me, me).start()
        for j in range(N_DEV):
            @pl.when(me != j)
            def _(j=j):
                for i in range(n):
                    remote(i, j, j, me, me).wait_send()
                    remote(i, j, j, j, j).wait_recv()
        for i in range(n):
            local(i).wait()

    any_spec = pl.BlockSpec(memory_space=pl.ANY)
    return pl.pallas_call(
        body, name=name, in_specs=[any_spec] * n, out_specs=[any_spec] * n, out_shape=out_shape,
        scratch_shapes=[pltpu.SemaphoreType.DMA((n, N_DEV)), pltpu.SemaphoreType.DMA((n, N_DEV)),
                        pltpu.SemaphoreType.DMA((n,))],
    )(*arrs)


HBM_SPEC = pl.BlockSpec(memory_space=pltpu.HBM)
SEM_SPEC = pl.BlockSpec(memory_space=pltpu.SEMAPHORE)
SPLIT_EFFECT = pltpu.SideEffectType.DATAFLOW_SIDE_EFFECTING


def _device_of(j):
    return (j // 4, (j // 2) % 2, j % 2)


def _split_copy(srcs, lands, send_sems, recv_sems, gather, i, j, dst_slot, recv_slot):
    return pltpu.make_async_remote_copy(
        src_ref=srcs[i] if gather else srcs[i].at[j], dst_ref=lands[i].at[dst_slot],
        send_sem=send_sems.at[i * N_DEV + j], recv_sem=recv_sems.at[i * N_DEV + recv_slot],
        device_id=_device_of(j), device_id_type=MESH_ID)


def _xchg_start(name, srcs, lands, gather):
    n = len(srcs)

    def body(*refs):
        src_refs, land_refs = refs[:n], refs[n:2 * n]
        send_sems, recv_sems, token = refs[2 * n], refs[2 * n + 1], refs[-1]
        me = _my_index()
        for j in range(N_DEV):
            @pl.when(me != j)
            def _(j=j):
                for i in range(n):
                    _split_copy(src_refs, land_refs, send_sems, recv_sems, gather, i, j, me, me).start()
        token[...] = jnp.zeros_like(token)

    thru = [pltpu.HBM(a.shape, a.dtype) for a in list(srcs) + list(lands)]
    res = pl.pallas_call(
        body, name=name,
        out_shape=(pltpu.SemaphoreType.DMA((n * N_DEV,)), pltpu.SemaphoreType.DMA((n * N_DEV,)), *thru,
                   SDS((8, LANES), F32)),
        in_specs=[HBM_SPEC] * (2 * n),
        out_specs=(SEM_SPEC, SEM_SPEC, *([HBM_SPEC] * (2 * n)), pl.BlockSpec(memory_space=pltpu.VMEM)),
        input_output_aliases={i: 2 + i for i in range(2 * n)},
        compiler_params=pltpu.CompilerParams(has_side_effects=SPLIT_EFFECT),
    )(*[pltpu.with_memory_space_constraint(a, pltpu.HBM) for a in list(srcs) + list(lands)])
    return res[0], res[1], list(res[2:2 + n]), list(res[2 + n:2 + 2 * n]), res[-1]


def _xchg_wait(name, started, gather, after):
    send_sems, recv_sems, srcs, lands, _ = started
    n = len(srcs)

    def body(*refs):
        src_refs, land_refs = refs[:n], refs[n:2 * n]
        send_r, recv_r = refs[2 * n], refs[2 * n + 1]
        me = _my_index()
        for j in range(N_DEV):
            @pl.when(me != j)
            def _(j=j):
                for i in range(n):
                    _split_copy(src_refs, land_refs, send_r, recv_r, gather, i, j, me, me).wait_send()
                    _split_copy(src_refs, land_refs, send_r, recv_r, gather, i, j, j, j).wait_recv()

    thru = [pltpu.HBM(a.shape, a.dtype) for a in list(srcs) + list(lands)]
    res = pl.pallas_call(
        body, name=name, out_shape=tuple(thru),
        in_specs=[HBM_SPEC] * (2 * n) + [SEM_SPEC, SEM_SPEC, pl.BlockSpec(memory_space=pl.ANY)],
        out_specs=tuple([HBM_SPEC] * (2 * n)),
        input_output_aliases={i: i for i in range(2 * n)},
        compiler_params=pltpu.CompilerParams(has_side_effects=SPLIT_EFFECT),
    )(*srcs, *lands, send_sems, recv_sems, after)
    return list(res[n:])


def _own_slot_landing(own):
    zone = lax.empty((N_DEV,) + own.shape, own.dtype)
    return lax.dynamic_update_slice(zone, own[None], (_my_index(),) + (0,) * own.ndim)


def _adam_layers(w, slot_list, m, v, name):
    n_l, r, c = w.shape
    ns = slot_list[0].shape[0]
    tr = r
    while tr * c * 4 > (1 << 20) and tr % 16 == 0:
        tr //= 2
    assert r % tr == 0 and len(slot_list) == n_l

    def body(*refs):
        w_ref, slot_refs = refs[0], refs[1:1 + n_l]
        m_ref, v_ref, go_ref, d_ref, mo_ref, vo_ref = refs[1 + n_l:]
        layer = pl.program_id(0)
        g = None
        for l, g_ref in enumerate(slot_refs):
            gl = g_ref[0].astype(F32)
            for s in range(1, ns):
                gl = gl + g_ref[s].astype(F32)
            g = gl if g is None else jnp.where(layer == l, gl, g)
        mn = ADAM_B1 * m_ref[...] + (1.0 - ADAM_B1) * g
        vn = ADAM_B2 * v_ref[...] + (1.0 - ADAM_B2) * (g * g)
        m_hat = mn / (1.0 - ADAM_B1 ** ADAM_STEP)
        v_hat = vn / (1.0 - ADAM_B2 ** ADAM_STEP)
        go_ref[...] = g
        d_ref[...] = -ADAM_LR * (m_hat / (jnp.sqrt(v_hat) + ADAM_EPS) + ADAM_WD * w_ref[...])
        mo_ref[...] = mn
        vo_ref[...] = vn

    blk = pl.BlockSpec((None, tr, c), lambda l, i: (l, i, 0))
    slot_specs = [pl.BlockSpec((ns, tr, c), lambda l, i, k=k: (0, jnp.where(l == k, i, 0), 0)) for k in range(n_l)]
    return pl.pallas_call(
        body, name=name, grid=(n_l, r // tr),
        in_specs=[blk] + slot_specs + [blk, blk],
        out_specs=[blk] * 4, out_shape=[SDS((n_l, r, c), F32)] * 4,
        compiler_params=_cparams(("arbitrary", "arbitrary")),
    )(w, *slot_list, m, v)


def _adam(w, gslots, m, v, name):
    r, c = w.shape
    ns = gslots.shape[0]
    tr = r
    while tr * c * 4 > (1 << 20) and tr % 16 == 0:
        tr //= 2
    assert r % tr == 0

    def body(w_ref, g_ref, m_ref, v_ref, go_ref, d_ref, mo_ref, vo_ref):
        g = g_ref[0].astype(F32)
        for s in range(1, ns):
            g = g + g_ref[s].astype(F32)
        wv = w_ref[...]
        mn = ADAM_B1 * m_ref[...] + (1.0 - ADAM_B1) * g
        vn = ADAM_B2 * v_ref[...] + (1.0 - ADAM_B2) * (g * g)
        m_hat = mn / (1.0 - ADAM_B1 ** ADAM_STEP)
        v_hat = vn / (1.0 - ADAM_B2 ** ADAM_STEP)
        go_ref[...] = g
        d_ref[...] = -ADAM_LR * (m_hat / (jnp.sqrt(v_hat) + ADAM_EPS) + ADAM_WD * wv)
        mo_ref[...] = mn
        vo_ref[...] = vn

    blk = pl.BlockSpec((tr, c), lambda i: (i, 0))
    return pl.pallas_call(
        body, name=name, grid=(r // tr,),
        in_specs=[blk, pl.BlockSpec((ns, tr, c), lambda i: (0, i, 0)), blk, blk],
        out_specs=[blk] * 4, out_shape=[SDS((r, c), F32)] * 4,
        compiler_params=_cparams(("parallel",)),
    )(w, gslots, m, v)


SMALL_NAMES = ("pre_norm", "post_norm", "ssm_a_re", "ssm_a_im", "ssm_log_dt", "ssm_b_re", "ssm_b_im", "ssm_c_re",
               "ssm_c_im", "ssm_d", "ssm_glu_b")
SHARDED_NAMES = ("even_w_in", "even_w_out", "ssm_glu_w", "odd_w_in", "pool_w", "odd_w_out")
WEIGHT_ORDER = ("pre_norm", "post_norm", "even_w_in", "even_w_out", "ssm_a_re", "ssm_a_im", "ssm_log_dt", "ssm_b_re",
                "ssm_b_im", "ssm_c_re", "ssm_c_im", "ssm_d", "ssm_glu_w", "ssm_glu_b", "odd_w_in", "pool_w",
                "pool_scale", "odd_w_out")
PACK_ROWS_ALIGN = 8


def _pack(parts):
    flat = jnp.concatenate([p.reshape(-1).astype(F32) for p in parts])
    rows = -(-flat.shape[0] // (LANES * PACK_ROWS_ALIGN)) * PACK_ROWS_ALIGN
    return jnp.pad(flat, (0, rows * LANES - flat.shape[0])).reshape(rows, LANES)


def _unpack(packed, shapes):
    flat = packed.reshape(-1)
    out, off = [], 0
    for shp in shapes:
        size = math.prod(shp)
        out.append(flat[off:off + size].reshape(shp))
        off += size
    return out


EVEN_SHARDED = ("w_in", "w_out", "glu_w")
ODD_SHARDED = ("w_in", "pool_w", "w_out")
FAMILY = {(0, "w_in"): "even_w_in", (0, "w_out"): "even_w_out", (0, "glu_w"): "ssm_glu_w",
          (1, "w_in"): "odd_w_in", (1, "pool_w"): "pool_w", (1, "w_out"): "odd_w_out"}


def _sharded_keys(layer):
    return EVEN_SHARDED if layer % 2 == 0 else ODD_SHARDED


def _local_step(x, tgt, small, get_weights, on_grads):
    tables = _rope_tables()
    preps, prep_vjps = [], []
    for i in range(2):
        out, vjp = jax.vjp(_ssm_prep, small["ssm_a_re"][i], small["ssm_a_im"][i], small["ssm_log_dt"][i],
                           small["ssm_b_re"][i], small["ssm_b_im"][i], small["ssm_c_re"][i], small["ssm_c_im"][i])
        preps.append(out)
        prep_vjps.append(vjp)

    def layer_args(layer, wts):
        i = layer // 2
        pre, post = _row(small["pre_norm"][layer]), _row(small["post_norm"][layer])
        if layer % 2 == 0:
            return (pre, post, wts["w_in"], wts["w_out"], wts["glu_w"], _row(small["ssm_glu_b"][i]),
                    _row(small["ssm_d"][i]), preps[i], tables)
        return (pre, post, wts["w_in"], wts["pool_w"], _row(wts["pool_scale"]), wts["w_out"])

    saved, args = [], []
    cur = x
    for layer in range(4):
        args.append(layer_args(layer, get_weights(layer, cur)))
        cur, sv = (_even_fwd if layer % 2 == 0 else _odd_fwd)(cur, *args[layer])
        saved.append(sv)
    g, sq = _loss_grad(cur, tgt)
    loss = 0.5 * jnp.sum(sq) / D

    lg = [None] * 4
    for layer in reversed(range(4)):
        g, lg[layer] = (_even_bwd if layer % 2 == 0 else _odd_bwd)(g, saved[layer], *args[layer])
        on_grads(layer, {k: lg[layer][k] for k in _sharded_keys(layer)})

    ssm_g = [prep_vjps[i](lg[2 * i]["prep"]) for i in range(2)]
    grads = {
        "pre_norm": jnp.concatenate([lg[l]["pre"] for l in range(4)], axis=0),
        "post_norm": jnp.concatenate([lg[l]["post"] for l in range(4)], axis=0),
        "ssm_d": jnp.concatenate([lg[0]["ssm_d"], lg[2]["ssm_d"]], axis=0),
        "ssm_glu_b": jnp.concatenate([lg[0]["glu_b"], lg[2]["glu_b"]], axis=0),
        "pool_scale": jnp.concatenate([lg[1]["pool_scale"], lg[3]["pool_scale"]], axis=0),
    }
    for idx, nm in enumerate(("ssm_a_re", "ssm_a_im", "ssm_log_dt", "ssm_b_re", "ssm_b_im", "ssm_c_re", "ssm_c_im")):
        grads[nm] = jnp.stack([ssm_g[0][idx], ssm_g[1][idx]])
    return loss, g, grads


def _to_slots(key, gfull):
    if key == "w_in":
        kk, nn = gfull.shape
        return gfull.reshape(kk, N_DEV, nn // N_DEV).transpose(1, 0, 2)
    if key in ("w_out", "glu_w"):
        rr, nn = gfull.shape
        return gfull.reshape(N_DEV, rr // N_DEV, nn)
    assert key == "pool_w"
    gg, rr, nn = gfull.shape
    return gfull.reshape(gg, N_DEV, rr // N_DEV, nn).transpose(1, 0, 2, 3)


def _from_gathered(key, gat):
    if key == "w_in":
        _, kk, nn = gat.shape
        return gat.transpose(1, 0, 2).reshape(kk, N_DEV * nn)
    if key in ("w_out", "glu_w"):
        _, rr, nn = gat.shape
        return gat.reshape(N_DEV * rr, nn)
    assert key == "pool_w"
    _, gg, rr, nn = gat.shape
    return gat.transpose(1, 0, 2, 3).reshape(gg, N_DEV * rr, nn)


def kernel(x, pre_norm, post_norm, even_w_in, even_w_out, ssm_a_re, ssm_a_im, ssm_log_dt, ssm_b_re, ssm_b_im, ssm_c_re, ssm_c_im, ssm_d, ssm_glu_w, ssm_glu_b, odd_w_in, pool_w, pool_scale, odd_w_out, loss_target, m_pre_norm, m_post_norm, m_even_w_in, m_even_w_out, m_ssm_a_re, m_ssm_a_im, m_ssm_log_dt, m_ssm_b_re, m_ssm_b_im, m_ssm_c_re, m_ssm_c_im, m_ssm_d, m_ssm_glu_w, m_ssm_glu_b, m_odd_w_in, m_pool_w, m_pool_scale, m_odd_w_out, v_pre_norm, v_post_norm, v_even_w_in, v_even_w_out, v_ssm_a_re, v_ssm_a_im, v_ssm_log_dt, v_ssm_b_re, v_ssm_b_im, v_ssm_c_re, v_ssm_c_im, v_ssm_d, v_ssm_glu_w, v_ssm_glu_b, v_odd_w_in, v_pool_w, v_pool_scale, v_odd_w_out):
    w = dict(pre_norm=pre_norm, post_norm=post_norm, even_w_in=even_w_in, even_w_out=even_w_out, ssm_a_re=ssm_a_re,
             ssm_a_im=ssm_a_im, ssm_log_dt=ssm_log_dt, ssm_b_re=ssm_b_re, ssm_b_im=ssm_b_im, ssm_c_re=ssm_c_re,
             ssm_c_im=ssm_c_im, ssm_d=ssm_d, ssm_glu_w=ssm_glu_w, ssm_glu_b=ssm_glu_b, odd_w_in=odd_w_in,
             pool_w=pool_w, pool_scale=pool_scale, odd_w_out=odd_w_out)
    mom = dict(pre_norm=m_pre_norm, post_norm=m_post_norm, even_w_in=m_even_w_in, even_w_out=m_even_w_out,
               ssm_a_re=m_ssm_a_re, ssm_a_im=m_ssm_a_im, ssm_log_dt=m_ssm_log_dt, ssm_b_re=m_ssm_b_re,
               ssm_b_im=m_ssm_b_im, ssm_c_re=m_ssm_c_re, ssm_c_im=m_ssm_c_im, ssm_d=m_ssm_d, ssm_glu_w=m_ssm_glu_w,
               ssm_glu_b=m_ssm_glu_b, odd_w_in=m_odd_w_in, pool_w=m_pool_w, pool_scale=m_pool_scale,
               odd_w_out=m_odd_w_out)
    var = dict(pre_norm=v_pre_norm, post_norm=v_post_norm, even_w_in=v_even_w_in, even_w_out=v_even_w_out,
               ssm_a_re=v_ssm_a_re, ssm_a_im=v_ssm_a_im, ssm_log_dt=v_ssm_log_dt, ssm_b_re=v_ssm_b_re,
               ssm_b_im=v_ssm_b_im, ssm_c_re=v_ssm_c_re, ssm_c_im=v_ssm_c_im, ssm_d=v_ssm_d, ssm_glu_w=v_ssm_glu_w,
               ssm_glu_b=v_ssm_glu_b, odd_w_in=v_odd_w_in, pool_w=v_pool_w, pool_scale=v_pool_scale,
               odd_w_out=v_odd_w_out)
    me = _my_index()
    scale_cols = pool_scale.shape[1]

    gather_started = []
    for layer in range(4):
        i = layer // 2
        shards = [w[FAMILY[(layer % 2, k)]][i].astype(BF16) for k in _sharded_keys(layer)]
        if layer % 2 == 1:
            shards.append(jnp.pad(pool_scale[i][None], ((0, PACK_ROWS_ALIGN - 1), (0, 0))))
        gather_started.append(_xchg_start(f"gather_start_{layer}", shards, [_own_slot_landing(s) for s in shards], True))
    token = sum(st[4][0, 0] for st in gather_started)
    small = {nm: w[nm] for nm in SMALL_NAMES}
    small["pre_norm"] = pre_norm + token

    def get_weights(layer, after):
        lands = _xchg_wait(f"gather_wait_{layer}", gather_started[layer], True, after)
        wts = {k: _from_gathered(k, gat) for k, gat in zip(_sharded_keys(layer), lands)}
        if layer % 2 == 1:
            wts["pool_scale"] = lands[-1][:, 0, :].reshape(N_DEV * scale_cols)
        return wts

    scatter_started = [None] * 4

    def on_grads(layer, lg):
        slots = [_to_slots(k, lg[k]) for k in _sharded_keys(layer)]
        lands = [_own_slot_landing(lax.dynamic_index_in_dim(s, me, 0, keepdims=False)) for s in slots]
        scatter_started[layer] = _xchg_start(f"scatter_start_{layer}", slots, lands, False)

    loss_local, grad_x, grads = _local_step(x[0], loss_target[0], small, get_weights, on_grads)
    loss = lax.psum(loss_local, ("x", "y", "c"))

    small_parts = [grads[nm] for nm in SMALL_NAMES] + [grads["pool_scale"]]
    small_shapes = [w[nm].shape for nm in SMALL_NAMES] + [(2, N_DEV * scale_cols)]
    (small_slots,) = _exchange([_pack(small_parts)], True, "gather_small_grads")

    recv = {}
    for layer in (3, 2, 1, 0):
        lands = _xchg_wait(f"scatter_wait_{layer}", scatter_started[layer], False, small_slots)
        for k, land in zip(_sharded_keys(layer), lands):
            recv[(layer, k)] = land
    res = {}
    for (parity, k), nm in FAMILY.items():
        shp = w[nm].shape
        cols = shp[-1]
        slot_list = [recv[(parity + 2 * i, k)].reshape(N_DEV, -1, cols) for i in range(2)]
        outs = _adam_layers(w[nm].reshape(2, -1, cols), slot_list, mom[nm].reshape(2, -1, cols),
                            var[nm].reshape(2, -1, cols), name=f"adam_{nm}")
        res[nm] = [o.reshape(shp) for o in outs]
    w_pack = _pack([w[nm] for nm in SMALL_NAMES] + [jnp.zeros((2, N_DEV * scale_cols), F32)])
    m_pack = _pack([mom[nm] for nm in SMALL_NAMES] + [jnp.zeros((2, N_DEV * scale_cols), F32)])
    v_pack = _pack([var[nm] for nm in SMALL_NAMES] + [jnp.zeros((2, N_DEV * scale_cols), F32)])
    outs = _adam(w_pack, small_slots, m_pack, v_pack, name="adam_small")
    unpacked = [_unpack(o, small_shapes) for o in outs]
    for idx, nm in enumerate(SMALL_NAMES):
        res[nm] = [unpacked[kind][idx] for kind in range(4)]
    g_scale = lax.dynamic_slice_in_dim(unpacked[0][-1], me * scale_cols, scale_cols, axis=1)
    pad = ((0, PACK_ROWS_ALIGN - 2), (0, 0))
    outs = _adam(jnp.pad(pool_scale, pad), jnp.pad(g_scale, pad)[None], jnp.pad(m_pool_scale, pad),
                 jnp.pad(v_pool_scale, pad), name="adam_pool_scale")
    res["pool_scale"] = [o[:2] for o in outs]

    out = [loss, grad_x[None]]
    for kind in range(4):
        out += [res[nm][kind] for nm in WEIGHT_ORDER]
    return tuple(out)
```

```python
import functools
import math

import jax
import jax.numpy as jnp
from jax import lax
from jax.experimental import pallas as pl
from jax.experimental.pallas import tpu as pltpu

F32 = jnp.float32
BF16 = jnp.bfloat16
SDS = jax.ShapeDtypeStruct

N_DEV = 8
S = 2048
D = 1024
HEAD_DIM = 64
ROT_DIM = 16
ROPE_THETA = 500000.0
ATT_W = 1024
SSM_W = 512
SSM_GROUPS = 32
SSM_GROUP = 16
SSM_STATE = 64
N_CPLX = SSM_GROUPS * SSM_STATE
POOL_W = 2048
POOL_GROUP = 512
EVEN_IN = 5120
EVEN_OUT = 1536
ODD_IN = 4096
RMS_EPS = 1e-6
LANES = 128
VMEM_LIMIT = 48 * 1024 * 1024

ADAM_LR = 0.001
ADAM_B1 = 0.9
ADAM_B2 = 0.999
ADAM_EPS = 1e-08
ADAM_WD = 0.01
ADAM_STEP = 10

MESH_ID = pl.DeviceIdType.MESH
NN = (((1,), (0,)), ((), ()))
NT = (((1,), (1,)), ((), ()))
TN = (((0,), (0,)), ((), ()))
_DN = {"nn": NN, "nt": NT, "tn": TN}


def _cparams(sem):
    return pltpu.CompilerParams(dimension_semantics=sem, vmem_limit_bytes=VMEM_LIMIT)


MM_TILES = (1024, 768, 512)


def _tile(dim):
    return next((t for t in MM_TILES if dim % t == 0), dim)


def _mm(a, b, mode, out_dtype):
    if mode == "nn":
        (m, k), n = a.shape, b.shape[1]
    elif mode == "nt":
        (m, k), n = a.shape, b.shape[0]
    else:
        (k, m), n = a.shape, b.shape[1]
    tm, tn, tk = _tile(m), _tile(n), _tile(k)
    nk = k // tk

    def body(a_ref, b_ref, o_ref, acc_ref):
        kk = pl.program_id(2)
        part = lax.dot_general(a_ref[...].astype(BF16), b_ref[...].astype(BF16), _DN[mode],
                               preferred_element_type=F32)
        if nk == 1:
            o_ref[...] = part.astype(o_ref.dtype)
            return

        @pl.when(kk == 0)
        def _():
            acc_ref[...] = part

        @pl.when((kk > 0) & (kk < nk - 1))
        def _():
            acc_ref[...] += part

        @pl.when(kk == nk - 1)
        def _():
            o_ref[...] = (acc_ref[...] + part).astype(o_ref.dtype)

    if mode == "nn":
        a_spec = pl.BlockSpec((tm, tk), lambda i, j, kk: (i, kk))
        b_spec = pl.BlockSpec((tk, tn), lambda i, j, kk: (kk, j))
    elif mode == "nt":
        a_spec = pl.BlockSpec((tm, tk), lambda i, j, kk: (i, kk))
        b_spec = pl.BlockSpec((tn, tk), lambda i, j, kk: (j, kk))
    else:
        a_spec = pl.BlockSpec((tk, tm), lambda i, j, kk: (kk, i))
        b_spec = pl.BlockSpec((tk, tn), lambda i, j, kk: (kk, j))
    return pl.pallas_call(
        body, name=f"mm_{mode}_{m}x{k}x{n}",
        grid=(m // tm, n // tn, nk),
        in_specs=[a_spec, b_spec],
        out_specs=pl.BlockSpec((tm, tn), lambda i, j, kk: (i, j)),
        out_shape=SDS((m, n), out_dtype),
        scratch_shapes=[pltpu.VMEM((tm, tn) if nk > 1 else (8, LANES), F32)],
        compiler_params=_cparams(("parallel", "parallel", "arbitrary")),
    )(a, b)


def _gmm(a, b, mode, out_dtype, tm=512):
    ng, gw = POOL_W // POOL_GROUP, POOL_GROUP
    ns = S // tm
    if mode in ("nn", "nt"):
        def body(a_ref, b_ref, o_ref):
            o_ref[...] = lax.dot_general(a_ref[...].astype(BF16), b_ref[...].astype(BF16), _DN[mode],
                                         preferred_element_type=F32).astype(o_ref.dtype)

        return pl.pallas_call(
            body, name=f"gmm_{mode}", grid=(ng, ns),
            in_specs=[pl.BlockSpec((tm, gw), lambda g, i: (i, g)),
                      pl.BlockSpec((None, gw, gw), lambda g, i: (g, 0, 0))],
            out_specs=pl.BlockSpec((tm, gw), lambda g, i: (i, g)),
            out_shape=SDS((S, POOL_W), out_dtype),
            compiler_params=_cparams(("parallel", "parallel")),
        )(a, b)

    def body_tn(a_ref, b_ref, o_ref, acc_ref):
        i = pl.program_id(1)

        @pl.when(i == 0)
        def _():
            acc_ref[...] = jnp.zeros_like(acc_ref)

        acc_ref[...] += lax.dot_general(a_ref[...].astype(BF16), b_ref[...].astype(BF16), TN,
                                        preferred_element_type=F32)

        @pl.when(i == ns - 1)
        def _():
            o_ref[...] = acc_ref[...].astype(o_ref.dtype)

    return pl.pallas_call(
        body_tn, name="gmm_tn", grid=(ng, ns),
        in_specs=[pl.BlockSpec((tm, gw), lambda g, i: (i, g)),
                  pl.BlockSpec((tm, gw), lambda g, i: (i, g))],
        out_specs=pl.BlockSpec((None, gw, gw), lambda g, i: (g, 0, 0)),
        out_shape=SDS((ng, gw, gw), out_dtype),
        scratch_shapes=[pltpu.VMEM((gw, gw), F32)],
        compiler_params=_cparams(("parallel", "arbitrary")),
    )(a, b)


def _rowwise(fn, inputs, out_defs, acc_defs=(), tm=256, name=None):
    n_in, n_out, n_acc = len(inputs), len(out_defs), len(acc_defs)
    in_specs, args = [], []
    for arr, width, cb in inputs:
        if arr.shape[0] == 1:
            in_specs.append(pl.BlockSpec((1, width), lambda i, cb=cb: (0, cb)))
        else:
            in_specs.append(pl.BlockSpec((tm, width), lambda i, cb=cb: (i, cb)))
        args.append(arr)
    out_shape = [SDS((S, w), dt) for w, dt in out_defs] + [SDS((1, w), F32) for w in acc_defs]
    out_specs = ([pl.BlockSpec((tm, w), lambda i: (i, 0)) for w, _ in out_defs]
                 + [pl.BlockSpec((1, w), lambda i: (0, 0)) for w in acc_defs])

    def kern(*refs):
        vals = [r[...] for r in refs[:n_in]]
        outs, accs = fn(*vals)
        for r, v in zip(refs[n_in:n_in + n_out], outs):
            r[...] = v.astype(r.dtype)
        if n_acc:
            acc_refs = refs[n_in + n_out:]

            @pl.when(pl.program_id(0) == 0)
            def _():
                for r in acc_refs:
                    r[...] = jnp.zeros_like(r)

            for r, v in zip(acc_refs, accs):
                r[...] += jnp.sum(v, axis=0, keepdims=True)

    res = pl.pallas_call(
        kern, name=name, grid=(S // tm,), in_specs=in_specs, out_specs=out_specs, out_shape=out_shape,
        compiler_params=_cparams(("arbitrary",)),
    )(*args)
    return res


def _sigmoid(x):
    return 1.0 / (1.0 + jnp.exp(-x))


def _silu_and_grad(x):
    s = _sigmoid(x)
    return x * s, s * (1.0 + x * (1.0 - s))


_GELU_K = math.sqrt(2.0 / math.pi)
_GELU_C = 0.044715


def _gelu_and_grad(x):
    t = jnp.tanh(_GELU_K * (x + _GELU_C * (x * x * x)))
    cdf = 0.5 * (1.0 + t)
    grad = cdf + 0.5 * x * (1.0 - t * t) * (_GELU_K * (1.0 + 3.0 * _GELU_C * x * x))
    return x * cdf, grad


def _rms(xv, gain):
    r = lax.rsqrt(jnp.mean(xv * xv, axis=-1, keepdims=True) + RMS_EPS)
    return xv * r * gain


def _rms_bwd(dout, xv, gain):
    r = lax.rsqrt(jnp.mean(xv * xv, axis=-1, keepdims=True) + RMS_EPS)
    xhat = xv * r
    dxhat = dout * gain
    dx = r * (dxhat - xhat * jnp.mean(dxhat * xhat, axis=-1, keepdims=True))
    return dx, dout * xhat


def _norm_fwd(x, gain):
    (h,) = _rowwise(lambda xv, g: ((_rms(xv, g),), ()), [(x, D, 0), (gain, D, 0)], [(D, BF16)], name="norm_fwd")
    return h


def _post_fwd(x, y, gain):
    (o,) = _rowwise(lambda xv, yv, g: ((xv + _rms(yv, g),), ()), [(x, D, 0), (y, D, 0), (gain, D, 0)],
                    [(D, F32)], name="post_fwd")
    return o


def _post_bwd(g, y, gain):
    def fn(gv, yv, gn):
        dx, dg = _rms_bwd(gv, yv, gn)
        return (dx,), (dg,)

    return _rowwise(fn, [(g, D, 0), (y, D, 0), (gain, D, 0)], [(D, BF16)], [D], name="post_bwd")


def _pre_bwd(g, dh, x, gain):
    def fn(gv, dhv, xv, gn):
        dx, dg = _rms_bwd(dhv, xv, gn)
        return (gv + dx,), (dg,)

    return _rowwise(fn, [(g, D, 0), (dh, D, 0), (x, D, 0), (gain, D, 0)], [(D, F32)], [D], name="pre_bwd")


def _loss_grad(xo, tgt):
    def fn(xv, tv):
        e = xv - tv
        return (e * (1.0 / D),), (e * e,)

    return _rowwise(fn, [(xo, D, 0), (tgt, D, 0)], [(D, F32)], [D], name="loss_grad")


def _pool(u_arr, col_block, transpose, out_dtype, tc=256):
    n_t = POOL_W // tc
    per_group = POOL_GROUP // tc

    def body(u_ref, o_ref):
        c = pl.program_id(0)
        grp = c // per_group
        xv = u_ref[...]
        t = lax.broadcasted_iota(jnp.int32, (S, 1), 0)
        win = jnp.left_shift(2, grp)
        cnt = jnp.minimum(t + 1, win).astype(F32)
        cur = xv / cnt if transpose else xv
        sums = []
        for k in (1, 2, 4, 8):
            if transpose:
                sh = jnp.where(t < S - k, pltpu.roll(cur, S - k, 0), 0.0)
            else:
                sh = jnp.where(t >= k, pltpu.roll(cur, k, 0), 0.0)
            cur = cur + sh
            sums.append(cur)
        tot = jnp.where(grp == 0, sums[0], jnp.where(grp == 1, sums[1], jnp.where(grp == 2, sums[2], sums[3])))
        res = tot - xv if transpose else tot / cnt - xv
        o_ref[...] = res.astype(o_ref.dtype)

    return pl.pallas_call(
        body, name="pool_bwd" if transpose else "pool_fwd", grid=(n_t,),
        in_specs=[pl.BlockSpec((S, tc), lambda c: (0, col_block * n_t + c))],
        out_specs=pl.BlockSpec((S, tc), lambda c: (0, c)),
        out_shape=SDS((S, POOL_W), out_dtype),
        compiler_params=_cparams(("parallel",)),
    )(u_arr)


def _rope_tables():
    pos = jnp.arange(S, dtype=jnp.int32).astype(F32)
    inv_freq = ROPE_THETA ** (-jnp.arange(0, ROT_DIM, 2, dtype=F32) / ROT_DIM)
    ang = pos[:, None] * inv_freq[None, :]
    cos8, sin8 = jnp.cos(ang), jnp.sin(ang)
    half = ROT_DIM // 2
    zeros = jnp.zeros((S, HEAD_DIM - ROT_DIM), F32)
    cos = jnp.concatenate([cos8, cos8, jnp.ones((S, HEAD_DIM - ROT_DIM), F32)], axis=1)
    lo = jnp.concatenate([-sin8, jnp.zeros((S, half), F32), zeros], axis=1)
    hi = jnp.concatenate([jnp.zeros((S, half), F32), sin8, zeros], axis=1)
    rep = LANES // HEAD_DIM
    return jnp.tile(cos, (1, rep)), jnp.tile(lo, (1, rep)), jnp.tile(hi, (1, rep))


def _rope(x_arr, col_tile0, tables, scale, transpose, out_dtype):
    cos, lo, hi = tables
    sign = -1.0 if transpose else 1.0
    half = ROT_DIM // 2

    def body(x_ref, c_ref, lo_ref, hi_ref, o_ref):
        xv = x_ref[...].astype(F32)
        up = pltpu.roll(xv, LANES - half, 1)
        dn = pltpu.roll(xv, half, 1)
        res = xv * c_ref[...] + sign * (up * lo_ref[...] + dn * hi_ref[...])
        o_ref[...] = (res * scale).astype(o_ref.dtype)

    n_t = ATT_W // LANES
    tab = pl.BlockSpec((S, LANES), lambda c: (0, 0))
    return pl.pallas_call(
        body, name="rope_bwd" if transpose else "rope_fwd", grid=(n_t,),
        in_specs=[pl.BlockSpec((S, LANES), lambda c: (0, col_tile0 + c)), tab, tab, tab],
        out_specs=pl.BlockSpec((S, LANES), lambda c: (0, c)),
        out_shape=SDS((S, ATT_W), out_dtype),
        compiler_params=_cparams(("parallel",)),
    )(x_arr, cos, lo, hi)


ATT_T = 256


def _multiplicity(delta):
    ok = delta >= 0
    near = jnp.where(ok & (delta <= 128), 1.0, 0.0)
    mid = jnp.where(ok & (delta <= 512) & ((delta & 3) == 0), 1.0, 0.0)
    far = jnp.where(ok & ((delta & 15) == 0), 1.0, 0.0)
    return near + mid + far


def _head_split(v, first):
    zero = jnp.zeros_like(v)
    return [jnp.where(first, v, zero), jnp.where(first, zero, v)]


def _flash_fwd(q, k, v):
    t = ATT_T
    n_hp = ATT_W // LANES

    def body(q_ref, k_ref, v_ref, o_ref, lse_ref):
        i = pl.program_id(1)
        first = lax.broadcasted_iota(jnp.int32, (1, LANES), 1) < HEAD_DIM
        qs = _head_split(q_ref[...], first)
        base = (lax.broadcasted_iota(jnp.int32, (t, t), 0) - lax.broadcasted_iota(jnp.int32, (t, t), 1))

        def kv_step(j, carry):
            m0, l0, m1, l1, acc = carry
            off = pl.multiple_of(j * t, t)
            kb = k_ref[pl.ds(off, t), :]
            vs = _head_split(v_ref[pl.ds(off, t), :], first)
            mult = _multiplicity(base + (i - j) * t)
            valid = mult > 0.0
            new = []
            pv = None
            for h, (m_prev, l_prev) in enumerate(((m0, l0), (m1, l1))):
                s = lax.dot_general(qs[h], kb, NT, preferred_element_type=F32)
                s = jnp.where(valid, s, -1e30)
                m_new = jnp.maximum(m_prev, jnp.max(s, axis=1, keepdims=True))
                p = jnp.exp(s - m_new) * mult
                alpha = jnp.exp(m_prev - m_new)
                l_new = alpha * l_prev + jnp.sum(p, axis=1, keepdims=True)
                d = lax.dot_general(p.astype(BF16), vs[h], NN, preferred_element_type=F32)
                pv = d if pv is None else pv + d
                new.append((m_new, l_new, alpha))
            acc = acc * jnp.where(first, new[0][2], new[1][2]) + pv
            return new[0][0], new[0][1], new[1][0], new[1][1], acc

        neg = jnp.full((t, 1), -1e30, F32)
        zero = jnp.zeros((t, 1), F32)
        m0, l0, m1, l1, acc = lax.fori_loop(0, i + 1, kv_step, (neg, zero, neg, zero, jnp.zeros((t, LANES), F32)))
        o_ref[...] = acc * jnp.where(first, 1.0 / l0, 1.0 / l1)
        lse_ref[...] = jnp.where(first, m0 + jnp.log(l0), m1 + jnp.log(l1))

    blk = pl.BlockSpec((t, LANES), lambda hp, i: (i, hp))
    full = pl.BlockSpec((S, LANES), lambda hp, i: (0, hp))
    return pl.pallas_call(
        body, name="flash_fwd", grid=(n_hp, S // t),
        in_specs=[blk, full, full], out_specs=[blk, blk],
        out_shape=[SDS((S, ATT_W), F32), SDS((S, ATT_W), F32)],
        compiler_params=_cparams(("parallel", "arbitrary")),
    )(q, k, v)


def _flash_bwd(q, k, v, o, do, lse):
    t = ATT_T
    n_hp = ATT_W // LANES
    n_t = S // t

    def body(q_ref, k_ref, v_ref, o_ref, do_ref, lse_ref, dq_ref, dk_ref, dv_ref):
        j = pl.program_id(1)
        first = lax.broadcasted_iota(jnp.int32, (1, LANES), 1) < HEAD_DIM

        @pl.when(j == 0)
        def _():
            dq_ref[...] = jnp.zeros_like(dq_ref)

        kb = k_ref[...]
        vb = v_ref[...]
        ks = _head_split(kb, first)
        base = (lax.broadcasted_iota(jnp.int32, (t, t), 0) - lax.broadcasted_iota(jnp.int32, (t, t), 1))

        def q_step(i, carry):
            dk_acc, dv_acc = carry
            rows = pl.ds(pl.multiple_of(i * t, t), t)
            qs = _head_split(q_ref[rows, :], first)
            dob = do_ref[rows, :]
            prod = dob * o_ref[rows, :]
            d_all = jnp.sum(prod, axis=1, keepdims=True)
            d0 = jnp.sum(jnp.where(first, prod, 0.0), axis=1, keepdims=True)
            lse_b = lse_ref[rows, :]
            lse0 = jnp.max(jnp.where(first, lse_b, -jnp.inf), axis=1, keepdims=True)
            lse1 = jnp.max(jnp.where(first, -jnp.inf, lse_b), axis=1, keepdims=True)
            dos = _head_split(dob.astype(BF16), first)
            mult = _multiplicity(base + (i - j) * t)
            valid = mult > 0.0
            dq_t = jnp.zeros((t, LANES), F32)
            for h, (lse_h, d_h) in enumerate(((lse0, d0), (lse1, d_all - d0))):
                s = lax.dot_general(qs[h], kb, NT, preferred_element_type=F32)
                p = jnp.exp(jnp.where(valid, s, -1e30) - lse_h) * mult
                dp = lax.dot_general(dos[h], vb, NT, preferred_element_type=F32)
                ds = (p * (dp - d_h)).astype(BF16)
                dv_acc = dv_acc + lax.dot_general(p.astype(BF16), dos[h], TN, preferred_element_type=F32)
                dk_acc = dk_acc + lax.dot_general(ds, qs[h], TN, preferred_element_type=F32)
                dq_t = dq_t + lax.dot_general(ds, ks[h], NN, preferred_element_type=F32)
            dq_ref[rows, :] += dq_t
            return dk_acc, dv_acc

        zero = jnp.zeros((t, LANES), F32)
        dk_acc, dv_acc = lax.fori_loop(j, n_t, q_step, (zero, zero))
        dk_ref[...] = dk_acc
        dv_ref[...] = dv_acc

    blk = pl.BlockSpec((t, LANES), lambda hp, j: (j, hp))
    full = pl.BlockSpec((S, LANES), lambda hp, j: (0, hp))
    return pl.pallas_call(
        body, name="flash_bwd", grid=(n_hp, n_t),
        in_specs=[full, blk, blk, full, full, full], out_specs=[full, blk, blk],
        out_shape=[SDS((S, ATT_W), F32)] * 3,
        compiler_params=_cparams(("parallel", "arbitrary")),
    )(q, k, v, o, do, lse)


SCAN_T = 256
ST_ROWS = 2 * N_CPLX // LANES
HALF = ST_ROWS // 2


def _scan_fwd(lam, bu):
    def body(lam_ref, bu_ref, st_ref, carry):
        @pl.when(pl.program_id(0) == 0)
        def _():
            carry[...] = jnp.zeros_like(carry)

        ar, ai = lam_ref[0:HALF, :], lam_ref[HALF:ST_ROWS, :]

        def step(t, c):
            sr, si = c
            b = bu_ref[t]
            nr = ar * sr - ai * si + b[0:HALF]
            ni = ar * si + ai * sr + b[HALF:ST_ROWS]
            st_ref[t, 0:HALF, :] = nr
            st_ref[t, HALF:ST_ROWS, :] = ni
            return nr, ni

        sr, si = lax.fori_loop(0, SCAN_T, step, (carry[0:HALF, :], carry[HALF:ST_ROWS, :]), unroll=8)
        carry[0:HALF, :] = sr
        carry[HALF:ST_ROWS, :] = si

    blk = pl.BlockSpec((SCAN_T, ST_ROWS, LANES), lambda i: (i, 0, 0))
    return pl.pallas_call(
        body, name="scan_fwd", grid=(S // SCAN_T,),
        in_specs=[pl.BlockSpec((ST_ROWS, LANES), lambda i: (0, 0)), blk], out_specs=blk,
        out_shape=SDS((S, ST_ROWS, LANES), F32),
        scratch_shapes=[pltpu.VMEM((ST_ROWS, LANES), F32)],
        compiler_params=_cparams(("arbitrary",)),
    )(lam, bu)


def _scan_bwd(lam, dst, states):
    n_blk = S // SCAN_T

    def body(lam_ref, d_ref, st_ref, g_ref, dlam_ref, carry):
        @pl.when(pl.program_id(0) == 0)
        def _():
            carry[...] = jnp.zeros_like(carry)
            dlam_ref[...] = jnp.zeros_like(dlam_ref)

        ar, ai = lam_ref[0:HALF, :], lam_ref[HALF:ST_ROWS, :]

        def step(kk, c):
            t = SCAN_T - 1 - kk
            gr, gi, dar, dai = c
            x = st_ref[t]
            xr, xi = x[0:HALF], x[HALF:ST_ROWS]
            dar = dar + gr * xr + gi * xi
            dai = dai + gi * xr - gr * xi
            d = d_ref[t]
            ngr = d[0:HALF] + ar * gr + ai * gi
            ngi = d[HALF:ST_ROWS] + ar * gi - ai * gr
            g_ref[t, 0:HALF, :] = ngr
            g_ref[t, HALF:ST_ROWS, :] = ngi
            return ngr, ngi, dar, dai

        zero = jnp.zeros((HALF, LANES), F32)
        gr, gi, dar, dai = lax.fori_loop(0, SCAN_T, step, (carry[0:HALF, :], carry[HALF:ST_ROWS, :], zero, zero),
                                         unroll=8)
        carry[0:HALF, :] = gr
        carry[HALF:ST_ROWS, :] = gi
        dlam_ref[0:HALF, :] += dar
        dlam_ref[HALF:ST_ROWS, :] += dai

    blk = pl.BlockSpec((SCAN_T, ST_ROWS, LANES), lambda i: (n_blk - 1 - i, 0, 0))
    small = pl.BlockSpec((ST_ROWS, LANES), lambda i: (0, 0))
    return pl.pallas_call(
        body, name="scan_bwd", grid=(n_blk,),
        in_specs=[small, blk, blk], out_specs=[blk, small],
        out_shape=[SDS((S, ST_ROWS, LANES), F32), SDS((ST_ROWS, LANES), F32)],
        scratch_shapes=[pltpu.VMEM((ST_ROWS, LANES), F32)],
        compiler_params=_cparams(("arbitrary",)),
    )(lam, dst, states)


def _ssm_prep(a_re, a_im, log_dt, b_re, b_im, c_re, c_im):
    lam = lax.complex(a_re, a_im)
    dt = jnp.exp(log_dt)[:, None]
    lam_bar = jnp.exp(lam * dt)
    b_bar = ((lam_bar - 1.0) / lam)[..., None] * lax.complex(b_re, b_im)
    lam_t = jnp.concatenate([jnp.real(lam_bar).reshape(HALF, LANES), jnp.imag(lam_bar).reshape(HALF, LANES)], axis=0)
    eye = jnp.eye(SSM_GROUPS, dtype=F32)

    def in_map(m):
        return jnp.einsum("gph,gk->ghkp", m, eye).reshape(SSM_W, N_CPLX)

    def out_map(m):
        return jnp.einsum("ghp,gk->gpkh", m, eye).reshape(N_CPLX, SSM_W)

    w_b = jnp.concatenate([in_map(jnp.real(b_bar)), in_map(jnp.imag(b_bar))], axis=1)
    w_c = jnp.concatenate([out_map(c_re), -out_map(c_im)], axis=0)
    return lam_t, w_b, w_c


def _row(v):
    return v.reshape(1, -1)


def _even_fwd(x, pre, post, w_in, w_out, glu_w, glu_b, ssm_d, prep, tables):
    lam_t, w_b, w_c = prep
    h = _norm_fwd(x, pre)
    proj = _mm(h, w_in, "nn", F32)
    q = _rope(proj, 0, tables, HEAD_DIM ** -0.5, False, BF16)
    k = _rope(proj, ATT_W // LANES, tables, 1.0, False, BF16)
    v = proj[:, 2 * ATT_W:3 * ATT_W].astype(BF16)
    att, lse = _flash_fwd(q, k, v)
    u_ssm = proj[:, 4 * ATT_W:4 * ATT_W + SSM_W]
    bu = _mm(u_ssm, w_b, "nn", F32)
    states = _scan_fwd(lam_t, bu.reshape(S, ST_ROWS, LANES))
    y = _mm(states.reshape(S, 2 * N_CPLX), w_c, "nn", F32)

    def act1(yv, uv, dv):
        return (_gelu_and_grad(yv + dv * uv)[0],), ()

    (z1,) = _rowwise(act1, [(y, SSM_W, 0), (proj, SSM_W, 8), (ssm_d, SSM_W, 0)], [(SSM_W, F32)], name="ssm_act_fwd")
    lin = _mm(z1, glu_w, "nn", F32)

    def gate(att_v, ga, gs, z1v, linv, bv):
        ssm_out = z1v * _sigmoid(linv + bv)
        return (jnp.concatenate([att_v * _silu_and_grad(ga)[0], ssm_out * _silu_and_grad(gs)[0]], axis=1),), ()

    (merged,) = _rowwise(gate, [(att, ATT_W, 0), (proj, ATT_W, 3), (proj, SSM_W, 9), (z1, SSM_W, 0),
                                (lin, SSM_W, 0), (glu_b, SSM_W, 0)], [(EVEN_OUT, BF16)], name="even_gate_fwd")
    yout = _mm(merged, w_out, "nn", F32)
    x_next = _post_fwd(x, yout, post)
    saved = (x, h, proj, q, k, v, att, lse, states, y, z1, lin, merged, yout)
    return x_next, saved


def _even_bwd(g, saved, pre, post, w_in, w_out, glu_w, glu_b, ssm_d, prep, tables):
    x, h, proj, q, k, v, att, lse, states, y, z1, lin, merged, yout = saved
    lam_t, w_b, w_c = prep
    dyout, dpost = _post_bwd(g, yout, post)
    dmerged = _mm(dyout, w_out, "nt", F32)
    dw_out = _mm(merged, dyout, "tn", BF16)

    def gate_bwd(dm_a, dm_s, att_v, ga, gs, z1v, linv, bv):
        sa, dsa = _silu_and_grad(ga)
        ss, dss = _silu_and_grad(gs)
        sig = _sigmoid(linv + bv)
        ssm_out = z1v * sig
        dssm = dm_s * ss
        dlin = dssm * z1v * sig * (1.0 - sig)
        return (dm_a * sa, dm_a * att_v * dsa, dm_s * ssm_out * dss, dssm * sig, dlin), (dlin,)

    datt, dg_att, dg_ssm, dz1a, dlin, dglu_b = _rowwise(
        gate_bwd, [(dmerged, ATT_W, 0), (dmerged, SSM_W, 2), (att, ATT_W, 0), (proj, ATT_W, 3), (proj, SSM_W, 9),
                   (z1, SSM_W, 0), (lin, SSM_W, 0), (glu_b, SSM_W, 0)],
        [(ATT_W, F32), (ATT_W, BF16), (SSM_W, BF16), (SSM_W, F32), (SSM_W, BF16)], [SSM_W], name="even_gate_bwd")
    dz1b = _mm(dlin, glu_w, "nt", F32)
    dglu_w = _mm(z1, dlin, "tn", BF16)

    def act1_bwd(da, db, yv, uv, dv):
        dpre = (da + db) * _gelu_and_grad(yv + dv * uv)[1]
        return (dpre, dpre * dv), (dpre * uv,)

    dy, du_direct, dd = _rowwise(act1_bwd, [(dz1a, SSM_W, 0), (dz1b, SSM_W, 0), (y, SSM_W, 0), (proj, SSM_W, 8),
                                            (ssm_d, SSM_W, 0)], [(SSM_W, BF16), (SSM_W, F32)], [SSM_W],
                                 name="ssm_act_bwd")
    dst = _mm(dy, w_c, "nt", F32)
    dw_c = _mm(states.reshape(S, 2 * N_CPLX), dy, "tn", F32)
    gst, dlam = _scan_bwd(lam_t, dst.reshape(S, ST_ROWS, LANES), states)
    dbu = gst.reshape(S, 2 * N_CPLX)
    u_ssm = proj[:, 4 * ATT_W:4 * ATT_W + SSM_W]
    du_ssm = _mm(dbu, w_b, "nt", F32) + du_direct
    dw_b = _mm(u_ssm, dbu, "tn", F32)
    dq, dk, dv = _flash_bwd(q, k, v, att, datt, lse)
    dq = _rope(dq, 0, tables, HEAD_DIM ** -0.5, True, BF16)
    dk = _rope(dk, 0, tables, 1.0, True, BF16)
    dproj = jnp.concatenate([dq, dk, dv.astype(BF16), dg_att, du_ssm.astype(BF16), dg_ssm], axis=1)
    dw_in = _mm(h, dproj, "tn", BF16)
    dh = _mm(dproj, w_in, "nt", F32)
    g_prev, dpre = _pre_bwd(g, dh, x, pre)
    return g_prev, dict(pre=dpre, post=dpost, w_in=dw_in, w_out=dw_out, glu_w=dglu_w, glu_b=dglu_b, ssm_d=dd,
                        prep=(dlam, dw_b, dw_c))


def _odd_fwd(x, pre, post, w_in, pool_w, pool_scale, w_out):
    h = _norm_fwd(x, pre)
    proj = _mm(h, w_in, "nn", F32)
    mixed = _pool(proj, 0, False, BF16)
    ylin = _gmm(mixed, pool_w, "nn", F32)

    def gate(yl, gt, sc):
        return (yl * sc * _silu_and_grad(gt)[0],), ()

    (z,) = _rowwise(gate, [(ylin, POOL_W, 0), (proj, POOL_W, 1), (pool_scale, POOL_W, 0)], [(POOL_W, BF16)],
                    name="odd_gate_fwd")
    yout = _mm(z, w_out, "nn", F32)
    x_next = _post_fwd(x, yout, post)
    return x_next, (x, h, proj, mixed, ylin, z, yout)


def _odd_bwd(g, saved, pre, post, w_in, pool_w, pool_scale, w_out):
    x, h, proj, mixed, ylin, z, yout = saved
    dyout, dpost = _post_bwd(g, yout, post)
    dz = _mm(dyout, w_out, "nt", F32)
    dw_out = _mm(z, dyout, "tn", BF16)

    def gate_bwd(dzv, yl, gt, sc):
        sg, dsg = _silu_and_grad(gt)
        tt = dzv * sg
        return (tt * sc, dzv * yl * sc * dsg), (tt * yl,)

    dylin, dgate, dscale = _rowwise(gate_bwd, [(dz, POOL_W, 0), (ylin, POOL_W, 0), (proj, POOL_W, 1),
                                               (pool_scale, POOL_W, 0)], [(POOL_W, BF16), (POOL_W, BF16)], [POOL_W],
                                    name="odd_gate_bwd")
    dmixed = _gmm(dylin, pool_w, "nt", F32)
    dpool_w = _gmm(mixed, dylin, "tn", BF16)
    du = _pool(dmixed, 0, True, BF16)
    dproj = jnp.concatenate([du, dgate], axis=1)
    dw_in = _mm(h, dproj, "tn", BF16)
    dh = _mm(dproj, w_in, "nt", F32)
    g_prev, dpre = _pre_bwd(g, dh, x, pre)
    return g_prev, dict(pre=dpre, post=dpost, w_in=dw_in, w_out=dw_out, pool_w=dpool_w, pool_scale=dscale)


def _my_index():
    return 4 * lax.axis_index("x") + 2 * lax.axis_index("y") + lax.axis_index("c")


def _exchange(arrs, gather, name):
    n = len(arrs)
    out_shape = [SDS((N_DEV,) + a.shape, a.dtype) if gather else SDS(a.shape, a.dtype) for a in arrs]

    def body(*refs):
        ins, outs = refs[:n], refs[n:2 * n]
        send_sems, recv_sems, local_sems = refs[2 * n:]
        me = _my_index()

        def src(i, j):
            return ins[i] if gather else ins[i].at[j]

        def remote(i, j, src_slot, dst_slot, recv_slot):
            return pltpu.make_async_remote_copy(
                src_ref=src(i, src_slot), dst_ref=outs[i].at[dst_slot], send_sem=send_sems.at[i, j],
                recv_sem=recv_sems.at[i, recv_slot], device_id=(j // 4, (j // 2) % 2, j % 2), device_id_type=MESH_ID)

        def local(i):
            return pltpu.make_async_copy(src(i, me), outs[i].at[me], local_sems.at[i])

        for i in range(n):
            local(i).start()
        for j in range(N_DEV):
            @pl.when(me != j)
            def _(j=j):
                for i in range(n):
                    remote(i, j, j, me, me).start()
        for j in range(N_DEV):
            @pl.when(me != j)
            def _(j=j):
                for i in range(n):
                    remote(i, j, j, me, me).wait_send()
                    remote(i, j, j, j, j).wait_recv()
        for i in range(n):
            local(i).wait()

    any_spec = pl.BlockSpec(memory_space=pl.ANY)
    return pl.pallas_call(
        body, name=name, in_specs=[any_spec] * n, out_specs=[any_spec] * n, out_shape=out_shape,
        scratch_shapes=[pltpu.SemaphoreType.DMA((n, N_DEV)), pltpu.SemaphoreType.DMA((n, N_DEV)),
                        pltpu.SemaphoreType.DMA((n,))],
    )(*arrs)


HBM_SPEC = pl.BlockSpec(memory_space=pltpu.HBM)
SEM_SPEC = pl.BlockSpec(memory_space=pltpu.SEMAPHORE)
SPLIT_EFFECT = pltpu.SideEffectType.DATAFLOW_SIDE_EFFECTING


def _device_of(j):
    return (j // 4, (j // 2) % 2, j % 2)


def _split_copy(srcs, lands, send_sems, recv_sems, gather, i, j, dst_slot, recv_slot):
    return pltpu.make_async_remote_copy(
        src_ref=srcs[i] if gather else srcs[i].at[j], dst_ref=lands[i].at[dst_slot],
        send_sem=send_sems.at[i * N_DEV + j], recv_sem=recv_sems.at[i * N_DEV + recv_slot],
        device_id=_device_of(j), device_id_type=MESH_ID)


def _xchg_start(name, srcs, lands, gather):
    n = len(srcs)

    def body(*refs):
        src_refs, land_refs = refs[:n], refs[n:2 * n]
        send_sems, recv_sems, token = refs[2 * n], refs[2 * n + 1], refs[-1]
        me = _my_index()
        for j in range(N_DEV):
            @pl.when(me != j)
            def _(j=j):
                for i in range(n):
                    _split_copy(src_refs, land_refs, send_sems, recv_sems, gather, i, j, me, me).start()
        token[...] = jnp.zeros_like(token)

    thru = [pltpu.HBM(a.shape, a.dtype) for a in list(srcs) + list(lands)]
    res = pl.pallas_call(
        body, name=name,
        out_shape=(pltpu.SemaphoreType.DMA((n * N_DEV,)), pltpu.SemaphoreType.DMA((n * N_DEV,)), *thru,
                   SDS((8, LANES), F32)),
        in_specs=[HBM_SPEC] * (2 * n),
        out_specs=(SEM_SPEC, SEM_SPEC, *([HBM_SPEC] * (2 * n)), pl.BlockSpec(memory_space=pltpu.VMEM)),
        input_output_aliases={i: 2 + i for i in range(2 * n)},
        compiler_params=pltpu.CompilerParams(has_side_effects=SPLIT_EFFECT),
    )(*[pltpu.with_memory_space_constraint(a, pltpu.HBM) for a in list(srcs) + list(lands)])
    return res[0], res[1], list(res[2:2 + n]), list(res[2 + n:2 + 2 * n]), res[-1]


def _xchg_wait(name, started, gather, after):
    send_sems, recv_sems, srcs, lands, _ = started
    n = len(srcs)

    def body(*refs):
        src_refs, land_refs = refs[:n], refs[n:2 * n]
        send_r, recv_r = refs[2 * n], refs[2 * n + 1]
        me = _my_index()
        for j in range(N_DEV):
            @pl.when(me != j)
            def _(j=j):
                for i in range(n):
                    _split_copy(src_refs, land_refs, send_r, recv_r, gather, i, j, me, me).wait_send()
                    _split_copy(src_refs, land_refs, send_r, recv_r, gather, i, j, j, j).wait_recv()

    thru = [pltpu.HBM(a.shape, a.dtype) for a in list(srcs) + list(lands)]
    res = pl.pallas_call(
        body, name=name, out_shape=tuple(thru),
        in_specs=[HBM_SPEC] * (2 * n) + [SEM_SPEC, SEM_SPEC, pl.BlockSpec(memory_space=pl.ANY)],
        out_specs=tuple([HBM_SPEC] * (2 * n)),
        input_output_aliases={i: i for i in range(2 * n)},
        compiler_params=pltpu.CompilerParams(has_side_effects=SPLIT_EFFECT),
    )(*srcs, *lands, send_sems, recv_sems, after)
    return list(res[n:])


def _own_slot_landing(own):
    zone = lax.empty((N_DEV,) + own.shape, own.dtype)
    return lax.dynamic_update_slice(zone, own[None], (_my_index(),) + (0,) * own.ndim)


def _adam_layers(w, slot_list, m, v, name):
    n_l, r, c = w.shape
    ns = slot_list[0].shape[0]
    tr = r
    while tr * c * 4 > (1 << 20) and tr % 16 == 0:
        tr //= 2
    assert r % tr == 0 and len(slot_list) == n_l

    def body(*refs):
        w_ref, slot_refs = refs[0], refs[1:1 + n_l]
        m_ref, v_ref, go_ref, d_ref, mo_ref, vo_ref = refs[1 + n_l:]
        layer = pl.program_id(0)
        g = None
        for l, g_ref in enumerate(slot_refs):
            gl = g_ref[0].astype(F32)
            for s in range(1, ns):
                gl = gl + g_ref[s].astype(F32)
            g = gl if g is None else jnp.where(layer == l, gl, g)
        mn = ADAM_B1 * m_ref[...] + (1.0 - ADAM_B1) * g
        vn = ADAM_B2 * v_ref[...] + (1.0 - ADAM_B2) * (g * g)
        m_hat = mn / (1.0 - ADAM_B1 ** ADAM_STEP)
        v_hat = vn / (1.0 - ADAM_B2 ** ADAM_STEP)
        go_ref[...] = g
        d_ref[...] = -ADAM_LR * (m_hat / (jnp.sqrt(v_hat) + ADAM_EPS) + ADAM_WD * w_ref[...])
        mo_ref[...] = mn
        vo_ref[...] = vn

    blk = pl.BlockSpec((None, tr, c), lambda l, i: (l, i, 0))
    slot_specs = [pl.BlockSpec((ns, tr, c), lambda l, i, k=k: (0, jnp.where(l == k, i, 0), 0)) for k in range(n_l)]
    return pl.pallas_call(
        body, name=name, grid=(n_l, r // tr),
        in_specs=[blk] + slot_specs + [blk, blk],
        out_specs=[blk] * 4, out_shape=[SDS((n_l, r, c), F32)] * 4,
        compiler_params=_cparams(("arbitrary", "arbitrary")),
    )(w, *slot_list, m, v)


def _adam(w, gslots, m, v, name):
    r, c = w.shape
    ns = gslots.shape[0]
    tr = r
    while tr * c * 4 > (1 << 20) and tr % 16 == 0:
        tr //= 2
    assert r % tr == 0

    def body(w_ref, g_ref, m_ref, v_ref, go_ref, d_ref, mo_ref, vo_ref):
        g = g_ref[0].astype(F32)
        for s in range(1, ns):
            g = g + g_ref[s].astype(F32)
        wv = w_ref[...]
        mn = ADAM_B1 * m_ref[...] + (1.0 - ADAM_B1) * g
        vn = ADAM_B2 * v_ref[...] + (1.0 - ADAM_B2) * (g * g)
        m_hat = mn / (1.0 - ADAM_B1 ** ADAM_STEP)
        v_hat = vn / (1.0 - ADAM_B2 ** ADAM_STEP)
        go_ref[...] = g
        d_ref[...] = -ADAM_LR * (m_hat / (jnp.sqrt(v_hat) + ADAM_EPS) + ADAM_WD * wv)
        mo_ref[...] = mn
        vo_ref[...] = vn

    blk = pl.BlockSpec((tr, c), lambda i: (i, 0))
    return pl.pallas_call(
        body, name=name, grid=(r // tr,),
        in_specs=[blk, pl.BlockSpec((ns, tr, c), lambda i: (0, i, 0)), blk, blk],
        out_specs=[blk] * 4, out_shape=[SDS((r, c), F32)] * 4,
        compiler_params=_cparams(("parallel",)),
    )(w, gslots, m, v)


SMALL_NAMES = ("pre_norm", "post_norm", "ssm_a_re", "ssm_a_im", "ssm_log_dt", "ssm_b_re", "ssm_b_im", "ssm_c_re",
               "ssm_c_im", "ssm_d", "ssm_glu_b")
SHARDED_NAMES = ("even_w_in", "even_w_out", "ssm_glu_w", "odd_w_in", "pool_w", "odd_w_out")
WEIGHT_ORDER = ("pre_norm", "post_norm", "even_w_in", "even_w_out", "ssm_a_re", "ssm_a_im", "ssm_log_dt", "ssm_b_re",
                "ssm_b_im", "ssm_c_re", "ssm_c_im", "ssm_d", "ssm_glu_w", "ssm_glu_b", "odd_w_in", "pool_w",
                "pool_scale", "odd_w_out")
PACK_ROWS_ALIGN = 8


def _pack(parts):
    flat = jnp.concatenate([p.reshape(-1).astype(F32) for p in parts])
    rows = -(-flat.shape[0] // (LANES * PACK_ROWS_ALIGN)) * PACK_ROWS_ALIGN
    return jnp.pad(flat, (0, rows * LANES - flat.shape[0])).reshape(rows, LANES)


def _unpack(packed, shapes):
    flat = packed.reshape(-1)
    out, off = [], 0
    for shp in shapes:
        size = math.prod(shp)
        out.append(flat[off:off + size].reshape(shp))
        off += size
    return out


EVEN_SHARDED = ("w_in", "w_out", "glu_w")
ODD_SHARDED = ("w_in", "pool_w", "w_out")
FAMILY = {(0, "w_in"): "even_w_in", (0, "w_out"): "even_w_out", (0, "glu_w"): "ssm_glu_w",
          (1, "w_in"): "odd_w_in", (1, "pool_w"): "pool_w", (1, "w_out"): "odd_w_out"}


def _sharded_keys(layer):
    return EVEN_SHARDED if layer % 2 == 0 else ODD_SHARDED


def _local_step(x, tgt, small, get_weights, on_grads):
    tables = _rope_tables()
    preps, prep_vjps = [], []
    for i in range(2):
        out, vjp = jax.vjp(_ssm_prep, small["ssm_a_re"][i], small["ssm_a_im"][i], small["ssm_log_dt"][i],
                           small["ssm_b_re"][i], small["ssm_b_im"][i], small["ssm_c_re"][i], small["ssm_c_im"][i])
        preps.append(out)
        prep_vjps.append(vjp)

    def layer_args(layer, wts):
        i = layer // 2
        pre, post = _row(small["pre_norm"][layer]), _row(small["post_norm"][layer])
        if layer % 2 == 0:
            return (pre, post, wts["w_in"], wts["w_out"], wts["glu_w"], _row(small["ssm_glu_b"][i]),
                    _row(small["ssm_d"][i]), preps[i], tables)
        return (pre, post, wts["w_in"], wts["pool_w"], _row(wts["pool_scale"]), wts["w_out"])

    saved, args = [], []
    cur = x
    for layer in range(4):
        args.append(layer_args(layer, get_weights(layer, cur)))
        cur, sv = (_even_fwd if layer % 2 == 0 else _odd_fwd)(cur, *args[layer])
        saved.append(sv)
    g, sq = _loss_grad(cur, tgt)
    loss = 0.5 * jnp.sum(sq) / D

    lg = [None] * 4
    token = jnp.zeros((), F32)
    for layer in reversed(range(4)):
        largs = list(args[layer])
        largs[1] = largs[1] + token
        g, lg[layer] = (_even_bwd if layer % 2 == 0 else _odd_bwd)(g, saved[layer], *largs)
        token = on_grads(layer, {k: lg[layer][k] for k in _sharded_keys(layer)})

    ssm_g = [prep_vjps[i](lg[2 * i]["prep"]) for i in range(2)]
    grads = {
        "pre_norm": jnp.concatenate([lg[l]["pre"] for l in range(4)], axis=0),
        "post_norm": jnp.concatenate([lg[l]["post"] for l in range(4)], axis=0),
        "ssm_d": jnp.concatenate([lg[0]["ssm_d"], lg[2]["ssm_d"]], axis=0),
        "ssm_glu_b": jnp.concatenate([lg[0]["glu_b"], lg[2]["glu_b"]], axis=0),
        "pool_scale": jnp.concatenate([lg[1]["pool_scale"], lg[3]["pool_scale"]], axis=0),
    }
    for idx, nm in enumerate(("ssm_a_re", "ssm_a_im", "ssm_log_dt", "ssm_b_re", "ssm_b_im", "ssm_c_re", "ssm_c_im")):
        grads[nm] = jnp.stack([ssm_g[0][idx], ssm_g[1][idx]])
    return loss, g, grads, token


def _to_slots(key, gfull):
    if key == "w_in":
        kk, nn = gfull.shape
        return gfull.reshape(kk, N_DEV, nn // N_DEV).transpose(1, 0, 2)
    if key in ("w_out", "glu_w"):
        rr, nn = gfull.shape
        return gfull.reshape(N_DEV, rr // N_DEV, nn)
    assert key == "pool_w"
    gg, rr, nn = gfull.shape
    return gfull.reshape(gg, N_DEV, rr // N_DEV, nn).transpose(1, 0, 2, 3)


def _from_gathered(key, gat):
    if key == "w_in":
        _, kk, nn = gat.shape
        return gat.transpose(1, 0, 2).reshape(kk, N_DEV * nn)
    if key in ("w_out", "glu_w"):
        _, rr, nn = gat.shape
        return gat.reshape(N_DEV * rr, nn)
    assert key == "pool_w"
    _, gg, rr, nn = gat.shape
    return gat.transpose(1, 0, 2, 3).reshape(gg, N_DEV * rr, nn)


def kernel(x, pre_norm, post_norm, even_w_in, even_w_out, ssm_a_re, ssm_a_im, ssm_log_dt, ssm_b_re, ssm_b_im, ssm_c_re, ssm_c_im, ssm_d, ssm_glu_w, ssm_glu_b, odd_w_in, pool_w, pool_scale, odd_w_out, loss_target, m_pre_norm, m_post_norm, m_even_w_in, m_even_w_out, m_ssm_a_re, m_ssm_a_im, m_ssm_log_dt, m_ssm_b_re, m_ssm_b_im, m_ssm_c_re, m_ssm_c_im, m_ssm_d, m_ssm_glu_w, m_ssm_glu_b, m_odd_w_in, m_pool_w, m_pool_scale, m_odd_w_out, v_pre_norm, v_post_norm, v_even_w_in, v_even_w_out, v_ssm_a_re, v_ssm_a_im, v_ssm_log_dt, v_ssm_b_re, v_ssm_b_im, v_ssm_c_re, v_ssm_c_im, v_ssm_d, v_ssm_glu_w, v_ssm_glu_b, v_odd_w_in, v_pool_w, v_pool_scale, v_odd_w_out):
    w = dict(pre_norm=pre_norm, post_norm=post_norm, even_w_in=even_w_in, even_w_out=even_w_out, ssm_a_re=ssm_a_re,
             ssm_a_im=ssm_a_im, ssm_log_dt=ssm_log_dt, ssm_b_re=ssm_b_re, ssm_b_im=ssm_b_im, ssm_c_re=ssm_c_re,
             ssm_c_im=ssm_c_im, ssm_d=ssm_d, ssm_glu_w=ssm_glu_w, ssm_glu_b=ssm_glu_b, odd_w_in=odd_w_in,
             pool_w=pool_w, pool_scale=pool_scale, odd_w_out=odd_w_out)
    mom = dict(pre_norm=m_pre_norm, post_norm=m_post_norm, even_w_in=m_even_w_in, even_w_out=m_even_w_out,
               ssm_a_re=m_ssm_a_re, ssm_a_im=m_ssm_a_im, ssm_log_dt=m_ssm_log_dt, ssm_b_re=m_ssm_b_re,
               ssm_b_im=m_ssm_b_im, ssm_c_re=m_ssm_c_re, ssm_c_im=m_ssm_c_im, ssm_d=m_ssm_d, ssm_glu_w=m_ssm_glu_w,
               ssm_glu_b=m_ssm_glu_b, odd_w_in=m_odd_w_in, pool_w=m_pool_w, pool_scale=m_pool_scale,
               odd_w_out=m_odd_w_out)
    var = dict(pre_norm=v_pre_norm, post_norm=v_post_norm, even_w_in=v_even_w_in, even_w_out=v_even_w_out,
               ssm_a_re=v_ssm_a_re, ssm_a_im=v_ssm_a_im, ssm_log_dt=v_ssm_log_dt, ssm_b_re=v_ssm_b_re,
               ssm_b_im=v_ssm_b_im, ssm_c_re=v_ssm_c_re, ssm_c_im=v_ssm_c_im, ssm_d=v_ssm_d, ssm_glu_w=v_ssm_glu_w,
               ssm_glu_b=v_ssm_glu_b, odd_w_in=v_odd_w_in, pool_w=v_pool_w, pool_scale=v_pool_scale,
               odd_w_out=v_odd_w_out)
    me = _my_index()
    scale_cols = pool_scale.shape[1]

    gather_started = []
    for layer in range(4):
        i = layer // 2
        shards = [w[FAMILY[(layer % 2, k)]][i].astype(BF16) for k in _sharded_keys(layer)]
        if layer % 2 == 1:
            shards.append(jnp.pad(pool_scale[i][None], ((0, PACK_ROWS_ALIGN - 1), (0, 0))))
        gather_started.append(_xchg_start(f"gather_start_{layer}", shards, [_own_slot_landing(s) for s in shards], True))
    token = sum(st[4][0, 0] for st in gather_started)
    small = {nm: w[nm] for nm in SMALL_NAMES}
    small["pre_norm"] = pre_norm + token

    def get_weights(layer, after):
        lands = _xchg_wait(f"gather_wait_{layer}", gather_started[layer], True, after)
        wts = {k: _from_gathered(k, gat) for k, gat in zip(_sharded_keys(layer), lands)}
        if layer % 2 == 1:
            wts["pool_scale"] = lands[-1][:, 0, :].reshape(N_DEV * scale_cols)
        return wts

    scatter_started = [None] * 4

    def on_grads(layer, lg):
        slots = [_to_slots(k, lg[k]) for k in _sharded_keys(layer)]
        lands = [_own_slot_landing(lax.dynamic_index_in_dim(s, me, 0, keepdims=False)) for s in slots]
        scatter_started[layer] = _xchg_start(f"scatter_start_{layer}", slots, lands, False)
        return scatter_started[layer][4][0, 0]

    loss_local, grad_x, grads, token = _local_step(x[0], loss_target[0], small, get_weights, on_grads)
    loss = lax.psum(loss_local, ("x", "y", "c"))

    small_parts = [grads[nm] for nm in SMALL_NAMES] + [grads["pool_scale"]]
    small_shapes = [w[nm].shape for nm in SMALL_NAMES] + [(2, N_DEV * scale_cols)]
    (small_slots,) = _exchange([_pack(small_parts) + token], True, "gather_small_grads")

    recv = {}
    for layer in (3, 2, 1, 0):
        lands = _xchg_wait(f"scatter_wait_{layer}", scatter_started[layer], False, small_slots)
        for k, land in zip(_sharded_keys(layer), lands):
            recv[(layer, k)] = land
    res = {}
    for (parity, k), nm in FAMILY.items():
        shp = w[nm].shape
        cols = shp[-1]
        slot_list = [recv[(parity + 2 * i, k)].reshape(N_DEV, -1, cols) for i in range(2)]
        outs = _adam_layers(w[nm].reshape(2, -1, cols), slot_list, mom[nm].reshape(2, -1, cols),
                            var[nm].reshape(2, -1, cols), name=f"adam_{nm}")
        res[nm] = [o.reshape(shp) for o in outs]
    w_pack = _pack([w[nm] for nm in SMALL_NAMES] + [jnp.zeros((2, N_DEV * scale_cols), F32)])
    m_pack = _pack([mom[nm] for nm in SMALL_NAMES] + [jnp.zeros((2, N_DEV * scale_cols), F32)])
    v_pack = _pack([var[nm] for nm in SMALL_NAMES] + [jnp.zeros((2, N_DEV * scale_cols), F32)])
    outs = _adam(w_pack, small_slots, m_pack, v_pack, name="adam_small")
    unpacked = [_unpack(o, small_shapes) for o in outs]
    for idx, nm in enumerate(SMALL_NAMES):
        res[nm] = [unpacked[kind][idx] for kind in range(4)]
    g_scale = lax.dynamic_slice_in_dim(unpacked[0][-1], me * scale_cols, scale_cols, axis=1)
    pad = ((0, PACK_ROWS_ALIGN - 2), (0, 0))
    outs = _adam(jnp.pad(pool_scale, pad), jnp.pad(g_scale, pad)[None], jnp.pad(m_pool_scale, pad),
                 jnp.pad(v_pool_scale, pad), name="adam_pool_scale")
    res["pool_scale"] = [o[:2] for o in outs]

    out = [loss, grad_x[None]]
    for kind in range(4):
        out += [res[nm][kind] for nm in WEIGHT_ORDER]
    return tuple(out)
```

```python
import functools
import math

import jax
import jax.numpy as jnp
from jax import lax
from jax.experimental import pallas as pl
from jax.experimental.pallas import tpu as pltpu

F32 = jnp.float32
BF16 = jnp.bfloat16
SDS = jax.ShapeDtypeStruct

N_DEV = 8
S = 2048
D = 1024
HEAD_DIM = 64
ROT_DIM = 16
ROPE_THETA = 500000.0
ATT_W = 1024
SSM_W = 512
SSM_GROUPS = 32
SSM_GROUP = 16
SSM_STATE = 64
N_CPLX = SSM_GROUPS * SSM_STATE
POOL_W = 2048
POOL_GROUP = 512
EVEN_IN = 5120
EVEN_OUT = 1536
ODD_IN = 4096
RMS_EPS = 1e-6
LANES = 128
VMEM_LIMIT = 48 * 1024 * 1024

ADAM_LR = 0.001
ADAM_B1 = 0.9
ADAM_B2 = 0.999
ADAM_EPS = 1e-08
ADAM_WD = 0.01
ADAM_STEP = 10

MESH_ID = pl.DeviceIdType.MESH
NN = (((1,), (0,)), ((), ()))
NT = (((1,), (1,)), ((), ()))
TN = (((0,), (0,)), ((), ()))
_DN = {"nn": NN, "nt": NT, "tn": TN}


def _cparams(sem):
    return pltpu.CompilerParams(dimension_semantics=sem, vmem_limit_bytes=VMEM_LIMIT)


MM_TILES = (1024, 768, 512)


def _tile(dim):
    return next((t for t in MM_TILES if dim % t == 0), dim)


def _mm(a, b, mode, out_dtype, b_blocks=False, out_blocks=False):
    if b_blocks:
        nblk, rows, cb = b.shape
        b2_shape = (rows, nblk * cb)
    else:
        b2_shape = b.shape
    if mode == "nn":
        (m, k), n = a.shape, b2_shape[1]
    elif mode == "nt":
        (m, k), n = a.shape, b2_shape[0]
    else:
        (k, m), n = a.shape, b2_shape[1]
    tm, tn, tk = _tile(m), _tile(n), _tile(k)
    if b_blocks and mode == "nn":
        tn = cb
    if b_blocks and mode == "nt":
        tk = cb
    if out_blocks:
        tn = n // N_DEV
    nk = k // tk

    def body(a_ref, b_ref, o_ref, acc_ref):
        kk = pl.program_id(2)
        part = lax.dot_general(a_ref[...].astype(BF16), b_ref[...].astype(BF16), _DN[mode],
                               preferred_element_type=F32)
        if nk == 1:
            o_ref[...] = part.astype(o_ref.dtype)
            return

        @pl.when(kk == 0)
        def _():
            acc_ref[...] = part

        @pl.when((kk > 0) & (kk < nk - 1))
        def _():
            acc_ref[...] += part

        @pl.when(kk == nk - 1)
        def _():
            o_ref[...] = (acc_ref[...] + part).astype(o_ref.dtype)

    if mode == "nn":
        a_spec = pl.BlockSpec((tm, tk), lambda i, j, kk: (i, kk))
        b_spec = pl.BlockSpec((tk, tn), lambda i, j, kk: (kk, j))
    elif mode == "nt":
        a_spec = pl.BlockSpec((tm, tk), lambda i, j, kk: (i, kk))
        b_spec = pl.BlockSpec((tn, tk), lambda i, j, kk: (j, kk))
    else:
        a_spec = pl.BlockSpec((tk, tm), lambda i, j, kk: (kk, i))
        b_spec = pl.BlockSpec((tk, tn), lambda i, j, kk: (kk, j))
    if b_blocks and mode == "nn":
        b_spec = pl.BlockSpec((None, tk, cb), lambda i, j, kk: (j, kk, 0))
    if b_blocks and mode == "nt":
        b_spec = pl.BlockSpec((None, tn, cb), lambda i, j, kk: (kk, j, 0))
    out_spec = pl.BlockSpec((tm, tn), lambda i, j, kk: (i, j))
    out_shape = SDS((m, n), out_dtype)
    if out_blocks:
        out_spec = pl.BlockSpec((None, tm, tn), lambda i, j, kk: (j, i, 0))
        out_shape = SDS((N_DEV, m, tn), out_dtype)
    return pl.pallas_call(
        body, name=f"mm_{mode}_{m}x{k}x{n}",
        grid=(m // tm, n // tn, nk),
        in_specs=[a_spec, b_spec],
        out_specs=out_spec,
        out_shape=out_shape,
        scratch_shapes=[pltpu.VMEM((tm, tn) if nk > 1 else (8, LANES), F32)],
        compiler_params=_cparams(("parallel", "parallel", "arbitrary")),
    )(a, b)


def _gmm(a, b, mode, out_dtype, tm=512):
    ng, gw = POOL_W // POOL_GROUP, POOL_GROUP
    ns = S // tm
    if mode in ("nn", "nt"):
        def body(a_ref, b_ref, o_ref):
            o_ref[...] = lax.dot_general(a_ref[...].astype(BF16), b_ref[...].astype(BF16), _DN[mode],
                                         preferred_element_type=F32).astype(o_ref.dtype)

        return pl.pallas_call(
            body, name=f"gmm_{mode}", grid=(ng, ns),
            in_specs=[pl.BlockSpec((tm, gw), lambda g, i: (i, g)),
                      pl.BlockSpec((None, gw, gw), lambda g, i: (g, 0, 0))],
            out_specs=pl.BlockSpec((tm, gw), lambda g, i: (i, g)),
            out_shape=SDS((S, POOL_W), out_dtype),
            compiler_params=_cparams(("parallel", "parallel")),
        )(a, b)

    def body_tn(a_ref, b_ref, o_ref, acc_ref):
        i = pl.program_id(1)

        @pl.when(i == 0)
        def _():
            acc_ref[...] = jnp.zeros_like(acc_ref)

        acc_ref[...] += lax.dot_general(a_ref[...].astype(BF16), b_ref[...].astype(BF16), TN,
                                        preferred_element_type=F32)

        @pl.when(i == ns - 1)
        def _():
            o_ref[...] = acc_ref[...].astype(o_ref.dtype)

    return pl.pallas_call(
        body_tn, name="gmm_tn", grid=(ng, ns),
        in_specs=[pl.BlockSpec((tm, gw), lambda g, i: (i, g)),
                  pl.BlockSpec((tm, gw), lambda g, i: (i, g))],
        out_specs=pl.BlockSpec((None, gw, gw), lambda g, i: (g, 0, 0)),
        out_shape=SDS((ng, gw, gw), out_dtype),
        scratch_shapes=[pltpu.VMEM((gw, gw), F32)],
        compiler_params=_cparams(("parallel", "arbitrary")),
    )(a, b)


def _rowwise(fn, inputs, out_defs, acc_defs=(), tm=256, name=None):
    n_in, n_out, n_acc = len(inputs), len(out_defs), len(acc_defs)
    in_specs, args = [], []
    for arr, width, cb in inputs:
        if arr.shape[0] == 1:
            in_specs.append(pl.BlockSpec((1, width), lambda i, cb=cb: (0, cb)))
        else:
            in_specs.append(pl.BlockSpec((tm, width), lambda i, cb=cb: (i, cb)))
        args.append(arr)
    out_shape = [SDS((S, w), dt) for w, dt in out_defs] + [SDS((1, w), F32) for w in acc_defs]
    out_specs = ([pl.BlockSpec((tm, w), lambda i: (i, 0)) for w, _ in out_defs]
                 + [pl.BlockSpec((1, w), lambda i: (0, 0)) for w in acc_defs])

    def kern(*refs):
        vals = [r[...] for r in refs[:n_in]]
        outs, accs = fn(*vals)
        for r, v in zip(refs[n_in:n_in + n_out], outs):
            r[...] = v.astype(r.dtype)
        if n_acc:
            acc_refs = refs[n_in + n_out:]

            @pl.when(pl.program_id(0) == 0)
            def _():
                for r in acc_refs:
                    r[...] = jnp.zeros_like(r)

            for r, v in zip(acc_refs, accs):
                r[...] += jnp.sum(v, axis=0, keepdims=True)

    res = pl.pallas_call(
        kern, name=name, grid=(S // tm,), in_specs=in_specs, out_specs=out_specs, out_shape=out_shape,
        compiler_params=_cparams(("arbitrary",)),
    )(*args)
    return res


def _sigmoid(x):
    return 1.0 / (1.0 + jnp.exp(-x))


def _silu_and_grad(x):
    s = _sigmoid(x)
    return x * s, s * (1.0 + x * (1.0 - s))


_GELU_K = math.sqrt(2.0 / math.pi)
_GELU_C = 0.044715


def _gelu_and_grad(x):
    t = jnp.tanh(_GELU_K * (x + _GELU_C * (x * x * x)))
    cdf = 0.5 * (1.0 + t)
    grad = cdf + 0.5 * x * (1.0 - t * t) * (_GELU_K * (1.0 + 3.0 * _GELU_C * x * x))
    return x * cdf, grad


def _rms(xv, gain):
    r = lax.rsqrt(jnp.mean(xv * xv, axis=-1, keepdims=True) + RMS_EPS)
    return xv * r * gain


def _rms_bwd(dout, xv, gain):
    r = lax.rsqrt(jnp.mean(xv * xv, axis=-1, keepdims=True) + RMS_EPS)
    xhat = xv * r
    dxhat = dout * gain
    dx = r * (dxhat - xhat * jnp.mean(dxhat * xhat, axis=-1, keepdims=True))
    return dx, dout * xhat


def _norm_fwd(x, gain):
    (h,) = _rowwise(lambda xv, g: ((_rms(xv, g),), ()), [(x, D, 0), (gain, D, 0)], [(D, BF16)], name="norm_fwd")
    return h


def _post_fwd(x, y, gain):
    (o,) = _rowwise(lambda xv, yv, g: ((xv + _rms(yv, g),), ()), [(x, D, 0), (y, D, 0), (gain, D, 0)],
                    [(D, F32)], name="post_fwd")
    return o


def _post_bwd(g, y, gain):
    def fn(gv, yv, gn):
        dx, dg = _rms_bwd(gv, yv, gn)
        return (dx,), (dg,)

    return _rowwise(fn, [(g, D, 0), (y, D, 0), (gain, D, 0)], [(D, BF16)], [D], name="post_bwd")


def _pre_bwd(g, dh, x, gain):
    def fn(gv, dhv, xv, gn):
        dx, dg = _rms_bwd(dhv, xv, gn)
        return (gv + dx,), (dg,)

    return _rowwise(fn, [(g, D, 0), (dh, D, 0), (x, D, 0), (gain, D, 0)], [(D, F32)], [D], name="pre_bwd")


def _loss_grad(xo, tgt):
    def fn(xv, tv):
        e = xv - tv
        return (e * (1.0 / D),), (e * e,)

    return _rowwise(fn, [(xo, D, 0), (tgt, D, 0)], [(D, F32)], [D], name="loss_grad")


def _pool(u_arr, col_block, transpose, out_dtype, tc=256):
    n_t = POOL_W // tc
    per_group = POOL_GROUP // tc

    def body(u_ref, o_ref):
        c = pl.program_id(0)
        grp = c // per_group
        xv = u_ref[...]
        t = lax.broadcasted_iota(jnp.int32, (S, 1), 0)
        win = jnp.left_shift(2, grp)
        cnt = jnp.minimum(t + 1, win).astype(F32)
        cur = xv / cnt if transpose else xv
        sums = []
        for k in (1, 2, 4, 8):
            if transpose:
                sh = jnp.where(t < S - k, pltpu.roll(cur, S - k, 0), 0.0)
            else:
                sh = jnp.where(t >= k, pltpu.roll(cur, k, 0), 0.0)
            cur = cur + sh
            sums.append(cur)
        tot = jnp.where(grp == 0, sums[0], jnp.where(grp == 1, sums[1], jnp.where(grp == 2, sums[2], sums[3])))
        res = tot - xv if transpose else tot / cnt - xv
        o_ref[...] = res.astype(o_ref.dtype)

    return pl.pallas_call(
        body, name="pool_bwd" if transpose else "pool_fwd", grid=(n_t,),
        in_specs=[pl.BlockSpec((S, tc), lambda c: (0, col_block * n_t + c))],
        out_specs=pl.BlockSpec((S, tc), lambda c: (0, c)),
        out_shape=SDS((S, POOL_W), out_dtype),
        compiler_params=_cparams(("parallel",)),
    )(u_arr)


def _rope_tables():
    pos = jnp.arange(S, dtype=jnp.int32).astype(F32)
    inv_freq = ROPE_THETA ** (-jnp.arange(0, ROT_DIM, 2, dtype=F32) / ROT_DIM)
    ang = pos[:, None] * inv_freq[None, :]
    cos8, sin8 = jnp.cos(ang), jnp.sin(ang)
    half = ROT_DIM // 2
    zeros = jnp.zeros((S, HEAD_DIM - ROT_DIM), F32)
    cos = jnp.concatenate([cos8, cos8, jnp.ones((S, HEAD_DIM - ROT_DIM), F32)], axis=1)
    lo = jnp.concatenate([-sin8, jnp.zeros((S, half), F32), zeros], axis=1)
    hi = jnp.concatenate([jnp.zeros((S, half), F32), sin8, zeros], axis=1)
    rep = LANES // HEAD_DIM
    return jnp.tile(cos, (1, rep)), jnp.tile(lo, (1, rep)), jnp.tile(hi, (1, rep))


def _rope(x_arr, col_tile0, tables, scale, transpose, out_dtype):
    cos, lo, hi = tables
    sign = -1.0 if transpose else 1.0
    half = ROT_DIM // 2

    def body(x_ref, c_ref, lo_ref, hi_ref, o_ref):
        xv = x_ref[...].astype(F32)
        up = pltpu.roll(xv, LANES - half, 1)
        dn = pltpu.roll(xv, half, 1)
        res = xv * c_ref[...] + sign * (up * lo_ref[...] + dn * hi_ref[...])
        o_ref[...] = (res * scale).astype(o_ref.dtype)

    n_t = ATT_W // LANES
    tab = pl.BlockSpec((S, LANES), lambda c: (0, 0))
    return pl.pallas_call(
        body, name="rope_bwd" if transpose else "rope_fwd", grid=(n_t,),
        in_specs=[pl.BlockSpec((S, LANES), lambda c: (0, col_tile0 + c)), tab, tab, tab],
        out_specs=pl.BlockSpec((S, LANES), lambda c: (0, c)),
        out_shape=SDS((S, ATT_W), out_dtype),
        compiler_params=_cparams(("parallel",)),
    )(x_arr, cos, lo, hi)


ATT_T = 512


def _multiplicity(delta):
    ok = delta >= 0
    near = jnp.where(ok & (delta <= 128), 1.0, 0.0)
    mid = jnp.where(ok & (delta <= 512) & ((delta & 3) == 0), 1.0, 0.0)
    far = jnp.where(ok & ((delta & 15) == 0), 1.0, 0.0)
    return near + mid + far


def _attention_bias():
    t = ATT_T
    pos = jnp.arange(t, dtype=jnp.int32)
    delta = jnp.arange(S // t, dtype=jnp.int32)[:, None, None] * t + pos[None, :, None] - pos[None, None, :]
    mult = _multiplicity(delta)
    return jnp.where(mult > 0.0, jnp.log(jnp.maximum(mult, 1.0)), -1e30).astype(F32)


def _head_split(v, first):
    zero = jnp.zeros_like(v)
    return [jnp.where(first, v, zero), jnp.where(first, zero, v)]


def _flash_fwd(q, k, v, bias):
    t = ATT_T
    n_hp = ATT_W // LANES

    def body(q_ref, k_ref, v_ref, b_ref, o_ref, lse_ref):
        i = pl.program_id(1)
        first = lax.broadcasted_iota(jnp.int32, (1, LANES), 1) < HEAD_DIM
        qs = _head_split(q_ref[...], first)

        def kv_step(j, carry):
            m0, l0, m1, l1, acc = carry
            off = pl.multiple_of(j * t, t)
            kb = k_ref[pl.ds(off, t), :]
            vs = _head_split(v_ref[pl.ds(off, t), :], first)
            bias_t = b_ref[i - j]
            new = []
            pv = None
            for h, (m_prev, l_prev) in enumerate(((m0, l0), (m1, l1))):
                s = lax.dot_general(qs[h], kb, NT, preferred_element_type=F32) + bias_t
                m_new = jnp.maximum(m_prev, jnp.max(s, axis=1, keepdims=True))
                p = jnp.exp(s - m_new)
                alpha = jnp.exp(m_prev - m_new)
                l_new = alpha * l_prev + jnp.sum(p, axis=1, keepdims=True)
                d = lax.dot_general(p.astype(BF16), vs[h], NN, preferred_element_type=F32)
                pv = d if pv is None else pv + d
                new.append((m_new, l_new, alpha))
            acc = acc * jnp.where(first, new[0][2], new[1][2]) + pv
            return new[0][0], new[0][1], new[1][0], new[1][1], acc

        neg = jnp.full((t, 1), -1e30, F32)
        zero = jnp.zeros((t, 1), F32)
        m0, l0, m1, l1, acc = lax.fori_loop(0, i + 1, kv_step, (neg, zero, neg, zero, jnp.zeros((t, LANES), F32)))
        o_ref[...] = acc * jnp.where(first, 1.0 / l0, 1.0 / l1)
        lse_ref[...] = jnp.where(first, m0 + jnp.log(l0), m1 + jnp.log(l1))

    blk = pl.BlockSpec((t, LANES), lambda hp, i: (i, hp))
    full = pl.BlockSpec((S, LANES), lambda hp, i: (0, hp))
    return pl.pallas_call(
        body, name="flash_fwd", grid=(n_hp, S // t),
        in_specs=[blk, full, full, pl.BlockSpec((S // t, t, t), lambda hp, i: (0, 0, 0))], out_specs=[blk, blk],
        out_shape=[SDS((S, ATT_W), F32), SDS((S, ATT_W), F32)],
        compiler_params=_cparams(("parallel", "arbitrary")),
    )(q, k, v, bias)


def _flash_bwd(q, k, v, o, do, lse, bias):
    t = ATT_T
    n_hp = ATT_W // LANES
    n_t = S // t

    def body(q_ref, k_ref, v_ref, o_ref, do_ref, lse_ref, b_ref, dq_ref, dk_ref, dv_ref):
        j = pl.program_id(1)
        first = lax.broadcasted_iota(jnp.int32, (1, LANES), 1) < HEAD_DIM

        @pl.when(j == 0)
        def _():
            dq_ref[...] = jnp.zeros_like(dq_ref)

        kb = k_ref[...]
        vb = v_ref[...]
        ks = _head_split(kb, first)

        def q_step(i, carry):
            dk_acc, dv_acc = carry
            rows = pl.ds(pl.multiple_of(i * t, t), t)
            qs = _head_split(q_ref[rows, :], first)
            dob = do_ref[rows, :]
            prod = dob * o_ref[rows, :]
            d_all = jnp.sum(prod, axis=1, keepdims=True)
            d0 = jnp.sum(jnp.where(first, prod, 0.0), axis=1, keepdims=True)
            lse_b = lse_ref[rows, :]
            lse0 = jnp.max(jnp.where(first, lse_b, -jnp.inf), axis=1, keepdims=True)
            lse1 = jnp.max(jnp.where(first, -jnp.inf, lse_b), axis=1, keepdims=True)
            dos = _head_split(dob.astype(BF16), first)
            bias_t = b_ref[i - j]
            dq_t = jnp.zeros((t, LANES), F32)
            for h, (lse_h, d_h) in enumerate(((lse0, d0), (lse1, d_all - d0))):
                s = lax.dot_general(qs[h], kb, NT, preferred_element_type=F32)
                p = jnp.exp(s + (bias_t - lse_h))
                dp = lax.dot_general(dos[h], vb, NT, preferred_element_type=F32)
                ds = (p * (dp - d_h)).astype(BF16)
                dv_acc = dv_acc + lax.dot_general(p.astype(BF16), dos[h], TN, preferred_element_type=F32)
                dk_acc = dk_acc + lax.dot_general(ds, qs[h], TN, preferred_element_type=F32)
                dq_t = dq_t + lax.dot_general(ds, ks[h], NN, preferred_element_type=F32)
            dq_ref[rows, :] += dq_t
            return dk_acc, dv_acc

        zero = jnp.zeros((t, LANES), F32)
        dk_acc, dv_acc = lax.fori_loop(j, n_t, q_step, (zero, zero))
        dk_ref[...] = dk_acc
        dv_ref[...] = dv_acc

    blk = pl.BlockSpec((t, LANES), lambda hp, j: (j, hp))
    full = pl.BlockSpec((S, LANES), lambda hp, j: (0, hp))
    return pl.pallas_call(
        body, name="flash_bwd", grid=(n_hp, n_t),
        in_specs=[full, blk, blk, full, full, full, pl.BlockSpec((n_t, t, t), lambda hp, j: (0, 0, 0))],
        out_specs=[full, blk, blk],
        out_shape=[SDS((S, ATT_W), F32)] * 3,
        compiler_params=_cparams(("parallel", "arbitrary")),
    )(q, k, v, o, do, lse, bias)


SCAN_T = 256
ST_ROWS = 2 * N_CPLX // LANES
HALF = ST_ROWS // 2


def _scan_fwd(lam, bu):
    def body(lam_ref, bu_ref, st_ref, carry):
        @pl.when(pl.program_id(0) == 0)
        def _():
            carry[...] = jnp.zeros_like(carry)

        ar, ai = lam_ref[0:HALF, :], lam_ref[HALF:ST_ROWS, :]

        def step(t, c):
            sr, si = c
            b = bu_ref[t]
            nr = ar * sr - ai * si + b[0:HALF]
            ni = ar * si + ai * sr + b[HALF:ST_ROWS]
            st_ref[t, 0:HALF, :] = nr
            st_ref[t, HALF:ST_ROWS, :] = ni
            return nr, ni

        sr, si = lax.fori_loop(0, SCAN_T, step, (carry[0:HALF, :], carry[HALF:ST_ROWS, :]), unroll=8)
        carry[0:HALF, :] = sr
        carry[HALF:ST_ROWS, :] = si

    blk = pl.BlockSpec((SCAN_T, ST_ROWS, LANES), lambda i: (i, 0, 0))
    return pl.pallas_call(
        body, name="scan_fwd", grid=(S // SCAN_T,),
        in_specs=[pl.BlockSpec((ST_ROWS, LANES), lambda i: (0, 0)), blk], out_specs=blk,
        out_shape=SDS((S, ST_ROWS, LANES), F32),
        scratch_shapes=[pltpu.VMEM((ST_ROWS, LANES), F32)],
        compiler_params=_cparams(("arbitrary",)),
    )(lam, bu)


def _scan_bwd(lam, dst, states):
    n_blk = S // SCAN_T

    def body(lam_ref, d_ref, st_ref, g_ref, dlam_ref, carry):
        @pl.when(pl.program_id(0) == 0)
        def _():
            carry[...] = jnp.zeros_like(carry)
            dlam_ref[...] = jnp.zeros_like(dlam_ref)

        ar, ai = lam_ref[0:HALF, :], lam_ref[HALF:ST_ROWS, :]

        def step(kk, c):
            t = SCAN_T - 1 - kk
            gr, gi, dar, dai = c
            x = st_ref[t]
            xr, xi = x[0:HALF], x[HALF:ST_ROWS]
            dar = dar + gr * xr + gi * xi
            dai = dai + gi * xr - gr * xi
            d = d_ref[t]
            ngr = d[0:HALF] + ar * gr + ai * gi
            ngi = d[HALF:ST_ROWS] + ar * gi - ai * gr
            g_ref[t, 0:HALF, :] = ngr
            g_ref[t, HALF:ST_ROWS, :] = ngi
            return ngr, ngi, dar, dai

        zero = jnp.zeros((HALF, LANES), F32)
        gr, gi, dar, dai = lax.fori_loop(0, SCAN_T, step, (carry[0:HALF, :], carry[HALF:ST_ROWS, :], zero, zero),
                                         unroll=8)
        carry[0:HALF, :] = gr
        carry[HALF:ST_ROWS, :] = gi
        dlam_ref[0:HALF, :] += dar
        dlam_ref[HALF:ST_ROWS, :] += dai

    blk = pl.BlockSpec((SCAN_T, ST_ROWS, LANES), lambda i: (n_blk - 1 - i, 0, 0))
    small = pl.BlockSpec((ST_ROWS, LANES), lambda i: (0, 0))
    return pl.pallas_call(
        body, name="scan_bwd", grid=(n_blk,),
        in_specs=[small, blk, blk], out_specs=[blk, small],
        out_shape=[SDS((S, ST_ROWS, LANES), F32), SDS((ST_ROWS, LANES), F32)],
        scratch_shapes=[pltpu.VMEM((ST_ROWS, LANES), F32)],
        compiler_params=_cparams(("arbitrary",)),
    )(lam, dst, states)


def _ssm_prep(a_re, a_im, log_dt, b_re, b_im, c_re, c_im):
    lam = lax.complex(a_re, a_im)
    dt = jnp.exp(log_dt)[:, None]
    lam_bar = jnp.exp(lam * dt)
    b_bar = ((lam_bar - 1.0) / lam)[..., None] * lax.complex(b_re, b_im)
    lam_t = jnp.concatenate([jnp.real(lam_bar).reshape(HALF, LANES), jnp.imag(lam_bar).reshape(HALF, LANES)], axis=0)
    on_diag = (lax.broadcasted_iota(jnp.int32, (SSM_W, N_CPLX), 0) // SSM_GROUP
               == lax.broadcasted_iota(jnp.int32, (SSM_W, N_CPLX), 1) // SSM_STATE)

    def block_diag(m):
        return jnp.where(on_diag, jnp.tile(m.reshape(SSM_W, SSM_STATE), (1, SSM_GROUPS)), 0.0)

    w_b = jnp.concatenate([block_diag(jnp.real(b_bar).transpose(0, 2, 1)),
                           block_diag(jnp.imag(b_bar).transpose(0, 2, 1))], axis=1)
    w_ct = jnp.concatenate([block_diag(c_re), -block_diag(c_im)], axis=1)
    return lam_t, w_b, w_ct


def _row(v):
    return v.reshape(1, -1)


def _even_fwd(x, pre, post, w_in, w_out, glu_w, glu_b, ssm_d, prep, tables):
    lam_t, w_b, w_ct = prep
    h = _norm_fwd(x, pre)
    proj = _mm(h, w_in, "nn", F32, b_blocks=True)
    rope_t, bias = tables[:3], tables[3]
    q = _rope(proj, 0, rope_t, HEAD_DIM ** -0.5, False, BF16)
    k = _rope(proj, ATT_W // LANES, rope_t, 1.0, False, BF16)
    v = proj[:, 2 * ATT_W:3 * ATT_W].astype(BF16)
    att, lse = _flash_fwd(q, k, v, bias)
    u_ssm = proj[:, 4 * ATT_W:4 * ATT_W + SSM_W]
    bu = _mm(u_ssm, w_b, "nn", F32)
    states = _scan_fwd(lam_t, bu.reshape(S, ST_ROWS, LANES))
    y = _mm(states.reshape(S, 2 * N_CPLX), w_ct, "nt", F32)

    def act1(yv, uv, dv):
        return (_gelu_and_grad(yv + dv * uv)[0],), ()

    (z1,) = _rowwise(act1, [(y, SSM_W, 0), (proj, SSM_W, 8), (ssm_d, SSM_W, 0)], [(SSM_W, F32)], name="ssm_act_fwd")
    lin = _mm(z1, glu_w, "nn", F32)

    def gate(att_v, ga, gs, z1v, linv, bv):
        ssm_out = z1v * _sigmoid(linv + bv)
        return (jnp.concatenate([att_v * _silu_and_grad(ga)[0], ssm_out * _silu_and_grad(gs)[0]], axis=1),), ()

    (merged,) = _rowwise(gate, [(att, ATT_W, 0), (proj, ATT_W, 3), (proj, SSM_W, 9), (z1, SSM_W, 0),
                                (lin, SSM_W, 0), (glu_b, SSM_W, 0)], [(EVEN_OUT, BF16)], name="even_gate_fwd")
    yout = _mm(merged, w_out, "nn", F32)
    x_next = _post_fwd(x, yout, post)
    saved = (x, h, proj, q, k, v, att, lse, states, y, z1, lin, merged, yout)
    return x_next, saved


def _even_bwd(g, saved, pre, post, w_in, w_out, glu_w, glu_b, ssm_d, prep, tables):
    x, h, proj, q, k, v, att, lse, states, y, z1, lin, merged, yout = saved
    lam_t, w_b, w_ct = prep
    dyout, dpost = _post_bwd(g, yout, post)
    dmerged = _mm(dyout, w_out, "nt", F32)
    dw_out = _mm(merged, dyout, "tn", BF16)

    def gate_bwd(dm_a, dm_s, att_v, ga, gs, z1v, linv, bv):
        sa, dsa = _silu_and_grad(ga)
        ss, dss = _silu_and_grad(gs)
        sig = _sigmoid(linv + bv)
        ssm_out = z1v * sig
        dssm = dm_s * ss
        dlin = dssm * z1v * sig * (1.0 - sig)
        return (dm_a * sa, dm_a * att_v * dsa, dm_s * ssm_out * dss, dssm * sig, dlin), (dlin,)

    datt, dg_att, dg_ssm, dz1a, dlin, dglu_b = _rowwise(
        gate_bwd, [(dmerged, ATT_W, 0), (dmerged, SSM_W, 2), (att, ATT_W, 0), (proj, ATT_W, 3), (proj, SSM_W, 9),
                   (z1, SSM_W, 0), (lin, SSM_W, 0), (glu_b, SSM_W, 0)],
        [(ATT_W, F32), (ATT_W, BF16), (SSM_W, BF16), (SSM_W, F32), (SSM_W, BF16)], [SSM_W], name="even_gate_bwd")
    dz1b = _mm(dlin, glu_w, "nt", F32)
    dglu_w = _mm(z1, dlin, "tn", BF16)

    def act1_bwd(da, db, yv, uv, dv):
        dpre = (da + db) * _gelu_and_grad(yv + dv * uv)[1]
        return (dpre, dpre * dv), (dpre * uv,)

    dy, du_direct, dd = _rowwise(act1_bwd, [(dz1a, SSM_W, 0), (dz1b, SSM_W, 0), (y, SSM_W, 0), (proj, SSM_W, 8),
                                            (ssm_d, SSM_W, 0)], [(SSM_W, BF16), (SSM_W, F32)], [SSM_W],
                                 name="ssm_act_bwd")
    dst = _mm(dy, w_ct, "nn", F32)
    dw_ct = _mm(dy, states.reshape(S, 2 * N_CPLX), "tn", F32)
    gst, dlam = _scan_bwd(lam_t, dst.reshape(S, ST_ROWS, LANES), states)
    dbu = gst.reshape(S, 2 * N_CPLX)
    u_ssm = proj[:, 4 * ATT_W:4 * ATT_W + SSM_W]
    du_ssm = _mm(dbu, w_b, "nt", F32) + du_direct
    dw_b = _mm(u_ssm, dbu, "tn", F32)
    dq, dk, dv = _flash_bwd(q, k, v, att, datt, lse, tables[3])
    dq = _rope(dq, 0, tables[:3], HEAD_DIM ** -0.5, True, BF16)
    dk = _rope(dk, 0, tables[:3], 1.0, True, BF16)
    dproj = jnp.concatenate([dq, dk, dv.astype(BF16), dg_att, du_ssm.astype(BF16), dg_ssm], axis=1)
    dw_in = _mm(h, dproj, "tn", BF16, out_blocks=True)
    dh = _mm(dproj, w_in, "nt", F32, b_blocks=True)
    g_prev, dpre = _pre_bwd(g, dh, x, pre)
    return g_prev, dict(pre=dpre, post=dpost, w_in=dw_in, w_out=dw_out, glu_w=dglu_w, glu_b=dglu_b, ssm_d=dd,
                        prep=(dlam, dw_b, dw_ct))


def _odd_fwd(x, pre, post, w_in, pool_w, pool_scale, w_out):
    h = _norm_fwd(x, pre)
    proj = _mm(h, w_in, "nn", F32, b_blocks=True)
    mixed = _pool(proj, 0, False, BF16)
    ylin = _gmm(mixed, pool_w, "nn", F32)

    def gate(yl, gt, sc):
        return (yl * sc * _silu_and_grad(gt)[0],), ()

    (z,) = _rowwise(gate, [(ylin, POOL_W, 0), (proj, POOL_W, 1), (pool_scale, POOL_W, 0)], [(POOL_W, BF16)],
                    name="odd_gate_fwd")
    yout = _mm(z, w_out, "nn", F32)
    x_next = _post_fwd(x, yout, post)
    return x_next, (x, h, proj, mixed, ylin, z, yout)


def _odd_bwd(g, saved, pre, post, w_in, pool_w, pool_scale, w_out):
    x, h, proj, mixed, ylin, z, yout = saved
    dyout, dpost = _post_bwd(g, yout, post)
    dz = _mm(dyout, w_out, "nt", F32)
    dw_out = _mm(z, dyout, "tn", BF16)

    def gate_bwd(dzv, yl, gt, sc):
        sg, dsg = _silu_and_grad(gt)
        tt = dzv * sg
        return (tt * sc, dzv * yl * sc * dsg), (tt * yl,)

    dylin, dgate, dscale = _rowwise(gate_bwd, [(dz, POOL_W, 0), (ylin, POOL_W, 0), (proj, POOL_W, 1),
                                               (pool_scale, POOL_W, 0)], [(POOL_W, BF16), (POOL_W, BF16)], [POOL_W],
                                    name="odd_gate_bwd")
    dmixed = _gmm(dylin, pool_w, "nt", F32)
    dpool_w = _gmm(mixed, dylin, "tn", BF16)
    du = _pool(dmixed, 0, True, BF16)
    dproj = jnp.concatenate([du, dgate], axis=1)
    dw_in = _mm(h, dproj, "tn", BF16, out_blocks=True)
    dh = _mm(dproj, w_in, "nt", F32, b_blocks=True)
    g_prev, dpre = _pre_bwd(g, dh, x, pre)
    return g_prev, dict(pre=dpre, post=dpost, w_in=dw_in, w_out=dw_out, pool_w=dpool_w, pool_scale=dscale)


def _my_index():
    return 4 * lax.axis_index("x") + 2 * lax.axis_index("y") + lax.axis_index("c")


def _exchange(arrs, gather, name):
    n = len(arrs)
    out_shape = [SDS((N_DEV,) + a.shape, a.dtype) if gather else SDS(a.shape, a.dtype) for a in arrs]

    def body(*refs):
        ins, outs = refs[:n], refs[n:2 * n]
        send_sems, recv_sems, local_sems = refs[2 * n:]
        me = _my_index()

        def src(i, j):
            return ins[i] if gather else ins[i].at[j]

        def remote(i, j, src_slot, dst_slot, recv_slot):
            return pltpu.make_async_remote_copy(
                src_ref=src(i, src_slot), dst_ref=outs[i].at[dst_slot], send_sem=send_sems.at[i, j],
                recv_sem=recv_sems.at[i, recv_slot], device_id=(j // 4, (j // 2) % 2, j % 2), device_id_type=MESH_ID)

        def local(i):
            return pltpu.make_async_copy(src(i, me), outs[i].at[me], local_sems.at[i])

        for i in range(n):
            local(i).start()
        for j in range(N_DEV):
            @pl.when(me != j)
            def _(j=j):
                for i in range(n):
                    remote(i, j, j, me, me).start()
        for j in range(N_DEV):
            @pl.when(me != j)
            def _(j=j):
                for i in range(n):
                    remote(i, j, j, me, me).wait_send()
                    remote(i, j, j, j, j).wait_recv()
        for i in range(n):
            local(i).wait()

    any_spec = pl.BlockSpec(memory_space=pl.ANY)
    return pl.pallas_call(
        body, name=name, in_specs=[any_spec] * n, out_specs=[any_spec] * n, out_shape=out_shape,
        scratch_shapes=[pltpu.SemaphoreType.DMA((n, N_DEV)), pltpu.SemaphoreType.DMA((n, N_DEV)),
                        pltpu.SemaphoreType.DMA((n,))],
    )(*arrs)


HBM_SPEC = pl.BlockSpec(memory_space=pltpu.HBM)
SEM_SPEC = pl.BlockSpec(memory_space=pltpu.SEMAPHORE)
SPLIT_EFFECT = pltpu.SideEffectType.DATAFLOW_SIDE_EFFECTING


def _device_of(j):
    return (j // 4, (j // 2) % 2, j % 2)


def _split_copy(srcs, lands, send_sems, recv_sems, gather, i, j, dst_slot, recv_slot):
    return pltpu.make_async_remote_copy(
        src_ref=srcs[i] if gather else srcs[i].at[j], dst_ref=lands[i].at[dst_slot],
        send_sem=send_sems.at[i * N_DEV + j], recv_sem=recv_sems.at[i * N_DEV + recv_slot],
        device_id=_device_of(j), device_id_type=MESH_ID)


def _xchg_start(name, srcs, lands, gather):
    n = len(srcs)

    def body(*refs):
        src_refs, land_refs = refs[:n], refs[n:2 * n]
        send_sems, recv_sems, token = refs[2 * n], refs[2 * n + 1], refs[-1]
        me = _my_index()
        for j in range(N_DEV):
            @pl.when(me != j)
            def _(j=j):
                for i in range(n):
                    _split_copy(src_refs, land_refs, send_sems, recv_sems, gather, i, j, me, me).start()
        token[...] = jnp.zeros_like(token)

    thru = [pltpu.HBM(a.shape, a.dtype) for a in list(srcs) + list(lands)]
    res = pl.pallas_call(
        body, name=name,
        out_shape=(pltpu.SemaphoreType.DMA((n * N_DEV,)), pltpu.SemaphoreType.DMA((n * N_DEV,)), *thru,
                   SDS((8, LANES), F32)),
        in_specs=[HBM_SPEC] * (2 * n),
        out_specs=(SEM_SPEC, SEM_SPEC, *([HBM_SPEC] * (2 * n)), pl.BlockSpec(memory_space=pltpu.VMEM)),
        input_output_aliases={i: 2 + i for i in range(2 * n)},
        compiler_params=pltpu.CompilerParams(has_side_effects=SPLIT_EFFECT),
    )(*[pltpu.with_memory_space_constraint(a, pltpu.HBM) for a in list(srcs) + list(lands)])
    return res[0], res[1], list(res[2:2 + n]), list(res[2 + n:2 + 2 * n]), res[-1]


def _xchg_wait(name, started, gather, after):
    send_sems, recv_sems, srcs, lands, _ = started
    n = len(srcs)

    def body(*refs):
        src_refs, land_refs = refs[:n], refs[n:2 * n]
        send_r, recv_r = refs[2 * n], refs[2 * n + 1]
        me = _my_index()
        for j in range(N_DEV):
            @pl.when(me != j)
            def _(j=j):
                for i in range(n):
                    _split_copy(src_refs, land_refs, send_r, recv_r, gather, i, j, me, me).wait_send()
                    _split_copy(src_refs, land_refs, send_r, recv_r, gather, i, j, j, j).wait_recv()

    thru = [pltpu.HBM(a.shape, a.dtype) for a in list(srcs) + list(lands)]
    res = pl.pallas_call(
        body, name=name, out_shape=tuple(thru),
        in_specs=[HBM_SPEC] * (2 * n) + [SEM_SPEC, SEM_SPEC, pl.BlockSpec(memory_space=pl.ANY)],
        out_specs=tuple([HBM_SPEC] * (2 * n)),
        input_output_aliases={i: i for i in range(2 * n)},
        compiler_params=pltpu.CompilerParams(has_side_effects=SPLIT_EFFECT),
    )(*srcs, *lands, send_sems, recv_sems, after)
    return list(res[n:])


def _own_slot_landing(own):
    zone = lax.empty((N_DEV,) + own.shape, own.dtype)
    return lax.dynamic_update_slice(zone, own[None], (_my_index(),) + (0,) * own.ndim)


def _adam_layers(w, slot_list, m, v, name):
    n_l, r, c = w.shape
    ns = slot_list[0].shape[0]
    tr = r
    while tr * c * 4 > (1 << 20) and tr % 16 == 0:
        tr //= 2
    assert r % tr == 0 and len(slot_list) == n_l

    def body(*refs):
        w_ref, slot_refs = refs[0], refs[1:1 + n_l]
        m_ref, v_ref, go_ref, d_ref, mo_ref, vo_ref = refs[1 + n_l:]
        layer = pl.program_id(0)
        g = None
        for l, g_ref in enumerate(slot_refs):
            gl = g_ref[0].astype(F32)
            for s in range(1, ns):
                gl = gl + g_ref[s].astype(F32)
            g = gl if g is None else jnp.where(layer == l, gl, g)
        mn = ADAM_B1 * m_ref[...] + (1.0 - ADAM_B1) * g
        vn = ADAM_B2 * v_ref[...] + (1.0 - ADAM_B2) * (g * g)
        m_hat = mn / (1.0 - ADAM_B1 ** ADAM_STEP)
        v_hat = vn / (1.0 - ADAM_B2 ** ADAM_STEP)
        go_ref[...] = g
        d_ref[...] = -ADAM_LR * (m_hat / (jnp.sqrt(v_hat) + ADAM_EPS) + ADAM_WD * w_ref[...])
        mo_ref[...] = mn
        vo_ref[...] = vn

    blk = pl.BlockSpec((None, tr, c), lambda l, i: (l, i, 0))
    slot_specs = [pl.BlockSpec((ns, tr, c), lambda l, i, k=k: (0, jnp.where(l == k, i, 0), 0)) for k in range(n_l)]
    return pl.pallas_call(
        body, name=name, grid=(n_l, r // tr),
        in_specs=[blk] + slot_specs + [blk, blk],
        out_specs=[blk] * 4, out_shape=[SDS((n_l, r, c), F32)] * 4,
        compiler_params=_cparams(("arbitrary", "arbitrary")),
    )(w, *slot_list, m, v)


def _adam(w, gslots, m, v, name):
    r, c = w.shape
    ns = gslots.shape[0]
    tr = r
    while tr * c * 4 > (1 << 20) and tr % 16 == 0:
        tr //= 2
    assert r % tr == 0

    def body(w_ref, g_ref, m_ref, v_ref, go_ref, d_ref, mo_ref, vo_ref):
        g = g_ref[0].astype(F32)
        for s in range(1, ns):
            g = g + g_ref[s].astype(F32)
        wv = w_ref[...]
        mn = ADAM_B1 * m_ref[...] + (1.0 - ADAM_B1) * g
        vn = ADAM_B2 * v_ref[...] + (1.0 - ADAM_B2) * (g * g)
        m_hat = mn / (1.0 - ADAM_B1 ** ADAM_STEP)
        v_hat = vn / (1.0 - ADAM_B2 ** ADAM_STEP)
        go_ref[...] = g
        d_ref[...] = -ADAM_LR * (m_hat / (jnp.sqrt(v_hat) + ADAM_EPS) + ADAM_WD * wv)
        mo_ref[...] = mn
        vo_ref[...] = vn

    blk = pl.BlockSpec((tr, c), lambda i: (i, 0))
    return pl.pallas_call(
        body, name=name, grid=(r // tr,),
        in_specs=[blk, pl.BlockSpec((ns, tr, c), lambda i: (0, i, 0)), blk, blk],
        out_specs=[blk] * 4, out_shape=[SDS((r, c), F32)] * 4,
        compiler_params=_cparams(("parallel",)),
    )(w, gslots, m, v)


SMALL_NAMES = ("pre_norm", "post_norm", "ssm_a_re", "ssm_a_im", "ssm_log_dt", "ssm_b_re", "ssm_b_im", "ssm_c_re",
               "ssm_c_im", "ssm_d", "ssm_glu_b")
SHARDED_NAMES = ("even_w_in", "even_w_out", "ssm_glu_w", "odd_w_in", "pool_w", "odd_w_out")
WEIGHT_ORDER = ("pre_norm", "post_norm", "even_w_in", "even_w_out", "ssm_a_re", "ssm_a_im", "ssm_log_dt", "ssm_b_re",
                "ssm_b_im", "ssm_c_re", "ssm_c_im", "ssm_d", "ssm_glu_w", "ssm_glu_b", "odd_w_in", "pool_w",
                "pool_scale", "odd_w_out")
PACK_ROWS_ALIGN = 8


def _pack(parts):
    flat = jnp.concatenate([p.reshape(-1).astype(F32) for p in parts])
    rows = -(-flat.shape[0] // (LANES * PACK_ROWS_ALIGN)) * PACK_ROWS_ALIGN
    return jnp.pad(flat, (0, rows * LANES - flat.shape[0])).reshape(rows, LANES)


def _unpack(packed, shapes):
    flat = packed.reshape(-1)
    out, off = [], 0
    for shp in shapes:
        size = math.prod(shp)
        out.append(flat[off:off + size].reshape(shp))
        off += size
    return out


EVEN_SHARDED = ("w_in", "w_out", "glu_w")
ODD_SHARDED = ("w_in", "pool_w", "w_out")
FAMILY = {(0, "w_in"): "even_w_in", (0, "w_out"): "even_w_out", (0, "glu_w"): "ssm_glu_w",
          (1, "w_in"): "odd_w_in", (1, "pool_w"): "pool_w", (1, "w_out"): "odd_w_out"}


def _sharded_keys(layer):
    return EVEN_SHARDED if layer % 2 == 0 else ODD_SHARDED


def _local_step(x, tgt, small, get_weights, on_grads):
    tables = _rope_tables() + (_attention_bias(),)
    preps, prep_vjps = [], []
    for i in range(2):
        out, vjp = jax.vjp(_ssm_prep, small["ssm_a_re"][i], small["ssm_a_im"][i], small["ssm_log_dt"][i],
                           small["ssm_b_re"][i], small["ssm_b_im"][i], small["ssm_c_re"][i], small["ssm_c_im"][i])
        preps.append(out)
        prep_vjps.append(vjp)

    def layer_args(layer, wts):
        i = layer // 2
        pre, post = _row(small["pre_norm"][layer]), _row(small["post_norm"][layer])
        if layer % 2 == 0:
            return (pre, post, wts["w_in"], wts["w_out"], wts["glu_w"], _row(small["ssm_glu_b"][i]),
                    _row(small["ssm_d"][i]), preps[i], tables)
        return (pre, post, wts["w_in"], wts["pool_w"], _row(wts["pool_scale"]), wts["w_out"])

    saved, args = [], []
    cur = x
    for layer in range(4):
        args.append(layer_args(layer, get_weights(layer, cur)))
        cur, sv = (_even_fwd if layer % 2 == 0 else _odd_fwd)(cur, *args[layer])
        saved.append(sv)
    g, sq = _loss_grad(cur, tgt)
    loss = 0.5 * jnp.sum(sq) / D

    lg = [None] * 4
    token = jnp.zeros((), F32)
    for layer in reversed(range(4)):
        largs = list(args[layer])
        largs[1] = largs[1] + token
        g, lg[layer] = (_even_bwd if layer % 2 == 0 else _odd_bwd)(g, saved[layer], *largs)
        token = on_grads(layer, {k: lg[layer][k] for k in _sharded_keys(layer)})

    ssm_g = [prep_vjps[i](lg[2 * i]["prep"]) for i in range(2)]
    grads = {
        "pre_norm": jnp.concatenate([lg[l]["pre"] for l in range(4)], axis=0),
        "post_norm": jnp.concatenate([lg[l]["post"] for l in range(4)], axis=0),
        "ssm_d": jnp.concatenate([lg[0]["ssm_d"], lg[2]["ssm_d"]], axis=0),
        "ssm_glu_b": jnp.concatenate([lg[0]["glu_b"], lg[2]["glu_b"]], axis=0),
        "pool_scale": jnp.concatenate([lg[1]["pool_scale"], lg[3]["pool_scale"]], axis=0),
    }
    for idx, nm in enumerate(("ssm_a_re", "ssm_a_im", "ssm_log_dt", "ssm_b_re", "ssm_b_im", "ssm_c_re", "ssm_c_im")):
        grads[nm] = jnp.stack([ssm_g[0][idx], ssm_g[1][idx]])
    return loss, g, grads, token


def _to_slots(key, gfull):
    if key == "w_in":
        return gfull
    if key in ("w_out", "glu_w"):
        rr, nn = gfull.shape
        return gfull.reshape(N_DEV, rr // N_DEV, nn)
    assert key == "pool_w"
    gg, rr, nn = gfull.shape
    return gfull.reshape(gg, N_DEV, rr // N_DEV, nn).transpose(1, 0, 2, 3)


def _from_gathered(key, gat):
    if key == "w_in":
        return gat
    if key in ("w_out", "glu_w"):
        _, rr, nn = gat.shape
        return gat.reshape(N_DEV * rr, nn)
    assert key == "pool_w"
    _, gg, rr, nn = gat.shape
    return gat.transpose(1, 0, 2, 3).reshape(gg, N_DEV * rr, nn)


def kernel(x, pre_norm, post_norm, even_w_in, even_w_out, ssm_a_re, ssm_a_im, ssm_log_dt, ssm_b_re, ssm_b_im, ssm_c_re, ssm_c_im, ssm_d, ssm_glu_w, ssm_glu_b, odd_w_in, pool_w, pool_scale, odd_w_out, loss_target, m_pre_norm, m_post_norm, m_even_w_in, m_even_w_out, m_ssm_a_re, m_ssm_a_im, m_ssm_log_dt, m_ssm_b_re, m_ssm_b_im, m_ssm_c_re, m_ssm_c_im, m_ssm_d, m_ssm_glu_w, m_ssm_glu_b, m_odd_w_in, m_pool_w, m_pool_scale, m_odd_w_out, v_pre_norm, v_post_norm, v_even_w_in, v_even_w_out, v_ssm_a_re, v_ssm_a_im, v_ssm_log_dt, v_ssm_b_re, v_ssm_b_im, v_ssm_c_re, v_ssm_c_im, v_ssm_d, v_ssm_glu_w, v_ssm_glu_b, v_odd_w_in, v_pool_w, v_pool_scale, v_odd_w_out):
    w = dict(pre_norm=pre_norm, post_norm=post_norm, even_w_in=even_w_in, even_w_out=even_w_out, ssm_a_re=ssm_a_re,
             ssm_a_im=ssm_a_im, ssm_log_dt=ssm_log_dt, ssm_b_re=ssm_b_re, ssm_b_im=ssm_b_im, ssm_c_re=ssm_c_re,
             ssm_c_im=ssm_c_im, ssm_d=ssm_d, ssm_glu_w=ssm_glu_w, ssm_glu_b=ssm_glu_b, odd_w_in=odd_w_in,
             pool_w=pool_w, pool_scale=pool_scale, odd_w_out=odd_w_out)
    mom = dict(pre_norm=m_pre_norm, post_norm=m_post_norm, even_w_in=m_even_w_in, even_w_out=m_even_w_out,
               ssm_a_re=m_ssm_a_re, ssm_a_im=m_ssm_a_im, ssm_log_dt=m_ssm_log_dt, ssm_b_re=m_ssm_b_re,
               ssm_b_im=m_ssm_b_im, ssm_c_re=m_ssm_c_re, ssm_c_im=m_ssm_c_im, ssm_d=m_ssm_d, ssm_glu_w=m_ssm_glu_w,
               ssm_glu_b=m_ssm_glu_b, odd_w_in=m_odd_w_in, pool_w=m_pool_w, pool_scale=m_pool_scale,
               odd_w_out=m_odd_w_out)
    var = dict(pre_norm=v_pre_norm, post_norm=v_post_norm, even_w_in=v_even_w_in, even_w_out=v_even_w_out,
               ssm_a_re=v_ssm_a_re, ssm_a_im=v_ssm_a_im, ssm_log_dt=v_ssm_log_dt, ssm_b_re=v_ssm_b_re,
               ssm_b_im=v_ssm_b_im, ssm_c_re=v_ssm_c_re, ssm_c_im=v_ssm_c_im, ssm_d=v_ssm_d, ssm_glu_w=v_ssm_glu_w,
               ssm_glu_b=v_ssm_glu_b, odd_w_in=v_odd_w_in, pool_w=v_pool_w, pool_scale=v_pool_scale,
               odd_w_out=v_odd_w_out)
    me = _my_index()
    scale_cols = pool_scale.shape[1]

    gather_started = []
    for layer in range(4):
        i = layer // 2
        shards = [w[FAMILY[(layer % 2, k)]][i].astype(BF16) for k in _sharded_keys(layer)]
        if layer % 2 == 1:
            shards.append(jnp.pad(pool_scale[i][None], ((0, PACK_ROWS_ALIGN - 1), (0, 0))))
        gather_started.append(_xchg_start(f"gather_start_{layer}", shards, [_own_slot_landing(s) for s in shards], True))
    token = sum(st[4][0, 0] for st in gather_started)
    small = {nm: w[nm] for nm in SMALL_NAMES}
    small["pre_norm"] = pre_norm + token

    def get_weights(layer, after):
        lands = _xchg_wait(f"gather_wait_{layer}", gather_started[layer], True, after)
        wts = {k: _from_gathered(k, gat) for k, gat in zip(_sharded_keys(layer), lands)}
        if layer % 2 == 1:
            wts["pool_scale"] = lands[-1][:, 0, :].reshape(N_DEV * scale_cols)
        return wts

    scatter_started = [None] * 4

    def on_grads(layer, lg):
        slots = [_to_slots(k, lg[k]) for k in _sharded_keys(layer)]
        lands = [_own_slot_landing(lax.dynamic_index_in_dim(s, me, 0, keepdims=False)) for s in slots]
        scatter_started[layer] = _xchg_start(f"scatter_start_{layer}", slots, lands, False)
        return scatter_started[layer][4][0, 0]

    loss_local, grad_x, grads, token = _local_step(x[0], loss_target[0], small, get_weights, on_grads)

    extra_parts = [grads["pool_scale"], loss_local.reshape(1)]
    extra_zeros = [jnp.zeros(p.shape, F32) for p in extra_parts]
    small_parts = [grads[nm] for nm in SMALL_NAMES] + extra_parts
    small_shapes = [w[nm].shape for nm in SMALL_NAMES] + [p.shape for p in extra_parts]
    (small_slots,) = _exchange([_pack(small_parts) + token], True, "gather_small_grads")

    recv = {}
    for layer in (3, 2, 1, 0):
        lands = _xchg_wait(f"scatter_wait_{layer}", scatter_started[layer], False, small_slots)
        for k, land in zip(_sharded_keys(layer), lands):
            recv[(layer, k)] = land
    res = {}
    for (parity, k), nm in FAMILY.items():
        shp = w[nm].shape
        cols = shp[-1]
        slot_list = [recv[(parity + 2 * i, k)].reshape(N_DEV, -1, cols) for i in range(2)]
        outs = _adam_layers(w[nm].reshape(2, -1, cols), slot_list, mom[nm].reshape(2, -1, cols),
                            var[nm].reshape(2, -1, cols), name=f"adam_{nm}")
        res[nm] = [o.reshape(shp) for o in outs]
    w_pack = _pack([w[nm] for nm in SMALL_NAMES] + extra_zeros)
    m_pack = _pack([mom[nm] for nm in SMALL_NAMES] + extra_zeros)
    v_pack = _pack([var[nm] for nm in SMALL_NAMES] + extra_zeros)
    outs = _adam(w_pack, small_slots, m_pack, v_pack, name="adam_small")
    unpacked = [_unpack(o, small_shapes) for o in outs]
    for idx, nm in enumerate(SMALL_NAMES):
        res[nm] = [unpacked[kind][idx] for kind in range(4)]
    loss = unpacked[0][-1].reshape(())
    g_scale = lax.dynamic_slice_in_dim(unpacked[0][-2], me * scale_cols, scale_cols, axis=1)
    pad = ((0, PACK_ROWS_ALIGN - 2), (0, 0))
    outs = _adam(jnp.pad(pool_scale, pad), jnp.pad(g_scale, pad)[None], jnp.pad(m_pool_scale, pad),
                 jnp.pad(v_pool_scale, pad), name="adam_pool_scale")
    res["pool_scale"] = [o[:2] for o in outs]

    out = [loss, grad_x[None]]
    for kind in range(4):
        out += [res[nm][kind] for nm in WEIGHT_ORDER]
    return tuple(out)
```

```python
import functools
import math

import jax
import jax.numpy as jnp
from jax import lax
from jax.experimental import pallas as pl
from jax.experimental.pallas import tpu as pltpu

F32 = jnp.float32
BF16 = jnp.bfloat16
SDS = jax.ShapeDtypeStruct

N_DEV = 8
S = 2048
D = 1024
HEAD_DIM = 64
ROT_DIM = 16
ROPE_THETA = 500000.0
ATT_W = 1024
SSM_W = 512
SSM_GROUPS = 32
SSM_GROUP = 16
SSM_STATE = 64
N_CPLX = SSM_GROUPS * SSM_STATE
POOL_W = 2048
POOL_GROUP = 512
EVEN_IN = 5120
EVEN_OUT = 1536
ODD_IN = 4096
RMS_EPS = 1e-6
LANES = 128
VMEM_LIMIT = 48 * 1024 * 1024

ADAM_LR = 0.001
ADAM_B1 = 0.9
ADAM_B2 = 0.999
ADAM_EPS = 1e-08
ADAM_WD = 0.01
ADAM_STEP = 10

MESH_ID = pl.DeviceIdType.MESH
NN = (((1,), (0,)), ((), ()))
NT = (((1,), (1,)), ((), ()))
TN = (((0,), (0,)), ((), ()))
_DN = {"nn": NN, "nt": NT, "tn": TN}


def _cparams(sem):
    return pltpu.CompilerParams(dimension_semantics=sem, vmem_limit_bytes=VMEM_LIMIT)


MM_TILES = (1024, 768, 512)


def _tile(dim):
    return next((t for t in MM_TILES if dim % t == 0), dim)


NT_BLOCKS_PER_STEP = 4


def _mm(a, b, mode, out_dtype, b_blocks=False, out_blocks=False, a_cols=None):
    if b_blocks:
        nblk, rows, cb = b.shape
        b2_shape = (rows, nblk * cb)
    else:
        b2_shape = b.shape
    a_shape = a.shape if a_cols is None else (a.shape[0], a_cols[1])
    if mode == "nn":
        (m, k), n = a_shape, b2_shape[1]
    elif mode == "nt":
        (m, k), n = a_shape, b2_shape[0]
    else:
        (k, m), n = a_shape, b2_shape[1]
    tm, tn, tk = _tile(m), _tile(n), _tile(k)
    per_step = 1
    if b_blocks and mode == "nn":
        tn = cb
    if b_blocks and mode == "nt":
        per_step = NT_BLOCKS_PER_STEP
        tk = per_step * cb
        tn = min(tn, MM_TILES[-1])
    if out_blocks:
        tn = n // N_DEV
    nk = k // tk
    a_unit = tm if mode == "tn" else tk
    assert a_cols is None or a_cols[0] % a_unit == 0
    a_off = 0 if a_cols is None else a_cols[0] // a_unit

    def body(a_ref, b_ref, o_ref, acc_ref):
        kk = pl.program_id(2)
        if per_step == 1:
            part = lax.dot_general(a_ref[...].astype(BF16), b_ref[...].astype(BF16), _DN[mode],
                                   preferred_element_type=F32)
        else:
            part = None
            for blk in range(per_step):
                d = lax.dot_general(a_ref[:, blk * cb:(blk + 1) * cb].astype(BF16), b_ref[blk].astype(BF16), NT,
                                    preferred_element_type=F32)
                part = d if part is None else part + d
        if nk == 1:
            o_ref[...] = part.astype(o_ref.dtype)
            return

        @pl.when(kk == 0)
        def _():
            acc_ref[...] = part

        @pl.when((kk > 0) & (kk < nk - 1))
        def _():
            acc_ref[...] += part

        @pl.when(kk == nk - 1)
        def _():
            o_ref[...] = (acc_ref[...] + part).astype(o_ref.dtype)

    if mode == "nn":
        a_spec = pl.BlockSpec((tm, tk), lambda i, j, kk: (i, a_off + kk))
        b_spec = pl.BlockSpec((tk, tn), lambda i, j, kk: (kk, j))
    elif mode == "nt":
        a_spec = pl.BlockSpec((tm, tk), lambda i, j, kk: (i, a_off + kk))
        b_spec = pl.BlockSpec((tn, tk), lambda i, j, kk: (j, kk))
    else:
        a_spec = pl.BlockSpec((tk, tm), lambda i, j, kk: (kk, a_off + i))
        b_spec = pl.BlockSpec((tk, tn), lambda i, j, kk: (kk, j))
    if b_blocks and mode == "nn":
        b_spec = pl.BlockSpec((None, tk, cb), lambda i, j, kk: (j, kk, 0))
    if b_blocks and mode == "nt":
        b_spec = pl.BlockSpec((per_step, tn, cb), lambda i, j, kk: (kk, j, 0))
    out_spec = pl.BlockSpec((tm, tn), lambda i, j, kk: (i, j))
    out_shape = SDS((m, n), out_dtype)
    if out_blocks:
        out_spec = pl.BlockSpec((None, tm, tn), lambda i, j, kk: (j, i, 0))
        out_shape = SDS((N_DEV, m, tn), out_dtype)
    return pl.pallas_call(
        body, name=f"mm_{mode}_{m}x{k}x{n}",
        grid=(m // tm, n // tn, nk),
        in_specs=[a_spec, b_spec],
        out_specs=out_spec,
        out_shape=out_shape,
        scratch_shapes=[pltpu.VMEM((tm, tn) if nk > 1 else (8, LANES), F32)],
        compiler_params=_cparams(("parallel", "parallel", "arbitrary")),
    )(a, b)


def _gmm(a, b, mode, out_dtype, tm=512):
    ng, gw = POOL_W // POOL_GROUP, POOL_GROUP
    ns = S // tm
    if mode in ("nn", "nt"):
        def body(a_ref, b_ref, o_ref):
            o_ref[...] = lax.dot_general(a_ref[...].astype(BF16), b_ref[...].astype(BF16), _DN[mode],
                                         preferred_element_type=F32).astype(o_ref.dtype)

        return pl.pallas_call(
            body, name=f"gmm_{mode}", grid=(ng, ns),
            in_specs=[pl.BlockSpec((tm, gw), lambda g, i: (i, g)),
                      pl.BlockSpec((None, gw, gw), lambda g, i: (g, 0, 0))],
            out_specs=pl.BlockSpec((tm, gw), lambda g, i: (i, g)),
            out_shape=SDS((S, POOL_W), out_dtype),
            compiler_params=_cparams(("parallel", "parallel")),
        )(a, b)

    def body_tn(a_ref, b_ref, o_ref, acc_ref):
        i = pl.program_id(1)

        @pl.when(i == 0)
        def _():
            acc_ref[...] = jnp.zeros_like(acc_ref)

        acc_ref[...] += lax.dot_general(a_ref[...].astype(BF16), b_ref[...].astype(BF16), TN,
                                        preferred_element_type=F32)

        @pl.when(i == ns - 1)
        def _():
            o_ref[...] = acc_ref[...].astype(o_ref.dtype)

    return pl.pallas_call(
        body_tn, name="gmm_tn", grid=(ng, ns),
        in_specs=[pl.BlockSpec((tm, gw), lambda g, i: (i, g)),
                  pl.BlockSpec((tm, gw), lambda g, i: (i, g))],
        out_specs=pl.BlockSpec((None, gw, gw), lambda g, i: (g, 0, 0)),
        out_shape=SDS((ng, gw, gw), out_dtype),
        scratch_shapes=[pltpu.VMEM((gw, gw), F32)],
        compiler_params=_cparams(("parallel", "arbitrary")),
    )(a, b)


def _rowwise(fn, inputs, out_defs, acc_defs=(), tm=256, name=None):
    n_in, n_out, n_acc = len(inputs), len(out_defs), len(acc_defs)
    in_specs, args = [], []
    for arr, width, cb in inputs:
        if arr.shape[0] == 1:
            in_specs.append(pl.BlockSpec((1, width), lambda i, cb=cb: (0, cb)))
        else:
            in_specs.append(pl.BlockSpec((tm, width), lambda i, cb=cb: (i, cb)))
        args.append(arr)
    out_defs = [d if len(d) == 4 else (d[0], d[1], d[0], 0) for d in out_defs]
    out_shape = [SDS((S, ww), dt) for _, dt, ww, _ in out_defs] + [SDS((1, w), F32) for w in acc_defs]
    out_specs = ([pl.BlockSpec((tm, w), lambda i, cb=cb: (i, cb)) for w, _, _, cb in out_defs]
                 + [pl.BlockSpec((1, w), lambda i: (0, 0)) for w in acc_defs])

    def kern(*refs):
        vals = [r[...] for r in refs[:n_in]]
        outs, accs = fn(*vals)
        for r, v in zip(refs[n_in:n_in + n_out], outs):
            r[...] = v.astype(r.dtype)
        if n_acc:
            acc_refs = refs[n_in + n_out:]

            @pl.when(pl.program_id(0) == 0)
            def _():
                for r in acc_refs:
                    r[...] = jnp.zeros_like(r)

            for r, v in zip(acc_refs, accs):
                r[...] += jnp.sum(v, axis=0, keepdims=True)

    res = pl.pallas_call(
        kern, name=name, grid=(S // tm,), in_specs=in_specs, out_specs=out_specs, out_shape=out_shape,
        compiler_params=_cparams(("arbitrary",)),
    )(*args)
    return res


def _sigmoid(x):
    return 1.0 / (1.0 + jnp.exp(-x))


def _silu_and_grad(x):
    s = _sigmoid(x)
    return x * s, s * (1.0 + x * (1.0 - s))


_GELU_K = math.sqrt(2.0 / math.pi)
_GELU_C = 0.044715


def _gelu_and_grad(x):
    t = jnp.tanh(_GELU_K * (x + _GELU_C * (x * x * x)))
    cdf = 0.5 * (1.0 + t)
    grad = cdf + 0.5 * x * (1.0 - t * t) * (_GELU_K * (1.0 + 3.0 * _GELU_C * x * x))
    return x * cdf, grad


def _rms(xv, gain):
    r = lax.rsqrt(jnp.mean(xv * xv, axis=-1, keepdims=True) + RMS_EPS)
    return xv * r * gain


def _rms_bwd(dout, xv, gain):
    r = lax.rsqrt(jnp.mean(xv * xv, axis=-1, keepdims=True) + RMS_EPS)
    xhat = xv * r
    dxhat = dout * gain
    dx = r * (dxhat - xhat * jnp.mean(dxhat * xhat, axis=-1, keepdims=True))
    return dx, dout * xhat


def _norm_fwd(x, gain):
    (h,) = _rowwise(lambda xv, g: ((_rms(xv, g),), ()), [(x, D, 0), (gain, D, 0)], [(D, BF16)], name="norm_fwd")
    return h


def _post_fwd(x, y, gain):
    (o,) = _rowwise(lambda xv, yv, g: ((xv + _rms(yv, g),), ()), [(x, D, 0), (y, D, 0), (gain, D, 0)],
                    [(D, F32)], name="post_fwd")
    return o


def _post_bwd(g, y, gain):
    def fn(gv, yv, gn):
        dx, dg = _rms_bwd(gv, yv, gn)
        return (dx,), (dg,)

    return _rowwise(fn, [(g, D, 0), (y, D, 0), (gain, D, 0)], [(D, BF16)], [D], name="post_bwd")


def _pre_bwd(g, dh, x, gain):
    def fn(gv, dhv, xv, gn):
        dx, dg = _rms_bwd(dhv, xv, gn)
        return (gv + dx,), (dg,)

    return _rowwise(fn, [(g, D, 0), (dh, D, 0), (x, D, 0), (gain, D, 0)], [(D, F32)], [D], name="pre_bwd")


def _loss_grad(xo, tgt):
    def fn(xv, tv):
        e = xv - tv
        return (e * (1.0 / D),), (e * e,)

    return _rowwise(fn, [(xo, D, 0), (tgt, D, 0)], [(D, F32)], [D], name="loss_grad")


def _pool(u_arr, col_block, transpose, out_dtype, into=None, tc=256):
    n_t = POOL_W // tc
    per_group = POOL_GROUP // tc

    def body(u_ref, *rest):
        o_ref = rest[-1]
        c = pl.program_id(0)
        grp = c // per_group
        xv = u_ref[...]
        t = lax.broadcasted_iota(jnp.int32, (S, 1), 0)
        win = jnp.left_shift(2, grp)
        cnt = jnp.minimum(t + 1, win).astype(F32)
        cur = xv / cnt if transpose else xv
        sums = []
        for k in (1, 2, 4, 8):
            if transpose:
                sh = jnp.where(t < S - k, pltpu.roll(cur, S - k, 0), 0.0)
            else:
                sh = jnp.where(t >= k, pltpu.roll(cur, k, 0), 0.0)
            cur = cur + sh
            sums.append(cur)
        tot = jnp.where(grp == 0, sums[0], jnp.where(grp == 1, sums[1], jnp.where(grp == 2, sums[2], sums[3])))
        res = tot - xv if transpose else tot / cnt - xv
        o_ref[...] = res.astype(o_ref.dtype)

    in_specs = [pl.BlockSpec((S, tc), lambda c: (0, col_block * n_t + c))]
    args = [u_arr]
    if into is not None:
        in_specs.append(pl.BlockSpec(memory_space=pl.ANY))
        args.append(into)
    return pl.pallas_call(
        body, name="pool_bwd" if transpose else "pool_fwd", grid=(n_t,),
        in_specs=in_specs,
        out_specs=pl.BlockSpec((S, tc), lambda c: (0, c)),
        out_shape=SDS((S, POOL_W) if into is None else into.shape, out_dtype),
        input_output_aliases={} if into is None else {1: 0},
        compiler_params=_cparams(("parallel",)),
    )(*args)


def _rope_tables():
    pos = jnp.arange(S, dtype=jnp.int32).astype(F32)
    inv_freq = ROPE_THETA ** (-jnp.arange(0, ROT_DIM, 2, dtype=F32) / ROT_DIM)
    ang = pos[:, None] * inv_freq[None, :]
    cos8, sin8 = jnp.cos(ang), jnp.sin(ang)
    half = ROT_DIM // 2
    zeros = jnp.zeros((S, HEAD_DIM - ROT_DIM), F32)
    cos = jnp.concatenate([cos8, cos8, jnp.ones((S, HEAD_DIM - ROT_DIM), F32)], axis=1)
    lo = jnp.concatenate([-sin8, jnp.zeros((S, half), F32), zeros], axis=1)
    hi = jnp.concatenate([jnp.zeros((S, half), F32), sin8, zeros], axis=1)
    rep = LANES // HEAD_DIM
    return jnp.tile(cos, (1, rep)), jnp.tile(lo, (1, rep)), jnp.tile(hi, (1, rep))


def _rotate(xv, cos, lo, hi, transpose):
    width = xv.shape[1]
    rep = width // LANES
    wide = lambda tab: jnp.concatenate([tab] * rep, axis=1)
    half = ROT_DIM // 2
    up = pltpu.roll(xv, width - half, 1)
    dn = pltpu.roll(xv, half, 1)
    mixed = up * wide(lo) + dn * wide(hi)
    return xv * wide(cos) - mixed if transpose else xv * wide(cos) + mixed


def _qkv_prep(proj, tables):
    cos, lo, hi = tables

    def fn(x, c, l, h):
        rot = _rotate(x[:, :2 * ATT_W], c, l, h, False)
        return (jnp.concatenate([(rot[:, :ATT_W] * HEAD_DIM ** -0.5).astype(BF16), rot[:, ATT_W:].astype(BF16),
                                 x[:, 2 * ATT_W:].astype(BF16)], axis=1),), ()

    (qkv,) = _rowwise(fn, [(proj, 3 * ATT_W, 0), (cos, LANES, 0), (lo, LANES, 0), (hi, LANES, 0)],
                      [(3 * ATT_W, BF16)], name="qkv_prep")
    return qkv


ATT_T = 512


def _multiplicity(delta):
    ok = delta >= 0
    near = jnp.where(ok & (delta <= 128), 1.0, 0.0)
    mid = jnp.where(ok & (delta <= 512) & ((delta & 3) == 0), 1.0, 0.0)
    far = jnp.where(ok & ((delta & 15) == 0), 1.0, 0.0)
    return near + mid + far


def _attention_bias():
    t = ATT_T
    pos = jnp.arange(t, dtype=jnp.int32)
    delta = jnp.arange(S // t, dtype=jnp.int32)[:, None, None] * t + pos[None, :, None] - pos[None, None, :]
    mult = _multiplicity(delta)
    return jnp.where(mult > 0.0, jnp.log(jnp.maximum(mult, 1.0)), -1e30).astype(F32)


def _head_split(v, first):
    zero = jnp.zeros_like(v)
    return [jnp.where(first, v, zero), jnp.where(first, zero, v)]


def _flash_fwd(qkv, bias):
    t = ATT_T
    n_hp = ATT_W // LANES

    def body(q_ref, k_ref, v_ref, b_ref, o_ref, lse_ref):
        i = pl.program_id(1)
        first = lax.broadcasted_iota(jnp.int32, (1, LANES), 1) < HEAD_DIM
        qs = _head_split(q_ref[...], first)

        def kv_step(j, carry):
            m0, l0, m1, l1, acc = carry
            off = pl.multiple_of(j * t, t)
            kb = k_ref[pl.ds(off, t), :]
            vs = _head_split(v_ref[pl.ds(off, t), :], first)
            bias_t = b_ref[i - j]
            new = []
            pv = None
            for h, (m_prev, l_prev) in enumerate(((m0, l0), (m1, l1))):
                s = lax.dot_general(qs[h], kb, NT, preferred_element_type=F32) + bias_t
                m_new = jnp.maximum(m_prev, jnp.max(s, axis=1, keepdims=True))
                p = jnp.exp(s - m_new)
                alpha = jnp.exp(m_prev - m_new)
                l_new = alpha * l_prev + jnp.sum(p, axis=1, keepdims=True)
                d = lax.dot_general(p.astype(BF16), vs[h], NN, preferred_element_type=F32)
                pv = d if pv is None else pv + d
                new.append((m_new, l_new, alpha))
            acc = acc * jnp.where(first, new[0][2], new[1][2]) + pv
            return new[0][0], new[0][1], new[1][0], new[1][1], acc

        neg = jnp.full((t, 1), -1e30, F32)
        zero = jnp.zeros((t, 1), F32)
        m0, l0, m1, l1, acc = lax.fori_loop(0, i + 1, kv_step, (neg, zero, neg, zero, jnp.zeros((t, LANES), F32)))
        o_ref[...] = acc * jnp.where(first, 1.0 / l0, 1.0 / l1)
        lse_ref[...] = jnp.where(first, m0 + jnp.log(l0), m1 + jnp.log(l1))

    blk = pl.BlockSpec((t, LANES), lambda hp, i: (i, hp))
    k_full = pl.BlockSpec((S, LANES), lambda hp, i: (0, n_hp + hp))
    v_full = pl.BlockSpec((S, LANES), lambda hp, i: (0, 2 * n_hp + hp))
    return pl.pallas_call(
        body, name="flash_fwd", grid=(n_hp, S // t),
        in_specs=[blk, k_full, v_full, pl.BlockSpec((S // t, t, t), lambda hp, i: (0, 0, 0))], out_specs=[blk, blk],
        out_shape=[SDS((S, ATT_W), F32), SDS((S, ATT_W), F32)],
        compiler_params=_cparams(("parallel", "arbitrary")),
    )(qkv, qkv, qkv, bias)


def _flash_bwd(qkv, o, do, lse, bias):
    t = ATT_T
    n_hp = ATT_W // LANES
    n_t = S // t

    def body(q_ref, k_ref, v_ref, o_ref, do_ref, lse_ref, b_ref, dq_ref, dk_ref, dv_ref):
        j = pl.program_id(1)
        first = lax.broadcasted_iota(jnp.int32, (1, LANES), 1) < HEAD_DIM

        @pl.when(j == 0)
        def _():
            dq_ref[...] = jnp.zeros_like(dq_ref)

        kb = k_ref[...]
        vb = v_ref[...]
        ks = _head_split(kb, first)

        def q_step(i, carry):
            dk_acc, dv_acc = carry
            rows = pl.ds(pl.multiple_of(i * t, t), t)
            qs = _head_split(q_ref[rows, :], first)
            dob = do_ref[rows, :]
            prod = dob * o_ref[rows, :]
            d_all = jnp.sum(prod, axis=1, keepdims=True)
            d0 = jnp.sum(jnp.where(first, prod, 0.0), axis=1, keepdims=True)
            lse_b = lse_ref[rows, :]
            lse0 = jnp.max(jnp.where(first, lse_b, -jnp.inf), axis=1, keepdims=True)
            lse1 = jnp.max(jnp.where(first, -jnp.inf, lse_b), axis=1, keepdims=True)
            dos = _head_split(dob.astype(BF16), first)
            bias_t = b_ref[i - j]
            dq_t = jnp.zeros((t, LANES), F32)
            for h, (lse_h, d_h) in enumerate(((lse0, d0), (lse1, d_all - d0))):
                s = lax.dot_general(qs[h], kb, NT, preferred_element_type=F32)
                p = jnp.exp(s + (bias_t - lse_h))
                dp = lax.dot_general(dos[h], vb, NT, preferred_element_type=F32)
                ds = (p * (dp - d_h)).astype(BF16)
                dv_acc = dv_acc + lax.dot_general(p.astype(BF16), dos[h], TN, preferred_element_type=F32)
                dk_acc = dk_acc + lax.dot_general(ds, qs[h], TN, preferred_element_type=F32)
                dq_t = dq_t + lax.dot_general(ds, ks[h], NN, preferred_element_type=F32)
            dq_ref[rows, :] += dq_t
            return dk_acc, dv_acc

        zero = jnp.zeros((t, LANES), F32)
        dk_acc, dv_acc = lax.fori_loop(j, n_t, q_step, (zero, zero))
        dk_ref[...] = dk_acc
        dv_ref[...] = dv_acc

    blk = pl.BlockSpec((t, LANES), lambda hp, j: (j, hp))
    full = pl.BlockSpec((S, LANES), lambda hp, j: (0, hp))
    k_blk = pl.BlockSpec((t, LANES), lambda hp, j: (j, n_hp + hp))
    v_blk = pl.BlockSpec((t, LANES), lambda hp, j: (j, 2 * n_hp + hp))
    return pl.pallas_call(
        body, name="flash_bwd", grid=(n_hp, n_t),
        in_specs=[full, k_blk, v_blk, full, full, full, pl.BlockSpec((n_t, t, t), lambda hp, j: (0, 0, 0))],
        out_specs=[full, blk, blk],
        out_shape=[SDS((S, ATT_W), F32)] * 3,
        compiler_params=_cparams(("parallel", "arbitrary")),
    )(qkv, qkv, qkv, o, do, lse, bias)


SCAN_T = 256
ST_ROWS = 2 * N_CPLX // LANES
HALF = ST_ROWS // 2


def _scan_fwd(lam, bu):
    def body(lam_ref, bu_ref, st_ref, carry):
        @pl.when(pl.program_id(0) == 0)
        def _():
            carry[...] = jnp.zeros_like(carry)

        ar, ai = lam_ref[0:HALF, :], lam_ref[HALF:ST_ROWS, :]

        def step(t, c):
            sr, si = c
            b = bu_ref[t]
            nr = ar * sr - ai * si + b[0:HALF]
            ni = ar * si + ai * sr + b[HALF:ST_ROWS]
            st_ref[t, 0:HALF, :] = nr
            st_ref[t, HALF:ST_ROWS, :] = ni
            return nr, ni

        sr, si = lax.fori_loop(0, SCAN_T, step, (carry[0:HALF, :], carry[HALF:ST_ROWS, :]), unroll=8)
        carry[0:HALF, :] = sr
        carry[HALF:ST_ROWS, :] = si

    blk = pl.BlockSpec((SCAN_T, ST_ROWS, LANES), lambda i: (i, 0, 0))
    return pl.pallas_call(
        body, name="scan_fwd", grid=(S // SCAN_T,),
        in_specs=[pl.BlockSpec((ST_ROWS, LANES), lambda i: (0, 0)), blk], out_specs=blk,
        out_shape=SDS((S, ST_ROWS, LANES), F32),
        scratch_shapes=[pltpu.VMEM((ST_ROWS, LANES), F32)],
        compiler_params=_cparams(("arbitrary",)),
    )(lam, bu)


def _scan_bwd(lam, dst, states):
    n_blk = S // SCAN_T

    def body(lam_ref, d_ref, st_ref, g_ref, dlam_ref, carry):
        @pl.when(pl.program_id(0) == 0)
        def _():
            carry[...] = jnp.zeros_like(carry)
            dlam_ref[...] = jnp.zeros_like(dlam_ref)

        ar, ai = lam_ref[0:HALF, :], lam_ref[HALF:ST_ROWS, :]

        def step(kk, c):
            t = SCAN_T - 1 - kk
            gr, gi, dar, dai = c
            x = st_ref[t]
            xr, xi = x[0:HALF], x[HALF:ST_ROWS]
            dar = dar + gr * xr + gi * xi
            dai = dai + gi * xr - gr * xi
            d = d_ref[t]
            ngr = d[0:HALF] + ar * gr + ai * gi
            ngi = d[HALF:ST_ROWS] + ar * gi - ai * gr
            g_ref[t, 0:HALF, :] = ngr
            g_ref[t, HALF:ST_ROWS, :] = ngi
            return ngr, ngi, dar, dai

        zero = jnp.zeros((HALF, LANES), F32)
        gr, gi, dar, dai = lax.fori_loop(0, SCAN_T, step, (carry[0:HALF, :], carry[HALF:ST_ROWS, :], zero, zero),
                                         unroll=8)
        carry[0:HALF, :] = gr
        carry[HALF:ST_ROWS, :] = gi
        dlam_ref[0:HALF, :] += dar
        dlam_ref[HALF:ST_ROWS, :] += dai

    blk = pl.BlockSpec((SCAN_T, ST_ROWS, LANES), lambda i: (n_blk - 1 - i, 0, 0))
    small = pl.BlockSpec((ST_ROWS, LANES), lambda i: (0, 0))
    return pl.pallas_call(
        body, name="scan_bwd", grid=(n_blk,),
        in_specs=[small, blk, blk], out_specs=[blk, small],
        out_shape=[SDS((S, ST_ROWS, LANES), F32), SDS((ST_ROWS, LANES), F32)],
        scratch_shapes=[pltpu.VMEM((ST_ROWS, LANES), F32)],
        compiler_params=_cparams(("arbitrary",)),
    )(lam, dst, states)


def _ssm_prep(a_re, a_im, log_dt, b_re, b_im, c_re, c_im):
    lam = lax.complex(a_re, a_im)
    dt = jnp.exp(log_dt)[:, None]
    lam_bar = jnp.exp(lam * dt)
    b_bar = ((lam_bar - 1.0) / lam)[..., None] * lax.complex(b_re, b_im)
    lam_t = jnp.concatenate([jnp.real(lam_bar).reshape(HALF, LANES), jnp.imag(lam_bar).reshape(HALF, LANES)], axis=0)
    on_diag = (lax.broadcasted_iota(jnp.int32, (SSM_W, N_CPLX), 0) // SSM_GROUP
               == lax.broadcasted_iota(jnp.int32, (SSM_W, N_CPLX), 1) // SSM_STATE)

    def block_diag(m):
        return jnp.where(on_diag, jnp.tile(m.reshape(SSM_W, SSM_STATE), (1, SSM_GROUPS)), 0.0)

    w_b = jnp.concatenate([block_diag(jnp.real(b_bar).transpose(0, 2, 1)),
                           block_diag(jnp.imag(b_bar).transpose(0, 2, 1))], axis=1)
    w_ct = jnp.concatenate([block_diag(c_re), -block_diag(c_im)], axis=1)
    return lam_t, w_b, w_ct


U_SSM_COLS = (4 * ATT_W, SSM_W)


def _row(v):
    return v.reshape(1, -1)


def _even_fwd(x, pre, post, w_in, w_out, glu_w, glu_b, ssm_d, prep, tables):
    lam_t, w_b, w_ct = prep
    h = _norm_fwd(x, pre)
    proj = _mm(h, w_in, "nn", F32, b_blocks=True)
    qkv = _qkv_prep(proj, tables[:3])
    att, lse = _flash_fwd(qkv, tables[3])
    bu = _mm(proj, w_b, "nn", F32, a_cols=U_SSM_COLS)
    states = _scan_fwd(lam_t, bu.reshape(S, ST_ROWS, LANES))
    y = _mm(states.reshape(S, 2 * N_CPLX), w_ct, "nt", F32)

    def act1(yv, uv, dv):
        return (_gelu_and_grad(yv + dv * uv)[0],), ()

    (z1,) = _rowwise(act1, [(y, SSM_W, 0), (proj, SSM_W, 8), (ssm_d, SSM_W, 0)], [(SSM_W, F32)], name="ssm_act_fwd")
    lin = _mm(z1, glu_w, "nn", F32)

    def gate(att_v, ga, gs, z1v, linv, bv):
        ssm_out = z1v * _sigmoid(linv + bv)
        return (jnp.concatenate([att_v * _silu_and_grad(ga)[0], ssm_out * _silu_and_grad(gs)[0]], axis=1),), ()

    (merged,) = _rowwise(gate, [(att, ATT_W, 0), (proj, ATT_W, 3), (proj, SSM_W, 9), (z1, SSM_W, 0),
                                (lin, SSM_W, 0), (glu_b, SSM_W, 0)], [(EVEN_OUT, BF16)], name="even_gate_fwd")
    yout = _mm(merged, w_out, "nn", F32)
    x_next = _post_fwd(x, yout, post)
    saved = (x, h, proj, qkv, att, lse, states, y, z1, lin, merged, yout)
    return x_next, saved


def _even_bwd(g, saved, pre, post, w_in, w_out, glu_w, glu_b, ssm_d, prep, tables):
    x, h, proj, qkv, att, lse, states, y, z1, lin, merged, yout = saved
    lam_t, w_b, w_ct = prep
    dyout, dpost = _post_bwd(g, yout, post)
    dmerged = _mm(dyout, w_out, "nt", F32)
    dw_out = _mm(merged, dyout, "tn", BF16)

    def gate_bwd(dm_a, dm_s, att_v, ga, gs, z1v, linv, bv):
        sa, dsa = _silu_and_grad(ga)
        ss, dss = _silu_and_grad(gs)
        sig = _sigmoid(linv + bv)
        ssm_out = z1v * sig
        dssm = dm_s * ss
        dlin = dssm * z1v * sig * (1.0 - sig)
        return (dm_a * sa, dm_a * att_v * dsa, dm_s * ssm_out * dss, dssm * sig, dlin), (dlin,)

    datt, dg_att, dg_ssm, dz1a, dlin, dglu_b = _rowwise(
        gate_bwd, [(dmerged, ATT_W, 0), (dmerged, SSM_W, 2), (att, ATT_W, 0), (proj, ATT_W, 3), (proj, SSM_W, 9),
                   (z1, SSM_W, 0), (lin, SSM_W, 0), (glu_b, SSM_W, 0)],
        [(ATT_W, F32), (ATT_W, BF16), (SSM_W, BF16), (SSM_W, F32), (SSM_W, BF16)], [SSM_W], name="even_gate_bwd")
    dz1b = _mm(dlin, glu_w, "nt", F32)
    dglu_w = _mm(z1, dlin, "tn", BF16)

    def act1_bwd(da, db, yv, uv, dv):
        dpre = (da + db) * _gelu_and_grad(yv + dv * uv)[1]
        return (dpre, dpre * dv), (dpre * uv,)

    dy, du_direct, dd = _rowwise(act1_bwd, [(dz1a, SSM_W, 0), (dz1b, SSM_W, 0), (y, SSM_W, 0), (proj, SSM_W, 8),
                                            (ssm_d, SSM_W, 0)], [(SSM_W, BF16), (SSM_W, F32)], [SSM_W],
                                 name="ssm_act_bwd")
    dst = _mm(dy, w_ct, "nn", F32)
    dw_ct = _mm(dy, states.reshape(S, 2 * N_CPLX), "tn", F32)
    gst, dlam = _scan_bwd(lam_t, dst.reshape(S, ST_ROWS, LANES), states)
    dbu = gst.reshape(S, 2 * N_CPLX)
    du_state = _mm(dbu, w_b, "nt", F32)
    dw_b = _mm(proj, dbu, "tn", F32, a_cols=U_SSM_COLS)
    dq, dk, dv = _flash_bwd(qkv, att, datt, lse, tables[3])

    def assemble(dqv, dkv, dvv, dga, dua, dub, dgs, c, l, h):
        rot = _rotate(jnp.concatenate([dqv, dkv], axis=1), c, l, h, True)
        return (jnp.concatenate([(rot[:, :ATT_W] * HEAD_DIM ** -0.5).astype(BF16), rot[:, ATT_W:].astype(BF16),
                                 dvv.astype(BF16), dga, (dua + dub).astype(BF16), dgs], axis=1),), ()

    (dproj,) = _rowwise(assemble, [(dq, ATT_W, 0), (dk, ATT_W, 0), (dv, ATT_W, 0), (dg_att, ATT_W, 0),
                                   (du_state, SSM_W, 0), (du_direct, SSM_W, 0), (dg_ssm, SSM_W, 0),
                                   (tables[0], LANES, 0), (tables[1], LANES, 0), (tables[2], LANES, 0)],
                        [(EVEN_IN, BF16)], name="dproj_assemble")
    dw_in = _mm(h, dproj, "tn", BF16, out_blocks=True)
    dh = _mm(dproj, w_in, "nt", F32, b_blocks=True)
    g_prev, dpre = _pre_bwd(g, dh, x, pre)
    return g_prev, dict(pre=dpre, post=dpost, w_in=dw_in, w_out=dw_out, glu_w=dglu_w, glu_b=dglu_b, ssm_d=dd,
                        prep=(dlam, dw_b, dw_ct))


def _odd_fwd(x, pre, post, w_in, pool_w, pool_scale, w_out):
    h = _norm_fwd(x, pre)
    proj = _mm(h, w_in, "nn", F32, b_blocks=True)
    mixed = _pool(proj, 0, False, BF16)
    ylin = _gmm(mixed, pool_w, "nn", F32)

    def gate(yl, gt, sc):
        return (yl * sc * _silu_and_grad(gt)[0],), ()

    (z,) = _rowwise(gate, [(ylin, POOL_W, 0), (proj, POOL_W, 1), (pool_scale, POOL_W, 0)], [(POOL_W, BF16)],
                    name="odd_gate_fwd")
    yout = _mm(z, w_out, "nn", F32)
    x_next = _post_fwd(x, yout, post)
    return x_next, (x, h, proj, mixed, ylin, z, yout)


def _odd_bwd(g, saved, pre, post, w_in, pool_w, pool_scale, w_out):
    x, h, proj, mixed, ylin, z, yout = saved
    dyout, dpost = _post_bwd(g, yout, post)
    dz = _mm(dyout, w_out, "nt", F32)
    dw_out = _mm(z, dyout, "tn", BF16)

    def gate_bwd(dzv, yl, gt, sc):
        sg, dsg = _silu_and_grad(gt)
        tt = dzv * sg
        return (tt * sc, dzv * yl * sc * dsg), (tt * yl,)

    dylin, dproj_gate, dscale = _rowwise(gate_bwd, [(dz, POOL_W, 0), (ylin, POOL_W, 0), (proj, POOL_W, 1),
                                                    (pool_scale, POOL_W, 0)],
                                         [(POOL_W, BF16), (POOL_W, BF16, ODD_IN, 1)], [POOL_W], name="odd_gate_bwd")
    dmixed = _gmm(dylin, pool_w, "nt", F32)
    dpool_w = _gmm(mixed, dylin, "tn", BF16)
    dproj = _pool(dmixed, 0, True, BF16, into=dproj_gate)
    dw_in = _mm(h, dproj, "tn", BF16, out_blocks=True)
    dh = _mm(dproj, w_in, "nt", F32, b_blocks=True)
    g_prev, dpre = _pre_bwd(g, dh, x, pre)
    return g_prev, dict(pre=dpre, post=dpost, w_in=dw_in, w_out=dw_out, pool_w=dpool_w, pool_scale=dscale)


def _my_index():
    return 4 * lax.axis_index("x") + 2 * lax.axis_index("y") + lax.axis_index("c")


def _exchange(arrs, gather, name, after=()):
    n = len(arrs)
    out_shape = [SDS((N_DEV,) + a.shape, a.dtype) if gather else SDS(a.shape, a.dtype) for a in arrs]

    def body(*refs):
        ins, outs = refs[:n], refs[n + len(after):2 * n + len(after)]
        send_sems, recv_sems, local_sems = refs[2 * n + len(after):]
        me = _my_index()

        def src(i, j):
            return ins[i] if gather else ins[i].at[j]

        def remote(i, j, src_slot, dst_slot, recv_slot):
            return pltpu.make_async_remote_copy(
                src_ref=src(i, src_slot), dst_ref=outs[i].at[dst_slot], send_sem=send_sems.at[i, j],
                recv_sem=recv_sems.at[i, recv_slot], device_id=(j // 4, (j // 2) % 2, j % 2), device_id_type=MESH_ID)

        def local(i):
            return pltpu.make_async_copy(src(i, me), outs[i].at[me], local_sems.at[i])

        for i in range(n):
            local(i).start()
        for j in range(N_DEV):
            @pl.when(me != j)
            def _(j=j):
                for i in range(n):
                    remote(i, j, j, me, me).start()
        for j in range(N_DEV):
            @pl.when(me != j)
            def _(j=j):
                for i in range(n):
                    remote(i, j, j, me, me).wait_send()
                    remote(i, j, j, j, j).wait_recv()
        for i in range(n):
            local(i).wait()

    any_spec = pl.BlockSpec(memory_space=pl.ANY)
    return pl.pallas_call(
        body, name=name, in_specs=[any_spec] * (n + len(after)), out_specs=[any_spec] * n, out_shape=out_shape,
        scratch_shapes=[pltpu.SemaphoreType.DMA((n, N_DEV)), pltpu.SemaphoreType.DMA((n, N_DEV)),
                        pltpu.SemaphoreType.DMA((n,))],
    )(*arrs, *after)


HBM_SPEC = pl.BlockSpec(memory_space=pltpu.HBM)
SEM_SPEC = pl.BlockSpec(memory_space=pltpu.SEMAPHORE)
SPLIT_EFFECT = pltpu.SideEffectType.DATAFLOW_SIDE_EFFECTING


def _device_of(j):
    return (j // 4, (j // 2) % 2, j % 2)


def _split_copy(srcs, lands, send_sems, recv_sems, gather, i, j, dst_slot, recv_slot):
    return pltpu.make_async_remote_copy(
        src_ref=srcs[i] if gather else srcs[i].at[j], dst_ref=lands[i].at[dst_slot],
        send_sem=send_sems.at[i * N_DEV + j], recv_sem=recv_sems.at[i * N_DEV + recv_slot],
        device_id=_device_of(j), device_id_type=MESH_ID)


def _xchg_start(name, srcs, lands, gather, after=()):
    n = len(srcs)
    n_in = 2 * n + len(after)

    def body(*refs):
        src_refs, land_refs = refs[:n], refs[n:2 * n]
        send_sems, recv_sems, token = refs[n_in], refs[n_in + 1], refs[-1]
        me = _my_index()
        for j in range(N_DEV):
            @pl.when(me != j)
            def _(j=j):
                for i in range(n):
                    _split_copy(src_refs, land_refs, send_sems, recv_sems, gather, i, j, me, me).start()
        token[...] = jnp.zeros_like(token)

    thru = [pltpu.HBM(a.shape, a.dtype) for a in list(srcs) + list(lands)]
    res = pl.pallas_call(
        body, name=name,
        out_shape=(pltpu.SemaphoreType.DMA((n * N_DEV,)), pltpu.SemaphoreType.DMA((n * N_DEV,)), *thru,
                   SDS((8, LANES), F32)),
        in_specs=[HBM_SPEC] * (2 * n) + [pl.BlockSpec(memory_space=pl.ANY)] * len(after),
        out_specs=(SEM_SPEC, SEM_SPEC, *([HBM_SPEC] * (2 * n)), pl.BlockSpec(memory_space=pltpu.VMEM)),
        input_output_aliases={i: 2 + i for i in range(2 * n)},
        compiler_params=pltpu.CompilerParams(has_side_effects=SPLIT_EFFECT),
    )(*[pltpu.with_memory_space_constraint(a, pltpu.HBM) for a in list(srcs) + list(lands)], *after)
    return res[0], res[1], list(res[2:2 + n]), list(res[2 + n:2 + 2 * n]), res[-1]


def _xchg_wait(name, started, gather, after):
    send_sems, recv_sems, srcs, lands, _ = started
    n = len(srcs)

    def body(*refs):
        src_refs, land_refs = refs[:n], refs[n:2 * n]
        send_r, recv_r = refs[2 * n], refs[2 * n + 1]
        me = _my_index()
        for j in range(N_DEV):
            @pl.when(me != j)
            def _(j=j):
                for i in range(n):
                    _split_copy(src_refs, land_refs, send_r, recv_r, gather, i, j, me, me).wait_send()
                    _split_copy(src_refs, land_refs, send_r, recv_r, gather, i, j, j, j).wait_recv()

    thru = [pltpu.HBM(a.shape, a.dtype) for a in list(srcs) + list(lands)]
    res = pl.pallas_call(
        body, name=name, out_shape=tuple(thru),
        in_specs=[HBM_SPEC] * (2 * n) + [SEM_SPEC, SEM_SPEC] + [pl.BlockSpec(memory_space=pl.ANY)] * len(after),
        out_specs=tuple([HBM_SPEC] * (2 * n)),
        input_output_aliases={i: i for i in range(2 * n)},
        compiler_params=pltpu.CompilerParams(has_side_effects=SPLIT_EFFECT),
    )(*srcs, *lands, send_sems, recv_sems, *after)
    return list(res[n:])


def _landing_zones(name, arrs, from_slots):
    n = len(arrs)

    def body(*refs):
        ins, outs, sems = refs[:n], refs[n:2 * n], refs[2 * n]
        me = _my_index()
        copies = [pltpu.make_async_copy(ins[i].at[me] if from_slots else ins[i], outs[i].at[me], sems.at[i])
                  for i in range(n)]
        for c in copies:
            c.start()
        for c in copies:
            c.wait()

    any_spec = pl.BlockSpec(memory_space=pl.ANY)
    return pl.pallas_call(
        body, name=name, in_specs=[any_spec] * n, out_specs=[any_spec] * n,
        out_shape=[SDS(a.shape if from_slots else (N_DEV,) + a.shape, a.dtype) for a in arrs],
        scratch_shapes=[pltpu.SemaphoreType.DMA((n,))],
    )(*arrs)


def _adam_layers(w, slot_list, m, v, name):
    n_l, r, c = w.shape
    ns = slot_list[0].shape[0]
    tr = r
    while tr * c * 4 > (1 << 20) and tr % 16 == 0:
        tr //= 2
    assert r % tr == 0 and len(slot_list) == n_l

    def body(*refs):
        w_ref, slot_refs = refs[0], refs[1:1 + n_l]
        m_ref, v_ref, go_ref, d_ref, mo_ref, vo_ref = refs[1 + n_l:]
        layer = pl.program_id(0)
        g = None
        for l, g_ref in enumerate(slot_refs):
            gl = g_ref[0].astype(F32)
            for s in range(1, ns):
                gl = gl + g_ref[s].astype(F32)
            g = gl if g is None else jnp.where(layer == l, gl, g)
        mn = ADAM_B1 * m_ref[...] + (1.0 - ADAM_B1) * g
        vn = ADAM_B2 * v_ref[...] + (1.0 - ADAM_B2) * (g * g)
        m_hat = mn / (1.0 - ADAM_B1 ** ADAM_STEP)
        v_hat = vn / (1.0 - ADAM_B2 ** ADAM_STEP)
        go_ref[...] = g
        d_ref[...] = -ADAM_LR * (m_hat / (jnp.sqrt(v_hat) + ADAM_EPS) + ADAM_WD * w_ref[...])
        mo_ref[...] = mn
        vo_ref[...] = vn

    blk = pl.BlockSpec((None, tr, c), lambda l, i: (l, i, 0))
    slot_specs = [pl.BlockSpec((ns, tr, c), lambda l, i, k=k: (0, jnp.where(l == k, i, 0), 0)) for k in range(n_l)]
    return pl.pallas_call(
        body, name=name, grid=(n_l, r // tr),
        in_specs=[blk] + slot_specs + [blk, blk],
        out_specs=[blk] * 4, out_shape=[SDS((n_l, r, c), F32)] * 4,
        compiler_params=_cparams(("arbitrary", "arbitrary")),
    )(w, *slot_list, m, v)


def _adam(w, gslots, m, v, name):
    r, c = w.shape
    ns = gslots.shape[0]
    tr = r
    while tr * c * 4 > (1 << 20) and tr % 16 == 0:
        tr //= 2
    assert r % tr == 0

    def body(w_ref, g_ref, m_ref, v_ref, go_ref, d_ref, mo_ref, vo_ref):
        g = g_ref[0].astype(F32)
        for s in range(1, ns):
            g = g + g_ref[s].astype(F32)
        wv = w_ref[...]
        mn = ADAM_B1 * m_ref[...] + (1.0 - ADAM_B1) * g
        vn = ADAM_B2 * v_ref[...] + (1.0 - ADAM_B2) * (g * g)
        m_hat = mn / (1.0 - ADAM_B1 ** ADAM_STEP)
        v_hat = vn / (1.0 - ADAM_B2 ** ADAM_STEP)
        go_ref[...] = g
        d_ref[...] = -ADAM_LR * (m_hat / (jnp.sqrt(v_hat) + ADAM_EPS) + ADAM_WD * wv)
        mo_ref[...] = mn
        vo_ref[...] = vn

    blk = pl.BlockSpec((tr, c), lambda i: (i, 0))
    return pl.pallas_call(
        body, name=name, grid=(r // tr,),
        in_specs=[blk, pl.BlockSpec((ns, tr, c), lambda i: (0, i, 0)), blk, blk],
        out_specs=[blk] * 4, out_shape=[SDS((r, c), F32)] * 4,
        compiler_params=_cparams(("parallel",)),
    )(w, gslots, m, v)


SMALL_NAMES = ("pre_norm", "post_norm", "ssm_a_re", "ssm_a_im", "ssm_log_dt", "ssm_b_re", "ssm_b_im", "ssm_c_re",
               "ssm_c_im", "ssm_d", "ssm_glu_b")
SHARDED_NAMES = ("even_w_in", "even_w_out", "ssm_glu_w", "odd_w_in", "pool_w", "odd_w_out")
WEIGHT_ORDER = ("pre_norm", "post_norm", "even_w_in", "even_w_out", "ssm_a_re", "ssm_a_im", "ssm_log_dt", "ssm_b_re",
                "ssm_b_im", "ssm_c_re", "ssm_c_im", "ssm_d", "ssm_glu_w", "ssm_glu_b", "odd_w_in", "pool_w",
                "pool_scale", "odd_w_out")
PACK_ROWS_ALIGN = 8


def _pack(parts):
    flat = jnp.concatenate([p.reshape(-1).astype(F32) for p in parts])
    rows = -(-flat.shape[0] // (LANES * PACK_ROWS_ALIGN)) * PACK_ROWS_ALIGN
    return jnp.pad(flat, (0, rows * LANES - flat.shape[0])).reshape(rows, LANES)


def _unpack(packed, shapes):
    flat = packed.reshape(-1)
    out, off = [], 0
    for shp in shapes:
        size = math.prod(shp)
        out.append(flat[off:off + size].reshape(shp))
        off += size
    return out


EVEN_SHARDED = ("w_in", "w_out", "glu_w")
ODD_SHARDED = ("w_in", "pool_w", "w_out")
FAMILY = {(0, "w_in"): "even_w_in", (0, "w_out"): "even_w_out", (0, "glu_w"): "ssm_glu_w",
          (1, "w_in"): "odd_w_in", (1, "pool_w"): "pool_w", (1, "w_out"): "odd_w_out"}


def _sharded_keys(layer):
    return EVEN_SHARDED if layer % 2 == 0 else ODD_SHARDED


def _local_step(x, tgt, small, get_weights, on_grads):
    tables = _rope_tables() + (_attention_bias(),)
    preps, prep_vjps = [], []
    for i in range(2):
        out, vjp = jax.vjp(_ssm_prep, small["ssm_a_re"][i], small["ssm_a_im"][i], small["ssm_log_dt"][i],
                           small["ssm_b_re"][i], small["ssm_b_im"][i], small["ssm_c_re"][i], small["ssm_c_im"][i])
        preps.append(out)
        prep_vjps.append(vjp)

    def layer_args(layer, wts):
        i = layer // 2
        pre, post = _row(small["pre_norm"][layer]) + wts.get("token", 0.0), _row(small["post_norm"][layer])
        if layer % 2 == 0:
            return (pre, post, wts["w_in"], wts["w_out"], wts["glu_w"], _row(small["ssm_glu_b"][i]),
                    _row(small["ssm_d"][i]), preps[i], tables)
        return (pre, post, wts["w_in"], wts["pool_w"], _row(wts["pool_scale"]), wts["w_out"])

    saved, args = [], []
    cur = x
    for layer in range(4):
        after = (cur,) if layer else (cur, tables[0], tables[3], preps[0][1], preps[0][2], preps[1][1], preps[1][2])
        args.append(layer_args(layer, get_weights(layer, after)))
        cur, sv = (_even_fwd if layer % 2 == 0 else _odd_fwd)(cur, *args[layer])
        saved.append(sv)
    g, sq = _loss_grad(cur, tgt)
    loss = 0.5 * jnp.sum(sq) / D

    lg = [None] * 4
    token = jnp.zeros((), F32)
    for layer in reversed(range(4)):
        largs = list(args[layer])
        largs[1] = largs[1] + token
        g, lg[layer] = (_even_bwd if layer % 2 == 0 else _odd_bwd)(g, saved[layer], *largs)
        token = on_grads(layer, {k: lg[layer][k] for k in _sharded_keys(layer)})

    ssm_g = [prep_vjps[i](lg[2 * i]["prep"]) for i in range(2)]
    grads = {
        "pre_norm": jnp.concatenate([lg[l]["pre"] for l in range(4)], axis=0),
        "post_norm": jnp.concatenate([lg[l]["post"] for l in range(4)], axis=0),
        "ssm_d": jnp.concatenate([lg[0]["ssm_d"], lg[2]["ssm_d"]], axis=0),
        "ssm_glu_b": jnp.concatenate([lg[0]["glu_b"], lg[2]["glu_b"]], axis=0),
        "pool_scale": jnp.concatenate([lg[1]["pool_scale"], lg[3]["pool_scale"]], axis=0),
    }
    for idx, nm in enumerate(("ssm_a_re", "ssm_a_im", "ssm_log_dt", "ssm_b_re", "ssm_b_im", "ssm_c_re", "ssm_c_im")):
        grads[nm] = jnp.stack([ssm_g[0][idx], ssm_g[1][idx]])
    return loss, g, grads, token


def _to_slots(key, gfull):
    if key == "w_in":
        return gfull
    if key in ("w_out", "glu_w"):
        rr, nn = gfull.shape
        return gfull.reshape(N_DEV, rr // N_DEV, nn)
    assert key == "pool_w"
    gg, rr, nn = gfull.shape
    return gfull.reshape(gg, N_DEV, rr // N_DEV, nn).transpose(1, 0, 2, 3)


def _from_gathered(key, gat):
    if key == "w_in":
        return gat
    if key in ("w_out", "glu_w"):
        _, rr, nn = gat.shape
        return gat.reshape(N_DEV * rr, nn)
    assert key == "pool_w"
    _, gg, rr, nn = gat.shape
    return gat.transpose(1, 0, 2, 3).reshape(gg, N_DEV * rr, nn)


def kernel(x, pre_norm, post_norm, even_w_in, even_w_out, ssm_a_re, ssm_a_im, ssm_log_dt, ssm_b_re, ssm_b_im, ssm_c_re, ssm_c_im, ssm_d, ssm_glu_w, ssm_glu_b, odd_w_in, pool_w, pool_scale, odd_w_out, loss_target, m_pre_norm, m_post_norm, m_even_w_in, m_even_w_out, m_ssm_a_re, m_ssm_a_im, m_ssm_log_dt, m_ssm_b_re, m_ssm_b_im, m_ssm_c_re, m_ssm_c_im, m_ssm_d, m_ssm_glu_w, m_ssm_glu_b, m_odd_w_in, m_pool_w, m_pool_scale, m_odd_w_out, v_pre_norm, v_post_norm, v_even_w_in, v_even_w_out, v_ssm_a_re, v_ssm_a_im, v_ssm_log_dt, v_ssm_b_re, v_ssm_b_im, v_ssm_c_re, v_ssm_c_im, v_ssm_d, v_ssm_glu_w, v_ssm_glu_b, v_odd_w_in, v_pool_w, v_pool_scale, v_odd_w_out):
    w = dict(pre_norm=pre_norm, post_norm=post_norm, even_w_in=even_w_in, even_w_out=even_w_out, ssm_a_re=ssm_a_re,
             ssm_a_im=ssm_a_im, ssm_log_dt=ssm_log_dt, ssm_b_re=ssm_b_re, ssm_b_im=ssm_b_im, ssm_c_re=ssm_c_re,
             ssm_c_im=ssm_c_im, ssm_d=ssm_d, ssm_glu_w=ssm_glu_w, ssm_glu_b=ssm_glu_b, odd_w_in=odd_w_in,
             pool_w=pool_w, pool_scale=pool_scale, odd_w_out=odd_w_out)
    mom = dict(pre_norm=m_pre_norm, post_norm=m_post_norm, even_w_in=m_even_w_in, even_w_out=m_even_w_out,
               ssm_a_re=m_ssm_a_re, ssm_a_im=m_ssm_a_im, ssm_log_dt=m_ssm_log_dt, ssm_b_re=m_ssm_b_re,
               ssm_b_im=m_ssm_b_im, ssm_c_re=m_ssm_c_re, ssm_c_im=m_ssm_c_im, ssm_d=m_ssm_d, ssm_glu_w=m_ssm_glu_w,
               ssm_glu_b=m_ssm_glu_b, odd_w_in=m_odd_w_in, pool_w=m_pool_w, pool_scale=m_pool_scale,
               odd_w_out=m_odd_w_out)
    var = dict(pre_norm=v_pre_norm, post_norm=v_post_norm, even_w_in=v_even_w_in, even_w_out=v_even_w_out,
               ssm_a_re=v_ssm_a_re, ssm_a_im=v_ssm_a_im, ssm_log_dt=v_ssm_log_dt, ssm_b_re=v_ssm_b_re,
               ssm_b_im=v_ssm_b_im, ssm_c_re=v_ssm_c_re, ssm_c_im=v_ssm_c_im, ssm_d=v_ssm_d, ssm_glu_w=v_ssm_glu_w,
               ssm_glu_b=v_ssm_glu_b, odd_w_in=v_odd_w_in, pool_w=v_pool_w, pool_scale=v_pool_scale,
               odd_w_out=v_odd_w_out)
    me = _my_index()
    scale_cols = pool_scale.shape[1]

    def start_gather(layer, after=()):
        i = layer // 2
        shards = [w[FAMILY[(layer % 2, k)]][i].astype(BF16) for k in _sharded_keys(layer)]
        if layer % 2 == 1:
            shards.append(jnp.pad(pool_scale[i][None], ((0, PACK_ROWS_ALIGN - 1), (0, 0))))
        lands = _landing_zones(f"gather_zones_{layer}", shards, False)
        return _xchg_start(f"gather_start_{layer}", shards, lands, True, after)

    gather_started = {0: start_gather(0)}
    small = {nm: w[nm] for nm in SMALL_NAMES}

    def get_weights(layer, after):
        lands = _xchg_wait(f"gather_wait_{layer}", gather_started[layer], True, after)
        wts = {k: _from_gathered(k, gat) for k, gat in zip(_sharded_keys(layer), lands)}
        if layer % 2 == 1:
            wts["pool_scale"] = lands[-1][:, 0, :].reshape(N_DEV * scale_cols)
        if layer == 0:
            for later in (1, 2, 3):
                gather_started[later] = start_gather(later, after=(lands[0],))
            wts["token"] = sum(gather_started[later][4][0, 0] for later in (1, 2, 3))
        return wts

    scatter_started = [None] * 4

    def on_grads(layer, lg):
        slots = [_to_slots(k, lg[k]) for k in _sharded_keys(layer)]
        lands = _landing_zones(f"scatter_zones_{layer}", slots, True)
        scatter_started[layer] = _xchg_start(f"scatter_start_{layer}", slots, lands, False)
        return scatter_started[layer][4][0, 0]

    loss_local, grad_x, grads, token = _local_step(x[0], loss_target[0], small, get_weights, on_grads)

    def adam_family(parity, k):
        nm = FAMILY[(parity, k)]
        shp = w[nm].shape
        cols = shp[-1]
        slot_list = [recv[(parity + 2 * i, k)].reshape(N_DEV, -1, cols) for i in range(2)]
        outs = _adam_layers(w[nm].reshape(2, -1, cols), slot_list, mom[nm].reshape(2, -1, cols),
                            var[nm].reshape(2, -1, cols), name=f"adam_{nm}")
        return [o.reshape(shp) for o in outs]

    recv, res = {}, {}
    for layer in (3, 1):
        lands = _xchg_wait(f"scatter_wait_{layer}", scatter_started[layer], False, (scatter_started[0][4],))
        for k, land in zip(_sharded_keys(layer), lands):
            recv[(layer, k)] = land
    for k in ODD_SHARDED:
        res[FAMILY[(1, k)]] = adam_family(1, k)

    extra_parts = [grads["pool_scale"], loss_local.reshape(1)]
    extra_zeros = [jnp.zeros(p.shape, F32) for p in extra_parts]
    small_parts = [grads[nm] for nm in SMALL_NAMES] + extra_parts
    small_shapes = [w[nm].shape for nm in SMALL_NAMES] + [p.shape for p in extra_parts]
    (small_slots,) = _exchange([_pack(small_parts) + token], True, "gather_small_grads",
                               after=tuple(res[FAMILY[(1, k)]][0] for k in ODD_SHARDED))

    for layer in (2, 0):
        lands = _xchg_wait(f"scatter_wait_{layer}", scatter_started[layer], False, (small_slots,))
        for k, land in zip(_sharded_keys(layer), lands):
            recv[(layer, k)] = land
    for k in EVEN_SHARDED:
        res[FAMILY[(0, k)]] = adam_family(0, k)
    w_pack = _pack([w[nm] for nm in SMALL_NAMES] + extra_zeros)
    m_pack = _pack([mom[nm] for nm in SMALL_NAMES] + extra_zeros)
    v_pack = _pack([var[nm] for nm in SMALL_NAMES] + extra_zeros)
    outs = _adam(w_pack, small_slots, m_pack, v_pack, name="adam_small")
    unpacked = [_unpack(o, small_shapes) for o in outs]
    for idx, nm in enumerate(SMALL_NAMES):
        res[nm] = [unpacked[kind][idx] for kind in range(4)]
    loss = unpacked[0][-1].reshape(())
    g_scale = lax.dynamic_slice_in_dim(unpacked[0][-2], me * scale_cols, scale_cols, axis=1)
    pad = ((0, PACK_ROWS_ALIGN - 2), (0, 0))
    outs = _adam(jnp.pad(pool_scale, pad), jnp.pad(g_scale, pad)[None], jnp.pad(m_pool_scale, pad),
                 jnp.pad(v_pool_scale, pad), name="adam_pool_scale")
    res["pool_scale"] = [o[:2] for o in outs]

    out = [loss, grad_x[None]]
    for kind in range(4):
        out += [res[nm][kind] for nm in WEIGHT_ORDER]
    return tuple(out)
```

```python
import functools
import math

import jax
import jax.numpy as jnp
from jax import lax
from jax.experimental import pallas as pl
from jax.experimental.pallas import tpu as pltpu

F32 = jnp.float32
BF16 = jnp.bfloat16
SDS = jax.ShapeDtypeStruct

N_DEV = 8
S = 2048
D = 1024
HEAD_DIM = 64
ROT_DIM = 16
ROPE_THETA = 500000.0
ATT_W = 1024
SSM_W = 512
SSM_GROUPS = 32
SSM_GROUP = 16
SSM_STATE = 64
N_CPLX = SSM_GROUPS * SSM_STATE
POOL_W = 2048
POOL_GROUP = 512
EVEN_IN = 5120
EVEN_OUT = 1536
ODD_IN = 4096
RMS_EPS = 1e-6
LANES = 128
VMEM_LIMIT = 48 * 1024 * 1024

ADAM_LR = 0.001
ADAM_B1 = 0.9
ADAM_B2 = 0.999
ADAM_EPS = 1e-08
ADAM_WD = 0.01
ADAM_STEP = 10

MESH_ID = pl.DeviceIdType.MESH
NN = (((1,), (0,)), ((), ()))
NT = (((1,), (1,)), ((), ()))
TN = (((0,), (0,)), ((), ()))
_DN = {"nn": NN, "nt": NT, "tn": TN}


def _cparams(sem):
    return pltpu.CompilerParams(dimension_semantics=sem, vmem_limit_bytes=VMEM_LIMIT)


MM_TILES = (1024, 768, 512)


def _tile(dim):
    return next((t for t in MM_TILES if dim % t == 0), dim)


NT_BLOCKS_PER_STEP = 4


def _mm(a, b, mode, out_dtype, b_blocks=False, out_blocks=False, a_cols=None):
    if b_blocks:
        nblk, rows, cb = b.shape
        b2_shape = (rows, nblk * cb)
    else:
        b2_shape = b.shape
    a_shape = a.shape if a_cols is None else (a.shape[0], a_cols[1])
    if mode == "nn":
        (m, k), n = a_shape, b2_shape[1]
    elif mode == "nt":
        (m, k), n = a_shape, b2_shape[0]
    else:
        (k, m), n = a_shape, b2_shape[1]
    tm, tn, tk = _tile(m), _tile(n), _tile(k)
    per_step = 1
    if b_blocks and mode == "nn":
        tn = cb
    if b_blocks and mode == "nt":
        per_step = NT_BLOCKS_PER_STEP
        tk = per_step * cb
        tn = min(tn, MM_TILES[-1])
    if out_blocks:
        tn = n // N_DEV
    nk = k // tk
    a_unit = tm if mode == "tn" else tk
    assert a_cols is None or a_cols[0] % a_unit == 0
    a_off = 0 if a_cols is None else a_cols[0] // a_unit

    def body(a_ref, b_ref, o_ref, acc_ref):
        kk = pl.program_id(2)
        if per_step == 1:
            part = lax.dot_general(a_ref[...].astype(BF16), b_ref[...].astype(BF16), _DN[mode],
                                   preferred_element_type=F32)
        else:
            part = None
            for blk in range(per_step):
                d = lax.dot_general(a_ref[:, blk * cb:(blk + 1) * cb].astype(BF16), b_ref[blk].astype(BF16), NT,
                                    preferred_element_type=F32)
                part = d if part is None else part + d
        if nk == 1:
            o_ref[...] = part.astype(o_ref.dtype)
            return

        @pl.when(kk == 0)
        def _():
            acc_ref[...] = part

        @pl.when((kk > 0) & (kk < nk - 1))
        def _():
            acc_ref[...] += part

        @pl.when(kk == nk - 1)
        def _():
            o_ref[...] = (acc_ref[...] + part).astype(o_ref.dtype)

    if mode == "nn":
        a_spec = pl.BlockSpec((tm, tk), lambda i, j, kk: (i, a_off + kk))
        b_spec = pl.BlockSpec((tk, tn), lambda i, j, kk: (kk, j))
    elif mode == "nt":
        a_spec = pl.BlockSpec((tm, tk), lambda i, j, kk: (i, a_off + kk))
        b_spec = pl.BlockSpec((tn, tk), lambda i, j, kk: (j, kk))
    else:
        a_spec = pl.BlockSpec((tk, tm), lambda i, j, kk: (kk, a_off + i))
        b_spec = pl.BlockSpec((tk, tn), lambda i, j, kk: (kk, j))
    if b_blocks and mode == "nn":
        b_spec = pl.BlockSpec((None, tk, cb), lambda i, j, kk: (j, kk, 0))
    if b_blocks and mode == "nt":
        b_spec = pl.BlockSpec((per_step, tn, cb), lambda i, j, kk: (kk, j, 0))
    out_spec = pl.BlockSpec((tm, tn), lambda i, j, kk: (i, j))
    out_shape = SDS((m, n), out_dtype)
    if out_blocks:
        out_spec = pl.BlockSpec((None, tm, tn), lambda i, j, kk: (j, i, 0))
        out_shape = SDS((N_DEV, m, tn), out_dtype)
    return pl.pallas_call(
        body, name=f"mm_{mode}_{m}x{k}x{n}",
        grid=(m // tm, n // tn, nk),
        in_specs=[a_spec, b_spec],
        out_specs=out_spec,
        out_shape=out_shape,
        scratch_shapes=[pltpu.VMEM((tm, tn) if nk > 1 else (8, LANES), F32)],
        compiler_params=_cparams(("parallel", "parallel", "arbitrary")),
    )(a, b)


def _gmm(a, b, mode, out_dtype, tm=512):
    ng, gw = POOL_W // POOL_GROUP, POOL_GROUP
    ns = S // tm
    if mode in ("nn", "nt"):
        def body(a_ref, b_ref, o_ref):
            o_ref[...] = lax.dot_general(a_ref[...].astype(BF16), b_ref[...].astype(BF16), _DN[mode],
                                         preferred_element_type=F32).astype(o_ref.dtype)

        return pl.pallas_call(
            body, name=f"gmm_{mode}", grid=(ng, ns),
            in_specs=[pl.BlockSpec((tm, gw), lambda g, i: (i, g)),
                      pl.BlockSpec((None, gw, gw), lambda g, i: (g, 0, 0))],
            out_specs=pl.BlockSpec((tm, gw), lambda g, i: (i, g)),
            out_shape=SDS((S, POOL_W), out_dtype),
            compiler_params=_cparams(("parallel", "parallel")),
        )(a, b)

    def body_tn(a_ref, b_ref, o_ref, acc_ref):
        i = pl.program_id(1)

        @pl.when(i == 0)
        def _():
            acc_ref[...] = jnp.zeros_like(acc_ref)

        acc_ref[...] += lax.dot_general(a_ref[...].astype(BF16), b_ref[...].astype(BF16), TN,
                                        preferred_element_type=F32)

        @pl.when(i == ns - 1)
        def _():
            o_ref[...] = acc_ref[...].astype(o_ref.dtype)

    return pl.pallas_call(
        body_tn, name="gmm_tn", grid=(ng, ns),
        in_specs=[pl.BlockSpec((tm, gw), lambda g, i: (i, g)),
                  pl.BlockSpec((tm, gw), lambda g, i: (i, g))],
        out_specs=pl.BlockSpec((None, gw, gw), lambda g, i: (g, 0, 0)),
        out_shape=SDS((ng, gw, gw), out_dtype),
        scratch_shapes=[pltpu.VMEM((gw, gw), F32)],
        compiler_params=_cparams(("parallel", "arbitrary")),
    )(a, b)


def _rowwise(fn, inputs, out_defs, acc_defs=(), tm=256, name=None):
    n_in, n_out, n_acc = len(inputs), len(out_defs), len(acc_defs)
    in_specs, args = [], []
    for arr, width, cb in inputs:
        if arr.shape[0] == 1:
            in_specs.append(pl.BlockSpec((1, width), lambda i, cb=cb: (0, cb)))
        else:
            in_specs.append(pl.BlockSpec((tm, width), lambda i, cb=cb: (i, cb)))
        args.append(arr)
    out_defs = [d if len(d) == 4 else (d[0], d[1], d[0], 0) for d in out_defs]
    out_shape = [SDS((S, ww), dt) for _, dt, ww, _ in out_defs] + [SDS((1, w), F32) for w in acc_defs]
    out_specs = ([pl.BlockSpec((tm, w), lambda i, cb=cb: (i, cb)) for w, _, _, cb in out_defs]
                 + [pl.BlockSpec((1, w), lambda i: (0, 0)) for w in acc_defs])

    def kern(*refs):
        vals = [r[...] for r in refs[:n_in]]
        outs, accs = fn(*vals)
        for r, v in zip(refs[n_in:n_in + n_out], outs):
            r[...] = v.astype(r.dtype)
        if n_acc:
            acc_refs = refs[n_in + n_out:]

            @pl.when(pl.program_id(0) == 0)
            def _():
                for r in acc_refs:
                    r[...] = jnp.zeros_like(r)

            for r, v in zip(acc_refs, accs):
                r[...] += jnp.sum(v, axis=0, keepdims=True)

    res = pl.pallas_call(
        kern, name=name, grid=(S // tm,), in_specs=in_specs, out_specs=out_specs, out_shape=out_shape,
        compiler_params=_cparams(("arbitrary",)),
    )(*args)
    return res


def _sigmoid(x):
    return 1.0 / (1.0 + jnp.exp(-x))


def _silu_and_grad(x):
    s = _sigmoid(x)
    return x * s, s * (1.0 + x * (1.0 - s))


_GELU_K = math.sqrt(2.0 / math.pi)
_GELU_C = 0.044715


def _gelu_and_grad(x):
    t = jnp.tanh(_GELU_K * (x + _GELU_C * (x * x * x)))
    cdf = 0.5 * (1.0 + t)
    grad = cdf + 0.5 * x * (1.0 - t * t) * (_GELU_K * (1.0 + 3.0 * _GELU_C * x * x))
    return x * cdf, grad


def _rms(xv, gain):
    r = lax.rsqrt(jnp.mean(xv * xv, axis=-1, keepdims=True) + RMS_EPS)
    return xv * r * gain


def _rms_bwd(dout, xv, gain):
    r = lax.rsqrt(jnp.mean(xv * xv, axis=-1, keepdims=True) + RMS_EPS)
    xhat = xv * r
    dxhat = dout * gain
    dx = r * (dxhat - xhat * jnp.mean(dxhat * xhat, axis=-1, keepdims=True))
    return dx, dout * xhat


def _norm_fwd(x, gain):
    (h,) = _rowwise(lambda xv, g: ((_rms(xv, g),), ()), [(x, D, 0), (gain, D, 0)], [(D, BF16)], name="norm_fwd")
    return h


def _post_fwd(x, y, gain):
    (o,) = _rowwise(lambda xv, yv, g: ((xv + _rms(yv, g),), ()), [(x, D, 0), (y, D, 0), (gain, D, 0)],
                    [(D, F32)], name="post_fwd")
    return o


def _post_bwd(g, y, gain):
    def fn(gv, yv, gn):
        dx, dg = _rms_bwd(gv, yv, gn)
        return (dx,), (dg,)

    return _rowwise(fn, [(g, D, 0), (y, D, 0), (gain, D, 0)], [(D, BF16)], [D], name="post_bwd")


def _pre_bwd(g, dh, x, gain):
    def fn(gv, dhv, xv, gn):
        dx, dg = _rms_bwd(dhv, xv, gn)
        return (gv + dx,), (dg,)

    return _rowwise(fn, [(g, D, 0), (dh, D, 0), (x, D, 0), (gain, D, 0)], [(D, F32)], [D], name="pre_bwd")


def _loss_grad(xo, tgt):
    def fn(xv, tv):
        e = xv - tv
        return (e * (1.0 / D),), (e * e,)

    return _rowwise(fn, [(xo, D, 0), (tgt, D, 0)], [(D, F32)], [D], name="loss_grad")


def _pool(u_arr, col_block, transpose, out_dtype, into=None, tc=256):
    n_t = POOL_W // tc
    per_group = POOL_GROUP // tc

    def body(u_ref, *rest):
        o_ref = rest[-1]
        c = pl.program_id(0)
        grp = c // per_group
        xv = u_ref[...]
        t = lax.broadcasted_iota(jnp.int32, (S, 1), 0)
        win = jnp.left_shift(2, grp)
        cnt = jnp.minimum(t + 1, win).astype(F32)
        cur = xv / cnt if transpose else xv
        sums = []
        for k in (1, 2, 4, 8):
            if transpose:
                sh = jnp.where(t < S - k, pltpu.roll(cur, S - k, 0), 0.0)
            else:
                sh = jnp.where(t >= k, pltpu.roll(cur, k, 0), 0.0)
            cur = cur + sh
            sums.append(cur)
        tot = jnp.where(grp == 0, sums[0], jnp.where(grp == 1, sums[1], jnp.where(grp == 2, sums[2], sums[3])))
        res = tot - xv if transpose else tot / cnt - xv
        o_ref[...] = res.astype(o_ref.dtype)

    in_specs = [pl.BlockSpec((S, tc), lambda c: (0, col_block * n_t + c))]
    args = [u_arr]
    if into is not None:
        in_specs.append(pl.BlockSpec(memory_space=pl.ANY))
        args.append(into)
    return pl.pallas_call(
        body, name="pool_bwd" if transpose else "pool_fwd", grid=(n_t,),
        in_specs=in_specs,
        out_specs=pl.BlockSpec((S, tc), lambda c: (0, c)),
        out_shape=SDS((S, POOL_W) if into is None else into.shape, out_dtype),
        input_output_aliases={} if into is None else {1: 0},
        compiler_params=_cparams(("parallel",)),
    )(*args)


def _rope_tables():
    pos = jnp.arange(S, dtype=jnp.int32).astype(F32)
    inv_freq = ROPE_THETA ** (-jnp.arange(0, ROT_DIM, 2, dtype=F32) / ROT_DIM)
    ang = pos[:, None] * inv_freq[None, :]
    cos8, sin8 = jnp.cos(ang), jnp.sin(ang)
    half = ROT_DIM // 2
    zeros = jnp.zeros((S, HEAD_DIM - ROT_DIM), F32)
    cos = jnp.concatenate([cos8, cos8, jnp.ones((S, HEAD_DIM - ROT_DIM), F32)], axis=1)
    lo = jnp.concatenate([-sin8, jnp.zeros((S, half), F32), zeros], axis=1)
    hi = jnp.concatenate([jnp.zeros((S, half), F32), sin8, zeros], axis=1)
    rep = LANES // HEAD_DIM
    return jnp.tile(cos, (1, rep)), jnp.tile(lo, (1, rep)), jnp.tile(hi, (1, rep))


def _rotate(xv, cos, lo, hi, transpose):
    width = xv.shape[1]
    rep = width // LANES
    wide = lambda tab: jnp.concatenate([tab] * rep, axis=1)
    half = ROT_DIM // 2
    up = pltpu.roll(xv, width - half, 1)
    dn = pltpu.roll(xv, half, 1)
    mixed = up * wide(lo) + dn * wide(hi)
    return xv * wide(cos) - mixed if transpose else xv * wide(cos) + mixed


def _qkv_prep(proj, tables):
    cos, lo, hi = tables

    def fn(x, c, l, h):
        rot = _rotate(x[:, :2 * ATT_W], c, l, h, False)
        return (jnp.concatenate([(rot[:, :ATT_W] * HEAD_DIM ** -0.5).astype(BF16), rot[:, ATT_W:].astype(BF16),
                                 x[:, 2 * ATT_W:].astype(BF16)], axis=1),), ()

    (qkv,) = _rowwise(fn, [(proj, 3 * ATT_W, 0), (cos, LANES, 0), (lo, LANES, 0), (hi, LANES, 0)],
                      [(3 * ATT_W, BF16)], name="qkv_prep")
    return qkv


ATT_T = 512


def _multiplicity(delta):
    ok = delta >= 0
    near = jnp.where(ok & (delta <= 128), 1.0, 0.0)
    mid = jnp.where(ok & (delta <= 512) & ((delta & 3) == 0), 1.0, 0.0)
    far = jnp.where(ok & ((delta & 15) == 0), 1.0, 0.0)
    return near + mid + far


def _attention_bias():
    t = ATT_T
    pos = jnp.arange(t, dtype=jnp.int32)
    delta = jnp.arange(S // t, dtype=jnp.int32)[:, None, None] * t + pos[None, :, None] - pos[None, None, :]
    mult = _multiplicity(delta)
    return jnp.where(mult > 0.0, jnp.log(jnp.maximum(mult, 1.0)), -1e30).astype(F32)


def _head_split(v, first):
    zero = jnp.zeros_like(v)
    return [jnp.where(first, v, zero), jnp.where(first, zero, v)]


def _flash_fwd(qkv, bias):
    t = ATT_T
    n_hp = ATT_W // LANES

    def body(q_ref, k_ref, v_ref, b_ref, o_ref, lse_ref):
        i = pl.program_id(1)
        first = lax.broadcasted_iota(jnp.int32, (1, LANES), 1) < HEAD_DIM
        qs = _head_split(q_ref[...], first)

        def kv_step(j, carry):
            m0, l0, m1, l1, acc = carry
            off = pl.multiple_of(j * t, t)
            kb = k_ref[pl.ds(off, t), :]
            vs = _head_split(v_ref[pl.ds(off, t), :], first)
            bias_t = b_ref[i - j]
            new = []
            pv = None
            for h, (m_prev, l_prev) in enumerate(((m0, l0), (m1, l1))):
                s = lax.dot_general(qs[h], kb, NT, preferred_element_type=F32) + bias_t
                m_new = jnp.maximum(m_prev, jnp.max(s, axis=1, keepdims=True))
                p = jnp.exp(s - m_new)
                alpha = jnp.exp(m_prev - m_new)
                l_new = alpha * l_prev + jnp.sum(p, axis=1, keepdims=True)
                d = lax.dot_general(p.astype(BF16), vs[h], NN, preferred_element_type=F32)
                pv = d if pv is None else pv + d
                new.append((m_new, l_new, alpha))
            acc = acc * jnp.where(first, new[0][2], new[1][2]) + pv
            return new[0][0], new[0][1], new[1][0], new[1][1], acc

        neg = jnp.full((t, 1), -1e30, F32)
        zero = jnp.zeros((t, 1), F32)
        m0, l0, m1, l1, acc = lax.fori_loop(0, i + 1, kv_step, (neg, zero, neg, zero, jnp.zeros((t, LANES), F32)))
        o_ref[...] = acc * jnp.where(first, 1.0 / l0, 1.0 / l1)
        lse_ref[...] = jnp.where(first, m0 + jnp.log(l0), m1 + jnp.log(l1))

    blk = pl.BlockSpec((t, LANES), lambda hp, i: (i, hp))
    k_full = pl.BlockSpec((S, LANES), lambda hp, i: (0, n_hp + hp))
    v_full = pl.BlockSpec((S, LANES), lambda hp, i: (0, 2 * n_hp + hp))
    return pl.pallas_call(
        body, name="flash_fwd", grid=(n_hp, S // t),
        in_specs=[blk, k_full, v_full, pl.BlockSpec((S // t, t, t), lambda hp, i: (0, 0, 0))], out_specs=[blk, blk],
        out_shape=[SDS((S, ATT_W), F32), SDS((S, ATT_W), F32)],
        compiler_params=_cparams(("parallel", "arbitrary")),
    )(qkv, qkv, qkv, bias)


def _flash_bwd(qkv, o, do, lse, bias):
    t = ATT_T
    n_hp = ATT_W // LANES
    n_t = S // t

    def body(q_ref, k_ref, v_ref, o_ref, do_ref, lse_ref, b_ref, dq_ref, dk_ref, dv_ref):
        j = pl.program_id(1)
        first = lax.broadcasted_iota(jnp.int32, (1, LANES), 1) < HEAD_DIM

        @pl.when(j == 0)
        def _():
            dq_ref[...] = jnp.zeros_like(dq_ref)

        kb = k_ref[...]
        vb = v_ref[...]
        ks = _head_split(kb, first)

        def q_step(i, carry):
            dk_acc, dv_acc = carry
            rows = pl.ds(pl.multiple_of(i * t, t), t)
            qs = _head_split(q_ref[rows, :], first)
            dob = do_ref[rows, :]
            prod = dob * o_ref[rows, :]
            d_all = jnp.sum(prod, axis=1, keepdims=True)
            d0 = jnp.sum(jnp.where(first, prod, 0.0), axis=1, keepdims=True)
            lse_b = lse_ref[rows, :]
            lse0 = jnp.max(jnp.where(first, lse_b, -jnp.inf), axis=1, keepdims=True)
            lse1 = jnp.max(jnp.where(first, -jnp.inf, lse_b), axis=1, keepdims=True)
            dos = _head_split(dob.astype(BF16), first)
            bias_t = b_ref[i - j]
            dq_t = jnp.zeros((t, LANES), F32)
            for h, (lse_h, d_h) in enumerate(((lse0, d0), (lse1, d_all - d0))):
                s = lax.dot_general(qs[h], kb, NT, preferred_element_type=F32)
                p = jnp.exp(s + (bias_t - lse_h))
                dp = lax.dot_general(dos[h], vb, NT, preferred_element_type=F32)
                ds = (p * (dp - d_h)).astype(BF16)
                dv_acc = dv_acc + lax.dot_general(p.astype(BF16), dos[h], TN, preferred_element_type=F32)
                dk_acc = dk_acc + lax.dot_general(ds, qs[h], TN, preferred_element_type=F32)
                dq_t = dq_t + lax.dot_general(ds, ks[h], NN, preferred_element_type=F32)
            dq_ref[rows, :] += dq_t
            return dk_acc, dv_acc

        zero = jnp.zeros((t, LANES), F32)
        dk_acc, dv_acc = lax.fori_loop(j, n_t, q_step, (zero, zero))
        dk_ref[...] = dk_acc
        dv_ref[...] = dv_acc

    blk = pl.BlockSpec((t, LANES), lambda hp, j: (j, hp))
    full = pl.BlockSpec((S, LANES), lambda hp, j: (0, hp))
    k_blk = pl.BlockSpec((t, LANES), lambda hp, j: (j, n_hp + hp))
    v_blk = pl.BlockSpec((t, LANES), lambda hp, j: (j, 2 * n_hp + hp))
    return pl.pallas_call(
        body, name="flash_bwd", grid=(n_hp, n_t),
        in_specs=[full, k_blk, v_blk, full, full, full, pl.BlockSpec((n_t, t, t), lambda hp, j: (0, 0, 0))],
        out_specs=[full, blk, blk],
        out_shape=[SDS((S, ATT_W), F32)] * 3,
        compiler_params=_cparams(("parallel", "arbitrary")),
    )(qkv, qkv, qkv, o, do, lse, bias)


SCAN_T = 256
ST_ROWS = 2 * N_CPLX // LANES
HALF = ST_ROWS // 2


def _scan_fwd(lam, bu):
    def body(lam_ref, bu_ref, st_ref, carry):
        @pl.when(pl.program_id(0) == 0)
        def _():
            carry[...] = jnp.zeros_like(carry)

        ar, ai = lam_ref[0:HALF, :], lam_ref[HALF:ST_ROWS, :]

        def step(t, c):
            sr, si = c
            b = bu_ref[t]
            nr = ar * sr - ai * si + b[0:HALF]
            ni = ar * si + ai * sr + b[HALF:ST_ROWS]
            st_ref[t, 0:HALF, :] = nr
            st_ref[t, HALF:ST_ROWS, :] = ni
            return nr, ni

        sr, si = lax.fori_loop(0, SCAN_T, step, (carry[0:HALF, :], carry[HALF:ST_ROWS, :]), unroll=8)
        carry[0:HALF, :] = sr
        carry[HALF:ST_ROWS, :] = si

    blk = pl.BlockSpec((SCAN_T, ST_ROWS, LANES), lambda i: (i, 0, 0))
    return pl.pallas_call(
        body, name="scan_fwd", grid=(S // SCAN_T,),
        in_specs=[pl.BlockSpec((ST_ROWS, LANES), lambda i: (0, 0)), blk], out_specs=blk,
        out_shape=SDS((S, ST_ROWS, LANES), F32),
        scratch_shapes=[pltpu.VMEM((ST_ROWS, LANES), F32)],
        compiler_params=_cparams(("arbitrary",)),
    )(lam, bu)


def _scan_bwd(lam, dst, states):
    n_blk = S // SCAN_T

    def body(lam_ref, d_ref, st_ref, g_ref, dlam_ref, carry):
        @pl.when(pl.program_id(0) == 0)
        def _():
            carry[...] = jnp.zeros_like(carry)
            dlam_ref[...] = jnp.zeros_like(dlam_ref)

        ar, ai = lam_ref[0:HALF, :], lam_ref[HALF:ST_ROWS, :]

        def step(kk, c):
            t = SCAN_T - 1 - kk
            gr, gi, dar, dai = c
            x = st_ref[t]
            xr, xi = x[0:HALF], x[HALF:ST_ROWS]
            dar = dar + gr * xr + gi * xi
            dai = dai + gi * xr - gr * xi
            d = d_ref[t]
            ngr = d[0:HALF] + ar * gr + ai * gi
            ngi = d[HALF:ST_ROWS] + ar * gi - ai * gr
            g_ref[t, 0:HALF, :] = ngr
            g_ref[t, HALF:ST_ROWS, :] = ngi
            return ngr, ngi, dar, dai

        zero = jnp.zeros((HALF, LANES), F32)
        gr, gi, dar, dai = lax.fori_loop(0, SCAN_T, step, (carry[0:HALF, :], carry[HALF:ST_ROWS, :], zero, zero),
                                         unroll=8)
        carry[0:HALF, :] = gr
        carry[HALF:ST_ROWS, :] = gi
        dlam_ref[0:HALF, :] += dar
        dlam_ref[HALF:ST_ROWS, :] += dai

    blk = pl.BlockSpec((SCAN_T, ST_ROWS, LANES), lambda i: (n_blk - 1 - i, 0, 0))
    small = pl.BlockSpec((ST_ROWS, LANES), lambda i: (0, 0))
    return pl.pallas_call(
        body, name="scan_bwd", grid=(n_blk,),
        in_specs=[small, blk, blk], out_specs=[blk, small],
        out_shape=[SDS((S, ST_ROWS, LANES), F32), SDS((ST_ROWS, LANES), F32)],
        scratch_shapes=[pltpu.VMEM((ST_ROWS, LANES), F32)],
        compiler_params=_cparams(("arbitrary",)),
    )(lam, dst, states)


def _ssm_prep(a_re, a_im, log_dt, b_re, b_im, c_re, c_im):
    lam = lax.complex(a_re, a_im)
    dt = jnp.exp(log_dt)[:, None]
    lam_bar = jnp.exp(lam * dt)
    b_bar = ((lam_bar - 1.0) / lam)[..., None] * lax.complex(b_re, b_im)
    lam_t = jnp.concatenate([jnp.real(lam_bar).reshape(HALF, LANES), jnp.imag(lam_bar).reshape(HALF, LANES)], axis=0)
    on_diag = (lax.broadcasted_iota(jnp.int32, (SSM_W, N_CPLX), 0) // SSM_GROUP
               == lax.broadcasted_iota(jnp.int32, (SSM_W, N_CPLX), 1) // SSM_STATE)

    def block_diag(m):
        return jnp.where(on_diag, jnp.tile(m.reshape(SSM_W, SSM_STATE), (1, SSM_GROUPS)), 0.0)

    w_b = jnp.concatenate([block_diag(jnp.real(b_bar).transpose(0, 2, 1)),
                           block_diag(jnp.imag(b_bar).transpose(0, 2, 1))], axis=1)
    w_ct = jnp.concatenate([block_diag(c_re), -block_diag(c_im)], axis=1)
    return lam_t, w_b, w_ct


U_SSM_COLS = (4 * ATT_W, SSM_W)


def _row(v):
    return v.reshape(1, -1)


def _even_fwd(x, pre, post, w_in, w_out, glu_w, glu_b, ssm_d, prep, tables):
    lam_t, w_b, w_ct = prep
    h = _norm_fwd(x, pre)
    proj = _mm(h, w_in, "nn", F32, b_blocks=True)
    qkv = _qkv_prep(proj, tables[:3])
    att, lse = _flash_fwd(qkv, tables[3])
    bu = _mm(proj, w_b, "nn", F32, a_cols=U_SSM_COLS)
    states = _scan_fwd(lam_t, bu.reshape(S, ST_ROWS, LANES))
    y = _mm(states.reshape(S, 2 * N_CPLX), w_ct, "nt", F32)

    def act1(yv, uv, dv):
        return (_gelu_and_grad(yv + dv * uv)[0],), ()

    (z1,) = _rowwise(act1, [(y, SSM_W, 0), (proj, SSM_W, 8), (ssm_d, SSM_W, 0)], [(SSM_W, F32)], name="ssm_act_fwd")
    lin = _mm(z1, glu_w, "nn", F32)

    def gate(att_v, ga, gs, z1v, linv, bv):
        ssm_out = z1v * _sigmoid(linv + bv)
        return (jnp.concatenate([att_v * _silu_and_grad(ga)[0], ssm_out * _silu_and_grad(gs)[0]], axis=1),), ()

    (merged,) = _rowwise(gate, [(att, ATT_W, 0), (proj, ATT_W, 3), (proj, SSM_W, 9), (z1, SSM_W, 0),
                                (lin, SSM_W, 0), (glu_b, SSM_W, 0)], [(EVEN_OUT, BF16)], name="even_gate_fwd")
    yout = _mm(merged, w_out, "nn", F32)
    x_next = _post_fwd(x, yout, post)
    saved = (x, h, proj, qkv, att, lse, states, y, z1, lin, merged, yout)
    return x_next, saved


def _even_bwd(g, saved, pre, post, w_in, w_out, glu_w, glu_b, ssm_d, prep, tables):
    x, h, proj, qkv, att, lse, states, y, z1, lin, merged, yout = saved
    lam_t, w_b, w_ct = prep
    dyout, dpost = _post_bwd(g, yout, post)
    dmerged = _mm(dyout, w_out, "nt", F32)
    dw_out = _mm(merged, dyout, "tn", BF16)

    def gate_bwd(dm_a, dm_s, att_v, ga, gs, z1v, linv, bv):
        sa, dsa = _silu_and_grad(ga)
        ss, dss = _silu_and_grad(gs)
        sig = _sigmoid(linv + bv)
        ssm_out = z1v * sig
        dssm = dm_s * ss
        dlin = dssm * z1v * sig * (1.0 - sig)
        return (dm_a * sa, dm_a * att_v * dsa, dm_s * ssm_out * dss, dssm * sig, dlin), (dlin,)

    datt, dg_att, dg_ssm, dz1a, dlin, dglu_b = _rowwise(
        gate_bwd, [(dmerged, ATT_W, 0), (dmerged, SSM_W, 2), (att, ATT_W, 0), (proj, ATT_W, 3), (proj, SSM_W, 9),
                   (z1, SSM_W, 0), (lin, SSM_W, 0), (glu_b, SSM_W, 0)],
        [(ATT_W, F32), (ATT_W, BF16), (SSM_W, BF16), (SSM_W, F32), (SSM_W, BF16)], [SSM_W], name="even_gate_bwd")
    dz1b = _mm(dlin, glu_w, "nt", F32)
    dglu_w = _mm(z1, dlin, "tn", BF16)

    def act1_bwd(da, db, yv, uv, dv):
        dpre = (da + db) * _gelu_and_grad(yv + dv * uv)[1]
        return (dpre, dpre * dv), (dpre * uv,)

    dy, du_direct, dd = _rowwise(act1_bwd, [(dz1a, SSM_W, 0), (dz1b, SSM_W, 0), (y, SSM_W, 0), (proj, SSM_W, 8),
                                            (ssm_d, SSM_W, 0)], [(SSM_W, BF16), (SSM_W, F32)], [SSM_W],
                                 name="ssm_act_bwd")
    dst = _mm(dy, w_ct, "nn", F32)
    dw_ct = _mm(dy, states.reshape(S, 2 * N_CPLX), "tn", F32)
    gst, dlam = _scan_bwd(lam_t, dst.reshape(S, ST_ROWS, LANES), states)
    dbu = gst.reshape(S, 2 * N_CPLX)
    du_state = _mm(dbu, w_b, "nt", F32)
    dw_b = _mm(proj, dbu, "tn", F32, a_cols=U_SSM_COLS)
    dq, dk, dv = _flash_bwd(qkv, att, datt, lse, tables[3])

    def assemble(dqv, dkv, dvv, dga, dua, dub, dgs, c, l, h):
        rot = _rotate(jnp.concatenate([dqv, dkv], axis=1), c, l, h, True)
        return (jnp.concatenate([(rot[:, :ATT_W] * HEAD_DIM ** -0.5).astype(BF16), rot[:, ATT_W:].astype(BF16),
                                 dvv.astype(BF16), dga, (dua + dub).astype(BF16), dgs], axis=1),), ()

    (dproj,) = _rowwise(assemble, [(dq, ATT_W, 0), (dk, ATT_W, 0), (dv, ATT_W, 0), (dg_att, ATT_W, 0),
                                   (du_state, SSM_W, 0), (du_direct, SSM_W, 0), (dg_ssm, SSM_W, 0),
                                   (tables[0], LANES, 0), (tables[1], LANES, 0), (tables[2], LANES, 0)],
                        [(EVEN_IN, BF16)], name="dproj_assemble")
    dw_in = _mm(h, dproj, "tn", BF16, out_blocks=True)
    dh = _mm(dproj, w_in, "nt", F32, b_blocks=True)
    g_prev, dpre = _pre_bwd(g, dh, x, pre)
    return g_prev, dict(pre=dpre, post=dpost, w_in=dw_in, w_out=dw_out, glu_w=dglu_w, glu_b=dglu_b, ssm_d=dd,
                        prep=(dlam, dw_b, dw_ct))


def _odd_fwd(x, pre, post, w_in, pool_w, pool_scale, w_out):
    h = _norm_fwd(x, pre)
    proj = _mm(h, w_in, "nn", F32, b_blocks=True)
    mixed = _pool(proj, 0, False, BF16)
    ylin = _gmm(mixed, pool_w, "nn", F32)

    def gate(yl, gt, sc):
        return (yl * sc * _silu_and_grad(gt)[0],), ()

    (z,) = _rowwise(gate, [(ylin, POOL_W, 0), (proj, POOL_W, 1), (pool_scale, POOL_W, 0)], [(POOL_W, BF16)],
                    name="odd_gate_fwd")
    yout = _mm(z, w_out, "nn", F32)
    x_next = _post_fwd(x, yout, post)
    return x_next, (x, h, proj, mixed, ylin, z, yout)


def _odd_bwd(g, saved, pre, post, w_in, pool_w, pool_scale, w_out):
    x, h, proj, mixed, ylin, z, yout = saved
    dyout, dpost = _post_bwd(g, yout, post)
    dz = _mm(dyout, w_out, "nt", F32)
    dw_out = _mm(z, dyout, "tn", BF16)

    def gate_bwd(dzv, yl, gt, sc):
        sg, dsg = _silu_and_grad(gt)
        tt = dzv * sg
        return (tt * sc, dzv * yl * sc * dsg), (tt * yl,)

    dylin, dproj_gate, dscale = _rowwise(gate_bwd, [(dz, POOL_W, 0), (ylin, POOL_W, 0), (proj, POOL_W, 1),
                                                    (pool_scale, POOL_W, 0)],
                                         [(POOL_W, BF16), (POOL_W, BF16, ODD_IN, 1)], [POOL_W], name="odd_gate_bwd")
    dmixed = _gmm(dylin, pool_w, "nt", F32)
    dpool_w = _gmm(mixed, dylin, "tn", BF16)
    dproj = _pool(dmixed, 0, True, BF16, into=dproj_gate)
    dw_in = _mm(h, dproj, "tn", BF16, out_blocks=True)
    dh = _mm(dproj, w_in, "nt", F32, b_blocks=True)
    g_prev, dpre = _pre_bwd(g, dh, x, pre)
    return g_prev, dict(pre=dpre, post=dpost, w_in=dw_in, w_out=dw_out, pool_w=dpool_w, pool_scale=dscale)


def _my_index():
    return 4 * lax.axis_index("x") + 2 * lax.axis_index("y") + lax.axis_index("c")


def _exchange(arrs, gather, name, after=()):
    n = len(arrs)
    out_shape = [SDS((N_DEV,) + a.shape, a.dtype) if gather else SDS(a.shape, a.dtype) for a in arrs]

    def body(*refs):
        ins, outs = refs[:n], refs[n + len(after):2 * n + len(after)]
        send_sems, recv_sems, local_sems = refs[2 * n + len(after):]
        me = _my_index()

        def src(i, j):
            return ins[i] if gather else ins[i].at[j]

        def remote(i, j, src_slot, dst_slot, recv_slot):
            return pltpu.make_async_remote_copy(
                src_ref=src(i, src_slot), dst_ref=outs[i].at[dst_slot], send_sem=send_sems.at[i, j],
                recv_sem=recv_sems.at[i, recv_slot], device_id=(j // 4, (j // 2) % 2, j % 2), device_id_type=MESH_ID)

        def local(i):
            return pltpu.make_async_copy(src(i, me), outs[i].at[me], local_sems.at[i])

        for i in range(n):
            local(i).start()
        for j in range(N_DEV):
            @pl.when(me != j)
            def _(j=j):
                for i in range(n):
                    remote(i, j, j, me, me).start()
        for j in range(N_DEV):
            @pl.when(me != j)
            def _(j=j):
                for i in range(n):
                    remote(i, j, j, me, me).wait_send()
                    remote(i, j, j, j, j).wait_recv()
        for i in range(n):
            local(i).wait()

    any_spec = pl.BlockSpec(memory_space=pl.ANY)
    return pl.pallas_call(
        body, name=name, in_specs=[any_spec] * (n + len(after)), out_specs=[any_spec] * n, out_shape=out_shape,
        scratch_shapes=[pltpu.SemaphoreType.DMA((n, N_DEV)), pltpu.SemaphoreType.DMA((n, N_DEV)),
                        pltpu.SemaphoreType.DMA((n,))],
    )(*arrs, *after)


HBM_SPEC = pl.BlockSpec(memory_space=pltpu.HBM)
SEM_SPEC = pl.BlockSpec(memory_space=pltpu.SEMAPHORE)
SPLIT_EFFECT = pltpu.SideEffectType.DATAFLOW_SIDE_EFFECTING


def _device_of(j):
    return (j // 4, (j // 2) % 2, j % 2)


def _split_copy(srcs, lands, send_sems, recv_sems, gather, i, j, dst_slot, recv_slot):
    return pltpu.make_async_remote_copy(
        src_ref=srcs[i] if gather else srcs[i].at[j], dst_ref=lands[i].at[dst_slot],
        send_sem=send_sems.at[i * N_DEV + j], recv_sem=recv_sems.at[i * N_DEV + recv_slot],
        device_id=_device_of(j), device_id_type=MESH_ID)


def _own_copy(srcs, lands, send_sems, gather, i, me):
    return pltpu.make_async_copy(srcs[i] if gather else srcs[i].at[me], lands[i].at[me], send_sems.at[i * N_DEV + me])


def _xchg_start(name, srcs, gather, after=()):
    n = len(srcs)
    n_in = n + len(after)

    def body(*refs):
        src_refs = refs[:n]
        send_sems, recv_sems, token = refs[n_in], refs[n_in + 1], refs[-1]
        land_refs = refs[n_in + 2 + n:n_in + 2 + 2 * n]
        me = _my_index()
        for j in range(N_DEV):
            @pl.when(me != j)
            def _(j=j):
                for i in range(n):
                    _split_copy(src_refs, land_refs, send_sems, recv_sems, gather, i, j, me, me).start()
        for i in range(n):
            _own_copy(src_refs, land_refs, send_sems, gather, i, me).start()
        token[...] = jnp.zeros_like(token)

    land_shapes = [((N_DEV,) + a.shape) if gather else a.shape for a in srcs]
    thru = ([pltpu.HBM(a.shape, a.dtype) for a in srcs] + [pltpu.HBM(s, a.dtype) for s, a in zip(land_shapes, srcs)])
    res = pl.pallas_call(
        body, name=name,
        out_shape=(pltpu.SemaphoreType.DMA((n * N_DEV,)), pltpu.SemaphoreType.DMA((n * N_DEV,)), *thru,
                   SDS((8, LANES), F32)),
        in_specs=[HBM_SPEC] * n + [pl.BlockSpec(memory_space=pl.ANY)] * len(after),
        out_specs=(SEM_SPEC, SEM_SPEC, *([HBM_SPEC] * (2 * n)), pl.BlockSpec(memory_space=pltpu.VMEM)),
        input_output_aliases={i: 2 + i for i in range(n)},
        compiler_params=pltpu.CompilerParams(has_side_effects=SPLIT_EFFECT),
    )(*[pltpu.with_memory_space_constraint(a, pltpu.HBM) for a in srcs], *after)
    return res[0], res[1], list(res[2:2 + n]), list(res[2 + n:2 + 2 * n]), res[-1]


def _xchg_wait(name, started, gather, after):
    send_sems, recv_sems, srcs, lands, _ = started
    n = len(srcs)

    def body(*refs):
        src_refs, land_refs = refs[:n], refs[n:2 * n]
        send_r, recv_r = refs[2 * n], refs[2 * n + 1]
        me = _my_index()
        for j in range(N_DEV):
            @pl.when(me != j)
            def _(j=j):
                for i in range(n):
                    _split_copy(src_refs, land_refs, send_r, recv_r, gather, i, j, me, me).wait_send()
                    _split_copy(src_refs, land_refs, send_r, recv_r, gather, i, j, j, j).wait_recv()
        for i in range(n):
            _own_copy(src_refs, land_refs, send_r, gather, i, me).wait()

    thru = [pltpu.HBM(a.shape, a.dtype) for a in list(srcs) + list(lands)]
    res = pl.pallas_call(
        body, name=name, out_shape=tuple(thru),
        in_specs=[HBM_SPEC] * (2 * n) + [SEM_SPEC, SEM_SPEC] + [pl.BlockSpec(memory_space=pl.ANY)] * len(after),
        out_specs=tuple([HBM_SPEC] * (2 * n)),
        input_output_aliases={i: i for i in range(2 * n)},
        compiler_params=pltpu.CompilerParams(has_side_effects=SPLIT_EFFECT),
    )(*srcs, *lands, send_sems, recv_sems, *after)
    return list(res[n:])


def _adam_layers(w, slot_list, m, v, name):
    n_l, r, c = w.shape
    ns = slot_list[0].shape[0]
    tr = r
    while tr * c * 4 > (1 << 20) and tr % 16 == 0:
        tr //= 2
    assert r % tr == 0 and len(slot_list) == n_l

    def body(*refs):
        w_ref, slot_refs = refs[0], refs[1:1 + n_l]
        m_ref, v_ref, go_ref, d_ref, mo_ref, vo_ref = refs[1 + n_l:]
        layer = pl.program_id(0)
        g = None
        for l, g_ref in enumerate(slot_refs):
            gl = g_ref[0].astype(F32)
            for s in range(1, ns):
                gl = gl + g_ref[s].astype(F32)
            g = gl if g is None else jnp.where(layer == l, gl, g)
        mn = ADAM_B1 * m_ref[...] + (1.0 - ADAM_B1) * g
        vn = ADAM_B2 * v_ref[...] + (1.0 - ADAM_B2) * (g * g)
        m_hat = mn / (1.0 - ADAM_B1 ** ADAM_STEP)
        v_hat = vn / (1.0 - ADAM_B2 ** ADAM_STEP)
        go_ref[...] = g
        d_ref[...] = -ADAM_LR * (m_hat / (jnp.sqrt(v_hat) + ADAM_EPS) + ADAM_WD * w_ref[...])
        mo_ref[...] = mn
        vo_ref[...] = vn

    blk = pl.BlockSpec((None, tr, c), lambda l, i: (l, i, 0))
    slot_specs = [pl.BlockSpec((ns, tr, c), lambda l, i, k=k: (0, jnp.where(l == k, i, 0), 0)) for k in range(n_l)]
    return pl.pallas_call(
        body, name=name, grid=(n_l, r // tr),
        in_specs=[blk] + slot_specs + [blk, blk],
        out_specs=[blk] * 4, out_shape=[SDS((n_l, r, c), F32)] * 4,
        compiler_params=_cparams(("arbitrary", "arbitrary")),
    )(w, *slot_list, m, v)


def _adam(w, gslots, m, v, name):
    r, c = w.shape
    ns = gslots.shape[0]
    tr = r
    while tr * c * 4 > (1 << 20) and tr % 16 == 0:
        tr //= 2
    assert r % tr == 0

    def body(w_ref, g_ref, m_ref, v_ref, go_ref, d_ref, mo_ref, vo_ref):
        g = g_ref[0].astype(F32)
        for s in range(1, ns):
            g = g + g_ref[s].astype(F32)
        wv = w_ref[...]
        mn = ADAM_B1 * m_ref[...] + (1.0 - ADAM_B1) * g
        vn = ADAM_B2 * v_ref[...] + (1.0 - ADAM_B2) * (g * g)
        m_hat = mn / (1.0 - ADAM_B1 ** ADAM_STEP)
        v_hat = vn / (1.0 - ADAM_B2 ** ADAM_STEP)
        go_ref[...] = g
        d_ref[...] = -ADAM_LR * (m_hat / (jnp.sqrt(v_hat) + ADAM_EPS) + ADAM_WD * wv)
        mo_ref[...] = mn
        vo_ref[...] = vn

    blk = pl.BlockSpec((tr, c), lambda i: (i, 0))
    return pl.pallas_call(
        body, name=name, grid=(r // tr,),
        in_specs=[blk, pl.BlockSpec((ns, tr, c), lambda i: (0, i, 0)), blk, blk],
        out_specs=[blk] * 4, out_shape=[SDS((r, c), F32)] * 4,
        compiler_params=_cparams(("parallel",)),
    )(w, gslots, m, v)


SMALL_NAMES = ("pre_norm", "post_norm", "ssm_a_re", "ssm_a_im", "ssm_log_dt", "ssm_b_re", "ssm_b_im", "ssm_c_re",
               "ssm_c_im", "ssm_d", "ssm_glu_b")
SHARDED_NAMES = ("even_w_in", "even_w_out", "ssm_glu_w", "odd_w_in", "pool_w", "odd_w_out")
WEIGHT_ORDER = ("pre_norm", "post_norm", "even_w_in", "even_w_out", "ssm_a_re", "ssm_a_im", "ssm_log_dt", "ssm_b_re",
                "ssm_b_im", "ssm_c_re", "ssm_c_im", "ssm_d", "ssm_glu_w", "ssm_glu_b", "odd_w_in", "pool_w",
                "pool_scale", "odd_w_out")
PACK_ROWS_ALIGN = 8


def _pack(parts):
    flat = jnp.concatenate([p.reshape(-1).astype(F32) for p in parts])
    rows = -(-flat.shape[0] // (LANES * PACK_ROWS_ALIGN)) * PACK_ROWS_ALIGN
    return jnp.pad(flat, (0, rows * LANES - flat.shape[0])).reshape(rows, LANES)


def _unpack(packed, shapes):
    flat = packed.reshape(-1)
    out, off = [], 0
    for shp in shapes:
        size = math.prod(shp)
        out.append(flat[off:off + size].reshape(shp))
        off += size
    return out


EVEN_SHARDED = ("w_in", "w_out", "glu_w")
ODD_SHARDED = ("w_in", "pool_w", "w_out")
FAMILY = {(0, "w_in"): "even_w_in", (0, "w_out"): "even_w_out", (0, "glu_w"): "ssm_glu_w",
          (1, "w_in"): "odd_w_in", (1, "pool_w"): "pool_w", (1, "w_out"): "odd_w_out"}


def _sharded_keys(layer):
    return EVEN_SHARDED if layer % 2 == 0 else ODD_SHARDED


def _local_step(x, tgt, small, get_weights, on_grads):
    tables = _rope_tables() + (_attention_bias(),)
    preps, prep_vjps = [], []
    for i in range(2):
        out, vjp = jax.vjp(_ssm_prep, small["ssm_a_re"][i], small["ssm_a_im"][i], small["ssm_log_dt"][i],
                           small["ssm_b_re"][i], small["ssm_b_im"][i], small["ssm_c_re"][i], small["ssm_c_im"][i])
        preps.append(out)
        prep_vjps.append(vjp)

    def layer_args(layer, wts):
        i = layer // 2
        pre, post = _row(small["pre_norm"][layer]) + wts.get("token", 0.0), _row(small["post_norm"][layer])
        if layer % 2 == 0:
            return (pre, post, wts["w_in"], wts["w_out"], wts["glu_w"], _row(small["ssm_glu_b"][i]),
                    _row(small["ssm_d"][i]), preps[i], tables)
        return (pre, post, wts["w_in"], wts["pool_w"], _row(wts["pool_scale"]), wts["w_out"])

    saved, args = [], []
    cur = x
    for layer in range(4):
        after = (cur,) if layer else (cur, tables[0], tables[3], preps[0][1], preps[0][2], preps[1][1], preps[1][2])
        args.append(layer_args(layer, get_weights(layer, after)))
        cur, sv = (_even_fwd if layer % 2 == 0 else _odd_fwd)(cur, *args[layer])
        saved.append(sv)
    g, sq = _loss_grad(cur, tgt)
    loss = 0.5 * jnp.sum(sq) / D

    lg = [None] * 4
    token = jnp.zeros((), F32)
    for layer in reversed(range(4)):
        largs = list(args[layer])
        largs[1] = largs[1] + token
        g, lg[layer] = (_even_bwd if layer % 2 == 0 else _odd_bwd)(g, saved[layer], *largs)
        token = on_grads(layer, {k: lg[layer][k] for k in _sharded_keys(layer)})

    ssm_g = [prep_vjps[i](lg[2 * i]["prep"]) for i in range(2)]
    grads = {
        "pre_norm": jnp.concatenate([lg[l]["pre"] for l in range(4)], axis=0),
        "post_norm": jnp.concatenate([lg[l]["post"] for l in range(4)], axis=0),
        "ssm_d": jnp.concatenate([lg[0]["ssm_d"], lg[2]["ssm_d"]], axis=0),
        "ssm_glu_b": jnp.concatenate([lg[0]["glu_b"], lg[2]["glu_b"]], axis=0),
        "pool_scale": jnp.concatenate([lg[1]["pool_scale"], lg[3]["pool_scale"]], axis=0),
    }
    for idx, nm in enumerate(("ssm_a_re", "ssm_a_im", "ssm_log_dt", "ssm_b_re", "ssm_b_im", "ssm_c_re", "ssm_c_im")):
        grads[nm] = jnp.stack([ssm_g[0][idx], ssm_g[1][idx]])
    return loss, g, grads, token


def _to_slots(key, gfull):
    if key == "w_in":
        return gfull
    if key in ("w_out", "glu_w"):
        rr, nn = gfull.shape
        return gfull.reshape(N_DEV, rr // N_DEV, nn)
    assert key == "pool_w"
    gg, rr, nn = gfull.shape
    return gfull.reshape(gg, N_DEV, rr // N_DEV, nn).transpose(1, 0, 2, 3)


def _from_gathered(key, gat):
    if key == "w_in":
        return gat
    if key in ("w_out", "glu_w"):
        _, rr, nn = gat.shape
        return gat.reshape(N_DEV * rr, nn)
    assert key == "pool_w"
    _, gg, rr, nn = gat.shape
    return gat.transpose(1, 0, 2, 3).reshape(gg, N_DEV * rr, nn)


def kernel(x, pre_norm, post_norm, even_w_in, even_w_out, ssm_a_re, ssm_a_im, ssm_log_dt, ssm_b_re, ssm_b_im, ssm_c_re, ssm_c_im, ssm_d, ssm_glu_w, ssm_glu_b, odd_w_in, pool_w, pool_scale, odd_w_out, loss_target, m_pre_norm, m_post_norm, m_even_w_in, m_even_w_out, m_ssm_a_re, m_ssm_a_im, m_ssm_log_dt, m_ssm_b_re, m_ssm_b_im, m_ssm_c_re, m_ssm_c_im, m_ssm_d, m_ssm_glu_w, m_ssm_glu_b, m_odd_w_in, m_pool_w, m_pool_scale, m_odd_w_out, v_pre_norm, v_post_norm, v_even_w_in, v_even_w_out, v_ssm_a_re, v_ssm_a_im, v_ssm_log_dt, v_ssm_b_re, v_ssm_b_im, v_ssm_c_re, v_ssm_c_im, v_ssm_d, v_ssm_glu_w, v_ssm_glu_b, v_odd_w_in, v_pool_w, v_pool_scale, v_odd_w_out):
    w = dict(pre_norm=pre_norm, post_norm=post_norm, even_w_in=even_w_in, even_w_out=even_w_out, ssm_a_re=ssm_a_re,
             ssm_a_im=ssm_a_im, ssm_log_dt=ssm_log_dt, ssm_b_re=ssm_b_re, ssm_b_im=ssm_b_im, ssm_c_re=ssm_c_re,
             ssm_c_im=ssm_c_im, ssm_d=ssm_d, ssm_glu_w=ssm_glu_w, ssm_glu_b=ssm_glu_b, odd_w_in=odd_w_in,
             pool_w=pool_w, pool_scale=pool_scale, odd_w_out=odd_w_out)
    mom = dict(pre_norm=m_pre_norm, post_norm=m_post_norm, even_w_in=m_even_w_in, even_w_out=m_even_w_out,
               ssm_a_re=m_ssm_a_re, ssm_a_im=m_ssm_a_im, ssm_log_dt=m_ssm_log_dt, ssm_b_re=m_ssm_b_re,
               ssm_b_im=m_ssm_b_im, ssm_c_re=m_ssm_c_re, ssm_c_im=m_ssm_c_im, ssm_d=m_ssm_d, ssm_glu_w=m_ssm_glu_w,
               ssm_glu_b=m_ssm_glu_b, odd_w_in=m_odd_w_in, pool_w=m_pool_w, pool_scale=m_pool_scale,
               odd_w_out=m_odd_w_out)
    var = dict(pre_norm=v_pre_norm, post_norm=v_post_norm, even_w_in=v_even_w_in, even_w_out=v_even_w_out,
               ssm_a_re=v_ssm_a_re, ssm_a_im=v_ssm_a_im, ssm_log_dt=v_ssm_log_dt, ssm_b_re=v_ssm_b_re,
               ssm_b_im=v_ssm_b_im, ssm_c_re=v_ssm_c_re, ssm_c_im=v_ssm_c_im, ssm_d=v_ssm_d, ssm_glu_w=v_ssm_glu_w,
               ssm_glu_b=v_ssm_glu_b, odd_w_in=v_odd_w_in, pool_w=v_pool_w, pool_scale=v_pool_scale,
               odd_w_out=v_odd_w_out)
    me = _my_index()
    scale_cols = pool_scale.shape[1]

    def start_gather(layer, after=()):
        i = layer // 2
        shards = [w[FAMILY[(layer % 2, k)]][i].astype(BF16) for k in _sharded_keys(layer)]
        if layer % 2 == 1:
            shards.append(jnp.pad(pool_scale[i][None], ((0, PACK_ROWS_ALIGN - 1), (0, 0))))
        return _xchg_start(f"gather_start_{layer}", shards, True, after)

    gather_started = {0: start_gather(0)}
    small = {nm: w[nm] for nm in SMALL_NAMES}

    def get_weights(layer, after):
        lands = _xchg_wait(f"gather_wait_{layer}", gather_started[layer], True, after)
        wts = {k: _from_gathered(k, gat) for k, gat in zip(_sharded_keys(layer), lands)}
        if layer % 2 == 1:
            wts["pool_scale"] = lands[-1][:, 0, :].reshape(N_DEV * scale_cols)
        if layer == 0:
            for later in (1, 2, 3):
                gather_started[later] = start_gather(later, after=(lands[0],))
            wts["token"] = sum(gather_started[later][4][0, 0] for later in (1, 2, 3))
        return wts

    scatter_started = [None] * 4

    def on_grads(layer, lg):
        slots = [_to_slots(k, lg[k]) for k in _sharded_keys(layer)]
        scatter_started[layer] = _xchg_start(f"scatter_start_{layer}", slots, False)
        return scatter_started[layer][4][0, 0]

    loss_local, grad_x, grads, token = _local_step(x[0], loss_target[0], small, get_weights, on_grads)

    def adam_family(parity, k):
        nm = FAMILY[(parity, k)]
        shp = w[nm].shape
        cols = shp[-1]
        slot_list = [recv[(parity + 2 * i, k)].reshape(N_DEV, -1, cols) for i in range(2)]
        outs = _adam_layers(w[nm].reshape(2, -1, cols), slot_list, mom[nm].reshape(2, -1, cols),
                            var[nm].reshape(2, -1, cols), name=f"adam_{nm}")
        return [o.reshape(shp) for o in outs]

    recv, res = {}, {}
    for layer in (3, 1):
        lands = _xchg_wait(f"scatter_wait_{layer}", scatter_started[layer], False, (scatter_started[0][4],))
        for k, land in zip(_sharded_keys(layer), lands):
            recv[(layer, k)] = land
    for k in ODD_SHARDED:
        res[FAMILY[(1, k)]] = adam_family(1, k)

    extra_parts = [grads["pool_scale"], loss_local.reshape(1)]
    extra_zeros = [jnp.zeros(p.shape, F32) for p in extra_parts]
    small_parts = [grads[nm] for nm in SMALL_NAMES] + extra_parts
    small_shapes = [w[nm].shape for nm in SMALL_NAMES] + [p.shape for p in extra_parts]
    (small_slots,) = _exchange([_pack(small_parts) + token], True, "gather_small_grads",
                               after=tuple(res[FAMILY[(1, k)]][0] for k in ODD_SHARDED))

    for layer in (2, 0):
        lands = _xchg_wait(f"scatter_wait_{layer}", scatter_started[layer], False, (small_slots,))
        for k, land in zip(_sharded_keys(layer), lands):
            recv[(layer, k)] = land
    for k in EVEN_SHARDED:
        res[FAMILY[(0, k)]] = adam_family(0, k)
    w_pack = _pack([w[nm] for nm in SMALL_NAMES] + extra_zeros)
    m_pack = _pack([mom[nm] for nm in SMALL_NAMES] + extra_zeros)
    v_pack = _pack([var[nm] for nm in SMALL_NAMES] + extra_zeros)
    outs = _adam(w_pack, small_slots, m_pack, v_pack, name="adam_small")
    unpacked = [_unpack(o, small_shapes) for o in outs]
    for idx, nm in enumerate(SMALL_NAMES):
        res[nm] = [unpacked[kind][idx] for kind in range(4)]
    loss = unpacked[0][-1].reshape(())
    g_scale = lax.dynamic_slice_in_dim(unpacked[0][-2], me * scale_cols, scale_cols, axis=1)
    pad = ((0, PACK_ROWS_ALIGN - 2), (0, 0))
    outs = _adam(jnp.pad(pool_scale, pad), jnp.pad(g_scale, pad)[None], jnp.pad(m_pool_scale, pad),
                 jnp.pad(v_pool_scale, pad), name="adam_pool_scale")
    res["pool_scale"] = [o[:2] for o in outs]

    out = [loss, grad_x[None]]
    for kind in range(4):
        out += [res[nm][kind] for nm in WEIGHT_ORDER]
    return tuple(out)
```

```python
import functools
import math

import jax
import jax.numpy as jnp
from jax import lax
from jax.experimental import pallas as pl
from jax.experimental.pallas import tpu as pltpu

F32 = jnp.float32
BF16 = jnp.bfloat16
SDS = jax.ShapeDtypeStruct

N_DEV = 8
S = 2048
D = 1024
HEAD_DIM = 64
ROT_DIM = 16
ROPE_THETA = 500000.0
ATT_W = 1024
SSM_W = 512
SSM_GROUPS = 32
SSM_GROUP = 16
SSM_STATE = 64
N_CPLX = SSM_GROUPS * SSM_STATE
POOL_W = 2048
POOL_GROUP = 512
EVEN_IN = 5120
EVEN_OUT = 1536
ODD_IN = 4096
RMS_EPS = 1e-6
LANES = 128
VMEM_LIMIT = 48 * 1024 * 1024

ADAM_LR = 0.001
ADAM_B1 = 0.9
ADAM_B2 = 0.999
ADAM_EPS = 1e-08
ADAM_WD = 0.01
ADAM_STEP = 10

MESH_ID = pl.DeviceIdType.MESH
NN = (((1,), (0,)), ((), ()))
NT = (((1,), (1,)), ((), ()))
TN = (((0,), (0,)), ((), ()))
_DN = {"nn": NN, "nt": NT, "tn": TN}


def _cparams(sem):
    return pltpu.CompilerParams(dimension_semantics=sem, vmem_limit_bytes=VMEM_LIMIT)


MM_TILES = (1024, 768, 512)


def _tile(dim):
    return next((t for t in MM_TILES if dim % t == 0), dim)


NT_BLOCKS_PER_STEP = 4


def _mm(a, b, mode, out_dtype, b_blocks=False, out_blocks=False, a_cols=None):
    if b_blocks:
        nblk, rows, cb = b.shape
        b2_shape = (rows, nblk * cb)
    else:
        b2_shape = b.shape
    a_shape = a.shape if a_cols is None else (a.shape[0], a_cols[1])
    if mode == "nn":
        (m, k), n = a_shape, b2_shape[1]
    elif mode == "nt":
        (m, k), n = a_shape, b2_shape[0]
    else:
        (k, m), n = a_shape, b2_shape[1]
    tm, tn, tk = _tile(m), _tile(n), _tile(k)
    per_step = 1
    if b_blocks and mode == "nn":
        tn = cb
    if b_blocks and mode == "nt":
        per_step = NT_BLOCKS_PER_STEP
        tk = per_step * cb
        tn = min(tn, MM_TILES[-1])
    if out_blocks:
        tn = n // N_DEV
    nk = k // tk
    a_unit = tm if mode == "tn" else tk
    assert a_cols is None or a_cols[0] % a_unit == 0
    a_off = 0 if a_cols is None else a_cols[0] // a_unit

    def body(a_ref, b_ref, o_ref, acc_ref):
        kk = pl.program_id(2)
        if per_step == 1:
            part = lax.dot_general(a_ref[...].astype(BF16), b_ref[...].astype(BF16), _DN[mode],
                                   preferred_element_type=F32)
        else:
            part = None
            for blk in range(per_step):
                d = lax.dot_general(a_ref[:, blk * cb:(blk + 1) * cb].astype(BF16), b_ref[blk].astype(BF16), NT,
                                    preferred_element_type=F32)
                part = d if part is None else part + d
        if nk == 1:
            o_ref[...] = part.astype(o_ref.dtype)
            return

        @pl.when(kk == 0)
        def _():
            acc_ref[...] = part

        @pl.when((kk > 0) & (kk < nk - 1))
        def _():
            acc_ref[...] += part

        @pl.when(kk == nk - 1)
        def _():
            o_ref[...] = (acc_ref[...] + part).astype(o_ref.dtype)

    if mode == "nn":
        a_spec = pl.BlockSpec((tm, tk), lambda i, j, kk: (i, a_off + kk))
        b_spec = pl.BlockSpec((tk, tn), lambda i, j, kk: (kk, j))
    elif mode == "nt":
        a_spec = pl.BlockSpec((tm, tk), lambda i, j, kk: (i, a_off + kk))
        b_spec = pl.BlockSpec((tn, tk), lambda i, j, kk: (j, kk))
    else:
        a_spec = pl.BlockSpec((tk, tm), lambda i, j, kk: (kk, a_off + i))
        b_spec = pl.BlockSpec((tk, tn), lambda i, j, kk: (kk, j))
    if b_blocks and mode == "nn":
        b_spec = pl.BlockSpec((None, tk, cb), lambda i, j, kk: (j, kk, 0))
    if b_blocks and mode == "nt":
        b_spec = pl.BlockSpec((per_step, tn, cb), lambda i, j, kk: (kk, j, 0))
    out_spec = pl.BlockSpec((tm, tn), lambda i, j, kk: (i, j))
    out_shape = SDS((m, n), out_dtype)
    if out_blocks:
        out_spec = pl.BlockSpec((None, tm, tn), lambda i, j, kk: (j, i, 0))
        out_shape = SDS((N_DEV, m, tn), out_dtype)
    return pl.pallas_call(
        body, name=f"mm_{mode}_{m}x{k}x{n}",
        grid=(m // tm, n // tn, nk),
        in_specs=[a_spec, b_spec],
        out_specs=out_spec,
        out_shape=out_shape,
        scratch_shapes=[pltpu.VMEM((tm, tn) if nk > 1 else (8, LANES), F32)],
        compiler_params=_cparams(("parallel", "parallel", "arbitrary")),
    )(a, b)


def _gmm(a, b, mode, out_dtype, tm=512):
    ng, gw = POOL_W // POOL_GROUP, POOL_GROUP
    ns = S // tm
    if mode in ("nn", "nt"):
        def body(a_ref, b_ref, o_ref):
            o_ref[...] = lax.dot_general(a_ref[...].astype(BF16), b_ref[...].astype(BF16), _DN[mode],
                                         preferred_element_type=F32).astype(o_ref.dtype)

        return pl.pallas_call(
            body, name=f"gmm_{mode}", grid=(ng, ns),
            in_specs=[pl.BlockSpec((tm, gw), lambda g, i: (i, g)),
                      pl.BlockSpec((None, gw, gw), lambda g, i: (g, 0, 0))],
            out_specs=pl.BlockSpec((tm, gw), lambda g, i: (i, g)),
            out_shape=SDS((S, POOL_W), out_dtype),
            compiler_params=_cparams(("parallel", "parallel")),
        )(a, b)

    def body_tn(a_ref, b_ref, o_ref, acc_ref):
        i = pl.program_id(1)

        @pl.when(i == 0)
        def _():
            acc_ref[...] = jnp.zeros_like(acc_ref)

        acc_ref[...] += lax.dot_general(a_ref[...].astype(BF16), b_ref[...].astype(BF16), TN,
                                        preferred_element_type=F32)

        @pl.when(i == ns - 1)
        def _():
            o_ref[...] = acc_ref[...].astype(o_ref.dtype)

    return pl.pallas_call(
        body_tn, name="gmm_tn", grid=(ng, ns),
        in_specs=[pl.BlockSpec((tm, gw), lambda g, i: (i, g)),
                  pl.BlockSpec((tm, gw), lambda g, i: (i, g))],
        out_specs=pl.BlockSpec((None, gw, gw), lambda g, i: (g, 0, 0)),
        out_shape=SDS((ng, gw, gw), out_dtype),
        scratch_shapes=[pltpu.VMEM((gw, gw), F32)],
        compiler_params=_cparams(("parallel", "arbitrary")),
    )(a, b)


def _rowwise(fn, inputs, out_defs, acc_defs=(), tm=256, name=None):
    n_in, n_out, n_acc = len(inputs), len(out_defs), len(acc_defs)
    in_specs, args = [], []
    for arr, width, cb in inputs:
        if arr.shape[0] == 1:
            in_specs.append(pl.BlockSpec((1, width), lambda i, cb=cb: (0, cb)))
        else:
            in_specs.append(pl.BlockSpec((tm, width), lambda i, cb=cb: (i, cb)))
        args.append(arr)
    out_defs = [d if len(d) == 4 else (d[0], d[1], d[0], 0) for d in out_defs]
    out_shape = [SDS((S, ww), dt) for _, dt, ww, _ in out_defs] + [SDS((1, w), F32) for w in acc_defs]
    out_specs = ([pl.BlockSpec((tm, w), lambda i, cb=cb: (i, cb)) for w, _, _, cb in out_defs]
                 + [pl.BlockSpec((1, w), lambda i: (0, 0)) for w in acc_defs])

    def kern(*refs):
        vals = [r[...] for r in refs[:n_in]]
        outs, accs = fn(*vals)
        for r, v in zip(refs[n_in:n_in + n_out], outs):
            r[...] = v.astype(r.dtype)
        if n_acc:
            acc_refs = refs[n_in + n_out:]

            @pl.when(pl.program_id(0) == 0)
            def _():
                for r in acc_refs:
                    r[...] = jnp.zeros_like(r)

            for r, v in zip(acc_refs, accs):
                r[...] += jnp.sum(v, axis=0, keepdims=True)

    res = pl.pallas_call(
        kern, name=name, grid=(S // tm,), in_specs=in_specs, out_specs=out_specs, out_shape=out_shape,
        compiler_params=_cparams(("arbitrary",)),
    )(*args)
    return res


def _sigmoid(x):
    return 1.0 / (1.0 + jnp.exp(-x))


def _silu_and_grad(x):
    s = _sigmoid(x)
    return x * s, s * (1.0 + x * (1.0 - s))


_GELU_K = math.sqrt(2.0 / math.pi)
_GELU_C = 0.044715


def _gelu_and_grad(x):
    t = jnp.tanh(_GELU_K * (x + _GELU_C * (x * x * x)))
    cdf = 0.5 * (1.0 + t)
    grad = cdf + 0.5 * x * (1.0 - t * t) * (_GELU_K * (1.0 + 3.0 * _GELU_C * x * x))
    return x * cdf, grad


def _rms(xv, gain):
    r = lax.rsqrt(jnp.mean(xv * xv, axis=-1, keepdims=True) + RMS_EPS)
    return xv * r * gain


def _rms_bwd(dout, xv, gain):
    r = lax.rsqrt(jnp.mean(xv * xv, axis=-1, keepdims=True) + RMS_EPS)
    xhat = xv * r
    dxhat = dout * gain
    dx = r * (dxhat - xhat * jnp.mean(dxhat * xhat, axis=-1, keepdims=True))
    return dx, dout * xhat


def _norm_fwd(x, gain):
    (h,) = _rowwise(lambda xv, g: ((_rms(xv, g),), ()), [(x, D, 0), (gain, D, 0)], [(D, BF16)], name="norm_fwd")
    return h


def _post_fwd(x, y, gain):
    (o,) = _rowwise(lambda xv, yv, g: ((xv + _rms(yv, g),), ()), [(x, D, 0), (y, D, 0), (gain, D, 0)],
                    [(D, F32)], name="post_fwd")
    return o


def _post_bwd(g, y, gain):
    def fn(gv, yv, gn):
        dx, dg = _rms_bwd(gv, yv, gn)
        return (dx,), (dg,)

    return _rowwise(fn, [(g, D, 0), (y, D, 0), (gain, D, 0)], [(D, BF16)], [D], name="post_bwd")


def _pre_bwd(g, dh, x, gain):
    def fn(gv, dhv, xv, gn):
        dx, dg = _rms_bwd(dhv, xv, gn)
        return (gv + dx,), (dg,)

    return _rowwise(fn, [(g, D, 0), (dh, D, 0), (x, D, 0), (gain, D, 0)], [(D, F32)], [D], name="pre_bwd")


def _loss_grad(xo, tgt):
    def fn(xv, tv):
        e = xv - tv
        return (e * (1.0 / D),), (e * e,)

    return _rowwise(fn, [(xo, D, 0), (tgt, D, 0)], [(D, F32)], [D], name="loss_grad")


def _pool(u_arr, col_block, transpose, out_dtype, into=None, tc=256):
    n_t = POOL_W // tc
    per_group = POOL_GROUP // tc

    def body(u_ref, *rest):
        o_ref = rest[-1]
        c = pl.program_id(0)
        grp = c // per_group
        xv = u_ref[...]
        t = lax.broadcasted_iota(jnp.int32, (S, 1), 0)
        win = jnp.left_shift(2, grp)
        cnt = jnp.minimum(t + 1, win).astype(F32)
        cur = xv / cnt if transpose else xv
        sums = []
        for k in (1, 2, 4, 8):
            if transpose:
                sh = jnp.where(t < S - k, pltpu.roll(cur, S - k, 0), 0.0)
            else:
                sh = jnp.where(t >= k, pltpu.roll(cur, k, 0), 0.0)
            cur = cur + sh
            sums.append(cur)
        tot = jnp.where(grp == 0, sums[0], jnp.where(grp == 1, sums[1], jnp.where(grp == 2, sums[2], sums[3])))
        res = tot - xv if transpose else tot / cnt - xv
        o_ref[...] = res.astype(o_ref.dtype)

    in_specs = [pl.BlockSpec((S, tc), lambda c: (0, col_block * n_t + c))]
    args = [u_arr]
    if into is not None:
        in_specs.append(pl.BlockSpec(memory_space=pl.ANY))
        args.append(into)
    return pl.pallas_call(
        body, name="pool_bwd" if transpose else "pool_fwd", grid=(n_t,),
        in_specs=in_specs,
        out_specs=pl.BlockSpec((S, tc), lambda c: (0, c)),
        out_shape=SDS((S, POOL_W) if into is None else into.shape, out_dtype),
        input_output_aliases={} if into is None else {1: 0},
        compiler_params=_cparams(("parallel",)),
    )(*args)


def _rope_tables(zero):
    pos = jnp.arange(S, dtype=jnp.int32).astype(F32) + zero
    inv_freq = ROPE_THETA ** (-jnp.arange(0, ROT_DIM, 2, dtype=F32) / ROT_DIM)
    ang = pos[:, None] * inv_freq[None, :]
    cos8, sin8 = jnp.cos(ang), jnp.sin(ang)
    half = ROT_DIM // 2
    zeros = jnp.zeros((S, HEAD_DIM - ROT_DIM), F32)
    cos = jnp.concatenate([cos8, cos8, jnp.ones((S, HEAD_DIM - ROT_DIM), F32)], axis=1)
    lo = jnp.concatenate([-sin8, jnp.zeros((S, half), F32), zeros], axis=1)
    hi = jnp.concatenate([jnp.zeros((S, half), F32), sin8, zeros], axis=1)
    rep = LANES // HEAD_DIM
    return jnp.tile(cos, (1, rep)), jnp.tile(lo, (1, rep)), jnp.tile(hi, (1, rep))


def _rotate(xv, cos, lo, hi, transpose):
    width = xv.shape[1]
    rep = width // LANES
    wide = lambda tab: jnp.concatenate([tab] * rep, axis=1)
    half = ROT_DIM // 2
    up = pltpu.roll(xv, width - half, 1)
    dn = pltpu.roll(xv, half, 1)
    mixed = up * wide(lo) + dn * wide(hi)
    return xv * wide(cos) - mixed if transpose else xv * wide(cos) + mixed


def _qkv_prep(proj, tables):
    cos, lo, hi = tables

    def fn(x, c, l, h):
        rot = _rotate(x[:, :2 * ATT_W], c, l, h, False)
        return (jnp.concatenate([(rot[:, :ATT_W] * HEAD_DIM ** -0.5).astype(BF16), rot[:, ATT_W:].astype(BF16),
                                 x[:, 2 * ATT_W:].astype(BF16)], axis=1),), ()

    (qkv,) = _rowwise(fn, [(proj, 3 * ATT_W, 0), (cos, LANES, 0), (lo, LANES, 0), (hi, LANES, 0)],
                      [(3 * ATT_W, BF16)], name="qkv_prep")
    return qkv


ATT_T = 512


def _multiplicity(delta):
    ok = delta >= 0
    near = jnp.where(ok & (delta <= 128), 1.0, 0.0)
    mid = jnp.where(ok & (delta <= 512) & ((delta & 3) == 0), 1.0, 0.0)
    far = jnp.where(ok & ((delta & 15) == 0), 1.0, 0.0)
    return near + mid + far


def _attention_bias(zero):
    t = ATT_T
    pos = jnp.arange(t, dtype=jnp.int32) + jnp.asarray(zero).astype(jnp.int32)
    delta = jnp.arange(S // t, dtype=jnp.int32)[:, None, None] * t + pos[None, :, None] - pos[None, None, :]
    mult = _multiplicity(delta)
    return jnp.where(mult > 0.0, jnp.log(jnp.maximum(mult, 1.0)), -1e30).astype(F32)


def _head_split(v, first):
    zero = jnp.zeros_like(v)
    return [jnp.where(first, v, zero), jnp.where(first, zero, v)]


def _flash_fwd(qkv, bias):
    t = ATT_T
    n_hp = ATT_W // LANES

    def body(q_ref, k_ref, v_ref, b_ref, o_ref, lse_ref):
        i = pl.program_id(1)
        first = lax.broadcasted_iota(jnp.int32, (1, LANES), 1) < HEAD_DIM
        qs = _head_split(q_ref[...], first)

        def kv_step(j, carry):
            m0, l0, m1, l1, acc = carry
            off = pl.multiple_of(j * t, t)
            kb = k_ref[pl.ds(off, t), :]
            vs = _head_split(v_ref[pl.ds(off, t), :], first)
            bias_t = b_ref[i - j]
            new = []
            pv = None
            for h, (m_prev, l_prev) in enumerate(((m0, l0), (m1, l1))):
                s = lax.dot_general(qs[h], kb, NT, preferred_element_type=F32) + bias_t
                m_new = jnp.maximum(m_prev, jnp.max(s, axis=1, keepdims=True))
                p = jnp.exp(s - m_new)
                alpha = jnp.exp(m_prev - m_new)
                l_new = alpha * l_prev + jnp.sum(p, axis=1, keepdims=True)
                d = lax.dot_general(p.astype(BF16), vs[h], NN, preferred_element_type=F32)
                pv = d if pv is None else pv + d
                new.append((m_new, l_new, alpha))
            acc = acc * jnp.where(first, new[0][2], new[1][2]) + pv
            return new[0][0], new[0][1], new[1][0], new[1][1], acc

        neg = jnp.full((t, 1), -1e30, F32)
        zero = jnp.zeros((t, 1), F32)
        m0, l0, m1, l1, acc = lax.fori_loop(0, i + 1, kv_step, (neg, zero, neg, zero, jnp.zeros((t, LANES), F32)))
        o_ref[...] = acc * jnp.where(first, 1.0 / l0, 1.0 / l1)
        lse_ref[...] = jnp.where(first, m0 + jnp.log(l0), m1 + jnp.log(l1))

    blk = pl.BlockSpec((t, LANES), lambda hp, i: (i, hp))
    k_full = pl.BlockSpec((S, LANES), lambda hp, i: (0, n_hp + hp))
    v_full = pl.BlockSpec((S, LANES), lambda hp, i: (0, 2 * n_hp + hp))
    return pl.pallas_call(
        body, name="flash_fwd", grid=(n_hp, S // t),
        in_specs=[blk, k_full, v_full, pl.BlockSpec((S // t, t, t), lambda hp, i: (0, 0, 0))], out_specs=[blk, blk],
        out_shape=[SDS((S, ATT_W), F32), SDS((S, ATT_W), F32)],
        compiler_params=_cparams(("parallel", "arbitrary")),
    )(qkv, qkv, qkv, bias)


def _flash_bwd(qkv, o, do, lse, bias):
    t = ATT_T
    n_hp = ATT_W // LANES
    n_t = S // t

    def body(q_ref, k_ref, v_ref, o_ref, do_ref, lse_ref, b_ref, dq_ref, dk_ref, dv_ref):
        j = pl.program_id(1)
        first = lax.broadcasted_iota(jnp.int32, (1, LANES), 1) < HEAD_DIM

        @pl.when(j == 0)
        def _():
            dq_ref[...] = jnp.zeros_like(dq_ref)

        kb = k_ref[...]
        vb = v_ref[...]
        ks = _head_split(kb, first)

        def q_step(i, carry):
            dk_acc, dv_acc = carry
            rows = pl.ds(pl.multiple_of(i * t, t), t)
            qs = _head_split(q_ref[rows, :], first)
            dob = do_ref[rows, :]
            prod = dob * o_ref[rows, :]
            d_all = jnp.sum(prod, axis=1, keepdims=True)
            d0 = jnp.sum(jnp.where(first, prod, 0.0), axis=1, keepdims=True)
            lse_b = lse_ref[rows, :]
            lse0 = jnp.max(jnp.where(first, lse_b, -jnp.inf), axis=1, keepdims=True)
            lse1 = jnp.max(jnp.where(first, -jnp.inf, lse_b), axis=1, keepdims=True)
            dos = _head_split(dob.astype(BF16), first)
            bias_t = b_ref[i - j]
            dq_t = jnp.zeros((t, LANES), F32)
            for h, (lse_h, d_h) in enumerate(((lse0, d0), (lse1, d_all - d0))):
                s = lax.dot_general(qs[h], kb, NT, preferred_element_type=F32)
                p = jnp.exp(s + (bias_t - lse_h))
                dp = lax.dot_general(dos[h], vb, NT, preferred_element_type=F32)
                ds = (p * (dp - d_h)).astype(BF16)
                dv_acc = dv_acc + lax.dot_general(p.astype(BF16), dos[h], TN, preferred_element_type=F32)
                dk_acc = dk_acc + lax.dot_general(ds, qs[h], TN, preferred_element_type=F32)
                dq_t = dq_t + lax.dot_general(ds, ks[h], NN, preferred_element_type=F32)
            dq_ref[rows, :] += dq_t
            return dk_acc, dv_acc

        zero = jnp.zeros((t, LANES), F32)
        dk_acc, dv_acc = lax.fori_loop(j, n_t, q_step, (zero, zero))
        dk_ref[...] = dk_acc
        dv_ref[...] = dv_acc

    blk = pl.BlockSpec((t, LANES), lambda hp, j: (j, hp))
    full = pl.BlockSpec((S, LANES), lambda hp, j: (0, hp))
    k_blk = pl.BlockSpec((t, LANES), lambda hp, j: (j, n_hp + hp))
    v_blk = pl.BlockSpec((t, LANES), lambda hp, j: (j, 2 * n_hp + hp))
    return pl.pallas_call(
        body, name="flash_bwd", grid=(n_hp, n_t),
        in_specs=[full, k_blk, v_blk, full, full, full, pl.BlockSpec((n_t, t, t), lambda hp, j: (0, 0, 0))],
        out_specs=[full, blk, blk],
        out_shape=[SDS((S, ATT_W), F32)] * 3,
        compiler_params=_cparams(("parallel", "arbitrary")),
    )(qkv, qkv, qkv, o, do, lse, bias)


SCAN_T = 256
ST_ROWS = 2 * N_CPLX // LANES
HALF = ST_ROWS // 2


def _scan_fwd(lam, bu):
    def body(lam_ref, bu_ref, st_ref, carry):
        @pl.when(pl.program_id(0) == 0)
        def _():
            carry[...] = jnp.zeros_like(carry)

        ar, ai = lam_ref[0:HALF, :], lam_ref[HALF:ST_ROWS, :]

        def step(t, c):
            sr, si = c
            b = bu_ref[t]
            nr = ar * sr - ai * si + b[0:HALF]
            ni = ar * si + ai * sr + b[HALF:ST_ROWS]
            st_ref[t, 0:HALF, :] = nr
            st_ref[t, HALF:ST_ROWS, :] = ni
            return nr, ni

        sr, si = lax.fori_loop(0, SCAN_T, step, (carry[0:HALF, :], carry[HALF:ST_ROWS, :]), unroll=8)
        carry[0:HALF, :] = sr
        carry[HALF:ST_ROWS, :] = si

    blk = pl.BlockSpec((SCAN_T, ST_ROWS, LANES), lambda i: (i, 0, 0))
    return pl.pallas_call(
        body, name="scan_fwd", grid=(S // SCAN_T,),
        in_specs=[pl.BlockSpec((ST_ROWS, LANES), lambda i: (0, 0)), blk], out_specs=blk,
        out_shape=SDS((S, ST_ROWS, LANES), F32),
        scratch_shapes=[pltpu.VMEM((ST_ROWS, LANES), F32)],
        compiler_params=_cparams(("arbitrary",)),
    )(lam, bu)


def _scan_bwd(lam, dst, states):
    n_blk = S // SCAN_T

    def body(lam_ref, d_ref, st_ref, g_ref, dlam_ref, carry):
        @pl.when(pl.program_id(0) == 0)
        def _():
            carry[...] = jnp.zeros_like(carry)
            dlam_ref[...] = jnp.zeros_like(dlam_ref)

        ar, ai = lam_ref[0:HALF, :], lam_ref[HALF:ST_ROWS, :]

        def step(kk, c):
            t = SCAN_T - 1 - kk
            gr, gi, dar, dai = c
            x = st_ref[t]
            xr, xi = x[0:HALF], x[HALF:ST_ROWS]
            dar = dar + gr * xr + gi * xi
            dai = dai + gi * xr - gr * xi
            d = d_ref[t]
            ngr = d[0:HALF] + ar * gr + ai * gi
            ngi = d[HALF:ST_ROWS] + ar * gi - ai * gr
            g_ref[t, 0:HALF, :] = ngr
            g_ref[t, HALF:ST_ROWS, :] = ngi
            return ngr, ngi, dar, dai

        zero = jnp.zeros((HALF, LANES), F32)
        gr, gi, dar, dai = lax.fori_loop(0, SCAN_T, step, (carry[0:HALF, :], carry[HALF:ST_ROWS, :], zero, zero),
                                         unroll=8)
        carry[0:HALF, :] = gr
        carry[HALF:ST_ROWS, :] = gi
        dlam_ref[0:HALF, :] += dar
        dlam_ref[HALF:ST_ROWS, :] += dai

    blk = pl.BlockSpec((SCAN_T, ST_ROWS, LANES), lambda i: (n_blk - 1 - i, 0, 0))
    small = pl.BlockSpec((ST_ROWS, LANES), lambda i: (0, 0))
    return pl.pallas_call(
        body, name="scan_bwd", grid=(n_blk,),
        in_specs=[small, blk, blk], out_specs=[blk, small],
        out_shape=[SDS((S, ST_ROWS, LANES), F32), SDS((ST_ROWS, LANES), F32)],
        scratch_shapes=[pltpu.VMEM((ST_ROWS, LANES), F32)],
        compiler_params=_cparams(("arbitrary",)),
    )(lam, dst, states)


def _ssm_prep(a_re, a_im, log_dt, b_re, b_im, c_re, c_im):
    lam = lax.complex(a_re, a_im)
    dt = jnp.exp(log_dt)[:, None]
    lam_bar = jnp.exp(lam * dt)
    b_bar = ((lam_bar - 1.0) / lam)[..., None] * lax.complex(b_re, b_im)
    lam_t = jnp.concatenate([jnp.real(lam_bar).reshape(HALF, LANES), jnp.imag(lam_bar).reshape(HALF, LANES)], axis=0)
    groups_per_super = SSM_GROUPS // SSM_SUPER
    on_diag = ((lax.broadcasted_iota(jnp.int32, (SSM_W, SB_COLS), 0) // SSM_GROUP) % groups_per_super
               == lax.broadcasted_iota(jnp.int32, (SSM_W, SB_COLS), 1) // SSM_STATE)

    def compact(m):
        return jnp.where(on_diag, jnp.tile(m.reshape(SSM_W, SSM_STATE), (1, groups_per_super)), 0.0)

    w_b = jnp.concatenate([compact(jnp.real(b_bar).transpose(0, 2, 1)),
                           compact(jnp.imag(b_bar).transpose(0, 2, 1))], axis=1)
    w_ct = jnp.concatenate([compact(c_re), -compact(c_im)], axis=1)
    return lam_t, w_b, w_ct


SSM_SUPER = 4
SB_ROWS = SSM_W // SSM_SUPER
SB_COLS = N_CPLX // SSM_SUPER


def _bdmm(a, w, mode, out_dtype, a_cols=None):
    a_off = 0 if a_cols is None else a_cols[0] // SB_ROWS
    if mode == "nn":
        def body(a_ref, w_ref, o_ref):
            o_ref[...] = lax.dot_general(a_ref[...].astype(BF16), w_ref[...].astype(BF16), NN,
                                         preferred_element_type=F32).astype(o_ref.dtype)

        return pl.pallas_call(
            body, name="bdmm_nn", grid=(2, SSM_SUPER),
            in_specs=[pl.BlockSpec((S, SB_ROWS), lambda h, b: (0, a_off + b)),
                      pl.BlockSpec((SB_ROWS, SB_COLS), lambda h, b: (b, h))],
            out_specs=pl.BlockSpec((S, SB_COLS), lambda h, b: (0, h * SSM_SUPER + b)),
            out_shape=SDS((S, 2 * N_CPLX), out_dtype),
            compiler_params=_cparams(("parallel", "parallel")),
        )(a, w)
    if mode == "nt":
        def body(re_ref, im_ref, wre_ref, wim_ref, o_ref):
            acc = lax.dot_general(re_ref[...].astype(BF16), wre_ref[...].astype(BF16), NT, preferred_element_type=F32)
            acc += lax.dot_general(im_ref[...].astype(BF16), wim_ref[...].astype(BF16), NT, preferred_element_type=F32)
            o_ref[...] = acc.astype(o_ref.dtype)

        return pl.pallas_call(
            body, name="bdmm_nt", grid=(SSM_SUPER,),
            in_specs=[pl.BlockSpec((S, SB_COLS), lambda b: (0, b)),
                      pl.BlockSpec((S, SB_COLS), lambda b: (0, SSM_SUPER + b)),
                      pl.BlockSpec((SB_ROWS, SB_COLS), lambda b: (b, 0)),
                      pl.BlockSpec((SB_ROWS, SB_COLS), lambda b: (b, 1))],
            out_specs=pl.BlockSpec((S, SB_ROWS), lambda b: (0, b)),
            out_shape=SDS((S, SSM_W), out_dtype),
            compiler_params=_cparams(("parallel",)),
        )(a, a, w, w)

    def body_tn(a_ref, w_ref, o_ref):
        o_ref[...] = lax.dot_general(a_ref[...].astype(BF16), w_ref[...].astype(BF16), TN,
                                     preferred_element_type=F32).astype(o_ref.dtype)

    return pl.pallas_call(
        body_tn, name="bdmm_tn", grid=(2, SSM_SUPER),
        in_specs=[pl.BlockSpec((S, SB_ROWS), lambda h, b: (0, a_off + b)),
                  pl.BlockSpec((S, SB_COLS), lambda h, b: (0, h * SSM_SUPER + b))],
        out_specs=pl.BlockSpec((SB_ROWS, SB_COLS), lambda h, b: (b, h)),
        out_shape=SDS((SSM_W, 2 * SB_COLS), out_dtype),
        compiler_params=_cparams(("parallel", "parallel")),
    )(a, w)


U_SSM_COLS = (4 * ATT_W, SSM_W)


def _row(v):
    return v.reshape(1, -1)


def _even_fwd(x, pre, post, w_in, w_out, glu_w, glu_b, ssm_d, prep, tables):
    lam_t, w_b, w_ct = prep
    h = _norm_fwd(x, pre)
    proj = _mm(h, w_in, "nn", F32, b_blocks=True)
    qkv = _qkv_prep(proj, tables[:3])
    att, lse = _flash_fwd(qkv, tables[3])
    bu = _bdmm(proj, w_b, "nn", F32, a_cols=U_SSM_COLS)
    states = _scan_fwd(lam_t, bu.reshape(S, ST_ROWS, LANES))
    y = _bdmm(states.reshape(S, 2 * N_CPLX), w_ct, "nt", F32)

    def act1(yv, uv, dv):
        return (_gelu_and_grad(yv + dv * uv)[0],), ()

    (z1,) = _rowwise(act1, [(y, SSM_W, 0), (proj, SSM_W, 8), (ssm_d, SSM_W, 0)], [(SSM_W, F32)], name="ssm_act_fwd")
    lin = _mm(z1, glu_w, "nn", F32)

    def gate(att_v, ga, gs, z1v, linv, bv):
        ssm_out = z1v * _sigmoid(linv + bv)
        return (jnp.concatenate([att_v * _silu_and_grad(ga)[0], ssm_out * _silu_and_grad(gs)[0]], axis=1),), ()

    (merged,) = _rowwise(gate, [(att, ATT_W, 0), (proj, ATT_W, 3), (proj, SSM_W, 9), (z1, SSM_W, 0),
                                (lin, SSM_W, 0), (glu_b, SSM_W, 0)], [(EVEN_OUT, BF16)], name="even_gate_fwd")
    yout = _mm(merged, w_out, "nn", F32)
    x_next = _post_fwd(x, yout, post)
    saved = (x, h, proj, qkv, att, lse, states, y, z1, lin, merged, yout)
    return x_next, saved


def _even_bwd(g, saved, pre, post, w_in, w_out, glu_w, glu_b, ssm_d, prep, tables):
    x, h, proj, qkv, att, lse, states, y, z1, lin, merged, yout = saved
    lam_t, w_b, w_ct = prep
    dyout, dpost = _post_bwd(g, yout, post)
    dmerged = _mm(dyout, w_out, "nt", F32)
    dw_out = _mm(merged, dyout, "tn", BF16)

    def gate_bwd(dm_a, dm_s, att_v, ga, gs, z1v, linv, bv):
        sa, dsa = _silu_and_grad(ga)
        ss, dss = _silu_and_grad(gs)
        sig = _sigmoid(linv + bv)
        ssm_out = z1v * sig
        dssm = dm_s * ss
        dlin = dssm * z1v * sig * (1.0 - sig)
        return (dm_a * sa, dm_a * att_v * dsa, dm_s * ssm_out * dss, dssm * sig, dlin), (dlin,)

    datt, dg_att, dg_ssm, dz1a, dlin, dglu_b = _rowwise(
        gate_bwd, [(dmerged, ATT_W, 0), (dmerged, SSM_W, 2), (att, ATT_W, 0), (proj, ATT_W, 3), (proj, SSM_W, 9),
                   (z1, SSM_W, 0), (lin, SSM_W, 0), (glu_b, SSM_W, 0)],
        [(ATT_W, F32), (ATT_W, BF16), (SSM_W, BF16), (SSM_W, F32), (SSM_W, BF16)], [SSM_W], name="even_gate_bwd")
    dz1b = _mm(dlin, glu_w, "nt", F32)
    dglu_w = _mm(z1, dlin, "tn", BF16)

    def act1_bwd(da, db, yv, uv, dv):
        dpre = (da + db) * _gelu_and_grad(yv + dv * uv)[1]
        return (dpre, dpre * dv), (dpre * uv,)

    dy, du_direct, dd = _rowwise(act1_bwd, [(dz1a, SSM_W, 0), (dz1b, SSM_W, 0), (y, SSM_W, 0), (proj, SSM_W, 8),
                                            (ssm_d, SSM_W, 0)], [(SSM_W, BF16), (SSM_W, F32)], [SSM_W],
                                 name="ssm_act_bwd")
    dst = _bdmm(dy, w_ct, "nn", F32)
    dw_ct = _bdmm(dy, states.reshape(S, 2 * N_CPLX), "tn", F32)
    gst, dlam = _scan_bwd(lam_t, dst.reshape(S, ST_ROWS, LANES), states)
    dbu = gst.reshape(S, 2 * N_CPLX)
    du_state = _bdmm(dbu, w_b, "nt", F32)
    dw_b = _bdmm(proj, dbu, "tn", F32, a_cols=U_SSM_COLS)
    dq, dk, dv = _flash_bwd(qkv, att, datt, lse, tables[3])

    def assemble(dqv, dkv, dvv, dga, dua, dub, dgs, c, l, h):
        rot = _rotate(jnp.concatenate([dqv, dkv], axis=1), c, l, h, True)
        return (jnp.concatenate([(rot[:, :ATT_W] * HEAD_DIM ** -0.5).astype(BF16), rot[:, ATT_W:].astype(BF16),
                                 dvv.astype(BF16), dga, (dua + dub).astype(BF16), dgs], axis=1),), ()

    (dproj,) = _rowwise(assemble, [(dq, ATT_W, 0), (dk, ATT_W, 0), (dv, ATT_W, 0), (dg_att, ATT_W, 0),
                                   (du_state, SSM_W, 0), (du_direct, SSM_W, 0), (dg_ssm, SSM_W, 0),
                                   (tables[0], LANES, 0), (tables[1], LANES, 0), (tables[2], LANES, 0)],
                        [(EVEN_IN, BF16)], name="dproj_assemble")
    dw_in = _mm(h, dproj, "tn", BF16, out_blocks=True)
    dh = _mm(dproj, w_in, "nt", F32, b_blocks=True)
    g_prev, dpre = _pre_bwd(g, dh, x, pre)
    return g_prev, dict(pre=dpre, post=dpost, w_in=dw_in, w_out=dw_out, glu_w=dglu_w, glu_b=dglu_b, ssm_d=dd,
                        prep=(dlam, dw_b, dw_ct))


def _odd_fwd(x, pre, post, w_in, pool_w, pool_scale, w_out):
    h = _norm_fwd(x, pre)
    proj = _mm(h, w_in, "nn", F32, b_blocks=True)
    mixed = _pool(proj, 0, False, BF16)
    ylin = _gmm(mixed, pool_w, "nn", F32)

    def gate(yl, gt, sc):
        return (yl * sc * _silu_and_grad(gt)[0],), ()

    (z,) = _rowwise(gate, [(ylin, POOL_W, 0), (proj, POOL_W, 1), (pool_scale, POOL_W, 0)], [(POOL_W, BF16)],
                    name="odd_gate_fwd")
    yout = _mm(z, w_out, "nn", F32)
    x_next = _post_fwd(x, yout, post)
    return x_next, (x, h, proj, mixed, ylin, z, yout)


def _odd_bwd(g, saved, pre, post, w_in, pool_w, pool_scale, w_out):
    x, h, proj, mixed, ylin, z, yout = saved
    dyout, dpost = _post_bwd(g, yout, post)
    dz = _mm(dyout, w_out, "nt", F32)
    dw_out = _mm(z, dyout, "tn", BF16)

    def gate_bwd(dzv, yl, gt, sc):
        sg, dsg = _silu_and_grad(gt)
        tt = dzv * sg
        return (tt * sc, dzv * yl * sc * dsg), (tt * yl,)

    dylin, dproj_gate, dscale = _rowwise(gate_bwd, [(dz, POOL_W, 0), (ylin, POOL_W, 0), (proj, POOL_W, 1),
                                                    (pool_scale, POOL_W, 0)],
                                         [(POOL_W, BF16), (POOL_W, BF16, ODD_IN, 1)], [POOL_W], name="odd_gate_bwd")
    dmixed = _gmm(dylin, pool_w, "nt", F32)
    dpool_w = _gmm(mixed, dylin, "tn", BF16)
    dproj = _pool(dmixed, 0, True, BF16, into=dproj_gate)
    dw_in = _mm(h, dproj, "tn", BF16, out_blocks=True)
    dh = _mm(dproj, w_in, "nt", F32, b_blocks=True)
    g_prev, dpre = _pre_bwd(g, dh, x, pre)
    return g_prev, dict(pre=dpre, post=dpost, w_in=dw_in, w_out=dw_out, pool_w=dpool_w, pool_scale=dscale)


def _my_index():
    return 4 * lax.axis_index("x") + 2 * lax.axis_index("y") + lax.axis_index("c")


def _exchange(arrs, gather, name, after=()):
    n = len(arrs)
    out_shape = [SDS((N_DEV,) + a.shape, a.dtype) if gather else SDS(a.shape, a.dtype) for a in arrs]

    def body(*refs):
        ins, outs = refs[:n], refs[n + len(after):2 * n + len(after)]
        send_sems, recv_sems, local_sems = refs[2 * n + len(after):]
        me = _my_index()

        def src(i, j):
            return ins[i] if gather else ins[i].at[j]

        def remote(i, j, src_slot, dst_slot, recv_slot):
            return pltpu.make_async_remote_copy(
                src_ref=src(i, src_slot), dst_ref=outs[i].at[dst_slot], send_sem=send_sems.at[i, j],
                recv_sem=recv_sems.at[i, recv_slot], device_id=(j // 4, (j // 2) % 2, j % 2), device_id_type=MESH_ID)

        def local(i):
            return pltpu.make_async_copy(src(i, me), outs[i].at[me], local_sems.at[i])

        for i in range(n):
            local(i).start()
        for j in range(N_DEV):
            @pl.when(me != j)
            def _(j=j):
                for i in range(n):
                    remote(i, j, j, me, me).start()
        for j in range(N_DEV):
            @pl.when(me != j)
            def _(j=j):
                for i in range(n):
                    remote(i, j, j, me, me).wait_send()
                    remote(i, j, j, j, j).wait_recv()
        for i in range(n):
            local(i).wait()

    any_spec = pl.BlockSpec(memory_space=pl.ANY)
    return pl.pallas_call(
        body, name=name, in_specs=[any_spec] * (n + len(after)), out_specs=[any_spec] * n, out_shape=out_shape,
        scratch_shapes=[pltpu.SemaphoreType.DMA((n, N_DEV)), pltpu.SemaphoreType.DMA((n, N_DEV)),
                        pltpu.SemaphoreType.DMA((n,))],
    )(*arrs, *after)


HBM_SPEC = pl.BlockSpec(memory_space=pltpu.HBM)
SEM_SPEC = pl.BlockSpec(memory_space=pltpu.SEMAPHORE)
SPLIT_EFFECT = pltpu.SideEffectType.DATAFLOW_SIDE_EFFECTING


def _device_of(j):
    return (j // 4, (j // 2) % 2, j % 2)


def _split_copy(srcs, lands, send_sems, recv_sems, gather, i, j, dst_slot, recv_slot):
    return pltpu.make_async_remote_copy(
        src_ref=srcs[i] if gather else srcs[i].at[j], dst_ref=lands[i].at[dst_slot],
        send_sem=send_sems.at[i * N_DEV + j], recv_sem=recv_sems.at[i * N_DEV + recv_slot],
        device_id=_device_of(j), device_id_type=MESH_ID)


def _own_copy(srcs, lands, send_sems, gather, i, me):
    return pltpu.make_async_copy(srcs[i] if gather else srcs[i].at[me], lands[i].at[me], send_sems.at[i * N_DEV + me])


def _xchg_start(name, srcs, gather, after=()):
    n = len(srcs)
    n_in = n + len(after)

    def body(*refs):
        src_refs = refs[:n]
        send_sems, recv_sems, token = refs[n_in], refs[n_in + 1], refs[-1]
        land_refs = refs[n_in + 2 + n:n_in + 2 + 2 * n]
        me = _my_index()
        for j in range(N_DEV):
            @pl.when(me != j)
            def _(j=j):
                for i in range(n):
                    _split_copy(src_refs, land_refs, send_sems, recv_sems, gather, i, j, me, me).start()
        for i in range(n):
            _own_copy(src_refs, land_refs, send_sems, gather, i, me).start()
        token[...] = jnp.zeros_like(token)

    land_shapes = [((N_DEV,) + a.shape) if gather else a.shape for a in srcs]
    thru = ([pltpu.HBM(a.shape, a.dtype) for a in srcs] + [pltpu.HBM(s, a.dtype) for s, a in zip(land_shapes, srcs)])
    res = pl.pallas_call(
        body, name=name,
        out_shape=(pltpu.SemaphoreType.DMA((n * N_DEV,)), pltpu.SemaphoreType.DMA((n * N_DEV,)), *thru,
                   SDS((8, LANES), F32)),
        in_specs=[HBM_SPEC] * n + [pl.BlockSpec(memory_space=pl.ANY)] * len(after),
        out_specs=(SEM_SPEC, SEM_SPEC, *([HBM_SPEC] * (2 * n)), pl.BlockSpec(memory_space=pltpu.VMEM)),
        input_output_aliases={i: 2 + i for i in range(n)},
        compiler_params=pltpu.CompilerParams(has_side_effects=SPLIT_EFFECT),
    )(*[pltpu.with_memory_space_constraint(a, pltpu.HBM) for a in srcs], *after)
    return res[0], res[1], list(res[2:2 + n]), list(res[2 + n:2 + 2 * n]), res[-1]


def _xchg_wait(name, started, gather, after):
    send_sems, recv_sems, srcs, lands, _ = started
    n = len(srcs)

    def body(*refs):
        src_refs, land_refs = refs[:n], refs[n:2 * n]
        send_r, recv_r = refs[2 * n], refs[2 * n + 1]
        me = _my_index()
        for j in range(N_DEV):
            @pl.when(me != j)
            def _(j=j):
                for i in range(n):
                    _split_copy(src_refs, land_refs, send_r, recv_r, gather, i, j, me, me).wait_send()
                    _split_copy(src_refs, land_refs, send_r, recv_r, gather, i, j, j, j).wait_recv()
        for i in range(n):
            _own_copy(src_refs, land_refs, send_r, gather, i, me).wait()

    thru = [pltpu.HBM(a.shape, a.dtype) for a in list(srcs) + list(lands)]
    res = pl.pallas_call(
        body, name=name, out_shape=tuple(thru),
        in_specs=[HBM_SPEC] * (2 * n) + [SEM_SPEC, SEM_SPEC] + [pl.BlockSpec(memory_space=pl.ANY)] * len(after),
        out_specs=tuple([HBM_SPEC] * (2 * n)),
        input_output_aliases={i: i for i in range(2 * n)},
        compiler_params=pltpu.CompilerParams(has_side_effects=SPLIT_EFFECT),
    )(*srcs, *lands, send_sems, recv_sems, *after)
    return list(res[n:])


def _adam_layers(w, slot_list, m, v, name):
    n_l, r, c = w.shape
    ns = slot_list[0].shape[0]
    tr = r
    while tr * c * 4 > (1 << 20) and tr % 16 == 0:
        tr //= 2
    assert r % tr == 0 and len(slot_list) == n_l

    def body(*refs):
        w_ref, slot_refs = refs[0], refs[1:1 + n_l]
        m_ref, v_ref, go_ref, d_ref, mo_ref, vo_ref = refs[1 + n_l:]
        layer = pl.program_id(0)
        g = None
        for l, g_ref in enumerate(slot_refs):
            gl = g_ref[0].astype(F32)
            for s in range(1, ns):
                gl = gl + g_ref[s].astype(F32)
            g = gl if g is None else jnp.where(layer == l, gl, g)
        mn = ADAM_B1 * m_ref[...] + (1.0 - ADAM_B1) * g
        vn = ADAM_B2 * v_ref[...] + (1.0 - ADAM_B2) * (g * g)
        m_hat = mn / (1.0 - ADAM_B1 ** ADAM_STEP)
        v_hat = vn / (1.0 - ADAM_B2 ** ADAM_STEP)
        go_ref[...] = g
        d_ref[...] = -ADAM_LR * (m_hat / (jnp.sqrt(v_hat) + ADAM_EPS) + ADAM_WD * w_ref[...])
        mo_ref[...] = mn
        vo_ref[...] = vn

    blk = pl.BlockSpec((None, tr, c), lambda l, i: (l, i, 0))
    slot_specs = [pl.BlockSpec((ns, tr, c), lambda l, i, k=k: (0, jnp.where(l == k, i, 0), 0)) for k in range(n_l)]
    return pl.pallas_call(
        body, name=name, grid=(n_l, r // tr),
        in_specs=[blk] + slot_specs + [blk, blk],
        out_specs=[blk] * 4, out_shape=[SDS((n_l, r, c), F32)] * 4,
        compiler_params=_cparams(("arbitrary", "arbitrary")),
    )(w, *slot_list, m, v)


def _adam(w, gslots, m, v, name):
    r, c = w.shape
    ns = gslots.shape[0]
    tr = r
    while tr * c * 4 > (1 << 20) and tr % 16 == 0:
        tr //= 2
    assert r % tr == 0

    def body(w_ref, g_ref, m_ref, v_ref, go_ref, d_ref, mo_ref, vo_ref):
        g = g_ref[0].astype(F32)
        for s in range(1, ns):
            g = g + g_ref[s].astype(F32)
        wv = w_ref[...]
        mn = ADAM_B1 * m_ref[...] + (1.0 - ADAM_B1) * g
        vn = ADAM_B2 * v_ref[...] + (1.0 - ADAM_B2) * (g * g)
        m_hat = mn / (1.0 - ADAM_B1 ** ADAM_STEP)
        v_hat = vn / (1.0 - ADAM_B2 ** ADAM_STEP)
        go_ref[...] = g
        d_ref[...] = -ADAM_LR * (m_hat / (jnp.sqrt(v_hat) + ADAM_EPS) + ADAM_WD * wv)
        mo_ref[...] = mn
        vo_ref[...] = vn

    blk = pl.BlockSpec((tr, c), lambda i: (i, 0))
    return pl.pallas_call(
        body, name=name, grid=(r // tr,),
        in_specs=[blk, pl.BlockSpec((ns, tr, c), lambda i: (0, i, 0)), blk, blk],
        out_specs=[blk] * 4, out_shape=[SDS((r, c), F32)] * 4,
        compiler_params=_cparams(("parallel",)),
    )(w, gslots, m, v)


SMALL_NAMES = ("pre_norm", "post_norm", "ssm_a_re", "ssm_a_im", "ssm_log_dt", "ssm_b_re", "ssm_b_im", "ssm_c_re",
               "ssm_c_im", "ssm_d", "ssm_glu_b")
SSM_NAMES = ("ssm_a_re", "ssm_a_im", "ssm_log_dt", "ssm_b_re", "ssm_b_im", "ssm_c_re", "ssm_c_im")
SHARDED_NAMES = ("even_w_in", "even_w_out", "ssm_glu_w", "odd_w_in", "pool_w", "odd_w_out")
WEIGHT_ORDER = ("pre_norm", "post_norm", "even_w_in", "even_w_out", "ssm_a_re", "ssm_a_im", "ssm_log_dt", "ssm_b_re",
                "ssm_b_im", "ssm_c_re", "ssm_c_im", "ssm_d", "ssm_glu_w", "ssm_glu_b", "odd_w_in", "pool_w",
                "pool_scale", "odd_w_out")
PACK_ROWS_ALIGN = 8


def _pack(parts):
    flat = jnp.concatenate([p.reshape(-1).astype(F32) for p in parts])
    rows = -(-flat.shape[0] // (LANES * PACK_ROWS_ALIGN)) * PACK_ROWS_ALIGN
    return jnp.pad(flat, (0, rows * LANES - flat.shape[0])).reshape(rows, LANES)


def _unpack(packed, shapes):
    flat = packed.reshape(-1)
    out, off = [], 0
    for shp in shapes:
        size = math.prod(shp)
        out.append(flat[off:off + size].reshape(shp))
        off += size
    return out


EVEN_SHARDED = ("w_in", "w_out", "glu_w")
ODD_SHARDED = ("w_in", "pool_w", "w_out")
FAMILY = {(0, "w_in"): "even_w_in", (0, "w_out"): "even_w_out", (0, "glu_w"): "ssm_glu_w",
          (1, "w_in"): "odd_w_in", (1, "pool_w"): "pool_w", (1, "w_out"): "odd_w_out"}


def _sharded_keys(layer):
    return EVEN_SHARDED if layer % 2 == 0 else ODD_SHARDED


def _local_step(x, tgt, small, get_weights, on_grads, zero=0.0):
    tables = _rope_tables(zero) + (_attention_bias(zero),)
    preps, prep_vjps = [], []
    for i in range(2):
        out, vjp = jax.vjp(_ssm_prep, small["ssm_a_re"][i] + zero, small["ssm_a_im"][i], small["ssm_log_dt"][i],
                           small["ssm_b_re"][i], small["ssm_b_im"][i], small["ssm_c_re"][i], small["ssm_c_im"][i])
        preps.append(out)
        prep_vjps.append(vjp)

    def layer_args(layer, wts):
        i = layer // 2
        pre, post = _row(small["pre_norm"][layer]) + wts.get("token", 0.0), _row(small["post_norm"][layer])
        if layer % 2 == 0:
            return (pre, post, wts["w_in"], wts["w_out"], wts["glu_w"], _row(small["ssm_glu_b"][i]),
                    _row(small["ssm_d"][i]), preps[i], tables)
        return (pre, post, wts["w_in"], wts["pool_w"], _row(wts["pool_scale"]), wts["w_out"])

    saved, args = [], []
    cur = x
    for layer in range(4):
        after = (cur,) if layer else (cur, tables[0], tables[3], preps[0][1], preps[0][2], preps[1][1], preps[1][2])
        args.append(layer_args(layer, get_weights(layer, after)))
        cur, sv = (_even_fwd if layer % 2 == 0 else _odd_fwd)(cur, *args[layer])
        saved.append(sv)
    g, sq = _loss_grad(cur, tgt)
    loss = 0.5 * jnp.sum(sq) / D

    lg = [None] * 4
    token = jnp.zeros((), F32)
    for layer in reversed(range(4)):
        largs = list(args[layer])
        largs[1] = largs[1] + token
        g, lg[layer] = (_even_bwd if layer % 2 == 0 else _odd_bwd)(g, saved[layer], *largs)
        if layer % 2 == 0:
            lg[layer]["ssm"] = prep_vjps[layer // 2](lg[layer].pop("prep"))
        token = on_grads(layer, lg[layer])
    return loss, g, token


def _to_slots(key, gfull):
    if key == "w_in":
        return gfull
    if key in ("w_out", "glu_w"):
        rr, nn = gfull.shape
        return gfull.reshape(N_DEV, rr // N_DEV, nn)
    assert key == "pool_w"
    gg, rr, nn = gfull.shape
    return gfull.reshape(gg, N_DEV, rr // N_DEV, nn).transpose(1, 0, 2, 3)


def _from_gathered(key, gat):
    if key == "w_in":
        return gat
    if key in ("w_out", "glu_w"):
        _, rr, nn = gat.shape
        return gat.reshape(N_DEV * rr, nn)
    assert key == "pool_w"
    _, gg, rr, nn = gat.shape
    return gat.transpose(1, 0, 2, 3).reshape(gg, N_DEV * rr, nn)


def kernel(x, pre_norm, post_norm, even_w_in, even_w_out, ssm_a_re, ssm_a_im, ssm_log_dt, ssm_b_re, ssm_b_im, ssm_c_re, ssm_c_im, ssm_d, ssm_glu_w, ssm_glu_b, odd_w_in, pool_w, pool_scale, odd_w_out, loss_target, m_pre_norm, m_post_norm, m_even_w_in, m_even_w_out, m_ssm_a_re, m_ssm_a_im, m_ssm_log_dt, m_ssm_b_re, m_ssm_b_im, m_ssm_c_re, m_ssm_c_im, m_ssm_d, m_ssm_glu_w, m_ssm_glu_b, m_odd_w_in, m_pool_w, m_pool_scale, m_odd_w_out, v_pre_norm, v_post_norm, v_even_w_in, v_even_w_out, v_ssm_a_re, v_ssm_a_im, v_ssm_log_dt, v_ssm_b_re, v_ssm_b_im, v_ssm_c_re, v_ssm_c_im, v_ssm_d, v_ssm_glu_w, v_ssm_glu_b, v_odd_w_in, v_pool_w, v_pool_scale, v_odd_w_out):
    w = dict(pre_norm=pre_norm, post_norm=post_norm, even_w_in=even_w_in, even_w_out=even_w_out, ssm_a_re=ssm_a_re,
             ssm_a_im=ssm_a_im, ssm_log_dt=ssm_log_dt, ssm_b_re=ssm_b_re, ssm_b_im=ssm_b_im, ssm_c_re=ssm_c_re,
             ssm_c_im=ssm_c_im, ssm_d=ssm_d, ssm_glu_w=ssm_glu_w, ssm_glu_b=ssm_glu_b, odd_w_in=odd_w_in,
             pool_w=pool_w, pool_scale=pool_scale, odd_w_out=odd_w_out)
    mom = dict(pre_norm=m_pre_norm, post_norm=m_post_norm, even_w_in=m_even_w_in, even_w_out=m_even_w_out,
               ssm_a_re=m_ssm_a_re, ssm_a_im=m_ssm_a_im, ssm_log_dt=m_ssm_log_dt, ssm_b_re=m_ssm_b_re,
               ssm_b_im=m_ssm_b_im, ssm_c_re=m_ssm_c_re, ssm_c_im=m_ssm_c_im, ssm_d=m_ssm_d, ssm_glu_w=m_ssm_glu_w,
               ssm_glu_b=m_ssm_glu_b, odd_w_in=m_odd_w_in, pool_w=m_pool_w, pool_scale=m_pool_scale,
               odd_w_out=m_odd_w_out)
    var = dict(pre_norm=v_pre_norm, post_norm=v_post_norm, even_w_in=v_even_w_in, even_w_out=v_even_w_out,
               ssm_a_re=v_ssm_a_re, ssm_a_im=v_ssm_a_im, ssm_log_dt=v_ssm_log_dt, ssm_b_re=v_ssm_b_re,
               ssm_b_im=v_ssm_b_im, ssm_c_re=v_ssm_c_re, ssm_c_im=v_ssm_c_im, ssm_d=v_ssm_d, ssm_glu_w=v_ssm_glu_w,
               ssm_glu_b=v_ssm_glu_b, odd_w_in=v_odd_w_in, pool_w=v_pool_w, pool_scale=v_pool_scale,
               odd_w_out=v_odd_w_out)
    me = _my_index()
    scale_cols = pool_scale.shape[1]

    def start_gather(layer, after=()):
        i = layer // 2
        shards = [w[FAMILY[(layer % 2, k)]][i].astype(BF16) for k in _sharded_keys(layer)]
        if layer % 2 == 1:
            shards.append(jnp.pad(pool_scale[i][None], ((0, PACK_ROWS_ALIGN - 1), (0, 0))))
        return _xchg_start(f"gather_start_{layer}", shards, True, after)

    gather_started = {0: start_gather(0)}
    small = {nm: w[nm] for nm in SMALL_NAMES}

    def get_weights(layer, after):
        lands = _xchg_wait(f"gather_wait_{layer}", gather_started[layer], True, after)
        wts = {k: _from_gathered(k, gat) for k, gat in zip(_sharded_keys(layer), lands)}
        if layer % 2 == 1:
            wts["pool_scale"] = lands[-1][:, 0, :].reshape(N_DEV * scale_cols)
        if layer == 0:
            for later in (1, 2, 3):
                gather_started[later] = start_gather(later, after=(lands[0],))
            wts["token"] = sum(gather_started[later][4][0, 0] for later in (1, 2, 3))
        return wts

    scatter_started = [None] * 4

    def small_split(tree):
        early = ([tree["pre_norm"][1:], tree["post_norm"][1:]] + [tree[nm][1] for nm in SSM_NAMES]
                 + [tree["ssm_d"][1:], tree["ssm_glu_b"][1:], jnp.zeros((2, N_DEV * scale_cols), F32)])
        late = ([tree["pre_norm"][:1], tree["post_norm"][:1]] + [tree[nm][0] for nm in SSM_NAMES]
                + [tree["ssm_d"][:1], tree["ssm_glu_b"][:1], jnp.zeros((1,), F32)])
        return early, late

    layer_grads = {}
    early_started = []

    def on_grads(layer, lg):
        layer_grads[layer] = lg
        slots = [_to_slots(k, lg[k]) for k in _sharded_keys(layer)]
        scatter_started[layer] = _xchg_start(f"scatter_start_{layer}", slots, False)
        zero = scatter_started[layer][4][0, 0]
        if layer == 1:
            lgs = layer_grads
            early = ([jnp.concatenate([lgs[l][k] for l in (1, 2, 3)], axis=0) for k in ("pre", "post")]
                     + list(lgs[2]["ssm"]) + [lgs[2]["ssm_d"], lgs[2]["glu_b"],
                                              jnp.concatenate([lgs[1]["pool_scale"], lgs[3]["pool_scale"]], axis=0)])
            early_started.append(_xchg_start("small_start", [_pack(early)], True))
            zero = zero + early_started[0][4][0, 0]
        return zero

    loss_local, grad_x, token = _local_step(x[0], loss_target[0], small, get_weights, on_grads,
                                            zero=gather_started[0][4][0, 0])

    def adam_family(parity, k):
        nm = FAMILY[(parity, k)]
        shp = w[nm].shape
        cols = shp[-1]
        slot_list = [recv[(parity + 2 * i, k)].reshape(N_DEV, -1, cols) for i in range(2)]
        outs = _adam_layers(w[nm].reshape(2, -1, cols), slot_list, mom[nm].reshape(2, -1, cols),
                            var[nm].reshape(2, -1, cols), name=f"adam_{nm}")
        return [o.reshape(shp) for o in outs]

    recv, res = {}, {}
    for layer in (3, 1):
        lands = _xchg_wait(f"scatter_wait_{layer}", scatter_started[layer], False, (scatter_started[0][4],))
        for k, land in zip(_sharded_keys(layer), lands):
            recv[(layer, k)] = land
    for k in ODD_SHARDED:
        res[FAMILY[(1, k)]] = adam_family(1, k)

    lg0 = layer_grads[0]
    late = [lg0["pre"], lg0["post"]] + list(lg0["ssm"]) + [lg0["ssm_d"], lg0["glu_b"], loss_local.reshape(1)]
    (late_slots,) = _exchange([_pack(late) + token], True, "gather_small_grads",
                              after=tuple(res[FAMILY[(1, k)]][0] for k in ODD_SHARDED))
    (early_slots,) = _xchg_wait("small_wait", early_started[0], True, (late_slots,))

    for layer in (2, 0):
        lands = _xchg_wait(f"scatter_wait_{layer}", scatter_started[layer], False, (late_slots,))
        for k, land in zip(_sharded_keys(layer), lands):
            recv[(layer, k)] = land
    for k in EVEN_SHARDED:
        res[FAMILY[(0, k)]] = adam_family(0, k)

    packs = [small_split(tree) for tree in (w, mom, var)]
    unpacked = []
    for which, slots in enumerate((early_slots, late_slots)):
        parts = [packs[t][which] for t in range(3)]
        outs = _adam(_pack(parts[0]), slots, _pack(parts[1]), _pack(parts[2]),
                     name="adam_small_early" if which == 0 else "adam_small_late")
        unpacked.append([_unpack(o, [p.shape for p in parts[0]]) for o in outs])
    n_ssm = len(SSM_NAMES)
    for kind in range(4):
        early, late = unpacked[0][kind], unpacked[1][kind]
        both = lambda idx: jnp.concatenate([late[idx], early[idx]], axis=0)
        vals = {"pre_norm": both(0), "post_norm": both(1), "ssm_d": both(2 + n_ssm), "ssm_glu_b": both(3 + n_ssm)}
        for j, nm in enumerate(SSM_NAMES):
            vals[nm] = jnp.stack([late[2 + j], early[2 + j]])
        for nm, val in vals.items():
            res.setdefault(nm, []).append(val)
    loss = unpacked[1][0][-1].reshape(())
    g_scale = lax.dynamic_slice_in_dim(unpacked[0][0][-1], me * scale_cols, scale_cols, axis=1)
    pad = ((0, PACK_ROWS_ALIGN - 2), (0, 0))
    outs = _adam(jnp.pad(pool_scale, pad), jnp.pad(g_scale, pad)[None], jnp.pad(m_pool_scale, pad),
                 jnp.pad(v_pool_scale, pad), name="adam_pool_scale")
    res["pool_scale"] = [o[:2] for o in outs]

    out = [loss, grad_x[None]]
    for kind in range(4):
        out += [res[nm][kind] for nm in WEIGHT_ORDER]
    return tuple(out)
```

```python
import functools
import math

import jax
import jax.numpy as jnp
from jax import lax
from jax.experimental import pallas as pl
from jax.experimental.pallas import tpu as pltpu

F32 = jnp.float32
BF16 = jnp.bfloat16
SDS = jax.ShapeDtypeStruct

N_DEV = 8
S = 2048
D = 1024
HEAD_DIM = 64
ROT_DIM = 16
ROPE_THETA = 500000.0
ATT_W = 1024
SSM_W = 512
SSM_GROUPS = 32
SSM_GROUP = 16
SSM_STATE = 64
N_CPLX = SSM_GROUPS * SSM_STATE
POOL_W = 2048
POOL_GROUP = 512
EVEN_IN = 5120
EVEN_OUT = 1536
ODD_IN = 4096
RMS_EPS = 1e-6
LANES = 128
VMEM_LIMIT = 48 * 1024 * 1024

ADAM_LR = 0.001
ADAM_B1 = 0.9
ADAM_B2 = 0.999
ADAM_EPS = 1e-08
ADAM_WD = 0.01
ADAM_STEP = 10

MESH_ID = pl.DeviceIdType.MESH
NN = (((1,), (0,)), ((), ()))
NT = (((1,), (1,)), ((), ()))
TN = (((0,), (0,)), ((), ()))
_DN = {"nn": NN, "nt": NT, "tn": TN}


def _cparams(sem):
    return pltpu.CompilerParams(dimension_semantics=sem, vmem_limit_bytes=VMEM_LIMIT)


MM_TILES = (1024, 768, 512)


def _tile(dim):
    return next((t for t in MM_TILES if dim % t == 0), dim)


NT_BLOCKS_PER_STEP = 4


def _mm(a, b, mode, out_dtype, b_blocks=False, out_blocks=False, a_cols=None, after=()):
    if b_blocks:
        nblk, rows, cb = b.shape
        b2_shape = (rows, nblk * cb)
    else:
        b2_shape = b.shape
    a_shape = a.shape if a_cols is None else (a.shape[0], a_cols[1])
    if mode == "nn":
        (m, k), n = a_shape, b2_shape[1]
    elif mode == "nt":
        (m, k), n = a_shape, b2_shape[0]
    else:
        (k, m), n = a_shape, b2_shape[1]
    tm, tn, tk = _tile(m), _tile(n), _tile(k)
    per_step = 1
    if b_blocks and mode == "nn":
        tn = cb
    if b_blocks and mode == "nt":
        per_step = NT_BLOCKS_PER_STEP
        tk = per_step * cb
        tn = min(tn, MM_TILES[-1])
    if out_blocks:
        tn = n // N_DEV
        tk = k
    nk = k // tk
    a_unit = tm if mode == "tn" else tk
    assert a_cols is None or a_cols[0] % a_unit == 0
    a_off = 0 if a_cols is None else a_cols[0] // a_unit

    def body(a_ref, b_ref, *rest):
        o_ref, acc_ref = rest[-2:]
        kk = pl.program_id(2)
        if per_step == 1:
            part = lax.dot_general(a_ref[...].astype(BF16), b_ref[...].astype(BF16), _DN[mode],
                                   preferred_element_type=F32)
        else:
            part = None
            for blk in range(per_step):
                d = lax.dot_general(a_ref[:, blk * cb:(blk + 1) * cb].astype(BF16), b_ref[blk].astype(BF16), NT,
                                    preferred_element_type=F32)
                part = d if part is None else part + d
        if nk == 1:
            o_ref[...] = part.astype(o_ref.dtype)
            return

        @pl.when(kk == 0)
        def _():
            acc_ref[...] = part

        @pl.when((kk > 0) & (kk < nk - 1))
        def _():
            acc_ref[...] += part

        @pl.when(kk == nk - 1)
        def _():
            o_ref[...] = (acc_ref[...] + part).astype(o_ref.dtype)

    if mode == "nn":
        a_spec = pl.BlockSpec((tm, tk), lambda i, j, kk: (i, a_off + kk))
        b_spec = pl.BlockSpec((tk, tn), lambda i, j, kk: (kk, j))
    elif mode == "nt":
        a_spec = pl.BlockSpec((tm, tk), lambda i, j, kk: (i, a_off + kk))
        b_spec = pl.BlockSpec((tn, tk), lambda i, j, kk: (j, kk))
    else:
        a_spec = pl.BlockSpec((tk, tm), lambda i, j, kk: (kk, a_off + i))
        b_spec = pl.BlockSpec((tk, tn), lambda i, j, kk: (kk, j))
    if b_blocks and mode == "nn":
        b_spec = pl.BlockSpec((None, tk, cb), lambda i, j, kk: (j, kk, 0))
    if b_blocks and mode == "nt":
        b_spec = pl.BlockSpec((per_step, tn, cb), lambda i, j, kk: (kk, j, 0))
    out_spec = pl.BlockSpec((tm, tn), lambda i, j, kk: (i, j))
    out_shape = SDS((m, n), out_dtype)
    if out_blocks:
        out_spec = pl.BlockSpec((None, tm, tn), lambda i, j, kk: (j, i, 0))
        out_shape = SDS((N_DEV, m, tn), out_dtype)
    return pl.pallas_call(
        body, name=f"mm_{mode}_{m}x{k}x{n}",
        grid=(m // tm, n // tn, nk),
        in_specs=[a_spec, b_spec] + [pl.BlockSpec(memory_space=pl.ANY)] * len(after),
        out_specs=out_spec,
        out_shape=out_shape,
        scratch_shapes=[pltpu.VMEM((tm, tn) if nk > 1 else (8, LANES), F32)],
        compiler_params=_cparams(("parallel", "parallel", "arbitrary")),
    )(a, b, *after)


def _gmm(a, b, mode, out_dtype, tm=512):
    ng, gw = POOL_W // POOL_GROUP, POOL_GROUP
    ns = S // tm
    if mode in ("nn", "nt"):
        def body(a_ref, b_ref, o_ref):
            o_ref[...] = lax.dot_general(a_ref[...].astype(BF16), b_ref[...].astype(BF16), _DN[mode],
                                         preferred_element_type=F32).astype(o_ref.dtype)

        return pl.pallas_call(
            body, name=f"gmm_{mode}", grid=(ng, ns),
            in_specs=[pl.BlockSpec((tm, gw), lambda g, i: (i, g)),
                      pl.BlockSpec((None, gw, gw), lambda g, i: (g, 0, 0))],
            out_specs=pl.BlockSpec((tm, gw), lambda g, i: (i, g)),
            out_shape=SDS((S, POOL_W), out_dtype),
            compiler_params=_cparams(("parallel", "parallel")),
        )(a, b)

    def body_tn(a_ref, b_ref, o_ref, acc_ref):
        i = pl.program_id(1)

        @pl.when(i == 0)
        def _():
            acc_ref[...] = jnp.zeros_like(acc_ref)

        acc_ref[...] += lax.dot_general(a_ref[...].astype(BF16), b_ref[...].astype(BF16), TN,
                                        preferred_element_type=F32)

        @pl.when(i == ns - 1)
        def _():
            o_ref[...] = acc_ref[...].astype(o_ref.dtype)

    return pl.pallas_call(
        body_tn, name="gmm_tn", grid=(ng, ns),
        in_specs=[pl.BlockSpec((tm, gw), lambda g, i: (i, g)),
                  pl.BlockSpec((tm, gw), lambda g, i: (i, g))],
        out_specs=pl.BlockSpec((None, gw, gw), lambda g, i: (g, 0, 0)),
        out_shape=SDS((ng, gw, gw), out_dtype),
        scratch_shapes=[pltpu.VMEM((gw, gw), F32)],
        compiler_params=_cparams(("parallel", "arbitrary")),
    )(a, b)


def _rowwise(fn, inputs, out_defs, acc_defs=(), tm=256, name=None):
    n_in, n_out, n_acc = len(inputs), len(out_defs), len(acc_defs)
    in_specs, args = [], []
    for arr, width, cb in inputs:
        if arr.shape[0] == 1:
            in_specs.append(pl.BlockSpec((1, width), lambda i, cb=cb: (0, cb)))
        else:
            in_specs.append(pl.BlockSpec((tm, width), lambda i, cb=cb: (i, cb)))
        args.append(arr)
    out_defs = [d if len(d) == 4 else (d[0], d[1], d[0], 0) for d in out_defs]
    out_shape = [SDS((S, ww), dt) for _, dt, ww, _ in out_defs] + [SDS((1, w), F32) for w in acc_defs]
    out_specs = ([pl.BlockSpec((tm, w), lambda i, cb=cb: (i, cb)) for w, _, _, cb in out_defs]
                 + [pl.BlockSpec((1, w), lambda i: (0, 0)) for w in acc_defs])

    def kern(*refs):
        vals = [r[...] for r in refs[:n_in]]
        outs, accs = fn(*vals)
        for r, v in zip(refs[n_in:n_in + n_out], outs):
            r[...] = v.astype(r.dtype)
        if n_acc:
            acc_refs = refs[n_in + n_out:]

            @pl.when(pl.program_id(0) == 0)
            def _():
                for r in acc_refs:
                    r[...] = jnp.zeros_like(r)

            for r, v in zip(acc_refs, accs):
                r[...] += jnp.sum(v, axis=0, keepdims=True)

    res = pl.pallas_call(
        kern, name=name, grid=(S // tm,), in_specs=in_specs, out_specs=out_specs, out_shape=out_shape,
        compiler_params=_cparams(("arbitrary",)),
    )(*args)
    return res


def _sigmoid(x):
    return 1.0 / (1.0 + jnp.exp(-x))


def _silu_and_grad(x):
    s = _sigmoid(x)
    return x * s, s * (1.0 + x * (1.0 - s))


_GELU_K = math.sqrt(2.0 / math.pi)
_GELU_C = 0.044715


def _gelu_and_grad(x):
    t = jnp.tanh(_GELU_K * (x + _GELU_C * (x * x * x)))
    cdf = 0.5 * (1.0 + t)
    grad = cdf + 0.5 * x * (1.0 - t * t) * (_GELU_K * (1.0 + 3.0 * _GELU_C * x * x))
    return x * cdf, grad


def _rms(xv, gain):
    r = lax.rsqrt(jnp.mean(xv * xv, axis=-1, keepdims=True) + RMS_EPS)
    return xv * r * gain


def _rms_bwd(dout, xv, gain):
    r = lax.rsqrt(jnp.mean(xv * xv, axis=-1, keepdims=True) + RMS_EPS)
    xhat = xv * r
    dxhat = dout * gain
    dx = r * (dxhat - xhat * jnp.mean(dxhat * xhat, axis=-1, keepdims=True))
    return dx, dout * xhat


def _norm_fwd(x, gain):
    (h,) = _rowwise(lambda xv, g: ((_rms(xv, g),), ()), [(x, D, 0), (gain, D, 0)], [(D, BF16)], name="norm_fwd")
    return h


def _post_fwd(x, y, gain):
    (o,) = _rowwise(lambda xv, yv, g: ((xv + _rms(yv, g),), ()), [(x, D, 0), (y, D, 0), (gain, D, 0)],
                    [(D, F32)], name="post_fwd")
    return o


def _post_bwd(g, y, gain):
    def fn(gv, yv, gn):
        dx, dg = _rms_bwd(gv, yv, gn)
        return (dx,), (dg,)

    return _rowwise(fn, [(g, D, 0), (y, D, 0), (gain, D, 0)], [(D, BF16)], [D], name="post_bwd")


def _pre_bwd(g, dh, x, gain):
    def fn(gv, dhv, xv, gn):
        dx, dg = _rms_bwd(dhv, xv, gn)
        return (gv + dx,), (dg,)

    return _rowwise(fn, [(g, D, 0), (dh, D, 0), (x, D, 0), (gain, D, 0)], [(D, F32)], [D], name="pre_bwd")


def _loss_grad(xo, tgt):
    def fn(xv, tv):
        e = xv - tv
        return (e * (1.0 / D),), (e * e,)

    return _rowwise(fn, [(xo, D, 0), (tgt, D, 0)], [(D, F32)], [D], name="loss_grad")


def _pool(u_arr, col_block, transpose, out_dtype, into=None, tc=256):
    n_t = POOL_W // tc
    per_group = POOL_GROUP // tc

    def body(u_ref, *rest):
        o_ref = rest[-1]
        c = pl.program_id(0)
        grp = c // per_group
        xv = u_ref[...]
        t = lax.broadcasted_iota(jnp.int32, (S, 1), 0)
        win = jnp.left_shift(2, grp)
        cnt = jnp.minimum(t + 1, win).astype(F32)
        cur = xv / cnt if transpose else xv
        sums = []
        for k in (1, 2, 4, 8):
            if transpose:
                sh = jnp.where(t < S - k, pltpu.roll(cur, S - k, 0), 0.0)
            else:
                sh = jnp.where(t >= k, pltpu.roll(cur, k, 0), 0.0)
            cur = cur + sh
            sums.append(cur)
        tot = jnp.where(grp == 0, sums[0], jnp.where(grp == 1, sums[1], jnp.where(grp == 2, sums[2], sums[3])))
        res = tot - xv if transpose else tot / cnt - xv
        o_ref[...] = res.astype(o_ref.dtype)

    in_specs = [pl.BlockSpec((S, tc), lambda c: (0, col_block * n_t + c))]
    args = [u_arr]
    if into is not None:
        in_specs.append(pl.BlockSpec(memory_space=pl.ANY))
        args.append(into)
    return pl.pallas_call(
        body, name="pool_bwd" if transpose else "pool_fwd", grid=(n_t,),
        in_specs=in_specs,
        out_specs=pl.BlockSpec((S, tc), lambda c: (0, c)),
        out_shape=SDS((S, POOL_W) if into is None else into.shape, out_dtype),
        input_output_aliases={} if into is None else {1: 0},
        compiler_params=_cparams(("parallel",)),
    )(*args)


def _rope_tables(zero):
    pos = jnp.arange(S, dtype=jnp.int32).astype(F32) + zero
    inv_freq = ROPE_THETA ** (-jnp.arange(0, ROT_DIM, 2, dtype=F32) / ROT_DIM)
    ang = pos[:, None] * inv_freq[None, :]
    cos8, sin8 = jnp.cos(ang), jnp.sin(ang)
    half = ROT_DIM // 2
    zeros = jnp.zeros((S, HEAD_DIM - ROT_DIM), F32)
    cos = jnp.concatenate([cos8, cos8, jnp.ones((S, HEAD_DIM - ROT_DIM), F32)], axis=1)
    lo = jnp.concatenate([-sin8, jnp.zeros((S, half), F32), zeros], axis=1)
    hi = jnp.concatenate([jnp.zeros((S, half), F32), sin8, zeros], axis=1)
    rep = LANES // HEAD_DIM
    return jnp.tile(cos, (1, rep)), jnp.tile(lo, (1, rep)), jnp.tile(hi, (1, rep))


def _rotate(xv, cos, lo, hi, transpose):
    width = xv.shape[1]
    rep = width // LANES
    wide = lambda tab: jnp.concatenate([tab] * rep, axis=1)
    half = ROT_DIM // 2
    up = pltpu.roll(xv, width - half, 1)
    dn = pltpu.roll(xv, half, 1)
    mixed = up * wide(lo) + dn * wide(hi)
    return xv * wide(cos) - mixed if transpose else xv * wide(cos) + mixed


def _qkv_prep(proj, tables):
    cos, lo, hi = tables

    def fn(x, c, l, h):
        rot = _rotate(x[:, :2 * ATT_W], c, l, h, False)
        return (jnp.concatenate([(rot[:, :ATT_W] * HEAD_DIM ** -0.5).astype(BF16), rot[:, ATT_W:].astype(BF16),
                                 x[:, 2 * ATT_W:].astype(BF16)], axis=1),), ()

    (qkv,) = _rowwise(fn, [(proj, 3 * ATT_W, 0), (cos, LANES, 0), (lo, LANES, 0), (hi, LANES, 0)],
                      [(3 * ATT_W, BF16)], name="qkv_prep")
    return qkv


ATT_T = 512


def _multiplicity(delta):
    ok = delta >= 0
    near = jnp.where(ok & (delta <= 128), 1.0, 0.0)
    mid = jnp.where(ok & (delta <= 512) & ((delta & 3) == 0), 1.0, 0.0)
    far = jnp.where(ok & ((delta & 15) == 0), 1.0, 0.0)
    return near + mid + far


def _attention_bias(zero):
    t = ATT_T
    pos = jnp.arange(t, dtype=jnp.int32) + jnp.asarray(zero).astype(jnp.int32)
    delta = jnp.arange(S // t, dtype=jnp.int32)[:, None, None] * t + pos[None, :, None] - pos[None, None, :]
    mult = _multiplicity(delta)
    return jnp.where(mult > 0.0, jnp.log(jnp.maximum(mult, 1.0)), -1e30).astype(F32)


def _head_split(v, first):
    zero = jnp.zeros_like(v)
    return [jnp.where(first, v, zero), jnp.where(first, zero, v)]


def _flash_fwd(qkv, bias):
    t = ATT_T
    n_hp = ATT_W // LANES

    def body(q_ref, k_ref, v_ref, b_ref, o_ref, lse_ref):
        i = pl.program_id(1)
        first = lax.broadcasted_iota(jnp.int32, (1, LANES), 1) < HEAD_DIM
        qs = _head_split(q_ref[...], first)

        def kv_step(j, carry):
            m0, l0, m1, l1, acc = carry
            off = pl.multiple_of(j * t, t)
            kb = k_ref[pl.ds(off, t), :]
            vs = _head_split(v_ref[pl.ds(off, t), :], first)
            bias_t = b_ref[i - j]
            new = []
            pv = None
            for h, (m_prev, l_prev) in enumerate(((m0, l0), (m1, l1))):
                s = lax.dot_general(qs[h], kb, NT, preferred_element_type=F32) + bias_t
                m_new = jnp.maximum(m_prev, jnp.max(s, axis=1, keepdims=True))
                p = jnp.exp(s - m_new)
                alpha = jnp.exp(m_prev - m_new)
                l_new = alpha * l_prev + jnp.sum(p, axis=1, keepdims=True)
                d = lax.dot_general(p.astype(BF16), vs[h], NN, preferred_element_type=F32)
                pv = d if pv is None else pv + d
                new.append((m_new, l_new, alpha))
            acc = acc * jnp.where(first, new[0][2], new[1][2]) + pv
            return new[0][0], new[0][1], new[1][0], new[1][1], acc

        neg = jnp.full((t, 1), -1e30, F32)
        zero = jnp.zeros((t, 1), F32)
        m0, l0, m1, l1, acc = lax.fori_loop(0, i + 1, kv_step, (neg, zero, neg, zero, jnp.zeros((t, LANES), F32)))
        o_ref[...] = acc * jnp.where(first, 1.0 / l0, 1.0 / l1)
        lse_ref[...] = jnp.where(first, m0 + jnp.log(l0), m1 + jnp.log(l1))

    blk = pl.BlockSpec((t, LANES), lambda hp, i: (i, hp))
    k_full = pl.BlockSpec((S, LANES), lambda hp, i: (0, n_hp + hp))
    v_full = pl.BlockSpec((S, LANES), lambda hp, i: (0, 2 * n_hp + hp))
    return pl.pallas_call(
        body, name="flash_fwd", grid=(n_hp, S // t),
        in_specs=[blk, k_full, v_full, pl.BlockSpec((S // t, t, t), lambda hp, i: (0, 0, 0))], out_specs=[blk, blk],
        out_shape=[SDS((S, ATT_W), F32), SDS((S, ATT_W), F32)],
        compiler_params=_cparams(("parallel", "arbitrary")),
    )(qkv, qkv, qkv, bias)


def _flash_bwd(qkv, o, do, lse, bias):
    t = ATT_T
    n_hp = ATT_W // LANES
    n_t = S // t

    def body(q_ref, k_ref, v_ref, o_ref, do_ref, lse_ref, b_ref, dq_ref, dk_ref, dv_ref):
        j = pl.program_id(1)
        first = lax.broadcasted_iota(jnp.int32, (1, LANES), 1) < HEAD_DIM

        @pl.when(j == 0)
        def _():
            dq_ref[...] = jnp.zeros_like(dq_ref)

        kb = k_ref[...]
        vb = v_ref[...]
        ks = _head_split(kb, first)

        def q_step(i, carry):
            dk_acc, dv_acc = carry
            rows = pl.ds(pl.multiple_of(i * t, t), t)
            qs = _head_split(q_ref[rows, :], first)
            dob = do_ref[rows, :]
            prod = dob * o_ref[rows, :]
            d_all = jnp.sum(prod, axis=1, keepdims=True)
            d0 = jnp.sum(jnp.where(first, prod, 0.0), axis=1, keepdims=True)
            lse_b = lse_ref[rows, :]
            lse0 = jnp.max(jnp.where(first, lse_b, -jnp.inf), axis=1, keepdims=True)
            lse1 = jnp.max(jnp.where(first, -jnp.inf, lse_b), axis=1, keepdims=True)
            dos = _head_split(dob.astype(BF16), first)
            bias_t = b_ref[i - j]
            dq_t = jnp.zeros((t, LANES), F32)
            for h, (lse_h, d_h) in enumerate(((lse0, d0), (lse1, d_all - d0))):
                s = lax.dot_general(qs[h], kb, NT, preferred_element_type=F32)
                p = jnp.exp(s + (bias_t - lse_h))
                dp = lax.dot_general(dos[h], vb, NT, preferred_element_type=F32)
                ds = (p * (dp - d_h)).astype(BF16)
                dv_acc = dv_acc + lax.dot_general(p.astype(BF16), dos[h], TN, preferred_element_type=F32)
                dk_acc = dk_acc + lax.dot_general(ds, qs[h], TN, preferred_element_type=F32)
                dq_t = dq_t + lax.dot_general(ds, ks[h], NN, preferred_element_type=F32)
            dq_ref[rows, :] += dq_t
            return dk_acc, dv_acc

        zero = jnp.zeros((t, LANES), F32)
        dk_acc, dv_acc = lax.fori_loop(j, n_t, q_step, (zero, zero))
        dk_ref[...] = dk_acc
        dv_ref[...] = dv_acc

    blk = pl.BlockSpec((t, LANES), lambda hp, j: (j, hp))
    full = pl.BlockSpec((S, LANES), lambda hp, j: (0, hp))
    k_blk = pl.BlockSpec((t, LANES), lambda hp, j: (j, n_hp + hp))
    v_blk = pl.BlockSpec((t, LANES), lambda hp, j: (j, 2 * n_hp + hp))
    return pl.pallas_call(
        body, name="flash_bwd", grid=(n_hp, n_t),
        in_specs=[full, k_blk, v_blk, full, full, full, pl.BlockSpec((n_t, t, t), lambda hp, j: (0, 0, 0))],
        out_specs=[full, blk, blk],
        out_shape=[SDS((S, ATT_W), F32)] * 3,
        compiler_params=_cparams(("parallel", "arbitrary")),
    )(qkv, qkv, qkv, o, do, lse, bias)


SCAN_T = 256
ST_ROWS = 2 * N_CPLX // LANES
HALF = ST_ROWS // 2


def _scan_fwd(lam, bu):
    def body(lam_ref, bu_ref, st_ref, carry):
        @pl.when(pl.program_id(0) == 0)
        def _():
            carry[...] = jnp.zeros_like(carry)

        ar, ai = lam_ref[0:HALF, :], lam_ref[HALF:ST_ROWS, :]

        def step(t, c):
            sr, si = c
            b = bu_ref[t]
            nr = ar * sr - ai * si + b[0:HALF]
            ni = ar * si + ai * sr + b[HALF:ST_ROWS]
            st_ref[t, 0:HALF, :] = nr
            st_ref[t, HALF:ST_ROWS, :] = ni
            return nr, ni

        sr, si = lax.fori_loop(0, SCAN_T, step, (carry[0:HALF, :], carry[HALF:ST_ROWS, :]), unroll=8)
        carry[0:HALF, :] = sr
        carry[HALF:ST_ROWS, :] = si

    blk = pl.BlockSpec((SCAN_T, ST_ROWS, LANES), lambda i: (i, 0, 0))
    return pl.pallas_call(
        body, name="scan_fwd", grid=(S // SCAN_T,),
        in_specs=[pl.BlockSpec((ST_ROWS, LANES), lambda i: (0, 0)), blk], out_specs=blk,
        out_shape=SDS((S, ST_ROWS, LANES), F32),
        scratch_shapes=[pltpu.VMEM((ST_ROWS, LANES), F32)],
        compiler_params=_cparams(("arbitrary",)),
    )(lam, bu)


def _scan_bwd(lam, dst, states):
    n_blk = S // SCAN_T

    def body(lam_ref, d_ref, st_ref, g_ref, dlam_ref, carry):
        @pl.when(pl.program_id(0) == 0)
        def _():
            carry[...] = jnp.zeros_like(carry)
            dlam_ref[...] = jnp.zeros_like(dlam_ref)

        ar, ai = lam_ref[0:HALF, :], lam_ref[HALF:ST_ROWS, :]

        def step(kk, c):
            t = SCAN_T - 1 - kk
            gr, gi, dar, dai = c
            x = st_ref[t]
            xr, xi = x[0:HALF], x[HALF:ST_ROWS]
            dar = dar + gr * xr + gi * xi
            dai = dai + gi * xr - gr * xi
            d = d_ref[t]
            ngr = d[0:HALF] + ar * gr + ai * gi
            ngi = d[HALF:ST_ROWS] + ar * gi - ai * gr
            g_ref[t, 0:HALF, :] = ngr
            g_ref[t, HALF:ST_ROWS, :] = ngi
            return ngr, ngi, dar, dai

        zero = jnp.zeros((HALF, LANES), F32)
        gr, gi, dar, dai = lax.fori_loop(0, SCAN_T, step, (carry[0:HALF, :], carry[HALF:ST_ROWS, :], zero, zero),
                                         unroll=8)
        carry[0:HALF, :] = gr
        carry[HALF:ST_ROWS, :] = gi
        dlam_ref[0:HALF, :] += dar
        dlam_ref[HALF:ST_ROWS, :] += dai

    blk = pl.BlockSpec((SCAN_T, ST_ROWS, LANES), lambda i: (n_blk - 1 - i, 0, 0))
    small = pl.BlockSpec((ST_ROWS, LANES), lambda i: (0, 0))
    return pl.pallas_call(
        body, name="scan_bwd", grid=(n_blk,),
        in_specs=[small, blk, blk], out_specs=[blk, small],
        out_shape=[SDS((S, ST_ROWS, LANES), F32), SDS((ST_ROWS, LANES), F32)],
        scratch_shapes=[pltpu.VMEM((ST_ROWS, LANES), F32)],
        compiler_params=_cparams(("arbitrary",)),
    )(lam, dst, states)


def _ssm_prep(a_re, a_im, log_dt, b_re, b_im, c_re, c_im):
    lam = lax.complex(a_re, a_im)
    dt = jnp.exp(log_dt)[:, None]
    lam_bar = jnp.exp(lam * dt)
    b_bar = ((lam_bar - 1.0) / lam)[..., None] * lax.complex(b_re, b_im)
    lam_t = jnp.concatenate([jnp.real(lam_bar).reshape(HALF, LANES), jnp.imag(lam_bar).reshape(HALF, LANES)], axis=0)
    groups_per_super = SSM_GROUPS // SSM_SUPER
    on_diag = ((lax.broadcasted_iota(jnp.int32, (SSM_W, SB_COLS), 0) // SSM_GROUP) % groups_per_super
               == lax.broadcasted_iota(jnp.int32, (SSM_W, SB_COLS), 1) // SSM_STATE)

    def compact(m):
        return jnp.where(on_diag, jnp.tile(m.reshape(SSM_W, SSM_STATE), (1, groups_per_super)), 0.0)

    w_b = jnp.concatenate([compact(jnp.real(b_bar).transpose(0, 2, 1)),
                           compact(jnp.imag(b_bar).transpose(0, 2, 1))], axis=1)
    w_ct = jnp.concatenate([compact(c_re), -compact(c_im)], axis=1)
    return lam_t, w_b, w_ct


SSM_SUPER = 4
SB_ROWS = SSM_W // SSM_SUPER
SB_COLS = N_CPLX // SSM_SUPER


def _bdmm(a, w, mode, out_dtype, a_cols=None):
    a_off = 0 if a_cols is None else a_cols[0] // SB_ROWS
    if mode == "nn":
        def body(a_ref, w_ref, o_ref):
            o_ref[...] = lax.dot_general(a_ref[...].astype(BF16), w_ref[...].astype(BF16), NN,
                                         preferred_element_type=F32).astype(o_ref.dtype)

        return pl.pallas_call(
            body, name="bdmm_nn", grid=(2, SSM_SUPER),
            in_specs=[pl.BlockSpec((S, SB_ROWS), lambda h, b: (0, a_off + b)),
                      pl.BlockSpec((SB_ROWS, SB_COLS), lambda h, b: (b, h))],
            out_specs=pl.BlockSpec((S, SB_COLS), lambda h, b: (0, h * SSM_SUPER + b)),
            out_shape=SDS((S, 2 * N_CPLX), out_dtype),
            compiler_params=_cparams(("parallel", "parallel")),
        )(a, w)
    if mode == "nt":
        def body(re_ref, im_ref, wre_ref, wim_ref, o_ref):
            acc = lax.dot_general(re_ref[...].astype(BF16), wre_ref[...].astype(BF16), NT, preferred_element_type=F32)
            acc += lax.dot_general(im_ref[...].astype(BF16), wim_ref[...].astype(BF16), NT, preferred_element_type=F32)
            o_ref[...] = acc.astype(o_ref.dtype)

        return pl.pallas_call(
            body, name="bdmm_nt", grid=(SSM_SUPER,),
            in_specs=[pl.BlockSpec((S, SB_COLS), lambda b: (0, b)),
                      pl.BlockSpec((S, SB_COLS), lambda b: (0, SSM_SUPER + b)),
                      pl.BlockSpec((SB_ROWS, SB_COLS), lambda b: (b, 0)),
                      pl.BlockSpec((SB_ROWS, SB_COLS), lambda b: (b, 1))],
            out_specs=pl.BlockSpec((S, SB_ROWS), lambda b: (0, b)),
            out_shape=SDS((S, SSM_W), out_dtype),
            compiler_params=_cparams(("parallel",)),
        )(a, a, w, w)

    def body_tn(a_ref, w_ref, o_ref):
        o_ref[...] = lax.dot_general(a_ref[...].astype(BF16), w_ref[...].astype(BF16), TN,
                                     preferred_element_type=F32).astype(o_ref.dtype)

    return pl.pallas_call(
        body_tn, name="bdmm_tn", grid=(2, SSM_SUPER),
        in_specs=[pl.BlockSpec((S, SB_ROWS), lambda h, b: (0, a_off + b)),
                  pl.BlockSpec((S, SB_COLS), lambda h, b: (0, h * SSM_SUPER + b))],
        out_specs=pl.BlockSpec((SB_ROWS, SB_COLS), lambda h, b: (b, h)),
        out_shape=SDS((SSM_W, 2 * SB_COLS), out_dtype),
        compiler_params=_cparams(("parallel", "parallel")),
    )(a, w)


U_SSM_COLS = (4 * ATT_W, SSM_W)


def _row(v):
    return v.reshape(1, -1)


def _even_fwd(x, pre, post, w_in, late_w, glu_b, ssm_d, prep, tables):
    lam_t, w_b, w_ct = prep
    h = _norm_fwd(x, pre)
    proj = _mm(h, w_in, "nn", F32, b_blocks=True)
    qkv = _qkv_prep(proj, tables[:3])
    w_out, glu_w = late_w(qkv)
    att, lse = _flash_fwd(qkv, tables[3])
    bu = _bdmm(proj, w_b, "nn", F32, a_cols=U_SSM_COLS)
    states = _scan_fwd(lam_t, bu.reshape(S, ST_ROWS, LANES))
    y = _bdmm(states.reshape(S, 2 * N_CPLX), w_ct, "nt", F32)

    def act1(yv, uv, dv):
        return (_gelu_and_grad(yv + dv * uv)[0],), ()

    (z1,) = _rowwise(act1, [(y, SSM_W, 0), (proj, SSM_W, 8), (ssm_d, SSM_W, 0)], [(SSM_W, F32)], name="ssm_act_fwd")
    lin = _mm(z1, glu_w, "nn", F32)

    def gate(att_v, ga, gs, z1v, linv, bv):
        ssm_out = z1v * _sigmoid(linv + bv)
        return (jnp.concatenate([att_v * _silu_and_grad(ga)[0], ssm_out * _silu_and_grad(gs)[0]], axis=1),), ()

    (merged,) = _rowwise(gate, [(att, ATT_W, 0), (proj, ATT_W, 3), (proj, SSM_W, 9), (z1, SSM_W, 0),
                                (lin, SSM_W, 0), (glu_b, SSM_W, 0)], [(EVEN_OUT, BF16)], name="even_gate_fwd")
    yout = _mm(merged, w_out, "nn", F32)
    x_next = _post_fwd(x, yout, post)
    saved = (x, h, proj, qkv, att, lse, states, y, z1, lin, merged, yout, w_out, glu_w)
    return x_next, saved


def _even_bwd(g, saved, pre, post, w_in, late_w, glu_b, ssm_d, prep, tables, on_w):
    x, h, proj, qkv, att, lse, states, y, z1, lin, merged, yout, w_out, glu_w = saved
    lam_t, w_b, w_ct = prep
    dyout, dpost = _post_bwd(g, yout, post)
    dmerged = _mm(dyout, w_out, "nt", F32)
    dw_out = _mm(merged, dyout, "tn", BF16)

    def gate_bwd(dm_a, dm_s, att_v, ga, gs, z1v, linv, bv):
        sa, dsa = _silu_and_grad(ga)
        ss, dss = _silu_and_grad(gs)
        sig = _sigmoid(linv + bv)
        ssm_out = z1v * sig
        dssm = dm_s * ss
        dlin = dssm * z1v * sig * (1.0 - sig)
        return (dm_a * sa, dm_a * att_v * dsa, dm_s * ssm_out * dss, dssm * sig, dlin), (dlin,)

    datt, dg_att, dg_ssm, dz1a, dlin, dglu_b = _rowwise(
        gate_bwd, [(dmerged, ATT_W, 0), (dmerged, SSM_W, 2), (att, ATT_W, 0), (proj, ATT_W, 3), (proj, SSM_W, 9),
                   (z1, SSM_W, 0), (lin, SSM_W, 0), (glu_b, SSM_W, 0)],
        [(ATT_W, F32), (ATT_W, BF16), (SSM_W, BF16), (SSM_W, F32), (SSM_W, BF16)], [SSM_W], name="even_gate_bwd")
    dz1b = _mm(dlin, glu_w, "nt", F32)
    dglu_w = _mm(z1, dlin, "tn", BF16)

    def act1_bwd(da, db, yv, uv, dv):
        dpre = (da + db) * _gelu_and_grad(yv + dv * uv)[1]
        return (dpre, dpre * dv), (dpre * uv,)

    dy, du_direct, dd = _rowwise(act1_bwd, [(dz1a, SSM_W, 0), (dz1b, SSM_W, 0), (y, SSM_W, 0), (proj, SSM_W, 8),
                                            (ssm_d, SSM_W, 0)], [(SSM_W, BF16), (SSM_W, F32)], [SSM_W],
                                 name="ssm_act_bwd")
    dst = _bdmm(dy, w_ct, "nn", F32)
    dw_ct = _bdmm(dy, states.reshape(S, 2 * N_CPLX), "tn", F32)
    gst, dlam = _scan_bwd(lam_t, dst.reshape(S, ST_ROWS, LANES), states)
    dbu = gst.reshape(S, 2 * N_CPLX)
    du_state = _bdmm(dbu, w_b, "nt", F32)
    dw_b = _bdmm(proj, dbu, "tn", F32, a_cols=U_SSM_COLS)
    dq, dk, dv = _flash_bwd(qkv, att, datt, lse, tables[3])

    def assemble(dqv, dkv, dvv, dga, dua, dub, dgs, c, l, h):
        rot = _rotate(jnp.concatenate([dqv, dkv], axis=1), c, l, h, True)
        return (jnp.concatenate([(rot[:, :ATT_W] * HEAD_DIM ** -0.5).astype(BF16), rot[:, ATT_W:].astype(BF16),
                                 dvv.astype(BF16), dga, (dua + dub).astype(BF16), dgs], axis=1),), ()

    (dproj,) = _rowwise(assemble, [(dq, ATT_W, 0), (dk, ATT_W, 0), (dv, ATT_W, 0), (dg_att, ATT_W, 0),
                                   (du_state, SSM_W, 0), (du_direct, SSM_W, 0), (dg_ssm, SSM_W, 0),
                                   (tables[0], LANES, 0), (tables[1], LANES, 0), (tables[2], LANES, 0)],
                        [(EVEN_IN, BF16)], name="dproj_assemble")
    dw_in = _mm(h, dproj, "tn", BF16, out_blocks=True)
    sent = on_w(dict(w_in=dw_in, w_out=dw_out, glu_w=dglu_w))
    dh = _mm(dproj, w_in, "nt", F32, b_blocks=True, after=(sent,))
    g_prev, dpre = _pre_bwd(g, dh, x, pre)
    return g_prev, dict(pre=dpre, post=dpost, glu_b=dglu_b, ssm_d=dd, prep=(dlam, dw_b, dw_ct))


def _odd_fwd(x, pre, post, w_in, pool_w, pool_scale, w_out):
    h = _norm_fwd(x, pre)
    proj = _mm(h, w_in, "nn", F32, b_blocks=True)
    mixed = _pool(proj, 0, False, BF16)
    ylin = _gmm(mixed, pool_w, "nn", F32)

    def gate(yl, gt, sc):
        return (yl * sc * _silu_and_grad(gt)[0],), ()

    (z,) = _rowwise(gate, [(ylin, POOL_W, 0), (proj, POOL_W, 1), (pool_scale, POOL_W, 0)], [(POOL_W, BF16)],
                    name="odd_gate_fwd")
    yout = _mm(z, w_out, "nn", F32)
    x_next = _post_fwd(x, yout, post)
    return x_next, (x, h, proj, mixed, ylin, z, yout)


def _odd_bwd(g, saved, pre, post, w_in, pool_w, pool_scale, w_out, on_w):
    x, h, proj, mixed, ylin, z, yout = saved
    dyout, dpost = _post_bwd(g, yout, post)
    dz = _mm(dyout, w_out, "nt", F32)
    dw_out = _mm(z, dyout, "tn", BF16)

    def gate_bwd(dzv, yl, gt, sc):
        sg, dsg = _silu_and_grad(gt)
        tt = dzv * sg
        return (tt * sc, dzv * yl * sc * dsg), (tt * yl,)

    dylin, dproj_gate, dscale = _rowwise(gate_bwd, [(dz, POOL_W, 0), (ylin, POOL_W, 0), (proj, POOL_W, 1),
                                                    (pool_scale, POOL_W, 0)],
                                         [(POOL_W, BF16), (POOL_W, BF16, ODD_IN, 1)], [POOL_W], name="odd_gate_bwd")
    dmixed = _gmm(dylin, pool_w, "nt", F32)
    dpool_w = _gmm(mixed, dylin, "tn", BF16)
    dproj = _pool(dmixed, 0, True, BF16, into=dproj_gate)
    dw_in = _mm(h, dproj, "tn", BF16, out_blocks=True)
    sent = on_w(dict(w_in=dw_in, w_out=dw_out, pool_w=dpool_w))
    dh = _mm(dproj, w_in, "nt", F32, b_blocks=True, after=(sent,))
    g_prev, dpre = _pre_bwd(g, dh, x, pre)
    return g_prev, dict(pre=dpre, post=dpost, pool_scale=dscale)


def _my_index():
    return 4 * lax.axis_index("x") + 2 * lax.axis_index("y") + lax.axis_index("c")


def _exchange(arrs, gather, name, after=()):
    n = len(arrs)
    out_shape = [SDS((N_DEV,) + a.shape, a.dtype) if gather else SDS(a.shape, a.dtype) for a in arrs]

    def body(*refs):
        ins, outs = refs[:n], refs[n + len(after):2 * n + len(after)]
        send_sems, recv_sems, local_sems = refs[2 * n + len(after):]
        me = _my_index()

        def src(i, j):
            return ins[i] if gather else ins[i].at[j]

        def remote(i, j, src_slot, dst_slot, recv_slot):
            return pltpu.make_async_remote_copy(
                src_ref=src(i, src_slot), dst_ref=outs[i].at[dst_slot], send_sem=send_sems.at[i, j],
                recv_sem=recv_sems.at[i, recv_slot], device_id=(j // 4, (j // 2) % 2, j % 2), device_id_type=MESH_ID)

        def local(i):
            return pltpu.make_async_copy(src(i, me), outs[i].at[me], local_sems.at[i])

        for i in range(n):
            local(i).start()
        for j in range(N_DEV):
            @pl.when(me != j)
            def _(j=j):
                for i in range(n):
                    remote(i, j, j, me, me).start()
        for j in range(N_DEV):
            @pl.when(me != j)
            def _(j=j):
                for i in range(n):
                    remote(i, j, j, me, me).wait_send()
                    remote(i, j, j, j, j).wait_recv()
        for i in range(n):
            local(i).wait()

    any_spec = pl.BlockSpec(memory_space=pl.ANY)
    return pl.pallas_call(
        body, name=name, in_specs=[any_spec] * (n + len(after)), out_specs=[any_spec] * n, out_shape=out_shape,
        scratch_shapes=[pltpu.SemaphoreType.DMA((n, N_DEV)), pltpu.SemaphoreType.DMA((n, N_DEV)),
                        pltpu.SemaphoreType.DMA((n,))],
    )(*arrs, *after)


HBM_SPEC = pl.BlockSpec(memory_space=pltpu.HBM)
SEM_SPEC = pl.BlockSpec(memory_space=pltpu.SEMAPHORE)
SPLIT_EFFECT = pltpu.SideEffectType.DATAFLOW_SIDE_EFFECTING


def _device_of(j):
    return (j // 4, (j // 2) % 2, j % 2)


def _split_copy(srcs, lands, send_sems, recv_sems, gather, i, j, dst_slot, recv_slot):
    return pltpu.make_async_remote_copy(
        src_ref=srcs[i] if gather else srcs[i].at[j], dst_ref=lands[i].at[dst_slot],
        send_sem=send_sems.at[i * N_DEV + j], recv_sem=recv_sems.at[i * N_DEV + recv_slot],
        device_id=_device_of(j), device_id_type=MESH_ID)


def _own_copy(srcs, lands, send_sems, gather, i, me):
    return pltpu.make_async_copy(srcs[i] if gather else srcs[i].at[me], lands[i].at[me], send_sems.at[i * N_DEV + me])


def _xchg_start(name, srcs, gather, after=()):
    n = len(srcs)
    n_in = n + len(after)

    def body(*refs):
        src_refs = refs[:n]
        send_sems, recv_sems, token = refs[n_in], refs[n_in + 1], refs[-1]
        land_refs = refs[n_in + 2 + n:n_in + 2 + 2 * n]
        me = _my_index()
        for j in range(N_DEV):
            @pl.when(me != j)
            def _(j=j):
                for i in range(n):
                    _split_copy(src_refs, land_refs, send_sems, recv_sems, gather, i, j, me, me).start()
        for i in range(n):
            _own_copy(src_refs, land_refs, send_sems, gather, i, me).start()
        token[...] = jnp.zeros_like(token)

    land_shapes = [((N_DEV,) + a.shape) if gather else a.shape for a in srcs]
    thru = ([pltpu.HBM(a.shape, a.dtype) for a in srcs] + [pltpu.HBM(s, a.dtype) for s, a in zip(land_shapes, srcs)])
    res = pl.pallas_call(
        body, name=name,
        out_shape=(pltpu.SemaphoreType.DMA((n * N_DEV,)), pltpu.SemaphoreType.DMA((n * N_DEV,)), *thru,
                   SDS((8, LANES), F32)),
        in_specs=[HBM_SPEC] * n + [pl.BlockSpec(memory_space=pl.ANY)] * len(after),
        out_specs=(SEM_SPEC, SEM_SPEC, *([HBM_SPEC] * (2 * n)), pl.BlockSpec(memory_space=pltpu.VMEM)),
        input_output_aliases={i: 2 + i for i in range(n)},
        compiler_params=pltpu.CompilerParams(has_side_effects=SPLIT_EFFECT),
    )(*[pltpu.with_memory_space_constraint(a, pltpu.HBM) for a in srcs], *after)
    return res[0], res[1], list(res[2:2 + n]), list(res[2 + n:2 + 2 * n]), res[-1]


def _xchg_wait(name, started, gather, after):
    send_sems, recv_sems, srcs, lands, _ = started
    n = len(srcs)

    def body(*refs):
        src_refs, land_refs = refs[:n], refs[n:2 * n]
        send_r, recv_r = refs[2 * n], refs[2 * n + 1]
        me = _my_index()
        for j in range(N_DEV):
            @pl.when(me != j)
            def _(j=j):
                for i in range(n):
                    _split_copy(src_refs, land_refs, send_r, recv_r, gather, i, j, me, me).wait_send()
                    _split_copy(src_refs, land_refs, send_r, recv_r, gather, i, j, j, j).wait_recv()
        for i in range(n):
            _own_copy(src_refs, land_refs, send_r, gather, i, me).wait()

    thru = [pltpu.HBM(a.shape, a.dtype) for a in list(srcs) + list(lands)]
    res = pl.pallas_call(
        body, name=name, out_shape=tuple(thru),
        in_specs=[HBM_SPEC] * (2 * n) + [SEM_SPEC, SEM_SPEC] + [pl.BlockSpec(memory_space=pl.ANY)] * len(after),
        out_specs=tuple([HBM_SPEC] * (2 * n)),
        input_output_aliases={i: i for i in range(2 * n)},
        compiler_params=pltpu.CompilerParams(has_side_effects=SPLIT_EFFECT),
    )(*srcs, *lands, send_sems, recv_sems, *after)
    return list(res[n:])


def _adam_layers(w, slot_list, m, v, name):
    n_l, r, c = w.shape
    ns = slot_list[0].shape[0]
    tr = r
    while tr * c * 4 > (1 << 20) and tr % 16 == 0:
        tr //= 2
    assert r % tr == 0 and len(slot_list) == n_l

    def body(*refs):
        w_ref, slot_refs = refs[0], refs[1:1 + n_l]
        m_ref, v_ref, go_ref, d_ref, mo_ref, vo_ref = refs[1 + n_l:]
        layer = pl.program_id(0)
        g = None
        for l, g_ref in enumerate(slot_refs):
            gl = g_ref[0].astype(F32)
            for s in range(1, ns):
                gl = gl + g_ref[s].astype(F32)
            g = gl if g is None else jnp.where(layer == l, gl, g)
        mn = ADAM_B1 * m_ref[...] + (1.0 - ADAM_B1) * g
        vn = ADAM_B2 * v_ref[...] + (1.0 - ADAM_B2) * (g * g)
        m_hat = mn / (1.0 - ADAM_B1 ** ADAM_STEP)
        v_hat = vn / (1.0 - ADAM_B2 ** ADAM_STEP)
        go_ref[...] = g
        d_ref[...] = -ADAM_LR * (m_hat / (jnp.sqrt(v_hat) + ADAM_EPS) + ADAM_WD * w_ref[...])
        mo_ref[...] = mn
        vo_ref[...] = vn

    blk = pl.BlockSpec((None, tr, c), lambda l, i: (l, i, 0))
    slot_specs = [pl.BlockSpec((ns, tr, c), lambda l, i, k=k: (0, jnp.where(l == k, i, 0), 0)) for k in range(n_l)]
    return pl.pallas_call(
        body, name=name, grid=(n_l, r // tr),
        in_specs=[blk] + slot_specs + [blk, blk],
        out_specs=[blk] * 4, out_shape=[SDS((n_l, r, c), F32)] * 4,
        compiler_params=_cparams(("arbitrary", "arbitrary")),
    )(w, *slot_list, m, v)


def _adam(w, gslots, m, v, name):
    r, c = w.shape
    ns = gslots.shape[0]
    tr = r
    while tr * c * 4 > (1 << 20) and tr % 16 == 0:
        tr //= 2
    assert r % tr == 0

    def body(w_ref, g_ref, m_ref, v_ref, go_ref, d_ref, mo_ref, vo_ref):
        g = g_ref[0].astype(F32)
        for s in range(1, ns):
            g = g + g_ref[s].astype(F32)
        wv = w_ref[...]
        mn = ADAM_B1 * m_ref[...] + (1.0 - ADAM_B1) * g
        vn = ADAM_B2 * v_ref[...] + (1.0 - ADAM_B2) * (g * g)
        m_hat = mn / (1.0 - ADAM_B1 ** ADAM_STEP)
        v_hat = vn / (1.0 - ADAM_B2 ** ADAM_STEP)
        go_ref[...] = g
        d_ref[...] = -ADAM_LR * (m_hat / (jnp.sqrt(v_hat) + ADAM_EPS) + ADAM_WD * wv)
        mo_ref[...] = mn
        vo_ref[...] = vn

    blk = pl.BlockSpec((tr, c), lambda i: (i, 0))
    return pl.pallas_call(
        body, name=name, grid=(r // tr,),
        in_specs=[blk, pl.BlockSpec((ns, tr, c), lambda i: (0, i, 0)), blk, blk],
        out_specs=[blk] * 4, out_shape=[SDS((r, c), F32)] * 4,
        compiler_params=_cparams(("parallel",)),
    )(w, gslots, m, v)


SMALL_NAMES = ("pre_norm", "post_norm", "ssm_a_re", "ssm_a_im", "ssm_log_dt", "ssm_b_re", "ssm_b_im", "ssm_c_re",
               "ssm_c_im", "ssm_d", "ssm_glu_b")
SSM_NAMES = ("ssm_a_re", "ssm_a_im", "ssm_log_dt", "ssm_b_re", "ssm_b_im", "ssm_c_re", "ssm_c_im")
SHARDED_NAMES = ("even_w_in", "even_w_out", "ssm_glu_w", "odd_w_in", "pool_w", "odd_w_out")
WEIGHT_ORDER = ("pre_norm", "post_norm", "even_w_in", "even_w_out", "ssm_a_re", "ssm_a_im", "ssm_log_dt", "ssm_b_re",
                "ssm_b_im", "ssm_c_re", "ssm_c_im", "ssm_d", "ssm_glu_w", "ssm_glu_b", "odd_w_in", "pool_w",
                "pool_scale", "odd_w_out")
PACK_ROWS_ALIGN = 8


def _pack(parts):
    flat = jnp.concatenate([p.reshape(-1).astype(F32) for p in parts])
    rows = -(-flat.shape[0] // (LANES * PACK_ROWS_ALIGN)) * PACK_ROWS_ALIGN
    return jnp.pad(flat, (0, rows * LANES - flat.shape[0])).reshape(rows, LANES)


def _unpack(packed, shapes):
    flat = packed.reshape(-1)
    out, off = [], 0
    for shp in shapes:
        size = math.prod(shp)
        out.append(flat[off:off + size].reshape(shp))
        off += size
    return out


EVEN_SHARDED = ("w_in", "w_out", "glu_w")
ODD_SHARDED = ("w_in", "pool_w", "w_out")
FAMILY = {(0, "w_in"): "even_w_in", (0, "w_out"): "even_w_out", (0, "glu_w"): "ssm_glu_w",
          (1, "w_in"): "odd_w_in", (1, "pool_w"): "pool_w", (1, "w_out"): "odd_w_out"}


def _sharded_keys(layer):
    return EVEN_SHARDED if layer % 2 == 0 else ODD_SHARDED


def _local_step(x, tgt, small, get_weights, on_w, on_grads, zero=0.0):
    tables = _rope_tables(zero) + (_attention_bias(zero),)
    preps, prep_vjps = [], []
    for i in range(2):
        out, vjp = jax.vjp(_ssm_prep, small["ssm_a_re"][i] + zero, small["ssm_a_im"][i], small["ssm_log_dt"][i],
                           small["ssm_b_re"][i], small["ssm_b_im"][i], small["ssm_c_re"][i], small["ssm_c_im"][i])
        preps.append(out)
        prep_vjps.append(vjp)

    def layer_args(layer, wts):
        i = layer // 2
        pre, post = _row(small["pre_norm"][layer]) + wts.get("token", 0.0), _row(small["post_norm"][layer])
        if layer % 2 == 0:
            return (pre, post, wts["w_in"], wts["late"], _row(small["ssm_glu_b"][i]), _row(small["ssm_d"][i]),
                    preps[i], tables)
        return (pre, post, wts["w_in"], wts["pool_w"], _row(wts["pool_scale"]), wts["w_out"])

    saved, args = [], []
    cur = x
    for layer in range(4):
        after = (cur,) if layer else (cur, tables[0], tables[3], preps[0][1], preps[0][2], preps[1][1], preps[1][2])
        args.append(layer_args(layer, get_weights(layer, after)))
        cur, sv = (_even_fwd if layer % 2 == 0 else _odd_fwd)(cur, *args[layer])
        saved.append(sv)
    g, sq = _loss_grad(cur, tgt)
    loss = 0.5 * jnp.sum(sq) / D

    lg = [None] * 4
    token = jnp.zeros((), F32)
    for layer in reversed(range(4)):
        largs = list(args[layer])
        largs[1] = largs[1] + token
        g, lg[layer] = (_even_bwd if layer % 2 == 0 else _odd_bwd)(g, saved[layer], *largs,
                                                                   on_w=functools.partial(on_w, layer))
        if layer % 2 == 0:
            lg[layer]["ssm"] = prep_vjps[layer // 2](lg[layer].pop("prep"))
        token = on_grads(layer, lg[layer])
    return loss, g, token


def _to_slots(key, gfull):
    if key == "w_in":
        return gfull
    if key in ("w_out", "glu_w"):
        rr, nn = gfull.shape
        return gfull.reshape(N_DEV, rr // N_DEV, nn)
    assert key == "pool_w"
    gg, rr, nn = gfull.shape
    return gfull.reshape(gg, N_DEV, rr // N_DEV, nn).transpose(1, 0, 2, 3)


def _from_gathered(key, gat):
    if key == "w_in":
        return gat
    if key in ("w_out", "glu_w"):
        _, rr, nn = gat.shape
        return gat.reshape(N_DEV * rr, nn)
    assert key == "pool_w"
    _, gg, rr, nn = gat.shape
    return gat.transpose(1, 0, 2, 3).reshape(gg, N_DEV * rr, nn)


def kernel(x, pre_norm, post_norm, even_w_in, even_w_out, ssm_a_re, ssm_a_im, ssm_log_dt, ssm_b_re, ssm_b_im, ssm_c_re, ssm_c_im, ssm_d, ssm_glu_w, ssm_glu_b, odd_w_in, pool_w, pool_scale, odd_w_out, loss_target, m_pre_norm, m_post_norm, m_even_w_in, m_even_w_out, m_ssm_a_re, m_ssm_a_im, m_ssm_log_dt, m_ssm_b_re, m_ssm_b_im, m_ssm_c_re, m_ssm_c_im, m_ssm_d, m_ssm_glu_w, m_ssm_glu_b, m_odd_w_in, m_pool_w, m_pool_scale, m_odd_w_out, v_pre_norm, v_post_norm, v_even_w_in, v_even_w_out, v_ssm_a_re, v_ssm_a_im, v_ssm_log_dt, v_ssm_b_re, v_ssm_b_im, v_ssm_c_re, v_ssm_c_im, v_ssm_d, v_ssm_glu_w, v_ssm_glu_b, v_odd_w_in, v_pool_w, v_pool_scale, v_odd_w_out):
    w = dict(pre_norm=pre_norm, post_norm=post_norm, even_w_in=even_w_in, even_w_out=even_w_out, ssm_a_re=ssm_a_re,
             ssm_a_im=ssm_a_im, ssm_log_dt=ssm_log_dt, ssm_b_re=ssm_b_re, ssm_b_im=ssm_b_im, ssm_c_re=ssm_c_re,
             ssm_c_im=ssm_c_im, ssm_d=ssm_d, ssm_glu_w=ssm_glu_w, ssm_glu_b=ssm_glu_b, odd_w_in=odd_w_in,
             pool_w=pool_w, pool_scale=pool_scale, odd_w_out=odd_w_out)
    mom = dict(pre_norm=m_pre_norm, post_norm=m_post_norm, even_w_in=m_even_w_in, even_w_out=m_even_w_out,
               ssm_a_re=m_ssm_a_re, ssm_a_im=m_ssm_a_im, ssm_log_dt=m_ssm_log_dt, ssm_b_re=m_ssm_b_re,
               ssm_b_im=m_ssm_b_im, ssm_c_re=m_ssm_c_re, ssm_c_im=m_ssm_c_im, ssm_d=m_ssm_d, ssm_glu_w=m_ssm_glu_w,
               ssm_glu_b=m_ssm_glu_b, odd_w_in=m_odd_w_in, pool_w=m_pool_w, pool_scale=m_pool_scale,
               odd_w_out=m_odd_w_out)
    var = dict(pre_norm=v_pre_norm, post_norm=v_post_norm, even_w_in=v_even_w_in, even_w_out=v_even_w_out,
               ssm_a_re=v_ssm_a_re, ssm_a_im=v_ssm_a_im, ssm_log_dt=v_ssm_log_dt, ssm_b_re=v_ssm_b_re,
               ssm_b_im=v_ssm_b_im, ssm_c_re=v_ssm_c_re, ssm_c_im=v_ssm_c_im, ssm_d=v_ssm_d, ssm_glu_w=v_ssm_glu_w,
               ssm_glu_b=v_ssm_glu_b, odd_w_in=v_odd_w_in, pool_w=v_pool_w, pool_scale=v_pool_scale,
               odd_w_out=v_odd_w_out)
    me = _my_index()
    scale_cols = pool_scale.shape[1]

    def start_gather(tag, layer, keys, after=()):
        i = layer // 2
        shards = [w[FAMILY[(layer % 2, k)]][i].astype(BF16) for k in keys]
        if layer % 2 == 1:
            shards.append(jnp.pad(pool_scale[i][None], ((0, PACK_ROWS_ALIGN - 1), (0, 0))))
        return _xchg_start(f"gather_start_{tag}", shards, True, after)

    gather_started = {0: start_gather("0", 0, EVEN_SHARDED[:1])}
    small = {nm: w[nm] for nm in SMALL_NAMES}

    def get_weights(layer, after):
        keys = EVEN_SHARDED[:1] if layer == 0 else _sharded_keys(layer)
        lands = _xchg_wait(f"gather_wait_{layer}", gather_started[layer], True, after)
        wts = {k: _from_gathered(k, gat) for k, gat in zip(keys, lands)}
        if layer % 2 == 1:
            wts["pool_scale"] = lands[-1][:, 0, :].reshape(N_DEV * scale_cols)
        if layer == 0:
            gather_started["0_late"] = start_gather("0_late", 0, EVEN_SHARDED[1:], after=(lands[0],))
            for later in (1, 2, 3):
                gather_started[later] = start_gather(str(later), later, _sharded_keys(later), after=(lands[0],))
            wts["token"] = sum(gather_started[tag][4][0, 0] for tag in ("0_late", 1, 2, 3))

            def late(after_late):
                late_lands = _xchg_wait("gather_wait_0_late", gather_started["0_late"], True, (after_late,))
                return tuple(_from_gathered(k, gat) for k, gat in zip(EVEN_SHARDED[1:], late_lands))

            wts["late"] = late
        elif layer == 2:
            wts["late"] = lambda after_late: (wts["w_out"], wts["glu_w"])
        return wts

    scatter_started = [None] * 4

    def on_w(layer, gw):
        slots = [_to_slots(k, gw[k]) for k in _sharded_keys(layer)]
        scatter_started[layer] = _xchg_start(f"scatter_start_{layer}", slots, False)
        return scatter_started[layer][4]

    def small_split(tree):
        early = ([tree["pre_norm"][1:], tree["post_norm"][1:]] + [tree[nm][1] for nm in SSM_NAMES]
                 + [tree["ssm_d"][1:], tree["ssm_glu_b"][1:], jnp.zeros((2, N_DEV * scale_cols), F32)])
        late = ([tree["pre_norm"][:1], tree["post_norm"][:1]] + [tree[nm][0] for nm in SSM_NAMES]
                + [tree["ssm_d"][:1], tree["ssm_glu_b"][:1], jnp.zeros((1,), F32)])
        return early, late

    layer_grads = {}
    early_started = []

    def on_grads(layer, lg):
        layer_grads[layer] = lg
        zero = jnp.zeros((), F32)
        if layer == 1:
            lgs = layer_grads
            early = ([jnp.concatenate([lgs[l][k] for l in (1, 2, 3)], axis=0) for k in ("pre", "post")]
                     + list(lgs[2]["ssm"]) + [lgs[2]["ssm_d"], lgs[2]["glu_b"],
                                              jnp.concatenate([lgs[1]["pool_scale"], lgs[3]["pool_scale"]], axis=0)])
            early_started.append(_xchg_start("small_start", [_pack(early)], True))
            zero = zero + early_started[0][4][0, 0]
        return zero

    loss_local, grad_x, token = _local_step(x[0], loss_target[0], small, get_weights, on_w, on_grads,
                                            zero=gather_started[0][4][0, 0])

    def adam_family(parity, k):
        nm = FAMILY[(parity, k)]
        shp = w[nm].shape
        cols = shp[-1]
        slot_list = [recv[(parity + 2 * i, k)].reshape(N_DEV, -1, cols) for i in range(2)]
        outs = _adam_layers(w[nm].reshape(2, -1, cols), slot_list, mom[nm].reshape(2, -1, cols),
                            var[nm].reshape(2, -1, cols), name=f"adam_{nm}")
        return [o.reshape(shp) for o in outs]

    recv, res = {}, {}
    for layer in (3, 1):
        lands = _xchg_wait(f"scatter_wait_{layer}", scatter_started[layer], False, (scatter_started[0][4],))
        for k, land in zip(_sharded_keys(layer), lands):
            recv[(layer, k)] = land
    for k in ODD_SHARDED:
        res[FAMILY[(1, k)]] = adam_family(1, k)

    lg0 = layer_grads[0]
    late = [lg0["pre"], lg0["post"]] + list(lg0["ssm"]) + [lg0["ssm_d"], lg0["glu_b"], loss_local.reshape(1)]
    (late_slots,) = _exchange([_pack(late) + token], True, "gather_small_grads",
                              after=tuple(res[FAMILY[(1, k)]][0] for k in ODD_SHARDED))
    (early_slots,) = _xchg_wait("small_wait", early_started[0], True, (late_slots,))

    for layer in (2, 0):
        lands = _xchg_wait(f"scatter_wait_{layer}", scatter_started[layer], False, (late_slots,))
        for k, land in zip(_sharded_keys(layer), lands):
            recv[(layer, k)] = land
    for k in EVEN_SHARDED:
        res[FAMILY[(0, k)]] = adam_family(0, k)

    packs = [small_split(tree) for tree in (w, mom, var)]
    unpacked = []
    for which, slots in enumerate((early_slots, late_slots)):
        parts = [packs[t][which] for t in range(3)]
        outs = _adam(_pack(parts[0]), slots, _pack(parts[1]), _pack(parts[2]),
                     name="adam_small_early" if which == 0 else "adam_small_late")
        unpacked.append([_unpack(o, [p.shape for p in parts[0]]) for o in outs])
    n_ssm = len(SSM_NAMES)
    for kind in range(4):
        early, late = unpacked[0][kind], unpacked[1][kind]
        both = lambda idx: jnp.concatenate([late[idx], early[idx]], axis=0)
        vals = {"pre_norm": both(0), "post_norm": both(1), "ssm_d": both(2 + n_ssm), "ssm_glu_b": both(3 + n_ssm)}
        for j, nm in enumerate(SSM_NAMES):
            vals[nm] = jnp.stack([late[2 + j], early[2 + j]])
        for nm, val in vals.items():
            res.setdefault(nm, []).append(val)
    loss = unpacked[1][0][-1].reshape(())
    g_scale = lax.dynamic_slice_in_dim(unpacked[0][0][-1], me * scale_cols, scale_cols, axis=1)
    pad = ((0, PACK_ROWS_ALIGN - 2), (0, 0))
    outs = _adam(jnp.pad(pool_scale, pad), jnp.pad(g_scale, pad)[None], jnp.pad(m_pool_scale, pad),
                 jnp.pad(v_pool_scale, pad), name="adam_pool_scale")
    res["pool_scale"] = [o[:2] for o in outs]

    out = [loss, grad_x[None]]
    for kind in range(4):
        out += [res[nm][kind] for nm in WEIGHT_ORDER]
    return tuple(out)
```

```python
import functools
import math

import jax
import jax.numpy as jnp
from jax import lax
from jax.experimental import pallas as pl
from jax.experimental.pallas import tpu as pltpu

F32 = jnp.float32
BF16 = jnp.bfloat16
SDS = jax.ShapeDtypeStruct

N_DEV = 8
S = 2048
D = 1024
HEAD_DIM = 64
ROT_DIM = 16
ROPE_THETA = 500000.0
ATT_W = 1024
SSM_W = 512
SSM_GROUPS = 32
SSM_GROUP = 16
SSM_STATE = 64
N_CPLX = SSM_GROUPS * SSM_STATE
POOL_W = 2048
POOL_GROUP = 512
EVEN_IN = 5120
EVEN_OUT = 1536
ODD_IN = 4096
RMS_EPS = 1e-6
LANES = 128
VMEM_LIMIT = 48 * 1024 * 1024

ADAM_LR = 0.001
ADAM_B1 = 0.9
ADAM_B2 = 0.999
ADAM_EPS = 1e-08
ADAM_WD = 0.01
ADAM_STEP = 10

MESH_ID = pl.DeviceIdType.MESH
NN = (((1,), (0,)), ((), ()))
NT = (((1,), (1,)), ((), ()))
TN = (((0,), (0,)), ((), ()))
_DN = {"nn": NN, "nt": NT, "tn": TN}


def _cparams(sem):
    return pltpu.CompilerParams(dimension_semantics=sem, vmem_limit_bytes=VMEM_LIMIT)


MM_TILES = (1024, 768, 512)


def _tile(dim):
    return next((t for t in MM_TILES if dim % t == 0), dim)


NT_BLOCKS_PER_STEP = 4


def _mm(a, b, mode, out_dtype, b_blocks=False, out_blocks=False, a_cols=None, after=()):
    if b_blocks:
        nblk, rows, cb = b.shape
        b2_shape = (rows, nblk * cb)
    else:
        b2_shape = b.shape
    a_shape = a.shape if a_cols is None else (a.shape[0], a_cols[1])
    if mode == "nn":
        (m, k), n = a_shape, b2_shape[1]
    elif mode == "nt":
        (m, k), n = a_shape, b2_shape[0]
    else:
        (k, m), n = a_shape, b2_shape[1]
    tm, tn, tk = _tile(m), _tile(n), _tile(k)
    per_step = 1
    if b_blocks and mode == "nn":
        tn = cb
    if b_blocks and mode == "nt":
        per_step = NT_BLOCKS_PER_STEP
        tk = per_step * cb
        tn = min(tn, MM_TILES[-1])
    if out_blocks:
        tn = n // N_DEV
        tk = k
    nk = k // tk
    a_unit = tm if mode == "tn" else tk
    assert a_cols is None or a_cols[0] % a_unit == 0
    a_off = 0 if a_cols is None else a_cols[0] // a_unit

    def body(a_ref, b_ref, *rest):
        o_ref, acc_ref = rest[-2:]
        kk = pl.program_id(2)
        if per_step == 1:
            part = lax.dot_general(a_ref[...].astype(BF16), b_ref[...].astype(BF16), _DN[mode],
                                   preferred_element_type=F32)
        else:
            part = None
            for blk in range(per_step):
                d = lax.dot_general(a_ref[:, blk * cb:(blk + 1) * cb].astype(BF16), b_ref[blk].astype(BF16), NT,
                                    preferred_element_type=F32)
                part = d if part is None else part + d
        if nk == 1:
            o_ref[...] = part.astype(o_ref.dtype)
            return

        @pl.when(kk == 0)
        def _():
            acc_ref[...] = part

        @pl.when((kk > 0) & (kk < nk - 1))
        def _():
            acc_ref[...] += part

        @pl.when(kk == nk - 1)
        def _():
            o_ref[...] = (acc_ref[...] + part).astype(o_ref.dtype)

    if mode == "nn":
        a_spec = pl.BlockSpec((tm, tk), lambda i, j, kk: (i, a_off + kk))
        b_spec = pl.BlockSpec((tk, tn), lambda i, j, kk: (kk, j))
    elif mode == "nt":
        a_spec = pl.BlockSpec((tm, tk), lambda i, j, kk: (i, a_off + kk))
        b_spec = pl.BlockSpec((tn, tk), lambda i, j, kk: (j, kk))
    else:
        a_spec = pl.BlockSpec((tk, tm), lambda i, j, kk: (kk, a_off + i))
        b_spec = pl.BlockSpec((tk, tn), lambda i, j, kk: (kk, j))
    if b_blocks and mode == "nn":
        b_spec = pl.BlockSpec((None, tk, cb), lambda i, j, kk: (j, kk, 0))
    if b_blocks and mode == "nt":
        b_spec = pl.BlockSpec((per_step, tn, cb), lambda i, j, kk: (kk, j, 0))
    out_spec = pl.BlockSpec((tm, tn), lambda i, j, kk: (i, j))
    out_shape = SDS((m, n), out_dtype)
    if out_blocks:
        out_spec = pl.BlockSpec((None, tm, tn), lambda i, j, kk: (j, i, 0))
        out_shape = SDS((N_DEV, m, tn), out_dtype)
    return pl.pallas_call(
        body, name=f"mm_{mode}_{m}x{k}x{n}",
        grid=(m // tm, n // tn, nk),
        in_specs=[a_spec, b_spec] + [pl.BlockSpec(memory_space=pl.ANY)] * len(after),
        out_specs=out_spec,
        out_shape=out_shape,
        scratch_shapes=[pltpu.VMEM((tm, tn) if nk > 1 else (8, LANES), F32)],
        compiler_params=_cparams(("parallel", "parallel", "arbitrary")),
    )(a, b, *after)


def _gmm(a, b, mode, out_dtype, tm=512):
    ng, gw = POOL_W // POOL_GROUP, POOL_GROUP
    ns = S // tm
    if mode in ("nn", "nt"):
        def body(a_ref, b_ref, o_ref):
            o_ref[...] = lax.dot_general(a_ref[...].astype(BF16), b_ref[...].astype(BF16), _DN[mode],
                                         preferred_element_type=F32).astype(o_ref.dtype)

        return pl.pallas_call(
            body, name=f"gmm_{mode}", grid=(ng, ns),
            in_specs=[pl.BlockSpec((tm, gw), lambda g, i: (i, g)),
                      pl.BlockSpec((None, gw, gw), lambda g, i: (g, 0, 0))],
            out_specs=pl.BlockSpec((tm, gw), lambda g, i: (i, g)),
            out_shape=SDS((S, POOL_W), out_dtype),
            compiler_params=_cparams(("parallel", "parallel")),
        )(a, b)

    def body_tn(a_ref, b_ref, o_ref, acc_ref):
        i = pl.program_id(1)

        @pl.when(i == 0)
        def _():
            acc_ref[...] = jnp.zeros_like(acc_ref)

        acc_ref[...] += lax.dot_general(a_ref[...].astype(BF16), b_ref[...].astype(BF16), TN,
                                        preferred_element_type=F32)

        @pl.when(i == ns - 1)
        def _():
            o_ref[...] = acc_ref[...].astype(o_ref.dtype)

    return pl.pallas_call(
        body_tn, name="gmm_tn", grid=(ng, ns),
        in_specs=[pl.BlockSpec((tm, gw), lambda g, i: (i, g)),
                  pl.BlockSpec((tm, gw), lambda g, i: (i, g))],
        out_specs=pl.BlockSpec((None, gw, gw), lambda g, i: (g, 0, 0)),
        out_shape=SDS((ng, gw, gw), out_dtype),
        scratch_shapes=[pltpu.VMEM((gw, gw), F32)],
        compiler_params=_cparams(("parallel", "arbitrary")),
    )(a, b)


def _rowwise(fn, inputs, out_defs, acc_defs=(), tm=256, name=None):
    n_in, n_out, n_acc = len(inputs), len(out_defs), len(acc_defs)
    in_specs, args = [], []
    for arr, width, cb in inputs:
        if arr.shape[0] == 1:
            in_specs.append(pl.BlockSpec((1, width), lambda i, cb=cb: (0, cb)))
        else:
            in_specs.append(pl.BlockSpec((tm, width), lambda i, cb=cb: (i, cb)))
        args.append(arr)
    out_defs = [d if len(d) == 4 else (d[0], d[1], d[0], 0) for d in out_defs]
    out_shape = [SDS((S, ww), dt) for _, dt, ww, _ in out_defs] + [SDS((1, w), F32) for w in acc_defs]
    out_specs = ([pl.BlockSpec((tm, w), lambda i, cb=cb: (i, cb)) for w, _, _, cb in out_defs]
                 + [pl.BlockSpec((1, w), lambda i: (0, 0)) for w in acc_defs])

    def kern(*refs):
        vals = [r[...] for r in refs[:n_in]]
        outs, accs = fn(*vals)
        for r, v in zip(refs[n_in:n_in + n_out], outs):
            r[...] = v.astype(r.dtype)
        if n_acc:
            acc_refs = refs[n_in + n_out:]

            @pl.when(pl.program_id(0) == 0)
            def _():
                for r in acc_refs:
                    r[...] = jnp.zeros_like(r)

            for r, v in zip(acc_refs, accs):
                r[...] += jnp.sum(v, axis=0, keepdims=True)

    res = pl.pallas_call(
        kern, name=name, grid=(S // tm,), in_specs=in_specs, out_specs=out_specs, out_shape=out_shape,
        compiler_params=_cparams(("arbitrary",)),
    )(*args)
    return res


def _sigmoid(x):
    return 1.0 / (1.0 + jnp.exp(-x))


def _silu_and_grad(x):
    s = _sigmoid(x)
    return x * s, s * (1.0 + x * (1.0 - s))


_GELU_K = math.sqrt(2.0 / math.pi)
_GELU_C = 0.044715


def _gelu_and_grad(x):
    t = jnp.tanh(_GELU_K * (x + _GELU_C * (x * x * x)))
    cdf = 0.5 * (1.0 + t)
    grad = cdf + 0.5 * x * (1.0 - t * t) * (_GELU_K * (1.0 + 3.0 * _GELU_C * x * x))
    return x * cdf, grad


def _rms(xv, gain):
    r = lax.rsqrt(jnp.mean(xv * xv, axis=-1, keepdims=True) + RMS_EPS)
    return xv * r * gain


def _rms_bwd(dout, xv, gain):
    r = lax.rsqrt(jnp.mean(xv * xv, axis=-1, keepdims=True) + RMS_EPS)
    xhat = xv * r
    dxhat = dout * gain
    dx = r * (dxhat - xhat * jnp.mean(dxhat * xhat, axis=-1, keepdims=True))
    return dx, dout * xhat


def _norm_fwd(x, gain):
    (h,) = _rowwise(lambda xv, g: ((_rms(xv, g),), ()), [(x, D, 0), (gain, D, 0)], [(D, BF16)], name="norm_fwd")
    return h


def _post_fwd(x, y, gain):
    (o,) = _rowwise(lambda xv, yv, g: ((xv + _rms(yv, g),), ()), [(x, D, 0), (y, D, 0), (gain, D, 0)],
                    [(D, F32)], name="post_fwd")
    return o


def _post_bwd(g, y, gain):
    def fn(gv, yv, gn):
        dx, dg = _rms_bwd(gv, yv, gn)
        return (dx,), (dg,)

    return _rowwise(fn, [(g, D, 0), (y, D, 0), (gain, D, 0)], [(D, BF16)], [D], name="post_bwd")


def _pre_bwd(g, dh, x, gain):
    def fn(gv, dhv, xv, gn):
        dx, dg = _rms_bwd(dhv, xv, gn)
        return (gv + dx,), (dg,)

    return _rowwise(fn, [(g, D, 0), (dh, D, 0), (x, D, 0), (gain, D, 0)], [(D, F32)], [D], name="pre_bwd")


def _loss_grad(xo, tgt):
    def fn(xv, tv):
        e = xv - tv
        return (e * (1.0 / D),), (e * e,)

    return _rowwise(fn, [(xo, D, 0), (tgt, D, 0)], [(D, F32)], [D], name="loss_grad")


def _pool(u_arr, col_block, transpose, out_dtype, into=None, tc=256):
    n_t = POOL_W // tc
    per_group = POOL_GROUP // tc

    def body(u_ref, *rest):
        o_ref = rest[-1]
        c = pl.program_id(0)
        grp = c // per_group
        xv = u_ref[...]
        t = lax.broadcasted_iota(jnp.int32, (S, 1), 0)
        win = jnp.left_shift(2, grp)
        cnt = jnp.minimum(t + 1, win).astype(F32)
        cur = xv / cnt if transpose else xv
        sums = []
        for k in (1, 2, 4, 8):
            if transpose:
                sh = jnp.where(t < S - k, pltpu.roll(cur, S - k, 0), 0.0)
            else:
                sh = jnp.where(t >= k, pltpu.roll(cur, k, 0), 0.0)
            cur = cur + sh
            sums.append(cur)
        tot = jnp.where(grp == 0, sums[0], jnp.where(grp == 1, sums[1], jnp.where(grp == 2, sums[2], sums[3])))
        res = tot - xv if transpose else tot / cnt - xv
        o_ref[...] = res.astype(o_ref.dtype)

    in_specs = [pl.BlockSpec((S, tc), lambda c: (0, col_block * n_t + c))]
    args = [u_arr]
    if into is not None:
        in_specs.append(pl.BlockSpec(memory_space=pl.ANY))
        args.append(into)
    return pl.pallas_call(
        body, name="pool_bwd" if transpose else "pool_fwd", grid=(n_t,),
        in_specs=in_specs,
        out_specs=pl.BlockSpec((S, tc), lambda c: (0, c)),
        out_shape=SDS((S, POOL_W) if into is None else into.shape, out_dtype),
        input_output_aliases={} if into is None else {1: 0},
        compiler_params=_cparams(("parallel",)),
    )(*args)


def _rope_tables(zero):
    pos = jnp.arange(S, dtype=jnp.int32).astype(F32) + zero
    inv_freq = ROPE_THETA ** (-jnp.arange(0, ROT_DIM, 2, dtype=F32) / ROT_DIM)
    ang = pos[:, None] * inv_freq[None, :]
    cos8, sin8 = jnp.cos(ang), jnp.sin(ang)
    half = ROT_DIM // 2
    zeros = jnp.zeros((S, HEAD_DIM - ROT_DIM), F32)
    cos = jnp.concatenate([cos8, cos8, jnp.ones((S, HEAD_DIM - ROT_DIM), F32)], axis=1)
    lo = jnp.concatenate([-sin8, jnp.zeros((S, half), F32), zeros], axis=1)
    hi = jnp.concatenate([jnp.zeros((S, half), F32), sin8, zeros], axis=1)
    rep = LANES // HEAD_DIM
    return jnp.tile(cos, (1, rep)), jnp.tile(lo, (1, rep)), jnp.tile(hi, (1, rep))


def _rotate(xv, cos, lo, hi, transpose):
    width = xv.shape[1]
    rep = width // LANES
    wide = lambda tab: jnp.concatenate([tab] * rep, axis=1)
    half = ROT_DIM // 2
    up = pltpu.roll(xv, width - half, 1)
    dn = pltpu.roll(xv, half, 1)
    mixed = up * wide(lo) + dn * wide(hi)
    return xv * wide(cos) - mixed if transpose else xv * wide(cos) + mixed


def _qkv_prep(proj, tables):
    cos, lo, hi = tables

    def fn(x, c, l, h):
        rot = _rotate(x[:, :2 * ATT_W], c, l, h, False)
        return (jnp.concatenate([(rot[:, :ATT_W] * HEAD_DIM ** -0.5).astype(BF16), rot[:, ATT_W:].astype(BF16),
                                 x[:, 2 * ATT_W:].astype(BF16)], axis=1),), ()

    (qkv,) = _rowwise(fn, [(proj, 3 * ATT_W, 0), (cos, LANES, 0), (lo, LANES, 0), (hi, LANES, 0)],
                      [(3 * ATT_W, BF16)], name="qkv_prep")
    return qkv


ATT_T = 512


def _multiplicity(delta):
    ok = delta >= 0
    near = jnp.where(ok & (delta <= 128), 1.0, 0.0)
    mid = jnp.where(ok & (delta <= 512) & ((delta & 3) == 0), 1.0, 0.0)
    far = jnp.where(ok & ((delta & 15) == 0), 1.0, 0.0)
    return near + mid + far


def _attention_bias(zero):
    t = ATT_T
    pos = jnp.arange(t, dtype=jnp.int32) + jnp.asarray(zero).astype(jnp.int32)
    delta = jnp.arange(S // t, dtype=jnp.int32)[:, None, None] * t + pos[None, :, None] - pos[None, None, :]
    mult = _multiplicity(delta)
    return jnp.where(mult > 0.0, jnp.log(jnp.maximum(mult, 1.0)), -1e30).astype(F32)


def _head_split(v, first):
    zero = jnp.zeros_like(v)
    return [jnp.where(first, v, zero), jnp.where(first, zero, v)]


def _flash_fwd(qkv, bias):
    t = ATT_T
    n_hp = ATT_W // LANES

    def body(q_ref, k_ref, v_ref, b_ref, o_ref, lse_ref):
        i = pl.program_id(1)
        first = lax.broadcasted_iota(jnp.int32, (1, LANES), 1) < HEAD_DIM
        qs = _head_split(q_ref[...], first)

        def kv_step(j, carry):
            m0, l0, m1, l1, acc = carry
            off = pl.multiple_of(j * t, t)
            kb = k_ref[pl.ds(off, t), :]
            vs = _head_split(v_ref[pl.ds(off, t), :], first)
            bias_t = b_ref[i - j]
            new = []
            pv = None
            for h, (m_prev, l_prev) in enumerate(((m0, l0), (m1, l1))):
                s = lax.dot_general(qs[h], kb, NT, preferred_element_type=F32) + bias_t
                m_new = jnp.maximum(m_prev, jnp.max(s, axis=1, keepdims=True))
                p = jnp.exp(s - m_new)
                alpha = jnp.exp(m_prev - m_new)
                l_new = alpha * l_prev + jnp.sum(p, axis=1, keepdims=True)
                d = lax.dot_general(p.astype(BF16), vs[h], NN, preferred_element_type=F32)
                pv = d if pv is None else pv + d
                new.append((m_new, l_new, alpha))
            acc = acc * jnp.where(first, new[0][2], new[1][2]) + pv
            return new[0][0], new[0][1], new[1][0], new[1][1], acc

        neg = jnp.full((t, 1), -1e30, F32)
        zero = jnp.zeros((t, 1), F32)
        m0, l0, m1, l1, acc = lax.fori_loop(0, i + 1, kv_step, (neg, zero, neg, zero, jnp.zeros((t, LANES), F32)))
        o_ref[...] = acc * jnp.where(first, 1.0 / l0, 1.0 / l1)
        lse_ref[...] = jnp.where(first, m0 + jnp.log(l0), m1 + jnp.log(l1))

    blk = pl.BlockSpec((t, LANES), lambda hp, i: (i, hp))
    k_full = pl.BlockSpec((S, LANES), lambda hp, i: (0, n_hp + hp))
    v_full = pl.BlockSpec((S, LANES), lambda hp, i: (0, 2 * n_hp + hp))
    return pl.pallas_call(
        body, name="flash_fwd", grid=(n_hp, S // t),
        in_specs=[blk, k_full, v_full, pl.BlockSpec((S // t, t, t), lambda hp, i: (0, 0, 0))], out_specs=[blk, blk],
        out_shape=[SDS((S, ATT_W), F32), SDS((S, ATT_W), F32)],
        compiler_params=_cparams(("parallel", "arbitrary")),
    )(qkv, qkv, qkv, bias)


def _flash_bwd(qkv, o, do, lse, bias):
    t = ATT_T
    n_hp = ATT_W // LANES
    n_t = S // t

    def body(q_ref, k_ref, v_ref, o_ref, do_ref, lse_ref, b_ref, dq_ref, dk_ref, dv_ref):
        j = pl.program_id(1)
        first = lax.broadcasted_iota(jnp.int32, (1, LANES), 1) < HEAD_DIM

        @pl.when(j == 0)
        def _():
            dq_ref[...] = jnp.zeros_like(dq_ref)

        kb = k_ref[...]
        vb = v_ref[...]
        ks = _head_split(kb, first)

        def q_step(i, carry):
            dk_acc, dv_acc = carry
            rows = pl.ds(pl.multiple_of(i * t, t), t)
            qs = _head_split(q_ref[rows, :], first)
            dob = do_ref[rows, :]
            prod = dob * o_ref[rows, :]
            d_all = jnp.sum(prod, axis=1, keepdims=True)
            d0 = jnp.sum(jnp.where(first, prod, 0.0), axis=1, keepdims=True)
            lse_b = lse_ref[rows, :]
            lse0 = jnp.max(jnp.where(first, lse_b, -jnp.inf), axis=1, keepdims=True)
            lse1 = jnp.max(jnp.where(first, -jnp.inf, lse_b), axis=1, keepdims=True)
            dos = _head_split(dob.astype(BF16), first)
            bias_t = b_ref[i - j]
            dq_t = jnp.zeros((t, LANES), F32)
            for h, (lse_h, d_h) in enumerate(((lse0, d0), (lse1, d_all - d0))):
                s = lax.dot_general(qs[h], kb, NT, preferred_element_type=F32)
                p = jnp.exp(s + (bias_t - lse_h))
                dp = lax.dot_general(dos[h], vb, NT, preferred_element_type=F32)
                ds = (p * (dp - d_h)).astype(BF16)
                dv_acc = dv_acc + lax.dot_general(p.astype(BF16), dos[h], TN, preferred_element_type=F32)
                dk_acc = dk_acc + lax.dot_general(ds, qs[h], TN, preferred_element_type=F32)
                dq_t = dq_t + lax.dot_general(ds, ks[h], NN, preferred_element_type=F32)
            dq_ref[rows, :] += dq_t
            return dk_acc, dv_acc

        zero = jnp.zeros((t, LANES), F32)
        dk_acc, dv_acc = lax.fori_loop(j, n_t, q_step, (zero, zero))
        dk_ref[...] = dk_acc
        dv_ref[...] = dv_acc

    blk = pl.BlockSpec((t, LANES), lambda hp, j: (j, hp))
    full = pl.BlockSpec((S, LANES), lambda hp, j: (0, hp))
    k_blk = pl.BlockSpec((t, LANES), lambda hp, j: (j, n_hp + hp))
    v_blk = pl.BlockSpec((t, LANES), lambda hp, j: (j, 2 * n_hp + hp))
    return pl.pallas_call(
        body, name="flash_bwd", grid=(n_hp, n_t),
        in_specs=[full, k_blk, v_blk, full, full, full, pl.BlockSpec((n_t, t, t), lambda hp, j: (0, 0, 0))],
        out_specs=[full, blk, blk],
        out_shape=[SDS((S, ATT_W), F32)] * 3,
        compiler_params=_cparams(("parallel", "arbitrary")),
    )(qkv, qkv, qkv, o, do, lse, bias)


SCAN_T = 256
ST_ROWS = 2 * N_CPLX // LANES
HALF = ST_ROWS // 2


def _scan_fwd(lam, bu):
    def body(lam_ref, bu_ref, st_ref, carry):
        @pl.when(pl.program_id(0) == 0)
        def _():
            carry[...] = jnp.zeros_like(carry)

        ar, ai = lam_ref[0:HALF, :], lam_ref[HALF:ST_ROWS, :]

        def step(t, c):
            sr, si = c
            b = bu_ref[t]
            nr = ar * sr - ai * si + b[0:HALF]
            ni = ar * si + ai * sr + b[HALF:ST_ROWS]
            st_ref[t, 0:HALF, :] = nr
            st_ref[t, HALF:ST_ROWS, :] = ni
            return nr, ni

        sr, si = lax.fori_loop(0, SCAN_T, step, (carry[0:HALF, :], carry[HALF:ST_ROWS, :]), unroll=8)
        carry[0:HALF, :] = sr
        carry[HALF:ST_ROWS, :] = si

    blk = pl.BlockSpec((SCAN_T, ST_ROWS, LANES), lambda i: (i, 0, 0))
    return pl.pallas_call(
        body, name="scan_fwd", grid=(S // SCAN_T,),
        in_specs=[pl.BlockSpec((ST_ROWS, LANES), lambda i: (0, 0)), blk], out_specs=blk,
        out_shape=SDS((S, ST_ROWS, LANES), F32),
        scratch_shapes=[pltpu.VMEM((ST_ROWS, LANES), F32)],
        compiler_params=_cparams(("arbitrary",)),
    )(lam, bu)


def _scan_bwd(lam, dst, states):
    n_blk = S // SCAN_T

    def body(lam_ref, d_ref, st_ref, g_ref, dlam_ref, carry):
        @pl.when(pl.program_id(0) == 0)
        def _():
            carry[...] = jnp.zeros_like(carry)
            dlam_ref[...] = jnp.zeros_like(dlam_ref)

        ar, ai = lam_ref[0:HALF, :], lam_ref[HALF:ST_ROWS, :]

        def step(kk, c):
            t = SCAN_T - 1 - kk
            gr, gi, dar, dai = c
            x = st_ref[t]
            xr, xi = x[0:HALF], x[HALF:ST_ROWS]
            dar = dar + gr * xr + gi * xi
            dai = dai + gi * xr - gr * xi
            d = d_ref[t]
            ngr = d[0:HALF] + ar * gr + ai * gi
            ngi = d[HALF:ST_ROWS] + ar * gi - ai * gr
            g_ref[t, 0:HALF, :] = ngr
            g_ref[t, HALF:ST_ROWS, :] = ngi
            return ngr, ngi, dar, dai

        zero = jnp.zeros((HALF, LANES), F32)
        gr, gi, dar, dai = lax.fori_loop(0, SCAN_T, step, (carry[0:HALF, :], carry[HALF:ST_ROWS, :], zero, zero),
                                         unroll=8)
        carry[0:HALF, :] = gr
        carry[HALF:ST_ROWS, :] = gi
        dlam_ref[0:HALF, :] += dar
        dlam_ref[HALF:ST_ROWS, :] += dai

    blk = pl.BlockSpec((SCAN_T, ST_ROWS, LANES), lambda i: (n_blk - 1 - i, 0, 0))
    small = pl.BlockSpec((ST_ROWS, LANES), lambda i: (0, 0))
    return pl.pallas_call(
        body, name="scan_bwd", grid=(n_blk,),
        in_specs=[small, blk, blk], out_specs=[blk, small],
        out_shape=[SDS((S, ST_ROWS, LANES), F32), SDS((ST_ROWS, LANES), F32)],
        scratch_shapes=[pltpu.VMEM((ST_ROWS, LANES), F32)],
        compiler_params=_cparams(("arbitrary",)),
    )(lam, dst, states)


def _ssm_prep(a_re, a_im, log_dt, b_re, b_im, c_re, c_im):
    lam = lax.complex(a_re, a_im)
    dt = jnp.exp(log_dt)[:, None]
    lam_bar = jnp.exp(lam * dt)
    b_bar = ((lam_bar - 1.0) / lam)[..., None] * lax.complex(b_re, b_im)
    lam_t = jnp.concatenate([jnp.real(lam_bar).reshape(HALF, LANES), jnp.imag(lam_bar).reshape(HALF, LANES)], axis=0)
    groups_per_super = SSM_GROUPS // SSM_SUPER
    on_diag = ((lax.broadcasted_iota(jnp.int32, (SSM_W, SB_COLS), 0) // SSM_GROUP) % groups_per_super
               == lax.broadcasted_iota(jnp.int32, (SSM_W, SB_COLS), 1) // SSM_STATE)

    def compact(m):
        return jnp.where(on_diag, jnp.tile(m.reshape(SSM_W, SSM_STATE), (1, groups_per_super)), 0.0)

    w_b = jnp.concatenate([compact(jnp.real(b_bar).transpose(0, 2, 1)),
                           compact(jnp.imag(b_bar).transpose(0, 2, 1))], axis=1)
    w_ct = jnp.concatenate([compact(c_re), -compact(c_im)], axis=1)
    return lam_t, w_b, w_ct


SSM_SUPER = 4
SB_ROWS = SSM_W // SSM_SUPER
SB_COLS = N_CPLX // SSM_SUPER


def _bdmm(a, w, mode, out_dtype, a_cols=None):
    a_off = 0 if a_cols is None else a_cols[0] // SB_ROWS
    if mode == "nn":
        def body(a_ref, w_ref, o_ref):
            o_ref[...] = lax.dot_general(a_ref[...].astype(BF16), w_ref[...].astype(BF16), NN,
                                         preferred_element_type=F32).astype(o_ref.dtype)

        return pl.pallas_call(
            body, name="bdmm_nn", grid=(2, SSM_SUPER),
            in_specs=[pl.BlockSpec((S, SB_ROWS), lambda h, b: (0, a_off + b)),
                      pl.BlockSpec((SB_ROWS, SB_COLS), lambda h, b: (b, h))],
            out_specs=pl.BlockSpec((S, SB_COLS), lambda h, b: (0, h * SSM_SUPER + b)),
            out_shape=SDS((S, 2 * N_CPLX), out_dtype),
            compiler_params=_cparams(("parallel", "parallel")),
        )(a, w)
    if mode == "nt":
        def body(re_ref, im_ref, wre_ref, wim_ref, o_ref):
            acc = lax.dot_general(re_ref[...].astype(BF16), wre_ref[...].astype(BF16), NT, preferred_element_type=F32)
            acc += lax.dot_general(im_ref[...].astype(BF16), wim_ref[...].astype(BF16), NT, preferred_element_type=F32)
            o_ref[...] = acc.astype(o_ref.dtype)

        return pl.pallas_call(
            body, name="bdmm_nt", grid=(SSM_SUPER,),
            in_specs=[pl.BlockSpec((S, SB_COLS), lambda b: (0, b)),
                      pl.BlockSpec((S, SB_COLS), lambda b: (0, SSM_SUPER + b)),
                      pl.BlockSpec((SB_ROWS, SB_COLS), lambda b: (b, 0)),
                      pl.BlockSpec((SB_ROWS, SB_COLS), lambda b: (b, 1))],
            out_specs=pl.BlockSpec((S, SB_ROWS), lambda b: (0, b)),
            out_shape=SDS((S, SSM_W), out_dtype),
            compiler_params=_cparams(("parallel",)),
        )(a, a, w, w)

    def body_tn(a_ref, w_ref, o_ref):
        o_ref[...] = lax.dot_general(a_ref[...].astype(BF16), w_ref[...].astype(BF16), TN,
                                     preferred_element_type=F32).astype(o_ref.dtype)

    return pl.pallas_call(
        body_tn, name="bdmm_tn", grid=(2, SSM_SUPER),
        in_specs=[pl.BlockSpec((S, SB_ROWS), lambda h, b: (0, a_off + b)),
                  pl.BlockSpec((S, SB_COLS), lambda h, b: (0, h * SSM_SUPER + b))],
        out_specs=pl.BlockSpec((SB_ROWS, SB_COLS), lambda h, b: (b, h)),
        out_shape=SDS((SSM_W, 2 * SB_COLS), out_dtype),
        compiler_params=_cparams(("parallel", "parallel")),
    )(a, w)


U_SSM_COLS = (4 * ATT_W, SSM_W)


def _row(v):
    return v.reshape(1, -1)


def _even_fwd(x, pre, post, w_in, late_w, glu_b, ssm_d, prep, tables):
    lam_t, w_b, w_ct = prep
    h = _norm_fwd(x, pre)
    proj = _mm(h, w_in, "nn", F32, b_blocks=True)
    qkv = _qkv_prep(proj, tables[:3])
    w_out, glu_w = late_w(qkv)
    att, lse = _flash_fwd(qkv, tables[3])
    bu = _bdmm(proj, w_b, "nn", F32, a_cols=U_SSM_COLS)
    states = _scan_fwd(lam_t, bu.reshape(S, ST_ROWS, LANES))
    y = _bdmm(states.reshape(S, 2 * N_CPLX), w_ct, "nt", F32)

    def act1(yv, uv, dv):
        return (_gelu_and_grad(yv + dv * uv)[0],), ()

    (z1,) = _rowwise(act1, [(y, SSM_W, 0), (proj, SSM_W, 8), (ssm_d, SSM_W, 0)], [(SSM_W, F32)], name="ssm_act_fwd")
    lin = _mm(z1, glu_w, "nn", F32)

    def gate(att_v, ga, gs, z1v, linv, bv):
        ssm_out = z1v * _sigmoid(linv + bv)
        return (jnp.concatenate([att_v * _silu_and_grad(ga)[0], ssm_out * _silu_and_grad(gs)[0]], axis=1),), ()

    (merged,) = _rowwise(gate, [(att, ATT_W, 0), (proj, ATT_W, 3), (proj, SSM_W, 9), (z1, SSM_W, 0),
                                (lin, SSM_W, 0), (glu_b, SSM_W, 0)], [(EVEN_OUT, BF16)], name="even_gate_fwd")
    yout = _mm(merged, w_out, "nn", F32)
    x_next = _post_fwd(x, yout, post)
    saved = (x, h, proj, qkv, att, lse, states, y, z1, lin, merged, yout, w_out, glu_w)
    return x_next, saved


def _even_bwd(g, saved, pre, post, w_in, late_w, glu_b, ssm_d, prep, tables, on_w):
    x, h, proj, qkv, att, lse, states, y, z1, lin, merged, yout, w_out, glu_w = saved
    lam_t, w_b, w_ct = prep
    dyout, dpost = _post_bwd(g, yout, post)
    dmerged = _mm(dyout, w_out, "nt", F32)
    dw_out = _mm(merged, dyout, "tn", BF16)

    def gate_bwd(dm_a, dm_s, att_v, ga, gs, z1v, linv, bv):
        sa, dsa = _silu_and_grad(ga)
        ss, dss = _silu_and_grad(gs)
        sig = _sigmoid(linv + bv)
        ssm_out = z1v * sig
        dssm = dm_s * ss
        dlin = dssm * z1v * sig * (1.0 - sig)
        return (dm_a * sa, dm_a * att_v * dsa, dm_s * ssm_out * dss, dssm * sig, dlin), (dlin,)

    datt, dg_att, dg_ssm, dz1a, dlin, dglu_b = _rowwise(
        gate_bwd, [(dmerged, ATT_W, 0), (dmerged, SSM_W, 2), (att, ATT_W, 0), (proj, ATT_W, 3), (proj, SSM_W, 9),
                   (z1, SSM_W, 0), (lin, SSM_W, 0), (glu_b, SSM_W, 0)],
        [(ATT_W, F32), (ATT_W, BF16), (SSM_W, BF16), (SSM_W, F32), (SSM_W, BF16)], [SSM_W], name="even_gate_bwd")
    dz1b = _mm(dlin, glu_w, "nt", F32)
    dglu_w = _mm(z1, dlin, "tn", BF16)

    def act1_bwd(da, db, yv, uv, dv):
        dpre = (da + db) * _gelu_and_grad(yv + dv * uv)[1]
        return (dpre, dpre * dv), (dpre * uv,)

    dy, du_direct, dd = _rowwise(act1_bwd, [(dz1a, SSM_W, 0), (dz1b, SSM_W, 0), (y, SSM_W, 0), (proj, SSM_W, 8),
                                            (ssm_d, SSM_W, 0)], [(SSM_W, BF16), (SSM_W, F32)], [SSM_W],
                                 name="ssm_act_bwd")
    dst = _bdmm(dy, w_ct, "nn", F32)
    dw_ct = _bdmm(dy, states.reshape(S, 2 * N_CPLX), "tn", F32)
    gst, dlam = _scan_bwd(lam_t, dst.reshape(S, ST_ROWS, LANES), states)
    dbu = gst.reshape(S, 2 * N_CPLX)
    du_state = _bdmm(dbu, w_b, "nt", F32)
    dw_b = _bdmm(proj, dbu, "tn", F32, a_cols=U_SSM_COLS)
    dq, dk, dv = _flash_bwd(qkv, att, datt, lse, tables[3])

    def assemble(dqv, dkv, dvv, dga, dua, dub, dgs, c, l, h):
        rot = _rotate(jnp.concatenate([dqv, dkv], axis=1), c, l, h, True)
        return (jnp.concatenate([(rot[:, :ATT_W] * HEAD_DIM ** -0.5).astype(BF16), rot[:, ATT_W:].astype(BF16),
                                 dvv.astype(BF16), dga, (dua + dub).astype(BF16), dgs], axis=1),), ()

    (dproj,) = _rowwise(assemble, [(dq, ATT_W, 0), (dk, ATT_W, 0), (dv, ATT_W, 0), (dg_att, ATT_W, 0),
                                   (du_state, SSM_W, 0), (du_direct, SSM_W, 0), (dg_ssm, SSM_W, 0),
                                   (tables[0], LANES, 0), (tables[1], LANES, 0), (tables[2], LANES, 0)],
                        [(EVEN_IN, BF16)], name="dproj_assemble")
    dw_in = _mm(h, dproj, "tn", BF16, out_blocks=True)
    sent = on_w(dict(w_in=dw_in, w_out=dw_out, glu_w=dglu_w))
    dh = _mm(dproj, w_in, "nt", F32, b_blocks=True, after=(sent,))
    g_prev, dpre = _pre_bwd(g, dh, x, pre)
    return g_prev, dict(pre=dpre, post=dpost, glu_b=dglu_b, ssm_d=dd, prep=(dlam, dw_b, dw_ct))


def _odd_fwd(x, pre, post, w_in, pool_w, pool_scale, w_out):
    h = _norm_fwd(x, pre)
    proj = _mm(h, w_in, "nn", F32, b_blocks=True)
    mixed = _pool(proj, 0, False, BF16)
    ylin = _gmm(mixed, pool_w, "nn", F32)

    def gate(yl, gt, sc):
        return (yl * sc * _silu_and_grad(gt)[0],), ()

    (z,) = _rowwise(gate, [(ylin, POOL_W, 0), (proj, POOL_W, 1), (pool_scale, POOL_W, 0)], [(POOL_W, BF16)],
                    name="odd_gate_fwd")
    yout = _mm(z, w_out, "nn", F32)
    x_next = _post_fwd(x, yout, post)
    return x_next, (x, h, proj, mixed, ylin, z, yout)


def _odd_bwd(g, saved, pre, post, w_in, pool_w, pool_scale, w_out, on_w):
    x, h, proj, mixed, ylin, z, yout = saved
    dyout, dpost = _post_bwd(g, yout, post)
    dz = _mm(dyout, w_out, "nt", F32)
    dw_out = _mm(z, dyout, "tn", BF16)

    def gate_bwd(dzv, yl, gt, sc):
        sg, dsg = _silu_and_grad(gt)
        tt = dzv * sg
        return (tt * sc, dzv * yl * sc * dsg), (tt * yl,)

    dylin, dproj_gate, dscale = _rowwise(gate_bwd, [(dz, POOL_W, 0), (ylin, POOL_W, 0), (proj, POOL_W, 1),
                                                    (pool_scale, POOL_W, 0)],
                                         [(POOL_W, BF16), (POOL_W, BF16, ODD_IN, 1)], [POOL_W], name="odd_gate_bwd")
    dmixed = _gmm(dylin, pool_w, "nt", F32)
    dpool_w = _gmm(mixed, dylin, "tn", BF16)
    dproj = _pool(dmixed, 0, True, BF16, into=dproj_gate)
    dw_in = _mm(h, dproj, "tn", BF16, out_blocks=True)
    sent = on_w(dict(w_in=dw_in, w_out=dw_out, pool_w=dpool_w))
    dh = _mm(dproj, w_in, "nt", F32, b_blocks=True, after=(sent,))
    g_prev, dpre = _pre_bwd(g, dh, x, pre)
    return g_prev, dict(pre=dpre, post=dpost, pool_scale=dscale)


def _my_index():
    return 4 * lax.axis_index("x") + 2 * lax.axis_index("y") + lax.axis_index("c")


def _exchange(arrs, gather, name, after=()):
    n = len(arrs)
    out_shape = [SDS((N_DEV,) + a.shape, a.dtype) if gather else SDS(a.shape, a.dtype) for a in arrs]

    def body(*refs):
        ins, outs = refs[:n], refs[n + len(after):2 * n + len(after)]
        send_sems, recv_sems, local_sems = refs[2 * n + len(after):]
        me = _my_index()

        def src(i, j):
            return ins[i] if gather else ins[i].at[j]

        def remote(i, j, src_slot, dst_slot, recv_slot):
            return pltpu.make_async_remote_copy(
                src_ref=src(i, src_slot), dst_ref=outs[i].at[dst_slot], send_sem=send_sems.at[i, j],
                recv_sem=recv_sems.at[i, recv_slot], device_id=(j // 4, (j // 2) % 2, j % 2), device_id_type=MESH_ID)

        def local(i):
            return pltpu.make_async_copy(src(i, me), outs[i].at[me], local_sems.at[i])

        for i in range(n):
            local(i).start()
        for j in range(N_DEV):
            @pl.when(me != j)
            def _(j=j):
                for i in range(n):
                    remote(i, j, j, me, me).start()
        for j in range(N_DEV):
            @pl.when(me != j)
            def _(j=j):
                for i in range(n):
                    remote(i, j, j, me, me).wait_send()
                    remote(i, j, j, j, j).wait_recv()
        for i in range(n):
            local(i).wait()

    any_spec = pl.BlockSpec(memory_space=pl.ANY)
    return pl.pallas_call(
        body, name=name, in_specs=[any_spec] * (n + len(after)), out_specs=[any_spec] * n, out_shape=out_shape,
        scratch_shapes=[pltpu.SemaphoreType.DMA((n, N_DEV)), pltpu.SemaphoreType.DMA((n, N_DEV)),
                        pltpu.SemaphoreType.DMA((n,))],
    )(*arrs, *after)


HBM_SPEC = pl.BlockSpec(memory_space=pltpu.HBM)
SEM_SPEC = pl.BlockSpec(memory_space=pltpu.SEMAPHORE)
SPLIT_EFFECT = pltpu.SideEffectType.DATAFLOW_SIDE_EFFECTING


def _device_of(j):
    return (j // 4, (j // 2) % 2, j % 2)


def _split_copy(srcs, lands, send_sems, recv_sems, gather, i, j, dst_slot, recv_slot):
    return pltpu.make_async_remote_copy(
        src_ref=srcs[i] if gather else srcs[i].at[j], dst_ref=lands[i].at[dst_slot],
        send_sem=send_sems.at[i * N_DEV + j], recv_sem=recv_sems.at[i * N_DEV + recv_slot],
        device_id=_device_of(j), device_id_type=MESH_ID)


def _own_copy(srcs, lands, send_sems, gather, i, me):
    return pltpu.make_async_copy(srcs[i] if gather else srcs[i].at[me], lands[i].at[me], send_sems.at[i * N_DEV + me])


def _xchg_start(name, srcs, gather, after=()):
    n = len(srcs)
    n_in = n + len(after)

    def body(*refs):
        src_refs = refs[:n]
        send_sems, recv_sems, token = refs[n_in], refs[n_in + 1], refs[-1]
        land_refs = refs[n_in + 2 + n:n_in + 2 + 2 * n]
        me = _my_index()
        for j in range(N_DEV):
            @pl.when(me != j)
            def _(j=j):
                for i in range(n):
                    _split_copy(src_refs, land_refs, send_sems, recv_sems, gather, i, j, me, me).start()
        for i in range(n):
            _own_copy(src_refs, land_refs, send_sems, gather, i, me).start()
        token[...] = jnp.zeros_like(token)

    land_shapes = [((N_DEV,) + a.shape) if gather else a.shape for a in srcs]
    thru = ([pltpu.HBM(a.shape, a.dtype) for a in srcs] + [pltpu.HBM(s, a.dtype) for s, a in zip(land_shapes, srcs)])
    res = pl.pallas_call(
        body, name=name,
        out_shape=(pltpu.SemaphoreType.DMA((n * N_DEV,)), pltpu.SemaphoreType.DMA((n * N_DEV,)), *thru,
                   SDS((8, LANES), F32)),
        in_specs=[HBM_SPEC] * n + [pl.BlockSpec(memory_space=pl.ANY)] * len(after),
        out_specs=(SEM_SPEC, SEM_SPEC, *([HBM_SPEC] * (2 * n)), pl.BlockSpec(memory_space=pltpu.VMEM)),
        input_output_aliases={i: 2 + i for i in range(n)},
        compiler_params=pltpu.CompilerParams(has_side_effects=SPLIT_EFFECT),
    )(*[pltpu.with_memory_space_constraint(a, pltpu.HBM) for a in srcs], *after)
    return res[0], res[1], list(res[2:2 + n]), list(res[2 + n:2 + 2 * n]), res[-1]


def _xchg_wait(name, started, gather, after):
    send_sems, recv_sems, srcs, lands, _ = started
    n = len(srcs)

    def body(*refs):
        src_refs, land_refs = refs[:n], refs[n:2 * n]
        send_r, recv_r = refs[2 * n], refs[2 * n + 1]
        me = _my_index()
        for j in range(N_DEV):
            @pl.when(me != j)
            def _(j=j):
                for i in range(n):
                    _split_copy(src_refs, land_refs, send_r, recv_r, gather, i, j, me, me).wait_send()
                    _split_copy(src_refs, land_refs, send_r, recv_r, gather, i, j, j, j).wait_recv()
        for i in range(n):
            _own_copy(src_refs, land_refs, send_r, gather, i, me).wait()

    thru = [pltpu.HBM(a.shape, a.dtype) for a in list(srcs) + list(lands)]
    res = pl.pallas_call(
        body, name=name, out_shape=tuple(thru),
        in_specs=[HBM_SPEC] * (2 * n) + [SEM_SPEC, SEM_SPEC] + [pl.BlockSpec(memory_space=pl.ANY)] * len(after),
        out_specs=tuple([HBM_SPEC] * (2 * n)),
        input_output_aliases={i: i for i in range(2 * n)},
        compiler_params=pltpu.CompilerParams(has_side_effects=SPLIT_EFFECT),
    )(*srcs, *lands, send_sems, recv_sems, *after)
    return list(res[n:])


def _adam_layers(w, slot_list, m, v, name):
    n_l, r, c = w.shape
    ns = slot_list[0].shape[0]
    tr = r
    while tr * c * 4 > (1 << 20) and tr % 16 == 0:
        tr //= 2
    assert r % tr == 0 and len(slot_list) == n_l

    def body(*refs):
        w_ref, slot_refs = refs[0], refs[1:1 + n_l]
        m_ref, v_ref, go_ref, d_ref, mo_ref, vo_ref = refs[1 + n_l:]
        layer = pl.program_id(0)
        g = None
        for l, g_ref in enumerate(slot_refs):
            gl = g_ref[0].astype(F32)
            for s in range(1, ns):
                gl = gl + g_ref[s].astype(F32)
            g = gl if g is None else jnp.where(layer == l, gl, g)
        mn = ADAM_B1 * m_ref[...] + (1.0 - ADAM_B1) * g
        vn = ADAM_B2 * v_ref[...] + (1.0 - ADAM_B2) * (g * g)
        m_hat = mn / (1.0 - ADAM_B1 ** ADAM_STEP)
        v_hat = vn / (1.0 - ADAM_B2 ** ADAM_STEP)
        go_ref[...] = g
        d_ref[...] = -ADAM_LR * (m_hat / (jnp.sqrt(v_hat) + ADAM_EPS) + ADAM_WD * w_ref[...])
        mo_ref[...] = mn
        vo_ref[...] = vn

    blk = pl.BlockSpec((None, tr, c), lambda l, i: (l, i, 0))
    slot_specs = [pl.BlockSpec((ns, tr, c), lambda l, i, k=k: (0, jnp.where(l == k, i, 0), 0)) for k in range(n_l)]
    return pl.pallas_call(
        body, name=name, grid=(n_l, r // tr),
        in_specs=[blk] + slot_specs + [blk, blk],
        out_specs=[blk] * 4, out_shape=[SDS((n_l, r, c), F32)] * 4,
        compiler_params=_cparams(("arbitrary", "arbitrary")),
    )(w, *slot_list, m, v)


def _adam(w, gslots, m, v, name):
    r, c = w.shape
    ns = gslots.shape[0]
    tr = r
    while tr * c * 4 > (1 << 20) and tr % 16 == 0:
        tr //= 2
    assert r % tr == 0

    def body(w_ref, g_ref, m_ref, v_ref, go_ref, d_ref, mo_ref, vo_ref):
        g = g_ref[0].astype(F32)
        for s in range(1, ns):
            g = g + g_ref[s].astype(F32)
        wv = w_ref[...]
        mn = ADAM_B1 * m_ref[...] + (1.0 - ADAM_B1) * g
        vn = ADAM_B2 * v_ref[...] + (1.0 - ADAM_B2) * (g * g)
        m_hat = mn / (1.0 - ADAM_B1 ** ADAM_STEP)
        v_hat = vn / (1.0 - ADAM_B2 ** ADAM_STEP)
        go_ref[...] = g
        d_ref[...] = -ADAM_LR * (m_hat / (jnp.sqrt(v_hat) + ADAM_EPS) + ADAM_WD * wv)
        mo_ref[...] = mn
        vo_ref[...] = vn

    blk = pl.BlockSpec((tr, c), lambda i: (i, 0))
    return pl.pallas_call(
        body, name=name, grid=(r // tr,),
        in_specs=[blk, pl.BlockSpec((ns, tr, c), lambda i: (0, i, 0)), blk, blk],
        out_specs=[blk] * 4, out_shape=[SDS((r, c), F32)] * 4,
        compiler_params=_cparams(("parallel",)),
    )(w, gslots, m, v)


def _sum_slots(slots, name):
    ns, r, c = slots.shape

    def body(g_ref, o_ref):
        g = g_ref[0]
        for s in range(1, ns):
            g = g + g_ref[s]
        o_ref[...] = g

    return pl.pallas_call(
        body, name=name, grid=(1,),
        in_specs=[pl.BlockSpec((ns, r, c), lambda i: (0, 0, 0))], out_specs=pl.BlockSpec((r, c), lambda i: (0, 0)),
        out_shape=SDS((r, c), F32), compiler_params=_cparams(("arbitrary",)),
    )(slots)


def _adam_params(params, name):
    n = len(params)

    def body(*refs):
        ins, outs = refs[:5 * n], refs[5 * n:]
        for p in range(n):
            w_ref, m_ref, v_ref, g_first, g_rest = ins[5 * p:5 * p + 5]
            go_ref, d_ref, mo_ref, vo_ref = outs[4 * p:4 * p + 4]
            for part, g_ref in ((slice(0, 1), g_first), (slice(1, w_ref.shape[0]), g_rest)):
                g = g_ref[...]
                mn = ADAM_B1 * m_ref[part] + (1.0 - ADAM_B1) * g
                vn = ADAM_B2 * v_ref[part] + (1.0 - ADAM_B2) * (g * g)
                m_hat = mn / (1.0 - ADAM_B1 ** ADAM_STEP)
                v_hat = vn / (1.0 - ADAM_B2 ** ADAM_STEP)
                go_ref[part] = g
                d_ref[part] = -ADAM_LR * (m_hat / (jnp.sqrt(v_hat) + ADAM_EPS) + ADAM_WD * w_ref[part])
                mo_ref[part] = mn
                vo_ref[part] = vn

    def whole(a):
        return pl.BlockSpec(a.shape, lambda i, nd=a.ndim: (0,) * nd)

    flat = [a for prm in params for a in prm]
    outs = pl.pallas_call(
        body, name=name, grid=(1,),
        in_specs=[whole(a) for a in flat],
        out_specs=[whole(prm[0]) for prm in params for _ in range(4)],
        out_shape=[SDS(prm[0].shape, F32) for prm in params for _ in range(4)],
        compiler_params=_cparams(("arbitrary",)),
    )(*flat)
    return [outs[4 * p:4 * p + 4] for p in range(n)]


SMALL_NAMES = ("pre_norm", "post_norm", "ssm_a_re", "ssm_a_im", "ssm_log_dt", "ssm_b_re", "ssm_b_im", "ssm_c_re",
               "ssm_c_im", "ssm_d", "ssm_glu_b")
SSM_NAMES = ("ssm_a_re", "ssm_a_im", "ssm_log_dt", "ssm_b_re", "ssm_b_im", "ssm_c_re", "ssm_c_im")
SHARDED_NAMES = ("even_w_in", "even_w_out", "ssm_glu_w", "odd_w_in", "pool_w", "odd_w_out")
WEIGHT_ORDER = ("pre_norm", "post_norm", "even_w_in", "even_w_out", "ssm_a_re", "ssm_a_im", "ssm_log_dt", "ssm_b_re",
                "ssm_b_im", "ssm_c_re", "ssm_c_im", "ssm_d", "ssm_glu_w", "ssm_glu_b", "odd_w_in", "pool_w",
                "pool_scale", "odd_w_out")
PACK_ROWS_ALIGN = 8


def _pack(parts):
    flat = jnp.concatenate([p.reshape(-1).astype(F32) for p in parts])
    rows = -(-flat.shape[0] // (LANES * PACK_ROWS_ALIGN)) * PACK_ROWS_ALIGN
    return jnp.pad(flat, (0, rows * LANES - flat.shape[0])).reshape(rows, LANES)


def _unpack(packed, shapes):
    flat = packed.reshape(-1)
    out, off = [], 0
    for shp in shapes:
        size = math.prod(shp)
        out.append(flat[off:off + size].reshape(shp))
        off += size
    return out


EVEN_SHARDED = ("w_in", "w_out", "glu_w")
ODD_SHARDED = ("w_in", "pool_w", "w_out")
FAMILY = {(0, "w_in"): "even_w_in", (0, "w_out"): "even_w_out", (0, "glu_w"): "ssm_glu_w",
          (1, "w_in"): "odd_w_in", (1, "pool_w"): "pool_w", (1, "w_out"): "odd_w_out"}


def _sharded_keys(layer):
    return EVEN_SHARDED if layer % 2 == 0 else ODD_SHARDED


def _local_step(x, tgt, small, get_weights, on_w, on_grads, zero=0.0):
    tables = _rope_tables(zero) + (_attention_bias(zero),)
    preps, prep_vjps = [], []
    for i in range(2):
        out, vjp = jax.vjp(_ssm_prep, small["ssm_a_re"][i] + zero, small["ssm_a_im"][i], small["ssm_log_dt"][i],
                           small["ssm_b_re"][i], small["ssm_b_im"][i], small["ssm_c_re"][i], small["ssm_c_im"][i])
        preps.append(out)
        prep_vjps.append(vjp)

    def layer_args(layer, wts):
        i = layer // 2
        pre, post = _row(small["pre_norm"][layer]) + wts.get("token", 0.0), _row(small["post_norm"][layer])
        if layer % 2 == 0:
            return (pre, post, wts["w_in"], wts["late"], _row(small["ssm_glu_b"][i]), _row(small["ssm_d"][i]),
                    preps[i], tables)
        return (pre, post, wts["w_in"], wts["pool_w"], _row(wts["pool_scale"]), wts["w_out"])

    saved, args = [], []
    cur = x
    for layer in range(4):
        after = (cur,) if layer else (cur, tables[0], tables[3], preps[0][1], preps[0][2], preps[1][1], preps[1][2])
        args.append(layer_args(layer, get_weights(layer, after)))
        cur, sv = (_even_fwd if layer % 2 == 0 else _odd_fwd)(cur, *args[layer])
        saved.append(sv)
    g, sq = _loss_grad(cur, tgt)
    loss = 0.5 * jnp.sum(sq) / D

    lg = [None] * 4
    token = jnp.zeros((), F32)
    for layer in reversed(range(4)):
        largs = list(args[layer])
        largs[1] = largs[1] + token
        g, lg[layer] = (_even_bwd if layer % 2 == 0 else _odd_bwd)(g, saved[layer], *largs,
                                                                   on_w=functools.partial(on_w, layer))
        if layer % 2 == 0:
            lg[layer]["ssm"] = prep_vjps[layer // 2](lg[layer].pop("prep"))
        token = on_grads(layer, lg[layer])
    return loss, g, token


def _to_slots(key, gfull):
    if key == "w_in":
        return gfull
    if key in ("w_out", "glu_w"):
        rr, nn = gfull.shape
        return gfull.reshape(N_DEV, rr // N_DEV, nn)
    assert key == "pool_w"
    gg, rr, nn = gfull.shape
    return gfull.reshape(gg, N_DEV, rr // N_DEV, nn).transpose(1, 0, 2, 3)


def _from_gathered(key, gat):
    if key == "w_in":
        return gat
    if key in ("w_out", "glu_w"):
        _, rr, nn = gat.shape
        return gat.reshape(N_DEV * rr, nn)
    assert key == "pool_w"
    _, gg, rr, nn = gat.shape
    return gat.transpose(1, 0, 2, 3).reshape(gg, N_DEV * rr, nn)


def kernel(x, pre_norm, post_norm, even_w_in, even_w_out, ssm_a_re, ssm_a_im, ssm_log_dt, ssm_b_re, ssm_b_im, ssm_c_re, ssm_c_im, ssm_d, ssm_glu_w, ssm_glu_b, odd_w_in, pool_w, pool_scale, odd_w_out, loss_target, m_pre_norm, m_post_norm, m_even_w_in, m_even_w_out, m_ssm_a_re, m_ssm_a_im, m_ssm_log_dt, m_ssm_b_re, m_ssm_b_im, m_ssm_c_re, m_ssm_c_im, m_ssm_d, m_ssm_glu_w, m_ssm_glu_b, m_odd_w_in, m_pool_w, m_pool_scale, m_odd_w_out, v_pre_norm, v_post_norm, v_even_w_in, v_even_w_out, v_ssm_a_re, v_ssm_a_im, v_ssm_log_dt, v_ssm_b_re, v_ssm_b_im, v_ssm_c_re, v_ssm_c_im, v_ssm_d, v_ssm_glu_w, v_ssm_glu_b, v_odd_w_in, v_pool_w, v_pool_scale, v_odd_w_out):
    w = dict(pre_norm=pre_norm, post_norm=post_norm, even_w_in=even_w_in, even_w_out=even_w_out, ssm_a_re=ssm_a_re,
             ssm_a_im=ssm_a_im, ssm_log_dt=ssm_log_dt, ssm_b_re=ssm_b_re, ssm_b_im=ssm_b_im, ssm_c_re=ssm_c_re,
             ssm_c_im=ssm_c_im, ssm_d=ssm_d, ssm_glu_w=ssm_glu_w, ssm_glu_b=ssm_glu_b, odd_w_in=odd_w_in,
             pool_w=pool_w, pool_scale=pool_scale, odd_w_out=odd_w_out)
    mom = dict(pre_norm=m_pre_norm, post_norm=m_post_norm, even_w_in=m_even_w_in, even_w_out=m_even_w_out,
               ssm_a_re=m_ssm_a_re, ssm_a_im=m_ssm_a_im, ssm_log_dt=m_ssm_log_dt, ssm_b_re=m_ssm_b_re,
               ssm_b_im=m_ssm_b_im, ssm_c_re=m_ssm_c_re, ssm_c_im=m_ssm_c_im, ssm_d=m_ssm_d, ssm_glu_w=m_ssm_glu_w,
               ssm_glu_b=m_ssm_glu_b, odd_w_in=m_odd_w_in, pool_w=m_pool_w, pool_scale=m_pool_scale,
               odd_w_out=m_odd_w_out)
    var = dict(pre_norm=v_pre_norm, post_norm=v_post_norm, even_w_in=v_even_w_in, even_w_out=v_even_w_out,
               ssm_a_re=v_ssm_a_re, ssm_a_im=v_ssm_a_im, ssm_log_dt=v_ssm_log_dt, ssm_b_re=v_ssm_b_re,
               ssm_b_im=v_ssm_b_im, ssm_c_re=v_ssm_c_re, ssm_c_im=v_ssm_c_im, ssm_d=v_ssm_d, ssm_glu_w=v_ssm_glu_w,
               ssm_glu_b=v_ssm_glu_b, odd_w_in=v_odd_w_in, pool_w=v_pool_w, pool_scale=v_pool_scale,
               odd_w_out=v_odd_w_out)
    me = _my_index()
    scale_cols = pool_scale.shape[1]

    def start_gather(tag, layer, keys, after=()):
        i = layer // 2
        shards = [w[FAMILY[(layer % 2, k)]][i].astype(BF16) for k in keys]
        if layer % 2 == 1:
            shards.append(jnp.pad(pool_scale[i][None], ((0, PACK_ROWS_ALIGN - 1), (0, 0))))
        return _xchg_start(f"gather_start_{tag}", shards, True, after)

    gather_started = {0: start_gather("0", 0, EVEN_SHARDED[:1])}
    small = {nm: w[nm] for nm in SMALL_NAMES}

    def get_weights(layer, after):
        keys = EVEN_SHARDED[:1] if layer == 0 else _sharded_keys(layer)
        lands = _xchg_wait(f"gather_wait_{layer}", gather_started[layer], True, after)
        wts = {k: _from_gathered(k, gat) for k, gat in zip(keys, lands)}
        if layer % 2 == 1:
            wts["pool_scale"] = lands[-1][:, 0, :].reshape(N_DEV * scale_cols)
        if layer == 0:
            gather_started["0_late"] = start_gather("0_late", 0, EVEN_SHARDED[1:], after=(lands[0],))
            for later in (1, 2, 3):
                gather_started[later] = start_gather(str(later), later, _sharded_keys(later), after=(lands[0],))
            wts["token"] = sum(gather_started[tag][4][0, 0] for tag in ("0_late", 1, 2, 3))

            def late(after_late):
                late_lands = _xchg_wait("gather_wait_0_late", gather_started["0_late"], True, (after_late,))
                return tuple(_from_gathered(k, gat) for k, gat in zip(EVEN_SHARDED[1:], late_lands))

            wts["late"] = late
        elif layer == 2:
            wts["late"] = lambda after_late: (wts["w_out"], wts["glu_w"])
        return wts

    scatter_started = [None] * 4

    def on_w(layer, gw):
        slots = [_to_slots(k, gw[k]) for k in _sharded_keys(layer)]
        scatter_started[layer] = _xchg_start(f"scatter_start_{layer}", slots, False)
        return scatter_started[layer][4]

    packed_names = ("pre_norm", "post_norm") + SSM_NAMES + ("ssm_d", "ssm_glu_b")
    tails = {nm: (SSM_GROUPS, SSM_STATE * SSM_GROUP) if nm in ("ssm_b_re", "ssm_b_im") else w[nm].shape[1:]
             for nm in packed_names}

    layer_grads = {}
    early_started = []

    def on_grads(layer, lg):
        layer_grads[layer] = lg
        zero = jnp.zeros((), F32)
        if layer == 1:
            lgs = layer_grads
            early = ([jnp.concatenate([lgs[l][k] for l in (1, 2, 3)], axis=0) for k in ("pre", "post")]
                     + list(lgs[2]["ssm"]) + [lgs[2]["ssm_d"], lgs[2]["glu_b"],
                                              jnp.concatenate([lgs[1]["pool_scale"], lgs[3]["pool_scale"]], axis=0)])
            early_started.append(_xchg_start("small_start", [_pack(early)], True))
            zero = zero + early_started[0][4][0, 0]
        return zero

    loss_local, grad_x, token = _local_step(x[0], loss_target[0], small, get_weights, on_w, on_grads,
                                            zero=gather_started[0][4][0, 0])

    def adam_family(parity, k):
        nm = FAMILY[(parity, k)]
        shp = w[nm].shape
        cols = shp[-1]
        slot_list = [recv[(parity + 2 * i, k)].reshape(N_DEV, -1, cols) for i in range(2)]
        outs = _adam_layers(w[nm].reshape(2, -1, cols), slot_list, mom[nm].reshape(2, -1, cols),
                            var[nm].reshape(2, -1, cols), name=f"adam_{nm}")
        return [o.reshape(shp) for o in outs]

    recv, res = {}, {}
    for layer in (3, 1):
        lands = _xchg_wait(f"scatter_wait_{layer}", scatter_started[layer], False, (scatter_started[0][4],))
        for k, land in zip(_sharded_keys(layer), lands):
            recv[(layer, k)] = land
    for k in ODD_SHARDED:
        res[FAMILY[(1, k)]] = adam_family(1, k)

    lg0 = layer_grads[0]
    late = [lg0["pre"], lg0["post"]] + list(lg0["ssm"]) + [lg0["ssm_d"], lg0["glu_b"], loss_local.reshape(1)]
    (late_slots,) = _exchange([_pack(late) + token], True, "gather_small_grads",
                              after=tuple(res[FAMILY[(1, k)]][0] for k in ODD_SHARDED))
    (early_slots,) = _xchg_wait("small_wait", early_started[0], True, (late_slots,))

    for layer in (2, 0):
        lands = _xchg_wait(f"scatter_wait_{layer}", scatter_started[layer], False, (late_slots,))
        for k, land in zip(_sharded_keys(layer), lands):
            recv[(layer, k)] = land
    for k in EVEN_SHARDED:
        res[FAMILY[(0, k)]] = adam_family(0, k)

    late_shapes = [(1,) + tails[nm] for nm in packed_names] + [(1,)]
    early_shapes = [(w[nm].shape[0] - 1,) + tails[nm] for nm in packed_names] + [(2, N_DEV * scale_cols)]
    g_late = _unpack(_sum_slots(late_slots, "sum_small_late"), late_shapes)
    g_early = _unpack(_sum_slots(early_slots, "sum_small_early"), early_shapes)
    dense = lambda nm, a: a.reshape((a.shape[0],) + tails[nm])
    outs = _adam_params([(dense(nm, w[nm]), dense(nm, mom[nm]), dense(nm, var[nm]), g_late[j], g_early[j])
                         for j, nm in enumerate(packed_names)], "adam_small")
    for nm, four in zip(packed_names, outs):
        res[nm] = [o.reshape(w[nm].shape) for o in four]
    loss = g_late[-1].reshape(())
    g_scale = lax.dynamic_slice_in_dim(g_early[-1], me * scale_cols, scale_cols, axis=1)
    pad = ((0, PACK_ROWS_ALIGN - 2), (0, 0))
    outs = _adam(jnp.pad(pool_scale, pad), jnp.pad(g_scale, pad)[None], jnp.pad(m_pool_scale, pad),
                 jnp.pad(v_pool_scale, pad), name="adam_pool_scale")
    res["pool_scale"] = [o[:2] for o in outs]

    out = [loss, grad_x[None]]
    for kind in range(4):
        out += [res[nm][kind] for nm in WEIGHT_ORDER]
    return tuple(out)
```

```python
import functools
import math

import jax
import jax.numpy as jnp
from jax import lax
from jax.experimental import pallas as pl
from jax.experimental.pallas import tpu as pltpu

F32 = jnp.float32
BF16 = jnp.bfloat16
SDS = jax.ShapeDtypeStruct

N_DEV = 8
S = 2048
D = 1024
HEAD_DIM = 64
ROT_DIM = 16
ROPE_THETA = 500000.0
ATT_W = 1024
SSM_W = 512
SSM_GROUPS = 32
SSM_GROUP = 16
SSM_STATE = 64
N_CPLX = SSM_GROUPS * SSM_STATE
POOL_W = 2048
POOL_GROUP = 512
EVEN_IN = 5120
EVEN_OUT = 1536
ODD_IN = 4096
RMS_EPS = 1e-6
LANES = 128
VMEM_LIMIT = 48 * 1024 * 1024

ADAM_LR = 0.001
ADAM_B1 = 0.9
ADAM_B2 = 0.999
ADAM_EPS = 1e-08
ADAM_WD = 0.01
ADAM_STEP = 10

MESH_ID = pl.DeviceIdType.MESH
NN = (((1,), (0,)), ((), ()))
NT = (((1,), (1,)), ((), ()))
TN = (((0,), (0,)), ((), ()))
_DN = {"nn": NN, "nt": NT, "tn": TN}


def _cparams(sem):
    return pltpu.CompilerParams(dimension_semantics=sem, vmem_limit_bytes=VMEM_LIMIT)


MM_TILES = (1024, 768, 512)


def _tile(dim):
    return next((t for t in MM_TILES if dim % t == 0), dim)


NT_BLOCKS_PER_STEP = 4


def _mm(a, b, mode, out_dtype, b_blocks=False, out_blocks=False, a_cols=None, after=()):
    if b_blocks:
        nblk, rows, cb = b.shape
        b2_shape = (rows, nblk * cb)
    else:
        b2_shape = b.shape
    a_shape = a.shape if a_cols is None else (a.shape[0], a_cols[1])
    if mode == "nn":
        (m, k), n = a_shape, b2_shape[1]
    elif mode == "nt":
        (m, k), n = a_shape, b2_shape[0]
    else:
        (k, m), n = a_shape, b2_shape[1]
    tm, tn, tk = _tile(m), _tile(n), _tile(k)
    per_step = 1
    if b_blocks and mode == "nn":
        tn = cb
    if b_blocks and mode == "nt":
        per_step = NT_BLOCKS_PER_STEP
        tk = per_step * cb
        tn = min(tn, MM_TILES[-1])
    if out_blocks:
        tn = n // N_DEV
        tk = k
    nk = k // tk
    a_unit = tm if mode == "tn" else tk
    assert a_cols is None or a_cols[0] % a_unit == 0
    a_off = 0 if a_cols is None else a_cols[0] // a_unit

    def body(a_ref, b_ref, *rest):
        o_ref, acc_ref = rest[-2:]
        kk = pl.program_id(2)
        if per_step == 1:
            part = lax.dot_general(a_ref[...].astype(BF16), b_ref[...].astype(BF16), _DN[mode],
                                   preferred_element_type=F32)
        else:
            part = None
            for blk in range(per_step):
                d = lax.dot_general(a_ref[:, blk * cb:(blk + 1) * cb].astype(BF16), b_ref[blk].astype(BF16), NT,
                                    preferred_element_type=F32)
                part = d if part is None else part + d
        if nk == 1:
            o_ref[...] = part.astype(o_ref.dtype)
            return

        @pl.when(kk == 0)
        def _():
            acc_ref[...] = part

        @pl.when((kk > 0) & (kk < nk - 1))
        def _():
            acc_ref[...] += part

        @pl.when(kk == nk - 1)
        def _():
            o_ref[...] = (acc_ref[...] + part).astype(o_ref.dtype)

    if mode == "nn":
        a_spec = pl.BlockSpec((tm, tk), lambda i, j, kk: (i, a_off + kk))
        b_spec = pl.BlockSpec((tk, tn), lambda i, j, kk: (kk, j))
    elif mode == "nt":
        a_spec = pl.BlockSpec((tm, tk), lambda i, j, kk: (i, a_off + kk))
        b_spec = pl.BlockSpec((tn, tk), lambda i, j, kk: (j, kk))
    else:
        a_spec = pl.BlockSpec((tk, tm), lambda i, j, kk: (kk, a_off + i))
        b_spec = pl.BlockSpec((tk, tn), lambda i, j, kk: (kk, j))
    if b_blocks and mode == "nn":
        b_spec = pl.BlockSpec((None, tk, cb), lambda i, j, kk: (j, kk, 0))
    if b_blocks and mode == "nt":
        b_spec = pl.BlockSpec((per_step, tn, cb), lambda i, j, kk: (kk, j, 0))
    out_spec = pl.BlockSpec((tm, tn), lambda i, j, kk: (i, j))
    out_shape = SDS((m, n), out_dtype)
    if out_blocks:
        out_spec = pl.BlockSpec((None, tm, tn), lambda i, j, kk: (j, i, 0))
        out_shape = SDS((N_DEV, m, tn), out_dtype)
    return pl.pallas_call(
        body, name=f"mm_{mode}_{m}x{k}x{n}",
        grid=(m // tm, n // tn, nk),
        in_specs=[a_spec, b_spec] + [pl.BlockSpec(memory_space=pl.ANY)] * len(after),
        out_specs=out_spec,
        out_shape=out_shape,
        scratch_shapes=[pltpu.VMEM((tm, tn) if nk > 1 else (8, LANES), F32)],
        compiler_params=_cparams(("parallel", "parallel", "arbitrary")),
    )(a, b, *after)


def _gmm(a, b, mode, out_dtype, tm=512):
    ng, gw = POOL_W // POOL_GROUP, POOL_GROUP
    ns = S // tm
    if mode in ("nn", "nt"):
        def body(a_ref, b_ref, o_ref):
            o_ref[...] = lax.dot_general(a_ref[...].astype(BF16), b_ref[...].astype(BF16), _DN[mode],
                                         preferred_element_type=F32).astype(o_ref.dtype)

        return pl.pallas_call(
            body, name=f"gmm_{mode}", grid=(ng, ns),
            in_specs=[pl.BlockSpec((tm, gw), lambda g, i: (i, g)),
                      pl.BlockSpec((None, gw, gw), lambda g, i: (g, 0, 0))],
            out_specs=pl.BlockSpec((tm, gw), lambda g, i: (i, g)),
            out_shape=SDS((S, POOL_W), out_dtype),
            compiler_params=_cparams(("parallel", "parallel")),
        )(a, b)

    def body_tn(a_ref, b_ref, o_ref, acc_ref):
        i = pl.program_id(1)

        @pl.when(i == 0)
        def _():
            acc_ref[...] = jnp.zeros_like(acc_ref)

        acc_ref[...] += lax.dot_general(a_ref[...].astype(BF16), b_ref[...].astype(BF16), TN,
                                        preferred_element_type=F32)

        @pl.when(i == ns - 1)
        def _():
            o_ref[...] = acc_ref[...].astype(o_ref.dtype)

    return pl.pallas_call(
        body_tn, name="gmm_tn", grid=(ng, ns),
        in_specs=[pl.BlockSpec((tm, gw), lambda g, i: (i, g)),
                  pl.BlockSpec((tm, gw), lambda g, i: (i, g))],
        out_specs=pl.BlockSpec((None, gw, gw), lambda g, i: (g, 0, 0)),
        out_shape=SDS((ng, gw, gw), out_dtype),
        scratch_shapes=[pltpu.VMEM((gw, gw), F32)],
        compiler_params=_cparams(("parallel", "arbitrary")),
    )(a, b)


def _rowwise(fn, inputs, out_defs, acc_defs=(), tm=256, name=None):
    n_in, n_out, n_acc = len(inputs), len(out_defs), len(acc_defs)
    in_specs, args = [], []
    for arr, width, cb in inputs:
        if arr.shape[0] == 1:
            in_specs.append(pl.BlockSpec((1, width), lambda i, cb=cb: (0, cb)))
        else:
            in_specs.append(pl.BlockSpec((tm, width), lambda i, cb=cb: (i, cb)))
        args.append(arr)
    out_defs = [d if len(d) == 4 else (d[0], d[1], d[0], 0) for d in out_defs]
    out_shape = [SDS((S, ww), dt) for _, dt, ww, _ in out_defs] + [SDS((1, w), F32) for w in acc_defs]
    out_specs = ([pl.BlockSpec((tm, w), lambda i, cb=cb: (i, cb)) for w, _, _, cb in out_defs]
                 + [pl.BlockSpec((1, w), lambda i: (0, 0)) for w in acc_defs])

    def kern(*refs):
        vals = [r[...] for r in refs[:n_in]]
        outs, accs = fn(*vals)
        for r, v in zip(refs[n_in:n_in + n_out], outs):
            r[...] = v.astype(r.dtype)
        if n_acc:
            acc_refs = refs[n_in + n_out:]

            @pl.when(pl.program_id(0) == 0)
            def _():
                for r in acc_refs:
                    r[...] = jnp.zeros_like(r)

            for r, v in zip(acc_refs, accs):
                r[...] += jnp.sum(v, axis=0, keepdims=True)

    res = pl.pallas_call(
        kern, name=name, grid=(S // tm,), in_specs=in_specs, out_specs=out_specs, out_shape=out_shape,
        compiler_params=_cparams(("arbitrary",)),
    )(*args)
    return res


def _sigmoid(x):
    return 1.0 / (1.0 + jnp.exp(-x))


def _silu_and_grad(x):
    s = _sigmoid(x)
    return x * s, s * (1.0 + x * (1.0 - s))


_GELU_K = math.sqrt(2.0 / math.pi)
_GELU_C = 0.044715


def _gelu_and_grad(x):
    t = jnp.tanh(_GELU_K * (x + _GELU_C * (x * x * x)))
    cdf = 0.5 * (1.0 + t)
    grad = cdf + 0.5 * x * (1.0 - t * t) * (_GELU_K * (1.0 + 3.0 * _GELU_C * x * x))
    return x * cdf, grad


def _rms(xv, gain):
    r = lax.rsqrt(jnp.mean(xv * xv, axis=-1, keepdims=True) + RMS_EPS)
    return xv * r * gain


def _rms_bwd(dout, xv, gain):
    r = lax.rsqrt(jnp.mean(xv * xv, axis=-1, keepdims=True) + RMS_EPS)
    xhat = xv * r
    dxhat = dout * gain
    dx = r * (dxhat - xhat * jnp.mean(dxhat * xhat, axis=-1, keepdims=True))
    return dx, dout * xhat


def _norm_fwd(x, gain):
    (h,) = _rowwise(lambda xv, g: ((_rms(xv, g),), ()), [(x, D, 0), (gain, D, 0)], [(D, BF16)], name="norm_fwd")
    return h


def _post_fwd(x, y, gain):
    (o,) = _rowwise(lambda xv, yv, g: ((xv + _rms(yv, g),), ()), [(x, D, 0), (y, D, 0), (gain, D, 0)],
                    [(D, F32)], name="post_fwd")
    return o


def _post_bwd(g, y, gain):
    def fn(gv, yv, gn):
        dx, dg = _rms_bwd(gv, yv, gn)
        return (dx,), (dg,)

    return _rowwise(fn, [(g, D, 0), (y, D, 0), (gain, D, 0)], [(D, BF16)], [D], name="post_bwd")


def _pre_bwd(g, dh, x, gain):
    def fn(gv, dhv, xv, gn):
        dx, dg = _rms_bwd(dhv, xv, gn)
        return (gv + dx,), (dg,)

    return _rowwise(fn, [(g, D, 0), (dh, D, 0), (x, D, 0), (gain, D, 0)], [(D, F32)], [D], name="pre_bwd")


def _loss_grad(xo, tgt):
    def fn(xv, tv):
        e = xv - tv
        return (e * (1.0 / D),), (e * e,)

    return _rowwise(fn, [(xo, D, 0), (tgt, D, 0)], [(D, F32)], [D], name="loss_grad")


def _pool(u_arr, col_block, transpose, out_dtype, into=None, tc=256):
    n_t = POOL_W // tc
    per_group = POOL_GROUP // tc

    def body(u_ref, *rest):
        o_ref = rest[-1]
        c = pl.program_id(0)
        grp = c // per_group
        xv = u_ref[...]
        t = lax.broadcasted_iota(jnp.int32, (S, 1), 0)
        win = jnp.left_shift(2, grp)
        cnt = jnp.minimum(t + 1, win).astype(F32)
        cur = xv / cnt if transpose else xv
        sums = []
        for k in (1, 2, 4, 8):
            if transpose:
                sh = jnp.where(t < S - k, pltpu.roll(cur, S - k, 0), 0.0)
            else:
                sh = jnp.where(t >= k, pltpu.roll(cur, k, 0), 0.0)
            cur = cur + sh
            sums.append(cur)
        tot = jnp.where(grp == 0, sums[0], jnp.where(grp == 1, sums[1], jnp.where(grp == 2, sums[2], sums[3])))
        res = tot - xv if transpose else tot / cnt - xv
        o_ref[...] = res.astype(o_ref.dtype)

    in_specs = [pl.BlockSpec((S, tc), lambda c: (0, col_block * n_t + c))]
    args = [u_arr]
    if into is not None:
        in_specs.append(pl.BlockSpec(memory_space=pl.ANY))
        args.append(into)
    return pl.pallas_call(
        body, name="pool_bwd" if transpose else "pool_fwd", grid=(n_t,),
        in_specs=in_specs,
        out_specs=pl.BlockSpec((S, tc), lambda c: (0, c)),
        out_shape=SDS((S, POOL_W) if into is None else into.shape, out_dtype),
        input_output_aliases={} if into is None else {1: 0},
        compiler_params=_cparams(("parallel",)),
    )(*args)


def _rope_tables(zero):
    pos = jnp.arange(S, dtype=jnp.int32).astype(F32) + zero
    inv_freq = ROPE_THETA ** (-jnp.arange(0, ROT_DIM, 2, dtype=F32) / ROT_DIM)
    ang = pos[:, None] * inv_freq[None, :]
    cos8, sin8 = jnp.cos(ang), jnp.sin(ang)
    half = ROT_DIM // 2
    zeros = jnp.zeros((S, HEAD_DIM - ROT_DIM), F32)
    cos = jnp.concatenate([cos8, cos8, jnp.ones((S, HEAD_DIM - ROT_DIM), F32)], axis=1)
    lo = jnp.concatenate([-sin8, jnp.zeros((S, half), F32), zeros], axis=1)
    hi = jnp.concatenate([jnp.zeros((S, half), F32), sin8, zeros], axis=1)
    rep = LANES // HEAD_DIM
    return jnp.tile(cos, (1, rep)), jnp.tile(lo, (1, rep)), jnp.tile(hi, (1, rep))


def _rotate(xv, cos, lo, hi, transpose):
    width = xv.shape[1]
    rep = width // LANES
    wide = lambda tab: jnp.concatenate([tab] * rep, axis=1)
    half = ROT_DIM // 2
    up = pltpu.roll(xv, width - half, 1)
    dn = pltpu.roll(xv, half, 1)
    mixed = up * wide(lo) + dn * wide(hi)
    return xv * wide(cos) - mixed if transpose else xv * wide(cos) + mixed


def _qkv_prep(proj, tables):
    cos, lo, hi = tables

    def fn(x, c, l, h):
        rot = _rotate(x[:, :2 * ATT_W], c, l, h, False)
        return (jnp.concatenate([(rot[:, :ATT_W] * HEAD_DIM ** -0.5).astype(BF16), rot[:, ATT_W:].astype(BF16),
                                 x[:, 2 * ATT_W:].astype(BF16)], axis=1),), ()

    (qkv,) = _rowwise(fn, [(proj, 3 * ATT_W, 0), (cos, LANES, 0), (lo, LANES, 0), (hi, LANES, 0)],
                      [(3 * ATT_W, BF16)], name="qkv_prep")
    return qkv


ATT_T = 512


def _multiplicity(delta):
    ok = delta >= 0
    near = jnp.where(ok & (delta <= 128), 1.0, 0.0)
    mid = jnp.where(ok & (delta <= 512) & ((delta & 3) == 0), 1.0, 0.0)
    far = jnp.where(ok & ((delta & 15) == 0), 1.0, 0.0)
    return near + mid + far


def _attention_bias(zero):
    t = ATT_T
    pos = jnp.arange(t, dtype=jnp.int32) + jnp.asarray(zero).astype(jnp.int32)
    delta = jnp.arange(S // t, dtype=jnp.int32)[:, None, None] * t + pos[None, :, None] - pos[None, None, :]
    mult = _multiplicity(delta)
    return jnp.where(mult > 0.0, jnp.log(jnp.maximum(mult, 1.0)), -1e30).astype(F32)


def _head_split(v, first):
    zero = jnp.zeros_like(v)
    return [jnp.where(first, v, zero), jnp.where(first, zero, v)]


def _flash_fwd(qkv, bias):
    t = ATT_T
    n_hp = ATT_W // LANES

    def body(q_ref, k_ref, v_ref, b_ref, o_ref, lse_ref):
        i = pl.program_id(1)
        first = lax.broadcasted_iota(jnp.int32, (1, LANES), 1) < HEAD_DIM
        qs = _head_split(q_ref[...], first)

        def kv_step(j, carry):
            m0, l0, m1, l1, acc = carry
            off = pl.multiple_of(j * t, t)
            kb = k_ref[pl.ds(off, t), :]
            vs = _head_split(v_ref[pl.ds(off, t), :], first)
            bias_t = b_ref[i - j]
            new = []
            pv = None
            for h, (m_prev, l_prev) in enumerate(((m0, l0), (m1, l1))):
                s = lax.dot_general(qs[h], kb, NT, preferred_element_type=F32) + bias_t
                m_new = jnp.maximum(m_prev, jnp.max(s, axis=1, keepdims=True))
                p = jnp.exp(s - m_new)
                alpha = jnp.exp(m_prev - m_new)
                l_new = alpha * l_prev + jnp.sum(p, axis=1, keepdims=True)
                d = lax.dot_general(p.astype(BF16), vs[h], NN, preferred_element_type=F32)
                pv = d if pv is None else pv + d
                new.append((m_new, l_new, alpha))
            acc = acc * jnp.where(first, new[0][2], new[1][2]) + pv
            return new[0][0], new[0][1], new[1][0], new[1][1], acc

        neg = jnp.full((t, 1), -1e30, F32)
        zero = jnp.zeros((t, 1), F32)
        m0, l0, m1, l1, acc = lax.fori_loop(0, i + 1, kv_step, (neg, zero, neg, zero, jnp.zeros((t, LANES), F32)))
        o_ref[...] = acc * jnp.where(first, 1.0 / l0, 1.0 / l1)
        lse_ref[...] = jnp.where(first, m0 + jnp.log(l0), m1 + jnp.log(l1))

    blk = pl.BlockSpec((t, LANES), lambda hp, i: (i, hp))
    k_full = pl.BlockSpec((S, LANES), lambda hp, i: (0, n_hp + hp))
    v_full = pl.BlockSpec((S, LANES), lambda hp, i: (0, 2 * n_hp + hp))
    return pl.pallas_call(
        body, name="flash_fwd", grid=(n_hp, S // t),
        in_specs=[blk, k_full, v_full, pl.BlockSpec((S // t, t, t), lambda hp, i: (0, 0, 0))], out_specs=[blk, blk],
        out_shape=[SDS((S, ATT_W), F32), SDS((S, ATT_W), F32)],
        compiler_params=_cparams(("parallel", "arbitrary")),
    )(qkv, qkv, qkv, bias)


def _flash_bwd(qkv, o, do, lse, bias, after=()):
    t = ATT_T
    n_hp = ATT_W // LANES
    n_t = S // t

    def body(q_ref, k_ref, v_ref, o_ref, do_ref, lse_ref, b_ref, *rest):
        dq_ref, dk_ref, dv_ref = rest[-3:]
        j = pl.program_id(1)
        first = lax.broadcasted_iota(jnp.int32, (1, LANES), 1) < HEAD_DIM

        @pl.when(j == 0)
        def _():
            dq_ref[...] = jnp.zeros_like(dq_ref)

        kb = k_ref[...]
        vb = v_ref[...]
        ks = _head_split(kb, first)

        def q_step(i, carry):
            dk_acc, dv_acc = carry
            rows = pl.ds(pl.multiple_of(i * t, t), t)
            qs = _head_split(q_ref[rows, :], first)
            dob = do_ref[rows, :]
            prod = dob * o_ref[rows, :]
            d_all = jnp.sum(prod, axis=1, keepdims=True)
            d0 = jnp.sum(jnp.where(first, prod, 0.0), axis=1, keepdims=True)
            lse_b = lse_ref[rows, :]
            lse0 = jnp.max(jnp.where(first, lse_b, -jnp.inf), axis=1, keepdims=True)
            lse1 = jnp.max(jnp.where(first, -jnp.inf, lse_b), axis=1, keepdims=True)
            dos = _head_split(dob.astype(BF16), first)
            bias_t = b_ref[i - j]
            dq_t = jnp.zeros((t, LANES), F32)
            for h, (lse_h, d_h) in enumerate(((lse0, d0), (lse1, d_all - d0))):
                s = lax.dot_general(qs[h], kb, NT, preferred_element_type=F32)
                p = jnp.exp(s + (bias_t - lse_h))
                dp = lax.dot_general(dos[h], vb, NT, preferred_element_type=F32)
                ds = (p * (dp - d_h)).astype(BF16)
                dv_acc = dv_acc + lax.dot_general(p.astype(BF16), dos[h], TN, preferred_element_type=F32)
                dk_acc = dk_acc + lax.dot_general(ds, qs[h], TN, preferred_element_type=F32)
                dq_t = dq_t + lax.dot_general(ds, ks[h], NN, preferred_element_type=F32)
            dq_ref[rows, :] += dq_t
            return dk_acc, dv_acc

        zero = jnp.zeros((t, LANES), F32)
        dk_acc, dv_acc = lax.fori_loop(j, n_t, q_step, (zero, zero))
        dk_ref[...] = dk_acc
        dv_ref[...] = dv_acc

    blk = pl.BlockSpec((t, LANES), lambda hp, j: (j, hp))
    full = pl.BlockSpec((S, LANES), lambda hp, j: (0, hp))
    k_blk = pl.BlockSpec((t, LANES), lambda hp, j: (j, n_hp + hp))
    v_blk = pl.BlockSpec((t, LANES), lambda hp, j: (j, 2 * n_hp + hp))
    return pl.pallas_call(
        body, name="flash_bwd", grid=(n_hp, n_t),
        in_specs=([full, k_blk, v_blk, full, full, full, pl.BlockSpec((n_t, t, t), lambda hp, j: (0, 0, 0))]
                  + [pl.BlockSpec(memory_space=pl.ANY)] * len(after)),
        out_specs=[full, blk, blk],
        out_shape=[SDS((S, ATT_W), F32)] * 3,
        compiler_params=_cparams(("parallel", "arbitrary")),
    )(qkv, qkv, qkv, o, do, lse, bias, *after)


SCAN_T = 256
ST_ROWS = 2 * N_CPLX // LANES
HALF = ST_ROWS // 2


def _scan_fwd(lam, bu):
    def body(lam_ref, bu_ref, st_ref, carry):
        @pl.when(pl.program_id(0) == 0)
        def _():
            carry[...] = jnp.zeros_like(carry)

        ar, ai = lam_ref[0:HALF, :], lam_ref[HALF:ST_ROWS, :]

        def step(t, c):
            sr, si = c
            b = bu_ref[t]
            nr = ar * sr - ai * si + b[0:HALF]
            ni = ar * si + ai * sr + b[HALF:ST_ROWS]
            st_ref[t, 0:HALF, :] = nr
            st_ref[t, HALF:ST_ROWS, :] = ni
            return nr, ni

        sr, si = lax.fori_loop(0, SCAN_T, step, (carry[0:HALF, :], carry[HALF:ST_ROWS, :]), unroll=8)
        carry[0:HALF, :] = sr
        carry[HALF:ST_ROWS, :] = si

    blk = pl.BlockSpec((SCAN_T, ST_ROWS, LANES), lambda i: (i, 0, 0))
    return pl.pallas_call(
        body, name="scan_fwd", grid=(S // SCAN_T,),
        in_specs=[pl.BlockSpec((ST_ROWS, LANES), lambda i: (0, 0)), blk], out_specs=blk,
        out_shape=SDS((S, ST_ROWS, LANES), F32),
        scratch_shapes=[pltpu.VMEM((ST_ROWS, LANES), F32)],
        compiler_params=_cparams(("arbitrary",)),
    )(lam, bu)


def _scan_bwd(lam, dst, states):
    n_blk = S // SCAN_T

    def body(lam_ref, d_ref, st_ref, g_ref, dlam_ref, carry):
        @pl.when(pl.program_id(0) == 0)
        def _():
            carry[...] = jnp.zeros_like(carry)
            dlam_ref[...] = jnp.zeros_like(dlam_ref)

        ar, ai = lam_ref[0:HALF, :], lam_ref[HALF:ST_ROWS, :]

        def step(kk, c):
            t = SCAN_T - 1 - kk
            gr, gi, dar, dai = c
            x = st_ref[t]
            xr, xi = x[0:HALF], x[HALF:ST_ROWS]
            dar = dar + gr * xr + gi * xi
            dai = dai + gi * xr - gr * xi
            d = d_ref[t]
            ngr = d[0:HALF] + ar * gr + ai * gi
            ngi = d[HALF:ST_ROWS] + ar * gi - ai * gr
            g_ref[t, 0:HALF, :] = ngr
            g_ref[t, HALF:ST_ROWS, :] = ngi
            return ngr, ngi, dar, dai

        zero = jnp.zeros((HALF, LANES), F32)
        gr, gi, dar, dai = lax.fori_loop(0, SCAN_T, step, (carry[0:HALF, :], carry[HALF:ST_ROWS, :], zero, zero),
                                         unroll=8)
        carry[0:HALF, :] = gr
        carry[HALF:ST_ROWS, :] = gi
        dlam_ref[0:HALF, :] += dar
        dlam_ref[HALF:ST_ROWS, :] += dai

    blk = pl.BlockSpec((SCAN_T, ST_ROWS, LANES), lambda i: (n_blk - 1 - i, 0, 0))
    small = pl.BlockSpec((ST_ROWS, LANES), lambda i: (0, 0))
    return pl.pallas_call(
        body, name="scan_bwd", grid=(n_blk,),
        in_specs=[small, blk, blk], out_specs=[blk, small],
        out_shape=[SDS((S, ST_ROWS, LANES), F32), SDS((ST_ROWS, LANES), F32)],
        scratch_shapes=[pltpu.VMEM((ST_ROWS, LANES), F32)],
        compiler_params=_cparams(("arbitrary",)),
    )(lam, dst, states)


def _ssm_prep(a_re, a_im, log_dt, b_re, b_im, c_re, c_im):
    lam = lax.complex(a_re, a_im)
    dt = jnp.exp(log_dt)[:, None]
    lam_bar = jnp.exp(lam * dt)
    b_bar = ((lam_bar - 1.0) / lam)[..., None] * lax.complex(b_re, b_im)
    lam_t = jnp.concatenate([jnp.real(lam_bar).reshape(HALF, LANES), jnp.imag(lam_bar).reshape(HALF, LANES)], axis=0)
    groups_per_super = SSM_GROUPS // SSM_SUPER
    on_diag = ((lax.broadcasted_iota(jnp.int32, (SSM_W, SB_COLS), 0) // SSM_GROUP) % groups_per_super
               == lax.broadcasted_iota(jnp.int32, (SSM_W, SB_COLS), 1) // SSM_STATE)

    def compact(m):
        return jnp.where(on_diag, jnp.tile(m.reshape(SSM_W, SSM_STATE), (1, groups_per_super)), 0.0)

    w_b = jnp.concatenate([compact(jnp.real(b_bar).transpose(0, 2, 1)),
                           compact(jnp.imag(b_bar).transpose(0, 2, 1))], axis=1)
    w_ct = jnp.concatenate([compact(c_re), -compact(c_im)], axis=1)
    return lam_t, w_b, w_ct


SSM_SUPER = 4
SB_ROWS = SSM_W // SSM_SUPER
SB_COLS = N_CPLX // SSM_SUPER


def _bdmm(a, w, mode, out_dtype, a_cols=None):
    a_off = 0 if a_cols is None else a_cols[0] // SB_ROWS
    if mode == "nn":
        def body(a_ref, w_ref, o_ref):
            o_ref[...] = lax.dot_general(a_ref[...].astype(BF16), w_ref[...].astype(BF16), NN,
                                         preferred_element_type=F32).astype(o_ref.dtype)

        return pl.pallas_call(
            body, name="bdmm_nn", grid=(2, SSM_SUPER),
            in_specs=[pl.BlockSpec((S, SB_ROWS), lambda h, b: (0, a_off + b)),
                      pl.BlockSpec((SB_ROWS, SB_COLS), lambda h, b: (b, h))],
            out_specs=pl.BlockSpec((S, SB_COLS), lambda h, b: (0, h * SSM_SUPER + b)),
            out_shape=SDS((S, 2 * N_CPLX), out_dtype),
            compiler_params=_cparams(("parallel", "parallel")),
        )(a, w)
    if mode == "nt":
        def body(re_ref, im_ref, wre_ref, wim_ref, o_ref):
            acc = lax.dot_general(re_ref[...].astype(BF16), wre_ref[...].astype(BF16), NT, preferred_element_type=F32)
            acc += lax.dot_general(im_ref[...].astype(BF16), wim_ref[...].astype(BF16), NT, preferred_element_type=F32)
            o_ref[...] = acc.astype(o_ref.dtype)

        return pl.pallas_call(
            body, name="bdmm_nt", grid=(SSM_SUPER,),
            in_specs=[pl.BlockSpec((S, SB_COLS), lambda b: (0, b)),
                      pl.BlockSpec((S, SB_COLS), lambda b: (0, SSM_SUPER + b)),
                      pl.BlockSpec((SB_ROWS, SB_COLS), lambda b: (b, 0)),
                      pl.BlockSpec((SB_ROWS, SB_COLS), lambda b: (b, 1))],
            out_specs=pl.BlockSpec((S, SB_ROWS), lambda b: (0, b)),
            out_shape=SDS((S, SSM_W), out_dtype),
            compiler_params=_cparams(("parallel",)),
        )(a, a, w, w)

    def body_tn(a_ref, w_ref, o_ref):
        o_ref[...] = lax.dot_general(a_ref[...].astype(BF16), w_ref[...].astype(BF16), TN,
                                     preferred_element_type=F32).astype(o_ref.dtype)

    return pl.pallas_call(
        body_tn, name="bdmm_tn", grid=(2, SSM_SUPER),
        in_specs=[pl.BlockSpec((S, SB_ROWS), lambda h, b: (0, a_off + b)),
                  pl.BlockSpec((S, SB_COLS), lambda h, b: (0, h * SSM_SUPER + b))],
        out_specs=pl.BlockSpec((SB_ROWS, SB_COLS), lambda h, b: (b, h)),
        out_shape=SDS((SSM_W, 2 * SB_COLS), out_dtype),
        compiler_params=_cparams(("parallel", "parallel")),
    )(a, w)


U_SSM_COLS = (4 * ATT_W, SSM_W)


def _row(v):
    return v.reshape(1, -1)


def _even_fwd(x, pre, post, w_in, late_w, glu_b, ssm_d, prep, tables):
    lam_t, w_b, w_ct = prep
    h = _norm_fwd(x, pre)
    proj = _mm(h, w_in, "nn", F32, b_blocks=True)
    qkv = _qkv_prep(proj, tables[:3])
    w_out, glu_w = late_w(qkv)
    att, lse = _flash_fwd(qkv, tables[3])
    bu = _bdmm(proj, w_b, "nn", F32, a_cols=U_SSM_COLS)
    states = _scan_fwd(lam_t, bu.reshape(S, ST_ROWS, LANES))
    y = _bdmm(states.reshape(S, 2 * N_CPLX), w_ct, "nt", F32)

    def act1(yv, uv, dv):
        return (_gelu_and_grad(yv + dv * uv)[0],), ()

    (z1,) = _rowwise(act1, [(y, SSM_W, 0), (proj, SSM_W, 8), (ssm_d, SSM_W, 0)], [(SSM_W, F32)], name="ssm_act_fwd")
    lin = _mm(z1, glu_w, "nn", F32)

    def gate(att_v, ga, gs, z1v, linv, bv):
        ssm_out = z1v * _sigmoid(linv + bv)
        return (jnp.concatenate([att_v * _silu_and_grad(ga)[0], ssm_out * _silu_and_grad(gs)[0]], axis=1),), ()

    (merged,) = _rowwise(gate, [(att, ATT_W, 0), (proj, ATT_W, 3), (proj, SSM_W, 9), (z1, SSM_W, 0),
                                (lin, SSM_W, 0), (glu_b, SSM_W, 0)], [(EVEN_OUT, BF16)], name="even_gate_fwd")
    yout = _mm(merged, w_out, "nn", F32)
    x_next = _post_fwd(x, yout, post)
    saved = (x, h, proj, qkv, att, lse, states, y, z1, lin, merged, yout, w_out, glu_w)
    return x_next, saved


def _even_bwd(g, saved, pre, post, w_in, late_w, glu_b, ssm_d, prep, tables, on_w, on_ssm):
    x, h, proj, qkv, att, lse, states, y, z1, lin, merged, yout, w_out, glu_w = saved
    lam_t, w_b, w_ct = prep
    dyout, dpost = _post_bwd(g, yout, post)
    dmerged = _mm(dyout, w_out, "nt", F32)
    dw_out = _mm(merged, dyout, "tn", BF16)

    def gate_bwd(dm_a, dm_s, att_v, ga, gs, z1v, linv, bv):
        sa, dsa = _silu_and_grad(ga)
        ss, dss = _silu_and_grad(gs)
        sig = _sigmoid(linv + bv)
        ssm_out = z1v * sig
        dssm = dm_s * ss
        dlin = dssm * z1v * sig * (1.0 - sig)
        return (dm_a * sa, dm_a * att_v * dsa, dm_s * ssm_out * dss, dssm * sig, dlin), (dlin,)

    datt, dg_att, dg_ssm, dz1a, dlin, dglu_b = _rowwise(
        gate_bwd, [(dmerged, ATT_W, 0), (dmerged, SSM_W, 2), (att, ATT_W, 0), (proj, ATT_W, 3), (proj, SSM_W, 9),
                   (z1, SSM_W, 0), (lin, SSM_W, 0), (glu_b, SSM_W, 0)],
        [(ATT_W, F32), (ATT_W, BF16), (SSM_W, BF16), (SSM_W, F32), (SSM_W, BF16)], [SSM_W], name="even_gate_bwd")
    dz1b = _mm(dlin, glu_w, "nt", F32)
    dglu_w = _mm(z1, dlin, "tn", BF16)

    def act1_bwd(da, db, yv, uv, dv):
        dpre = (da + db) * _gelu_and_grad(yv + dv * uv)[1]
        return (dpre, dpre * dv), (dpre * uv,)

    dy, du_direct, dd = _rowwise(act1_bwd, [(dz1a, SSM_W, 0), (dz1b, SSM_W, 0), (y, SSM_W, 0), (proj, SSM_W, 8),
                                            (ssm_d, SSM_W, 0)], [(SSM_W, BF16), (SSM_W, F32)], [SSM_W],
                                 name="ssm_act_bwd")
    dst = _bdmm(dy, w_ct, "nn", F32)
    dw_ct = _bdmm(dy, states.reshape(S, 2 * N_CPLX), "tn", F32)
    gst, dlam = _scan_bwd(lam_t, dst.reshape(S, ST_ROWS, LANES), states)
    dbu = gst.reshape(S, 2 * N_CPLX)
    du_state = _bdmm(dbu, w_b, "nt", F32)
    dw_b = _bdmm(proj, dbu, "tn", F32, a_cols=U_SSM_COLS)
    sent_ssm = on_ssm((dlam, dw_b, dw_ct))
    dq, dk, dv = _flash_bwd(qkv, att, datt, lse, tables[3], after=() if sent_ssm is None else (sent_ssm,))

    def assemble(dqv, dkv, dvv, dga, dua, dub, dgs, c, l, h):
        rot = _rotate(jnp.concatenate([dqv, dkv], axis=1), c, l, h, True)
        return (jnp.concatenate([(rot[:, :ATT_W] * HEAD_DIM ** -0.5).astype(BF16), rot[:, ATT_W:].astype(BF16),
                                 dvv.astype(BF16), dga, (dua + dub).astype(BF16), dgs], axis=1),), ()

    (dproj,) = _rowwise(assemble, [(dq, ATT_W, 0), (dk, ATT_W, 0), (dv, ATT_W, 0), (dg_att, ATT_W, 0),
                                   (du_state, SSM_W, 0), (du_direct, SSM_W, 0), (dg_ssm, SSM_W, 0),
                                   (tables[0], LANES, 0), (tables[1], LANES, 0), (tables[2], LANES, 0)],
                        [(EVEN_IN, BF16)], name="dproj_assemble")
    dw_in = _mm(h, dproj, "tn", BF16, out_blocks=True)
    sent = on_w(dict(w_in=dw_in, w_out=dw_out, glu_w=dglu_w))
    dh = _mm(dproj, w_in, "nt", F32, b_blocks=True, after=(sent,))
    g_prev, dpre = _pre_bwd(g, dh, x, pre)
    return g_prev, dict(pre=dpre, post=dpost, glu_b=dglu_b, ssm_d=dd)


def _odd_fwd(x, pre, post, w_in, pool_w, pool_scale, w_out):
    h = _norm_fwd(x, pre)
    proj = _mm(h, w_in, "nn", F32, b_blocks=True)
    mixed = _pool(proj, 0, False, BF16)
    ylin = _gmm(mixed, pool_w, "nn", F32)

    def gate(yl, gt, sc):
        return (yl * sc * _silu_and_grad(gt)[0],), ()

    (z,) = _rowwise(gate, [(ylin, POOL_W, 0), (proj, POOL_W, 1), (pool_scale, POOL_W, 0)], [(POOL_W, BF16)],
                    name="odd_gate_fwd")
    yout = _mm(z, w_out, "nn", F32)
    x_next = _post_fwd(x, yout, post)
    return x_next, (x, h, proj, mixed, ylin, z, yout)


def _odd_bwd(g, saved, pre, post, w_in, pool_w, pool_scale, w_out, on_w):
    x, h, proj, mixed, ylin, z, yout = saved
    dyout, dpost = _post_bwd(g, yout, post)
    dz = _mm(dyout, w_out, "nt", F32)
    dw_out = _mm(z, dyout, "tn", BF16)

    def gate_bwd(dzv, yl, gt, sc):
        sg, dsg = _silu_and_grad(gt)
        tt = dzv * sg
        return (tt * sc, dzv * yl * sc * dsg), (tt * yl,)

    dylin, dproj_gate, dscale = _rowwise(gate_bwd, [(dz, POOL_W, 0), (ylin, POOL_W, 0), (proj, POOL_W, 1),
                                                    (pool_scale, POOL_W, 0)],
                                         [(POOL_W, BF16), (POOL_W, BF16, ODD_IN, 1)], [POOL_W], name="odd_gate_bwd")
    dmixed = _gmm(dylin, pool_w, "nt", F32)
    dpool_w = _gmm(mixed, dylin, "tn", BF16)
    dproj = _pool(dmixed, 0, True, BF16, into=dproj_gate)
    dw_in = _mm(h, dproj, "tn", BF16, out_blocks=True)
    sent = on_w(dict(w_in=dw_in, w_out=dw_out, pool_w=dpool_w))
    dh = _mm(dproj, w_in, "nt", F32, b_blocks=True, after=(sent,))
    g_prev, dpre = _pre_bwd(g, dh, x, pre)
    return g_prev, dict(pre=dpre, post=dpost, pool_scale=dscale)


def _my_index():
    return 4 * lax.axis_index("x") + 2 * lax.axis_index("y") + lax.axis_index("c")


HBM_SPEC = pl.BlockSpec(memory_space=pltpu.HBM)
SEM_SPEC = pl.BlockSpec(memory_space=pltpu.SEMAPHORE)
SPLIT_EFFECT = pltpu.SideEffectType.DATAFLOW_SIDE_EFFECTING


def _device_of(j):
    return (j // 4, (j // 2) % 2, j % 2)


def _split_copy(srcs, lands, send_sems, recv_sems, gather, i, j, dst_slot, recv_slot):
    return pltpu.make_async_remote_copy(
        src_ref=srcs[i] if gather else srcs[i].at[j], dst_ref=lands[i].at[dst_slot],
        send_sem=send_sems.at[i * N_DEV + j], recv_sem=recv_sems.at[i * N_DEV + recv_slot],
        device_id=_device_of(j), device_id_type=MESH_ID)


def _own_copy(srcs, lands, send_sems, gather, i, me):
    return pltpu.make_async_copy(srcs[i] if gather else srcs[i].at[me], lands[i].at[me], send_sems.at[i * N_DEV + me])


def _xchg_start(name, srcs, gather, after=()):
    n = len(srcs)
    n_in = n + len(after)

    def body(*refs):
        src_refs = refs[:n]
        send_sems, recv_sems, token = refs[n_in], refs[n_in + 1], refs[-1]
        land_refs = refs[n_in + 2 + n:n_in + 2 + 2 * n]
        me = _my_index()
        for j in range(N_DEV):
            @pl.when(me != j)
            def _(j=j):
                for i in range(n):
                    _split_copy(src_refs, land_refs, send_sems, recv_sems, gather, i, j, me, me).start()
        for i in range(n):
            _own_copy(src_refs, land_refs, send_sems, gather, i, me).start()
        token[...] = jnp.zeros_like(token)

    land_shapes = [((N_DEV,) + a.shape) if gather else a.shape for a in srcs]
    thru = ([pltpu.HBM(a.shape, a.dtype) for a in srcs] + [pltpu.HBM(s, a.dtype) for s, a in zip(land_shapes, srcs)])
    res = pl.pallas_call(
        body, name=name,
        out_shape=(pltpu.SemaphoreType.DMA((n * N_DEV,)), pltpu.SemaphoreType.DMA((n * N_DEV,)), *thru,
                   SDS((8, LANES), F32)),
        in_specs=[HBM_SPEC] * n + [pl.BlockSpec(memory_space=pl.ANY)] * len(after),
        out_specs=(SEM_SPEC, SEM_SPEC, *([HBM_SPEC] * (2 * n)), pl.BlockSpec(memory_space=pltpu.VMEM)),
        input_output_aliases={i: 2 + i for i in range(n)},
        compiler_params=pltpu.CompilerParams(has_side_effects=SPLIT_EFFECT),
    )(*[pltpu.with_memory_space_constraint(a, pltpu.HBM) for a in srcs], *after)
    return res[0], res[1], list(res[2:2 + n]), list(res[2 + n:2 + 2 * n]), res[-1]


def _xchg_wait(name, started, gather, after):
    send_sems, recv_sems, srcs, lands, _ = started
    n = len(srcs)

    def body(*refs):
        src_refs, land_refs = refs[:n], refs[n:2 * n]
        send_r, recv_r = refs[2 * n], refs[2 * n + 1]
        me = _my_index()
        for j in range(N_DEV):
            @pl.when(me != j)
            def _(j=j):
                for i in range(n):
                    _split_copy(src_refs, land_refs, send_r, recv_r, gather, i, j, me, me).wait_send()
                    _split_copy(src_refs, land_refs, send_r, recv_r, gather, i, j, j, j).wait_recv()
        for i in range(n):
            _own_copy(src_refs, land_refs, send_r, gather, i, me).wait()

    thru = [pltpu.HBM(a.shape, a.dtype) for a in list(srcs) + list(lands)]
    res = pl.pallas_call(
        body, name=name, out_shape=tuple(thru),
        in_specs=[HBM_SPEC] * (2 * n) + [SEM_SPEC, SEM_SPEC] + [pl.BlockSpec(memory_space=pl.ANY)] * len(after),
        out_specs=tuple([HBM_SPEC] * (2 * n)),
        input_output_aliases={i: i for i in range(2 * n)},
        compiler_params=pltpu.CompilerParams(has_side_effects=SPLIT_EFFECT),
    )(*srcs, *lands, send_sems, recv_sems, *after)
    return list(res[n:])


def _adam_layers(w, slot_list, m, v, name):
    n_l, r, c = w.shape
    ns = slot_list[0].shape[0]
    tr = r
    while tr * c * 4 > (1 << 20) and tr % 16 == 0:
        tr //= 2
    assert r % tr == 0 and len(slot_list) == n_l

    def body(*refs):
        w_ref, slot_refs = refs[0], refs[1:1 + n_l]
        m_ref, v_ref, go_ref, d_ref, mo_ref, vo_ref = refs[1 + n_l:]
        layer = pl.program_id(0)
        g = None
        for l, g_ref in enumerate(slot_refs):
            gl = g_ref[0].astype(F32)
            for s in range(1, ns):
                gl = gl + g_ref[s].astype(F32)
            g = gl if g is None else jnp.where(layer == l, gl, g)
        mn = ADAM_B1 * m_ref[...] + (1.0 - ADAM_B1) * g
        vn = ADAM_B2 * v_ref[...] + (1.0 - ADAM_B2) * (g * g)
        m_hat = mn / (1.0 - ADAM_B1 ** ADAM_STEP)
        v_hat = vn / (1.0 - ADAM_B2 ** ADAM_STEP)
        go_ref[...] = g
        d_ref[...] = -ADAM_LR * (m_hat / (jnp.sqrt(v_hat) + ADAM_EPS) + ADAM_WD * w_ref[...])
        mo_ref[...] = mn
        vo_ref[...] = vn

    blk = pl.BlockSpec((None, tr, c), lambda l, i: (l, i, 0))
    slot_specs = [pl.BlockSpec((ns, tr, c), lambda l, i, k=k: (0, jnp.where(l == k, i, 0), 0)) for k in range(n_l)]
    return pl.pallas_call(
        body, name=name, grid=(n_l, r // tr),
        in_specs=[blk] + slot_specs + [blk, blk],
        out_specs=[blk] * 4, out_shape=[SDS((n_l, r, c), F32)] * 4,
        compiler_params=_cparams(("arbitrary", "arbitrary")),
    )(w, *slot_list, m, v)


def _adam(w, gslots, m, v, name):
    r, c = w.shape
    ns = gslots.shape[0]
    tr = r
    while tr * c * 4 > (1 << 20) and tr % 16 == 0:
        tr //= 2
    assert r % tr == 0

    def body(w_ref, g_ref, m_ref, v_ref, go_ref, d_ref, mo_ref, vo_ref):
        g = g_ref[0].astype(F32)
        for s in range(1, ns):
            g = g + g_ref[s].astype(F32)
        wv = w_ref[...]
        mn = ADAM_B1 * m_ref[...] + (1.0 - ADAM_B1) * g
        vn = ADAM_B2 * v_ref[...] + (1.0 - ADAM_B2) * (g * g)
        m_hat = mn / (1.0 - ADAM_B1 ** ADAM_STEP)
        v_hat = vn / (1.0 - ADAM_B2 ** ADAM_STEP)
        go_ref[...] = g
        d_ref[...] = -ADAM_LR * (m_hat / (jnp.sqrt(v_hat) + ADAM_EPS) + ADAM_WD * wv)
        mo_ref[...] = mn
        vo_ref[...] = vn

    blk = pl.BlockSpec((tr, c), lambda i: (i, 0))
    return pl.pallas_call(
        body, name=name, grid=(r // tr,),
        in_specs=[blk, pl.BlockSpec((ns, tr, c), lambda i: (0, i, 0)), blk, blk],
        out_specs=[blk] * 4, out_shape=[SDS((r, c), F32)] * 4,
        compiler_params=_cparams(("parallel",)),
    )(w, gslots, m, v)


def _sum_slots(slots, name):
    ns, r, c = slots.shape

    def body(g_ref, o_ref):
        g = g_ref[0]
        for s in range(1, ns):
            g = g + g_ref[s]
        o_ref[...] = g

    return pl.pallas_call(
        body, name=name, grid=(1,),
        in_specs=[pl.BlockSpec((ns, r, c), lambda i: (0, 0, 0))], out_specs=pl.BlockSpec((r, c), lambda i: (0, 0)),
        out_shape=SDS((r, c), F32), compiler_params=_cparams(("arbitrary",)),
    )(slots)


def _adam_params(params, name):
    n = len(params)

    def body(*refs):
        ins, outs = refs[:5 * n], refs[5 * n:]
        for p in range(n):
            w_ref, m_ref, v_ref, g_first, g_rest = ins[5 * p:5 * p + 5]
            go_ref, d_ref, mo_ref, vo_ref = outs[4 * p:4 * p + 4]
            for part, g_ref in ((slice(0, 1), g_first), (slice(1, w_ref.shape[0]), g_rest)):
                g = g_ref[...]
                mn = ADAM_B1 * m_ref[part] + (1.0 - ADAM_B1) * g
                vn = ADAM_B2 * v_ref[part] + (1.0 - ADAM_B2) * (g * g)
                m_hat = mn / (1.0 - ADAM_B1 ** ADAM_STEP)
                v_hat = vn / (1.0 - ADAM_B2 ** ADAM_STEP)
                go_ref[part] = g
                d_ref[part] = -ADAM_LR * (m_hat / (jnp.sqrt(v_hat) + ADAM_EPS) + ADAM_WD * w_ref[part])
                mo_ref[part] = mn
                vo_ref[part] = vn

    def whole(a):
        return pl.BlockSpec(a.shape, lambda i, nd=a.ndim: (0,) * nd)

    flat = [pltpu.with_memory_space_constraint(a, pltpu.HBM) for prm in params for a in prm]
    outs = pl.pallas_call(
        body, name=name, grid=(1,),
        in_specs=[whole(a) for a in flat],
        out_specs=[whole(prm[0]) for prm in params for _ in range(4)],
        out_shape=[SDS(prm[0].shape, F32) for prm in params for _ in range(4)],
        compiler_params=_cparams(("arbitrary",)),
    )(*flat)
    return [outs[4 * p:4 * p + 4] for p in range(n)]


SMALL_NAMES = ("pre_norm", "post_norm", "ssm_a_re", "ssm_a_im", "ssm_log_dt", "ssm_b_re", "ssm_b_im", "ssm_c_re",
               "ssm_c_im", "ssm_d", "ssm_glu_b")
SSM_NAMES = ("ssm_a_re", "ssm_a_im", "ssm_log_dt", "ssm_b_re", "ssm_b_im", "ssm_c_re", "ssm_c_im")
SHARDED_NAMES = ("even_w_in", "even_w_out", "ssm_glu_w", "odd_w_in", "pool_w", "odd_w_out")
WEIGHT_ORDER = ("pre_norm", "post_norm", "even_w_in", "even_w_out", "ssm_a_re", "ssm_a_im", "ssm_log_dt", "ssm_b_re",
                "ssm_b_im", "ssm_c_re", "ssm_c_im", "ssm_d", "ssm_glu_w", "ssm_glu_b", "odd_w_in", "pool_w",
                "pool_scale", "odd_w_out")
PACK_ROWS_ALIGN = 8


def _pack(parts):
    flat = jnp.concatenate([p.reshape(-1).astype(F32) for p in parts])
    rows = -(-flat.shape[0] // (LANES * PACK_ROWS_ALIGN)) * PACK_ROWS_ALIGN
    return jnp.pad(flat, (0, rows * LANES - flat.shape[0])).reshape(rows, LANES)


def _unpack(packed, shapes):
    flat = packed.reshape(-1)
    out, off = [], 0
    for shp in shapes:
        size = math.prod(shp)
        out.append(flat[off:off + size].reshape(shp))
        off += size
    return out


EVEN_SHARDED = ("w_in", "w_out", "glu_w")
ODD_SHARDED = ("w_in", "pool_w", "w_out")
FAMILY = {(0, "w_in"): "even_w_in", (0, "w_out"): "even_w_out", (0, "glu_w"): "ssm_glu_w",
          (1, "w_in"): "odd_w_in", (1, "pool_w"): "pool_w", (1, "w_out"): "odd_w_out"}


def _sharded_keys(layer):
    return EVEN_SHARDED if layer % 2 == 0 else ODD_SHARDED


def _local_step(x, tgt, small, get_weights, on_w, on_ssm, on_grads, zero=0.0):
    tables = _rope_tables(zero) + (_attention_bias(zero),)
    preps, prep_vjps = [], []
    for i in range(2):
        out, vjp = jax.vjp(_ssm_prep, small["ssm_a_re"][i] + zero, small["ssm_a_im"][i], small["ssm_log_dt"][i],
                           small["ssm_b_re"][i], small["ssm_b_im"][i], small["ssm_c_re"][i], small["ssm_c_im"][i])
        preps.append(out)
        prep_vjps.append(vjp)

    def layer_args(layer, wts):
        i = layer // 2
        pre, post = _row(small["pre_norm"][layer]) + wts.get("token", 0.0), _row(small["post_norm"][layer])
        if layer % 2 == 0:
            return (pre, post, wts["w_in"], wts["late"], _row(small["ssm_glu_b"][i]), _row(small["ssm_d"][i]),
                    preps[i], tables)
        return (pre, post, wts["w_in"], wts["pool_w"], _row(wts["pool_scale"]), wts["w_out"])

    saved, args = [], []
    cur = x
    for layer in range(4):
        after = (cur,) if layer else (cur, tables[0], tables[3], preps[0][1], preps[0][2], preps[1][1], preps[1][2])
        args.append(layer_args(layer, get_weights(layer, after)))
        cur, sv = (_even_fwd if layer % 2 == 0 else _odd_fwd)(cur, *args[layer])
        saved.append(sv)
    g, sq = _loss_grad(cur, tgt)
    loss = 0.5 * jnp.sum(sq) / D

    lg = [None] * 4
    token = jnp.zeros((), F32)
    for layer in reversed(range(4)):
        largs = list(args[layer])
        largs[1] = largs[1] + token
        hooks = dict(on_w=functools.partial(on_w, layer))
        ssm_grads = []
        if layer % 2 == 0:
            def ssm_hook(cotangents, layer=layer):
                ssm_grads.append(prep_vjps[layer // 2](cotangents))
                return on_ssm(layer, ssm_grads[0])

            hooks["on_ssm"] = ssm_hook
        g, lg[layer] = (_even_bwd if layer % 2 == 0 else _odd_bwd)(g, saved[layer], *largs, **hooks)
        if ssm_grads:
            lg[layer]["ssm"] = ssm_grads[0]
        token = on_grads(layer, lg[layer])
    return loss, g, token


def _to_slots(key, gfull):
    if key == "w_in":
        return gfull
    if key in ("w_out", "glu_w"):
        rr, nn = gfull.shape
        return gfull.reshape(N_DEV, rr // N_DEV, nn)
    assert key == "pool_w"
    gg, rr, nn = gfull.shape
    return gfull.reshape(gg, N_DEV, rr // N_DEV, nn).transpose(1, 0, 2, 3)


def _from_gathered(key, gat):
    if key == "w_in":
        return gat
    if key in ("w_out", "glu_w"):
        _, rr, nn = gat.shape
        return gat.reshape(N_DEV * rr, nn)
    assert key == "pool_w"
    _, gg, rr, nn = gat.shape
    return gat.transpose(1, 0, 2, 3).reshape(gg, N_DEV * rr, nn)


def kernel(x, pre_norm, post_norm, even_w_in, even_w_out, ssm_a_re, ssm_a_im, ssm_log_dt, ssm_b_re, ssm_b_im, ssm_c_re, ssm_c_im, ssm_d, ssm_glu_w, ssm_glu_b, odd_w_in, pool_w, pool_scale, odd_w_out, loss_target, m_pre_norm, m_post_norm, m_even_w_in, m_even_w_out, m_ssm_a_re, m_ssm_a_im, m_ssm_log_dt, m_ssm_b_re, m_ssm_b_im, m_ssm_c_re, m_ssm_c_im, m_ssm_d, m_ssm_glu_w, m_ssm_glu_b, m_odd_w_in, m_pool_w, m_pool_scale, m_odd_w_out, v_pre_norm, v_post_norm, v_even_w_in, v_even_w_out, v_ssm_a_re, v_ssm_a_im, v_ssm_log_dt, v_ssm_b_re, v_ssm_b_im, v_ssm_c_re, v_ssm_c_im, v_ssm_d, v_ssm_glu_w, v_ssm_glu_b, v_odd_w_in, v_pool_w, v_pool_scale, v_odd_w_out):
    w = dict(pre_norm=pre_norm, post_norm=post_norm, even_w_in=even_w_in, even_w_out=even_w_out, ssm_a_re=ssm_a_re,
             ssm_a_im=ssm_a_im, ssm_log_dt=ssm_log_dt, ssm_b_re=ssm_b_re, ssm_b_im=ssm_b_im, ssm_c_re=ssm_c_re,
             ssm_c_im=ssm_c_im, ssm_d=ssm_d, ssm_glu_w=ssm_glu_w, ssm_glu_b=ssm_glu_b, odd_w_in=odd_w_in,
             pool_w=pool_w, pool_scale=pool_scale, odd_w_out=odd_w_out)
    mom = dict(pre_norm=m_pre_norm, post_norm=m_post_norm, even_w_in=m_even_w_in, even_w_out=m_even_w_out,
               ssm_a_re=m_ssm_a_re, ssm_a_im=m_ssm_a_im, ssm_log_dt=m_ssm_log_dt, ssm_b_re=m_ssm_b_re,
               ssm_b_im=m_ssm_b_im, ssm_c_re=m_ssm_c_re, ssm_c_im=m_ssm_c_im, ssm_d=m_ssm_d, ssm_glu_w=m_ssm_glu_w,
               ssm_glu_b=m_ssm_glu_b, odd_w_in=m_odd_w_in, pool_w=m_pool_w, pool_scale=m_pool_scale,
               odd_w_out=m_odd_w_out)
    var = dict(pre_norm=v_pre_norm, post_norm=v_post_norm, even_w_in=v_even_w_in, even_w_out=v_even_w_out,
               ssm_a_re=v_ssm_a_re, ssm_a_im=v_ssm_a_im, ssm_log_dt=v_ssm_log_dt, ssm_b_re=v_ssm_b_re,
               ssm_b_im=v_ssm_b_im, ssm_c_re=v_ssm_c_re, ssm_c_im=v_ssm_c_im, ssm_d=v_ssm_d, ssm_glu_w=v_ssm_glu_w,
               ssm_glu_b=v_ssm_glu_b, odd_w_in=v_odd_w_in, pool_w=v_pool_w, pool_scale=v_pool_scale,
               odd_w_out=v_odd_w_out)
    me = _my_index()
    scale_cols = pool_scale.shape[1]

    def start_gather(tag, layer, keys, after=()):
        i = layer // 2
        shards = [w[FAMILY[(layer % 2, k)]][i].astype(BF16) for k in keys]
        if layer % 2 == 1:
            shards.append(jnp.pad(pool_scale[i][None], ((0, PACK_ROWS_ALIGN - 1), (0, 0))))
        return _xchg_start(f"gather_start_{tag}", shards, True, after)

    gather_started = {0: start_gather("0", 0, EVEN_SHARDED[:1])}
    small = {nm: w[nm] for nm in SMALL_NAMES}

    def get_weights(layer, after):
        keys = EVEN_SHARDED[:1] if layer == 0 else _sharded_keys(layer)
        lands = _xchg_wait(f"gather_wait_{layer}", gather_started[layer], True, after)
        wts = {k: _from_gathered(k, gat) for k, gat in zip(keys, lands)}
        if layer % 2 == 1:
            wts["pool_scale"] = lands[-1][:, 0, :].reshape(N_DEV * scale_cols)
        if layer == 0:
            gather_started["0_late"] = start_gather("0_late", 0, EVEN_SHARDED[1:], after=(lands[0],))
            for later in (1, 2, 3):
                gather_started[later] = start_gather(str(later), later, _sharded_keys(later), after=(lands[0],))
            wts["token"] = sum(gather_started[tag][4][0, 0] for tag in ("0_late", 1, 2, 3))

            def late(after_late):
                late_lands = _xchg_wait("gather_wait_0_late", gather_started["0_late"], True, (after_late,))
                return tuple(_from_gathered(k, gat) for k, gat in zip(EVEN_SHARDED[1:], late_lands))

            wts["late"] = late
        elif layer == 2:
            wts["late"] = lambda after_late: (wts["w_out"], wts["glu_w"])
        return wts

    scatter_started = [None] * 4

    def on_w(layer, gw):
        slots = [_to_slots(k, gw[k]) for k in _sharded_keys(layer)]
        scatter_started[layer] = _xchg_start(f"scatter_start_{layer}", slots, False)
        return scatter_started[layer][4]

    packed_names = ("pre_norm", "post_norm") + SSM_NAMES + ("ssm_d", "ssm_glu_b")
    tails = {nm: (SSM_GROUPS, SSM_STATE * SSM_GROUP) if nm in ("ssm_b_re", "ssm_b_im") else w[nm].shape[1:]
             for nm in packed_names}

    layer_grads = {}
    early_started, mid_started = [], []

    def on_ssm(layer, ssm_grads):
        if layer != 0:
            return None
        mid_started.append(_xchg_start("mid_start", [_pack(list(ssm_grads))], True))
        return mid_started[0][4]

    def on_grads(layer, lg):
        layer_grads[layer] = lg
        zero = jnp.zeros((), F32)
        if layer == 1:
            lgs = layer_grads
            early = ([jnp.concatenate([lgs[l][k] for l in (1, 2, 3)], axis=0) for k in ("pre", "post")]
                     + list(lgs[2]["ssm"]) + [lgs[2]["ssm_d"], lgs[2]["glu_b"],
                                              jnp.concatenate([lgs[1]["pool_scale"], lgs[3]["pool_scale"]], axis=0)])
            early_started.append(_xchg_start("small_start", [_pack(early)], True))
            zero = zero + early_started[0][4][0, 0]
        return zero

    loss_local, grad_x, token = _local_step(x[0], loss_target[0], small, get_weights, on_w, on_ssm, on_grads,
                                            zero=gather_started[0][4][0, 0])

    lg0 = layer_grads[0]
    late_started = _xchg_start("late_start", [_pack([lg0["pre"], lg0["post"], lg0["ssm_d"], lg0["glu_b"],
                                                     loss_local.reshape(1)]) + token], True)

    def adam_family(parity, k):
        nm = FAMILY[(parity, k)]
        shp = w[nm].shape
        cols = shp[-1]
        slot_list = [recv[(parity + 2 * i, k)].reshape(N_DEV, -1, cols) for i in range(2)]
        outs = _adam_layers(w[nm].reshape(2, -1, cols), slot_list, mom[nm].reshape(2, -1, cols),
                            var[nm].reshape(2, -1, cols), name=f"adam_{nm}")
        return [o.reshape(shp) for o in outs]

    recv, res = {}, {}
    for layer in (3, 1):
        lands = _xchg_wait(f"scatter_wait_{layer}", scatter_started[layer], False, (late_started[4],))
        for k, land in zip(_sharded_keys(layer), lands):
            recv[(layer, k)] = land
    for k in ODD_SHARDED:
        res[FAMILY[(1, k)]] = adam_family(1, k)
    odd_done = tuple(res[FAMILY[(1, k)]][0] for k in ODD_SHARDED)

    (early_slots,) = _xchg_wait("small_wait", early_started[0], True, odd_done)
    (mid_slots,) = _xchg_wait("mid_wait", mid_started[0], True, odd_done)
    early_shapes = [(w[nm].shape[0] - 1,) + tails[nm] for nm in packed_names] + [(2, N_DEV * scale_cols)]
    g_early = _unpack(_sum_slots(early_slots, "sum_small_early"), early_shapes)
    g_mid = _unpack(_sum_slots(mid_slots, "sum_small_mid"), [(1,) + tails[nm] for nm in SSM_NAMES])

    (late_slots,) = _xchg_wait("late_wait", late_started, True, (g_early[0], g_mid[0]))
    for layer in (2, 0):
        lands = _xchg_wait(f"scatter_wait_{layer}", scatter_started[layer], False, (late_slots,))
        for k, land in zip(_sharded_keys(layer), lands):
            recv[(layer, k)] = land
    for k in EVEN_SHARDED:
        res[FAMILY[(0, k)]] = adam_family(0, k)

    late_names = ("pre_norm", "post_norm", "ssm_d", "ssm_glu_b")
    g_late = _unpack(_sum_slots(late_slots, "sum_small_late"), [(1,) + tails[nm] for nm in late_names] + [(1,)])
    g_first = dict(zip(late_names, g_late))
    g_first.update(zip(SSM_NAMES, g_mid))
    dense = lambda nm, a: a.reshape((a.shape[0],) + tails[nm])
    outs = _adam_params([(dense(nm, w[nm]), dense(nm, mom[nm]), dense(nm, var[nm]), g_first[nm], g_early[j])
                         for j, nm in enumerate(packed_names)], "adam_small")
    for nm, four in zip(packed_names, outs):
        res[nm] = [o.reshape(w[nm].shape) for o in four]
    loss = g_late[-1].reshape(())
    g_scale = lax.dynamic_slice_in_dim(g_early[-1], me * scale_cols, scale_cols, axis=1)
    pad = ((0, PACK_ROWS_ALIGN - 2), (0, 0))
    outs = _adam(jnp.pad(pool_scale, pad), jnp.pad(g_scale, pad)[None], jnp.pad(m_pool_scale, pad),
                 jnp.pad(v_pool_scale, pad), name="adam_pool_scale")
    res["pool_scale"] = [o[:2] for o in outs]

    out = [loss, grad_x[None]]
    for kind in range(4):
        out += [res[nm][kind] for nm in WEIGHT_ORDER]
    return tuple(out)
```

```python
import functools
import math

import jax
import jax.numpy as jnp
from jax import lax
from jax.experimental import pallas as pl
from jax.experimental.pallas import tpu as pltpu

F32 = jnp.float32
BF16 = jnp.bfloat16
SDS = jax.ShapeDtypeStruct

N_DEV = 8
S = 2048
D = 1024
HEAD_DIM = 64
ROT_DIM = 16
ROPE_THETA = 500000.0
ATT_W = 1024
SSM_W = 512
SSM_GROUPS = 32
SSM_GROUP = 16
SSM_STATE = 64
N_CPLX = SSM_GROUPS * SSM_STATE
POOL_W = 2048
POOL_GROUP = 512
EVEN_IN = 5120
EVEN_OUT = 1536
ODD_IN = 4096
RMS_EPS = 1e-6
LANES = 128
VMEM_LIMIT = 48 * 1024 * 1024

ADAM_LR = 0.001
ADAM_B1 = 0.9
ADAM_B2 = 0.999
ADAM_EPS = 1e-08
ADAM_WD = 0.01
ADAM_STEP = 10

MESH_ID = pl.DeviceIdType.MESH
NN = (((1,), (0,)), ((), ()))
NT = (((1,), (1,)), ((), ()))
TN = (((0,), (0,)), ((), ()))
_DN = {"nn": NN, "nt": NT, "tn": TN}


def _cparams(sem):
    return pltpu.CompilerParams(dimension_semantics=sem, vmem_limit_bytes=VMEM_LIMIT)


MM_TILES = (1024, 768, 512)


def _tile(dim):
    return next((t for t in MM_TILES if dim % t == 0), dim)


NT_BLOCKS_PER_STEP = 4


def _mm(a, b, mode, out_dtype, b_blocks=False, out_blocks=False, a_cols=None, after=()):
    if b_blocks:
        nblk, rows, cb = b.shape
        b2_shape = (rows, nblk * cb)
    else:
        b2_shape = b.shape
    a_shape = a.shape if a_cols is None else (a.shape[0], a_cols[1])
    if mode == "nn":
        (m, k), n = a_shape, b2_shape[1]
    elif mode == "nt":
        (m, k), n = a_shape, b2_shape[0]
    else:
        (k, m), n = a_shape, b2_shape[1]
    tm, tn, tk = _tile(m), _tile(n), _tile(k)
    per_step = 1
    if b_blocks and mode == "nn":
        tn = cb
    if b_blocks and mode == "nt":
        per_step = NT_BLOCKS_PER_STEP
        tk = per_step * cb
        tn = min(tn, MM_TILES[-1])
    if out_blocks:
        tn = n // N_DEV
        tk = k
    nk = k // tk
    a_unit = tm if mode == "tn" else tk
    assert a_cols is None or a_cols[0] % a_unit == 0
    a_off = 0 if a_cols is None else a_cols[0] // a_unit

    def body(a_ref, b_ref, *rest):
        o_ref, acc_ref = rest[-2:]
        kk = pl.program_id(2)
        if per_step == 1:
            part = lax.dot_general(a_ref[...].astype(BF16), b_ref[...].astype(BF16), _DN[mode],
                                   preferred_element_type=F32)
        else:
            part = None
            for blk in range(per_step):
                d = lax.dot_general(a_ref[:, blk * cb:(blk + 1) * cb].astype(BF16), b_ref[blk].astype(BF16), NT,
                                    preferred_element_type=F32)
                part = d if part is None else part + d
        if nk == 1:
            o_ref[...] = part.astype(o_ref.dtype)
            return

        @pl.when(kk == 0)
        def _():
            acc_ref[...] = part

        @pl.when((kk > 0) & (kk < nk - 1))
        def _():
            acc_ref[...] += part

        @pl.when(kk == nk - 1)
        def _():
            o_ref[...] = (acc_ref[...] + part).astype(o_ref.dtype)

    if mode == "nn":
        a_spec = pl.BlockSpec((tm, tk), lambda i, j, kk: (i, a_off + kk))
        b_spec = pl.BlockSpec((tk, tn), lambda i, j, kk: (kk, j))
    elif mode == "nt":
        a_spec = pl.BlockSpec((tm, tk), lambda i, j, kk: (i, a_off + kk))
        b_spec = pl.BlockSpec((tn, tk), lambda i, j, kk: (j, kk))
    else:
        a_spec = pl.BlockSpec((tk, tm), lambda i, j, kk: (kk, a_off + i))
        b_spec = pl.BlockSpec((tk, tn), lambda i, j, kk: (kk, j))
    if b_blocks and mode == "nn":
        b_spec = pl.BlockSpec((None, tk, cb), lambda i, j, kk: (j, kk, 0))
    if b_blocks and mode == "nt":
        b_spec = pl.BlockSpec((per_step, tn, cb), lambda i, j, kk: (kk, j, 0))
    out_spec = pl.BlockSpec((tm, tn), lambda i, j, kk: (i, j))
    out_shape = SDS((m, n), out_dtype)
    if out_blocks:
        out_spec = pl.BlockSpec((None, tm, tn), lambda i, j, kk: (j, i, 0))
        out_shape = SDS((N_DEV, m, tn), out_dtype)
    return pl.pallas_call(
        body, name=f"mm_{mode}_{m}x{k}x{n}",
        grid=(m // tm, n // tn, nk),
        in_specs=[a_spec, b_spec] + [pl.BlockSpec(memory_space=pl.ANY)] * len(after),
        out_specs=out_spec,
        out_shape=out_shape,
        scratch_shapes=[pltpu.VMEM((tm, tn) if nk > 1 else (8, LANES), F32)],
        compiler_params=_cparams(("parallel", "parallel", "arbitrary")),
    )(a, b, *after)


def _gmm(a, b, mode, out_dtype, tm=512):
    ng, gw = POOL_W // POOL_GROUP, POOL_GROUP
    ns = S // tm
    if mode in ("nn", "nt"):
        def body(a_ref, b_ref, o_ref):
            o_ref[...] = lax.dot_general(a_ref[...].astype(BF16), b_ref[...].astype(BF16), _DN[mode],
                                         preferred_element_type=F32).astype(o_ref.dtype)

        return pl.pallas_call(
            body, name=f"gmm_{mode}", grid=(ng, ns),
            in_specs=[pl.BlockSpec((tm, gw), lambda g, i: (i, g)),
                      pl.BlockSpec((None, gw, gw), lambda g, i: (g, 0, 0))],
            out_specs=pl.BlockSpec((tm, gw), lambda g, i: (i, g)),
            out_shape=SDS((S, POOL_W), out_dtype),
            compiler_params=_cparams(("parallel", "parallel")),
        )(a, b)

    def body_tn(a_ref, b_ref, o_ref, acc_ref):
        i = pl.program_id(1)

        @pl.when(i == 0)
        def _():
            acc_ref[...] = jnp.zeros_like(acc_ref)

        acc_ref[...] += lax.dot_general(a_ref[...].astype(BF16), b_ref[...].astype(BF16), TN,
                                        preferred_element_type=F32)

        @pl.when(i == ns - 1)
        def _():
            o_ref[...] = acc_ref[...].astype(o_ref.dtype)

    return pl.pallas_call(
        body_tn, name="gmm_tn", grid=(ng, ns),
        in_specs=[pl.BlockSpec((tm, gw), lambda g, i: (i, g)),
                  pl.BlockSpec((tm, gw), lambda g, i: (i, g))],
        out_specs=pl.BlockSpec((None, gw, gw), lambda g, i: (g, 0, 0)),
        out_shape=SDS((ng, gw, gw), out_dtype),
        scratch_shapes=[pltpu.VMEM((gw, gw), F32)],
        compiler_params=_cparams(("parallel", "arbitrary")),
    )(a, b)


def _rowwise(fn, inputs, out_defs, acc_defs=(), tm=256, name=None, after=()):
    n_in, n_out, n_acc = len(inputs), len(out_defs), len(acc_defs)
    n_after = len(after)
    in_specs, args = [], []
    for arr, width, cb in inputs:
        if arr.shape[0] == 1:
            in_specs.append(pl.BlockSpec((1, width), lambda i, cb=cb: (0, cb)))
        else:
            in_specs.append(pl.BlockSpec((tm, width), lambda i, cb=cb: (i, cb)))
        args.append(arr)
    out_defs = [d if len(d) == 4 else (d[0], d[1], d[0], 0) for d in out_defs]
    out_shape = [SDS((S, ww), dt) for _, dt, ww, _ in out_defs] + [SDS((1, w), F32) for w in acc_defs]
    out_specs = ([pl.BlockSpec((tm, w), lambda i, cb=cb: (i, cb)) for w, _, _, cb in out_defs]
                 + [pl.BlockSpec((1, w), lambda i: (0, 0)) for w in acc_defs])

    def kern(*refs):
        vals = [r[...] for r in refs[:n_in]]
        outs, accs = fn(*vals)
        out_refs = refs[n_in + n_after:]
        for r, v in zip(out_refs[:n_out], outs):
            r[...] = v.astype(r.dtype)
        if n_acc:
            acc_refs = out_refs[n_out:]

            @pl.when(pl.program_id(0) == 0)
            def _():
                for r in acc_refs:
                    r[...] = jnp.zeros_like(r)

            for r, v in zip(acc_refs, accs):
                r[...] += jnp.sum(v, axis=0, keepdims=True)

    res = pl.pallas_call(
        kern, name=name, grid=(S // tm,), in_specs=in_specs + [pl.BlockSpec(memory_space=pl.ANY)] * n_after,
        out_specs=out_specs, out_shape=out_shape, compiler_params=_cparams(("arbitrary",)),
    )(*args, *after)
    return res


def _sigmoid(x):
    return 1.0 / (1.0 + jnp.exp(-x))


def _silu_and_grad(x):
    s = _sigmoid(x)
    return x * s, s * (1.0 + x * (1.0 - s))


_GELU_K = math.sqrt(2.0 / math.pi)
_GELU_C = 0.044715


def _gelu_and_grad(x):
    t = jnp.tanh(_GELU_K * (x + _GELU_C * (x * x * x)))
    cdf = 0.5 * (1.0 + t)
    grad = cdf + 0.5 * x * (1.0 - t * t) * (_GELU_K * (1.0 + 3.0 * _GELU_C * x * x))
    return x * cdf, grad


def _rms(xv, gain):
    r = lax.rsqrt(jnp.mean(xv * xv, axis=-1, keepdims=True) + RMS_EPS)
    return xv * r * gain


def _rms_bwd(dout, xv, gain):
    r = lax.rsqrt(jnp.mean(xv * xv, axis=-1, keepdims=True) + RMS_EPS)
    xhat = xv * r
    dxhat = dout * gain
    dx = r * (dxhat - xhat * jnp.mean(dxhat * xhat, axis=-1, keepdims=True))
    return dx, dout * xhat


def _norm_fwd(x, gain):
    (h,) = _rowwise(lambda xv, g: ((_rms(xv, g),), ()), [(x, D, 0), (gain, D, 0)], [(D, BF16)], name="norm_fwd")
    return h


def _post_fwd(x, y, gain):
    (o,) = _rowwise(lambda xv, yv, g: ((xv + _rms(yv, g),), ()), [(x, D, 0), (y, D, 0), (gain, D, 0)],
                    [(D, F32)], name="post_fwd")
    return o


def _post_bwd(g, y, gain):
    def fn(gv, yv, gn):
        dx, dg = _rms_bwd(gv, yv, gn)
        return (dx,), (dg,)

    return _rowwise(fn, [(g, D, 0), (y, D, 0), (gain, D, 0)], [(D, BF16)], [D], name="post_bwd")


def _pre_bwd(g, dh, x, gain):
    def fn(gv, dhv, xv, gn):
        dx, dg = _rms_bwd(dhv, xv, gn)
        return (gv + dx,), (dg,)

    return _rowwise(fn, [(g, D, 0), (dh, D, 0), (x, D, 0), (gain, D, 0)], [(D, F32)], [D], name="pre_bwd")


def _loss_grad(xo, tgt):
    def fn(xv, tv):
        e = xv - tv
        return (e * (1.0 / D),), (e * e,)

    return _rowwise(fn, [(xo, D, 0), (tgt, D, 0)], [(D, F32)], [D], name="loss_grad")


def _pool(u_arr, col_block, transpose, out_dtype, into=None, tc=256):
    n_t = POOL_W // tc
    per_group = POOL_GROUP // tc

    def body(u_ref, *rest):
        o_ref = rest[-1]
        c = pl.program_id(0)
        grp = c // per_group
        xv = u_ref[...]
        t = lax.broadcasted_iota(jnp.int32, (S, 1), 0)
        win = jnp.left_shift(2, grp)
        cnt = jnp.minimum(t + 1, win).astype(F32)
        cur = xv / cnt if transpose else xv
        sums = []
        for k in (1, 2, 4, 8):
            if transpose:
                sh = jnp.where(t < S - k, pltpu.roll(cur, S - k, 0), 0.0)
            else:
                sh = jnp.where(t >= k, pltpu.roll(cur, k, 0), 0.0)
            cur = cur + sh
            sums.append(cur)
        tot = jnp.where(grp == 0, sums[0], jnp.where(grp == 1, sums[1], jnp.where(grp == 2, sums[2], sums[3])))
        res = tot - xv if transpose else tot / cnt - xv
        o_ref[...] = res.astype(o_ref.dtype)

    in_specs = [pl.BlockSpec((S, tc), lambda c: (0, col_block * n_t + c))]
    args = [u_arr]
    if into is not None:
        in_specs.append(pl.BlockSpec(memory_space=pl.ANY))
        args.append(into)
    return pl.pallas_call(
        body, name="pool_bwd" if transpose else "pool_fwd", grid=(n_t,),
        in_specs=in_specs,
        out_specs=pl.BlockSpec((S, tc), lambda c: (0, c)),
        out_shape=SDS((S, POOL_W) if into is None else into.shape, out_dtype),
        input_output_aliases={} if into is None else {1: 0},
        compiler_params=_cparams(("parallel",)),
    )(*args)


def _rope_tables(zero):
    pos = jnp.arange(S, dtype=jnp.int32).astype(F32) + zero
    inv_freq = ROPE_THETA ** (-jnp.arange(0, ROT_DIM, 2, dtype=F32) / ROT_DIM)
    ang = pos[:, None] * inv_freq[None, :]
    cos8, sin8 = jnp.cos(ang), jnp.sin(ang)
    half = ROT_DIM // 2
    zeros = jnp.zeros((S, HEAD_DIM - ROT_DIM), F32)
    cos = jnp.concatenate([cos8, cos8, jnp.ones((S, HEAD_DIM - ROT_DIM), F32)], axis=1)
    lo = jnp.concatenate([-sin8, jnp.zeros((S, half), F32), zeros], axis=1)
    hi = jnp.concatenate([jnp.zeros((S, half), F32), sin8, zeros], axis=1)
    rep = LANES // HEAD_DIM
    return jnp.tile(cos, (1, rep)), jnp.tile(lo, (1, rep)), jnp.tile(hi, (1, rep))


def _rotate(xv, cos, lo, hi, transpose):
    width = xv.shape[1]
    rep = width // LANES
    wide = lambda tab: jnp.concatenate([tab] * rep, axis=1)
    half = ROT_DIM // 2
    up = pltpu.roll(xv, width - half, 1)
    dn = pltpu.roll(xv, half, 1)
    mixed = up * wide(lo) + dn * wide(hi)
    return xv * wide(cos) - mixed if transpose else xv * wide(cos) + mixed


def _qkv_prep(proj, tables):
    cos, lo, hi = tables

    def fn(x, c, l, h):
        rot = _rotate(x[:, :2 * ATT_W], c, l, h, False)
        return (jnp.concatenate([(rot[:, :ATT_W] * HEAD_DIM ** -0.5).astype(BF16), rot[:, ATT_W:].astype(BF16),
                                 x[:, 2 * ATT_W:].astype(BF16)], axis=1),), ()

    (qkv,) = _rowwise(fn, [(proj, 3 * ATT_W, 0), (cos, LANES, 0), (lo, LANES, 0), (hi, LANES, 0)],
                      [(3 * ATT_W, BF16)], name="qkv_prep")
    return qkv


ATT_T = 512


def _multiplicity(delta):
    ok = delta >= 0
    near = jnp.where(ok & (delta <= 128), 1.0, 0.0)
    mid = jnp.where(ok & (delta <= 512) & ((delta & 3) == 0), 1.0, 0.0)
    far = jnp.where(ok & ((delta & 15) == 0), 1.0, 0.0)
    return near + mid + far


def _attention_bias(zero):
    t = ATT_T
    pos = jnp.arange(t, dtype=jnp.int32) + jnp.asarray(zero).astype(jnp.int32)
    delta = jnp.arange(S // t, dtype=jnp.int32)[:, None, None] * t + pos[None, :, None] - pos[None, None, :]
    mult = _multiplicity(delta)
    return jnp.where(mult > 0.0, jnp.log(jnp.maximum(mult, 1.0)), -1e30).astype(F32)


def _head_split(v, first):
    zero = jnp.zeros_like(v)
    return [jnp.where(first, v, zero), jnp.where(first, zero, v)]


def _flash_fwd(qkv, bias):
    t = ATT_T
    n_hp = ATT_W // LANES

    def body(q_ref, k_ref, v_ref, b_ref, o_ref, lse_ref):
        i = pl.program_id(1)
        first = lax.broadcasted_iota(jnp.int32, (1, LANES), 1) < HEAD_DIM
        qs = _head_split(q_ref[...], first)

        def kv_step(j, carry):
            m0, l0, m1, l1, acc = carry
            off = pl.multiple_of(j * t, t)
            kb = k_ref[pl.ds(off, t), :]
            vs = _head_split(v_ref[pl.ds(off, t), :], first)
            bias_t = b_ref[i - j]
            new = []
            pv = None
            for h, (m_prev, l_prev) in enumerate(((m0, l0), (m1, l1))):
                s = lax.dot_general(qs[h], kb, NT, preferred_element_type=F32) + bias_t
                m_new = jnp.maximum(m_prev, jnp.max(s, axis=1, keepdims=True))
                p = jnp.exp(s - m_new)
                alpha = jnp.exp(m_prev - m_new)
                l_new = alpha * l_prev + jnp.sum(p, axis=1, keepdims=True)
                d = lax.dot_general(p.astype(BF16), vs[h], NN, preferred_element_type=F32)
                pv = d if pv is None else pv + d
                new.append((m_new, l_new, alpha))
            acc = acc * jnp.where(first, new[0][2], new[1][2]) + pv
            return new[0][0], new[0][1], new[1][0], new[1][1], acc

        neg = jnp.full((t, 1), -1e30, F32)
        zero = jnp.zeros((t, 1), F32)
        m0, l0, m1, l1, acc = lax.fori_loop(0, i + 1, kv_step, (neg, zero, neg, zero, jnp.zeros((t, LANES), F32)))
        o_ref[...] = acc * jnp.where(first, 1.0 / l0, 1.0 / l1)
        lse_ref[...] = jnp.where(first, m0 + jnp.log(l0), m1 + jnp.log(l1))

    blk = pl.BlockSpec((t, LANES), lambda hp, i: (i, hp))
    k_full = pl.BlockSpec((S, LANES), lambda hp, i: (0, n_hp + hp))
    v_full = pl.BlockSpec((S, LANES), lambda hp, i: (0, 2 * n_hp + hp))
    return pl.pallas_call(
        body, name="flash_fwd", grid=(n_hp, S // t),
        in_specs=[blk, k_full, v_full, pl.BlockSpec((S // t, t, t), lambda hp, i: (0, 0, 0))], out_specs=[blk, blk],
        out_shape=[SDS((S, ATT_W), F32), SDS((S, ATT_W), F32)],
        compiler_params=_cparams(("parallel", "arbitrary")),
    )(qkv, qkv, qkv, bias)


def _flash_bwd(qkv, o, do, lse, bias, after=()):
    t = ATT_T
    n_hp = ATT_W // LANES
    n_t = S // t

    def body(q_ref, k_ref, v_ref, o_ref, do_ref, lse_ref, b_ref, *rest):
        dq_ref, dk_ref, dv_ref = rest[-3:]
        j = pl.program_id(1)
        first = lax.broadcasted_iota(jnp.int32, (1, LANES), 1) < HEAD_DIM

        @pl.when(j == 0)
        def _():
            dq_ref[...] = jnp.zeros_like(dq_ref)

        kb = k_ref[...]
        vb = v_ref[...]
        ks = _head_split(kb, first)

        def q_step(i, carry):
            dk_acc, dv_acc = carry
            rows = pl.ds(pl.multiple_of(i * t, t), t)
            qs = _head_split(q_ref[rows, :], first)
            dob = do_ref[rows, :]
            prod = dob * o_ref[rows, :]
            d_all = jnp.sum(prod, axis=1, keepdims=True)
            d0 = jnp.sum(jnp.where(first, prod, 0.0), axis=1, keepdims=True)
            lse_b = lse_ref[rows, :]
            lse0 = jnp.max(jnp.where(first, lse_b, -jnp.inf), axis=1, keepdims=True)
            lse1 = jnp.max(jnp.where(first, -jnp.inf, lse_b), axis=1, keepdims=True)
            dos = _head_split(dob.astype(BF16), first)
            bias_t = b_ref[i - j]
            dq_t = jnp.zeros((t, LANES), F32)
            for h, (lse_h, d_h) in enumerate(((lse0, d0), (lse1, d_all - d0))):
                s = lax.dot_general(qs[h], kb, NT, preferred_element_type=F32)
                p = jnp.exp(s + (bias_t - lse_h))
                dp = lax.dot_general(dos[h], vb, NT, preferred_element_type=F32)
                ds = (p * (dp - d_h)).astype(BF16)
                dv_acc = dv_acc + lax.dot_general(p.astype(BF16), dos[h], TN, preferred_element_type=F32)
                dk_acc = dk_acc + lax.dot_general(ds, qs[h], TN, preferred_element_type=F32)
                dq_t = dq_t + lax.dot_general(ds, ks[h], NN, preferred_element_type=F32)
            dq_ref[rows, :] += dq_t
            return dk_acc, dv_acc

        zero = jnp.zeros((t, LANES), F32)
        dk_acc, dv_acc = lax.fori_loop(j, n_t, q_step, (zero, zero))
        dk_ref[...] = dk_acc
        dv_ref[...] = dv_acc

    blk = pl.BlockSpec((t, LANES), lambda hp, j: (j, hp))
    full = pl.BlockSpec((S, LANES), lambda hp, j: (0, hp))
    k_blk = pl.BlockSpec((t, LANES), lambda hp, j: (j, n_hp + hp))
    v_blk = pl.BlockSpec((t, LANES), lambda hp, j: (j, 2 * n_hp + hp))
    return pl.pallas_call(
        body, name="flash_bwd", grid=(n_hp, n_t),
        in_specs=([full, k_blk, v_blk, full, full, full, pl.BlockSpec((n_t, t, t), lambda hp, j: (0, 0, 0))]
                  + [pl.BlockSpec(memory_space=pl.ANY)] * len(after)),
        out_specs=[full, blk, blk],
        out_shape=[SDS((S, ATT_W), F32)] * 3,
        compiler_params=_cparams(("parallel", "arbitrary")),
    )(qkv, qkv, qkv, o, do, lse, bias, *after)


SCAN_T = 256
ST_ROWS = 2 * N_CPLX // LANES
HALF = ST_ROWS // 2


def _scan_fwd(lam, bu):
    def body(lam_ref, bu_ref, st_ref, carry):
        @pl.when(pl.program_id(0) == 0)
        def _():
            carry[...] = jnp.zeros_like(carry)

        ar, ai = lam_ref[0:HALF, :], lam_ref[HALF:ST_ROWS, :]

        def step(t, c):
            sr, si = c
            b = bu_ref[:, t, :]
            nr = ar * sr - ai * si + b[0:HALF]
            ni = ar * si + ai * sr + b[HALF:ST_ROWS]
            st_ref[0:HALF, t, :] = nr
            st_ref[HALF:ST_ROWS, t, :] = ni
            return nr, ni

        sr, si = lax.fori_loop(0, SCAN_T, step, (carry[0:HALF, :], carry[HALF:ST_ROWS, :]), unroll=8)
        carry[0:HALF, :] = sr
        carry[HALF:ST_ROWS, :] = si

    blk = pl.BlockSpec((ST_ROWS, SCAN_T, LANES), lambda i: (0, i, 0))
    return pl.pallas_call(
        body, name="scan_fwd", grid=(S // SCAN_T,),
        in_specs=[pl.BlockSpec((ST_ROWS, LANES), lambda i: (0, 0)), blk], out_specs=blk,
        out_shape=SDS((ST_ROWS, S, LANES), F32),
        scratch_shapes=[pltpu.VMEM((ST_ROWS, LANES), F32)],
        compiler_params=_cparams(("arbitrary",)),
    )(lam, bu)


def _scan_bwd(lam, dst, states):
    n_blk = S // SCAN_T

    def body(lam_ref, d_ref, st_ref, g_ref, dlam_ref, carry):
        @pl.when(pl.program_id(0) == 0)
        def _():
            carry[...] = jnp.zeros_like(carry)
            dlam_ref[...] = jnp.zeros_like(dlam_ref)

        ar, ai = lam_ref[0:HALF, :], lam_ref[HALF:ST_ROWS, :]

        def step(kk, c):
            t = SCAN_T - 1 - kk
            gr, gi, dar, dai = c
            x = st_ref[:, t, :]
            xr, xi = x[0:HALF], x[HALF:ST_ROWS]
            dar = dar + gr * xr + gi * xi
            dai = dai + gi * xr - gr * xi
            d = d_ref[:, t, :]
            ngr = d[0:HALF] + ar * gr + ai * gi
            ngi = d[HALF:ST_ROWS] + ar * gi - ai * gr
            g_ref[0:HALF, t, :] = ngr
            g_ref[HALF:ST_ROWS, t, :] = ngi
            return ngr, ngi, dar, dai

        zero = jnp.zeros((HALF, LANES), F32)
        gr, gi, dar, dai = lax.fori_loop(0, SCAN_T, step, (carry[0:HALF, :], carry[HALF:ST_ROWS, :], zero, zero),
                                         unroll=8)
        carry[0:HALF, :] = gr
        carry[HALF:ST_ROWS, :] = gi
        dlam_ref[0:HALF, :] += dar
        dlam_ref[HALF:ST_ROWS, :] += dai

    blk = pl.BlockSpec((ST_ROWS, SCAN_T, LANES), lambda i: (0, n_blk - 1 - i, 0))
    small = pl.BlockSpec((ST_ROWS, LANES), lambda i: (0, 0))
    return pl.pallas_call(
        body, name="scan_bwd", grid=(n_blk,),
        in_specs=[small, blk, blk], out_specs=[blk, small],
        out_shape=[SDS((ST_ROWS, S, LANES), F32), SDS((ST_ROWS, LANES), F32)],
        scratch_shapes=[pltpu.VMEM((ST_ROWS, LANES), F32)],
        compiler_params=_cparams(("arbitrary",)),
    )(lam, dst, states)


def _ssm_prep(a_re, a_im, log_dt, b_re, b_im, c_re, c_im):
    lam = lax.complex(a_re, a_im)
    dt = jnp.exp(log_dt)[:, None]
    lam_bar = jnp.exp(lam * dt)
    b_bar = ((lam_bar - 1.0) / lam)[..., None] * lax.complex(b_re, b_im)
    lam_t = jnp.concatenate([jnp.real(lam_bar).reshape(HALF, LANES), jnp.imag(lam_bar).reshape(HALF, LANES)], axis=0)
    groups_per_super = SSM_GROUPS // SSM_SUPER
    on_diag = ((lax.broadcasted_iota(jnp.int32, (SSM_W, SB_COLS), 0) // SSM_GROUP) % groups_per_super
               == lax.broadcasted_iota(jnp.int32, (SSM_W, SB_COLS), 1) // SSM_STATE)

    def compact(m):
        return jnp.where(on_diag, jnp.tile(m.reshape(SSM_W, SSM_STATE), (1, groups_per_super)), 0.0)

    w_b = jnp.concatenate([compact(jnp.real(b_bar).transpose(0, 2, 1)),
                           compact(jnp.imag(b_bar).transpose(0, 2, 1))], axis=1)
    w_ct = jnp.concatenate([compact(c_re), -compact(c_im)], axis=1)
    return lam_t, w_b, w_ct


SSM_SUPER = 4
SB_ROWS = SSM_W // SSM_SUPER
SB_COLS = N_CPLX // SSM_SUPER


def _bdmm(a, w, mode, out_dtype, a_cols=None):
    a_off = 0 if a_cols is None else a_cols[0] // SB_ROWS
    rows = SB_COLS // LANES
    state_blk = (rows, S, LANES)

    def gather(ref):
        return jnp.concatenate([ref[q] for q in range(rows)], axis=1)

    if mode == "nn":
        def body(a_ref, w_ref, o_ref):
            part = lax.dot_general(a_ref[...].astype(BF16), w_ref[...].astype(BF16), NN,
                                   preferred_element_type=F32).astype(o_ref.dtype)
            for q in range(rows):
                o_ref[q] = part[:, q * LANES:(q + 1) * LANES]

        return pl.pallas_call(
            body, name="bdmm_nn", grid=(2, SSM_SUPER),
            in_specs=[pl.BlockSpec((S, SB_ROWS), lambda h, b: (0, a_off + b)),
                      pl.BlockSpec((SB_ROWS, SB_COLS), lambda h, b: (b, h))],
            out_specs=pl.BlockSpec(state_blk, lambda h, b: (h * SSM_SUPER + b, 0, 0)),
            out_shape=SDS((ST_ROWS, S, LANES), out_dtype),
            compiler_params=_cparams(("parallel", "parallel")),
        )(a, w)
    if mode == "nt":
        def body(re_ref, im_ref, wre_ref, wim_ref, o_ref):
            acc = lax.dot_general(gather(re_ref).astype(BF16), wre_ref[...].astype(BF16), NT,
                                  preferred_element_type=F32)
            acc += lax.dot_general(gather(im_ref).astype(BF16), wim_ref[...].astype(BF16), NT,
                                   preferred_element_type=F32)
            o_ref[...] = acc.astype(o_ref.dtype)

        return pl.pallas_call(
            body, name="bdmm_nt", grid=(SSM_SUPER,),
            in_specs=[pl.BlockSpec(state_blk, lambda b: (b, 0, 0)),
                      pl.BlockSpec(state_blk, lambda b: (SSM_SUPER + b, 0, 0)),
                      pl.BlockSpec((SB_ROWS, SB_COLS), lambda b: (b, 0)),
                      pl.BlockSpec((SB_ROWS, SB_COLS), lambda b: (b, 1))],
            out_specs=pl.BlockSpec((S, SB_ROWS), lambda b: (0, b)),
            out_shape=SDS((S, SSM_W), out_dtype),
            compiler_params=_cparams(("parallel",)),
        )(a, a, w, w)

    def body_tn(a_ref, w_ref, o_ref):
        o_ref[...] = lax.dot_general(a_ref[...].astype(BF16), gather(w_ref).astype(BF16), TN,
                                     preferred_element_type=F32).astype(o_ref.dtype)

    return pl.pallas_call(
        body_tn, name="bdmm_tn", grid=(2, SSM_SUPER),
        in_specs=[pl.BlockSpec((S, SB_ROWS), lambda h, b: (0, a_off + b)),
                  pl.BlockSpec(state_blk, lambda h, b: (h * SSM_SUPER + b, 0, 0))],
        out_specs=pl.BlockSpec((SB_ROWS, SB_COLS), lambda h, b: (b, h)),
        out_shape=SDS((SSM_W, 2 * SB_COLS), out_dtype),
        compiler_params=_cparams(("parallel", "parallel")),
    )(a, w)


U_SSM_COLS = (4 * ATT_W, SSM_W)


def _row(v):
    return v.reshape(1, -1)


def _even_fwd(x, pre, post, w_in, late_w, glu_b, ssm_d, prep, tables):
    lam_t, w_b, w_ct = prep
    h = _norm_fwd(x, pre)
    proj = _mm(h, w_in, "nn", F32, b_blocks=True)
    qkv = _qkv_prep(proj, tables[:3])
    w_out, glu_w = late_w(qkv)
    att, lse = _flash_fwd(qkv, tables[3])
    bu = _bdmm(proj, w_b, "nn", F32, a_cols=U_SSM_COLS)
    states = _scan_fwd(lam_t, bu)
    y = _bdmm(states, w_ct, "nt", F32)

    def act1(yv, uv, dv):
        return (_gelu_and_grad(yv + dv * uv)[0],), ()

    (z1,) = _rowwise(act1, [(y, SSM_W, 0), (proj, SSM_W, 8), (ssm_d, SSM_W, 0)], [(SSM_W, F32)], name="ssm_act_fwd")
    lin = _mm(z1, glu_w, "nn", F32)

    def gate(att_v, ga, gs, z1v, linv, bv):
        ssm_out = z1v * _sigmoid(linv + bv)
        return (jnp.concatenate([att_v * _silu_and_grad(ga)[0], ssm_out * _silu_and_grad(gs)[0]], axis=1),), ()

    (merged,) = _rowwise(gate, [(att, ATT_W, 0), (proj, ATT_W, 3), (proj, SSM_W, 9), (z1, SSM_W, 0),
                                (lin, SSM_W, 0), (glu_b, SSM_W, 0)], [(EVEN_OUT, BF16)], name="even_gate_fwd")
    yout = _mm(merged, w_out, "nn", F32)
    x_next = _post_fwd(x, yout, post)
    saved = (x, h, proj, qkv, att, lse, states, y, z1, lin, merged, yout, w_out, glu_w)
    return x_next, saved


def _even_bwd(g, saved, pre, post, w_in, late_w, glu_b, ssm_d, prep, tables, on_w, on_ssm):
    x, h, proj, qkv, att, lse, states, y, z1, lin, merged, yout, w_out, glu_w = saved
    lam_t, w_b, w_ct = prep
    dyout, dpost = _post_bwd(g, yout, post)
    dmerged = _mm(dyout, w_out, "nt", F32)
    dw_out = _mm(merged, dyout, "tn", BF16)

    def gate_bwd(dm_a, dm_s, att_v, ga, gs, z1v, linv, bv):
        sa, dsa = _silu_and_grad(ga)
        ss, dss = _silu_and_grad(gs)
        sig = _sigmoid(linv + bv)
        ssm_out = z1v * sig
        dssm = dm_s * ss
        dlin = dssm * z1v * sig * (1.0 - sig)
        return (dm_a * sa, dm_a * att_v * dsa, dm_s * ssm_out * dss, dssm * sig, dlin), (dlin,)

    datt, dg_att, dg_ssm, dz1a, dlin, dglu_b = _rowwise(
        gate_bwd, [(dmerged, ATT_W, 0), (dmerged, SSM_W, 2), (att, ATT_W, 0), (proj, ATT_W, 3), (proj, SSM_W, 9),
                   (z1, SSM_W, 0), (lin, SSM_W, 0), (glu_b, SSM_W, 0)],
        [(ATT_W, F32), (ATT_W, BF16), (SSM_W, BF16), (SSM_W, F32), (SSM_W, BF16)], [SSM_W], name="even_gate_bwd")
    dz1b = _mm(dlin, glu_w, "nt", F32)
    dglu_w = _mm(z1, dlin, "tn", BF16)

    def act1_bwd(da, db, yv, uv, dv):
        dpre = (da + db) * _gelu_and_grad(yv + dv * uv)[1]
        return (dpre, dpre * dv), (dpre * uv,)

    sent_late_w = on_w(dict(w_out=dw_out, glu_w=dglu_w))
    dy, du_direct, dd = _rowwise(act1_bwd, [(dz1a, SSM_W, 0), (dz1b, SSM_W, 0), (y, SSM_W, 0), (proj, SSM_W, 8),
                                            (ssm_d, SSM_W, 0)], [(SSM_W, BF16), (SSM_W, F32)], [SSM_W],
                                 name="ssm_act_bwd", after=(sent_late_w,))
    dst = _bdmm(dy, w_ct, "nn", F32)
    dw_ct = _bdmm(dy, states, "tn", F32)
    dbu, dlam = _scan_bwd(lam_t, dst, states)
    du_state = _bdmm(dbu, w_b, "nt", F32)
    dw_b = _bdmm(proj, dbu, "tn", F32, a_cols=U_SSM_COLS)
    sent_ssm = on_ssm((dlam, dw_b, dw_ct))
    dq, dk, dv = _flash_bwd(qkv, att, datt, lse, tables[3], after=() if sent_ssm is None else (sent_ssm,))

    def assemble(dqv, dkv, dvv, dga, dua, dub, dgs, c, l, h):
        rot = _rotate(jnp.concatenate([dqv, dkv], axis=1), c, l, h, True)
        return (jnp.concatenate([(rot[:, :ATT_W] * HEAD_DIM ** -0.5).astype(BF16), rot[:, ATT_W:].astype(BF16),
                                 dvv.astype(BF16), dga, (dua + dub).astype(BF16), dgs], axis=1),), ()

    (dproj,) = _rowwise(assemble, [(dq, ATT_W, 0), (dk, ATT_W, 0), (dv, ATT_W, 0), (dg_att, ATT_W, 0),
                                   (du_state, SSM_W, 0), (du_direct, SSM_W, 0), (dg_ssm, SSM_W, 0),
                                   (tables[0], LANES, 0), (tables[1], LANES, 0), (tables[2], LANES, 0)],
                        [(EVEN_IN, BF16)], name="dproj_assemble")
    dw_in = _mm(h, dproj, "tn", BF16, out_blocks=True)
    sent = on_w(dict(w_in=dw_in))
    dh = _mm(dproj, w_in, "nt", F32, b_blocks=True, after=(sent,))
    g_prev, dpre = _pre_bwd(g, dh, x, pre)
    return g_prev, dict(pre=dpre, post=dpost, glu_b=dglu_b, ssm_d=dd)


def _odd_fwd(x, pre, post, w_in, pool_w, pool_scale, w_out):
    h = _norm_fwd(x, pre)
    proj = _mm(h, w_in, "nn", F32, b_blocks=True)
    mixed = _pool(proj, 0, False, BF16)
    ylin = _gmm(mixed, pool_w, "nn", F32)

    def gate(yl, gt, sc):
        return (yl * sc * _silu_and_grad(gt)[0],), ()

    (z,) = _rowwise(gate, [(ylin, POOL_W, 0), (proj, POOL_W, 1), (pool_scale, POOL_W, 0)], [(POOL_W, BF16)],
                    name="odd_gate_fwd")
    yout = _mm(z, w_out, "nn", F32)
    x_next = _post_fwd(x, yout, post)
    return x_next, (x, h, proj, mixed, ylin, z, yout)


def _odd_bwd(g, saved, pre, post, w_in, pool_w, pool_scale, w_out, on_w):
    x, h, proj, mixed, ylin, z, yout = saved
    dyout, dpost = _post_bwd(g, yout, post)
    dz = _mm(dyout, w_out, "nt", F32)
    dw_out = _mm(z, dyout, "tn", BF16)

    def gate_bwd(dzv, yl, gt, sc):
        sg, dsg = _silu_and_grad(gt)
        tt = dzv * sg
        return (tt * sc, dzv * yl * sc * dsg), (tt * yl,)

    dylin, dproj_gate, dscale = _rowwise(gate_bwd, [(dz, POOL_W, 0), (ylin, POOL_W, 0), (proj, POOL_W, 1),
                                                    (pool_scale, POOL_W, 0)],
                                         [(POOL_W, BF16), (POOL_W, BF16, ODD_IN, 1)], [POOL_W], name="odd_gate_bwd")
    dmixed = _gmm(dylin, pool_w, "nt", F32)
    dpool_w = _gmm(mixed, dylin, "tn", BF16)
    dproj = _pool(dmixed, 0, True, BF16, into=dproj_gate)
    dw_in = _mm(h, dproj, "tn", BF16, out_blocks=True)
    sent = on_w(dict(w_in=dw_in, w_out=dw_out, pool_w=dpool_w))
    dh = _mm(dproj, w_in, "nt", F32, b_blocks=True, after=(sent,))
    g_prev, dpre = _pre_bwd(g, dh, x, pre)
    return g_prev, dict(pre=dpre, post=dpost, pool_scale=dscale)


def _my_index():
    return 4 * lax.axis_index("x") + 2 * lax.axis_index("y") + lax.axis_index("c")


HBM_SPEC = pl.BlockSpec(memory_space=pltpu.HBM)
SEM_SPEC = pl.BlockSpec(memory_space=pltpu.SEMAPHORE)
SPLIT_EFFECT = pltpu.SideEffectType.DATAFLOW_SIDE_EFFECTING


def _device_of(j):
    return (j // 4, (j // 2) % 2, j % 2)


def _split_copy(srcs, lands, send_sems, recv_sems, gather, i, j, dst_slot, recv_slot):
    return pltpu.make_async_remote_copy(
        src_ref=srcs[i] if gather else srcs[i].at[j], dst_ref=lands[i].at[dst_slot],
        send_sem=send_sems.at[i * N_DEV + j], recv_sem=recv_sems.at[i * N_DEV + recv_slot],
        device_id=_device_of(j), device_id_type=MESH_ID)


def _own_copy(srcs, lands, send_sems, gather, i, me):
    return pltpu.make_async_copy(srcs[i] if gather else srcs[i].at[me], lands[i].at[me], send_sems.at[i * N_DEV + me])


def _xchg_start(name, srcs, gather, after=()):
    n = len(srcs)
    n_in = n + len(after)

    def body(*refs):
        src_refs = refs[:n]
        send_sems, recv_sems, token = refs[n_in], refs[n_in + 1], refs[-1]
        land_refs = refs[n_in + 2 + n:n_in + 2 + 2 * n]
        me = _my_index()
        for j in range(N_DEV):
            @pl.when(me != j)
            def _(j=j):
                for i in range(n):
                    _split_copy(src_refs, land_refs, send_sems, recv_sems, gather, i, j, me, me).start()
        for i in range(n):
            _own_copy(src_refs, land_refs, send_sems, gather, i, me).start()
        token[...] = jnp.zeros_like(token)

    land_shapes = [((N_DEV,) + a.shape) if gather else a.shape for a in srcs]
    thru = ([pltpu.HBM(a.shape, a.dtype) for a in srcs] + [pltpu.HBM(s, a.dtype) for s, a in zip(land_shapes, srcs)])
    res = pl.pallas_call(
        body, name=name,
        out_shape=(pltpu.SemaphoreType.DMA((n * N_DEV,)), pltpu.SemaphoreType.DMA((n * N_DEV,)), *thru,
                   SDS((8, LANES), F32)),
        in_specs=[HBM_SPEC] * n + [pl.BlockSpec(memory_space=pl.ANY)] * len(after),
        out_specs=(SEM_SPEC, SEM_SPEC, *([HBM_SPEC] * (2 * n)), pl.BlockSpec(memory_space=pltpu.VMEM)),
        input_output_aliases={i: 2 + i for i in range(n)},
        compiler_params=pltpu.CompilerParams(has_side_effects=SPLIT_EFFECT),
    )(*[pltpu.with_memory_space_constraint(a, pltpu.HBM) for a in srcs], *after)
    return res[0], res[1], list(res[2:2 + n]), list(res[2 + n:2 + 2 * n]), res[-1]


def _xchg_wait(name, started, gather, after):
    send_sems, recv_sems, srcs, lands, _ = started
    n = len(srcs)

    def body(*refs):
        src_refs, land_refs = refs[:n], refs[n:2 * n]
        send_r, recv_r = refs[2 * n], refs[2 * n + 1]
        me = _my_index()
        for j in range(N_DEV):
            @pl.when(me != j)
            def _(j=j):
                for i in range(n):
                    _split_copy(src_refs, land_refs, send_r, recv_r, gather, i, j, me, me).wait_send()
                    _split_copy(src_refs, land_refs, send_r, recv_r, gather, i, j, j, j).wait_recv()
        for i in range(n):
            _own_copy(src_refs, land_refs, send_r, gather, i, me).wait()

    thru = [pltpu.HBM(a.shape, a.dtype) for a in list(srcs) + list(lands)]
    res = pl.pallas_call(
        body, name=name, out_shape=tuple(thru),
        in_specs=[HBM_SPEC] * (2 * n) + [SEM_SPEC, SEM_SPEC] + [pl.BlockSpec(memory_space=pl.ANY)] * len(after),
        out_specs=tuple([HBM_SPEC] * (2 * n)),
        input_output_aliases={i: i for i in range(2 * n)},
        compiler_params=pltpu.CompilerParams(has_side_effects=SPLIT_EFFECT),
    )(*srcs, *lands, send_sems, recv_sems, *after)
    return list(res[n:])


def _adam_layers(w, slot_list, m, v, name):
    n_l, r, c = w.shape
    ns = slot_list[0].shape[0]
    tr = r
    while tr * c * 4 > (1 << 20) and tr % 16 == 0:
        tr //= 2
    assert r % tr == 0 and len(slot_list) == n_l

    def body(*refs):
        w_ref, slot_refs = refs[0], refs[1:1 + n_l]
        m_ref, v_ref, go_ref, d_ref, mo_ref, vo_ref = refs[1 + n_l:]
        layer = pl.program_id(0)
        g = None
        for l, g_ref in enumerate(slot_refs):
            gl = g_ref[0].astype(F32)
            for s in range(1, ns):
                gl = gl + g_ref[s].astype(F32)
            g = gl if g is None else jnp.where(layer == l, gl, g)
        mn = ADAM_B1 * m_ref[...] + (1.0 - ADAM_B1) * g
        vn = ADAM_B2 * v_ref[...] + (1.0 - ADAM_B2) * (g * g)
        m_hat = mn / (1.0 - ADAM_B1 ** ADAM_STEP)
        v_hat = vn / (1.0 - ADAM_B2 ** ADAM_STEP)
        go_ref[...] = g
        d_ref[...] = -ADAM_LR * (m_hat / (jnp.sqrt(v_hat) + ADAM_EPS) + ADAM_WD * w_ref[...])
        mo_ref[...] = mn
        vo_ref[...] = vn

    blk = pl.BlockSpec((None, tr, c), lambda l, i: (l, i, 0))
    slot_specs = [pl.BlockSpec((ns, tr, c), lambda l, i, k=k: (0, jnp.where(l == k, i, 0), 0)) for k in range(n_l)]
    return pl.pallas_call(
        body, name=name, grid=(n_l, r // tr),
        in_specs=[blk] + slot_specs + [blk, blk],
        out_specs=[blk] * 4, out_shape=[SDS((n_l, r, c), F32)] * 4,
        compiler_params=_cparams(("arbitrary", "arbitrary")),
    )(w, *slot_list, m, v)


def _adam(w, gslots, m, v, name):
    r, c = w.shape
    ns = gslots.shape[0]
    tr = r
    while tr * c * 4 > (1 << 20) and tr % 16 == 0:
        tr //= 2
    assert r % tr == 0

    def body(w_ref, g_ref, m_ref, v_ref, go_ref, d_ref, mo_ref, vo_ref):
        g = g_ref[0].astype(F32)
        for s in range(1, ns):
            g = g + g_ref[s].astype(F32)
        wv = w_ref[...]
        mn = ADAM_B1 * m_ref[...] + (1.0 - ADAM_B1) * g
        vn = ADAM_B2 * v_ref[...] + (1.0 - ADAM_B2) * (g * g)
        m_hat = mn / (1.0 - ADAM_B1 ** ADAM_STEP)
        v_hat = vn / (1.0 - ADAM_B2 ** ADAM_STEP)
        go_ref[...] = g
        d_ref[...] = -ADAM_LR * (m_hat / (jnp.sqrt(v_hat) + ADAM_EPS) + ADAM_WD * wv)
        mo_ref[...] = mn
        vo_ref[...] = vn

    blk = pl.BlockSpec((tr, c), lambda i: (i, 0))
    return pl.pallas_call(
        body, name=name, grid=(r // tr,),
        in_specs=[blk, pl.BlockSpec((ns, tr, c), lambda i: (0, i, 0)), blk, blk],
        out_specs=[blk] * 4, out_shape=[SDS((r, c), F32)] * 4,
        compiler_params=_cparams(("parallel",)),
    )(w, gslots, m, v)


def _sum_slots(slots, name):
    ns, r, c = slots.shape

    def body(g_ref, o_ref):
        g = g_ref[0]
        for s in range(1, ns):
            g = g + g_ref[s]
        o_ref[...] = g

    return pl.pallas_call(
        body, name=name, grid=(1,),
        in_specs=[pl.BlockSpec((ns, r, c), lambda i: (0, 0, 0))], out_specs=pl.BlockSpec((r, c), lambda i: (0, 0)),
        out_shape=SDS((r, c), F32), compiler_params=_cparams(("arbitrary",)),
    )(slots)


def _adam_params(params, name):
    n = len(params)

    def body(*refs):
        ins, outs = refs[:5 * n], refs[5 * n:]
        for p in range(n):
            w_ref, m_ref, v_ref, g_first, g_rest = ins[5 * p:5 * p + 5]
            go_ref, d_ref, mo_ref, vo_ref = outs[4 * p:4 * p + 4]
            for part, g_ref in ((slice(0, 1), g_first), (slice(1, w_ref.shape[0]), g_rest)):
                g = g_ref[...]
                mn = ADAM_B1 * m_ref[part] + (1.0 - ADAM_B1) * g
                vn = ADAM_B2 * v_ref[part] + (1.0 - ADAM_B2) * (g * g)
                m_hat = mn / (1.0 - ADAM_B1 ** ADAM_STEP)
                v_hat = vn / (1.0 - ADAM_B2 ** ADAM_STEP)
                go_ref[part] = g
                d_ref[part] = -ADAM_LR * (m_hat / (jnp.sqrt(v_hat) + ADAM_EPS) + ADAM_WD * w_ref[part])
                mo_ref[part] = mn
                vo_ref[part] = vn

    def whole(a):
        return pl.BlockSpec(a.shape, lambda i, nd=a.ndim: (0,) * nd)

    flat = [pltpu.with_memory_space_constraint(a, pltpu.HBM) for prm in params for a in prm]
    outs = pl.pallas_call(
        body, name=name, grid=(1,),
        in_specs=[whole(a) for a in flat],
        out_specs=[whole(prm[0]) for prm in params for _ in range(4)],
        out_shape=[SDS(prm[0].shape, F32) for prm in params for _ in range(4)],
        compiler_params=_cparams(("arbitrary",)),
    )(*flat)
    return [outs[4 * p:4 * p + 4] for p in range(n)]


SMALL_NAMES = ("pre_norm", "post_norm", "ssm_a_re", "ssm_a_im", "ssm_log_dt", "ssm_b_re", "ssm_b_im", "ssm_c_re",
               "ssm_c_im", "ssm_d", "ssm_glu_b")
SSM_NAMES = ("ssm_a_re", "ssm_a_im", "ssm_log_dt", "ssm_b_re", "ssm_b_im", "ssm_c_re", "ssm_c_im")
SHARDED_NAMES = ("even_w_in", "even_w_out", "ssm_glu_w", "odd_w_in", "pool_w", "odd_w_out")
WEIGHT_ORDER = ("pre_norm", "post_norm", "even_w_in", "even_w_out", "ssm_a_re", "ssm_a_im", "ssm_log_dt", "ssm_b_re",
                "ssm_b_im", "ssm_c_re", "ssm_c_im", "ssm_d", "ssm_glu_w", "ssm_glu_b", "odd_w_in", "pool_w",
                "pool_scale", "odd_w_out")
PACK_ROWS_ALIGN = 8


def _pack(parts):
    flat = jnp.concatenate([p.reshape(-1).astype(F32) for p in parts])
    rows = -(-flat.shape[0] // (LANES * PACK_ROWS_ALIGN)) * PACK_ROWS_ALIGN
    return jnp.pad(flat, (0, rows * LANES - flat.shape[0])).reshape(rows, LANES)


def _unpack(packed, shapes):
    flat = packed.reshape(-1)
    out, off = [], 0
    for shp in shapes:
        size = math.prod(shp)
        out.append(flat[off:off + size].reshape(shp))
        off += size
    return out


EVEN_SHARDED = ("w_in", "w_out", "glu_w")
ODD_SHARDED = ("w_in", "pool_w", "w_out")
FAMILY = {(0, "w_in"): "even_w_in", (0, "w_out"): "even_w_out", (0, "glu_w"): "ssm_glu_w",
          (1, "w_in"): "odd_w_in", (1, "pool_w"): "pool_w", (1, "w_out"): "odd_w_out"}


def _sharded_keys(layer):
    return EVEN_SHARDED if layer % 2 == 0 else ODD_SHARDED


def _local_step(x, tgt, small, get_weights, on_w, on_ssm, on_grads, zero=0.0):
    tables = _rope_tables(zero) + (_attention_bias(zero),)
    preps, prep_vjps = [], []
    for i in range(2):
        out, vjp = jax.vjp(_ssm_prep, small["ssm_a_re"][i] + zero, small["ssm_a_im"][i], small["ssm_log_dt"][i],
                           small["ssm_b_re"][i], small["ssm_b_im"][i], small["ssm_c_re"][i], small["ssm_c_im"][i])
        preps.append(out)
        prep_vjps.append(vjp)

    def layer_args(layer, wts):
        i = layer // 2
        pre, post = _row(small["pre_norm"][layer]) + wts.get("token", 0.0), _row(small["post_norm"][layer])
        if layer % 2 == 0:
            return (pre, post, wts["w_in"], wts["late"], _row(small["ssm_glu_b"][i]), _row(small["ssm_d"][i]),
                    preps[i], tables)
        return (pre, post, wts["w_in"], wts["pool_w"], _row(wts["pool_scale"]), wts["w_out"])

    saved, args = [], []
    cur = x
    for layer in range(4):
        after = (cur,) if layer else (cur, tables[0], tables[3], preps[0][1], preps[0][2], preps[1][1], preps[1][2])
        args.append(layer_args(layer, get_weights(layer, after)))
        cur, sv = (_even_fwd if layer % 2 == 0 else _odd_fwd)(cur, *args[layer])
        saved.append(sv)
    g, sq = _loss_grad(cur, tgt)
    loss = 0.5 * jnp.sum(sq) / D

    lg = [None] * 4
    token = jnp.zeros((), F32)
    for layer in reversed(range(4)):
        largs = list(args[layer])
        largs[1] = largs[1] + token
        hooks = dict(on_w=functools.partial(on_w, layer))
        ssm_grads = []
        if layer % 2 == 0:
            def ssm_hook(cotangents, layer=layer):
                ssm_grads.append(prep_vjps[layer // 2](cotangents))
                return on_ssm(layer, ssm_grads[0])

            hooks["on_ssm"] = ssm_hook
        g, lg[layer] = (_even_bwd if layer % 2 == 0 else _odd_bwd)(g, saved[layer], *largs, **hooks)
        if ssm_grads:
            lg[layer]["ssm"] = ssm_grads[0]
        token = on_grads(layer, lg[layer])
    return loss, g, token


def _to_slots(key, gfull):
    if key == "w_in":
        return gfull
    if key in ("w_out", "glu_w"):
        rr, nn = gfull.shape
        return gfull.reshape(N_DEV, rr // N_DEV, nn)
    assert key == "pool_w"
    gg, rr, nn = gfull.shape
    return gfull.reshape(gg, N_DEV, rr // N_DEV, nn).transpose(1, 0, 2, 3)


def _from_gathered(key, gat):
    if key == "w_in":
        return gat
    if key in ("w_out", "glu_w"):
        _, rr, nn = gat.shape
        return gat.reshape(N_DEV * rr, nn)
    assert key == "pool_w"
    _, gg, rr, nn = gat.shape
    return gat.transpose(1, 0, 2, 3).reshape(gg, N_DEV * rr, nn)


def kernel(x, pre_norm, post_norm, even_w_in, even_w_out, ssm_a_re, ssm_a_im, ssm_log_dt, ssm_b_re, ssm_b_im, ssm_c_re, ssm_c_im, ssm_d, ssm_glu_w, ssm_glu_b, odd_w_in, pool_w, pool_scale, odd_w_out, loss_target, m_pre_norm, m_post_norm, m_even_w_in, m_even_w_out, m_ssm_a_re, m_ssm_a_im, m_ssm_log_dt, m_ssm_b_re, m_ssm_b_im, m_ssm_c_re, m_ssm_c_im, m_ssm_d, m_ssm_glu_w, m_ssm_glu_b, m_odd_w_in, m_pool_w, m_pool_scale, m_odd_w_out, v_pre_norm, v_post_norm, v_even_w_in, v_even_w_out, v_ssm_a_re, v_ssm_a_im, v_ssm_log_dt, v_ssm_b_re, v_ssm_b_im, v_ssm_c_re, v_ssm_c_im, v_ssm_d, v_ssm_glu_w, v_ssm_glu_b, v_odd_w_in, v_pool_w, v_pool_scale, v_odd_w_out):
    w = dict(pre_norm=pre_norm, post_norm=post_norm, even_w_in=even_w_in, even_w_out=even_w_out, ssm_a_re=ssm_a_re,
             ssm_a_im=ssm_a_im, ssm_log_dt=ssm_log_dt, ssm_b_re=ssm_b_re, ssm_b_im=ssm_b_im, ssm_c_re=ssm_c_re,
             ssm_c_im=ssm_c_im, ssm_d=ssm_d, ssm_glu_w=ssm_glu_w, ssm_glu_b=ssm_glu_b, odd_w_in=odd_w_in,
             pool_w=pool_w, pool_scale=pool_scale, odd_w_out=odd_w_out)
    mom = dict(pre_norm=m_pre_norm, post_norm=m_post_norm, even_w_in=m_even_w_in, even_w_out=m_even_w_out,
               ssm_a_re=m_ssm_a_re, ssm_a_im=m_ssm_a_im, ssm_log_dt=m_ssm_log_dt, ssm_b_re=m_ssm_b_re,
               ssm_b_im=m_ssm_b_im, ssm_c_re=m_ssm_c_re, ssm_c_im=m_ssm_c_im, ssm_d=m_ssm_d, ssm_glu_w=m_ssm_glu_w,
               ssm_glu_b=m_ssm_glu_b, odd_w_in=m_odd_w_in, pool_w=m_pool_w, pool_scale=m_pool_scale,
               odd_w_out=m_odd_w_out)
    var = dict(pre_norm=v_pre_norm, post_norm=v_post_norm, even_w_in=v_even_w_in, even_w_out=v_even_w_out,
               ssm_a_re=v_ssm_a_re, ssm_a_im=v_ssm_a_im, ssm_log_dt=v_ssm_log_dt, ssm_b_re=v_ssm_b_re,
               ssm_b_im=v_ssm_b_im, ssm_c_re=v_ssm_c_re, ssm_c_im=v_ssm_c_im, ssm_d=v_ssm_d, ssm_glu_w=v_ssm_glu_w,
               ssm_glu_b=v_ssm_glu_b, odd_w_in=v_odd_w_in, pool_w=v_pool_w, pool_scale=v_pool_scale,
               odd_w_out=v_odd_w_out)
    me = _my_index()
    scale_cols = pool_scale.shape[1]

    def start_gather(tag, layer, keys, after=()):
        i = layer // 2
        shards = [w[FAMILY[(layer % 2, k)]][i].astype(BF16) for k in keys]
        if layer % 2 == 1:
            shards.append(jnp.pad(pool_scale[i][None], ((0, PACK_ROWS_ALIGN - 1), (0, 0))))
        return _xchg_start(f"gather_start_{tag}", shards, True, after)

    gather_started = {0: start_gather("0", 0, EVEN_SHARDED[:1])}
    small = {nm: w[nm] for nm in SMALL_NAMES}

    def get_weights(layer, after):
        keys = EVEN_SHARDED[:1] if layer == 0 else _sharded_keys(layer)
        lands = _xchg_wait(f"gather_wait_{layer}", gather_started[layer], True, after)
        wts = {k: _from_gathered(k, gat) for k, gat in zip(keys, lands)}
        if layer % 2 == 1:
            wts["pool_scale"] = lands[-1][:, 0, :].reshape(N_DEV * scale_cols)
        if layer == 0:
            gather_started["0_late"] = start_gather("0_late", 0, EVEN_SHARDED[1:], after=(lands[0],))
            for later in (1, 2, 3):
                gather_started[later] = start_gather(str(later), later, _sharded_keys(later), after=(lands[0],))
            wts["token"] = sum(gather_started[tag][4][0, 0] for tag in ("0_late", 1, 2, 3))

            def late(after_late):
                late_lands = _xchg_wait("gather_wait_0_late", gather_started["0_late"], True, (after_late,))
                return tuple(_from_gathered(k, gat) for k, gat in zip(EVEN_SHARDED[1:], late_lands))

            wts["late"] = late
        elif layer == 2:
            wts["late"] = lambda after_late: (wts["w_out"], wts["glu_w"])
        return wts

    scatter_started = []

    def on_w(layer, gw):
        keys = tuple(k for k in _sharded_keys(layer) if k in gw)
        started = _xchg_start(f"scatter_start_{layer}_{keys[0]}", [_to_slots(k, gw[k]) for k in keys], False)
        scatter_started.append((layer, keys, started))
        return started[4]

    def wait_scatters(layers, after):
        for layer, keys, started in scatter_started:
            if layer in layers:
                lands = _xchg_wait(f"scatter_wait_{layer}_{keys[0]}", started, False, after)
                for k, land in zip(keys, lands):
                    recv[(layer, k)] = land

    packed_names = ("pre_norm", "post_norm") + SSM_NAMES + ("ssm_d", "ssm_glu_b")
    tails = {nm: (SSM_GROUPS, SSM_STATE * SSM_GROUP) if nm in ("ssm_b_re", "ssm_b_im") else w[nm].shape[1:]
             for nm in packed_names}

    layer_grads = {}
    early_started, mid_started = [], []

    def on_ssm(layer, ssm_grads):
        if layer != 0:
            return None
        mid_started.append(_xchg_start("mid_start", [_pack(list(ssm_grads))], True))
        return mid_started[0][4]

    def on_grads(layer, lg):
        layer_grads[layer] = lg
        zero = jnp.zeros((), F32)
        if layer == 1:
            lgs = layer_grads
            early = ([jnp.concatenate([lgs[l][k] for l in (1, 2, 3)], axis=0) for k in ("pre", "post")]
                     + list(lgs[2]["ssm"]) + [lgs[2]["ssm_d"], lgs[2]["glu_b"],
                                              jnp.concatenate([lgs[1]["pool_scale"], lgs[3]["pool_scale"]], axis=0)])
            early_started.append(_xchg_start("small_start", [_pack(early)], True))
            zero = zero + early_started[0][4][0, 0]
        return zero

    loss_local, grad_x, token = _local_step(x[0], loss_target[0], small, get_weights, on_w, on_ssm, on_grads,
                                            zero=gather_started[0][4][0, 0])

    lg0 = layer_grads[0]
    late_started = _xchg_start("late_start", [_pack([lg0["pre"], lg0["post"], lg0["ssm_d"], lg0["glu_b"],
                                                     loss_local.reshape(1)]) + token], True)

    def adam_family(parity, k):
        nm = FAMILY[(parity, k)]
        shp = w[nm].shape
        cols = shp[-1]
        slot_list = [recv[(parity + 2 * i, k)].reshape(N_DEV, -1, cols) for i in range(2)]
        outs = _adam_layers(w[nm].reshape(2, -1, cols), slot_list, mom[nm].reshape(2, -1, cols),
                            var[nm].reshape(2, -1, cols), name=f"adam_{nm}")
        return [o.reshape(shp) for o in outs]

    recv, res = {}, {}
    wait_scatters((3, 1), (late_started[4],))
    for k in ODD_SHARDED:
        res[FAMILY[(1, k)]] = adam_family(1, k)
    odd_done = tuple(res[FAMILY[(1, k)]][0] for k in ODD_SHARDED)

    (early_slots,) = _xchg_wait("small_wait", early_started[0], True, odd_done)
    (mid_slots,) = _xchg_wait("mid_wait", mid_started[0], True, odd_done)
    early_shapes = [(w[nm].shape[0] - 1,) + tails[nm] for nm in packed_names] + [(2, N_DEV * scale_cols)]
    g_early = _unpack(_sum_slots(early_slots, "sum_small_early"), early_shapes)
    g_mid = _unpack(_sum_slots(mid_slots, "sum_small_mid"), [(1,) + tails[nm] for nm in SSM_NAMES])

    (late_slots,) = _xchg_wait("late_wait", late_started, True, (g_early[0], g_mid[0]))
    wait_scatters((2, 0), (late_slots,))
    for k in EVEN_SHARDED:
        res[FAMILY[(0, k)]] = adam_family(0, k)

    late_names = ("pre_norm", "post_norm", "ssm_d", "ssm_glu_b")
    g_late = _unpack(_sum_slots(late_slots, "sum_small_late"), [(1,) + tails[nm] for nm in late_names] + [(1,)])
    g_first = dict(zip(late_names, g_late))
    g_first.update(zip(SSM_NAMES, g_mid))
    dense = lambda nm, a: a.reshape((a.shape[0],) + tails[nm])
    outs = _adam_params([(dense(nm, w[nm]), dense(nm, mom[nm]), dense(nm, var[nm]), g_first[nm], g_early[j])
                         for j, nm in enumerate(packed_names)], "adam_small")
    for nm, four in zip(packed_names, outs):
        res[nm] = [o.reshape(w[nm].shape) for o in four]
    loss = g_late[-1].reshape(())
    g_scale = lax.dynamic_slice_in_dim(g_early[-1], me * scale_cols, scale_cols, axis=1)
    pad = ((0, PACK_ROWS_ALIGN - 2), (0, 0))
    outs = _adam(jnp.pad(pool_scale, pad), jnp.pad(g_scale, pad)[None], jnp.pad(m_pool_scale, pad),
                 jnp.pad(v_pool_scale, pad), name="adam_pool_scale")
    res["pool_scale"] = [o[:2] for o in outs]

    out = [loss, grad_x[None]]
    for kind in range(4):
        out += [res[nm][kind] for nm in WEIGHT_ORDER]
    return tuple(out)
```

```python
import functools
import math

import jax
import jax.numpy as jnp
from jax import lax
from jax.experimental import pallas as pl
from jax.experimental.pallas import tpu as pltpu

F32 = jnp.float32
BF16 = jnp.bfloat16
SDS = jax.ShapeDtypeStruct

N_DEV = 8
S = 2048
D = 1024
HEAD_DIM = 64
ROT_DIM = 16
ROPE_THETA = 500000.0
ATT_W = 1024
SSM_W = 512
SSM_GROUPS = 32
SSM_GROUP = 16
SSM_STATE = 64
N_CPLX = SSM_GROUPS * SSM_STATE
POOL_W = 2048
POOL_GROUP = 512
EVEN_IN = 5120
EVEN_OUT = 1536
ODD_IN = 4096
RMS_EPS = 1e-6
LANES = 128
VMEM_LIMIT = 48 * 1024 * 1024

ADAM_LR = 0.001
ADAM_B1 = 0.9
ADAM_B2 = 0.999
ADAM_EPS = 1e-08
ADAM_WD = 0.01
ADAM_STEP = 10

MESH_ID = pl.DeviceIdType.MESH
NN = (((1,), (0,)), ((), ()))
NT = (((1,), (1,)), ((), ()))
TN = (((0,), (0,)), ((), ()))
_DN = {"nn": NN, "nt": NT, "tn": TN}


def _cparams(sem):
    return pltpu.CompilerParams(dimension_semantics=sem, vmem_limit_bytes=VMEM_LIMIT)


MM_TILES = (1024, 768, 512)


def _tile(dim):
    return next((t for t in MM_TILES if dim % t == 0), dim)


NT_BLOCKS_PER_STEP = 4


def _mm(a, b, mode, out_dtype, b_blocks=False, out_blocks=False, a_cols=None, after=()):
    if b_blocks:
        nblk, rows, cb = b.shape
        b2_shape = (rows, nblk * cb)
    else:
        b2_shape = b.shape
    a_shape = a.shape if a_cols is None else (a.shape[0], a_cols[1])
    if mode == "nn":
        (m, k), n = a_shape, b2_shape[1]
    elif mode == "nt":
        (m, k), n = a_shape, b2_shape[0]
    else:
        (k, m), n = a_shape, b2_shape[1]
    tm, tn, tk = _tile(m), _tile(n), _tile(k)
    per_step = 1
    if b_blocks and mode == "nn":
        tn = cb
    if b_blocks and mode == "nt":
        per_step = NT_BLOCKS_PER_STEP
        tk = per_step * cb
        tn = min(tn, MM_TILES[-1])
    if out_blocks:
        tn = n // N_DEV
        tk = k
    nk = k // tk
    a_unit = tm if mode == "tn" else tk
    assert a_cols is None or a_cols[0] % a_unit == 0
    a_off = 0 if a_cols is None else a_cols[0] // a_unit

    def body(a_ref, b_ref, *rest):
        o_ref, acc_ref = rest[-2:]
        kk = pl.program_id(2)
        if per_step == 1:
            part = lax.dot_general(a_ref[...].astype(BF16), b_ref[...].astype(BF16), _DN[mode],
                                   preferred_element_type=F32)
        else:
            part = None
            for blk in range(per_step):
                d = lax.dot_general(a_ref[:, blk * cb:(blk + 1) * cb].astype(BF16), b_ref[blk].astype(BF16), NT,
                                    preferred_element_type=F32)
                part = d if part is None else part + d
        if nk == 1:
            o_ref[...] = part.astype(o_ref.dtype)
            return

        @pl.when(kk == 0)
        def _():
            acc_ref[...] = part

        @pl.when((kk > 0) & (kk < nk - 1))
        def _():
            acc_ref[...] += part

        @pl.when(kk == nk - 1)
        def _():
            o_ref[...] = (acc_ref[...] + part).astype(o_ref.dtype)

    if mode == "nn":
        a_spec = pl.BlockSpec((tm, tk), lambda i, j, kk: (i, a_off + kk))
        b_spec = pl.BlockSpec((tk, tn), lambda i, j, kk: (kk, j))
    elif mode == "nt":
        a_spec = pl.BlockSpec((tm, tk), lambda i, j, kk: (i, a_off + kk))
        b_spec = pl.BlockSpec((tn, tk), lambda i, j, kk: (j, kk))
    else:
        a_spec = pl.BlockSpec((tk, tm), lambda i, j, kk: (kk, a_off + i))
        b_spec = pl.BlockSpec((tk, tn), lambda i, j, kk: (kk, j))
    if b_blocks and mode == "nn":
        b_spec = pl.BlockSpec((None, tk, cb), lambda i, j, kk: (j, kk, 0))
    if b_blocks and mode == "nt":
        b_spec = pl.BlockSpec((per_step, tn, cb), lambda i, j, kk: (kk, j, 0))
    out_spec = pl.BlockSpec((tm, tn), lambda i, j, kk: (i, j))
    out_shape = SDS((m, n), out_dtype)
    if out_blocks:
        out_spec = pl.BlockSpec((None, tm, tn), lambda i, j, kk: (j, i, 0))
        out_shape = SDS((N_DEV, m, tn), out_dtype)
    return pl.pallas_call(
        body, name=f"mm_{mode}_{m}x{k}x{n}",
        grid=(m // tm, n // tn, nk),
        in_specs=[a_spec, b_spec] + [pl.BlockSpec(memory_space=pl.ANY)] * len(after),
        out_specs=out_spec,
        out_shape=out_shape,
        scratch_shapes=[pltpu.VMEM((tm, tn) if nk > 1 else (8, LANES), F32)],
        compiler_params=_cparams(("parallel", "parallel", "arbitrary")),
    )(a, b, *after)


def _gmm(a, b, mode, out_dtype, tm=512):
    ng, gw = POOL_W // POOL_GROUP, POOL_GROUP
    ns = S // tm
    if mode in ("nn", "nt"):
        def body(a_ref, b_ref, o_ref):
            o_ref[...] = lax.dot_general(a_ref[...].astype(BF16), b_ref[...].astype(BF16), _DN[mode],
                                         preferred_element_type=F32).astype(o_ref.dtype)

        return pl.pallas_call(
            body, name=f"gmm_{mode}", grid=(ng, ns),
            in_specs=[pl.BlockSpec((tm, gw), lambda g, i: (i, g)),
                      pl.BlockSpec((None, gw, gw), lambda g, i: (g, 0, 0))],
            out_specs=pl.BlockSpec((tm, gw), lambda g, i: (i, g)),
            out_shape=SDS((S, POOL_W), out_dtype),
            compiler_params=_cparams(("parallel", "parallel")),
        )(a, b)

    def body_tn(a_ref, b_ref, o_ref, acc_ref):
        i = pl.program_id(1)

        @pl.when(i == 0)
        def _():
            acc_ref[...] = jnp.zeros_like(acc_ref)

        acc_ref[...] += lax.dot_general(a_ref[...].astype(BF16), b_ref[...].astype(BF16), TN,
                                        preferred_element_type=F32)

        @pl.when(i == ns - 1)
        def _():
            o_ref[...] = acc_ref[...].astype(o_ref.dtype)

    return pl.pallas_call(
        body_tn, name="gmm_tn", grid=(ng, ns),
        in_specs=[pl.BlockSpec((tm, gw), lambda g, i: (i, g)),
                  pl.BlockSpec((tm, gw), lambda g, i: (i, g))],
        out_specs=pl.BlockSpec((None, gw, gw), lambda g, i: (g, 0, 0)),
        out_shape=SDS((ng, gw, gw), out_dtype),
        scratch_shapes=[pltpu.VMEM((gw, gw), F32)],
        compiler_params=_cparams(("parallel", "arbitrary")),
    )(a, b)


def _rowwise(fn, inputs, out_defs, acc_defs=(), tm=256, name=None, after=()):
    n_in, n_out, n_acc = len(inputs), len(out_defs), len(acc_defs)
    n_after = len(after)
    in_specs, args = [], []
    for arr, width, cb in inputs:
        if arr.shape[0] == 1:
            in_specs.append(pl.BlockSpec((1, width), lambda i, cb=cb: (0, cb)))
        else:
            in_specs.append(pl.BlockSpec((tm, width), lambda i, cb=cb: (i, cb)))
        args.append(arr)
    out_defs = [d if len(d) == 4 else (d[0], d[1], d[0], 0) for d in out_defs]
    out_shape = [SDS((S, ww), dt) for _, dt, ww, _ in out_defs] + [SDS((1, w), F32) for w in acc_defs]
    out_specs = ([pl.BlockSpec((tm, w), lambda i, cb=cb: (i, cb)) for w, _, _, cb in out_defs]
                 + [pl.BlockSpec((1, w), lambda i: (0, 0)) for w in acc_defs])

    def kern(*refs):
        vals = [r[...] for r in refs[:n_in]]
        outs, accs = fn(*vals)
        out_refs = refs[n_in + n_after:]
        for r, v in zip(out_refs[:n_out], outs):
            r[...] = v.astype(r.dtype)
        if n_acc:
            acc_refs = out_refs[n_out:]

            @pl.when(pl.program_id(0) == 0)
            def _():
                for r in acc_refs:
                    r[...] = jnp.zeros_like(r)

            for r, v in zip(acc_refs, accs):
                r[...] += jnp.sum(v, axis=0, keepdims=True)

    res = pl.pallas_call(
        kern, name=name, grid=(S // tm,), in_specs=in_specs + [pl.BlockSpec(memory_space=pl.ANY)] * n_after,
        out_specs=out_specs, out_shape=out_shape, compiler_params=_cparams(("arbitrary",)),
    )(*args, *after)
    return res


def _sigmoid(x):
    return 1.0 / (1.0 + jnp.exp(-x))


def _silu_and_grad(x):
    s = _sigmoid(x)
    return x * s, s * (1.0 + x * (1.0 - s))


_GELU_K = math.sqrt(2.0 / math.pi)
_GELU_C = 0.044715


def _gelu_and_grad(x):
    t = jnp.tanh(_GELU_K * (x + _GELU_C * (x * x * x)))
    cdf = 0.5 * (1.0 + t)
    grad = cdf + 0.5 * x * (1.0 - t * t) * (_GELU_K * (1.0 + 3.0 * _GELU_C * x * x))
    return x * cdf, grad


def _rms(xv, gain):
    r = lax.rsqrt(jnp.mean(xv * xv, axis=-1, keepdims=True) + RMS_EPS)
    return xv * r * gain


def _rms_bwd(dout, xv, gain):
    r = lax.rsqrt(jnp.mean(xv * xv, axis=-1, keepdims=True) + RMS_EPS)
    xhat = xv * r
    dxhat = dout * gain
    dx = r * (dxhat - xhat * jnp.mean(dxhat * xhat, axis=-1, keepdims=True))
    return dx, dout * xhat


def _norm_fwd(x, gain):
    (h,) = _rowwise(lambda xv, g: ((_rms(xv, g),), ()), [(x, D, 0), (gain, D, 0)], [(D, BF16)], name="norm_fwd")
    return h


def _post_fwd(x, y, gain):
    (o,) = _rowwise(lambda xv, yv, g: ((xv + _rms(yv, g),), ()), [(x, D, 0), (y, D, 0), (gain, D, 0)],
                    [(D, F32)], name="post_fwd")
    return o


def _post_bwd(g, y, gain):
    def fn(gv, yv, gn):
        dx, dg = _rms_bwd(gv, yv, gn)
        return (dx,), (dg,)

    return _rowwise(fn, [(g, D, 0), (y, D, 0), (gain, D, 0)], [(D, BF16)], [D], name="post_bwd")


def _pre_bwd(g, dh, x, gain):
    def fn(gv, dhv, xv, gn):
        dx, dg = _rms_bwd(dhv, xv, gn)
        return (gv + dx,), (dg,)

    return _rowwise(fn, [(g, D, 0), (dh, D, 0), (x, D, 0), (gain, D, 0)], [(D, F32)], [D], name="pre_bwd")


def _loss_grad(xo, tgt):
    def fn(xv, tv):
        e = xv - tv
        return (e * (1.0 / D),), (e * e,)

    return _rowwise(fn, [(xo, D, 0), (tgt, D, 0)], [(D, F32)], [D], name="loss_grad")


def _pool(u_arr, col_block, transpose, out_dtype, into=None, tc=256):
    n_t = POOL_W // tc
    per_group = POOL_GROUP // tc

    def body(u_ref, *rest):
        o_ref = rest[-1]
        c = pl.program_id(0)
        grp = c // per_group
        xv = u_ref[...]
        t = lax.broadcasted_iota(jnp.int32, (S, 1), 0)
        win = jnp.left_shift(2, grp)
        cnt = jnp.minimum(t + 1, win).astype(F32)
        cur = xv / cnt if transpose else xv
        sums = []
        for k in (1, 2, 4, 8):
            if transpose:
                sh = jnp.where(t < S - k, pltpu.roll(cur, S - k, 0), 0.0)
            else:
                sh = jnp.where(t >= k, pltpu.roll(cur, k, 0), 0.0)
            cur = cur + sh
            sums.append(cur)
        tot = jnp.where(grp == 0, sums[0], jnp.where(grp == 1, sums[1], jnp.where(grp == 2, sums[2], sums[3])))
        res = tot - xv if transpose else tot / cnt - xv
        o_ref[...] = res.astype(o_ref.dtype)

    in_specs = [pl.BlockSpec((S, tc), lambda c: (0, col_block * n_t + c))]
    args = [u_arr]
    if into is not None:
        in_specs.append(pl.BlockSpec(memory_space=pl.ANY))
        args.append(into)
    return pl.pallas_call(
        body, name="pool_bwd" if transpose else "pool_fwd", grid=(n_t,),
        in_specs=in_specs,
        out_specs=pl.BlockSpec((S, tc), lambda c: (0, c)),
        out_shape=SDS((S, POOL_W) if into is None else into.shape, out_dtype),
        input_output_aliases={} if into is None else {1: 0},
        compiler_params=_cparams(("parallel",)),
    )(*args)


def _rope_tables(zero):
    pos = jnp.arange(S, dtype=jnp.int32).astype(F32) + zero
    inv_freq = ROPE_THETA ** (-jnp.arange(0, ROT_DIM, 2, dtype=F32) / ROT_DIM)
    ang = pos[:, None] * inv_freq[None, :]
    cos8, sin8 = jnp.cos(ang), jnp.sin(ang)
    half = ROT_DIM // 2
    zeros = jnp.zeros((S, HEAD_DIM - ROT_DIM), F32)
    cos = jnp.concatenate([cos8, cos8, jnp.ones((S, HEAD_DIM - ROT_DIM), F32)], axis=1)
    lo = jnp.concatenate([-sin8, jnp.zeros((S, half), F32), zeros], axis=1)
    hi = jnp.concatenate([jnp.zeros((S, half), F32), sin8, zeros], axis=1)
    rep = LANES // HEAD_DIM
    return jnp.tile(cos, (1, rep)), jnp.tile(lo, (1, rep)), jnp.tile(hi, (1, rep))


def _rotate(xv, cos, lo, hi, transpose):
    width = xv.shape[1]
    rep = width // LANES
    wide = lambda tab: jnp.concatenate([tab] * rep, axis=1)
    half = ROT_DIM // 2
    up = pltpu.roll(xv, width - half, 1)
    dn = pltpu.roll(xv, half, 1)
    mixed = up * wide(lo) + dn * wide(hi)
    return xv * wide(cos) - mixed if transpose else xv * wide(cos) + mixed


def _qkv_prep(proj, tables):
    cos, lo, hi = tables

    def fn(x, c, l, h):
        rot = _rotate(x[:, :2 * ATT_W], c, l, h, False)
        return (jnp.concatenate([(rot[:, :ATT_W] * HEAD_DIM ** -0.5).astype(BF16), rot[:, ATT_W:].astype(BF16),
                                 x[:, 2 * ATT_W:].astype(BF16)], axis=1),), ()

    (qkv,) = _rowwise(fn, [(proj, 3 * ATT_W, 0), (cos, LANES, 0), (lo, LANES, 0), (hi, LANES, 0)],
                      [(3 * ATT_W, BF16)], name="qkv_prep")
    return qkv


ATT_T = 512


def _multiplicity(delta):
    ok = delta >= 0
    near = jnp.where(ok & (delta <= 128), 1.0, 0.0)
    mid = jnp.where(ok & (delta <= 512) & ((delta & 3) == 0), 1.0, 0.0)
    far = jnp.where(ok & ((delta & 15) == 0), 1.0, 0.0)
    return near + mid + far


def _attention_bias(zero):
    t = ATT_T
    pos = jnp.arange(t, dtype=jnp.int32) + jnp.asarray(zero).astype(jnp.int32)
    delta = jnp.arange(S // t, dtype=jnp.int32)[:, None, None] * t + pos[None, :, None] - pos[None, None, :]
    mult = _multiplicity(delta)
    return jnp.where(mult > 0.0, jnp.log(jnp.maximum(mult, 1.0)), -1e30).astype(F32)


def _head_split(v, first):
    zero = jnp.zeros_like(v)
    return [jnp.where(first, v, zero), jnp.where(first, zero, v)]


def _flash_fwd(qkv, bias):
    t = ATT_T
    n_hp = ATT_W // LANES

    def body(q_ref, k_ref, v_ref, b_ref, o_ref, lse_ref):
        i = pl.program_id(1)
        first = lax.broadcasted_iota(jnp.int32, (1, LANES), 1) < HEAD_DIM
        qs = _head_split(q_ref[...], first)

        def kv_step(j, carry):
            m0, l0, m1, l1, acc = carry
            off = pl.multiple_of(j * t, t)
            kb = k_ref[pl.ds(off, t), :]
            vs = _head_split(v_ref[pl.ds(off, t), :], first)
            bias_t = b_ref[i - j]
            new = []
            pv = None
            for h, (m_prev, l_prev) in enumerate(((m0, l0), (m1, l1))):
                s = lax.dot_general(qs[h], kb, NT, preferred_element_type=F32) + bias_t
                m_new = jnp.maximum(m_prev, jnp.max(s, axis=1, keepdims=True))
                p = jnp.exp(s - m_new)
                alpha = jnp.exp(m_prev - m_new)
                l_new = alpha * l_prev + jnp.sum(p, axis=1, keepdims=True)
                d = lax.dot_general(p.astype(BF16), vs[h], NN, preferred_element_type=F32)
                pv = d if pv is None else pv + d
                new.append((m_new, l_new, alpha))
            acc = acc * jnp.where(first, new[0][2], new[1][2]) + pv
            return new[0][0], new[0][1], new[1][0], new[1][1], acc

        neg = jnp.full((t, 1), -1e30, F32)
        zero = jnp.zeros((t, 1), F32)
        m0, l0, m1, l1, acc = lax.fori_loop(0, i + 1, kv_step, (neg, zero, neg, zero, jnp.zeros((t, LANES), F32)))
        o_ref[...] = acc * jnp.where(first, 1.0 / l0, 1.0 / l1)
        lse_ref[...] = jnp.where(first, m0 + jnp.log(l0), m1 + jnp.log(l1))

    blk = pl.BlockSpec((t, LANES), lambda hp, i: (i, hp))
    k_full = pl.BlockSpec((S, LANES), lambda hp, i: (0, n_hp + hp))
    v_full = pl.BlockSpec((S, LANES), lambda hp, i: (0, 2 * n_hp + hp))
    return pl.pallas_call(
        body, name="flash_fwd", grid=(n_hp, S // t),
        in_specs=[blk, k_full, v_full, pl.BlockSpec((S // t, t, t), lambda hp, i: (0, 0, 0))], out_specs=[blk, blk],
        out_shape=[SDS((S, ATT_W), F32), SDS((S, ATT_W), F32)],
        compiler_params=_cparams(("parallel", "arbitrary")),
    )(qkv, qkv, qkv, bias)


def _flash_bwd(qkv, o, do, lse, bias, after=()):
    t = ATT_T
    n_hp = ATT_W // LANES
    n_t = S // t

    def body(q_ref, k_ref, v_ref, o_ref, do_ref, lse_ref, b_ref, *rest):
        dq_ref, dk_ref, dv_ref = rest[-3:]
        j = pl.program_id(1)
        first = lax.broadcasted_iota(jnp.int32, (1, LANES), 1) < HEAD_DIM

        @pl.when(j == 0)
        def _():
            dq_ref[...] = jnp.zeros_like(dq_ref)

        kb = k_ref[...]
        vb = v_ref[...]
        ks = _head_split(kb, first)

        def q_step(i, carry):
            dk_acc, dv_acc = carry
            rows = pl.ds(pl.multiple_of(i * t, t), t)
            qs = _head_split(q_ref[rows, :], first)
            dob = do_ref[rows, :]
            prod = dob * o_ref[rows, :]
            d_all = jnp.sum(prod, axis=1, keepdims=True)
            d0 = jnp.sum(jnp.where(first, prod, 0.0), axis=1, keepdims=True)
            lse_b = lse_ref[rows, :]
            lse0 = jnp.max(jnp.where(first, lse_b, -jnp.inf), axis=1, keepdims=True)
            lse1 = jnp.max(jnp.where(first, -jnp.inf, lse_b), axis=1, keepdims=True)
            dos = _head_split(dob.astype(BF16), first)
            bias_t = b_ref[i - j]
            dq_t = jnp.zeros((t, LANES), F32)
            for h, (lse_h, d_h) in enumerate(((lse0, d0), (lse1, d_all - d0))):
                s = lax.dot_general(qs[h], kb, NT, preferred_element_type=F32)
                p = jnp.exp(s + (bias_t - lse_h))
                dp = lax.dot_general(dos[h], vb, NT, preferred_element_type=F32)
                ds = (p * (dp - d_h)).astype(BF16)
                dv_acc = dv_acc + lax.dot_general(p.astype(BF16), dos[h], TN, preferred_element_type=F32)
                dk_acc = dk_acc + lax.dot_general(ds, qs[h], TN, preferred_element_type=F32)
                dq_t = dq_t + lax.dot_general(ds, ks[h], NN, preferred_element_type=F32)
            dq_ref[rows, :] += dq_t
            return dk_acc, dv_acc

        zero = jnp.zeros((t, LANES), F32)
        dk_acc, dv_acc = lax.fori_loop(j, n_t, q_step, (zero, zero))
        dk_ref[...] = dk_acc
        dv_ref[...] = dv_acc

    blk = pl.BlockSpec((t, LANES), lambda hp, j: (j, hp))
    full = pl.BlockSpec((S, LANES), lambda hp, j: (0, hp))
    k_blk = pl.BlockSpec((t, LANES), lambda hp, j: (j, n_hp + hp))
    v_blk = pl.BlockSpec((t, LANES), lambda hp, j: (j, 2 * n_hp + hp))
    return pl.pallas_call(
        body, name="flash_bwd", grid=(n_hp, n_t),
        in_specs=([full, k_blk, v_blk, full, full, full, pl.BlockSpec((n_t, t, t), lambda hp, j: (0, 0, 0))]
                  + [pl.BlockSpec(memory_space=pl.ANY)] * len(after)),
        out_specs=[full, blk, blk],
        out_shape=[SDS((S, ATT_W), F32)] * 3,
        compiler_params=_cparams(("parallel", "arbitrary")),
    )(qkv, qkv, qkv, o, do, lse, bias, *after)


SCAN_T = 256
ST_ROWS = 2 * N_CPLX // LANES
HALF = ST_ROWS // 2


def _scan_fwd(lam, bu):
    def body(lam_ref, bu_ref, st_ref, carry):
        @pl.when(pl.program_id(0) == 0)
        def _():
            carry[...] = jnp.zeros_like(carry)

        ar, ai = lam_ref[0:HALF, :], lam_ref[HALF:ST_ROWS, :]

        def step(t, c):
            sr, si = c
            b = bu_ref[t]
            nr = ar * sr - ai * si + b[0:HALF]
            ni = ar * si + ai * sr + b[HALF:ST_ROWS]
            st_ref[t, 0:HALF, :] = nr
            st_ref[t, HALF:ST_ROWS, :] = ni
            return nr, ni

        sr, si = lax.fori_loop(0, SCAN_T, step, (carry[0:HALF, :], carry[HALF:ST_ROWS, :]), unroll=8)
        carry[0:HALF, :] = sr
        carry[HALF:ST_ROWS, :] = si

    blk = pl.BlockSpec((SCAN_T, ST_ROWS, LANES), lambda i: (i, 0, 0))
    return pl.pallas_call(
        body, name="scan_fwd", grid=(S // SCAN_T,),
        in_specs=[pl.BlockSpec((ST_ROWS, LANES), lambda i: (0, 0)), blk], out_specs=blk,
        out_shape=SDS((S, ST_ROWS, LANES), F32),
        scratch_shapes=[pltpu.VMEM((ST_ROWS, LANES), F32)],
        compiler_params=_cparams(("arbitrary",)),
    )(lam, bu)


def _scan_bwd(lam, dst, states):
    n_blk = S // SCAN_T

    def body(lam_ref, d_ref, st_ref, g_ref, dlam_ref, carry):
        @pl.when(pl.program_id(0) == 0)
        def _():
            carry[...] = jnp.zeros_like(carry)
            dlam_ref[...] = jnp.zeros_like(dlam_ref)

        ar, ai = lam_ref[0:HALF, :], lam_ref[HALF:ST_ROWS, :]

        def step(kk, c):
            t = SCAN_T - 1 - kk
            gr, gi, dar, dai = c
            x = st_ref[t]
            xr, xi = x[0:HALF], x[HALF:ST_ROWS]
            dar = dar + gr * xr + gi * xi
            dai = dai + gi * xr - gr * xi
            d = d_ref[t]
            ngr = d[0:HALF] + ar * gr + ai * gi
            ngi = d[HALF:ST_ROWS] + ar * gi - ai * gr
            g_ref[t, 0:HALF, :] = ngr
            g_ref[t, HALF:ST_ROWS, :] = ngi
            return ngr, ngi, dar, dai

        zero = jnp.zeros((HALF, LANES), F32)
        gr, gi, dar, dai = lax.fori_loop(0, SCAN_T, step, (carry[0:HALF, :], carry[HALF:ST_ROWS, :], zero, zero),
                                         unroll=8)
        carry[0:HALF, :] = gr
        carry[HALF:ST_ROWS, :] = gi
        dlam_ref[0:HALF, :] += dar
        dlam_ref[HALF:ST_ROWS, :] += dai

    blk = pl.BlockSpec((SCAN_T, ST_ROWS, LANES), lambda i: (n_blk - 1 - i, 0, 0))
    small = pl.BlockSpec((ST_ROWS, LANES), lambda i: (0, 0))
    return pl.pallas_call(
        body, name="scan_bwd", grid=(n_blk,),
        in_specs=[small, blk, blk], out_specs=[blk, small],
        out_shape=[SDS((S, ST_ROWS, LANES), F32), SDS((ST_ROWS, LANES), F32)],
        scratch_shapes=[pltpu.VMEM((ST_ROWS, LANES), F32)],
        compiler_params=_cparams(("arbitrary",)),
    )(lam, dst, states)


def _ssm_prep(a_re, a_im, log_dt, b_re, b_im, c_re, c_im):
    lam = lax.complex(a_re, a_im)
    dt = jnp.exp(log_dt)[:, None]
    lam_bar = jnp.exp(lam * dt)
    b_bar = ((lam_bar - 1.0) / lam)[..., None] * lax.complex(b_re, b_im)
    lam_t = jnp.concatenate([jnp.real(lam_bar).reshape(HALF, LANES), jnp.imag(lam_bar).reshape(HALF, LANES)], axis=0)
    groups_per_super = SSM_GROUPS // SSM_SUPER
    on_diag = ((lax.broadcasted_iota(jnp.int32, (SSM_W, SB_COLS), 0) // SSM_GROUP) % groups_per_super
               == lax.broadcasted_iota(jnp.int32, (SSM_W, SB_COLS), 1) // SSM_STATE)

    def compact(m):
        return jnp.where(on_diag, jnp.tile(m.reshape(SSM_W, SSM_STATE), (1, groups_per_super)), 0.0)

    w_b = jnp.concatenate([compact(jnp.real(b_bar).transpose(0, 2, 1)),
                           compact(jnp.imag(b_bar).transpose(0, 2, 1))], axis=1)
    w_ct = jnp.concatenate([compact(c_re), -compact(c_im)], axis=1)
    return lam_t, w_b, w_ct


SSM_SUPER = 4
SB_ROWS = SSM_W // SSM_SUPER
SB_COLS = N_CPLX // SSM_SUPER


def _bdmm(a, w, mode, out_dtype, a_cols=None):
    a_off = 0 if a_cols is None else a_cols[0] // SB_ROWS
    if mode == "nn":
        def body(a_ref, w_ref, o_ref):
            o_ref[...] = lax.dot_general(a_ref[...].astype(BF16), w_ref[...].astype(BF16), NN,
                                         preferred_element_type=F32).astype(o_ref.dtype)

        return pl.pallas_call(
            body, name="bdmm_nn", grid=(2, SSM_SUPER),
            in_specs=[pl.BlockSpec((S, SB_ROWS), lambda h, b: (0, a_off + b)),
                      pl.BlockSpec((SB_ROWS, SB_COLS), lambda h, b: (b, h))],
            out_specs=pl.BlockSpec((S, SB_COLS), lambda h, b: (0, h * SSM_SUPER + b)),
            out_shape=SDS((S, 2 * N_CPLX), out_dtype),
            compiler_params=_cparams(("parallel", "parallel")),
        )(a, w)
    if mode == "nt":
        def body(re_ref, im_ref, wre_ref, wim_ref, o_ref):
            acc = lax.dot_general(re_ref[...].astype(BF16), wre_ref[...].astype(BF16), NT, preferred_element_type=F32)
            acc += lax.dot_general(im_ref[...].astype(BF16), wim_ref[...].astype(BF16), NT, preferred_element_type=F32)
            o_ref[...] = acc.astype(o_ref.dtype)

        return pl.pallas_call(
            body, name="bdmm_nt", grid=(SSM_SUPER,),
            in_specs=[pl.BlockSpec((S, SB_COLS), lambda b: (0, b)),
                      pl.BlockSpec((S, SB_COLS), lambda b: (0, SSM_SUPER + b)),
                      pl.BlockSpec((SB_ROWS, SB_COLS), lambda b: (b, 0)),
                      pl.BlockSpec((SB_ROWS, SB_COLS), lambda b: (b, 1))],
            out_specs=pl.BlockSpec((S, SB_ROWS), lambda b: (0, b)),
            out_shape=SDS((S, SSM_W), out_dtype),
            compiler_params=_cparams(("parallel",)),
        )(a, a, w, w)

    def body_tn(a_ref, w_ref, o_ref):
        o_ref[...] = lax.dot_general(a_ref[...].astype(BF16), w_ref[...].astype(BF16), TN,
                                     preferred_element_type=F32).astype(o_ref.dtype)

    return pl.pallas_call(
        body_tn, name="bdmm_tn", grid=(2, SSM_SUPER),
        in_specs=[pl.BlockSpec((S, SB_ROWS), lambda h, b: (0, a_off + b)),
                  pl.BlockSpec((S, SB_COLS), lambda h, b: (0, h * SSM_SUPER + b))],
        out_specs=pl.BlockSpec((SB_ROWS, SB_COLS), lambda h, b: (b, h)),
        out_shape=SDS((SSM_W, 2 * SB_COLS), out_dtype),
        compiler_params=_cparams(("parallel", "parallel")),
    )(a, w)


U_SSM_COLS = (4 * ATT_W, SSM_W)


def _row(v):
    return v.reshape(1, -1)


def _even_fwd(x, pre, post, w_in, late_w, glu_b, ssm_d, prep, tables):
    lam_t, w_b, w_ct = prep
    h = _norm_fwd(x, pre)
    proj = _mm(h, w_in, "nn", F32, b_blocks=True)
    qkv = _qkv_prep(proj, tables[:3])
    w_out, glu_w = late_w(qkv)
    att, lse = _flash_fwd(qkv, tables[3])
    bu = _bdmm(proj, w_b, "nn", F32, a_cols=U_SSM_COLS)
    states = _scan_fwd(lam_t, bu.reshape(S, ST_ROWS, LANES))
    y = _bdmm(states.reshape(S, 2 * N_CPLX), w_ct, "nt", F32)

    def act1(yv, uv, dv):
        return (_gelu_and_grad(yv + dv * uv)[0],), ()

    (z1,) = _rowwise(act1, [(y, SSM_W, 0), (proj, SSM_W, 8), (ssm_d, SSM_W, 0)], [(SSM_W, F32)], name="ssm_act_fwd")
    lin = _mm(z1, glu_w, "nn", F32)

    def gate(att_v, ga, gs, z1v, linv, bv):
        ssm_out = z1v * _sigmoid(linv + bv)
        return (jnp.concatenate([att_v * _silu_and_grad(ga)[0], ssm_out * _silu_and_grad(gs)[0]], axis=1),), ()

    (merged,) = _rowwise(gate, [(att, ATT_W, 0), (proj, ATT_W, 3), (proj, SSM_W, 9), (z1, SSM_W, 0),
                                (lin, SSM_W, 0), (glu_b, SSM_W, 0)], [(EVEN_OUT, BF16)], name="even_gate_fwd")
    yout = _mm(merged, w_out, "nn", F32)
    x_next = _post_fwd(x, yout, post)
    saved = (x, h, proj, qkv, att, lse, states, y, z1, lin, merged, yout, w_out, glu_w)
    return x_next, saved


def _even_bwd(g, saved, pre, post, w_in, late_w, glu_b, ssm_d, prep, tables, on_w, on_ssm):
    x, h, proj, qkv, att, lse, states, y, z1, lin, merged, yout, w_out, glu_w = saved
    lam_t, w_b, w_ct = prep
    dyout, dpost = _post_bwd(g, yout, post)
    dmerged = _mm(dyout, w_out, "nt", F32)
    dw_out = _mm(merged, dyout, "tn", BF16)

    def gate_bwd(dm_a, dm_s, att_v, ga, gs, z1v, linv, bv):
        sa, dsa = _silu_and_grad(ga)
        ss, dss = _silu_and_grad(gs)
        sig = _sigmoid(linv + bv)
        ssm_out = z1v * sig
        dssm = dm_s * ss
        dlin = dssm * z1v * sig * (1.0 - sig)
        return (dm_a * sa, dm_a * att_v * dsa, dm_s * ssm_out * dss, dssm * sig, dlin), (dlin,)

    datt, dg_att, dg_ssm, dz1a, dlin, dglu_b = _rowwise(
        gate_bwd, [(dmerged, ATT_W, 0), (dmerged, SSM_W, 2), (att, ATT_W, 0), (proj, ATT_W, 3), (proj, SSM_W, 9),
                   (z1, SSM_W, 0), (lin, SSM_W, 0), (glu_b, SSM_W, 0)],
        [(ATT_W, F32), (ATT_W, BF16), (SSM_W, BF16), (SSM_W, F32), (SSM_W, BF16)], [SSM_W], name="even_gate_bwd")
    dz1b = _mm(dlin, glu_w, "nt", F32)
    dglu_w = _mm(z1, dlin, "tn", BF16)

    def act1_bwd(da, db, yv, uv, dv):
        dpre = (da + db) * _gelu_and_grad(yv + dv * uv)[1]
        return (dpre, dpre * dv), (dpre * uv,)

    sent_late_w = on_w(dict(w_out=dw_out, glu_w=dglu_w))
    dy, du_direct, dd = _rowwise(act1_bwd, [(dz1a, SSM_W, 0), (dz1b, SSM_W, 0), (y, SSM_W, 0), (proj, SSM_W, 8),
                                            (ssm_d, SSM_W, 0)], [(SSM_W, BF16), (SSM_W, F32)], [SSM_W],
                                 name="ssm_act_bwd", after=(sent_late_w,))
    dst = _bdmm(dy, w_ct, "nn", F32)
    dw_ct = _bdmm(dy, states.reshape(S, 2 * N_CPLX), "tn", F32)
    gst, dlam = _scan_bwd(lam_t, dst.reshape(S, ST_ROWS, LANES), states)
    dbu = gst.reshape(S, 2 * N_CPLX)
    du_state = _bdmm(dbu, w_b, "nt", F32)
    dw_b = _bdmm(proj, dbu, "tn", F32, a_cols=U_SSM_COLS)
    sent_ssm = on_ssm((dlam, dw_b, dw_ct))
    dq, dk, dv = _flash_bwd(qkv, att, datt, lse, tables[3], after=() if sent_ssm is None else (sent_ssm,))

    def assemble(dqv, dkv, dvv, dga, dua, dub, dgs, c, l, h):
        rot = _rotate(jnp.concatenate([dqv, dkv], axis=1), c, l, h, True)
        return (jnp.concatenate([(rot[:, :ATT_W] * HEAD_DIM ** -0.5).astype(BF16), rot[:, ATT_W:].astype(BF16),
                                 dvv.astype(BF16), dga, (dua + dub).astype(BF16), dgs], axis=1),), ()

    (dproj,) = _rowwise(assemble, [(dq, ATT_W, 0), (dk, ATT_W, 0), (dv, ATT_W, 0), (dg_att, ATT_W, 0),
                                   (du_state, SSM_W, 0), (du_direct, SSM_W, 0), (dg_ssm, SSM_W, 0),
                                   (tables[0], LANES, 0), (tables[1], LANES, 0), (tables[2], LANES, 0)],
                        [(EVEN_IN, BF16)], name="dproj_assemble")
    dw_in = _mm(h, dproj, "tn", BF16, out_blocks=True)
    sent = on_w(dict(w_in=dw_in))
    dh = _mm(dproj, w_in, "nt", F32, b_blocks=True, after=(sent,))
    g_prev, dpre = _pre_bwd(g, dh, x, pre)
    return g_prev, dict(pre=dpre, post=dpost, glu_b=dglu_b, ssm_d=dd)


def _odd_fwd(x, pre, post, w_in, pool_w, pool_scale, w_out):
    h = _norm_fwd(x, pre)
    proj = _mm(h, w_in, "nn", F32, b_blocks=True)
    mixed = _pool(proj, 0, False, BF16)
    ylin = _gmm(mixed, pool_w, "nn", F32)

    def gate(yl, gt, sc):
        return (yl * sc * _silu_and_grad(gt)[0],), ()

    (z,) = _rowwise(gate, [(ylin, POOL_W, 0), (proj, POOL_W, 1), (pool_scale, POOL_W, 0)], [(POOL_W, BF16)],
                    name="odd_gate_fwd")
    yout = _mm(z, w_out, "nn", F32)
    x_next = _post_fwd(x, yout, post)
    return x_next, (x, h, proj, mixed, ylin, z, yout)


def _odd_bwd(g, saved, pre, post, w_in, pool_w, pool_scale, w_out, on_w):
    x, h, proj, mixed, ylin, z, yout = saved
    dyout, dpost = _post_bwd(g, yout, post)
    dz = _mm(dyout, w_out, "nt", F32)
    dw_out = _mm(z, dyout, "tn", BF16)

    def gate_bwd(dzv, yl, gt, sc):
        sg, dsg = _silu_and_grad(gt)
        tt = dzv * sg
        return (tt * sc, dzv * yl * sc * dsg), (tt * yl,)

    dylin, dproj_gate, dscale = _rowwise(gate_bwd, [(dz, POOL_W, 0), (ylin, POOL_W, 0), (proj, POOL_W, 1),
                                                    (pool_scale, POOL_W, 0)],
                                         [(POOL_W, BF16), (POOL_W, BF16, ODD_IN, 1)], [POOL_W], name="odd_gate_bwd")
    dmixed = _gmm(dylin, pool_w, "nt", F32)
    dpool_w = _gmm(mixed, dylin, "tn", BF16)
    dproj = _pool(dmixed, 0, True, BF16, into=dproj_gate)
    dw_in = _mm(h, dproj, "tn", BF16, out_blocks=True)
    sent = on_w(dict(w_in=dw_in, w_out=dw_out, pool_w=dpool_w))
    dh = _mm(dproj, w_in, "nt", F32, b_blocks=True, after=(sent,))
    g_prev, dpre = _pre_bwd(g, dh, x, pre)
    return g_prev, dict(pre=dpre, post=dpost, pool_scale=dscale)


def _my_index():
    return 4 * lax.axis_index("x") + 2 * lax.axis_index("y") + lax.axis_index("c")


HBM_SPEC = pl.BlockSpec(memory_space=pltpu.HBM)
SEM_SPEC = pl.BlockSpec(memory_space=pltpu.SEMAPHORE)
SPLIT_EFFECT = pltpu.SideEffectType.DATAFLOW_SIDE_EFFECTING


def _device_of(j):
    return (j // 4, (j // 2) % 2, j % 2)


def _split_copy(srcs, lands, send_sems, recv_sems, gather, i, j, dst_slot, recv_slot):
    return pltpu.make_async_remote_copy(
        src_ref=srcs[i] if gather else srcs[i].at[j], dst_ref=lands[i].at[dst_slot],
        send_sem=send_sems.at[i * N_DEV + j], recv_sem=recv_sems.at[i * N_DEV + recv_slot],
        device_id=_device_of(j), device_id_type=MESH_ID)


def _own_copy(srcs, lands, send_sems, gather, i, me):
    return pltpu.make_async_copy(srcs[i] if gather else srcs[i].at[me], lands[i].at[me], send_sems.at[i * N_DEV + me])


def _xchg_start(name, srcs, gather, after=()):
    n = len(srcs)
    n_in = n + len(after)

    def body(*refs):
        src_refs = refs[:n]
        send_sems, recv_sems, token = refs[n_in], refs[n_in + 1], refs[-1]
        land_refs = refs[n_in + 2 + n:n_in + 2 + 2 * n]
        me = _my_index()
        for j in range(N_DEV):
            @pl.when(me != j)
            def _(j=j):
                for i in range(n):
                    _split_copy(src_refs, land_refs, send_sems, recv_sems, gather, i, j, me, me).start()
        for i in range(n):
            _own_copy(src_refs, land_refs, send_sems, gather, i, me).start()
        token[...] = jnp.zeros_like(token)

    land_shapes = [((N_DEV,) + a.shape) if gather else a.shape for a in srcs]
    thru = ([pltpu.HBM(a.shape, a.dtype) for a in srcs] + [pltpu.HBM(s, a.dtype) for s, a in zip(land_shapes, srcs)])
    res = pl.pallas_call(
        body, name=name,
        out_shape=(pltpu.SemaphoreType.DMA((n * N_DEV,)), pltpu.SemaphoreType.DMA((n * N_DEV,)), *thru,
                   SDS((8, LANES), F32)),
        in_specs=[HBM_SPEC] * n + [pl.BlockSpec(memory_space=pl.ANY)] * len(after),
        out_specs=(SEM_SPEC, SEM_SPEC, *([HBM_SPEC] * (2 * n)), pl.BlockSpec(memory_space=pltpu.VMEM)),
        input_output_aliases={i: 2 + i for i in range(n)},
        compiler_params=pltpu.CompilerParams(has_side_effects=SPLIT_EFFECT),
    )(*[pltpu.with_memory_space_constraint(a, pltpu.HBM) for a in srcs], *after)
    return res[0], res[1], list(res[2:2 + n]), list(res[2 + n:2 + 2 * n]), res[-1]


def _xchg_wait(name, started, gather, after):
    send_sems, recv_sems, srcs, lands, _ = started
    n = len(srcs)

    def body(*refs):
        src_refs, land_refs = refs[:n], refs[n:2 * n]
        send_r, recv_r = refs[2 * n], refs[2 * n + 1]
        me = _my_index()
        for j in range(N_DEV):
            @pl.when(me != j)
            def _(j=j):
                for i in range(n):
                    _split_copy(src_refs, land_refs, send_r, recv_r, gather, i, j, me, me).wait_send()
                    _split_copy(src_refs, land_refs, send_r, recv_r, gather, i, j, j, j).wait_recv()
        for i in range(n):
            _own_copy(src_refs, land_refs, send_r, gather, i, me).wait()

    thru = [pltpu.HBM(a.shape, a.dtype) for a in list(srcs) + list(lands)]
    res = pl.pallas_call(
        body, name=name, out_shape=tuple(thru),
        in_specs=[HBM_SPEC] * (2 * n) + [SEM_SPEC, SEM_SPEC] + [pl.BlockSpec(memory_space=pl.ANY)] * len(after),
        out_specs=tuple([HBM_SPEC] * (2 * n)),
        input_output_aliases={i: i for i in range(2 * n)},
        compiler_params=pltpu.CompilerParams(has_side_effects=SPLIT_EFFECT),
    )(*srcs, *lands, send_sems, recv_sems, *after)
    return list(res[n:])


def _adam_layers(w, slot_list, m, v, name):
    n_l, r, c = w.shape
    ns = slot_list[0].shape[0]
    tr = r
    while tr * c * 4 > (1 << 20) and tr % 16 == 0:
        tr //= 2
    assert r % tr == 0 and len(slot_list) == n_l

    def body(*refs):
        w_ref, slot_refs = refs[0], refs[1:1 + n_l]
        m_ref, v_ref, go_ref, d_ref, mo_ref, vo_ref = refs[1 + n_l:]
        layer = pl.program_id(0)
        g = None
        for l, g_ref in enumerate(slot_refs):
            gl = g_ref[0].astype(F32)
            for s in range(1, ns):
                gl = gl + g_ref[s].astype(F32)
            g = gl if g is None else jnp.where(layer == l, gl, g)
        mn = ADAM_B1 * m_ref[...] + (1.0 - ADAM_B1) * g
        vn = ADAM_B2 * v_ref[...] + (1.0 - ADAM_B2) * (g * g)
        m_hat = mn / (1.0 - ADAM_B1 ** ADAM_STEP)
        v_hat = vn / (1.0 - ADAM_B2 ** ADAM_STEP)
        go_ref[...] = g
        d_ref[...] = -ADAM_LR * (m_hat / (jnp.sqrt(v_hat) + ADAM_EPS) + ADAM_WD * w_ref[...])
        mo_ref[...] = mn
        vo_ref[...] = vn

    blk = pl.BlockSpec((None, tr, c), lambda l, i: (l, i, 0))
    slot_specs = [pl.BlockSpec((ns, tr, c), lambda l, i, k=k: (0, jnp.where(l == k, i, 0), 0)) for k in range(n_l)]
    return pl.pallas_call(
        body, name=name, grid=(n_l, r // tr),
        in_specs=[blk] + slot_specs + [blk, blk],
        out_specs=[blk] * 4, out_shape=[SDS((n_l, r, c), F32)] * 4,
        compiler_params=_cparams(("arbitrary", "arbitrary")),
    )(w, *slot_list, m, v)


def _adam(w, gslots, m, v, name):
    r, c = w.shape
    ns = gslots.shape[0]
    tr = r
    while tr * c * 4 > (1 << 20) and tr % 16 == 0:
        tr //= 2
    assert r % tr == 0

    def body(w_ref, g_ref, m_ref, v_ref, go_ref, d_ref, mo_ref, vo_ref):
        g = g_ref[0].astype(F32)
        for s in range(1, ns):
            g = g + g_ref[s].astype(F32)
        wv = w_ref[...]
        mn = ADAM_B1 * m_ref[...] + (1.0 - ADAM_B1) * g
        vn = ADAM_B2 * v_ref[...] + (1.0 - ADAM_B2) * (g * g)
        m_hat = mn / (1.0 - ADAM_B1 ** ADAM_STEP)
        v_hat = vn / (1.0 - ADAM_B2 ** ADAM_STEP)
        go_ref[...] = g
        d_ref[...] = -ADAM_LR * (m_hat / (jnp.sqrt(v_hat) + ADAM_EPS) + ADAM_WD * wv)
        mo_ref[...] = mn
        vo_ref[...] = vn

    blk = pl.BlockSpec((tr, c), lambda i: (i, 0))
    return pl.pallas_call(
        body, name=name, grid=(r // tr,),
        in_specs=[blk, pl.BlockSpec((ns, tr, c), lambda i: (0, i, 0)), blk, blk],
        out_specs=[blk] * 4, out_shape=[SDS((r, c), F32)] * 4,
        compiler_params=_cparams(("parallel",)),
    )(w, gslots, m, v)


def _sum_slots(slots, name):
    ns, r, c = slots.shape

    def body(g_ref, o_ref):
        g = g_ref[0]
        for s in range(1, ns):
            g = g + g_ref[s]
        o_ref[...] = g

    return pl.pallas_call(
        body, name=name, grid=(1,),
        in_specs=[pl.BlockSpec((ns, r, c), lambda i: (0, 0, 0))], out_specs=pl.BlockSpec((r, c), lambda i: (0, 0)),
        out_shape=SDS((r, c), F32), compiler_params=_cparams(("arbitrary",)),
    )(slots)


def _adam_params(params, name):
    n = len(params)

    def body(*refs):
        ins, outs = refs[:5 * n], refs[5 * n:]
        for p in range(n):
            w_ref, m_ref, v_ref, g_first, g_rest = ins[5 * p:5 * p + 5]
            go_ref, d_ref, mo_ref, vo_ref = outs[4 * p:4 * p + 4]
            for part, g_ref in ((slice(0, 1), g_first), (slice(1, w_ref.shape[0]), g_rest)):
                g = g_ref[...]
                mn = ADAM_B1 * m_ref[part] + (1.0 - ADAM_B1) * g
                vn = ADAM_B2 * v_ref[part] + (1.0 - ADAM_B2) * (g * g)
                m_hat = mn / (1.0 - ADAM_B1 ** ADAM_STEP)
                v_hat = vn / (1.0 - ADAM_B2 ** ADAM_STEP)
                go_ref[part] = g
                d_ref[part] = -ADAM_LR * (m_hat / (jnp.sqrt(v_hat) + ADAM_EPS) + ADAM_WD * w_ref[part])
                mo_ref[part] = mn
                vo_ref[part] = vn

    def whole(a):
        return pl.BlockSpec(a.shape, lambda i, nd=a.ndim: (0,) * nd)

    flat = [pltpu.with_memory_space_constraint(a, pltpu.HBM) for prm in params for a in prm]
    outs = pl.pallas_call(
        body, name=name, grid=(1,),
        in_specs=[whole(a) for a in flat],
        out_specs=[whole(prm[0]) for prm in params for _ in range(4)],
        out_shape=[SDS(prm[0].shape, F32) for prm in params for _ in range(4)],
        compiler_params=_cparams(("arbitrary",)),
    )(*flat)
    return [outs[4 * p:4 * p + 4] for p in range(n)]


SMALL_NAMES = ("pre_norm", "post_norm", "ssm_a_re", "ssm_a_im", "ssm_log_dt", "ssm_b_re", "ssm_b_im", "ssm_c_re",
               "ssm_c_im", "ssm_d", "ssm_glu_b")
SSM_NAMES = ("ssm_a_re", "ssm_a_im", "ssm_log_dt", "ssm_b_re", "ssm_b_im", "ssm_c_re", "ssm_c_im")
SHARDED_NAMES = ("even_w_in", "even_w_out", "ssm_glu_w", "odd_w_in", "pool_w", "odd_w_out")
WEIGHT_ORDER = ("pre_norm", "post_norm", "even_w_in", "even_w_out", "ssm_a_re", "ssm_a_im", "ssm_log_dt", "ssm_b_re",
                "ssm_b_im", "ssm_c_re", "ssm_c_im", "ssm_d", "ssm_glu_w", "ssm_glu_b", "odd_w_in", "pool_w",
                "pool_scale", "odd_w_out")
PACK_ROWS_ALIGN = 8


def _pack(parts):
    flat = jnp.concatenate([p.reshape(-1).astype(F32) for p in parts])
    rows = -(-flat.shape[0] // (LANES * PACK_ROWS_ALIGN)) * PACK_ROWS_ALIGN
    return jnp.pad(flat, (0, rows * LANES - flat.shape[0])).reshape(rows, LANES)


def _unpack(packed, shapes):
    flat = packed.reshape(-1)
    out, off = [], 0
    for shp in shapes:
        size = math.prod(shp)
        out.append(flat[off:off + size].reshape(shp))
        off += size
    return out


EVEN_SHARDED = ("w_in", "w_out", "glu_w")
ODD_SHARDED = ("w_in", "pool_w", "w_out")
FAMILY = {(0, "w_in"): "even_w_in", (0, "w_out"): "even_w_out", (0, "glu_w"): "ssm_glu_w",
          (1, "w_in"): "odd_w_in", (1, "pool_w"): "pool_w", (1, "w_out"): "odd_w_out"}


def _sharded_keys(layer):
    return EVEN_SHARDED if layer % 2 == 0 else ODD_SHARDED


def _local_step(x, tgt, small, get_weights, on_w, on_ssm, on_grads, zero=0.0):
    tables = _rope_tables(zero) + (_attention_bias(zero),)
    preps, prep_vjps = [], []
    for i in range(2):
        out, vjp = jax.vjp(_ssm_prep, small["ssm_a_re"][i] + zero, small["ssm_a_im"][i], small["ssm_log_dt"][i],
                           small["ssm_b_re"][i], small["ssm_b_im"][i], small["ssm_c_re"][i], small["ssm_c_im"][i])
        preps.append(out)
        prep_vjps.append(vjp)

    def layer_args(layer, wts):
        i = layer // 2
        pre, post = _row(small["pre_norm"][layer]) + wts.get("token", 0.0), _row(small["post_norm"][layer])
        if layer % 2 == 0:
            return (pre, post, wts["w_in"], wts["late"], _row(small["ssm_glu_b"][i]), _row(small["ssm_d"][i]),
                    preps[i], tables)
        return (pre, post, wts["w_in"], wts["pool_w"], _row(wts["pool_scale"]), wts["w_out"])

    saved, args = [], []
    cur = x
    for layer in range(4):
        after = (cur,) if layer else (cur, tables[0], tables[3], preps[0][1], preps[0][2], preps[1][1], preps[1][2])
        args.append(layer_args(layer, get_weights(layer, after)))
        cur, sv = (_even_fwd if layer % 2 == 0 else _odd_fwd)(cur, *args[layer])
        saved.append(sv)
    g, sq = _loss_grad(cur, tgt)
    loss = 0.5 * jnp.sum(sq) / D

    lg = [None] * 4
    token = jnp.zeros((), F32)
    for layer in reversed(range(4)):
        largs = list(args[layer])
        largs[1] = largs[1] + token
        hooks = dict(on_w=functools.partial(on_w, layer))
        ssm_grads = []
        if layer % 2 == 0:
            def ssm_hook(cotangents, layer=layer):
                ssm_grads.append(prep_vjps[layer // 2](cotangents))
                return on_ssm(layer, ssm_grads[0])

            hooks["on_ssm"] = ssm_hook
        g, lg[layer] = (_even_bwd if layer % 2 == 0 else _odd_bwd)(g, saved[layer], *largs, **hooks)
        if ssm_grads:
            lg[layer]["ssm"] = ssm_grads[0]
        token = on_grads(layer, lg[layer])
    return loss, g, token


def _to_slots(key, gfull):
    if key == "w_in":
        return gfull
    if key in ("w_out", "glu_w"):
        rr, nn = gfull.shape
        return gfull.reshape(N_DEV, rr // N_DEV, nn)
    assert key == "pool_w"
    gg, rr, nn = gfull.shape
    return gfull.reshape(gg, N_DEV, rr // N_DEV, nn).transpose(1, 0, 2, 3)


def _from_gathered(key, gat):
    if key == "w_in":
        return gat
    if key in ("w_out", "glu_w"):
        _, rr, nn = gat.shape
        return gat.reshape(N_DEV * rr, nn)
    assert key == "pool_w"
    _, gg, rr, nn = gat.shape
    return gat.transpose(1, 0, 2, 3).reshape(gg, N_DEV * rr, nn)


def kernel(x, pre_norm, post_norm, even_w_in, even_w_out, ssm_a_re, ssm_a_im, ssm_log_dt, ssm_b_re, ssm_b_im, ssm_c_re, ssm_c_im, ssm_d, ssm_glu_w, ssm_glu_b, odd_w_in, pool_w, pool_scale, odd_w_out, loss_target, m_pre_norm, m_post_norm, m_even_w_in, m_even_w_out, m_ssm_a_re, m_ssm_a_im, m_ssm_log_dt, m_ssm_b_re, m_ssm_b_im, m_ssm_c_re, m_ssm_c_im, m_ssm_d, m_ssm_glu_w, m_ssm_glu_b, m_odd_w_in, m_pool_w, m_pool_scale, m_odd_w_out, v_pre_norm, v_post_norm, v_even_w_in, v_even_w_out, v_ssm_a_re, v_ssm_a_im, v_ssm_log_dt, v_ssm_b_re, v_ssm_b_im, v_ssm_c_re, v_ssm_c_im, v_ssm_d, v_ssm_glu_w, v_ssm_glu_b, v_odd_w_in, v_pool_w, v_pool_scale, v_odd_w_out):
    w = dict(pre_norm=pre_norm, post_norm=post_norm, even_w_in=even_w_in, even_w_out=even_w_out, ssm_a_re=ssm_a_re,
             ssm_a_im=ssm_a_im, ssm_log_dt=ssm_log_dt, ssm_b_re=ssm_b_re, ssm_b_im=ssm_b_im, ssm_c_re=ssm_c_re,
             ssm_c_im=ssm_c_im, ssm_d=ssm_d, ssm_glu_w=ssm_glu_w, ssm_glu_b=ssm_glu_b, odd_w_in=odd_w_in,
             pool_w=pool_w, pool_scale=pool_scale, odd_w_out=odd_w_out)
    mom = dict(pre_norm=m_pre_norm, post_norm=m_post_norm, even_w_in=m_even_w_in, even_w_out=m_even_w_out,
               ssm_a_re=m_ssm_a_re, ssm_a_im=m_ssm_a_im, ssm_log_dt=m_ssm_log_dt, ssm_b_re=m_ssm_b_re,
               ssm_b_im=m_ssm_b_im, ssm_c_re=m_ssm_c_re, ssm_c_im=m_ssm_c_im, ssm_d=m_ssm_d, ssm_glu_w=m_ssm_glu_w,
               ssm_glu_b=m_ssm_glu_b, odd_w_in=m_odd_w_in, pool_w=m_pool_w, pool_scale=m_pool_scale,
               odd_w_out=m_odd_w_out)
    var = dict(pre_norm=v_pre_norm, post_norm=v_post_norm, even_w_in=v_even_w_in, even_w_out=v_even_w_out,
               ssm_a_re=v_ssm_a_re, ssm_a_im=v_ssm_a_im, ssm_log_dt=v_ssm_log_dt, ssm_b_re=v_ssm_b_re,
               ssm_b_im=v_ssm_b_im, ssm_c_re=v_ssm_c_re, ssm_c_im=v_ssm_c_im, ssm_d=v_ssm_d, ssm_glu_w=v_ssm_glu_w,
               ssm_glu_b=v_ssm_glu_b, odd_w_in=v_odd_w_in, pool_w=v_pool_w, pool_scale=v_pool_scale,
               odd_w_out=v_odd_w_out)
    me = _my_index()
    scale_cols = pool_scale.shape[1]

    def start_gather(tag, layer, keys, after=()):
        i = layer // 2
        shards = [w[FAMILY[(layer % 2, k)]][i].astype(BF16) for k in keys]
        if layer % 2 == 1:
            shards.append(jnp.pad(pool_scale[i][None], ((0, PACK_ROWS_ALIGN - 1), (0, 0))))
        return _xchg_start(f"gather_start_{tag}", shards, True, after)

    gather_started = {0: start_gather("0", 0, EVEN_SHARDED[:1])}
    small = {nm: w[nm] for nm in SMALL_NAMES}

    def get_weights(layer, after):
        keys = EVEN_SHARDED[:1] if layer == 0 else _sharded_keys(layer)
        lands = _xchg_wait(f"gather_wait_{layer}", gather_started[layer], True, after)
        wts = {k: _from_gathered(k, gat) for k, gat in zip(keys, lands)}
        if layer % 2 == 1:
            wts["pool_scale"] = lands[-1][:, 0, :].reshape(N_DEV * scale_cols)
        if layer == 0:
            prev = gather_started["0_late"] = start_gather("0_late", 0, EVEN_SHARDED[1:], after=(lands[0],))
            for later in (1, 2, 3):
                prev = gather_started[later] = start_gather(str(later), later, _sharded_keys(later), after=(prev[4],))
            wts["token"] = sum(gather_started[tag][4][0, 0] for tag in ("0_late", 1, 2, 3))

            def late(after_late):
                late_lands = _xchg_wait("gather_wait_0_late", gather_started["0_late"], True, (after_late,))
                return tuple(_from_gathered(k, gat) for k, gat in zip(EVEN_SHARDED[1:], late_lands))

            wts["late"] = late
        elif layer == 2:
            wts["late"] = lambda after_late: (wts["w_out"], wts["glu_w"])
        return wts

    scatter_started = []

    def on_w(layer, gw):
        keys = tuple(k for k in _sharded_keys(layer) if k in gw)
        started = _xchg_start(f"scatter_start_{layer}_{keys[0]}", [_to_slots(k, gw[k]) for k in keys], False)
        scatter_started.append((layer, keys, started))
        return started[4]

    def wait_scatters(layers, after):
        for layer, keys, started in scatter_started:
            if layer in layers:
                lands = _xchg_wait(f"scatter_wait_{layer}_{keys[0]}", started, False, after)
                for k, land in zip(keys, lands):
                    recv[(layer, k)] = land

    packed_names = ("pre_norm", "post_norm") + SSM_NAMES + ("ssm_d", "ssm_glu_b")
    tails = {nm: (SSM_GROUPS, SSM_STATE * SSM_GROUP) if nm in ("ssm_b_re", "ssm_b_im") else w[nm].shape[1:]
             for nm in packed_names}

    layer_grads = {}
    early_started, mid_started = [], []

    def on_ssm(layer, ssm_grads):
        if layer != 0:
            return None
        mid_started.append(_xchg_start("mid_start", [_pack(list(ssm_grads))], True))
        return mid_started[0][4]

    def on_grads(layer, lg):
        layer_grads[layer] = lg
        zero = jnp.zeros((), F32)
        if layer == 1:
            lgs = layer_grads
            early = ([jnp.concatenate([lgs[l][k] for l in (1, 2, 3)], axis=0) for k in ("pre", "post")]
                     + list(lgs[2]["ssm"]) + [lgs[2]["ssm_d"], lgs[2]["glu_b"],
                                              jnp.concatenate([lgs[1]["pool_scale"], lgs[3]["pool_scale"]], axis=0)])
            early_started.append(_xchg_start("small_start", [_pack(early)], True))
            zero = zero + early_started[0][4][0, 0]
        return zero

    loss_local, grad_x, token = _local_step(x[0], loss_target[0], small, get_weights, on_w, on_ssm, on_grads,
                                            zero=gather_started[0][4][0, 0])

    lg0 = layer_grads[0]
    late_started = _xchg_start("late_start", [_pack([lg0["pre"], lg0["post"], lg0["ssm_d"], lg0["glu_b"],
                                                     loss_local.reshape(1)]) + token], True)

    def adam_family(parity, k):
        nm = FAMILY[(parity, k)]
        shp = w[nm].shape
        cols = shp[-1]
        slot_list = [recv[(parity + 2 * i, k)].reshape(N_DEV, -1, cols) for i in range(2)]
        outs = _adam_layers(w[nm].reshape(2, -1, cols), slot_list, mom[nm].reshape(2, -1, cols),
                            var[nm].reshape(2, -1, cols), name=f"adam_{nm}")
        return [o.reshape(shp) for o in outs]

    recv, res = {}, {}
    wait_scatters((3, 1), (late_started[4],))
    for k in ODD_SHARDED:
        res[FAMILY[(1, k)]] = adam_family(1, k)
    odd_done = tuple(res[FAMILY[(1, k)]][0] for k in ODD_SHARDED)

    (early_slots,) = _xchg_wait("small_wait", early_started[0], True, odd_done)
    (mid_slots,) = _xchg_wait("mid_wait", mid_started[0], True, odd_done)
    early_shapes = [(w[nm].shape[0] - 1,) + tails[nm] for nm in packed_names] + [(2, N_DEV * scale_cols)]
    g_early = _unpack(_sum_slots(early_slots, "sum_small_early"), early_shapes)
    g_mid = _unpack(_sum_slots(mid_slots, "sum_small_mid"), [(1,) + tails[nm] for nm in SSM_NAMES])

    (late_slots,) = _xchg_wait("late_wait", late_started, True, (g_early[0], g_mid[0]))
    wait_scatters((2, 0), (late_slots,))
    for k in EVEN_SHARDED:
        res[FAMILY[(0, k)]] = adam_family(0, k)

    late_names = ("pre_norm", "post_norm", "ssm_d", "ssm_glu_b")
    g_late = _unpack(_sum_slots(late_slots, "sum_small_late"), [(1,) + tails[nm] for nm in late_names] + [(1,)])
    g_first = dict(zip(late_names, g_late))
    g_first.update(zip(SSM_NAMES, g_mid))
    dense = lambda nm, a: a.reshape((a.shape[0],) + tails[nm])
    outs = _adam_params([(dense(nm, w[nm]), dense(nm, mom[nm]), dense(nm, var[nm]), g_first[nm], g_early[j])
                         for j, nm in enumerate(packed_names)], "adam_small")
    for nm, four in zip(packed_names, outs):
        res[nm] = [o.reshape(w[nm].shape) for o in four]
    loss = g_late[-1].reshape(())
    g_scale = lax.dynamic_slice_in_dim(g_early[-1], me * scale_cols, scale_cols, axis=1)
    pad = ((0, PACK_ROWS_ALIGN - 2), (0, 0))
    outs = _adam(jnp.pad(pool_scale, pad), jnp.pad(g_scale, pad)[None], jnp.pad(m_pool_scale, pad),
                 jnp.pad(v_pool_scale, pad), name="adam_pool_scale")
    res["pool_scale"] = [o[:2] for o in outs]

    out = [loss, grad_x[None]]
    for kind in range(4):
        out += [res[nm][kind] for nm in WEIGHT_ORDER]
    return tuple(out)
```

```python
import functools
import math

import jax
import jax.numpy as jnp
from jax import lax
from jax.experimental import pallas as pl
from jax.experimental.pallas import tpu as pltpu

F32 = jnp.float32
BF16 = jnp.bfloat16
SDS = jax.ShapeDtypeStruct

N_DEV = 8
S = 2048
D = 1024
HEAD_DIM = 64
ROT_DIM = 16
ROPE_THETA = 500000.0
ATT_W = 1024
SSM_W = 512
SSM_GROUPS = 32
SSM_GROUP = 16
SSM_STATE = 64
N_CPLX = SSM_GROUPS * SSM_STATE
POOL_W = 2048
POOL_GROUP = 512
EVEN_IN = 5120
EVEN_OUT = 1536
ODD_IN = 4096
RMS_EPS = 1e-6
LANES = 128
VMEM_LIMIT = 48 * 1024 * 1024

ADAM_LR = 0.001
ADAM_B1 = 0.9
ADAM_B2 = 0.999
ADAM_EPS = 1e-08
ADAM_WD = 0.01
ADAM_STEP = 10

MESH_ID = pl.DeviceIdType.MESH
NN = (((1,), (0,)), ((), ()))
NT = (((1,), (1,)), ((), ()))
TN = (((0,), (0,)), ((), ()))
_DN = {"nn": NN, "nt": NT, "tn": TN}


def _cparams(sem):
    return pltpu.CompilerParams(dimension_semantics=sem, vmem_limit_bytes=VMEM_LIMIT)


MM_TILES = (1024, 768, 512)


def _tile(dim):
    return next((t for t in MM_TILES if dim % t == 0), dim)


NT_BLOCKS_PER_STEP = 4


def _mm(a, b, mode, out_dtype, b_blocks=False, out_blocks=False, a_cols=None, after=()):
    if b_blocks:
        nblk, rows, cb = b.shape
        b2_shape = (rows, nblk * cb)
    else:
        b2_shape = b.shape
    a_shape = a.shape if a_cols is None else (a.shape[0], a_cols[1])
    if mode == "nn":
        (m, k), n = a_shape, b2_shape[1]
    elif mode == "nt":
        (m, k), n = a_shape, b2_shape[0]
    else:
        (k, m), n = a_shape, b2_shape[1]
    tm, tn, tk = _tile(m), _tile(n), _tile(k)
    per_step = 1
    if b_blocks and mode == "nn":
        tn = cb
    if b_blocks and mode == "nt":
        per_step = NT_BLOCKS_PER_STEP
        tk = per_step * cb
        tn = min(tn, MM_TILES[-1])
    if out_blocks:
        tn = n // N_DEV
        tk = k
    nk = k // tk
    a_unit = tm if mode == "tn" else tk
    assert a_cols is None or a_cols[0] % a_unit == 0
    a_off = 0 if a_cols is None else a_cols[0] // a_unit

    def body(a_ref, b_ref, *rest):
        o_ref, acc_ref = rest[-2:]
        kk = pl.program_id(2)
        if per_step == 1:
            part = lax.dot_general(a_ref[...].astype(BF16), b_ref[...].astype(BF16), _DN[mode],
                                   preferred_element_type=F32)
        else:
            part = None
            for blk in range(per_step):
                d = lax.dot_general(a_ref[:, blk * cb:(blk + 1) * cb].astype(BF16), b_ref[blk].astype(BF16), NT,
                                    preferred_element_type=F32)
                part = d if part is None else part + d
        if nk == 1:
            o_ref[...] = part.astype(o_ref.dtype)
            return

        @pl.when(kk == 0)
        def _():
            acc_ref[...] = part

        @pl.when((kk > 0) & (kk < nk - 1))
        def _():
            acc_ref[...] += part

        @pl.when(kk == nk - 1)
        def _():
            o_ref[...] = (acc_ref[...] + part).astype(o_ref.dtype)

    if mode == "nn":
        a_spec = pl.BlockSpec((tm, tk), lambda i, j, kk: (i, a_off + kk))
        b_spec = pl.BlockSpec((tk, tn), lambda i, j, kk: (kk, j))
    elif mode == "nt":
        a_spec = pl.BlockSpec((tm, tk), lambda i, j, kk: (i, a_off + kk))
        b_spec = pl.BlockSpec((tn, tk), lambda i, j, kk: (j, kk))
    else:
        a_spec = pl.BlockSpec((tk, tm), lambda i, j, kk: (kk, a_off + i))
        b_spec = pl.BlockSpec((tk, tn), lambda i, j, kk: (kk, j))
    if b_blocks and mode == "nn":
        b_spec = pl.BlockSpec((None, tk, cb), lambda i, j, kk: (j, kk, 0))
    if b_blocks and mode == "nt":
        b_spec = pl.BlockSpec((per_step, tn, cb), lambda i, j, kk: (kk, j, 0))
    out_spec = pl.BlockSpec((tm, tn), lambda i, j, kk: (i, j))
    out_shape = SDS((m, n), out_dtype)
    if out_blocks:
        out_spec = pl.BlockSpec((None, tm, tn), lambda i, j, kk: (j, i, 0))
        out_shape = SDS((N_DEV, m, tn), out_dtype)
    return pl.pallas_call(
        body, name=f"mm_{mode}_{m}x{k}x{n}",
        grid=(m // tm, n // tn, nk),
        in_specs=[a_spec, b_spec] + [pl.BlockSpec(memory_space=pl.ANY)] * len(after),
        out_specs=out_spec,
        out_shape=out_shape,
        scratch_shapes=[pltpu.VMEM((tm, tn) if nk > 1 else (8, LANES), F32)],
        compiler_params=_cparams(("parallel", "parallel", "arbitrary")),
    )(a, b, *after)


def _gmm(a, b, mode, out_dtype, tm=512):
    ng, gw = POOL_W // POOL_GROUP, POOL_GROUP
    ns = S // tm
    if mode in ("nn", "nt"):
        def body(a_ref, b_ref, o_ref):
            o_ref[...] = lax.dot_general(a_ref[...].astype(BF16), b_ref[...].astype(BF16), _DN[mode],
                                         preferred_element_type=F32).astype(o_ref.dtype)

        return pl.pallas_call(
            body, name=f"gmm_{mode}", grid=(ng, ns),
            in_specs=[pl.BlockSpec((tm, gw), lambda g, i: (i, g)),
                      pl.BlockSpec((None, gw, gw), lambda g, i: (g, 0, 0))],
            out_specs=pl.BlockSpec((tm, gw), lambda g, i: (i, g)),
            out_shape=SDS((S, POOL_W), out_dtype),
            compiler_params=_cparams(("parallel", "parallel")),
        )(a, b)

    def body_tn(a_ref, b_ref, o_ref, acc_ref):
        i = pl.program_id(1)

        @pl.when(i == 0)
        def _():
            acc_ref[...] = jnp.zeros_like(acc_ref)

        acc_ref[...] += lax.dot_general(a_ref[...].astype(BF16), b_ref[...].astype(BF16), TN,
                                        preferred_element_type=F32)

        @pl.when(i == ns - 1)
        def _():
            o_ref[...] = acc_ref[...].astype(o_ref.dtype)

    return pl.pallas_call(
        body_tn, name="gmm_tn", grid=(ng, ns),
        in_specs=[pl.BlockSpec((tm, gw), lambda g, i: (i, g)),
                  pl.BlockSpec((tm, gw), lambda g, i: (i, g))],
        out_specs=pl.BlockSpec((None, gw, gw), lambda g, i: (g, 0, 0)),
        out_shape=SDS((ng, gw, gw), out_dtype),
        scratch_shapes=[pltpu.VMEM((gw, gw), F32)],
        compiler_params=_cparams(("parallel", "arbitrary")),
    )(a, b)


def _rowwise(fn, inputs, out_defs, acc_defs=(), tm=256, name=None, after=()):
    n_in, n_out, n_acc = len(inputs), len(out_defs), len(acc_defs)
    n_after = len(after)
    in_specs, args = [], []
    for arr, width, cb in inputs:
        if arr.shape[0] == 1:
            in_specs.append(pl.BlockSpec((1, width), lambda i, cb=cb: (0, cb)))
        else:
            in_specs.append(pl.BlockSpec((tm, width), lambda i, cb=cb: (i, cb)))
        args.append(arr)
    out_defs = [d if len(d) == 4 else (d[0], d[1], d[0], 0) for d in out_defs]
    out_shape = [SDS((S, ww), dt) for _, dt, ww, _ in out_defs] + [SDS((1, w), F32) for w in acc_defs]
    out_specs = ([pl.BlockSpec((tm, w), lambda i, cb=cb: (i, cb)) for w, _, _, cb in out_defs]
                 + [pl.BlockSpec((1, w), lambda i: (0, 0)) for w in acc_defs])

    def kern(*refs):
        vals = [r[...] for r in refs[:n_in]]
        outs, accs = fn(*vals)
        out_refs = refs[n_in + n_after:]
        for r, v in zip(out_refs[:n_out], outs):
            r[...] = v.astype(r.dtype)
        if n_acc:
            acc_refs = out_refs[n_out:]

            @pl.when(pl.program_id(0) == 0)
            def _():
                for r in acc_refs:
                    r[...] = jnp.zeros_like(r)

            for r, v in zip(acc_refs, accs):
                r[...] += jnp.sum(v, axis=0, keepdims=True)

    res = pl.pallas_call(
        kern, name=name, grid=(S // tm,), in_specs=in_specs + [pl.BlockSpec(memory_space=pl.ANY)] * n_after,
        out_specs=out_specs, out_shape=out_shape, compiler_params=_cparams(("arbitrary",)),
    )(*args, *after)
    return res


def _sigmoid(x):
    return 1.0 / (1.0 + jnp.exp(-x))


def _silu_and_grad(x):
    s = _sigmoid(x)
    return x * s, s * (1.0 + x * (1.0 - s))


_GELU_K = math.sqrt(2.0 / math.pi)
_GELU_C = 0.044715


def _gelu_and_grad(x):
    t = jnp.tanh(_GELU_K * (x + _GELU_C * (x * x * x)))
    cdf = 0.5 * (1.0 + t)
    grad = cdf + 0.5 * x * (1.0 - t * t) * (_GELU_K * (1.0 + 3.0 * _GELU_C * x * x))
    return x * cdf, grad


def _rms(xv, gain):
    r = lax.rsqrt(jnp.mean(xv * xv, axis=-1, keepdims=True) + RMS_EPS)
    return xv * r * gain


def _rms_bwd(dout, xv, gain):
    r = lax.rsqrt(jnp.mean(xv * xv, axis=-1, keepdims=True) + RMS_EPS)
    xhat = xv * r
    dxhat = dout * gain
    dx = r * (dxhat - xhat * jnp.mean(dxhat * xhat, axis=-1, keepdims=True))
    return dx, dout * xhat


def _norm_fwd(x, gain):
    (h,) = _rowwise(lambda xv, g: ((_rms(xv, g),), ()), [(x, D, 0), (gain, D, 0)], [(D, BF16)], name="norm_fwd")
    return h


def _post_fwd(x, y, gain):
    (o,) = _rowwise(lambda xv, yv, g: ((xv + _rms(yv, g),), ()), [(x, D, 0), (y, D, 0), (gain, D, 0)],
                    [(D, F32)], name="post_fwd")
    return o


def _post_bwd(g, y, gain):
    def fn(gv, yv, gn):
        dx, dg = _rms_bwd(gv, yv, gn)
        return (dx,), (dg,)

    return _rowwise(fn, [(g, D, 0), (y, D, 0), (gain, D, 0)], [(D, BF16)], [D], name="post_bwd")


def _pre_bwd(g, dh, x, gain):
    def fn(gv, dhv, xv, gn):
        dx, dg = _rms_bwd(dhv, xv, gn)
        return (gv + dx,), (dg,)

    return _rowwise(fn, [(g, D, 0), (dh, D, 0), (x, D, 0), (gain, D, 0)], [(D, F32)], [D], name="pre_bwd")


def _loss_grad(xo, tgt):
    def fn(xv, tv):
        e = xv - tv
        return (e * (1.0 / D),), (e * e,)

    return _rowwise(fn, [(xo, D, 0), (tgt, D, 0)], [(D, F32)], [D], name="loss_grad")


def _pool(u_arr, col_block, transpose, out_dtype, into=None, tc=256):
    n_t = POOL_W // tc
    per_group = POOL_GROUP // tc

    def body(u_ref, *rest):
        o_ref = rest[-1]
        c = pl.program_id(0)
        grp = c // per_group
        xv = u_ref[...]
        t = lax.broadcasted_iota(jnp.int32, (S, 1), 0)
        win = jnp.left_shift(2, grp)
        cnt = jnp.minimum(t + 1, win).astype(F32)
        cur = xv / cnt if transpose else xv
        sums = []
        for k in (1, 2, 4, 8):
            if transpose:
                sh = jnp.where(t < S - k, pltpu.roll(cur, S - k, 0), 0.0)
            else:
                sh = jnp.where(t >= k, pltpu.roll(cur, k, 0), 0.0)
            cur = cur + sh
            sums.append(cur)
        tot = jnp.where(grp == 0, sums[0], jnp.where(grp == 1, sums[1], jnp.where(grp == 2, sums[2], sums[3])))
        res = tot - xv if transpose else tot / cnt - xv
        o_ref[...] = res.astype(o_ref.dtype)

    in_specs = [pl.BlockSpec((S, tc), lambda c: (0, col_block * n_t + c))]
    args = [u_arr]
    if into is not None:
        in_specs.append(pl.BlockSpec(memory_space=pl.ANY))
        args.append(into)
    return pl.pallas_call(
        body, name="pool_bwd" if transpose else "pool_fwd", grid=(n_t,),
        in_specs=in_specs,
        out_specs=pl.BlockSpec((S, tc), lambda c: (0, c)),
        out_shape=SDS((S, POOL_W) if into is None else into.shape, out_dtype),
        input_output_aliases={} if into is None else {1: 0},
        compiler_params=_cparams(("parallel",)),
    )(*args)


def _rope_tables(zero):
    pos = jnp.arange(S, dtype=jnp.int32).astype(F32) + zero
    inv_freq = ROPE_THETA ** (-jnp.arange(0, ROT_DIM, 2, dtype=F32) / ROT_DIM)
    ang = pos[:, None] * inv_freq[None, :]
    cos8, sin8 = jnp.cos(ang), jnp.sin(ang)
    half = ROT_DIM // 2
    zeros = jnp.zeros((S, HEAD_DIM - ROT_DIM), F32)
    cos = jnp.concatenate([cos8, cos8, jnp.ones((S, HEAD_DIM - ROT_DIM), F32)], axis=1)
    lo = jnp.concatenate([-sin8, jnp.zeros((S, half), F32), zeros], axis=1)
    hi = jnp.concatenate([jnp.zeros((S, half), F32), sin8, zeros], axis=1)
    rep = LANES // HEAD_DIM
    return jnp.tile(cos, (1, rep)), jnp.tile(lo, (1, rep)), jnp.tile(hi, (1, rep))


def _rotate(xv, cos, lo, hi, transpose):
    width = xv.shape[1]
    rep = width // LANES
    wide = lambda tab: jnp.concatenate([tab] * rep, axis=1)
    half = ROT_DIM // 2
    up = pltpu.roll(xv, width - half, 1)
    dn = pltpu.roll(xv, half, 1)
    mixed = up * wide(lo) + dn * wide(hi)
    return xv * wide(cos) - mixed if transpose else xv * wide(cos) + mixed


def _qkv_prep(proj, tables):
    cos, lo, hi = tables

    def fn(x, c, l, h):
        rot = _rotate(x[:, :2 * ATT_W], c, l, h, False)
        return (jnp.concatenate([(rot[:, :ATT_W] * HEAD_DIM ** -0.5).astype(BF16), rot[:, ATT_W:].astype(BF16),
                                 x[:, 2 * ATT_W:].astype(BF16)], axis=1),), ()

    (qkv,) = _rowwise(fn, [(proj, 3 * ATT_W, 0), (cos, LANES, 0), (lo, LANES, 0), (hi, LANES, 0)],
                      [(3 * ATT_W, BF16)], name="qkv_prep")
    return qkv


ATT_T = 512


def _multiplicity(delta):
    ok = delta >= 0
    near = jnp.where(ok & (delta <= 128), 1.0, 0.0)
    mid = jnp.where(ok & (delta <= 512) & ((delta & 3) == 0), 1.0, 0.0)
    far = jnp.where(ok & ((delta & 15) == 0), 1.0, 0.0)
    return near + mid + far


def _attention_bias(zero):
    t = ATT_T
    pos = jnp.arange(t, dtype=jnp.int32) + jnp.asarray(zero).astype(jnp.int32)
    delta = jnp.arange(S // t, dtype=jnp.int32)[:, None, None] * t + pos[None, :, None] - pos[None, None, :]
    mult = _multiplicity(delta)
    return jnp.where(mult > 0.0, jnp.log(jnp.maximum(mult, 1.0)), -1e30).astype(F32)


def _head_split(v, first):
    zero = jnp.zeros_like(v)
    return [jnp.where(first, v, zero), jnp.where(first, zero, v)]


def _flash_fwd(qkv, bias):
    t = ATT_T
    n_hp = ATT_W // LANES

    def body(q_ref, k_ref, v_ref, b_ref, o_ref, lse_ref):
        i = pl.program_id(1)
        first = lax.broadcasted_iota(jnp.int32, (1, LANES), 1) < HEAD_DIM
        qs = _head_split(q_ref[...], first)

        def kv_step(j, carry):
            m0, l0, m1, l1, acc = carry
            off = pl.multiple_of(j * t, t)
            kb = k_ref[pl.ds(off, t), :]
            vs = _head_split(v_ref[pl.ds(off, t), :], first)
            bias_t = b_ref[i - j]
            new = []
            pv = None
            for h, (m_prev, l_prev) in enumerate(((m0, l0), (m1, l1))):
                s = lax.dot_general(qs[h], kb, NT, preferred_element_type=F32) + bias_t
                m_new = jnp.maximum(m_prev, jnp.max(s, axis=1, keepdims=True))
                p = jnp.exp(s - m_new)
                alpha = jnp.exp(m_prev - m_new)
                l_new = alpha * l_prev + jnp.sum(p, axis=1, keepdims=True)
                d = lax.dot_general(p.astype(BF16), vs[h], NN, preferred_element_type=F32)
                pv = d if pv is None else pv + d
                new.append((m_new, l_new, alpha))
            acc = acc * jnp.where(first, new[0][2], new[1][2]) + pv
            return new[0][0], new[0][1], new[1][0], new[1][1], acc

        neg = jnp.full((t, 1), -1e30, F32)
        zero = jnp.zeros((t, 1), F32)
        m0, l0, m1, l1, acc = lax.fori_loop(0, i + 1, kv_step, (neg, zero, neg, zero, jnp.zeros((t, LANES), F32)))
        o_ref[...] = acc * jnp.where(first, 1.0 / l0, 1.0 / l1)
        lse_ref[...] = jnp.where(first, m0 + jnp.log(l0), m1 + jnp.log(l1))

    blk = pl.BlockSpec((t, LANES), lambda hp, i: (i, hp))
    k_full = pl.BlockSpec((S, LANES), lambda hp, i: (0, n_hp + hp))
    v_full = pl.BlockSpec((S, LANES), lambda hp, i: (0, 2 * n_hp + hp))
    return pl.pallas_call(
        body, name="flash_fwd", grid=(n_hp, S // t),
        in_specs=[blk, k_full, v_full, pl.BlockSpec((S // t, t, t), lambda hp, i: (0, 0, 0))], out_specs=[blk, blk],
        out_shape=[SDS((S, ATT_W), F32), SDS((S, ATT_W), F32)],
        compiler_params=_cparams(("parallel", "arbitrary")),
    )(qkv, qkv, qkv, bias)


def _flash_bwd(qkv, o, do, lse, bias, after=()):
    t = ATT_T
    n_hp = ATT_W // LANES
    n_t = S // t

    def body(q_ref, k_ref, v_ref, o_ref, do_ref, lse_ref, b_ref, *rest):
        dq_ref, dk_ref, dv_ref = rest[-3:]
        j = pl.program_id(1)
        first = lax.broadcasted_iota(jnp.int32, (1, LANES), 1) < HEAD_DIM

        @pl.when(j == 0)
        def _():
            dq_ref[...] = jnp.zeros_like(dq_ref)

        kb = k_ref[...]
        vb = v_ref[...]
        ks = _head_split(kb, first)

        def q_step(i, carry):
            dk_acc, dv_acc = carry
            rows = pl.ds(pl.multiple_of(i * t, t), t)
            qs = _head_split(q_ref[rows, :], first)
            dob = do_ref[rows, :]
            prod = dob * o_ref[rows, :]
            d_all = jnp.sum(prod, axis=1, keepdims=True)
            d0 = jnp.sum(jnp.where(first, prod, 0.0), axis=1, keepdims=True)
            lse_b = lse_ref[rows, :]
            lse0 = jnp.max(jnp.where(first, lse_b, -jnp.inf), axis=1, keepdims=True)
            lse1 = jnp.max(jnp.where(first, -jnp.inf, lse_b), axis=1, keepdims=True)
            dos = _head_split(dob.astype(BF16), first)
            bias_t = b_ref[i - j]
            dq_t = jnp.zeros((t, LANES), F32)
            for h, (lse_h, d_h) in enumerate(((lse0, d0), (lse1, d_all - d0))):
                s = lax.dot_general(qs[h], kb, NT, preferred_element_type=F32)
                p = jnp.exp(s + (bias_t - lse_h))
                dp = lax.dot_general(dos[h], vb, NT, preferred_element_type=F32)
                ds = (p * (dp - d_h)).astype(BF16)
                dv_acc = dv_acc + lax.dot_general(p.astype(BF16), dos[h], TN, preferred_element_type=F32)
                dk_acc = dk_acc + lax.dot_general(ds, qs[h], TN, preferred_element_type=F32)
                dq_t = dq_t + lax.dot_general(ds, ks[h], NN, preferred_element_type=F32)
            dq_ref[rows, :] += dq_t
            return dk_acc, dv_acc

        zero = jnp.zeros((t, LANES), F32)
        dk_acc, dv_acc = lax.fori_loop(j, n_t, q_step, (zero, zero))
        dk_ref[...] = dk_acc
        dv_ref[...] = dv_acc

    blk = pl.BlockSpec((t, LANES), lambda hp, j: (j, hp))
    full = pl.BlockSpec((S, LANES), lambda hp, j: (0, hp))
    k_blk = pl.BlockSpec((t, LANES), lambda hp, j: (j, n_hp + hp))
    v_blk = pl.BlockSpec((t, LANES), lambda hp, j: (j, 2 * n_hp + hp))
    return pl.pallas_call(
        body, name="flash_bwd", grid=(n_hp, n_t),
        in_specs=([full, k_blk, v_blk, full, full, full, pl.BlockSpec((n_t, t, t), lambda hp, j: (0, 0, 0))]
                  + [pl.BlockSpec(memory_space=pl.ANY)] * len(after)),
        out_specs=[full, blk, blk],
        out_shape=[SDS((S, ATT_W), F32)] * 3,
        compiler_params=_cparams(("parallel", "arbitrary")),
    )(qkv, qkv, qkv, o, do, lse, bias, *after)


SCAN_T = 256
SCAN_GROUP = 8
SCAN_STEPS = (1, 2, 4)
ST_ROWS = 2 * N_CPLX // LANES
HALF = ST_ROWS // 2


def _scan_tables(lam_t):
    lam = lax.complex(lam_t[:HALF].reshape(N_CPLX), lam_t[HALF:].reshape(N_CPLX))
    pows = [lam]
    for _ in range(SCAN_GROUP - 1):
        pows.append(pows[-1] * lam)
    pows = jnp.stack(pows)
    sub = jnp.arange(SCAN_GROUP)[:, None]
    fwd = [jnp.where(sub >= k, pows[k - 1][None, :], 0.0) for k in SCAN_STEPS] + [pows]
    conj = jnp.conj(pows)
    bwd = [jnp.where(sub <= SCAN_GROUP - 1 - k, conj[k - 1][None, :], 0.0) for k in SCAN_STEPS] + [conj[::-1]]

    def pack(tabs):
        return jnp.stack([jnp.concatenate([jnp.real(t), jnp.imag(t)], axis=1) for t in tabs]).astype(F32)

    return pack(fwd), pack(bwd)


def _cmul_add(xr, xi, lr, li, sr, si):
    return xr + lr * sr - li * si, xi + lr * si + li * sr


def _group_scan(xr, xi, tab_ref, cr, ci, reverse):
    for j, k in enumerate(SCAN_STEPS):
        shift = SCAN_GROUP - k if reverse else k
        xr, xi = _cmul_add(xr, xi, tab_ref[j, :, :N_CPLX], tab_ref[j, :, N_CPLX:],
                           pltpu.roll(xr, shift, 0), pltpu.roll(xi, shift, 0))
    return _cmul_add(xr, xi, tab_ref[3, :, :N_CPLX], tab_ref[3, :, N_CPLX:],
                     jnp.broadcast_to(cr, (SCAN_GROUP, N_CPLX)), jnp.broadcast_to(ci, (SCAN_GROUP, N_CPLX)))


def _scan_fwd(tab, bu):
    nc = N_CPLX

    def body(tab_ref, bu_ref, st_ref, carry):
        @pl.when(pl.program_id(0) == 0)
        def _():
            carry[...] = jnp.zeros_like(carry)

        def group(a, c):
            rows = pl.ds(pl.multiple_of(a * SCAN_GROUP, SCAN_GROUP), SCAN_GROUP)
            xr, xi = _group_scan(bu_ref[rows, :nc], bu_ref[rows, nc:], tab_ref, c[0], c[1], False)
            st_ref[rows, :nc] = xr
            st_ref[rows, nc:] = xi
            return xr[SCAN_GROUP - 1:SCAN_GROUP, :], xi[SCAN_GROUP - 1:SCAN_GROUP, :]

        cr, ci = lax.fori_loop(0, SCAN_T // SCAN_GROUP, group, (carry[:, :nc], carry[:, nc:]), unroll=2)
        carry[:, :nc] = cr
        carry[:, nc:] = ci

    blk = pl.BlockSpec((SCAN_T, 2 * nc), lambda i: (i, 0))
    return pl.pallas_call(
        body, name="scan_fwd", grid=(S // SCAN_T,),
        in_specs=[pl.BlockSpec((4, SCAN_GROUP, 2 * nc), lambda i: (0, 0, 0)), blk], out_specs=blk,
        out_shape=SDS((S, 2 * nc), F32),
        scratch_shapes=[pltpu.VMEM((1, 2 * nc), F32)],
        compiler_params=_cparams(("arbitrary",)),
    )(tab, bu)


def _scan_bwd(tab, dst, states):
    n_blk = S // SCAN_T
    nc = N_CPLX

    def body(tab_ref, d_ref, x_ref, g_ref, dlam_ref, carry, acc):
        i = pl.program_id(0)

        @pl.when(i == 0)
        def _():
            carry[...] = jnp.zeros_like(carry)
            acc[...] = jnp.zeros_like(acc)

        last_row = lax.broadcasted_iota(jnp.int32, (SCAN_GROUP, 1), 0) == SCAN_GROUP - 1

        def group(j, c):
            cr, ci = c
            rows = pl.ds(pl.multiple_of((SCAN_T // SCAN_GROUP - 1 - j) * SCAN_GROUP, SCAN_GROUP), SCAN_GROUP)
            gr, gi = _group_scan(d_ref[rows, :nc], d_ref[rows, nc:], tab_ref, cr, ci, True)
            g_ref[rows, :nc] = gr
            g_ref[rows, nc:] = gi
            nr = jnp.where(last_row, jnp.broadcast_to(cr, (SCAN_GROUP, nc)), pltpu.roll(gr, SCAN_GROUP - 1, 0))
            ni = jnp.where(last_row, jnp.broadcast_to(ci, (SCAN_GROUP, nc)), pltpu.roll(gi, SCAN_GROUP - 1, 0))
            sr, si = x_ref[rows, :nc], x_ref[rows, nc:]
            acc[:, :nc] += nr * sr + ni * si
            acc[:, nc:] += ni * sr - nr * si
            return gr[0:1, :], gi[0:1, :]

        cr, ci = lax.fori_loop(0, SCAN_T // SCAN_GROUP, group, (carry[:, :nc], carry[:, nc:]), unroll=2)
        carry[:, :nc] = cr
        carry[:, nc:] = ci

        @pl.when(i == n_blk - 1)
        def _():
            dlam_ref[...] = jnp.sum(acc[...], axis=0, keepdims=True)

    blk = pl.BlockSpec((SCAN_T, 2 * nc), lambda i: (n_blk - 1 - i, 0))
    return pl.pallas_call(
        body, name="scan_bwd", grid=(n_blk,),
        in_specs=[pl.BlockSpec((4, SCAN_GROUP, 2 * nc), lambda i: (0, 0, 0)), blk, blk],
        out_specs=[blk, pl.BlockSpec((1, 2 * nc), lambda i: (0, 0))],
        out_shape=[SDS((S, 2 * nc), F32), SDS((1, 2 * nc), F32)],
        scratch_shapes=[pltpu.VMEM((1, 2 * nc), F32), pltpu.VMEM((SCAN_GROUP, 2 * nc), F32)],
        compiler_params=_cparams(("arbitrary",)),
    )(tab, dst, states)


def _ssm_prep(a_re, a_im, log_dt, b_re, b_im, c_re, c_im):
    lam = lax.complex(a_re, a_im)
    dt = jnp.exp(log_dt)[:, None]
    lam_bar = jnp.exp(lam * dt)
    b_bar = ((lam_bar - 1.0) / lam)[..., None] * lax.complex(b_re, b_im)
    lam_t = jnp.concatenate([jnp.real(lam_bar).reshape(HALF, LANES), jnp.imag(lam_bar).reshape(HALF, LANES)], axis=0)
    groups_per_super = SSM_GROUPS // SSM_SUPER
    on_diag = ((lax.broadcasted_iota(jnp.int32, (SSM_W, SB_COLS), 0) // SSM_GROUP) % groups_per_super
               == lax.broadcasted_iota(jnp.int32, (SSM_W, SB_COLS), 1) // SSM_STATE)

    def compact(m):
        return jnp.where(on_diag, jnp.tile(m.reshape(SSM_W, SSM_STATE), (1, groups_per_super)), 0.0)

    w_b = jnp.concatenate([compact(jnp.real(b_bar).transpose(0, 2, 1)),
                           compact(jnp.imag(b_bar).transpose(0, 2, 1))], axis=1)
    w_ct = jnp.concatenate([compact(c_re), -compact(c_im)], axis=1)
    return lam_t, w_b, w_ct


SSM_SUPER = 4
SB_ROWS = SSM_W // SSM_SUPER
SB_COLS = N_CPLX // SSM_SUPER


def _bdmm(a, w, mode, out_dtype, a_cols=None):
    a_off = 0 if a_cols is None else a_cols[0] // SB_ROWS
    if mode == "nn":
        def body(a_ref, w_ref, o_ref):
            o_ref[...] = lax.dot_general(a_ref[...].astype(BF16), w_ref[...].astype(BF16), NN,
                                         preferred_element_type=F32).astype(o_ref.dtype)

        return pl.pallas_call(
            body, name="bdmm_nn", grid=(2, SSM_SUPER),
            in_specs=[pl.BlockSpec((S, SB_ROWS), lambda h, b: (0, a_off + b)),
                      pl.BlockSpec((SB_ROWS, SB_COLS), lambda h, b: (b, h))],
            out_specs=pl.BlockSpec((S, SB_COLS), lambda h, b: (0, h * SSM_SUPER + b)),
            out_shape=SDS((S, 2 * N_CPLX), out_dtype),
            compiler_params=_cparams(("parallel", "parallel")),
        )(a, w)
    if mode == "nt":
        def body(re_ref, im_ref, wre_ref, wim_ref, o_ref):
            acc = lax.dot_general(re_ref[...].astype(BF16), wre_ref[...].astype(BF16), NT, preferred_element_type=F32)
            acc += lax.dot_general(im_ref[...].astype(BF16), wim_ref[...].astype(BF16), NT, preferred_element_type=F32)
            o_ref[...] = acc.astype(o_ref.dtype)

        return pl.pallas_call(
            body, name="bdmm_nt", grid=(SSM_SUPER,),
            in_specs=[pl.BlockSpec((S, SB_COLS), lambda b: (0, b)),
                      pl.BlockSpec((S, SB_COLS), lambda b: (0, SSM_SUPER + b)),
                      pl.BlockSpec((SB_ROWS, SB_COLS), lambda b: (b, 0)),
                      pl.BlockSpec((SB_ROWS, SB_COLS), lambda b: (b, 1))],
            out_specs=pl.BlockSpec((S, SB_ROWS), lambda b: (0, b)),
            out_shape=SDS((S, SSM_W), out_dtype),
            compiler_params=_cparams(("parallel",)),
        )(a, a, w, w)

    def body_tn(a_ref, w_ref, o_ref):
        o_ref[...] = lax.dot_general(a_ref[...].astype(BF16), w_ref[...].astype(BF16), TN,
                                     preferred_element_type=F32).astype(o_ref.dtype)

    return pl.pallas_call(
        body_tn, name="bdmm_tn", grid=(2, SSM_SUPER),
        in_specs=[pl.BlockSpec((S, SB_ROWS), lambda h, b: (0, a_off + b)),
                  pl.BlockSpec((S, SB_COLS), lambda h, b: (0, h * SSM_SUPER + b))],
        out_specs=pl.BlockSpec((SB_ROWS, SB_COLS), lambda h, b: (b, h)),
        out_shape=SDS((SSM_W, 2 * SB_COLS), out_dtype),
        compiler_params=_cparams(("parallel", "parallel")),
    )(a, w)


U_SSM_COLS = (4 * ATT_W, SSM_W)


def _row(v):
    return v.reshape(1, -1)


def _even_fwd(x, pre, post, w_in, late_w, glu_b, ssm_d, prep, tables):
    lam_t, w_b, w_ct = prep
    h = _norm_fwd(x, pre)
    proj = _mm(h, w_in, "nn", F32, b_blocks=True)
    qkv = _qkv_prep(proj, tables[:3])
    w_out, glu_w = late_w(qkv)
    att, lse = _flash_fwd(qkv, tables[3])
    bu = _bdmm(proj, w_b, "nn", F32, a_cols=U_SSM_COLS)
    scan_fwd_tab, scan_bwd_tab = _scan_tables(lam_t)
    states = _scan_fwd(scan_fwd_tab, bu)
    y = _bdmm(states, w_ct, "nt", F32)

    def act1(yv, uv, dv):
        return (_gelu_and_grad(yv + dv * uv)[0],), ()

    (z1,) = _rowwise(act1, [(y, SSM_W, 0), (proj, SSM_W, 8), (ssm_d, SSM_W, 0)], [(SSM_W, F32)], name="ssm_act_fwd")
    lin = _mm(z1, glu_w, "nn", F32)

    def gate(att_v, ga, gs, z1v, linv, bv):
        ssm_out = z1v * _sigmoid(linv + bv)
        return (jnp.concatenate([att_v * _silu_and_grad(ga)[0], ssm_out * _silu_and_grad(gs)[0]], axis=1),), ()

    (merged,) = _rowwise(gate, [(att, ATT_W, 0), (proj, ATT_W, 3), (proj, SSM_W, 9), (z1, SSM_W, 0),
                                (lin, SSM_W, 0), (glu_b, SSM_W, 0)], [(EVEN_OUT, BF16)], name="even_gate_fwd")
    yout = _mm(merged, w_out, "nn", F32)
    x_next = _post_fwd(x, yout, post)
    saved = (x, h, proj, qkv, att, lse, states, y, z1, lin, merged, yout, w_out, glu_w, scan_bwd_tab)
    return x_next, saved


def _even_bwd(g, saved, pre, post, w_in, late_w, glu_b, ssm_d, prep, tables, on_w, on_ssm):
    x, h, proj, qkv, att, lse, states, y, z1, lin, merged, yout, w_out, glu_w, scan_bwd_tab = saved
    lam_t, w_b, w_ct = prep
    dyout, dpost = _post_bwd(g, yout, post)
    dmerged = _mm(dyout, w_out, "nt", F32)
    dw_out = _mm(merged, dyout, "tn", BF16)

    def gate_bwd(dm_a, dm_s, att_v, ga, gs, z1v, linv, bv):
        sa, dsa = _silu_and_grad(ga)
        ss, dss = _silu_and_grad(gs)
        sig = _sigmoid(linv + bv)
        ssm_out = z1v * sig
        dssm = dm_s * ss
        dlin = dssm * z1v * sig * (1.0 - sig)
        return (dm_a * sa, dm_a * att_v * dsa, dm_s * ssm_out * dss, dssm * sig, dlin), (dlin,)

    datt, dg_att, dg_ssm, dz1a, dlin, dglu_b = _rowwise(
        gate_bwd, [(dmerged, ATT_W, 0), (dmerged, SSM_W, 2), (att, ATT_W, 0), (proj, ATT_W, 3), (proj, SSM_W, 9),
                   (z1, SSM_W, 0), (lin, SSM_W, 0), (glu_b, SSM_W, 0)],
        [(ATT_W, F32), (ATT_W, BF16), (SSM_W, BF16), (SSM_W, F32), (SSM_W, BF16)], [SSM_W], name="even_gate_bwd")
    dz1b = _mm(dlin, glu_w, "nt", F32)
    dglu_w = _mm(z1, dlin, "tn", BF16)

    def act1_bwd(da, db, yv, uv, dv):
        dpre = (da + db) * _gelu_and_grad(yv + dv * uv)[1]
        return (dpre, dpre * dv), (dpre * uv,)

    sent_late_w = on_w(dict(w_out=dw_out, glu_w=dglu_w))
    dy, du_direct, dd = _rowwise(act1_bwd, [(dz1a, SSM_W, 0), (dz1b, SSM_W, 0), (y, SSM_W, 0), (proj, SSM_W, 8),
                                            (ssm_d, SSM_W, 0)], [(SSM_W, BF16), (SSM_W, F32)], [SSM_W],
                                 name="ssm_act_bwd", after=(sent_late_w,))
    dst = _bdmm(dy, w_ct, "nn", F32)
    dw_ct = _bdmm(dy, states, "tn", F32)
    dbu, dlam_row = _scan_bwd(scan_bwd_tab, dst, states)
    dlam = jnp.concatenate([dlam_row[0, :N_CPLX].reshape(HALF, LANES), dlam_row[0, N_CPLX:].reshape(HALF, LANES)],
                           axis=0)
    du_state = _bdmm(dbu, w_b, "nt", F32)
    dw_b = _bdmm(proj, dbu, "tn", F32, a_cols=U_SSM_COLS)
    sent_ssm = on_ssm((dlam, dw_b, dw_ct))
    dq, dk, dv = _flash_bwd(qkv, att, datt, lse, tables[3], after=() if sent_ssm is None else (sent_ssm,))

    def assemble(dqv, dkv, dvv, dga, dua, dub, dgs, c, l, h):
        rot = _rotate(jnp.concatenate([dqv, dkv], axis=1), c, l, h, True)
        return (jnp.concatenate([(rot[:, :ATT_W] * HEAD_DIM ** -0.5).astype(BF16), rot[:, ATT_W:].astype(BF16),
                                 dvv.astype(BF16), dga, (dua + dub).astype(BF16), dgs], axis=1),), ()

    (dproj,) = _rowwise(assemble, [(dq, ATT_W, 0), (dk, ATT_W, 0), (dv, ATT_W, 0), (dg_att, ATT_W, 0),
                                   (du_state, SSM_W, 0), (du_direct, SSM_W, 0), (dg_ssm, SSM_W, 0),
                                   (tables[0], LANES, 0), (tables[1], LANES, 0), (tables[2], LANES, 0)],
                        [(EVEN_IN, BF16)], name="dproj_assemble")
    dw_in = _mm(h, dproj, "tn", BF16, out_blocks=True)
    sent = on_w(dict(w_in=dw_in))
    dh = _mm(dproj, w_in, "nt", F32, b_blocks=True, after=(sent,))
    g_prev, dpre = _pre_bwd(g, dh, x, pre)
    return g_prev, dict(pre=dpre, post=dpost, glu_b=dglu_b, ssm_d=dd)


def _odd_fwd(x, pre, post, w_in, pool_w, pool_scale, w_out):
    h = _norm_fwd(x, pre)
    proj = _mm(h, w_in, "nn", F32, b_blocks=True)
    mixed = _pool(proj, 0, False, BF16)
    ylin = _gmm(mixed, pool_w, "nn", F32)

    def gate(yl, gt, sc):
        return (yl * sc * _silu_and_grad(gt)[0],), ()

    (z,) = _rowwise(gate, [(ylin, POOL_W, 0), (proj, POOL_W, 1), (pool_scale, POOL_W, 0)], [(POOL_W, BF16)],
                    name="odd_gate_fwd")
    yout = _mm(z, w_out, "nn", F32)
    x_next = _post_fwd(x, yout, post)
    return x_next, (x, h, proj, mixed, ylin, z, yout)


def _odd_bwd(g, saved, pre, post, w_in, pool_w, pool_scale, w_out, on_w):
    x, h, proj, mixed, ylin, z, yout = saved
    dyout, dpost = _post_bwd(g, yout, post)
    dz = _mm(dyout, w_out, "nt", F32)
    dw_out = _mm(z, dyout, "tn", BF16)

    def gate_bwd(dzv, yl, gt, sc):
        sg, dsg = _silu_and_grad(gt)
        tt = dzv * sg
        return (tt * sc, dzv * yl * sc * dsg), (tt * yl,)

    dylin, dproj_gate, dscale = _rowwise(gate_bwd, [(dz, POOL_W, 0), (ylin, POOL_W, 0), (proj, POOL_W, 1),
                                                    (pool_scale, POOL_W, 0)],
                                         [(POOL_W, BF16), (POOL_W, BF16, ODD_IN, 1)], [POOL_W], name="odd_gate_bwd")
    dmixed = _gmm(dylin, pool_w, "nt", F32)
    dpool_w = _gmm(mixed, dylin, "tn", BF16)
    dproj = _pool(dmixed, 0, True, BF16, into=dproj_gate)
    dw_in = _mm(h, dproj, "tn", BF16, out_blocks=True)
    sent = on_w(dict(w_in=dw_in, w_out=dw_out, pool_w=dpool_w))
    dh = _mm(dproj, w_in, "nt", F32, b_blocks=True, after=(sent,))
    g_prev, dpre = _pre_bwd(g, dh, x, pre)
    return g_prev, dict(pre=dpre, post=dpost, pool_scale=dscale)


def _my_index():
    return 4 * lax.axis_index("x") + 2 * lax.axis_index("y") + lax.axis_index("c")


HBM_SPEC = pl.BlockSpec(memory_space=pltpu.HBM)
SEM_SPEC = pl.BlockSpec(memory_space=pltpu.SEMAPHORE)
SPLIT_EFFECT = pltpu.SideEffectType.DATAFLOW_SIDE_EFFECTING


def _device_of(j):
    return (j // 4, (j // 2) % 2, j % 2)


def _split_copy(srcs, lands, send_sems, recv_sems, gather, i, j, dst_slot, recv_slot):
    return pltpu.make_async_remote_copy(
        src_ref=srcs[i] if gather else srcs[i].at[j], dst_ref=lands[i].at[dst_slot],
        send_sem=send_sems.at[i * N_DEV + j], recv_sem=recv_sems.at[i * N_DEV + recv_slot],
        device_id=_device_of(j), device_id_type=MESH_ID)


def _own_copy(srcs, lands, send_sems, gather, i, me):
    return pltpu.make_async_copy(srcs[i] if gather else srcs[i].at[me], lands[i].at[me], send_sems.at[i * N_DEV + me])


def _xchg_start(name, srcs, gather, after=()):
    n = len(srcs)
    n_in = n + len(after)

    def body(*refs):
        src_refs = refs[:n]
        send_sems, recv_sems, token = refs[n_in], refs[n_in + 1], refs[-1]
        land_refs = refs[n_in + 2 + n:n_in + 2 + 2 * n]
        me = _my_index()
        for j in range(N_DEV):
            @pl.when(me != j)
            def _(j=j):
                for i in range(n):
                    _split_copy(src_refs, land_refs, send_sems, recv_sems, gather, i, j, me, me).start()
        for i in range(n):
            _own_copy(src_refs, land_refs, send_sems, gather, i, me).start()
        token[...] = jnp.zeros_like(token)

    land_shapes = [((N_DEV,) + a.shape) if gather else a.shape for a in srcs]
    thru = ([pltpu.HBM(a.shape, a.dtype) for a in srcs] + [pltpu.HBM(s, a.dtype) for s, a in zip(land_shapes, srcs)])
    res = pl.pallas_call(
        body, name=name,
        out_shape=(pltpu.SemaphoreType.DMA((n * N_DEV,)), pltpu.SemaphoreType.DMA((n * N_DEV,)), *thru,
                   SDS((8, LANES), F32)),
        in_specs=[HBM_SPEC] * n + [pl.BlockSpec(memory_space=pl.ANY)] * len(after),
        out_specs=(SEM_SPEC, SEM_SPEC, *([HBM_SPEC] * (2 * n)), pl.BlockSpec(memory_space=pltpu.VMEM)),
        input_output_aliases={i: 2 + i for i in range(n)},
        compiler_params=pltpu.CompilerParams(has_side_effects=SPLIT_EFFECT),
    )(*[pltpu.with_memory_space_constraint(a, pltpu.HBM) for a in srcs], *after)
    return res[0], res[1], list(res[2:2 + n]), list(res[2 + n:2 + 2 * n]), res[-1]


def _xchg_wait(name, started, gather, after):
    send_sems, recv_sems, srcs, lands, _ = started
    n = len(srcs)

    def body(*refs):
        src_refs, land_refs = refs[:n], refs[n:2 * n]
        send_r, recv_r = refs[2 * n], refs[2 * n + 1]
        me = _my_index()
        for j in range(N_DEV):
            @pl.when(me != j)
            def _(j=j):
                for i in range(n):
                    _split_copy(src_refs, land_refs, send_r, recv_r, gather, i, j, me, me).wait_send()
                    _split_copy(src_refs, land_refs, send_r, recv_r, gather, i, j, j, j).wait_recv()
        for i in range(n):
            _own_copy(src_refs, land_refs, send_r, gather, i, me).wait()

    thru = [pltpu.HBM(a.shape, a.dtype) for a in list(srcs) + list(lands)]
    res = pl.pallas_call(
        body, name=name, out_shape=tuple(thru),
        in_specs=[HBM_SPEC] * (2 * n) + [SEM_SPEC, SEM_SPEC] + [pl.BlockSpec(memory_space=pl.ANY)] * len(after),
        out_specs=tuple([HBM_SPEC] * (2 * n)),
        input_output_aliases={i: i for i in range(2 * n)},
        compiler_params=pltpu.CompilerParams(has_side_effects=SPLIT_EFFECT),
    )(*srcs, *lands, send_sems, recv_sems, *after)
    return list(res[n:])


def _adam_layers(w, slot_list, m, v, name):
    n_l, r, c = w.shape
    ns = slot_list[0].shape[0]
    tr = r
    while tr * c * 4 > (1 << 20) and tr % 16 == 0:
        tr //= 2
    assert r % tr == 0 and len(slot_list) == n_l

    def body(*refs):
        w_ref, slot_refs = refs[0], refs[1:1 + n_l]
        m_ref, v_ref, go_ref, d_ref, mo_ref, vo_ref = refs[1 + n_l:]
        layer = pl.program_id(0)
        g = None
        for l, g_ref in enumerate(slot_refs):
            gl = g_ref[0].astype(F32)
            for s in range(1, ns):
                gl = gl + g_ref[s].astype(F32)
            g = gl if g is None else jnp.where(layer == l, gl, g)
        mn = ADAM_B1 * m_ref[...] + (1.0 - ADAM_B1) * g
        vn = ADAM_B2 * v_ref[...] + (1.0 - ADAM_B2) * (g * g)
        m_hat = mn / (1.0 - ADAM_B1 ** ADAM_STEP)
        v_hat = vn / (1.0 - ADAM_B2 ** ADAM_STEP)
        go_ref[...] = g
        d_ref[...] = -ADAM_LR * (m_hat / (jnp.sqrt(v_hat) + ADAM_EPS) + ADAM_WD * w_ref[...])
        mo_ref[...] = mn
        vo_ref[...] = vn

    blk = pl.BlockSpec((None, tr, c), lambda l, i: (l, i, 0))
    slot_specs = [pl.BlockSpec((ns, tr, c), lambda l, i, k=k: (0, jnp.where(l == k, i, 0), 0)) for k in range(n_l)]
    return pl.pallas_call(
        body, name=name, grid=(n_l, r // tr),
        in_specs=[blk] + slot_specs + [blk, blk],
        out_specs=[blk] * 4, out_shape=[SDS((n_l, r, c), F32)] * 4,
        compiler_params=_cparams(("arbitrary", "arbitrary")),
    )(w, *slot_list, m, v)


def _adam(w, gslots, m, v, name):
    r, c = w.shape
    ns = gslots.shape[0]
    tr = r
    while tr * c * 4 > (1 << 20) and tr % 16 == 0:
        tr //= 2
    assert r % tr == 0

    def body(w_ref, g_ref, m_ref, v_ref, go_ref, d_ref, mo_ref, vo_ref):
        g = g_ref[0].astype(F32)
        for s in range(1, ns):
            g = g + g_ref[s].astype(F32)
        wv = w_ref[...]
        mn = ADAM_B1 * m_ref[...] + (1.0 - ADAM_B1) * g
        vn = ADAM_B2 * v_ref[...] + (1.0 - ADAM_B2) * (g * g)
        m_hat = mn / (1.0 - ADAM_B1 ** ADAM_STEP)
        v_hat = vn / (1.0 - ADAM_B2 ** ADAM_STEP)
        go_ref[...] = g
        d_ref[...] = -ADAM_LR * (m_hat / (jnp.sqrt(v_hat) + ADAM_EPS) + ADAM_WD * wv)
        mo_ref[...] = mn
        vo_ref[...] = vn

    blk = pl.BlockSpec((tr, c), lambda i: (i, 0))
    return pl.pallas_call(
        body, name=name, grid=(r // tr,),
        in_specs=[blk, pl.BlockSpec((ns, tr, c), lambda i: (0, i, 0)), blk, blk],
        out_specs=[blk] * 4, out_shape=[SDS((r, c), F32)] * 4,
        compiler_params=_cparams(("parallel",)),
    )(w, gslots, m, v)


def _sum_slots(slots, name):
    ns, r, c = slots.shape

    def body(g_ref, o_ref):
        g = g_ref[0]
        for s in range(1, ns):
            g = g + g_ref[s]
        o_ref[...] = g

    return pl.pallas_call(
        body, name=name, grid=(1,),
        in_specs=[pl.BlockSpec((ns, r, c), lambda i: (0, 0, 0))], out_specs=pl.BlockSpec((r, c), lambda i: (0, 0)),
        out_shape=SDS((r, c), F32), compiler_params=_cparams(("arbitrary",)),
    )(slots)


def _adam_params(params, name):
    n = len(params)

    def body(*refs):
        ins, outs = refs[:5 * n], refs[5 * n:]
        for p in range(n):
            w_ref, m_ref, v_ref, g_first, g_rest = ins[5 * p:5 * p + 5]
            go_ref, d_ref, mo_ref, vo_ref = outs[4 * p:4 * p + 4]
            for part, g_ref in ((slice(0, 1), g_first), (slice(1, w_ref.shape[0]), g_rest)):
                g = g_ref[...]
                mn = ADAM_B1 * m_ref[part] + (1.0 - ADAM_B1) * g
                vn = ADAM_B2 * v_ref[part] + (1.0 - ADAM_B2) * (g * g)
                m_hat = mn / (1.0 - ADAM_B1 ** ADAM_STEP)
                v_hat = vn / (1.0 - ADAM_B2 ** ADAM_STEP)
                go_ref[part] = g
                d_ref[part] = -ADAM_LR * (m_hat / (jnp.sqrt(v_hat) + ADAM_EPS) + ADAM_WD * w_ref[part])
                mo_ref[part] = mn
                vo_ref[part] = vn

    def whole(a):
        return pl.BlockSpec(a.shape, lambda i, nd=a.ndim: (0,) * nd)

    flat = [pltpu.with_memory_space_constraint(a, pltpu.HBM) for prm in params for a in prm]
    outs = pl.pallas_call(
        body, name=name, grid=(1,),
        in_specs=[whole(a) for a in flat],
        out_specs=[whole(prm[0]) for prm in params for _ in range(4)],
        out_shape=[SDS(prm[0].shape, F32) for prm in params for _ in range(4)],
        compiler_params=_cparams(("arbitrary",)),
    )(*flat)
    return [outs[4 * p:4 * p + 4] for p in range(n)]


SMALL_NAMES = ("pre_norm", "post_norm", "ssm_a_re", "ssm_a_im", "ssm_log_dt", "ssm_b_re", "ssm_b_im", "ssm_c_re",
               "ssm_c_im", "ssm_d", "ssm_glu_b")
SSM_NAMES = ("ssm_a_re", "ssm_a_im", "ssm_log_dt", "ssm_b_re", "ssm_b_im", "ssm_c_re", "ssm_c_im")
SHARDED_NAMES = ("even_w_in", "even_w_out", "ssm_glu_w", "odd_w_in", "pool_w", "odd_w_out")
WEIGHT_ORDER = ("pre_norm", "post_norm", "even_w_in", "even_w_out", "ssm_a_re", "ssm_a_im", "ssm_log_dt", "ssm_b_re",
                "ssm_b_im", "ssm_c_re", "ssm_c_im", "ssm_d", "ssm_glu_w", "ssm_glu_b", "odd_w_in", "pool_w",
                "pool_scale", "odd_w_out")
PACK_ROWS_ALIGN = 8


def _pack(parts):
    flat = jnp.concatenate([p.reshape(-1).astype(F32) for p in parts])
    rows = -(-flat.shape[0] // (LANES * PACK_ROWS_ALIGN)) * PACK_ROWS_ALIGN
    return jnp.pad(flat, (0, rows * LANES - flat.shape[0])).reshape(rows, LANES)


def _unpack(packed, shapes):
    flat = packed.reshape(-1)
    out, off = [], 0
    for shp in shapes:
        size = math.prod(shp)
        out.append(flat[off:off + size].reshape(shp))
        off += size
    return out


EVEN_SHARDED = ("w_in", "w_out", "glu_w")
ODD_SHARDED = ("w_in", "pool_w", "w_out")
FAMILY = {(0, "w_in"): "even_w_in", (0, "w_out"): "even_w_out", (0, "glu_w"): "ssm_glu_w",
          (1, "w_in"): "odd_w_in", (1, "pool_w"): "pool_w", (1, "w_out"): "odd_w_out"}


def _sharded_keys(layer):
    return EVEN_SHARDED if layer % 2 == 0 else ODD_SHARDED


def _local_step(x, tgt, small, get_weights, on_w, on_ssm, on_grads, zero=0.0):
    tables = _rope_tables(zero) + (_attention_bias(zero),)
    preps, prep_vjps = [], []
    for i in range(2):
        out, vjp = jax.vjp(_ssm_prep, small["ssm_a_re"][i] + zero, small["ssm_a_im"][i], small["ssm_log_dt"][i],
                           small["ssm_b_re"][i], small["ssm_b_im"][i], small["ssm_c_re"][i], small["ssm_c_im"][i])
        preps.append(out)
        prep_vjps.append(vjp)

    def layer_args(layer, wts):
        i = layer // 2
        pre, post = _row(small["pre_norm"][layer]) + wts.get("token", 0.0), _row(small["post_norm"][layer])
        if layer % 2 == 0:
            return (pre, post, wts["w_in"], wts["late"], _row(small["ssm_glu_b"][i]), _row(small["ssm_d"][i]),
                    preps[i], tables)
        return (pre, post, wts["w_in"], wts["pool_w"], _row(wts["pool_scale"]), wts["w_out"])

    saved, args = [], []
    cur = x
    for layer in range(4):
        after = (cur,) if layer else (cur, tables[0], tables[3], preps[0][1], preps[0][2], preps[1][1], preps[1][2])
        args.append(layer_args(layer, get_weights(layer, after)))
        cur, sv = (_even_fwd if layer % 2 == 0 else _odd_fwd)(cur, *args[layer])
        saved.append(sv)
    g, sq = _loss_grad(cur, tgt)
    loss = 0.5 * jnp.sum(sq) / D

    lg = [None] * 4
    token = jnp.zeros((), F32)
    for layer in reversed(range(4)):
        largs = list(args[layer])
        largs[1] = largs[1] + token
        hooks = dict(on_w=functools.partial(on_w, layer))
        ssm_grads = []
        if layer % 2 == 0:
            def ssm_hook(cotangents, layer=layer):
                ssm_grads.append(prep_vjps[layer // 2](cotangents))
                return on_ssm(layer, ssm_grads[0])

            hooks["on_ssm"] = ssm_hook
        g, lg[layer] = (_even_bwd if layer % 2 == 0 else _odd_bwd)(g, saved[layer], *largs, **hooks)
        if ssm_grads:
            lg[layer]["ssm"] = ssm_grads[0]
        token = on_grads(layer, lg[layer])
    return loss, g, token


def _to_slots(key, gfull):
    if key == "w_in":
        return gfull
    if key in ("w_out", "glu_w"):
        rr, nn = gfull.shape
        return gfull.reshape(N_DEV, rr // N_DEV, nn)
    assert key == "pool_w"
    gg, rr, nn = gfull.shape
    return gfull.reshape(gg, N_DEV, rr // N_DEV, nn).transpose(1, 0, 2, 3)


def _from_gathered(key, gat):
    if key == "w_in":
        return gat
    if key in ("w_out", "glu_w"):
        _, rr, nn = gat.shape
        return gat.reshape(N_DEV * rr, nn)
    assert key == "pool_w"
    _, gg, rr, nn = gat.shape
    return gat.transpose(1, 0, 2, 3).reshape(gg, N_DEV * rr, nn)


def kernel(x, pre_norm, post_norm, even_w_in, even_w_out, ssm_a_re, ssm_a_im, ssm_log_dt, ssm_b_re, ssm_b_im, ssm_c_re, ssm_c_im, ssm_d, ssm_glu_w, ssm_glu_b, odd_w_in, pool_w, pool_scale, odd_w_out, loss_target, m_pre_norm, m_post_norm, m_even_w_in, m_even_w_out, m_ssm_a_re, m_ssm_a_im, m_ssm_log_dt, m_ssm_b_re, m_ssm_b_im, m_ssm_c_re, m_ssm_c_im, m_ssm_d, m_ssm_glu_w, m_ssm_glu_b, m_odd_w_in, m_pool_w, m_pool_scale, m_odd_w_out, v_pre_norm, v_post_norm, v_even_w_in, v_even_w_out, v_ssm_a_re, v_ssm_a_im, v_ssm_log_dt, v_ssm_b_re, v_ssm_b_im, v_ssm_c_re, v_ssm_c_im, v_ssm_d, v_ssm_glu_w, v_ssm_glu_b, v_odd_w_in, v_pool_w, v_pool_scale, v_odd_w_out):
    w = dict(pre_norm=pre_norm, post_norm=post_norm, even_w_in=even_w_in, even_w_out=even_w_out, ssm_a_re=ssm_a_re,
             ssm_a_im=ssm_a_im, ssm_log_dt=ssm_log_dt, ssm_b_re=ssm_b_re, ssm_b_im=ssm_b_im, ssm_c_re=ssm_c_re,
             ssm_c_im=ssm_c_im, ssm_d=ssm_d, ssm_glu_w=ssm_glu_w, ssm_glu_b=ssm_glu_b, odd_w_in=odd_w_in,
             pool_w=pool_w, pool_scale=pool_scale, odd_w_out=odd_w_out)
    mom = dict(pre_norm=m_pre_norm, post_norm=m_post_norm, even_w_in=m_even_w_in, even_w_out=m_even_w_out,
               ssm_a_re=m_ssm_a_re, ssm_a_im=m_ssm_a_im, ssm_log_dt=m_ssm_log_dt, ssm_b_re=m_ssm_b_re,
               ssm_b_im=m_ssm_b_im, ssm_c_re=m_ssm_c_re, ssm_c_im=m_ssm_c_im, ssm_d=m_ssm_d, ssm_glu_w=m_ssm_glu_w,
               ssm_glu_b=m_ssm_glu_b, odd_w_in=m_odd_w_in, pool_w=m_pool_w, pool_scale=m_pool_scale,
               odd_w_out=m_odd_w_out)
    var = dict(pre_norm=v_pre_norm, post_norm=v_post_norm, even_w_in=v_even_w_in, even_w_out=v_even_w_out,
               ssm_a_re=v_ssm_a_re, ssm_a_im=v_ssm_a_im, ssm_log_dt=v_ssm_log_dt, ssm_b_re=v_ssm_b_re,
               ssm_b_im=v_ssm_b_im, ssm_c_re=v_ssm_c_re, ssm_c_im=v_ssm_c_im, ssm_d=v_ssm_d, ssm_glu_w=v_ssm_glu_w,
               ssm_glu_b=v_ssm_glu_b, odd_w_in=v_odd_w_in, pool_w=v_pool_w, pool_scale=v_pool_scale,
               odd_w_out=v_odd_w_out)
    me = _my_index()
    scale_cols = pool_scale.shape[1]

    def start_gather(tag, layer, keys, after=()):
        i = layer // 2
        shards = [w[FAMILY[(layer % 2, k)]][i].astype(BF16) for k in keys]
        if layer % 2 == 1:
            shards.append(jnp.pad(pool_scale[i][None], ((0, PACK_ROWS_ALIGN - 1), (0, 0))))
        return _xchg_start(f"gather_start_{tag}", shards, True, after)

    gather_started = {0: start_gather("0", 0, EVEN_SHARDED[:1])}
    small = {nm: w[nm] for nm in SMALL_NAMES}

    def get_weights(layer, after):
        keys = EVEN_SHARDED[:1] if layer == 0 else _sharded_keys(layer)
        lands = _xchg_wait(f"gather_wait_{layer}", gather_started[layer], True, after)
        wts = {k: _from_gathered(k, gat) for k, gat in zip(keys, lands)}
        if layer % 2 == 1:
            wts["pool_scale"] = lands[-1][:, 0, :].reshape(N_DEV * scale_cols)
        if layer == 0:
            prev = gather_started["0_late"] = start_gather("0_late", 0, EVEN_SHARDED[1:], after=(lands[0],))
            for later in (1, 2, 3):
                prev = gather_started[later] = start_gather(str(later), later, _sharded_keys(later), after=(prev[4],))
            wts["token"] = sum(gather_started[tag][4][0, 0] for tag in ("0_late", 1, 2, 3))

            def late(after_late):
                late_lands = _xchg_wait("gather_wait_0_late", gather_started["0_late"], True, (after_late,))
                return tuple(_from_gathered(k, gat) for k, gat in zip(EVEN_SHARDED[1:], late_lands))

            wts["late"] = late
        elif layer == 2:
            wts["late"] = lambda after_late: (wts["w_out"], wts["glu_w"])
        return wts

    scatter_started = []

    def on_w(layer, gw):
        keys = tuple(k for k in _sharded_keys(layer) if k in gw)
        started = _xchg_start(f"scatter_start_{layer}_{keys[0]}", [_to_slots(k, gw[k]) for k in keys], False)
        scatter_started.append((layer, keys, started))
        return started[4]

    def wait_scatters(layers, after):
        for layer, keys, started in scatter_started:
            if layer in layers:
                lands = _xchg_wait(f"scatter_wait_{layer}_{keys[0]}", started, False, after)
                for k, land in zip(keys, lands):
                    recv[(layer, k)] = land

    packed_names = ("pre_norm", "post_norm") + SSM_NAMES + ("ssm_d", "ssm_glu_b")
    tails = {nm: (SSM_GROUPS, SSM_STATE * SSM_GROUP) if nm in ("ssm_b_re", "ssm_b_im") else w[nm].shape[1:]
             for nm in packed_names}

    layer_grads = {}
    early_started, mid_started = [], []

    def on_ssm(layer, ssm_grads):
        if layer != 0:
            return None
        mid_started.append(_xchg_start("mid_start", [_pack(list(ssm_grads))], True))
        return mid_started[0][4]

    def on_grads(layer, lg):
        layer_grads[layer] = lg
        zero = jnp.zeros((), F32)
        if layer == 1:
            lgs = layer_grads
            early = ([jnp.concatenate([lgs[l][k] for l in (1, 2, 3)], axis=0) for k in ("pre", "post")]
                     + list(lgs[2]["ssm"]) + [lgs[2]["ssm_d"], lgs[2]["glu_b"],
                                              jnp.concatenate([lgs[1]["pool_scale"], lgs[3]["pool_scale"]], axis=0)])
            early_started.append(_xchg_start("small_start", [_pack(early)], True))
            zero = zero + early_started[0][4][0, 0]
        return zero

    loss_local, grad_x, token = _local_step(x[0], loss_target[0], small, get_weights, on_w, on_ssm, on_grads,
                                            zero=gather_started[0][4][0, 0])

    lg0 = layer_grads[0]
    late_started = _xchg_start("late_start", [_pack([lg0["pre"], lg0["post"], lg0["ssm_d"], lg0["glu_b"],
                                                     loss_local.reshape(1)]) + token], True)

    def adam_family(parity, k):
        nm = FAMILY[(parity, k)]
        shp = w[nm].shape
        cols = shp[-1]
        slot_list = [recv[(parity + 2 * i, k)].reshape(N_DEV, -1, cols) for i in range(2)]
        outs = _adam_layers(w[nm].reshape(2, -1, cols), slot_list, mom[nm].reshape(2, -1, cols),
                            var[nm].reshape(2, -1, cols), name=f"adam_{nm}")
        return [o.reshape(shp) for o in outs]

    recv, res = {}, {}
    wait_scatters((3, 1), (late_started[4],))
    for k in ODD_SHARDED:
        res[FAMILY[(1, k)]] = adam_family(1, k)
    odd_done = tuple(res[FAMILY[(1, k)]][0] for k in ODD_SHARDED)

    (early_slots,) = _xchg_wait("small_wait", early_started[0], True, odd_done)
    (mid_slots,) = _xchg_wait("mid_wait", mid_started[0], True, odd_done)
    early_shapes = [(w[nm].shape[0] - 1,) + tails[nm] for nm in packed_names] + [(2, N_DEV * scale_cols)]
    g_early = _unpack(_sum_slots(early_slots, "sum_small_early"), early_shapes)
    g_mid = _unpack(_sum_slots(mid_slots, "sum_small_mid"), [(1,) + tails[nm] for nm in SSM_NAMES])

    (late_slots,) = _xchg_wait("late_wait", late_started, True, (g_early[0], g_mid[0]))
    wait_scatters((2, 0), (late_slots,))
    for k in EVEN_SHARDED:
        res[FAMILY[(0, k)]] = adam_family(0, k)

    late_names = ("pre_norm", "post_norm", "ssm_d", "ssm_glu_b")
    g_late = _unpack(_sum_slots(late_slots, "sum_small_late"), [(1,) + tails[nm] for nm in late_names] + [(1,)])
    g_first = dict(zip(late_names, g_late))
    g_first.update(zip(SSM_NAMES, g_mid))
    dense = lambda nm, a: a.reshape((a.shape[0],) + tails[nm])
    outs = _adam_params([(dense(nm, w[nm]), dense(nm, mom[nm]), dense(nm, var[nm]), g_first[nm], g_early[j])
                         for j, nm in enumerate(packed_names)], "adam_small")
    for nm, four in zip(packed_names, outs):
        res[nm] = [o.reshape(w[nm].shape) for o in four]
    loss = g_late[-1].reshape(())
    g_scale = lax.dynamic_slice_in_dim(g_early[-1], me * scale_cols, scale_cols, axis=1)
    pad = ((0, PACK_ROWS_ALIGN - 2), (0, 0))
    outs = _adam(jnp.pad(pool_scale, pad), jnp.pad(g_scale, pad)[None], jnp.pad(m_pool_scale, pad),
                 jnp.pad(v_pool_scale, pad), name="adam_pool_scale")
    res["pool_scale"] = [o[:2] for o in outs]

    out = [loss, grad_x[None]]
    for kind in range(4):
        out += [res[nm][kind] for nm in WEIGHT_ORDER]
    return tuple(out)
```

```python
import functools
import math

import jax
import jax.numpy as jnp
from jax import lax
from jax.experimental import pallas as pl
from jax.experimental.pallas import tpu as pltpu

F32 = jnp.float32
BF16 = jnp.bfloat16
SDS = jax.ShapeDtypeStruct

N_DEV = 8
S = 2048
D = 1024
HEAD_DIM = 64
ROT_DIM = 16
ROPE_THETA = 500000.0
ATT_W = 1024
SSM_W = 512
SSM_GROUPS = 32
SSM_GROUP = 16
SSM_STATE = 64
N_CPLX = SSM_GROUPS * SSM_STATE
POOL_W = 2048
POOL_GROUP = 512
EVEN_IN = 5120
EVEN_OUT = 1536
ODD_IN = 4096
RMS_EPS = 1e-6
LANES = 128
VMEM_LIMIT = 48 * 1024 * 1024

ADAM_LR = 0.001
ADAM_B1 = 0.9
ADAM_B2 = 0.999
ADAM_EPS = 1e-08
ADAM_WD = 0.01
ADAM_STEP = 10

MESH_ID = pl.DeviceIdType.MESH
NN = (((1,), (0,)), ((), ()))
NT = (((1,), (1,)), ((), ()))
TN = (((0,), (0,)), ((), ()))
_DN = {"nn": NN, "nt": NT, "tn": TN}


def _cparams(sem):
    return pltpu.CompilerParams(dimension_semantics=sem, vmem_limit_bytes=VMEM_LIMIT)


MM_TILES = (1024, 768, 512)


def _tile(dim):
    return next((t for t in MM_TILES if dim % t == 0), dim)


NT_BLOCKS_PER_STEP = 4


def _mm(a, b, mode, out_dtype, b_blocks=False, out_blocks=False, a_cols=None, after=()):
    if b_blocks:
        nblk, rows, cb = b.shape
        b2_shape = (rows, nblk * cb)
    else:
        b2_shape = b.shape
    a_shape = a.shape if a_cols is None else (a.shape[0], a_cols[1])
    if mode == "nn":
        (m, k), n = a_shape, b2_shape[1]
    elif mode == "nt":
        (m, k), n = a_shape, b2_shape[0]
    else:
        (k, m), n = a_shape, b2_shape[1]
    tm, tn, tk = _tile(m), _tile(n), _tile(k)
    per_step = 1
    if b_blocks and mode == "nn":
        tn = cb
    if b_blocks and mode == "nt":
        per_step = NT_BLOCKS_PER_STEP
        tk = per_step * cb
        tn = min(tn, MM_TILES[-1])
    if out_blocks:
        tn = n // N_DEV
        tk = k
    nk = k // tk
    a_unit = tm if mode == "tn" else tk
    assert a_cols is None or a_cols[0] % a_unit == 0
    a_off = 0 if a_cols is None else a_cols[0] // a_unit

    def body(a_ref, b_ref, *rest):
        o_ref, acc_ref = rest[-2:]
        kk = pl.program_id(2)
        if per_step == 1:
            part = lax.dot_general(a_ref[...].astype(BF16), b_ref[...].astype(BF16), _DN[mode],
                                   preferred_element_type=F32)
        else:
            part = None
            for blk in range(per_step):
                d = lax.dot_general(a_ref[:, blk * cb:(blk + 1) * cb].astype(BF16), b_ref[blk].astype(BF16), NT,
                                    preferred_element_type=F32)
                part = d if part is None else part + d
        if nk == 1:
            o_ref[...] = part.astype(o_ref.dtype)
            return

        @pl.when(kk == 0)
        def _():
            acc_ref[...] = part

        @pl.when((kk > 0) & (kk < nk - 1))
        def _():
            acc_ref[...] += part

        @pl.when(kk == nk - 1)
        def _():
            o_ref[...] = (acc_ref[...] + part).astype(o_ref.dtype)

    if mode == "nn":
        a_spec = pl.BlockSpec((tm, tk), lambda i, j, kk: (i, a_off + kk))
        b_spec = pl.BlockSpec((tk, tn), lambda i, j, kk: (kk, j))
    elif mode == "nt":
        a_spec = pl.BlockSpec((tm, tk), lambda i, j, kk: (i, a_off + kk))
        b_spec = pl.BlockSpec((tn, tk), lambda i, j, kk: (j, kk))
    else:
        a_spec = pl.BlockSpec((tk, tm), lambda i, j, kk: (kk, a_off + i))
        b_spec = pl.BlockSpec((tk, tn), lambda i, j, kk: (kk, j))
    if b_blocks and mode == "nn":
        b_spec = pl.BlockSpec((None, tk, cb), lambda i, j, kk: (j, kk, 0))
    if b_blocks and mode == "nt":
        b_spec = pl.BlockSpec((per_step, tn, cb), lambda i, j, kk: (kk, j, 0))
    out_spec = pl.BlockSpec((tm, tn), lambda i, j, kk: (i, j))
    out_shape = SDS((m, n), out_dtype)
    if out_blocks:
        out_spec = pl.BlockSpec((None, tm, tn), lambda i, j, kk: (j, i, 0))
        out_shape = SDS((N_DEV, m, tn), out_dtype)
    return pl.pallas_call(
        body, name=f"mm_{mode}_{m}x{k}x{n}",
        grid=(m // tm, n // tn, nk),
        in_specs=[a_spec, b_spec] + [pl.BlockSpec(memory_space=pl.ANY)] * len(after),
        out_specs=out_spec,
        out_shape=out_shape,
        scratch_shapes=[pltpu.VMEM((tm, tn) if nk > 1 else (8, LANES), F32)],
        compiler_params=_cparams(("parallel", "parallel", "arbitrary")),
    )(a, b, *after)


def _gmm(a, b, mode, out_dtype, tm=S):
    ng, gw = POOL_W // POOL_GROUP, POOL_GROUP
    ns = S // tm
    if mode in ("nn", "nt"):
        def body(a_ref, b_ref, o_ref):
            o_ref[...] = lax.dot_general(a_ref[...].astype(BF16), b_ref[...].astype(BF16), _DN[mode],
                                         preferred_element_type=F32).astype(o_ref.dtype)

        return pl.pallas_call(
            body, name=f"gmm_{mode}", grid=(ng, ns),
            in_specs=[pl.BlockSpec((tm, gw), lambda g, i: (i, g)),
                      pl.BlockSpec((None, gw, gw), lambda g, i: (g, 0, 0))],
            out_specs=pl.BlockSpec((tm, gw), lambda g, i: (i, g)),
            out_shape=SDS((S, POOL_W), out_dtype),
            compiler_params=_cparams(("parallel", "parallel")),
        )(a, b)

    def body_tn(a_ref, b_ref, o_ref, acc_ref):
        i = pl.program_id(1)

        @pl.when(i == 0)
        def _():
            acc_ref[...] = jnp.zeros_like(acc_ref)

        acc_ref[...] += lax.dot_general(a_ref[...].astype(BF16), b_ref[...].astype(BF16), TN,
                                        preferred_element_type=F32)

        @pl.when(i == ns - 1)
        def _():
            o_ref[...] = acc_ref[...].astype(o_ref.dtype)

    return pl.pallas_call(
        body_tn, name="gmm_tn", grid=(ng, ns),
        in_specs=[pl.BlockSpec((tm, gw), lambda g, i: (i, g)),
                  pl.BlockSpec((tm, gw), lambda g, i: (i, g))],
        out_specs=pl.BlockSpec((None, gw, gw), lambda g, i: (g, 0, 0)),
        out_shape=SDS((ng, gw, gw), out_dtype),
        scratch_shapes=[pltpu.VMEM((gw, gw), F32)],
        compiler_params=_cparams(("parallel", "arbitrary")),
    )(a, b)


def _rowwise(fn, inputs, out_defs, acc_defs=(), tm=256, name=None, after=()):
    n_in, n_out, n_acc = len(inputs), len(out_defs), len(acc_defs)
    n_after = len(after)
    in_specs, args = [], []
    for arr, width, cb in inputs:
        if arr.shape[0] == 1:
            in_specs.append(pl.BlockSpec((1, width), lambda i, cb=cb: (0, cb)))
        else:
            in_specs.append(pl.BlockSpec((tm, width), lambda i, cb=cb: (i, cb)))
        args.append(arr)
    out_defs = [d if len(d) == 4 else (d[0], d[1], d[0], 0) for d in out_defs]
    out_shape = [SDS((S, ww), dt) for _, dt, ww, _ in out_defs] + [SDS((1, w), F32) for w in acc_defs]
    out_specs = ([pl.BlockSpec((tm, w), lambda i, cb=cb: (i, cb)) for w, _, _, cb in out_defs]
                 + [pl.BlockSpec((1, w), lambda i: (0, 0)) for w in acc_defs])

    def kern(*refs):
        vals = [r[...] for r in refs[:n_in]]
        outs, accs = fn(*vals)
        out_refs = refs[n_in + n_after:]
        for r, v in zip(out_refs[:n_out], outs):
            r[...] = v.astype(r.dtype)
        if n_acc:
            acc_refs = out_refs[n_out:]

            @pl.when(pl.program_id(0) == 0)
            def _():
                for r in acc_refs:
                    r[...] = jnp.zeros_like(r)

            for r, v in zip(acc_refs, accs):
                r[...] += jnp.sum(v, axis=0, keepdims=True)

    res = pl.pallas_call(
        kern, name=name, grid=(S // tm,), in_specs=in_specs + [pl.BlockSpec(memory_space=pl.ANY)] * n_after,
        out_specs=out_specs, out_shape=out_shape, compiler_params=_cparams(("arbitrary",)),
    )(*args, *after)
    return res


def _sigmoid(x):
    return 1.0 / (1.0 + jnp.exp(-x))


def _silu_and_grad(x):
    s = _sigmoid(x)
    return x * s, s * (1.0 + x * (1.0 - s))


_GELU_K = math.sqrt(2.0 / math.pi)
_GELU_C = 0.044715


def _gelu_and_grad(x):
    t = jnp.tanh(_GELU_K * (x + _GELU_C * (x * x * x)))
    cdf = 0.5 * (1.0 + t)
    grad = cdf + 0.5 * x * (1.0 - t * t) * (_GELU_K * (1.0 + 3.0 * _GELU_C * x * x))
    return x * cdf, grad


def _rms(xv, gain):
    r = lax.rsqrt(jnp.mean(xv * xv, axis=-1, keepdims=True) + RMS_EPS)
    return xv * r * gain


def _rms_bwd(dout, xv, gain):
    r = lax.rsqrt(jnp.mean(xv * xv, axis=-1, keepdims=True) + RMS_EPS)
    xhat = xv * r
    dxhat = dout * gain
    dx = r * (dxhat - xhat * jnp.mean(dxhat * xhat, axis=-1, keepdims=True))
    return dx, dout * xhat


def _norm_fwd(x, gain):
    (h,) = _rowwise(lambda xv, g: ((_rms(xv, g),), ()), [(x, D, 0), (gain, D, 0)], [(D, BF16)], name="norm_fwd")
    return h


def _post_fwd(x, y, gain, next_gain):
    def fn(xv, yv, g, gn):
        out = xv + _rms(yv, g)
        return (out, _rms(out, gn)), ()

    return _rowwise(fn, [(x, D, 0), (y, D, 0), (gain, D, 0), (next_gain, D, 0)], [(D, F32), (D, BF16)],
                    name="post_fwd")


def _post_fwd_loss(x, y, gain, tgt):
    def fn(xv, yv, g, tv):
        e = xv + _rms(yv, g) - tv
        return (e * (1.0 / D),), (e * e,)

    return _rowwise(fn, [(x, D, 0), (y, D, 0), (gain, D, 0), (tgt, D, 0)], [(D, F32)], [D], name="post_fwd_loss")


def _post_bwd(g, y, gain):
    def fn(gv, yv, gn):
        dx, dg = _rms_bwd(gv, yv, gn)
        return (dx,), (dg,)

    return _rowwise(fn, [(g, D, 0), (y, D, 0), (gain, D, 0)], [(D, BF16)], [D], name="post_bwd")


def _pre_bwd(g, dh, x, gain):
    def fn(gv, dhv, xv, gn):
        dx, dg = _rms_bwd(dhv, xv, gn)
        return (gv + dx,), (dg,)

    return _rowwise(fn, [(g, D, 0), (dh, D, 0), (x, D, 0), (gain, D, 0)], [(D, F32)], [D], name="pre_bwd")


def _pool(u_arr, col_block, transpose, out_dtype, into=None, tc=256):
    n_t = POOL_W // tc
    per_group = POOL_GROUP // tc

    def body(u_ref, *rest):
        o_ref = rest[-1]
        grp = pl.program_id(0) // per_group
        t = lax.broadcasted_iota(jnp.int32, (S, 1), 0)
        for g in range(POOL_W // POOL_GROUP):
            @pl.when(grp == g)
            def _(g=g):
                xv = u_ref[...]
                cnt = jnp.minimum(t + 1, 2 << g).astype(F32)
                cur = xv / cnt if transpose else xv
                for k in (1, 2, 4, 8)[:g + 1]:
                    if transpose:
                        cur = cur + jnp.where(t < S - k, pltpu.roll(cur, S - k, 0), 0.0)
                    else:
                        cur = cur + jnp.where(t >= k, pltpu.roll(cur, k, 0), 0.0)
                res = cur - xv if transpose else cur / cnt - xv
                o_ref[...] = res.astype(o_ref.dtype)

    in_specs = [pl.BlockSpec((S, tc), lambda c: (0, col_block * n_t + c))]
    args = [u_arr]
    if into is not None:
        in_specs.append(pl.BlockSpec(memory_space=pl.ANY))
        args.append(into)
    return pl.pallas_call(
        body, name="pool_bwd" if transpose else "pool_fwd", grid=(n_t,),
        in_specs=in_specs,
        out_specs=pl.BlockSpec((S, tc), lambda c: (0, c)),
        out_shape=SDS((S, POOL_W) if into is None else into.shape, out_dtype),
        input_output_aliases={} if into is None else {1: 0},
        compiler_params=_cparams(("parallel",)),
    )(*args)


def _rope_tables(zero):
    pos = jnp.arange(S, dtype=jnp.int32).astype(F32) + zero
    inv_freq = ROPE_THETA ** (-jnp.arange(0, ROT_DIM, 2, dtype=F32) / ROT_DIM)
    ang = pos[:, None] * inv_freq[None, :]
    cos8, sin8 = jnp.cos(ang), jnp.sin(ang)
    half = ROT_DIM // 2
    zeros = jnp.zeros((S, HEAD_DIM - ROT_DIM), F32)
    cos = jnp.concatenate([cos8, cos8, jnp.ones((S, HEAD_DIM - ROT_DIM), F32)], axis=1)
    lo = jnp.concatenate([-sin8, jnp.zeros((S, half), F32), zeros], axis=1)
    hi = jnp.concatenate([jnp.zeros((S, half), F32), sin8, zeros], axis=1)
    rep = LANES // HEAD_DIM
    return jnp.tile(cos, (1, rep)), jnp.tile(lo, (1, rep)), jnp.tile(hi, (1, rep))


def _rotate(xv, cos, lo, hi, transpose):
    width = xv.shape[1]
    rep = width // LANES
    wide = lambda tab: jnp.concatenate([tab] * rep, axis=1)
    half = ROT_DIM // 2
    up = pltpu.roll(xv, width - half, 1)
    dn = pltpu.roll(xv, half, 1)
    mixed = up * wide(lo) + dn * wide(hi)
    return xv * wide(cos) - mixed if transpose else xv * wide(cos) + mixed


def _qkv_prep(proj, tables):
    cos, lo, hi = tables

    def fn(x, c, l, h):
        rot = _rotate(x[:, :2 * ATT_W], c, l, h, False)
        return (jnp.concatenate([(rot[:, :ATT_W] * HEAD_DIM ** -0.5).astype(BF16), rot[:, ATT_W:].astype(BF16),
                                 x[:, 2 * ATT_W:].astype(BF16)], axis=1),), ()

    (qkv,) = _rowwise(fn, [(proj, 3 * ATT_W, 0), (cos, LANES, 0), (lo, LANES, 0), (hi, LANES, 0)],
                      [(3 * ATT_W, BF16)], name="qkv_prep")
    return qkv


ATT_T = 512


def _multiplicity(delta):
    ok = delta >= 0
    near = jnp.where(ok & (delta <= 128), 1.0, 0.0)
    mid = jnp.where(ok & (delta <= 512) & ((delta & 3) == 0), 1.0, 0.0)
    far = jnp.where(ok & ((delta & 15) == 0), 1.0, 0.0)
    return near + mid + far


def _attention_bias(zero):
    t = ATT_T
    pos = jnp.arange(t, dtype=jnp.int32) + jnp.asarray(zero).astype(jnp.int32)
    delta = jnp.arange(S // t, dtype=jnp.int32)[:, None, None] * t + pos[None, :, None] - pos[None, None, :]
    mult = _multiplicity(delta)
    return jnp.where(mult > 0.0, jnp.log(jnp.maximum(mult, 1.0)), -1e30).astype(F32)


def _head_split(v, first):
    zero = jnp.zeros_like(v)
    return [jnp.where(first, v, zero), jnp.where(first, zero, v)]


def _flash_fwd(qkv, bias):
    t = ATT_T
    n_hp = ATT_W // LANES

    def body(q_ref, k_ref, v_ref, b_ref, o_ref, lse_ref):
        i = pl.program_id(1)
        first = lax.broadcasted_iota(jnp.int32, (1, LANES), 1) < HEAD_DIM
        qs = _head_split(q_ref[...], first)

        def kv_step(j, carry):
            m0, l0, m1, l1, acc = carry
            off = pl.multiple_of(j * t, t)
            kb = k_ref[pl.ds(off, t), :]
            vs = _head_split(v_ref[pl.ds(off, t), :], first)
            bias_t = b_ref[i - j]
            new = []
            pv = None
            for h, (m_prev, l_prev) in enumerate(((m0, l0), (m1, l1))):
                s = lax.dot_general(qs[h], kb, NT, preferred_element_type=F32) + bias_t
                m_new = jnp.maximum(m_prev, jnp.max(s, axis=1, keepdims=True))
                p = jnp.exp(s - m_new)
                alpha = jnp.exp(m_prev - m_new)
                l_new = alpha * l_prev + jnp.sum(p, axis=1, keepdims=True)
                d = lax.dot_general(p.astype(BF16), vs[h], NN, preferred_element_type=F32)
                pv = d if pv is None else pv + d
                new.append((m_new, l_new, alpha))
            acc = acc * jnp.where(first, new[0][2], new[1][2]) + pv
            return new[0][0], new[0][1], new[1][0], new[1][1], acc

        neg = jnp.full((t, 1), -1e30, F32)
        zero = jnp.zeros((t, 1), F32)
        m0, l0, m1, l1, acc = lax.fori_loop(0, i + 1, kv_step, (neg, zero, neg, zero, jnp.zeros((t, LANES), F32)))
        o_ref[...] = acc * jnp.where(first, 1.0 / l0, 1.0 / l1)
        lse_ref[...] = jnp.where(first, m0 + jnp.log(l0), m1 + jnp.log(l1))

    blk = pl.BlockSpec((t, LANES), lambda hp, i: (i, hp))
    k_full = pl.BlockSpec((S, LANES), lambda hp, i: (0, n_hp + hp))
    v_full = pl.BlockSpec((S, LANES), lambda hp, i: (0, 2 * n_hp + hp))
    return pl.pallas_call(
        body, name="flash_fwd", grid=(n_hp, S // t),
        in_specs=[blk, k_full, v_full, pl.BlockSpec((S // t, t, t), lambda hp, i: (0, 0, 0))], out_specs=[blk, blk],
        out_shape=[SDS((S, ATT_W), F32), SDS((S, ATT_W), F32)],
        compiler_params=_cparams(("parallel", "arbitrary")),
    )(qkv, qkv, qkv, bias)


def _flash_bwd(qkv, o, do, lse, bias, after=()):
    t = ATT_T
    n_hp = ATT_W // LANES
    n_t = S // t

    def body(q_ref, k_ref, v_ref, o_ref, do_ref, lse_ref, b_ref, *rest):
        dq_ref, dk_ref, dv_ref = rest[-3:]
        j = pl.program_id(1)
        first = lax.broadcasted_iota(jnp.int32, (1, LANES), 1) < HEAD_DIM

        @pl.when(j == 0)
        def _():
            dq_ref[...] = jnp.zeros_like(dq_ref)

        kb = k_ref[...]
        vb = v_ref[...]
        ks = _head_split(kb, first)

        def q_step(i, carry):
            dk_acc, dv_acc = carry
            rows = pl.ds(pl.multiple_of(i * t, t), t)
            qs = _head_split(q_ref[rows, :], first)
            dob = do_ref[rows, :]
            prod = dob * o_ref[rows, :]
            d_all = jnp.sum(prod, axis=1, keepdims=True)
            d0 = jnp.sum(jnp.where(first, prod, 0.0), axis=1, keepdims=True)
            lse_b = lse_ref[rows, :]
            lse0 = jnp.max(jnp.where(first, lse_b, -jnp.inf), axis=1, keepdims=True)
            lse1 = jnp.max(jnp.where(first, -jnp.inf, lse_b), axis=1, keepdims=True)
            dos = _head_split(dob.astype(BF16), first)
            bias_t = b_ref[i - j]
            dq_t = jnp.zeros((t, LANES), F32)
            for h, (lse_h, d_h) in enumerate(((lse0, d0), (lse1, d_all - d0))):
                s = lax.dot_general(qs[h], kb, NT, preferred_element_type=F32)
                p = jnp.exp(s + (bias_t - lse_h))
                dp = lax.dot_general(dos[h], vb, NT, preferred_element_type=F32)
                ds = (p * (dp - d_h)).astype(BF16)
                dv_acc = dv_acc + lax.dot_general(p.astype(BF16), dos[h], TN, preferred_element_type=F32)
                dk_acc = dk_acc + lax.dot_general(ds, qs[h], TN, preferred_element_type=F32)
                dq_t = dq_t + lax.dot_general(ds, ks[h], NN, preferred_element_type=F32)
            dq_ref[rows, :] += dq_t
            return dk_acc, dv_acc

        zero = jnp.zeros((t, LANES), F32)
        dk_acc, dv_acc = lax.fori_loop(j, n_t, q_step, (zero, zero))
        dk_ref[...] = dk_acc
        dv_ref[...] = dv_acc

    blk = pl.BlockSpec((t, LANES), lambda hp, j: (j, hp))
    full = pl.BlockSpec((S, LANES), lambda hp, j: (0, hp))
    k_blk = pl.BlockSpec((t, LANES), lambda hp, j: (j, n_hp + hp))
    v_blk = pl.BlockSpec((t, LANES), lambda hp, j: (j, 2 * n_hp + hp))
    return pl.pallas_call(
        body, name="flash_bwd", grid=(n_hp, n_t),
        in_specs=([full, k_blk, v_blk, full, full, full, pl.BlockSpec((n_t, t, t), lambda hp, j: (0, 0, 0))]
                  + [pl.BlockSpec(memory_space=pl.ANY)] * len(after)),
        out_specs=[full, blk, blk],
        out_shape=[SDS((S, ATT_W), F32)] * 3,
        compiler_params=_cparams(("parallel", "arbitrary")),
    )(qkv, qkv, qkv, o, do, lse, bias, *after)


SCAN_T = 256
SCAN_GROUP = 8
SCAN_STEPS = (1, 2, 4)
ST_ROWS = 2 * N_CPLX // LANES
HALF = ST_ROWS // 2


def _scan_tables(lam_t):
    lam = lax.complex(lam_t[:HALF].reshape(N_CPLX), lam_t[HALF:].reshape(N_CPLX))
    pows = [lam]
    for _ in range(SCAN_GROUP - 1):
        pows.append(pows[-1] * lam)
    pows = jnp.stack(pows)
    sub = jnp.arange(SCAN_GROUP)[:, None]
    fwd = [jnp.where(sub >= k, pows[k - 1][None, :], 0.0) for k in SCAN_STEPS] + [pows]
    conj = jnp.conj(pows)
    bwd = [jnp.where(sub <= SCAN_GROUP - 1 - k, conj[k - 1][None, :], 0.0) for k in SCAN_STEPS] + [conj[::-1]]

    def pack(tabs):
        return jnp.stack([jnp.concatenate([jnp.real(t), jnp.imag(t)], axis=1) for t in tabs]).astype(F32)

    return pack(fwd), pack(bwd)


def _cmul_add(xr, xi, lr, li, sr, si):
    return xr + lr * sr - li * si, xi + lr * si + li * sr


def _group_scan(xr, xi, tab_ref, cr, ci, reverse):
    for j, k in enumerate(SCAN_STEPS):
        shift = SCAN_GROUP - k if reverse else k
        xr, xi = _cmul_add(xr, xi, tab_ref[j, :, :N_CPLX], tab_ref[j, :, N_CPLX:],
                           pltpu.roll(xr, shift, 0), pltpu.roll(xi, shift, 0))
    return _cmul_add(xr, xi, tab_ref[3, :, :N_CPLX], tab_ref[3, :, N_CPLX:],
                     jnp.broadcast_to(cr, (SCAN_GROUP, N_CPLX)), jnp.broadcast_to(ci, (SCAN_GROUP, N_CPLX)))


def _scan_fwd(tab, bu):
    nc = N_CPLX

    def body(tab_ref, bu_ref, st_ref, carry):
        @pl.when(pl.program_id(0) == 0)
        def _():
            carry[...] = jnp.zeros_like(carry)

        def group(a, c):
            rows = pl.ds(pl.multiple_of(a * SCAN_GROUP, SCAN_GROUP), SCAN_GROUP)
            xr, xi = _group_scan(bu_ref[rows, :nc], bu_ref[rows, nc:], tab_ref, c[0], c[1], False)
            st_ref[rows, :nc] = xr
            st_ref[rows, nc:] = xi
            return xr[SCAN_GROUP - 1:SCAN_GROUP, :], xi[SCAN_GROUP - 1:SCAN_GROUP, :]

        cr, ci = lax.fori_loop(0, SCAN_T // SCAN_GROUP, group, (carry[:, :nc], carry[:, nc:]), unroll=2)
        carry[:, :nc] = cr
        carry[:, nc:] = ci

    blk = pl.BlockSpec((SCAN_T, 2 * nc), lambda i: (i, 0))
    return pl.pallas_call(
        body, name="scan_fwd", grid=(S // SCAN_T,),
        in_specs=[pl.BlockSpec((4, SCAN_GROUP, 2 * nc), lambda i: (0, 0, 0)), blk], out_specs=blk,
        out_shape=SDS((S, 2 * nc), F32),
        scratch_shapes=[pltpu.VMEM((1, 2 * nc), F32)],
        compiler_params=_cparams(("arbitrary",)),
    )(tab, bu)


def _scan_bwd(tab, dst, states):
    n_blk = S // SCAN_T
    nc = N_CPLX

    def body(tab_ref, d_ref, x_ref, g_ref, dlam_ref, carry, acc):
        i = pl.program_id(0)

        @pl.when(i == 0)
        def _():
            carry[...] = jnp.zeros_like(carry)
            acc[...] = jnp.zeros_like(acc)

        last_row = lax.broadcasted_iota(jnp.int32, (SCAN_GROUP, 1), 0) == SCAN_GROUP - 1

        def group(j, c):
            cr, ci = c
            rows = pl.ds(pl.multiple_of((SCAN_T // SCAN_GROUP - 1 - j) * SCAN_GROUP, SCAN_GROUP), SCAN_GROUP)
            gr, gi = _group_scan(d_ref[rows, :nc], d_ref[rows, nc:], tab_ref, cr, ci, True)
            g_ref[rows, :nc] = gr
            g_ref[rows, nc:] = gi
            nr = jnp.where(last_row, jnp.broadcast_to(cr, (SCAN_GROUP, nc)), pltpu.roll(gr, SCAN_GROUP - 1, 0))
            ni = jnp.where(last_row, jnp.broadcast_to(ci, (SCAN_GROUP, nc)), pltpu.roll(gi, SCAN_GROUP - 1, 0))
            sr, si = x_ref[rows, :nc], x_ref[rows, nc:]
            acc[:, :nc] += nr * sr + ni * si
            acc[:, nc:] += ni * sr - nr * si
            return gr[0:1, :], gi[0:1, :]

        cr, ci = lax.fori_loop(0, SCAN_T // SCAN_GROUP, group, (carry[:, :nc], carry[:, nc:]), unroll=2)
        carry[:, :nc] = cr
        carry[:, nc:] = ci

        @pl.when(i == n_blk - 1)
        def _():
            dlam_ref[...] = jnp.sum(acc[...], axis=0, keepdims=True)

    blk = pl.BlockSpec((SCAN_T, 2 * nc), lambda i: (n_blk - 1 - i, 0))
    return pl.pallas_call(
        body, name="scan_bwd", grid=(n_blk,),
        in_specs=[pl.BlockSpec((4, SCAN_GROUP, 2 * nc), lambda i: (0, 0, 0)), blk, blk],
        out_specs=[blk, pl.BlockSpec((1, 2 * nc), lambda i: (0, 0))],
        out_shape=[SDS((S, 2 * nc), F32), SDS((1, 2 * nc), F32)],
        scratch_shapes=[pltpu.VMEM((1, 2 * nc), F32), pltpu.VMEM((SCAN_GROUP, 2 * nc), F32)],
        compiler_params=_cparams(("arbitrary",)),
    )(tab, dst, states)


def _ssm_prep(a_re, a_im, log_dt, b_re, b_im, c_re, c_im):
    lam = lax.complex(a_re, a_im)
    dt = jnp.exp(log_dt)[:, None]
    lam_bar = jnp.exp(lam * dt)
    b_bar = ((lam_bar - 1.0) / lam)[..., None] * lax.complex(b_re, b_im)
    lam_t = jnp.concatenate([jnp.real(lam_bar).reshape(HALF, LANES), jnp.imag(lam_bar).reshape(HALF, LANES)], axis=0)
    groups_per_super = SSM_GROUPS // SSM_SUPER
    on_diag = ((lax.broadcasted_iota(jnp.int32, (SSM_W, SB_COLS), 0) // SSM_GROUP) % groups_per_super
               == lax.broadcasted_iota(jnp.int32, (SSM_W, SB_COLS), 1) // SSM_STATE)

    def compact(m):
        return jnp.where(on_diag, jnp.tile(m.reshape(SSM_W, SSM_STATE), (1, groups_per_super)), 0.0)

    w_b = jnp.concatenate([compact(jnp.real(b_bar).transpose(0, 2, 1)),
                           compact(jnp.imag(b_bar).transpose(0, 2, 1))], axis=1)
    w_ct = jnp.concatenate([compact(c_re), -compact(c_im)], axis=1)
    return lam_t, w_b, w_ct


SSM_SUPER = 4
SB_ROWS = SSM_W // SSM_SUPER
SB_COLS = N_CPLX // SSM_SUPER


def _bdmm(a, w, mode, out_dtype, a_cols=None):
    a_off = 0 if a_cols is None else a_cols[0] // SB_ROWS
    if mode == "nn":
        def body(a_ref, w_ref, o_ref):
            o_ref[...] = lax.dot_general(a_ref[...].astype(BF16), w_ref[...].astype(BF16), NN,
                                         preferred_element_type=F32).astype(o_ref.dtype)

        return pl.pallas_call(
            body, name="bdmm_nn", grid=(2, SSM_SUPER),
            in_specs=[pl.BlockSpec((S, SB_ROWS), lambda h, b: (0, a_off + b)),
                      pl.BlockSpec((SB_ROWS, SB_COLS), lambda h, b: (b, h))],
            out_specs=pl.BlockSpec((S, SB_COLS), lambda h, b: (0, h * SSM_SUPER + b)),
            out_shape=SDS((S, 2 * N_CPLX), out_dtype),
            compiler_params=_cparams(("parallel", "parallel")),
        )(a, w)
    if mode == "nt":
        def body(re_ref, im_ref, wre_ref, wim_ref, o_ref):
            acc = lax.dot_general(re_ref[...].astype(BF16), wre_ref[...].astype(BF16), NT, preferred_element_type=F32)
            acc += lax.dot_general(im_ref[...].astype(BF16), wim_ref[...].astype(BF16), NT, preferred_element_type=F32)
            o_ref[...] = acc.astype(o_ref.dtype)

        return pl.pallas_call(
            body, name="bdmm_nt", grid=(SSM_SUPER,),
            in_specs=[pl.BlockSpec((S, SB_COLS), lambda b: (0, b)),
                      pl.BlockSpec((S, SB_COLS), lambda b: (0, SSM_SUPER + b)),
                      pl.BlockSpec((SB_ROWS, SB_COLS), lambda b: (b, 0)),
                      pl.BlockSpec((SB_ROWS, SB_COLS), lambda b: (b, 1))],
            out_specs=pl.BlockSpec((S, SB_ROWS), lambda b: (0, b)),
            out_shape=SDS((S, SSM_W), out_dtype),
            compiler_params=_cparams(("parallel",)),
        )(a, a, w, w)

    def body_tn(a_ref, w_ref, o_ref):
        o_ref[...] = lax.dot_general(a_ref[...].astype(BF16), w_ref[...].astype(BF16), TN,
                                     preferred_element_type=F32).astype(o_ref.dtype)

    return pl.pallas_call(
        body_tn, name="bdmm_tn", grid=(2, SSM_SUPER),
        in_specs=[pl.BlockSpec((S, SB_ROWS), lambda h, b: (0, a_off + b)),
                  pl.BlockSpec((S, SB_COLS), lambda h, b: (0, h * SSM_SUPER + b))],
        out_specs=pl.BlockSpec((SB_ROWS, SB_COLS), lambda h, b: (b, h)),
        out_shape=SDS((SSM_W, 2 * SB_COLS), out_dtype),
        compiler_params=_cparams(("parallel", "parallel")),
    )(a, w)


U_SSM_COLS = (4 * ATT_W, SSM_W)


def _row(v):
    return v.reshape(1, -1)


def _even_fwd(x, h, tail, pre, post, w_in, late_w, glu_b, ssm_d, prep, tables):
    lam_t, w_b, w_ct = prep
    proj = _mm(h, w_in, "nn", F32, b_blocks=True)
    qkv = _qkv_prep(proj, tables[:3])
    w_out, glu_w = late_w(qkv)
    att, lse = _flash_fwd(qkv, tables[3])
    bu = _bdmm(proj, w_b, "nn", F32, a_cols=U_SSM_COLS)
    scan_fwd_tab, scan_bwd_tab = _scan_tables(lam_t)
    states = _scan_fwd(scan_fwd_tab, bu)
    y = _bdmm(states, w_ct, "nt", F32)

    def act1(yv, uv, dv):
        return (_gelu_and_grad(yv + dv * uv)[0],), ()

    (z1,) = _rowwise(act1, [(y, SSM_W, 0), (proj, SSM_W, 8), (ssm_d, SSM_W, 0)], [(SSM_W, F32)], name="ssm_act_fwd")
    lin = _mm(z1, glu_w, "nn", F32)

    def gate(att_v, ga, gs, z1v, linv, bv):
        ssm_out = z1v * _sigmoid(linv + bv)
        return (jnp.concatenate([att_v * _silu_and_grad(ga)[0], ssm_out * _silu_and_grad(gs)[0]], axis=1),), ()

    (merged,) = _rowwise(gate, [(att, ATT_W, 0), (proj, ATT_W, 3), (proj, SSM_W, 9), (z1, SSM_W, 0),
                                (lin, SSM_W, 0), (glu_b, SSM_W, 0)], [(EVEN_OUT, BF16)], name="even_gate_fwd")
    yout = _mm(merged, w_out, "nn", F32)
    saved = (x, h, proj, qkv, att, lse, states, y, z1, lin, merged, yout, w_out, glu_w, scan_bwd_tab)
    return tail(x, yout, post) + (saved,)


def _even_bwd(g, saved, pre, post, w_in, late_w, glu_b, ssm_d, prep, tables, on_w, on_ssm):
    x, h, proj, qkv, att, lse, states, y, z1, lin, merged, yout, w_out, glu_w, scan_bwd_tab = saved
    lam_t, w_b, w_ct = prep
    dyout, dpost = _post_bwd(g, yout, post)
    dmerged = _mm(dyout, w_out, "nt", F32)
    dw_out = _mm(merged, dyout, "tn", BF16)

    def gate_bwd(dm_a, dm_s, att_v, ga, gs, z1v, linv, bv):
        sa, dsa = _silu_and_grad(ga)
        ss, dss = _silu_and_grad(gs)
        sig = _sigmoid(linv + bv)
        ssm_out = z1v * sig
        dssm = dm_s * ss
        dlin = dssm * z1v * sig * (1.0 - sig)
        return (dm_a * sa, dm_a * att_v * dsa, dm_s * ssm_out * dss, dssm * sig, dlin), (dlin,)

    datt, dg_att, dg_ssm, dz1a, dlin, dglu_b = _rowwise(
        gate_bwd, [(dmerged, ATT_W, 0), (dmerged, SSM_W, 2), (att, ATT_W, 0), (proj, ATT_W, 3), (proj, SSM_W, 9),
                   (z1, SSM_W, 0), (lin, SSM_W, 0), (glu_b, SSM_W, 0)],
        [(ATT_W, F32), (ATT_W, BF16), (SSM_W, BF16), (SSM_W, F32), (SSM_W, BF16)], [SSM_W], name="even_gate_bwd")
    dz1b = _mm(dlin, glu_w, "nt", F32)
    dglu_w = _mm(z1, dlin, "tn", BF16)

    def act1_bwd(da, db, yv, uv, dv):
        dpre = (da + db) * _gelu_and_grad(yv + dv * uv)[1]
        return (dpre, dpre * dv), (dpre * uv,)

    sent_late_w = on_w(dict(w_out=dw_out, glu_w=dglu_w))
    dy, du_direct, dd = _rowwise(act1_bwd, [(dz1a, SSM_W, 0), (dz1b, SSM_W, 0), (y, SSM_W, 0), (proj, SSM_W, 8),
                                            (ssm_d, SSM_W, 0)], [(SSM_W, BF16), (SSM_W, F32)], [SSM_W],
                                 name="ssm_act_bwd", after=(sent_late_w,))
    dst = _bdmm(dy, w_ct, "nn", F32)
    dw_ct = _bdmm(dy, states, "tn", F32)
    dbu, dlam_row = _scan_bwd(scan_bwd_tab, dst, states)
    dlam = jnp.concatenate([dlam_row[0, :N_CPLX].reshape(HALF, LANES), dlam_row[0, N_CPLX:].reshape(HALF, LANES)],
                           axis=0)
    du_state = _bdmm(dbu, w_b, "nt", F32)
    dw_b = _bdmm(proj, dbu, "tn", F32, a_cols=U_SSM_COLS)
    sent_ssm = on_ssm((dlam, dw_b, dw_ct))
    dq, dk, dv = _flash_bwd(qkv, att, datt, lse, tables[3], after=() if sent_ssm is None else (sent_ssm,))

    def assemble(dqv, dkv, dvv, dga, dua, dub, dgs, c, l, h):
        rot = _rotate(jnp.concatenate([dqv, dkv], axis=1), c, l, h, True)
        return (jnp.concatenate([(rot[:, :ATT_W] * HEAD_DIM ** -0.5).astype(BF16), rot[:, ATT_W:].astype(BF16),
                                 dvv.astype(BF16), dga, (dua + dub).astype(BF16), dgs], axis=1),), ()

    (dproj,) = _rowwise(assemble, [(dq, ATT_W, 0), (dk, ATT_W, 0), (dv, ATT_W, 0), (dg_att, ATT_W, 0),
                                   (du_state, SSM_W, 0), (du_direct, SSM_W, 0), (dg_ssm, SSM_W, 0),
                                   (tables[0], LANES, 0), (tables[1], LANES, 0), (tables[2], LANES, 0)],
                        [(EVEN_IN, BF16)], name="dproj_assemble")
    dw_in = _mm(h, dproj, "tn", BF16, out_blocks=True)
    sent = on_w(dict(w_in=dw_in))
    dh = _mm(dproj, w_in, "nt", F32, b_blocks=True, after=(sent,))
    g_prev, dpre = _pre_bwd(g, dh, x, pre)
    return g_prev, dict(pre=dpre, post=dpost, glu_b=dglu_b, ssm_d=dd)


def _odd_fwd(x, h, tail, pre, post, w_in, pool_w, pool_scale, w_out):
    proj = _mm(h, w_in, "nn", F32, b_blocks=True)
    mixed = _pool(proj, 0, False, BF16)
    ylin = _gmm(mixed, pool_w, "nn", F32)

    def gate(yl, gt, sc):
        return (yl * sc * _silu_and_grad(gt)[0],), ()

    (z,) = _rowwise(gate, [(ylin, POOL_W, 0), (proj, POOL_W, 1), (pool_scale, POOL_W, 0)], [(POOL_W, BF16)],
                    name="odd_gate_fwd")
    yout = _mm(z, w_out, "nn", F32)
    return tail(x, yout, post) + ((x, h, proj, mixed, ylin, z, yout),)


def _odd_bwd(g, saved, pre, post, w_in, pool_w, pool_scale, w_out, on_w):
    x, h, proj, mixed, ylin, z, yout = saved
    dyout, dpost = _post_bwd(g, yout, post)
    dz = _mm(dyout, w_out, "nt", F32)
    dw_out = _mm(z, dyout, "tn", BF16)

    def gate_bwd(dzv, yl, gt, sc):
        sg, dsg = _silu_and_grad(gt)
        tt = dzv * sg
        return (tt * sc, dzv * yl * sc * dsg), (tt * yl,)

    dylin, dproj_gate, dscale = _rowwise(gate_bwd, [(dz, POOL_W, 0), (ylin, POOL_W, 0), (proj, POOL_W, 1),
                                                    (pool_scale, POOL_W, 0)],
                                         [(POOL_W, BF16), (POOL_W, BF16, ODD_IN, 1)], [POOL_W], name="odd_gate_bwd")
    dmixed = _gmm(dylin, pool_w, "nt", F32)
    dpool_w = _gmm(mixed, dylin, "tn", BF16)
    dproj = _pool(dmixed, 0, True, BF16, into=dproj_gate)
    dw_in = _mm(h, dproj, "tn", BF16, out_blocks=True)
    sent = on_w(dict(w_in=dw_in, w_out=dw_out, pool_w=dpool_w))
    dh = _mm(dproj, w_in, "nt", F32, b_blocks=True, after=(sent,))
    g_prev, dpre = _pre_bwd(g, dh, x, pre)
    return g_prev, dict(pre=dpre, post=dpost, pool_scale=dscale)


def _my_index():
    return 4 * lax.axis_index("x") + 2 * lax.axis_index("y") + lax.axis_index("c")


HBM_SPEC = pl.BlockSpec(memory_space=pltpu.HBM)
SEM_SPEC = pl.BlockSpec(memory_space=pltpu.SEMAPHORE)
SPLIT_EFFECT = pltpu.SideEffectType.DATAFLOW_SIDE_EFFECTING


def _device_of(j):
    return (j // 4, (j // 2) % 2, j % 2)


def _split_copy(srcs, lands, send_sems, recv_sems, gather, i, j, dst_slot, recv_slot):
    return pltpu.make_async_remote_copy(
        src_ref=srcs[i] if gather else srcs[i].at[j], dst_ref=lands[i].at[dst_slot],
        send_sem=send_sems.at[i * N_DEV + j], recv_sem=recv_sems.at[i * N_DEV + recv_slot],
        device_id=_device_of(j), device_id_type=MESH_ID)


def _own_copy(srcs, lands, send_sems, gather, i, me):
    return pltpu.make_async_copy(srcs[i] if gather else srcs[i].at[me], lands[i].at[me], send_sems.at[i * N_DEV + me])


def _xchg_start(name, srcs, gather, after=()):
    n = len(srcs)
    n_in = n + len(after)

    def body(*refs):
        src_refs = refs[:n]
        send_sems, recv_sems, token = refs[n_in], refs[n_in + 1], refs[-1]
        land_refs = refs[n_in + 2 + n:n_in + 2 + 2 * n]
        me = _my_index()
        for j in range(N_DEV):
            @pl.when(me != j)
            def _(j=j):
                for i in range(n):
                    _split_copy(src_refs, land_refs, send_sems, recv_sems, gather, i, j, me, me).start()
        for i in range(n):
            _own_copy(src_refs, land_refs, send_sems, gather, i, me).start()
        token[...] = jnp.zeros_like(token)

    land_shapes = [((N_DEV,) + a.shape) if gather else a.shape for a in srcs]
    thru = ([pltpu.HBM(a.shape, a.dtype) for a in srcs] + [pltpu.HBM(s, a.dtype) for s, a in zip(land_shapes, srcs)])
    res = pl.pallas_call(
        body, name=name,
        out_shape=(pltpu.SemaphoreType.DMA((n * N_DEV,)), pltpu.SemaphoreType.DMA((n * N_DEV,)), *thru,
                   SDS((8, LANES), F32)),
        in_specs=[HBM_SPEC] * n + [pl.BlockSpec(memory_space=pl.ANY)] * len(after),
        out_specs=(SEM_SPEC, SEM_SPEC, *([HBM_SPEC] * (2 * n)), pl.BlockSpec(memory_space=pltpu.VMEM)),
        input_output_aliases={i: 2 + i for i in range(n)},
        compiler_params=pltpu.CompilerParams(has_side_effects=SPLIT_EFFECT),
    )(*[pltpu.with_memory_space_constraint(a, pltpu.HBM) for a in srcs], *after)
    return res[0], res[1], list(res[2:2 + n]), list(res[2 + n:2 + 2 * n]), res[-1]


def _xchg_wait(name, started, gather, after):
    send_sems, recv_sems, srcs, lands, _ = started
    n = len(srcs)

    def body(*refs):
        src_refs, land_refs = refs[:n], refs[n:2 * n]
        send_r, recv_r = refs[2 * n], refs[2 * n + 1]
        me = _my_index()
        for j in range(N_DEV):
            @pl.when(me != j)
            def _(j=j):
                for i in range(n):
                    _split_copy(src_refs, land_refs, send_r, recv_r, gather, i, j, me, me).wait_send()
                    _split_copy(src_refs, land_refs, send_r, recv_r, gather, i, j, j, j).wait_recv()
        for i in range(n):
            _own_copy(src_refs, land_refs, send_r, gather, i, me).wait()

    thru = [pltpu.HBM(a.shape, a.dtype) for a in list(srcs) + list(lands)]
    res = pl.pallas_call(
        body, name=name, out_shape=tuple(thru),
        in_specs=[HBM_SPEC] * (2 * n) + [SEM_SPEC, SEM_SPEC] + [pl.BlockSpec(memory_space=pl.ANY)] * len(after),
        out_specs=tuple([HBM_SPEC] * (2 * n)),
        input_output_aliases={i: i for i in range(2 * n)},
        compiler_params=pltpu.CompilerParams(has_side_effects=SPLIT_EFFECT),
    )(*srcs, *lands, send_sems, recv_sems, *after)
    return list(res[n:])


def _adam_layers(w, slot_list, m, v, name):
    n_l, r, c = w.shape
    ns = slot_list[0].shape[0]
    tr = r
    while tr * c * 4 > (1 << 20) and tr % 16 == 0:
        tr //= 2
    assert r % tr == 0 and len(slot_list) == n_l

    def body(*refs):
        w_ref, slot_refs = refs[0], refs[1:1 + n_l]
        m_ref, v_ref, go_ref, d_ref, mo_ref, vo_ref = refs[1 + n_l:]
        layer = pl.program_id(0)
        g = None
        for l, g_ref in enumerate(slot_refs):
            gl = g_ref[0].astype(F32)
            for s in range(1, ns):
                gl = gl + g_ref[s].astype(F32)
            g = gl if g is None else jnp.where(layer == l, gl, g)
        mn = ADAM_B1 * m_ref[...] + (1.0 - ADAM_B1) * g
        vn = ADAM_B2 * v_ref[...] + (1.0 - ADAM_B2) * (g * g)
        m_hat = mn / (1.0 - ADAM_B1 ** ADAM_STEP)
        v_hat = vn / (1.0 - ADAM_B2 ** ADAM_STEP)
        go_ref[...] = g
        d_ref[...] = -ADAM_LR * (m_hat / (jnp.sqrt(v_hat) + ADAM_EPS) + ADAM_WD * w_ref[...])
        mo_ref[...] = mn
        vo_ref[...] = vn

    blk = pl.BlockSpec((None, tr, c), lambda l, i: (l, i, 0))
    slot_specs = [pl.BlockSpec((ns, tr, c), lambda l, i, k=k: (0, jnp.where(l == k, i, 0), 0)) for k in range(n_l)]
    return pl.pallas_call(
        body, name=name, grid=(n_l, r // tr),
        in_specs=[blk] + slot_specs + [blk, blk],
        out_specs=[blk] * 4, out_shape=[SDS((n_l, r, c), F32)] * 4,
        compiler_params=_cparams(("arbitrary", "arbitrary")),
    )(w, *slot_list, m, v)


def _adam(w, gslots, m, v, name):
    r, c = w.shape
    ns = gslots.shape[0]
    tr = r
    while tr * c * 4 > (1 << 20) and tr % 16 == 0:
        tr //= 2
    assert r % tr == 0

    def body(w_ref, g_ref, m_ref, v_ref, go_ref, d_ref, mo_ref, vo_ref):
        g = g_ref[0].astype(F32)
        for s in range(1, ns):
            g = g + g_ref[s].astype(F32)
        wv = w_ref[...]
        mn = ADAM_B1 * m_ref[...] + (1.0 - ADAM_B1) * g
        vn = ADAM_B2 * v_ref[...] + (1.0 - ADAM_B2) * (g * g)
        m_hat = mn / (1.0 - ADAM_B1 ** ADAM_STEP)
        v_hat = vn / (1.0 - ADAM_B2 ** ADAM_STEP)
        go_ref[...] = g
        d_ref[...] = -ADAM_LR * (m_hat / (jnp.sqrt(v_hat) + ADAM_EPS) + ADAM_WD * wv)
        mo_ref[...] = mn
        vo_ref[...] = vn

    blk = pl.BlockSpec((tr, c), lambda i: (i, 0))
    return pl.pallas_call(
        body, name=name, grid=(r // tr,),
        in_specs=[blk, pl.BlockSpec((ns, tr, c), lambda i: (0, i, 0)), blk, blk],
        out_specs=[blk] * 4, out_shape=[SDS((r, c), F32)] * 4,
        compiler_params=_cparams(("parallel",)),
    )(w, gslots, m, v)


def _sum_slots(slots, name):
    ns, r, c = slots.shape

    def body(g_ref, o_ref):
        g = g_ref[0]
        for s in range(1, ns):
            g = g + g_ref[s]
        o_ref[...] = g

    return pl.pallas_call(
        body, name=name, grid=(1,),
        in_specs=[pl.BlockSpec((ns, r, c), lambda i: (0, 0, 0))], out_specs=pl.BlockSpec((r, c), lambda i: (0, 0)),
        out_shape=SDS((r, c), F32), compiler_params=_cparams(("arbitrary",)),
    )(slots)


def _adam_params(params, name):
    n = len(params)

    def body(*refs):
        ins, outs = refs[:5 * n], refs[5 * n:]
        for p in range(n):
            w_ref, m_ref, v_ref, g_first, g_rest = ins[5 * p:5 * p + 5]
            go_ref, d_ref, mo_ref, vo_ref = outs[4 * p:4 * p + 4]
            for part, g_ref in ((slice(0, 1), g_first), (slice(1, w_ref.shape[0]), g_rest)):
                g = g_ref[...]
                mn = ADAM_B1 * m_ref[part] + (1.0 - ADAM_B1) * g
                vn = ADAM_B2 * v_ref[part] + (1.0 - ADAM_B2) * (g * g)
                m_hat = mn / (1.0 - ADAM_B1 ** ADAM_STEP)
                v_hat = vn / (1.0 - ADAM_B2 ** ADAM_STEP)
                go_ref[part] = g
                d_ref[part] = -ADAM_LR * (m_hat / (jnp.sqrt(v_hat) + ADAM_EPS) + ADAM_WD * w_ref[part])
                mo_ref[part] = mn
                vo_ref[part] = vn

    def whole(a):
        return pl.BlockSpec(a.shape, lambda i, nd=a.ndim: (0,) * nd)

    flat = [pltpu.with_memory_space_constraint(a, pltpu.HBM) for prm in params for a in prm]
    outs = pl.pallas_call(
        body, name=name, grid=(1,),
        in_specs=[whole(a) for a in flat],
        out_specs=[whole(prm[0]) for prm in params for _ in range(4)],
        out_shape=[SDS(prm[0].shape, F32) for prm in params for _ in range(4)],
        compiler_params=_cparams(("arbitrary",)),
    )(*flat)
    return [outs[4 * p:4 * p + 4] for p in range(n)]


SMALL_NAMES = ("pre_norm", "post_norm", "ssm_a_re", "ssm_a_im", "ssm_log_dt", "ssm_b_re", "ssm_b_im", "ssm_c_re",
               "ssm_c_im", "ssm_d", "ssm_glu_b")
SSM_NAMES = ("ssm_a_re", "ssm_a_im", "ssm_log_dt", "ssm_b_re", "ssm_b_im", "ssm_c_re", "ssm_c_im")
SHARDED_NAMES = ("even_w_in", "even_w_out", "ssm_glu_w", "odd_w_in", "pool_w", "odd_w_out")
WEIGHT_ORDER = ("pre_norm", "post_norm", "even_w_in", "even_w_out", "ssm_a_re", "ssm_a_im", "ssm_log_dt", "ssm_b_re",
                "ssm_b_im", "ssm_c_re", "ssm_c_im", "ssm_d", "ssm_glu_w", "ssm_glu_b", "odd_w_in", "pool_w",
                "pool_scale", "odd_w_out")
PACK_ROWS_ALIGN = 8


def _pack(parts):
    flat = jnp.concatenate([p.reshape(-1).astype(F32) for p in parts])
    rows = -(-flat.shape[0] // (LANES * PACK_ROWS_ALIGN)) * PACK_ROWS_ALIGN
    return jnp.pad(flat, (0, rows * LANES - flat.shape[0])).reshape(rows, LANES)


def _unpack(packed, shapes):
    flat = packed.reshape(-1)
    out, off = [], 0
    for shp in shapes:
        size = math.prod(shp)
        out.append(flat[off:off + size].reshape(shp))
        off += size
    return out


EVEN_SHARDED = ("w_in", "w_out", "glu_w")
ODD_SHARDED = ("w_in", "pool_w", "w_out")
FAMILY = {(0, "w_in"): "even_w_in", (0, "w_out"): "even_w_out", (0, "glu_w"): "ssm_glu_w",
          (1, "w_in"): "odd_w_in", (1, "pool_w"): "pool_w", (1, "w_out"): "odd_w_out"}


def _sharded_keys(layer):
    return EVEN_SHARDED if layer % 2 == 0 else ODD_SHARDED


def _local_step(x, tgt, small, get_weights, on_w, on_ssm, on_grads, zero=0.0):
    tables = _rope_tables(zero) + (_attention_bias(zero),)
    preps, prep_vjps = [], []
    for i in range(2):
        out, vjp = jax.vjp(_ssm_prep, small["ssm_a_re"][i] + zero, small["ssm_a_im"][i], small["ssm_log_dt"][i],
                           small["ssm_b_re"][i], small["ssm_b_im"][i], small["ssm_c_re"][i], small["ssm_c_im"][i])
        preps.append(out)
        prep_vjps.append(vjp)

    def layer_args(layer, wts):
        i = layer // 2
        pre, post = _row(small["pre_norm"][layer]) + wts.get("token", 0.0), _row(small["post_norm"][layer])
        if layer % 2 == 0:
            return (pre, post, wts["w_in"], wts["late"], _row(small["ssm_glu_b"][i]), _row(small["ssm_d"][i]),
                    preps[i], tables)
        return (pre, post, wts["w_in"], wts["pool_w"], _row(wts["pool_scale"]), wts["w_out"])

    saved, args = [], []
    cur = x
    for layer in range(4):
        after = (cur,) if layer else (cur, tables[0], tables[3], preps[0][1], preps[0][2], preps[1][1], preps[1][2])
        args.append(layer_args(layer, get_weights(layer, after)))
        if layer == 0:
            h = _norm_fwd(cur, args[0][0])
        if layer < 3:
            def tail(xv, yv, post, next_gain=_row(small["pre_norm"][layer + 1])):
                return tuple(_post_fwd(xv, yv, post, next_gain))
        else:
            def tail(xv, yv, post):
                return tuple(_post_fwd_loss(xv, yv, post, tgt))
        cur, h, sv = (_even_fwd if layer % 2 == 0 else _odd_fwd)(cur, h, tail, *args[layer])
        saved.append(sv)
    g, sq = cur, h
    loss = 0.5 * jnp.sum(sq) / D

    lg = [None] * 4
    token = jnp.zeros((), F32)
    for layer in reversed(range(4)):
        largs = list(args[layer])
        largs[1] = largs[1] + token
        hooks = dict(on_w=functools.partial(on_w, layer))
        ssm_grads = []
        if layer % 2 == 0:
            def ssm_hook(cotangents, layer=layer):
                ssm_grads.append(prep_vjps[layer // 2](cotangents))
                return on_ssm(layer, ssm_grads[0])

            hooks["on_ssm"] = ssm_hook
        g, lg[layer] = (_even_bwd if layer % 2 == 0 else _odd_bwd)(g, saved[layer], *largs, **hooks)
        if ssm_grads:
            lg[layer]["ssm"] = ssm_grads[0]
        token = on_grads(layer, lg[layer])
    return loss, g, token


def _to_slots(key, gfull):
    if key == "w_in":
        return gfull
    if key in ("w_out", "glu_w"):
        rr, nn = gfull.shape
        return gfull.reshape(N_DEV, rr // N_DEV, nn)
    assert key == "pool_w"
    gg, rr, nn = gfull.shape
    return gfull.reshape(gg, N_DEV, rr // N_DEV, nn).transpose(1, 0, 2, 3)


def _from_gathered(key, gat):
    if key == "w_in":
        return gat
    if key in ("w_out", "glu_w"):
        _, rr, nn = gat.shape
        return gat.reshape(N_DEV * rr, nn)
    assert key == "pool_w"
    _, gg, rr, nn = gat.shape
    return gat.transpose(1, 0, 2, 3).reshape(gg, N_DEV * rr, nn)


def kernel(x, pre_norm, post_norm, even_w_in, even_w_out, ssm_a_re, ssm_a_im, ssm_log_dt, ssm_b_re, ssm_b_im, ssm_c_re, ssm_c_im, ssm_d, ssm_glu_w, ssm_glu_b, odd_w_in, pool_w, pool_scale, odd_w_out, loss_target, m_pre_norm, m_post_norm, m_even_w_in, m_even_w_out, m_ssm_a_re, m_ssm_a_im, m_ssm_log_dt, m_ssm_b_re, m_ssm_b_im, m_ssm_c_re, m_ssm_c_im, m_ssm_d, m_ssm_glu_w, m_ssm_glu_b, m_odd_w_in, m_pool_w, m_pool_scale, m_odd_w_out, v_pre_norm, v_post_norm, v_even_w_in, v_even_w_out, v_ssm_a_re, v_ssm_a_im, v_ssm_log_dt, v_ssm_b_re, v_ssm_b_im, v_ssm_c_re, v_ssm_c_im, v_ssm_d, v_ssm_glu_w, v_ssm_glu_b, v_odd_w_in, v_pool_w, v_pool_scale, v_odd_w_out):
    w = dict(pre_norm=pre_norm, post_norm=post_norm, even_w_in=even_w_in, even_w_out=even_w_out, ssm_a_re=ssm_a_re,
             ssm_a_im=ssm_a_im, ssm_log_dt=ssm_log_dt, ssm_b_re=ssm_b_re, ssm_b_im=ssm_b_im, ssm_c_re=ssm_c_re,
             ssm_c_im=ssm_c_im, ssm_d=ssm_d, ssm_glu_w=ssm_glu_w, ssm_glu_b=ssm_glu_b, odd_w_in=odd_w_in,
             pool_w=pool_w, pool_scale=pool_scale, odd_w_out=odd_w_out)
    mom = dict(pre_norm=m_pre_norm, post_norm=m_post_norm, even_w_in=m_even_w_in, even_w_out=m_even_w_out,
               ssm_a_re=m_ssm_a_re, ssm_a_im=m_ssm_a_im, ssm_log_dt=m_ssm_log_dt, ssm_b_re=m_ssm_b_re,
               ssm_b_im=m_ssm_b_im, ssm_c_re=m_ssm_c_re, ssm_c_im=m_ssm_c_im, ssm_d=m_ssm_d, ssm_glu_w=m_ssm_glu_w,
               ssm_glu_b=m_ssm_glu_b, odd_w_in=m_odd_w_in, pool_w=m_pool_w, pool_scale=m_pool_scale,
               odd_w_out=m_odd_w_out)
    var = dict(pre_norm=v_pre_norm, post_norm=v_post_norm, even_w_in=v_even_w_in, even_w_out=v_even_w_out,
               ssm_a_re=v_ssm_a_re, ssm_a_im=v_ssm_a_im, ssm_log_dt=v_ssm_log_dt, ssm_b_re=v_ssm_b_re,
               ssm_b_im=v_ssm_b_im, ssm_c_re=v_ssm_c_re, ssm_c_im=v_ssm_c_im, ssm_d=v_ssm_d, ssm_glu_w=v_ssm_glu_w,
               ssm_glu_b=v_ssm_glu_b, odd_w_in=v_odd_w_in, pool_w=v_pool_w, pool_scale=v_pool_scale,
               odd_w_out=v_odd_w_out)
    me = _my_index()
    scale_cols = pool_scale.shape[1]

    def start_gather(tag, layer, keys, after=()):
        i = layer // 2
        shards = [w[FAMILY[(layer % 2, k)]][i].astype(BF16) for k in keys]
        if layer % 2 == 1:
            shards.append(jnp.pad(pool_scale[i][None], ((0, PACK_ROWS_ALIGN - 1), (0, 0))))
        return _xchg_start(f"gather_start_{tag}", shards, True, after)

    gather_started = {0: start_gather("0", 0, EVEN_SHARDED[:1])}
    small = {nm: w[nm] for nm in SMALL_NAMES}

    def get_weights(layer, after):
        keys = EVEN_SHARDED[:1] if layer == 0 else _sharded_keys(layer)
        lands = _xchg_wait(f"gather_wait_{layer}", gather_started[layer], True, after)
        wts = {k: _from_gathered(k, gat) for k, gat in zip(keys, lands)}
        if layer % 2 == 1:
            wts["pool_scale"] = lands[-1][:, 0, :].reshape(N_DEV * scale_cols)
        if layer == 0:
            prev = gather_started["0_late"] = start_gather("0_late", 0, EVEN_SHARDED[1:], after=(lands[0],))
            for later in (1, 2, 3):
                prev = gather_started[later] = start_gather(str(later), later, _sharded_keys(later), after=(prev[4],))
            wts["token"] = sum(gather_started[tag][4][0, 0] for tag in ("0_late", 1, 2, 3))

            def late(after_late):
                late_lands = _xchg_wait("gather_wait_0_late", gather_started["0_late"], True, (after_late,))
                return tuple(_from_gathered(k, gat) for k, gat in zip(EVEN_SHARDED[1:], late_lands))

            wts["late"] = late
        elif layer == 2:
            wts["late"] = lambda after_late: (wts["w_out"], wts["glu_w"])
        return wts

    scatter_started = []

    def on_w(layer, gw):
        keys = tuple(k for k in _sharded_keys(layer) if k in gw)
        started = _xchg_start(f"scatter_start_{layer}_{keys[0]}", [_to_slots(k, gw[k]) for k in keys], False)
        scatter_started.append((layer, keys, started))
        return started[4]

    def wait_scatters(layers, after):
        for layer, keys, started in scatter_started:
            if layer in layers:
                lands = _xchg_wait(f"scatter_wait_{layer}_{keys[0]}", started, False, after)
                for k, land in zip(keys, lands):
                    recv[(layer, k)] = land

    packed_names = ("pre_norm", "post_norm") + SSM_NAMES + ("ssm_d", "ssm_glu_b")
    tails = {nm: (SSM_GROUPS, SSM_STATE * SSM_GROUP) if nm in ("ssm_b_re", "ssm_b_im") else w[nm].shape[1:]
             for nm in packed_names}

    layer_grads = {}
    early_started, mid_started = [], []

    def on_ssm(layer, ssm_grads):
        if layer != 0:
            return None
        mid_started.append(_xchg_start("mid_start", [_pack(list(ssm_grads))], True))
        return mid_started[0][4]

    def on_grads(layer, lg):
        layer_grads[layer] = lg
        zero = jnp.zeros((), F32)
        if layer == 1:
            lgs = layer_grads
            early = ([jnp.concatenate([lgs[l][k] for l in (1, 2, 3)], axis=0) for k in ("pre", "post")]
                     + list(lgs[2]["ssm"]) + [lgs[2]["ssm_d"], lgs[2]["glu_b"],
                                              jnp.concatenate([lgs[1]["pool_scale"], lgs[3]["pool_scale"]], axis=0)])
            early_started.append(_xchg_start("small_start", [_pack(early)], True))
            zero = zero + early_started[0][4][0, 0]
        return zero

    loss_local, grad_x, token = _local_step(x[0], loss_target[0], small, get_weights, on_w, on_ssm, on_grads,
                                            zero=gather_started[0][4][0, 0])

    lg0 = layer_grads[0]
    late_started = _xchg_start("late_start", [_pack([lg0["pre"], lg0["post"], lg0["ssm_d"], lg0["glu_b"],
                                                     loss_local.reshape(1)]) + token], True)

    def adam_family(parity, k):
        nm = FAMILY[(parity, k)]
        shp = w[nm].shape
        cols = shp[-1]
        slot_list = [recv[(parity + 2 * i, k)].reshape(N_DEV, -1, cols) for i in range(2)]
        outs = _adam_layers(w[nm].reshape(2, -1, cols), slot_list, mom[nm].reshape(2, -1, cols),
                            var[nm].reshape(2, -1, cols), name=f"adam_{nm}")
        return [o.reshape(shp) for o in outs]

    recv, res = {}, {}
    wait_scatters((3, 1), (late_started[4],))
    for k in ODD_SHARDED:
        res[FAMILY[(1, k)]] = adam_family(1, k)
    odd_done = tuple(res[FAMILY[(1, k)]][0] for k in ODD_SHARDED)

    (early_slots,) = _xchg_wait("small_wait", early_started[0], True, odd_done)
    (mid_slots,) = _xchg_wait("mid_wait", mid_started[0], True, odd_done)
    early_shapes = [(w[nm].shape[0] - 1,) + tails[nm] for nm in packed_names] + [(2, N_DEV * scale_cols)]
    g_early = _unpack(_sum_slots(early_slots, "sum_small_early"), early_shapes)
    g_mid = _unpack(_sum_slots(mid_slots, "sum_small_mid"), [(1,) + tails[nm] for nm in SSM_NAMES])

    (late_slots,) = _xchg_wait("late_wait", late_started, True, (g_early[0], g_mid[0]))
    wait_scatters((2, 0), (late_slots,))
    for k in EVEN_SHARDED:
        res[FAMILY[(0, k)]] = adam_family(0, k)

    late_names = ("pre_norm", "post_norm", "ssm_d", "ssm_glu_b")
    g_late = _unpack(_sum_slots(late_slots, "sum_small_late"), [(1,) + tails[nm] for nm in late_names] + [(1,)])
    g_first = dict(zip(late_names, g_late))
    g_first.update(zip(SSM_NAMES, g_mid))
    dense = lambda nm, a: a.reshape((a.shape[0],) + tails[nm])
    outs = _adam_params([(dense(nm, w[nm]), dense(nm, mom[nm]), dense(nm, var[nm]), g_first[nm], g_early[j])
                         for j, nm in enumerate(packed_names)], "adam_small")
    for nm, four in zip(packed_names, outs):
        res[nm] = [o.reshape(w[nm].shape) for o in four]
    loss = g_late[-1].reshape(())
    g_scale = lax.dynamic_slice_in_dim(g_early[-1], me * scale_cols, scale_cols, axis=1)
    pad = ((0, PACK_ROWS_ALIGN - 2), (0, 0))
    outs = _adam(jnp.pad(pool_scale, pad), jnp.pad(g_scale, pad)[None], jnp.pad(m_pool_scale, pad),
                 jnp.pad(v_pool_scale, pad), name="adam_pool_scale")
    res["pool_scale"] = [o[:2] for o in outs]

    out = [loss, grad_x[None]]
    for kind in range(4):
        out += [res[nm][kind] for nm in WEIGHT_ORDER]
    return tuple(out)
```

```python
import functools
import math

import jax
import jax.numpy as jnp
from jax import lax
from jax.experimental import pallas as pl
from jax.experimental.pallas import tpu as pltpu

F32 = jnp.float32
BF16 = jnp.bfloat16
SDS = jax.ShapeDtypeStruct

N_DEV = 8
S = 2048
D = 1024
HEAD_DIM = 64
ROT_DIM = 16
ROPE_THETA = 500000.0
ATT_W = 1024
SSM_W = 512
SSM_GROUPS = 32
SSM_GROUP = 16
SSM_STATE = 64
N_CPLX = SSM_GROUPS * SSM_STATE
POOL_W = 2048
POOL_GROUP = 512
EVEN_IN = 5120
EVEN_OUT = 1536
ODD_IN = 4096
RMS_EPS = 1e-6
LANES = 128
VMEM_LIMIT = 48 * 1024 * 1024

ADAM_LR = 0.001
ADAM_B1 = 0.9
ADAM_B2 = 0.999
ADAM_EPS = 1e-08
ADAM_WD = 0.01
ADAM_STEP = 10

MESH_ID = pl.DeviceIdType.MESH
NN = (((1,), (0,)), ((), ()))
NT = (((1,), (1,)), ((), ()))
TN = (((0,), (0,)), ((), ()))
_DN = {"nn": NN, "nt": NT, "tn": TN}


def _cparams(sem):
    return pltpu.CompilerParams(dimension_semantics=sem, vmem_limit_bytes=VMEM_LIMIT)


MM_TILES = (1024, 768, 512)


def _tile(dim):
    return next((t for t in MM_TILES if dim % t == 0), dim)


NT_BLOCKS_PER_STEP = 4


def _mm(a, b, mode, out_dtype, b_blocks=False, out_blocks=False, a_cols=None, after=()):
    if b_blocks:
        nblk, rows, cb = b.shape
        b2_shape = (rows, nblk * cb)
    else:
        b2_shape = b.shape
    a_shape = a.shape if a_cols is None else (a.shape[0], a_cols[1])
    if mode == "nn":
        (m, k), n = a_shape, b2_shape[1]
    elif mode == "nt":
        (m, k), n = a_shape, b2_shape[0]
    else:
        (k, m), n = a_shape, b2_shape[1]
    tm, tn, tk = _tile(m), _tile(n), _tile(k)
    per_step = 1
    if b_blocks and mode == "nn":
        tn = cb
    if b_blocks and mode == "nt":
        per_step = NT_BLOCKS_PER_STEP
        tk = per_step * cb
        tn = min(tn, MM_TILES[-1])
    if out_blocks:
        tn = n // N_DEV
        tk = k
    nk = k // tk
    a_unit = tm if mode == "tn" else tk
    assert a_cols is None or a_cols[0] % a_unit == 0
    a_off = 0 if a_cols is None else a_cols[0] // a_unit

    def body(a_ref, b_ref, *rest):
        o_ref, acc_ref = rest[-2:]
        kk = pl.program_id(2)
        if per_step == 1:
            part = lax.dot_general(a_ref[...].astype(BF16), b_ref[...].astype(BF16), _DN[mode],
                                   preferred_element_type=F32)
        else:
            part = None
            for blk in range(per_step):
                d = lax.dot_general(a_ref[:, blk * cb:(blk + 1) * cb].astype(BF16), b_ref[blk].astype(BF16), NT,
                                    preferred_element_type=F32)
                part = d if part is None else part + d
        if nk == 1:
            o_ref[...] = part.astype(o_ref.dtype)
            return

        @pl.when(kk == 0)
        def _():
            acc_ref[...] = part

        @pl.when((kk > 0) & (kk < nk - 1))
        def _():
            acc_ref[...] += part

        @pl.when(kk == nk - 1)
        def _():
            o_ref[...] = (acc_ref[...] + part).astype(o_ref.dtype)

    if mode == "nn":
        a_spec = pl.BlockSpec((tm, tk), lambda i, j, kk: (i, a_off + kk))
        b_spec = pl.BlockSpec((tk, tn), lambda i, j, kk: (kk, j))
    elif mode == "nt":
        a_spec = pl.BlockSpec((tm, tk), lambda i, j, kk: (i, a_off + kk))
        b_spec = pl.BlockSpec((tn, tk), lambda i, j, kk: (j, kk))
    else:
        a_spec = pl.BlockSpec((tk, tm), lambda i, j, kk: (kk, a_off + i))
        b_spec = pl.BlockSpec((tk, tn), lambda i, j, kk: (kk, j))
    if b_blocks and mode == "nn":
        b_spec = pl.BlockSpec((None, tk, cb), lambda i, j, kk: (j, kk, 0))
    if b_blocks and mode == "nt":
        b_spec = pl.BlockSpec((per_step, tn, cb), lambda i, j, kk: (kk, j, 0))
    out_spec = pl.BlockSpec((tm, tn), lambda i, j, kk: (i, j))
    out_shape = SDS((m, n), out_dtype)
    if out_blocks:
        out_spec = pl.BlockSpec((None, tm, tn), lambda i, j, kk: (j, i, 0))
        out_shape = SDS((N_DEV, m, tn), out_dtype)
    return pl.pallas_call(
        body, name=f"mm_{mode}_{m}x{k}x{n}",
        grid=(m // tm, n // tn, nk),
        in_specs=[a_spec, b_spec] + [pl.BlockSpec(memory_space=pl.ANY)] * len(after),
        out_specs=out_spec,
        out_shape=out_shape,
        scratch_shapes=[pltpu.VMEM((tm, tn) if nk > 1 else (8, LANES), F32)],
        compiler_params=_cparams(("parallel", "parallel", "arbitrary")),
    )(a, b, *after)


def _gmm(a, b, mode, out_dtype, tm=S):
    ng, gw = POOL_W // POOL_GROUP, POOL_GROUP
    ns = S // tm
    if mode in ("nn", "nt"):
        def body(a_ref, b_ref, o_ref):
            o_ref[...] = lax.dot_general(a_ref[...].astype(BF16), b_ref[...].astype(BF16), _DN[mode],
                                         preferred_element_type=F32).astype(o_ref.dtype)

        return pl.pallas_call(
            body, name=f"gmm_{mode}", grid=(ng, ns),
            in_specs=[pl.BlockSpec((tm, gw), lambda g, i: (i, g)),
                      pl.BlockSpec((None, gw, gw), lambda g, i: (g, 0, 0))],
            out_specs=pl.BlockSpec((tm, gw), lambda g, i: (i, g)),
            out_shape=SDS((S, POOL_W), out_dtype),
            compiler_params=_cparams(("parallel", "parallel")),
        )(a, b)

    def body_tn(a_ref, b_ref, o_ref, acc_ref):
        i = pl.program_id(1)

        @pl.when(i == 0)
        def _():
            acc_ref[...] = jnp.zeros_like(acc_ref)

        acc_ref[...] += lax.dot_general(a_ref[...].astype(BF16), b_ref[...].astype(BF16), TN,
                                        preferred_element_type=F32)

        @pl.when(i == ns - 1)
        def _():
            o_ref[...] = acc_ref[...].astype(o_ref.dtype)

    return pl.pallas_call(
        body_tn, name="gmm_tn", grid=(ng, ns),
        in_specs=[pl.BlockSpec((tm, gw), lambda g, i: (i, g)),
                  pl.BlockSpec((tm, gw), lambda g, i: (i, g))],
        out_specs=pl.BlockSpec((None, gw, gw), lambda g, i: (g, 0, 0)),
        out_shape=SDS((ng, gw, gw), out_dtype),
        scratch_shapes=[pltpu.VMEM((gw, gw), F32)],
        compiler_params=_cparams(("parallel", "arbitrary")),
    )(a, b)


def _rowwise(fn, inputs, out_defs, acc_defs=(), tm=256, name=None, after=()):
    n_in, n_out, n_acc = len(inputs), len(out_defs), len(acc_defs)
    n_after = len(after)
    in_specs, args = [], []
    for arr, width, cb in inputs:
        if arr.shape[0] == 1:
            in_specs.append(pl.BlockSpec((1, width), lambda i, cb=cb: (0, cb)))
        else:
            in_specs.append(pl.BlockSpec((tm, width), lambda i, cb=cb: (i, cb)))
        args.append(arr)
    out_defs = [d if len(d) == 4 else (d[0], d[1], d[0], 0) for d in out_defs]
    out_shape = [SDS((S, ww), dt) for _, dt, ww, _ in out_defs] + [SDS((1, w), F32) for w in acc_defs]
    out_specs = ([pl.BlockSpec((tm, w), lambda i, cb=cb: (i, cb)) for w, _, _, cb in out_defs]
                 + [pl.BlockSpec((1, w), lambda i: (0, 0)) for w in acc_defs])

    def kern(*refs):
        vals = [r[...] for r in refs[:n_in]]
        outs, accs = fn(*vals)
        out_refs = refs[n_in + n_after:]
        for r, v in zip(out_refs[:n_out], outs):
            r[...] = v.astype(r.dtype)
        if n_acc:
            acc_refs = out_refs[n_out:]

            @pl.when(pl.program_id(0) == 0)
            def _():
                for r in acc_refs:
                    r[...] = jnp.zeros_like(r)

            for r, v in zip(acc_refs, accs):
                r[...] += jnp.sum(v, axis=0, keepdims=True)

    res = pl.pallas_call(
        kern, name=name, grid=(S // tm,), in_specs=in_specs + [pl.BlockSpec(memory_space=pl.ANY)] * n_after,
        out_specs=out_specs, out_shape=out_shape, compiler_params=_cparams(("arbitrary",)),
    )(*args, *after)
    return res


def _sigmoid(x):
    return 1.0 / (1.0 + jnp.exp(-x))


def _silu_and_grad(x):
    s = _sigmoid(x)
    return x * s, s * (1.0 + x * (1.0 - s))


_GELU_K = math.sqrt(2.0 / math.pi)
_GELU_C = 0.044715


def _gelu_and_grad(x):
    t = jnp.tanh(_GELU_K * (x + _GELU_C * (x * x * x)))
    cdf = 0.5 * (1.0 + t)
    grad = cdf + 0.5 * x * (1.0 - t * t) * (_GELU_K * (1.0 + 3.0 * _GELU_C * x * x))
    return x * cdf, grad


def _rms(xv, gain):
    r = lax.rsqrt(jnp.mean(xv * xv, axis=-1, keepdims=True) + RMS_EPS)
    return xv * r * gain


def _rms_bwd(dout, xv, gain):
    r = lax.rsqrt(jnp.mean(xv * xv, axis=-1, keepdims=True) + RMS_EPS)
    xhat = xv * r
    dxhat = dout * gain
    dx = r * (dxhat - xhat * jnp.mean(dxhat * xhat, axis=-1, keepdims=True))
    return dx, dout * xhat


def _norm_fwd(x, gain):
    (h,) = _rowwise(lambda xv, g: ((_rms(xv, g),), ()), [(x, D, 0), (gain, D, 0)], [(D, BF16)], name="norm_fwd")
    return h


def _post_fwd(x, y, gain, next_gain):
    def fn(xv, yv, g, gn):
        out = xv + _rms(yv, g)
        return (out, _rms(out, gn)), ()

    return _rowwise(fn, [(x, D, 0), (y, D, 0), (gain, D, 0), (next_gain, D, 0)], [(D, F32), (D, BF16)],
                    name="post_fwd")


def _post_fwd_loss(x, y, gain, tgt):
    def fn(xv, yv, g, tv):
        e = xv + _rms(yv, g) - tv
        return (e * (1.0 / D),), (e * e,)

    return _rowwise(fn, [(x, D, 0), (y, D, 0), (gain, D, 0), (tgt, D, 0)], [(D, F32)], [D], name="post_fwd_loss")


def _post_bwd(g, y, gain):
    def fn(gv, yv, gn):
        dx, dg = _rms_bwd(gv, yv, gn)
        return (dx,), (dg,)

    return _rowwise(fn, [(g, D, 0), (y, D, 0), (gain, D, 0)], [(D, BF16)], [D], name="post_bwd")


def _pre_bwd(g, dh, x, gain):
    def fn(gv, dhv, xv, gn):
        dx, dg = _rms_bwd(dhv, xv, gn)
        return (gv + dx,), (dg,)

    return _rowwise(fn, [(g, D, 0), (dh, D, 0), (x, D, 0), (gain, D, 0)], [(D, F32)], [D], name="pre_bwd")


def _pool(u_arr, col_block, transpose, out_dtype, into=None, tc=256):
    n_t = POOL_W // tc
    per_group = POOL_GROUP // tc

    def body(u_ref, *rest):
        o_ref = rest[-1]
        grp = pl.program_id(0) // per_group
        t = lax.broadcasted_iota(jnp.int32, (S, 1), 0)
        for g in range(POOL_W // POOL_GROUP):
            @pl.when(grp == g)
            def _(g=g):
                xv = u_ref[...]
                cnt = jnp.minimum(t + 1, 2 << g).astype(F32)
                cur = xv / cnt if transpose else xv
                for k in (1, 2, 4, 8)[:g + 1]:
                    if transpose:
                        cur = cur + jnp.where(t < S - k, pltpu.roll(cur, S - k, 0), 0.0)
                    else:
                        cur = cur + jnp.where(t >= k, pltpu.roll(cur, k, 0), 0.0)
                res = cur - xv if transpose else cur / cnt - xv
                o_ref[...] = res.astype(o_ref.dtype)

    in_specs = [pl.BlockSpec((S, tc), lambda c: (0, col_block * n_t + c))]
    args = [u_arr]
    if into is not None:
        in_specs.append(pl.BlockSpec(memory_space=pl.ANY))
        args.append(into)
    return pl.pallas_call(
        body, name="pool_bwd" if transpose else "pool_fwd", grid=(n_t,),
        in_specs=in_specs,
        out_specs=pl.BlockSpec((S, tc), lambda c: (0, c)),
        out_shape=SDS((S, POOL_W) if into is None else into.shape, out_dtype),
        input_output_aliases={} if into is None else {1: 0},
        compiler_params=_cparams(("parallel",)),
    )(*args)


def _rope_tables(zero):
    pos = jnp.arange(S, dtype=jnp.int32).astype(F32) + zero
    inv_freq = ROPE_THETA ** (-jnp.arange(0, ROT_DIM, 2, dtype=F32) / ROT_DIM)
    ang = pos[:, None] * inv_freq[None, :]
    cos8, sin8 = jnp.cos(ang), jnp.sin(ang)
    half = ROT_DIM // 2
    zeros = jnp.zeros((S, HEAD_DIM - ROT_DIM), F32)
    cos = jnp.concatenate([cos8, cos8, jnp.ones((S, HEAD_DIM - ROT_DIM), F32)], axis=1)
    lo = jnp.concatenate([-sin8, jnp.zeros((S, half), F32), zeros], axis=1)
    hi = jnp.concatenate([jnp.zeros((S, half), F32), sin8, zeros], axis=1)
    rep = LANES // HEAD_DIM
    return jnp.tile(cos, (1, rep)), jnp.tile(lo, (1, rep)), jnp.tile(hi, (1, rep))


def _rotate(xv, cos, lo, hi, transpose):
    width = xv.shape[1]
    rep = width // LANES
    wide = lambda tab: jnp.concatenate([tab] * rep, axis=1)
    half = ROT_DIM // 2
    up = pltpu.roll(xv, width - half, 1)
    dn = pltpu.roll(xv, half, 1)
    mixed = up * wide(lo) + dn * wide(hi)
    return xv * wide(cos) - mixed if transpose else xv * wide(cos) + mixed


def _qkv_prep(proj, tables):
    cos, lo, hi = tables

    def fn(x, c, l, h):
        rot = _rotate(x[:, :2 * ATT_W], c, l, h, False)
        return (jnp.concatenate([(rot[:, :ATT_W] * HEAD_DIM ** -0.5).astype(BF16), rot[:, ATT_W:].astype(BF16),
                                 x[:, 2 * ATT_W:].astype(BF16)], axis=1),), ()

    (qkv,) = _rowwise(fn, [(proj, 3 * ATT_W, 0), (cos, LANES, 0), (lo, LANES, 0), (hi, LANES, 0)],
                      [(3 * ATT_W, BF16)], name="qkv_prep")
    return qkv


ATT_T = 512


def _multiplicity(delta):
    ok = delta >= 0
    near = jnp.where(ok & (delta <= 128), 1.0, 0.0)
    mid = jnp.where(ok & (delta <= 512) & ((delta & 3) == 0), 1.0, 0.0)
    far = jnp.where(ok & ((delta & 15) == 0), 1.0, 0.0)
    return near + mid + far


def _attention_bias(zero):
    t = ATT_T
    pos = jnp.arange(t, dtype=jnp.int32) + jnp.asarray(zero).astype(jnp.int32)
    delta = jnp.arange(S // t, dtype=jnp.int32)[:, None, None] * t + pos[None, :, None] - pos[None, None, :]
    mult = _multiplicity(delta)
    return jnp.where(mult > 0.0, jnp.log(jnp.maximum(mult, 1.0)), -1e30).astype(F32)


def _head_split(v, first):
    zero = jnp.zeros_like(v)
    return [jnp.where(first, v, zero), jnp.where(first, zero, v)]


def _flash_fwd(qkv, bias):
    t = ATT_T
    n_hp = ATT_W // LANES

    def body(q_ref, k_ref, v_ref, b_ref, o_ref, lse_ref):
        i = pl.program_id(1)
        first = lax.broadcasted_iota(jnp.int32, (1, LANES), 1) < HEAD_DIM
        qs = _head_split(q_ref[...], first)

        def kv_step(j, carry):
            m0, l0, m1, l1, acc = carry
            off = pl.multiple_of(j * t, t)
            kb = k_ref[pl.ds(off, t), :]
            vs = _head_split(v_ref[pl.ds(off, t), :], first)
            bias_t = b_ref[i - j]
            new = []
            pv = None
            for h, (m_prev, l_prev) in enumerate(((m0, l0), (m1, l1))):
                s = lax.dot_general(qs[h], kb, NT, preferred_element_type=F32) + bias_t
                m_new = jnp.maximum(m_prev, jnp.max(s, axis=1, keepdims=True))
                p = jnp.exp(s - m_new)
                alpha = jnp.exp(m_prev - m_new)
                l_new = alpha * l_prev + jnp.sum(p, axis=1, keepdims=True)
                d = lax.dot_general(p.astype(BF16), vs[h], NN, preferred_element_type=F32)
                pv = d if pv is None else pv + d
                new.append((m_new, l_new, alpha))
            acc = acc * jnp.where(first, new[0][2], new[1][2]) + pv
            return new[0][0], new[0][1], new[1][0], new[1][1], acc

        neg = jnp.full((t, 1), -1e30, F32)
        zero = jnp.zeros((t, 1), F32)
        m0, l0, m1, l1, acc = lax.fori_loop(0, i + 1, kv_step, (neg, zero, neg, zero, jnp.zeros((t, LANES), F32)))
        o_ref[...] = acc * jnp.where(first, 1.0 / l0, 1.0 / l1)
        lse_ref[...] = jnp.where(first, m0 + jnp.log(l0), m1 + jnp.log(l1))

    blk = pl.BlockSpec((t, LANES), lambda hp, i: (i, hp))
    k_full = pl.BlockSpec((S, LANES), lambda hp, i: (0, n_hp + hp))
    v_full = pl.BlockSpec((S, LANES), lambda hp, i: (0, 2 * n_hp + hp))
    return pl.pallas_call(
        body, name="flash_fwd", grid=(n_hp, S // t),
        in_specs=[blk, k_full, v_full, pl.BlockSpec((S // t, t, t), lambda hp, i: (0, 0, 0))], out_specs=[blk, blk],
        out_shape=[SDS((S, ATT_W), F32), SDS((S, ATT_W), F32)],
        compiler_params=_cparams(("parallel", "arbitrary")),
    )(qkv, qkv, qkv, bias)


def _flash_bwd(qkv, o, do, lse, bias, after=()):
    t = ATT_T
    n_hp = ATT_W // LANES
    n_t = S // t

    def body(q_ref, k_ref, v_ref, o_ref, do_ref, lse_ref, b_ref, *rest):
        dq_ref, dk_ref, dv_ref = rest[-3:]
        j = pl.program_id(1)
        first = lax.broadcasted_iota(jnp.int32, (1, LANES), 1) < HEAD_DIM

        @pl.when(j == 0)
        def _():
            dq_ref[...] = jnp.zeros_like(dq_ref)

        kb = k_ref[...]
        vb = v_ref[...]
        ks = _head_split(kb, first)

        def q_step(i, carry):
            dk_acc, dv_acc = carry
            rows = pl.ds(pl.multiple_of(i * t, t), t)
            qs = _head_split(q_ref[rows, :], first)
            dob = do_ref[rows, :]
            prod = dob * o_ref[rows, :]
            d_all = jnp.sum(prod, axis=1, keepdims=True)
            d0 = jnp.sum(jnp.where(first, prod, 0.0), axis=1, keepdims=True)
            lse_b = lse_ref[rows, :]
            lse0 = jnp.max(jnp.where(first, lse_b, -jnp.inf), axis=1, keepdims=True)
            lse1 = jnp.max(jnp.where(first, -jnp.inf, lse_b), axis=1, keepdims=True)
            dos = _head_split(dob.astype(BF16), first)
            bias_t = b_ref[i - j]
            dq_t = jnp.zeros((t, LANES), F32)
            for h, (lse_h, d_h) in enumerate(((lse0, d0), (lse1, d_all - d0))):
                s = lax.dot_general(qs[h], kb, NT, preferred_element_type=F32)
                p = jnp.exp(s + (bias_t - lse_h))
                dp = lax.dot_general(dos[h], vb, NT, preferred_element_type=F32)
                ds = (p * (dp - d_h)).astype(BF16)
                dv_acc = dv_acc + lax.dot_general(p.astype(BF16), dos[h], TN, preferred_element_type=F32)
                dk_acc = dk_acc + lax.dot_general(ds, qs[h], TN, preferred_element_type=F32)
                dq_t = dq_t + lax.dot_general(ds, ks[h], NN, preferred_element_type=F32)
            dq_ref[rows, :] += dq_t
            return dk_acc, dv_acc

        zero = jnp.zeros((t, LANES), F32)
        dk_acc, dv_acc = lax.fori_loop(j, n_t, q_step, (zero, zero))
        dk_ref[...] = dk_acc
        dv_ref[...] = dv_acc

    blk = pl.BlockSpec((t, LANES), lambda hp, j: (j, hp))
    full = pl.BlockSpec((S, LANES), lambda hp, j: (0, hp))
    k_blk = pl.BlockSpec((t, LANES), lambda hp, j: (j, n_hp + hp))
    v_blk = pl.BlockSpec((t, LANES), lambda hp, j: (j, 2 * n_hp + hp))
    return pl.pallas_call(
        body, name="flash_bwd", grid=(n_hp, n_t),
        in_specs=([full, k_blk, v_blk, full, full, full, pl.BlockSpec((n_t, t, t), lambda hp, j: (0, 0, 0))]
                  + [pl.BlockSpec(memory_space=pl.ANY)] * len(after)),
        out_specs=[full, blk, blk],
        out_shape=[SDS((S, ATT_W), F32)] * 3,
        compiler_params=_cparams(("parallel", "arbitrary")),
    )(qkv, qkv, qkv, o, do, lse, bias, *after)


SCAN_T = 256
SCAN_GROUP = 8
SCAN_STEPS = (1, 2, 4)
ST_ROWS = 2 * N_CPLX // LANES
HALF = ST_ROWS // 2


def _scan_tables(lam_t):
    lam = lax.complex(lam_t[:HALF].reshape(N_CPLX), lam_t[HALF:].reshape(N_CPLX))
    pows = [lam]
    for _ in range(SCAN_GROUP - 1):
        pows.append(pows[-1] * lam)
    pows = jnp.stack(pows)
    sub = jnp.arange(SCAN_GROUP)[:, None]
    fwd = [jnp.where(sub >= k, pows[k - 1][None, :], 0.0) for k in SCAN_STEPS] + [pows]
    conj = jnp.conj(pows)
    bwd = [jnp.where(sub <= SCAN_GROUP - 1 - k, conj[k - 1][None, :], 0.0) for k in SCAN_STEPS] + [conj[::-1]]

    def pack(tabs):
        return jnp.stack([jnp.concatenate([jnp.real(t), jnp.imag(t)], axis=1) for t in tabs]).astype(F32)

    return pack(fwd), pack(bwd)


def _cmul_add(xr, xi, lr, li, sr, si):
    return xr + lr * sr - li * si, xi + lr * si + li * sr


def _group_scan(xr, xi, tab_ref, cr, ci, reverse):
    for j, k in enumerate(SCAN_STEPS):
        shift = SCAN_GROUP - k if reverse else k
        xr, xi = _cmul_add(xr, xi, tab_ref[j, :, :N_CPLX], tab_ref[j, :, N_CPLX:],
                           pltpu.roll(xr, shift, 0), pltpu.roll(xi, shift, 0))
    return _cmul_add(xr, xi, tab_ref[3, :, :N_CPLX], tab_ref[3, :, N_CPLX:],
                     jnp.broadcast_to(cr, (SCAN_GROUP, N_CPLX)), jnp.broadcast_to(ci, (SCAN_GROUP, N_CPLX)))


SSM_SUPER = 4
SB_ROWS = SSM_W // SSM_SUPER
SB_COLS = N_CPLX // SSM_SUPER


def _super_blocks():
    return [(slice(b * SB_ROWS, (b + 1) * SB_ROWS), slice(h * SB_COLS, (h + 1) * SB_COLS),
             slice(h * N_CPLX + b * SB_COLS, h * N_CPLX + (b + 1) * SB_COLS))
            for b in range(SSM_SUPER) for h in range(2)]


def _dot16(a, b, dims):
    return lax.dot_general(a.astype(BF16), b.astype(BF16), dims, preferred_element_type=F32)


def _s5_fwd(tab, u_arr, u_cols, w_b, w_ct):
    nc = N_CPLX

    def body(tab_ref, u_ref, wb_ref, wct_ref, st_ref, y_ref, carry, bu_scr):
        @pl.when(pl.program_id(0) == 0)
        def _():
            carry[...] = jnp.zeros_like(carry)

        for rows_b, cols_c, cols_s in _super_blocks():
            bu_scr[:, cols_s] = _dot16(u_ref[:, rows_b], wb_ref[rows_b, cols_c], NN)

        def group(a, c):
            rows = pl.ds(pl.multiple_of(a * SCAN_GROUP, SCAN_GROUP), SCAN_GROUP)
            xr, xi = _group_scan(bu_scr[rows, :nc], bu_scr[rows, nc:], tab_ref, c[0], c[1], False)
            st_ref[rows, :nc] = xr
            st_ref[rows, nc:] = xi
            return xr[SCAN_GROUP - 1:SCAN_GROUP, :], xi[SCAN_GROUP - 1:SCAN_GROUP, :]

        cr, ci = lax.fori_loop(0, SCAN_T // SCAN_GROUP, group, (carry[:, :nc], carry[:, nc:]), unroll=2)
        carry[:, :nc] = cr
        carry[:, nc:] = ci

        for b in range(SSM_SUPER):
            (rows_b, cols_re, st_re), (_, cols_im, st_im) = _super_blocks()[2 * b:2 * b + 2]
            y_ref[:, rows_b] = (_dot16(st_ref[:, st_re], wct_ref[rows_b, cols_re], NT)
                                + _dot16(st_ref[:, st_im], wct_ref[rows_b, cols_im], NT))

    const = lambda shape: pl.BlockSpec(shape, lambda i: (0,) * len(shape))
    return pl.pallas_call(
        body, name="s5_fwd", grid=(S // SCAN_T,),
        in_specs=[const((4, SCAN_GROUP, 2 * nc)), pl.BlockSpec((SCAN_T, SSM_W), lambda i: (i, u_cols[0] // SSM_W)),
                  const((SSM_W, 2 * SB_COLS)), const((SSM_W, 2 * SB_COLS))],
        out_specs=[pl.BlockSpec((SCAN_T, 2 * nc), lambda i: (i, 0)), pl.BlockSpec((SCAN_T, SSM_W), lambda i: (i, 0))],
        out_shape=[SDS((S, 2 * nc), F32), SDS((S, SSM_W), F32)],
        scratch_shapes=[pltpu.VMEM((1, 2 * nc), F32), pltpu.VMEM((SCAN_T, 2 * nc), F32)],
        compiler_params=_cparams(("arbitrary",)),
    )(tab, u_arr, w_b, w_ct)


def _s5_bwd(tab, dy, states, u_arr, u_cols, w_b, w_ct):
    n_blk = S // SCAN_T
    nc = N_CPLX

    def body(tab_ref, dy_ref, x_ref, u_ref, wb_ref, wct_ref, du_ref, dlam_ref, dwb_ref, dwct_ref,
             carry, acc, d_scr, g_ref):
        i = pl.program_id(0)

        @pl.when(i == 0)
        def _():
            carry[...] = jnp.zeros_like(carry)
            acc[...] = jnp.zeros_like(acc)
            dwb_ref[...] = jnp.zeros_like(dwb_ref)
            dwct_ref[...] = jnp.zeros_like(dwct_ref)

        for rows_b, cols_c, cols_s in _super_blocks():
            d_scr[:, cols_s] = _dot16(dy_ref[:, rows_b], wct_ref[rows_b, cols_c], NN)

        last_row = lax.broadcasted_iota(jnp.int32, (SCAN_GROUP, 1), 0) == SCAN_GROUP - 1
        d_ref = d_scr

        def group(j, c):
            cr, ci = c
            rows = pl.ds(pl.multiple_of((SCAN_T // SCAN_GROUP - 1 - j) * SCAN_GROUP, SCAN_GROUP), SCAN_GROUP)
            gr, gi = _group_scan(d_ref[rows, :nc], d_ref[rows, nc:], tab_ref, cr, ci, True)
            g_ref[rows, :nc] = gr
            g_ref[rows, nc:] = gi
            nr = jnp.where(last_row, jnp.broadcast_to(cr, (SCAN_GROUP, nc)), pltpu.roll(gr, SCAN_GROUP - 1, 0))
            ni = jnp.where(last_row, jnp.broadcast_to(ci, (SCAN_GROUP, nc)), pltpu.roll(gi, SCAN_GROUP - 1, 0))
            sr, si = x_ref[rows, :nc], x_ref[rows, nc:]
            acc[:, :nc] += nr * sr + ni * si
            acc[:, nc:] += ni * sr - nr * si
            return gr[0:1, :], gi[0:1, :]

        cr, ci = lax.fori_loop(0, SCAN_T // SCAN_GROUP, group, (carry[:, :nc], carry[:, nc:]), unroll=2)
        carry[:, :nc] = cr
        carry[:, nc:] = ci

        for b in range(SSM_SUPER):
            (rows_b, cols_re, st_re), (_, cols_im, st_im) = _super_blocks()[2 * b:2 * b + 2]
            du_ref[:, rows_b] = (_dot16(g_ref[:, st_re], wb_ref[rows_b, cols_re], NT)
                                 + _dot16(g_ref[:, st_im], wb_ref[rows_b, cols_im], NT))
            for cols_c, cols_s in ((cols_re, st_re), (cols_im, st_im)):
                dwb_ref[rows_b, cols_c] += _dot16(u_ref[:, rows_b], g_ref[:, cols_s], TN)
                dwct_ref[rows_b, cols_c] += _dot16(dy_ref[:, rows_b], x_ref[:, cols_s], TN)

        @pl.when(i == n_blk - 1)
        def _():
            dlam_ref[...] = jnp.sum(acc[...], axis=0, keepdims=True)

    const = lambda shape: pl.BlockSpec(shape, lambda i: (0,) * len(shape))
    rows = lambda width, col_block=0: pl.BlockSpec((SCAN_T, width), lambda i: (n_blk - 1 - i, col_block))
    maps = const((SSM_W, 2 * SB_COLS))
    return pl.pallas_call(
        body, name="s5_bwd", grid=(n_blk,),
        in_specs=[const((4, SCAN_GROUP, 2 * nc)), rows(SSM_W), rows(2 * nc), rows(SSM_W, u_cols[0] // SSM_W), maps, maps],
        out_specs=[rows(SSM_W), const((1, 2 * nc)), maps, maps],
        out_shape=[SDS((S, SSM_W), F32), SDS((1, 2 * nc), F32), SDS((SSM_W, 2 * SB_COLS), F32),
                   SDS((SSM_W, 2 * SB_COLS), F32)],
        scratch_shapes=[pltpu.VMEM((1, 2 * nc), F32), pltpu.VMEM((SCAN_GROUP, 2 * nc), F32),
                        pltpu.VMEM((SCAN_T, 2 * nc), F32), pltpu.VMEM((SCAN_T, 2 * nc), F32)],
        compiler_params=_cparams(("arbitrary",)),
    )(tab, dy, states, u_arr, w_b, w_ct)


def _ssm_prep(a_re, a_im, log_dt, b_re, b_im, c_re, c_im):
    lam = lax.complex(a_re, a_im)
    dt = jnp.exp(log_dt)[:, None]
    lam_bar = jnp.exp(lam * dt)
    b_bar = ((lam_bar - 1.0) / lam)[..., None] * lax.complex(b_re, b_im)
    lam_t = jnp.concatenate([jnp.real(lam_bar).reshape(HALF, LANES), jnp.imag(lam_bar).reshape(HALF, LANES)], axis=0)
    groups_per_super = SSM_GROUPS // SSM_SUPER
    on_diag = ((lax.broadcasted_iota(jnp.int32, (SSM_W, SB_COLS), 0) // SSM_GROUP) % groups_per_super
               == lax.broadcasted_iota(jnp.int32, (SSM_W, SB_COLS), 1) // SSM_STATE)

    def compact(m):
        return jnp.where(on_diag, jnp.tile(m.reshape(SSM_W, SSM_STATE), (1, groups_per_super)), 0.0)

    w_b = jnp.concatenate([compact(jnp.real(b_bar).transpose(0, 2, 1)),
                           compact(jnp.imag(b_bar).transpose(0, 2, 1))], axis=1)
    w_ct = jnp.concatenate([compact(c_re), -compact(c_im)], axis=1)
    return lam_t, w_b, w_ct


U_SSM_COLS = (4 * ATT_W, SSM_W)


def _row(v):
    return v.reshape(1, -1)


def _even_fwd(x, h, tail, pre, post, w_in, late_w, glu_b, ssm_d, prep, tables):
    lam_t, w_b, w_ct = prep
    proj = _mm(h, w_in, "nn", F32, b_blocks=True)
    qkv = _qkv_prep(proj, tables[:3])
    w_out, glu_w = late_w(qkv)
    att, lse = _flash_fwd(qkv, tables[3])
    scan_fwd_tab, scan_bwd_tab = _scan_tables(lam_t)
    states, y = _s5_fwd(scan_fwd_tab, proj, U_SSM_COLS, w_b, w_ct)

    def act1(yv, uv, dv):
        return (_gelu_and_grad(yv + dv * uv)[0],), ()

    (z1,) = _rowwise(act1, [(y, SSM_W, 0), (proj, SSM_W, 8), (ssm_d, SSM_W, 0)], [(SSM_W, F32)], name="ssm_act_fwd")
    lin = _mm(z1, glu_w, "nn", F32)

    def gate(att_v, ga, gs, z1v, linv, bv):
        ssm_out = z1v * _sigmoid(linv + bv)
        return (jnp.concatenate([att_v * _silu_and_grad(ga)[0], ssm_out * _silu_and_grad(gs)[0]], axis=1),), ()

    (merged,) = _rowwise(gate, [(att, ATT_W, 0), (proj, ATT_W, 3), (proj, SSM_W, 9), (z1, SSM_W, 0),
                                (lin, SSM_W, 0), (glu_b, SSM_W, 0)], [(EVEN_OUT, BF16)], name="even_gate_fwd")
    yout = _mm(merged, w_out, "nn", F32)
    saved = (x, h, proj, qkv, att, lse, states, y, z1, lin, merged, yout, w_out, glu_w, scan_bwd_tab)
    return tail(x, yout, post) + (saved,)


def _even_bwd(g, saved, pre, post, w_in, late_w, glu_b, ssm_d, prep, tables, on_w, on_ssm):
    x, h, proj, qkv, att, lse, states, y, z1, lin, merged, yout, w_out, glu_w, scan_bwd_tab = saved
    lam_t, w_b, w_ct = prep
    dyout, dpost = _post_bwd(g, yout, post)
    dmerged = _mm(dyout, w_out, "nt", F32)
    dw_out = _mm(merged, dyout, "tn", BF16)

    def gate_bwd(dm_a, dm_s, att_v, ga, gs, z1v, linv, bv):
        sa, dsa = _silu_and_grad(ga)
        ss, dss = _silu_and_grad(gs)
        sig = _sigmoid(linv + bv)
        ssm_out = z1v * sig
        dssm = dm_s * ss
        dlin = dssm * z1v * sig * (1.0 - sig)
        return (dm_a * sa, dm_a * att_v * dsa, dm_s * ssm_out * dss, dssm * sig, dlin), (dlin,)

    datt, dg_att, dg_ssm, dz1a, dlin, dglu_b = _rowwise(
        gate_bwd, [(dmerged, ATT_W, 0), (dmerged, SSM_W, 2), (att, ATT_W, 0), (proj, ATT_W, 3), (proj, SSM_W, 9),
                   (z1, SSM_W, 0), (lin, SSM_W, 0), (glu_b, SSM_W, 0)],
        [(ATT_W, F32), (ATT_W, BF16), (SSM_W, BF16), (SSM_W, F32), (SSM_W, BF16)], [SSM_W], name="even_gate_bwd")
    dz1b = _mm(dlin, glu_w, "nt", F32)
    dglu_w = _mm(z1, dlin, "tn", BF16)

    def act1_bwd(da, db, yv, uv, dv):
        dpre = (da + db) * _gelu_and_grad(yv + dv * uv)[1]
        return (dpre, dpre * dv), (dpre * uv,)

    sent_late_w = on_w(dict(w_out=dw_out, glu_w=dglu_w))
    dy, du_direct, dd = _rowwise(act1_bwd, [(dz1a, SSM_W, 0), (dz1b, SSM_W, 0), (y, SSM_W, 0), (proj, SSM_W, 8),
                                            (ssm_d, SSM_W, 0)], [(SSM_W, BF16), (SSM_W, F32)], [SSM_W],
                                 name="ssm_act_bwd", after=(sent_late_w,))
    du_state, dlam_row, dw_b, dw_ct = _s5_bwd(scan_bwd_tab, dy, states, proj, U_SSM_COLS, w_b, w_ct)
    dlam = jnp.concatenate([dlam_row[0, :N_CPLX].reshape(HALF, LANES), dlam_row[0, N_CPLX:].reshape(HALF, LANES)],
                           axis=0)
    sent_ssm = on_ssm((dlam, dw_b, dw_ct))
    dq, dk, dv = _flash_bwd(qkv, att, datt, lse, tables[3], after=() if sent_ssm is None else (sent_ssm,))

    def assemble(dqv, dkv, dvv, dga, dua, dub, dgs, c, l, h):
        rot = _rotate(jnp.concatenate([dqv, dkv], axis=1), c, l, h, True)
        return (jnp.concatenate([(rot[:, :ATT_W] * HEAD_DIM ** -0.5).astype(BF16), rot[:, ATT_W:].astype(BF16),
                                 dvv.astype(BF16), dga, (dua + dub).astype(BF16), dgs], axis=1),), ()

    (dproj,) = _rowwise(assemble, [(dq, ATT_W, 0), (dk, ATT_W, 0), (dv, ATT_W, 0), (dg_att, ATT_W, 0),
                                   (du_state, SSM_W, 0), (du_direct, SSM_W, 0), (dg_ssm, SSM_W, 0),
                                   (tables[0], LANES, 0), (tables[1], LANES, 0), (tables[2], LANES, 0)],
                        [(EVEN_IN, BF16)], name="dproj_assemble")
    dw_in = _mm(h, dproj, "tn", BF16, out_blocks=True)
    sent = on_w(dict(w_in=dw_in))
    dh = _mm(dproj, w_in, "nt", F32, b_blocks=True, after=(sent,))
    g_prev, dpre = _pre_bwd(g, dh, x, pre)
    return g_prev, dict(pre=dpre, post=dpost, glu_b=dglu_b, ssm_d=dd)


def _odd_fwd(x, h, tail, pre, post, w_in, pool_w, pool_scale, w_out):
    proj = _mm(h, w_in, "nn", F32, b_blocks=True)
    mixed = _pool(proj, 0, False, BF16)
    ylin = _gmm(mixed, pool_w, "nn", F32)

    def gate(yl, gt, sc):
        return (yl * sc * _silu_and_grad(gt)[0],), ()

    (z,) = _rowwise(gate, [(ylin, POOL_W, 0), (proj, POOL_W, 1), (pool_scale, POOL_W, 0)], [(POOL_W, BF16)],
                    name="odd_gate_fwd")
    yout = _mm(z, w_out, "nn", F32)
    return tail(x, yout, post) + ((x, h, proj, mixed, ylin, z, yout),)


def _odd_bwd(g, saved, pre, post, w_in, pool_w, pool_scale, w_out, on_w):
    x, h, proj, mixed, ylin, z, yout = saved
    dyout, dpost = _post_bwd(g, yout, post)
    dz = _mm(dyout, w_out, "nt", F32)
    dw_out = _mm(z, dyout, "tn", BF16)

    def gate_bwd(dzv, yl, gt, sc):
        sg, dsg = _silu_and_grad(gt)
        tt = dzv * sg
        return (tt * sc, dzv * yl * sc * dsg), (tt * yl,)

    dylin, dproj_gate, dscale = _rowwise(gate_bwd, [(dz, POOL_W, 0), (ylin, POOL_W, 0), (proj, POOL_W, 1),
                                                    (pool_scale, POOL_W, 0)],
                                         [(POOL_W, BF16), (POOL_W, BF16, ODD_IN, 1)], [POOL_W], name="odd_gate_bwd")
    dmixed = _gmm(dylin, pool_w, "nt", F32)
    dpool_w = _gmm(mixed, dylin, "tn", BF16)
    dproj = _pool(dmixed, 0, True, BF16, into=dproj_gate)
    dw_in = _mm(h, dproj, "tn", BF16, out_blocks=True)
    sent = on_w(dict(w_in=dw_in, w_out=dw_out, pool_w=dpool_w))
    dh = _mm(dproj, w_in, "nt", F32, b_blocks=True, after=(sent,))
    g_prev, dpre = _pre_bwd(g, dh, x, pre)
    return g_prev, dict(pre=dpre, post=dpost, pool_scale=dscale)


def _my_index():
    return 4 * lax.axis_index("x") + 2 * lax.axis_index("y") + lax.axis_index("c")


HBM_SPEC = pl.BlockSpec(memory_space=pltpu.HBM)
SEM_SPEC = pl.BlockSpec(memory_space=pltpu.SEMAPHORE)
SPLIT_EFFECT = pltpu.SideEffectType.DATAFLOW_SIDE_EFFECTING


def _device_of(j):
    return (j // 4, (j // 2) % 2, j % 2)


def _split_copy(srcs, lands, send_sems, recv_sems, gather, i, j, dst_slot, recv_slot):
    return pltpu.make_async_remote_copy(
        src_ref=srcs[i] if gather else srcs[i].at[j], dst_ref=lands[i].at[dst_slot],
        send_sem=send_sems.at[i * N_DEV + j], recv_sem=recv_sems.at[i * N_DEV + recv_slot],
        device_id=_device_of(j), device_id_type=MESH_ID)


def _own_copy(srcs, lands, send_sems, gather, i, me):
    return pltpu.make_async_copy(srcs[i] if gather else srcs[i].at[me], lands[i].at[me], send_sems.at[i * N_DEV + me])


def _xchg_start(name, srcs, gather, after=()):
    n = len(srcs)
    n_in = n + len(after)

    def body(*refs):
        src_refs = refs[:n]
        send_sems, recv_sems, token = refs[n_in], refs[n_in + 1], refs[-1]
        land_refs = refs[n_in + 2 + n:n_in + 2 + 2 * n]
        me = _my_index()
        for j in range(N_DEV):
            @pl.when(me != j)
            def _(j=j):
                for i in range(n):
                    _split_copy(src_refs, land_refs, send_sems, recv_sems, gather, i, j, me, me).start()
        for i in range(n):
            _own_copy(src_refs, land_refs, send_sems, gather, i, me).start()
        token[...] = jnp.zeros_like(token)

    land_shapes = [((N_DEV,) + a.shape) if gather else a.shape for a in srcs]
    thru = ([pltpu.HBM(a.shape, a.dtype) for a in srcs] + [pltpu.HBM(s, a.dtype) for s, a in zip(land_shapes, srcs)])
    res = pl.pallas_call(
        body, name=name,
        out_shape=(pltpu.SemaphoreType.DMA((n * N_DEV,)), pltpu.SemaphoreType.DMA((n * N_DEV,)), *thru,
                   SDS((8, LANES), F32)),
        in_specs=[HBM_SPEC] * n + [pl.BlockSpec(memory_space=pl.ANY)] * len(after),
        out_specs=(SEM_SPEC, SEM_SPEC, *([HBM_SPEC] * (2 * n)), pl.BlockSpec(memory_space=pltpu.VMEM)),
        input_output_aliases={i: 2 + i for i in range(n)},
        compiler_params=pltpu.CompilerParams(has_side_effects=SPLIT_EFFECT),
    )(*[pltpu.with_memory_space_constraint(a, pltpu.HBM) for a in srcs], *after)
    return res[0], res[1], list(res[2:2 + n]), list(res[2 + n:2 + 2 * n]), res[-1]


def _xchg_wait(name, started, gather, after):
    send_sems, recv_sems, srcs, lands, _ = started
    n = len(srcs)

    def body(*refs):
        src_refs, land_refs = refs[:n], refs[n:2 * n]
        send_r, recv_r = refs[2 * n], refs[2 * n + 1]
        me = _my_index()
        for j in range(N_DEV):
            @pl.when(me != j)
            def _(j=j):
                for i in range(n):
                    _split_copy(src_refs, land_refs, send_r, recv_r, gather, i, j, me, me).wait_send()
                    _split_copy(src_refs, land_refs, send_r, recv_r, gather, i, j, j, j).wait_recv()
        for i in range(n):
            _own_copy(src_refs, land_refs, send_r, gather, i, me).wait()

    thru = [pltpu.HBM(a.shape, a.dtype) for a in list(srcs) + list(lands)]
    res = pl.pallas_call(
        body, name=name, out_shape=tuple(thru),
        in_specs=[HBM_SPEC] * (2 * n) + [SEM_SPEC, SEM_SPEC] + [pl.BlockSpec(memory_space=pl.ANY)] * len(after),
        out_specs=tuple([HBM_SPEC] * (2 * n)),
        input_output_aliases={i: i for i in range(2 * n)},
        compiler_params=pltpu.CompilerParams(has_side_effects=SPLIT_EFFECT),
    )(*srcs, *lands, send_sems, recv_sems, *after)
    return list(res[n:])


def _adam_layers(w, slot_list, m, v, name):
    n_l, r, c = w.shape
    ns = slot_list[0].shape[0]
    tr = r
    while tr * c * 4 > (1 << 20) and tr % 16 == 0:
        tr //= 2
    assert r % tr == 0 and len(slot_list) == n_l

    def body(*refs):
        w_ref, slot_refs = refs[0], refs[1:1 + n_l]
        m_ref, v_ref, go_ref, d_ref, mo_ref, vo_ref = refs[1 + n_l:]
        layer = pl.program_id(0)
        g = None
        for l, g_ref in enumerate(slot_refs):
            gl = g_ref[0].astype(F32)
            for s in range(1, ns):
                gl = gl + g_ref[s].astype(F32)
            g = gl if g is None else jnp.where(layer == l, gl, g)
        mn = ADAM_B1 * m_ref[...] + (1.0 - ADAM_B1) * g
        vn = ADAM_B2 * v_ref[...] + (1.0 - ADAM_B2) * (g * g)
        m_hat = mn / (1.0 - ADAM_B1 ** ADAM_STEP)
        v_hat = vn / (1.0 - ADAM_B2 ** ADAM_STEP)
        go_ref[...] = g
        d_ref[...] = -ADAM_LR * (m_hat / (jnp.sqrt(v_hat) + ADAM_EPS) + ADAM_WD * w_ref[...])
        mo_ref[...] = mn
        vo_ref[...] = vn

    blk = pl.BlockSpec((None, tr, c), lambda l, i: (l, i, 0))
    slot_specs = [pl.BlockSpec((ns, tr, c), lambda l, i, k=k: (0, jnp.where(l == k, i, 0), 0)) for k in range(n_l)]
    return pl.pallas_call(
        body, name=name, grid=(n_l, r // tr),
        in_specs=[blk] + slot_specs + [blk, blk],
        out_specs=[blk] * 4, out_shape=[SDS((n_l, r, c), F32)] * 4,
        compiler_params=_cparams(("arbitrary", "arbitrary")),
    )(w, *slot_list, m, v)


def _adam(w, gslots, m, v, name):
    r, c = w.shape
    ns = gslots.shape[0]
    tr = r
    while tr * c * 4 > (1 << 20) and tr % 16 == 0:
        tr //= 2
    assert r % tr == 0

    def body(w_ref, g_ref, m_ref, v_ref, go_ref, d_ref, mo_ref, vo_ref):
        g = g_ref[0].astype(F32)
        for s in range(1, ns):
            g = g + g_ref[s].astype(F32)
        wv = w_ref[...]
        mn = ADAM_B1 * m_ref[...] + (1.0 - ADAM_B1) * g
        vn = ADAM_B2 * v_ref[...] + (1.0 - ADAM_B2) * (g * g)
        m_hat = mn / (1.0 - ADAM_B1 ** ADAM_STEP)
        v_hat = vn / (1.0 - ADAM_B2 ** ADAM_STEP)
        go_ref[...] = g
        d_ref[...] = -ADAM_LR * (m_hat / (jnp.sqrt(v_hat) + ADAM_EPS) + ADAM_WD * wv)
        mo_ref[...] = mn
        vo_ref[...] = vn

    blk = pl.BlockSpec((tr, c), lambda i: (i, 0))
    return pl.pallas_call(
        body, name=name, grid=(r // tr,),
        in_specs=[blk, pl.BlockSpec((ns, tr, c), lambda i: (0, i, 0)), blk, blk],
        out_specs=[blk] * 4, out_shape=[SDS((r, c), F32)] * 4,
        compiler_params=_cparams(("parallel",)),
    )(w, gslots, m, v)


def _sum_slots(slots, name):
    ns, r, c = slots.shape

    def body(g_ref, o_ref):
        g = g_ref[0]
        for s in range(1, ns):
            g = g + g_ref[s]
        o_ref[...] = g

    return pl.pallas_call(
        body, name=name, grid=(1,),
        in_specs=[pl.BlockSpec((ns, r, c), lambda i: (0, 0, 0))], out_specs=pl.BlockSpec((r, c), lambda i: (0, 0)),
        out_shape=SDS((r, c), F32), compiler_params=_cparams(("arbitrary",)),
    )(slots)


def _adam_params(params, name):
    n = len(params)

    def body(*refs):
        ins, outs = refs[:5 * n], refs[5 * n:]
        for p in range(n):
            w_ref, m_ref, v_ref, g_first, g_rest = ins[5 * p:5 * p + 5]
            go_ref, d_ref, mo_ref, vo_ref = outs[4 * p:4 * p + 4]
            for part, g_ref in ((slice(0, 1), g_first), (slice(1, w_ref.shape[0]), g_rest)):
                g = g_ref[...]
                mn = ADAM_B1 * m_ref[part] + (1.0 - ADAM_B1) * g
                vn = ADAM_B2 * v_ref[part] + (1.0 - ADAM_B2) * (g * g)
                m_hat = mn / (1.0 - ADAM_B1 ** ADAM_STEP)
                v_hat = vn / (1.0 - ADAM_B2 ** ADAM_STEP)
                go_ref[part] = g
                d_ref[part] = -ADAM_LR * (m_hat / (jnp.sqrt(v_hat) + ADAM_EPS) + ADAM_WD * w_ref[part])
                mo_ref[part] = mn
                vo_ref[part] = vn

    def whole(a):
        return pl.BlockSpec(a.shape, lambda i, nd=a.ndim: (0,) * nd)

    flat = [pltpu.with_memory_space_constraint(a, pltpu.HBM) for prm in params for a in prm]
    outs = pl.pallas_call(
        body, name=name, grid=(1,),
        in_specs=[whole(a) for a in flat],
        out_specs=[whole(prm[0]) for prm in params for _ in range(4)],
        out_shape=[SDS(prm[0].shape, F32) for prm in params for _ in range(4)],
        compiler_params=_cparams(("arbitrary",)),
    )(*flat)
    return [outs[4 * p:4 * p + 4] for p in range(n)]


SMALL_NAMES = ("pre_norm", "post_norm", "ssm_a_re", "ssm_a_im", "ssm_log_dt", "ssm_b_re", "ssm_b_im", "ssm_c_re",
               "ssm_c_im", "ssm_d", "ssm_glu_b")
SSM_NAMES = ("ssm_a_re", "ssm_a_im", "ssm_log_dt", "ssm_b_re", "ssm_b_im", "ssm_c_re", "ssm_c_im")
SHARDED_NAMES = ("even_w_in", "even_w_out", "ssm_glu_w", "odd_w_in", "pool_w", "odd_w_out")
WEIGHT_ORDER = ("pre_norm", "post_norm", "even_w_in", "even_w_out", "ssm_a_re", "ssm_a_im", "ssm_log_dt", "ssm_b_re",
                "ssm_b_im", "ssm_c_re", "ssm_c_im", "ssm_d", "ssm_glu_w", "ssm_glu_b", "odd_w_in", "pool_w",
                "pool_scale", "odd_w_out")
PACK_ROWS_ALIGN = 8


def _pack(parts):
    flat = jnp.concatenate([p.reshape(-1).astype(F32) for p in parts])
    rows = -(-flat.shape[0] // (LANES * PACK_ROWS_ALIGN)) * PACK_ROWS_ALIGN
    return jnp.pad(flat, (0, rows * LANES - flat.shape[0])).reshape(rows, LANES)


def _unpack(packed, shapes):
    flat = packed.reshape(-1)
    out, off = [], 0
    for shp in shapes:
        size = math.prod(shp)
        out.append(flat[off:off + size].reshape(shp))
        off += size
    return out


EVEN_SHARDED = ("w_in", "w_out", "glu_w")
ODD_SHARDED = ("w_in", "pool_w", "w_out")
FAMILY = {(0, "w_in"): "even_w_in", (0, "w_out"): "even_w_out", (0, "glu_w"): "ssm_glu_w",
          (1, "w_in"): "odd_w_in", (1, "pool_w"): "pool_w", (1, "w_out"): "odd_w_out"}


def _sharded_keys(layer):
    return EVEN_SHARDED if layer % 2 == 0 else ODD_SHARDED


def _local_step(x, tgt, small, get_weights, on_w, on_ssm, on_grads, zero=0.0):
    tables = _rope_tables(zero) + (_attention_bias(zero),)
    preps, prep_vjps = [], []
    for i in range(2):
        out, vjp = jax.vjp(_ssm_prep, small["ssm_a_re"][i] + zero, small["ssm_a_im"][i], small["ssm_log_dt"][i],
                           small["ssm_b_re"][i], small["ssm_b_im"][i], small["ssm_c_re"][i], small["ssm_c_im"][i])
        preps.append(out)
        prep_vjps.append(vjp)

    def layer_args(layer, wts):
        i = layer // 2
        pre, post = _row(small["pre_norm"][layer]) + wts.get("token", 0.0), _row(small["post_norm"][layer])
        if layer % 2 == 0:
            return (pre, post, wts["w_in"], wts["late"], _row(small["ssm_glu_b"][i]), _row(small["ssm_d"][i]),
                    preps[i], tables)
        return (pre, post, wts["w_in"], wts["pool_w"], _row(wts["pool_scale"]), wts["w_out"])

    saved, args = [], []
    cur = x
    for layer in range(4):
        after = (cur,) if layer else (cur, tables[0], tables[3], preps[0][1], preps[0][2], preps[1][1], preps[1][2])
        args.append(layer_args(layer, get_weights(layer, after)))
        if layer == 0:
            h = _norm_fwd(cur, args[0][0])
        if layer < 3:
            def tail(xv, yv, post, next_gain=_row(small["pre_norm"][layer + 1])):
                return tuple(_post_fwd(xv, yv, post, next_gain))
        else:
            def tail(xv, yv, post):
                return tuple(_post_fwd_loss(xv, yv, post, tgt))
        cur, h, sv = (_even_fwd if layer % 2 == 0 else _odd_fwd)(cur, h, tail, *args[layer])
        saved.append(sv)
    g, sq = cur, h
    loss = 0.5 * jnp.sum(sq) / D

    lg = [None] * 4
    token = jnp.zeros((), F32)
    for layer in reversed(range(4)):
        largs = list(args[layer])
        largs[1] = largs[1] + token
        hooks = dict(on_w=functools.partial(on_w, layer))
        ssm_grads = []
        if layer % 2 == 0:
            def ssm_hook(cotangents, layer=layer):
                ssm_grads.append(prep_vjps[layer // 2](cotangents))
                return on_ssm(layer, ssm_grads[0])

            hooks["on_ssm"] = ssm_hook
        g, lg[layer] = (_even_bwd if layer % 2 == 0 else _odd_bwd)(g, saved[layer], *largs, **hooks)
        if ssm_grads:
            lg[layer]["ssm"] = ssm_grads[0]
        token = on_grads(layer, lg[layer])
    return loss, g, token


def _to_slots(key, gfull):
    if key == "w_in":
        return gfull
    if key in ("w_out", "glu_w"):
        rr, nn = gfull.shape
        return gfull.reshape(N_DEV, rr // N_DEV, nn)
    assert key == "pool_w"
    gg, rr, nn = gfull.shape
    return gfull.reshape(gg, N_DEV, rr // N_DEV, nn).transpose(1, 0, 2, 3)


def _from_gathered(key, gat):
    if key == "w_in":
        return gat
    if key in ("w_out", "glu_w"):
        _, rr, nn = gat.shape
        return gat.reshape(N_DEV * rr, nn)
    assert key == "pool_w"
    _, gg, rr, nn = gat.shape
    return gat.transpose(1, 0, 2, 3).reshape(gg, N_DEV * rr, nn)


def kernel(x, pre_norm, post_norm, even_w_in, even_w_out, ssm_a_re, ssm_a_im, ssm_log_dt, ssm_b_re, ssm_b_im, ssm_c_re, ssm_c_im, ssm_d, ssm_glu_w, ssm_glu_b, odd_w_in, pool_w, pool_scale, odd_w_out, loss_target, m_pre_norm, m_post_norm, m_even_w_in, m_even_w_out, m_ssm_a_re, m_ssm_a_im, m_ssm_log_dt, m_ssm_b_re, m_ssm_b_im, m_ssm_c_re, m_ssm_c_im, m_ssm_d, m_ssm_glu_w, m_ssm_glu_b, m_odd_w_in, m_pool_w, m_pool_scale, m_odd_w_out, v_pre_norm, v_post_norm, v_even_w_in, v_even_w_out, v_ssm_a_re, v_ssm_a_im, v_ssm_log_dt, v_ssm_b_re, v_ssm_b_im, v_ssm_c_re, v_ssm_c_im, v_ssm_d, v_ssm_glu_w, v_ssm_glu_b, v_odd_w_in, v_pool_w, v_pool_scale, v_odd_w_out):
    w = dict(pre_norm=pre_norm, post_norm=post_norm, even_w_in=even_w_in, even_w_out=even_w_out, ssm_a_re=ssm_a_re,
             ssm_a_im=ssm_a_im, ssm_log_dt=ssm_log_dt, ssm_b_re=ssm_b_re, ssm_b_im=ssm_b_im, ssm_c_re=ssm_c_re,
             ssm_c_im=ssm_c_im, ssm_d=ssm_d, ssm_glu_w=ssm_glu_w, ssm_glu_b=ssm_glu_b, odd_w_in=odd_w_in,
             pool_w=pool_w, pool_scale=pool_scale, odd_w_out=odd_w_out)
    mom = dict(pre_norm=m_pre_norm, post_norm=m_post_norm, even_w_in=m_even_w_in, even_w_out=m_even_w_out,
               ssm_a_re=m_ssm_a_re, ssm_a_im=m_ssm_a_im, ssm_log_dt=m_ssm_log_dt, ssm_b_re=m_ssm_b_re,
               ssm_b_im=m_ssm_b_im, ssm_c_re=m_ssm_c_re, ssm_c_im=m_ssm_c_im, ssm_d=m_ssm_d, ssm_glu_w=m_ssm_glu_w,
               ssm_glu_b=m_ssm_glu_b, odd_w_in=m_odd_w_in, pool_w=m_pool_w, pool_scale=m_pool_scale,
               odd_w_out=m_odd_w_out)
    var = dict(pre_norm=v_pre_norm, post_norm=v_post_norm, even_w_in=v_even_w_in, even_w_out=v_even_w_out,
               ssm_a_re=v_ssm_a_re, ssm_a_im=v_ssm_a_im, ssm_log_dt=v_ssm_log_dt, ssm_b_re=v_ssm_b_re,
               ssm_b_im=v_ssm_b_im, ssm_c_re=v_ssm_c_re, ssm_c_im=v_ssm_c_im, ssm_d=v_ssm_d, ssm_glu_w=v_ssm_glu_w,
               ssm_glu_b=v_ssm_glu_b, odd_w_in=v_odd_w_in, pool_w=v_pool_w, pool_scale=v_pool_scale,
               odd_w_out=v_odd_w_out)
    me = _my_index()
    scale_cols = pool_scale.shape[1]

    def start_gather(tag, layer, keys, after=()):
        i = layer // 2
        shards = [w[FAMILY[(layer % 2, k)]][i].astype(BF16) for k in keys]
        if layer % 2 == 1:
            shards.append(jnp.pad(pool_scale[i][None], ((0, PACK_ROWS_ALIGN - 1), (0, 0))))
        return _xchg_start(f"gather_start_{tag}", shards, True, after)

    gather_started = {0: start_gather("0", 0, EVEN_SHARDED[:1])}
    small = {nm: w[nm] for nm in SMALL_NAMES}

    def get_weights(layer, after):
        keys = EVEN_SHARDED[:1] if layer == 0 else _sharded_keys(layer)
        lands = _xchg_wait(f"gather_wait_{layer}", gather_started[layer], True, after)
        wts = {k: _from_gathered(k, gat) for k, gat in zip(keys, lands)}
        if layer % 2 == 1:
            wts["pool_scale"] = lands[-1][:, 0, :].reshape(N_DEV * scale_cols)
        if layer == 0:
            prev = gather_started["0_late"] = start_gather("0_late", 0, EVEN_SHARDED[1:], after=(lands[0],))
            for later in (1, 2, 3):
                prev = gather_started[later] = start_gather(str(later), later, _sharded_keys(later), after=(prev[4],))
            wts["token"] = sum(gather_started[tag][4][0, 0] for tag in ("0_late", 1, 2, 3))

            def late(after_late):
                late_lands = _xchg_wait("gather_wait_0_late", gather_started["0_late"], True, (after_late,))
                return tuple(_from_gathered(k, gat) for k, gat in zip(EVEN_SHARDED[1:], late_lands))

            wts["late"] = late
        elif layer == 2:
            wts["late"] = lambda after_late: (wts["w_out"], wts["glu_w"])
        return wts

    scatter_started = []

    def on_w(layer, gw):
        keys = tuple(k for k in _sharded_keys(layer) if k in gw)
        started = _xchg_start(f"scatter_start_{layer}_{keys[0]}", [_to_slots(k, gw[k]) for k in keys], False)
        scatter_started.append((layer, keys, started))
        return started[4]

    def wait_scatters(layers, after):
        for layer, keys, started in scatter_started:
            if layer in layers:
                lands = _xchg_wait(f"scatter_wait_{layer}_{keys[0]}", started, False, after)
                for k, land in zip(keys, lands):
                    recv[(layer, k)] = land

    packed_names = ("pre_norm", "post_norm") + SSM_NAMES + ("ssm_d", "ssm_glu_b")
    tails = {nm: (SSM_GROUPS, SSM_STATE * SSM_GROUP) if nm in ("ssm_b_re", "ssm_b_im") else w[nm].shape[1:]
             for nm in packed_names}

    layer_grads = {}
    early_started, mid_started = [], []

    def on_ssm(layer, ssm_grads):
        if layer != 0:
            return None
        mid_started.append(_xchg_start("mid_start", [_pack(list(ssm_grads))], True))
        return mid_started[0][4]

    def on_grads(layer, lg):
        layer_grads[layer] = lg
        zero = jnp.zeros((), F32)
        if layer == 1:
            lgs = layer_grads
            early = ([jnp.concatenate([lgs[l][k] for l in (1, 2, 3)], axis=0) for k in ("pre", "post")]
                     + list(lgs[2]["ssm"]) + [lgs[2]["ssm_d"], lgs[2]["glu_b"],
                                              jnp.concatenate([lgs[1]["pool_scale"], lgs[3]["pool_scale"]], axis=0)])
            early_started.append(_xchg_start("small_start", [_pack(early)], True))
            zero = zero + early_started[0][4][0, 0]
        return zero

    loss_local, grad_x, token = _local_step(x[0], loss_target[0], small, get_weights, on_w, on_ssm, on_grads,
                                            zero=gather_started[0][4][0, 0])

    lg0 = layer_grads[0]
    late_started = _xchg_start("late_start", [_pack([lg0["pre"], lg0["post"], lg0["ssm_d"], lg0["glu_b"],
                                                     loss_local.reshape(1)]) + token], True)

    def adam_family(parity, k):
        nm = FAMILY[(parity, k)]
        shp = w[nm].shape
        cols = shp[-1]
        slot_list = [recv[(parity + 2 * i, k)].reshape(N_DEV, -1, cols) for i in range(2)]
        outs = _adam_layers(w[nm].reshape(2, -1, cols), slot_list, mom[nm].reshape(2, -1, cols),
                            var[nm].reshape(2, -1, cols), name=f"adam_{nm}")
        return [o.reshape(shp) for o in outs]

    recv, res = {}, {}
    wait_scatters((3, 1), (late_started[4],))
    for k in ODD_SHARDED:
        res[FAMILY[(1, k)]] = adam_family(1, k)
    odd_done = tuple(res[FAMILY[(1, k)]][0] for k in ODD_SHARDED)

    (early_slots,) = _xchg_wait("small_wait", early_started[0], True, odd_done)
    (mid_slots,) = _xchg_wait("mid_wait", mid_started[0], True, odd_done)
    early_shapes = [(w[nm].shape[0] - 1,) + tails[nm] for nm in packed_names] + [(2, N_DEV * scale_cols)]
    g_early = _unpack(_sum_slots(early_slots, "sum_small_early"), early_shapes)
    g_mid = _unpack(_sum_slots(mid_slots, "sum_small_mid"), [(1,) + tails[nm] for nm in SSM_NAMES])

    (late_slots,) = _xchg_wait("late_wait", late_started, True, (g_early[0], g_mid[0]))
    wait_scatters((2, 0), (late_slots,))
    for k in EVEN_SHARDED:
        res[FAMILY[(0, k)]] = adam_family(0, k)

    late_names = ("pre_norm", "post_norm", "ssm_d", "ssm_glu_b")
    g_late = _unpack(_sum_slots(late_slots, "sum_small_late"), [(1,) + tails[nm] for nm in late_names] + [(1,)])
    g_first = dict(zip(late_names, g_late))
    g_first.update(zip(SSM_NAMES, g_mid))
    dense = lambda nm, a: a.reshape((a.shape[0],) + tails[nm])
    outs = _adam_params([(dense(nm, w[nm]), dense(nm, mom[nm]), dense(nm, var[nm]), g_first[nm], g_early[j])
                         for j, nm in enumerate(packed_names)], "adam_small")
    for nm, four in zip(packed_names, outs):
        res[nm] = [o.reshape(w[nm].shape) for o in four]
    loss = g_late[-1].reshape(())
    g_scale = lax.dynamic_slice_in_dim(g_early[-1], me * scale_cols, scale_cols, axis=1)
    pad = ((0, PACK_ROWS_ALIGN - 2), (0, 0))
    outs = _adam(jnp.pad(pool_scale, pad), jnp.pad(g_scale, pad)[None], jnp.pad(m_pool_scale, pad),
                 jnp.pad(v_pool_scale, pad), name="adam_pool_scale")
    res["pool_scale"] = [o[:2] for o in outs]

    out = [loss, grad_x[None]]
    for kind in range(4):
        out += [res[nm][kind] for nm in WEIGHT_ORDER]
    return tuple(out)
```

```python
import functools
import math

import jax
import jax.numpy as jnp
from jax import lax
from jax.experimental import pallas as pl
from jax.experimental.pallas import tpu as pltpu

F32 = jnp.float32
BF16 = jnp.bfloat16
SDS = jax.ShapeDtypeStruct

N_DEV = 8
S = 2048
D = 1024
HEAD_DIM = 64
ROT_DIM = 16
ROPE_THETA = 500000.0
ATT_W = 1024
SSM_W = 512
SSM_GROUPS = 32
SSM_GROUP = 16
SSM_STATE = 64
N_CPLX = SSM_GROUPS * SSM_STATE
POOL_W = 2048
POOL_GROUP = 512
EVEN_IN = 5120
EVEN_OUT = 1536
ODD_IN = 4096
RMS_EPS = 1e-6
LANES = 128
VMEM_LIMIT = 48 * 1024 * 1024

ADAM_LR = 0.001
ADAM_B1 = 0.9
ADAM_B2 = 0.999
ADAM_EPS = 1e-08
ADAM_WD = 0.01
ADAM_STEP = 10

MESH_ID = pl.DeviceIdType.MESH
NN = (((1,), (0,)), ((), ()))
NT = (((1,), (1,)), ((), ()))
TN = (((0,), (0,)), ((), ()))
_DN = {"nn": NN, "nt": NT, "tn": TN}


def _cparams(sem):
    return pltpu.CompilerParams(dimension_semantics=sem, vmem_limit_bytes=VMEM_LIMIT)


MM_TILES = (1024, 768, 512)


def _tile(dim):
    return next((t for t in MM_TILES if dim % t == 0), dim)


NT_BLOCKS_PER_STEP = 2


def _mm(a, b, mode, out_dtype, b_blocks=False, out_blocks=False, a_cols=None, after=()):
    if b_blocks:
        nblk, rows, cb = b.shape
        b2_shape = (rows, nblk * cb)
    else:
        b2_shape = b.shape
    a_shape = a.shape if a_cols is None else (a.shape[0], a_cols[1])
    if mode == "nn":
        (m, k), n = a_shape, b2_shape[1]
    elif mode == "nt":
        (m, k), n = a_shape, b2_shape[0]
    else:
        (k, m), n = a_shape, b2_shape[1]
    tm, tn, tk = _tile(m), _tile(n), _tile(k)
    per_step = 1
    if b_blocks and mode == "nn":
        tn = cb
        tm = m
    if b_blocks and mode == "nt":
        per_step = NT_BLOCKS_PER_STEP
        tk = per_step * cb
    if out_blocks:
        tn = n // N_DEV
        tk = k
    nk = k // tk
    a_unit = tm if mode == "tn" else tk
    assert a_cols is None or a_cols[0] % a_unit == 0
    a_off = 0 if a_cols is None else a_cols[0] // a_unit

    def body(a_ref, b_ref, *rest):
        o_ref, acc_ref = rest[-2:]
        kk = pl.program_id(2)
        if per_step == 1:
            part = lax.dot_general(a_ref[...].astype(BF16), b_ref[...].astype(BF16), _DN[mode],
                                   preferred_element_type=F32)
        else:
            part = None
            for blk in range(per_step):
                d = lax.dot_general(a_ref[:, blk * cb:(blk + 1) * cb].astype(BF16), b_ref[blk].astype(BF16), NT,
                                    preferred_element_type=F32)
                part = d if part is None else part + d
        if nk == 1:
            o_ref[...] = part.astype(o_ref.dtype)
            return

        @pl.when(kk == 0)
        def _():
            acc_ref[...] = part

        @pl.when((kk > 0) & (kk < nk - 1))
        def _():
            acc_ref[...] += part

        @pl.when(kk == nk - 1)
        def _():
            o_ref[...] = (acc_ref[...] + part).astype(o_ref.dtype)

    if mode == "nn":
        a_spec = pl.BlockSpec((tm, tk), lambda i, j, kk: (i, a_off + kk))
        b_spec = pl.BlockSpec((tk, tn), lambda i, j, kk: (kk, j))
    elif mode == "nt":
        a_spec = pl.BlockSpec((tm, tk), lambda i, j, kk: (i, a_off + kk))
        b_spec = pl.BlockSpec((tn, tk), lambda i, j, kk: (j, kk))
    else:
        a_spec = pl.BlockSpec((tk, tm), lambda i, j, kk: (kk, a_off + i))
        b_spec = pl.BlockSpec((tk, tn), lambda i, j, kk: (kk, j))
    if b_blocks and mode == "nn":
        b_spec = pl.BlockSpec((None, tk, cb), lambda i, j, kk: (j, kk, 0))
    if b_blocks and mode == "nt":
        b_spec = pl.BlockSpec((per_step, tn, cb), lambda i, j, kk: (kk, j, 0))
    out_spec = pl.BlockSpec((tm, tn), lambda i, j, kk: (i, j))
    out_shape = SDS((m, n), out_dtype)
    if out_blocks:
        out_spec = pl.BlockSpec((None, tm, tn), lambda i, j, kk: (j, i, 0))
        out_shape = SDS((N_DEV, m, tn), out_dtype)
    return pl.pallas_call(
        body, name=f"mm_{mode}_{m}x{k}x{n}",
        grid=(m // tm, n // tn, nk),
        in_specs=[a_spec, b_spec] + [pl.BlockSpec(memory_space=pl.ANY)] * len(after),
        out_specs=out_spec,
        out_shape=out_shape,
        scratch_shapes=[pltpu.VMEM((tm, tn) if nk > 1 else (8, LANES), F32)],
        compiler_params=_cparams(("parallel", "parallel", "arbitrary")),
    )(a, b, *after)


def _gmm(a, b, mode, out_dtype, tm=S):
    ng, gw = POOL_W // POOL_GROUP, POOL_GROUP
    ns = S // tm
    if mode in ("nn", "nt"):
        def body(a_ref, b_ref, o_ref):
            o_ref[...] = lax.dot_general(a_ref[...].astype(BF16), b_ref[...].astype(BF16), _DN[mode],
                                         preferred_element_type=F32).astype(o_ref.dtype)

        return pl.pallas_call(
            body, name=f"gmm_{mode}", grid=(ng, ns),
            in_specs=[pl.BlockSpec((tm, gw), lambda g, i: (i, g)),
                      pl.BlockSpec((None, gw, gw), lambda g, i: (g, 0, 0))],
            out_specs=pl.BlockSpec((tm, gw), lambda g, i: (i, g)),
            out_shape=SDS((S, POOL_W), out_dtype),
            compiler_params=_cparams(("parallel", "parallel")),
        )(a, b)

    def body_tn(a_ref, b_ref, o_ref, acc_ref):
        i = pl.program_id(1)

        @pl.when(i == 0)
        def _():
            acc_ref[...] = jnp.zeros_like(acc_ref)

        acc_ref[...] += lax.dot_general(a_ref[...].astype(BF16), b_ref[...].astype(BF16), TN,
                                        preferred_element_type=F32)

        @pl.when(i == ns - 1)
        def _():
            o_ref[...] = acc_ref[...].astype(o_ref.dtype)

    return pl.pallas_call(
        body_tn, name="gmm_tn", grid=(ng, ns),
        in_specs=[pl.BlockSpec((tm, gw), lambda g, i: (i, g)),
                  pl.BlockSpec((tm, gw), lambda g, i: (i, g))],
        out_specs=pl.BlockSpec((None, gw, gw), lambda g, i: (g, 0, 0)),
        out_shape=SDS((ng, gw, gw), out_dtype),
        scratch_shapes=[pltpu.VMEM((gw, gw), F32)],
        compiler_params=_cparams(("parallel", "arbitrary")),
    )(a, b)


def _rowwise(fn, inputs, out_defs, acc_defs=(), tm=512, name=None, after=()):
    n_in, n_out, n_acc = len(inputs), len(out_defs), len(acc_defs)
    n_after = len(after)
    in_specs, args = [], []
    for arr, width, cb in inputs:
        if arr.shape[0] == 1:
            in_specs.append(pl.BlockSpec((1, width), lambda i, cb=cb: (0, cb)))
        else:
            in_specs.append(pl.BlockSpec((tm, width), lambda i, cb=cb: (i, cb)))
        args.append(arr)
    out_defs = [d if len(d) == 4 else (d[0], d[1], d[0], 0) for d in out_defs]
    out_shape = [SDS((S, ww), dt) for _, dt, ww, _ in out_defs] + [SDS((1, w), F32) for w in acc_defs]
    out_specs = ([pl.BlockSpec((tm, w), lambda i, cb=cb: (i, cb)) for w, _, _, cb in out_defs]
                 + [pl.BlockSpec((1, w), lambda i: (0, 0)) for w in acc_defs])

    def kern(*refs):
        vals = [r[...] for r in refs[:n_in]]
        outs, accs = fn(*vals)
        out_refs = refs[n_in + n_after:]
        for r, v in zip(out_refs[:n_out], outs):
            r[...] = v.astype(r.dtype)
        if n_acc:
            acc_refs = out_refs[n_out:]

            @pl.when(pl.program_id(0) == 0)
            def _():
                for r in acc_refs:
                    r[...] = jnp.zeros_like(r)

            for r, v in zip(acc_refs, accs):
                r[...] += jnp.sum(v, axis=0, keepdims=True)

    res = pl.pallas_call(
        kern, name=name, grid=(S // tm,), in_specs=in_specs + [pl.BlockSpec(memory_space=pl.ANY)] * n_after,
        out_specs=out_specs, out_shape=out_shape, compiler_params=_cparams(("arbitrary",)),
    )(*args, *after)
    return res


def _sigmoid(x):
    return 1.0 / (1.0 + jnp.exp(-x))


def _silu_and_grad(x):
    s = _sigmoid(x)
    return x * s, s * (1.0 + x * (1.0 - s))


_GELU_K = math.sqrt(2.0 / math.pi)
_GELU_C = 0.044715


def _gelu_and_grad(x):
    t = jnp.tanh(_GELU_K * (x + _GELU_C * (x * x * x)))
    cdf = 0.5 * (1.0 + t)
    grad = cdf + 0.5 * x * (1.0 - t * t) * (_GELU_K * (1.0 + 3.0 * _GELU_C * x * x))
    return x * cdf, grad


def _rms(xv, gain):
    r = lax.rsqrt(jnp.mean(xv * xv, axis=-1, keepdims=True) + RMS_EPS)
    return xv * r * gain


def _rms_bwd(dout, xv, gain):
    r = lax.rsqrt(jnp.mean(xv * xv, axis=-1, keepdims=True) + RMS_EPS)
    xhat = xv * r
    dxhat = dout * gain
    dx = r * (dxhat - xhat * jnp.mean(dxhat * xhat, axis=-1, keepdims=True))
    return dx, dout * xhat


def _norm_fwd(x, gain):
    (h,) = _rowwise(lambda xv, g: ((_rms(xv, g),), ()), [(x, D, 0), (gain, D, 0)], [(D, BF16)], name="norm_fwd")
    return h


def _post_fwd(x, y, gain, next_gain):
    def fn(xv, yv, g, gn):
        out = xv + _rms(yv, g)
        return (out, _rms(out, gn)), ()

    return _rowwise(fn, [(x, D, 0), (y, D, 0), (gain, D, 0), (next_gain, D, 0)], [(D, F32), (D, BF16)],
                    name="post_fwd")


def _post_fwd_loss(x, y, gain, tgt):
    def fn(xv, yv, g, tv):
        e = xv + _rms(yv, g) - tv
        return (e * (1.0 / D),), (e * e,)

    return _rowwise(fn, [(x, D, 0), (y, D, 0), (gain, D, 0), (tgt, D, 0)], [(D, F32)], [D], name="post_fwd_loss")


def _post_bwd(g, y, gain):
    def fn(gv, yv, gn):
        dx, dg = _rms_bwd(gv, yv, gn)
        return (dx,), (dg,)

    return _rowwise(fn, [(g, D, 0), (y, D, 0), (gain, D, 0)], [(D, BF16)], [D], name="post_bwd")


def _pre_bwd(g, dh, x, gain):
    def fn(gv, dhv, xv, gn):
        dx, dg = _rms_bwd(dhv, xv, gn)
        return (gv + dx,), (dg,)

    return _rowwise(fn, [(g, D, 0), (dh, D, 0), (x, D, 0), (gain, D, 0)], [(D, F32)], [D], name="pre_bwd")


def _pool(u_arr, col_block, transpose, out_dtype, into=None, tc=256):
    n_t = POOL_W // tc
    per_group = POOL_GROUP // tc

    def body(u_ref, *rest):
        o_ref = rest[-1]
        grp = pl.program_id(0) // per_group
        t = lax.broadcasted_iota(jnp.int32, (S, 1), 0)
        for g in range(POOL_W // POOL_GROUP):
            @pl.when(grp == g)
            def _(g=g):
                xv = u_ref[...]
                cnt = jnp.minimum(t + 1, 2 << g).astype(F32)
                cur = xv / cnt if transpose else xv
                for k in (1, 2, 4, 8)[:g + 1]:
                    if transpose:
                        cur = cur + jnp.where(t < S - k, pltpu.roll(cur, S - k, 0), 0.0)
                    else:
                        cur = cur + jnp.where(t >= k, pltpu.roll(cur, k, 0), 0.0)
                res = cur - xv if transpose else cur / cnt - xv
                o_ref[...] = res.astype(o_ref.dtype)

    in_specs = [pl.BlockSpec((S, tc), lambda c: (0, col_block * n_t + c))]
    args = [u_arr]
    if into is not None:
        in_specs.append(pl.BlockSpec(memory_space=pl.ANY))
        args.append(into)
    return pl.pallas_call(
        body, name="pool_bwd" if transpose else "pool_fwd", grid=(n_t,),
        in_specs=in_specs,
        out_specs=pl.BlockSpec((S, tc), lambda c: (0, c)),
        out_shape=SDS((S, POOL_W) if into is None else into.shape, out_dtype),
        input_output_aliases={} if into is None else {1: 0},
        compiler_params=_cparams(("parallel",)),
    )(*args)


def _rope_tables(zero):
    pos = jnp.arange(S, dtype=jnp.int32).astype(F32) + zero
    inv_freq = ROPE_THETA ** (-jnp.arange(0, ROT_DIM, 2, dtype=F32) / ROT_DIM)
    ang = pos[:, None] * inv_freq[None, :]
    cos8, sin8 = jnp.cos(ang), jnp.sin(ang)
    half = ROT_DIM // 2
    zeros = jnp.zeros((S, HEAD_DIM - ROT_DIM), F32)
    cos = jnp.concatenate([cos8, cos8, jnp.ones((S, HEAD_DIM - ROT_DIM), F32)], axis=1)
    lo = jnp.concatenate([-sin8, jnp.zeros((S, half), F32), zeros], axis=1)
    hi = jnp.concatenate([jnp.zeros((S, half), F32), sin8, zeros], axis=1)
    rep = LANES // HEAD_DIM
    return jnp.tile(cos, (1, rep)), jnp.tile(lo, (1, rep)), jnp.tile(hi, (1, rep))


def _rotate(xv, cos, lo, hi, transpose):
    width = xv.shape[1]
    rep = width // LANES
    wide = lambda tab: jnp.concatenate([tab] * rep, axis=1)
    half = ROT_DIM // 2
    up = pltpu.roll(xv, width - half, 1)
    dn = pltpu.roll(xv, half, 1)
    mixed = up * wide(lo) + dn * wide(hi)
    return xv * wide(cos) - mixed if transpose else xv * wide(cos) + mixed


def _qkv_prep(proj, tables):
    cos, lo, hi = tables

    def fn(x, c, l, h):
        rot = _rotate(x[:, :2 * ATT_W], c, l, h, False)
        return (jnp.concatenate([(rot[:, :ATT_W] * HEAD_DIM ** -0.5).astype(BF16), rot[:, ATT_W:].astype(BF16),
                                 x[:, 2 * ATT_W:].astype(BF16)], axis=1),), ()

    (qkv,) = _rowwise(fn, [(proj, 3 * ATT_W, 0), (cos, LANES, 0), (lo, LANES, 0), (hi, LANES, 0)],
                      [(3 * ATT_W, BF16)], name="qkv_prep")
    return qkv


ATT_T = 512


def _multiplicity(delta):
    ok = delta >= 0
    near = jnp.where(ok & (delta <= 128), 1.0, 0.0)
    mid = jnp.where(ok & (delta <= 512) & ((delta & 3) == 0), 1.0, 0.0)
    far = jnp.where(ok & ((delta & 15) == 0), 1.0, 0.0)
    return near + mid + far


def _attention_bias(zero):
    t = ATT_T
    pos = jnp.arange(t, dtype=jnp.int32) + jnp.asarray(zero).astype(jnp.int32)
    delta = jnp.arange(S // t, dtype=jnp.int32)[:, None, None] * t + pos[None, :, None] - pos[None, None, :]
    mult = _multiplicity(delta)
    return jnp.where(mult > 0.0, jnp.log(jnp.maximum(mult, 1.0)), -1e30).astype(F32)


def _head_split(v, first):
    zero = jnp.zeros_like(v)
    return [jnp.where(first, v, zero), jnp.where(first, zero, v)]


def _flash_fwd(qkv, bias):
    t = ATT_T
    n_hp = ATT_W // LANES

    def body(q_ref, k_ref, v_ref, b_ref, o_ref, lse_ref):
        i = pl.program_id(1)
        first = lax.broadcasted_iota(jnp.int32, (1, LANES), 1) < HEAD_DIM
        qs = _head_split(q_ref[...], first)

        def kv_step(j, carry):
            m0, l0, m1, l1, acc = carry
            off = pl.multiple_of(j * t, t)
            kb = k_ref[pl.ds(off, t), :]
            vs = _head_split(v_ref[pl.ds(off, t), :], first)
            bias_t = b_ref[i - j]
            new = []
            pv = None
            for h, (m_prev, l_prev) in enumerate(((m0, l0), (m1, l1))):
                s = lax.dot_general(qs[h], kb, NT, preferred_element_type=F32) + bias_t
                m_new = jnp.maximum(m_prev, jnp.max(s, axis=1, keepdims=True))
                p = jnp.exp(s - m_new)
                alpha = jnp.exp(m_prev - m_new)
                l_new = alpha * l_prev + jnp.sum(p, axis=1, keepdims=True)
                d = lax.dot_general(p.astype(BF16), vs[h], NN, preferred_element_type=F32)
                pv = d if pv is None else pv + d
                new.append((m_new, l_new, alpha))
            acc = acc * jnp.where(first, new[0][2], new[1][2]) + pv
            return new[0][0], new[0][1], new[1][0], new[1][1], acc

        neg = jnp.full((t, 1), -1e30, F32)
        zero = jnp.zeros((t, 1), F32)
        m0, l0, m1, l1, acc = lax.fori_loop(0, i + 1, kv_step, (neg, zero, neg, zero, jnp.zeros((t, LANES), F32)))
        o_ref[...] = acc * jnp.where(first, 1.0 / l0, 1.0 / l1)
        lse_ref[...] = jnp.where(first, m0 + jnp.log(l0), m1 + jnp.log(l1))

    blk = pl.BlockSpec((t, LANES), lambda hp, i: (i, hp))
    k_full = pl.BlockSpec((S, LANES), lambda hp, i: (0, n_hp + hp))
    v_full = pl.BlockSpec((S, LANES), lambda hp, i: (0, 2 * n_hp + hp))
    return pl.pallas_call(
        body, name="flash_fwd", grid=(n_hp, S // t),
        in_specs=[blk, k_full, v_full, pl.BlockSpec((S // t, t, t), lambda hp, i: (0, 0, 0))], out_specs=[blk, blk],
        out_shape=[SDS((S, ATT_W), F32), SDS((S, ATT_W), F32)],
        compiler_params=_cparams(("parallel", "arbitrary")),
    )(qkv, qkv, qkv, bias)


def _flash_bwd(qkv, o, do, lse, bias, after=()):
    t = ATT_T
    n_hp = ATT_W // LANES
    n_t = S // t

    def body(q_ref, k_ref, v_ref, o_ref, do_ref, lse_ref, b_ref, *rest):
        dq_ref, dk_ref, dv_ref = rest[-3:]
        j = pl.program_id(1)
        first = lax.broadcasted_iota(jnp.int32, (1, LANES), 1) < HEAD_DIM

        @pl.when(j == 0)
        def _():
            dq_ref[...] = jnp.zeros_like(dq_ref)

        kb = k_ref[...]
        vb = v_ref[...]
        ks = _head_split(kb, first)

        def q_step(i, carry):
            dk_acc, dv_acc = carry
            rows = pl.ds(pl.multiple_of(i * t, t), t)
            qs = _head_split(q_ref[rows, :], first)
            dob = do_ref[rows, :]
            prod = dob * o_ref[rows, :]
            d_all = jnp.sum(prod, axis=1, keepdims=True)
            d0 = jnp.sum(jnp.where(first, prod, 0.0), axis=1, keepdims=True)
            lse_b = lse_ref[rows, :]
            lse0 = jnp.max(jnp.where(first, lse_b, -jnp.inf), axis=1, keepdims=True)
            lse1 = jnp.max(jnp.where(first, -jnp.inf, lse_b), axis=1, keepdims=True)
            dos = _head_split(dob.astype(BF16), first)
            bias_t = b_ref[i - j]
            dq_t = jnp.zeros((t, LANES), F32)
            for h, (lse_h, d_h) in enumerate(((lse0, d0), (lse1, d_all - d0))):
                s = lax.dot_general(qs[h], kb, NT, preferred_element_type=F32)
                p = jnp.exp(s + (bias_t - lse_h))
                dp = lax.dot_general(dos[h], vb, NT, preferred_element_type=F32)
                ds = (p * (dp - d_h)).astype(BF16)
                dv_acc = dv_acc + lax.dot_general(p.astype(BF16), dos[h], TN, preferred_element_type=F32)
                dk_acc = dk_acc + lax.dot_general(ds, qs[h], TN, preferred_element_type=F32)
                dq_t = dq_t + lax.dot_general(ds, ks[h], NN, preferred_element_type=F32)
            dq_ref[rows, :] += dq_t
            return dk_acc, dv_acc

        zero = jnp.zeros((t, LANES), F32)
        dk_acc, dv_acc = lax.fori_loop(j, n_t, q_step, (zero, zero))
        dk_ref[...] = dk_acc
        dv_ref[...] = dv_acc

    blk = pl.BlockSpec((t, LANES), lambda hp, j: (j, hp))
    full = pl.BlockSpec((S, LANES), lambda hp, j: (0, hp))
    k_blk = pl.BlockSpec((t, LANES), lambda hp, j: (j, n_hp + hp))
    v_blk = pl.BlockSpec((t, LANES), lambda hp, j: (j, 2 * n_hp + hp))
    return pl.pallas_call(
        body, name="flash_bwd", grid=(n_hp, n_t),
        in_specs=([full, k_blk, v_blk, full, full, full, pl.BlockSpec((n_t, t, t), lambda hp, j: (0, 0, 0))]
                  + [pl.BlockSpec(memory_space=pl.ANY)] * len(after)),
        out_specs=[full, blk, blk],
        out_shape=[SDS((S, ATT_W), F32)] * 3,
        compiler_params=_cparams(("parallel", "arbitrary")),
    )(qkv, qkv, qkv, o, do, lse, bias, *after)


SCAN_T = 256
SCAN_GROUP = 8
SCAN_STEPS = (1, 2, 4)
ST_ROWS = 2 * N_CPLX // LANES
HALF = ST_ROWS // 2


def _scan_tables(lam_t):
    lam = lax.complex(lam_t[:HALF].reshape(N_CPLX), lam_t[HALF:].reshape(N_CPLX))
    pows = [lam]
    for _ in range(SCAN_GROUP - 1):
        pows.append(pows[-1] * lam)
    pows = jnp.stack(pows)
    sub = jnp.arange(SCAN_GROUP)[:, None]
    fwd = [jnp.where(sub >= k, pows[k - 1][None, :], 0.0) for k in SCAN_STEPS] + [pows]
    conj = jnp.conj(pows)
    bwd = [jnp.where(sub <= SCAN_GROUP - 1 - k, conj[k - 1][None, :], 0.0) for k in SCAN_STEPS] + [conj[::-1]]

    def pack(tabs):
        return jnp.stack([jnp.concatenate([jnp.real(t), jnp.imag(t)], axis=1) for t in tabs]).astype(F32)

    return pack(fwd), pack(bwd)


def _cmul_add(xr, xi, lr, li, sr, si):
    return xr + lr * sr - li * si, xi + lr * si + li * sr


def _group_scan(xr, xi, tab_ref, cr, ci, reverse):
    for j, k in enumerate(SCAN_STEPS):
        shift = SCAN_GROUP - k if reverse else k
        xr, xi = _cmul_add(xr, xi, tab_ref[j, :, :N_CPLX], tab_ref[j, :, N_CPLX:],
                           pltpu.roll(xr, shift, 0), pltpu.roll(xi, shift, 0))
    return _cmul_add(xr, xi, tab_ref[3, :, :N_CPLX], tab_ref[3, :, N_CPLX:],
                     jnp.broadcast_to(cr, (SCAN_GROUP, N_CPLX)), jnp.broadcast_to(ci, (SCAN_GROUP, N_CPLX)))


SSM_SUPER = 4
SB_ROWS = SSM_W // SSM_SUPER
SB_COLS = N_CPLX // SSM_SUPER


def _super_blocks():
    return [(slice(b * SB_ROWS, (b + 1) * SB_ROWS), slice(h * SB_COLS, (h + 1) * SB_COLS),
             slice(h * N_CPLX + b * SB_COLS, h * N_CPLX + (b + 1) * SB_COLS))
            for b in range(SSM_SUPER) for h in range(2)]


def _dot16(a, b, dims):
    return lax.dot_general(a.astype(BF16), b.astype(BF16), dims, preferred_element_type=F32)


def _s5_fwd(tab, u_arr, u_cols, w_b, w_ct):
    nc = N_CPLX

    def body(tab_ref, u_ref, wb_ref, wct_ref, st_ref, y_ref, carry, bu_scr):
        @pl.when(pl.program_id(0) == 0)
        def _():
            carry[...] = jnp.zeros_like(carry)

        for rows_b, cols_c, cols_s in _super_blocks():
            bu_scr[:, cols_s] = _dot16(u_ref[:, rows_b], wb_ref[rows_b, cols_c], NN)

        def group(a, c):
            rows = pl.ds(pl.multiple_of(a * SCAN_GROUP, SCAN_GROUP), SCAN_GROUP)
            xr, xi = _group_scan(bu_scr[rows, :nc], bu_scr[rows, nc:], tab_ref, c[0], c[1], False)
            st_ref[rows, :nc] = xr
            st_ref[rows, nc:] = xi
            return xr[SCAN_GROUP - 1:SCAN_GROUP, :], xi[SCAN_GROUP - 1:SCAN_GROUP, :]

        cr, ci = lax.fori_loop(0, SCAN_T // SCAN_GROUP, group, (carry[:, :nc], carry[:, nc:]), unroll=2)
        carry[:, :nc] = cr
        carry[:, nc:] = ci

        for b in range(SSM_SUPER):
            (rows_b, cols_re, st_re), (_, cols_im, st_im) = _super_blocks()[2 * b:2 * b + 2]
            y_ref[:, rows_b] = (_dot16(st_ref[:, st_re], wct_ref[rows_b, cols_re], NT)
                                + _dot16(st_ref[:, st_im], wct_ref[rows_b, cols_im], NT))

    const = lambda shape: pl.BlockSpec(shape, lambda i: (0,) * len(shape))
    return pl.pallas_call(
        body, name="s5_fwd", grid=(S // SCAN_T,),
        in_specs=[const((4, SCAN_GROUP, 2 * nc)), pl.BlockSpec((SCAN_T, SSM_W), lambda i: (i, u_cols[0] // SSM_W)),
                  const((SSM_W, 2 * SB_COLS)), const((SSM_W, 2 * SB_COLS))],
        out_specs=[pl.BlockSpec((SCAN_T, 2 * nc), lambda i: (i, 0)), pl.BlockSpec((SCAN_T, SSM_W), lambda i: (i, 0))],
        out_shape=[SDS((S, 2 * nc), F32), SDS((S, SSM_W), F32)],
        scratch_shapes=[pltpu.VMEM((1, 2 * nc), F32), pltpu.VMEM((SCAN_T, 2 * nc), F32)],
        compiler_params=_cparams(("arbitrary",)),
    )(tab, u_arr, w_b, w_ct)


def _s5_bwd(tab, dy, states, u_arr, u_cols, w_b, w_ct):
    n_blk = S // SCAN_T
    nc = N_CPLX

    def body(tab_ref, dy_ref, x_ref, u_ref, wb_ref, wct_ref, du_ref, dlam_ref, dwb_ref, dwct_ref,
             carry, acc, d_scr, g_ref):
        i = pl.program_id(0)

        @pl.when(i == 0)
        def _():
            carry[...] = jnp.zeros_like(carry)
            acc[...] = jnp.zeros_like(acc)
            dwb_ref[...] = jnp.zeros_like(dwb_ref)
            dwct_ref[...] = jnp.zeros_like(dwct_ref)

        for rows_b, cols_c, cols_s in _super_blocks():
            d_scr[:, cols_s] = _dot16(dy_ref[:, rows_b], wct_ref[rows_b, cols_c], NN)

        last_row = lax.broadcasted_iota(jnp.int32, (SCAN_GROUP, 1), 0) == SCAN_GROUP - 1
        d_ref = d_scr

        def group(j, c):
            cr, ci = c
            rows = pl.ds(pl.multiple_of((SCAN_T // SCAN_GROUP - 1 - j) * SCAN_GROUP, SCAN_GROUP), SCAN_GROUP)
            gr, gi = _group_scan(d_ref[rows, :nc], d_ref[rows, nc:], tab_ref, cr, ci, True)
            g_ref[rows, :nc] = gr
            g_ref[rows, nc:] = gi
            nr = jnp.where(last_row, jnp.broadcast_to(cr, (SCAN_GROUP, nc)), pltpu.roll(gr, SCAN_GROUP - 1, 0))
            ni = jnp.where(last_row, jnp.broadcast_to(ci, (SCAN_GROUP, nc)), pltpu.roll(gi, SCAN_GROUP - 1, 0))
            sr, si = x_ref[rows, :nc], x_ref[rows, nc:]
            acc[:, :nc] += nr * sr + ni * si
            acc[:, nc:] += ni * sr - nr * si
            return gr[0:1, :], gi[0:1, :]

        cr, ci = lax.fori_loop(0, SCAN_T // SCAN_GROUP, group, (carry[:, :nc], carry[:, nc:]), unroll=2)
        carry[:, :nc] = cr
        carry[:, nc:] = ci

        for b in range(SSM_SUPER):
            (rows_b, cols_re, st_re), (_, cols_im, st_im) = _super_blocks()[2 * b:2 * b + 2]
            du_ref[:, rows_b] = (_dot16(g_ref[:, st_re], wb_ref[rows_b, cols_re], NT)
                                 + _dot16(g_ref[:, st_im], wb_ref[rows_b, cols_im], NT))
            for cols_c, cols_s in ((cols_re, st_re), (cols_im, st_im)):
                dwb_ref[rows_b, cols_c] += _dot16(u_ref[:, rows_b], g_ref[:, cols_s], TN)
                dwct_ref[rows_b, cols_c] += _dot16(dy_ref[:, rows_b], x_ref[:, cols_s], TN)

        @pl.when(i == n_blk - 1)
        def _():
            dlam_ref[...] = jnp.sum(acc[...], axis=0, keepdims=True)

    const = lambda shape: pl.BlockSpec(shape, lambda i: (0,) * len(shape))
    rows = lambda width, col_block=0: pl.BlockSpec((SCAN_T, width), lambda i: (n_blk - 1 - i, col_block))
    maps = const((SSM_W, 2 * SB_COLS))
    return pl.pallas_call(
        body, name="s5_bwd", grid=(n_blk,),
        in_specs=[const((4, SCAN_GROUP, 2 * nc)), rows(SSM_W), rows(2 * nc), rows(SSM_W, u_cols[0] // SSM_W), maps, maps],
        out_specs=[rows(SSM_W), const((1, 2 * nc)), maps, maps],
        out_shape=[SDS((S, SSM_W), F32), SDS((1, 2 * nc), F32), SDS((SSM_W, 2 * SB_COLS), F32),
                   SDS((SSM_W, 2 * SB_COLS), F32)],
        scratch_shapes=[pltpu.VMEM((1, 2 * nc), F32), pltpu.VMEM((SCAN_GROUP, 2 * nc), F32),
                        pltpu.VMEM((SCAN_T, 2 * nc), F32), pltpu.VMEM((SCAN_T, 2 * nc), F32)],
        compiler_params=_cparams(("arbitrary",)),
    )(tab, dy, states, u_arr, w_b, w_ct)


def _ssm_prep(a_re, a_im, log_dt, b_re, b_im, c_re, c_im):
    lam = lax.complex(a_re, a_im)
    dt = jnp.exp(log_dt)[:, None]
    lam_bar = jnp.exp(lam * dt)
    b_bar = ((lam_bar - 1.0) / lam)[..., None] * lax.complex(b_re, b_im)
    lam_t = jnp.concatenate([jnp.real(lam_bar).reshape(HALF, LANES), jnp.imag(lam_bar).reshape(HALF, LANES)], axis=0)
    groups_per_super = SSM_GROUPS // SSM_SUPER
    on_diag = ((lax.broadcasted_iota(jnp.int32, (SSM_W, SB_COLS), 0) // SSM_GROUP) % groups_per_super
               == lax.broadcasted_iota(jnp.int32, (SSM_W, SB_COLS), 1) // SSM_STATE)

    def compact(m):
        return jnp.where(on_diag, jnp.tile(m.reshape(SSM_W, SSM_STATE), (1, groups_per_super)), 0.0)

    w_b = jnp.concatenate([compact(jnp.real(b_bar).transpose(0, 2, 1)),
                           compact(jnp.imag(b_bar).transpose(0, 2, 1))], axis=1)
    w_ct = jnp.concatenate([compact(c_re), -compact(c_im)], axis=1)
    return lam_t, w_b, w_ct


U_SSM_COLS = (4 * ATT_W, SSM_W)


def _row(v):
    return v.reshape(1, -1)


def _even_fwd(x, h, tail, pre, post, w_in, late_w, glu_b, ssm_d, prep, tables):
    lam_t, w_b, w_ct = prep
    proj = _mm(h, w_in, "nn", F32, b_blocks=True)
    qkv = _qkv_prep(proj, tables[:3])
    w_out, glu_w = late_w(qkv)
    att, lse = _flash_fwd(qkv, tables[3])
    scan_fwd_tab, scan_bwd_tab = _scan_tables(lam_t)
    states, y = _s5_fwd(scan_fwd_tab, proj, U_SSM_COLS, w_b, w_ct)

    def act1(yv, uv, dv):
        return (_gelu_and_grad(yv + dv * uv)[0],), ()

    (z1,) = _rowwise(act1, [(y, SSM_W, 0), (proj, SSM_W, 8), (ssm_d, SSM_W, 0)], [(SSM_W, F32)], name="ssm_act_fwd")
    lin = _mm(z1, glu_w, "nn", F32)

    def gate(att_v, ga, gs, z1v, linv, bv):
        ssm_out = z1v * _sigmoid(linv + bv)
        return (jnp.concatenate([att_v * _silu_and_grad(ga)[0], ssm_out * _silu_and_grad(gs)[0]], axis=1),), ()

    (merged,) = _rowwise(gate, [(att, ATT_W, 0), (proj, ATT_W, 3), (proj, SSM_W, 9), (z1, SSM_W, 0),
                                (lin, SSM_W, 0), (glu_b, SSM_W, 0)], [(EVEN_OUT, BF16)], name="even_gate_fwd")
    yout = _mm(merged, w_out, "nn", F32)
    saved = (x, h, proj, qkv, att, lse, states, y, z1, lin, merged, yout, w_out, glu_w, scan_bwd_tab)
    return tail(x, yout, post) + (saved,)


def _even_bwd(g, saved, pre, post, w_in, late_w, glu_b, ssm_d, prep, tables, on_w, on_ssm):
    x, h, proj, qkv, att, lse, states, y, z1, lin, merged, yout, w_out, glu_w, scan_bwd_tab = saved
    lam_t, w_b, w_ct = prep
    dyout, dpost = _post_bwd(g, yout, post)
    dmerged = _mm(dyout, w_out, "nt", F32)
    dw_out = _mm(merged, dyout, "tn", BF16)

    def gate_bwd(dm_a, dm_s, att_v, ga, gs, z1v, linv, bv):
        sa, dsa = _silu_and_grad(ga)
        ss, dss = _silu_and_grad(gs)
        sig = _sigmoid(linv + bv)
        ssm_out = z1v * sig
        dssm = dm_s * ss
        dlin = dssm * z1v * sig * (1.0 - sig)
        return (dm_a * sa, dm_a * att_v * dsa, dm_s * ssm_out * dss, dssm * sig, dlin), (dlin,)

    datt, dg_att, dg_ssm, dz1a, dlin, dglu_b = _rowwise(
        gate_bwd, [(dmerged, ATT_W, 0), (dmerged, SSM_W, 2), (att, ATT_W, 0), (proj, ATT_W, 3), (proj, SSM_W, 9),
                   (z1, SSM_W, 0), (lin, SSM_W, 0), (glu_b, SSM_W, 0)],
        [(ATT_W, F32), (ATT_W, BF16), (SSM_W, BF16), (SSM_W, F32), (SSM_W, BF16)], [SSM_W], name="even_gate_bwd")
    dz1b = _mm(dlin, glu_w, "nt", F32)
    dglu_w = _mm(z1, dlin, "tn", BF16)

    def act1_bwd(da, db, yv, uv, dv):
        dpre = (da + db) * _gelu_and_grad(yv + dv * uv)[1]
        return (dpre, dpre * dv), (dpre * uv,)

    sent_late_w = on_w(dict(w_out=dw_out, glu_w=dglu_w))
    dy, du_direct, dd = _rowwise(act1_bwd, [(dz1a, SSM_W, 0), (dz1b, SSM_W, 0), (y, SSM_W, 0), (proj, SSM_W, 8),
                                            (ssm_d, SSM_W, 0)], [(SSM_W, BF16), (SSM_W, F32)], [SSM_W],
                                 name="ssm_act_bwd", after=(sent_late_w,))
    du_state, dlam_row, dw_b, dw_ct = _s5_bwd(scan_bwd_tab, dy, states, proj, U_SSM_COLS, w_b, w_ct)
    dlam = jnp.concatenate([dlam_row[0, :N_CPLX].reshape(HALF, LANES), dlam_row[0, N_CPLX:].reshape(HALF, LANES)],
                           axis=0)
    sent_ssm = on_ssm((dlam, dw_b, dw_ct))
    dq, dk, dv = _flash_bwd(qkv, att, datt, lse, tables[3], after=() if sent_ssm is None else (sent_ssm,))

    def assemble(dqv, dkv, dvv, dga, dua, dub, dgs, c, l, h):
        rot = _rotate(jnp.concatenate([dqv, dkv], axis=1), c, l, h, True)
        return (jnp.concatenate([(rot[:, :ATT_W] * HEAD_DIM ** -0.5).astype(BF16), rot[:, ATT_W:].astype(BF16),
                                 dvv.astype(BF16), dga, (dua + dub).astype(BF16), dgs], axis=1),), ()

    (dproj,) = _rowwise(assemble, [(dq, ATT_W, 0), (dk, ATT_W, 0), (dv, ATT_W, 0), (dg_att, ATT_W, 0),
                                   (du_state, SSM_W, 0), (du_direct, SSM_W, 0), (dg_ssm, SSM_W, 0),
                                   (tables[0], LANES, 0), (tables[1], LANES, 0), (tables[2], LANES, 0)],
                        [(EVEN_IN, BF16)], name="dproj_assemble")
    dw_in = _mm(h, dproj, "tn", BF16, out_blocks=True)
    sent = on_w(dict(w_in=dw_in))
    dh = _mm(dproj, w_in, "nt", F32, b_blocks=True, after=(sent,))
    g_prev, dpre = _pre_bwd(g, dh, x, pre)
    return g_prev, dict(pre=dpre, post=dpost, glu_b=dglu_b, ssm_d=dd)


def _odd_fwd(x, h, tail, pre, post, w_in, pool_w, pool_scale, w_out):
    proj = _mm(h, w_in, "nn", F32, b_blocks=True)
    mixed = _pool(proj, 0, False, BF16)
    ylin = _gmm(mixed, pool_w, "nn", F32)

    def gate(yl, gt, sc):
        return (yl * sc * _silu_and_grad(gt)[0],), ()

    (z,) = _rowwise(gate, [(ylin, POOL_W, 0), (proj, POOL_W, 1), (pool_scale, POOL_W, 0)], [(POOL_W, BF16)],
                    name="odd_gate_fwd")
    yout = _mm(z, w_out, "nn", F32)
    return tail(x, yout, post) + ((x, h, proj, mixed, ylin, z, yout),)


def _odd_bwd(g, saved, pre, post, w_in, pool_w, pool_scale, w_out, on_w):
    x, h, proj, mixed, ylin, z, yout = saved
    dyout, dpost = _post_bwd(g, yout, post)
    dz = _mm(dyout, w_out, "nt", F32)
    dw_out = _mm(z, dyout, "tn", BF16)

    def gate_bwd(dzv, yl, gt, sc):
        sg, dsg = _silu_and_grad(gt)
        tt = dzv * sg
        return (tt * sc, dzv * yl * sc * dsg), (tt * yl,)

    dylin, dproj_gate, dscale = _rowwise(gate_bwd, [(dz, POOL_W, 0), (ylin, POOL_W, 0), (proj, POOL_W, 1),
                                                    (pool_scale, POOL_W, 0)],
                                         [(POOL_W, BF16), (POOL_W, BF16, ODD_IN, 1)], [POOL_W], name="odd_gate_bwd")
    dmixed = _gmm(dylin, pool_w, "nt", F32)
    dpool_w = _gmm(mixed, dylin, "tn", BF16)
    dproj = _pool(dmixed, 0, True, BF16, into=dproj_gate)
    dw_in = _mm(h, dproj, "tn", BF16, out_blocks=True)
    sent = on_w(dict(w_in=dw_in, w_out=dw_out, pool_w=dpool_w))
    dh = _mm(dproj, w_in, "nt", F32, b_blocks=True, after=(sent,))
    g_prev, dpre = _pre_bwd(g, dh, x, pre)
    return g_prev, dict(pre=dpre, post=dpost, pool_scale=dscale)


def _my_index():
    return 4 * lax.axis_index("x") + 2 * lax.axis_index("y") + lax.axis_index("c")


HBM_SPEC = pl.BlockSpec(memory_space=pltpu.HBM)
SEM_SPEC = pl.BlockSpec(memory_space=pltpu.SEMAPHORE)
SPLIT_EFFECT = pltpu.SideEffectType.DATAFLOW_SIDE_EFFECTING


def _device_of(j):
    return (j // 4, (j // 2) % 2, j % 2)


def _split_copy(srcs, lands, send_sems, recv_sems, gather, i, j, dst_slot, recv_slot):
    return pltpu.make_async_remote_copy(
        src_ref=srcs[i] if gather else srcs[i].at[j], dst_ref=lands[i].at[dst_slot],
        send_sem=send_sems.at[i * N_DEV + j], recv_sem=recv_sems.at[i * N_DEV + recv_slot],
        device_id=_device_of(j), device_id_type=MESH_ID)


def _own_copy(srcs, lands, send_sems, gather, i, me):
    return pltpu.make_async_copy(srcs[i] if gather else srcs[i].at[me], lands[i].at[me], send_sems.at[i * N_DEV + me])


def _xchg_start(name, srcs, gather, after=()):
    n = len(srcs)
    n_in = n + len(after)

    def body(*refs):
        src_refs = refs[:n]
        send_sems, recv_sems, token = refs[n_in], refs[n_in + 1], refs[-1]
        land_refs = refs[n_in + 2 + n:n_in + 2 + 2 * n]
        me = _my_index()
        for j in range(N_DEV):
            @pl.when(me != j)
            def _(j=j):
                for i in range(n):
                    _split_copy(src_refs, land_refs, send_sems, recv_sems, gather, i, j, me, me).start()
        for i in range(n):
            _own_copy(src_refs, land_refs, send_sems, gather, i, me).start()
        token[...] = jnp.zeros_like(token)

    land_shapes = [((N_DEV,) + a.shape) if gather else a.shape for a in srcs]
    thru = ([pltpu.HBM(a.shape, a.dtype) for a in srcs] + [pltpu.HBM(s, a.dtype) for s, a in zip(land_shapes, srcs)])
    res = pl.pallas_call(
        body, name=name,
        out_shape=(pltpu.SemaphoreType.DMA((n * N_DEV,)), pltpu.SemaphoreType.DMA((n * N_DEV,)), *thru,
                   SDS((8, LANES), F32)),
        in_specs=[HBM_SPEC] * n + [pl.BlockSpec(memory_space=pl.ANY)] * len(after),
        out_specs=(SEM_SPEC, SEM_SPEC, *([HBM_SPEC] * (2 * n)), pl.BlockSpec(memory_space=pltpu.VMEM)),
        input_output_aliases={i: 2 + i for i in range(n)},
        compiler_params=pltpu.CompilerParams(has_side_effects=SPLIT_EFFECT),
    )(*[pltpu.with_memory_space_constraint(a, pltpu.HBM) for a in srcs], *after)
    return res[0], res[1], list(res[2:2 + n]), list(res[2 + n:2 + 2 * n]), res[-1]


def _xchg_wait(name, started, gather, after):
    send_sems, recv_sems, srcs, lands, _ = started
    n = len(srcs)

    def body(*refs):
        src_refs, land_refs = refs[:n], refs[n:2 * n]
        send_r, recv_r = refs[2 * n], refs[2 * n + 1]
        me = _my_index()
        for j in range(N_DEV):
            @pl.when(me != j)
            def _(j=j):
                for i in range(n):
                    _split_copy(src_refs, land_refs, send_r, recv_r, gather, i, j, me, me).wait_send()
                    _split_copy(src_refs, land_refs, send_r, recv_r, gather, i, j, j, j).wait_recv()
        for i in range(n):
            _own_copy(src_refs, land_refs, send_r, gather, i, me).wait()

    thru = [pltpu.HBM(a.shape, a.dtype) for a in list(srcs) + list(lands)]
    res = pl.pallas_call(
        body, name=name, out_shape=tuple(thru),
        in_specs=[HBM_SPEC] * (2 * n) + [SEM_SPEC, SEM_SPEC] + [pl.BlockSpec(memory_space=pl.ANY)] * len(after),
        out_specs=tuple([HBM_SPEC] * (2 * n)),
        input_output_aliases={i: i for i in range(2 * n)},
        compiler_params=pltpu.CompilerParams(has_side_effects=SPLIT_EFFECT),
    )(*srcs, *lands, send_sems, recv_sems, *after)
    return list(res[n:])


def _adam_layers(w, slot_list, m, v, name):
    n_l, r, c = w.shape
    ns = slot_list[0].shape[0]
    tr = r
    while tr * c * 4 > (1 << 20) and tr % 16 == 0:
        tr //= 2
    assert r % tr == 0 and len(slot_list) == n_l

    def body(*refs):
        w_ref, slot_refs = refs[0], refs[1:1 + n_l]
        m_ref, v_ref, go_ref, d_ref, mo_ref, vo_ref = refs[1 + n_l:]
        layer = pl.program_id(0)
        g = None
        for l, g_ref in enumerate(slot_refs):
            gl = g_ref[0].astype(F32)
            for s in range(1, ns):
                gl = gl + g_ref[s].astype(F32)
            g = gl if g is None else jnp.where(layer == l, gl, g)
        mn = ADAM_B1 * m_ref[...] + (1.0 - ADAM_B1) * g
        vn = ADAM_B2 * v_ref[...] + (1.0 - ADAM_B2) * (g * g)
        m_hat = mn / (1.0 - ADAM_B1 ** ADAM_STEP)
        v_hat = vn / (1.0 - ADAM_B2 ** ADAM_STEP)
        go_ref[...] = g
        d_ref[...] = -ADAM_LR * (m_hat / (jnp.sqrt(v_hat) + ADAM_EPS) + ADAM_WD * w_ref[...])
        mo_ref[...] = mn
        vo_ref[...] = vn

    blk = pl.BlockSpec((None, tr, c), lambda l, i: (l, i, 0))
    slot_specs = [pl.BlockSpec((ns, tr, c), lambda l, i, k=k: (0, jnp.where(l == k, i, 0), 0)) for k in range(n_l)]
    return pl.pallas_call(
        body, name=name, grid=(n_l, r // tr),
        in_specs=[blk] + slot_specs + [blk, blk],
        out_specs=[blk] * 4, out_shape=[SDS((n_l, r, c), F32)] * 4,
        compiler_params=_cparams(("arbitrary", "arbitrary")),
    )(*[pltpu.with_memory_space_constraint(a, pltpu.HBM) for a in (w, *slot_list, m, v)])


def _adam(w, gslots, m, v, name):
    r, c = w.shape
    ns = gslots.shape[0]
    tr = r
    while tr * c * 4 > (1 << 20) and tr % 16 == 0:
        tr //= 2
    assert r % tr == 0

    def body(w_ref, g_ref, m_ref, v_ref, go_ref, d_ref, mo_ref, vo_ref):
        g = g_ref[0].astype(F32)
        for s in range(1, ns):
            g = g + g_ref[s].astype(F32)
        wv = w_ref[...]
        mn = ADAM_B1 * m_ref[...] + (1.0 - ADAM_B1) * g
        vn = ADAM_B2 * v_ref[...] + (1.0 - ADAM_B2) * (g * g)
        m_hat = mn / (1.0 - ADAM_B1 ** ADAM_STEP)
        v_hat = vn / (1.0 - ADAM_B2 ** ADAM_STEP)
        go_ref[...] = g
        d_ref[...] = -ADAM_LR * (m_hat / (jnp.sqrt(v_hat) + ADAM_EPS) + ADAM_WD * wv)
        mo_ref[...] = mn
        vo_ref[...] = vn

    blk = pl.BlockSpec((tr, c), lambda i: (i, 0))
    return pl.pallas_call(
        body, name=name, grid=(r // tr,),
        in_specs=[blk, pl.BlockSpec((ns, tr, c), lambda i: (0, i, 0)), blk, blk],
        out_specs=[blk] * 4, out_shape=[SDS((r, c), F32)] * 4,
        compiler_params=_cparams(("parallel",)),
    )(w, gslots, m, v)


def _sum_slots(slots, name):
    ns, r, c = slots.shape

    def body(g_ref, o_ref):
        g = g_ref[0]
        for s in range(1, ns):
            g = g + g_ref[s]
        o_ref[...] = g

    return pl.pallas_call(
        body, name=name, grid=(1,),
        in_specs=[pl.BlockSpec((ns, r, c), lambda i: (0, 0, 0))], out_specs=pl.BlockSpec((r, c), lambda i: (0, 0)),
        out_shape=SDS((r, c), F32), compiler_params=_cparams(("arbitrary",)),
    )(slots)


def _adam_params(params, name):
    n = len(params)

    def body(*refs):
        ins, outs = refs[:5 * n], refs[5 * n:]
        for p in range(n):
            w_ref, m_ref, v_ref, g_first, g_rest = ins[5 * p:5 * p + 5]
            go_ref, d_ref, mo_ref, vo_ref = outs[4 * p:4 * p + 4]
            for part, g_ref in ((slice(0, 1), g_first), (slice(1, w_ref.shape[0]), g_rest)):
                g = g_ref[...]
                mn = ADAM_B1 * m_ref[part] + (1.0 - ADAM_B1) * g
                vn = ADAM_B2 * v_ref[part] + (1.0 - ADAM_B2) * (g * g)
                m_hat = mn / (1.0 - ADAM_B1 ** ADAM_STEP)
                v_hat = vn / (1.0 - ADAM_B2 ** ADAM_STEP)
                go_ref[part] = g
                d_ref[part] = -ADAM_LR * (m_hat / (jnp.sqrt(v_hat) + ADAM_EPS) + ADAM_WD * w_ref[part])
                mo_ref[part] = mn
                vo_ref[part] = vn

    def whole(a):
        return pl.BlockSpec(a.shape, lambda i, nd=a.ndim: (0,) * nd)

    flat = [pltpu.with_memory_space_constraint(a, pltpu.HBM) for prm in params for a in prm]
    outs = pl.pallas_call(
        body, name=name, grid=(1,),
        in_specs=[whole(a) for a in flat],
        out_specs=[whole(prm[0]) for prm in params for _ in range(4)],
        out_shape=[SDS(prm[0].shape, F32) for prm in params for _ in range(4)],
        compiler_params=_cparams(("arbitrary",)),
    )(*flat)
    return [outs[4 * p:4 * p + 4] for p in range(n)]


SMALL_NAMES = ("pre_norm", "post_norm", "ssm_a_re", "ssm_a_im", "ssm_log_dt", "ssm_b_re", "ssm_b_im", "ssm_c_re",
               "ssm_c_im", "ssm_d", "ssm_glu_b")
SSM_NAMES = ("ssm_a_re", "ssm_a_im", "ssm_log_dt", "ssm_b_re", "ssm_b_im", "ssm_c_re", "ssm_c_im")
SHARDED_NAMES = ("even_w_in", "even_w_out", "ssm_glu_w", "odd_w_in", "pool_w", "odd_w_out")
WEIGHT_ORDER = ("pre_norm", "post_norm", "even_w_in", "even_w_out", "ssm_a_re", "ssm_a_im", "ssm_log_dt", "ssm_b_re",
                "ssm_b_im", "ssm_c_re", "ssm_c_im", "ssm_d", "ssm_glu_w", "ssm_glu_b", "odd_w_in", "pool_w",
                "pool_scale", "odd_w_out")
PACK_ROWS_ALIGN = 8


def _pack(parts):
    flat = jnp.concatenate([p.reshape(-1).astype(F32) for p in parts])
    rows = -(-flat.shape[0] // (LANES * PACK_ROWS_ALIGN)) * PACK_ROWS_ALIGN
    return jnp.pad(flat, (0, rows * LANES - flat.shape[0])).reshape(rows, LANES)


def _unpack(packed, shapes):
    flat = packed.reshape(-1)
    out, off = [], 0
    for shp in shapes:
        size = math.prod(shp)
        out.append(flat[off:off + size].reshape(shp))
        off += size
    return out


EVEN_SHARDED = ("w_in", "w_out", "glu_w")
ODD_SHARDED = ("w_in", "pool_w", "w_out")
FAMILY = {(0, "w_in"): "even_w_in", (0, "w_out"): "even_w_out", (0, "glu_w"): "ssm_glu_w",
          (1, "w_in"): "odd_w_in", (1, "pool_w"): "pool_w", (1, "w_out"): "odd_w_out"}


def _sharded_keys(layer):
    return EVEN_SHARDED if layer % 2 == 0 else ODD_SHARDED


def _local_step(x, tgt, small, get_weights, on_w, on_ssm, on_grads, zero=0.0):
    tables = _rope_tables(zero) + (_attention_bias(zero),)
    preps, prep_vjps = [], []
    for i in range(2):
        out, vjp = jax.vjp(_ssm_prep, small["ssm_a_re"][i] + zero, small["ssm_a_im"][i], small["ssm_log_dt"][i],
                           small["ssm_b_re"][i], small["ssm_b_im"][i], small["ssm_c_re"][i], small["ssm_c_im"][i])
        preps.append(out)
        prep_vjps.append(vjp)

    def layer_args(layer, wts):
        i = layer // 2
        pre, post = _row(small["pre_norm"][layer]) + wts.get("token", 0.0), _row(small["post_norm"][layer])
        if layer % 2 == 0:
            return (pre, post, wts["w_in"], wts["late"], _row(small["ssm_glu_b"][i]), _row(small["ssm_d"][i]),
                    preps[i], tables)
        return (pre, post, wts["w_in"], wts["pool_w"], _row(wts["pool_scale"]), wts["w_out"])

    saved, args = [], []
    cur = x
    for layer in range(4):
        after = (cur,) if layer else (cur, tables[0], tables[3], preps[0][1], preps[0][2], preps[1][1], preps[1][2])
        args.append(layer_args(layer, get_weights(layer, after)))
        if layer == 0:
            h = _norm_fwd(cur, args[0][0])
        if layer < 3:
            def tail(xv, yv, post, next_gain=_row(small["pre_norm"][layer + 1])):
                return tuple(_post_fwd(xv, yv, post, next_gain))
        else:
            def tail(xv, yv, post):
                return tuple(_post_fwd_loss(xv, yv, post, tgt))
        cur, h, sv = (_even_fwd if layer % 2 == 0 else _odd_fwd)(cur, h, tail, *args[layer])
        saved.append(sv)
    g, sq = cur, h
    loss = 0.5 * jnp.sum(sq) / D

    lg = [None] * 4
    token = jnp.zeros((), F32)
    for layer in reversed(range(4)):
        largs = list(args[layer])
        largs[1] = largs[1] + token
        hooks = dict(on_w=functools.partial(on_w, layer))
        ssm_grads = []
        if layer % 2 == 0:
            def ssm_hook(cotangents, layer=layer):
                ssm_grads.append(prep_vjps[layer // 2](cotangents))
                return on_ssm(layer, ssm_grads[0])

            hooks["on_ssm"] = ssm_hook
        g, lg[layer] = (_even_bwd if layer % 2 == 0 else _odd_bwd)(g, saved[layer], *largs, **hooks)
        if ssm_grads:
            lg[layer]["ssm"] = ssm_grads[0]
        token = on_grads(layer, lg[layer])
    return loss, g, token


def _to_slots(key, gfull):
    if key == "w_in":
        return gfull
    if key in ("w_out", "glu_w"):
        rr, nn = gfull.shape
        return gfull.reshape(N_DEV, rr // N_DEV, nn)
    assert key == "pool_w"
    gg, rr, nn = gfull.shape
    return gfull.reshape(gg, N_DEV, rr // N_DEV, nn).transpose(1, 0, 2, 3)


def _from_gathered(key, gat):
    if key == "w_in":
        return gat
    if key in ("w_out", "glu_w"):
        _, rr, nn = gat.shape
        return gat.reshape(N_DEV * rr, nn)
    assert key == "pool_w"
    _, gg, rr, nn = gat.shape
    return gat.transpose(1, 0, 2, 3).reshape(gg, N_DEV * rr, nn)


def kernel(x, pre_norm, post_norm, even_w_in, even_w_out, ssm_a_re, ssm_a_im, ssm_log_dt, ssm_b_re, ssm_b_im, ssm_c_re, ssm_c_im, ssm_d, ssm_glu_w, ssm_glu_b, odd_w_in, pool_w, pool_scale, odd_w_out, loss_target, m_pre_norm, m_post_norm, m_even_w_in, m_even_w_out, m_ssm_a_re, m_ssm_a_im, m_ssm_log_dt, m_ssm_b_re, m_ssm_b_im, m_ssm_c_re, m_ssm_c_im, m_ssm_d, m_ssm_glu_w, m_ssm_glu_b, m_odd_w_in, m_pool_w, m_pool_scale, m_odd_w_out, v_pre_norm, v_post_norm, v_even_w_in, v_even_w_out, v_ssm_a_re, v_ssm_a_im, v_ssm_log_dt, v_ssm_b_re, v_ssm_b_im, v_ssm_c_re, v_ssm_c_im, v_ssm_d, v_ssm_glu_w, v_ssm_glu_b, v_odd_w_in, v_pool_w, v_pool_scale, v_odd_w_out):
    w = dict(pre_norm=pre_norm, post_norm=post_norm, even_w_in=even_w_in, even_w_out=even_w_out, ssm_a_re=ssm_a_re,
             ssm_a_im=ssm_a_im, ssm_log_dt=ssm_log_dt, ssm_b_re=ssm_b_re, ssm_b_im=ssm_b_im, ssm_c_re=ssm_c_re,
             ssm_c_im=ssm_c_im, ssm_d=ssm_d, ssm_glu_w=ssm_glu_w, ssm_glu_b=ssm_glu_b, odd_w_in=odd_w_in,
             pool_w=pool_w, pool_scale=pool_scale, odd_w_out=odd_w_out)
    mom = dict(pre_norm=m_pre_norm, post_norm=m_post_norm, even_w_in=m_even_w_in, even_w_out=m_even_w_out,
               ssm_a_re=m_ssm_a_re, ssm_a_im=m_ssm_a_im, ssm_log_dt=m_ssm_log_dt, ssm_b_re=m_ssm_b_re,
               ssm_b_im=m_ssm_b_im, ssm_c_re=m_ssm_c_re, ssm_c_im=m_ssm_c_im, ssm_d=m_ssm_d, ssm_glu_w=m_ssm_glu_w,
               ssm_glu_b=m_ssm_glu_b, odd_w_in=m_odd_w_in, pool_w=m_pool_w, pool_scale=m_pool_scale,
               odd_w_out=m_odd_w_out)
    var = dict(pre_norm=v_pre_norm, post_norm=v_post_norm, even_w_in=v_even_w_in, even_w_out=v_even_w_out,
               ssm_a_re=v_ssm_a_re, ssm_a_im=v_ssm_a_im, ssm_log_dt=v_ssm_log_dt, ssm_b_re=v_ssm_b_re,
               ssm_b_im=v_ssm_b_im, ssm_c_re=v_ssm_c_re, ssm_c_im=v_ssm_c_im, ssm_d=v_ssm_d, ssm_glu_w=v_ssm_glu_w,
               ssm_glu_b=v_ssm_glu_b, odd_w_in=v_odd_w_in, pool_w=v_pool_w, pool_scale=v_pool_scale,
               odd_w_out=v_odd_w_out)
    me = _my_index()
    scale_cols = pool_scale.shape[1]

    def start_gather(tag, layer, keys, after=()):
        i = layer // 2
        shards = [w[FAMILY[(layer % 2, k)]][i].astype(BF16) for k in keys]
        if layer % 2 == 1:
            shards.append(jnp.pad(pool_scale[i][None], ((0, PACK_ROWS_ALIGN - 1), (0, 0))))
        return _xchg_start(f"gather_start_{tag}", shards, True, after)

    gather_started = {0: start_gather("0", 0, EVEN_SHARDED[:1])}
    small = {nm: w[nm] for nm in SMALL_NAMES}

    def get_weights(layer, after):
        keys = EVEN_SHARDED[:1] if layer == 0 else _sharded_keys(layer)
        lands = _xchg_wait(f"gather_wait_{layer}", gather_started[layer], True, after)
        wts = {k: _from_gathered(k, gat) for k, gat in zip(keys, lands)}
        if layer % 2 == 1:
            wts["pool_scale"] = lands[-1][:, 0, :].reshape(N_DEV * scale_cols)
        if layer == 0:
            prev = gather_started["0_late"] = start_gather("0_late", 0, EVEN_SHARDED[1:], after=(lands[0],))
            for later in (1, 2, 3):
                prev = gather_started[later] = start_gather(str(later), later, _sharded_keys(later), after=(prev[4],))
            wts["token"] = sum(gather_started[tag][4][0, 0] for tag in ("0_late", 1, 2, 3))

            def late(after_late):
                late_lands = _xchg_wait("gather_wait_0_late", gather_started["0_late"], True, (after_late,))
                return tuple(_from_gathered(k, gat) for k, gat in zip(EVEN_SHARDED[1:], late_lands))

            wts["late"] = late
        elif layer == 2:
            wts["late"] = lambda after_late: (wts["w_out"], wts["glu_w"])
        return wts

    scatter_started = []

    def on_w(layer, gw):
        keys = tuple(k for k in _sharded_keys(layer) if k in gw)
        started = _xchg_start(f"scatter_start_{layer}_{keys[0]}", [_to_slots(k, gw[k]) for k in keys], False)
        scatter_started.append((layer, keys, started))
        return started[4]

    def wait_scatters(layers, after):
        for layer, keys, started in scatter_started:
            if layer in layers:
                lands = _xchg_wait(f"scatter_wait_{layer}_{keys[0]}", started, False, after)
                for k, land in zip(keys, lands):
                    recv[(layer, k)] = land

    packed_names = ("pre_norm", "post_norm") + SSM_NAMES + ("ssm_d", "ssm_glu_b")
    tails = {nm: (SSM_GROUPS, SSM_STATE * SSM_GROUP) if nm in ("ssm_b_re", "ssm_b_im") else w[nm].shape[1:]
             for nm in packed_names}

    layer_grads = {}
    early_started, mid_started = [], []

    def on_ssm(layer, ssm_grads):
        if layer != 0:
            return None
        mid_started.append(_xchg_start("mid_start", [_pack(list(ssm_grads))], True))
        return mid_started[0][4]

    def on_grads(layer, lg):
        layer_grads[layer] = lg
        zero = jnp.zeros((), F32)
        if layer == 1:
            lgs = layer_grads
            early = ([jnp.concatenate([lgs[l][k] for l in (1, 2, 3)], axis=0) for k in ("pre", "post")]
                     + list(lgs[2]["ssm"]) + [lgs[2]["ssm_d"], lgs[2]["glu_b"],
                                              jnp.concatenate([lgs[1]["pool_scale"], lgs[3]["pool_scale"]], axis=0)])
            early_started.append(_xchg_start("small_start", [_pack(early)], True))
            zero = zero + early_started[0][4][0, 0]
        return zero

    loss_local, grad_x, token = _local_step(x[0], loss_target[0], small, get_weights, on_w, on_ssm, on_grads,
                                            zero=gather_started[0][4][0, 0])

    lg0 = layer_grads[0]
    late_started = _xchg_start("late_start", [_pack([lg0["pre"], lg0["post"], lg0["ssm_d"], lg0["glu_b"],
                                                     loss_local.reshape(1)]) + token], True)

    def adam_family(parity, k):
        nm = FAMILY[(parity, k)]
        shp = w[nm].shape
        cols = shp[-1]
        slot_list = [recv[(parity + 2 * i, k)].reshape(N_DEV, -1, cols) for i in range(2)]
        outs = _adam_layers(w[nm].reshape(2, -1, cols), slot_list, mom[nm].reshape(2, -1, cols),
                            var[nm].reshape(2, -1, cols), name=f"adam_{nm}")
        return [o.reshape(shp) for o in outs]

    recv, res = {}, {}
    wait_scatters((3, 1), (late_started[4],))
    for k in ODD_SHARDED:
        res[FAMILY[(1, k)]] = adam_family(1, k)
    odd_done = tuple(res[FAMILY[(1, k)]][0] for k in ODD_SHARDED)

    (early_slots,) = _xchg_wait("small_wait", early_started[0], True, odd_done)
    (mid_slots,) = _xchg_wait("mid_wait", mid_started[0], True, odd_done)
    early_shapes = [(w[nm].shape[0] - 1,) + tails[nm] for nm in packed_names] + [(2, N_DEV * scale_cols)]
    g_early = _unpack(_sum_slots(early_slots, "sum_small_early"), early_shapes)
    g_mid = _unpack(_sum_slots(mid_slots, "sum_small_mid"), [(1,) + tails[nm] for nm in SSM_NAMES])

    (late_slots,) = _xchg_wait("late_wait", late_started, True, (g_early[0], g_mid[0]))
    wait_scatters((2, 0), (late_slots,))
    for k in EVEN_SHARDED:
        res[FAMILY[(0, k)]] = adam_family(0, k)

    late_names = ("pre_norm", "post_norm", "ssm_d", "ssm_glu_b")
    g_late = _unpack(_sum_slots(late_slots, "sum_small_late"), [(1,) + tails[nm] for nm in late_names] + [(1,)])
    g_first = dict(zip(late_names, g_late))
    g_first.update(zip(SSM_NAMES, g_mid))
    dense = lambda nm, a: a.reshape((a.shape[0],) + tails[nm])
    outs = _adam_params([(dense(nm, w[nm]), dense(nm, mom[nm]), dense(nm, var[nm]), g_first[nm], g_early[j])
                         for j, nm in enumerate(packed_names)], "adam_small")
    for nm, four in zip(packed_names, outs):
        res[nm] = [o.reshape(w[nm].shape) for o in four]
    loss = g_late[-1].reshape(())
    g_scale = lax.dynamic_slice_in_dim(g_early[-1], me * scale_cols, scale_cols, axis=1)
    pad = ((0, PACK_ROWS_ALIGN - 2), (0, 0))
    outs = _adam(jnp.pad(pool_scale, pad), jnp.pad(g_scale, pad)[None], jnp.pad(m_pool_scale, pad),
                 jnp.pad(v_pool_scale, pad), name="adam_pool_scale")
    res["pool_scale"] = [o[:2] for o in outs]

    out = [loss, grad_x[None]]
    for kind in range(4):
        out += [res[nm][kind] for nm in WEIGHT_ORDER]
    return tuple(out)
```

```python
import functools
import math

import jax
import jax.numpy as jnp
from jax import lax
from jax.experimental import pallas as pl
from jax.experimental.pallas import tpu as pltpu

F32 = jnp.float32
BF16 = jnp.bfloat16
SDS = jax.ShapeDtypeStruct

N_DEV = 8
S = 2048
D = 1024
HEAD_DIM = 64
ROT_DIM = 16
ROPE_THETA = 500000.0
ATT_W = 1024
SSM_W = 512
SSM_GROUPS = 32
SSM_GROUP = 16
SSM_STATE = 64
N_CPLX = SSM_GROUPS * SSM_STATE
POOL_W = 2048
POOL_GROUP = 512
EVEN_IN = 5120
EVEN_OUT = 1536
ODD_IN = 4096
RMS_EPS = 1e-6
LANES = 128
VMEM_LIMIT = 48 * 1024 * 1024

ADAM_LR = 0.001
ADAM_B1 = 0.9
ADAM_B2 = 0.999
ADAM_EPS = 1e-08
ADAM_WD = 0.01
ADAM_STEP = 10

MESH_ID = pl.DeviceIdType.MESH
NN = (((1,), (0,)), ((), ()))
NT = (((1,), (1,)), ((), ()))
TN = (((0,), (0,)), ((), ()))
_DN = {"nn": NN, "nt": NT, "tn": TN}


def _cparams(sem):
    return pltpu.CompilerParams(dimension_semantics=sem, vmem_limit_bytes=VMEM_LIMIT)


def _in_hbm(arrs):
    return [pltpu.with_memory_space_constraint(a, pltpu.HBM) for a in arrs]


MM_TILES = (1024, 768, 512)


def _tile(dim):
    return next((t for t in MM_TILES if dim % t == 0), dim)


NT_BLOCKS_PER_STEP = 2


def _mm(a, b, mode, out_dtype, b_blocks=False, out_blocks=False, a_cols=None, after=()):
    if b_blocks:
        nblk, rows, cb = b.shape
        b2_shape = (rows, nblk * cb)
    else:
        b2_shape = b.shape
    a_shape = a.shape if a_cols is None else (a.shape[0], a_cols[1])
    if mode == "nn":
        (m, k), n = a_shape, b2_shape[1]
    elif mode == "nt":
        (m, k), n = a_shape, b2_shape[0]
    else:
        (k, m), n = a_shape, b2_shape[1]
    tm, tn, tk = _tile(m), _tile(n), _tile(k)
    per_step = 1
    if b_blocks and mode == "nn":
        tn = cb
        tm = m
    if b_blocks and mode == "nt":
        per_step = NT_BLOCKS_PER_STEP
        tk = per_step * cb
    if out_blocks:
        tn = n // N_DEV
        tk = k
    nk = k // tk
    a_unit = tm if mode == "tn" else tk
    assert a_cols is None or a_cols[0] % a_unit == 0
    a_off = 0 if a_cols is None else a_cols[0] // a_unit

    def body(a_ref, b_ref, *rest):
        o_ref, acc_ref = rest[-2:]
        kk = pl.program_id(2)
        if per_step == 1:
            part = lax.dot_general(a_ref[...].astype(BF16), b_ref[...].astype(BF16), _DN[mode],
                                   preferred_element_type=F32)
        else:
            part = None
            for blk in range(per_step):
                d = lax.dot_general(a_ref[:, blk * cb:(blk + 1) * cb].astype(BF16), b_ref[blk].astype(BF16), NT,
                                    preferred_element_type=F32)
                part = d if part is None else part + d
        if nk == 1:
            o_ref[...] = part.astype(o_ref.dtype)
            return

        @pl.when(kk == 0)
        def _():
            acc_ref[...] = part

        @pl.when((kk > 0) & (kk < nk - 1))
        def _():
            acc_ref[...] += part

        @pl.when(kk == nk - 1)
        def _():
            o_ref[...] = (acc_ref[...] + part).astype(o_ref.dtype)

    if mode == "nn":
        a_spec = pl.BlockSpec((tm, tk), lambda i, j, kk: (i, a_off + kk))
        b_spec = pl.BlockSpec((tk, tn), lambda i, j, kk: (kk, j))
    elif mode == "nt":
        a_spec = pl.BlockSpec((tm, tk), lambda i, j, kk: (i, a_off + kk))
        b_spec = pl.BlockSpec((tn, tk), lambda i, j, kk: (j, kk))
    else:
        a_spec = pl.BlockSpec((tk, tm), lambda i, j, kk: (kk, a_off + i))
        b_spec = pl.BlockSpec((tk, tn), lambda i, j, kk: (kk, j))
    if b_blocks and mode == "nn":
        b_spec = pl.BlockSpec((None, tk, cb), lambda i, j, kk: (j, kk, 0))
    if b_blocks and mode == "nt":
        b_spec = pl.BlockSpec((per_step, tn, cb), lambda i, j, kk: (kk, j, 0))
    out_spec = pl.BlockSpec((tm, tn), lambda i, j, kk: (i, j))
    out_shape = SDS((m, n), out_dtype)
    if out_blocks:
        out_spec = pl.BlockSpec((None, tm, tn), lambda i, j, kk: (j, i, 0))
        out_shape = SDS((N_DEV, m, tn), out_dtype)
    return pl.pallas_call(
        body, name=f"mm_{mode}_{m}x{k}x{n}",
        grid=(m // tm, n // tn, nk),
        in_specs=[a_spec, b_spec] + [pl.BlockSpec(memory_space=pl.ANY)] * len(after),
        out_specs=out_spec,
        out_shape=out_shape,
        scratch_shapes=[pltpu.VMEM((tm, tn) if nk > 1 else (8, LANES), F32)],
        compiler_params=_cparams(("parallel", "parallel", "arbitrary")),
    )(*_in_hbm((a, b)), *after)


def _gmm(a, b, mode, out_dtype, tm=S):
    ng, gw = POOL_W // POOL_GROUP, POOL_GROUP
    ns = S // tm
    if mode in ("nn", "nt"):
        def body(a_ref, b_ref, o_ref):
            o_ref[...] = lax.dot_general(a_ref[...].astype(BF16), b_ref[...].astype(BF16), _DN[mode],
                                         preferred_element_type=F32).astype(o_ref.dtype)

        return pl.pallas_call(
            body, name=f"gmm_{mode}", grid=(ng, ns),
            in_specs=[pl.BlockSpec((tm, gw), lambda g, i: (i, g)),
                      pl.BlockSpec((None, gw, gw), lambda g, i: (g, 0, 0))],
            out_specs=pl.BlockSpec((tm, gw), lambda g, i: (i, g)),
            out_shape=SDS((S, POOL_W), out_dtype),
            compiler_params=_cparams(("parallel", "parallel")),
        )(*_in_hbm((a, b)))

    def body_tn(a_ref, b_ref, o_ref, acc_ref):
        i = pl.program_id(1)

        @pl.when(i == 0)
        def _():
            acc_ref[...] = jnp.zeros_like(acc_ref)

        acc_ref[...] += lax.dot_general(a_ref[...].astype(BF16), b_ref[...].astype(BF16), TN,
                                        preferred_element_type=F32)

        @pl.when(i == ns - 1)
        def _():
            o_ref[...] = acc_ref[...].astype(o_ref.dtype)

    return pl.pallas_call(
        body_tn, name="gmm_tn", grid=(ng, ns),
        in_specs=[pl.BlockSpec((tm, gw), lambda g, i: (i, g)),
                  pl.BlockSpec((tm, gw), lambda g, i: (i, g))],
        out_specs=pl.BlockSpec((None, gw, gw), lambda g, i: (g, 0, 0)),
        out_shape=SDS((ng, gw, gw), out_dtype),
        scratch_shapes=[pltpu.VMEM((gw, gw), F32)],
        compiler_params=_cparams(("parallel", "arbitrary")),
    )(*_in_hbm((a, b)))


def _rowwise(fn, inputs, out_defs, acc_defs=(), tm=512, name=None, after=()):
    n_in, n_out, n_acc = len(inputs), len(out_defs), len(acc_defs)
    n_after = len(after)
    in_specs, args = [], []
    for arr, width, cb in inputs:
        if arr.shape[0] == 1:
            in_specs.append(pl.BlockSpec((1, width), lambda i, cb=cb: (0, cb)))
        else:
            in_specs.append(pl.BlockSpec((tm, width), lambda i, cb=cb: (i, cb)))
        args.append(arr)
    out_defs = [d if len(d) == 4 else (d[0], d[1], d[0], 0) for d in out_defs]
    out_shape = [SDS((S, ww), dt) for _, dt, ww, _ in out_defs] + [SDS((1, w), F32) for w in acc_defs]
    out_specs = ([pl.BlockSpec((tm, w), lambda i, cb=cb: (i, cb)) for w, _, _, cb in out_defs]
                 + [pl.BlockSpec((1, w), lambda i: (0, 0)) for w in acc_defs])

    def kern(*refs):
        vals = [r[...] for r in refs[:n_in]]
        outs, accs = fn(*vals)
        out_refs = refs[n_in + n_after:]
        for r, v in zip(out_refs[:n_out], outs):
            r[...] = v.astype(r.dtype)
        if n_acc:
            acc_refs = out_refs[n_out:]

            @pl.when(pl.program_id(0) == 0)
            def _():
                for r in acc_refs:
                    r[...] = jnp.zeros_like(r)

            for r, v in zip(acc_refs, accs):
                r[...] += jnp.sum(v, axis=0, keepdims=True)

    res = pl.pallas_call(
        kern, name=name, grid=(S // tm,), in_specs=in_specs + [pl.BlockSpec(memory_space=pl.ANY)] * n_after,
        out_specs=out_specs, out_shape=out_shape, compiler_params=_cparams(("arbitrary",)),
    )(*_in_hbm(args), *after)
    return res


def _sigmoid(x):
    return 1.0 / (1.0 + jnp.exp(-x))


def _silu_and_grad(x):
    s = _sigmoid(x)
    return x * s, s * (1.0 + x * (1.0 - s))


_GELU_K = math.sqrt(2.0 / math.pi)
_GELU_C = 0.044715


def _gelu_and_grad(x):
    t = jnp.tanh(_GELU_K * (x + _GELU_C * (x * x * x)))
    cdf = 0.5 * (1.0 + t)
    grad = cdf + 0.5 * x * (1.0 - t * t) * (_GELU_K * (1.0 + 3.0 * _GELU_C * x * x))
    return x * cdf, grad


def _rms(xv, gain):
    r = lax.rsqrt(jnp.mean(xv * xv, axis=-1, keepdims=True) + RMS_EPS)
    return xv * r * gain


def _rms_bwd(dout, xv, gain):
    r = lax.rsqrt(jnp.mean(xv * xv, axis=-1, keepdims=True) + RMS_EPS)
    xhat = xv * r
    dxhat = dout * gain
    dx = r * (dxhat - xhat * jnp.mean(dxhat * xhat, axis=-1, keepdims=True))
    return dx, dout * xhat


def _norm_fwd(x, gain):
    (h,) = _rowwise(lambda xv, g: ((_rms(xv, g),), ()), [(x, D, 0), (gain, D, 0)], [(D, BF16)], name="norm_fwd")
    return h


def _post_fwd(x, y, gain, next_gain):
    def fn(xv, yv, g, gn):
        out = xv + _rms(yv, g)
        return (out, _rms(out, gn)), ()

    return _rowwise(fn, [(x, D, 0), (y, D, 0), (gain, D, 0), (next_gain, D, 0)], [(D, F32), (D, BF16)],
                    name="post_fwd")


def _post_fwd_loss(x, y, gain, tgt):
    def fn(xv, yv, g, tv):
        e = xv + _rms(yv, g) - tv
        return (e * (1.0 / D),), (e * e,)

    return _rowwise(fn, [(x, D, 0), (y, D, 0), (gain, D, 0), (tgt, D, 0)], [(D, F32)], [D], name="post_fwd_loss")


def _post_bwd(g, y, gain):
    def fn(gv, yv, gn):
        dx, dg = _rms_bwd(gv, yv, gn)
        return (dx,), (dg,)

    return _rowwise(fn, [(g, D, 0), (y, D, 0), (gain, D, 0)], [(D, BF16)], [D], name="post_bwd")


def _pre_bwd(g, dh, x, gain):
    def fn(gv, dhv, xv, gn):
        dx, dg = _rms_bwd(dhv, xv, gn)
        return (gv + dx,), (dg,)

    return _rowwise(fn, [(g, D, 0), (dh, D, 0), (x, D, 0), (gain, D, 0)], [(D, F32)], [D], name="pre_bwd")


def _pool(u_arr, col_block, transpose, out_dtype, into=None, tc=256):
    n_t = POOL_W // tc
    per_group = POOL_GROUP // tc

    def body(u_ref, *rest):
        o_ref = rest[-1]
        grp = pl.program_id(0) // per_group
        t = lax.broadcasted_iota(jnp.int32, (S, 1), 0)
        for g in range(POOL_W // POOL_GROUP):
            @pl.when(grp == g)
            def _(g=g):
                xv = u_ref[...]
                cnt = jnp.minimum(t + 1, 2 << g).astype(F32)
                cur = xv / cnt if transpose else xv
                for k in (1, 2, 4, 8)[:g + 1]:
                    if transpose:
                        cur = cur + jnp.where(t < S - k, pltpu.roll(cur, S - k, 0), 0.0)
                    else:
                        cur = cur + jnp.where(t >= k, pltpu.roll(cur, k, 0), 0.0)
                res = cur - xv if transpose else cur / cnt - xv
                o_ref[...] = res.astype(o_ref.dtype)

    in_specs = [pl.BlockSpec((S, tc), lambda c: (0, col_block * n_t + c))]
    args = [u_arr]
    if into is not None:
        in_specs.append(pl.BlockSpec(memory_space=pl.ANY))
        args.append(into)
    return pl.pallas_call(
        body, name="pool_bwd" if transpose else "pool_fwd", grid=(n_t,),
        in_specs=in_specs,
        out_specs=pl.BlockSpec((S, tc), lambda c: (0, c)),
        out_shape=SDS((S, POOL_W) if into is None else into.shape, out_dtype),
        input_output_aliases={} if into is None else {1: 0},
        compiler_params=_cparams(("parallel",)),
    )(*args)


def _rope_tables(zero):
    pos = jnp.arange(S, dtype=jnp.int32).astype(F32) + zero
    inv_freq = ROPE_THETA ** (-jnp.arange(0, ROT_DIM, 2, dtype=F32) / ROT_DIM)
    ang = pos[:, None] * inv_freq[None, :]
    cos8, sin8 = jnp.cos(ang), jnp.sin(ang)
    half = ROT_DIM // 2
    zeros = jnp.zeros((S, HEAD_DIM - ROT_DIM), F32)
    cos = jnp.concatenate([cos8, cos8, jnp.ones((S, HEAD_DIM - ROT_DIM), F32)], axis=1)
    lo = jnp.concatenate([-sin8, jnp.zeros((S, half), F32), zeros], axis=1)
    hi = jnp.concatenate([jnp.zeros((S, half), F32), sin8, zeros], axis=1)
    rep = LANES // HEAD_DIM
    return jnp.tile(cos, (1, rep)), jnp.tile(lo, (1, rep)), jnp.tile(hi, (1, rep))


def _rotate(xv, cos, lo, hi, transpose):
    width = xv.shape[1]
    rep = width // LANES
    wide = lambda tab: jnp.concatenate([tab] * rep, axis=1)
    half = ROT_DIM // 2
    up = pltpu.roll(xv, width - half, 1)
    dn = pltpu.roll(xv, half, 1)
    mixed = up * wide(lo) + dn * wide(hi)
    return xv * wide(cos) - mixed if transpose else xv * wide(cos) + mixed


def _qkv_prep(proj, tables):
    cos, lo, hi = tables

    def fn(x, c, l, h):
        rot = _rotate(x[:, :2 * ATT_W], c, l, h, False)
        return (jnp.concatenate([(rot[:, :ATT_W] * HEAD_DIM ** -0.5).astype(BF16), rot[:, ATT_W:].astype(BF16),
                                 x[:, 2 * ATT_W:].astype(BF16)], axis=1),), ()

    (qkv,) = _rowwise(fn, [(proj, 3 * ATT_W, 0), (cos, LANES, 0), (lo, LANES, 0), (hi, LANES, 0)],
                      [(3 * ATT_W, BF16)], name="qkv_prep")
    return qkv


ATT_T = 512


def _multiplicity(delta):
    ok = delta >= 0
    near = jnp.where(ok & (delta <= 128), 1.0, 0.0)
    mid = jnp.where(ok & (delta <= 512) & ((delta & 3) == 0), 1.0, 0.0)
    far = jnp.where(ok & ((delta & 15) == 0), 1.0, 0.0)
    return near + mid + far


def _attention_bias(zero):
    t = ATT_T
    pos = jnp.arange(t, dtype=jnp.int32) + jnp.asarray(zero).astype(jnp.int32)
    delta = jnp.arange(S // t, dtype=jnp.int32)[:, None, None] * t + pos[None, :, None] - pos[None, None, :]
    mult = _multiplicity(delta)
    return jnp.where(mult > 0.0, jnp.log(jnp.maximum(mult, 1.0)), -1e30).astype(F32)


def _head_split(v, first):
    zero = jnp.zeros_like(v)
    return [jnp.where(first, v, zero), jnp.where(first, zero, v)]


def _flash_fwd(qkv, bias):
    t = ATT_T
    n_hp = ATT_W // LANES

    def body(q_ref, k_ref, v_ref, b_ref, o_ref, lse_ref):
        i = pl.program_id(1)
        first = lax.broadcasted_iota(jnp.int32, (1, LANES), 1) < HEAD_DIM
        qs = _head_split(q_ref[...], first)

        def kv_step(j, carry):
            m0, l0, m1, l1, acc = carry
            off = pl.multiple_of(j * t, t)
            kb = k_ref[pl.ds(off, t), :]
            vs = _head_split(v_ref[pl.ds(off, t), :], first)
            bias_t = b_ref[i - j]
            new = []
            pv = None
            for h, (m_prev, l_prev) in enumerate(((m0, l0), (m1, l1))):
                s = lax.dot_general(qs[h], kb, NT, preferred_element_type=F32) + bias_t
                m_new = jnp.maximum(m_prev, jnp.max(s, axis=1, keepdims=True))
                p = jnp.exp(s - m_new)
                alpha = jnp.exp(m_prev - m_new)
                l_new = alpha * l_prev + jnp.sum(p, axis=1, keepdims=True)
                d = lax.dot_general(p.astype(BF16), vs[h], NN, preferred_element_type=F32)
                pv = d if pv is None else pv + d
                new.append((m_new, l_new, alpha))
            acc = acc * jnp.where(first, new[0][2], new[1][2]) + pv
            return new[0][0], new[0][1], new[1][0], new[1][1], acc

        neg = jnp.full((t, 1), -1e30, F32)
        zero = jnp.zeros((t, 1), F32)
        m0, l0, m1, l1, acc = lax.fori_loop(0, i + 1, kv_step, (neg, zero, neg, zero, jnp.zeros((t, LANES), F32)))
        o_ref[...] = acc * jnp.where(first, 1.0 / l0, 1.0 / l1)
        lse_ref[...] = jnp.where(first, m0 + jnp.log(l0), m1 + jnp.log(l1))

    blk = pl.BlockSpec((t, LANES), lambda hp, i: (i, hp))
    k_full = pl.BlockSpec((S, LANES), lambda hp, i: (0, n_hp + hp))
    v_full = pl.BlockSpec((S, LANES), lambda hp, i: (0, 2 * n_hp + hp))
    return pl.pallas_call(
        body, name="flash_fwd", grid=(n_hp, S // t),
        in_specs=[blk, k_full, v_full, pl.BlockSpec((S // t, t, t), lambda hp, i: (0, 0, 0))], out_specs=[blk, blk],
        out_shape=[SDS((S, ATT_W), F32), SDS((S, ATT_W), F32)],
        compiler_params=_cparams(("parallel", "arbitrary")),
    )(*_in_hbm((qkv, qkv, qkv, bias)))


def _flash_bwd(qkv, o, do, lse, bias, after=()):
    t = ATT_T
    n_hp = ATT_W // LANES
    n_t = S // t

    def body(q_ref, k_ref, v_ref, o_ref, do_ref, lse_ref, b_ref, *rest):
        dq_ref, dk_ref, dv_ref = rest[-3:]
        j = pl.program_id(1)
        first = lax.broadcasted_iota(jnp.int32, (1, LANES), 1) < HEAD_DIM

        @pl.when(j == 0)
        def _():
            dq_ref[...] = jnp.zeros_like(dq_ref)

        kb = k_ref[...]
        vb = v_ref[...]
        ks = _head_split(kb, first)

        def q_step(i, carry):
            dk_acc, dv_acc = carry
            rows = pl.ds(pl.multiple_of(i * t, t), t)
            qs = _head_split(q_ref[rows, :], first)
            dob = do_ref[rows, :]
            prod = dob * o_ref[rows, :]
            d_all = jnp.sum(prod, axis=1, keepdims=True)
            d0 = jnp.sum(jnp.where(first, prod, 0.0), axis=1, keepdims=True)
            lse_b = lse_ref[rows, :]
            lse0 = jnp.max(jnp.where(first, lse_b, -jnp.inf), axis=1, keepdims=True)
            lse1 = jnp.max(jnp.where(first, -jnp.inf, lse_b), axis=1, keepdims=True)
            dos = _head_split(dob.astype(BF16), first)
            bias_t = b_ref[i - j]
            dq_t = jnp.zeros((t, LANES), F32)
            for h, (lse_h, d_h) in enumerate(((lse0, d0), (lse1, d_all - d0))):
                s = lax.dot_general(qs[h], kb, NT, preferred_element_type=F32)
                p = jnp.exp(s + (bias_t - lse_h))
                dp = lax.dot_general(dos[h], vb, NT, preferred_element_type=F32)
                ds = (p * (dp - d_h)).astype(BF16)
                dv_acc = dv_acc + lax.dot_general(p.astype(BF16), dos[h], TN, preferred_element_type=F32)
                dk_acc = dk_acc + lax.dot_general(ds, qs[h], TN, preferred_element_type=F32)
                dq_t = dq_t + lax.dot_general(ds, ks[h], NN, preferred_element_type=F32)
            dq_ref[rows, :] += dq_t
            return dk_acc, dv_acc

        zero = jnp.zeros((t, LANES), F32)
        dk_acc, dv_acc = lax.fori_loop(j, n_t, q_step, (zero, zero))
        dk_ref[...] = dk_acc
        dv_ref[...] = dv_acc

    blk = pl.BlockSpec((t, LANES), lambda hp, j: (j, hp))
    full = pl.BlockSpec((S, LANES), lambda hp, j: (0, hp))
    k_blk = pl.BlockSpec((t, LANES), lambda hp, j: (j, n_hp + hp))
    v_blk = pl.BlockSpec((t, LANES), lambda hp, j: (j, 2 * n_hp + hp))
    return pl.pallas_call(
        body, name="flash_bwd", grid=(n_hp, n_t),
        in_specs=([full, k_blk, v_blk, full, full, full, pl.BlockSpec((n_t, t, t), lambda hp, j: (0, 0, 0))]
                  + [pl.BlockSpec(memory_space=pl.ANY)] * len(after)),
        out_specs=[full, blk, blk],
        out_shape=[SDS((S, ATT_W), F32)] * 3,
        compiler_params=_cparams(("parallel", "arbitrary")),
    )(*_in_hbm((qkv, qkv, qkv, o, do, lse, bias)), *after)


SCAN_T = 256
SCAN_GROUP = 8
SCAN_STEPS = (1, 2, 4)
ST_ROWS = 2 * N_CPLX // LANES
HALF = ST_ROWS // 2


def _scan_tables(lam_t):
    lam = lax.complex(lam_t[:HALF].reshape(N_CPLX), lam_t[HALF:].reshape(N_CPLX))
    pows = [lam]
    for _ in range(SCAN_GROUP - 1):
        pows.append(pows[-1] * lam)
    pows = jnp.stack(pows)
    sub = jnp.arange(SCAN_GROUP)[:, None]
    fwd = [jnp.where(sub >= k, pows[k - 1][None, :], 0.0) for k in SCAN_STEPS] + [pows]
    conj = jnp.conj(pows)
    bwd = [jnp.where(sub <= SCAN_GROUP - 1 - k, conj[k - 1][None, :], 0.0) for k in SCAN_STEPS] + [conj[::-1]]

    def pack(tabs):
        return jnp.stack([jnp.concatenate([jnp.real(t), jnp.imag(t)], axis=1) for t in tabs]).astype(F32)

    return pack(fwd), pack(bwd)


def _cmul_add(xr, xi, lr, li, sr, si):
    return xr + lr * sr - li * si, xi + lr * si + li * sr


def _group_scan(xr, xi, tab_ref, cr, ci, reverse):
    for j, k in enumerate(SCAN_STEPS):
        shift = SCAN_GROUP - k if reverse else k
        xr, xi = _cmul_add(xr, xi, tab_ref[j, :, :N_CPLX], tab_ref[j, :, N_CPLX:],
                           pltpu.roll(xr, shift, 0), pltpu.roll(xi, shift, 0))
    return _cmul_add(xr, xi, tab_ref[3, :, :N_CPLX], tab_ref[3, :, N_CPLX:],
                     jnp.broadcast_to(cr, (SCAN_GROUP, N_CPLX)), jnp.broadcast_to(ci, (SCAN_GROUP, N_CPLX)))


SSM_SUPER = 4
SB_ROWS = SSM_W // SSM_SUPER
SB_COLS = N_CPLX // SSM_SUPER


def _super_blocks():
    return [(slice(b * SB_ROWS, (b + 1) * SB_ROWS), slice(h * SB_COLS, (h + 1) * SB_COLS),
             slice(h * N_CPLX + b * SB_COLS, h * N_CPLX + (b + 1) * SB_COLS))
            for b in range(SSM_SUPER) for h in range(2)]


def _dot16(a, b, dims):
    return lax.dot_general(a.astype(BF16), b.astype(BF16), dims, preferred_element_type=F32)


def _s5_fwd(tab, u_arr, u_cols, w_b, w_ct):
    nc = N_CPLX

    def body(tab_ref, u_ref, wb_ref, wct_ref, st_ref, y_ref, carry, bu_scr):
        @pl.when(pl.program_id(0) == 0)
        def _():
            carry[...] = jnp.zeros_like(carry)

        for rows_b, cols_c, cols_s in _super_blocks():
            bu_scr[:, cols_s] = _dot16(u_ref[:, rows_b], wb_ref[rows_b, cols_c], NN)

        def group(a, c):
            rows = pl.ds(pl.multiple_of(a * SCAN_GROUP, SCAN_GROUP), SCAN_GROUP)
            xr, xi = _group_scan(bu_scr[rows, :nc], bu_scr[rows, nc:], tab_ref, c[0], c[1], False)
            st_ref[rows, :nc] = xr
            st_ref[rows, nc:] = xi
            return xr[SCAN_GROUP - 1:SCAN_GROUP, :], xi[SCAN_GROUP - 1:SCAN_GROUP, :]

        cr, ci = lax.fori_loop(0, SCAN_T // SCAN_GROUP, group, (carry[:, :nc], carry[:, nc:]), unroll=2)
        carry[:, :nc] = cr
        carry[:, nc:] = ci

        for b in range(SSM_SUPER):
            (rows_b, cols_re, st_re), (_, cols_im, st_im) = _super_blocks()[2 * b:2 * b + 2]
            y_ref[:, rows_b] = (_dot16(st_ref[:, st_re], wct_ref[rows_b, cols_re], NT)
                                + _dot16(st_ref[:, st_im], wct_ref[rows_b, cols_im], NT))

    const = lambda shape: pl.BlockSpec(shape, lambda i: (0,) * len(shape))
    return pl.pallas_call(
        body, name="s5_fwd", grid=(S // SCAN_T,),
        in_specs=[const((4, SCAN_GROUP, 2 * nc)), pl.BlockSpec((SCAN_T, SSM_W), lambda i: (i, u_cols[0] // SSM_W)),
                  const((SSM_W, 2 * SB_COLS)), const((SSM_W, 2 * SB_COLS))],
        out_specs=[pl.BlockSpec((SCAN_T, 2 * nc), lambda i: (i, 0)), pl.BlockSpec((SCAN_T, SSM_W), lambda i: (i, 0))],
        out_shape=[SDS((S, 2 * nc), F32), SDS((S, SSM_W), F32)],
        scratch_shapes=[pltpu.VMEM((1, 2 * nc), F32), pltpu.VMEM((SCAN_T, 2 * nc), F32)],
        compiler_params=_cparams(("arbitrary",)),
    )(*_in_hbm((tab, u_arr, w_b, w_ct)))


def _s5_bwd(tab, dy, states, u_arr, u_cols, w_b, w_ct):
    n_blk = S // SCAN_T
    nc = N_CPLX

    def body(tab_ref, dy_ref, x_ref, u_ref, wb_ref, wct_ref, du_ref, dlam_ref, dwb_ref, dwct_ref,
             carry, acc, d_scr, g_ref):
        i = pl.program_id(0)

        @pl.when(i == 0)
        def _():
            carry[...] = jnp.zeros_like(carry)
            acc[...] = jnp.zeros_like(acc)
            dwb_ref[...] = jnp.zeros_like(dwb_ref)
            dwct_ref[...] = jnp.zeros_like(dwct_ref)

        for rows_b, cols_c, cols_s in _super_blocks():
            d_scr[:, cols_s] = _dot16(dy_ref[:, rows_b], wct_ref[rows_b, cols_c], NN)

        last_row = lax.broadcasted_iota(jnp.int32, (SCAN_GROUP, 1), 0) == SCAN_GROUP - 1
        d_ref = d_scr

        def group(j, c):
            cr, ci = c
            rows = pl.ds(pl.multiple_of((SCAN_T // SCAN_GROUP - 1 - j) * SCAN_GROUP, SCAN_GROUP), SCAN_GROUP)
            gr, gi = _group_scan(d_ref[rows, :nc], d_ref[rows, nc:], tab_ref, cr, ci, True)
            g_ref[rows, :nc] = gr
            g_ref[rows, nc:] = gi
            nr = jnp.where(last_row, jnp.broadcast_to(cr, (SCAN_GROUP, nc)), pltpu.roll(gr, SCAN_GROUP - 1, 0))
            ni = jnp.where(last_row, jnp.broadcast_to(ci, (SCAN_GROUP, nc)), pltpu.roll(gi, SCAN_GROUP - 1, 0))
            sr, si = x_ref[rows, :nc], x_ref[rows, nc:]
            acc[:, :nc] += nr * sr + ni * si
            acc[:, nc:] += ni * sr - nr * si
            return gr[0:1, :], gi[0:1, :]

        cr, ci = lax.fori_loop(0, SCAN_T // SCAN_GROUP, group, (carry[:, :nc], carry[:, nc:]), unroll=2)
        carry[:, :nc] = cr
        carry[:, nc:] = ci

        for b in range(SSM_SUPER):
            (rows_b, cols_re, st_re), (_, cols_im, st_im) = _super_blocks()[2 * b:2 * b + 2]
            du_ref[:, rows_b] = (_dot16(g_ref[:, st_re], wb_ref[rows_b, cols_re], NT)
                                 + _dot16(g_ref[:, st_im], wb_ref[rows_b, cols_im], NT))
            for cols_c, cols_s in ((cols_re, st_re), (cols_im, st_im)):
                dwb_ref[rows_b, cols_c] += _dot16(u_ref[:, rows_b], g_ref[:, cols_s], TN)
                dwct_ref[rows_b, cols_c] += _dot16(dy_ref[:, rows_b], x_ref[:, cols_s], TN)

        @pl.when(i == n_blk - 1)
        def _():
            dlam_ref[...] = jnp.sum(acc[...], axis=0, keepdims=True)

    const = lambda shape: pl.BlockSpec(shape, lambda i: (0,) * len(shape))
    rows = lambda width, col_block=0: pl.BlockSpec((SCAN_T, width), lambda i: (n_blk - 1 - i, col_block))
    maps = const((SSM_W, 2 * SB_COLS))
    return pl.pallas_call(
        body, name="s5_bwd", grid=(n_blk,),
        in_specs=[const((4, SCAN_GROUP, 2 * nc)), rows(SSM_W), rows(2 * nc), rows(SSM_W, u_cols[0] // SSM_W), maps, maps],
        out_specs=[rows(SSM_W), const((1, 2 * nc)), maps, maps],
        out_shape=[SDS((S, SSM_W), F32), SDS((1, 2 * nc), F32), SDS((SSM_W, 2 * SB_COLS), F32),
                   SDS((SSM_W, 2 * SB_COLS), F32)],
        scratch_shapes=[pltpu.VMEM((1, 2 * nc), F32), pltpu.VMEM((SCAN_GROUP, 2 * nc), F32),
                        pltpu.VMEM((SCAN_T, 2 * nc), F32), pltpu.VMEM((SCAN_T, 2 * nc), F32)],
        compiler_params=_cparams(("arbitrary",)),
    )(*_in_hbm((tab, dy, states, u_arr, w_b, w_ct)))


def _ssm_prep(a_re, a_im, log_dt, b_re, b_im, c_re, c_im):
    lam = lax.complex(a_re, a_im)
    dt = jnp.exp(log_dt)[:, None]
    lam_bar = jnp.exp(lam * dt)
    b_bar = ((lam_bar - 1.0) / lam)[..., None] * lax.complex(b_re, b_im)
    lam_t = jnp.concatenate([jnp.real(lam_bar).reshape(HALF, LANES), jnp.imag(lam_bar).reshape(HALF, LANES)], axis=0)
    groups_per_super = SSM_GROUPS // SSM_SUPER
    on_diag = ((lax.broadcasted_iota(jnp.int32, (SSM_W, SB_COLS), 0) // SSM_GROUP) % groups_per_super
               == lax.broadcasted_iota(jnp.int32, (SSM_W, SB_COLS), 1) // SSM_STATE)

    def compact(m):
        return jnp.where(on_diag, jnp.tile(m.reshape(SSM_W, SSM_STATE), (1, groups_per_super)), 0.0)

    w_b = jnp.concatenate([compact(jnp.real(b_bar).transpose(0, 2, 1)),
                           compact(jnp.imag(b_bar).transpose(0, 2, 1))], axis=1)
    w_ct = jnp.concatenate([compact(c_re), -compact(c_im)], axis=1)
    return lam_t, w_b, w_ct


U_SSM_COLS = (4 * ATT_W, SSM_W)


def _row(v):
    return v.reshape(1, -1)


def _even_fwd(x, h, tail, pre, post, w_in, late_w, glu_b, ssm_d, prep, tables):
    lam_t, w_b, w_ct = prep
    proj = _mm(h, w_in, "nn", F32, b_blocks=True)
    qkv = _qkv_prep(proj, tables[:3])
    w_out, glu_w = late_w(qkv)
    att, lse = _flash_fwd(qkv, tables[3])
    scan_fwd_tab, scan_bwd_tab = _scan_tables(lam_t)
    states, y = _s5_fwd(scan_fwd_tab, proj, U_SSM_COLS, w_b, w_ct)

    def act1(yv, uv, dv):
        return (_gelu_and_grad(yv + dv * uv)[0],), ()

    (z1,) = _rowwise(act1, [(y, SSM_W, 0), (proj, SSM_W, 8), (ssm_d, SSM_W, 0)], [(SSM_W, F32)], name="ssm_act_fwd")
    lin = _mm(z1, glu_w, "nn", F32)

    def gate(att_v, ga, gs, z1v, linv, bv):
        ssm_out = z1v * _sigmoid(linv + bv)
        return (jnp.concatenate([att_v * _silu_and_grad(ga)[0], ssm_out * _silu_and_grad(gs)[0]], axis=1),), ()

    (merged,) = _rowwise(gate, [(att, ATT_W, 0), (proj, ATT_W, 3), (proj, SSM_W, 9), (z1, SSM_W, 0),
                                (lin, SSM_W, 0), (glu_b, SSM_W, 0)], [(EVEN_OUT, BF16)], name="even_gate_fwd")
    yout = _mm(merged, w_out, "nn", F32)
    saved = (x, h, proj, qkv, att, lse, states, y, z1, lin, merged, yout, w_out, glu_w, scan_bwd_tab)
    return tail(x, yout, post) + (saved,)


def _even_bwd(g, saved, pre, post, w_in, late_w, glu_b, ssm_d, prep, tables, on_w, on_ssm):
    x, h, proj, qkv, att, lse, states, y, z1, lin, merged, yout, w_out, glu_w, scan_bwd_tab = saved
    lam_t, w_b, w_ct = prep
    dyout, dpost = _post_bwd(g, yout, post)
    dmerged = _mm(dyout, w_out, "nt", F32)
    dw_out = _mm(merged, dyout, "tn", BF16)

    def gate_bwd(dm_a, dm_s, att_v, ga, gs, z1v, linv, bv):
        sa, dsa = _silu_and_grad(ga)
        ss, dss = _silu_and_grad(gs)
        sig = _sigmoid(linv + bv)
        ssm_out = z1v * sig
        dssm = dm_s * ss
        dlin = dssm * z1v * sig * (1.0 - sig)
        return (dm_a * sa, dm_a * att_v * dsa, dm_s * ssm_out * dss, dssm * sig, dlin), (dlin,)

    datt, dg_att, dg_ssm, dz1a, dlin, dglu_b = _rowwise(
        gate_bwd, [(dmerged, ATT_W, 0), (dmerged, SSM_W, 2), (att, ATT_W, 0), (proj, ATT_W, 3), (proj, SSM_W, 9),
                   (z1, SSM_W, 0), (lin, SSM_W, 0), (glu_b, SSM_W, 0)],
        [(ATT_W, F32), (ATT_W, BF16), (SSM_W, BF16), (SSM_W, F32), (SSM_W, BF16)], [SSM_W], name="even_gate_bwd")
    dz1b = _mm(dlin, glu_w, "nt", F32)
    dglu_w = _mm(z1, dlin, "tn", BF16)

    def act1_bwd(da, db, yv, uv, dv):
        dpre = (da + db) * _gelu_and_grad(yv + dv * uv)[1]
        return (dpre, dpre * dv), (dpre * uv,)

    sent_late_w = on_w(dict(w_out=dw_out, glu_w=dglu_w))
    dy, du_direct, dd = _rowwise(act1_bwd, [(dz1a, SSM_W, 0), (dz1b, SSM_W, 0), (y, SSM_W, 0), (proj, SSM_W, 8),
                                            (ssm_d, SSM_W, 0)], [(SSM_W, BF16), (SSM_W, F32)], [SSM_W],
                                 name="ssm_act_bwd", after=(sent_late_w,))
    du_state, dlam_row, dw_b, dw_ct = _s5_bwd(scan_bwd_tab, dy, states, proj, U_SSM_COLS, w_b, w_ct)
    dlam = jnp.concatenate([dlam_row[0, :N_CPLX].reshape(HALF, LANES), dlam_row[0, N_CPLX:].reshape(HALF, LANES)],
                           axis=0)
    sent_ssm = on_ssm((dlam, dw_b, dw_ct))
    dq, dk, dv = _flash_bwd(qkv, att, datt, lse, tables[3], after=() if sent_ssm is None else (sent_ssm,))

    def assemble(dqv, dkv, dvv, dga, dua, dub, dgs, c, l, h):
        rot = _rotate(jnp.concatenate([dqv, dkv], axis=1), c, l, h, True)
        return (jnp.concatenate([(rot[:, :ATT_W] * HEAD_DIM ** -0.5).astype(BF16), rot[:, ATT_W:].astype(BF16),
                                 dvv.astype(BF16), dga, (dua + dub).astype(BF16), dgs], axis=1),), ()

    (dproj,) = _rowwise(assemble, [(dq, ATT_W, 0), (dk, ATT_W, 0), (dv, ATT_W, 0), (dg_att, ATT_W, 0),
                                   (du_state, SSM_W, 0), (du_direct, SSM_W, 0), (dg_ssm, SSM_W, 0),
                                   (tables[0], LANES, 0), (tables[1], LANES, 0), (tables[2], LANES, 0)],
                        [(EVEN_IN, BF16)], name="dproj_assemble")
    dw_in = _mm(h, dproj, "tn", BF16, out_blocks=True)
    sent = on_w(dict(w_in=dw_in))
    dh = _mm(dproj, w_in, "nt", F32, b_blocks=True, after=(sent,))
    g_prev, dpre = _pre_bwd(g, dh, x, pre)
    return g_prev, dict(pre=dpre, post=dpost, glu_b=dglu_b, ssm_d=dd)


def _odd_fwd(x, h, tail, pre, post, w_in, pool_w, pool_scale, w_out):
    proj = _mm(h, w_in, "nn", F32, b_blocks=True)
    mixed = _pool(proj, 0, False, BF16)
    ylin = _gmm(mixed, pool_w, "nn", F32)

    def gate(yl, gt, sc):
        return (yl * sc * _silu_and_grad(gt)[0],), ()

    (z,) = _rowwise(gate, [(ylin, POOL_W, 0), (proj, POOL_W, 1), (pool_scale, POOL_W, 0)], [(POOL_W, BF16)],
                    name="odd_gate_fwd")
    yout = _mm(z, w_out, "nn", F32)
    return tail(x, yout, post) + ((x, h, proj, mixed, ylin, z, yout),)


def _odd_bwd(g, saved, pre, post, w_in, pool_w, pool_scale, w_out, on_w):
    x, h, proj, mixed, ylin, z, yout = saved
    dyout, dpost = _post_bwd(g, yout, post)
    dz = _mm(dyout, w_out, "nt", F32)
    dw_out = _mm(z, dyout, "tn", BF16)

    def gate_bwd(dzv, yl, gt, sc):
        sg, dsg = _silu_and_grad(gt)
        tt = dzv * sg
        return (tt * sc, dzv * yl * sc * dsg), (tt * yl,)

    dylin, dproj_gate, dscale = _rowwise(gate_bwd, [(dz, POOL_W, 0), (ylin, POOL_W, 0), (proj, POOL_W, 1),
                                                    (pool_scale, POOL_W, 0)],
                                         [(POOL_W, BF16), (POOL_W, BF16, ODD_IN, 1)], [POOL_W], name="odd_gate_bwd")
    dmixed = _gmm(dylin, pool_w, "nt", F32)
    dpool_w = _gmm(mixed, dylin, "tn", BF16)
    dproj = _pool(dmixed, 0, True, BF16, into=dproj_gate)
    dw_in = _mm(h, dproj, "tn", BF16, out_blocks=True)
    sent = on_w(dict(w_in=dw_in, w_out=dw_out, pool_w=dpool_w))
    dh = _mm(dproj, w_in, "nt", F32, b_blocks=True, after=(sent,))
    g_prev, dpre = _pre_bwd(g, dh, x, pre)
    return g_prev, dict(pre=dpre, post=dpost, pool_scale=dscale)


def _my_index():
    return 4 * lax.axis_index("x") + 2 * lax.axis_index("y") + lax.axis_index("c")


HBM_SPEC = pl.BlockSpec(memory_space=pltpu.HBM)
SEM_SPEC = pl.BlockSpec(memory_space=pltpu.SEMAPHORE)
SPLIT_EFFECT = pltpu.SideEffectType.DATAFLOW_SIDE_EFFECTING


def _device_of(j):
    return (j // 4, (j // 2) % 2, j % 2)


def _split_copy(srcs, lands, send_sems, recv_sems, gather, i, j, dst_slot, recv_slot):
    return pltpu.make_async_remote_copy(
        src_ref=srcs[i] if gather else srcs[i].at[j], dst_ref=lands[i].at[dst_slot],
        send_sem=send_sems.at[i * N_DEV + j], recv_sem=recv_sems.at[i * N_DEV + recv_slot],
        device_id=_device_of(j), device_id_type=MESH_ID)


def _own_copy(srcs, lands, send_sems, gather, i, me):
    return pltpu.make_async_copy(srcs[i] if gather else srcs[i].at[me], lands[i].at[me], send_sems.at[i * N_DEV + me])


def _xchg_start(name, srcs, gather, after=()):
    n = len(srcs)
    n_in = n + len(after)

    def body(*refs):
        src_refs = refs[:n]
        send_sems, recv_sems, token = refs[n_in], refs[n_in + 1], refs[-1]
        land_refs = refs[n_in + 2 + n:n_in + 2 + 2 * n]
        me = _my_index()
        for j in range(N_DEV):
            @pl.when(me != j)
            def _(j=j):
                for i in range(n):
                    _split_copy(src_refs, land_refs, send_sems, recv_sems, gather, i, j, me, me).start()
        for i in range(n):
            _own_copy(src_refs, land_refs, send_sems, gather, i, me).start()
        token[...] = jnp.zeros_like(token)

    land_shapes = [((N_DEV,) + a.shape) if gather else a.shape for a in srcs]
    thru = ([pltpu.HBM(a.shape, a.dtype) for a in srcs] + [pltpu.HBM(s, a.dtype) for s, a in zip(land_shapes, srcs)])
    res = pl.pallas_call(
        body, name=name,
        out_shape=(pltpu.SemaphoreType.DMA((n * N_DEV,)), pltpu.SemaphoreType.DMA((n * N_DEV,)), *thru,
                   SDS((8, LANES), F32)),
        in_specs=[HBM_SPEC] * n + [pl.BlockSpec(memory_space=pl.ANY)] * len(after),
        out_specs=(SEM_SPEC, SEM_SPEC, *([HBM_SPEC] * (2 * n)), pl.BlockSpec(memory_space=pltpu.VMEM)),
        input_output_aliases={i: 2 + i for i in range(n)},
        compiler_params=pltpu.CompilerParams(has_side_effects=SPLIT_EFFECT),
    )(*[pltpu.with_memory_space_constraint(a, pltpu.HBM) for a in srcs], *after)
    return res[0], res[1], list(res[2:2 + n]), list(res[2 + n:2 + 2 * n]), res[-1]


def _xchg_wait(name, started, gather, after):
    send_sems, recv_sems, srcs, lands, _ = started
    n = len(srcs)

    def body(*refs):
        src_refs, land_refs = refs[:n], refs[n:2 * n]
        send_r, recv_r = refs[2 * n], refs[2 * n + 1]
        me = _my_index()
        for j in range(N_DEV):
            @pl.when(me != j)
            def _(j=j):
                for i in range(n):
                    _split_copy(src_refs, land_refs, send_r, recv_r, gather, i, j, me, me).wait_send()
                    _split_copy(src_refs, land_refs, send_r, recv_r, gather, i, j, j, j).wait_recv()
        for i in range(n):
            _own_copy(src_refs, land_refs, send_r, gather, i, me).wait()

    thru = [pltpu.HBM(a.shape, a.dtype) for a in list(srcs) + list(lands)]
    res = pl.pallas_call(
        body, name=name, out_shape=tuple(thru),
        in_specs=[HBM_SPEC] * (2 * n) + [SEM_SPEC, SEM_SPEC] + [pl.BlockSpec(memory_space=pl.ANY)] * len(after),
        out_specs=tuple([HBM_SPEC] * (2 * n)),
        input_output_aliases={i: i for i in range(2 * n)},
        compiler_params=pltpu.CompilerParams(has_side_effects=SPLIT_EFFECT),
    )(*srcs, *lands, send_sems, recv_sems, *after)
    return list(res[n:])


def _adam_layers(w, slot_list, m, v, name):
    n_l, r, c = w.shape
    ns = slot_list[0].shape[0]
    tr = r
    while tr * c * 4 > (1 << 20) and tr % 16 == 0:
        tr //= 2
    assert r % tr == 0 and len(slot_list) == n_l

    def body(*refs):
        w_ref, slot_refs = refs[0], refs[1:1 + n_l]
        m_ref, v_ref, go_ref, d_ref, mo_ref, vo_ref = refs[1 + n_l:]
        layer = pl.program_id(0)
        g = None
        for l, g_ref in enumerate(slot_refs):
            gl = g_ref[0].astype(F32)
            for s in range(1, ns):
                gl = gl + g_ref[s].astype(F32)
            g = gl if g is None else jnp.where(layer == l, gl, g)
        mn = ADAM_B1 * m_ref[...] + (1.0 - ADAM_B1) * g
        vn = ADAM_B2 * v_ref[...] + (1.0 - ADAM_B2) * (g * g)
        m_hat = mn / (1.0 - ADAM_B1 ** ADAM_STEP)
        v_hat = vn / (1.0 - ADAM_B2 ** ADAM_STEP)
        go_ref[...] = g
        d_ref[...] = -ADAM_LR * (m_hat / (jnp.sqrt(v_hat) + ADAM_EPS) + ADAM_WD * w_ref[...])
        mo_ref[...] = mn
        vo_ref[...] = vn

    blk = pl.BlockSpec((None, tr, c), lambda l, i: (l, i, 0))
    slot_specs = [pl.BlockSpec((ns, tr, c), lambda l, i, k=k: (0, jnp.where(l == k, i, 0), 0)) for k in range(n_l)]
    return pl.pallas_call(
        body, name=name, grid=(n_l, r // tr),
        in_specs=[blk] + slot_specs + [blk, blk],
        out_specs=[blk] * 4, out_shape=[SDS((n_l, r, c), F32)] * 4,
        compiler_params=_cparams(("arbitrary", "arbitrary")),
    )(*_in_hbm((w, *slot_list, m, v)))


def _adam(w, gslots, m, v, name):
    r, c = w.shape
    ns = gslots.shape[0]
    tr = r
    while tr * c * 4 > (1 << 20) and tr % 16 == 0:
        tr //= 2
    assert r % tr == 0

    def body(w_ref, g_ref, m_ref, v_ref, go_ref, d_ref, mo_ref, vo_ref):
        g = g_ref[0].astype(F32)
        for s in range(1, ns):
            g = g + g_ref[s].astype(F32)
        wv = w_ref[...]
        mn = ADAM_B1 * m_ref[...] + (1.0 - ADAM_B1) * g
        vn = ADAM_B2 * v_ref[...] + (1.0 - ADAM_B2) * (g * g)
        m_hat = mn / (1.0 - ADAM_B1 ** ADAM_STEP)
        v_hat = vn / (1.0 - ADAM_B2 ** ADAM_STEP)
        go_ref[...] = g
        d_ref[...] = -ADAM_LR * (m_hat / (jnp.sqrt(v_hat) + ADAM_EPS) + ADAM_WD * wv)
        mo_ref[...] = mn
        vo_ref[...] = vn

    blk = pl.BlockSpec((tr, c), lambda i: (i, 0))
    return pl.pallas_call(
        body, name=name, grid=(r // tr,),
        in_specs=[blk, pl.BlockSpec((ns, tr, c), lambda i: (0, i, 0)), blk, blk],
        out_specs=[blk] * 4, out_shape=[SDS((r, c), F32)] * 4,
        compiler_params=_cparams(("parallel",)),
    )(w, gslots, m, v)


def _sum_slots(slots, name):
    ns, r, c = slots.shape

    def body(g_ref, o_ref):
        g = g_ref[0]
        for s in range(1, ns):
            g = g + g_ref[s]
        o_ref[...] = g

    return pl.pallas_call(
        body, name=name, grid=(1,),
        in_specs=[pl.BlockSpec((ns, r, c), lambda i: (0, 0, 0))], out_specs=pl.BlockSpec((r, c), lambda i: (0, 0)),
        out_shape=SDS((r, c), F32), compiler_params=_cparams(("arbitrary",)),
    )(slots)


def _adam_params(params, name):
    n = len(params)

    def body(*refs):
        ins, outs = refs[:5 * n], refs[5 * n:]
        for p in range(n):
            w_ref, m_ref, v_ref, g_first, g_rest = ins[5 * p:5 * p + 5]
            go_ref, d_ref, mo_ref, vo_ref = outs[4 * p:4 * p + 4]
            for part, g_ref in ((slice(0, 1), g_first), (slice(1, w_ref.shape[0]), g_rest)):
                g = g_ref[...]
                mn = ADAM_B1 * m_ref[part] + (1.0 - ADAM_B1) * g
                vn = ADAM_B2 * v_ref[part] + (1.0 - ADAM_B2) * (g * g)
                m_hat = mn / (1.0 - ADAM_B1 ** ADAM_STEP)
                v_hat = vn / (1.0 - ADAM_B2 ** ADAM_STEP)
                go_ref[part] = g
                d_ref[part] = -ADAM_LR * (m_hat / (jnp.sqrt(v_hat) + ADAM_EPS) + ADAM_WD * w_ref[part])
                mo_ref[part] = mn
                vo_ref[part] = vn

    def whole(a):
        return pl.BlockSpec(a.shape, lambda i, nd=a.ndim: (0,) * nd)

    flat = [pltpu.with_memory_space_constraint(a, pltpu.HBM) for prm in params for a in prm]
    outs = pl.pallas_call(
        body, name=name, grid=(1,),
        in_specs=[whole(a) for a in flat],
        out_specs=[whole(prm[0]) for prm in params for _ in range(4)],
        out_shape=[SDS(prm[0].shape, F32) for prm in params for _ in range(4)],
        compiler_params=_cparams(("arbitrary",)),
    )(*flat)
    return [outs[4 * p:4 * p + 4] for p in range(n)]


SMALL_NAMES = ("pre_norm", "post_norm", "ssm_a_re", "ssm_a_im", "ssm_log_dt", "ssm_b_re", "ssm_b_im", "ssm_c_re",
               "ssm_c_im", "ssm_d", "ssm_glu_b")
SSM_NAMES = ("ssm_a_re", "ssm_a_im", "ssm_log_dt", "ssm_b_re", "ssm_b_im", "ssm_c_re", "ssm_c_im")
SHARDED_NAMES = ("even_w_in", "even_w_out", "ssm_glu_w", "odd_w_in", "pool_w", "odd_w_out")
WEIGHT_ORDER = ("pre_norm", "post_norm", "even_w_in", "even_w_out", "ssm_a_re", "ssm_a_im", "ssm_log_dt", "ssm_b_re",
                "ssm_b_im", "ssm_c_re", "ssm_c_im", "ssm_d", "ssm_glu_w", "ssm_glu_b", "odd_w_in", "pool_w",
                "pool_scale", "odd_w_out")
PACK_ROWS_ALIGN = 8


def _pack(parts):
    flat = jnp.concatenate([p.reshape(-1).astype(F32) for p in parts])
    rows = -(-flat.shape[0] // (LANES * PACK_ROWS_ALIGN)) * PACK_ROWS_ALIGN
    return jnp.pad(flat, (0, rows * LANES - flat.shape[0])).reshape(rows, LANES)


def _unpack(packed, shapes):
    flat = packed.reshape(-1)
    out, off = [], 0
    for shp in shapes:
        size = math.prod(shp)
        out.append(flat[off:off + size].reshape(shp))
        off += size
    return out


EVEN_SHARDED = ("w_in", "w_out", "glu_w")
ODD_SHARDED = ("w_in", "pool_w", "w_out")
FAMILY = {(0, "w_in"): "even_w_in", (0, "w_out"): "even_w_out", (0, "glu_w"): "ssm_glu_w",
          (1, "w_in"): "odd_w_in", (1, "pool_w"): "pool_w", (1, "w_out"): "odd_w_out"}


def _sharded_keys(layer):
    return EVEN_SHARDED if layer % 2 == 0 else ODD_SHARDED


def _local_step(x, tgt, small, get_weights, on_w, on_ssm, on_grads, zero=0.0):
    tables = _rope_tables(zero) + (_attention_bias(zero),)
    preps, prep_vjps = [], []
    for i in range(2):
        out, vjp = jax.vjp(_ssm_prep, small["ssm_a_re"][i] + zero, small["ssm_a_im"][i], small["ssm_log_dt"][i],
                           small["ssm_b_re"][i], small["ssm_b_im"][i], small["ssm_c_re"][i], small["ssm_c_im"][i])
        preps.append(out)
        prep_vjps.append(vjp)

    def layer_args(layer, wts):
        i = layer // 2
        pre, post = _row(small["pre_norm"][layer]) + wts.get("token", 0.0), _row(small["post_norm"][layer])
        if layer % 2 == 0:
            return (pre, post, wts["w_in"], wts["late"], _row(small["ssm_glu_b"][i]), _row(small["ssm_d"][i]),
                    preps[i], tables)
        return (pre, post, wts["w_in"], wts["pool_w"], _row(wts["pool_scale"]), wts["w_out"])

    saved, args = [], []
    cur = x
    for layer in range(4):
        after = (cur,) if layer else (cur, tables[0], tables[3], preps[0][1], preps[0][2], preps[1][1], preps[1][2])
        args.append(layer_args(layer, get_weights(layer, after)))
        if layer == 0:
            h = _norm_fwd(cur, args[0][0])
        if layer < 3:
            def tail(xv, yv, post, next_gain=_row(small["pre_norm"][layer + 1])):
                return tuple(_post_fwd(xv, yv, post, next_gain))
        else:
            def tail(xv, yv, post):
                return tuple(_post_fwd_loss(xv, yv, post, tgt))
        cur, h, sv = (_even_fwd if layer % 2 == 0 else _odd_fwd)(cur, h, tail, *args[layer])
        saved.append(sv)
    g, sq = cur, h
    loss = 0.5 * jnp.sum(sq) / D

    lg = [None] * 4
    token = jnp.zeros((), F32)
    for layer in reversed(range(4)):
        largs = list(args[layer])
        largs[1] = largs[1] + token
        hooks = dict(on_w=functools.partial(on_w, layer))
        ssm_grads = []
        if layer % 2 == 0:
            def ssm_hook(cotangents, layer=layer):
                ssm_grads.append(prep_vjps[layer // 2](cotangents))
                return on_ssm(layer, ssm_grads[0])

            hooks["on_ssm"] = ssm_hook
        g, lg[layer] = (_even_bwd if layer % 2 == 0 else _odd_bwd)(g, saved[layer], *largs, **hooks)
        if ssm_grads:
            lg[layer]["ssm"] = ssm_grads[0]
        token = on_grads(layer, lg[layer])
    return loss, g, token


def _to_slots(key, gfull):
    if key == "w_in":
        return gfull
    if key in ("w_out", "glu_w"):
        rr, nn = gfull.shape
        return gfull.reshape(N_DEV, rr // N_DEV, nn)
    assert key == "pool_w"
    gg, rr, nn = gfull.shape
    return gfull.reshape(gg, N_DEV, rr // N_DEV, nn).transpose(1, 0, 2, 3)


def _from_gathered(key, gat):
    if key == "w_in":
        return gat
    if key in ("w_out", "glu_w"):
        _, rr, nn = gat.shape
        return gat.reshape(N_DEV * rr, nn)
    assert key == "pool_w"
    _, gg, rr, nn = gat.shape
    return gat.transpose(1, 0, 2, 3).reshape(gg, N_DEV * rr, nn)


def kernel(x, pre_norm, post_norm, even_w_in, even_w_out, ssm_a_re, ssm_a_im, ssm_log_dt, ssm_b_re, ssm_b_im, ssm_c_re, ssm_c_im, ssm_d, ssm_glu_w, ssm_glu_b, odd_w_in, pool_w, pool_scale, odd_w_out, loss_target, m_pre_norm, m_post_norm, m_even_w_in, m_even_w_out, m_ssm_a_re, m_ssm_a_im, m_ssm_log_dt, m_ssm_b_re, m_ssm_b_im, m_ssm_c_re, m_ssm_c_im, m_ssm_d, m_ssm_glu_w, m_ssm_glu_b, m_odd_w_in, m_pool_w, m_pool_scale, m_odd_w_out, v_pre_norm, v_post_norm, v_even_w_in, v_even_w_out, v_ssm_a_re, v_ssm_a_im, v_ssm_log_dt, v_ssm_b_re, v_ssm_b_im, v_ssm_c_re, v_ssm_c_im, v_ssm_d, v_ssm_glu_w, v_ssm_glu_b, v_odd_w_in, v_pool_w, v_pool_scale, v_odd_w_out):
    w = dict(pre_norm=pre_norm, post_norm=post_norm, even_w_in=even_w_in, even_w_out=even_w_out, ssm_a_re=ssm_a_re,
             ssm_a_im=ssm_a_im, ssm_log_dt=ssm_log_dt, ssm_b_re=ssm_b_re, ssm_b_im=ssm_b_im, ssm_c_re=ssm_c_re,
             ssm_c_im=ssm_c_im, ssm_d=ssm_d, ssm_glu_w=ssm_glu_w, ssm_glu_b=ssm_glu_b, odd_w_in=odd_w_in,
             pool_w=pool_w, pool_scale=pool_scale, odd_w_out=odd_w_out)
    mom = dict(pre_norm=m_pre_norm, post_norm=m_post_norm, even_w_in=m_even_w_in, even_w_out=m_even_w_out,
               ssm_a_re=m_ssm_a_re, ssm_a_im=m_ssm_a_im, ssm_log_dt=m_ssm_log_dt, ssm_b_re=m_ssm_b_re,
               ssm_b_im=m_ssm_b_im, ssm_c_re=m_ssm_c_re, ssm_c_im=m_ssm_c_im, ssm_d=m_ssm_d, ssm_glu_w=m_ssm_glu_w,
               ssm_glu_b=m_ssm_glu_b, odd_w_in=m_odd_w_in, pool_w=m_pool_w, pool_scale=m_pool_scale,
               odd_w_out=m_odd_w_out)
    var = dict(pre_norm=v_pre_norm, post_norm=v_post_norm, even_w_in=v_even_w_in, even_w_out=v_even_w_out,
               ssm_a_re=v_ssm_a_re, ssm_a_im=v_ssm_a_im, ssm_log_dt=v_ssm_log_dt, ssm_b_re=v_ssm_b_re,
               ssm_b_im=v_ssm_b_im, ssm_c_re=v_ssm_c_re, ssm_c_im=v_ssm_c_im, ssm_d=v_ssm_d, ssm_glu_w=v_ssm_glu_w,
               ssm_glu_b=v_ssm_glu_b, odd_w_in=v_odd_w_in, pool_w=v_pool_w, pool_scale=v_pool_scale,
               odd_w_out=v_odd_w_out)
    me = _my_index()
    scale_cols = pool_scale.shape[1]

    def start_gather(tag, layer, keys, after=()):
        i = layer // 2
        shards = [w[FAMILY[(layer % 2, k)]][i].astype(BF16) for k in keys]
        if layer % 2 == 1:
            shards.append(jnp.pad(pool_scale[i][None], ((0, PACK_ROWS_ALIGN - 1), (0, 0))))
        return _xchg_start(f"gather_start_{tag}", shards, True, after)

    gather_started = {0: start_gather("0", 0, EVEN_SHARDED[:1])}
    small = {nm: w[nm] for nm in SMALL_NAMES}

    def get_weights(layer, after):
        keys = EVEN_SHARDED[:1] if layer == 0 else _sharded_keys(layer)
        lands = _xchg_wait(f"gather_wait_{layer}", gather_started[layer], True, after)
        wts = {k: _from_gathered(k, gat) for k, gat in zip(keys, lands)}
        if layer % 2 == 1:
            wts["pool_scale"] = lands[-1][:, 0, :].reshape(N_DEV * scale_cols)
        if layer == 0:
            prev = gather_started["0_late"] = start_gather("0_late", 0, EVEN_SHARDED[1:], after=(lands[0],))
            for later in (1, 2, 3):
                prev = gather_started[later] = start_gather(str(later), later, _sharded_keys(later), after=(prev[4],))
            wts["token"] = sum(gather_started[tag][4][0, 0] for tag in ("0_late", 1, 2, 3))

            def late(after_late):
                late_lands = _xchg_wait("gather_wait_0_late", gather_started["0_late"], True, (after_late,))
                return tuple(_from_gathered(k, gat) for k, gat in zip(EVEN_SHARDED[1:], late_lands))

            wts["late"] = late
        elif layer == 2:
            wts["late"] = lambda after_late: (wts["w_out"], wts["glu_w"])
        return wts

    scatter_started = []

    def on_w(layer, gw):
        keys = tuple(k for k in _sharded_keys(layer) if k in gw)
        started = _xchg_start(f"scatter_start_{layer}_{keys[0]}", [_to_slots(k, gw[k]) for k in keys], False)
        scatter_started.append((layer, keys, started))
        return started[4]

    def wait_scatters(layers, after):
        for layer, keys, started in scatter_started:
            if layer in layers:
                lands = _xchg_wait(f"scatter_wait_{layer}_{keys[0]}", started, False, after)
                for k, land in zip(keys, lands):
                    recv[(layer, k)] = land

    packed_names = ("pre_norm", "post_norm") + SSM_NAMES + ("ssm_d", "ssm_glu_b")
    tails = {nm: (SSM_GROUPS, SSM_STATE * SSM_GROUP) if nm in ("ssm_b_re", "ssm_b_im") else w[nm].shape[1:]
             for nm in packed_names}

    layer_grads = {}
    early_started, mid_started = [], []

    def on_ssm(layer, ssm_grads):
        if layer != 0:
            return None
        mid_started.append(_xchg_start("mid_start", [_pack(list(ssm_grads))], True))
        return mid_started[0][4]

    def on_grads(layer, lg):
        layer_grads[layer] = lg
        zero = jnp.zeros((), F32)
        if layer == 1:
            lgs = layer_grads
            early = ([jnp.concatenate([lgs[l][k] for l in (1, 2, 3)], axis=0) for k in ("pre", "post")]
                     + list(lgs[2]["ssm"]) + [lgs[2]["ssm_d"], lgs[2]["glu_b"],
                                              jnp.concatenate([lgs[1]["pool_scale"], lgs[3]["pool_scale"]], axis=0)])
            early_started.append(_xchg_start("small_start", [_pack(early)], True))
            zero = zero + early_started[0][4][0, 0]
        return zero

    loss_local, grad_x, token = _local_step(x[0], loss_target[0], small, get_weights, on_w, on_ssm, on_grads,
                                            zero=gather_started[0][4][0, 0])

    lg0 = layer_grads[0]
    late_started = _xchg_start("late_start", [_pack([lg0["pre"], lg0["post"], lg0["ssm_d"], lg0["glu_b"],
                                                     loss_local.reshape(1)]) + token], True)

    def adam_family(parity, k):
        nm = FAMILY[(parity, k)]
        shp = w[nm].shape
        cols = shp[-1]
        slot_list = [recv[(parity + 2 * i, k)].reshape(N_DEV, -1, cols) for i in range(2)]
        outs = _adam_layers(w[nm].reshape(2, -1, cols), slot_list, mom[nm].reshape(2, -1, cols),
                            var[nm].reshape(2, -1, cols), name=f"adam_{nm}")
        return [o.reshape(shp) for o in outs]

    recv, res = {}, {}
    wait_scatters((3, 1), (late_started[4],))
    for k in ODD_SHARDED:
        res[FAMILY[(1, k)]] = adam_family(1, k)
    odd_done = tuple(res[FAMILY[(1, k)]][0] for k in ODD_SHARDED)

    (early_slots,) = _xchg_wait("small_wait", early_started[0], True, odd_done)
    (mid_slots,) = _xchg_wait("mid_wait", mid_started[0], True, odd_done)
    early_shapes = [(w[nm].shape[0] - 1,) + tails[nm] for nm in packed_names] + [(2, N_DEV * scale_cols)]
    g_early = _unpack(_sum_slots(early_slots, "sum_small_early"), early_shapes)
    g_mid = _unpack(_sum_slots(mid_slots, "sum_small_mid"), [(1,) + tails[nm] for nm in SSM_NAMES])

    (late_slots,) = _xchg_wait("late_wait", late_started, True, (g_early[0], g_mid[0]))
    wait_scatters((2, 0), (late_slots,))
    for k in EVEN_SHARDED:
        res[FAMILY[(0, k)]] = adam_family(0, k)

    late_names = ("pre_norm", "post_norm", "ssm_d", "ssm_glu_b")
    g_late = _unpack(_sum_slots(late_slots, "sum_small_late"), [(1,) + tails[nm] for nm in late_names] + [(1,)])
    g_first = dict(zip(late_names, g_late))
    g_first.update(zip(SSM_NAMES, g_mid))
    dense = lambda nm, a: a.reshape((a.shape[0],) + tails[nm])
    outs = _adam_params([(dense(nm, w[nm]), dense(nm, mom[nm]), dense(nm, var[nm]), g_first[nm], g_early[j])
                         for j, nm in enumerate(packed_names)], "adam_small")
    for nm, four in zip(packed_names, outs):
        res[nm] = [o.reshape(w[nm].shape) for o in four]
    loss = g_late[-1].reshape(())
    g_scale = lax.dynamic_slice_in_dim(g_early[-1], me * scale_cols, scale_cols, axis=1)
    pad = ((0, PACK_ROWS_ALIGN - 2), (0, 0))
    outs = _adam(jnp.pad(pool_scale, pad), jnp.pad(g_scale, pad)[None], jnp.pad(m_pool_scale, pad),
                 jnp.pad(v_pool_scale, pad), name="adam_pool_scale")
    res["pool_scale"] = [o[:2] for o in outs]

    out = [loss, grad_x[None]]
    for kind in range(4):
        out += [res[nm][kind] for nm in WEIGHT_ORDER]
    return tuple(out)
```

```python
import functools
import math

import jax
import jax.numpy as jnp
from jax import lax
from jax.experimental import pallas as pl
from jax.experimental.pallas import tpu as pltpu

F32 = jnp.float32
BF16 = jnp.bfloat16
SDS = jax.ShapeDtypeStruct

N_DEV = 8
S = 2048
D = 1024
HEAD_DIM = 64
ROT_DIM = 16
ROPE_THETA = 500000.0
ATT_W = 1024
SSM_W = 512
SSM_GROUPS = 32
SSM_GROUP = 16
SSM_STATE = 64
N_CPLX = SSM_GROUPS * SSM_STATE
POOL_W = 2048
POOL_GROUP = 512
EVEN_IN = 5120
EVEN_OUT = 1536
ODD_IN = 4096
RMS_EPS = 1e-6
LANES = 128
VMEM_LIMIT = 48 * 1024 * 1024

ADAM_LR = 0.001
ADAM_B1 = 0.9
ADAM_B2 = 0.999
ADAM_EPS = 1e-08
ADAM_WD = 0.01
ADAM_STEP = 10

MESH_ID = pl.DeviceIdType.MESH
NN = (((1,), (0,)), ((), ()))
NT = (((1,), (1,)), ((), ()))
TN = (((0,), (0,)), ((), ()))
_DN = {"nn": NN, "nt": NT, "tn": TN}


def _cparams(sem):
    return pltpu.CompilerParams(dimension_semantics=sem, vmem_limit_bytes=VMEM_LIMIT)


def _in_hbm(arrs):
    return [pltpu.with_memory_space_constraint(a, pltpu.HBM) for a in arrs]


MM_TILES = (1024, 768, 512)


def _tile(dim):
    return next((t for t in MM_TILES if dim % t == 0), dim)


NT_BLOCKS_PER_STEP = 2
MAX_WHOLE_K = 2048


def _mm(a, b, mode, out_dtype, b_blocks=False, out_blocks=False, a_cols=None, after=()):
    if b_blocks:
        nblk, rows, cb = b.shape
        b2_shape = (rows, nblk * cb)
    else:
        b2_shape = b.shape
    a_shape = a.shape if a_cols is None else (a.shape[0], a_cols[1])
    if mode == "nn":
        (m, k), n = a_shape, b2_shape[1]
    elif mode == "nt":
        (m, k), n = a_shape, b2_shape[0]
    else:
        (k, m), n = a_shape, b2_shape[1]
    tm, tn, tk = _tile(m), _tile(n), _tile(k)
    if k <= MAX_WHOLE_K:
        tk = k
    per_step = 1
    if b_blocks and mode == "nn":
        tn = cb
        tm = m
    if b_blocks and mode == "nt":
        per_step = NT_BLOCKS_PER_STEP
        tk = per_step * cb
    if out_blocks:
        tn = n // N_DEV
        tk = k
    nk = k // tk
    a_unit = tm if mode == "tn" else tk
    assert a_cols is None or a_cols[0] % a_unit == 0
    a_off = 0 if a_cols is None else a_cols[0] // a_unit

    def body(a_ref, b_ref, *rest):
        o_ref, acc_ref = rest[-2:]
        kk = pl.program_id(2)
        if per_step == 1:
            part = lax.dot_general(a_ref[...].astype(BF16), b_ref[...].astype(BF16), _DN[mode],
                                   preferred_element_type=F32)
        else:
            part = None
            for blk in range(per_step):
                d = lax.dot_general(a_ref[:, blk * cb:(blk + 1) * cb].astype(BF16), b_ref[blk].astype(BF16), NT,
                                    preferred_element_type=F32)
                part = d if part is None else part + d
        if nk == 1:
            o_ref[...] = part.astype(o_ref.dtype)
            return

        @pl.when(kk == 0)
        def _():
            acc_ref[...] = part

        @pl.when((kk > 0) & (kk < nk - 1))
        def _():
            acc_ref[...] += part

        @pl.when(kk == nk - 1)
        def _():
            o_ref[...] = (acc_ref[...] + part).astype(o_ref.dtype)

    if mode == "nn":
        a_spec = pl.BlockSpec((tm, tk), lambda i, j, kk: (i, a_off + kk))
        b_spec = pl.BlockSpec((tk, tn), lambda i, j, kk: (kk, j))
    elif mode == "nt":
        a_spec = pl.BlockSpec((tm, tk), lambda i, j, kk: (i, a_off + kk))
        b_spec = pl.BlockSpec((tn, tk), lambda i, j, kk: (j, kk))
    else:
        a_spec = pl.BlockSpec((tk, tm), lambda i, j, kk: (kk, a_off + i))
        b_spec = pl.BlockSpec((tk, tn), lambda i, j, kk: (kk, j))
    if b_blocks and mode == "nn":
        b_spec = pl.BlockSpec((None, tk, cb), lambda i, j, kk: (j, kk, 0))
    if b_blocks and mode == "nt":
        b_spec = pl.BlockSpec((per_step, tn, cb), lambda i, j, kk: (kk, j, 0))
    out_spec = pl.BlockSpec((tm, tn), lambda i, j, kk: (i, j))
    out_shape = SDS((m, n), out_dtype)
    if out_blocks:
        out_spec = pl.BlockSpec((None, tm, tn), lambda i, j, kk: (j, i, 0))
        out_shape = SDS((N_DEV, m, tn), out_dtype)
    return pl.pallas_call(
        body, name=f"mm_{mode}_{m}x{k}x{n}",
        grid=(m // tm, n // tn, nk),
        in_specs=[a_spec, b_spec] + [pl.BlockSpec(memory_space=pl.ANY)] * len(after),
        out_specs=out_spec,
        out_shape=out_shape,
        scratch_shapes=[pltpu.VMEM((tm, tn) if nk > 1 else (8, LANES), F32)],
        compiler_params=_cparams(("parallel", "parallel", "arbitrary")),
    )(a, b, *after)


def _gmm(a, b, mode, out_dtype, tm=S):
    ng, gw = POOL_W // POOL_GROUP, POOL_GROUP
    ns = S // tm
    if mode in ("nn", "nt"):
        def body(a_ref, b_ref, o_ref):
            o_ref[...] = lax.dot_general(a_ref[...].astype(BF16), b_ref[...].astype(BF16), _DN[mode],
                                         preferred_element_type=F32).astype(o_ref.dtype)

        return pl.pallas_call(
            body, name=f"gmm_{mode}", grid=(ng, ns),
            in_specs=[pl.BlockSpec((tm, gw), lambda g, i: (i, g)),
                      pl.BlockSpec((None, gw, gw), lambda g, i: (g, 0, 0))],
            out_specs=pl.BlockSpec((tm, gw), lambda g, i: (i, g)),
            out_shape=SDS((S, POOL_W), out_dtype),
            compiler_params=_cparams(("parallel", "parallel")),
        )(a, b)

    def body_tn(a_ref, b_ref, o_ref, acc_ref):
        i = pl.program_id(1)

        @pl.when(i == 0)
        def _():
            acc_ref[...] = jnp.zeros_like(acc_ref)

        acc_ref[...] += lax.dot_general(a_ref[...].astype(BF16), b_ref[...].astype(BF16), TN,
                                        preferred_element_type=F32)

        @pl.when(i == ns - 1)
        def _():
            o_ref[...] = acc_ref[...].astype(o_ref.dtype)

    return pl.pallas_call(
        body_tn, name="gmm_tn", grid=(ng, ns),
        in_specs=[pl.BlockSpec((tm, gw), lambda g, i: (i, g)),
                  pl.BlockSpec((tm, gw), lambda g, i: (i, g))],
        out_specs=pl.BlockSpec((None, gw, gw), lambda g, i: (g, 0, 0)),
        out_shape=SDS((ng, gw, gw), out_dtype),
        scratch_shapes=[pltpu.VMEM((gw, gw), F32)],
        compiler_params=_cparams(("parallel", "arbitrary")),
    )(a, b)


def _rowwise(fn, inputs, out_defs, acc_defs=(), tm=512, name=None, after=()):
    n_in, n_out, n_acc = len(inputs), len(out_defs), len(acc_defs)
    n_after = len(after)
    in_specs, args = [], []
    for arr, width, cb in inputs:
        if arr.shape[0] == 1:
            in_specs.append(pl.BlockSpec((1, width), lambda i, cb=cb: (0, cb)))
        else:
            in_specs.append(pl.BlockSpec((tm, width), lambda i, cb=cb: (i, cb)))
        args.append(arr)
    out_defs = [d if len(d) == 4 else (d[0], d[1], d[0], 0) for d in out_defs]
    out_shape = [SDS((S, ww), dt) for _, dt, ww, _ in out_defs] + [SDS((1, w), F32) for w in acc_defs]
    out_specs = ([pl.BlockSpec((tm, w), lambda i, cb=cb: (i, cb)) for w, _, _, cb in out_defs]
                 + [pl.BlockSpec((1, w), lambda i: (0, 0)) for w in acc_defs])

    def kern(*refs):
        vals = [r[...] for r in refs[:n_in]]
        outs, accs = fn(*vals)
        out_refs = refs[n_in + n_after:]
        for r, v in zip(out_refs[:n_out], outs):
            r[...] = v.astype(r.dtype)
        if n_acc:
            acc_refs = out_refs[n_out:]

            @pl.when(pl.program_id(0) == 0)
            def _():
                for r in acc_refs:
                    r[...] = jnp.zeros_like(r)

            for r, v in zip(acc_refs, accs):
                r[...] += jnp.sum(v, axis=0, keepdims=True)

    res = pl.pallas_call(
        kern, name=name, grid=(S // tm,), in_specs=in_specs + [pl.BlockSpec(memory_space=pl.ANY)] * n_after,
        out_specs=out_specs, out_shape=out_shape, compiler_params=_cparams(("arbitrary",)),
    )(*args, *after)
    return res


def _sigmoid(x):
    return 1.0 / (1.0 + jnp.exp(-x))


def _silu_and_grad(x):
    s = _sigmoid(x)
    return x * s, s * (1.0 + x * (1.0 - s))


_GELU_K = math.sqrt(2.0 / math.pi)
_GELU_C = 0.044715


def _gelu_and_grad(x):
    t = jnp.tanh(_GELU_K * (x + _GELU_C * (x * x * x)))
    cdf = 0.5 * (1.0 + t)
    grad = cdf + 0.5 * x * (1.0 - t * t) * (_GELU_K * (1.0 + 3.0 * _GELU_C * x * x))
    return x * cdf, grad


def _rms(xv, gain):
    r = lax.rsqrt(jnp.mean(xv * xv, axis=-1, keepdims=True) + RMS_EPS)
    return xv * r * gain


def _rms_bwd(dout, xv, gain):
    r = lax.rsqrt(jnp.mean(xv * xv, axis=-1, keepdims=True) + RMS_EPS)
    xhat = xv * r
    dxhat = dout * gain
    dx = r * (dxhat - xhat * jnp.mean(dxhat * xhat, axis=-1, keepdims=True))
    return dx, dout * xhat


def _norm_fwd(x, gain):
    (h,) = _rowwise(lambda xv, g: ((_rms(xv, g),), ()), [(x, D, 0), (gain, D, 0)], [(D, BF16)], name="norm_fwd")
    return h


def _post_fwd(x, y, gain, next_gain):
    def fn(xv, yv, g, gn):
        out = xv + _rms(yv, g)
        return (out, _rms(out, gn)), ()

    return _rowwise(fn, [(x, D, 0), (y, D, 0), (gain, D, 0), (next_gain, D, 0)], [(D, F32), (D, BF16)],
                    name="post_fwd")


def _post_fwd_loss(x, y, gain, tgt):
    def fn(xv, yv, g, tv):
        e = xv + _rms(yv, g) - tv
        return (e * (1.0 / D),), (e * e,)

    return _rowwise(fn, [(x, D, 0), (y, D, 0), (gain, D, 0), (tgt, D, 0)], [(D, F32)], [D], name="post_fwd_loss")


def _post_bwd(g, y, gain):
    def fn(gv, yv, gn):
        dx, dg = _rms_bwd(gv, yv, gn)
        return (dx,), (dg,)

    return _rowwise(fn, [(g, D, 0), (y, D, 0), (gain, D, 0)], [(D, BF16)], [D], name="post_bwd")


def _pre_bwd(g, dh, x, gain):
    def fn(gv, dhv, xv, gn):
        dx, dg = _rms_bwd(dhv, xv, gn)
        return (gv + dx,), (dg,)

    return _rowwise(fn, [(g, D, 0), (dh, D, 0), (x, D, 0), (gain, D, 0)], [(D, F32)], [D], name="pre_bwd")


def _pool(u_arr, col_block, transpose, out_dtype, into=None, tc=256):
    n_t = POOL_W // tc
    per_group = POOL_GROUP // tc

    def body(u_ref, *rest):
        o_ref = rest[-1]
        grp = pl.program_id(0) // per_group
        t = lax.broadcasted_iota(jnp.int32, (S, 1), 0)
        for g in range(POOL_W // POOL_GROUP):
            @pl.when(grp == g)
            def _(g=g):
                xv = u_ref[...]
                cnt = jnp.minimum(t + 1, 2 << g).astype(F32)
                cur = xv / cnt if transpose else xv
                for k in (1, 2, 4, 8)[:g + 1]:
                    if transpose:
                        cur = cur + jnp.where(t < S - k, pltpu.roll(cur, S - k, 0), 0.0)
                    else:
                        cur = cur + jnp.where(t >= k, pltpu.roll(cur, k, 0), 0.0)
                res = cur - xv if transpose else cur / cnt - xv
                o_ref[...] = res.astype(o_ref.dtype)

    in_specs = [pl.BlockSpec((S, tc), lambda c: (0, col_block * n_t + c))]
    args = [u_arr]
    if into is not None:
        in_specs.append(pl.BlockSpec(memory_space=pl.ANY))
        args.append(into)
    return pl.pallas_call(
        body, name="pool_bwd" if transpose else "pool_fwd", grid=(n_t,),
        in_specs=in_specs,
        out_specs=pl.BlockSpec((S, tc), lambda c: (0, c)),
        out_shape=SDS((S, POOL_W) if into is None else into.shape, out_dtype),
        input_output_aliases={} if into is None else {1: 0},
        compiler_params=_cparams(("parallel",)),
    )(*args)


def _rope_tables(zero):
    pos = jnp.arange(S, dtype=jnp.int32).astype(F32) + zero
    inv_freq = ROPE_THETA ** (-jnp.arange(0, ROT_DIM, 2, dtype=F32) / ROT_DIM)
    ang = pos[:, None] * inv_freq[None, :]
    cos8, sin8 = jnp.cos(ang), jnp.sin(ang)
    half = ROT_DIM // 2
    zeros = jnp.zeros((S, HEAD_DIM - ROT_DIM), F32)
    cos = jnp.concatenate([cos8, cos8, jnp.ones((S, HEAD_DIM - ROT_DIM), F32)], axis=1)
    lo = jnp.concatenate([-sin8, jnp.zeros((S, half), F32), zeros], axis=1)
    hi = jnp.concatenate([jnp.zeros((S, half), F32), sin8, zeros], axis=1)
    rep = LANES // HEAD_DIM
    return jnp.tile(cos, (1, rep)), jnp.tile(lo, (1, rep)), jnp.tile(hi, (1, rep))


def _rotate(xv, cos, lo, hi, transpose):
    width = xv.shape[1]
    rep = width // LANES
    wide = lambda tab: jnp.concatenate([tab] * rep, axis=1)
    half = ROT_DIM // 2
    up = pltpu.roll(xv, width - half, 1)
    dn = pltpu.roll(xv, half, 1)
    mixed = up * wide(lo) + dn * wide(hi)
    return xv * wide(cos) - mixed if transpose else xv * wide(cos) + mixed


def _qkv_prep(proj, tables):
    cos, lo, hi = tables

    def fn(x, c, l, h):
        rot = _rotate(x[:, :2 * ATT_W], c, l, h, False)
        return (jnp.concatenate([(rot[:, :ATT_W] * HEAD_DIM ** -0.5).astype(BF16), rot[:, ATT_W:].astype(BF16),
                                 x[:, 2 * ATT_W:].astype(BF16)], axis=1),), ()

    (qkv,) = _rowwise(fn, [(proj, 3 * ATT_W, 0), (cos, LANES, 0), (lo, LANES, 0), (hi, LANES, 0)],
                      [(3 * ATT_W, BF16)], name="qkv_prep")
    return qkv


ATT_T = 512


def _multiplicity(delta):
    ok = delta >= 0
    near = jnp.where(ok & (delta <= 128), 1.0, 0.0)
    mid = jnp.where(ok & (delta <= 512) & ((delta & 3) == 0), 1.0, 0.0)
    far = jnp.where(ok & ((delta & 15) == 0), 1.0, 0.0)
    return near + mid + far


def _attention_bias(zero):
    t = ATT_T
    pos = jnp.arange(t, dtype=jnp.int32) + jnp.asarray(zero).astype(jnp.int32)
    delta = jnp.arange(S // t, dtype=jnp.int32)[:, None, None] * t + pos[None, :, None] - pos[None, None, :]
    mult = _multiplicity(delta)
    return jnp.where(mult > 0.0, jnp.log(jnp.maximum(mult, 1.0)), -1e30).astype(F32)


def _head_split(v, first):
    zero = jnp.zeros_like(v)
    return [jnp.where(first, v, zero), jnp.where(first, zero, v)]


def _flash_fwd(qkv, bias):
    t = ATT_T
    n_hp = ATT_W // LANES

    def body(q_ref, k_ref, v_ref, b_ref, o_ref, lse_ref):
        i = pl.program_id(1)
        first = lax.broadcasted_iota(jnp.int32, (1, LANES), 1) < HEAD_DIM
        qs = _head_split(q_ref[...], first)

        def kv_step(j, carry):
            m0, l0, m1, l1, acc = carry
            off = pl.multiple_of(j * t, t)
            kb = k_ref[pl.ds(off, t), :]
            vs = _head_split(v_ref[pl.ds(off, t), :], first)
            bias_t = b_ref[i - j]
            new = []
            pv = None
            for h, (m_prev, l_prev) in enumerate(((m0, l0), (m1, l1))):
                s = lax.dot_general(qs[h], kb, NT, preferred_element_type=F32) + bias_t
                m_new = jnp.maximum(m_prev, jnp.max(s, axis=1, keepdims=True))
                p = jnp.exp(s - m_new)
                alpha = jnp.exp(m_prev - m_new)
                l_new = alpha * l_prev + jnp.sum(p, axis=1, keepdims=True)
                d = lax.dot_general(p.astype(BF16), vs[h], NN, preferred_element_type=F32)
                pv = d if pv is None else pv + d
                new.append((m_new, l_new, alpha))
            acc = acc * jnp.where(first, new[0][2], new[1][2]) + pv
            return new[0][0], new[0][1], new[1][0], new[1][1], acc

        neg = jnp.full((t, 1), -1e30, F32)
        zero = jnp.zeros((t, 1), F32)
        m0, l0, m1, l1, acc = lax.fori_loop(0, i + 1, kv_step, (neg, zero, neg, zero, jnp.zeros((t, LANES), F32)))
        o_ref[...] = acc * jnp.where(first, 1.0 / l0, 1.0 / l1)
        lse_ref[...] = jnp.where(first, m0 + jnp.log(l0), m1 + jnp.log(l1))

    blk = pl.BlockSpec((t, LANES), lambda hp, i: (i, hp))
    k_full = pl.BlockSpec((S, LANES), lambda hp, i: (0, n_hp + hp))
    v_full = pl.BlockSpec((S, LANES), lambda hp, i: (0, 2 * n_hp + hp))
    return pl.pallas_call(
        body, name="flash_fwd", grid=(n_hp, S // t),
        in_specs=[blk, k_full, v_full, pl.BlockSpec((S // t, t, t), lambda hp, i: (0, 0, 0))], out_specs=[blk, blk],
        out_shape=[SDS((S, ATT_W), F32), SDS((S, ATT_W), F32)],
        compiler_params=_cparams(("parallel", "arbitrary")),
    )(qkv, qkv, qkv, bias)


def _flash_bwd(qkv, o, do, lse, bias, after=()):
    t = ATT_T
    n_hp = ATT_W // LANES
    n_t = S // t

    def body(q_ref, k_ref, v_ref, o_ref, do_ref, lse_ref, b_ref, *rest):
        dq_ref, dk_ref, dv_ref = rest[-3:]
        j = pl.program_id(1)
        first = lax.broadcasted_iota(jnp.int32, (1, LANES), 1) < HEAD_DIM

        @pl.when(j == 0)
        def _():
            dq_ref[...] = jnp.zeros_like(dq_ref)

        kb = k_ref[...]
        vb = v_ref[...]
        ks = _head_split(kb, first)

        def q_step(i, carry):
            dk_acc, dv_acc = carry
            rows = pl.ds(pl.multiple_of(i * t, t), t)
            qs = _head_split(q_ref[rows, :], first)
            dob = do_ref[rows, :]
            prod = dob * o_ref[rows, :]
            d_all = jnp.sum(prod, axis=1, keepdims=True)
            d0 = jnp.sum(jnp.where(first, prod, 0.0), axis=1, keepdims=True)
            lse_b = lse_ref[rows, :]
            lse0 = jnp.max(jnp.where(first, lse_b, -jnp.inf), axis=1, keepdims=True)
            lse1 = jnp.max(jnp.where(first, -jnp.inf, lse_b), axis=1, keepdims=True)
            dos = _head_split(dob.astype(BF16), first)
            bias_t = b_ref[i - j]
            dq_t = jnp.zeros((t, LANES), F32)
            for h, (lse_h, d_h) in enumerate(((lse0, d0), (lse1, d_all - d0))):
                s = lax.dot_general(qs[h], kb, NT, preferred_element_type=F32)
                p = jnp.exp(s + (bias_t - lse_h))
                dp = lax.dot_general(dos[h], vb, NT, preferred_element_type=F32)
                ds = (p * (dp - d_h)).astype(BF16)
                dv_acc = dv_acc + lax.dot_general(p.astype(BF16), dos[h], TN, preferred_element_type=F32)
                dk_acc = dk_acc + lax.dot_general(ds, qs[h], TN, preferred_element_type=F32)
                dq_t = dq_t + lax.dot_general(ds, ks[h], NN, preferred_element_type=F32)
            dq_ref[rows, :] += dq_t
            return dk_acc, dv_acc

        zero = jnp.zeros((t, LANES), F32)
        dk_acc, dv_acc = lax.fori_loop(j, n_t, q_step, (zero, zero))
        dk_ref[...] = dk_acc
        dv_ref[...] = dv_acc

    blk = pl.BlockSpec((t, LANES), lambda hp, j: (j, hp))
    full = pl.BlockSpec((S, LANES), lambda hp, j: (0, hp))
    k_blk = pl.BlockSpec((t, LANES), lambda hp, j: (j, n_hp + hp))
    v_blk = pl.BlockSpec((t, LANES), lambda hp, j: (j, 2 * n_hp + hp))
    return pl.pallas_call(
        body, name="flash_bwd", grid=(n_hp, n_t),
        in_specs=([full, k_blk, v_blk, full, full, full, pl.BlockSpec((n_t, t, t), lambda hp, j: (0, 0, 0))]
                  + [pl.BlockSpec(memory_space=pl.ANY)] * len(after)),
        out_specs=[full, blk, blk],
        out_shape=[SDS((S, ATT_W), F32)] * 3,
        compiler_params=_cparams(("parallel", "arbitrary")),
    )(qkv, qkv, qkv, o, do, lse, bias, *after)


SCAN_T = 256
SCAN_GROUP = 8
SCAN_STEPS = (1, 2, 4)
ST_ROWS = 2 * N_CPLX // LANES
HALF = ST_ROWS // 2


def _scan_tables(lam_t):
    lam = lax.complex(lam_t[:HALF].reshape(N_CPLX), lam_t[HALF:].reshape(N_CPLX))
    pows = [lam]
    for _ in range(SCAN_GROUP - 1):
        pows.append(pows[-1] * lam)
    pows = jnp.stack(pows)
    sub = jnp.arange(SCAN_GROUP)[:, None]
    fwd = [jnp.where(sub >= k, pows[k - 1][None, :], 0.0) for k in SCAN_STEPS] + [pows]
    conj = jnp.conj(pows)
    bwd = [jnp.where(sub <= SCAN_GROUP - 1 - k, conj[k - 1][None, :], 0.0) for k in SCAN_STEPS] + [conj[::-1]]

    def pack(tabs):
        return jnp.stack([jnp.concatenate([jnp.real(t), jnp.imag(t)], axis=1) for t in tabs]).astype(F32)

    return pack(fwd), pack(bwd)


def _cmul_add(xr, xi, lr, li, sr, si):
    return xr + lr * sr - li * si, xi + lr * si + li * sr


def _group_scan(xr, xi, tab_ref, cr, ci, reverse):
    for j, k in enumerate(SCAN_STEPS):
        shift = SCAN_GROUP - k if reverse else k
        xr, xi = _cmul_add(xr, xi, tab_ref[j, :, :N_CPLX], tab_ref[j, :, N_CPLX:],
                           pltpu.roll(xr, shift, 0), pltpu.roll(xi, shift, 0))
    return _cmul_add(xr, xi, tab_ref[3, :, :N_CPLX], tab_ref[3, :, N_CPLX:],
                     jnp.broadcast_to(cr, (SCAN_GROUP, N_CPLX)), jnp.broadcast_to(ci, (SCAN_GROUP, N_CPLX)))


SSM_SUPER = 4
SB_ROWS = SSM_W // SSM_SUPER
SB_COLS = N_CPLX // SSM_SUPER


def _super_blocks():
    return [(slice(b * SB_ROWS, (b + 1) * SB_ROWS), slice(h * SB_COLS, (h + 1) * SB_COLS),
             slice(h * N_CPLX + b * SB_COLS, h * N_CPLX + (b + 1) * SB_COLS))
            for b in range(SSM_SUPER) for h in range(2)]


def _dot16(a, b, dims):
    return lax.dot_general(a.astype(BF16), b.astype(BF16), dims, preferred_element_type=F32)


def _s5_fwd(tab, u_arr, u_cols, w_b, w_ct):
    nc = N_CPLX

    def body(tab_ref, u_ref, wb_ref, wct_ref, st_ref, y_ref, carry, bu_scr):
        @pl.when(pl.program_id(0) == 0)
        def _():
            carry[...] = jnp.zeros_like(carry)

        for rows_b, cols_c, cols_s in _super_blocks():
            bu_scr[:, cols_s] = _dot16(u_ref[:, rows_b], wb_ref[rows_b, cols_c], NN)

        def group(a, c):
            rows = pl.ds(pl.multiple_of(a * SCAN_GROUP, SCAN_GROUP), SCAN_GROUP)
            xr, xi = _group_scan(bu_scr[rows, :nc], bu_scr[rows, nc:], tab_ref, c[0], c[1], False)
            st_ref[rows, :nc] = xr
            st_ref[rows, nc:] = xi
            return xr[SCAN_GROUP - 1:SCAN_GROUP, :], xi[SCAN_GROUP - 1:SCAN_GROUP, :]

        cr, ci = lax.fori_loop(0, SCAN_T // SCAN_GROUP, group, (carry[:, :nc], carry[:, nc:]), unroll=2)
        carry[:, :nc] = cr
        carry[:, nc:] = ci

        for b in range(SSM_SUPER):
            (rows_b, cols_re, st_re), (_, cols_im, st_im) = _super_blocks()[2 * b:2 * b + 2]
            y_ref[:, rows_b] = (_dot16(st_ref[:, st_re], wct_ref[rows_b, cols_re], NT)
                                + _dot16(st_ref[:, st_im], wct_ref[rows_b, cols_im], NT))

    const = lambda shape: pl.BlockSpec(shape, lambda i: (0,) * len(shape))
    return pl.pallas_call(
        body, name="s5_fwd", grid=(S // SCAN_T,),
        in_specs=[const((4, SCAN_GROUP, 2 * nc)), pl.BlockSpec((SCAN_T, SSM_W), lambda i: (i, u_cols[0] // SSM_W)),
                  const((SSM_W, 2 * SB_COLS)), const((SSM_W, 2 * SB_COLS))],
        out_specs=[pl.BlockSpec((SCAN_T, 2 * nc), lambda i: (i, 0)), pl.BlockSpec((SCAN_T, SSM_W), lambda i: (i, 0))],
        out_shape=[SDS((S, 2 * nc), F32), SDS((S, SSM_W), F32)],
        scratch_shapes=[pltpu.VMEM((1, 2 * nc), F32), pltpu.VMEM((SCAN_T, 2 * nc), F32)],
        compiler_params=_cparams(("arbitrary",)),
    )(tab, u_arr, w_b, w_ct)


def _s5_bwd(tab, dy, states, u_arr, u_cols, w_b, w_ct):
    n_blk = S // SCAN_T
    nc = N_CPLX

    def body(tab_ref, dy_ref, x_ref, u_ref, wb_ref, wct_ref, du_ref, dlam_ref, dwb_ref, dwct_ref,
             carry, acc, d_scr, g_ref):
        i = pl.program_id(0)

        @pl.when(i == 0)
        def _():
            carry[...] = jnp.zeros_like(carry)
            acc[...] = jnp.zeros_like(acc)
            dwb_ref[...] = jnp.zeros_like(dwb_ref)
            dwct_ref[...] = jnp.zeros_like(dwct_ref)

        for rows_b, cols_c, cols_s in _super_blocks():
            d_scr[:, cols_s] = _dot16(dy_ref[:, rows_b], wct_ref[rows_b, cols_c], NN)

        last_row = lax.broadcasted_iota(jnp.int32, (SCAN_GROUP, 1), 0) == SCAN_GROUP - 1
        d_ref = d_scr

        def group(j, c):
            cr, ci = c
            rows = pl.ds(pl.multiple_of((SCAN_T // SCAN_GROUP - 1 - j) * SCAN_GROUP, SCAN_GROUP), SCAN_GROUP)
            gr, gi = _group_scan(d_ref[rows, :nc], d_ref[rows, nc:], tab_ref, cr, ci, True)
            g_ref[rows, :nc] = gr
            g_ref[rows, nc:] = gi
            nr = jnp.where(last_row, jnp.broadcast_to(cr, (SCAN_GROUP, nc)), pltpu.roll(gr, SCAN_GROUP - 1, 0))
            ni = jnp.where(last_row, jnp.broadcast_to(ci, (SCAN_GROUP, nc)), pltpu.roll(gi, SCAN_GROUP - 1, 0))
            sr, si = x_ref[rows, :nc], x_ref[rows, nc:]
            acc[:, :nc] += nr * sr + ni * si
            acc[:, nc:] += ni * sr - nr * si
            return gr[0:1, :], gi[0:1, :]

        cr, ci = lax.fori_loop(0, SCAN_T // SCAN_GROUP, group, (carry[:, :nc], carry[:, nc:]), unroll=2)
        carry[:, :nc] = cr
        carry[:, nc:] = ci

        for b in range(SSM_SUPER):
            (rows_b, cols_re, st_re), (_, cols_im, st_im) = _super_blocks()[2 * b:2 * b + 2]
            du_ref[:, rows_b] = (_dot16(g_ref[:, st_re], wb_ref[rows_b, cols_re], NT)
                                 + _dot16(g_ref[:, st_im], wb_ref[rows_b, cols_im], NT))
            for cols_c, cols_s in ((cols_re, st_re), (cols_im, st_im)):
                dwb_ref[rows_b, cols_c] += _dot16(u_ref[:, rows_b], g_ref[:, cols_s], TN)
                dwct_ref[rows_b, cols_c] += _dot16(dy_ref[:, rows_b], x_ref[:, cols_s], TN)

        @pl.when(i == n_blk - 1)
        def _():
            dlam_ref[...] = jnp.sum(acc[...], axis=0, keepdims=True)

    const = lambda shape: pl.BlockSpec(shape, lambda i: (0,) * len(shape))
    rows = lambda width, col_block=0: pl.BlockSpec((SCAN_T, width), lambda i: (n_blk - 1 - i, col_block))
    maps = const((SSM_W, 2 * SB_COLS))
    return pl.pallas_call(
        body, name="s5_bwd", grid=(n_blk,),
        in_specs=[const((4, SCAN_GROUP, 2 * nc)), rows(SSM_W), rows(2 * nc), rows(SSM_W, u_cols[0] // SSM_W), maps, maps],
        out_specs=[rows(SSM_W), const((1, 2 * nc)), maps, maps],
        out_shape=[SDS((S, SSM_W), F32), SDS((1, 2 * nc), F32), SDS((SSM_W, 2 * SB_COLS), F32),
                   SDS((SSM_W, 2 * SB_COLS), F32)],
        scratch_shapes=[pltpu.VMEM((1, 2 * nc), F32), pltpu.VMEM((SCAN_GROUP, 2 * nc), F32),
                        pltpu.VMEM((SCAN_T, 2 * nc), F32), pltpu.VMEM((SCAN_T, 2 * nc), F32)],
        compiler_params=_cparams(("arbitrary",)),
    )(tab, dy, states, u_arr, w_b, w_ct)


def _ssm_prep(a_re, a_im, log_dt, b_re, b_im, c_re, c_im):
    lam = lax.complex(a_re, a_im)
    dt = jnp.exp(log_dt)[:, None]
    lam_bar = jnp.exp(lam * dt)
    b_bar = ((lam_bar - 1.0) / lam)[..., None] * lax.complex(b_re, b_im)
    lam_t = jnp.concatenate([jnp.real(lam_bar).reshape(HALF, LANES), jnp.imag(lam_bar).reshape(HALF, LANES)], axis=0)
    groups_per_super = SSM_GROUPS // SSM_SUPER
    on_diag = ((lax.broadcasted_iota(jnp.int32, (SSM_W, SB_COLS), 0) // SSM_GROUP) % groups_per_super
               == lax.broadcasted_iota(jnp.int32, (SSM_W, SB_COLS), 1) // SSM_STATE)

    def compact(m):
        return jnp.where(on_diag, jnp.tile(m.reshape(SSM_W, SSM_STATE), (1, groups_per_super)), 0.0)

    w_b = jnp.concatenate([compact(jnp.real(b_bar).transpose(0, 2, 1)),
                           compact(jnp.imag(b_bar).transpose(0, 2, 1))], axis=1)
    w_ct = jnp.concatenate([compact(c_re), -compact(c_im)], axis=1)
    return lam_t, w_b, w_ct


U_SSM_COLS = (4 * ATT_W, SSM_W)


def _row(v):
    return v.reshape(1, -1)


def _even_fwd(x, h, tail, pre, post, w_in, late_w, glu_b, ssm_d, prep, tables):
    lam_t, w_b, w_ct = prep
    proj = _mm(h, w_in, "nn", F32, b_blocks=True)
    qkv = _qkv_prep(proj, tables[:3])
    w_out, glu_w = late_w(qkv)
    att, lse = _flash_fwd(qkv, tables[3])
    scan_fwd_tab, scan_bwd_tab = _scan_tables(lam_t)
    states, y = _s5_fwd(scan_fwd_tab, proj, U_SSM_COLS, w_b, w_ct)

    def act1(yv, uv, dv):
        return (_gelu_and_grad(yv + dv * uv)[0],), ()

    (z1,) = _rowwise(act1, [(y, SSM_W, 0), (proj, SSM_W, 8), (ssm_d, SSM_W, 0)], [(SSM_W, F32)], name="ssm_act_fwd")
    lin = _mm(z1, glu_w, "nn", F32)

    def gate(att_v, ga, gs, z1v, linv, bv):
        ssm_out = z1v * _sigmoid(linv + bv)
        return (jnp.concatenate([att_v * _silu_and_grad(ga)[0], ssm_out * _silu_and_grad(gs)[0]], axis=1),), ()

    (merged,) = _rowwise(gate, [(att, ATT_W, 0), (proj, ATT_W, 3), (proj, SSM_W, 9), (z1, SSM_W, 0),
                                (lin, SSM_W, 0), (glu_b, SSM_W, 0)], [(EVEN_OUT, BF16)], name="even_gate_fwd")
    yout = _mm(merged, w_out, "nn", F32)
    saved = (x, h, proj, qkv, att, lse, states, y, z1, lin, merged, yout, w_out, glu_w, scan_bwd_tab)
    return tail(x, yout, post) + (saved,)


def _even_bwd(g, saved, pre, post, w_in, late_w, glu_b, ssm_d, prep, tables, on_w, on_ssm):
    x, h, proj, qkv, att, lse, states, y, z1, lin, merged, yout, w_out, glu_w, scan_bwd_tab = saved
    lam_t, w_b, w_ct = prep
    dyout, dpost = _post_bwd(g, yout, post)
    dmerged = _mm(dyout, w_out, "nt", F32)
    dw_out = _mm(merged, dyout, "tn", BF16)

    def gate_bwd(dm_a, dm_s, att_v, ga, gs, z1v, linv, bv):
        sa, dsa = _silu_and_grad(ga)
        ss, dss = _silu_and_grad(gs)
        sig = _sigmoid(linv + bv)
        ssm_out = z1v * sig
        dssm = dm_s * ss
        dlin = dssm * z1v * sig * (1.0 - sig)
        return (dm_a * sa, dm_a * att_v * dsa, dm_s * ssm_out * dss, dssm * sig, dlin), (dlin,)

    datt, dg_att, dg_ssm, dz1a, dlin, dglu_b = _rowwise(
        gate_bwd, [(dmerged, ATT_W, 0), (dmerged, SSM_W, 2), (att, ATT_W, 0), (proj, ATT_W, 3), (proj, SSM_W, 9),
                   (z1, SSM_W, 0), (lin, SSM_W, 0), (glu_b, SSM_W, 0)],
        [(ATT_W, F32), (ATT_W, BF16), (SSM_W, BF16), (SSM_W, F32), (SSM_W, BF16)], [SSM_W], name="even_gate_bwd")
    dz1b = _mm(dlin, glu_w, "nt", F32)
    dglu_w = _mm(z1, dlin, "tn", BF16)

    def act1_bwd(da, db, yv, uv, dv):
        dpre = (da + db) * _gelu_and_grad(yv + dv * uv)[1]
        return (dpre, dpre * dv), (dpre * uv,)

    sent_late_w = on_w(dict(w_out=dw_out, glu_w=dglu_w))
    dy, du_direct, dd = _rowwise(act1_bwd, [(dz1a, SSM_W, 0), (dz1b, SSM_W, 0), (y, SSM_W, 0), (proj, SSM_W, 8),
                                            (ssm_d, SSM_W, 0)], [(SSM_W, BF16), (SSM_W, F32)], [SSM_W],
                                 name="ssm_act_bwd", after=(sent_late_w,))
    du_state, dlam_row, dw_b, dw_ct = _s5_bwd(scan_bwd_tab, dy, states, proj, U_SSM_COLS, w_b, w_ct)
    dlam = jnp.concatenate([dlam_row[0, :N_CPLX].reshape(HALF, LANES), dlam_row[0, N_CPLX:].reshape(HALF, LANES)],
                           axis=0)
    sent_ssm = on_ssm((dlam, dw_b, dw_ct))
    dq, dk, dv = _flash_bwd(qkv, att, datt, lse, tables[3], after=() if sent_ssm is None else (sent_ssm,))

    def assemble(dqv, dkv, dvv, dga, dua, dub, dgs, c, l, h):
        rot = _rotate(jnp.concatenate([dqv, dkv], axis=1), c, l, h, True)
        return (jnp.concatenate([(rot[:, :ATT_W] * HEAD_DIM ** -0.5).astype(BF16), rot[:, ATT_W:].astype(BF16),
                                 dvv.astype(BF16), dga, (dua + dub).astype(BF16), dgs], axis=1),), ()

    (dproj,) = _rowwise(assemble, [(dq, ATT_W, 0), (dk, ATT_W, 0), (dv, ATT_W, 0), (dg_att, ATT_W, 0),
                                   (du_state, SSM_W, 0), (du_direct, SSM_W, 0), (dg_ssm, SSM_W, 0),
                                   (tables[0], LANES, 0), (tables[1], LANES, 0), (tables[2], LANES, 0)],
                        [(EVEN_IN, BF16)], name="dproj_assemble")
    dw_in = _mm(h, dproj, "tn", BF16, out_blocks=True)
    sent = on_w(dict(w_in=dw_in))
    dh = _mm(dproj, w_in, "nt", F32, b_blocks=True, after=(sent,))
    g_prev, dpre = _pre_bwd(g, dh, x, pre)
    return g_prev, dict(pre=dpre, post=dpost, glu_b=dglu_b, ssm_d=dd)


def _odd_fwd(x, h, tail, pre, post, w_in, pool_w, pool_scale, w_out):
    proj = _mm(h, w_in, "nn", F32, b_blocks=True)
    mixed = _pool(proj, 0, False, BF16)
    ylin = _gmm(mixed, pool_w, "nn", F32)

    def gate(yl, gt, sc):
        return (yl * sc * _silu_and_grad(gt)[0],), ()

    (z,) = _rowwise(gate, [(ylin, POOL_W, 0), (proj, POOL_W, 1), (pool_scale, POOL_W, 0)], [(POOL_W, BF16)],
                    name="odd_gate_fwd")
    yout = _mm(z, w_out, "nn", F32)
    return tail(x, yout, post) + ((x, h, proj, mixed, ylin, z, yout),)


def _odd_bwd(g, saved, pre, post, w_in, pool_w, pool_scale, w_out, on_w):
    x, h, proj, mixed, ylin, z, yout = saved
    dyout, dpost = _post_bwd(g, yout, post)
    dz = _mm(dyout, w_out, "nt", F32)
    dw_out = _mm(z, dyout, "tn", BF16)

    def gate_bwd(dzv, yl, gt, sc):
        sg, dsg = _silu_and_grad(gt)
        tt = dzv * sg
        return (tt * sc, dzv * yl * sc * dsg), (tt * yl,)

    dylin, dproj_gate, dscale = _rowwise(gate_bwd, [(dz, POOL_W, 0), (ylin, POOL_W, 0), (proj, POOL_W, 1),
                                                    (pool_scale, POOL_W, 0)],
                                         [(POOL_W, BF16), (POOL_W, BF16, ODD_IN, 1)], [POOL_W], name="odd_gate_bwd")
    dmixed = _gmm(dylin, pool_w, "nt", F32)
    dpool_w = _gmm(mixed, dylin, "tn", BF16)
    dproj = _pool(dmixed, 0, True, BF16, into=dproj_gate)
    dw_in = _mm(h, dproj, "tn", BF16, out_blocks=True)
    sent = on_w(dict(w_in=dw_in, w_out=dw_out, pool_w=dpool_w))
    dh = _mm(dproj, w_in, "nt", F32, b_blocks=True, after=(sent,))
    g_prev, dpre = _pre_bwd(g, dh, x, pre)
    return g_prev, dict(pre=dpre, post=dpost, pool_scale=dscale)


def _my_index():
    return 4 * lax.axis_index("x") + 2 * lax.axis_index("y") + lax.axis_index("c")


HBM_SPEC = pl.BlockSpec(memory_space=pltpu.HBM)
SEM_SPEC = pl.BlockSpec(memory_space=pltpu.SEMAPHORE)
SPLIT_EFFECT = pltpu.SideEffectType.DATAFLOW_SIDE_EFFECTING


def _device_of(j):
    return (j // 4, (j // 2) % 2, j % 2)


def _split_copy(srcs, lands, send_sems, recv_sems, gather, i, j, dst_slot, recv_slot):
    return pltpu.make_async_remote_copy(
        src_ref=srcs[i] if gather else srcs[i].at[j], dst_ref=lands[i].at[dst_slot],
        send_sem=send_sems.at[i * N_DEV + j], recv_sem=recv_sems.at[i * N_DEV + recv_slot],
        device_id=_device_of(j), device_id_type=MESH_ID)


def _own_copy(srcs, lands, send_sems, gather, i, me):
    return pltpu.make_async_copy(srcs[i] if gather else srcs[i].at[me], lands[i].at[me], send_sems.at[i * N_DEV + me])


def _xchg_start(name, srcs, gather, after=()):
    n = len(srcs)
    n_in = n + len(after)

    def body(*refs):
        src_refs = refs[:n]
        send_sems, recv_sems, token = refs[n_in], refs[n_in + 1], refs[-1]
        land_refs = refs[n_in + 2 + n:n_in + 2 + 2 * n]
        me = _my_index()
        for j in range(N_DEV):
            @pl.when(me != j)
            def _(j=j):
                for i in range(n):
                    _split_copy(src_refs, land_refs, send_sems, recv_sems, gather, i, j, me, me).start()
        for i in range(n):
            _own_copy(src_refs, land_refs, send_sems, gather, i, me).start()
        token[...] = jnp.zeros_like(token)

    land_shapes = [((N_DEV,) + a.shape) if gather else a.shape for a in srcs]
    thru = ([pltpu.HBM(a.shape, a.dtype) for a in srcs] + [pltpu.HBM(s, a.dtype) for s, a in zip(land_shapes, srcs)])
    res = pl.pallas_call(
        body, name=name,
        out_shape=(pltpu.SemaphoreType.DMA((n * N_DEV,)), pltpu.SemaphoreType.DMA((n * N_DEV,)), *thru,
                   SDS((8, LANES), F32)),
        in_specs=[HBM_SPEC] * n + [pl.BlockSpec(memory_space=pl.ANY)] * len(after),
        out_specs=(SEM_SPEC, SEM_SPEC, *([HBM_SPEC] * (2 * n)), pl.BlockSpec(memory_space=pltpu.VMEM)),
        input_output_aliases={i: 2 + i for i in range(n)},
        compiler_params=pltpu.CompilerParams(has_side_effects=SPLIT_EFFECT),
    )(*[pltpu.with_memory_space_constraint(a, pltpu.HBM) for a in srcs], *after)
    return res[0], res[1], list(res[2:2 + n]), list(res[2 + n:2 + 2 * n]), res[-1]


def _xchg_wait(name, started, gather, after):
    send_sems, recv_sems, srcs, lands, _ = started
    n = len(srcs)

    def body(*refs):
        src_refs, land_refs = refs[:n], refs[n:2 * n]
        send_r, recv_r = refs[2 * n], refs[2 * n + 1]
        me = _my_index()
        for j in range(N_DEV):
            @pl.when(me != j)
            def _(j=j):
                for i in range(n):
                    _split_copy(src_refs, land_refs, send_r, recv_r, gather, i, j, me, me).wait_send()
                    _split_copy(src_refs, land_refs, send_r, recv_r, gather, i, j, j, j).wait_recv()
        for i in range(n):
            _own_copy(src_refs, land_refs, send_r, gather, i, me).wait()

    thru = [pltpu.HBM(a.shape, a.dtype) for a in list(srcs) + list(lands)]
    res = pl.pallas_call(
        body, name=name, out_shape=tuple(thru),
        in_specs=[HBM_SPEC] * (2 * n) + [SEM_SPEC, SEM_SPEC] + [pl.BlockSpec(memory_space=pl.ANY)] * len(after),
        out_specs=tuple([HBM_SPEC] * (2 * n)),
        input_output_aliases={i: i for i in range(2 * n)},
        compiler_params=pltpu.CompilerParams(has_side_effects=SPLIT_EFFECT),
    )(*srcs, *lands, send_sems, recv_sems, *after)
    return list(res[n:])


def _adam_layers(w, slot_list, m, v, name):
    n_l, r, c = w.shape
    ns = slot_list[0].shape[0]
    tr = r
    while tr * c * 4 > (1 << 20) and tr % 16 == 0:
        tr //= 2
    assert r % tr == 0 and len(slot_list) == n_l

    def body(*refs):
        w_ref, slot_refs = refs[0], refs[1:1 + n_l]
        m_ref, v_ref, go_ref, d_ref, mo_ref, vo_ref = refs[1 + n_l:]
        layer = pl.program_id(0)
        g = None
        for l, g_ref in enumerate(slot_refs):
            gl = g_ref[0].astype(F32)
            for s in range(1, ns):
                gl = gl + g_ref[s].astype(F32)
            g = gl if g is None else jnp.where(layer == l, gl, g)
        mn = ADAM_B1 * m_ref[...] + (1.0 - ADAM_B1) * g
        vn = ADAM_B2 * v_ref[...] + (1.0 - ADAM_B2) * (g * g)
        m_hat = mn / (1.0 - ADAM_B1 ** ADAM_STEP)
        v_hat = vn / (1.0 - ADAM_B2 ** ADAM_STEP)
        go_ref[...] = g
        d_ref[...] = -ADAM_LR * (m_hat / (jnp.sqrt(v_hat) + ADAM_EPS) + ADAM_WD * w_ref[...])
        mo_ref[...] = mn
        vo_ref[...] = vn

    blk = pl.BlockSpec((None, tr, c), lambda l, i: (l, i, 0))
    slot_specs = [pl.BlockSpec((ns, tr, c), lambda l, i, k=k: (0, jnp.where(l == k, i, 0), 0)) for k in range(n_l)]
    return pl.pallas_call(
        body, name=name, grid=(n_l, r // tr),
        in_specs=[blk] + slot_specs + [blk, blk],
        out_specs=[blk] * 4, out_shape=[SDS((n_l, r, c), F32)] * 4,
        compiler_params=_cparams(("arbitrary", "arbitrary")),
    )(*_in_hbm((w, *slot_list, m, v)))


def _adam(w, gslots, m, v, name):
    r, c = w.shape
    ns = gslots.shape[0]
    tr = r
    while tr * c * 4 > (1 << 20) and tr % 16 == 0:
        tr //= 2
    assert r % tr == 0

    def body(w_ref, g_ref, m_ref, v_ref, go_ref, d_ref, mo_ref, vo_ref):
        g = g_ref[0].astype(F32)
        for s in range(1, ns):
            g = g + g_ref[s].astype(F32)
        wv = w_ref[...]
        mn = ADAM_B1 * m_ref[...] + (1.0 - ADAM_B1) * g
        vn = ADAM_B2 * v_ref[...] + (1.0 - ADAM_B2) * (g * g)
        m_hat = mn / (1.0 - ADAM_B1 ** ADAM_STEP)
        v_hat = vn / (1.0 - ADAM_B2 ** ADAM_STEP)
        go_ref[...] = g
        d_ref[...] = -ADAM_LR * (m_hat / (jnp.sqrt(v_hat) + ADAM_EPS) + ADAM_WD * wv)
        mo_ref[...] = mn
        vo_ref[...] = vn

    blk = pl.BlockSpec((tr, c), lambda i: (i, 0))
    return pl.pallas_call(
        body, name=name, grid=(r // tr,),
        in_specs=[blk, pl.BlockSpec((ns, tr, c), lambda i: (0, i, 0)), blk, blk],
        out_specs=[blk] * 4, out_shape=[SDS((r, c), F32)] * 4,
        compiler_params=_cparams(("parallel",)),
    )(w, gslots, m, v)


def _sum_slots(slots, name):
    ns, r, c = slots.shape

    def body(g_ref, o_ref):
        g = g_ref[0]
        for s in range(1, ns):
            g = g + g_ref[s]
        o_ref[...] = g

    return pl.pallas_call(
        body, name=name, grid=(1,),
        in_specs=[pl.BlockSpec((ns, r, c), lambda i: (0, 0, 0))], out_specs=pl.BlockSpec((r, c), lambda i: (0, 0)),
        out_shape=SDS((r, c), F32), compiler_params=_cparams(("arbitrary",)),
    )(slots)


def _adam_params(params, name):
    n = len(params)

    def body(*refs):
        ins, outs = refs[:5 * n], refs[5 * n:]
        for p in range(n):
            w_ref, m_ref, v_ref, g_first, g_rest = ins[5 * p:5 * p + 5]
            go_ref, d_ref, mo_ref, vo_ref = outs[4 * p:4 * p + 4]
            for part, g_ref in ((slice(0, 1), g_first), (slice(1, w_ref.shape[0]), g_rest)):
                g = g_ref[...]
                mn = ADAM_B1 * m_ref[part] + (1.0 - ADAM_B1) * g
                vn = ADAM_B2 * v_ref[part] + (1.0 - ADAM_B2) * (g * g)
                m_hat = mn / (1.0 - ADAM_B1 ** ADAM_STEP)
                v_hat = vn / (1.0 - ADAM_B2 ** ADAM_STEP)
                go_ref[part] = g
                d_ref[part] = -ADAM_LR * (m_hat / (jnp.sqrt(v_hat) + ADAM_EPS) + ADAM_WD * w_ref[part])
                mo_ref[part] = mn
                vo_ref[part] = vn

    def whole(a):
        return pl.BlockSpec(a.shape, lambda i, nd=a.ndim: (0,) * nd)

    flat = _in_hbm([a for prm in params for a in prm])
    outs = pl.pallas_call(
        body, name=name, grid=(1,),
        in_specs=[whole(a) for a in flat],
        out_specs=[whole(prm[0]) for prm in params for _ in range(4)],
        out_shape=[SDS(prm[0].shape, F32) for prm in params for _ in range(4)],
        compiler_params=_cparams(("arbitrary",)),
    )(*flat)
    return [outs[4 * p:4 * p + 4] for p in range(n)]


SMALL_NAMES = ("pre_norm", "post_norm", "ssm_a_re", "ssm_a_im", "ssm_log_dt", "ssm_b_re", "ssm_b_im", "ssm_c_re",
               "ssm_c_im", "ssm_d", "ssm_glu_b")
SSM_NAMES = ("ssm_a_re", "ssm_a_im", "ssm_log_dt", "ssm_b_re", "ssm_b_im", "ssm_c_re", "ssm_c_im")
SHARDED_NAMES = ("even_w_in", "even_w_out", "ssm_glu_w", "odd_w_in", "pool_w", "odd_w_out")
WEIGHT_ORDER = ("pre_norm", "post_norm", "even_w_in", "even_w_out", "ssm_a_re", "ssm_a_im", "ssm_log_dt", "ssm_b_re",
                "ssm_b_im", "ssm_c_re", "ssm_c_im", "ssm_d", "ssm_glu_w", "ssm_glu_b", "odd_w_in", "pool_w",
                "pool_scale", "odd_w_out")
PACK_ROWS_ALIGN = 8


def _pack(parts):
    flat = jnp.concatenate([p.reshape(-1).astype(F32) for p in parts])
    rows = -(-flat.shape[0] // (LANES * PACK_ROWS_ALIGN)) * PACK_ROWS_ALIGN
    return jnp.pad(flat, (0, rows * LANES - flat.shape[0])).reshape(rows, LANES)


def _unpack(packed, shapes):
    flat = packed.reshape(-1)
    out, off = [], 0
    for shp in shapes:
        size = math.prod(shp)
        out.append(flat[off:off + size].reshape(shp))
        off += size
    return out


EVEN_SHARDED = ("w_in", "w_out", "glu_w")
ODD_SHARDED = ("w_in", "pool_w", "w_out")
FAMILY = {(0, "w_in"): "even_w_in", (0, "w_out"): "even_w_out", (0, "glu_w"): "ssm_glu_w",
          (1, "w_in"): "odd_w_in", (1, "pool_w"): "pool_w", (1, "w_out"): "odd_w_out"}


def _sharded_keys(layer):
    return EVEN_SHARDED if layer % 2 == 0 else ODD_SHARDED


def _local_step(x, tgt, small, get_weights, on_w, on_ssm, on_grads, zero=0.0):
    tables = _rope_tables(zero) + (_attention_bias(zero),)
    preps, prep_vjps = [], []
    for i in range(2):
        out, vjp = jax.vjp(_ssm_prep, small["ssm_a_re"][i] + zero, small["ssm_a_im"][i], small["ssm_log_dt"][i],
                           small["ssm_b_re"][i], small["ssm_b_im"][i], small["ssm_c_re"][i], small["ssm_c_im"][i])
        preps.append(out)
        prep_vjps.append(vjp)

    def layer_args(layer, wts):
        i = layer // 2
        pre, post = _row(small["pre_norm"][layer]) + wts.get("token", 0.0), _row(small["post_norm"][layer])
        if layer % 2 == 0:
            return (pre, post, wts["w_in"], wts["late"], _row(small["ssm_glu_b"][i]), _row(small["ssm_d"][i]),
                    preps[i], tables)
        return (pre, post, wts["w_in"], wts["pool_w"], _row(wts["pool_scale"]), wts["w_out"])

    saved, args = [], []
    cur = x
    for layer in range(4):
        after = (cur,) if layer else (cur, tables[0], tables[3], preps[0][1], preps[0][2], preps[1][1], preps[1][2])
        args.append(layer_args(layer, get_weights(layer, after)))
        if layer == 0:
            h = _norm_fwd(cur, args[0][0])
        if layer < 3:
            def tail(xv, yv, post, next_gain=_row(small["pre_norm"][layer + 1])):
                return tuple(_post_fwd(xv, yv, post, next_gain))
        else:
            def tail(xv, yv, post):
                return tuple(_post_fwd_loss(xv, yv, post, tgt))
        cur, h, sv = (_even_fwd if layer % 2 == 0 else _odd_fwd)(cur, h, tail, *args[layer])
        saved.append(sv)
    g, sq = cur, h
    loss = 0.5 * jnp.sum(sq) / D

    lg = [None] * 4
    token = jnp.zeros((), F32)
    for layer in reversed(range(4)):
        largs = list(args[layer])
        largs[1] = largs[1] + token
        hooks = dict(on_w=functools.partial(on_w, layer))
        ssm_grads = []
        if layer % 2 == 0:
            def ssm_hook(cotangents, layer=layer):
                ssm_grads.append(prep_vjps[layer // 2](cotangents))
                return on_ssm(layer, ssm_grads[0])

            hooks["on_ssm"] = ssm_hook
        g, lg[layer] = (_even_bwd if layer % 2 == 0 else _odd_bwd)(g, saved[layer], *largs, **hooks)
        if ssm_grads:
            lg[layer]["ssm"] = ssm_grads[0]
        token = on_grads(layer, lg[layer])
    return loss, g, token


def _to_slots(key, gfull):
    if key == "w_in":
        return gfull
    if key in ("w_out", "glu_w"):
        rr, nn = gfull.shape
        return gfull.reshape(N_DEV, rr // N_DEV, nn)
    assert key == "pool_w"
    gg, rr, nn = gfull.shape
    return gfull.reshape(gg, N_DEV, rr // N_DEV, nn).transpose(1, 0, 2, 3)


def _from_gathered(key, gat):
    if key == "w_in":
        return gat
    if key in ("w_out", "glu_w"):
        _, rr, nn = gat.shape
        return gat.reshape(N_DEV * rr, nn)
    assert key == "pool_w"
    _, gg, rr, nn = gat.shape
    return gat.transpose(1, 0, 2, 3).reshape(gg, N_DEV * rr, nn)


def kernel(x, pre_norm, post_norm, even_w_in, even_w_out, ssm_a_re, ssm_a_im, ssm_log_dt, ssm_b_re, ssm_b_im, ssm_c_re, ssm_c_im, ssm_d, ssm_glu_w, ssm_glu_b, odd_w_in, pool_w, pool_scale, odd_w_out, loss_target, m_pre_norm, m_post_norm, m_even_w_in, m_even_w_out, m_ssm_a_re, m_ssm_a_im, m_ssm_log_dt, m_ssm_b_re, m_ssm_b_im, m_ssm_c_re, m_ssm_c_im, m_ssm_d, m_ssm_glu_w, m_ssm_glu_b, m_odd_w_in, m_pool_w, m_pool_scale, m_odd_w_out, v_pre_norm, v_post_norm, v_even_w_in, v_even_w_out, v_ssm_a_re, v_ssm_a_im, v_ssm_log_dt, v_ssm_b_re, v_ssm_b_im, v_ssm_c_re, v_ssm_c_im, v_ssm_d, v_ssm_glu_w, v_ssm_glu_b, v_odd_w_in, v_pool_w, v_pool_scale, v_odd_w_out):
    w = dict(pre_norm=pre_norm, post_norm=post_norm, even_w_in=even_w_in, even_w_out=even_w_out, ssm_a_re=ssm_a_re,
             ssm_a_im=ssm_a_im, ssm_log_dt=ssm_log_dt, ssm_b_re=ssm_b_re, ssm_b_im=ssm_b_im, ssm_c_re=ssm_c_re,
             ssm_c_im=ssm_c_im, ssm_d=ssm_d, ssm_glu_w=ssm_glu_w, ssm_glu_b=ssm_glu_b, odd_w_in=odd_w_in,
             pool_w=pool_w, pool_scale=pool_scale, odd_w_out=odd_w_out)
    mom = dict(pre_norm=m_pre_norm, post_norm=m_post_norm, even_w_in=m_even_w_in, even_w_out=m_even_w_out,
               ssm_a_re=m_ssm_a_re, ssm_a_im=m_ssm_a_im, ssm_log_dt=m_ssm_log_dt, ssm_b_re=m_ssm_b_re,
               ssm_b_im=m_ssm_b_im, ssm_c_re=m_ssm_c_re, ssm_c_im=m_ssm_c_im, ssm_d=m_ssm_d, ssm_glu_w=m_ssm_glu_w,
               ssm_glu_b=m_ssm_glu_b, odd_w_in=m_odd_w_in, pool_w=m_pool_w, pool_scale=m_pool_scale,
               odd_w_out=m_odd_w_out)
    var = dict(pre_norm=v_pre_norm, post_norm=v_post_norm, even_w_in=v_even_w_in, even_w_out=v_even_w_out,
               ssm_a_re=v_ssm_a_re, ssm_a_im=v_ssm_a_im, ssm_log_dt=v_ssm_log_dt, ssm_b_re=v_ssm_b_re,
               ssm_b_im=v_ssm_b_im, ssm_c_re=v_ssm_c_re, ssm_c_im=v_ssm_c_im, ssm_d=v_ssm_d, ssm_glu_w=v_ssm_glu_w,
               ssm_glu_b=v_ssm_glu_b, odd_w_in=v_odd_w_in, pool_w=v_pool_w, pool_scale=v_pool_scale,
               odd_w_out=v_odd_w_out)
    me = _my_index()
    scale_cols = pool_scale.shape[1]

    def start_gather(tag, layer, keys, after=()):
        i = layer // 2
        shards = [w[FAMILY[(layer % 2, k)]][i].astype(BF16) for k in keys]
        if layer % 2 == 1:
            shards.append(jnp.pad(pool_scale[i][None], ((0, PACK_ROWS_ALIGN - 1), (0, 0))))
        return _xchg_start(f"gather_start_{tag}", shards, True, after)

    gather_started = {0: start_gather("0", 0, EVEN_SHARDED[:1])}
    small = {nm: w[nm] for nm in SMALL_NAMES}

    def get_weights(layer, after):
        keys = EVEN_SHARDED[:1] if layer == 0 else _sharded_keys(layer)
        lands = _xchg_wait(f"gather_wait_{layer}", gather_started[layer], True, after)
        wts = {k: _from_gathered(k, gat) for k, gat in zip(keys, lands)}
        if layer % 2 == 1:
            wts["pool_scale"] = lands[-1][:, 0, :].reshape(N_DEV * scale_cols)
        if layer == 0:
            prev = gather_started["0_late"] = start_gather("0_late", 0, EVEN_SHARDED[1:], after=(lands[0],))
            for later in (1, 2, 3):
                prev = gather_started[later] = start_gather(str(later), later, _sharded_keys(later), after=(prev[4],))
            wts["token"] = sum(gather_started[tag][4][0, 0] for tag in ("0_late", 1, 2, 3))

            def late(after_late):
                late_lands = _xchg_wait("gather_wait_0_late", gather_started["0_late"], True, (after_late,))
                return tuple(_from_gathered(k, gat) for k, gat in zip(EVEN_SHARDED[1:], late_lands))

            wts["late"] = late
        elif layer == 2:
            wts["late"] = lambda after_late: (wts["w_out"], wts["glu_w"])
        return wts

    scatter_started = []

    def on_w(layer, gw):
        keys = tuple(k for k in _sharded_keys(layer) if k in gw)
        started = _xchg_start(f"scatter_start_{layer}_{keys[0]}", [_to_slots(k, gw[k]) for k in keys], False)
        scatter_started.append((layer, keys, started))
        return started[4]

    def wait_scatters(layers, after):
        for layer, keys, started in scatter_started:
            if layer in layers:
                lands = _xchg_wait(f"scatter_wait_{layer}_{keys[0]}", started, False, after)
                for k, land in zip(keys, lands):
                    recv[(layer, k)] = land

    packed_names = ("pre_norm", "post_norm") + SSM_NAMES + ("ssm_d", "ssm_glu_b")
    tails = {nm: (SSM_GROUPS, SSM_STATE * SSM_GROUP) if nm in ("ssm_b_re", "ssm_b_im") else w[nm].shape[1:]
             for nm in packed_names}

    layer_grads = {}
    early_started, mid_started = [], []

    def on_ssm(layer, ssm_grads):
        if layer != 0:
            return None
        mid_started.append(_xchg_start("mid_start", [_pack(list(ssm_grads))], True))
        return mid_started[0][4]

    def on_grads(layer, lg):
        layer_grads[layer] = lg
        zero = jnp.zeros((), F32)
        if layer == 1:
            lgs = layer_grads
            early = ([jnp.concatenate([lgs[l][k] for l in (1, 2, 3)], axis=0) for k in ("pre", "post")]
                     + list(lgs[2]["ssm"]) + [lgs[2]["ssm_d"], lgs[2]["glu_b"],
                                              jnp.concatenate([lgs[1]["pool_scale"], lgs[3]["pool_scale"]], axis=0)])
            early_started.append(_xchg_start("small_start", [_pack(early)], True))
            zero = zero + early_started[0][4][0, 0]
        return zero

    loss_local, grad_x, token = _local_step(x[0], loss_target[0], small, get_weights, on_w, on_ssm, on_grads,
                                            zero=gather_started[0][4][0, 0])

    lg0 = layer_grads[0]
    late_started = _xchg_start("late_start", [_pack([lg0["pre"], lg0["post"], lg0["ssm_d"], lg0["glu_b"],
                                                     loss_local.reshape(1)]) + token], True)

    def adam_family(parity, k):
        nm = FAMILY[(parity, k)]
        shp = w[nm].shape
        cols = shp[-1]
        slot_list = [recv[(parity + 2 * i, k)].reshape(N_DEV, -1, cols) for i in range(2)]
        outs = _adam_layers(w[nm].reshape(2, -1, cols), slot_list, mom[nm].reshape(2, -1, cols),
                            var[nm].reshape(2, -1, cols), name=f"adam_{nm}")
        return [o.reshape(shp) for o in outs]

    recv, res = {}, {}
    wait_scatters((3, 1), (late_started[4],))
    for k in ODD_SHARDED:
        res[FAMILY[(1, k)]] = adam_family(1, k)
    odd_done = tuple(res[FAMILY[(1, k)]][0] for k in ODD_SHARDED)

    (early_slots,) = _xchg_wait("small_wait", early_started[0], True, odd_done)
    (mid_slots,) = _xchg_wait("mid_wait", mid_started[0], True, odd_done)
    early_shapes = [(w[nm].shape[0] - 1,) + tails[nm] for nm in packed_names] + [(2, N_DEV * scale_cols)]
    g_early = _unpack(_sum_slots(early_slots, "sum_small_early"), early_shapes)
    g_mid = _unpack(_sum_slots(mid_slots, "sum_small_mid"), [(1,) + tails[nm] for nm in SSM_NAMES])

    (late_slots,) = _xchg_wait("late_wait", late_started, True, (g_early[0], g_mid[0]))
    wait_scatters((2, 0), (late_slots,))
    for k in EVEN_SHARDED:
        res[FAMILY[(0, k)]] = adam_family(0, k)

    late_names = ("pre_norm", "post_norm", "ssm_d", "ssm_glu_b")
    g_late = _unpack(_sum_slots(late_slots, "sum_small_late"), [(1,) + tails[nm] for nm in late_names] + [(1,)])
    g_first = dict(zip(late_names, g_late))
    g_first.update(zip(SSM_NAMES, g_mid))
    dense = lambda nm, a: a.reshape((a.shape[0],) + tails[nm])
    outs = _adam_params([(dense(nm, w[nm]), dense(nm, mom[nm]), dense(nm, var[nm]), g_first[nm], g_early[j])
                         for j, nm in enumerate(packed_names)], "adam_small")
    for nm, four in zip(packed_names, outs):
        res[nm] = [o.reshape(w[nm].shape) for o in four]
    loss = g_late[-1].reshape(())
    g_scale = lax.dynamic_slice_in_dim(g_early[-1], me * scale_cols, scale_cols, axis=1)
    pad = ((0, PACK_ROWS_ALIGN - 2), (0, 0))
    outs = _adam(jnp.pad(pool_scale, pad), jnp.pad(g_scale, pad)[None], jnp.pad(m_pool_scale, pad),
                 jnp.pad(v_pool_scale, pad), name="adam_pool_scale")
    res["pool_scale"] = [o[:2] for o in outs]

    out = [loss, grad_x[None]]
    for kind in range(4):
        out += [res[nm][kind] for nm in WEIGHT_ORDER]
    return tuple(out)
```

```python
import functools
import math

import jax
import jax.numpy as jnp
from jax import lax
from jax.experimental import pallas as pl
from jax.experimental.pallas import tpu as pltpu

F32 = jnp.float32
BF16 = jnp.bfloat16
SDS = jax.ShapeDtypeStruct

N_DEV = 8
S = 2048
D = 1024
HEAD_DIM = 64
ROT_DIM = 16
ROPE_THETA = 500000.0
ATT_W = 1024
SSM_W = 512
SSM_GROUPS = 32
SSM_GROUP = 16
SSM_STATE = 64
N_CPLX = SSM_GROUPS * SSM_STATE
POOL_W = 2048
POOL_GROUP = 512
EVEN_IN = 5120
EVEN_OUT = 1536
ODD_IN = 4096
RMS_EPS = 1e-6
LANES = 128
VMEM_LIMIT = 48 * 1024 * 1024

ADAM_LR = 0.001
ADAM_B1 = 0.9
ADAM_B2 = 0.999
ADAM_EPS = 1e-08
ADAM_WD = 0.01
ADAM_STEP = 10

MESH_ID = pl.DeviceIdType.MESH
NN = (((1,), (0,)), ((), ()))
NT = (((1,), (1,)), ((), ()))
TN = (((0,), (0,)), ((), ()))
_DN = {"nn": NN, "nt": NT, "tn": TN}


def _cparams(sem):
    return pltpu.CompilerParams(dimension_semantics=sem, vmem_limit_bytes=VMEM_LIMIT)


def _in_hbm(arrs):
    return [pltpu.with_memory_space_constraint(a, pltpu.HBM) for a in arrs]


MM_TILES = (1024, 768, 512)


def _tile(dim):
    return next((t for t in MM_TILES if dim % t == 0), dim)


BLOCK_PAIR = 2
MAX_WHOLE_K = 2048


def _mm(a, b, mode, out_dtype, b_blocks=False, out_blocks=False, after=()):
    if b_blocks:
        nblk, rows, cb = b.shape
        b2_shape = (rows, nblk * cb)
    else:
        b2_shape = b.shape
    if mode == "nn":
        (m, k), n = a.shape, b2_shape[1]
    elif mode == "nt":
        (m, k), n = a.shape, b2_shape[0]
    else:
        (k, m), n = a.shape, b2_shape[1]
    tm, tn, tk = _tile(m), _tile(n), _tile(k)
    if k <= MAX_WHOLE_K:
        tk = k
    if b_blocks and mode == "nn":
        tn = BLOCK_PAIR * cb
    if b_blocks and mode == "nt":
        tk = BLOCK_PAIR * cb
    if out_blocks:
        cb = n // N_DEV
        tn = BLOCK_PAIR * cb
        tk = k
    nk = k // tk

    def body(a_ref, b_ref, *rest):
        o_ref, acc_ref = rest[-2:]
        kk = pl.program_id(2)
        bv = jnp.concatenate([b_ref[p] for p in range(BLOCK_PAIR)], axis=1) if b_blocks else b_ref[...]
        part = lax.dot_general(a_ref[...].astype(BF16), bv.astype(BF16), _DN[mode], preferred_element_type=F32)

        def write(res):
            if out_blocks:
                for p in range(BLOCK_PAIR):
                    o_ref[p] = res[:, p * cb:(p + 1) * cb].astype(o_ref.dtype)
            else:
                o_ref[...] = res.astype(o_ref.dtype)

        if nk == 1:
            write(part)
            return

        @pl.when(kk == 0)
        def _():
            acc_ref[...] = part

        @pl.when((kk > 0) & (kk < nk - 1))
        def _():
            acc_ref[...] += part

        @pl.when(kk == nk - 1)
        def _():
            write(acc_ref[...] + part)

    if mode == "nn":
        a_spec = pl.BlockSpec((tm, tk), lambda i, j, kk: (i, kk))
        b_spec = pl.BlockSpec((tk, tn), lambda i, j, kk: (kk, j))
    elif mode == "nt":
        a_spec = pl.BlockSpec((tm, tk), lambda i, j, kk: (i, kk))
        b_spec = pl.BlockSpec((tn, tk), lambda i, j, kk: (j, kk))
    else:
        a_spec = pl.BlockSpec((tk, tm), lambda i, j, kk: (kk, i))
        b_spec = pl.BlockSpec((tk, tn), lambda i, j, kk: (kk, j))
    if b_blocks and mode == "nn":
        b_spec = pl.BlockSpec((BLOCK_PAIR, tk, cb), lambda i, j, kk: (j, kk, 0))
    if b_blocks and mode == "nt":
        b_spec = pl.BlockSpec((BLOCK_PAIR, tn, cb), lambda i, j, kk: (kk, j, 0))
    out_spec = pl.BlockSpec((tm, tn), lambda i, j, kk: (i, j))
    out_shape = SDS((m, n), out_dtype)
    if out_blocks:
        out_spec = pl.BlockSpec((BLOCK_PAIR, tm, cb), lambda i, j, kk: (j, i, 0))
        out_shape = SDS((N_DEV, m, cb), out_dtype)
    return pl.pallas_call(
        body, name=f"mm_{mode}_{m}x{k}x{n}",
        grid=(m // tm, n // tn, nk),
        in_specs=[a_spec, b_spec] + [pl.BlockSpec(memory_space=pl.ANY)] * len(after),
        out_specs=out_spec,
        out_shape=out_shape,
        scratch_shapes=[pltpu.VMEM((tm, tn) if nk > 1 else (8, LANES), F32)],
        compiler_params=_cparams(("parallel", "parallel", "arbitrary")),
    )(a, b, *after)


def _gmm(a, b, mode, out_dtype, tm=S):
    ng, gw = POOL_W // POOL_GROUP, POOL_GROUP
    ns = S // tm
    if mode in ("nn", "nt"):
        def body(a_ref, b_ref, o_ref):
            o_ref[...] = lax.dot_general(a_ref[...].astype(BF16), b_ref[...].astype(BF16), _DN[mode],
                                         preferred_element_type=F32).astype(o_ref.dtype)

        return pl.pallas_call(
            body, name=f"gmm_{mode}", grid=(ng, ns),
            in_specs=[pl.BlockSpec((tm, gw), lambda g, i: (i, g)),
                      pl.BlockSpec((None, gw, gw), lambda g, i: (g, 0, 0))],
            out_specs=pl.BlockSpec((tm, gw), lambda g, i: (i, g)),
            out_shape=SDS((S, POOL_W), out_dtype),
            compiler_params=_cparams(("parallel", "parallel")),
        )(a, b)

    def body_tn(a_ref, b_ref, o_ref, acc_ref):
        i = pl.program_id(1)

        @pl.when(i == 0)
        def _():
            acc_ref[...] = jnp.zeros_like(acc_ref)

        acc_ref[...] += lax.dot_general(a_ref[...].astype(BF16), b_ref[...].astype(BF16), TN,
                                        preferred_element_type=F32)

        @pl.when(i == ns - 1)
        def _():
            o_ref[...] = acc_ref[...].astype(o_ref.dtype)

    return pl.pallas_call(
        body_tn, name="gmm_tn", grid=(ng, ns),
        in_specs=[pl.BlockSpec((tm, gw), lambda g, i: (i, g)),
                  pl.BlockSpec((tm, gw), lambda g, i: (i, g))],
        out_specs=pl.BlockSpec((None, gw, gw), lambda g, i: (g, 0, 0)),
        out_shape=SDS((ng, gw, gw), out_dtype),
        scratch_shapes=[pltpu.VMEM((gw, gw), F32)],
        compiler_params=_cparams(("parallel", "arbitrary")),
    )(a, b)


def _rowwise(fn, inputs, out_defs, acc_defs=(), tm=512, name=None, after=()):
    n_in, n_out, n_acc = len(inputs), len(out_defs), len(acc_defs)
    n_after = len(after)
    in_specs, args = [], []
    for arr, width, cb in inputs:
        if arr.shape[0] == 1:
            in_specs.append(pl.BlockSpec((1, width), lambda i, cb=cb: (0, cb)))
        else:
            in_specs.append(pl.BlockSpec((tm, width), lambda i, cb=cb: (i, cb)))
        args.append(arr)
    out_defs = [d if len(d) == 4 else (d[0], d[1], d[0], 0) for d in out_defs]
    out_shape = [SDS((S, ww), dt) for _, dt, ww, _ in out_defs] + [SDS((1, w), F32) for w in acc_defs]
    out_specs = ([pl.BlockSpec((tm, w), lambda i, cb=cb: (i, cb)) for w, _, _, cb in out_defs]
                 + [pl.BlockSpec((1, w), lambda i: (0, 0)) for w in acc_defs])

    def kern(*refs):
        vals = [r[...] for r in refs[:n_in]]
        outs, accs = fn(*vals)
        out_refs = refs[n_in + n_after:]
        for r, v in zip(out_refs[:n_out], outs):
            r[...] = v.astype(r.dtype)
        if n_acc:
            acc_refs = out_refs[n_out:]

            @pl.when(pl.program_id(0) == 0)
            def _():
                for r in acc_refs:
                    r[...] = jnp.zeros_like(r)

            for r, v in zip(acc_refs, accs):
                r[...] += jnp.sum(v, axis=0, keepdims=True)

    res = pl.pallas_call(
        kern, name=name, grid=(S // tm,), in_specs=in_specs + [pl.BlockSpec(memory_space=pl.ANY)] * n_after,
        out_specs=out_specs, out_shape=out_shape, compiler_params=_cparams(("arbitrary",)),
    )(*args, *after)
    return res


def _sigmoid(x):
    return 1.0 / (1.0 + jnp.exp(-x))


def _silu_and_grad(x):
    s = _sigmoid(x)
    return x * s, s * (1.0 + x * (1.0 - s))


_GELU_K = math.sqrt(2.0 / math.pi)
_GELU_C = 0.044715


def _gelu_and_grad(x):
    t = jnp.tanh(_GELU_K * (x + _GELU_C * (x * x * x)))
    cdf = 0.5 * (1.0 + t)
    grad = cdf + 0.5 * x * (1.0 - t * t) * (_GELU_K * (1.0 + 3.0 * _GELU_C * x * x))
    return x * cdf, grad


def _rms(xv, gain):
    r = lax.rsqrt(jnp.mean(xv * xv, axis=-1, keepdims=True) + RMS_EPS)
    return xv * r * gain


def _rms_bwd(dout, xv, gain):
    r = lax.rsqrt(jnp.mean(xv * xv, axis=-1, keepdims=True) + RMS_EPS)
    xhat = xv * r
    dxhat = dout * gain
    dx = r * (dxhat - xhat * jnp.mean(dxhat * xhat, axis=-1, keepdims=True))
    return dx, dout * xhat


def _norm_fwd(x, gain):
    (h,) = _rowwise(lambda xv, g: ((_rms(xv, g),), ()), [(x, D, 0), (gain, D, 0)], [(D, BF16)], name="norm_fwd")
    return h


def _post_fwd(x, y, gain, next_gain):
    def fn(xv, yv, g, gn):
        out = xv + _rms(yv, g)
        return (out, _rms(out, gn)), ()

    return _rowwise(fn, [(x, D, 0), (y, D, 0), (gain, D, 0), (next_gain, D, 0)], [(D, F32), (D, BF16)],
                    name="post_fwd")


def _post_fwd_loss(x, y, gain, tgt):
    def fn(xv, yv, g, tv):
        e = xv + _rms(yv, g) - tv
        return (e * (1.0 / D),), (e * e,)

    return _rowwise(fn, [(x, D, 0), (y, D, 0), (gain, D, 0), (tgt, D, 0)], [(D, F32)], [D], name="post_fwd_loss")


def _post_bwd(g, y, gain):
    def fn(gv, yv, gn):
        dx, dg = _rms_bwd(gv, yv, gn)
        return (dx,), (dg,)

    return _rowwise(fn, [(g, D, 0), (y, D, 0), (gain, D, 0)], [(D, BF16)], [D], name="post_bwd")


def _pre_bwd(g, dh, x, gain):
    def fn(gv, dhv, xv, gn):
        dx, dg = _rms_bwd(dhv, xv, gn)
        return (gv + dx,), (dg,)

    return _rowwise(fn, [(g, D, 0), (dh, D, 0), (x, D, 0), (gain, D, 0)], [(D, F32)], [D], name="pre_bwd")


def _pool(u_arr, col_block, transpose, out_dtype, into=None, tc=256):
    n_t = POOL_W // tc
    per_group = POOL_GROUP // tc

    def body(u_ref, *rest):
        o_ref = rest[-1]
        grp = pl.program_id(0) // per_group
        t = lax.broadcasted_iota(jnp.int32, (S, 1), 0)
        for g in range(POOL_W // POOL_GROUP):
            @pl.when(grp == g)
            def _(g=g):
                xv = u_ref[...]
                cnt = jnp.minimum(t + 1, 2 << g).astype(F32)
                cur = xv / cnt if transpose else xv
                for k in (1, 2, 4, 8)[:g + 1]:
                    if transpose:
                        cur = cur + jnp.where(t < S - k, pltpu.roll(cur, S - k, 0), 0.0)
                    else:
                        cur = cur + jnp.where(t >= k, pltpu.roll(cur, k, 0), 0.0)
                res = cur - xv if transpose else cur / cnt - xv
                o_ref[...] = res.astype(o_ref.dtype)

    in_specs = [pl.BlockSpec((S, tc), lambda c: (0, col_block * n_t + c))]
    args = [u_arr]
    if into is not None:
        in_specs.append(pl.BlockSpec(memory_space=pl.ANY))
        args.append(into)
    return pl.pallas_call(
        body, name="pool_bwd" if transpose else "pool_fwd", grid=(n_t,),
        in_specs=in_specs,
        out_specs=pl.BlockSpec((S, tc), lambda c: (0, c)),
        out_shape=SDS((S, POOL_W) if into is None else into.shape, out_dtype),
        input_output_aliases={} if into is None else {1: 0},
        compiler_params=_cparams(("parallel",)),
    )(*args)


def _rope_tables(zero):
    pos = jnp.arange(S, dtype=jnp.int32).astype(F32) + zero
    inv_freq = ROPE_THETA ** (-jnp.arange(0, ROT_DIM, 2, dtype=F32) / ROT_DIM)
    ang = pos[:, None] * inv_freq[None, :]
    cos8, sin8 = jnp.cos(ang), jnp.sin(ang)
    half = ROT_DIM // 2
    zeros = jnp.zeros((S, HEAD_DIM - ROT_DIM), F32)
    cos = jnp.concatenate([cos8, cos8, jnp.ones((S, HEAD_DIM - ROT_DIM), F32)], axis=1)
    lo = jnp.concatenate([-sin8, jnp.zeros((S, half), F32), zeros], axis=1)
    hi = jnp.concatenate([jnp.zeros((S, half), F32), sin8, zeros], axis=1)
    rep = LANES // HEAD_DIM
    return jnp.tile(cos, (1, rep)), jnp.tile(lo, (1, rep)), jnp.tile(hi, (1, rep))


def _rotate(xv, cos, lo, hi, transpose):
    width = xv.shape[1]
    rep = width // LANES
    wide = lambda tab: jnp.concatenate([tab] * rep, axis=1)
    half = ROT_DIM // 2
    up = pltpu.roll(xv, width - half, 1)
    dn = pltpu.roll(xv, half, 1)
    mixed = up * wide(lo) + dn * wide(hi)
    return xv * wide(cos) - mixed if transpose else xv * wide(cos) + mixed


def _qkv_prep(proj, tables):
    cos, lo, hi = tables

    def fn(x, c, l, h):
        rot = _rotate(x[:, :2 * ATT_W], c, l, h, False)
        return (jnp.concatenate([(rot[:, :ATT_W] * HEAD_DIM ** -0.5).astype(BF16), rot[:, ATT_W:].astype(BF16),
                                 x[:, 2 * ATT_W:].astype(BF16)], axis=1),), ()

    (qkv,) = _rowwise(fn, [(proj, 3 * ATT_W, 0), (cos, LANES, 0), (lo, LANES, 0), (hi, LANES, 0)],
                      [(3 * ATT_W, BF16)], name="qkv_prep")
    return qkv


ATT_T = 512


def _multiplicity(delta):
    ok = delta >= 0
    near = jnp.where(ok & (delta <= 128), 1.0, 0.0)
    mid = jnp.where(ok & (delta <= 512) & ((delta & 3) == 0), 1.0, 0.0)
    far = jnp.where(ok & ((delta & 15) == 0), 1.0, 0.0)
    return near + mid + far


def _attention_bias(zero):
    t = ATT_T
    pos = jnp.arange(t, dtype=jnp.int32) + jnp.asarray(zero).astype(jnp.int32)
    delta = jnp.arange(S // t, dtype=jnp.int32)[:, None, None] * t + pos[None, :, None] - pos[None, None, :]
    mult = _multiplicity(delta)
    return jnp.where(mult > 0.0, jnp.log(jnp.maximum(mult, 1.0)), -1e30).astype(F32)


def _head_split(v, first):
    zero = jnp.zeros_like(v)
    return [jnp.where(first, v, zero), jnp.where(first, zero, v)]


def _flash_fwd(qkv, bias):
    t = ATT_T
    n_hp = ATT_W // LANES

    def body(q_ref, k_ref, v_ref, b_ref, o_ref, lse_ref):
        i = pl.program_id(1)
        first = lax.broadcasted_iota(jnp.int32, (1, LANES), 1) < HEAD_DIM
        qs = _head_split(q_ref[...], first)

        def kv_step(j, carry):
            m0, l0, m1, l1, acc = carry
            off = pl.multiple_of(j * t, t)
            kb = k_ref[pl.ds(off, t), :]
            vs = _head_split(v_ref[pl.ds(off, t), :], first)
            bias_t = b_ref[i - j]
            new = []
            pv = None
            for h, (m_prev, l_prev) in enumerate(((m0, l0), (m1, l1))):
                s = lax.dot_general(qs[h], kb, NT, preferred_element_type=F32) + bias_t
                m_new = jnp.maximum(m_prev, jnp.max(s, axis=1, keepdims=True))
                p = jnp.exp(s - m_new)
                alpha = jnp.exp(m_prev - m_new)
                l_new = alpha * l_prev + jnp.sum(p, axis=1, keepdims=True)
                d = lax.dot_general(p.astype(BF16), vs[h], NN, preferred_element_type=F32)
                pv = d if pv is None else pv + d
                new.append((m_new, l_new, alpha))
            acc = acc * jnp.where(first, new[0][2], new[1][2]) + pv
            return new[0][0], new[0][1], new[1][0], new[1][1], acc

        neg = jnp.full((t, 1), -1e30, F32)
        zero = jnp.zeros((t, 1), F32)
        m0, l0, m1, l1, acc = lax.fori_loop(0, i + 1, kv_step, (neg, zero, neg, zero, jnp.zeros((t, LANES), F32)))
        o_ref[...] = acc * jnp.where(first, 1.0 / l0, 1.0 / l1)
        lse_ref[...] = jnp.where(first, m0 + jnp.log(l0), m1 + jnp.log(l1))

    blk = pl.BlockSpec((t, LANES), lambda hp, i: (i, hp))
    k_full = pl.BlockSpec((S, LANES), lambda hp, i: (0, n_hp + hp))
    v_full = pl.BlockSpec((S, LANES), lambda hp, i: (0, 2 * n_hp + hp))
    return pl.pallas_call(
        body, name="flash_fwd", grid=(n_hp, S // t),
        in_specs=[blk, k_full, v_full, pl.BlockSpec((S // t, t, t), lambda hp, i: (0, 0, 0))], out_specs=[blk, blk],
        out_shape=[SDS((S, ATT_W), F32), SDS((S, ATT_W), F32)],
        compiler_params=_cparams(("parallel", "arbitrary")),
    )(qkv, qkv, qkv, bias)


def _flash_bwd(qkv, o, do, lse, bias, after=()):
    t = ATT_T
    n_hp = ATT_W // LANES
    n_t = S // t

    def body(q_ref, k_ref, v_ref, o_ref, do_ref, lse_ref, b_ref, *rest):
        dq_ref, dk_ref, dv_ref = rest[-3:]
        j = pl.program_id(1)
        first = lax.broadcasted_iota(jnp.int32, (1, LANES), 1) < HEAD_DIM

        @pl.when(j == 0)
        def _():
            dq_ref[...] = jnp.zeros_like(dq_ref)

        kb = k_ref[...]
        vb = v_ref[...]
        ks = _head_split(kb, first)

        def q_step(i, carry):
            dk_acc, dv_acc = carry
            rows = pl.ds(pl.multiple_of(i * t, t), t)
            qs = _head_split(q_ref[rows, :], first)
            dob = do_ref[rows, :]
            prod = dob * o_ref[rows, :]
            d_all = jnp.sum(prod, axis=1, keepdims=True)
            d0 = jnp.sum(jnp.where(first, prod, 0.0), axis=1, keepdims=True)
            lse_b = lse_ref[rows, :]
            lse0 = jnp.max(jnp.where(first, lse_b, -jnp.inf), axis=1, keepdims=True)
            lse1 = jnp.max(jnp.where(first, -jnp.inf, lse_b), axis=1, keepdims=True)
            dos = _head_split(dob.astype(BF16), first)
            bias_t = b_ref[i - j]
            dq_t = jnp.zeros((t, LANES), F32)
            for h, (lse_h, d_h) in enumerate(((lse0, d0), (lse1, d_all - d0))):
                s = lax.dot_general(qs[h], kb, NT, preferred_element_type=F32)
                p = jnp.exp(s + (bias_t - lse_h))
                dp = lax.dot_general(dos[h], vb, NT, preferred_element_type=F32)
                ds = (p * (dp - d_h)).astype(BF16)
                dv_acc = dv_acc + lax.dot_general(p.astype(BF16), dos[h], TN, preferred_element_type=F32)
                dk_acc = dk_acc + lax.dot_general(ds, qs[h], TN, preferred_element_type=F32)
                dq_t = dq_t + lax.dot_general(ds, ks[h], NN, preferred_element_type=F32)
            dq_ref[rows, :] += dq_t
            return dk_acc, dv_acc

        zero = jnp.zeros((t, LANES), F32)
        dk_acc, dv_acc = lax.fori_loop(j, n_t, q_step, (zero, zero))
        dk_ref[...] = dk_acc
        dv_ref[...] = dv_acc

    blk = pl.BlockSpec((t, LANES), lambda hp, j: (j, hp))
    full = pl.BlockSpec((S, LANES), lambda hp, j: (0, hp))
    k_blk = pl.BlockSpec((t, LANES), lambda hp, j: (j, n_hp + hp))
    v_blk = pl.BlockSpec((t, LANES), lambda hp, j: (j, 2 * n_hp + hp))
    return pl.pallas_call(
        body, name="flash_bwd", grid=(n_hp, n_t),
        in_specs=([full, k_blk, v_blk, full, full, full, pl.BlockSpec((n_t, t, t), lambda hp, j: (0, 0, 0))]
                  + [pl.BlockSpec(memory_space=pl.ANY)] * len(after)),
        out_specs=[full, blk, blk],
        out_shape=[SDS((S, ATT_W), F32)] * 3,
        compiler_params=_cparams(("parallel", "arbitrary")),
    )(qkv, qkv, qkv, o, do, lse, bias, *after)


SCAN_T = 256
SCAN_GROUP = 8
SCAN_STEPS = (1, 2, 4)
ST_ROWS = 2 * N_CPLX // LANES
HALF = ST_ROWS // 2


def _scan_tables(lam_t):
    lam = lax.complex(lam_t[:HALF].reshape(N_CPLX), lam_t[HALF:].reshape(N_CPLX))
    pows = [lam]
    for _ in range(SCAN_GROUP - 1):
        pows.append(pows[-1] * lam)
    pows = jnp.stack(pows)
    sub = jnp.arange(SCAN_GROUP)[:, None]
    fwd = [jnp.where(sub >= k, pows[k - 1][None, :], 0.0) for k in SCAN_STEPS] + [pows]
    conj = jnp.conj(pows)
    bwd = [jnp.where(sub <= SCAN_GROUP - 1 - k, conj[k - 1][None, :], 0.0) for k in SCAN_STEPS] + [conj[::-1]]

    def pack(tabs):
        return jnp.stack([jnp.concatenate([jnp.real(t), jnp.imag(t)], axis=1) for t in tabs]).astype(F32)

    return pack(fwd), pack(bwd)


def _cmul_add(xr, xi, lr, li, sr, si):
    return xr + lr * sr - li * si, xi + lr * si + li * sr


def _group_scan(xr, xi, tab_ref, cr, ci, reverse):
    for j, k in enumerate(SCAN_STEPS):
        shift = SCAN_GROUP - k if reverse else k
        xr, xi = _cmul_add(xr, xi, tab_ref[j, :, :N_CPLX], tab_ref[j, :, N_CPLX:],
                           pltpu.roll(xr, shift, 0), pltpu.roll(xi, shift, 0))
    return _cmul_add(xr, xi, tab_ref[3, :, :N_CPLX], tab_ref[3, :, N_CPLX:],
                     jnp.broadcast_to(cr, (SCAN_GROUP, N_CPLX)), jnp.broadcast_to(ci, (SCAN_GROUP, N_CPLX)))


SSM_SUPER = 4
SB_ROWS = SSM_W // SSM_SUPER
SB_COLS = N_CPLX // SSM_SUPER


def _super_blocks():
    return [(slice(b * SB_ROWS, (b + 1) * SB_ROWS), slice(h * SB_COLS, (h + 1) * SB_COLS),
             slice(h * N_CPLX + b * SB_COLS, h * N_CPLX + (b + 1) * SB_COLS))
            for b in range(SSM_SUPER) for h in range(2)]


def _dot16(a, b, dims):
    return lax.dot_general(a.astype(BF16), b.astype(BF16), dims, preferred_element_type=F32)


def _s5_fwd(tab, u_arr, u_cols, w_b, w_ct):
    nc = N_CPLX

    def body(tab_ref, u_ref, wb_ref, wct_ref, st_ref, y_ref, carry, bu_scr):
        @pl.when(pl.program_id(0) == 0)
        def _():
            carry[...] = jnp.zeros_like(carry)

        for rows_b, cols_c, cols_s in _super_blocks():
            bu_scr[:, cols_s] = _dot16(u_ref[:, rows_b], wb_ref[rows_b, cols_c], NN)

        def group(a, c):
            rows = pl.ds(pl.multiple_of(a * SCAN_GROUP, SCAN_GROUP), SCAN_GROUP)
            xr, xi = _group_scan(bu_scr[rows, :nc], bu_scr[rows, nc:], tab_ref, c[0], c[1], False)
            st_ref[rows, :nc] = xr
            st_ref[rows, nc:] = xi
            return xr[SCAN_GROUP - 1:SCAN_GROUP, :], xi[SCAN_GROUP - 1:SCAN_GROUP, :]

        cr, ci = lax.fori_loop(0, SCAN_T // SCAN_GROUP, group, (carry[:, :nc], carry[:, nc:]), unroll=2)
        carry[:, :nc] = cr
        carry[:, nc:] = ci

        for b in range(SSM_SUPER):
            (rows_b, cols_re, st_re), (_, cols_im, st_im) = _super_blocks()[2 * b:2 * b + 2]
            y_ref[:, rows_b] = (_dot16(st_ref[:, st_re], wct_ref[rows_b, cols_re], NT)
                                + _dot16(st_ref[:, st_im], wct_ref[rows_b, cols_im], NT))

    const = lambda shape: pl.BlockSpec(shape, lambda i: (0,) * len(shape))
    return pl.pallas_call(
        body, name="s5_fwd", grid=(S // SCAN_T,),
        in_specs=[const((4, SCAN_GROUP, 2 * nc)), pl.BlockSpec((SCAN_T, SSM_W), lambda i: (i, u_cols[0] // SSM_W)),
                  const((SSM_W, 2 * SB_COLS)), const((SSM_W, 2 * SB_COLS))],
        out_specs=[pl.BlockSpec((SCAN_T, 2 * nc), lambda i: (i, 0)), pl.BlockSpec((SCAN_T, SSM_W), lambda i: (i, 0))],
        out_shape=[SDS((S, 2 * nc), F32), SDS((S, SSM_W), F32)],
        scratch_shapes=[pltpu.VMEM((1, 2 * nc), F32), pltpu.VMEM((SCAN_T, 2 * nc), F32)],
        compiler_params=_cparams(("arbitrary",)),
    )(tab, u_arr, w_b, w_ct)


def _s5_bwd(tab, dy, states, u_arr, u_cols, w_b, w_ct):
    n_blk = S // SCAN_T
    nc = N_CPLX

    def body(tab_ref, dy_ref, x_ref, u_ref, wb_ref, wct_ref, du_ref, dlam_ref, dwb_ref, dwct_ref,
             carry, acc, d_scr, g_ref):
        i = pl.program_id(0)

        @pl.when(i == 0)
        def _():
            carry[...] = jnp.zeros_like(carry)
            acc[...] = jnp.zeros_like(acc)
            dwb_ref[...] = jnp.zeros_like(dwb_ref)
            dwct_ref[...] = jnp.zeros_like(dwct_ref)

        for rows_b, cols_c, cols_s in _super_blocks():
            d_scr[:, cols_s] = _dot16(dy_ref[:, rows_b], wct_ref[rows_b, cols_c], NN)

        last_row = lax.broadcasted_iota(jnp.int32, (SCAN_GROUP, 1), 0) == SCAN_GROUP - 1
        d_ref = d_scr

        def group(j, c):
            cr, ci = c
            rows = pl.ds(pl.multiple_of((SCAN_T // SCAN_GROUP - 1 - j) * SCAN_GROUP, SCAN_GROUP), SCAN_GROUP)
            gr, gi = _group_scan(d_ref[rows, :nc], d_ref[rows, nc:], tab_ref, cr, ci, True)
            g_ref[rows, :nc] = gr
            g_ref[rows, nc:] = gi
            nr = jnp.where(last_row, jnp.broadcast_to(cr, (SCAN_GROUP, nc)), pltpu.roll(gr, SCAN_GROUP - 1, 0))
            ni = jnp.where(last_row, jnp.broadcast_to(ci, (SCAN_GROUP, nc)), pltpu.roll(gi, SCAN_GROUP - 1, 0))
            sr, si = x_ref[rows, :nc], x_ref[rows, nc:]
            acc[:, :nc] += nr * sr + ni * si
            acc[:, nc:] += ni * sr - nr * si
            return gr[0:1, :], gi[0:1, :]

        cr, ci = lax.fori_loop(0, SCAN_T // SCAN_GROUP, group, (carry[:, :nc], carry[:, nc:]), unroll=2)
        carry[:, :nc] = cr
        carry[:, nc:] = ci

        for b in range(SSM_SUPER):
            (rows_b, cols_re, st_re), (_, cols_im, st_im) = _super_blocks()[2 * b:2 * b + 2]
            du_ref[:, rows_b] = (_dot16(g_ref[:, st_re], wb_ref[rows_b, cols_re], NT)
                                 + _dot16(g_ref[:, st_im], wb_ref[rows_b, cols_im], NT))
            for cols_c, cols_s in ((cols_re, st_re), (cols_im, st_im)):
                dwb_ref[rows_b, cols_c] += _dot16(u_ref[:, rows_b], g_ref[:, cols_s], TN)
                dwct_ref[rows_b, cols_c] += _dot16(dy_ref[:, rows_b], x_ref[:, cols_s], TN)

        @pl.when(i == n_blk - 1)
        def _():
            dlam_ref[...] = jnp.sum(acc[...], axis=0, keepdims=True)

    const = lambda shape: pl.BlockSpec(shape, lambda i: (0,) * len(shape))
    rows = lambda width, col_block=0: pl.BlockSpec((SCAN_T, width), lambda i: (n_blk - 1 - i, col_block))
    maps = const((SSM_W, 2 * SB_COLS))
    return pl.pallas_call(
        body, name="s5_bwd", grid=(n_blk,),
        in_specs=[const((4, SCAN_GROUP, 2 * nc)), rows(SSM_W), rows(2 * nc), rows(SSM_W, u_cols[0] // SSM_W), maps, maps],
        out_specs=[rows(SSM_W), const((1, 2 * nc)), maps, maps],
        out_shape=[SDS((S, SSM_W), F32), SDS((1, 2 * nc), F32), SDS((SSM_W, 2 * SB_COLS), F32),
                   SDS((SSM_W, 2 * SB_COLS), F32)],
        scratch_shapes=[pltpu.VMEM((1, 2 * nc), F32), pltpu.VMEM((SCAN_GROUP, 2 * nc), F32),
                        pltpu.VMEM((SCAN_T, 2 * nc), F32), pltpu.VMEM((SCAN_T, 2 * nc), F32)],
        compiler_params=_cparams(("arbitrary",)),
    )(tab, dy, states, u_arr, w_b, w_ct)


def _ssm_prep(a_re, a_im, log_dt, b_re, b_im, c_re, c_im):
    lam = lax.complex(a_re, a_im)
    dt = jnp.exp(log_dt)[:, None]
    lam_bar = jnp.exp(lam * dt)
    b_bar = ((lam_bar - 1.0) / lam)[..., None] * lax.complex(b_re, b_im)
    lam_t = jnp.concatenate([jnp.real(lam_bar).reshape(HALF, LANES), jnp.imag(lam_bar).reshape(HALF, LANES)], axis=0)
    groups_per_super = SSM_GROUPS // SSM_SUPER
    on_diag = ((lax.broadcasted_iota(jnp.int32, (SSM_W, SB_COLS), 0) // SSM_GROUP) % groups_per_super
               == lax.broadcasted_iota(jnp.int32, (SSM_W, SB_COLS), 1) // SSM_STATE)

    def compact(m):
        return jnp.where(on_diag, jnp.tile(m.reshape(SSM_W, SSM_STATE), (1, groups_per_super)), 0.0)

    w_b = jnp.concatenate([compact(jnp.real(b_bar).transpose(0, 2, 1)),
                           compact(jnp.imag(b_bar).transpose(0, 2, 1))], axis=1)
    w_ct = jnp.concatenate([compact(c_re), -compact(c_im)], axis=1)
    return lam_t, w_b, w_ct


U_SSM_COLS = (4 * ATT_W, SSM_W)


def _row(v):
    return v.reshape(1, -1)


def _even_fwd(x, h, tail, pre, post, w_in, late_w, glu_b, ssm_d, prep, tables):
    lam_t, w_b, w_ct = prep
    proj = _mm(h, w_in, "nn", F32, b_blocks=True)
    qkv = _qkv_prep(proj, tables[:3])
    w_out, glu_w = late_w(qkv)
    att, lse = _flash_fwd(qkv, tables[3])
    scan_fwd_tab, scan_bwd_tab = _scan_tables(lam_t)
    states, y = _s5_fwd(scan_fwd_tab, proj, U_SSM_COLS, w_b, w_ct)

    def act1(yv, uv, dv):
        return (_gelu_and_grad(yv + dv * uv)[0],), ()

    (z1,) = _rowwise(act1, [(y, SSM_W, 0), (proj, SSM_W, 8), (ssm_d, SSM_W, 0)], [(SSM_W, F32)], name="ssm_act_fwd")
    lin = _mm(z1, glu_w, "nn", F32)

    def gate(att_v, ga, gs, z1v, linv, bv):
        ssm_out = z1v * _sigmoid(linv + bv)
        return (jnp.concatenate([att_v * _silu_and_grad(ga)[0], ssm_out * _silu_and_grad(gs)[0]], axis=1),), ()

    (merged,) = _rowwise(gate, [(att, ATT_W, 0), (proj, ATT_W, 3), (proj, SSM_W, 9), (z1, SSM_W, 0),
                                (lin, SSM_W, 0), (glu_b, SSM_W, 0)], [(EVEN_OUT, BF16)], name="even_gate_fwd")
    yout = _mm(merged, w_out, "nn", F32)
    saved = (x, h, proj, qkv, att, lse, states, y, z1, lin, merged, yout, w_out, glu_w, scan_bwd_tab)
    return tail(x, yout, post) + (saved,)


def _even_bwd(g, saved, pre, post, w_in, late_w, glu_b, ssm_d, prep, tables, on_w, on_ssm):
    x, h, proj, qkv, att, lse, states, y, z1, lin, merged, yout, w_out, glu_w, scan_bwd_tab = saved
    lam_t, w_b, w_ct = prep
    dyout, dpost = _post_bwd(g, yout, post)
    dmerged = _mm(dyout, w_out, "nt", F32)
    dw_out = _mm(merged, dyout, "tn", BF16)

    def gate_bwd(dm_a, dm_s, att_v, ga, gs, z1v, linv, bv):
        sa, dsa = _silu_and_grad(ga)
        ss, dss = _silu_and_grad(gs)
        sig = _sigmoid(linv + bv)
        ssm_out = z1v * sig
        dssm = dm_s * ss
        dlin = dssm * z1v * sig * (1.0 - sig)
        return (dm_a * sa, dm_a * att_v * dsa, dm_s * ssm_out * dss, dssm * sig, dlin), (dlin,)

    datt, dg_att, dg_ssm, dz1a, dlin, dglu_b = _rowwise(
        gate_bwd, [(dmerged, ATT_W, 0), (dmerged, SSM_W, 2), (att, ATT_W, 0), (proj, ATT_W, 3), (proj, SSM_W, 9),
                   (z1, SSM_W, 0), (lin, SSM_W, 0), (glu_b, SSM_W, 0)],
        [(ATT_W, F32), (ATT_W, BF16), (SSM_W, BF16), (SSM_W, F32), (SSM_W, BF16)], [SSM_W], name="even_gate_bwd")
    dz1b = _mm(dlin, glu_w, "nt", F32)
    dglu_w = _mm(z1, dlin, "tn", BF16)

    def act1_bwd(da, db, yv, uv, dv):
        dpre = (da + db) * _gelu_and_grad(yv + dv * uv)[1]
        return (dpre, dpre * dv), (dpre * uv,)

    sent_late_w = on_w(dict(w_out=dw_out, glu_w=dglu_w))
    dy, du_direct, dd = _rowwise(act1_bwd, [(dz1a, SSM_W, 0), (dz1b, SSM_W, 0), (y, SSM_W, 0), (proj, SSM_W, 8),
                                            (ssm_d, SSM_W, 0)], [(SSM_W, BF16), (SSM_W, F32)], [SSM_W],
                                 name="ssm_act_bwd", after=(sent_late_w,))
    du_state, dlam_row, dw_b, dw_ct = _s5_bwd(scan_bwd_tab, dy, states, proj, U_SSM_COLS, w_b, w_ct)
    dlam = jnp.concatenate([dlam_row[0, :N_CPLX].reshape(HALF, LANES), dlam_row[0, N_CPLX:].reshape(HALF, LANES)],
                           axis=0)
    sent_ssm = on_ssm((dlam, dw_b, dw_ct))
    dq, dk, dv = _flash_bwd(qkv, att, datt, lse, tables[3], after=() if sent_ssm is None else (sent_ssm,))

    def assemble(dqv, dkv, dvv, dga, dua, dub, dgs, c, l, h):
        rot = _rotate(jnp.concatenate([dqv, dkv], axis=1), c, l, h, True)
        return (jnp.concatenate([(rot[:, :ATT_W] * HEAD_DIM ** -0.5).astype(BF16), rot[:, ATT_W:].astype(BF16),
                                 dvv.astype(BF16), dga, (dua + dub).astype(BF16), dgs], axis=1),), ()

    (dproj,) = _rowwise(assemble, [(dq, ATT_W, 0), (dk, ATT_W, 0), (dv, ATT_W, 0), (dg_att, ATT_W, 0),
                                   (du_state, SSM_W, 0), (du_direct, SSM_W, 0), (dg_ssm, SSM_W, 0),
                                   (tables[0], LANES, 0), (tables[1], LANES, 0), (tables[2], LANES, 0)],
                        [(EVEN_IN, BF16)], name="dproj_assemble")
    dw_in = _mm(h, dproj, "tn", BF16, out_blocks=True)
    sent = on_w(dict(w_in=dw_in))
    dh = _mm(dproj, w_in, "nt", F32, b_blocks=True, after=(sent,))
    g_prev, dpre = _pre_bwd(g, dh, x, pre)
    return g_prev, dict(pre=dpre, post=dpost, glu_b=dglu_b, ssm_d=dd)


def _odd_fwd(x, h, tail, pre, post, w_in, pool_w, pool_scale, w_out):
    proj = _mm(h, w_in, "nn", F32, b_blocks=True)
    mixed = _pool(proj, 0, False, BF16)
    ylin = _gmm(mixed, pool_w, "nn", F32)

    def gate(yl, gt, sc):
        return (yl * sc * _silu_and_grad(gt)[0],), ()

    (z,) = _rowwise(gate, [(ylin, POOL_W, 0), (proj, POOL_W, 1), (pool_scale, POOL_W, 0)], [(POOL_W, BF16)],
                    name="odd_gate_fwd")
    yout = _mm(z, w_out, "nn", F32)
    return tail(x, yout, post) + ((x, h, proj, mixed, ylin, z, yout),)


def _odd_bwd(g, saved, pre, post, w_in, pool_w, pool_scale, w_out, on_w):
    x, h, proj, mixed, ylin, z, yout = saved
    dyout, dpost = _post_bwd(g, yout, post)
    dz = _mm(dyout, w_out, "nt", F32)
    dw_out = _mm(z, dyout, "tn", BF16)

    def gate_bwd(dzv, yl, gt, sc):
        sg, dsg = _silu_and_grad(gt)
        tt = dzv * sg
        return (tt * sc, dzv * yl * sc * dsg), (tt * yl,)

    dylin, dproj_gate, dscale = _rowwise(gate_bwd, [(dz, POOL_W, 0), (ylin, POOL_W, 0), (proj, POOL_W, 1),
                                                    (pool_scale, POOL_W, 0)],
                                         [(POOL_W, BF16), (POOL_W, BF16, ODD_IN, 1)], [POOL_W], name="odd_gate_bwd")
    dmixed = _gmm(dylin, pool_w, "nt", F32)
    dpool_w = _gmm(mixed, dylin, "tn", BF16)
    dproj = _pool(dmixed, 0, True, BF16, into=dproj_gate)
    dw_in = _mm(h, dproj, "tn", BF16, out_blocks=True)
    sent = on_w(dict(w_in=dw_in, w_out=dw_out, pool_w=dpool_w))
    dh = _mm(dproj, w_in, "nt", F32, b_blocks=True, after=(sent,))
    g_prev, dpre = _pre_bwd(g, dh, x, pre)
    return g_prev, dict(pre=dpre, post=dpost, pool_scale=dscale)


def _my_index():
    return 4 * lax.axis_index("x") + 2 * lax.axis_index("y") + lax.axis_index("c")


HBM_SPEC = pl.BlockSpec(memory_space=pltpu.HBM)
SEM_SPEC = pl.BlockSpec(memory_space=pltpu.SEMAPHORE)
SPLIT_EFFECT = pltpu.SideEffectType.DATAFLOW_SIDE_EFFECTING


def _device_of(j):
    return (j // 4, (j // 2) % 2, j % 2)


def _split_copy(srcs, lands, send_sems, recv_sems, gather, i, j, dst_slot, recv_slot):
    return pltpu.make_async_remote_copy(
        src_ref=srcs[i] if gather else srcs[i].at[j], dst_ref=lands[i].at[dst_slot],
        send_sem=send_sems.at[i * N_DEV + j], recv_sem=recv_sems.at[i * N_DEV + recv_slot],
        device_id=_device_of(j), device_id_type=MESH_ID)


def _own_copy(srcs, lands, send_sems, gather, i, me):
    return pltpu.make_async_copy(srcs[i] if gather else srcs[i].at[me], lands[i].at[me], send_sems.at[i * N_DEV + me])


def _xchg_start(name, srcs, gather, after=()):
    n = len(srcs)
    n_in = n + len(after)

    def body(*refs):
        src_refs = refs[:n]
        send_sems, recv_sems, token = refs[n_in], refs[n_in + 1], refs[-1]
        land_refs = refs[n_in + 2 + n:n_in + 2 + 2 * n]
        me = _my_index()
        for j in range(N_DEV):
            @pl.when(me != j)
            def _(j=j):
                for i in range(n):
                    _split_copy(src_refs, land_refs, send_sems, recv_sems, gather, i, j, me, me).start()
        for i in range(n):
            _own_copy(src_refs, land_refs, send_sems, gather, i, me).start()
        token[...] = jnp.zeros_like(token)

    land_shapes = [((N_DEV,) + a.shape) if gather else a.shape for a in srcs]
    thru = ([pltpu.HBM(a.shape, a.dtype) for a in srcs] + [pltpu.HBM(s, a.dtype) for s, a in zip(land_shapes, srcs)])
    res = pl.pallas_call(
        body, name=name,
        out_shape=(pltpu.SemaphoreType.DMA((n * N_DEV,)), pltpu.SemaphoreType.DMA((n * N_DEV,)), *thru,
                   SDS((8, LANES), F32)),
        in_specs=[HBM_SPEC] * n + [pl.BlockSpec(memory_space=pl.ANY)] * len(after),
        out_specs=(SEM_SPEC, SEM_SPEC, *([HBM_SPEC] * (2 * n)), pl.BlockSpec(memory_space=pltpu.VMEM)),
        input_output_aliases={i: 2 + i for i in range(n)},
        compiler_params=pltpu.CompilerParams(has_side_effects=SPLIT_EFFECT),
    )(*[pltpu.with_memory_space_constraint(a, pltpu.HBM) for a in srcs], *after)
    return res[0], res[1], list(res[2:2 + n]), list(res[2 + n:2 + 2 * n]), res[-1]


def _xchg_wait(name, started, gather, after):
    send_sems, recv_sems, srcs, lands, _ = started
    n = len(srcs)

    def body(*refs):
        src_refs, land_refs = refs[:n], refs[n:2 * n]
        send_r, recv_r = refs[2 * n], refs[2 * n + 1]
        me = _my_index()
        for j in range(N_DEV):
            @pl.when(me != j)
            def _(j=j):
                for i in range(n):
                    _split_copy(src_refs, land_refs, send_r, recv_r, gather, i, j, me, me).wait_send()
                    _split_copy(src_refs, land_refs, send_r, recv_r, gather, i, j, j, j).wait_recv()
        for i in range(n):
            _own_copy(src_refs, land_refs, send_r, gather, i, me).wait()

    thru = [pltpu.HBM(a.shape, a.dtype) for a in list(srcs) + list(lands)]
    res = pl.pallas_call(
        body, name=name, out_shape=tuple(thru),
        in_specs=[HBM_SPEC] * (2 * n) + [SEM_SPEC, SEM_SPEC] + [pl.BlockSpec(memory_space=pl.ANY)] * len(after),
        out_specs=tuple([HBM_SPEC] * (2 * n)),
        input_output_aliases={i: i for i in range(2 * n)},
        compiler_params=pltpu.CompilerParams(has_side_effects=SPLIT_EFFECT),
    )(*srcs, *lands, send_sems, recv_sems, *after)
    return list(res[n:])


def _adam_layers(w, slot_list, m, v, name):
    n_l, r, c = w.shape
    ns = slot_list[0].shape[0]
    tr = r
    while tr * c * 4 > (1 << 20) and tr % 16 == 0:
        tr //= 2
    assert r % tr == 0 and len(slot_list) == n_l

    def body(*refs):
        w_ref, slot_refs = refs[0], refs[1:1 + n_l]
        m_ref, v_ref, go_ref, d_ref, mo_ref, vo_ref = refs[1 + n_l:]
        layer = pl.program_id(0)
        g = None
        for l, g_ref in enumerate(slot_refs):
            gl = g_ref[0].astype(F32)
            for s in range(1, ns):
                gl = gl + g_ref[s].astype(F32)
            g = gl if g is None else jnp.where(layer == l, gl, g)
        mn = ADAM_B1 * m_ref[...] + (1.0 - ADAM_B1) * g
        vn = ADAM_B2 * v_ref[...] + (1.0 - ADAM_B2) * (g * g)
        m_hat = mn / (1.0 - ADAM_B1 ** ADAM_STEP)
        v_hat = vn / (1.0 - ADAM_B2 ** ADAM_STEP)
        go_ref[...] = g
        d_ref[...] = -ADAM_LR * (m_hat / (jnp.sqrt(v_hat) + ADAM_EPS) + ADAM_WD * w_ref[...])
        mo_ref[...] = mn
        vo_ref[...] = vn

    blk = pl.BlockSpec((None, tr, c), lambda l, i: (l, i, 0))
    slot_specs = [pl.BlockSpec((ns, tr, c), lambda l, i, k=k: (0, jnp.where(l == k, i, 0), 0)) for k in range(n_l)]
    return pl.pallas_call(
        body, name=name, grid=(n_l, r // tr),
        in_specs=[blk] + slot_specs + [blk, blk],
        out_specs=[blk] * 4, out_shape=[SDS((n_l, r, c), F32)] * 4,
        compiler_params=_cparams(("arbitrary", "arbitrary")),
    )(*_in_hbm((w, *slot_list, m, v)))


def _adam(w, gslots, m, v, name):
    r, c = w.shape
    ns = gslots.shape[0]
    tr = r
    while tr * c * 4 > (1 << 20) and tr % 16 == 0:
        tr //= 2
    assert r % tr == 0

    def body(w_ref, g_ref, m_ref, v_ref, go_ref, d_ref, mo_ref, vo_ref):
        g = g_ref[0].astype(F32)
        for s in range(1, ns):
            g = g + g_ref[s].astype(F32)
        wv = w_ref[...]
        mn = ADAM_B1 * m_ref[...] + (1.0 - ADAM_B1) * g
        vn = ADAM_B2 * v_ref[...] + (1.0 - ADAM_B2) * (g * g)
        m_hat = mn / (1.0 - ADAM_B1 ** ADAM_STEP)
        v_hat = vn / (1.0 - ADAM_B2 ** ADAM_STEP)
        go_ref[...] = g
        d_ref[...] = -ADAM_LR * (m_hat / (jnp.sqrt(v_hat) + ADAM_EPS) + ADAM_WD * wv)
        mo_ref[...] = mn
        vo_ref[...] = vn

    blk = pl.BlockSpec((tr, c), lambda i: (i, 0))
    return pl.pallas_call(
        body, name=name, grid=(r // tr,),
        in_specs=[blk, pl.BlockSpec((ns, tr, c), lambda i: (0, i, 0)), blk, blk],
        out_specs=[blk] * 4, out_shape=[SDS((r, c), F32)] * 4,
        compiler_params=_cparams(("parallel",)),
    )(w, gslots, m, v)


def _sum_slots(slots, name):
    ns, r, c = slots.shape

    def body(g_ref, o_ref):
        g = g_ref[0]
        for s in range(1, ns):
            g = g + g_ref[s]
        o_ref[...] = g

    return pl.pallas_call(
        body, name=name, grid=(1,),
        in_specs=[pl.BlockSpec((ns, r, c), lambda i: (0, 0, 0))], out_specs=pl.BlockSpec((r, c), lambda i: (0, 0)),
        out_shape=SDS((r, c), F32), compiler_params=_cparams(("arbitrary",)),
    )(slots)


def _adam_params(params, name):
    n = len(params)

    def body(*refs):
        ins, outs = refs[:5 * n], refs[5 * n:]
        for p in range(n):
            w_ref, m_ref, v_ref, g_first, g_rest = ins[5 * p:5 * p + 5]
            go_ref, d_ref, mo_ref, vo_ref = outs[4 * p:4 * p + 4]
            for part, g_ref in ((slice(0, 1), g_first), (slice(1, w_ref.shape[0]), g_rest)):
                g = g_ref[...]
                mn = ADAM_B1 * m_ref[part] + (1.0 - ADAM_B1) * g
                vn = ADAM_B2 * v_ref[part] + (1.0 - ADAM_B2) * (g * g)
                m_hat = mn / (1.0 - ADAM_B1 ** ADAM_STEP)
                v_hat = vn / (1.0 - ADAM_B2 ** ADAM_STEP)
                go_ref[part] = g
                d_ref[part] = -ADAM_LR * (m_hat / (jnp.sqrt(v_hat) + ADAM_EPS) + ADAM_WD * w_ref[part])
                mo_ref[part] = mn
                vo_ref[part] = vn

    def whole(a):
        return pl.BlockSpec(a.shape, lambda i, nd=a.ndim: (0,) * nd)

    flat = _in_hbm([a for prm in params for a in prm])
    outs = pl.pallas_call(
        body, name=name, grid=(1,),
        in_specs=[whole(a) for a in flat],
        out_specs=[whole(prm[0]) for prm in params for _ in range(4)],
        out_shape=[SDS(prm[0].shape, F32) for prm in params for _ in range(4)],
        compiler_params=_cparams(("arbitrary",)),
    )(*flat)
    return [outs[4 * p:4 * p + 4] for p in range(n)]


SMALL_NAMES = ("pre_norm", "post_norm", "ssm_a_re", "ssm_a_im", "ssm_log_dt", "ssm_b_re", "ssm_b_im", "ssm_c_re",
               "ssm_c_im", "ssm_d", "ssm_glu_b")
SSM_NAMES = ("ssm_a_re", "ssm_a_im", "ssm_log_dt", "ssm_b_re", "ssm_b_im", "ssm_c_re", "ssm_c_im")
WEIGHT_ORDER = ("pre_norm", "post_norm", "even_w_in", "even_w_out", "ssm_a_re", "ssm_a_im", "ssm_log_dt", "ssm_b_re",
                "ssm_b_im", "ssm_c_re", "ssm_c_im", "ssm_d", "ssm_glu_w", "ssm_glu_b", "odd_w_in", "pool_w",
                "pool_scale", "odd_w_out")
PACK_ROWS_ALIGN = 8


def _pack(parts):
    flat = jnp.concatenate([p.reshape(-1).astype(F32) for p in parts])
    rows = -(-flat.shape[0] // (LANES * PACK_ROWS_ALIGN)) * PACK_ROWS_ALIGN
    return jnp.pad(flat, (0, rows * LANES - flat.shape[0])).reshape(rows, LANES)


def _unpack(packed, shapes):
    flat = packed.reshape(-1)
    out, off = [], 0
    for shp in shapes:
        size = math.prod(shp)
        out.append(flat[off:off + size].reshape(shp))
        off += size
    return out


EVEN_SHARDED = ("w_in", "w_out", "glu_w")
ODD_SHARDED = ("w_in", "pool_w", "w_out")
FAMILY = {(0, "w_in"): "even_w_in", (0, "w_out"): "even_w_out", (0, "glu_w"): "ssm_glu_w",
          (1, "w_in"): "odd_w_in", (1, "pool_w"): "pool_w", (1, "w_out"): "odd_w_out"}


def _sharded_keys(layer):
    return EVEN_SHARDED if layer % 2 == 0 else ODD_SHARDED


def _local_step(x, tgt, small, get_weights, on_w, on_ssm, on_grads, zero=0.0):
    tables = _rope_tables(zero) + (_attention_bias(zero),)
    preps, prep_vjps = [], []
    for i in range(2):
        out, vjp = jax.vjp(_ssm_prep, small["ssm_a_re"][i] + zero, small["ssm_a_im"][i], small["ssm_log_dt"][i],
                           small["ssm_b_re"][i], small["ssm_b_im"][i], small["ssm_c_re"][i], small["ssm_c_im"][i])
        preps.append(out)
        prep_vjps.append(vjp)

    def layer_args(layer, wts):
        i = layer // 2
        pre, post = _row(small["pre_norm"][layer]) + wts.get("token", 0.0), _row(small["post_norm"][layer])
        if layer % 2 == 0:
            return (pre, post, wts["w_in"], wts["late"], _row(small["ssm_glu_b"][i]), _row(small["ssm_d"][i]),
                    preps[i], tables)
        return (pre, post, wts["w_in"], wts["pool_w"], _row(wts["pool_scale"]), wts["w_out"])

    saved, args = [], []
    cur = x
    for layer in range(4):
        after = (cur,) if layer else (cur, tables[0], tables[3], preps[0][1], preps[0][2], preps[1][1], preps[1][2])
        args.append(layer_args(layer, get_weights(layer, after)))
        if layer == 0:
            h = _norm_fwd(cur, args[0][0])
        if layer < 3:
            def tail(xv, yv, post, next_gain=_row(small["pre_norm"][layer + 1])):
                return tuple(_post_fwd(xv, yv, post, next_gain))
        else:
            def tail(xv, yv, post):
                return tuple(_post_fwd_loss(xv, yv, post, tgt))
        cur, h, sv = (_even_fwd if layer % 2 == 0 else _odd_fwd)(cur, h, tail, *args[layer])
        saved.append(sv)
    g, sq = cur, h
    loss = 0.5 * jnp.sum(sq) / D

    lg = [None] * 4
    token = jnp.zeros((), F32)
    for layer in reversed(range(4)):
        largs = list(args[layer])
        largs[1] = largs[1] + token
        hooks = dict(on_w=functools.partial(on_w, layer))
        ssm_grads = []
        if layer % 2 == 0:
            def ssm_hook(cotangents, layer=layer):
                ssm_grads.append(prep_vjps[layer // 2](cotangents))
                return on_ssm(layer, ssm_grads[0])

            hooks["on_ssm"] = ssm_hook
        g, lg[layer] = (_even_bwd if layer % 2 == 0 else _odd_bwd)(g, saved[layer], *largs, **hooks)
        if ssm_grads:
            lg[layer]["ssm"] = ssm_grads[0]
        token = on_grads(layer, lg[layer])
    return loss, g, token


def _to_slots(key, gfull):
    if key == "w_in":
        return gfull
    if key in ("w_out", "glu_w"):
        rr, nn = gfull.shape
        return gfull.reshape(N_DEV, rr // N_DEV, nn)
    assert key == "pool_w"
    gg, rr, nn = gfull.shape
    return gfull.reshape(gg, N_DEV, rr // N_DEV, nn).transpose(1, 0, 2, 3)


def _from_gathered(key, gat):
    if key == "w_in":
        return gat
    if key in ("w_out", "glu_w"):
        _, rr, nn = gat.shape
        return gat.reshape(N_DEV * rr, nn)
    assert key == "pool_w"
    _, gg, rr, nn = gat.shape
    return gat.transpose(1, 0, 2, 3).reshape(gg, N_DEV * rr, nn)


def kernel(x, pre_norm, post_norm, even_w_in, even_w_out, ssm_a_re, ssm_a_im, ssm_log_dt, ssm_b_re, ssm_b_im, ssm_c_re, ssm_c_im, ssm_d, ssm_glu_w, ssm_glu_b, odd_w_in, pool_w, pool_scale, odd_w_out, loss_target, m_pre_norm, m_post_norm, m_even_w_in, m_even_w_out, m_ssm_a_re, m_ssm_a_im, m_ssm_log_dt, m_ssm_b_re, m_ssm_b_im, m_ssm_c_re, m_ssm_c_im, m_ssm_d, m_ssm_glu_w, m_ssm_glu_b, m_odd_w_in, m_pool_w, m_pool_scale, m_odd_w_out, v_pre_norm, v_post_norm, v_even_w_in, v_even_w_out, v_ssm_a_re, v_ssm_a_im, v_ssm_log_dt, v_ssm_b_re, v_ssm_b_im, v_ssm_c_re, v_ssm_c_im, v_ssm_d, v_ssm_glu_w, v_ssm_glu_b, v_odd_w_in, v_pool_w, v_pool_scale, v_odd_w_out):
    w = dict(pre_norm=pre_norm, post_norm=post_norm, even_w_in=even_w_in, even_w_out=even_w_out, ssm_a_re=ssm_a_re,
             ssm_a_im=ssm_a_im, ssm_log_dt=ssm_log_dt, ssm_b_re=ssm_b_re, ssm_b_im=ssm_b_im, ssm_c_re=ssm_c_re,
             ssm_c_im=ssm_c_im, ssm_d=ssm_d, ssm_glu_w=ssm_glu_w, ssm_glu_b=ssm_glu_b, odd_w_in=odd_w_in,
             pool_w=pool_w, pool_scale=pool_scale, odd_w_out=odd_w_out)
    mom = dict(pre_norm=m_pre_norm, post_norm=m_post_norm, even_w_in=m_even_w_in, even_w_out=m_even_w_out,
               ssm_a_re=m_ssm_a_re, ssm_a_im=m_ssm_a_im, ssm_log_dt=m_ssm_log_dt, ssm_b_re=m_ssm_b_re,
               ssm_b_im=m_ssm_b_im, ssm_c_re=m_ssm_c_re, ssm_c_im=m_ssm_c_im, ssm_d=m_ssm_d, ssm_glu_w=m_ssm_glu_w,
               ssm_glu_b=m_ssm_glu_b, odd_w_in=m_odd_w_in, pool_w=m_pool_w, pool_scale=m_pool_scale,
               odd_w_out=m_odd_w_out)
    var = dict(pre_norm=v_pre_norm, post_norm=v_post_norm, even_w_in=v_even_w_in, even_w_out=v_even_w_out,
               ssm_a_re=v_ssm_a_re, ssm_a_im=v_ssm_a_im, ssm_log_dt=v_ssm_log_dt, ssm_b_re=v_ssm_b_re,
               ssm_b_im=v_ssm_b_im, ssm_c_re=v_ssm_c_re, ssm_c_im=v_ssm_c_im, ssm_d=v_ssm_d, ssm_glu_w=v_ssm_glu_w,
               ssm_glu_b=v_ssm_glu_b, odd_w_in=v_odd_w_in, pool_w=v_pool_w, pool_scale=v_pool_scale,
               odd_w_out=v_odd_w_out)
    me = _my_index()
    scale_cols = pool_scale.shape[1]

    def start_gather(tag, layer, keys, after=()):
        i = layer // 2
        shards = [w[FAMILY[(layer % 2, k)]][i].astype(BF16) for k in keys]
        if layer % 2 == 1:
            shards.append(jnp.pad(pool_scale[i][None], ((0, PACK_ROWS_ALIGN - 1), (0, 0))))
        return _xchg_start(f"gather_start_{tag}", shards, True, after)

    gather_started = {0: start_gather("0", 0, EVEN_SHARDED[:1])}
    small = {nm: w[nm] for nm in SMALL_NAMES}

    def get_weights(layer, after):
        keys = EVEN_SHARDED[:1] if layer == 0 else _sharded_keys(layer)
        lands = _xchg_wait(f"gather_wait_{layer}", gather_started[layer], True, after)
        wts = {k: _from_gathered(k, gat) for k, gat in zip(keys, lands)}
        if layer % 2 == 1:
            wts["pool_scale"] = lands[-1][:, 0, :].reshape(N_DEV * scale_cols)
        if layer == 0:
            prev = gather_started["0_late"] = start_gather("0_late", 0, EVEN_SHARDED[1:], after=(lands[0],))
            for later in (1, 2, 3):
                prev = gather_started[later] = start_gather(str(later), later, _sharded_keys(later), after=(prev[4],))
            wts["token"] = sum(gather_started[tag][4][0, 0] for tag in ("0_late", 1, 2, 3))

            def late(after_late):
                late_lands = _xchg_wait("gather_wait_0_late", gather_started["0_late"], True, (after_late,))
                return tuple(_from_gathered(k, gat) for k, gat in zip(EVEN_SHARDED[1:], late_lands))

            wts["late"] = late
        elif layer == 2:
            wts["late"] = lambda after_late: (wts["w_out"], wts["glu_w"])
        return wts

    scatter_started = []

    def on_w(layer, gw):
        keys = tuple(k for k in _sharded_keys(layer) if k in gw)
        started = _xchg_start(f"scatter_start_{layer}_{keys[0]}", [_to_slots(k, gw[k]) for k in keys], False)
        scatter_started.append((layer, keys, started))
        return started[4]

    def wait_scatters(layers, after):
        for layer, keys, started in scatter_started:
            if layer in layers:
                lands = _xchg_wait(f"scatter_wait_{layer}_{keys[0]}", started, False, after)
                for k, land in zip(keys, lands):
                    recv[(layer, k)] = land

    packed_names = ("pre_norm", "post_norm") + SSM_NAMES + ("ssm_d", "ssm_glu_b")
    tails = {nm: (SSM_GROUPS, SSM_STATE * SSM_GROUP) if nm in ("ssm_b_re", "ssm_b_im") else w[nm].shape[1:]
             for nm in packed_names}

    layer_grads = {}
    early_started, mid_started = [], []

    def on_ssm(layer, ssm_grads):
        if layer != 0:
            return None
        mid_started.append(_xchg_start("mid_start", [_pack(list(ssm_grads))], True))
        return mid_started[0][4]

    def on_grads(layer, lg):
        layer_grads[layer] = lg
        zero = jnp.zeros((), F32)
        if layer == 1:
            lgs = layer_grads
            early = ([jnp.concatenate([lgs[l][k] for l in (1, 2, 3)], axis=0) for k in ("pre", "post")]
                     + list(lgs[2]["ssm"]) + [lgs[2]["ssm_d"], lgs[2]["glu_b"],
                                              jnp.concatenate([lgs[1]["pool_scale"], lgs[3]["pool_scale"]], axis=0)])
            early_started.append(_xchg_start("small_start", [_pack(early)], True))
            zero = zero + early_started[0][4][0, 0]
        return zero

    loss_local, grad_x, token = _local_step(x[0], loss_target[0], small, get_weights, on_w, on_ssm, on_grads,
                                            zero=gather_started[0][4][0, 0])

    lg0 = layer_grads[0]
    late_started = _xchg_start("late_start", [_pack([lg0["pre"], lg0["post"], lg0["ssm_d"], lg0["glu_b"],
                                                     loss_local.reshape(1)]) + token], True)

    def adam_family(parity, k):
        nm = FAMILY[(parity, k)]
        shp = w[nm].shape
        cols = shp[-1]
        slot_list = [recv[(parity + 2 * i, k)].reshape(N_DEV, -1, cols) for i in range(2)]
        outs = _adam_layers(w[nm].reshape(2, -1, cols), slot_list, mom[nm].reshape(2, -1, cols),
                            var[nm].reshape(2, -1, cols), name=f"adam_{nm}")
        return [o.reshape(shp) for o in outs]

    recv, res = {}, {}
    wait_scatters((3, 1), (late_started[4],))
    for k in ODD_SHARDED:
        res[FAMILY[(1, k)]] = adam_family(1, k)
    odd_done = tuple(res[FAMILY[(1, k)]][0] for k in ODD_SHARDED)

    (early_slots,) = _xchg_wait("small_wait", early_started[0], True, odd_done)
    (mid_slots,) = _xchg_wait("mid_wait", mid_started[0], True, odd_done)
    early_shapes = [(w[nm].shape[0] - 1,) + tails[nm] for nm in packed_names] + [(2, N_DEV * scale_cols)]
    g_early = _unpack(_sum_slots(early_slots, "sum_small_early"), early_shapes)
    g_mid = _unpack(_sum_slots(mid_slots, "sum_small_mid"), [(1,) + tails[nm] for nm in SSM_NAMES])

    (late_slots,) = _xchg_wait("late_wait", late_started, True, (g_early[0], g_mid[0]))
    wait_scatters((2, 0), (late_slots,))
    for k in EVEN_SHARDED:
        res[FAMILY[(0, k)]] = adam_family(0, k)

    late_names = ("pre_norm", "post_norm", "ssm_d", "ssm_glu_b")
    g_late = _unpack(_sum_slots(late_slots, "sum_small_late"), [(1,) + tails[nm] for nm in late_names] + [(1,)])
    g_first = dict(zip(late_names, g_late))
    g_first.update(zip(SSM_NAMES, g_mid))
    dense = lambda nm, a: a.reshape((a.shape[0],) + tails[nm])
    outs = _adam_params([(dense(nm, w[nm]), dense(nm, mom[nm]), dense(nm, var[nm]), g_first[nm], g_early[j])
                         for j, nm in enumerate(packed_names)], "adam_small")
    for nm, four in zip(packed_names, outs):
        res[nm] = [o.reshape(w[nm].shape) for o in four]
    loss = g_late[-1].reshape(())
    g_scale = lax.dynamic_slice_in_dim(g_early[-1], me * scale_cols, scale_cols, axis=1)
    pad = ((0, PACK_ROWS_ALIGN - 2), (0, 0))
    outs = _adam(jnp.pad(pool_scale, pad), jnp.pad(g_scale, pad)[None], jnp.pad(m_pool_scale, pad),
                 jnp.pad(v_pool_scale, pad), name="adam_pool_scale")
    res["pool_scale"] = [o[:2] for o in outs]

    out = [loss, grad_x[None]]
    for kind in range(4):
        out += [res[nm][kind] for nm in WEIGHT_ORDER]
    return tuple(out)
```

```python
import functools
import math

import jax
import jax.numpy as jnp
from jax import lax
from jax.experimental import pallas as pl
from jax.experimental.pallas import tpu as pltpu

F32 = jnp.float32
BF16 = jnp.bfloat16
SDS = jax.ShapeDtypeStruct

N_DEV = 8
S = 2048
D = 1024
HEAD_DIM = 64
ROT_DIM = 16
ROPE_THETA = 500000.0
ATT_W = 1024
SSM_W = 512
SSM_GROUPS = 32
SSM_GROUP = 16
SSM_STATE = 64
N_CPLX = SSM_GROUPS * SSM_STATE
POOL_W = 2048
POOL_GROUP = 512
EVEN_IN = 5120
EVEN_OUT = 1536
ODD_IN = 4096
RMS_EPS = 1e-6
LANES = 128
VMEM_LIMIT = 48 * 1024 * 1024

ADAM_LR = 0.001
ADAM_B1 = 0.9
ADAM_B2 = 0.999
ADAM_EPS = 1e-08
ADAM_WD = 0.01
ADAM_STEP = 10

MESH_ID = pl.DeviceIdType.MESH
NN = (((1,), (0,)), ((), ()))
NT = (((1,), (1,)), ((), ()))
TN = (((0,), (0,)), ((), ()))
_DN = {"nn": NN, "nt": NT, "tn": TN}


def _cparams(sem):
    return pltpu.CompilerParams(dimension_semantics=sem, vmem_limit_bytes=VMEM_LIMIT)


def _in_hbm(arrs):
    return [pltpu.with_memory_space_constraint(a, pltpu.HBM) for a in arrs]


MM_TILES = (1024, 768, 512)


def _tile(dim):
    return next((t for t in MM_TILES if dim % t == 0), dim)


BLOCK_PAIR = 2
MAX_WHOLE_K = 2048


def _mm(a, b, mode, out_dtype, b_blocks=False, out_blocks=False, after=()):
    if b_blocks:
        nblk, rows, cb = b.shape
        b2_shape = (rows, nblk * cb)
    else:
        b2_shape = b.shape
    if mode == "nn":
        (m, k), n = a.shape, b2_shape[1]
    elif mode == "nt":
        (m, k), n = a.shape, b2_shape[0]
    else:
        (k, m), n = a.shape, b2_shape[1]
    tm, tn, tk = _tile(m), _tile(n), _tile(k)
    if k <= MAX_WHOLE_K:
        tk = k
    if b_blocks and mode == "nn":
        tn = BLOCK_PAIR * cb
        if tn <= MM_TILES[0]:
            tm = m
    if b_blocks and mode == "nt":
        tk = BLOCK_PAIR * cb
    if out_blocks:
        cb = n // N_DEV
        tn = BLOCK_PAIR * cb
        tk = k
    nk = k // tk

    def body(a_ref, b_ref, *rest):
        o_ref, acc_ref = rest[-2:]
        kk = pl.program_id(2)
        bv = jnp.concatenate([b_ref[p] for p in range(BLOCK_PAIR)], axis=1) if b_blocks else b_ref[...]
        part = lax.dot_general(a_ref[...].astype(BF16), bv.astype(BF16), _DN[mode], preferred_element_type=F32)

        def write(res):
            if out_blocks:
                for p in range(BLOCK_PAIR):
                    o_ref[p] = res[:, p * cb:(p + 1) * cb].astype(o_ref.dtype)
            else:
                o_ref[...] = res.astype(o_ref.dtype)

        if nk == 1:
            write(part)
            return

        @pl.when(kk == 0)
        def _():
            acc_ref[...] = part

        @pl.when((kk > 0) & (kk < nk - 1))
        def _():
            acc_ref[...] += part

        @pl.when(kk == nk - 1)
        def _():
            write(acc_ref[...] + part)

    if mode == "nn":
        a_spec = pl.BlockSpec((tm, tk), lambda i, j, kk: (i, kk))
        b_spec = pl.BlockSpec((tk, tn), lambda i, j, kk: (kk, j))
    elif mode == "nt":
        a_spec = pl.BlockSpec((tm, tk), lambda i, j, kk: (i, kk))
        b_spec = pl.BlockSpec((tn, tk), lambda i, j, kk: (j, kk))
    else:
        a_spec = pl.BlockSpec((tk, tm), lambda i, j, kk: (kk, i))
        b_spec = pl.BlockSpec((tk, tn), lambda i, j, kk: (kk, j))
    if b_blocks and mode == "nn":
        b_spec = pl.BlockSpec((BLOCK_PAIR, tk, cb), lambda i, j, kk: (j, kk, 0))
    if b_blocks and mode == "nt":
        b_spec = pl.BlockSpec((BLOCK_PAIR, tn, cb), lambda i, j, kk: (kk, j, 0))
    out_spec = pl.BlockSpec((tm, tn), lambda i, j, kk: (i, j))
    out_shape = SDS((m, n), out_dtype)
    if out_blocks:
        out_spec = pl.BlockSpec((BLOCK_PAIR, tm, cb), lambda i, j, kk: (j, i, 0))
        out_shape = SDS((N_DEV, m, cb), out_dtype)
    return pl.pallas_call(
        body, name=f"mm_{mode}_{m}x{k}x{n}",
        grid=(m // tm, n // tn, nk),
        in_specs=[a_spec, b_spec] + [pl.BlockSpec(memory_space=pl.ANY)] * len(after),
        out_specs=out_spec,
        out_shape=out_shape,
        scratch_shapes=[pltpu.VMEM((tm, tn) if nk > 1 else (8, LANES), F32)],
        compiler_params=_cparams(("parallel", "parallel", "arbitrary")),
    )(a, b, *after)


def _gmm(a, b, mode, out_dtype, tm=S):
    ng, gw = POOL_W // POOL_GROUP, POOL_GROUP
    ns = S // tm
    if mode in ("nn", "nt"):
        def body(a_ref, b_ref, o_ref):
            o_ref[...] = lax.dot_general(a_ref[...].astype(BF16), b_ref[...].astype(BF16), _DN[mode],
                                         preferred_element_type=F32).astype(o_ref.dtype)

        return pl.pallas_call(
            body, name=f"gmm_{mode}", grid=(ng, ns),
            in_specs=[pl.BlockSpec((tm, gw), lambda g, i: (i, g)),
                      pl.BlockSpec((None, gw, gw), lambda g, i: (g, 0, 0))],
            out_specs=pl.BlockSpec((tm, gw), lambda g, i: (i, g)),
            out_shape=SDS((S, POOL_W), out_dtype),
            compiler_params=_cparams(("parallel", "parallel")),
        )(a, b)

    def body_tn(a_ref, b_ref, o_ref, acc_ref):
        i = pl.program_id(1)

        @pl.when(i == 0)
        def _():
            acc_ref[...] = jnp.zeros_like(acc_ref)

        acc_ref[...] += lax.dot_general(a_ref[...].astype(BF16), b_ref[...].astype(BF16), TN,
                                        preferred_element_type=F32)

        @pl.when(i == ns - 1)
        def _():
            o_ref[...] = acc_ref[...].astype(o_ref.dtype)

    return pl.pallas_call(
        body_tn, name="gmm_tn", grid=(ng, ns),
        in_specs=[pl.BlockSpec((tm, gw), lambda g, i: (i, g)),
                  pl.BlockSpec((tm, gw), lambda g, i: (i, g))],
        out_specs=pl.BlockSpec((None, gw, gw), lambda g, i: (g, 0, 0)),
        out_shape=SDS((ng, gw, gw), out_dtype),
        scratch_shapes=[pltpu.VMEM((gw, gw), F32)],
        compiler_params=_cparams(("parallel", "arbitrary")),
    )(a, b)


def _rowwise(fn, inputs, out_defs, acc_defs=(), tm=512, name=None, after=()):
    n_in, n_out, n_acc = len(inputs), len(out_defs), len(acc_defs)
    n_after = len(after)
    in_specs, args = [], []
    for arr, width, cb in inputs:
        if arr.shape[0] != S:
            in_specs.append(pl.BlockSpec((arr.shape[0], width), lambda i, cb=cb: (0, cb)))
        else:
            in_specs.append(pl.BlockSpec((tm, width), lambda i, cb=cb: (i, cb)))
        args.append(arr)
    out_defs = [d if len(d) == 4 else (d[0], d[1], d[0], 0) for d in out_defs]
    out_shape = [SDS((S, ww), dt) for _, dt, ww, _ in out_defs] + [SDS((1, w), F32) for w in acc_defs]
    out_specs = ([pl.BlockSpec((tm, w), lambda i, cb=cb: (i, cb)) for w, _, _, cb in out_defs]
                 + [pl.BlockSpec((1, w), lambda i: (0, 0)) for w in acc_defs])

    def kern(*refs):
        vals = [r[...] for r in refs[:n_in]]
        outs, accs = fn(*vals)
        out_refs = refs[n_in + n_after:]
        for r, v in zip(out_refs[:n_out], outs):
            r[...] = v.astype(r.dtype)
        if n_acc:
            acc_refs = out_refs[n_out:]

            @pl.when(pl.program_id(0) == 0)
            def _():
                for r in acc_refs:
                    r[...] = jnp.zeros_like(r)

            for r, v in zip(acc_refs, accs):
                r[...] += jnp.sum(v, axis=0, keepdims=True)

    res = pl.pallas_call(
        kern, name=name, grid=(S // tm,), in_specs=in_specs + [pl.BlockSpec(memory_space=pl.ANY)] * n_after,
        out_specs=out_specs, out_shape=out_shape, compiler_params=_cparams(("arbitrary",)),
    )(*args, *after)
    return res


def _sigmoid(x):
    return 1.0 / (1.0 + jnp.exp(-x))


def _silu_and_grad(x):
    s = _sigmoid(x)
    return x * s, s * (1.0 + x * (1.0 - s))


_GELU_K = math.sqrt(2.0 / math.pi)
_GELU_C = 0.044715


def _gelu_and_grad(x):
    t = jnp.tanh(_GELU_K * (x + _GELU_C * (x * x * x)))
    cdf = 0.5 * (1.0 + t)
    grad = cdf + 0.5 * x * (1.0 - t * t) * (_GELU_K * (1.0 + 3.0 * _GELU_C * x * x))
    return x * cdf, grad


def _rms(xv, gain):
    r = lax.rsqrt(jnp.mean(xv * xv, axis=-1, keepdims=True) + RMS_EPS)
    return xv * r * gain


def _rms_bwd(dout, xv, gain):
    r = lax.rsqrt(jnp.mean(xv * xv, axis=-1, keepdims=True) + RMS_EPS)
    xhat = xv * r
    dxhat = dout * gain
    dx = r * (dxhat - xhat * jnp.mean(dxhat * xhat, axis=-1, keepdims=True))
    return dx, dout * xhat


def _norm_fwd(x, gain):
    (h,) = _rowwise(lambda xv, g: ((_rms(xv, g),), ()), [(x, D, 0), (gain, D, 0)], [(D, BF16)], name="norm_fwd")
    return h


def _post_fwd(x, y, gain, next_gain):
    def fn(xv, yv, g, gn):
        out = xv + _rms(yv, g)
        return (out, _rms(out, gn)), ()

    return _rowwise(fn, [(x, D, 0), (y, D, 0), (gain, D, 0), (next_gain, D, 0)], [(D, F32), (D, BF16)],
                    name="post_fwd")


def _post_fwd_loss(x, y, gain, tgt):
    def fn(xv, yv, g, tv):
        e = xv + _rms(yv, g) - tv
        return (e * (1.0 / D),), (e * e,)

    return _rowwise(fn, [(x, D, 0), (y, D, 0), (gain, D, 0), (tgt, D, 0)], [(D, F32)], [D], name="post_fwd_loss")


def _post_bwd(g, y, gain):
    def fn(gv, yv, gn):
        dx, dg = _rms_bwd(gv, yv, gn)
        return (dx,), (dg,)

    return _rowwise(fn, [(g, D, 0), (y, D, 0), (gain, D, 0)], [(D, BF16)], [D], name="post_bwd")


def _pre_bwd(g, dh, x, gain):
    def fn(gv, dhv, xv, gn):
        dx, dg = _rms_bwd(dhv, xv, gn)
        return (gv + dx,), (dg,)

    return _rowwise(fn, [(g, D, 0), (dh, D, 0), (x, D, 0), (gain, D, 0)], [(D, F32)], [D], name="pre_bwd")


def _pool(u_arr, col_block, transpose, out_dtype, into=None, tc=256):
    n_t = POOL_W // tc
    per_group = POOL_GROUP // tc

    def body(u_ref, *rest):
        o_ref = rest[-1]
        grp = pl.program_id(0) // per_group
        t = lax.broadcasted_iota(jnp.int32, (S, 1), 0)
        for g in range(POOL_W // POOL_GROUP):
            @pl.when(grp == g)
            def _(g=g):
                xv = u_ref[...]
                cnt = jnp.minimum(t + 1, 2 << g).astype(F32)
                cur = xv / cnt if transpose else xv
                for k in (1, 2, 4, 8)[:g + 1]:
                    if transpose:
                        cur = cur + jnp.where(t < S - k, pltpu.roll(cur, S - k, 0), 0.0)
                    else:
                        cur = cur + jnp.where(t >= k, pltpu.roll(cur, k, 0), 0.0)
                res = cur - xv if transpose else cur / cnt - xv
                o_ref[...] = res.astype(o_ref.dtype)

    in_specs = [pl.BlockSpec((S, tc), lambda c: (0, col_block * n_t + c))]
    args = [u_arr]
    if into is not None:
        in_specs.append(pl.BlockSpec(memory_space=pl.ANY))
        args.append(into)
    return pl.pallas_call(
        body, name="pool_bwd" if transpose else "pool_fwd", grid=(n_t,),
        in_specs=in_specs,
        out_specs=pl.BlockSpec((S, tc), lambda c: (0, c)),
        out_shape=SDS((S, POOL_W) if into is None else into.shape, out_dtype),
        input_output_aliases={} if into is None else {1: 0},
        compiler_params=_cparams(("parallel",)),
    )(*args)


def _rope_tables(zero):
    pos = jnp.arange(S, dtype=jnp.int32).astype(F32) + zero
    inv_freq = ROPE_THETA ** (-jnp.arange(0, ROT_DIM, 2, dtype=F32) / ROT_DIM)
    ang = pos[:, None] * inv_freq[None, :]
    cos8, sin8 = jnp.cos(ang), jnp.sin(ang)
    half = ROT_DIM // 2
    zeros = jnp.zeros((S, HEAD_DIM - ROT_DIM), F32)
    cos = jnp.concatenate([cos8, cos8, jnp.ones((S, HEAD_DIM - ROT_DIM), F32)], axis=1)
    sin = jnp.concatenate([-sin8, sin8, zeros], axis=1)
    rep = LANES // HEAD_DIM
    lane = jnp.arange(LANES)
    dim = lane % HEAD_DIM
    partner = jnp.where(dim < half, lane + half, jnp.where(dim < ROT_DIM, lane - half, -1))
    swap = (lane[:, None] == partner[None, :]).astype(BF16)
    return jnp.tile(cos, (1, rep)), jnp.tile(sin, (1, rep)), swap


def _rotate(xv, cos, sin, swap, transpose):
    rep = xv.shape[1] // LANES
    wide = lambda tab: jnp.concatenate([tab] * rep, axis=1)
    xb = xv.astype(BF16)
    partner = jnp.concatenate([lax.dot_general(xb[:, t * LANES:(t + 1) * LANES], swap, NN, preferred_element_type=F32)
                               for t in range(rep)], axis=1)
    mixed = partner * wide(sin)
    return xv * wide(cos) - mixed if transpose else xv * wide(cos) + mixed


def _qkv_prep(proj, tables):
    cos, sin, swap = tables

    def fn(x, c, s, sw):
        rot = _rotate(x[:, :2 * ATT_W], c, s, sw, False)
        return (jnp.concatenate([(rot[:, :ATT_W] * HEAD_DIM ** -0.5).astype(BF16), rot[:, ATT_W:].astype(BF16),
                                 x[:, 2 * ATT_W:].astype(BF16)], axis=1),), ()

    (qkv,) = _rowwise(fn, [(proj, 3 * ATT_W, 0), (cos, LANES, 0), (sin, LANES, 0), (swap, LANES, 0)],
                      [(3 * ATT_W, BF16)], name="qkv_prep")
    return qkv


ATT_T = 512


def _multiplicity(delta):
    ok = delta >= 0
    near = jnp.where(ok & (delta <= 128), 1.0, 0.0)
    mid = jnp.where(ok & (delta <= 512) & ((delta & 3) == 0), 1.0, 0.0)
    far = jnp.where(ok & ((delta & 15) == 0), 1.0, 0.0)
    return near + mid + far


def _attention_bias(zero):
    t = ATT_T
    pos = jnp.arange(t, dtype=jnp.int32) + jnp.asarray(zero).astype(jnp.int32)
    delta = jnp.arange(S // t, dtype=jnp.int32)[:, None, None] * t + pos[None, :, None] - pos[None, None, :]
    mult = _multiplicity(delta)
    return jnp.where(mult > 0.0, jnp.log(jnp.maximum(mult, 1.0)), -1e30).astype(F32)


def _head_split(v, first):
    zero = jnp.zeros_like(v)
    return [jnp.where(first, v, zero), jnp.where(first, zero, v)]


def _flash_fwd(qkv, bias):
    t = ATT_T
    n_hp = ATT_W // LANES

    def body(q_ref, k_ref, v_ref, b_ref, o_ref, lse_ref):
        i = pl.program_id(1)
        first = lax.broadcasted_iota(jnp.int32, (1, LANES), 1) < HEAD_DIM
        qs = _head_split(q_ref[...], first)

        def kv_step(j, carry):
            m0, l0, m1, l1, acc = carry
            off = pl.multiple_of(j * t, t)
            kb = k_ref[pl.ds(off, t), :]
            vs = _head_split(v_ref[pl.ds(off, t), :], first)
            bias_t = b_ref[i - j]
            new = []
            pv = None
            for h, (m_prev, l_prev) in enumerate(((m0, l0), (m1, l1))):
                s = lax.dot_general(qs[h], kb, NT, preferred_element_type=F32) + bias_t
                m_new = jnp.maximum(m_prev, jnp.max(s, axis=1, keepdims=True))
                p = jnp.exp(s - m_new)
                alpha = jnp.exp(m_prev - m_new)
                l_new = alpha * l_prev + jnp.sum(p, axis=1, keepdims=True)
                d = lax.dot_general(p.astype(BF16), vs[h], NN, preferred_element_type=F32)
                pv = d if pv is None else pv + d
                new.append((m_new, l_new, alpha))
            acc = acc * jnp.where(first, new[0][2], new[1][2]) + pv
            return new[0][0], new[0][1], new[1][0], new[1][1], acc

        neg = jnp.full((t, 1), -1e30, F32)
        zero = jnp.zeros((t, 1), F32)
        m0, l0, m1, l1, acc = lax.fori_loop(0, i + 1, kv_step, (neg, zero, neg, zero, jnp.zeros((t, LANES), F32)))
        o_ref[...] = acc * jnp.where(first, 1.0 / l0, 1.0 / l1)
        lse_ref[...] = jnp.where(first, m0 + jnp.log(l0), m1 + jnp.log(l1))

    blk = pl.BlockSpec((t, LANES), lambda hp, i: (i, hp))
    k_full = pl.BlockSpec((S, LANES), lambda hp, i: (0, n_hp + hp))
    v_full = pl.BlockSpec((S, LANES), lambda hp, i: (0, 2 * n_hp + hp))
    return pl.pallas_call(
        body, name="flash_fwd", grid=(n_hp, S // t),
        in_specs=[blk, k_full, v_full, pl.BlockSpec((S // t, t, t), lambda hp, i: (0, 0, 0))], out_specs=[blk, blk],
        out_shape=[SDS((S, ATT_W), F32), SDS((S, ATT_W), F32)],
        compiler_params=_cparams(("parallel", "arbitrary")),
    )(qkv, qkv, qkv, bias)


def _flash_bwd(qkv, o, do, lse, bias, after=()):
    t = ATT_T
    n_hp = ATT_W // LANES
    n_t = S // t

    def body(q_ref, k_ref, v_ref, o_ref, do_ref, lse_ref, b_ref, *rest):
        dq_ref, dk_ref, dv_ref = rest[-3:]
        j = pl.program_id(1)
        first = lax.broadcasted_iota(jnp.int32, (1, LANES), 1) < HEAD_DIM

        @pl.when(j == 0)
        def _():
            dq_ref[...] = jnp.zeros_like(dq_ref)

        kb = k_ref[...]
        vb = v_ref[...]
        ks = _head_split(kb, first)

        def q_step(i, carry):
            dk_acc, dv_acc = carry
            rows = pl.ds(pl.multiple_of(i * t, t), t)
            qs = _head_split(q_ref[rows, :], first)
            dob = do_ref[rows, :]
            prod = dob * o_ref[rows, :]
            d_all = jnp.sum(prod, axis=1, keepdims=True)
            d0 = jnp.sum(jnp.where(first, prod, 0.0), axis=1, keepdims=True)
            lse_b = lse_ref[rows, :]
            lse0 = jnp.max(jnp.where(first, lse_b, -jnp.inf), axis=1, keepdims=True)
            lse1 = jnp.max(jnp.where(first, -jnp.inf, lse_b), axis=1, keepdims=True)
            dos = _head_split(dob.astype(BF16), first)
            bias_t = b_ref[i - j]
            dq_t = jnp.zeros((t, LANES), F32)
            for h, (lse_h, d_h) in enumerate(((lse0, d0), (lse1, d_all - d0))):
                s = lax.dot_general(qs[h], kb, NT, preferred_element_type=F32)
                p = jnp.exp(s + (bias_t - lse_h))
                dp = lax.dot_general(dos[h], vb, NT, preferred_element_type=F32)
                ds = (p * (dp - d_h)).astype(BF16)
                dv_acc = dv_acc + lax.dot_general(p.astype(BF16), dos[h], TN, preferred_element_type=F32)
                dk_acc = dk_acc + lax.dot_general(ds, qs[h], TN, preferred_element_type=F32)
                dq_t = dq_t + lax.dot_general(ds, ks[h], NN, preferred_element_type=F32)
            dq_ref[rows, :] += dq_t
            return dk_acc, dv_acc

        zero = jnp.zeros((t, LANES), F32)
        dk_acc, dv_acc = lax.fori_loop(j, n_t, q_step, (zero, zero))
        dk_ref[...] = dk_acc
        dv_ref[...] = dv_acc

    blk = pl.BlockSpec((t, LANES), lambda hp, j: (j, hp))
    full = pl.BlockSpec((S, LANES), lambda hp, j: (0, hp))
    k_blk = pl.BlockSpec((t, LANES), lambda hp, j: (j, n_hp + hp))
    v_blk = pl.BlockSpec((t, LANES), lambda hp, j: (j, 2 * n_hp + hp))
    return pl.pallas_call(
        body, name="flash_bwd", grid=(n_hp, n_t),
        in_specs=([full, k_blk, v_blk, full, full, full, pl.BlockSpec((n_t, t, t), lambda hp, j: (0, 0, 0))]
                  + [pl.BlockSpec(memory_space=pl.ANY)] * len(after)),
        out_specs=[full, blk, blk],
        out_shape=[SDS((S, ATT_W), F32)] * 3,
        compiler_params=_cparams(("parallel", "arbitrary")),
    )(qkv, qkv, qkv, o, do, lse, bias, *after)


SCAN_T = 256
SCAN_GROUP = 8
SCAN_STEPS = (1, 2, 4)
ST_ROWS = 2 * N_CPLX // LANES
HALF = ST_ROWS // 2


def _scan_tables(lam_t):
    lam = lax.complex(lam_t[:HALF].reshape(N_CPLX), lam_t[HALF:].reshape(N_CPLX))
    pows = [lam]
    for _ in range(SCAN_GROUP - 1):
        pows.append(pows[-1] * lam)
    pows = jnp.stack(pows)
    sub = jnp.arange(SCAN_GROUP)[:, None]
    fwd = [jnp.where(sub >= k, pows[k - 1][None, :], 0.0) for k in SCAN_STEPS] + [pows]
    conj = jnp.conj(pows)
    bwd = [jnp.where(sub <= SCAN_GROUP - 1 - k, conj[k - 1][None, :], 0.0) for k in SCAN_STEPS] + [conj[::-1]]

    def pack(tabs):
        return jnp.stack([jnp.concatenate([jnp.real(t), jnp.imag(t)], axis=1) for t in tabs]).astype(F32)

    return pack(fwd), pack(bwd)


def _cmul_add(xr, xi, lr, li, sr, si):
    return xr + lr * sr - li * si, xi + lr * si + li * sr


def _group_scan(xr, xi, tab_ref, cr, ci, reverse):
    for j, k in enumerate(SCAN_STEPS):
        shift = SCAN_GROUP - k if reverse else k
        xr, xi = _cmul_add(xr, xi, tab_ref[j, :, :N_CPLX], tab_ref[j, :, N_CPLX:],
                           pltpu.roll(xr, shift, 0), pltpu.roll(xi, shift, 0))
    return _cmul_add(xr, xi, tab_ref[3, :, :N_CPLX], tab_ref[3, :, N_CPLX:],
                     jnp.broadcast_to(cr, (SCAN_GROUP, N_CPLX)), jnp.broadcast_to(ci, (SCAN_GROUP, N_CPLX)))


SSM_SUPER = 4
SB_ROWS = SSM_W // SSM_SUPER
SB_COLS = N_CPLX // SSM_SUPER


def _super_blocks():
    return [(slice(b * SB_ROWS, (b + 1) * SB_ROWS), slice(h * SB_COLS, (h + 1) * SB_COLS),
             slice(h * N_CPLX + b * SB_COLS, h * N_CPLX + (b + 1) * SB_COLS))
            for b in range(SSM_SUPER) for h in range(2)]


def _dot16(a, b, dims):
    return lax.dot_general(a.astype(BF16), b.astype(BF16), dims, preferred_element_type=F32)


def _s5_fwd(tab, u_arr, u_cols, w_b, w_ct):
    nc = N_CPLX

    def body(tab_ref, u_ref, wb_ref, wct_ref, st_ref, y_ref, carry, bu_scr):
        @pl.when(pl.program_id(0) == 0)
        def _():
            carry[...] = jnp.zeros_like(carry)

        for rows_b, cols_c, cols_s in _super_blocks():
            bu_scr[:, cols_s] = _dot16(u_ref[:, rows_b], wb_ref[rows_b, cols_c], NN)

        def group(a, c):
            rows = pl.ds(pl.multiple_of(a * SCAN_GROUP, SCAN_GROUP), SCAN_GROUP)
            xr, xi = _group_scan(bu_scr[rows, :nc], bu_scr[rows, nc:], tab_ref, c[0], c[1], False)
            st_ref[rows, :nc] = xr
            st_ref[rows, nc:] = xi
            return xr[SCAN_GROUP - 1:SCAN_GROUP, :], xi[SCAN_GROUP - 1:SCAN_GROUP, :]

        cr, ci = lax.fori_loop(0, SCAN_T // SCAN_GROUP, group, (carry[:, :nc], carry[:, nc:]), unroll=2)
        carry[:, :nc] = cr
        carry[:, nc:] = ci

        for b in range(SSM_SUPER):
            (rows_b, cols_re, st_re), (_, cols_im, st_im) = _super_blocks()[2 * b:2 * b + 2]
            y_ref[:, rows_b] = (_dot16(st_ref[:, st_re], wct_ref[rows_b, cols_re], NT)
                                + _dot16(st_ref[:, st_im], wct_ref[rows_b, cols_im], NT))

    const = lambda shape: pl.BlockSpec(shape, lambda i: (0,) * len(shape))
    return pl.pallas_call(
        body, name="s5_fwd", grid=(S // SCAN_T,),
        in_specs=[const((4, SCAN_GROUP, 2 * nc)), pl.BlockSpec((SCAN_T, SSM_W), lambda i: (i, u_cols[0] // SSM_W)),
                  const((SSM_W, 2 * SB_COLS)), const((SSM_W, 2 * SB_COLS))],
        out_specs=[pl.BlockSpec((SCAN_T, 2 * nc), lambda i: (i, 0)), pl.BlockSpec((SCAN_T, SSM_W), lambda i: (i, 0))],
        out_shape=[SDS((S, 2 * nc), F32), SDS((S, SSM_W), F32)],
        scratch_shapes=[pltpu.VMEM((1, 2 * nc), F32), pltpu.VMEM((SCAN_T, 2 * nc), F32)],
        compiler_params=_cparams(("arbitrary",)),
    )(tab, u_arr, w_b, w_ct)


def _s5_bwd(tab, dy, states, u_arr, u_cols, w_b, w_ct):
    n_blk = S // SCAN_T
    nc = N_CPLX

    def body(tab_ref, dy_ref, x_ref, u_ref, wb_ref, wct_ref, du_ref, dlam_ref, dwb_ref, dwct_ref,
             carry, acc, d_scr, g_ref):
        i = pl.program_id(0)

        @pl.when(i == 0)
        def _():
            carry[...] = jnp.zeros_like(carry)
            acc[...] = jnp.zeros_like(acc)
            dwb_ref[...] = jnp.zeros_like(dwb_ref)
            dwct_ref[...] = jnp.zeros_like(dwct_ref)

        for rows_b, cols_c, cols_s in _super_blocks():
            d_scr[:, cols_s] = _dot16(dy_ref[:, rows_b], wct_ref[rows_b, cols_c], NN)

        last_row = lax.broadcasted_iota(jnp.int32, (SCAN_GROUP, 1), 0) == SCAN_GROUP - 1
        d_ref = d_scr

        def group(j, c):
            cr, ci = c
            rows = pl.ds(pl.multiple_of((SCAN_T // SCAN_GROUP - 1 - j) * SCAN_GROUP, SCAN_GROUP), SCAN_GROUP)
            gr, gi = _group_scan(d_ref[rows, :nc], d_ref[rows, nc:], tab_ref, cr, ci, True)
            g_ref[rows, :nc] = gr
            g_ref[rows, nc:] = gi
            nr = jnp.where(last_row, jnp.broadcast_to(cr, (SCAN_GROUP, nc)), pltpu.roll(gr, SCAN_GROUP - 1, 0))
            ni = jnp.where(last_row, jnp.broadcast_to(ci, (SCAN_GROUP, nc)), pltpu.roll(gi, SCAN_GROUP - 1, 0))
            sr, si = x_ref[rows, :nc], x_ref[rows, nc:]
            acc[:, :nc] += nr * sr + ni * si
            acc[:, nc:] += ni * sr - nr * si
            return gr[0:1, :], gi[0:1, :]

        cr, ci = lax.fori_loop(0, SCAN_T // SCAN_GROUP, group, (carry[:, :nc], carry[:, nc:]), unroll=2)
        carry[:, :nc] = cr
        carry[:, nc:] = ci

        for b in range(SSM_SUPER):
            (rows_b, cols_re, st_re), (_, cols_im, st_im) = _super_blocks()[2 * b:2 * b + 2]
            du_ref[:, rows_b] = (_dot16(g_ref[:, st_re], wb_ref[rows_b, cols_re], NT)
                                 + _dot16(g_ref[:, st_im], wb_ref[rows_b, cols_im], NT))
            for cols_c, cols_s in ((cols_re, st_re), (cols_im, st_im)):
                dwb_ref[rows_b, cols_c] += _dot16(u_ref[:, rows_b], g_ref[:, cols_s], TN)
                dwct_ref[rows_b, cols_c] += _dot16(dy_ref[:, rows_b], x_ref[:, cols_s], TN)

        @pl.when(i == n_blk - 1)
        def _():
            dlam_ref[...] = jnp.sum(acc[...], axis=0, keepdims=True)

    const = lambda shape: pl.BlockSpec(shape, lambda i: (0,) * len(shape))
    rows = lambda width, col_block=0: pl.BlockSpec((SCAN_T, width), lambda i: (n_blk - 1 - i, col_block))
    maps = const((SSM_W, 2 * SB_COLS))
    return pl.pallas_call(
        body, name="s5_bwd", grid=(n_blk,),
        in_specs=[const((4, SCAN_GROUP, 2 * nc)), rows(SSM_W), rows(2 * nc), rows(SSM_W, u_cols[0] // SSM_W), maps, maps],
        out_specs=[rows(SSM_W), const((1, 2 * nc)), maps, maps],
        out_shape=[SDS((S, SSM_W), F32), SDS((1, 2 * nc), F32), SDS((SSM_W, 2 * SB_COLS), F32),
                   SDS((SSM_W, 2 * SB_COLS), F32)],
        scratch_shapes=[pltpu.VMEM((1, 2 * nc), F32), pltpu.VMEM((SCAN_GROUP, 2 * nc), F32),
                        pltpu.VMEM((SCAN_T, 2 * nc), F32), pltpu.VMEM((SCAN_T, 2 * nc), F32)],
        compiler_params=_cparams(("arbitrary",)),
    )(tab, dy, states, u_arr, w_b, w_ct)


def _ssm_prep(a_re, a_im, log_dt, b_re, b_im, c_re, c_im):
    lam = lax.complex(a_re, a_im)
    dt = jnp.exp(log_dt)[:, None]
    lam_bar = jnp.exp(lam * dt)
    b_bar = ((lam_bar - 1.0) / lam)[..., None] * lax.complex(b_re, b_im)
    lam_t = jnp.concatenate([jnp.real(lam_bar).reshape(HALF, LANES), jnp.imag(lam_bar).reshape(HALF, LANES)], axis=0)
    groups_per_super = SSM_GROUPS // SSM_SUPER
    on_diag = ((lax.broadcasted_iota(jnp.int32, (SSM_W, SB_COLS), 0) // SSM_GROUP) % groups_per_super
               == lax.broadcasted_iota(jnp.int32, (SSM_W, SB_COLS), 1) // SSM_STATE)

    def compact(m):
        return jnp.where(on_diag, jnp.tile(m.reshape(SSM_W, SSM_STATE), (1, groups_per_super)), 0.0)

    w_b = jnp.concatenate([compact(jnp.real(b_bar).transpose(0, 2, 1)),
                           compact(jnp.imag(b_bar).transpose(0, 2, 1))], axis=1)
    w_ct = jnp.concatenate([compact(c_re), -compact(c_im)], axis=1)
    return lam_t, w_b, w_ct


U_SSM_COLS = (4 * ATT_W, SSM_W)


def _row(v):
    return v.reshape(1, -1)


def _even_fwd(x, h, tail, pre, post, w_in, late_w, glu_b, ssm_d, prep, tables):
    lam_t, w_b, w_ct = prep
    proj = _mm(h, w_in, "nn", F32, b_blocks=True)
    qkv = _qkv_prep(proj, tables[:3])
    w_out, glu_w = late_w(qkv)
    att, lse = _flash_fwd(qkv, tables[3])
    scan_fwd_tab, scan_bwd_tab = _scan_tables(lam_t)
    states, y = _s5_fwd(scan_fwd_tab, proj, U_SSM_COLS, w_b, w_ct)

    def act1(yv, uv, dv):
        return (_gelu_and_grad(yv + dv * uv)[0],), ()

    (z1,) = _rowwise(act1, [(y, SSM_W, 0), (proj, SSM_W, 8), (ssm_d, SSM_W, 0)], [(SSM_W, F32)], name="ssm_act_fwd")
    lin = _mm(z1, glu_w, "nn", F32)

    def gate(att_v, ga, gs, z1v, linv, bv):
        ssm_out = z1v * _sigmoid(linv + bv)
        return (jnp.concatenate([att_v * _silu_and_grad(ga)[0], ssm_out * _silu_and_grad(gs)[0]], axis=1),), ()

    (merged,) = _rowwise(gate, [(att, ATT_W, 0), (proj, ATT_W, 3), (proj, SSM_W, 9), (z1, SSM_W, 0),
                                (lin, SSM_W, 0), (glu_b, SSM_W, 0)], [(EVEN_OUT, BF16)], name="even_gate_fwd")
    yout = _mm(merged, w_out, "nn", F32)
    saved = (x, h, proj, qkv, att, lse, states, y, z1, lin, merged, yout, w_out, glu_w, scan_bwd_tab)
    return tail(x, yout, post) + (saved,)


def _even_bwd(g, saved, pre, post, w_in, late_w, glu_b, ssm_d, prep, tables, on_w, on_ssm):
    x, h, proj, qkv, att, lse, states, y, z1, lin, merged, yout, w_out, glu_w, scan_bwd_tab = saved
    lam_t, w_b, w_ct = prep
    dyout, dpost = _post_bwd(g, yout, post)
    dmerged = _mm(dyout, w_out, "nt", F32)
    dw_out = _mm(merged, dyout, "tn", BF16)

    def gate_bwd(dm_a, dm_s, att_v, ga, gs, z1v, linv, bv):
        sa, dsa = _silu_and_grad(ga)
        ss, dss = _silu_and_grad(gs)
        sig = _sigmoid(linv + bv)
        ssm_out = z1v * sig
        dssm = dm_s * ss
        dlin = dssm * z1v * sig * (1.0 - sig)
        return (dm_a * sa, dm_a * att_v * dsa, dm_s * ssm_out * dss, dssm * sig, dlin), (dlin,)

    datt, dg_att, dg_ssm, dz1a, dlin, dglu_b = _rowwise(
        gate_bwd, [(dmerged, ATT_W, 0), (dmerged, SSM_W, 2), (att, ATT_W, 0), (proj, ATT_W, 3), (proj, SSM_W, 9),
                   (z1, SSM_W, 0), (lin, SSM_W, 0), (glu_b, SSM_W, 0)],
        [(ATT_W, F32), (ATT_W, BF16), (SSM_W, BF16), (SSM_W, F32), (SSM_W, BF16)], [SSM_W], name="even_gate_bwd")
    dz1b = _mm(dlin, glu_w, "nt", F32)
    dglu_w = _mm(z1, dlin, "tn", BF16)

    def act1_bwd(da, db, yv, uv, dv):
        dpre = (da + db) * _gelu_and_grad(yv + dv * uv)[1]
        return (dpre, dpre * dv), (dpre * uv,)

    sent_late_w = on_w(dict(w_out=dw_out, glu_w=dglu_w))
    dy, du_direct, dd = _rowwise(act1_bwd, [(dz1a, SSM_W, 0), (dz1b, SSM_W, 0), (y, SSM_W, 0), (proj, SSM_W, 8),
                                            (ssm_d, SSM_W, 0)], [(SSM_W, BF16), (SSM_W, F32)], [SSM_W],
                                 name="ssm_act_bwd", after=(sent_late_w,))
    du_state, dlam_row, dw_b, dw_ct = _s5_bwd(scan_bwd_tab, dy, states, proj, U_SSM_COLS, w_b, w_ct)
    dlam = jnp.concatenate([dlam_row[0, :N_CPLX].reshape(HALF, LANES), dlam_row[0, N_CPLX:].reshape(HALF, LANES)],
                           axis=0)
    sent_ssm = on_ssm((dlam, dw_b, dw_ct))
    dq, dk, dv = _flash_bwd(qkv, att, datt, lse, tables[3], after=() if sent_ssm is None else (sent_ssm,))

    def assemble(dqv, dkv, dvv, dga, dua, dub, dgs, c, s, sw):
        rot = _rotate(jnp.concatenate([dqv, dkv], axis=1), c, s, sw, True)
        return (jnp.concatenate([(rot[:, :ATT_W] * HEAD_DIM ** -0.5).astype(BF16), rot[:, ATT_W:].astype(BF16),
                                 dvv.astype(BF16), dga, (dua + dub).astype(BF16), dgs], axis=1),), ()

    (dproj,) = _rowwise(assemble, [(dq, ATT_W, 0), (dk, ATT_W, 0), (dv, ATT_W, 0), (dg_att, ATT_W, 0),
                                   (du_state, SSM_W, 0), (du_direct, SSM_W, 0), (dg_ssm, SSM_W, 0),
                                   (tables[0], LANES, 0), (tables[1], LANES, 0), (tables[2], LANES, 0)],
                        [(EVEN_IN, BF16)], name="dproj_assemble")
    dw_in = _mm(h, dproj, "tn", BF16, out_blocks=True)
    sent = on_w(dict(w_in=dw_in))
    dh = _mm(dproj, w_in, "nt", F32, b_blocks=True, after=(sent,))
    g_prev, dpre = _pre_bwd(g, dh, x, pre)
    return g_prev, dict(pre=dpre, post=dpost, glu_b=dglu_b, ssm_d=dd)


def _odd_fwd(x, h, tail, pre, post, w_in, pool_w, pool_scale, w_out):
    proj = _mm(h, w_in, "nn", F32, b_blocks=True)
    mixed = _pool(proj, 0, False, BF16)
    ylin = _gmm(mixed, pool_w, "nn", F32)

    def gate(yl, gt, sc):
        return (yl * sc * _silu_and_grad(gt)[0],), ()

    (z,) = _rowwise(gate, [(ylin, POOL_W, 0), (proj, POOL_W, 1), (pool_scale, POOL_W, 0)], [(POOL_W, BF16)],
                    name="odd_gate_fwd")
    yout = _mm(z, w_out, "nn", F32)
    return tail(x, yout, post) + ((x, h, proj, mixed, ylin, z, yout),)


def _odd_bwd(g, saved, pre, post, w_in, pool_w, pool_scale, w_out, on_w):
    x, h, proj, mixed, ylin, z, yout = saved
    dyout, dpost = _post_bwd(g, yout, post)
    dz = _mm(dyout, w_out, "nt", F32)
    dw_out = _mm(z, dyout, "tn", BF16)

    def gate_bwd(dzv, yl, gt, sc):
        sg, dsg = _silu_and_grad(gt)
        tt = dzv * sg
        return (tt * sc, dzv * yl * sc * dsg), (tt * yl,)

    dylin, dproj_gate, dscale = _rowwise(gate_bwd, [(dz, POOL_W, 0), (ylin, POOL_W, 0), (proj, POOL_W, 1),
                                                    (pool_scale, POOL_W, 0)],
                                         [(POOL_W, BF16), (POOL_W, BF16, ODD_IN, 1)], [POOL_W], name="odd_gate_bwd")
    dmixed = _gmm(dylin, pool_w, "nt", F32)
    dpool_w = _gmm(mixed, dylin, "tn", BF16)
    dproj = _pool(dmixed, 0, True, BF16, into=dproj_gate)
    dw_in = _mm(h, dproj, "tn", BF16, out_blocks=True)
    sent = on_w(dict(w_in=dw_in, w_out=dw_out, pool_w=dpool_w))
    dh = _mm(dproj, w_in, "nt", F32, b_blocks=True, after=(sent,))
    g_prev, dpre = _pre_bwd(g, dh, x, pre)
    return g_prev, dict(pre=dpre, post=dpost, pool_scale=dscale)


def _my_index():
    return 4 * lax.axis_index("x") + 2 * lax.axis_index("y") + lax.axis_index("c")


HBM_SPEC = pl.BlockSpec(memory_space=pltpu.HBM)
SEM_SPEC = pl.BlockSpec(memory_space=pltpu.SEMAPHORE)
SPLIT_EFFECT = pltpu.SideEffectType.DATAFLOW_SIDE_EFFECTING


def _device_of(j):
    return (j // 4, (j // 2) % 2, j % 2)


def _split_copy(srcs, lands, send_sems, recv_sems, gather, i, j, dst_slot, recv_slot):
    return pltpu.make_async_remote_copy(
        src_ref=srcs[i] if gather else srcs[i].at[j], dst_ref=lands[i].at[dst_slot],
        send_sem=send_sems.at[i * N_DEV + j], recv_sem=recv_sems.at[i * N_DEV + recv_slot],
        device_id=_device_of(j), device_id_type=MESH_ID)


def _own_copy(srcs, lands, send_sems, gather, i, me):
    return pltpu.make_async_copy(srcs[i] if gather else srcs[i].at[me], lands[i].at[me], send_sems.at[i * N_DEV + me])


def _xchg_start(name, srcs, gather, after=()):
    n = len(srcs)
    n_in = n + len(after)

    def body(*refs):
        src_refs = refs[:n]
        send_sems, recv_sems, token = refs[n_in], refs[n_in + 1], refs[-1]
        land_refs = refs[n_in + 2 + n:n_in + 2 + 2 * n]
        me = _my_index()
        for j in range(N_DEV):
            @pl.when(me != j)
            def _(j=j):
                for i in range(n):
                    _split_copy(src_refs, land_refs, send_sems, recv_sems, gather, i, j, me, me).start()
        for i in range(n):
            _own_copy(src_refs, land_refs, send_sems, gather, i, me).start()
        token[...] = jnp.zeros_like(token)

    land_shapes = [((N_DEV,) + a.shape) if gather else a.shape for a in srcs]
    thru = ([pltpu.HBM(a.shape, a.dtype) for a in srcs] + [pltpu.HBM(s, a.dtype) for s, a in zip(land_shapes, srcs)])
    res = pl.pallas_call(
        body, name=name,
        out_shape=(pltpu.SemaphoreType.DMA((n * N_DEV,)), pltpu.SemaphoreType.DMA((n * N_DEV,)), *thru,
                   SDS((8, LANES), F32)),
        in_specs=[HBM_SPEC] * n + [pl.BlockSpec(memory_space=pl.ANY)] * len(after),
        out_specs=(SEM_SPEC, SEM_SPEC, *([HBM_SPEC] * (2 * n)), pl.BlockSpec(memory_space=pltpu.VMEM)),
        input_output_aliases={i: 2 + i for i in range(n)},
        compiler_params=pltpu.CompilerParams(has_side_effects=SPLIT_EFFECT),
    )(*[pltpu.with_memory_space_constraint(a, pltpu.HBM) for a in srcs], *after)
    return res[0], res[1], list(res[2:2 + n]), list(res[2 + n:2 + 2 * n]), res[-1]


def _xchg_wait(name, started, gather, after):
    send_sems, recv_sems, srcs, lands, _ = started
    n = len(srcs)

    def body(*refs):
        src_refs, land_refs = refs[:n], refs[n:2 * n]
        send_r, recv_r = refs[2 * n], refs[2 * n + 1]
        me = _my_index()
        for j in range(N_DEV):
            @pl.when(me != j)
            def _(j=j):
                for i in range(n):
                    _split_copy(src_refs, land_refs, send_r, recv_r, gather, i, j, me, me).wait_send()
                    _split_copy(src_refs, land_refs, send_r, recv_r, gather, i, j, j, j).wait_recv()
        for i in range(n):
            _own_copy(src_refs, land_refs, send_r, gather, i, me).wait()

    thru = [pltpu.HBM(a.shape, a.dtype) for a in list(srcs) + list(lands)]
    res = pl.pallas_call(
        body, name=name, out_shape=tuple(thru),
        in_specs=[HBM_SPEC] * (2 * n) + [SEM_SPEC, SEM_SPEC] + [pl.BlockSpec(memory_space=pl.ANY)] * len(after),
        out_specs=tuple([HBM_SPEC] * (2 * n)),
        input_output_aliases={i: i for i in range(2 * n)},
        compiler_params=pltpu.CompilerParams(has_side_effects=SPLIT_EFFECT),
    )(*srcs, *lands, send_sems, recv_sems, *after)
    return list(res[n:])


def _adam_layers(w, slot_list, m, v, name):
    n_l, r, c = w.shape
    ns = slot_list[0].shape[0]
    tr = r
    while tr * c * 4 > (1 << 20) and tr % 16 == 0:
        tr //= 2
    assert r % tr == 0 and len(slot_list) == n_l

    def body(*refs):
        w_ref, slot_refs = refs[0], refs[1:1 + n_l]
        m_ref, v_ref, go_ref, d_ref, mo_ref, vo_ref = refs[1 + n_l:]
        layer = pl.program_id(0)
        g = None
        for l, g_ref in enumerate(slot_refs):
            gl = g_ref[0].astype(F32)
            for s in range(1, ns):
                gl = gl + g_ref[s].astype(F32)
            g = gl if g is None else jnp.where(layer == l, gl, g)
        mn = ADAM_B1 * m_ref[...] + (1.0 - ADAM_B1) * g
        vn = ADAM_B2 * v_ref[...] + (1.0 - ADAM_B2) * (g * g)
        m_hat = mn / (1.0 - ADAM_B1 ** ADAM_STEP)
        v_hat = vn / (1.0 - ADAM_B2 ** ADAM_STEP)
        go_ref[...] = g
        d_ref[...] = -ADAM_LR * (m_hat / (jnp.sqrt(v_hat) + ADAM_EPS) + ADAM_WD * w_ref[...])
        mo_ref[...] = mn
        vo_ref[...] = vn

    blk = pl.BlockSpec((None, tr, c), lambda l, i: (l, i, 0))
    slot_specs = [pl.BlockSpec((ns, tr, c), lambda l, i, k=k: (0, jnp.where(l == k, i, 0), 0)) for k in range(n_l)]
    return pl.pallas_call(
        body, name=name, grid=(n_l, r // tr),
        in_specs=[blk] + slot_specs + [blk, blk],
        out_specs=[blk] * 4, out_shape=[SDS((n_l, r, c), F32)] * 4,
        compiler_params=_cparams(("arbitrary", "arbitrary")),
    )(*_in_hbm((w, *slot_list, m, v)))


def _adam(w, gslots, m, v, name):
    r, c = w.shape
    ns = gslots.shape[0]
    tr = r
    while tr * c * 4 > (1 << 20) and tr % 16 == 0:
        tr //= 2
    assert r % tr == 0

    def body(w_ref, g_ref, m_ref, v_ref, go_ref, d_ref, mo_ref, vo_ref):
        g = g_ref[0].astype(F32)
        for s in range(1, ns):
            g = g + g_ref[s].astype(F32)
        wv = w_ref[...]
        mn = ADAM_B1 * m_ref[...] + (1.0 - ADAM_B1) * g
        vn = ADAM_B2 * v_ref[...] + (1.0 - ADAM_B2) * (g * g)
        m_hat = mn / (1.0 - ADAM_B1 ** ADAM_STEP)
        v_hat = vn / (1.0 - ADAM_B2 ** ADAM_STEP)
        go_ref[...] = g
        d_ref[...] = -ADAM_LR * (m_hat / (jnp.sqrt(v_hat) + ADAM_EPS) + ADAM_WD * wv)
        mo_ref[...] = mn
        vo_ref[...] = vn

    blk = pl.BlockSpec((tr, c), lambda i: (i, 0))
    return pl.pallas_call(
        body, name=name, grid=(r // tr,),
        in_specs=[blk, pl.BlockSpec((ns, tr, c), lambda i: (0, i, 0)), blk, blk],
        out_specs=[blk] * 4, out_shape=[SDS((r, c), F32)] * 4,
        compiler_params=_cparams(("parallel",)),
    )(w, gslots, m, v)


def _sum_slots(slots, name):
    ns, r, c = slots.shape

    def body(g_ref, o_ref):
        g = g_ref[0]
        for s in range(1, ns):
            g = g + g_ref[s]
        o_ref[...] = g

    return pl.pallas_call(
        body, name=name, grid=(1,),
        in_specs=[pl.BlockSpec((ns, r, c), lambda i: (0, 0, 0))], out_specs=pl.BlockSpec((r, c), lambda i: (0, 0)),
        out_shape=SDS((r, c), F32), compiler_params=_cparams(("arbitrary",)),
    )(slots)


def _adam_params(params, name):
    n = len(params)

    def body(*refs):
        ins, outs = refs[:5 * n], refs[5 * n:]
        for p in range(n):
            w_ref, m_ref, v_ref, g_first, g_rest = ins[5 * p:5 * p + 5]
            go_ref, d_ref, mo_ref, vo_ref = outs[4 * p:4 * p + 4]
            for part, g_ref in ((slice(0, 1), g_first), (slice(1, w_ref.shape[0]), g_rest)):
                g = g_ref[...]
                mn = ADAM_B1 * m_ref[part] + (1.0 - ADAM_B1) * g
                vn = ADAM_B2 * v_ref[part] + (1.0 - ADAM_B2) * (g * g)
                m_hat = mn / (1.0 - ADAM_B1 ** ADAM_STEP)
                v_hat = vn / (1.0 - ADAM_B2 ** ADAM_STEP)
                go_ref[part] = g
                d_ref[part] = -ADAM_LR * (m_hat / (jnp.sqrt(v_hat) + ADAM_EPS) + ADAM_WD * w_ref[part])
                mo_ref[part] = mn
                vo_ref[part] = vn

    def whole(a):
        return pl.BlockSpec(a.shape, lambda i, nd=a.ndim: (0,) * nd)

    flat = _in_hbm([a for prm in params for a in prm])
    outs = pl.pallas_call(
        body, name=name, grid=(1,),
        in_specs=[whole(a) for a in flat],
        out_specs=[whole(prm[0]) for prm in params for _ in range(4)],
        out_shape=[SDS(prm[0].shape, F32) for prm in params for _ in range(4)],
        compiler_params=_cparams(("arbitrary",)),
    )(*flat)
    return [outs[4 * p:4 * p + 4] for p in range(n)]


SMALL_NAMES = ("pre_norm", "post_norm", "ssm_a_re", "ssm_a_im", "ssm_log_dt", "ssm_b_re", "ssm_b_im", "ssm_c_re",
               "ssm_c_im", "ssm_d", "ssm_glu_b")
SSM_NAMES = ("ssm_a_re", "ssm_a_im", "ssm_log_dt", "ssm_b_re", "ssm_b_im", "ssm_c_re", "ssm_c_im")
WEIGHT_ORDER = ("pre_norm", "post_norm", "even_w_in", "even_w_out", "ssm_a_re", "ssm_a_im", "ssm_log_dt", "ssm_b_re",
                "ssm_b_im", "ssm_c_re", "ssm_c_im", "ssm_d", "ssm_glu_w", "ssm_glu_b", "odd_w_in", "pool_w",
                "pool_scale", "odd_w_out")
PACK_ROWS_ALIGN = 8


def _pack(parts):
    flat = jnp.concatenate([p.reshape(-1).astype(F32) for p in parts])
    rows = -(-flat.shape[0] // (LANES * PACK_ROWS_ALIGN)) * PACK_ROWS_ALIGN
    return jnp.pad(flat, (0, rows * LANES - flat.shape[0])).reshape(rows, LANES)


def _unpack(packed, shapes):
    flat = packed.reshape(-1)
    out, off = [], 0
    for shp in shapes:
        size = math.prod(shp)
        out.append(flat[off:off + size].reshape(shp))
        off += size
    return out


EVEN_SHARDED = ("w_in", "w_out", "glu_w")
ODD_SHARDED = ("w_in", "pool_w", "w_out")
FAMILY = {(0, "w_in"): "even_w_in", (0, "w_out"): "even_w_out", (0, "glu_w"): "ssm_glu_w",
          (1, "w_in"): "odd_w_in", (1, "pool_w"): "pool_w", (1, "w_out"): "odd_w_out"}


def _sharded_keys(layer):
    return EVEN_SHARDED if layer % 2 == 0 else ODD_SHARDED


def _local_step(x, tgt, small, get_weights, on_w, on_ssm, on_grads, zero=0.0):
    tables = _rope_tables(zero) + (_attention_bias(zero),)
    preps, prep_vjps = [], []
    for i in range(2):
        out, vjp = jax.vjp(_ssm_prep, small["ssm_a_re"][i] + zero, small["ssm_a_im"][i], small["ssm_log_dt"][i],
                           small["ssm_b_re"][i], small["ssm_b_im"][i], small["ssm_c_re"][i], small["ssm_c_im"][i])
        preps.append(out)
        prep_vjps.append(vjp)

    def layer_args(layer, wts):
        i = layer // 2
        pre, post = _row(small["pre_norm"][layer]) + wts.get("token", 0.0), _row(small["post_norm"][layer])
        if layer % 2 == 0:
            return (pre, post, wts["w_in"], wts["late"], _row(small["ssm_glu_b"][i]), _row(small["ssm_d"][i]),
                    preps[i], tables)
        return (pre, post, wts["w_in"], wts["pool_w"], _row(wts["pool_scale"]), wts["w_out"])

    saved, args = [], []
    cur = x
    for layer in range(4):
        after = (cur,) if layer else (cur, tables[0], tables[3], preps[0][1], preps[0][2], preps[1][1], preps[1][2])
        args.append(layer_args(layer, get_weights(layer, after)))
        if layer == 0:
            h = _norm_fwd(cur, args[0][0])
        if layer < 3:
            def tail(xv, yv, post, next_gain=_row(small["pre_norm"][layer + 1])):
                return tuple(_post_fwd(xv, yv, post, next_gain))
        else:
            def tail(xv, yv, post):
                return tuple(_post_fwd_loss(xv, yv, post, tgt))
        cur, h, sv = (_even_fwd if layer % 2 == 0 else _odd_fwd)(cur, h, tail, *args[layer])
        saved.append(sv)
    g, sq = cur, h
    loss = 0.5 * jnp.sum(sq) / D

    lg = [None] * 4
    token = jnp.zeros((), F32)
    for layer in reversed(range(4)):
        largs = list(args[layer])
        largs[1] = largs[1] + token
        hooks = dict(on_w=functools.partial(on_w, layer))
        ssm_grads = []
        if layer % 2 == 0:
            def ssm_hook(cotangents, layer=layer):
                ssm_grads.append(prep_vjps[layer // 2](cotangents))
                return on_ssm(layer, ssm_grads[0])

            hooks["on_ssm"] = ssm_hook
        g, lg[layer] = (_even_bwd if layer % 2 == 0 else _odd_bwd)(g, saved[layer], *largs, **hooks)
        if ssm_grads:
            lg[layer]["ssm"] = ssm_grads[0]
        token = on_grads(layer, lg[layer])
    return loss, g, token


def _to_slots(key, gfull):
    if key == "w_in":
        return gfull
    if key in ("w_out", "glu_w"):
        rr, nn = gfull.shape
        return gfull.reshape(N_DEV, rr // N_DEV, nn)
    assert key == "pool_w"
    gg, rr, nn = gfull.shape
    return gfull.reshape(gg, N_DEV, rr // N_DEV, nn).transpose(1, 0, 2, 3)


def _from_gathered(key, gat):
    if key == "w_in":
        return gat
    if key in ("w_out", "glu_w"):
        _, rr, nn = gat.shape
        return gat.reshape(N_DEV * rr, nn)
    assert key == "pool_w"
    _, gg, rr, nn = gat.shape
    return gat.transpose(1, 0, 2, 3).reshape(gg, N_DEV * rr, nn)


def kernel(x, pre_norm, post_norm, even_w_in, even_w_out, ssm_a_re, ssm_a_im, ssm_log_dt, ssm_b_re, ssm_b_im, ssm_c_re, ssm_c_im, ssm_d, ssm_glu_w, ssm_glu_b, odd_w_in, pool_w, pool_scale, odd_w_out, loss_target, m_pre_norm, m_post_norm, m_even_w_in, m_even_w_out, m_ssm_a_re, m_ssm_a_im, m_ssm_log_dt, m_ssm_b_re, m_ssm_b_im, m_ssm_c_re, m_ssm_c_im, m_ssm_d, m_ssm_glu_w, m_ssm_glu_b, m_odd_w_in, m_pool_w, m_pool_scale, m_odd_w_out, v_pre_norm, v_post_norm, v_even_w_in, v_even_w_out, v_ssm_a_re, v_ssm_a_im, v_ssm_log_dt, v_ssm_b_re, v_ssm_b_im, v_ssm_c_re, v_ssm_c_im, v_ssm_d, v_ssm_glu_w, v_ssm_glu_b, v_odd_w_in, v_pool_w, v_pool_scale, v_odd_w_out):
    w = dict(pre_norm=pre_norm, post_norm=post_norm, even_w_in=even_w_in, even_w_out=even_w_out, ssm_a_re=ssm_a_re,
             ssm_a_im=ssm_a_im, ssm_log_dt=ssm_log_dt, ssm_b_re=ssm_b_re, ssm_b_im=ssm_b_im, ssm_c_re=ssm_c_re,
             ssm_c_im=ssm_c_im, ssm_d=ssm_d, ssm_glu_w=ssm_glu_w, ssm_glu_b=ssm_glu_b, odd_w_in=odd_w_in,
             pool_w=pool_w, pool_scale=pool_scale, odd_w_out=odd_w_out)
    mom = dict(pre_norm=m_pre_norm, post_norm=m_post_norm, even_w_in=m_even_w_in, even_w_out=m_even_w_out,
               ssm_a_re=m_ssm_a_re, ssm_a_im=m_ssm_a_im, ssm_log_dt=m_ssm_log_dt, ssm_b_re=m_ssm_b_re,
               ssm_b_im=m_ssm_b_im, ssm_c_re=m_ssm_c_re, ssm_c_im=m_ssm_c_im, ssm_d=m_ssm_d, ssm_glu_w=m_ssm_glu_w,
               ssm_glu_b=m_ssm_glu_b, odd_w_in=m_odd_w_in, pool_w=m_pool_w, pool_scale=m_pool_scale,
               odd_w_out=m_odd_w_out)
    var = dict(pre_norm=v_pre_norm, post_norm=v_post_norm, even_w_in=v_even_w_in, even_w_out=v_even_w_out,
               ssm_a_re=v_ssm_a_re, ssm_a_im=v_ssm_a_im, ssm_log_dt=v_ssm_log_dt, ssm_b_re=v_ssm_b_re,
               ssm_b_im=v_ssm_b_im, ssm_c_re=v_ssm_c_re, ssm_c_im=v_ssm_c_im, ssm_d=v_ssm_d, ssm_glu_w=v_ssm_glu_w,
               ssm_glu_b=v_ssm_glu_b, odd_w_in=v_odd_w_in, pool_w=v_pool_w, pool_scale=v_pool_scale,
               odd_w_out=v_odd_w_out)
    me = _my_index()
    scale_cols = pool_scale.shape[1]

    def start_gather(tag, layer, keys, after=()):
        i = layer // 2
        shards = [w[FAMILY[(layer % 2, k)]][i].astype(BF16) for k in keys]
        if layer % 2 == 1:
            shards.append(jnp.pad(pool_scale[i][None], ((0, PACK_ROWS_ALIGN - 1), (0, 0))))
        return _xchg_start(f"gather_start_{tag}", shards, True, after)

    gather_started = {0: start_gather("0", 0, EVEN_SHARDED[:1])}
    small = {nm: w[nm] for nm in SMALL_NAMES}

    def get_weights(layer, after):
        keys = EVEN_SHARDED[:1] if layer == 0 else _sharded_keys(layer)
        lands = _xchg_wait(f"gather_wait_{layer}", gather_started[layer], True, after)
        wts = {k: _from_gathered(k, gat) for k, gat in zip(keys, lands)}
        if layer % 2 == 1:
            wts["pool_scale"] = lands[-1][:, 0, :].reshape(N_DEV * scale_cols)
        if layer == 0:
            prev = gather_started["0_late"] = start_gather("0_late", 0, EVEN_SHARDED[1:], after=(lands[0],))
            for later in (1, 2, 3):
                prev = gather_started[later] = start_gather(str(later), later, _sharded_keys(later), after=(prev[4],))
            wts["token"] = sum(gather_started[tag][4][0, 0] for tag in ("0_late", 1, 2, 3))

            def late(after_late):
                late_lands = _xchg_wait("gather_wait_0_late", gather_started["0_late"], True, (after_late,))
                return tuple(_from_gathered(k, gat) for k, gat in zip(EVEN_SHARDED[1:], late_lands))

            wts["late"] = late
        elif layer == 2:
            wts["late"] = lambda after_late: (wts["w_out"], wts["glu_w"])
        return wts

    scatter_started = []

    def on_w(layer, gw):
        keys = tuple(k for k in _sharded_keys(layer) if k in gw)
        started = _xchg_start(f"scatter_start_{layer}_{keys[0]}", [_to_slots(k, gw[k]) for k in keys], False)
        scatter_started.append((layer, keys, started))
        return started[4]

    def wait_scatters(layers, after):
        for layer, keys, started in scatter_started:
            if layer in layers:
                lands = _xchg_wait(f"scatter_wait_{layer}_{keys[0]}", started, False, after)
                for k, land in zip(keys, lands):
                    recv[(layer, k)] = land

    packed_names = ("pre_norm", "post_norm") + SSM_NAMES + ("ssm_d", "ssm_glu_b")
    tails = {nm: (SSM_GROUPS, SSM_STATE * SSM_GROUP) if nm in ("ssm_b_re", "ssm_b_im") else w[nm].shape[1:]
             for nm in packed_names}

    layer_grads = {}
    early_started, mid_started = [], []

    def on_ssm(layer, ssm_grads):
        if layer != 0:
            return None
        mid_started.append(_xchg_start("mid_start", [_pack(list(ssm_grads))], True))
        return mid_started[0][4]

    def on_grads(layer, lg):
        layer_grads[layer] = lg
        zero = jnp.zeros((), F32)
        if layer == 1:
            lgs = layer_grads
            early = ([jnp.concatenate([lgs[l][k] for l in (1, 2, 3)], axis=0) for k in ("pre", "post")]
                     + list(lgs[2]["ssm"]) + [lgs[2]["ssm_d"], lgs[2]["glu_b"],
                                              jnp.concatenate([lgs[1]["pool_scale"], lgs[3]["pool_scale"]], axis=0)])
            early_started.append(_xchg_start("small_start", [_pack(early)], True))
            zero = zero + early_started[0][4][0, 0]
        return zero

    loss_local, grad_x, token = _local_step(x[0], loss_target[0], small, get_weights, on_w, on_ssm, on_grads,
                                            zero=gather_started[0][4][0, 0])

    lg0 = layer_grads[0]
    late_started = _xchg_start("late_start", [_pack([lg0["pre"], lg0["post"], lg0["ssm_d"], lg0["glu_b"],
                                                     loss_local.reshape(1)]) + token], True)

    def adam_family(parity, k):
        nm = FAMILY[(parity, k)]
        shp = w[nm].shape
        cols = shp[-1]
        slot_list = [recv[(parity + 2 * i, k)].reshape(N_DEV, -1, cols) for i in range(2)]
        outs = _adam_layers(w[nm].reshape(2, -1, cols), slot_list, mom[nm].reshape(2, -1, cols),
                            var[nm].reshape(2, -1, cols), name=f"adam_{nm}")
        return [o.reshape(shp) for o in outs]

    recv, res = {}, {}
    wait_scatters((3, 1), (late_started[4],))
    for k in ODD_SHARDED:
        res[FAMILY[(1, k)]] = adam_family(1, k)
    odd_done = tuple(res[FAMILY[(1, k)]][0] for k in ODD_SHARDED)

    (early_slots,) = _xchg_wait("small_wait", early_started[0], True, odd_done)
    (mid_slots,) = _xchg_wait("mid_wait", mid_started[0], True, odd_done)
    early_shapes = [(w[nm].shape[0] - 1,) + tails[nm] for nm in packed_names] + [(2, N_DEV * scale_cols)]
    g_early = _unpack(_sum_slots(early_slots, "sum_small_early"), early_shapes)
    g_mid = _unpack(_sum_slots(mid_slots, "sum_small_mid"), [(1,) + tails[nm] for nm in SSM_NAMES])

    (late_slots,) = _xchg_wait("late_wait", late_started, True, (g_early[0], g_mid[0]))
    wait_scatters((2, 0), (late_slots,))
    for k in EVEN_SHARDED:
        res[FAMILY[(0, k)]] = adam_family(0, k)

    late_names = ("pre_norm", "post_norm", "ssm_d", "ssm_glu_b")
    g_late = _unpack(_sum_slots(late_slots, "sum_small_late"), [(1,) + tails[nm] for nm in late_names] + [(1,)])
    g_first = dict(zip(late_names, g_late))
    g_first.update(zip(SSM_NAMES, g_mid))
    dense = lambda nm, a: a.reshape((a.shape[0],) + tails[nm])
    outs = _adam_params([(dense(nm, w[nm]), dense(nm, mom[nm]), dense(nm, var[nm]), g_first[nm], g_early[j])
                         for j, nm in enumerate(packed_names)], "adam_small")
    for nm, four in zip(packed_names, outs):
        res[nm] = [o.reshape(w[nm].shape) for o in four]
    loss = g_late[-1].reshape(())
    g_scale = lax.dynamic_slice_in_dim(g_early[-1], me * scale_cols, scale_cols, axis=1)
    pad = ((0, PACK_ROWS_ALIGN - 2), (0, 0))
    outs = _adam(jnp.pad(pool_scale, pad), jnp.pad(g_scale, pad)[None], jnp.pad(m_pool_scale, pad),
                 jnp.pad(v_pool_scale, pad), name="adam_pool_scale")
    res["pool_scale"] = [o[:2] for o in outs]

    out = [loss, grad_x[None]]
    for kind in range(4):
        out += [res[nm][kind] for nm in WEIGHT_ORDER]
    return tuple(out)
```

```python
import functools
import math

import jax
import jax.numpy as jnp
from jax import lax
from jax.experimental import pallas as pl
from jax.experimental.pallas import tpu as pltpu

F32 = jnp.float32
BF16 = jnp.bfloat16
SDS = jax.ShapeDtypeStruct

N_DEV = 8
S = 2048
D = 1024
HEAD_DIM = 64
ROT_DIM = 16
ROPE_THETA = 500000.0
ATT_W = 1024
SSM_W = 512
SSM_GROUPS = 32
SSM_GROUP = 16
SSM_STATE = 64
N_CPLX = SSM_GROUPS * SSM_STATE
POOL_W = 2048
POOL_GROUP = 512
EVEN_IN = 5120
EVEN_OUT = 1536
ODD_IN = 4096
RMS_EPS = 1e-6
LANES = 128
VMEM_LIMIT = 48 * 1024 * 1024

ADAM_LR = 0.001
ADAM_B1 = 0.9
ADAM_B2 = 0.999
ADAM_EPS = 1e-08
ADAM_WD = 0.01
ADAM_STEP = 10

MESH_ID = pl.DeviceIdType.MESH
NN = (((1,), (0,)), ((), ()))
NT = (((1,), (1,)), ((), ()))
TN = (((0,), (0,)), ((), ()))
_DN = {"nn": NN, "nt": NT, "tn": TN}


def _cparams(sem):
    return pltpu.CompilerParams(dimension_semantics=sem, vmem_limit_bytes=VMEM_LIMIT)


def _in_hbm(arrs):
    return [pltpu.with_memory_space_constraint(a, pltpu.HBM) for a in arrs]


MM_TILES = (1024, 768, 512)


def _tile(dim):
    return next((t for t in MM_TILES if dim % t == 0), dim)


BLOCK_PAIR = 2
MAX_WHOLE_K = 2048


def _mm(a, b, mode, out_dtype, b_blocks=False, out_blocks=False, after=()):
    if b_blocks:
        nblk, rows, cb = b.shape
        b2_shape = (rows, nblk * cb)
    else:
        b2_shape = b.shape
    if mode == "nn":
        (m, k), n = a.shape, b2_shape[1]
    elif mode == "nt":
        (m, k), n = a.shape, b2_shape[0]
    else:
        (k, m), n = a.shape, b2_shape[1]
    tm, tn, tk = _tile(m), _tile(n), _tile(k)
    if k <= MAX_WHOLE_K:
        tk = k
    if b_blocks and mode == "nn":
        tn = BLOCK_PAIR * cb
        if tn <= MM_TILES[0]:
            tm = m
    if b_blocks and mode == "nt":
        tk = BLOCK_PAIR * cb
    if out_blocks:
        cb = n // N_DEV
        tn = BLOCK_PAIR * cb
        tk = k
    nk = k // tk

    def body(a_ref, b_ref, *rest):
        o_ref, acc_ref = rest[-2:]
        kk = pl.program_id(2)
        bv = jnp.concatenate([b_ref[p] for p in range(BLOCK_PAIR)], axis=1) if b_blocks else b_ref[...]
        part = lax.dot_general(a_ref[...].astype(BF16), bv.astype(BF16), _DN[mode], preferred_element_type=F32)

        def write(res):
            if out_blocks:
                for p in range(BLOCK_PAIR):
                    o_ref[p] = res[:, p * cb:(p + 1) * cb].astype(o_ref.dtype)
            else:
                o_ref[...] = res.astype(o_ref.dtype)

        if nk == 1:
            write(part)
            return

        @pl.when(kk == 0)
        def _():
            acc_ref[...] = part

        @pl.when((kk > 0) & (kk < nk - 1))
        def _():
            acc_ref[...] += part

        @pl.when(kk == nk - 1)
        def _():
            write(acc_ref[...] + part)

    if mode == "nn":
        a_spec = pl.BlockSpec((tm, tk), lambda i, j, kk: (i, kk))
        b_spec = pl.BlockSpec((tk, tn), lambda i, j, kk: (kk, j))
    elif mode == "nt":
        a_spec = pl.BlockSpec((tm, tk), lambda i, j, kk: (i, kk))
        b_spec = pl.BlockSpec((tn, tk), lambda i, j, kk: (j, kk))
    else:
        a_spec = pl.BlockSpec((tk, tm), lambda i, j, kk: (kk, i))
        b_spec = pl.BlockSpec((tk, tn), lambda i, j, kk: (kk, j))
    if b_blocks and mode == "nn":
        b_spec = pl.BlockSpec((BLOCK_PAIR, tk, cb), lambda i, j, kk: (j, kk, 0))
    if b_blocks and mode == "nt":
        b_spec = pl.BlockSpec((BLOCK_PAIR, tn, cb), lambda i, j, kk: (kk, j, 0))
    out_spec = pl.BlockSpec((tm, tn), lambda i, j, kk: (i, j))
    out_shape = SDS((m, n), out_dtype)
    if out_blocks:
        out_spec = pl.BlockSpec((BLOCK_PAIR, tm, cb), lambda i, j, kk: (j, i, 0))
        out_shape = SDS((N_DEV, m, cb), out_dtype)
    return pl.pallas_call(
        body, name=f"mm_{mode}_{m}x{k}x{n}",
        grid=(m // tm, n // tn, nk),
        in_specs=[a_spec, b_spec] + [pl.BlockSpec(memory_space=pl.ANY)] * len(after),
        out_specs=out_spec,
        out_shape=out_shape,
        scratch_shapes=[pltpu.VMEM((tm, tn) if nk > 1 else (8, LANES), F32)],
        compiler_params=_cparams(("parallel", "parallel", "arbitrary")),
    )(a, b, *after)


def _gmm(a, b, mode, out_dtype, tm=S):
    ng, gw = POOL_W // POOL_GROUP, POOL_GROUP
    ns = S // tm
    if mode in ("nn", "nt"):
        def body(a_ref, b_ref, o_ref):
            o_ref[...] = lax.dot_general(a_ref[...].astype(BF16), b_ref[...].astype(BF16), _DN[mode],
                                         preferred_element_type=F32).astype(o_ref.dtype)

        return pl.pallas_call(
            body, name=f"gmm_{mode}", grid=(ng, ns),
            in_specs=[pl.BlockSpec((tm, gw), lambda g, i: (i, g)),
                      pl.BlockSpec((None, gw, gw), lambda g, i: (g, 0, 0))],
            out_specs=pl.BlockSpec((tm, gw), lambda g, i: (i, g)),
            out_shape=SDS((S, POOL_W), out_dtype),
            compiler_params=_cparams(("parallel", "parallel")),
        )(a, b)

    def body_tn(a_ref, b_ref, o_ref, acc_ref):
        i = pl.program_id(1)

        @pl.when(i == 0)
        def _():
            acc_ref[...] = jnp.zeros_like(acc_ref)

        acc_ref[...] += lax.dot_general(a_ref[...].astype(BF16), b_ref[...].astype(BF16), TN,
                                        preferred_element_type=F32)

        @pl.when(i == ns - 1)
        def _():
            o_ref[...] = acc_ref[...].astype(o_ref.dtype)

    return pl.pallas_call(
        body_tn, name="gmm_tn", grid=(ng, ns),
        in_specs=[pl.BlockSpec((tm, gw), lambda g, i: (i, g)),
                  pl.BlockSpec((tm, gw), lambda g, i: (i, g))],
        out_specs=pl.BlockSpec((None, gw, gw), lambda g, i: (g, 0, 0)),
        out_shape=SDS((ng, gw, gw), out_dtype),
        scratch_shapes=[pltpu.VMEM((gw, gw), F32)],
        compiler_params=_cparams(("parallel", "arbitrary")),
    )(a, b)


def _rowwise(fn, inputs, out_defs, acc_defs=(), tm=512, name=None, after=()):
    n_in, n_out, n_acc = len(inputs), len(out_defs), len(acc_defs)
    n_after = len(after)
    in_specs, args = [], []
    for arr, width, cb in inputs:
        if arr.shape[0] != S:
            in_specs.append(pl.BlockSpec((arr.shape[0], width), lambda i, cb=cb: (0, cb)))
        else:
            in_specs.append(pl.BlockSpec((tm, width), lambda i, cb=cb: (i, cb)))
        args.append(arr)
    out_defs = [d if len(d) == 4 else (d[0], d[1], d[0], 0) for d in out_defs]
    out_shape = [SDS((S, ww), dt) for _, dt, ww, _ in out_defs] + [SDS((1, w), F32) for w in acc_defs]
    out_specs = ([pl.BlockSpec((tm, w), lambda i, cb=cb: (i, cb)) for w, _, _, cb in out_defs]
                 + [pl.BlockSpec((1, w), lambda i: (0, 0)) for w in acc_defs])

    def kern(*refs):
        vals = [r[...] for r in refs[:n_in]]
        outs, accs = fn(*vals)
        out_refs = refs[n_in + n_after:]
        for r, v in zip(out_refs[:n_out], outs):
            r[...] = v.astype(r.dtype)
        if n_acc:
            acc_refs = out_refs[n_out:]

            @pl.when(pl.program_id(0) == 0)
            def _():
                for r in acc_refs:
                    r[...] = jnp.zeros_like(r)

            for r, v in zip(acc_refs, accs):
                r[...] += jnp.sum(v, axis=0, keepdims=True)

    res = pl.pallas_call(
        kern, name=name, grid=(S // tm,), in_specs=in_specs + [pl.BlockSpec(memory_space=pl.ANY)] * n_after,
        out_specs=out_specs, out_shape=out_shape, compiler_params=_cparams(("arbitrary",)),
    )(*args, *after)
    return res


def _sigmoid(x):
    return 1.0 / (1.0 + jnp.exp(-x))


def _silu_and_grad(x):
    s = _sigmoid(x)
    return x * s, s * (1.0 + x * (1.0 - s))


_GELU_K = math.sqrt(2.0 / math.pi)
_GELU_C = 0.044715


def _gelu_and_grad(x):
    t = jnp.tanh(_GELU_K * (x + _GELU_C * (x * x * x)))
    cdf = 0.5 * (1.0 + t)
    grad = cdf + 0.5 * x * (1.0 - t * t) * (_GELU_K * (1.0 + 3.0 * _GELU_C * x * x))
    return x * cdf, grad


def _rms(xv, gain):
    r = lax.rsqrt(jnp.mean(xv * xv, axis=-1, keepdims=True) + RMS_EPS)
    return xv * r * gain


def _rms_bwd(dout, xv, gain):
    r = lax.rsqrt(jnp.mean(xv * xv, axis=-1, keepdims=True) + RMS_EPS)
    xhat = xv * r
    dxhat = dout * gain
    dx = r * (dxhat - xhat * jnp.mean(dxhat * xhat, axis=-1, keepdims=True))
    return dx, dout * xhat


def _norm_fwd(x, gain):
    (h,) = _rowwise(lambda xv, g: ((_rms(xv, g),), ()), [(x, D, 0), (gain, D, 0)], [(D, BF16)], name="norm_fwd")
    return h


def _post_fwd(x, y, gain, next_gain):
    def fn(xv, yv, g, gn):
        out = xv + _rms(yv, g)
        return (out, _rms(out, gn)), ()

    return _rowwise(fn, [(x, D, 0), (y, D, 0), (gain, D, 0), (next_gain, D, 0)], [(D, F32), (D, BF16)],
                    name="post_fwd")


def _post_fwd_loss(x, y, gain, tgt):
    def fn(xv, yv, g, tv):
        e = xv + _rms(yv, g) - tv
        return (e * (1.0 / D),), (e * e,)

    return _rowwise(fn, [(x, D, 0), (y, D, 0), (gain, D, 0), (tgt, D, 0)], [(D, F32)], [D], name="post_fwd_loss")


def _post_bwd(g, y, gain):
    def fn(gv, yv, gn):
        dx, dg = _rms_bwd(gv, yv, gn)
        return (dx,), (dg,)

    return _rowwise(fn, [(g, D, 0), (y, D, 0), (gain, D, 0)], [(D, BF16)], [D], name="post_bwd")


def _pre_bwd(g, dh, x, gain):
    def fn(gv, dhv, xv, gn):
        dx, dg = _rms_bwd(dhv, xv, gn)
        return (gv + dx,), (dg,)

    return _rowwise(fn, [(g, D, 0), (dh, D, 0), (x, D, 0), (gain, D, 0)], [(D, F32)], [D], name="pre_bwd")


def _pool(u_arr, col_block, transpose, out_dtype, into=None, tc=256):
    n_t = POOL_W // tc
    per_group = POOL_GROUP // tc

    def body(u_ref, *rest):
        o_ref = rest[-1]
        grp = pl.program_id(0) // per_group
        t = lax.broadcasted_iota(jnp.int32, (S, 1), 0)
        for g in range(POOL_W // POOL_GROUP):
            @pl.when(grp == g)
            def _(g=g):
                xv = u_ref[...]
                cnt = jnp.minimum(t + 1, 2 << g).astype(F32)
                cur = xv / cnt if transpose else xv
                for k in (1, 2, 4, 8)[:g + 1]:
                    if transpose:
                        cur = cur + jnp.where(t < S - k, pltpu.roll(cur, S - k, 0), 0.0)
                    else:
                        cur = cur + jnp.where(t >= k, pltpu.roll(cur, k, 0), 0.0)
                res = cur - xv if transpose else cur / cnt - xv
                o_ref[...] = res.astype(o_ref.dtype)

    in_specs = [pl.BlockSpec((S, tc), lambda c: (0, col_block * n_t + c))]
    args = [u_arr]
    if into is not None:
        in_specs.append(pl.BlockSpec(memory_space=pl.ANY))
        args.append(into)
    return pl.pallas_call(
        body, name="pool_bwd" if transpose else "pool_fwd", grid=(n_t,),
        in_specs=in_specs,
        out_specs=pl.BlockSpec((S, tc), lambda c: (0, c)),
        out_shape=SDS((S, POOL_W) if into is None else into.shape, out_dtype),
        input_output_aliases={} if into is None else {1: 0},
        compiler_params=_cparams(("parallel",)),
    )(*args)


def _rope_tables(zero):
    pos = jnp.arange(S, dtype=jnp.int32).astype(F32) + zero
    inv_freq = ROPE_THETA ** (-jnp.arange(0, ROT_DIM, 2, dtype=F32) / ROT_DIM)
    ang = pos[:, None] * inv_freq[None, :]
    cos8, sin8 = jnp.cos(ang), jnp.sin(ang)
    half = ROT_DIM // 2
    zeros = jnp.zeros((S, HEAD_DIM - ROT_DIM), F32)
    cos = jnp.concatenate([cos8, cos8, jnp.ones((S, HEAD_DIM - ROT_DIM), F32)], axis=1)
    sin = jnp.concatenate([-sin8, sin8, zeros], axis=1)
    rep = LANES // HEAD_DIM
    lane = jnp.arange(LANES)
    dim = lane % HEAD_DIM
    partner = jnp.where(dim < half, lane + half, jnp.where(dim < ROT_DIM, lane - half, -1))
    swap = (lane[:, None] == partner[None, :]).astype(BF16)
    return jnp.tile(cos, (1, rep)), jnp.tile(sin, (1, rep)), swap


def _rotate(xv, cos, sin, swap, transpose):
    rep = xv.shape[1] // LANES
    wide = lambda tab: jnp.concatenate([tab] * rep, axis=1)
    xb = xv.astype(BF16)
    partner = jnp.concatenate([lax.dot_general(xb[:, t * LANES:(t + 1) * LANES], swap, NN, preferred_element_type=F32)
                               for t in range(rep)], axis=1)
    mixed = partner * wide(sin)
    return xv * wide(cos) - mixed if transpose else xv * wide(cos) + mixed


def _qkv_prep(proj, tables):
    cos, sin, swap = tables

    def fn(x, c, s, sw):
        rot = _rotate(x[:, :2 * ATT_W], c, s, sw, False)
        return (jnp.concatenate([(rot[:, :ATT_W] * HEAD_DIM ** -0.5).astype(BF16), rot[:, ATT_W:].astype(BF16),
                                 x[:, 2 * ATT_W:].astype(BF16)], axis=1),), ()

    (qkv,) = _rowwise(fn, [(proj, 3 * ATT_W, 0), (cos, LANES, 0), (sin, LANES, 0), (swap, LANES, 0)],
                      [(3 * ATT_W, BF16)], name="qkv_prep")
    return qkv


ATT_T = 512


def _multiplicity(delta):
    ok = delta >= 0
    near = jnp.where(ok & (delta <= 128), 1.0, 0.0)
    mid = jnp.where(ok & (delta <= 512) & ((delta & 3) == 0), 1.0, 0.0)
    far = jnp.where(ok & ((delta & 15) == 0), 1.0, 0.0)
    return near + mid + far


def _attention_bias(zero):
    t = ATT_T
    pos = jnp.arange(t, dtype=jnp.int32) + jnp.asarray(zero).astype(jnp.int32)
    delta = jnp.arange(S // t, dtype=jnp.int32)[:, None, None] * t + pos[None, :, None] - pos[None, None, :]
    mult = _multiplicity(delta)
    return jnp.where(mult > 0.0, jnp.log(jnp.maximum(mult, 1.0)), -1e30).astype(F32)


def _head_split(v, first):
    zero = jnp.zeros_like(v)
    return [jnp.where(first, v, zero), jnp.where(first, zero, v)]


def _flash_fwd(qkv, bias):
    t = ATT_T
    n_hp = ATT_W // LANES

    def body(q_ref, k_ref, v_ref, b_ref, o_ref, lse_ref):
        i = pl.program_id(1)
        first = lax.broadcasted_iota(jnp.int32, (1, LANES), 1) < HEAD_DIM
        qs = _head_split(q_ref[...], first)

        def kv_step(j, carry):
            m0, l0, m1, l1, acc = carry
            off = pl.multiple_of(j * t, t)
            kb = k_ref[pl.ds(off, t), :]
            vs = _head_split(v_ref[pl.ds(off, t), :], first)
            bias_t = b_ref[i - j]
            new = []
            pv = None
            for h, (m_prev, l_prev) in enumerate(((m0, l0), (m1, l1))):
                s = lax.dot_general(qs[h], kb, NT, preferred_element_type=F32) + bias_t
                m_new = jnp.maximum(m_prev, jnp.max(s, axis=1, keepdims=True))
                p = jnp.exp(s - m_new)
                alpha = jnp.exp(m_prev - m_new)
                l_new = alpha * l_prev + jnp.sum(p, axis=1, keepdims=True)
                d = lax.dot_general(p.astype(BF16), vs[h], NN, preferred_element_type=F32)
                pv = d if pv is None else pv + d
                new.append((m_new, l_new, alpha))
            acc = acc * jnp.where(first, new[0][2], new[1][2]) + pv
            return new[0][0], new[0][1], new[1][0], new[1][1], acc

        neg = jnp.full((t, 1), -1e30, F32)
        zero = jnp.zeros((t, 1), F32)
        m0, l0, m1, l1, acc = lax.fori_loop(0, i + 1, kv_step, (neg, zero, neg, zero, jnp.zeros((t, LANES), F32)))
        o_ref[...] = acc * jnp.where(first, 1.0 / l0, 1.0 / l1)
        lse_ref[...] = jnp.where(first, m0 + jnp.log(l0), m1 + jnp.log(l1))

    blk = pl.BlockSpec((t, LANES), lambda hp, i: (i, hp))
    k_full = pl.BlockSpec((S, LANES), lambda hp, i: (0, n_hp + hp))
    v_full = pl.BlockSpec((S, LANES), lambda hp, i: (0, 2 * n_hp + hp))
    return pl.pallas_call(
        body, name="flash_fwd", grid=(n_hp, S // t),
        in_specs=[blk, k_full, v_full, pl.BlockSpec((S // t, t, t), lambda hp, i: (0, 0, 0))], out_specs=[blk, blk],
        out_shape=[SDS((S, ATT_W), F32), SDS((S, ATT_W), F32)],
        compiler_params=_cparams(("parallel", "arbitrary")),
    )(qkv, qkv, qkv, bias)


def _flash_bwd(qkv, o, do, lse, bias, after=()):
    t = ATT_T
    n_hp = ATT_W // LANES
    n_t = S // t

    def body(q_ref, k_ref, v_ref, o_ref, do_ref, lse_ref, b_ref, *rest):
        dq_ref, dk_ref, dv_ref = rest[-3:]
        j = pl.program_id(1)
        first = lax.broadcasted_iota(jnp.int32, (1, LANES), 1) < HEAD_DIM

        @pl.when(j == 0)
        def _():
            dq_ref[...] = jnp.zeros_like(dq_ref)

        kb = k_ref[...]
        vb = v_ref[...]
        ks = _head_split(kb, first)

        def q_step(i, carry):
            dk_acc, dv_acc = carry
            rows = pl.ds(pl.multiple_of(i * t, t), t)
            qs = _head_split(q_ref[rows, :], first)
            dob = do_ref[rows, :]
            prod = dob * o_ref[rows, :]
            d_all = jnp.sum(prod, axis=1, keepdims=True)
            d0 = jnp.sum(jnp.where(first, prod, 0.0), axis=1, keepdims=True)
            lse_b = lse_ref[rows, :]
            lse0 = jnp.max(jnp.where(first, lse_b, -jnp.inf), axis=1, keepdims=True)
            lse1 = jnp.max(jnp.where(first, -jnp.inf, lse_b), axis=1, keepdims=True)
            dos = _head_split(dob.astype(BF16), first)
            bias_t = b_ref[i - j]
            dq_t = jnp.zeros((t, LANES), F32)
            for h, (lse_h, d_h) in enumerate(((lse0, d0), (lse1, d_all - d0))):
                s = lax.dot_general(qs[h], kb, NT, preferred_element_type=F32)
                p = jnp.exp(s + (bias_t - lse_h))
                dp = lax.dot_general(dos[h], vb, NT, preferred_element_type=F32)
                ds = (p * (dp - d_h)).astype(BF16)
                dv_acc = dv_acc + lax.dot_general(p.astype(BF16), dos[h], TN, preferred_element_type=F32)
                dk_acc = dk_acc + lax.dot_general(ds, qs[h], TN, preferred_element_type=F32)
                dq_t = dq_t + lax.dot_general(ds, ks[h], NN, preferred_element_type=F32)
            dq_ref[rows, :] += dq_t
            return dk_acc, dv_acc

        zero = jnp.zeros((t, LANES), F32)
        dk_acc, dv_acc = lax.fori_loop(j, n_t, q_step, (zero, zero))
        dk_ref[...] = dk_acc
        dv_ref[...] = dv_acc

    blk = pl.BlockSpec((t, LANES), lambda hp, j: (j, hp))
    full = pl.BlockSpec((S, LANES), lambda hp, j: (0, hp))
    k_blk = pl.BlockSpec((t, LANES), lambda hp, j: (j, n_hp + hp))
    v_blk = pl.BlockSpec((t, LANES), lambda hp, j: (j, 2 * n_hp + hp))
    return pl.pallas_call(
        body, name="flash_bwd", grid=(n_hp, n_t),
        in_specs=([full, k_blk, v_blk, full, full, full, pl.BlockSpec((n_t, t, t), lambda hp, j: (0, 0, 0))]
                  + [pl.BlockSpec(memory_space=pl.ANY)] * len(after)),
        out_specs=[full, blk, blk],
        out_shape=[SDS((S, ATT_W), F32)] * 3,
        compiler_params=_cparams(("parallel", "arbitrary")),
    )(qkv, qkv, qkv, o, do, lse, bias, *after)


SCAN_T = 256
SCAN_GROUP = 8
SCAN_STEPS = (1, 2, 4)
ST_ROWS = 2 * N_CPLX // LANES
HALF = ST_ROWS // 2


def _scan_tables(lam_t):
    lam = lax.complex(lam_t[:HALF].reshape(N_CPLX), lam_t[HALF:].reshape(N_CPLX))
    pows = [lam]
    for _ in range(SCAN_GROUP - 1):
        pows.append(pows[-1] * lam)
    pows = jnp.stack(pows)
    sub = jnp.arange(SCAN_GROUP)[:, None]
    fwd = [jnp.where(sub >= k, pows[k - 1][None, :], 0.0) for k in SCAN_STEPS] + [pows]
    conj = jnp.conj(pows)
    bwd = [jnp.where(sub <= SCAN_GROUP - 1 - k, conj[k - 1][None, :], 0.0) for k in SCAN_STEPS] + [conj[::-1]]

    def pack(tabs):
        return jnp.stack([jnp.concatenate([jnp.real(t), jnp.imag(t)], axis=1) for t in tabs]).astype(F32)

    return pack(fwd), pack(bwd)


def _cmul_add(xr, xi, lr, li, sr, si):
    return xr + lr * sr - li * si, xi + lr * si + li * sr


def _group_scan(xr, xi, tab_ref, cr, ci, reverse):
    for j, k in enumerate(SCAN_STEPS):
        shift = SCAN_GROUP - k if reverse else k
        xr, xi = _cmul_add(xr, xi, tab_ref[j, :, :N_CPLX], tab_ref[j, :, N_CPLX:],
                           pltpu.roll(xr, shift, 0), pltpu.roll(xi, shift, 0))
    return _cmul_add(xr, xi, tab_ref[3, :, :N_CPLX], tab_ref[3, :, N_CPLX:],
                     jnp.broadcast_to(cr, (SCAN_GROUP, N_CPLX)), jnp.broadcast_to(ci, (SCAN_GROUP, N_CPLX)))


SSM_SUPER = 4
SB_ROWS = SSM_W // SSM_SUPER
SB_COLS = N_CPLX // SSM_SUPER


def _super_blocks():
    return [(slice(b * SB_ROWS, (b + 1) * SB_ROWS), slice(h * SB_COLS, (h + 1) * SB_COLS),
             slice(h * N_CPLX + b * SB_COLS, h * N_CPLX + (b + 1) * SB_COLS))
            for b in range(SSM_SUPER) for h in range(2)]


def _dot16(a, b, dims):
    return lax.dot_general(a.astype(BF16), b.astype(BF16), dims, preferred_element_type=F32)


def _s5_fwd(tab, u_arr, u_cols, w_b, w_ct):
    nc = N_CPLX

    def body(tab_ref, u_ref, wb_ref, wct_ref, st_ref, y_ref, carry, bu_scr):
        @pl.when(pl.program_id(0) == 0)
        def _():
            carry[...] = jnp.zeros_like(carry)

        for rows_b, cols_c, cols_s in _super_blocks():
            bu_scr[:, cols_s] = _dot16(u_ref[:, rows_b], wb_ref[rows_b, cols_c], NN)

        def group(a, c):
            rows = pl.ds(pl.multiple_of(a * SCAN_GROUP, SCAN_GROUP), SCAN_GROUP)
            xr, xi = _group_scan(bu_scr[rows, :nc], bu_scr[rows, nc:], tab_ref, c[0], c[1], False)
            st_ref[rows, :nc] = xr
            st_ref[rows, nc:] = xi
            return xr[SCAN_GROUP - 1:SCAN_GROUP, :], xi[SCAN_GROUP - 1:SCAN_GROUP, :]

        cr, ci = lax.fori_loop(0, SCAN_T // SCAN_GROUP, group, (carry[:, :nc], carry[:, nc:]), unroll=2)
        carry[:, :nc] = cr
        carry[:, nc:] = ci

        for b in range(SSM_SUPER):
            (rows_b, cols_re, st_re), (_, cols_im, st_im) = _super_blocks()[2 * b:2 * b + 2]
            y_ref[:, rows_b] = (_dot16(st_ref[:, st_re], wct_ref[rows_b, cols_re], NT)
                                + _dot16(st_ref[:, st_im], wct_ref[rows_b, cols_im], NT))

    const = lambda shape: pl.BlockSpec(shape, lambda i: (0,) * len(shape))
    return pl.pallas_call(
        body, name="s5_fwd", grid=(S // SCAN_T,),
        in_specs=[const((4, SCAN_GROUP, 2 * nc)), pl.BlockSpec((SCAN_T, SSM_W), lambda i: (i, u_cols[0] // SSM_W)),
                  const((SSM_W, 2 * SB_COLS)), const((SSM_W, 2 * SB_COLS))],
        out_specs=[pl.BlockSpec((SCAN_T, 2 * nc), lambda i: (i, 0)), pl.BlockSpec((SCAN_T, SSM_W), lambda i: (i, 0))],
        out_shape=[SDS((S, 2 * nc), F32), SDS((S, SSM_W), F32)],
        scratch_shapes=[pltpu.VMEM((1, 2 * nc), F32), pltpu.VMEM((SCAN_T, 2 * nc), F32)],
        compiler_params=_cparams(("arbitrary",)),
    )(tab, u_arr, w_b, w_ct)


def _s5_bwd(tab, dy, states, u_arr, u_cols, w_b, w_ct):
    n_blk = S // SCAN_T
    nc = N_CPLX

    def body(tab_ref, dy_ref, x_ref, u_ref, wb_ref, wct_ref, du_ref, dlam_ref, dwb_ref, dwct_ref,
             carry, acc, d_scr, g_ref):
        i = pl.program_id(0)

        @pl.when(i == 0)
        def _():
            carry[...] = jnp.zeros_like(carry)
            acc[...] = jnp.zeros_like(acc)
            dwb_ref[...] = jnp.zeros_like(dwb_ref)
            dwct_ref[...] = jnp.zeros_like(dwct_ref)

        for rows_b, cols_c, cols_s in _super_blocks():
            d_scr[:, cols_s] = _dot16(dy_ref[:, rows_b], wct_ref[rows_b, cols_c], NN)

        last_row = lax.broadcasted_iota(jnp.int32, (SCAN_GROUP, 1), 0) == SCAN_GROUP - 1
        d_ref = d_scr

        def group(j, c):
            cr, ci = c
            rows = pl.ds(pl.multiple_of((SCAN_T // SCAN_GROUP - 1 - j) * SCAN_GROUP, SCAN_GROUP), SCAN_GROUP)
            gr, gi = _group_scan(d_ref[rows, :nc], d_ref[rows, nc:], tab_ref, cr, ci, True)
            g_ref[rows, :nc] = gr
            g_ref[rows, nc:] = gi
            nr = jnp.where(last_row, jnp.broadcast_to(cr, (SCAN_GROUP, nc)), pltpu.roll(gr, SCAN_GROUP - 1, 0))
            ni = jnp.where(last_row, jnp.broadcast_to(ci, (SCAN_GROUP, nc)), pltpu.roll(gi, SCAN_GROUP - 1, 0))
            sr, si = x_ref[rows, :nc], x_ref[rows, nc:]
            acc[:, :nc] += nr * sr + ni * si
            acc[:, nc:] += ni * sr - nr * si
            return gr[0:1, :], gi[0:1, :]

        cr, ci = lax.fori_loop(0, SCAN_T // SCAN_GROUP, group, (carry[:, :nc], carry[:, nc:]), unroll=2)
        carry[:, :nc] = cr
        carry[:, nc:] = ci

        for b in range(SSM_SUPER):
            (rows_b, cols_re, st_re), (_, cols_im, st_im) = _super_blocks()[2 * b:2 * b + 2]
            du_ref[:, rows_b] = (_dot16(g_ref[:, st_re], wb_ref[rows_b, cols_re], NT)
                                 + _dot16(g_ref[:, st_im], wb_ref[rows_b, cols_im], NT))
            for cols_c, cols_s in ((cols_re, st_re), (cols_im, st_im)):
                dwb_ref[rows_b, cols_c] += _dot16(u_ref[:, rows_b], g_ref[:, cols_s], TN)
                dwct_ref[rows_b, cols_c] += _dot16(dy_ref[:, rows_b], x_ref[:, cols_s], TN)

        @pl.when(i == n_blk - 1)
        def _():
            dlam_ref[...] = jnp.sum(acc[...], axis=0, keepdims=True)

    const = lambda shape: pl.BlockSpec(shape, lambda i: (0,) * len(shape))
    rows = lambda width, col_block=0: pl.BlockSpec((SCAN_T, width), lambda i: (n_blk - 1 - i, col_block))
    maps = const((SSM_W, 2 * SB_COLS))
    return pl.pallas_call(
        body, name="s5_bwd", grid=(n_blk,),
        in_specs=[const((4, SCAN_GROUP, 2 * nc)), rows(SSM_W), rows(2 * nc), rows(SSM_W, u_cols[0] // SSM_W), maps, maps],
        out_specs=[rows(SSM_W), const((1, 2 * nc)), maps, maps],
        out_shape=[SDS((S, SSM_W), F32), SDS((1, 2 * nc), F32), SDS((SSM_W, 2 * SB_COLS), F32),
                   SDS((SSM_W, 2 * SB_COLS), F32)],
        scratch_shapes=[pltpu.VMEM((1, 2 * nc), F32), pltpu.VMEM((SCAN_GROUP, 2 * nc), F32),
                        pltpu.VMEM((SCAN_T, 2 * nc), F32), pltpu.VMEM((SCAN_T, 2 * nc), F32)],
        compiler_params=_cparams(("arbitrary",)),
    )(tab, dy, states, u_arr, w_b, w_ct)


def _ssm_prep(a_re, a_im, log_dt, b_re, b_im, c_re, c_im):
    lam = lax.complex(a_re, a_im)
    dt = jnp.exp(log_dt)[:, None]
    lam_bar = jnp.exp(lam * dt)
    b_bar = ((lam_bar - 1.0) / lam)[..., None] * lax.complex(b_re, b_im)
    lam_t = jnp.concatenate([jnp.real(lam_bar).reshape(HALF, LANES), jnp.imag(lam_bar).reshape(HALF, LANES)], axis=0)
    groups_per_super = SSM_GROUPS // SSM_SUPER
    on_diag = ((lax.broadcasted_iota(jnp.int32, (SSM_W, SB_COLS), 0) // SSM_GROUP) % groups_per_super
               == lax.broadcasted_iota(jnp.int32, (SSM_W, SB_COLS), 1) // SSM_STATE)

    def compact(m):
        return jnp.where(on_diag, jnp.tile(m.reshape(SSM_W, SSM_STATE), (1, groups_per_super)), 0.0)

    w_b = jnp.concatenate([compact(jnp.real(b_bar).transpose(0, 2, 1)),
                           compact(jnp.imag(b_bar).transpose(0, 2, 1))], axis=1)
    w_ct = jnp.concatenate([compact(c_re), -compact(c_im)], axis=1)
    return lam_t, w_b, w_ct


U_SSM_COLS = (4 * ATT_W, SSM_W)


def _row(v):
    return v.reshape(1, -1)


def _even_fwd(x, h, tail, pre, post, w_in, late_w, glu_b, ssm_d, prep, tables):
    lam_t, w_b, w_ct = prep
    proj = _mm(h, w_in, "nn", F32, b_blocks=True)
    qkv = _qkv_prep(proj, tables[:3])
    w_out, glu_w = late_w(qkv)
    att, lse = _flash_fwd(qkv, tables[3])
    scan_fwd_tab, scan_bwd_tab = _scan_tables(lam_t)
    states, y = _s5_fwd(scan_fwd_tab, proj, U_SSM_COLS, w_b, w_ct)

    def act1(yv, uv, dv):
        return (_gelu_and_grad(yv + dv * uv)[0],), ()

    (z1,) = _rowwise(act1, [(y, SSM_W, 0), (proj, SSM_W, 8), (ssm_d, SSM_W, 0)], [(SSM_W, F32)], name="ssm_act_fwd")
    lin = _mm(z1, glu_w, "nn", F32)

    def gate(att_v, ga, gs, z1v, linv, bv):
        ssm_out = z1v * _sigmoid(linv + bv)
        return (jnp.concatenate([att_v * _silu_and_grad(ga)[0], ssm_out * _silu_and_grad(gs)[0]], axis=1),), ()

    (merged,) = _rowwise(gate, [(att, ATT_W, 0), (proj, ATT_W, 3), (proj, SSM_W, 9), (z1, SSM_W, 0),
                                (lin, SSM_W, 0), (glu_b, SSM_W, 0)], [(EVEN_OUT, BF16)], name="even_gate_fwd")
    yout = _mm(merged, w_out, "nn", F32)
    saved = (x, h, proj, qkv, att, lse, states, y, z1, lin, merged, yout, w_out, glu_w, scan_bwd_tab)
    return tail(x, yout, post) + (saved,)


def _even_bwd(g, saved, pre, post, w_in, late_w, glu_b, ssm_d, prep, tables, on_w, on_ssm):
    x, h, proj, qkv, att, lse, states, y, z1, lin, merged, yout, w_out, glu_w, scan_bwd_tab = saved
    lam_t, w_b, w_ct = prep
    dyout, dpost = _post_bwd(g, yout, post)
    dmerged = _mm(dyout, w_out, "nt", F32)
    dw_out = _mm(merged, dyout, "tn", BF16)

    def gate_bwd(dm_a, dm_s, att_v, ga, gs, z1v, linv, bv):
        sa, dsa = _silu_and_grad(ga)
        ss, dss = _silu_and_grad(gs)
        sig = _sigmoid(linv + bv)
        ssm_out = z1v * sig
        dssm = dm_s * ss
        dlin = dssm * z1v * sig * (1.0 - sig)
        return (dm_a * sa, dm_a * att_v * dsa, dm_s * ssm_out * dss, dssm * sig, dlin), (dlin,)

    datt, dg_att, dg_ssm, dz1a, dlin, dglu_b = _rowwise(
        gate_bwd, [(dmerged, ATT_W, 0), (dmerged, SSM_W, 2), (att, ATT_W, 0), (proj, ATT_W, 3), (proj, SSM_W, 9),
                   (z1, SSM_W, 0), (lin, SSM_W, 0), (glu_b, SSM_W, 0)],
        [(ATT_W, F32), (ATT_W, BF16), (SSM_W, BF16), (SSM_W, F32), (SSM_W, BF16)], [SSM_W], name="even_gate_bwd")
    dz1b = _mm(dlin, glu_w, "nt", F32)
    dglu_w = _mm(z1, dlin, "tn", BF16)

    def act1_bwd(da, db, yv, uv, dv):
        dpre = (da + db) * _gelu_and_grad(yv + dv * uv)[1]
        return (dpre, dpre * dv), (dpre * uv,)

    sent_late_w = on_w(dict(w_out=dw_out, glu_w=dglu_w))
    dy, du_direct, dd = _rowwise(act1_bwd, [(dz1a, SSM_W, 0), (dz1b, SSM_W, 0), (y, SSM_W, 0), (proj, SSM_W, 8),
                                            (ssm_d, SSM_W, 0)], [(SSM_W, BF16), (SSM_W, F32)], [SSM_W],
                                 name="ssm_act_bwd", after=(sent_late_w,))
    du_state, dlam_row, dw_b, dw_ct = _s5_bwd(scan_bwd_tab, dy, states, proj, U_SSM_COLS, w_b, w_ct)
    dlam = jnp.concatenate([dlam_row[0, :N_CPLX].reshape(HALF, LANES), dlam_row[0, N_CPLX:].reshape(HALF, LANES)],
                           axis=0)
    sent_ssm = on_ssm((dlam, dw_b, dw_ct))
    dq, dk, dv = _flash_bwd(qkv, att, datt, lse, tables[3], after=() if sent_ssm is None else (sent_ssm,))

    def assemble(dqv, dkv, dvv, dga, dua, dub, dgs, c, s, sw):
        rot = _rotate(jnp.concatenate([dqv, dkv], axis=1), c, s, sw, True)
        return (jnp.concatenate([(rot[:, :ATT_W] * HEAD_DIM ** -0.5).astype(BF16), rot[:, ATT_W:].astype(BF16),
                                 dvv.astype(BF16), dga, (dua + dub).astype(BF16), dgs], axis=1),), ()

    (dproj,) = _rowwise(assemble, [(dq, ATT_W, 0), (dk, ATT_W, 0), (dv, ATT_W, 0), (dg_att, ATT_W, 0),
                                   (du_state, SSM_W, 0), (du_direct, SSM_W, 0), (dg_ssm, SSM_W, 0),
                                   (tables[0], LANES, 0), (tables[1], LANES, 0), (tables[2], LANES, 0)],
                        [(EVEN_IN, BF16)], name="dproj_assemble")
    dw_in = _mm(h, dproj, "tn", BF16, out_blocks=True)
    sent = on_w(dict(w_in=dw_in))
    dh = _mm(dproj, w_in, "nt", F32, b_blocks=True, after=(sent,))
    g_prev, dpre = _pre_bwd(g, dh, x, pre)
    return g_prev, dict(pre=dpre, post=dpost, glu_b=dglu_b, ssm_d=dd)


def _odd_fwd(x, h, tail, pre, post, w_in, pool_w, pool_scale, w_out):
    proj = _mm(h, w_in, "nn", F32, b_blocks=True)
    mixed = _pool(proj, 0, False, BF16)
    ylin = _gmm(mixed, pool_w, "nn", F32)

    def gate(yl, gt, sc):
        return (yl * sc * _silu_and_grad(gt)[0],), ()

    (z,) = _rowwise(gate, [(ylin, POOL_W, 0), (proj, POOL_W, 1), (pool_scale, POOL_W, 0)], [(POOL_W, BF16)],
                    name="odd_gate_fwd")
    yout = _mm(z, w_out, "nn", F32)
    return tail(x, yout, post) + ((x, h, proj, mixed, ylin, z, yout),)


def _odd_bwd(g, saved, pre, post, w_in, pool_w, pool_scale, w_out, on_w):
    x, h, proj, mixed, ylin, z, yout = saved
    dyout, dpost = _post_bwd(g, yout, post)
    dz = _mm(dyout, w_out, "nt", F32)
    dw_out = _mm(z, dyout, "tn", BF16)

    def gate_bwd(dzv, yl, gt, sc):
        sg, dsg = _silu_and_grad(gt)
        tt = dzv * sg
        return (tt * sc, dzv * yl * sc * dsg), (tt * yl,)

    dylin, dproj_gate, dscale = _rowwise(gate_bwd, [(dz, POOL_W, 0), (ylin, POOL_W, 0), (proj, POOL_W, 1),
                                                    (pool_scale, POOL_W, 0)],
                                         [(POOL_W, BF16), (POOL_W, BF16, ODD_IN, 1)], [POOL_W], name="odd_gate_bwd")
    dmixed = _gmm(dylin, pool_w, "nt", F32)
    dpool_w = _gmm(mixed, dylin, "tn", BF16)
    dproj = _pool(dmixed, 0, True, BF16, into=dproj_gate)
    dw_in = _mm(h, dproj, "tn", BF16, out_blocks=True)
    sent = on_w(dict(w_in=dw_in, w_out=dw_out, pool_w=dpool_w))
    dh = _mm(dproj, w_in, "nt", F32, b_blocks=True, after=(sent,))
    g_prev, dpre = _pre_bwd(g, dh, x, pre)
    return g_prev, dict(pre=dpre, post=dpost, pool_scale=dscale)


def _my_index():
    return 4 * lax.axis_index("x") + 2 * lax.axis_index("y") + lax.axis_index("c")


HBM_SPEC = pl.BlockSpec(memory_space=pltpu.HBM)
SEM_SPEC = pl.BlockSpec(memory_space=pltpu.SEMAPHORE)
SPLIT_EFFECT = pltpu.SideEffectType.DATAFLOW_SIDE_EFFECTING


def _device_of(j):
    return (j // 4, (j // 2) % 2, j % 2)


def _split_copy(srcs, lands, send_sems, recv_sems, gather, i, j, dst_slot, recv_slot):
    return pltpu.make_async_remote_copy(
        src_ref=srcs[i] if gather else srcs[i].at[j], dst_ref=lands[i].at[dst_slot],
        send_sem=send_sems.at[i * N_DEV + j], recv_sem=recv_sems.at[i * N_DEV + recv_slot],
        device_id=_device_of(j), device_id_type=MESH_ID)


def _own_copy(srcs, lands, send_sems, gather, i, me):
    return pltpu.make_async_copy(srcs[i] if gather else srcs[i].at[me], lands[i].at[me], send_sems.at[i * N_DEV + me])


def _xchg_start(name, srcs, gather, after=()):
    n = len(srcs)
    n_in = n + len(after)

    def body(*refs):
        src_refs = refs[:n]
        send_sems, recv_sems, token = refs[n_in], refs[n_in + 1], refs[-1]
        land_refs = refs[n_in + 2 + n:n_in + 2 + 2 * n]
        me = _my_index()
        for j in range(N_DEV):
            @pl.when(me != j)
            def _(j=j):
                for i in range(n):
                    _split_copy(src_refs, land_refs, send_sems, recv_sems, gather, i, j, me, me).start()
        for i in range(n):
            _own_copy(src_refs, land_refs, send_sems, gather, i, me).start()
        token[...] = jnp.zeros_like(token)

    land_shapes = [((N_DEV,) + a.shape) if gather else a.shape for a in srcs]
    thru = ([pltpu.HBM(a.shape, a.dtype) for a in srcs] + [pltpu.HBM(s, a.dtype) for s, a in zip(land_shapes, srcs)])
    res = pl.pallas_call(
        body, name=name,
        out_shape=(pltpu.SemaphoreType.DMA((n * N_DEV,)), pltpu.SemaphoreType.DMA((n * N_DEV,)), *thru,
                   SDS((8, LANES), F32)),
        in_specs=[HBM_SPEC] * n + [pl.BlockSpec(memory_space=pl.ANY)] * len(after),
        out_specs=(SEM_SPEC, SEM_SPEC, *([HBM_SPEC] * (2 * n)), pl.BlockSpec(memory_space=pltpu.VMEM)),
        input_output_aliases={i: 2 + i for i in range(n)},
        compiler_params=pltpu.CompilerParams(has_side_effects=SPLIT_EFFECT),
    )(*[pltpu.with_memory_space_constraint(a, pltpu.HBM) for a in srcs], *after)
    return res[0], res[1], list(res[2:2 + n]), list(res[2 + n:2 + 2 * n]), res[-1]


def _xchg_wait(name, started, gather, after):
    send_sems, recv_sems, srcs, lands, _ = started
    n = len(srcs)

    def body(*refs):
        src_refs, land_refs = refs[:n], refs[n:2 * n]
        send_r, recv_r = refs[2 * n], refs[2 * n + 1]
        me = _my_index()
        for j in range(N_DEV):
            @pl.when(me != j)
            def _(j=j):
                for i in range(n):
                    _split_copy(src_refs, land_refs, send_r, recv_r, gather, i, j, me, me).wait_send()
                    _split_copy(src_refs, land_refs, send_r, recv_r, gather, i, j, j, j).wait_recv()
        for i in range(n):
            _own_copy(src_refs, land_refs, send_r, gather, i, me).wait()

    thru = [pltpu.HBM(a.shape, a.dtype) for a in list(srcs) + list(lands)]
    res = pl.pallas_call(
        body, name=name, out_shape=tuple(thru),
        in_specs=[HBM_SPEC] * (2 * n) + [SEM_SPEC, SEM_SPEC] + [pl.BlockSpec(memory_space=pl.ANY)] * len(after),
        out_specs=tuple([HBM_SPEC] * (2 * n)),
        input_output_aliases={i: i for i in range(2 * n)},
        compiler_params=pltpu.CompilerParams(has_side_effects=SPLIT_EFFECT),
    )(*srcs, *lands, send_sems, recv_sems, *after)
    return list(res[n:])


def _adam_layer(w, slots, m, v, layer, name, into=None):
    n_l, r, c = w.shape
    ns = slots.shape[0]
    tr = r
    while tr * c * 4 > (1 << 20) and tr % 16 == 0:
        tr //= 2
    assert r % tr == 0

    def body(w_ref, g_ref, m_ref, v_ref, *rest):
        go_ref, d_ref, mo_ref, vo_ref = rest[-4:]
        g = g_ref[0].astype(F32)
        for s in range(1, ns):
            g = g + g_ref[s].astype(F32)
        mn = ADAM_B1 * m_ref[...] + (1.0 - ADAM_B1) * g
        vn = ADAM_B2 * v_ref[...] + (1.0 - ADAM_B2) * (g * g)
        m_hat = mn / (1.0 - ADAM_B1 ** ADAM_STEP)
        v_hat = vn / (1.0 - ADAM_B2 ** ADAM_STEP)
        go_ref[...] = g
        d_ref[...] = -ADAM_LR * (m_hat / (jnp.sqrt(v_hat) + ADAM_EPS) + ADAM_WD * w_ref[...])
        mo_ref[...] = mn
        vo_ref[...] = vn

    blk = pl.BlockSpec((None, tr, c), lambda i: (layer, i, 0))
    earlier = () if into is None else tuple(into)
    return pl.pallas_call(
        body, name=name, grid=(r // tr,),
        in_specs=[blk, pl.BlockSpec((ns, tr, c), lambda i: (0, i, 0)), blk, blk]
        + [pl.BlockSpec(memory_space=pl.ANY)] * len(earlier),
        out_specs=[blk] * 4, out_shape=[SDS((n_l, r, c), F32)] * 4,
        input_output_aliases={4 + q: q for q in range(len(earlier))},
        compiler_params=_cparams(("arbitrary",)),
    )(*_in_hbm((w, slots, m, v)), *earlier)


def _adam(w, gslots, m, v, name):
    r, c = w.shape
    ns = gslots.shape[0]
    tr = r
    while tr * c * 4 > (1 << 20) and tr % 16 == 0:
        tr //= 2
    assert r % tr == 0

    def body(w_ref, g_ref, m_ref, v_ref, go_ref, d_ref, mo_ref, vo_ref):
        g = g_ref[0].astype(F32)
        for s in range(1, ns):
            g = g + g_ref[s].astype(F32)
        wv = w_ref[...]
        mn = ADAM_B1 * m_ref[...] + (1.0 - ADAM_B1) * g
        vn = ADAM_B2 * v_ref[...] + (1.0 - ADAM_B2) * (g * g)
        m_hat = mn / (1.0 - ADAM_B1 ** ADAM_STEP)
        v_hat = vn / (1.0 - ADAM_B2 ** ADAM_STEP)
        go_ref[...] = g
        d_ref[...] = -ADAM_LR * (m_hat / (jnp.sqrt(v_hat) + ADAM_EPS) + ADAM_WD * wv)
        mo_ref[...] = mn
        vo_ref[...] = vn

    blk = pl.BlockSpec((tr, c), lambda i: (i, 0))
    return pl.pallas_call(
        body, name=name, grid=(r // tr,),
        in_specs=[blk, pl.BlockSpec((ns, tr, c), lambda i: (0, i, 0)), blk, blk],
        out_specs=[blk] * 4, out_shape=[SDS((r, c), F32)] * 4,
        compiler_params=_cparams(("parallel",)),
    )(w, gslots, m, v)


def _sum_slots(slots, name):
    ns, r, c = slots.shape

    def body(g_ref, o_ref):
        g = g_ref[0]
        for s in range(1, ns):
            g = g + g_ref[s]
        o_ref[...] = g

    return pl.pallas_call(
        body, name=name, grid=(1,),
        in_specs=[pl.BlockSpec((ns, r, c), lambda i: (0, 0, 0))], out_specs=pl.BlockSpec((r, c), lambda i: (0, 0)),
        out_shape=SDS((r, c), F32), compiler_params=_cparams(("arbitrary",)),
    )(slots)


def _adam_params(params, name):
    n = len(params)

    def body(*refs):
        ins, outs = refs[:5 * n], refs[5 * n:]
        for p in range(n):
            w_ref, m_ref, v_ref, g_first, g_rest = ins[5 * p:5 * p + 5]
            go_ref, d_ref, mo_ref, vo_ref = outs[4 * p:4 * p + 4]
            for part, g_ref in ((slice(0, 1), g_first), (slice(1, w_ref.shape[0]), g_rest)):
                g = g_ref[...]
                mn = ADAM_B1 * m_ref[part] + (1.0 - ADAM_B1) * g
                vn = ADAM_B2 * v_ref[part] + (1.0 - ADAM_B2) * (g * g)
                m_hat = mn / (1.0 - ADAM_B1 ** ADAM_STEP)
                v_hat = vn / (1.0 - ADAM_B2 ** ADAM_STEP)
                go_ref[part] = g
                d_ref[part] = -ADAM_LR * (m_hat / (jnp.sqrt(v_hat) + ADAM_EPS) + ADAM_WD * w_ref[part])
                mo_ref[part] = mn
                vo_ref[part] = vn

    def whole(a):
        return pl.BlockSpec(a.shape, lambda i, nd=a.ndim: (0,) * nd)

    flat = _in_hbm([a for prm in params for a in prm])
    outs = pl.pallas_call(
        body, name=name, grid=(1,),
        in_specs=[whole(a) for a in flat],
        out_specs=[whole(prm[0]) for prm in params for _ in range(4)],
        out_shape=[SDS(prm[0].shape, F32) for prm in params for _ in range(4)],
        compiler_params=_cparams(("arbitrary",)),
    )(*flat)
    return [outs[4 * p:4 * p + 4] for p in range(n)]


SMALL_NAMES = ("pre_norm", "post_norm", "ssm_a_re", "ssm_a_im", "ssm_log_dt", "ssm_b_re", "ssm_b_im", "ssm_c_re",
               "ssm_c_im", "ssm_d", "ssm_glu_b")
SSM_NAMES = ("ssm_a_re", "ssm_a_im", "ssm_log_dt", "ssm_b_re", "ssm_b_im", "ssm_c_re", "ssm_c_im")
WEIGHT_ORDER = ("pre_norm", "post_norm", "even_w_in", "even_w_out", "ssm_a_re", "ssm_a_im", "ssm_log_dt", "ssm_b_re",
                "ssm_b_im", "ssm_c_re", "ssm_c_im", "ssm_d", "ssm_glu_w", "ssm_glu_b", "odd_w_in", "pool_w",
                "pool_scale", "odd_w_out")
PACK_ROWS_ALIGN = 8


def _pack(parts):
    flat = jnp.concatenate([p.reshape(-1).astype(F32) for p in parts])
    rows = -(-flat.shape[0] // (LANES * PACK_ROWS_ALIGN)) * PACK_ROWS_ALIGN
    return jnp.pad(flat, (0, rows * LANES - flat.shape[0])).reshape(rows, LANES)


def _unpack(packed, shapes):
    flat = packed.reshape(-1)
    out, off = [], 0
    for shp in shapes:
        size = math.prod(shp)
        out.append(flat[off:off + size].reshape(shp))
        off += size
    return out


EVEN_SHARDED = ("w_in", "w_out", "glu_w")
ODD_SHARDED = ("w_in", "pool_w", "w_out")
FAMILY = {(0, "w_in"): "even_w_in", (0, "w_out"): "even_w_out", (0, "glu_w"): "ssm_glu_w",
          (1, "w_in"): "odd_w_in", (1, "pool_w"): "pool_w", (1, "w_out"): "odd_w_out"}


def _sharded_keys(layer):
    return EVEN_SHARDED if layer % 2 == 0 else ODD_SHARDED


def _local_step(x, tgt, small, get_weights, on_w, on_ssm, on_grads, zero=0.0):
    tables = _rope_tables(zero) + (_attention_bias(zero),)
    preps, prep_vjps = [], []
    for i in range(2):
        out, vjp = jax.vjp(_ssm_prep, small["ssm_a_re"][i] + zero, small["ssm_a_im"][i], small["ssm_log_dt"][i],
                           small["ssm_b_re"][i], small["ssm_b_im"][i], small["ssm_c_re"][i], small["ssm_c_im"][i])
        preps.append(out)
        prep_vjps.append(vjp)

    def layer_args(layer, wts):
        i = layer // 2
        pre, post = _row(small["pre_norm"][layer]) + wts.get("token", 0.0), _row(small["post_norm"][layer])
        if layer % 2 == 0:
            return (pre, post, wts["w_in"], wts["late"], _row(small["ssm_glu_b"][i]), _row(small["ssm_d"][i]),
                    preps[i], tables)
        return (pre, post, wts["w_in"], wts["pool_w"], _row(wts["pool_scale"]), wts["w_out"])

    saved, args = [], []
    cur = x
    for layer in range(4):
        after = (cur,) if layer else (cur, tables[0], tables[3], preps[0][1], preps[0][2], preps[1][1], preps[1][2])
        args.append(layer_args(layer, get_weights(layer, after)))
        if layer == 0:
            h = _norm_fwd(cur, args[0][0])
        if layer < 3:
            def tail(xv, yv, post, next_gain=_row(small["pre_norm"][layer + 1])):
                return tuple(_post_fwd(xv, yv, post, next_gain))
        else:
            def tail(xv, yv, post):
                return tuple(_post_fwd_loss(xv, yv, post, tgt))
        cur, h, sv = (_even_fwd if layer % 2 == 0 else _odd_fwd)(cur, h, tail, *args[layer])
        saved.append(sv)
    g, sq = cur, h
    loss = 0.5 * jnp.sum(sq) / D

    lg = [None] * 4
    token = jnp.zeros((), F32)
    for layer in reversed(range(4)):
        largs = list(args[layer])
        largs[1] = largs[1] + token
        hooks = dict(on_w=functools.partial(on_w, layer))
        ssm_grads = []
        if layer % 2 == 0:
            def ssm_hook(cotangents, layer=layer):
                ssm_grads.append(prep_vjps[layer // 2](cotangents))
                return on_ssm(layer, ssm_grads[0])

            hooks["on_ssm"] = ssm_hook
        g, lg[layer] = (_even_bwd if layer % 2 == 0 else _odd_bwd)(g, saved[layer], *largs, **hooks)
        if ssm_grads:
            lg[layer]["ssm"] = ssm_grads[0]
        token = on_grads(layer, lg[layer])
    return loss, g, token


def _to_slots(key, gfull):
    if key == "w_in":
        return gfull
    if key in ("w_out", "glu_w"):
        rr, nn = gfull.shape
        return gfull.reshape(N_DEV, rr // N_DEV, nn)
    assert key == "pool_w"
    gg, rr, nn = gfull.shape
    return gfull.reshape(gg, N_DEV, rr // N_DEV, nn).transpose(1, 0, 2, 3)


def _from_gathered(key, gat):
    if key == "w_in":
        return gat
    if key in ("w_out", "glu_w"):
        _, rr, nn = gat.shape
        return gat.reshape(N_DEV * rr, nn)
    assert key == "pool_w"
    _, gg, rr, nn = gat.shape
    return gat.transpose(1, 0, 2, 3).reshape(gg, N_DEV * rr, nn)


def kernel(x, pre_norm, post_norm, even_w_in, even_w_out, ssm_a_re, ssm_a_im, ssm_log_dt, ssm_b_re, ssm_b_im, ssm_c_re, ssm_c_im, ssm_d, ssm_glu_w, ssm_glu_b, odd_w_in, pool_w, pool_scale, odd_w_out, loss_target, m_pre_norm, m_post_norm, m_even_w_in, m_even_w_out, m_ssm_a_re, m_ssm_a_im, m_ssm_log_dt, m_ssm_b_re, m_ssm_b_im, m_ssm_c_re, m_ssm_c_im, m_ssm_d, m_ssm_glu_w, m_ssm_glu_b, m_odd_w_in, m_pool_w, m_pool_scale, m_odd_w_out, v_pre_norm, v_post_norm, v_even_w_in, v_even_w_out, v_ssm_a_re, v_ssm_a_im, v_ssm_log_dt, v_ssm_b_re, v_ssm_b_im, v_ssm_c_re, v_ssm_c_im, v_ssm_d, v_ssm_glu_w, v_ssm_glu_b, v_odd_w_in, v_pool_w, v_pool_scale, v_odd_w_out):
    w = dict(pre_norm=pre_norm, post_norm=post_norm, even_w_in=even_w_in, even_w_out=even_w_out, ssm_a_re=ssm_a_re,
             ssm_a_im=ssm_a_im, ssm_log_dt=ssm_log_dt, ssm_b_re=ssm_b_re, ssm_b_im=ssm_b_im, ssm_c_re=ssm_c_re,
             ssm_c_im=ssm_c_im, ssm_d=ssm_d, ssm_glu_w=ssm_glu_w, ssm_glu_b=ssm_glu_b, odd_w_in=odd_w_in,
             pool_w=pool_w, pool_scale=pool_scale, odd_w_out=odd_w_out)
    mom = dict(pre_norm=m_pre_norm, post_norm=m_post_norm, even_w_in=m_even_w_in, even_w_out=m_even_w_out,
               ssm_a_re=m_ssm_a_re, ssm_a_im=m_ssm_a_im, ssm_log_dt=m_ssm_log_dt, ssm_b_re=m_ssm_b_re,
               ssm_b_im=m_ssm_b_im, ssm_c_re=m_ssm_c_re, ssm_c_im=m_ssm_c_im, ssm_d=m_ssm_d, ssm_glu_w=m_ssm_glu_w,
               ssm_glu_b=m_ssm_glu_b, odd_w_in=m_odd_w_in, pool_w=m_pool_w, pool_scale=m_pool_scale,
               odd_w_out=m_odd_w_out)
    var = dict(pre_norm=v_pre_norm, post_norm=v_post_norm, even_w_in=v_even_w_in, even_w_out=v_even_w_out,
               ssm_a_re=v_ssm_a_re, ssm_a_im=v_ssm_a_im, ssm_log_dt=v_ssm_log_dt, ssm_b_re=v_ssm_b_re,
               ssm_b_im=v_ssm_b_im, ssm_c_re=v_ssm_c_re, ssm_c_im=v_ssm_c_im, ssm_d=v_ssm_d, ssm_glu_w=v_ssm_glu_w,
               ssm_glu_b=v_ssm_glu_b, odd_w_in=v_odd_w_in, pool_w=v_pool_w, pool_scale=v_pool_scale,
               odd_w_out=v_odd_w_out)
    me = _my_index()
    scale_cols = pool_scale.shape[1]

    def start_gather(tag, layer, keys, after=()):
        i = layer // 2
        shards = [w[FAMILY[(layer % 2, k)]][i].astype(BF16) for k in keys]
        if layer % 2 == 1:
            shards.append(jnp.pad(pool_scale[i][None], ((0, PACK_ROWS_ALIGN - 1), (0, 0))))
        return _xchg_start(f"gather_start_{tag}", shards, True, after)

    gather_started = {0: start_gather("0", 0, EVEN_SHARDED[:1])}
    small = {nm: w[nm] for nm in SMALL_NAMES}

    def get_weights(layer, after):
        keys = EVEN_SHARDED[:1] if layer == 0 else _sharded_keys(layer)
        lands = _xchg_wait(f"gather_wait_{layer}", gather_started[layer], True, after)
        wts = {k: _from_gathered(k, gat) for k, gat in zip(keys, lands)}
        if layer % 2 == 1:
            wts["pool_scale"] = lands[-1][:, 0, :].reshape(N_DEV * scale_cols)
        if layer == 0:
            prev = gather_started["0_late"] = start_gather("0_late", 0, EVEN_SHARDED[1:], after=(lands[0],))
            for later in (1, 2, 3):
                prev = gather_started[later] = start_gather(str(later), later, _sharded_keys(later), after=(prev[4],))
            wts["token"] = sum(gather_started[tag][4][0, 0] for tag in ("0_late", 1, 2, 3))

            def late(after_late):
                late_lands = _xchg_wait("gather_wait_0_late", gather_started["0_late"], True, (after_late,))
                return tuple(_from_gathered(k, gat) for k, gat in zip(EVEN_SHARDED[1:], late_lands))

            wts["late"] = late
        elif layer == 2:
            wts["late"] = lambda after_late: (wts["w_out"], wts["glu_w"])
        return wts

    scatter_started = []

    def on_w(layer, gw):
        keys = tuple(k for k in _sharded_keys(layer) if k in gw)
        started = _xchg_start(f"scatter_start_{layer}_{keys[0]}", [_to_slots(k, gw[k]) for k in keys], False)
        scatter_started.append((layer, keys, started))
        return started[4]

    def wait_scatters(layers, after):
        for layer, keys, started in scatter_started:
            if layer in layers:
                lands = _xchg_wait(f"scatter_wait_{layer}_{keys[0]}", started, False, after)
                for k, land in zip(keys, lands):
                    recv[(layer, k)] = land

    packed_names = ("pre_norm", "post_norm") + SSM_NAMES + ("ssm_d", "ssm_glu_b")
    tails = {nm: (SSM_GROUPS, SSM_STATE * SSM_GROUP) if nm in ("ssm_b_re", "ssm_b_im") else w[nm].shape[1:]
             for nm in packed_names}

    layer_grads = {}
    early_started, mid_started = [], []

    def on_ssm(layer, ssm_grads):
        if layer != 0:
            return None
        mid_started.append(_xchg_start("mid_start", [_pack(list(ssm_grads))], True))
        return mid_started[0][4]

    def on_grads(layer, lg):
        layer_grads[layer] = lg
        zero = jnp.zeros((), F32)
        if layer == 1:
            lgs = layer_grads
            early = ([jnp.concatenate([lgs[l][k] for l in (1, 2, 3)], axis=0) for k in ("pre", "post")]
                     + list(lgs[2]["ssm"]) + [lgs[2]["ssm_d"], lgs[2]["glu_b"],
                                              jnp.concatenate([lgs[1]["pool_scale"], lgs[3]["pool_scale"]], axis=0)])
            early_started.append(_xchg_start("small_start", [_pack(early)], True))
            zero = zero + early_started[0][4][0, 0]
        return zero

    loss_local, grad_x, token = _local_step(x[0], loss_target[0], small, get_weights, on_w, on_ssm, on_grads,
                                            zero=gather_started[0][4][0, 0])

    lg0 = layer_grads[0]
    late_started = _xchg_start("late_start", [_pack([lg0["pre"], lg0["post"], lg0["ssm_d"], lg0["glu_b"],
                                                     loss_local.reshape(1)]) + token], True)

    def adam_family(parity, k, which, into=None):
        nm = FAMILY[(parity, k)]
        cols = w[nm].shape[-1]
        return _adam_layer(w[nm].reshape(2, -1, cols), recv[(parity + 2 * which, k)].reshape(N_DEV, -1, cols),
                           mom[nm].reshape(2, -1, cols), var[nm].reshape(2, -1, cols), which,
                           f"adam_{nm}_{which}", into)

    recv, res = {}, {}
    wait_scatters((3, 2, 1), (late_started[4],))
    for k in ODD_SHARDED:
        res[FAMILY[(1, k)]] = adam_family(1, k, 1, adam_family(1, k, 0))
    half_done = {k: adam_family(0, k, 1) for k in EVEN_SHARDED}
    odd_done = tuple(half_done[k][0] for k in EVEN_SHARDED)

    (early_slots,) = _xchg_wait("small_wait", early_started[0], True, odd_done)
    (mid_slots,) = _xchg_wait("mid_wait", mid_started[0], True, odd_done)
    early_shapes = [(w[nm].shape[0] - 1,) + tails[nm] for nm in packed_names] + [(2, N_DEV * scale_cols)]
    g_early = _unpack(_sum_slots(early_slots, "sum_small_early"), early_shapes)
    g_mid = _unpack(_sum_slots(mid_slots, "sum_small_mid"), [(1,) + tails[nm] for nm in SSM_NAMES])

    (late_slots,) = _xchg_wait("late_wait", late_started, True, (g_early[0], g_mid[0]))
    wait_scatters((0,), (late_slots,))
    for k in EVEN_SHARDED:
        res[FAMILY[(0, k)]] = adam_family(0, k, 0, half_done[k])
    for nm in FAMILY.values():
        res[nm] = [o.reshape(w[nm].shape) for o in res[nm]]

    late_names = ("pre_norm", "post_norm", "ssm_d", "ssm_glu_b")
    g_late = _unpack(_sum_slots(late_slots, "sum_small_late"), [(1,) + tails[nm] for nm in late_names] + [(1,)])
    g_first = dict(zip(late_names, g_late))
    g_first.update(zip(SSM_NAMES, g_mid))
    dense = lambda nm, a: a.reshape((a.shape[0],) + tails[nm])
    outs = _adam_params([(dense(nm, w[nm]), dense(nm, mom[nm]), dense(nm, var[nm]), g_first[nm], g_early[j])
                         for j, nm in enumerate(packed_names)], "adam_small")
    for nm, four in zip(packed_names, outs):
        res[nm] = [o.reshape(w[nm].shape) for o in four]
    loss = g_late[-1].reshape(())
    g_scale = lax.dynamic_slice_in_dim(g_early[-1], me * scale_cols, scale_cols, axis=1)
    pad = ((0, PACK_ROWS_ALIGN - 2), (0, 0))
    outs = _adam(jnp.pad(pool_scale, pad), jnp.pad(g_scale, pad)[None], jnp.pad(m_pool_scale, pad),
                 jnp.pad(v_pool_scale, pad), name="adam_pool_scale")
    res["pool_scale"] = [o[:2] for o in outs]

    out = [loss, grad_x[None]]
    for kind in range(4):
        out += [res[nm][kind] for nm in WEIGHT_ORDER]
    return tuple(out)
```

```python
import functools
import math

import jax
import jax.numpy as jnp
from jax import lax
from jax.experimental import pallas as pl
from jax.experimental.pallas import tpu as pltpu

F32 = jnp.float32
BF16 = jnp.bfloat16
SDS = jax.ShapeDtypeStruct

N_DEV = 8
S = 2048
D = 1024
HEAD_DIM = 64
ROT_DIM = 16
ROPE_THETA = 500000.0
ATT_W = 1024
SSM_W = 512
SSM_GROUPS = 32
SSM_GROUP = 16
SSM_STATE = 64
N_CPLX = SSM_GROUPS * SSM_STATE
POOL_W = 2048
POOL_GROUP = 512
EVEN_IN = 5120
EVEN_OUT = 1536
ODD_IN = 4096
RMS_EPS = 1e-6
LANES = 128
VMEM_LIMIT = 48 * 1024 * 1024

ADAM_LR = 0.001
ADAM_B1 = 0.9
ADAM_B2 = 0.999
ADAM_EPS = 1e-08
ADAM_WD = 0.01
ADAM_STEP = 10

MESH_ID = pl.DeviceIdType.MESH
NN = (((1,), (0,)), ((), ()))
NT = (((1,), (1,)), ((), ()))
TN = (((0,), (0,)), ((), ()))
_DN = {"nn": NN, "nt": NT, "tn": TN}


def _cparams(sem):
    return pltpu.CompilerParams(dimension_semantics=sem, vmem_limit_bytes=VMEM_LIMIT)


def _in_hbm(arrs):
    return [pltpu.with_memory_space_constraint(a, pltpu.HBM) for a in arrs]


MM_TILES = (1024, 768, 512)


def _tile(dim):
    return next((t for t in MM_TILES if dim % t == 0), dim)


BLOCK_PAIR = 2
MAX_WHOLE_K = 2048


def _mm(a, b, mode, out_dtype, b_blocks=False, out_blocks=False, after=()):
    if b_blocks:
        nblk, rows, cb = b.shape
        b2_shape = (rows, nblk * cb)
    else:
        b2_shape = b.shape
    if mode == "nn":
        (m, k), n = a.shape, b2_shape[1]
    elif mode == "nt":
        (m, k), n = a.shape, b2_shape[0]
    else:
        (k, m), n = a.shape, b2_shape[1]
    tm, tn, tk = _tile(m), _tile(n), _tile(k)
    if k <= MAX_WHOLE_K:
        tk = k
    if b_blocks and mode == "nn":
        tn = BLOCK_PAIR * cb
        if tn <= MM_TILES[0]:
            tm = m
    if b_blocks and mode == "nt":
        tk = BLOCK_PAIR * cb
    if out_blocks:
        cb = n // N_DEV
        tn = BLOCK_PAIR * cb
        tk = k
    nk = k // tk

    def body(a_ref, b_ref, *rest):
        o_ref, acc_ref = rest[-2:]
        kk = pl.program_id(2)
        bv = jnp.concatenate([b_ref[p] for p in range(BLOCK_PAIR)], axis=1) if b_blocks else b_ref[...]
        part = lax.dot_general(a_ref[...].astype(BF16), bv.astype(BF16), _DN[mode], preferred_element_type=F32)

        def write(res):
            if out_blocks:
                for p in range(BLOCK_PAIR):
                    o_ref[p] = res[:, p * cb:(p + 1) * cb].astype(o_ref.dtype)
            else:
                o_ref[...] = res.astype(o_ref.dtype)

        if nk == 1:
            write(part)
            return

        @pl.when(kk == 0)
        def _():
            acc_ref[...] = part

        @pl.when((kk > 0) & (kk < nk - 1))
        def _():
            acc_ref[...] += part

        @pl.when(kk == nk - 1)
        def _():
            write(acc_ref[...] + part)

    if mode == "nn":
        a_spec = pl.BlockSpec((tm, tk), lambda i, j, kk: (i, kk))
        b_spec = pl.BlockSpec((tk, tn), lambda i, j, kk: (kk, j))
    elif mode == "nt":
        a_spec = pl.BlockSpec((tm, tk), lambda i, j, kk: (i, kk))
        b_spec = pl.BlockSpec((tn, tk), lambda i, j, kk: (j, kk))
    else:
        a_spec = pl.BlockSpec((tk, tm), lambda i, j, kk: (kk, i))
        b_spec = pl.BlockSpec((tk, tn), lambda i, j, kk: (kk, j))
    if b_blocks and mode == "nn":
        b_spec = pl.BlockSpec((BLOCK_PAIR, tk, cb), lambda i, j, kk: (j, kk, 0))
    if b_blocks and mode == "nt":
        b_spec = pl.BlockSpec((BLOCK_PAIR, tn, cb), lambda i, j, kk: (kk, j, 0))
    out_spec = pl.BlockSpec((tm, tn), lambda i, j, kk: (i, j))
    out_shape = SDS((m, n), out_dtype)
    if out_blocks:
        out_spec = pl.BlockSpec((BLOCK_PAIR, tm, cb), lambda i, j, kk: (j, i, 0))
        out_shape = SDS((N_DEV, m, cb), out_dtype)
    return pl.pallas_call(
        body, name=f"mm_{mode}_{m}x{k}x{n}",
        grid=(m // tm, n // tn, nk),
        in_specs=[a_spec, b_spec] + [pl.BlockSpec(memory_space=pl.ANY)] * len(after),
        out_specs=out_spec,
        out_shape=out_shape,
        scratch_shapes=[pltpu.VMEM((tm, tn) if nk > 1 else (8, LANES), F32)],
        compiler_params=_cparams(("parallel", "parallel", "arbitrary")),
    )(a, b, *after)


def _gmm(a, b, mode, out_dtype, tm=S):
    ng, gw = POOL_W // POOL_GROUP, POOL_GROUP
    ns = S // tm
    if mode in ("nn", "nt"):
        def body(a_ref, b_ref, o_ref):
            o_ref[...] = lax.dot_general(a_ref[...].astype(BF16), b_ref[...].astype(BF16), _DN[mode],
                                         preferred_element_type=F32).astype(o_ref.dtype)

        return pl.pallas_call(
            body, name=f"gmm_{mode}", grid=(ng, ns),
            in_specs=[pl.BlockSpec((tm, gw), lambda g, i: (i, g)),
                      pl.BlockSpec((None, gw, gw), lambda g, i: (g, 0, 0))],
            out_specs=pl.BlockSpec((tm, gw), lambda g, i: (i, g)),
            out_shape=SDS((S, POOL_W), out_dtype),
            compiler_params=_cparams(("parallel", "parallel")),
        )(a, b)

    def body_tn(a_ref, b_ref, o_ref, acc_ref):
        i = pl.program_id(1)

        @pl.when(i == 0)
        def _():
            acc_ref[...] = jnp.zeros_like(acc_ref)

        acc_ref[...] += lax.dot_general(a_ref[...].astype(BF16), b_ref[...].astype(BF16), TN,
                                        preferred_element_type=F32)

        @pl.when(i == ns - 1)
        def _():
            o_ref[...] = acc_ref[...].astype(o_ref.dtype)

    return pl.pallas_call(
        body_tn, name="gmm_tn", grid=(ng, ns),
        in_specs=[pl.BlockSpec((tm, gw), lambda g, i: (i, g)),
                  pl.BlockSpec((tm, gw), lambda g, i: (i, g))],
        out_specs=pl.BlockSpec((None, gw, gw), lambda g, i: (g, 0, 0)),
        out_shape=SDS((ng, gw, gw), out_dtype),
        scratch_shapes=[pltpu.VMEM((gw, gw), F32)],
        compiler_params=_cparams(("parallel", "arbitrary")),
    )(a, b)


def _rowwise(fn, inputs, out_defs, acc_defs=(), tm=512, name=None, after=()):
    n_in, n_out, n_acc = len(inputs), len(out_defs), len(acc_defs)
    n_after = len(after)
    in_specs, args = [], []
    for arr, width, cb in inputs:
        if arr.shape[0] != S:
            in_specs.append(pl.BlockSpec((arr.shape[0], width), lambda i, cb=cb: (0, cb)))
        else:
            in_specs.append(pl.BlockSpec((tm, width), lambda i, cb=cb: (i, cb)))
        args.append(arr)
    out_defs = [d if len(d) == 4 else (d[0], d[1], d[0], 0) for d in out_defs]
    out_shape = [SDS((S, ww), dt) for _, dt, ww, _ in out_defs] + [SDS((1, w), F32) for w in acc_defs]
    out_specs = ([pl.BlockSpec((tm, w), lambda i, cb=cb: (i, cb)) for w, _, _, cb in out_defs]
                 + [pl.BlockSpec((1, w), lambda i: (0, 0)) for w in acc_defs])

    def kern(*refs):
        vals = [r[...] for r in refs[:n_in]]
        outs, accs = fn(*vals)
        out_refs = refs[n_in + n_after:]
        for r, v in zip(out_refs[:n_out], outs):
            r[...] = v.astype(r.dtype)
        if n_acc:
            acc_refs = out_refs[n_out:]

            @pl.when(pl.program_id(0) == 0)
            def _():
                for r in acc_refs:
                    r[...] = jnp.zeros_like(r)

            for r, v in zip(acc_refs, accs):
                r[...] += jnp.sum(v, axis=0, keepdims=True)

    res = pl.pallas_call(
        kern, name=name, grid=(S // tm,), in_specs=in_specs + [pl.BlockSpec(memory_space=pl.ANY)] * n_after,
        out_specs=out_specs, out_shape=out_shape, compiler_params=_cparams(("arbitrary",)),
    )(*args, *after)
    return res


def _sigmoid(x):
    return 1.0 / (1.0 + jnp.exp(-x))


def _silu_and_grad(x):
    s = _sigmoid(x)
    return x * s, s * (1.0 + x * (1.0 - s))


_GELU_K = math.sqrt(2.0 / math.pi)
_GELU_C = 0.044715


def _gelu_and_grad(x):
    t = jnp.tanh(_GELU_K * (x + _GELU_C * (x * x * x)))
    cdf = 0.5 * (1.0 + t)
    grad = cdf + 0.5 * x * (1.0 - t * t) * (_GELU_K * (1.0 + 3.0 * _GELU_C * x * x))
    return x * cdf, grad


def _rms(xv, gain):
    r = lax.rsqrt(jnp.mean(xv * xv, axis=-1, keepdims=True) + RMS_EPS)
    return xv * r * gain


def _rms_bwd(dout, xv, gain):
    r = lax.rsqrt(jnp.mean(xv * xv, axis=-1, keepdims=True) + RMS_EPS)
    xhat = xv * r
    dxhat = dout * gain
    dx = r * (dxhat - xhat * jnp.mean(dxhat * xhat, axis=-1, keepdims=True))
    return dx, dout * xhat


def _norm_fwd(x, gain):
    (h,) = _rowwise(lambda xv, g: ((_rms(xv, g),), ()), [(x, D, 0), (gain, D, 0)], [(D, BF16)], name="norm_fwd")
    return h


def _post_fwd(x, y, gain, next_gain):
    def fn(xv, yv, g, gn):
        out = xv + _rms(yv, g)
        return (out, _rms(out, gn)), ()

    return _rowwise(fn, [(x, D, 0), (y, D, 0), (gain, D, 0), (next_gain, D, 0)], [(D, F32), (D, BF16)],
                    name="post_fwd")


def _post_fwd_loss(x, y, gain, tgt):
    def fn(xv, yv, g, tv):
        e = xv + _rms(yv, g) - tv
        return (e * (1.0 / D),), (e * e,)

    return _rowwise(fn, [(x, D, 0), (y, D, 0), (gain, D, 0), (tgt, D, 0)], [(D, F32)], [D], name="post_fwd_loss")


def _post_bwd(g, y, gain):
    def fn(gv, yv, gn):
        dx, dg = _rms_bwd(gv, yv, gn)
        return (dx,), (dg,)

    return _rowwise(fn, [(g, D, 0), (y, D, 0), (gain, D, 0)], [(D, BF16)], [D], name="post_bwd")


def _pre_bwd(g, dh, x, gain):
    def fn(gv, dhv, xv, gn):
        dx, dg = _rms_bwd(dhv, xv, gn)
        return (gv + dx,), (dg,)

    return _rowwise(fn, [(g, D, 0), (dh, D, 0), (x, D, 0), (gain, D, 0)], [(D, F32)], [D], name="pre_bwd")


def _pool(u_arr, col_block, transpose, out_dtype, into=None, tc=256):
    n_t = POOL_W // tc
    per_group = POOL_GROUP // tc

    def body(u_ref, *rest):
        o_ref = rest[-1]
        grp = pl.program_id(0) // per_group
        t = lax.broadcasted_iota(jnp.int32, (S, 1), 0)
        for g in range(POOL_W // POOL_GROUP):
            @pl.when(grp == g)
            def _(g=g):
                xv = u_ref[...]
                cnt = jnp.minimum(t + 1, 2 << g).astype(F32)
                cur = xv / cnt if transpose else xv
                for k in (1, 2, 4, 8)[:g + 1]:
                    if transpose:
                        cur = cur + jnp.where(t < S - k, pltpu.roll(cur, S - k, 0), 0.0)
                    else:
                        cur = cur + jnp.where(t >= k, pltpu.roll(cur, k, 0), 0.0)
                res = cur - xv if transpose else cur / cnt - xv
                o_ref[...] = res.astype(o_ref.dtype)

    in_specs = [pl.BlockSpec((S, tc), lambda c: (0, col_block * n_t + c))]
    args = [u_arr]
    if into is not None:
        in_specs.append(pl.BlockSpec(memory_space=pl.ANY))
        args.append(into)
    return pl.pallas_call(
        body, name="pool_bwd" if transpose else "pool_fwd", grid=(n_t,),
        in_specs=in_specs,
        out_specs=pl.BlockSpec((S, tc), lambda c: (0, c)),
        out_shape=SDS((S, POOL_W) if into is None else into.shape, out_dtype),
        input_output_aliases={} if into is None else {1: 0},
        compiler_params=_cparams(("parallel",)),
    )(*args)


def _rope_tables(zero):
    pos = jnp.arange(S, dtype=jnp.int32).astype(F32) + zero
    inv_freq = ROPE_THETA ** (-jnp.arange(0, ROT_DIM, 2, dtype=F32) / ROT_DIM)
    ang = pos[:, None] * inv_freq[None, :]
    cos8, sin8 = jnp.cos(ang), jnp.sin(ang)
    half = ROT_DIM // 2
    zeros = jnp.zeros((S, HEAD_DIM - ROT_DIM), F32)
    cos = jnp.concatenate([cos8, cos8, jnp.ones((S, HEAD_DIM - ROT_DIM), F32)], axis=1)
    sin = jnp.concatenate([-sin8, sin8, zeros], axis=1)
    rep = LANES // HEAD_DIM
    lane = jnp.arange(LANES)
    dim = lane % HEAD_DIM
    partner = jnp.where(dim < half, lane + half, jnp.where(dim < ROT_DIM, lane - half, -1))
    swap = (lane[:, None] == partner[None, :]).astype(BF16)
    return jnp.tile(cos, (1, rep)), jnp.tile(sin, (1, rep)), swap


def _rotate(xv, cos, sin, swap, transpose):
    rep = xv.shape[1] // LANES
    wide = lambda tab: jnp.concatenate([tab] * rep, axis=1)
    xb = xv.astype(BF16)
    partner = jnp.concatenate([lax.dot_general(xb[:, t * LANES:(t + 1) * LANES], swap, NN, preferred_element_type=F32)
                               for t in range(rep)], axis=1)
    mixed = partner * wide(sin)
    return xv * wide(cos) - mixed if transpose else xv * wide(cos) + mixed


def _qkv_prep(proj, tables):
    cos, sin, swap = tables

    def fn(x, c, s, sw):
        rot = _rotate(x[:, :2 * ATT_W], c, s, sw, False)
        return (jnp.concatenate([(rot[:, :ATT_W] * HEAD_DIM ** -0.5).astype(BF16), rot[:, ATT_W:].astype(BF16),
                                 x[:, 2 * ATT_W:].astype(BF16)], axis=1),), ()

    (qkv,) = _rowwise(fn, [(proj, 3 * ATT_W, 0), (cos, LANES, 0), (sin, LANES, 0), (swap, LANES, 0)],
                      [(3 * ATT_W, BF16)], name="qkv_prep")
    return qkv


ATT_T = 512


def _multiplicity(delta):
    ok = delta >= 0
    near = jnp.where(ok & (delta <= 128), 1.0, 0.0)
    mid = jnp.where(ok & (delta <= 512) & ((delta & 3) == 0), 1.0, 0.0)
    far = jnp.where(ok & ((delta & 15) == 0), 1.0, 0.0)
    return near + mid + far


def _attention_bias(zero):
    t = ATT_T
    pos = jnp.arange(t, dtype=jnp.int32) + jnp.asarray(zero).astype(jnp.int32)
    delta = jnp.arange(S // t, dtype=jnp.int32)[:, None, None] * t + pos[None, :, None] - pos[None, None, :]
    mult = _multiplicity(delta)
    return jnp.where(mult > 0.0, jnp.log(jnp.maximum(mult, 1.0)), -1e30).astype(F32)


def _head_split(v, first):
    zero = jnp.zeros_like(v)
    return [jnp.where(first, v, zero), jnp.where(first, zero, v)]


def _flash_fwd(qkv, bias):
    t = ATT_T
    n_hp = ATT_W // LANES

    def body(q_ref, k_ref, v_ref, b_ref, o_ref, lse_ref):
        i = pl.program_id(1)
        first = lax.broadcasted_iota(jnp.int32, (1, LANES), 1) < HEAD_DIM
        qs = _head_split(q_ref[...], first)

        def kv_step(j, carry):
            m0, l0, m1, l1, acc = carry
            off = pl.multiple_of(j * t, t)
            kb = k_ref[pl.ds(off, t), :]
            vs = _head_split(v_ref[pl.ds(off, t), :], first)
            bias_t = b_ref[i - j]
            new = []
            pv = None
            for h, (m_prev, l_prev) in enumerate(((m0, l0), (m1, l1))):
                s = lax.dot_general(qs[h], kb, NT, preferred_element_type=F32) + bias_t
                m_new = jnp.maximum(m_prev, jnp.max(s, axis=1, keepdims=True))
                p = jnp.exp(s - m_new)
                alpha = jnp.exp(m_prev - m_new)
                l_new = alpha * l_prev + jnp.sum(p, axis=1, keepdims=True)
                d = lax.dot_general(p.astype(BF16), vs[h], NN, preferred_element_type=F32)
                pv = d if pv is None else pv + d
                new.append((m_new, l_new, alpha))
            acc = acc * jnp.where(first, new[0][2], new[1][2]) + pv
            return new[0][0], new[0][1], new[1][0], new[1][1], acc

        neg = jnp.full((t, 1), -1e30, F32)
        zero = jnp.zeros((t, 1), F32)
        m0, l0, m1, l1, acc = lax.fori_loop(0, i + 1, kv_step, (neg, zero, neg, zero, jnp.zeros((t, LANES), F32)))
        o_ref[...] = acc * jnp.where(first, 1.0 / l0, 1.0 / l1)
        lse_ref[...] = jnp.where(first, m0 + jnp.log(l0), m1 + jnp.log(l1))

    blk = pl.BlockSpec((t, LANES), lambda hp, i: (i, hp))
    k_full = pl.BlockSpec((S, LANES), lambda hp, i: (0, n_hp + hp))
    v_full = pl.BlockSpec((S, LANES), lambda hp, i: (0, 2 * n_hp + hp))
    return pl.pallas_call(
        body, name="flash_fwd", grid=(n_hp, S // t),
        in_specs=[blk, k_full, v_full, pl.BlockSpec((S // t, t, t), lambda hp, i: (0, 0, 0))], out_specs=[blk, blk],
        out_shape=[SDS((S, ATT_W), F32), SDS((S, ATT_W), F32)],
        compiler_params=_cparams(("parallel", "arbitrary")),
    )(qkv, qkv, qkv, bias)


def _flash_bwd(qkv, o, do, lse, bias, after=()):
    t = ATT_T
    n_hp = ATT_W // LANES
    n_t = S // t

    def body(q_ref, k_ref, v_ref, o_ref, do_ref, lse_ref, b_ref, *rest):
        dq_ref, dk_ref, dv_ref = rest[-3:]
        j = pl.program_id(1)
        first = lax.broadcasted_iota(jnp.int32, (1, LANES), 1) < HEAD_DIM

        @pl.when(j == 0)
        def _():
            dq_ref[...] = jnp.zeros_like(dq_ref)

        kb = k_ref[...]
        vb = v_ref[...]
        ks = _head_split(kb, first)

        def q_step(i, carry):
            dk_acc, dv_acc = carry
            rows = pl.ds(pl.multiple_of(i * t, t), t)
            qs = _head_split(q_ref[rows, :], first)
            dob = do_ref[rows, :]
            prod = dob * o_ref[rows, :]
            d_all = jnp.sum(prod, axis=1, keepdims=True)
            d0 = jnp.sum(jnp.where(first, prod, 0.0), axis=1, keepdims=True)
            lse_b = lse_ref[rows, :]
            lse0 = jnp.max(jnp.where(first, lse_b, -jnp.inf), axis=1, keepdims=True)
            lse1 = jnp.max(jnp.where(first, -jnp.inf, lse_b), axis=1, keepdims=True)
            dos = _head_split(dob.astype(BF16), first)
            bias_t = b_ref[i - j]
            dq_t = jnp.zeros((t, LANES), F32)
            for h, (lse_h, d_h) in enumerate(((lse0, d0), (lse1, d_all - d0))):
                s = lax.dot_general(qs[h], kb, NT, preferred_element_type=F32)
                p = jnp.exp(s + (bias_t - lse_h))
                dp = lax.dot_general(dos[h], vb, NT, preferred_element_type=F32)
                ds = (p * (dp - d_h)).astype(BF16)
                dv_acc = dv_acc + lax.dot_general(p.astype(BF16), dos[h], TN, preferred_element_type=F32)
                dk_acc = dk_acc + lax.dot_general(ds, qs[h], TN, preferred_element_type=F32)
                dq_t = dq_t + lax.dot_general(ds, ks[h], NN, preferred_element_type=F32)
            dq_ref[rows, :] += dq_t
            return dk_acc, dv_acc

        zero = jnp.zeros((t, LANES), F32)
        dk_acc, dv_acc = lax.fori_loop(j, n_t, q_step, (zero, zero))
        dk_ref[...] = dk_acc
        dv_ref[...] = dv_acc

    blk = pl.BlockSpec((t, LANES), lambda hp, j: (j, hp))
    full = pl.BlockSpec((S, LANES), lambda hp, j: (0, hp))
    k_blk = pl.BlockSpec((t, LANES), lambda hp, j: (j, n_hp + hp))
    v_blk = pl.BlockSpec((t, LANES), lambda hp, j: (j, 2 * n_hp + hp))
    return pl.pallas_call(
        body, name="flash_bwd", grid=(n_hp, n_t),
        in_specs=([full, k_blk, v_blk, full, full, full, pl.BlockSpec((n_t, t, t), lambda hp, j: (0, 0, 0))]
                  + [pl.BlockSpec(memory_space=pl.ANY)] * len(after)),
        out_specs=[full, blk, blk],
        out_shape=[SDS((S, ATT_W), F32)] * 3,
        compiler_params=_cparams(("parallel", "arbitrary")),
    )(qkv, qkv, qkv, o, do, lse, bias, *after)


SCAN_T = 256
SCAN_GROUP = 8
SCAN_STEPS = (1, 2, 4)
ST_ROWS = 2 * N_CPLX // LANES
HALF = ST_ROWS // 2


def _scan_tables(lam_t):
    lam = lax.complex(lam_t[:HALF].reshape(N_CPLX), lam_t[HALF:].reshape(N_CPLX))
    pows = jnp.cumprod(jnp.broadcast_to(lam, (SCAN_GROUP, N_CPLX)), axis=0)
    shifts = jnp.asarray(SCAN_STEPS)
    sub = jnp.arange(SCAN_GROUP)[None, :, None]
    steps = pows[shifts - 1][:, None, :]
    fwd = jnp.concatenate([jnp.where(sub >= shifts[:, None, None], steps, 0.0), pows[None]], axis=0)
    bwd = jnp.concatenate([jnp.where(sub <= SCAN_GROUP - 1 - shifts[:, None, None], jnp.conj(steps), 0.0),
                           jnp.conj(pows)[None, ::-1]], axis=0)

    def pack(tabs):
        return jnp.concatenate([jnp.real(tabs), jnp.imag(tabs)], axis=-1).astype(F32)

    return pack(fwd), pack(bwd)


def _cmul_add(xr, xi, lr, li, sr, si):
    return xr + lr * sr - li * si, xi + lr * si + li * sr


def _group_scan(xr, xi, tab_ref, cr, ci, reverse):
    for j, k in enumerate(SCAN_STEPS):
        shift = SCAN_GROUP - k if reverse else k
        xr, xi = _cmul_add(xr, xi, tab_ref[j, :, :N_CPLX], tab_ref[j, :, N_CPLX:],
                           pltpu.roll(xr, shift, 0), pltpu.roll(xi, shift, 0))
    return _cmul_add(xr, xi, tab_ref[3, :, :N_CPLX], tab_ref[3, :, N_CPLX:],
                     jnp.broadcast_to(cr, (SCAN_GROUP, N_CPLX)), jnp.broadcast_to(ci, (SCAN_GROUP, N_CPLX)))


SSM_SUPER = 4
SB_ROWS = SSM_W // SSM_SUPER
SB_COLS = N_CPLX // SSM_SUPER


def _super_blocks():
    return [(slice(b * SB_ROWS, (b + 1) * SB_ROWS), slice(h * SB_COLS, (h + 1) * SB_COLS),
             slice(h * N_CPLX + b * SB_COLS, h * N_CPLX + (b + 1) * SB_COLS))
            for b in range(SSM_SUPER) for h in range(2)]


def _dot16(a, b, dims):
    return lax.dot_general(a.astype(BF16), b.astype(BF16), dims, preferred_element_type=F32)


def _s5_fwd(tab, u_arr, u_cols, w_b, w_ct):
    nc = N_CPLX

    def body(tab_ref, u_ref, wb_ref, wct_ref, st_ref, y_ref, carry, bu_scr):
        @pl.when(pl.program_id(0) == 0)
        def _():
            carry[...] = jnp.zeros_like(carry)

        for rows_b, cols_c, cols_s in _super_blocks():
            bu_scr[:, cols_s] = _dot16(u_ref[:, rows_b], wb_ref[rows_b, cols_c], NN)

        def group(a, c):
            rows = pl.ds(pl.multiple_of(a * SCAN_GROUP, SCAN_GROUP), SCAN_GROUP)
            xr, xi = _group_scan(bu_scr[rows, :nc], bu_scr[rows, nc:], tab_ref, c[0], c[1], False)
            st_ref[rows, :nc] = xr
            st_ref[rows, nc:] = xi
            return xr[SCAN_GROUP - 1:SCAN_GROUP, :], xi[SCAN_GROUP - 1:SCAN_GROUP, :]

        cr, ci = lax.fori_loop(0, SCAN_T // SCAN_GROUP, group, (carry[:, :nc], carry[:, nc:]), unroll=2)
        carry[:, :nc] = cr
        carry[:, nc:] = ci

        for b in range(SSM_SUPER):
            (rows_b, cols_re, st_re), (_, cols_im, st_im) = _super_blocks()[2 * b:2 * b + 2]
            y_ref[:, rows_b] = (_dot16(st_ref[:, st_re], wct_ref[rows_b, cols_re], NT)
                                + _dot16(st_ref[:, st_im], wct_ref[rows_b, cols_im], NT))

    const = lambda shape: pl.BlockSpec(shape, lambda i: (0,) * len(shape))
    return pl.pallas_call(
        body, name="s5_fwd", grid=(S // SCAN_T,),
        in_specs=[const((4, SCAN_GROUP, 2 * nc)), pl.BlockSpec((SCAN_T, SSM_W), lambda i: (i, u_cols[0] // SSM_W)),
                  const((SSM_W, 2 * SB_COLS)), const((SSM_W, 2 * SB_COLS))],
        out_specs=[pl.BlockSpec((SCAN_T, 2 * nc), lambda i: (i, 0)), pl.BlockSpec((SCAN_T, SSM_W), lambda i: (i, 0))],
        out_shape=[SDS((S, 2 * nc), F32), SDS((S, SSM_W), F32)],
        scratch_shapes=[pltpu.VMEM((1, 2 * nc), F32), pltpu.VMEM((SCAN_T, 2 * nc), F32)],
        compiler_params=_cparams(("arbitrary",)),
    )(tab, u_arr, w_b, w_ct)


def _s5_bwd(tab, dy, states, u_arr, u_cols, w_b, w_ct):
    n_blk = S // SCAN_T
    nc = N_CPLX

    def body(tab_ref, dy_ref, x_ref, u_ref, wb_ref, wct_ref, du_ref, dlam_ref, dwb_ref, dwct_ref,
             carry, acc, d_scr, g_ref):
        i = pl.program_id(0)

        @pl.when(i == 0)
        def _():
            carry[...] = jnp.zeros_like(carry)
            acc[...] = jnp.zeros_like(acc)
            dwb_ref[...] = jnp.zeros_like(dwb_ref)
            dwct_ref[...] = jnp.zeros_like(dwct_ref)

        for rows_b, cols_c, cols_s in _super_blocks():
            d_scr[:, cols_s] = _dot16(dy_ref[:, rows_b], wct_ref[rows_b, cols_c], NN)

        last_row = lax.broadcasted_iota(jnp.int32, (SCAN_GROUP, 1), 0) == SCAN_GROUP - 1
        d_ref = d_scr

        def group(j, c):
            cr, ci = c
            rows = pl.ds(pl.multiple_of((SCAN_T // SCAN_GROUP - 1 - j) * SCAN_GROUP, SCAN_GROUP), SCAN_GROUP)
            gr, gi = _group_scan(d_ref[rows, :nc], d_ref[rows, nc:], tab_ref, cr, ci, True)
            g_ref[rows, :nc] = gr
            g_ref[rows, nc:] = gi
            nr = jnp.where(last_row, jnp.broadcast_to(cr, (SCAN_GROUP, nc)), pltpu.roll(gr, SCAN_GROUP - 1, 0))
            ni = jnp.where(last_row, jnp.broadcast_to(ci, (SCAN_GROUP, nc)), pltpu.roll(gi, SCAN_GROUP - 1, 0))
            sr, si = x_ref[rows, :nc], x_ref[rows, nc:]
            acc[:, :nc] += nr * sr + ni * si
            acc[:, nc:] += ni * sr - nr * si
            return gr[0:1, :], gi[0:1, :]

        cr, ci = lax.fori_loop(0, SCAN_T // SCAN_GROUP, group, (carry[:, :nc], carry[:, nc:]), unroll=2)
        carry[:, :nc] = cr
        carry[:, nc:] = ci

        for b in range(SSM_SUPER):
            (rows_b, cols_re, st_re), (_, cols_im, st_im) = _super_blocks()[2 * b:2 * b + 2]
            du_ref[:, rows_b] = (_dot16(g_ref[:, st_re], wb_ref[rows_b, cols_re], NT)
                                 + _dot16(g_ref[:, st_im], wb_ref[rows_b, cols_im], NT))
            for cols_c, cols_s in ((cols_re, st_re), (cols_im, st_im)):
                dwb_ref[rows_b, cols_c] += _dot16(u_ref[:, rows_b], g_ref[:, cols_s], TN)
                dwct_ref[rows_b, cols_c] += _dot16(dy_ref[:, rows_b], x_ref[:, cols_s], TN)

        @pl.when(i == n_blk - 1)
        def _():
            dlam_ref[...] = jnp.sum(acc[...], axis=0, keepdims=True)

    const = lambda shape: pl.BlockSpec(shape, lambda i: (0,) * len(shape))
    rows = lambda width, col_block=0: pl.BlockSpec((SCAN_T, width), lambda i: (n_blk - 1 - i, col_block))
    maps = const((SSM_W, 2 * SB_COLS))
    return pl.pallas_call(
        body, name="s5_bwd", grid=(n_blk,),
        in_specs=[const((4, SCAN_GROUP, 2 * nc)), rows(SSM_W), rows(2 * nc), rows(SSM_W, u_cols[0] // SSM_W), maps, maps],
        out_specs=[rows(SSM_W), const((1, 2 * nc)), maps, maps],
        out_shape=[SDS((S, SSM_W), F32), SDS((1, 2 * nc), F32), SDS((SSM_W, 2 * SB_COLS), F32),
                   SDS((SSM_W, 2 * SB_COLS), F32)],
        scratch_shapes=[pltpu.VMEM((1, 2 * nc), F32), pltpu.VMEM((SCAN_GROUP, 2 * nc), F32),
                        pltpu.VMEM((SCAN_T, 2 * nc), F32), pltpu.VMEM((SCAN_T, 2 * nc), F32)],
        compiler_params=_cparams(("arbitrary",)),
    )(tab, dy, states, u_arr, w_b, w_ct)


def _ssm_prep(a_re, a_im, log_dt, b_re, b_im, c_re, c_im):
    lam = lax.complex(a_re, a_im)
    dt = jnp.exp(log_dt)[:, None]
    lam_bar = jnp.exp(lam * dt)
    b_bar = ((lam_bar - 1.0) / lam)[..., None] * lax.complex(b_re, b_im)
    lam_t = jnp.concatenate([jnp.real(lam_bar).reshape(HALF, LANES), jnp.imag(lam_bar).reshape(HALF, LANES)], axis=0)
    groups_per_super = SSM_GROUPS // SSM_SUPER
    on_diag = ((lax.broadcasted_iota(jnp.int32, (SSM_W, SB_COLS), 0) // SSM_GROUP) % groups_per_super
               == lax.broadcasted_iota(jnp.int32, (SSM_W, SB_COLS), 1) // SSM_STATE)

    repeat = (lax.broadcasted_iota(jnp.int32, (SSM_STATE, SB_COLS), 0)
              == lax.broadcasted_iota(jnp.int32, (SSM_STATE, SB_COLS), 1) % SSM_STATE).astype(F32)

    def compact(m):
        tiled = jnp.dot(m.reshape(SSM_W, SSM_STATE), repeat, precision=lax.Precision.HIGHEST)
        return jnp.where(on_diag, tiled, 0.0)

    w_b = jnp.concatenate([compact(jnp.real(b_bar).transpose(0, 2, 1)),
                           compact(jnp.imag(b_bar).transpose(0, 2, 1))], axis=1)
    w_ct = jnp.concatenate([compact(c_re), -compact(c_im)], axis=1)
    return lam_t, w_b, w_ct


U_SSM_COLS = (4 * ATT_W, SSM_W)


def _row(v):
    return v.reshape(1, -1)


def _even_fwd(x, h, tail, pre, post, w_in, late_w, glu_b, ssm_d, prep, tables):
    lam_t, w_b, w_ct = prep
    proj = _mm(h, w_in, "nn", F32, b_blocks=True)
    qkv = _qkv_prep(proj, tables[:3])
    w_out, glu_w = late_w(qkv)
    att, lse = _flash_fwd(qkv, tables[3])
    scan_fwd_tab, scan_bwd_tab = _scan_tables(lam_t)
    states, y = _s5_fwd(scan_fwd_tab, proj, U_SSM_COLS, w_b, w_ct)

    def act1(yv, uv, dv):
        return (_gelu_and_grad(yv + dv * uv)[0],), ()

    (z1,) = _rowwise(act1, [(y, SSM_W, 0), (proj, SSM_W, 8), (ssm_d, SSM_W, 0)], [(SSM_W, F32)], name="ssm_act_fwd")
    lin = _mm(z1, glu_w, "nn", F32)

    def gate(att_v, ga, gs, z1v, linv, bv):
        ssm_out = z1v * _sigmoid(linv + bv)
        return (jnp.concatenate([att_v * _silu_and_grad(ga)[0], ssm_out * _silu_and_grad(gs)[0]], axis=1),), ()

    (merged,) = _rowwise(gate, [(att, ATT_W, 0), (proj, ATT_W, 3), (proj, SSM_W, 9), (z1, SSM_W, 0),
                                (lin, SSM_W, 0), (glu_b, SSM_W, 0)], [(EVEN_OUT, BF16)], name="even_gate_fwd")
    yout = _mm(merged, w_out, "nn", F32)
    saved = (x, h, proj, qkv, att, lse, states, y, z1, lin, merged, yout, w_out, glu_w, scan_bwd_tab)
    return tail(x, yout, post) + (saved,)


def _even_bwd(g, saved, pre, post, w_in, late_w, glu_b, ssm_d, prep, tables, on_w, on_ssm):
    x, h, proj, qkv, att, lse, states, y, z1, lin, merged, yout, w_out, glu_w, scan_bwd_tab = saved
    lam_t, w_b, w_ct = prep
    dyout, dpost = _post_bwd(g, yout, post)
    dmerged = _mm(dyout, w_out, "nt", F32)
    dw_out = _mm(merged, dyout, "tn", BF16)

    def gate_bwd(dm_a, dm_s, att_v, ga, gs, z1v, linv, bv):
        sa, dsa = _silu_and_grad(ga)
        ss, dss = _silu_and_grad(gs)
        sig = _sigmoid(linv + bv)
        ssm_out = z1v * sig
        dssm = dm_s * ss
        dlin = dssm * z1v * sig * (1.0 - sig)
        return (dm_a * sa, dm_a * att_v * dsa, dm_s * ssm_out * dss, dssm * sig, dlin), (dlin,)

    datt, dg_att, dg_ssm, dz1a, dlin, dglu_b = _rowwise(
        gate_bwd, [(dmerged, ATT_W, 0), (dmerged, SSM_W, 2), (att, ATT_W, 0), (proj, ATT_W, 3), (proj, SSM_W, 9),
                   (z1, SSM_W, 0), (lin, SSM_W, 0), (glu_b, SSM_W, 0)],
        [(ATT_W, F32), (ATT_W, BF16), (SSM_W, BF16), (SSM_W, F32), (SSM_W, BF16)], [SSM_W], name="even_gate_bwd")
    dz1b = _mm(dlin, glu_w, "nt", F32)
    dglu_w = _mm(z1, dlin, "tn", BF16)

    def act1_bwd(da, db, yv, uv, dv):
        dpre = (da + db) * _gelu_and_grad(yv + dv * uv)[1]
        return (dpre, dpre * dv), (dpre * uv,)

    sent_late_w = on_w(dict(w_out=dw_out, glu_w=dglu_w))
    dy, du_direct, dd = _rowwise(act1_bwd, [(dz1a, SSM_W, 0), (dz1b, SSM_W, 0), (y, SSM_W, 0), (proj, SSM_W, 8),
                                            (ssm_d, SSM_W, 0)], [(SSM_W, BF16), (SSM_W, F32)], [SSM_W],
                                 name="ssm_act_bwd", after=(sent_late_w,))
    du_state, dlam_row, dw_b, dw_ct = _s5_bwd(scan_bwd_tab, dy, states, proj, U_SSM_COLS, w_b, w_ct)
    dlam = jnp.concatenate([dlam_row[0, :N_CPLX].reshape(HALF, LANES), dlam_row[0, N_CPLX:].reshape(HALF, LANES)],
                           axis=0)
    sent_ssm = on_ssm((dlam, dw_b, dw_ct))
    dq, dk, dv = _flash_bwd(qkv, att, datt, lse, tables[3], after=() if sent_ssm is None else (sent_ssm,))

    def assemble(dqv, dkv, dvv, dga, dua, dub, dgs, c, s, sw):
        rot = _rotate(jnp.concatenate([dqv, dkv], axis=1), c, s, sw, True)
        return (jnp.concatenate([(rot[:, :ATT_W] * HEAD_DIM ** -0.5).astype(BF16), rot[:, ATT_W:].astype(BF16),
                                 dvv.astype(BF16), dga, (dua + dub).astype(BF16), dgs], axis=1),), ()

    (dproj,) = _rowwise(assemble, [(dq, ATT_W, 0), (dk, ATT_W, 0), (dv, ATT_W, 0), (dg_att, ATT_W, 0),
                                   (du_state, SSM_W, 0), (du_direct, SSM_W, 0), (dg_ssm, SSM_W, 0),
                                   (tables[0], LANES, 0), (tables[1], LANES, 0), (tables[2], LANES, 0)],
                        [(EVEN_IN, BF16)], name="dproj_assemble")
    dw_in = _mm(h, dproj, "tn", BF16, out_blocks=True)
    sent = on_w(dict(w_in=dw_in))
    dh = _mm(dproj, w_in, "nt", F32, b_blocks=True, after=(sent,))
    g_prev, dpre = _pre_bwd(g, dh, x, pre)
    return g_prev, dict(pre=dpre, post=dpost, glu_b=dglu_b, ssm_d=dd)


def _odd_fwd(x, h, tail, pre, post, w_in, pool_w, pool_scale, w_out):
    proj = _mm(h, w_in, "nn", F32, b_blocks=True)
    mixed = _pool(proj, 0, False, BF16)
    ylin = _gmm(mixed, pool_w, "nn", F32)

    def gate(yl, gt, sc):
        return (yl * sc * _silu_and_grad(gt)[0],), ()

    (z,) = _rowwise(gate, [(ylin, POOL_W, 0), (proj, POOL_W, 1), (pool_scale, POOL_W, 0)], [(POOL_W, BF16)],
                    name="odd_gate_fwd")
    yout = _mm(z, w_out, "nn", F32)
    return tail(x, yout, post) + ((x, h, proj, mixed, ylin, z, yout),)


def _odd_bwd(g, saved, pre, post, w_in, pool_w, pool_scale, w_out, on_w):
    x, h, proj, mixed, ylin, z, yout = saved
    dyout, dpost = _post_bwd(g, yout, post)
    dz = _mm(dyout, w_out, "nt", F32)
    dw_out = _mm(z, dyout, "tn", BF16)

    def gate_bwd(dzv, yl, gt, sc):
        sg, dsg = _silu_and_grad(gt)
        tt = dzv * sg
        return (tt * sc, dzv * yl * sc * dsg), (tt * yl,)

    dylin, dproj_gate, dscale = _rowwise(gate_bwd, [(dz, POOL_W, 0), (ylin, POOL_W, 0), (proj, POOL_W, 1),
                                                    (pool_scale, POOL_W, 0)],
                                         [(POOL_W, BF16), (POOL_W, BF16, ODD_IN, 1)], [POOL_W], name="odd_gate_bwd")
    dmixed = _gmm(dylin, pool_w, "nt", F32)
    dpool_w = _gmm(mixed, dylin, "tn", BF16)
    dproj = _pool(dmixed, 0, True, BF16, into=dproj_gate)
    dw_in = _mm(h, dproj, "tn", BF16, out_blocks=True)
    sent = on_w(dict(w_in=dw_in, w_out=dw_out, pool_w=dpool_w))
    dh = _mm(dproj, w_in, "nt", F32, b_blocks=True, after=(sent,))
    g_prev, dpre = _pre_bwd(g, dh, x, pre)
    return g_prev, dict(pre=dpre, post=dpost, pool_scale=dscale)


def _my_index():
    return 4 * lax.axis_index("x") + 2 * lax.axis_index("y") + lax.axis_index("c")


HBM_SPEC = pl.BlockSpec(memory_space=pltpu.HBM)
SEM_SPEC = pl.BlockSpec(memory_space=pltpu.SEMAPHORE)
SPLIT_EFFECT = pltpu.SideEffectType.DATAFLOW_SIDE_EFFECTING


def _device_of(j):
    return (j // 4, (j // 2) % 2, j % 2)


def _split_copy(srcs, lands, send_sems, recv_sems, gather, i, j, dst_slot, recv_slot):
    return pltpu.make_async_remote_copy(
        src_ref=srcs[i] if gather else srcs[i].at[j], dst_ref=lands[i].at[dst_slot],
        send_sem=send_sems.at[i * N_DEV + j], recv_sem=recv_sems.at[i * N_DEV + recv_slot],
        device_id=_device_of(j), device_id_type=MESH_ID)


def _own_copy(srcs, lands, send_sems, gather, i, me):
    return pltpu.make_async_copy(srcs[i] if gather else srcs[i].at[me], lands[i].at[me], send_sems.at[i * N_DEV + me])


def _xchg_start(name, srcs, gather, after=()):
    n = len(srcs)
    n_in = n + len(after)

    def body(*refs):
        src_refs = refs[:n]
        send_sems, recv_sems, token = refs[n_in], refs[n_in + 1], refs[-1]
        land_refs = refs[n_in + 2 + n:n_in + 2 + 2 * n]
        me = _my_index()
        for j in range(N_DEV):
            @pl.when(me != j)
            def _(j=j):
                for i in range(n):
                    _split_copy(src_refs, land_refs, send_sems, recv_sems, gather, i, j, me, me).start()
        for i in range(n):
            _own_copy(src_refs, land_refs, send_sems, gather, i, me).start()
        token[...] = jnp.zeros_like(token)

    land_shapes = [((N_DEV,) + a.shape) if gather else a.shape for a in srcs]
    thru = ([pltpu.HBM(a.shape, a.dtype) for a in srcs] + [pltpu.HBM(s, a.dtype) for s, a in zip(land_shapes, srcs)])
    res = pl.pallas_call(
        body, name=name,
        out_shape=(pltpu.SemaphoreType.DMA((n * N_DEV,)), pltpu.SemaphoreType.DMA((n * N_DEV,)), *thru,
                   SDS((8, LANES), F32)),
        in_specs=[HBM_SPEC] * n + [pl.BlockSpec(memory_space=pl.ANY)] * len(after),
        out_specs=(SEM_SPEC, SEM_SPEC, *([HBM_SPEC] * (2 * n)), pl.BlockSpec(memory_space=pltpu.VMEM)),
        input_output_aliases={i: 2 + i for i in range(n)},
        compiler_params=pltpu.CompilerParams(has_side_effects=SPLIT_EFFECT),
    )(*[pltpu.with_memory_space_constraint(a, pltpu.HBM) for a in srcs], *after)
    return res[0], res[1], list(res[2:2 + n]), list(res[2 + n:2 + 2 * n]), res[-1]


def _xchg_wait(name, started, gather, after):
    send_sems, recv_sems, srcs, lands, _ = started
    n = len(srcs)

    def body(*refs):
        src_refs, land_refs = refs[:n], refs[n:2 * n]
        send_r, recv_r = refs[2 * n], refs[2 * n + 1]
        me = _my_index()
        for j in range(N_DEV):
            @pl.when(me != j)
            def _(j=j):
                for i in range(n):
                    _split_copy(src_refs, land_refs, send_r, recv_r, gather, i, j, me, me).wait_send()
                    _split_copy(src_refs, land_refs, send_r, recv_r, gather, i, j, j, j).wait_recv()
        for i in range(n):
            _own_copy(src_refs, land_refs, send_r, gather, i, me).wait()

    thru = [pltpu.HBM(a.shape, a.dtype) for a in list(srcs) + list(lands)]
    res = pl.pallas_call(
        body, name=name, out_shape=tuple(thru),
        in_specs=[HBM_SPEC] * (2 * n) + [SEM_SPEC, SEM_SPEC] + [pl.BlockSpec(memory_space=pl.ANY)] * len(after),
        out_specs=tuple([HBM_SPEC] * (2 * n)),
        input_output_aliases={i: i for i in range(2 * n)},
        compiler_params=pltpu.CompilerParams(has_side_effects=SPLIT_EFFECT),
    )(*srcs, *lands, send_sems, recv_sems, *after)
    return list(res[n:])


def _adam_layer(w, slots, m, v, layer, name, into=None):
    n_l, r, c = w.shape
    ns = slots.shape[0]
    tr = r
    while tr * c * 4 > (1 << 20) and tr % 16 == 0:
        tr //= 2
    assert r % tr == 0

    def body(w_ref, g_ref, m_ref, v_ref, *rest):
        go_ref, d_ref, mo_ref, vo_ref = rest[-4:]
        g = g_ref[0].astype(F32)
        for s in range(1, ns):
            g = g + g_ref[s].astype(F32)
        mn = ADAM_B1 * m_ref[...] + (1.0 - ADAM_B1) * g
        vn = ADAM_B2 * v_ref[...] + (1.0 - ADAM_B2) * (g * g)
        m_hat = mn / (1.0 - ADAM_B1 ** ADAM_STEP)
        v_hat = vn / (1.0 - ADAM_B2 ** ADAM_STEP)
        go_ref[...] = g
        d_ref[...] = -ADAM_LR * (m_hat / (jnp.sqrt(v_hat) + ADAM_EPS) + ADAM_WD * w_ref[...])
        mo_ref[...] = mn
        vo_ref[...] = vn

    blk = pl.BlockSpec((None, tr, c), lambda i: (layer, i, 0))
    earlier = () if into is None else tuple(into)
    return pl.pallas_call(
        body, name=name, grid=(r // tr,),
        in_specs=[blk, pl.BlockSpec((ns, tr, c), lambda i: (0, i, 0)), blk, blk]
        + [pl.BlockSpec(memory_space=pl.ANY)] * len(earlier),
        out_specs=[blk] * 4, out_shape=[SDS((n_l, r, c), F32)] * 4,
        input_output_aliases={4 + q: q for q in range(len(earlier))},
        compiler_params=_cparams(("arbitrary",)),
    )(*_in_hbm((w, slots, m, v)), *earlier)


def _adam(w, gslots, m, v, name):
    r, c = w.shape
    ns = gslots.shape[0]
    tr = r
    while tr * c * 4 > (1 << 20) and tr % 16 == 0:
        tr //= 2
    assert r % tr == 0

    def body(w_ref, g_ref, m_ref, v_ref, go_ref, d_ref, mo_ref, vo_ref):
        g = g_ref[0].astype(F32)
        for s in range(1, ns):
            g = g + g_ref[s].astype(F32)
        wv = w_ref[...]
        mn = ADAM_B1 * m_ref[...] + (1.0 - ADAM_B1) * g
        vn = ADAM_B2 * v_ref[...] + (1.0 - ADAM_B2) * (g * g)
        m_hat = mn / (1.0 - ADAM_B1 ** ADAM_STEP)
        v_hat = vn / (1.0 - ADAM_B2 ** ADAM_STEP)
        go_ref[...] = g
        d_ref[...] = -ADAM_LR * (m_hat / (jnp.sqrt(v_hat) + ADAM_EPS) + ADAM_WD * wv)
        mo_ref[...] = mn
        vo_ref[...] = vn

    blk = pl.BlockSpec((tr, c), lambda i: (i, 0))
    return pl.pallas_call(
        body, name=name, grid=(r // tr,),
        in_specs=[blk, pl.BlockSpec((ns, tr, c), lambda i: (0, i, 0)), blk, blk],
        out_specs=[blk] * 4, out_shape=[SDS((r, c), F32)] * 4,
        compiler_params=_cparams(("parallel",)),
    )(w, gslots, m, v)


def _sum_slots(slots, name):
    ns, r, c = slots.shape

    def body(g_ref, o_ref):
        g = g_ref[0]
        for s in range(1, ns):
            g = g + g_ref[s]
        o_ref[...] = g

    return pl.pallas_call(
        body, name=name, grid=(1,),
        in_specs=[pl.BlockSpec((ns, r, c), lambda i: (0, 0, 0))], out_specs=pl.BlockSpec((r, c), lambda i: (0, 0)),
        out_shape=SDS((r, c), F32), compiler_params=_cparams(("arbitrary",)),
    )(slots)


def _adam_params(params, name):
    n = len(params)

    def body(*refs):
        ins, outs = refs[:5 * n], refs[5 * n:]
        for p in range(n):
            w_ref, m_ref, v_ref, g_first, g_rest = ins[5 * p:5 * p + 5]
            go_ref, d_ref, mo_ref, vo_ref = outs[4 * p:4 * p + 4]
            for part, g_ref in ((slice(0, 1), g_first), (slice(1, w_ref.shape[0]), g_rest)):
                g = g_ref[...]
                mn = ADAM_B1 * m_ref[part] + (1.0 - ADAM_B1) * g
                vn = ADAM_B2 * v_ref[part] + (1.0 - ADAM_B2) * (g * g)
                m_hat = mn / (1.0 - ADAM_B1 ** ADAM_STEP)
                v_hat = vn / (1.0 - ADAM_B2 ** ADAM_STEP)
                go_ref[part] = g
                d_ref[part] = -ADAM_LR * (m_hat / (jnp.sqrt(v_hat) + ADAM_EPS) + ADAM_WD * w_ref[part])
                mo_ref[part] = mn
                vo_ref[part] = vn

    def whole(a):
        return pl.BlockSpec(a.shape, lambda i, nd=a.ndim: (0,) * nd)

    flat = _in_hbm([a for prm in params for a in prm])
    outs = pl.pallas_call(
        body, name=name, grid=(1,),
        in_specs=[whole(a) for a in flat],
        out_specs=[whole(prm[0]) for prm in params for _ in range(4)],
        out_shape=[SDS(prm[0].shape, F32) for prm in params for _ in range(4)],
        compiler_params=_cparams(("arbitrary",)),
    )(*flat)
    return [outs[4 * p:4 * p + 4] for p in range(n)]


SMALL_NAMES = ("pre_norm", "post_norm", "ssm_a_re", "ssm_a_im", "ssm_log_dt", "ssm_b_re", "ssm_b_im", "ssm_c_re",
               "ssm_c_im", "ssm_d", "ssm_glu_b")
SSM_NAMES = ("ssm_a_re", "ssm_a_im", "ssm_log_dt", "ssm_b_re", "ssm_b_im", "ssm_c_re", "ssm_c_im")
WEIGHT_ORDER = ("pre_norm", "post_norm", "even_w_in", "even_w_out", "ssm_a_re", "ssm_a_im", "ssm_log_dt", "ssm_b_re",
                "ssm_b_im", "ssm_c_re", "ssm_c_im", "ssm_d", "ssm_glu_w", "ssm_glu_b", "odd_w_in", "pool_w",
                "pool_scale", "odd_w_out")
PACK_ROWS_ALIGN = 8


def _pack(parts):
    flat = jnp.concatenate([p.reshape(-1).astype(F32) for p in parts])
    rows = -(-flat.shape[0] // (LANES * PACK_ROWS_ALIGN)) * PACK_ROWS_ALIGN
    return jnp.pad(flat, (0, rows * LANES - flat.shape[0])).reshape(rows, LANES)


def _unpack(packed, shapes):
    flat = packed.reshape(-1)
    out, off = [], 0
    for shp in shapes:
        size = math.prod(shp)
        out.append(flat[off:off + size].reshape(shp))
        off += size
    return out


EVEN_SHARDED = ("w_in", "w_out", "glu_w")
ODD_SHARDED = ("w_in", "pool_w", "w_out")
FAMILY = {(0, "w_in"): "even_w_in", (0, "w_out"): "even_w_out", (0, "glu_w"): "ssm_glu_w",
          (1, "w_in"): "odd_w_in", (1, "pool_w"): "pool_w", (1, "w_out"): "odd_w_out"}


def _sharded_keys(layer):
    return EVEN_SHARDED if layer % 2 == 0 else ODD_SHARDED


def _local_step(x, tgt, small, get_weights, on_w, on_ssm, on_grads, zero=0.0):
    tables = _rope_tables(zero) + (_attention_bias(zero),)
    preps, prep_vjps = [], []
    for i in range(2):
        out, vjp = jax.vjp(_ssm_prep, small["ssm_a_re"][i] + zero, small["ssm_a_im"][i], small["ssm_log_dt"][i],
                           small["ssm_b_re"][i], small["ssm_b_im"][i], small["ssm_c_re"][i], small["ssm_c_im"][i])
        preps.append(out)
        prep_vjps.append(vjp)

    def layer_args(layer, wts):
        i = layer // 2
        pre, post = _row(small["pre_norm"][layer]) + wts.get("token", 0.0), _row(small["post_norm"][layer])
        if layer % 2 == 0:
            return (pre, post, wts["w_in"], wts["late"], _row(small["ssm_glu_b"][i]), _row(small["ssm_d"][i]),
                    preps[i], tables)
        return (pre, post, wts["w_in"], wts["pool_w"], _row(wts["pool_scale"]), wts["w_out"])

    saved, args = [], []
    cur = x
    for layer in range(4):
        after = (cur,) if layer else (cur, tables[0], tables[3], preps[0][1], preps[0][2], preps[1][1], preps[1][2])
        args.append(layer_args(layer, get_weights(layer, after)))
        if layer == 0:
            h = _norm_fwd(cur, args[0][0])
        if layer < 3:
            def tail(xv, yv, post, next_gain=_row(small["pre_norm"][layer + 1])):
                return tuple(_post_fwd(xv, yv, post, next_gain))
        else:
            def tail(xv, yv, post):
                return tuple(_post_fwd_loss(xv, yv, post, tgt))
        cur, h, sv = (_even_fwd if layer % 2 == 0 else _odd_fwd)(cur, h, tail, *args[layer])
        saved.append(sv)
    g, sq = cur, h
    loss = 0.5 * jnp.sum(sq) / D

    lg = [None] * 4
    token = jnp.zeros((), F32)
    for layer in reversed(range(4)):
        largs = list(args[layer])
        largs[1] = largs[1] + token
        hooks = dict(on_w=functools.partial(on_w, layer))
        ssm_grads = []
        if layer % 2 == 0:
            def ssm_hook(cotangents, layer=layer):
                ssm_grads.append(prep_vjps[layer // 2](cotangents))
                return on_ssm(layer, ssm_grads[0])

            hooks["on_ssm"] = ssm_hook
        g, lg[layer] = (_even_bwd if layer % 2 == 0 else _odd_bwd)(g, saved[layer], *largs, **hooks)
        if ssm_grads:
            lg[layer]["ssm"] = ssm_grads[0]
        token = on_grads(layer, lg[layer])
    return loss, g, token


def _to_slots(key, gfull):
    if key == "w_in":
        return gfull
    if key in ("w_out", "glu_w"):
        rr, nn = gfull.shape
        return gfull.reshape(N_DEV, rr // N_DEV, nn)
    assert key == "pool_w"
    gg, rr, nn = gfull.shape
    return gfull.reshape(gg, N_DEV, rr // N_DEV, nn).transpose(1, 0, 2, 3)


def _from_gathered(key, gat):
    if key == "w_in":
        return gat
    if key in ("w_out", "glu_w"):
        _, rr, nn = gat.shape
        return gat.reshape(N_DEV * rr, nn)
    assert key == "pool_w"
    _, gg, rr, nn = gat.shape
    return gat.transpose(1, 0, 2, 3).reshape(gg, N_DEV * rr, nn)


def kernel(x, pre_norm, post_norm, even_w_in, even_w_out, ssm_a_re, ssm_a_im, ssm_log_dt, ssm_b_re, ssm_b_im, ssm_c_re, ssm_c_im, ssm_d, ssm_glu_w, ssm_glu_b, odd_w_in, pool_w, pool_scale, odd_w_out, loss_target, m_pre_norm, m_post_norm, m_even_w_in, m_even_w_out, m_ssm_a_re, m_ssm_a_im, m_ssm_log_dt, m_ssm_b_re, m_ssm_b_im, m_ssm_c_re, m_ssm_c_im, m_ssm_d, m_ssm_glu_w, m_ssm_glu_b, m_odd_w_in, m_pool_w, m_pool_scale, m_odd_w_out, v_pre_norm, v_post_norm, v_even_w_in, v_even_w_out, v_ssm_a_re, v_ssm_a_im, v_ssm_log_dt, v_ssm_b_re, v_ssm_b_im, v_ssm_c_re, v_ssm_c_im, v_ssm_d, v_ssm_glu_w, v_ssm_glu_b, v_odd_w_in, v_pool_w, v_pool_scale, v_odd_w_out):
    w = dict(pre_norm=pre_norm, post_norm=post_norm, even_w_in=even_w_in, even_w_out=even_w_out, ssm_a_re=ssm_a_re,
             ssm_a_im=ssm_a_im, ssm_log_dt=ssm_log_dt, ssm_b_re=ssm_b_re, ssm_b_im=ssm_b_im, ssm_c_re=ssm_c_re,
             ssm_c_im=ssm_c_im, ssm_d=ssm_d, ssm_glu_w=ssm_glu_w, ssm_glu_b=ssm_glu_b, odd_w_in=odd_w_in,
             pool_w=pool_w, pool_scale=pool_scale, odd_w_out=odd_w_out)
    mom = dict(pre_norm=m_pre_norm, post_norm=m_post_norm, even_w_in=m_even_w_in, even_w_out=m_even_w_out,
               ssm_a_re=m_ssm_a_re, ssm_a_im=m_ssm_a_im, ssm_log_dt=m_ssm_log_dt, ssm_b_re=m_ssm_b_re,
               ssm_b_im=m_ssm_b_im, ssm_c_re=m_ssm_c_re, ssm_c_im=m_ssm_c_im, ssm_d=m_ssm_d, ssm_glu_w=m_ssm_glu_w,
               ssm_glu_b=m_ssm_glu_b, odd_w_in=m_odd_w_in, pool_w=m_pool_w, pool_scale=m_pool_scale,
               odd_w_out=m_odd_w_out)
    var = dict(pre_norm=v_pre_norm, post_norm=v_post_norm, even_w_in=v_even_w_in, even_w_out=v_even_w_out,
               ssm_a_re=v_ssm_a_re, ssm_a_im=v_ssm_a_im, ssm_log_dt=v_ssm_log_dt, ssm_b_re=v_ssm_b_re,
               ssm_b_im=v_ssm_b_im, ssm_c_re=v_ssm_c_re, ssm_c_im=v_ssm_c_im, ssm_d=v_ssm_d, ssm_glu_w=v_ssm_glu_w,
               ssm_glu_b=v_ssm_glu_b, odd_w_in=v_odd_w_in, pool_w=v_pool_w, pool_scale=v_pool_scale,
               odd_w_out=v_odd_w_out)
    me = _my_index()
    scale_cols = pool_scale.shape[1]

    def start_gather(tag, layer, keys, after=()):
        i = layer // 2
        shards = [w[FAMILY[(layer % 2, k)]][i].astype(BF16) for k in keys]
        if layer % 2 == 1:
            shards.append(jnp.pad(pool_scale[i][None], ((0, PACK_ROWS_ALIGN - 1), (0, 0))))
        return _xchg_start(f"gather_start_{tag}", shards, True, after)

    gather_started = {0: start_gather("0", 0, EVEN_SHARDED[:1])}
    small = {nm: w[nm] for nm in SMALL_NAMES}

    def get_weights(layer, after):
        keys = EVEN_SHARDED[:1] if layer == 0 else _sharded_keys(layer)
        lands = _xchg_wait(f"gather_wait_{layer}", gather_started[layer], True, after)
        wts = {k: _from_gathered(k, gat) for k, gat in zip(keys, lands)}
        if layer % 2 == 1:
            wts["pool_scale"] = lands[-1][:, 0, :].reshape(N_DEV * scale_cols)
        if layer == 0:
            prev = gather_started["0_late"] = start_gather("0_late", 0, EVEN_SHARDED[1:], after=(lands[0],))
            for later in (1, 2, 3):
                prev = gather_started[later] = start_gather(str(later), later, _sharded_keys(later), after=(prev[4],))
            wts["token"] = sum(gather_started[tag][4][0, 0] for tag in ("0_late", 1, 2, 3))

            def late(after_late):
                late_lands = _xchg_wait("gather_wait_0_late", gather_started["0_late"], True, (after_late,))
                return tuple(_from_gathered(k, gat) for k, gat in zip(EVEN_SHARDED[1:], late_lands))

            wts["late"] = late
        elif layer == 2:
            wts["late"] = lambda after_late: (wts["w_out"], wts["glu_w"])
        return wts

    scatter_started = []

    def on_w(layer, gw):
        keys = tuple(k for k in _sharded_keys(layer) if k in gw)
        started = _xchg_start(f"scatter_start_{layer}_{keys[0]}", [_to_slots(k, gw[k]) for k in keys], False)
        scatter_started.append((layer, keys, started))
        return started[4]

    def wait_scatters(layers, after):
        for layer, keys, started in scatter_started:
            if layer in layers:
                lands = _xchg_wait(f"scatter_wait_{layer}_{keys[0]}", started, False, after)
                for k, land in zip(keys, lands):
                    recv[(layer, k)] = land

    packed_names = ("pre_norm", "post_norm") + SSM_NAMES + ("ssm_d", "ssm_glu_b")
    tails = {nm: (SSM_GROUPS, SSM_STATE * SSM_GROUP) if nm in ("ssm_b_re", "ssm_b_im") else w[nm].shape[1:]
             for nm in packed_names}

    layer_grads = {}
    early_started, mid_started = [], []

    def on_ssm(layer, ssm_grads):
        if layer != 0:
            return None
        mid_started.append(_xchg_start("mid_start", [_pack(list(ssm_grads))], True))
        return mid_started[0][4]

    def on_grads(layer, lg):
        layer_grads[layer] = lg
        zero = jnp.zeros((), F32)
        if layer == 1:
            lgs = layer_grads
            early = ([jnp.concatenate([lgs[l][k] for l in (1, 2, 3)], axis=0) for k in ("pre", "post")]
                     + list(lgs[2]["ssm"]) + [lgs[2]["ssm_d"], lgs[2]["glu_b"],
                                              jnp.concatenate([lgs[1]["pool_scale"], lgs[3]["pool_scale"]], axis=0)])
            early_started.append(_xchg_start("small_start", [_pack(early)], True))
            zero = zero + early_started[0][4][0, 0]
        return zero

    loss_local, grad_x, token = _local_step(x[0], loss_target[0], small, get_weights, on_w, on_ssm, on_grads,
                                            zero=gather_started[0][4][0, 0])

    lg0 = layer_grads[0]
    late_started = _xchg_start("late_start", [_pack([lg0["pre"], lg0["post"], lg0["ssm_d"], lg0["glu_b"],
                                                     loss_local.reshape(1)]) + token], True)

    def adam_family(parity, k, which, into=None):
        nm = FAMILY[(parity, k)]
        cols = w[nm].shape[-1]
        return _adam_layer(w[nm].reshape(2, -1, cols), recv[(parity + 2 * which, k)].reshape(N_DEV, -1, cols),
                           mom[nm].reshape(2, -1, cols), var[nm].reshape(2, -1, cols), which,
                           f"adam_{nm}_{which}", into)

    recv, res = {}, {}
    wait_scatters((3, 2, 1), (late_started[4],))
    for k in ODD_SHARDED:
        res[FAMILY[(1, k)]] = adam_family(1, k, 1, adam_family(1, k, 0))
    half_done = {k: adam_family(0, k, 1) for k in EVEN_SHARDED}
    odd_done = tuple(half_done[k][0] for k in EVEN_SHARDED)

    (early_slots,) = _xchg_wait("small_wait", early_started[0], True, odd_done)
    (mid_slots,) = _xchg_wait("mid_wait", mid_started[0], True, odd_done)
    early_shapes = [(w[nm].shape[0] - 1,) + tails[nm] for nm in packed_names] + [(2, N_DEV * scale_cols)]
    g_early = _unpack(_sum_slots(early_slots, "sum_small_early"), early_shapes)
    g_mid = _unpack(_sum_slots(mid_slots, "sum_small_mid"), [(1,) + tails[nm] for nm in SSM_NAMES])

    (late_slots,) = _xchg_wait("late_wait", late_started, True, (g_early[0], g_mid[0]))
    wait_scatters((0,), (late_slots,))
    for k in EVEN_SHARDED:
        res[FAMILY[(0, k)]] = adam_family(0, k, 0, half_done[k])
    for nm in FAMILY.values():
        res[nm] = [o.reshape(w[nm].shape) for o in res[nm]]

    late_names = ("pre_norm", "post_norm", "ssm_d", "ssm_glu_b")
    g_late = _unpack(_sum_slots(late_slots, "sum_small_late"), [(1,) + tails[nm] for nm in late_names] + [(1,)])
    g_first = dict(zip(late_names, g_late))
    g_first.update(zip(SSM_NAMES, g_mid))
    dense = lambda nm, a: a.reshape((a.shape[0],) + tails[nm])
    outs = _adam_params([(dense(nm, w[nm]), dense(nm, mom[nm]), dense(nm, var[nm]), g_first[nm], g_early[j])
                         for j, nm in enumerate(packed_names)], "adam_small")
    for nm, four in zip(packed_names, outs):
        res[nm] = [o.reshape(w[nm].shape) for o in four]
    loss = g_late[-1].reshape(())
    g_scale = lax.dynamic_slice_in_dim(g_early[-1], me * scale_cols, scale_cols, axis=1)
    pad = ((0, PACK_ROWS_ALIGN - 2), (0, 0))
    outs = _adam(jnp.pad(pool_scale, pad), jnp.pad(g_scale, pad)[None], jnp.pad(m_pool_scale, pad),
                 jnp.pad(v_pool_scale, pad), name="adam_pool_scale")
    res["pool_scale"] = [o[:2] for o in outs]

    out = [loss, grad_x[None]]
    for kind in range(4):
        out += [res[nm][kind] for nm in WEIGHT_ORDER]
    return tuple(out)
```

```python
import functools
import math

import jax
import jax.numpy as jnp
from jax import lax
from jax.experimental import pallas as pl
from jax.experimental.pallas import tpu as pltpu

F32 = jnp.float32
BF16 = jnp.bfloat16
SDS = jax.ShapeDtypeStruct

N_DEV = 8
S = 2048
D = 1024
HEAD_DIM = 64
ROT_DIM = 16
ROPE_THETA = 500000.0
ATT_W = 1024
SSM_W = 512
SSM_GROUPS = 32
SSM_GROUP = 16
SSM_STATE = 64
N_CPLX = SSM_GROUPS * SSM_STATE
POOL_W = 2048
POOL_GROUP = 512
EVEN_IN = 5120
EVEN_OUT = 1536
ODD_IN = 4096
RMS_EPS = 1e-6
LANES = 128
VMEM_LIMIT = 48 * 1024 * 1024

ADAM_LR = 0.001
ADAM_B1 = 0.9
ADAM_B2 = 0.999
ADAM_EPS = 1e-08
ADAM_WD = 0.01
ADAM_STEP = 10

MESH_ID = pl.DeviceIdType.MESH
NN = (((1,), (0,)), ((), ()))
NT = (((1,), (1,)), ((), ()))
TN = (((0,), (0,)), ((), ()))
_DN = {"nn": NN, "nt": NT, "tn": TN}


def _cparams(sem):
    return pltpu.CompilerParams(dimension_semantics=sem, vmem_limit_bytes=VMEM_LIMIT)


def _in_hbm(arrs):
    return [pltpu.with_memory_space_constraint(a, pltpu.HBM) for a in arrs]


MM_TILES = (1024, 768, 512)


def _tile(dim):
    return next((t for t in MM_TILES if dim % t == 0), dim)


BLOCK_PAIR = 2
MAX_WHOLE_K = 2048


def _mm(a, b, mode, out_dtype, b_blocks=False, out_blocks=False, after=()):
    if b_blocks:
        nblk, rows, cb = b.shape
        b2_shape = (rows, nblk * cb)
    else:
        b2_shape = b.shape
    if mode == "nn":
        (m, k), n = a.shape, b2_shape[1]
    elif mode == "nt":
        (m, k), n = a.shape, b2_shape[0]
    else:
        (k, m), n = a.shape, b2_shape[1]
    tm, tn, tk = _tile(m), _tile(n), _tile(k)
    if k <= MAX_WHOLE_K:
        tk = k
    if b_blocks and mode == "nn":
        tn = BLOCK_PAIR * cb
        if tn <= MM_TILES[0]:
            tm = m
    if b_blocks and mode == "nt":
        tk = BLOCK_PAIR * cb
    if out_blocks:
        cb = n // N_DEV
        tn = BLOCK_PAIR * cb
        tk = k
    nk = k // tk

    def body(a_ref, b_ref, *rest):
        o_ref, acc_ref = rest[-2:]
        kk = pl.program_id(2)
        bv = jnp.concatenate([b_ref[p] for p in range(BLOCK_PAIR)], axis=1) if b_blocks else b_ref[...]
        part = lax.dot_general(a_ref[...].astype(BF16), bv.astype(BF16), _DN[mode], preferred_element_type=F32)

        def write(res):
            if out_blocks:
                for p in range(BLOCK_PAIR):
                    o_ref[p] = res[:, p * cb:(p + 1) * cb].astype(o_ref.dtype)
            else:
                o_ref[...] = res.astype(o_ref.dtype)

        if nk == 1:
            write(part)
            return

        @pl.when(kk == 0)
        def _():
            acc_ref[...] = part

        @pl.when((kk > 0) & (kk < nk - 1))
        def _():
            acc_ref[...] += part

        @pl.when(kk == nk - 1)
        def _():
            write(acc_ref[...] + part)

    if mode == "nn":
        a_spec = pl.BlockSpec((tm, tk), lambda i, j, kk: (i, kk))
        b_spec = pl.BlockSpec((tk, tn), lambda i, j, kk: (kk, j))
    elif mode == "nt":
        a_spec = pl.BlockSpec((tm, tk), lambda i, j, kk: (i, kk))
        b_spec = pl.BlockSpec((tn, tk), lambda i, j, kk: (j, kk))
    else:
        a_spec = pl.BlockSpec((tk, tm), lambda i, j, kk: (kk, i))
        b_spec = pl.BlockSpec((tk, tn), lambda i, j, kk: (kk, j))
    if b_blocks and mode == "nn":
        b_spec = pl.BlockSpec((BLOCK_PAIR, tk, cb), lambda i, j, kk: (j, kk, 0))
    if b_blocks and mode == "nt":
        b_spec = pl.BlockSpec((BLOCK_PAIR, tn, cb), lambda i, j, kk: (kk, j, 0))
    out_spec = pl.BlockSpec((tm, tn), lambda i, j, kk: (i, j))
    out_shape = SDS((m, n), out_dtype)
    if out_blocks:
        out_spec = pl.BlockSpec((BLOCK_PAIR, tm, cb), lambda i, j, kk: (j, i, 0))
        out_shape = SDS((N_DEV, m, cb), out_dtype)
    return pl.pallas_call(
        body, name=f"mm_{mode}_{m}x{k}x{n}",
        grid=(m // tm, n // tn, nk),
        in_specs=[a_spec, b_spec] + [pl.BlockSpec(memory_space=pl.ANY)] * len(after),
        out_specs=out_spec,
        out_shape=out_shape,
        scratch_shapes=[pltpu.VMEM((tm, tn) if nk > 1 else (8, LANES), F32)],
        compiler_params=_cparams(("parallel", "parallel", "arbitrary")),
    )(a, b, *after)


def _gmm(a, b, mode, out_dtype, tm=S):
    ng, gw = POOL_W // POOL_GROUP, POOL_GROUP
    ns = S // tm
    if mode in ("nn", "nt"):
        def body(a_ref, b_ref, o_ref):
            o_ref[...] = lax.dot_general(a_ref[...].astype(BF16), b_ref[...].astype(BF16), _DN[mode],
                                         preferred_element_type=F32).astype(o_ref.dtype)

        return pl.pallas_call(
            body, name=f"gmm_{mode}", grid=(ng, ns),
            in_specs=[pl.BlockSpec((tm, gw), lambda g, i: (i, g)),
                      pl.BlockSpec((None, gw, gw), lambda g, i: (g, 0, 0))],
            out_specs=pl.BlockSpec((tm, gw), lambda g, i: (i, g)),
            out_shape=SDS((S, POOL_W), out_dtype),
            compiler_params=_cparams(("parallel", "parallel")),
        )(a, b)

    def body_tn(a_ref, b_ref, o_ref, acc_ref):
        i = pl.program_id(1)

        @pl.when(i == 0)
        def _():
            acc_ref[...] = jnp.zeros_like(acc_ref)

        acc_ref[...] += lax.dot_general(a_ref[...].astype(BF16), b_ref[...].astype(BF16), TN,
                                        preferred_element_type=F32)

        @pl.when(i == ns - 1)
        def _():
            o_ref[...] = acc_ref[...].astype(o_ref.dtype)

    return pl.pallas_call(
        body_tn, name="gmm_tn", grid=(ng, ns),
        in_specs=[pl.BlockSpec((tm, gw), lambda g, i: (i, g)),
                  pl.BlockSpec((tm, gw), lambda g, i: (i, g))],
        out_specs=pl.BlockSpec((None, gw, gw), lambda g, i: (g, 0, 0)),
        out_shape=SDS((ng, gw, gw), out_dtype),
        scratch_shapes=[pltpu.VMEM((gw, gw), F32)],
        compiler_params=_cparams(("parallel", "arbitrary")),
    )(a, b)


def _rowwise(fn, inputs, out_defs, acc_defs=(), tm=512, name=None, after=()):
    n_in, n_out, n_acc = len(inputs), len(out_defs), len(acc_defs)
    n_after = len(after)
    in_specs, args = [], []
    for arr, width, cb in inputs:
        if arr.shape[0] != S:
            in_specs.append(pl.BlockSpec((arr.shape[0], width), lambda i, cb=cb: (0, cb)))
        else:
            in_specs.append(pl.BlockSpec((tm, width), lambda i, cb=cb: (i, cb)))
        args.append(arr)
    out_defs = [d if len(d) == 4 else (d[0], d[1], d[0], 0) for d in out_defs]
    out_shape = [SDS((S, ww), dt) for _, dt, ww, _ in out_defs] + [SDS((1, w), F32) for w in acc_defs]
    out_specs = ([pl.BlockSpec((tm, w), lambda i, cb=cb: (i, cb)) for w, _, _, cb in out_defs]
                 + [pl.BlockSpec((1, w), lambda i: (0, 0)) for w in acc_defs])

    def kern(*refs):
        vals = [r[...] for r in refs[:n_in]]
        outs, accs = fn(*vals)
        out_refs = refs[n_in + n_after:]
        for r, v in zip(out_refs[:n_out], outs):
            r[...] = v.astype(r.dtype)
        if n_acc:
            acc_refs = out_refs[n_out:]

            @pl.when(pl.program_id(0) == 0)
            def _():
                for r in acc_refs:
                    r[...] = jnp.zeros_like(r)

            for r, v in zip(acc_refs, accs):
                r[...] += jnp.sum(v, axis=0, keepdims=True)

    res = pl.pallas_call(
        kern, name=name, grid=(S // tm,), in_specs=in_specs + [pl.BlockSpec(memory_space=pl.ANY)] * n_after,
        out_specs=out_specs, out_shape=out_shape, compiler_params=_cparams(("arbitrary",)),
    )(*args, *after)
    return res


def _sigmoid(x):
    return 1.0 / (1.0 + jnp.exp(-x))


def _silu_and_grad(x):
    s = _sigmoid(x)
    return x * s, s * (1.0 + x * (1.0 - s))


_GELU_K = math.sqrt(2.0 / math.pi)
_GELU_C = 0.044715


def _gelu_and_grad(x):
    t = jnp.tanh(_GELU_K * (x + _GELU_C * (x * x * x)))
    cdf = 0.5 * (1.0 + t)
    grad = cdf + 0.5 * x * (1.0 - t * t) * (_GELU_K * (1.0 + 3.0 * _GELU_C * x * x))
    return x * cdf, grad


def _rms(xv, gain):
    r = lax.rsqrt(jnp.mean(xv * xv, axis=-1, keepdims=True) + RMS_EPS)
    return xv * r * gain


def _rms_bwd(dout, xv, gain):
    r = lax.rsqrt(jnp.mean(xv * xv, axis=-1, keepdims=True) + RMS_EPS)
    xhat = xv * r
    dxhat = dout * gain
    dx = r * (dxhat - xhat * jnp.mean(dxhat * xhat, axis=-1, keepdims=True))
    return dx, dout * xhat


def _norm_fwd(x, gain):
    (h,) = _rowwise(lambda xv, g: ((_rms(xv, g),), ()), [(x, D, 0), (gain, D, 0)], [(D, BF16)], name="norm_fwd")
    return h


def _post_fwd(x, y, gain, next_gain):
    def fn(xv, yv, g, gn):
        out = xv + _rms(yv, g)
        return (out, _rms(out, gn)), ()

    return _rowwise(fn, [(x, D, 0), (y, D, 0), (gain, D, 0), (next_gain, D, 0)], [(D, F32), (D, BF16)],
                    name="post_fwd")


def _post_fwd_loss(x, y, gain, tgt):
    def fn(xv, yv, g, tv):
        e = xv + _rms(yv, g) - tv
        return (e * (1.0 / D),), (e * e,)

    return _rowwise(fn, [(x, D, 0), (y, D, 0), (gain, D, 0), (tgt, D, 0)], [(D, F32)], [D], name="post_fwd_loss")


def _post_bwd(g, y, gain):
    def fn(gv, yv, gn):
        dx, dg = _rms_bwd(gv, yv, gn)
        return (dx,), (dg,)

    return _rowwise(fn, [(g, D, 0), (y, D, 0), (gain, D, 0)], [(D, BF16)], [D], name="post_bwd")


def _pre_bwd(g, dh, x, gain):
    def fn(gv, dhv, xv, gn):
        dx, dg = _rms_bwd(dhv, xv, gn)
        return (gv + dx,), (dg,)

    return _rowwise(fn, [(g, D, 0), (dh, D, 0), (x, D, 0), (gain, D, 0)], [(D, F32)], [D], name="pre_bwd")


def _pool(u_arr, col_block, transpose, out_dtype, into=None, tc=256):
    n_t = POOL_W // tc
    per_group = POOL_GROUP // tc

    def body(u_ref, *rest):
        o_ref = rest[-1]
        grp = pl.program_id(0) // per_group
        t = lax.broadcasted_iota(jnp.int32, (S, 1), 0)
        for g in range(POOL_W // POOL_GROUP):
            @pl.when(grp == g)
            def _(g=g):
                xv = u_ref[...]
                cnt = jnp.minimum(t + 1, 2 << g).astype(F32)
                cur = xv / cnt if transpose else xv
                for k in (1, 2, 4, 8)[:g + 1]:
                    if transpose:
                        cur = cur + jnp.where(t < S - k, pltpu.roll(cur, S - k, 0), 0.0)
                    else:
                        cur = cur + jnp.where(t >= k, pltpu.roll(cur, k, 0), 0.0)
                res = cur - xv if transpose else cur / cnt - xv
                o_ref[...] = res.astype(o_ref.dtype)

    in_specs = [pl.BlockSpec((S, tc), lambda c: (0, col_block * n_t + c))]
    args = [u_arr]
    if into is not None:
        in_specs.append(pl.BlockSpec(memory_space=pl.ANY))
        args.append(into)
    return pl.pallas_call(
        body, name="pool_bwd" if transpose else "pool_fwd", grid=(n_t,),
        in_specs=in_specs,
        out_specs=pl.BlockSpec((S, tc), lambda c: (0, c)),
        out_shape=SDS((S, POOL_W) if into is None else into.shape, out_dtype),
        input_output_aliases={} if into is None else {1: 0},
        compiler_params=_cparams(("parallel",)),
    )(*args)


def _rope_tables(zero):
    pos = jnp.arange(S, dtype=jnp.int32).astype(F32) + zero
    inv_freq = ROPE_THETA ** (-jnp.arange(0, ROT_DIM, 2, dtype=F32) / ROT_DIM)
    ang = pos[:, None] * inv_freq[None, :]
    cos8, sin8 = jnp.cos(ang), jnp.sin(ang)
    half = ROT_DIM // 2
    zeros = jnp.zeros((S, HEAD_DIM - ROT_DIM), F32)
    cos = jnp.concatenate([cos8, cos8, jnp.ones((S, HEAD_DIM - ROT_DIM), F32)], axis=1)
    sin = jnp.concatenate([-sin8, sin8, zeros], axis=1)
    rep = LANES // HEAD_DIM
    lane = jnp.arange(LANES)
    dim = lane % HEAD_DIM
    partner = jnp.where(dim < half, lane + half, jnp.where(dim < ROT_DIM, lane - half, -1))
    swap = (lane[:, None] == partner[None, :]).astype(BF16)
    return jnp.tile(cos, (1, rep)), jnp.tile(sin, (1, rep)), swap


def _rotate(xv, cos, sin, swap, transpose):
    rep = xv.shape[1] // LANES
    wide = lambda tab: jnp.concatenate([tab] * rep, axis=1)
    xb = xv.astype(BF16)
    partner = jnp.concatenate([lax.dot_general(xb[:, t * LANES:(t + 1) * LANES], swap, NN, preferred_element_type=F32)
                               for t in range(rep)], axis=1)
    mixed = partner * wide(sin)
    return xv * wide(cos) - mixed if transpose else xv * wide(cos) + mixed


def _qkv_prep(proj, tables):
    cos, sin, swap = tables

    def fn(x, c, s, sw):
        rot = _rotate(x[:, :2 * ATT_W], c, s, sw, False)
        return (jnp.concatenate([(rot[:, :ATT_W] * HEAD_DIM ** -0.5).astype(BF16), rot[:, ATT_W:].astype(BF16),
                                 x[:, 2 * ATT_W:].astype(BF16)], axis=1),), ()

    (qkv,) = _rowwise(fn, [(proj, 3 * ATT_W, 0), (cos, LANES, 0), (sin, LANES, 0), (swap, LANES, 0)],
                      [(3 * ATT_W, BF16)], name="qkv_prep")
    return qkv


ATT_T = 512


def _multiplicity(delta):
    ok = delta >= 0
    near = jnp.where(ok & (delta <= 128), 1.0, 0.0)
    mid = jnp.where(ok & (delta <= 512) & ((delta & 3) == 0), 1.0, 0.0)
    far = jnp.where(ok & ((delta & 15) == 0), 1.0, 0.0)
    return near + mid + far


def _attention_bias(zero):
    t = ATT_T
    pos = jnp.arange(t, dtype=jnp.int32) + jnp.asarray(zero).astype(jnp.int32)
    delta = jnp.arange(S // t, dtype=jnp.int32)[:, None, None] * t + pos[None, :, None] - pos[None, None, :]
    mult = _multiplicity(delta)
    return jnp.where(mult > 0.0, jnp.log(jnp.maximum(mult, 1.0)), -1e30).astype(F32)


def _head_split(v, first):
    zero = jnp.zeros_like(v)
    return [jnp.where(first, v, zero), jnp.where(first, zero, v)]


def _flash_fwd(qkv, bias):
    t = ATT_T
    n_hp = ATT_W // LANES

    def body(q_ref, k_ref, v_ref, b_ref, o_ref, lse_ref):
        i = pl.program_id(1)
        first = lax.broadcasted_iota(jnp.int32, (1, LANES), 1) < HEAD_DIM
        qs = _head_split(q_ref[...], first)

        def kv_step(j, carry):
            m0, l0, m1, l1, acc = carry
            off = pl.multiple_of(j * t, t)
            kb = k_ref[pl.ds(off, t), :]
            vs = _head_split(v_ref[pl.ds(off, t), :], first)
            bias_t = b_ref[i - j]
            new = []
            pv = None
            for h, (m_prev, l_prev) in enumerate(((m0, l0), (m1, l1))):
                s = lax.dot_general(qs[h], kb, NT, preferred_element_type=F32) + bias_t
                m_new = jnp.maximum(m_prev, jnp.max(s, axis=1, keepdims=True))
                p = jnp.exp(s - m_new)
                alpha = jnp.exp(m_prev - m_new)
                l_new = alpha * l_prev + jnp.sum(p, axis=1, keepdims=True)
                d = lax.dot_general(p.astype(BF16), vs[h], NN, preferred_element_type=F32)
                pv = d if pv is None else pv + d
                new.append((m_new, l_new, alpha))
            acc = acc * jnp.where(first, new[0][2], new[1][2]) + pv
            return new[0][0], new[0][1], new[1][0], new[1][1], acc

        neg = jnp.full((t, 1), -1e30, F32)
        zero = jnp.zeros((t, 1), F32)
        m0, l0, m1, l1, acc = lax.fori_loop(0, i + 1, kv_step, (neg, zero, neg, zero, jnp.zeros((t, LANES), F32)))
        o_ref[...] = acc * jnp.where(first, 1.0 / l0, 1.0 / l1)
        lse_ref[...] = jnp.where(first, m0 + jnp.log(l0), m1 + jnp.log(l1))

    blk = pl.BlockSpec((t, LANES), lambda hp, i: (i, hp))
    k_full = pl.BlockSpec((S, LANES), lambda hp, i: (0, n_hp + hp))
    v_full = pl.BlockSpec((S, LANES), lambda hp, i: (0, 2 * n_hp + hp))
    return pl.pallas_call(
        body, name="flash_fwd", grid=(n_hp, S // t),
        in_specs=[blk, k_full, v_full, pl.BlockSpec((S // t, t, t), lambda hp, i: (0, 0, 0))], out_specs=[blk, blk],
        out_shape=[SDS((S, ATT_W), F32), SDS((S, ATT_W), F32)],
        compiler_params=_cparams(("parallel", "arbitrary")),
    )(qkv, qkv, qkv, bias)


def _flash_bwd(qkv, o, do, lse, bias, after=()):
    t = ATT_T
    n_hp = ATT_W // LANES
    n_t = S // t

    def body(q_ref, k_ref, v_ref, o_ref, do_ref, lse_ref, b_ref, *rest):
        dq_ref, dk_ref, dv_ref = rest[-3:]
        j = pl.program_id(1)
        first = lax.broadcasted_iota(jnp.int32, (1, LANES), 1) < HEAD_DIM

        @pl.when(j == 0)
        def _():
            dq_ref[...] = jnp.zeros_like(dq_ref)

        kb = k_ref[...]
        vb = v_ref[...]
        ks = _head_split(kb, first)

        def q_step(i, carry):
            dk_acc, dv_acc = carry
            rows = pl.ds(pl.multiple_of(i * t, t), t)
            qs = _head_split(q_ref[rows, :], first)
            dob = do_ref[rows, :]
            prod = dob * o_ref[rows, :]
            d_all = jnp.sum(prod, axis=1, keepdims=True)
            d0 = jnp.sum(jnp.where(first, prod, 0.0), axis=1, keepdims=True)
            lse_b = lse_ref[rows, :]
            lse0 = jnp.max(jnp.where(first, lse_b, -jnp.inf), axis=1, keepdims=True)
            lse1 = jnp.max(jnp.where(first, -jnp.inf, lse_b), axis=1, keepdims=True)
            dos = _head_split(dob.astype(BF16), first)
            bias_t = b_ref[i - j]
            dq_t = jnp.zeros((t, LANES), F32)
            for h, (lse_h, d_h) in enumerate(((lse0, d0), (lse1, d_all - d0))):
                s = lax.dot_general(qs[h], kb, NT, preferred_element_type=F32)
                p = jnp.exp(s + (bias_t - lse_h))
                dp = lax.dot_general(dos[h], vb, NT, preferred_element_type=F32)
                ds = (p * (dp - d_h)).astype(BF16)
                dv_acc = dv_acc + lax.dot_general(p.astype(BF16), dos[h], TN, preferred_element_type=F32)
                dk_acc = dk_acc + lax.dot_general(ds, qs[h], TN, preferred_element_type=F32)
                dq_t = dq_t + lax.dot_general(ds, ks[h], NN, preferred_element_type=F32)
            dq_ref[rows, :] += dq_t
            return dk_acc, dv_acc

        zero = jnp.zeros((t, LANES), F32)
        dk_acc, dv_acc = lax.fori_loop(j, n_t, q_step, (zero, zero))
        dk_ref[...] = dk_acc
        dv_ref[...] = dv_acc

    blk = pl.BlockSpec((t, LANES), lambda hp, j: (j, hp))
    full = pl.BlockSpec((S, LANES), lambda hp, j: (0, hp))
    k_blk = pl.BlockSpec((t, LANES), lambda hp, j: (j, n_hp + hp))
    v_blk = pl.BlockSpec((t, LANES), lambda hp, j: (j, 2 * n_hp + hp))
    return pl.pallas_call(
        body, name="flash_bwd", grid=(n_hp, n_t),
        in_specs=([full, k_blk, v_blk, full, full, full, pl.BlockSpec((n_t, t, t), lambda hp, j: (0, 0, 0))]
                  + [pl.BlockSpec(memory_space=pl.ANY)] * len(after)),
        out_specs=[full, blk, blk],
        out_shape=[SDS((S, ATT_W), F32)] * 3,
        compiler_params=_cparams(("parallel", "arbitrary")),
    )(qkv, qkv, qkv, o, do, lse, bias, *after)


SCAN_T = 256
SCAN_GROUP = 8
SCAN_STEPS = (1, 2, 4)
ST_ROWS = 2 * N_CPLX // LANES
HALF = ST_ROWS // 2


def _scan_tables(lam_t):
    lam = lax.complex(lam_t[:HALF].reshape(N_CPLX), lam_t[HALF:].reshape(N_CPLX))
    pows = jnp.cumprod(jnp.broadcast_to(lam, (SCAN_GROUP, N_CPLX)), axis=0)
    shifts = jnp.asarray(SCAN_STEPS)
    sub = jnp.arange(SCAN_GROUP)[None, :, None]
    steps = pows[shifts - 1][:, None, :]
    fwd = jnp.concatenate([jnp.where(sub >= shifts[:, None, None], steps, 0.0), pows[None]], axis=0)
    bwd = jnp.concatenate([jnp.where(sub <= SCAN_GROUP - 1 - shifts[:, None, None], jnp.conj(steps), 0.0),
                           jnp.conj(pows)[None, ::-1]], axis=0)

    def pack(tabs):
        return jnp.concatenate([jnp.real(tabs), jnp.imag(tabs)], axis=-1).astype(F32)

    return pack(fwd), pack(bwd)


def _cmul_add(xr, xi, lr, li, sr, si):
    return xr + lr * sr - li * si, xi + lr * si + li * sr


def _group_scan(xr, xi, tab_ref, cr, ci, reverse):
    for j, k in enumerate(SCAN_STEPS):
        shift = SCAN_GROUP - k if reverse else k
        xr, xi = _cmul_add(xr, xi, tab_ref[j, :, :N_CPLX], tab_ref[j, :, N_CPLX:],
                           pltpu.roll(xr, shift, 0), pltpu.roll(xi, shift, 0))
    return _cmul_add(xr, xi, tab_ref[3, :, :N_CPLX], tab_ref[3, :, N_CPLX:],
                     jnp.broadcast_to(cr, (SCAN_GROUP, N_CPLX)), jnp.broadcast_to(ci, (SCAN_GROUP, N_CPLX)))


SSM_SUPER = 4
SB_ROWS = SSM_W // SSM_SUPER
SB_COLS = N_CPLX // SSM_SUPER


def _super_blocks():
    return [(slice(b * SB_ROWS, (b + 1) * SB_ROWS), slice(h * SB_COLS, (h + 1) * SB_COLS),
             slice(h * N_CPLX + b * SB_COLS, h * N_CPLX + (b + 1) * SB_COLS))
            for b in range(SSM_SUPER) for h in range(2)]


def _dot16(a, b, dims):
    return lax.dot_general(a.astype(BF16), b.astype(BF16), dims, preferred_element_type=F32)


def _s5_fwd(tab, u_arr, u_cols, w_b, w_ct):
    nc = N_CPLX

    def body(tab_ref, u_ref, wb_ref, wct_ref, st_ref, y_ref, carry, bu_scr):
        @pl.when(pl.program_id(0) == 0)
        def _():
            carry[...] = jnp.zeros_like(carry)

        for rows_b, cols_c, cols_s in _super_blocks():
            bu_scr[:, cols_s] = _dot16(u_ref[:, rows_b], wb_ref[rows_b, cols_c], NN)

        def group(a, c):
            rows = pl.ds(pl.multiple_of(a * SCAN_GROUP, SCAN_GROUP), SCAN_GROUP)
            xr, xi = _group_scan(bu_scr[rows, :nc], bu_scr[rows, nc:], tab_ref, c[0], c[1], False)
            st_ref[rows, :nc] = xr
            st_ref[rows, nc:] = xi
            return xr[SCAN_GROUP - 1:SCAN_GROUP, :], xi[SCAN_GROUP - 1:SCAN_GROUP, :]

        cr, ci = lax.fori_loop(0, SCAN_T // SCAN_GROUP, group, (carry[:, :nc], carry[:, nc:]), unroll=2)
        carry[:, :nc] = cr
        carry[:, nc:] = ci

        for b in range(SSM_SUPER):
            (rows_b, cols_re, st_re), (_, cols_im, st_im) = _super_blocks()[2 * b:2 * b + 2]
            y_ref[:, rows_b] = (_dot16(st_ref[:, st_re], wct_ref[rows_b, cols_re], NT)
                                + _dot16(st_ref[:, st_im], wct_ref[rows_b, cols_im], NT))

    const = lambda shape: pl.BlockSpec(shape, lambda i: (0,) * len(shape))
    return pl.pallas_call(
        body, name="s5_fwd", grid=(S // SCAN_T,),
        in_specs=[const((4, SCAN_GROUP, 2 * nc)), pl.BlockSpec((SCAN_T, SSM_W), lambda i: (i, u_cols[0] // SSM_W)),
                  const((SSM_W, 2 * SB_COLS)), const((SSM_W, 2 * SB_COLS))],
        out_specs=[pl.BlockSpec((SCAN_T, 2 * nc), lambda i: (i, 0)), pl.BlockSpec((SCAN_T, SSM_W), lambda i: (i, 0))],
        out_shape=[SDS((S, 2 * nc), F32), SDS((S, SSM_W), F32)],
        scratch_shapes=[pltpu.VMEM((1, 2 * nc), F32), pltpu.VMEM((SCAN_T, 2 * nc), F32)],
        compiler_params=_cparams(("arbitrary",)),
    )(tab, u_arr, w_b, w_ct)


def _s5_bwd(tab, dy, states, u_arr, u_cols, w_b, w_ct):
    n_blk = S // SCAN_T
    nc = N_CPLX

    def body(tab_ref, dy_ref, x_ref, u_ref, wb_ref, wct_ref, du_ref, dlam_ref, dwb_ref, dwct_ref,
             carry, acc, d_scr, g_ref):
        i = pl.program_id(0)

        @pl.when(i == 0)
        def _():
            carry[...] = jnp.zeros_like(carry)
            acc[...] = jnp.zeros_like(acc)
            dwb_ref[...] = jnp.zeros_like(dwb_ref)
            dwct_ref[...] = jnp.zeros_like(dwct_ref)

        for rows_b, cols_c, cols_s in _super_blocks():
            d_scr[:, cols_s] = _dot16(dy_ref[:, rows_b], wct_ref[rows_b, cols_c], NN)

        last_row = lax.broadcasted_iota(jnp.int32, (SCAN_GROUP, 1), 0) == SCAN_GROUP - 1
        d_ref = d_scr

        def group(j, c):
            cr, ci = c
            rows = pl.ds(pl.multiple_of((SCAN_T // SCAN_GROUP - 1 - j) * SCAN_GROUP, SCAN_GROUP), SCAN_GROUP)
            gr, gi = _group_scan(d_ref[rows, :nc], d_ref[rows, nc:], tab_ref, cr, ci, True)
            g_ref[rows, :nc] = gr
            g_ref[rows, nc:] = gi
            nr = jnp.where(last_row, jnp.broadcast_to(cr, (SCAN_GROUP, nc)), pltpu.roll(gr, SCAN_GROUP - 1, 0))
            ni = jnp.where(last_row, jnp.broadcast_to(ci, (SCAN_GROUP, nc)), pltpu.roll(gi, SCAN_GROUP - 1, 0))
            sr, si = x_ref[rows, :nc], x_ref[rows, nc:]
            acc[:, :nc] += nr * sr + ni * si
            acc[:, nc:] += ni * sr - nr * si
            return gr[0:1, :], gi[0:1, :]

        cr, ci = lax.fori_loop(0, SCAN_T // SCAN_GROUP, group, (carry[:, :nc], carry[:, nc:]), unroll=2)
        carry[:, :nc] = cr
        carry[:, nc:] = ci

        for b in range(SSM_SUPER):
            (rows_b, cols_re, st_re), (_, cols_im, st_im) = _super_blocks()[2 * b:2 * b + 2]
            du_ref[:, rows_b] = (_dot16(g_ref[:, st_re], wb_ref[rows_b, cols_re], NT)
                                 + _dot16(g_ref[:, st_im], wb_ref[rows_b, cols_im], NT))
            for cols_c, cols_s in ((cols_re, st_re), (cols_im, st_im)):
                dwb_ref[rows_b, cols_c] += _dot16(u_ref[:, rows_b], g_ref[:, cols_s], TN)
                dwct_ref[rows_b, cols_c] += _dot16(dy_ref[:, rows_b], x_ref[:, cols_s], TN)

        @pl.when(i == n_blk - 1)
        def _():
            dlam_ref[...] = jnp.sum(acc[...], axis=0, keepdims=True)

    const = lambda shape: pl.BlockSpec(shape, lambda i: (0,) * len(shape))
    rows = lambda width, col_block=0: pl.BlockSpec((SCAN_T, width), lambda i: (n_blk - 1 - i, col_block))
    maps = const((SSM_W, 2 * SB_COLS))
    return pl.pallas_call(
        body, name="s5_bwd", grid=(n_blk,),
        in_specs=[const((4, SCAN_GROUP, 2 * nc)), rows(SSM_W), rows(2 * nc), rows(SSM_W, u_cols[0] // SSM_W), maps, maps],
        out_specs=[rows(SSM_W), const((1, 2 * nc)), maps, maps],
        out_shape=[SDS((S, SSM_W), F32), SDS((1, 2 * nc), F32), SDS((SSM_W, 2 * SB_COLS), F32),
                   SDS((SSM_W, 2 * SB_COLS), F32)],
        scratch_shapes=[pltpu.VMEM((1, 2 * nc), F32), pltpu.VMEM((SCAN_GROUP, 2 * nc), F32),
                        pltpu.VMEM((SCAN_T, 2 * nc), F32), pltpu.VMEM((SCAN_T, 2 * nc), F32)],
        compiler_params=_cparams(("arbitrary",)),
    )(tab, dy, states, u_arr, w_b, w_ct)


def _ssm_prep(a_re, a_im, log_dt, b_re, b_im, c_re, c_im):
    lam = lax.complex(a_re, a_im)
    dt = jnp.exp(log_dt)[:, None]
    lam_bar = jnp.exp(lam * dt)
    b_bar = ((lam_bar - 1.0) / lam)[..., None] * lax.complex(b_re, b_im)
    lam_t = jnp.concatenate([jnp.real(lam_bar).reshape(HALF, LANES), jnp.imag(lam_bar).reshape(HALF, LANES)], axis=0)
    groups_per_super = SSM_GROUPS // SSM_SUPER
    on_diag = ((lax.broadcasted_iota(jnp.int32, (SSM_W, SB_COLS), 0) // SSM_GROUP) % groups_per_super
               == lax.broadcasted_iota(jnp.int32, (SSM_W, SB_COLS), 1) // SSM_STATE)

    repeat = (lax.broadcasted_iota(jnp.int32, (SSM_STATE, SB_COLS), 0)
              == lax.broadcasted_iota(jnp.int32, (SSM_STATE, SB_COLS), 1) % SSM_STATE).astype(F32)

    def compact(m):
        tiled = jnp.dot(m.reshape(SSM_W, SSM_STATE), repeat, precision=lax.Precision.HIGHEST)
        return jnp.where(on_diag, tiled, 0.0)

    w_b = jnp.concatenate([compact(jnp.real(b_bar).transpose(0, 2, 1)),
                           compact(jnp.imag(b_bar).transpose(0, 2, 1))], axis=1)
    w_ct = jnp.concatenate([compact(c_re), -compact(c_im)], axis=1)
    return lam_t, w_b, w_ct


U_SSM_COLS = (4 * ATT_W, SSM_W)


def _row(v):
    return v.reshape(1, -1)


def _even_fwd(x, h, tail, pre, post, w_in, late_w, glu_b, ssm_d, prep, tables):
    lam_t, w_b, w_ct = prep
    proj = _mm(h, w_in, "nn", F32, b_blocks=True)
    qkv = _qkv_prep(proj, tables[:3])
    w_out, glu_w = late_w(qkv)
    att, lse = _flash_fwd(qkv, tables[3])
    scan_fwd_tab, scan_bwd_tab = _scan_tables(lam_t)
    states, y = _s5_fwd(scan_fwd_tab, proj, U_SSM_COLS, w_b, w_ct)

    def act1(yv, uv, dv):
        return (_gelu_and_grad(yv + dv * uv)[0],), ()

    (z1,) = _rowwise(act1, [(y, SSM_W, 0), (proj, SSM_W, 8), (ssm_d, SSM_W, 0)], [(SSM_W, F32)], name="ssm_act_fwd")
    lin = _mm(z1, glu_w, "nn", F32)

    def gate(att_v, ga, gs, z1v, linv, bv):
        ssm_out = z1v * _sigmoid(linv + bv)
        return (jnp.concatenate([att_v * _silu_and_grad(ga)[0], ssm_out * _silu_and_grad(gs)[0]], axis=1),), ()

    (merged,) = _rowwise(gate, [(att, ATT_W, 0), (proj, ATT_W, 3), (proj, SSM_W, 9), (z1, SSM_W, 0),
                                (lin, SSM_W, 0), (glu_b, SSM_W, 0)], [(EVEN_OUT, BF16)], name="even_gate_fwd")
    yout = _mm(merged, w_out, "nn", F32)
    saved = (x, h, proj, qkv, att, lse, states, y, z1, lin, merged, yout, w_out, glu_w, scan_bwd_tab)
    return tail(x, yout, post) + (saved,)


def _even_bwd(g, saved, pre, post, w_in, late_w, glu_b, ssm_d, prep, tables, on_w, on_ssm):
    x, h, proj, qkv, att, lse, states, y, z1, lin, merged, yout, w_out, glu_w, scan_bwd_tab = saved
    lam_t, w_b, w_ct = prep
    dyout, dpost = _post_bwd(g, yout, post)
    dmerged = _mm(dyout, w_out, "nt", F32)
    dw_out = _mm(merged, dyout, "tn", BF16)

    def gate_bwd(dm_a, dm_s, att_v, ga, gs, z1v, linv, bv):
        sa, dsa = _silu_and_grad(ga)
        ss, dss = _silu_and_grad(gs)
        sig = _sigmoid(linv + bv)
        ssm_out = z1v * sig
        dssm = dm_s * ss
        dlin = dssm * z1v * sig * (1.0 - sig)
        return (dm_a * sa, dm_a * att_v * dsa, dm_s * ssm_out * dss, dssm * sig, dlin), (dlin,)

    datt, dg_att, dg_ssm, dz1a, dlin, dglu_b = _rowwise(
        gate_bwd, [(dmerged, ATT_W, 0), (dmerged, SSM_W, 2), (att, ATT_W, 0), (proj, ATT_W, 3), (proj, SSM_W, 9),
                   (z1, SSM_W, 0), (lin, SSM_W, 0), (glu_b, SSM_W, 0)],
        [(ATT_W, F32), (ATT_W, BF16), (SSM_W, BF16), (SSM_W, F32), (SSM_W, BF16)], [SSM_W], name="even_gate_bwd")
    dz1b = _mm(dlin, glu_w, "nt", F32)
    dglu_w = _mm(z1, dlin, "tn", BF16)

    def act1_bwd(da, db, yv, uv, dv):
        dpre = (da + db) * _gelu_and_grad(yv + dv * uv)[1]
        return (dpre, dpre * dv), (dpre * uv,)

    sent_late_w = on_w(dict(w_out=dw_out, glu_w=dglu_w))
    dy, du_direct, dd = _rowwise(act1_bwd, [(dz1a, SSM_W, 0), (dz1b, SSM_W, 0), (y, SSM_W, 0), (proj, SSM_W, 8),
                                            (ssm_d, SSM_W, 0)], [(SSM_W, BF16), (SSM_W, F32)], [SSM_W],
                                 name="ssm_act_bwd", after=(sent_late_w,))
    du_state, dlam_row, dw_b, dw_ct = _s5_bwd(scan_bwd_tab, dy, states, proj, U_SSM_COLS, w_b, w_ct)
    dlam = jnp.concatenate([dlam_row[0, :N_CPLX].reshape(HALF, LANES), dlam_row[0, N_CPLX:].reshape(HALF, LANES)],
                           axis=0)
    sent_ssm = on_ssm((dlam, dw_b, dw_ct))
    dq, dk, dv = _flash_bwd(qkv, att, datt, lse, tables[3], after=() if sent_ssm is None else (sent_ssm,))

    def assemble(dqv, dkv, dvv, dga, dua, dub, dgs, c, s, sw):
        rot = _rotate(jnp.concatenate([dqv, dkv], axis=1), c, s, sw, True)
        return (jnp.concatenate([(rot[:, :ATT_W] * HEAD_DIM ** -0.5).astype(BF16), rot[:, ATT_W:].astype(BF16),
                                 dvv.astype(BF16), dga, (dua + dub).astype(BF16), dgs], axis=1),), ()

    (dproj,) = _rowwise(assemble, [(dq, ATT_W, 0), (dk, ATT_W, 0), (dv, ATT_W, 0), (dg_att, ATT_W, 0),
                                   (du_state, SSM_W, 0), (du_direct, SSM_W, 0), (dg_ssm, SSM_W, 0),
                                   (tables[0], LANES, 0), (tables[1], LANES, 0), (tables[2], LANES, 0)],
                        [(EVEN_IN, BF16)], name="dproj_assemble")
    dw_in = _mm(h, dproj, "tn", BF16, out_blocks=True)
    sent = on_w(dict(w_in=dw_in))
    dh = _mm(dproj, w_in, "nt", F32, b_blocks=True, after=(sent,))
    g_prev, dpre = _pre_bwd(g, dh, x, pre)
    return g_prev, dict(pre=dpre, post=dpost, glu_b=dglu_b, ssm_d=dd)


def _odd_fwd(x, h, tail, pre, post, w_in, pool_w, pool_scale, w_out):
    proj = _mm(h, w_in, "nn", F32, b_blocks=True)
    mixed = _pool(proj, 0, False, BF16)
    ylin = _gmm(mixed, pool_w, "nn", F32)

    def gate(yl, gt, sc):
        return (yl * sc * _silu_and_grad(gt)[0],), ()

    (z,) = _rowwise(gate, [(ylin, POOL_W, 0), (proj, POOL_W, 1), (pool_scale, POOL_W, 0)], [(POOL_W, BF16)],
                    name="odd_gate_fwd")
    yout = _mm(z, w_out, "nn", F32)
    return tail(x, yout, post) + ((x, h, proj, mixed, ylin, z, yout),)


def _odd_bwd(g, saved, pre, post, w_in, pool_w, pool_scale, w_out, on_w):
    x, h, proj, mixed, ylin, z, yout = saved
    dyout, dpost = _post_bwd(g, yout, post)
    dz = _mm(dyout, w_out, "nt", F32)
    dw_out = _mm(z, dyout, "tn", BF16)

    def gate_bwd(dzv, yl, gt, sc):
        sg, dsg = _silu_and_grad(gt)
        tt = dzv * sg
        return (tt * sc, dzv * yl * sc * dsg), (tt * yl,)

    dylin, dproj_gate, dscale = _rowwise(gate_bwd, [(dz, POOL_W, 0), (ylin, POOL_W, 0), (proj, POOL_W, 1),
                                                    (pool_scale, POOL_W, 0)],
                                         [(POOL_W, BF16), (POOL_W, BF16, ODD_IN, 1)], [POOL_W], name="odd_gate_bwd")
    dmixed = _gmm(dylin, pool_w, "nt", F32)
    dpool_w = _gmm(mixed, dylin, "tn", BF16)
    dproj = _pool(dmixed, 0, True, BF16, into=dproj_gate)
    dw_in = _mm(h, dproj, "tn", BF16, out_blocks=True)
    sent = on_w(dict(w_in=dw_in, w_out=dw_out, pool_w=dpool_w))
    dh = _mm(dproj, w_in, "nt", F32, b_blocks=True, after=(sent,))
    g_prev, dpre = _pre_bwd(g, dh, x, pre)
    return g_prev, dict(pre=dpre, post=dpost, pool_scale=dscale)


def _my_index():
    return 4 * lax.axis_index("x") + 2 * lax.axis_index("y") + lax.axis_index("c")


HBM_SPEC = pl.BlockSpec(memory_space=pltpu.HBM)
SEM_SPEC = pl.BlockSpec(memory_space=pltpu.SEMAPHORE)
SPLIT_EFFECT = pltpu.SideEffectType.DATAFLOW_SIDE_EFFECTING


def _device_of(j):
    return (j // 4, (j // 2) % 2, j % 2)


def _split_copy(srcs, lands, send_sems, recv_sems, gather, i, j, dst_slot, recv_slot):
    return pltpu.make_async_remote_copy(
        src_ref=srcs[i] if gather else srcs[i].at[j], dst_ref=lands[i].at[dst_slot],
        send_sem=send_sems.at[i * N_DEV + j], recv_sem=recv_sems.at[i * N_DEV + recv_slot],
        device_id=_device_of(j), device_id_type=MESH_ID)


def _own_copy(srcs, lands, send_sems, gather, i, me):
    return pltpu.make_async_copy(srcs[i] if gather else srcs[i].at[me], lands[i].at[me], send_sems.at[i * N_DEV + me])


def _xchg_start(name, srcs, gather, after=()):
    n = len(srcs)
    n_in = n + len(after)

    def body(*refs):
        src_refs = refs[:n]
        send_sems, recv_sems, token = refs[n_in], refs[n_in + 1], refs[-1]
        land_refs = refs[n_in + 2 + n:n_in + 2 + 2 * n]
        me = _my_index()
        for j in range(N_DEV):
            @pl.when(me != j)
            def _(j=j):
                for i in range(n):
                    _split_copy(src_refs, land_refs, send_sems, recv_sems, gather, i, j, me, me).start()
        for i in range(n):
            _own_copy(src_refs, land_refs, send_sems, gather, i, me).start()
        token[...] = jnp.zeros_like(token)

    land_shapes = [((N_DEV,) + a.shape) if gather else a.shape for a in srcs]
    thru = ([pltpu.HBM(a.shape, a.dtype) for a in srcs] + [pltpu.HBM(s, a.dtype) for s, a in zip(land_shapes, srcs)])
    res = pl.pallas_call(
        body, name=name,
        out_shape=(pltpu.SemaphoreType.DMA((n * N_DEV,)), pltpu.SemaphoreType.DMA((n * N_DEV,)), *thru,
                   SDS((8, LANES), F32)),
        in_specs=[HBM_SPEC] * n + [pl.BlockSpec(memory_space=pl.ANY)] * len(after),
        out_specs=(SEM_SPEC, SEM_SPEC, *([HBM_SPEC] * (2 * n)), pl.BlockSpec(memory_space=pltpu.VMEM)),
        input_output_aliases={i: 2 + i for i in range(n)},
        compiler_params=pltpu.CompilerParams(has_side_effects=SPLIT_EFFECT),
    )(*[pltpu.with_memory_space_constraint(a, pltpu.HBM) for a in srcs], *after)
    return res[0], res[1], list(res[2:2 + n]), list(res[2 + n:2 + 2 * n]), res[-1]


def _xchg_wait(name, started, gather, after):
    send_sems, recv_sems, srcs, lands, _ = started
    n = len(srcs)

    def body(*refs):
        src_refs, land_refs = refs[:n], refs[n:2 * n]
        send_r, recv_r = refs[2 * n], refs[2 * n + 1]
        me = _my_index()
        for j in range(N_DEV):
            @pl.when(me != j)
            def _(j=j):
                for i in range(n):
                    _split_copy(src_refs, land_refs, send_r, recv_r, gather, i, j, me, me).wait_send()
                    _split_copy(src_refs, land_refs, send_r, recv_r, gather, i, j, j, j).wait_recv()
        for i in range(n):
            _own_copy(src_refs, land_refs, send_r, gather, i, me).wait()

    thru = [pltpu.HBM(a.shape, a.dtype) for a in list(srcs) + list(lands)]
    res = pl.pallas_call(
        body, name=name, out_shape=tuple(thru),
        in_specs=[HBM_SPEC] * (2 * n) + [SEM_SPEC, SEM_SPEC] + [pl.BlockSpec(memory_space=pl.ANY)] * len(after),
        out_specs=tuple([HBM_SPEC] * (2 * n)),
        input_output_aliases={i: i for i in range(2 * n)},
        compiler_params=pltpu.CompilerParams(has_side_effects=SPLIT_EFFECT),
    )(*srcs, *lands, send_sems, recv_sems, *after)
    return list(res[n:])


def _adam_layer(w, slots, m, v, layer, name, into=None):
    n_l, r, c = w.shape
    ns = slots.shape[0]
    tr = r
    while tr * c * 4 > (1 << 20) and tr % 16 == 0:
        tr //= 2
    assert r % tr == 0

    def body(w_ref, g_ref, m_ref, v_ref, *rest):
        go_ref, d_ref, mo_ref, vo_ref = rest[-4:]
        g = g_ref[0].astype(F32)
        for s in range(1, ns):
            g = g + g_ref[s].astype(F32)
        mn = ADAM_B1 * m_ref[...] + (1.0 - ADAM_B1) * g
        vn = ADAM_B2 * v_ref[...] + (1.0 - ADAM_B2) * (g * g)
        m_hat = mn / (1.0 - ADAM_B1 ** ADAM_STEP)
        v_hat = vn / (1.0 - ADAM_B2 ** ADAM_STEP)
        go_ref[...] = g
        d_ref[...] = -ADAM_LR * (m_hat / (jnp.sqrt(v_hat) + ADAM_EPS) + ADAM_WD * w_ref[...])
        mo_ref[...] = mn
        vo_ref[...] = vn

    blk = pl.BlockSpec((None, tr, c), lambda i: (layer, i, 0))
    earlier = () if into is None else tuple(into)
    return pl.pallas_call(
        body, name=name, grid=(r // tr,),
        in_specs=[blk, pl.BlockSpec((ns, tr, c), lambda i: (0, i, 0)), blk, blk]
        + [pl.BlockSpec(memory_space=pl.ANY)] * len(earlier),
        out_specs=[blk] * 4, out_shape=[SDS((n_l, r, c), F32)] * 4,
        input_output_aliases={4 + q: q for q in range(len(earlier))},
        compiler_params=_cparams(("arbitrary",)),
    )(*_in_hbm((w, slots, m, v)), *earlier)


def _adam(w, gslots, m, v, name):
    r, c = w.shape
    ns = gslots.shape[0]
    tr = r
    while tr * c * 4 > (1 << 20) and tr % 16 == 0:
        tr //= 2
    assert r % tr == 0

    def body(w_ref, g_ref, m_ref, v_ref, go_ref, d_ref, mo_ref, vo_ref):
        g = g_ref[0].astype(F32)
        for s in range(1, ns):
            g = g + g_ref[s].astype(F32)
        wv = w_ref[...]
        mn = ADAM_B1 * m_ref[...] + (1.0 - ADAM_B1) * g
        vn = ADAM_B2 * v_ref[...] + (1.0 - ADAM_B2) * (g * g)
        m_hat = mn / (1.0 - ADAM_B1 ** ADAM_STEP)
        v_hat = vn / (1.0 - ADAM_B2 ** ADAM_STEP)
        go_ref[...] = g
        d_ref[...] = -ADAM_LR * (m_hat / (jnp.sqrt(v_hat) + ADAM_EPS) + ADAM_WD * wv)
        mo_ref[...] = mn
        vo_ref[...] = vn

    blk = pl.BlockSpec((tr, c), lambda i: (i, 0))
    return pl.pallas_call(
        body, name=name, grid=(r // tr,),
        in_specs=[blk, pl.BlockSpec((ns, tr, c), lambda i: (0, i, 0)), blk, blk],
        out_specs=[blk] * 4, out_shape=[SDS((r, c), F32)] * 4,
        compiler_params=_cparams(("parallel",)),
    )(w, gslots, m, v)


def _sum_slots(slots, name):
    ns, r, c = slots.shape

    def body(g_ref, o_ref):
        g = g_ref[0]
        for s in range(1, ns):
            g = g + g_ref[s]
        o_ref[...] = g

    return pl.pallas_call(
        body, name=name, grid=(1,),
        in_specs=[pl.BlockSpec((ns, r, c), lambda i: (0, 0, 0))], out_specs=pl.BlockSpec((r, c), lambda i: (0, 0)),
        out_shape=SDS((r, c), F32), compiler_params=_cparams(("arbitrary",)),
    )(slots)


def _adam_params(params, name):
    n = len(params)

    def body(*refs):
        ins, outs = refs[:5 * n], refs[5 * n:]
        for p in range(n):
            w_ref, m_ref, v_ref, g_first, g_rest = ins[5 * p:5 * p + 5]
            go_ref, d_ref, mo_ref, vo_ref = outs[4 * p:4 * p + 4]
            for part, g_ref in ((slice(0, 1), g_first), (slice(1, w_ref.shape[0]), g_rest)):
                g = g_ref[...]
                mn = ADAM_B1 * m_ref[part] + (1.0 - ADAM_B1) * g
                vn = ADAM_B2 * v_ref[part] + (1.0 - ADAM_B2) * (g * g)
                m_hat = mn / (1.0 - ADAM_B1 ** ADAM_STEP)
                v_hat = vn / (1.0 - ADAM_B2 ** ADAM_STEP)
                go_ref[part] = g
                d_ref[part] = -ADAM_LR * (m_hat / (jnp.sqrt(v_hat) + ADAM_EPS) + ADAM_WD * w_ref[part])
                mo_ref[part] = mn
                vo_ref[part] = vn

    def whole(a):
        return pl.BlockSpec(a.shape, lambda i, nd=a.ndim: (0,) * nd)

    flat = _in_hbm([a for prm in params for a in prm])
    outs = pl.pallas_call(
        body, name=name, grid=(1,),
        in_specs=[whole(a) for a in flat],
        out_specs=[whole(prm[0]) for prm in params for _ in range(4)],
        out_shape=[SDS(prm[0].shape, F32) for prm in params for _ in range(4)],
        compiler_params=_cparams(("arbitrary",)),
    )(*flat)
    return [outs[4 * p:4 * p + 4] for p in range(n)]


SMALL_NAMES = ("pre_norm", "post_norm", "ssm_a_re", "ssm_a_im", "ssm_log_dt", "ssm_b_re", "ssm_b_im", "ssm_c_re",
               "ssm_c_im", "ssm_d", "ssm_glu_b")
SSM_NAMES = ("ssm_a_re", "ssm_a_im", "ssm_log_dt", "ssm_b_re", "ssm_b_im", "ssm_c_re", "ssm_c_im")
WEIGHT_ORDER = ("pre_norm", "post_norm", "even_w_in", "even_w_out", "ssm_a_re", "ssm_a_im", "ssm_log_dt", "ssm_b_re",
                "ssm_b_im", "ssm_c_re", "ssm_c_im", "ssm_d", "ssm_glu_w", "ssm_glu_b", "odd_w_in", "pool_w",
                "pool_scale", "odd_w_out")
PACK_ROWS_ALIGN = 8


def _pack(parts):
    flat = jnp.concatenate([p.reshape(-1).astype(F32) for p in parts])
    rows = -(-flat.shape[0] // (LANES * PACK_ROWS_ALIGN)) * PACK_ROWS_ALIGN
    return jnp.pad(flat, (0, rows * LANES - flat.shape[0])).reshape(rows, LANES)


def _unpack(packed, shapes):
    flat = packed.reshape(-1)
    out, off = [], 0
    for shp in shapes:
        size = math.prod(shp)
        out.append(flat[off:off + size].reshape(shp))
        off += size
    return out


EVEN_SHARDED = ("w_in", "w_out", "glu_w")
ODD_SHARDED = ("w_in", "pool_w", "w_out")
FAMILY = {(0, "w_in"): "even_w_in", (0, "w_out"): "even_w_out", (0, "glu_w"): "ssm_glu_w",
          (1, "w_in"): "odd_w_in", (1, "pool_w"): "pool_w", (1, "w_out"): "odd_w_out"}


def _sharded_keys(layer):
    return EVEN_SHARDED if layer % 2 == 0 else ODD_SHARDED


def _local_step(x, tgt, small, get_weights, on_w, on_ssm, on_grads, zero=0.0):
    tables = _rope_tables(zero) + (_attention_bias(zero),)
    preps, prep_vjps = [], []
    for i in range(2):
        out, vjp = jax.vjp(_ssm_prep, small["ssm_a_re"][i] + zero, small["ssm_a_im"][i], small["ssm_log_dt"][i],
                           small["ssm_b_re"][i], small["ssm_b_im"][i], small["ssm_c_re"][i], small["ssm_c_im"][i])
        preps.append(out)
        prep_vjps.append(vjp)

    def layer_args(layer, wts):
        i = layer // 2
        pre, post = _row(small["pre_norm"][layer]) + wts.get("token", 0.0), _row(small["post_norm"][layer])
        if layer % 2 == 0:
            return (pre, post, wts["w_in"], wts["late"], _row(small["ssm_glu_b"][i]), _row(small["ssm_d"][i]),
                    preps[i], tables)
        return (pre, post, wts["w_in"], wts["pool_w"], _row(wts["pool_scale"]), wts["w_out"])

    saved, args = [], []
    cur = x
    for layer in range(4):
        after = (cur,) if layer else (cur, tables[0], tables[3], preps[0][1], preps[0][2], preps[1][1], preps[1][2])
        args.append(layer_args(layer, get_weights(layer, after)))
        if layer == 0:
            h = _norm_fwd(cur, args[0][0])
        if layer < 3:
            def tail(xv, yv, post, next_gain=_row(small["pre_norm"][layer + 1])):
                return tuple(_post_fwd(xv, yv, post, next_gain))
        else:
            def tail(xv, yv, post):
                return tuple(_post_fwd_loss(xv, yv, post, tgt))
        cur, h, sv = (_even_fwd if layer % 2 == 0 else _odd_fwd)(cur, h, tail, *args[layer])
        saved.append(sv)
    g, sq = cur, h
    loss = 0.5 * jnp.sum(sq) / D

    lg = [None] * 4
    token = jnp.zeros((), F32)
    for layer in reversed(range(4)):
        largs = list(args[layer])
        largs[1] = largs[1] + token
        hooks = dict(on_w=functools.partial(on_w, layer))
        ssm_grads = []
        if layer % 2 == 0:
            def ssm_hook(cotangents, layer=layer):
                ssm_grads.append(prep_vjps[layer // 2](cotangents))
                return on_ssm(layer, ssm_grads[0])

            hooks["on_ssm"] = ssm_hook
        g, lg[layer] = (_even_bwd if layer % 2 == 0 else _odd_bwd)(g, saved[layer], *largs, **hooks)
        if ssm_grads:
            lg[layer]["ssm"] = ssm_grads[0]
        token = on_grads(layer, lg[layer])
    return loss, g, token


def _to_slots(key, gfull):
    if key == "w_in":
        return gfull
    if key in ("w_out", "glu_w"):
        rr, nn = gfull.shape
        return gfull.reshape(N_DEV, rr // N_DEV, nn)
    assert key == "pool_w"
    gg, rr, nn = gfull.shape
    return gfull.reshape(gg, N_DEV, rr // N_DEV, nn).transpose(1, 0, 2, 3)


def _from_gathered(key, gat):
    if key == "w_in":
        return gat
    if key in ("w_out", "glu_w"):
        _, rr, nn = gat.shape
        return gat.reshape(N_DEV * rr, nn)
    assert key == "pool_w"
    _, gg, rr, nn = gat.shape
    return gat.transpose(1, 0, 2, 3).reshape(gg, N_DEV * rr, nn)


def kernel(x, pre_norm, post_norm, even_w_in, even_w_out, ssm_a_re, ssm_a_im, ssm_log_dt, ssm_b_re, ssm_b_im, ssm_c_re, ssm_c_im, ssm_d, ssm_glu_w, ssm_glu_b, odd_w_in, pool_w, pool_scale, odd_w_out, loss_target, m_pre_norm, m_post_norm, m_even_w_in, m_even_w_out, m_ssm_a_re, m_ssm_a_im, m_ssm_log_dt, m_ssm_b_re, m_ssm_b_im, m_ssm_c_re, m_ssm_c_im, m_ssm_d, m_ssm_glu_w, m_ssm_glu_b, m_odd_w_in, m_pool_w, m_pool_scale, m_odd_w_out, v_pre_norm, v_post_norm, v_even_w_in, v_even_w_out, v_ssm_a_re, v_ssm_a_im, v_ssm_log_dt, v_ssm_b_re, v_ssm_b_im, v_ssm_c_re, v_ssm_c_im, v_ssm_d, v_ssm_glu_w, v_ssm_glu_b, v_odd_w_in, v_pool_w, v_pool_scale, v_odd_w_out):
    w = dict(pre_norm=pre_norm, post_norm=post_norm, even_w_in=even_w_in, even_w_out=even_w_out, ssm_a_re=ssm_a_re,
             ssm_a_im=ssm_a_im, ssm_log_dt=ssm_log_dt, ssm_b_re=ssm_b_re, ssm_b_im=ssm_b_im, ssm_c_re=ssm_c_re,
             ssm_c_im=ssm_c_im, ssm_d=ssm_d, ssm_glu_w=ssm_glu_w, ssm_glu_b=ssm_glu_b, odd_w_in=odd_w_in,
             pool_w=pool_w, pool_scale=pool_scale, odd_w_out=odd_w_out)
    mom = dict(pre_norm=m_pre_norm, post_norm=m_post_norm, even_w_in=m_even_w_in, even_w_out=m_even_w_out,
               ssm_a_re=m_ssm_a_re, ssm_a_im=m_ssm_a_im, ssm_log_dt=m_ssm_log_dt, ssm_b_re=m_ssm_b_re,
               ssm_b_im=m_ssm_b_im, ssm_c_re=m_ssm_c_re, ssm_c_im=m_ssm_c_im, ssm_d=m_ssm_d, ssm_glu_w=m_ssm_glu_w,
               ssm_glu_b=m_ssm_glu_b, odd_w_in=m_odd_w_in, pool_w=m_pool_w, pool_scale=m_pool_scale,
               odd_w_out=m_odd_w_out)
    var = dict(pre_norm=v_pre_norm, post_norm=v_post_norm, even_w_in=v_even_w_in, even_w_out=v_even_w_out,
               ssm_a_re=v_ssm_a_re, ssm_a_im=v_ssm_a_im, ssm_log_dt=v_ssm_log_dt, ssm_b_re=v_ssm_b_re,
               ssm_b_im=v_ssm_b_im, ssm_c_re=v_ssm_c_re, ssm_c_im=v_ssm_c_im, ssm_d=v_ssm_d, ssm_glu_w=v_ssm_glu_w,
               ssm_glu_b=v_ssm_glu_b, odd_w_in=v_odd_w_in, pool_w=v_pool_w, pool_scale=v_pool_scale,
               odd_w_out=v_odd_w_out)
    me = _my_index()
    scale_cols = pool_scale.shape[1]

    def shards_of(layer, keys):
        i = layer // 2
        shards = [w[FAMILY[(layer % 2, k)]][i].astype(BF16) for k in keys]
        if layer % 2 == 1:
            shards.append(jnp.pad(pool_scale[i][None], ((0, PACK_ROWS_ALIGN - 1), (0, 0))))
        return shards

    def start_gather(tag, after=()):
        return _xchg_start(f"gather_start_{tag}", shards[tag], True, after)

    shards = {0: shards_of(0, EVEN_SHARDED[:1]), "0_late": shards_of(0, EVEN_SHARDED[1:])}
    shards.update({layer: shards_of(layer, _sharded_keys(layer)) for layer in (1, 2, 3)})
    gather_started = {0: start_gather(0)}
    small = {nm: w[nm] for nm in SMALL_NAMES}
    packed_names = ("pre_norm", "post_norm") + SSM_NAMES + ("ssm_d", "ssm_glu_b")
    tails = {nm: (SSM_GROUPS, SSM_STATE * SSM_GROUP) if nm in ("ssm_b_re", "ssm_b_im") else w[nm].shape[1:]
             for nm in packed_names}
    dense = lambda nm, a: a.reshape((a.shape[0],) + tails[nm])
    small_operands = {nm: tuple(dense(nm, tree[nm]) for tree in (w, mom, var)) for nm in packed_names}
    early_work = ([a for tag in ("0_late", 1, 2, 3) for a in shards[tag]]
                  + [a for nm in ("ssm_b_re", "ssm_b_im") for a in small_operands[nm]])

    def get_weights(layer, after):
        keys = EVEN_SHARDED[:1] if layer == 0 else _sharded_keys(layer)
        if layer == 0:
            after = tuple(after) + tuple(early_work)
        lands = _xchg_wait(f"gather_wait_{layer}", gather_started[layer], True, after)
        wts = {k: _from_gathered(k, gat) for k, gat in zip(keys, lands)}
        if layer % 2 == 1:
            wts["pool_scale"] = lands[-1][:, 0, :].reshape(N_DEV * scale_cols)
        if layer == 0:
            prev = gather_started["0_late"] = start_gather("0_late", after=(lands[0],))
            for later in (1, 2, 3):
                prev = gather_started[later] = start_gather(later, after=(prev[4],))
            wts["token"] = sum(gather_started[tag][4][0, 0] for tag in ("0_late", 1, 2, 3))

            def late(after_late):
                late_lands = _xchg_wait("gather_wait_0_late", gather_started["0_late"], True, (after_late,))
                return tuple(_from_gathered(k, gat) for k, gat in zip(EVEN_SHARDED[1:], late_lands))

            wts["late"] = late
        elif layer == 2:
            wts["late"] = lambda after_late: (wts["w_out"], wts["glu_w"])
        return wts

    scatter_started = []

    def on_w(layer, gw):
        keys = tuple(k for k in _sharded_keys(layer) if k in gw)
        started = _xchg_start(f"scatter_start_{layer}_{keys[0]}", [_to_slots(k, gw[k]) for k in keys], False)
        scatter_started.append((layer, keys, started))
        return started[4]

    def wait_scatters(layers, after):
        for layer, keys, started in scatter_started:
            if layer in layers:
                lands = _xchg_wait(f"scatter_wait_{layer}_{keys[0]}", started, False, after)
                for k, land in zip(keys, lands):
                    recv[(layer, k)] = land

    layer_grads = {}
    early_started, mid_started = [], []

    def on_ssm(layer, ssm_grads):
        if layer != 0:
            return None
        mid_started.append(_xchg_start("mid_start", [_pack(list(ssm_grads))], True))
        return mid_started[0][4]

    def on_grads(layer, lg):
        layer_grads[layer] = lg
        zero = jnp.zeros((), F32)
        if layer == 1:
            lgs = layer_grads
            early = ([jnp.concatenate([lgs[l][k] for l in (1, 2, 3)], axis=0) for k in ("pre", "post")]
                     + list(lgs[2]["ssm"]) + [lgs[2]["ssm_d"], lgs[2]["glu_b"],
                                              jnp.concatenate([lgs[1]["pool_scale"], lgs[3]["pool_scale"]], axis=0)])
            early_started.append(_xchg_start("small_start", [_pack(early)], True))
            zero = zero + early_started[0][4][0, 0]
        return zero

    loss_local, grad_x, token = _local_step(x[0], loss_target[0], small, get_weights, on_w, on_ssm, on_grads,
                                            zero=gather_started[0][4][0, 0])

    lg0 = layer_grads[0]
    late_started = _xchg_start("late_start", [_pack([lg0["pre"], lg0["post"], lg0["ssm_d"], lg0["glu_b"],
                                                     loss_local.reshape(1)]) + token], True)

    def adam_family(parity, k, which, into=None):
        nm = FAMILY[(parity, k)]
        cols = w[nm].shape[-1]
        return _adam_layer(w[nm].reshape(2, -1, cols), recv[(parity + 2 * which, k)].reshape(N_DEV, -1, cols),
                           mom[nm].reshape(2, -1, cols), var[nm].reshape(2, -1, cols), which,
                           f"adam_{nm}_{which}", into)

    recv, res = {}, {}
    wait_scatters((3, 2, 1), (late_started[4],))
    for k in ODD_SHARDED:
        res[FAMILY[(1, k)]] = adam_family(1, k, 1, adam_family(1, k, 0))
    half_done = {k: adam_family(0, k, 1) for k in EVEN_SHARDED}
    odd_done = tuple(half_done[k][0] for k in EVEN_SHARDED)

    (early_slots,) = _xchg_wait("small_wait", early_started[0], True, odd_done)
    (mid_slots,) = _xchg_wait("mid_wait", mid_started[0], True, odd_done)
    early_shapes = [(w[nm].shape[0] - 1,) + tails[nm] for nm in packed_names] + [(2, N_DEV * scale_cols)]
    g_early = _unpack(_sum_slots(early_slots, "sum_small_early"), early_shapes)
    g_mid = _unpack(_sum_slots(mid_slots, "sum_small_mid"), [(1,) + tails[nm] for nm in SSM_NAMES])

    (late_slots,) = _xchg_wait("late_wait", late_started, True, (g_early[0], g_mid[0]))
    wait_scatters((0,), (late_slots,))
    for k in EVEN_SHARDED:
        res[FAMILY[(0, k)]] = adam_family(0, k, 0, half_done[k])
    for nm in FAMILY.values():
        res[nm] = [o.reshape(w[nm].shape) for o in res[nm]]

    late_names = ("pre_norm", "post_norm", "ssm_d", "ssm_glu_b")
    g_late = _unpack(_sum_slots(late_slots, "sum_small_late"), [(1,) + tails[nm] for nm in late_names] + [(1,)])
    g_first = dict(zip(late_names, g_late))
    g_first.update(zip(SSM_NAMES, g_mid))
    outs = _adam_params([small_operands[nm] + (g_first[nm], g_early[j]) for j, nm in enumerate(packed_names)],
                        "adam_small")
    for nm, four in zip(packed_names, outs):
        res[nm] = [o.reshape(w[nm].shape) for o in four]
    loss = g_late[-1].reshape(())
    g_scale = lax.dynamic_slice_in_dim(g_early[-1], me * scale_cols, scale_cols, axis=1)
    pad = ((0, PACK_ROWS_ALIGN - 2), (0, 0))
    outs = _adam(jnp.pad(pool_scale, pad), jnp.pad(g_scale, pad)[None], jnp.pad(m_pool_scale, pad),
                 jnp.pad(v_pool_scale, pad), name="adam_pool_scale")
    res["pool_scale"] = [o[:2] for o in outs]

    out = [loss, grad_x[None]]
    for kind in range(4):
        out += [res[nm][kind] for nm in WEIGHT_ORDER]
    return tuple(out)
```

```python
import functools
import math

import jax
import jax.numpy as jnp
from jax import lax
from jax.experimental import pallas as pl
from jax.experimental.pallas import tpu as pltpu

F32 = jnp.float32
BF16 = jnp.bfloat16
SDS = jax.ShapeDtypeStruct

N_DEV = 8
S = 2048
D = 1024
HEAD_DIM = 64
ROT_DIM = 16
ROPE_THETA = 500000.0
ATT_W = 1024
SSM_W = 512
SSM_GROUPS = 32
SSM_GROUP = 16
SSM_STATE = 64
N_CPLX = SSM_GROUPS * SSM_STATE
POOL_W = 2048
POOL_GROUP = 512
EVEN_IN = 5120
EVEN_OUT = 1536
ODD_IN = 4096
RMS_EPS = 1e-6
LANES = 128
VMEM_LIMIT = 48 * 1024 * 1024

ADAM_LR = 0.001
ADAM_B1 = 0.9
ADAM_B2 = 0.999
ADAM_EPS = 1e-08
ADAM_WD = 0.01
ADAM_STEP = 10

MESH_ID = pl.DeviceIdType.MESH
NN = (((1,), (0,)), ((), ()))
NT = (((1,), (1,)), ((), ()))
TN = (((0,), (0,)), ((), ()))
_DN = {"nn": NN, "nt": NT, "tn": TN}


def _cparams(sem):
    return pltpu.CompilerParams(dimension_semantics=sem, vmem_limit_bytes=VMEM_LIMIT)


def _in_hbm(arrs):
    return [pltpu.with_memory_space_constraint(a, pltpu.HBM) for a in arrs]


MM_TILES = (1024, 768, 512)


def _tile(dim):
    return next((t for t in MM_TILES if dim % t == 0), dim)


BLOCK_PAIR = 2
MAX_WHOLE_K = 2048


def _mm(a, b, mode, out_dtype, b_blocks=False, out_blocks=False, after=()):
    if b_blocks:
        nblk, rows, cb = b.shape
        b2_shape = (rows, nblk * cb)
    else:
        b2_shape = b.shape
    if mode == "nn":
        (m, k), n = a.shape, b2_shape[1]
    elif mode == "nt":
        (m, k), n = a.shape, b2_shape[0]
    else:
        (k, m), n = a.shape, b2_shape[1]
    tm, tn, tk = _tile(m), _tile(n), _tile(k)
    if k <= MAX_WHOLE_K:
        tk = k
    if b_blocks and mode == "nn":
        tn = BLOCK_PAIR * cb
        if tn <= MM_TILES[0]:
            tm = m
    if b_blocks and mode == "nt":
        tk = BLOCK_PAIR * cb
    if out_blocks:
        cb = n // N_DEV
        tn = BLOCK_PAIR * cb
        tk = k
    nk = k // tk

    def body(a_ref, b_ref, *rest):
        o_ref, acc_ref = rest[-2:]
        kk = pl.program_id(2)
        bv = jnp.concatenate([b_ref[p] for p in range(BLOCK_PAIR)], axis=1) if b_blocks else b_ref[...]
        part = lax.dot_general(a_ref[...].astype(BF16), bv.astype(BF16), _DN[mode], preferred_element_type=F32)

        def write(res):
            if out_blocks:
                for p in range(BLOCK_PAIR):
                    o_ref[p] = res[:, p * cb:(p + 1) * cb].astype(o_ref.dtype)
            else:
                o_ref[...] = res.astype(o_ref.dtype)

        if nk == 1:
            write(part)
            return

        @pl.when(kk == 0)
        def _():
            acc_ref[...] = part

        @pl.when((kk > 0) & (kk < nk - 1))
        def _():
            acc_ref[...] += part

        @pl.when(kk == nk - 1)
        def _():
            write(acc_ref[...] + part)

    if mode == "nn":
        a_spec = pl.BlockSpec((tm, tk), lambda i, j, kk: (i, kk))
        b_spec = pl.BlockSpec((tk, tn), lambda i, j, kk: (kk, j))
    elif mode == "nt":
        a_spec = pl.BlockSpec((tm, tk), lambda i, j, kk: (i, kk))
        b_spec = pl.BlockSpec((tn, tk), lambda i, j, kk: (j, kk))
    else:
        a_spec = pl.BlockSpec((tk, tm), lambda i, j, kk: (kk, i))
        b_spec = pl.BlockSpec((tk, tn), lambda i, j, kk: (kk, j))
    if b_blocks and mode == "nn":
        b_spec = pl.BlockSpec((BLOCK_PAIR, tk, cb), lambda i, j, kk: (j, kk, 0))
    if b_blocks and mode == "nt":
        b_spec = pl.BlockSpec((BLOCK_PAIR, tn, cb), lambda i, j, kk: (kk, j, 0))
    out_spec = pl.BlockSpec((tm, tn), lambda i, j, kk: (i, j))
    out_shape = SDS((m, n), out_dtype)
    if out_blocks:
        out_spec = pl.BlockSpec((BLOCK_PAIR, tm, cb), lambda i, j, kk: (j, i, 0))
        out_shape = SDS((N_DEV, m, cb), out_dtype)
    return pl.pallas_call(
        body, name=f"mm_{mode}_{m}x{k}x{n}",
        grid=(m // tm, n // tn, nk),
        in_specs=[a_spec, b_spec] + [pl.BlockSpec(memory_space=pl.ANY)] * len(after),
        out_specs=out_spec,
        out_shape=out_shape,
        scratch_shapes=[pltpu.VMEM((tm, tn) if nk > 1 else (8, LANES), F32)],
        compiler_params=_cparams(("parallel", "parallel", "arbitrary")),
    )(a, b, *after)


def _gmm(a, b, mode, out_dtype, tm=S):
    ng, gw = POOL_W // POOL_GROUP, POOL_GROUP
    ns = S // tm
    if mode in ("nn", "nt"):
        def body(a_ref, b_ref, o_ref):
            o_ref[...] = lax.dot_general(a_ref[...].astype(BF16), b_ref[...].astype(BF16), _DN[mode],
                                         preferred_element_type=F32).astype(o_ref.dtype)

        return pl.pallas_call(
            body, name=f"gmm_{mode}", grid=(ng, ns),
            in_specs=[pl.BlockSpec((tm, gw), lambda g, i: (i, g)),
                      pl.BlockSpec((None, gw, gw), lambda g, i: (g, 0, 0))],
            out_specs=pl.BlockSpec((tm, gw), lambda g, i: (i, g)),
            out_shape=SDS((S, POOL_W), out_dtype),
            compiler_params=_cparams(("parallel", "parallel")),
        )(a, b)

    def body_tn(a_ref, b_ref, o_ref, acc_ref):
        i = pl.program_id(1)

        @pl.when(i == 0)
        def _():
            acc_ref[...] = jnp.zeros_like(acc_ref)

        acc_ref[...] += lax.dot_general(a_ref[...].astype(BF16), b_ref[...].astype(BF16), TN,
                                        preferred_element_type=F32)

        @pl.when(i == ns - 1)
        def _():
            o_ref[...] = acc_ref[...].astype(o_ref.dtype)

    return pl.pallas_call(
        body_tn, name="gmm_tn", grid=(ng, ns),
        in_specs=[pl.BlockSpec((tm, gw), lambda g, i: (i, g)),
                  pl.BlockSpec((tm, gw), lambda g, i: (i, g))],
        out_specs=pl.BlockSpec((None, gw, gw), lambda g, i: (g, 0, 0)),
        out_shape=SDS((ng, gw, gw), out_dtype),
        scratch_shapes=[pltpu.VMEM((gw, gw), F32)],
        compiler_params=_cparams(("parallel", "arbitrary")),
    )(a, b)


def _rowwise(fn, inputs, out_defs, acc_defs=(), tm=512, name=None, after=()):
    n_in, n_out, n_acc = len(inputs), len(out_defs), len(acc_defs)
    n_after = len(after)
    in_specs, args = [], []
    for arr, width, cb in inputs:
        if arr.shape[0] != S:
            in_specs.append(pl.BlockSpec((arr.shape[0], width), lambda i, cb=cb: (0, cb)))
        else:
            in_specs.append(pl.BlockSpec((tm, width), lambda i, cb=cb: (i, cb)))
        args.append(arr)
    out_defs = [d if len(d) == 4 else (d[0], d[1], d[0], 0) for d in out_defs]
    out_shape = [SDS((S, ww), dt) for _, dt, ww, _ in out_defs] + [SDS((1, w), F32) for w in acc_defs]
    out_specs = ([pl.BlockSpec((tm, w), lambda i, cb=cb: (i, cb)) for w, _, _, cb in out_defs]
                 + [pl.BlockSpec((1, w), lambda i: (0, 0)) for w in acc_defs])

    def kern(*refs):
        vals = [r[...] for r in refs[:n_in]]
        outs, accs = fn(*vals)
        out_refs = refs[n_in + n_after:]
        for r, v in zip(out_refs[:n_out], outs):
            r[...] = v.astype(r.dtype)
        if n_acc:
            acc_refs = out_refs[n_out:]

            @pl.when(pl.program_id(0) == 0)
            def _():
                for r in acc_refs:
                    r[...] = jnp.zeros_like(r)

            for r, v in zip(acc_refs, accs):
                r[...] += jnp.sum(v, axis=0, keepdims=True)

    res = pl.pallas_call(
        kern, name=name, grid=(S // tm,), in_specs=in_specs + [pl.BlockSpec(memory_space=pl.ANY)] * n_after,
        out_specs=out_specs, out_shape=out_shape, compiler_params=_cparams(("arbitrary",)),
    )(*args, *after)
    return res


def _sigmoid(x):
    return 1.0 / (1.0 + jnp.exp(-x))


def _silu_and_grad(x):
    s = _sigmoid(x)
    return x * s, s * (1.0 + x * (1.0 - s))


_GELU_K = math.sqrt(2.0 / math.pi)
_GELU_C = 0.044715


def _gelu_and_grad(x):
    t = jnp.tanh(_GELU_K * (x + _GELU_C * (x * x * x)))
    cdf = 0.5 * (1.0 + t)
    grad = cdf + 0.5 * x * (1.0 - t * t) * (_GELU_K * (1.0 + 3.0 * _GELU_C * x * x))
    return x * cdf, grad


def _rms(xv, gain):
    r = lax.rsqrt(jnp.mean(xv * xv, axis=-1, keepdims=True) + RMS_EPS)
    return xv * r * gain


def _rms_bwd(dout, xv, gain):
    r = lax.rsqrt(jnp.mean(xv * xv, axis=-1, keepdims=True) + RMS_EPS)
    xhat = xv * r
    dxhat = dout * gain
    dx = r * (dxhat - xhat * jnp.mean(dxhat * xhat, axis=-1, keepdims=True))
    return dx, dout * xhat


def _norm_fwd(x, gain):
    (h,) = _rowwise(lambda xv, g: ((_rms(xv, g),), ()), [(x, D, 0), (gain, D, 0)], [(D, BF16)], name="norm_fwd")
    return h


def _post_fwd(x, y, gain, next_gain):
    def fn(xv, yv, g, gn):
        out = xv + _rms(yv, g)
        return (out, _rms(out, gn)), ()

    return _rowwise(fn, [(x, D, 0), (y, D, 0), (gain, D, 0), (next_gain, D, 0)], [(D, F32), (D, BF16)],
                    name="post_fwd")


def _post_fwd_loss(x, y, gain, tgt):
    def fn(xv, yv, g, tv):
        e = xv + _rms(yv, g) - tv
        return (e * (1.0 / D),), (e * e,)

    return _rowwise(fn, [(x, D, 0), (y, D, 0), (gain, D, 0), (tgt, D, 0)], [(D, F32)], [D], name="post_fwd_loss")


def _post_bwd(g, y, gain):
    def fn(gv, yv, gn):
        dx, dg = _rms_bwd(gv, yv, gn)
        return (dx,), (dg,)

    return _rowwise(fn, [(g, D, 0), (y, D, 0), (gain, D, 0)], [(D, BF16)], [D], name="post_bwd")


def _pre_bwd(g, dh, x, gain):
    def fn(gv, dhv, xv, gn):
        dx, dg = _rms_bwd(dhv, xv, gn)
        return (gv + dx,), (dg,)

    return _rowwise(fn, [(g, D, 0), (dh, D, 0), (x, D, 0), (gain, D, 0)], [(D, F32)], [D], name="pre_bwd")


def _pool(u_arr, col_block, transpose, out_dtype, into=None, tc=256):
    n_t = POOL_W // tc
    per_group = POOL_GROUP // tc

    def body(u_ref, *rest):
        o_ref = rest[-1]
        grp = pl.program_id(0) // per_group
        t = lax.broadcasted_iota(jnp.int32, (S, 1), 0)
        for g in range(POOL_W // POOL_GROUP):
            @pl.when(grp == g)
            def _(g=g):
                xv = u_ref[...]
                cnt = jnp.minimum(t + 1, 2 << g).astype(F32)
                cur = xv / cnt if transpose else xv
                for k in (1, 2, 4, 8)[:g + 1]:
                    if transpose:
                        cur = cur + jnp.where(t < S - k, pltpu.roll(cur, S - k, 0), 0.0)
                    else:
                        cur = cur + jnp.where(t >= k, pltpu.roll(cur, k, 0), 0.0)
                res = cur - xv if transpose else cur / cnt - xv
                o_ref[...] = res.astype(o_ref.dtype)

    in_specs = [pl.BlockSpec((S, tc), lambda c: (0, col_block * n_t + c))]
    args = [u_arr]
    if into is not None:
        in_specs.append(pl.BlockSpec(memory_space=pl.ANY))
        args.append(into)
    return pl.pallas_call(
        body, name="pool_bwd" if transpose else "pool_fwd", grid=(n_t,),
        in_specs=in_specs,
        out_specs=pl.BlockSpec((S, tc), lambda c: (0, c)),
        out_shape=SDS((S, POOL_W) if into is None else into.shape, out_dtype),
        input_output_aliases={} if into is None else {1: 0},
        compiler_params=_cparams(("parallel",)),
    )(*args)


def _rope_tables(zero):
    pos = jnp.arange(S, dtype=jnp.int32).astype(F32) + zero
    inv_freq = ROPE_THETA ** (-jnp.arange(0, ROT_DIM, 2, dtype=F32) / ROT_DIM)
    ang = pos[:, None] * inv_freq[None, :]
    cos8, sin8 = jnp.cos(ang), jnp.sin(ang)
    half = ROT_DIM // 2
    zeros = jnp.zeros((S, HEAD_DIM - ROT_DIM), F32)
    cos = jnp.concatenate([cos8, cos8, jnp.ones((S, HEAD_DIM - ROT_DIM), F32)], axis=1)
    sin = jnp.concatenate([-sin8, sin8, zeros], axis=1)
    rep = LANES // HEAD_DIM
    lane = jnp.arange(LANES)
    dim = lane % HEAD_DIM
    partner = jnp.where(dim < half, lane + half, jnp.where(dim < ROT_DIM, lane - half, -1))
    swap = (lane[:, None] == partner[None, :]).astype(BF16)
    return jnp.tile(cos, (1, rep)), jnp.tile(sin, (1, rep)), swap


def _rotate(xv, cos, sin, swap, transpose):
    rep = xv.shape[1] // LANES
    wide = lambda tab: jnp.concatenate([tab] * rep, axis=1)
    xb = xv.astype(BF16)
    partner = jnp.concatenate([lax.dot_general(xb[:, t * LANES:(t + 1) * LANES], swap, NN, preferred_element_type=F32)
                               for t in range(rep)], axis=1)
    mixed = partner * wide(sin)
    return xv * wide(cos) - mixed if transpose else xv * wide(cos) + mixed


def _qkv_prep(proj, tables):
    cos, sin, swap = tables

    def fn(x, c, s, sw):
        rot = _rotate(x[:, :2 * ATT_W], c, s, sw, False)
        return (jnp.concatenate([(rot[:, :ATT_W] * HEAD_DIM ** -0.5).astype(BF16), rot[:, ATT_W:].astype(BF16),
                                 x[:, 2 * ATT_W:].astype(BF16)], axis=1),), ()

    (qkv,) = _rowwise(fn, [(proj, 3 * ATT_W, 0), (cos, LANES, 0), (sin, LANES, 0), (swap, LANES, 0)],
                      [(3 * ATT_W, BF16)], name="qkv_prep")
    return qkv


ATT_T = 512


def _multiplicity(delta):
    ok = delta >= 0
    near = jnp.where(ok & (delta <= 128), 1.0, 0.0)
    mid = jnp.where(ok & (delta <= 512) & ((delta & 3) == 0), 1.0, 0.0)
    far = jnp.where(ok & ((delta & 15) == 0), 1.0, 0.0)
    return near + mid + far


def _attention_bias(zero):
    t = ATT_T
    pos = jnp.arange(t, dtype=jnp.int32) + jnp.asarray(zero).astype(jnp.int32)
    delta = jnp.arange(S // t, dtype=jnp.int32)[:, None, None] * t + pos[None, :, None] - pos[None, None, :]
    mult = _multiplicity(delta)
    return jnp.where(mult > 0.0, jnp.log(jnp.maximum(mult, 1.0)), -1e30).astype(F32)


def _head_split(v, first):
    zero = jnp.zeros_like(v)
    return [jnp.where(first, v, zero), jnp.where(first, zero, v)]


def _flash_fwd(qkv, bias):
    t = ATT_T
    n_hp = ATT_W // LANES

    def body(q_ref, k_ref, v_ref, b_ref, o_ref, lse_ref):
        i = pl.program_id(1)
        first = lax.broadcasted_iota(jnp.int32, (1, LANES), 1) < HEAD_DIM
        qs = _head_split(q_ref[...], first)

        def kv_step(j, carry):
            m0, l0, m1, l1, acc = carry
            off = pl.multiple_of(j * t, t)
            kb = k_ref[pl.ds(off, t), :]
            vs = _head_split(v_ref[pl.ds(off, t), :], first)
            bias_t = b_ref[i - j]
            new = []
            pv = None
            for h, (m_prev, l_prev) in enumerate(((m0, l0), (m1, l1))):
                s = lax.dot_general(qs[h], kb, NT, preferred_element_type=F32) + bias_t
                m_new = jnp.maximum(m_prev, jnp.max(s, axis=1, keepdims=True))
                p = jnp.exp(s - m_new)
                alpha = jnp.exp(m_prev - m_new)
                l_new = alpha * l_prev + jnp.sum(p, axis=1, keepdims=True)
                d = lax.dot_general(p.astype(BF16), vs[h], NN, preferred_element_type=F32)
                pv = d if pv is None else pv + d
                new.append((m_new, l_new, alpha))
            acc = acc * jnp.where(first, new[0][2], new[1][2]) + pv
            return new[0][0], new[0][1], new[1][0], new[1][1], acc

        neg = jnp.full((t, 1), -1e30, F32)
        zero = jnp.zeros((t, 1), F32)
        m0, l0, m1, l1, acc = lax.fori_loop(0, i + 1, kv_step, (neg, zero, neg, zero, jnp.zeros((t, LANES), F32)))
        o_ref[...] = acc * jnp.where(first, 1.0 / l0, 1.0 / l1)
        lse_ref[...] = jnp.where(first, m0 + jnp.log(l0), m1 + jnp.log(l1))

    blk = pl.BlockSpec((t, LANES), lambda hp, i: (i, hp))
    k_full = pl.BlockSpec((S, LANES), lambda hp, i: (0, n_hp + hp))
    v_full = pl.BlockSpec((S, LANES), lambda hp, i: (0, 2 * n_hp + hp))
    return pl.pallas_call(
        body, name="flash_fwd", grid=(n_hp, S // t),
        in_specs=[blk, k_full, v_full, pl.BlockSpec((S // t, t, t), lambda hp, i: (0, 0, 0))], out_specs=[blk, blk],
        out_shape=[SDS((S, ATT_W), F32), SDS((S, ATT_W), F32)],
        compiler_params=_cparams(("parallel", "arbitrary")),
    )(qkv, qkv, qkv, bias)


def _flash_bwd(qkv, o, do, lse, bias, after=()):
    t = ATT_T
    n_hp = ATT_W // LANES
    n_t = S // t

    def body(q_ref, k_ref, v_ref, o_ref, do_ref, lse_ref, b_ref, *rest):
        dq_ref, dk_ref, dv_ref = rest[-3:]
        j = pl.program_id(1)
        first = lax.broadcasted_iota(jnp.int32, (1, LANES), 1) < HEAD_DIM

        @pl.when(j == 0)
        def _():
            dq_ref[...] = jnp.zeros_like(dq_ref)

        kb = k_ref[...]
        vb = v_ref[...]
        ks = _head_split(kb, first)

        def q_step(i, carry):
            dk_acc, dv_acc = carry
            rows = pl.ds(pl.multiple_of(i * t, t), t)
            qs = _head_split(q_ref[rows, :], first)
            dob = do_ref[rows, :]
            prod = dob * o_ref[rows, :]
            d_all = jnp.sum(prod, axis=1, keepdims=True)
            d0 = jnp.sum(jnp.where(first, prod, 0.0), axis=1, keepdims=True)
            lse_b = lse_ref[rows, :]
            lse0 = jnp.max(jnp.where(first, lse_b, -jnp.inf), axis=1, keepdims=True)
            lse1 = jnp.max(jnp.where(first, -jnp.inf, lse_b), axis=1, keepdims=True)
            dos = _head_split(dob.astype(BF16), first)
            bias_t = b_ref[i - j]
            dq_t = jnp.zeros((t, LANES), F32)
            for h, (lse_h, d_h) in enumerate(((lse0, d0), (lse1, d_all - d0))):
                s = lax.dot_general(qs[h], kb, NT, preferred_element_type=F32)
                p = jnp.exp(s + (bias_t - lse_h))
                dp = lax.dot_general(dos[h], vb, NT, preferred_element_type=F32)
                ds = (p * (dp - d_h)).astype(BF16)
                dv_acc = dv_acc + lax.dot_general(p.astype(BF16), dos[h], TN, preferred_element_type=F32)
                dk_acc = dk_acc + lax.dot_general(ds, qs[h], TN, preferred_element_type=F32)
                dq_t = dq_t + lax.dot_general(ds, ks[h], NN, preferred_element_type=F32)
            dq_ref[rows, :] += dq_t
            return dk_acc, dv_acc

        zero = jnp.zeros((t, LANES), F32)
        dk_acc, dv_acc = lax.fori_loop(j, n_t, q_step, (zero, zero))
        dk_ref[...] = dk_acc
        dv_ref[...] = dv_acc

    blk = pl.BlockSpec((t, LANES), lambda hp, j: (j, hp))
    full = pl.BlockSpec((S, LANES), lambda hp, j: (0, hp))
    k_blk = pl.BlockSpec((t, LANES), lambda hp, j: (j, n_hp + hp))
    v_blk = pl.BlockSpec((t, LANES), lambda hp, j: (j, 2 * n_hp + hp))
    return pl.pallas_call(
        body, name="flash_bwd", grid=(n_hp, n_t),
        in_specs=([full, k_blk, v_blk, full, full, full, pl.BlockSpec((n_t, t, t), lambda hp, j: (0, 0, 0))]
                  + [pl.BlockSpec(memory_space=pl.ANY)] * len(after)),
        out_specs=[full, blk, blk],
        out_shape=[SDS((S, ATT_W), F32)] * 3,
        compiler_params=_cparams(("parallel", "arbitrary")),
    )(qkv, qkv, qkv, o, do, lse, bias, *after)


SCAN_T = 256
SCAN_GROUP = 8
SCAN_STEPS = (1, 2, 4)
ST_ROWS = 2 * N_CPLX // LANES
HALF = ST_ROWS // 2


def _scan_tables(lam_t):
    lam = lax.complex(lam_t[:HALF].reshape(N_CPLX), lam_t[HALF:].reshape(N_CPLX))
    pows = jnp.cumprod(jnp.broadcast_to(lam, (SCAN_GROUP, N_CPLX)), axis=0)
    shifts = jnp.asarray(SCAN_STEPS)
    sub = jnp.arange(SCAN_GROUP)[None, :, None]
    steps = pows[shifts - 1][:, None, :]
    fwd = jnp.concatenate([jnp.where(sub >= shifts[:, None, None], steps, 0.0), pows[None]], axis=0)
    bwd = jnp.concatenate([jnp.where(sub <= SCAN_GROUP - 1 - shifts[:, None, None], jnp.conj(steps), 0.0),
                           jnp.conj(pows)[None, ::-1]], axis=0)

    def pack(tabs):
        return jnp.concatenate([jnp.real(tabs), jnp.imag(tabs)], axis=-1).astype(F32)

    return pack(fwd), pack(bwd)


def _cmul_add(xr, xi, lr, li, sr, si):
    return xr + lr * sr - li * si, xi + lr * si + li * sr


def _group_scan(xr, xi, tab_ref, cr, ci, reverse):
    for j, k in enumerate(SCAN_STEPS):
        shift = SCAN_GROUP - k if reverse else k
        xr, xi = _cmul_add(xr, xi, tab_ref[j, :, :N_CPLX], tab_ref[j, :, N_CPLX:],
                           pltpu.roll(xr, shift, 0), pltpu.roll(xi, shift, 0))
    return _cmul_add(xr, xi, tab_ref[3, :, :N_CPLX], tab_ref[3, :, N_CPLX:],
                     jnp.broadcast_to(cr, (SCAN_GROUP, N_CPLX)), jnp.broadcast_to(ci, (SCAN_GROUP, N_CPLX)))


SSM_SUPER = 4
SB_ROWS = SSM_W // SSM_SUPER
SB_COLS = N_CPLX // SSM_SUPER


def _super_blocks():
    return [(slice(b * SB_ROWS, (b + 1) * SB_ROWS), slice(h * SB_COLS, (h + 1) * SB_COLS),
             slice(h * N_CPLX + b * SB_COLS, h * N_CPLX + (b + 1) * SB_COLS))
            for b in range(SSM_SUPER) for h in range(2)]


def _dot16(a, b, dims):
    return lax.dot_general(a.astype(BF16), b.astype(BF16), dims, preferred_element_type=F32)


def _s5_fwd(tab, u_arr, u_cols, w_b, w_ct):
    nc = N_CPLX

    def body(tab_ref, u_ref, wb_ref, wct_ref, st_ref, y_ref, carry, bu_scr):
        @pl.when(pl.program_id(0) == 0)
        def _():
            carry[...] = jnp.zeros_like(carry)

        for rows_b, cols_c, cols_s in _super_blocks():
            bu_scr[:, cols_s] = _dot16(u_ref[:, rows_b], wb_ref[rows_b, cols_c], NN)

        def group(a, c):
            rows = pl.ds(pl.multiple_of(a * SCAN_GROUP, SCAN_GROUP), SCAN_GROUP)
            xr, xi = _group_scan(bu_scr[rows, :nc], bu_scr[rows, nc:], tab_ref, c[0], c[1], False)
            st_ref[rows, :nc] = xr
            st_ref[rows, nc:] = xi
            return xr[SCAN_GROUP - 1:SCAN_GROUP, :], xi[SCAN_GROUP - 1:SCAN_GROUP, :]

        cr, ci = lax.fori_loop(0, SCAN_T // SCAN_GROUP, group, (carry[:, :nc], carry[:, nc:]), unroll=2)
        carry[:, :nc] = cr
        carry[:, nc:] = ci

        for b in range(SSM_SUPER):
            (rows_b, cols_re, st_re), (_, cols_im, st_im) = _super_blocks()[2 * b:2 * b + 2]
            y_ref[:, rows_b] = (_dot16(st_ref[:, st_re], wct_ref[rows_b, cols_re], NT)
                                + _dot16(st_ref[:, st_im], wct_ref[rows_b, cols_im], NT))

    const = lambda shape: pl.BlockSpec(shape, lambda i: (0,) * len(shape))
    return pl.pallas_call(
        body, name="s5_fwd", grid=(S // SCAN_T,),
        in_specs=[const((4, SCAN_GROUP, 2 * nc)), pl.BlockSpec((SCAN_T, SSM_W), lambda i: (i, u_cols[0] // SSM_W)),
                  const((SSM_W, 2 * SB_COLS)), const((SSM_W, 2 * SB_COLS))],
        out_specs=[pl.BlockSpec((SCAN_T, 2 * nc), lambda i: (i, 0)), pl.BlockSpec((SCAN_T, SSM_W), lambda i: (i, 0))],
        out_shape=[SDS((S, 2 * nc), F32), SDS((S, SSM_W), F32)],
        scratch_shapes=[pltpu.VMEM((1, 2 * nc), F32), pltpu.VMEM((SCAN_T, 2 * nc), F32)],
        compiler_params=_cparams(("arbitrary",)),
    )(tab, u_arr, w_b, w_ct)


def _s5_bwd(tab, dy, states, u_arr, u_cols, w_b, w_ct):
    n_blk = S // SCAN_T
    nc = N_CPLX

    def body(tab_ref, dy_ref, x_ref, u_ref, wb_ref, wct_ref, du_ref, dlam_ref, dwb_ref, dwct_ref,
             carry, acc, d_scr, g_ref):
        i = pl.program_id(0)

        @pl.when(i == 0)
        def _():
            carry[...] = jnp.zeros_like(carry)
            acc[...] = jnp.zeros_like(acc)
            dwb_ref[...] = jnp.zeros_like(dwb_ref)
            dwct_ref[...] = jnp.zeros_like(dwct_ref)

        for rows_b, cols_c, cols_s in _super_blocks():
            d_scr[:, cols_s] = _dot16(dy_ref[:, rows_b], wct_ref[rows_b, cols_c], NN)

        last_row = lax.broadcasted_iota(jnp.int32, (SCAN_GROUP, 1), 0) == SCAN_GROUP - 1
        d_ref = d_scr

        def group(j, c):
            cr, ci = c
            rows = pl.ds(pl.multiple_of((SCAN_T // SCAN_GROUP - 1 - j) * SCAN_GROUP, SCAN_GROUP), SCAN_GROUP)
            gr, gi = _group_scan(d_ref[rows, :nc], d_ref[rows, nc:], tab_ref, cr, ci, True)
            g_ref[rows, :nc] = gr
            g_ref[rows, nc:] = gi
            nr = jnp.where(last_row, jnp.broadcast_to(cr, (SCAN_GROUP, nc)), pltpu.roll(gr, SCAN_GROUP - 1, 0))
            ni = jnp.where(last_row, jnp.broadcast_to(ci, (SCAN_GROUP, nc)), pltpu.roll(gi, SCAN_GROUP - 1, 0))
            sr, si = x_ref[rows, :nc], x_ref[rows, nc:]
            acc[:, :nc] += nr * sr + ni * si
            acc[:, nc:] += ni * sr - nr * si
            return gr[0:1, :], gi[0:1, :]

        cr, ci = lax.fori_loop(0, SCAN_T // SCAN_GROUP, group, (carry[:, :nc], carry[:, nc:]), unroll=2)
        carry[:, :nc] = cr
        carry[:, nc:] = ci

        for b in range(SSM_SUPER):
            (rows_b, cols_re, st_re), (_, cols_im, st_im) = _super_blocks()[2 * b:2 * b + 2]
            du_ref[:, rows_b] = (_dot16(g_ref[:, st_re], wb_ref[rows_b, cols_re], NT)
                                 + _dot16(g_ref[:, st_im], wb_ref[rows_b, cols_im], NT))
            for cols_c, cols_s in ((cols_re, st_re), (cols_im, st_im)):
                dwb_ref[rows_b, cols_c] += _dot16(u_ref[:, rows_b], g_ref[:, cols_s], TN)
                dwct_ref[rows_b, cols_c] += _dot16(dy_ref[:, rows_b], x_ref[:, cols_s], TN)

        @pl.when(i == n_blk - 1)
        def _():
            dlam_ref[...] = jnp.sum(acc[...], axis=0, keepdims=True)

    const = lambda shape: pl.BlockSpec(shape, lambda i: (0,) * len(shape))
    rows = lambda width, col_block=0: pl.BlockSpec((SCAN_T, width), lambda i: (n_blk - 1 - i, col_block))
    maps = const((SSM_W, 2 * SB_COLS))
    return pl.pallas_call(
        body, name="s5_bwd", grid=(n_blk,),
        in_specs=[const((4, SCAN_GROUP, 2 * nc)), rows(SSM_W), rows(2 * nc), rows(SSM_W, u_cols[0] // SSM_W), maps, maps],
        out_specs=[rows(SSM_W), const((1, 2 * nc)), maps, maps],
        out_shape=[SDS((S, SSM_W), F32), SDS((1, 2 * nc), F32), SDS((SSM_W, 2 * SB_COLS), F32),
                   SDS((SSM_W, 2 * SB_COLS), F32)],
        scratch_shapes=[pltpu.VMEM((1, 2 * nc), F32), pltpu.VMEM((SCAN_GROUP, 2 * nc), F32),
                        pltpu.VMEM((SCAN_T, 2 * nc), F32), pltpu.VMEM((SCAN_T, 2 * nc), F32)],
        compiler_params=_cparams(("arbitrary",)),
    )(tab, dy, states, u_arr, w_b, w_ct)


def _ssm_prep(a_re, a_im, log_dt, b_re, b_im, c_re, c_im):
    lam = lax.complex(a_re, a_im)
    dt = jnp.exp(log_dt)[:, None]
    lam_bar = jnp.exp(lam * dt)
    b_bar = ((lam_bar - 1.0) / lam)[..., None] * lax.complex(b_re, b_im)
    lam_t = jnp.concatenate([jnp.real(lam_bar).reshape(HALF, LANES), jnp.imag(lam_bar).reshape(HALF, LANES)], axis=0)
    groups_per_super = SSM_GROUPS // SSM_SUPER
    on_diag = ((lax.broadcasted_iota(jnp.int32, (SSM_W, SB_COLS), 0) // SSM_GROUP) % groups_per_super
               == lax.broadcasted_iota(jnp.int32, (SSM_W, SB_COLS), 1) // SSM_STATE)

    repeat = (lax.broadcasted_iota(jnp.int32, (SSM_STATE, SB_COLS), 0)
              == lax.broadcasted_iota(jnp.int32, (SSM_STATE, SB_COLS), 1) % SSM_STATE).astype(F32)

    def compact(m):
        tiled = jnp.dot(m.reshape(SSM_W, SSM_STATE), repeat, precision=lax.Precision.HIGHEST)
        return jnp.where(on_diag, tiled, 0.0)

    w_b = jnp.concatenate([compact(jnp.real(b_bar).transpose(0, 2, 1)),
                           compact(jnp.imag(b_bar).transpose(0, 2, 1))], axis=1)
    w_ct = jnp.concatenate([compact(c_re), -compact(c_im)], axis=1)
    return lam_t, w_b, w_ct


U_SSM_COLS = (4 * ATT_W, SSM_W)


def _row(v):
    return v.reshape(1, -1)


def _even_fwd(x, h, tail, pre, post, w_in, late_w, glu_b, ssm_d, prep, tables):
    lam_t, w_b, w_ct = prep
    proj = _mm(h, w_in, "nn", F32, b_blocks=True)
    qkv = _qkv_prep(proj, tables[:3])
    att, lse = _flash_fwd(qkv, tables[3])
    w_out, glu_w = late_w(att)
    scan_fwd_tab, scan_bwd_tab = _scan_tables(lam_t)
    states, y = _s5_fwd(scan_fwd_tab, proj, U_SSM_COLS, w_b, w_ct)

    def act1(yv, uv, dv):
        return (_gelu_and_grad(yv + dv * uv)[0],), ()

    (z1,) = _rowwise(act1, [(y, SSM_W, 0), (proj, SSM_W, 8), (ssm_d, SSM_W, 0)], [(SSM_W, F32)], name="ssm_act_fwd")
    lin = _mm(z1, glu_w, "nn", F32)

    def gate(att_v, ga, gs, z1v, linv, bv):
        ssm_out = z1v * _sigmoid(linv + bv)
        return (jnp.concatenate([att_v * _silu_and_grad(ga)[0], ssm_out * _silu_and_grad(gs)[0]], axis=1),), ()

    (merged,) = _rowwise(gate, [(att, ATT_W, 0), (proj, ATT_W, 3), (proj, SSM_W, 9), (z1, SSM_W, 0),
                                (lin, SSM_W, 0), (glu_b, SSM_W, 0)], [(EVEN_OUT, BF16)], name="even_gate_fwd")
    yout = _mm(merged, w_out, "nn", F32)
    saved = (x, h, proj, qkv, att, lse, states, y, z1, lin, merged, yout, w_out, glu_w, scan_bwd_tab)
    return tail(x, yout, post) + (saved,)


def _even_bwd(g, saved, pre, post, w_in, late_w, glu_b, ssm_d, prep, tables, on_w, on_ssm):
    x, h, proj, qkv, att, lse, states, y, z1, lin, merged, yout, w_out, glu_w, scan_bwd_tab = saved
    lam_t, w_b, w_ct = prep
    dyout, dpost = _post_bwd(g, yout, post)
    dmerged = _mm(dyout, w_out, "nt", F32)
    dw_out = _mm(merged, dyout, "tn", BF16)

    def gate_bwd(dm_a, dm_s, att_v, ga, gs, z1v, linv, bv):
        sa, dsa = _silu_and_grad(ga)
        ss, dss = _silu_and_grad(gs)
        sig = _sigmoid(linv + bv)
        ssm_out = z1v * sig
        dssm = dm_s * ss
        dlin = dssm * z1v * sig * (1.0 - sig)
        return (dm_a * sa, dm_a * att_v * dsa, dm_s * ssm_out * dss, dssm * sig, dlin), (dlin,)

    datt, dg_att, dg_ssm, dz1a, dlin, dglu_b = _rowwise(
        gate_bwd, [(dmerged, ATT_W, 0), (dmerged, SSM_W, 2), (att, ATT_W, 0), (proj, ATT_W, 3), (proj, SSM_W, 9),
                   (z1, SSM_W, 0), (lin, SSM_W, 0), (glu_b, SSM_W, 0)],
        [(ATT_W, F32), (ATT_W, BF16), (SSM_W, BF16), (SSM_W, F32), (SSM_W, BF16)], [SSM_W], name="even_gate_bwd")
    dz1b = _mm(dlin, glu_w, "nt", F32)
    dglu_w = _mm(z1, dlin, "tn", BF16)

    def act1_bwd(da, db, yv, uv, dv):
        dpre = (da + db) * _gelu_and_grad(yv + dv * uv)[1]
        return (dpre, dpre * dv), (dpre * uv,)

    sent_late_w = on_w(dict(w_out=dw_out, glu_w=dglu_w))
    dy, du_direct, dd = _rowwise(act1_bwd, [(dz1a, SSM_W, 0), (dz1b, SSM_W, 0), (y, SSM_W, 0), (proj, SSM_W, 8),
                                            (ssm_d, SSM_W, 0)], [(SSM_W, BF16), (SSM_W, F32)], [SSM_W],
                                 name="ssm_act_bwd", after=(sent_late_w,))
    du_state, dlam_row, dw_b, dw_ct = _s5_bwd(scan_bwd_tab, dy, states, proj, U_SSM_COLS, w_b, w_ct)
    dlam = jnp.concatenate([dlam_row[0, :N_CPLX].reshape(HALF, LANES), dlam_row[0, N_CPLX:].reshape(HALF, LANES)],
                           axis=0)
    sent_ssm = on_ssm((dlam, dw_b, dw_ct))
    dq, dk, dv = _flash_bwd(qkv, att, datt, lse, tables[3], after=() if sent_ssm is None else (sent_ssm,))

    def assemble(dqv, dkv, dvv, dga, dua, dub, dgs, c, s, sw):
        rot = _rotate(jnp.concatenate([dqv, dkv], axis=1), c, s, sw, True)
        return (jnp.concatenate([(rot[:, :ATT_W] * HEAD_DIM ** -0.5).astype(BF16), rot[:, ATT_W:].astype(BF16),
                                 dvv.astype(BF16), dga, (dua + dub).astype(BF16), dgs], axis=1),), ()

    (dproj,) = _rowwise(assemble, [(dq, ATT_W, 0), (dk, ATT_W, 0), (dv, ATT_W, 0), (dg_att, ATT_W, 0),
                                   (du_state, SSM_W, 0), (du_direct, SSM_W, 0), (dg_ssm, SSM_W, 0),
                                   (tables[0], LANES, 0), (tables[1], LANES, 0), (tables[2], LANES, 0)],
                        [(EVEN_IN, BF16)], name="dproj_assemble")
    dw_in = _mm(h, dproj, "tn", BF16, out_blocks=True)
    sent = on_w(dict(w_in=dw_in))
    dh = _mm(dproj, w_in, "nt", F32, b_blocks=True, after=(sent,))
    g_prev, dpre = _pre_bwd(g, dh, x, pre)
    return g_prev, dict(pre=dpre, post=dpost, glu_b=dglu_b, ssm_d=dd)


def _odd_fwd(x, h, tail, pre, post, w_in, pool_w, pool_scale, w_out):
    proj = _mm(h, w_in, "nn", F32, b_blocks=True)
    mixed = _pool(proj, 0, False, BF16)
    ylin = _gmm(mixed, pool_w, "nn", F32)

    def gate(yl, gt, sc):
        return (yl * sc * _silu_and_grad(gt)[0],), ()

    (z,) = _rowwise(gate, [(ylin, POOL_W, 0), (proj, POOL_W, 1), (pool_scale, POOL_W, 0)], [(POOL_W, BF16)],
                    name="odd_gate_fwd")
    yout = _mm(z, w_out, "nn", F32)
    return tail(x, yout, post) + ((x, h, proj, mixed, ylin, z, yout),)


def _odd_bwd(g, saved, pre, post, w_in, pool_w, pool_scale, w_out, on_w):
    x, h, proj, mixed, ylin, z, yout = saved
    dyout, dpost = _post_bwd(g, yout, post)
    dz = _mm(dyout, w_out, "nt", F32)
    dw_out = _mm(z, dyout, "tn", BF16)

    def gate_bwd(dzv, yl, gt, sc):
        sg, dsg = _silu_and_grad(gt)
        tt = dzv * sg
        return (tt * sc, dzv * yl * sc * dsg), (tt * yl,)

    dylin, dproj_gate, dscale = _rowwise(gate_bwd, [(dz, POOL_W, 0), (ylin, POOL_W, 0), (proj, POOL_W, 1),
                                                    (pool_scale, POOL_W, 0)],
                                         [(POOL_W, BF16), (POOL_W, BF16, ODD_IN, 1)], [POOL_W], name="odd_gate_bwd")
    dmixed = _gmm(dylin, pool_w, "nt", F32)
    dpool_w = _gmm(mixed, dylin, "tn", BF16)
    dproj = _pool(dmixed, 0, True, BF16, into=dproj_gate)
    dw_in = _mm(h, dproj, "tn", BF16, out_blocks=True)
    sent = on_w(dict(w_in=dw_in, w_out=dw_out, pool_w=dpool_w))
    dh = _mm(dproj, w_in, "nt", F32, b_blocks=True, after=(sent,))
    g_prev, dpre = _pre_bwd(g, dh, x, pre)
    return g_prev, dict(pre=dpre, post=dpost, pool_scale=dscale)


def _my_index():
    return 4 * lax.axis_index("x") + 2 * lax.axis_index("y") + lax.axis_index("c")


HBM_SPEC = pl.BlockSpec(memory_space=pltpu.HBM)
SEM_SPEC = pl.BlockSpec(memory_space=pltpu.SEMAPHORE)
SPLIT_EFFECT = pltpu.SideEffectType.DATAFLOW_SIDE_EFFECTING


def _device_of(j):
    return (j // 4, (j // 2) % 2, j % 2)


def _split_copy(srcs, lands, send_sems, recv_sems, gather, i, j, dst_slot, recv_slot):
    return pltpu.make_async_remote_copy(
        src_ref=srcs[i] if gather else srcs[i].at[j], dst_ref=lands[i].at[dst_slot],
        send_sem=send_sems.at[i * N_DEV + j], recv_sem=recv_sems.at[i * N_DEV + recv_slot],
        device_id=_device_of(j), device_id_type=MESH_ID)


def _own_copy(srcs, lands, send_sems, gather, i, me):
    return pltpu.make_async_copy(srcs[i] if gather else srcs[i].at[me], lands[i].at[me], send_sems.at[i * N_DEV + me])


def _xchg_start(name, srcs, gather, after=()):
    n = len(srcs)
    n_in = n + len(after)

    def body(*refs):
        src_refs = refs[:n]
        send_sems, recv_sems, token = refs[n_in], refs[n_in + 1], refs[-1]
        land_refs = refs[n_in + 2 + n:n_in + 2 + 2 * n]
        me = _my_index()
        for j in range(N_DEV):
            @pl.when(me != j)
            def _(j=j):
                for i in range(n):
                    _split_copy(src_refs, land_refs, send_sems, recv_sems, gather, i, j, me, me).start()
        for i in range(n):
            _own_copy(src_refs, land_refs, send_sems, gather, i, me).start()
        token[...] = jnp.zeros_like(token)

    land_shapes = [((N_DEV,) + a.shape) if gather else a.shape for a in srcs]
    thru = ([pltpu.HBM(a.shape, a.dtype) for a in srcs] + [pltpu.HBM(s, a.dtype) for s, a in zip(land_shapes, srcs)])
    res = pl.pallas_call(
        body, name=name,
        out_shape=(pltpu.SemaphoreType.DMA((n * N_DEV,)), pltpu.SemaphoreType.DMA((n * N_DEV,)), *thru,
                   SDS((8, LANES), F32)),
        in_specs=[HBM_SPEC] * n + [pl.BlockSpec(memory_space=pl.ANY)] * len(after),
        out_specs=(SEM_SPEC, SEM_SPEC, *([HBM_SPEC] * (2 * n)), pl.BlockSpec(memory_space=pltpu.VMEM)),
        input_output_aliases={i: 2 + i for i in range(n)},
        compiler_params=pltpu.CompilerParams(has_side_effects=SPLIT_EFFECT),
    )(*[pltpu.with_memory_space_constraint(a, pltpu.HBM) for a in srcs], *after)
    return res[0], res[1], list(res[2:2 + n]), list(res[2 + n:2 + 2 * n]), res[-1]


def _xchg_wait(name, started, gather, after):
    send_sems, recv_sems, srcs, lands, _ = started
    n = len(srcs)

    def body(*refs):
        src_refs, land_refs = refs[:n], refs[n:2 * n]
        send_r, recv_r = refs[2 * n], refs[2 * n + 1]
        me = _my_index()
        for j in range(N_DEV):
            @pl.when(me != j)
            def _(j=j):
                for i in range(n):
                    _split_copy(src_refs, land_refs, send_r, recv_r, gather, i, j, me, me).wait_send()
                    _split_copy(src_refs, land_refs, send_r, recv_r, gather, i, j, j, j).wait_recv()
        for i in range(n):
            _own_copy(src_refs, land_refs, send_r, gather, i, me).wait()

    thru = [pltpu.HBM(a.shape, a.dtype) for a in list(srcs) + list(lands)]
    res = pl.pallas_call(
        body, name=name, out_shape=tuple(thru),
        in_specs=[HBM_SPEC] * (2 * n) + [SEM_SPEC, SEM_SPEC] + [pl.BlockSpec(memory_space=pl.ANY)] * len(after),
        out_specs=tuple([HBM_SPEC] * (2 * n)),
        input_output_aliases={i: i for i in range(2 * n)},
        compiler_params=pltpu.CompilerParams(has_side_effects=SPLIT_EFFECT),
    )(*srcs, *lands, send_sems, recv_sems, *after)
    return list(res[n:])


def _adam_layer(w, slots, m, v, layer, name, into=None):
    n_l, r, c = w.shape
    ns = slots.shape[0]
    tr = r
    while tr * c * 4 > (1 << 20) and tr % 16 == 0:
        tr //= 2
    assert r % tr == 0

    def body(w_ref, g_ref, m_ref, v_ref, *rest):
        go_ref, d_ref, mo_ref, vo_ref = rest[-4:]
        g = g_ref[0].astype(F32)
        for s in range(1, ns):
            g = g + g_ref[s].astype(F32)
        mn = ADAM_B1 * m_ref[...] + (1.0 - ADAM_B1) * g
        vn = ADAM_B2 * v_ref[...] + (1.0 - ADAM_B2) * (g * g)
        m_hat = mn / (1.0 - ADAM_B1 ** ADAM_STEP)
        v_hat = vn / (1.0 - ADAM_B2 ** ADAM_STEP)
        go_ref[...] = g
        d_ref[...] = -ADAM_LR * (m_hat / (jnp.sqrt(v_hat) + ADAM_EPS) + ADAM_WD * w_ref[...])
        mo_ref[...] = mn
        vo_ref[...] = vn

    blk = pl.BlockSpec((None, tr, c), lambda i: (layer, i, 0))
    earlier = () if into is None else tuple(into)
    return pl.pallas_call(
        body, name=name, grid=(r // tr,),
        in_specs=[blk, pl.BlockSpec((ns, tr, c), lambda i: (0, i, 0)), blk, blk]
        + [pl.BlockSpec(memory_space=pl.ANY)] * len(earlier),
        out_specs=[blk] * 4, out_shape=[SDS((n_l, r, c), F32)] * 4,
        input_output_aliases={4 + q: q for q in range(len(earlier))},
        compiler_params=_cparams(("arbitrary",)),
    )(*_in_hbm((w, slots, m, v)), *earlier)


def _adam(w, gslots, m, v, name):
    r, c = w.shape
    ns = gslots.shape[0]
    tr = r
    while tr * c * 4 > (1 << 20) and tr % 16 == 0:
        tr //= 2
    assert r % tr == 0

    def body(w_ref, g_ref, m_ref, v_ref, go_ref, d_ref, mo_ref, vo_ref):
        g = g_ref[0].astype(F32)
        for s in range(1, ns):
            g = g + g_ref[s].astype(F32)
        wv = w_ref[...]
        mn = ADAM_B1 * m_ref[...] + (1.0 - ADAM_B1) * g
        vn = ADAM_B2 * v_ref[...] + (1.0 - ADAM_B2) * (g * g)
        m_hat = mn / (1.0 - ADAM_B1 ** ADAM_STEP)
        v_hat = vn / (1.0 - ADAM_B2 ** ADAM_STEP)
        go_ref[...] = g
        d_ref[...] = -ADAM_LR * (m_hat / (jnp.sqrt(v_hat) + ADAM_EPS) + ADAM_WD * wv)
        mo_ref[...] = mn
        vo_ref[...] = vn

    blk = pl.BlockSpec((tr, c), lambda i: (i, 0))
    return pl.pallas_call(
        body, name=name, grid=(r // tr,),
        in_specs=[blk, pl.BlockSpec((ns, tr, c), lambda i: (0, i, 0)), blk, blk],
        out_specs=[blk] * 4, out_shape=[SDS((r, c), F32)] * 4,
        compiler_params=_cparams(("parallel",)),
    )(w, gslots, m, v)


def _sum_slots(slots, name):
    ns, r, c = slots.shape

    def body(g_ref, o_ref):
        g = g_ref[0]
        for s in range(1, ns):
            g = g + g_ref[s]
        o_ref[...] = g

    return pl.pallas_call(
        body, name=name, grid=(1,),
        in_specs=[pl.BlockSpec((ns, r, c), lambda i: (0, 0, 0))], out_specs=pl.BlockSpec((r, c), lambda i: (0, 0)),
        out_shape=SDS((r, c), F32), compiler_params=_cparams(("arbitrary",)),
    )(slots)


def _adam_params(params, name):
    n = len(params)

    def body(*refs):
        ins, outs = refs[:5 * n], refs[5 * n:]
        for p in range(n):
            w_ref, m_ref, v_ref, g_first, g_rest = ins[5 * p:5 * p + 5]
            go_ref, d_ref, mo_ref, vo_ref = outs[4 * p:4 * p + 4]
            for part, g_ref in ((slice(0, 1), g_first), (slice(1, w_ref.shape[0]), g_rest)):
                g = g_ref[...]
                mn = ADAM_B1 * m_ref[part] + (1.0 - ADAM_B1) * g
                vn = ADAM_B2 * v_ref[part] + (1.0 - ADAM_B2) * (g * g)
                m_hat = mn / (1.0 - ADAM_B1 ** ADAM_STEP)
                v_hat = vn / (1.0 - ADAM_B2 ** ADAM_STEP)
                go_ref[part] = g
                d_ref[part] = -ADAM_LR * (m_hat / (jnp.sqrt(v_hat) + ADAM_EPS) + ADAM_WD * w_ref[part])
                mo_ref[part] = mn
                vo_ref[part] = vn

    def whole(a):
        return pl.BlockSpec(a.shape, lambda i, nd=a.ndim: (0,) * nd)

    flat = _in_hbm([a for prm in params for a in prm])
    outs = pl.pallas_call(
        body, name=name, grid=(1,),
        in_specs=[whole(a) for a in flat],
        out_specs=[whole(prm[0]) for prm in params for _ in range(4)],
        out_shape=[SDS(prm[0].shape, F32) for prm in params for _ in range(4)],
        compiler_params=_cparams(("arbitrary",)),
    )(*flat)
    return [outs[4 * p:4 * p + 4] for p in range(n)]


SMALL_NAMES = ("pre_norm", "post_norm", "ssm_a_re", "ssm_a_im", "ssm_log_dt", "ssm_b_re", "ssm_b_im", "ssm_c_re",
               "ssm_c_im", "ssm_d", "ssm_glu_b")
SSM_NAMES = ("ssm_a_re", "ssm_a_im", "ssm_log_dt", "ssm_b_re", "ssm_b_im", "ssm_c_re", "ssm_c_im")
WEIGHT_ORDER = ("pre_norm", "post_norm", "even_w_in", "even_w_out", "ssm_a_re", "ssm_a_im", "ssm_log_dt", "ssm_b_re",
                "ssm_b_im", "ssm_c_re", "ssm_c_im", "ssm_d", "ssm_glu_w", "ssm_glu_b", "odd_w_in", "pool_w",
                "pool_scale", "odd_w_out")
PACK_ROWS_ALIGN = 8


def _pack(parts):
    flat = jnp.concatenate([p.reshape(-1).astype(F32) for p in parts])
    rows = -(-flat.shape[0] // (LANES * PACK_ROWS_ALIGN)) * PACK_ROWS_ALIGN
    return jnp.pad(flat, (0, rows * LANES - flat.shape[0])).reshape(rows, LANES)


def _unpack(packed, shapes):
    flat = packed.reshape(-1)
    out, off = [], 0
    for shp in shapes:
        size = math.prod(shp)
        out.append(flat[off:off + size].reshape(shp))
        off += size
    return out


EVEN_SHARDED = ("w_in", "w_out", "glu_w")
ODD_SHARDED = ("w_in", "pool_w", "w_out")
FAMILY = {(0, "w_in"): "even_w_in", (0, "w_out"): "even_w_out", (0, "glu_w"): "ssm_glu_w",
          (1, "w_in"): "odd_w_in", (1, "pool_w"): "pool_w", (1, "w_out"): "odd_w_out"}


def _sharded_keys(layer):
    return EVEN_SHARDED if layer % 2 == 0 else ODD_SHARDED


def _local_step(x, tgt, small, get_weights, on_w, on_ssm, on_grads, zero=0.0):
    tables = _rope_tables(zero) + (_attention_bias(zero),)
    preps, prep_vjps = [], []
    for i in range(2):
        out, vjp = jax.vjp(_ssm_prep, small["ssm_a_re"][i] + zero, small["ssm_a_im"][i], small["ssm_log_dt"][i],
                           small["ssm_b_re"][i], small["ssm_b_im"][i], small["ssm_c_re"][i], small["ssm_c_im"][i])
        preps.append(out)
        prep_vjps.append(vjp)

    def layer_args(layer, wts):
        i = layer // 2
        pre, post = _row(small["pre_norm"][layer]) + wts.get("token", 0.0), _row(small["post_norm"][layer])
        if layer % 2 == 0:
            return (pre, post, wts["w_in"], wts["late"], _row(small["ssm_glu_b"][i]), _row(small["ssm_d"][i]),
                    preps[i], tables)
        return (pre, post, wts["w_in"], wts["pool_w"], _row(wts["pool_scale"]), wts["w_out"])

    saved, args = [], []
    cur = x
    for layer in range(4):
        after = (cur,) if layer else (cur, tables[0], tables[3], preps[0][1], preps[0][2], preps[1][1], preps[1][2])
        args.append(layer_args(layer, get_weights(layer, after)))
        if layer == 0:
            h = _norm_fwd(cur, args[0][0])
        if layer < 3:
            def tail(xv, yv, post, next_gain=_row(small["pre_norm"][layer + 1])):
                return tuple(_post_fwd(xv, yv, post, next_gain))
        else:
            def tail(xv, yv, post):
                return tuple(_post_fwd_loss(xv, yv, post, tgt))
        cur, h, sv = (_even_fwd if layer % 2 == 0 else _odd_fwd)(cur, h, tail, *args[layer])
        saved.append(sv)
    g, sq = cur, h
    loss = 0.5 * jnp.sum(sq) / D

    lg = [None] * 4
    token = jnp.zeros((), F32)
    for layer in reversed(range(4)):
        largs = list(args[layer])
        largs[1] = largs[1] + token
        hooks = dict(on_w=functools.partial(on_w, layer))
        ssm_grads = []
        if layer % 2 == 0:
            def ssm_hook(cotangents, layer=layer):
                ssm_grads.append(prep_vjps[layer // 2](cotangents))
                return on_ssm(layer, ssm_grads[0])

            hooks["on_ssm"] = ssm_hook
        g, lg[layer] = (_even_bwd if layer % 2 == 0 else _odd_bwd)(g, saved[layer], *largs, **hooks)
        if ssm_grads:
            lg[layer]["ssm"] = ssm_grads[0]
        token = on_grads(layer, lg[layer])
    return loss, g, token


def _to_slots(key, gfull):
    if key == "w_in":
        return gfull
    if key in ("w_out", "glu_w"):
        rr, nn = gfull.shape
        return gfull.reshape(N_DEV, rr // N_DEV, nn)
    assert key == "pool_w"
    gg, rr, nn = gfull.shape
    return gfull.reshape(gg, N_DEV, rr // N_DEV, nn).transpose(1, 0, 2, 3)


def _from_gathered(key, gat):
    if key == "w_in":
        return gat
    if key in ("w_out", "glu_w"):
        _, rr, nn = gat.shape
        return gat.reshape(N_DEV * rr, nn)
    assert key == "pool_w"
    _, gg, rr, nn = gat.shape
    return gat.transpose(1, 0, 2, 3).reshape(gg, N_DEV * rr, nn)


def kernel(x, pre_norm, post_norm, even_w_in, even_w_out, ssm_a_re, ssm_a_im, ssm_log_dt, ssm_b_re, ssm_b_im, ssm_c_re, ssm_c_im, ssm_d, ssm_glu_w, ssm_glu_b, odd_w_in, pool_w, pool_scale, odd_w_out, loss_target, m_pre_norm, m_post_norm, m_even_w_in, m_even_w_out, m_ssm_a_re, m_ssm_a_im, m_ssm_log_dt, m_ssm_b_re, m_ssm_b_im, m_ssm_c_re, m_ssm_c_im, m_ssm_d, m_ssm_glu_w, m_ssm_glu_b, m_odd_w_in, m_pool_w, m_pool_scale, m_odd_w_out, v_pre_norm, v_post_norm, v_even_w_in, v_even_w_out, v_ssm_a_re, v_ssm_a_im, v_ssm_log_dt, v_ssm_b_re, v_ssm_b_im, v_ssm_c_re, v_ssm_c_im, v_ssm_d, v_ssm_glu_w, v_ssm_glu_b, v_odd_w_in, v_pool_w, v_pool_scale, v_odd_w_out):
    w = dict(pre_norm=pre_norm, post_norm=post_norm, even_w_in=even_w_in, even_w_out=even_w_out, ssm_a_re=ssm_a_re,
             ssm_a_im=ssm_a_im, ssm_log_dt=ssm_log_dt, ssm_b_re=ssm_b_re, ssm_b_im=ssm_b_im, ssm_c_re=ssm_c_re,
             ssm_c_im=ssm_c_im, ssm_d=ssm_d, ssm_glu_w=ssm_glu_w, ssm_glu_b=ssm_glu_b, odd_w_in=odd_w_in,
             pool_w=pool_w, pool_scale=pool_scale, odd_w_out=odd_w_out)
    mom = dict(pre_norm=m_pre_norm, post_norm=m_post_norm, even_w_in=m_even_w_in, even_w_out=m_even_w_out,
               ssm_a_re=m_ssm_a_re, ssm_a_im=m_ssm_a_im, ssm_log_dt=m_ssm_log_dt, ssm_b_re=m_ssm_b_re,
               ssm_b_im=m_ssm_b_im, ssm_c_re=m_ssm_c_re, ssm_c_im=m_ssm_c_im, ssm_d=m_ssm_d, ssm_glu_w=m_ssm_glu_w,
               ssm_glu_b=m_ssm_glu_b, odd_w_in=m_odd_w_in, pool_w=m_pool_w, pool_scale=m_pool_scale,
               odd_w_out=m_odd_w_out)
    var = dict(pre_norm=v_pre_norm, post_norm=v_post_norm, even_w_in=v_even_w_in, even_w_out=v_even_w_out,
               ssm_a_re=v_ssm_a_re, ssm_a_im=v_ssm_a_im, ssm_log_dt=v_ssm_log_dt, ssm_b_re=v_ssm_b_re,
               ssm_b_im=v_ssm_b_im, ssm_c_re=v_ssm_c_re, ssm_c_im=v_ssm_c_im, ssm_d=v_ssm_d, ssm_glu_w=v_ssm_glu_w,
               ssm_glu_b=v_ssm_glu_b, odd_w_in=v_odd_w_in, pool_w=v_pool_w, pool_scale=v_pool_scale,
               odd_w_out=v_odd_w_out)
    me = _my_index()
    scale_cols = pool_scale.shape[1]

    def shards_of(layer, keys):
        i = layer // 2
        shards = [w[FAMILY[(layer % 2, k)]][i].astype(BF16) for k in keys]
        if layer % 2 == 1:
            shards.append(jnp.pad(pool_scale[i][None], ((0, PACK_ROWS_ALIGN - 1), (0, 0))))
        return shards

    def start_gather(tag, after=()):
        return _xchg_start(f"gather_start_{tag}", shards[tag], True, after)

    shards = {0: shards_of(0, EVEN_SHARDED[:1]), "0_late": shards_of(0, EVEN_SHARDED[1:])}
    shards.update({layer: shards_of(layer, _sharded_keys(layer)) for layer in (1, 2, 3)})
    gather_started = {0: start_gather(0)}
    small = {nm: w[nm] for nm in SMALL_NAMES}
    packed_names = ("pre_norm", "post_norm") + SSM_NAMES + ("ssm_d", "ssm_glu_b")
    tails = {nm: (SSM_GROUPS, SSM_STATE * SSM_GROUP) if nm in ("ssm_b_re", "ssm_b_im") else w[nm].shape[1:]
             for nm in packed_names}
    dense = lambda nm, a: a.reshape((a.shape[0],) + tails[nm])
    small_operands = {nm: tuple(dense(nm, tree[nm]) for tree in (w, mom, var)) for nm in packed_names}
    early_work = ([a for tag in ("0_late", 1, 2, 3) for a in shards[tag]]
                  + [a for nm in ("ssm_b_re", "ssm_b_im") for a in small_operands[nm]])

    def get_weights(layer, after):
        keys = EVEN_SHARDED[:1] if layer == 0 else _sharded_keys(layer)
        if layer == 0:
            after = tuple(after) + tuple(early_work)
        lands = _xchg_wait(f"gather_wait_{layer}", gather_started[layer], True, after)
        wts = {k: _from_gathered(k, gat) for k, gat in zip(keys, lands)}
        if layer % 2 == 1:
            wts["pool_scale"] = lands[-1][:, 0, :].reshape(N_DEV * scale_cols)
        if layer == 0:
            prev = gather_started["0_late"] = start_gather("0_late", after=(lands[0],))
            for later in (1, 2, 3):
                prev = gather_started[later] = start_gather(later, after=(prev[4],))
            wts["token"] = sum(gather_started[tag][4][0, 0] for tag in ("0_late", 1, 2, 3))

            def late(after_late):
                late_lands = _xchg_wait("gather_wait_0_late", gather_started["0_late"], True, (after_late,))
                return tuple(_from_gathered(k, gat) for k, gat in zip(EVEN_SHARDED[1:], late_lands))

            wts["late"] = late
        elif layer == 2:
            wts["late"] = lambda after_late: (wts["w_out"], wts["glu_w"])
        return wts

    scatter_started = []

    def on_w(layer, gw):
        keys = tuple(k for k in _sharded_keys(layer) if k in gw)
        started = _xchg_start(f"scatter_start_{layer}_{keys[0]}", [_to_slots(k, gw[k]) for k in keys], False)
        scatter_started.append((layer, keys, started))
        return started[4]

    def wait_scatters(layers, after):
        for layer, keys, started in scatter_started:
            if layer in layers:
                lands = _xchg_wait(f"scatter_wait_{layer}_{keys[0]}", started, False, after)
                for k, land in zip(keys, lands):
                    recv[(layer, k)] = land

    layer_grads = {}
    early_started, mid_started = [], []

    def on_ssm(layer, ssm_grads):
        if layer != 0:
            return None
        mid_started.append(_xchg_start("mid_start", [_pack(list(ssm_grads))], True))
        return mid_started[0][4]

    def on_grads(layer, lg):
        layer_grads[layer] = lg
        zero = jnp.zeros((), F32)
        if layer == 1:
            lgs = layer_grads
            early = ([jnp.concatenate([lgs[l][k] for l in (1, 2, 3)], axis=0) for k in ("pre", "post")]
                     + list(lgs[2]["ssm"]) + [lgs[2]["ssm_d"], lgs[2]["glu_b"],
                                              jnp.concatenate([lgs[1]["pool_scale"], lgs[3]["pool_scale"]], axis=0)])
            early_started.append(_xchg_start("small_start", [_pack(early)], True))
            zero = zero + early_started[0][4][0, 0]
        return zero

    loss_local, grad_x, token = _local_step(x[0], loss_target[0], small, get_weights, on_w, on_ssm, on_grads,
                                            zero=gather_started[0][4][0, 0])

    lg0 = layer_grads[0]
    late_started = _xchg_start("late_start", [_pack([lg0["pre"], lg0["post"], lg0["ssm_d"], lg0["glu_b"],
                                                     loss_local.reshape(1)]) + token], True)

    def adam_family(parity, k, which, into=None):
        nm = FAMILY[(parity, k)]
        cols = w[nm].shape[-1]
        return _adam_layer(w[nm].reshape(2, -1, cols), recv[(parity + 2 * which, k)].reshape(N_DEV, -1, cols),
                           mom[nm].reshape(2, -1, cols), var[nm].reshape(2, -1, cols), which,
                           f"adam_{nm}_{which}", into)

    recv, res = {}, {}
    wait_scatters((3, 2, 1), (late_started[4],))
    for k in ODD_SHARDED:
        res[FAMILY[(1, k)]] = adam_family(1, k, 1, adam_family(1, k, 0))
    half_done = {k: adam_family(0, k, 1) for k in EVEN_SHARDED}
    odd_done = tuple(half_done[k][0] for k in EVEN_SHARDED)

    (early_slots,) = _xchg_wait("small_wait", early_started[0], True, odd_done)
    (mid_slots,) = _xchg_wait("mid_wait", mid_started[0], True, odd_done)
    early_shapes = [(w[nm].shape[0] - 1,) + tails[nm] for nm in packed_names] + [(2, N_DEV * scale_cols)]
    g_early = _unpack(_sum_slots(early_slots, "sum_small_early"), early_shapes)
    g_mid = _unpack(_sum_slots(mid_slots, "sum_small_mid"), [(1,) + tails[nm] for nm in SSM_NAMES])

    (late_slots,) = _xchg_wait("late_wait", late_started, True, (g_early[0], g_mid[0]))
    wait_scatters((0,), (late_slots,))
    for k in EVEN_SHARDED:
        res[FAMILY[(0, k)]] = adam_family(0, k, 0, half_done[k])
    for nm in FAMILY.values():
        res[nm] = [o.reshape(w[nm].shape) for o in res[nm]]

    late_names = ("pre_norm", "post_norm", "ssm_d", "ssm_glu_b")
    g_late = _unpack(_sum_slots(late_slots, "sum_small_late"), [(1,) + tails[nm] for nm in late_names] + [(1,)])
    g_first = dict(zip(late_names, g_late))
    g_first.update(zip(SSM_NAMES, g_mid))
    outs = _adam_params([small_operands[nm] + (g_first[nm], g_early[j]) for j, nm in enumerate(packed_names)],
                        "adam_small")
    for nm, four in zip(packed_names, outs):
        res[nm] = [o.reshape(w[nm].shape) for o in four]
    loss = g_late[-1].reshape(())
    g_scale = lax.dynamic_slice_in_dim(g_early[-1], me * scale_cols, scale_cols, axis=1)
    pad = ((0, PACK_ROWS_ALIGN - 2), (0, 0))
    outs = _adam(jnp.pad(pool_scale, pad), jnp.pad(g_scale, pad)[None], jnp.pad(m_pool_scale, pad),
                 jnp.pad(v_pool_scale, pad), name="adam_pool_scale")
    res["pool_scale"] = [o[:2] for o in outs]

    out = [loss, grad_x[None]]
    for kind in range(4):
        out += [res[nm][kind] for nm in WEIGHT_ORDER]
    return tuple(out)
```

```python
import functools
import math

import jax
import jax.numpy as jnp
from jax import lax
from jax.experimental import pallas as pl
from jax.experimental.pallas import tpu as pltpu

F32 = jnp.float32
BF16 = jnp.bfloat16
SDS = jax.ShapeDtypeStruct

N_DEV = 8
S = 2048
D = 1024
HEAD_DIM = 64
ROT_DIM = 16
ROPE_THETA = 500000.0
ATT_W = 1024
SSM_W = 512
SSM_GROUPS = 32
SSM_GROUP = 16
SSM_STATE = 64
N_CPLX = SSM_GROUPS * SSM_STATE
POOL_W = 2048
POOL_GROUP = 512
EVEN_IN = 5120
EVEN_OUT = 1536
ODD_IN = 4096
RMS_EPS = 1e-6
LANES = 128
VMEM_LIMIT = 48 * 1024 * 1024

ADAM_LR = 0.001
ADAM_B1 = 0.9
ADAM_B2 = 0.999
ADAM_EPS = 1e-08
ADAM_WD = 0.01
ADAM_STEP = 10

MESH_ID = pl.DeviceIdType.MESH
NN = (((1,), (0,)), ((), ()))
NT = (((1,), (1,)), ((), ()))
TN = (((0,), (0,)), ((), ()))
_DN = {"nn": NN, "nt": NT, "tn": TN}


def _cparams(sem):
    return pltpu.CompilerParams(dimension_semantics=sem, vmem_limit_bytes=VMEM_LIMIT)


def _in_hbm(arrs):
    return [pltpu.with_memory_space_constraint(a, pltpu.HBM) for a in arrs]


MM_TILES = (1024, 768, 512)


def _tile(dim):
    return next((t for t in MM_TILES if dim % t == 0), dim)


BLOCK_PAIR = 2
MAX_WHOLE_K = 2048


def _mm(a, b, mode, out_dtype, b_blocks=False, out_blocks=False, after=()):
    if b_blocks:
        nblk, rows, cb = b.shape
        b2_shape = (rows, nblk * cb)
    else:
        b2_shape = b.shape
    if mode == "nn":
        (m, k), n = a.shape, b2_shape[1]
    elif mode == "nt":
        (m, k), n = a.shape, b2_shape[0]
    else:
        (k, m), n = a.shape, b2_shape[1]
    tm, tn, tk = _tile(m), _tile(n), _tile(k)
    if k <= MAX_WHOLE_K:
        tk = k
    if b_blocks and mode == "nn":
        tn = BLOCK_PAIR * cb
        if tn <= MM_TILES[0]:
            tm = m
    if b_blocks and mode == "nt":
        tk = BLOCK_PAIR * cb
    if out_blocks:
        cb = n // N_DEV
        tn = BLOCK_PAIR * cb
        tk = k
    nk = k // tk

    def body(a_ref, b_ref, *rest):
        o_ref, acc_ref = rest[-2:]
        kk = pl.program_id(2)
        bv = jnp.concatenate([b_ref[p] for p in range(BLOCK_PAIR)], axis=1) if b_blocks else b_ref[...]
        part = lax.dot_general(a_ref[...].astype(BF16), bv.astype(BF16), _DN[mode], preferred_element_type=F32)

        def write(res):
            if out_blocks:
                for p in range(BLOCK_PAIR):
                    o_ref[p] = res[:, p * cb:(p + 1) * cb].astype(o_ref.dtype)
            else:
                o_ref[...] = res.astype(o_ref.dtype)

        if nk == 1:
            write(part)
            return

        @pl.when(kk == 0)
        def _():
            acc_ref[...] = part

        @pl.when((kk > 0) & (kk < nk - 1))
        def _():
            acc_ref[...] += part

        @pl.when(kk == nk - 1)
        def _():
            write(acc_ref[...] + part)

    if mode == "nn":
        a_spec = pl.BlockSpec((tm, tk), lambda i, j, kk: (i, kk))
        b_spec = pl.BlockSpec((tk, tn), lambda i, j, kk: (kk, j))
    elif mode == "nt":
        a_spec = pl.BlockSpec((tm, tk), lambda i, j, kk: (i, kk))
        b_spec = pl.BlockSpec((tn, tk), lambda i, j, kk: (j, kk))
    else:
        a_spec = pl.BlockSpec((tk, tm), lambda i, j, kk: (kk, i))
        b_spec = pl.BlockSpec((tk, tn), lambda i, j, kk: (kk, j))
    if b_blocks and mode == "nn":
        b_spec = pl.BlockSpec((BLOCK_PAIR, tk, cb), lambda i, j, kk: (j, kk, 0))
    if b_blocks and mode == "nt":
        b_spec = pl.BlockSpec((BLOCK_PAIR, tn, cb), lambda i, j, kk: (kk, j, 0))
    out_spec = pl.BlockSpec((tm, tn), lambda i, j, kk: (i, j))
    out_shape = SDS((m, n), out_dtype)
    if out_blocks:
        out_spec = pl.BlockSpec((BLOCK_PAIR, tm, cb), lambda i, j, kk: (j, i, 0))
        out_shape = SDS((N_DEV, m, cb), out_dtype)
    return pl.pallas_call(
        body, name=f"mm_{mode}_{m}x{k}x{n}",
        grid=(m // tm, n // tn, nk),
        in_specs=[a_spec, b_spec] + [pl.BlockSpec(memory_space=pl.ANY)] * len(after),
        out_specs=out_spec,
        out_shape=out_shape,
        scratch_shapes=[pltpu.VMEM((tm, tn) if nk > 1 else (8, LANES), F32)],
        compiler_params=_cparams(("parallel", "parallel", "arbitrary")),
    )(a, b, *after)


def _gmm(a, b, mode, out_dtype, tm=S):
    ng, gw = POOL_W // POOL_GROUP, POOL_GROUP
    ns = S // tm
    if mode in ("nn", "nt"):
        def body(a_ref, b_ref, o_ref):
            o_ref[...] = lax.dot_general(a_ref[...].astype(BF16), b_ref[...].astype(BF16), _DN[mode],
                                         preferred_element_type=F32).astype(o_ref.dtype)

        return pl.pallas_call(
            body, name=f"gmm_{mode}", grid=(ng, ns),
            in_specs=[pl.BlockSpec((tm, gw), lambda g, i: (i, g)),
                      pl.BlockSpec((None, gw, gw), lambda g, i: (g, 0, 0))],
            out_specs=pl.BlockSpec((tm, gw), lambda g, i: (i, g)),
            out_shape=SDS((S, POOL_W), out_dtype),
            compiler_params=_cparams(("parallel", "parallel")),
        )(a, b)

    def body_tn(a_ref, b_ref, o_ref, acc_ref):
        i = pl.program_id(1)

        @pl.when(i == 0)
        def _():
            acc_ref[...] = jnp.zeros_like(acc_ref)

        acc_ref[...] += lax.dot_general(a_ref[...].astype(BF16), b_ref[...].astype(BF16), TN,
                                        preferred_element_type=F32)

        @pl.when(i == ns - 1)
        def _():
            o_ref[...] = acc_ref[...].astype(o_ref.dtype)

    return pl.pallas_call(
        body_tn, name="gmm_tn", grid=(ng, ns),
        in_specs=[pl.BlockSpec((tm, gw), lambda g, i: (i, g)),
                  pl.BlockSpec((tm, gw), lambda g, i: (i, g))],
        out_specs=pl.BlockSpec((None, gw, gw), lambda g, i: (g, 0, 0)),
        out_shape=SDS((ng, gw, gw), out_dtype),
        scratch_shapes=[pltpu.VMEM((gw, gw), F32)],
        compiler_params=_cparams(("parallel", "arbitrary")),
    )(a, b)


def _rowwise(fn, inputs, out_defs, acc_defs=(), tm=512, name=None, after=()):
    n_in, n_out, n_acc = len(inputs), len(out_defs), len(acc_defs)
    n_after = len(after)
    in_specs, args = [], []
    for arr, width, cb in inputs:
        if arr.shape[0] != S:
            in_specs.append(pl.BlockSpec((arr.shape[0], width), lambda i, cb=cb: (0, cb)))
        else:
            in_specs.append(pl.BlockSpec((tm, width), lambda i, cb=cb: (i, cb)))
        args.append(arr)
    out_defs = [d if len(d) == 4 else (d[0], d[1], d[0], 0) for d in out_defs]
    out_shape = [SDS((S, ww), dt) for _, dt, ww, _ in out_defs] + [SDS((1, w), F32) for w in acc_defs]
    out_specs = ([pl.BlockSpec((tm, w), lambda i, cb=cb: (i, cb)) for w, _, _, cb in out_defs]
                 + [pl.BlockSpec((1, w), lambda i: (0, 0)) for w in acc_defs])

    def kern(*refs):
        vals = [r[...] for r in refs[:n_in]]
        outs, accs = fn(*vals)
        out_refs = refs[n_in + n_after:]
        for r, v in zip(out_refs[:n_out], outs):
            r[...] = v.astype(r.dtype)
        if n_acc:
            acc_refs = out_refs[n_out:]

            @pl.when(pl.program_id(0) == 0)
            def _():
                for r in acc_refs:
                    r[...] = jnp.zeros_like(r)

            for r, v in zip(acc_refs, accs):
                r[...] += jnp.sum(v, axis=0, keepdims=True)

    res = pl.pallas_call(
        kern, name=name, grid=(S // tm,), in_specs=in_specs + [pl.BlockSpec(memory_space=pl.ANY)] * n_after,
        out_specs=out_specs, out_shape=out_shape, compiler_params=_cparams(("arbitrary",)),
    )(*args, *after)
    return res


def _sigmoid(x):
    return 1.0 / (1.0 + jnp.exp(-x))


def _silu_and_grad(x):
    s = _sigmoid(x)
    return x * s, s * (1.0 + x * (1.0 - s))


_GELU_K = math.sqrt(2.0 / math.pi)
_GELU_C = 0.044715


def _gelu_and_grad(x):
    t = jnp.tanh(_GELU_K * (x + _GELU_C * (x * x * x)))
    cdf = 0.5 * (1.0 + t)
    grad = cdf + 0.5 * x * (1.0 - t * t) * (_GELU_K * (1.0 + 3.0 * _GELU_C * x * x))
    return x * cdf, grad


def _rms(xv, gain):
    r = lax.rsqrt(jnp.mean(xv * xv, axis=-1, keepdims=True) + RMS_EPS)
    return xv * r * gain


def _rms_bwd(dout, xv, gain):
    r = lax.rsqrt(jnp.mean(xv * xv, axis=-1, keepdims=True) + RMS_EPS)
    xhat = xv * r
    dxhat = dout * gain
    dx = r * (dxhat - xhat * jnp.mean(dxhat * xhat, axis=-1, keepdims=True))
    return dx, dout * xhat


def _norm_fwd(x, gain):
    (h,) = _rowwise(lambda xv, g: ((_rms(xv, g),), ()), [(x, D, 0), (gain, D, 0)], [(D, BF16)], name="norm_fwd")
    return h


def _post_fwd(x, y, gain, next_gain):
    def fn(xv, yv, g, gn):
        out = xv + _rms(yv, g)
        return (out, _rms(out, gn)), ()

    return _rowwise(fn, [(x, D, 0), (y, D, 0), (gain, D, 0), (next_gain, D, 0)], [(D, F32), (D, BF16)],
                    name="post_fwd")


def _post_fwd_loss(x, y, gain, tgt):
    def fn(xv, yv, g, tv):
        e = xv + _rms(yv, g) - tv
        return (e * (1.0 / D),), (e * e,)

    return _rowwise(fn, [(x, D, 0), (y, D, 0), (gain, D, 0), (tgt, D, 0)], [(D, F32)], [D], name="post_fwd_loss")


def _post_bwd(g, y, gain):
    def fn(gv, yv, gn):
        dx, dg = _rms_bwd(gv, yv, gn)
        return (dx,), (dg,)

    return _rowwise(fn, [(g, D, 0), (y, D, 0), (gain, D, 0)], [(D, BF16)], [D], name="post_bwd")


def _pre_bwd(g, dh, x, gain):
    def fn(gv, dhv, xv, gn):
        dx, dg = _rms_bwd(dhv, xv, gn)
        return (gv + dx,), (dg,)

    return _rowwise(fn, [(g, D, 0), (dh, D, 0), (x, D, 0), (gain, D, 0)], [(D, F32)], [D], name="pre_bwd")


def _pool(u_arr, col_block, transpose, out_dtype, into=None, tc=256):
    n_t = POOL_W // tc
    per_group = POOL_GROUP // tc

    def body(u_ref, *rest):
        o_ref = rest[-1]
        grp = pl.program_id(0) // per_group
        t = lax.broadcasted_iota(jnp.int32, (S, 1), 0)
        for g in range(POOL_W // POOL_GROUP):
            @pl.when(grp == g)
            def _(g=g):
                xv = u_ref[...]
                cnt = jnp.minimum(t + 1, 2 << g).astype(F32)
                cur = xv / cnt if transpose else xv
                for k in (1, 2, 4, 8)[:g + 1]:
                    if transpose:
                        cur = cur + jnp.where(t < S - k, pltpu.roll(cur, S - k, 0), 0.0)
                    else:
                        cur = cur + jnp.where(t >= k, pltpu.roll(cur, k, 0), 0.0)
                res = cur - xv if transpose else cur / cnt - xv
                o_ref[...] = res.astype(o_ref.dtype)

    in_specs = [pl.BlockSpec((S, tc), lambda c: (0, col_block * n_t + c))]
    args = [u_arr]
    if into is not None:
        in_specs.append(pl.BlockSpec(memory_space=pl.ANY))
        args.append(into)
    return pl.pallas_call(
        body, name="pool_bwd" if transpose else "pool_fwd", grid=(n_t,),
        in_specs=in_specs,
        out_specs=pl.BlockSpec((S, tc), lambda c: (0, c)),
        out_shape=SDS((S, POOL_W) if into is None else into.shape, out_dtype),
        input_output_aliases={} if into is None else {1: 0},
        compiler_params=_cparams(("parallel",)),
    )(*args)


def _rope_tables(zero):
    pos = jnp.arange(S, dtype=jnp.int32).astype(F32) + zero
    inv_freq = ROPE_THETA ** (-jnp.arange(0, ROT_DIM, 2, dtype=F32) / ROT_DIM)
    ang = pos[:, None] * inv_freq[None, :]
    cos8, sin8 = jnp.cos(ang), jnp.sin(ang)
    half = ROT_DIM // 2
    zeros = jnp.zeros((S, HEAD_DIM - ROT_DIM), F32)
    cos = jnp.concatenate([cos8, cos8, jnp.ones((S, HEAD_DIM - ROT_DIM), F32)], axis=1)
    sin = jnp.concatenate([-sin8, sin8, zeros], axis=1)
    rep = LANES // HEAD_DIM
    lane = jnp.arange(LANES)
    dim = lane % HEAD_DIM
    partner = jnp.where(dim < half, lane + half, jnp.where(dim < ROT_DIM, lane - half, -1))
    swap = (lane[:, None] == partner[None, :]).astype(BF16)
    return jnp.tile(cos, (1, rep)), jnp.tile(sin, (1, rep)), swap


def _rotate(xv, cos, sin, swap, transpose):
    rep = xv.shape[1] // LANES
    wide = lambda tab: jnp.concatenate([tab] * rep, axis=1)
    xb = xv.astype(BF16)
    partner = jnp.concatenate([lax.dot_general(xb[:, t * LANES:(t + 1) * LANES], swap, NN, preferred_element_type=F32)
                               for t in range(rep)], axis=1)
    mixed = partner * wide(sin)
    return xv * wide(cos) - mixed if transpose else xv * wide(cos) + mixed


def _qkv_prep(proj, tables):
    cos, sin, swap = tables

    def fn(x, c, s, sw):
        rot = _rotate(x[:, :2 * ATT_W], c, s, sw, False)
        return (jnp.concatenate([(rot[:, :ATT_W] * HEAD_DIM ** -0.5).astype(BF16), rot[:, ATT_W:].astype(BF16),
                                 x[:, 2 * ATT_W:].astype(BF16)], axis=1),), ()

    (qkv,) = _rowwise(fn, [(proj, 3 * ATT_W, 0), (cos, LANES, 0), (sin, LANES, 0), (swap, LANES, 0)],
                      [(3 * ATT_W, BF16)], name="qkv_prep")
    return qkv


ATT_T = 512


def _multiplicity(delta):
    ok = delta >= 0
    near = jnp.where(ok & (delta <= 128), 1.0, 0.0)
    mid = jnp.where(ok & (delta <= 512) & ((delta & 3) == 0), 1.0, 0.0)
    far = jnp.where(ok & ((delta & 15) == 0), 1.0, 0.0)
    return near + mid + far


def _attention_bias(zero):
    t = ATT_T
    pos = jnp.arange(t, dtype=jnp.int32) + jnp.asarray(zero).astype(jnp.int32)
    delta = jnp.arange(S // t, dtype=jnp.int32)[:, None, None] * t + pos[None, :, None] - pos[None, None, :]
    mult = _multiplicity(delta)
    return jnp.where(mult > 0.0, jnp.log(jnp.maximum(mult, 1.0)), -1e30).astype(F32)


def _head_split(v, first):
    zero = jnp.zeros_like(v)
    return [jnp.where(first, v, zero), jnp.where(first, zero, v)]


def _flash_fwd(qkv, bias):
    t = ATT_T
    n_hp = ATT_W // LANES

    def body(q_ref, k_ref, v_ref, b_ref, o_ref, lse_ref):
        i = pl.program_id(1)
        first = lax.broadcasted_iota(jnp.int32, (1, LANES), 1) < HEAD_DIM
        qs = _head_split(q_ref[...], first)

        def kv_step(j, carry):
            m0, l0, m1, l1, acc = carry
            off = pl.multiple_of(j * t, t)
            kb = k_ref[pl.ds(off, t), :]
            vs = _head_split(v_ref[pl.ds(off, t), :], first)
            bias_t = b_ref[i - j]
            new = []
            pv = None
            for h, (m_prev, l_prev) in enumerate(((m0, l0), (m1, l1))):
                s = lax.dot_general(qs[h], kb, NT, preferred_element_type=F32) + bias_t
                m_new = jnp.maximum(m_prev, jnp.max(s, axis=1, keepdims=True))
                p = jnp.exp(s - m_new)
                alpha = jnp.exp(m_prev - m_new)
                l_new = alpha * l_prev + jnp.sum(p, axis=1, keepdims=True)
                d = lax.dot_general(p.astype(BF16), vs[h], NN, preferred_element_type=F32)
                pv = d if pv is None else pv + d
                new.append((m_new, l_new, alpha))
            acc = acc * jnp.where(first, new[0][2], new[1][2]) + pv
            return new[0][0], new[0][1], new[1][0], new[1][1], acc

        neg = jnp.full((t, 1), -1e30, F32)
        zero = jnp.zeros((t, 1), F32)
        m0, l0, m1, l1, acc = lax.fori_loop(0, i + 1, kv_step, (neg, zero, neg, zero, jnp.zeros((t, LANES), F32)))
        o_ref[...] = acc * jnp.where(first, 1.0 / l0, 1.0 / l1)
        lse_ref[...] = jnp.where(first, m0 + jnp.log(l0), m1 + jnp.log(l1))

    blk = pl.BlockSpec((t, LANES), lambda hp, i: (i, hp))
    k_full = pl.BlockSpec((S, LANES), lambda hp, i: (0, n_hp + hp))
    v_full = pl.BlockSpec((S, LANES), lambda hp, i: (0, 2 * n_hp + hp))
    return pl.pallas_call(
        body, name="flash_fwd", grid=(n_hp, S // t),
        in_specs=[blk, k_full, v_full, pl.BlockSpec((S // t, t, t), lambda hp, i: (0, 0, 0))], out_specs=[blk, blk],
        out_shape=[SDS((S, ATT_W), F32), SDS((S, ATT_W), F32)],
        compiler_params=_cparams(("parallel", "arbitrary")),
    )(qkv, qkv, qkv, bias)


def _flash_bwd(qkv, o, do, lse, bias, after=()):
    t = ATT_T
    n_hp = ATT_W // LANES
    n_t = S // t

    def body(q_ref, k_ref, v_ref, o_ref, do_ref, lse_ref, b_ref, *rest):
        dq_ref, dk_ref, dv_ref = rest[-3:]
        j = pl.program_id(1)
        first = lax.broadcasted_iota(jnp.int32, (1, LANES), 1) < HEAD_DIM

        @pl.when(j == 0)
        def _():
            dq_ref[...] = jnp.zeros_like(dq_ref)

        kb = k_ref[...]
        vb = v_ref[...]
        ks = _head_split(kb, first)

        def q_step(i, carry):
            dk_acc, dv_acc = carry
            rows = pl.ds(pl.multiple_of(i * t, t), t)
            qs = _head_split(q_ref[rows, :], first)
            dob = do_ref[rows, :]
            prod = dob * o_ref[rows, :]
            d_all = jnp.sum(prod, axis=1, keepdims=True)
            d0 = jnp.sum(jnp.where(first, prod, 0.0), axis=1, keepdims=True)
            lse_b = lse_ref[rows, :]
            lse0 = jnp.max(jnp.where(first, lse_b, -jnp.inf), axis=1, keepdims=True)
            lse1 = jnp.max(jnp.where(first, -jnp.inf, lse_b), axis=1, keepdims=True)
            dos = _head_split(dob.astype(BF16), first)
            bias_t = b_ref[i - j]
            dq_t = jnp.zeros((t, LANES), F32)
            for h, (lse_h, d_h) in enumerate(((lse0, d0), (lse1, d_all - d0))):
                s = lax.dot_general(qs[h], kb, NT, preferred_element_type=F32)
                p = jnp.exp(s + (bias_t - lse_h))
                dp = lax.dot_general(dos[h], vb, NT, preferred_element_type=F32)
                ds = (p * (dp - d_h)).astype(BF16)
                dv_acc = dv_acc + lax.dot_general(p.astype(BF16), dos[h], TN, preferred_element_type=F32)
                dk_acc = dk_acc + lax.dot_general(ds, qs[h], TN, preferred_element_type=F32)
                dq_t = dq_t + lax.dot_general(ds, ks[h], NN, preferred_element_type=F32)
            dq_ref[rows, :] += dq_t
            return dk_acc, dv_acc

        zero = jnp.zeros((t, LANES), F32)
        dk_acc, dv_acc = lax.fori_loop(j, n_t, q_step, (zero, zero))
        dk_ref[...] = dk_acc
        dv_ref[...] = dv_acc

    blk = pl.BlockSpec((t, LANES), lambda hp, j: (j, hp))
    full = pl.BlockSpec((S, LANES), lambda hp, j: (0, hp))
    k_blk = pl.BlockSpec((t, LANES), lambda hp, j: (j, n_hp + hp))
    v_blk = pl.BlockSpec((t, LANES), lambda hp, j: (j, 2 * n_hp + hp))
    return pl.pallas_call(
        body, name="flash_bwd", grid=(n_hp, n_t),
        in_specs=([full, k_blk, v_blk, full, full, full, pl.BlockSpec((n_t, t, t), lambda hp, j: (0, 0, 0))]
                  + [pl.BlockSpec(memory_space=pl.ANY)] * len(after)),
        out_specs=[full, blk, blk],
        out_shape=[SDS((S, ATT_W), F32)] * 3,
        compiler_params=_cparams(("parallel", "arbitrary")),
    )(qkv, qkv, qkv, o, do, lse, bias, *after)


SCAN_T = 256
SCAN_GROUP = 8
SCAN_STEPS = (1, 2, 4)
ST_ROWS = 2 * N_CPLX // LANES
HALF = ST_ROWS // 2


def _scan_tables(lam_t):
    lam = lax.complex(lam_t[:HALF].reshape(N_CPLX), lam_t[HALF:].reshape(N_CPLX))
    pows = jnp.cumprod(jnp.broadcast_to(lam, (SCAN_GROUP, N_CPLX)), axis=0)
    shifts = jnp.asarray(SCAN_STEPS)
    sub = jnp.arange(SCAN_GROUP)[None, :, None]
    steps = pows[shifts - 1][:, None, :]
    fwd = jnp.concatenate([jnp.where(sub >= shifts[:, None, None], steps, 0.0), pows[None]], axis=0)
    bwd = jnp.concatenate([jnp.where(sub <= SCAN_GROUP - 1 - shifts[:, None, None], jnp.conj(steps), 0.0),
                           jnp.conj(pows)[None, ::-1]], axis=0)

    def pack(tabs):
        return jnp.concatenate([jnp.real(tabs), jnp.imag(tabs)], axis=-1).astype(F32)

    return pack(fwd), pack(bwd)


def _cmul_add(xr, xi, lr, li, sr, si):
    return xr + lr * sr - li * si, xi + lr * si + li * sr


def _group_scan(xr, xi, tab_ref, cr, ci, reverse):
    for j, k in enumerate(SCAN_STEPS):
        shift = SCAN_GROUP - k if reverse else k
        xr, xi = _cmul_add(xr, xi, tab_ref[j, :, :N_CPLX], tab_ref[j, :, N_CPLX:],
                           pltpu.roll(xr, shift, 0), pltpu.roll(xi, shift, 0))
    return _cmul_add(xr, xi, tab_ref[3, :, :N_CPLX], tab_ref[3, :, N_CPLX:],
                     jnp.broadcast_to(cr, (SCAN_GROUP, N_CPLX)), jnp.broadcast_to(ci, (SCAN_GROUP, N_CPLX)))


SSM_SUPER = 4
SB_ROWS = SSM_W // SSM_SUPER
SB_COLS = N_CPLX // SSM_SUPER


def _super_blocks():
    return [(slice(b * SB_ROWS, (b + 1) * SB_ROWS), slice(h * SB_COLS, (h + 1) * SB_COLS),
             slice(h * N_CPLX + b * SB_COLS, h * N_CPLX + (b + 1) * SB_COLS))
            for b in range(SSM_SUPER) for h in range(2)]


def _dot16(a, b, dims):
    return lax.dot_general(a.astype(BF16), b.astype(BF16), dims, preferred_element_type=F32)


def _s5_fwd(tab, u_arr, u_cols, w_b, w_ct):
    nc = N_CPLX

    def body(tab_ref, u_ref, wb_ref, wct_ref, st_ref, y_ref, carry, bu_scr):
        @pl.when(pl.program_id(0) == 0)
        def _():
            carry[...] = jnp.zeros_like(carry)

        for rows_b, cols_c, cols_s in _super_blocks():
            bu_scr[:, cols_s] = _dot16(u_ref[:, rows_b], wb_ref[rows_b, cols_c], NN)

        def group(a, c):
            rows = pl.ds(pl.multiple_of(a * SCAN_GROUP, SCAN_GROUP), SCAN_GROUP)
            xr, xi = _group_scan(bu_scr[rows, :nc], bu_scr[rows, nc:], tab_ref, c[0], c[1], False)
            st_ref[rows, :nc] = xr
            st_ref[rows, nc:] = xi
            return xr[SCAN_GROUP - 1:SCAN_GROUP, :], xi[SCAN_GROUP - 1:SCAN_GROUP, :]

        cr, ci = lax.fori_loop(0, SCAN_T // SCAN_GROUP, group, (carry[:, :nc], carry[:, nc:]), unroll=2)
        carry[:, :nc] = cr
        carry[:, nc:] = ci

        for b in range(SSM_SUPER):
            (rows_b, cols_re, st_re), (_, cols_im, st_im) = _super_blocks()[2 * b:2 * b + 2]
            y_ref[:, rows_b] = (_dot16(st_ref[:, st_re], wct_ref[rows_b, cols_re], NT)
                                + _dot16(st_ref[:, st_im], wct_ref[rows_b, cols_im], NT))

    const = lambda shape: pl.BlockSpec(shape, lambda i: (0,) * len(shape))
    return pl.pallas_call(
        body, name="s5_fwd", grid=(S // SCAN_T,),
        in_specs=[const((4, SCAN_GROUP, 2 * nc)), pl.BlockSpec((SCAN_T, SSM_W), lambda i: (i, u_cols[0] // SSM_W)),
                  const((SSM_W, 2 * SB_COLS)), const((SSM_W, 2 * SB_COLS))],
        out_specs=[pl.BlockSpec((SCAN_T, 2 * nc), lambda i: (i, 0)), pl.BlockSpec((SCAN_T, SSM_W), lambda i: (i, 0))],
        out_shape=[SDS((S, 2 * nc), F32), SDS((S, SSM_W), F32)],
        scratch_shapes=[pltpu.VMEM((1, 2 * nc), F32), pltpu.VMEM((SCAN_T, 2 * nc), F32)],
        compiler_params=_cparams(("arbitrary",)),
    )(tab, u_arr, w_b, w_ct)


def _s5_bwd(tab, dy, states, u_arr, u_cols, w_b, w_ct):
    n_blk = S // SCAN_T
    nc = N_CPLX

    def body(tab_ref, dy_ref, x_ref, u_ref, wb_ref, wct_ref, du_ref, dlam_ref, dwb_ref, dwct_ref,
             carry, acc, d_scr, g_ref):
        i = pl.program_id(0)

        @pl.when(i == 0)
        def _():
            carry[...] = jnp.zeros_like(carry)
            acc[...] = jnp.zeros_like(acc)
            dwb_ref[...] = jnp.zeros_like(dwb_ref)
            dwct_ref[...] = jnp.zeros_like(dwct_ref)

        for rows_b, cols_c, cols_s in _super_blocks():
            d_scr[:, cols_s] = _dot16(dy_ref[:, rows_b], wct_ref[rows_b, cols_c], NN)

        last_row = lax.broadcasted_iota(jnp.int32, (SCAN_GROUP, 1), 0) == SCAN_GROUP - 1
        d_ref = d_scr

        def group(j, c):
            cr, ci = c
            rows = pl.ds(pl.multiple_of((SCAN_T // SCAN_GROUP - 1 - j) * SCAN_GROUP, SCAN_GROUP), SCAN_GROUP)
            gr, gi = _group_scan(d_ref[rows, :nc], d_ref[rows, nc:], tab_ref, cr, ci, True)
            g_ref[rows, :nc] = gr
            g_ref[rows, nc:] = gi
            nr = jnp.where(last_row, jnp.broadcast_to(cr, (SCAN_GROUP, nc)), pltpu.roll(gr, SCAN_GROUP - 1, 0))
            ni = jnp.where(last_row, jnp.broadcast_to(ci, (SCAN_GROUP, nc)), pltpu.roll(gi, SCAN_GROUP - 1, 0))
            sr, si = x_ref[rows, :nc], x_ref[rows, nc:]
            acc[:, :nc] += nr * sr + ni * si
            acc[:, nc:] += ni * sr - nr * si
            return gr[0:1, :], gi[0:1, :]

        cr, ci = lax.fori_loop(0, SCAN_T // SCAN_GROUP, group, (carry[:, :nc], carry[:, nc:]), unroll=2)
        carry[:, :nc] = cr
        carry[:, nc:] = ci

        for b in range(SSM_SUPER):
            (rows_b, cols_re, st_re), (_, cols_im, st_im) = _super_blocks()[2 * b:2 * b + 2]
            du_ref[:, rows_b] = (_dot16(g_ref[:, st_re], wb_ref[rows_b, cols_re], NT)
                                 + _dot16(g_ref[:, st_im], wb_ref[rows_b, cols_im], NT))
            for cols_c, cols_s in ((cols_re, st_re), (cols_im, st_im)):
                dwb_ref[rows_b, cols_c] += _dot16(u_ref[:, rows_b], g_ref[:, cols_s], TN)
                dwct_ref[rows_b, cols_c] += _dot16(dy_ref[:, rows_b], x_ref[:, cols_s], TN)

        @pl.when(i == n_blk - 1)
        def _():
            dlam_ref[...] = jnp.sum(acc[...], axis=0, keepdims=True)

    const = lambda shape: pl.BlockSpec(shape, lambda i: (0,) * len(shape))
    rows = lambda width, col_block=0: pl.BlockSpec((SCAN_T, width), lambda i: (n_blk - 1 - i, col_block))
    maps = const((SSM_W, 2 * SB_COLS))
    return pl.pallas_call(
        body, name="s5_bwd", grid=(n_blk,),
        in_specs=[const((4, SCAN_GROUP, 2 * nc)), rows(SSM_W), rows(2 * nc), rows(SSM_W, u_cols[0] // SSM_W), maps, maps],
        out_specs=[rows(SSM_W), const((1, 2 * nc)), maps, maps],
        out_shape=[SDS((S, SSM_W), F32), SDS((1, 2 * nc), F32), SDS((SSM_W, 2 * SB_COLS), F32),
                   SDS((SSM_W, 2 * SB_COLS), F32)],
        scratch_shapes=[pltpu.VMEM((1, 2 * nc), F32), pltpu.VMEM((SCAN_GROUP, 2 * nc), F32),
                        pltpu.VMEM((SCAN_T, 2 * nc), F32), pltpu.VMEM((SCAN_T, 2 * nc), F32)],
        compiler_params=_cparams(("arbitrary",)),
    )(tab, dy, states, u_arr, w_b, w_ct)


def _ssm_prep(a_re, a_im, log_dt, b_re, b_im, c_re, c_im):
    lam = lax.complex(a_re, a_im)
    dt = jnp.exp(log_dt)[:, None]
    lam_bar = jnp.exp(lam * dt)
    b_bar = ((lam_bar - 1.0) / lam)[..., None] * lax.complex(b_re, b_im)
    lam_t = jnp.concatenate([jnp.real(lam_bar).reshape(HALF, LANES), jnp.imag(lam_bar).reshape(HALF, LANES)], axis=0)
    groups_per_super = SSM_GROUPS // SSM_SUPER
    on_diag = ((lax.broadcasted_iota(jnp.int32, (SSM_W, SB_COLS), 0) // SSM_GROUP) % groups_per_super
               == lax.broadcasted_iota(jnp.int32, (SSM_W, SB_COLS), 1) // SSM_STATE)

    repeat = (lax.broadcasted_iota(jnp.int32, (SSM_STATE, SB_COLS), 0)
              == lax.broadcasted_iota(jnp.int32, (SSM_STATE, SB_COLS), 1) % SSM_STATE).astype(F32)

    def compact(m):
        tiled = jnp.dot(m.reshape(SSM_W, SSM_STATE), repeat, precision=lax.Precision.HIGHEST)
        return jnp.where(on_diag, tiled, 0.0)

    w_b = jnp.concatenate([compact(jnp.real(b_bar).transpose(0, 2, 1)),
                           compact(jnp.imag(b_bar).transpose(0, 2, 1))], axis=1)
    w_ct = jnp.concatenate([compact(c_re), -compact(c_im)], axis=1)
    return lam_t, w_b, w_ct


U_SSM_COLS = (4 * ATT_W, SSM_W)


def _row(v):
    return v.reshape(1, -1)


def _even_fwd(x, h, tail, pre, post, w_in, late_w, glu_b, ssm_d, prep, tables):
    lam_t, w_b, w_ct = prep
    proj = _mm(h, w_in, "nn", F32, b_blocks=True)
    qkv = _qkv_prep(proj, tables[:3])
    att, lse = _flash_fwd(qkv, tables[3])
    w_out, glu_w = late_w(att)
    scan_fwd_tab, scan_bwd_tab = _scan_tables(lam_t)
    states, y = _s5_fwd(scan_fwd_tab, proj, U_SSM_COLS, w_b, w_ct)

    def act1(yv, uv, dv):
        return (_gelu_and_grad(yv + dv * uv)[0],), ()

    (z1,) = _rowwise(act1, [(y, SSM_W, 0), (proj, SSM_W, 8), (ssm_d, SSM_W, 0)], [(SSM_W, F32)], name="ssm_act_fwd")
    lin = _mm(z1, glu_w, "nn", F32)

    def gate(att_v, ga, gs, z1v, linv, bv):
        ssm_out = z1v * _sigmoid(linv + bv)
        return (jnp.concatenate([att_v * _silu_and_grad(ga)[0], ssm_out * _silu_and_grad(gs)[0]], axis=1),), ()

    (merged,) = _rowwise(gate, [(att, ATT_W, 0), (proj, ATT_W, 3), (proj, SSM_W, 9), (z1, SSM_W, 0),
                                (lin, SSM_W, 0), (glu_b, SSM_W, 0)], [(EVEN_OUT, BF16)], name="even_gate_fwd")
    yout = _mm(merged, w_out, "nn", F32)
    saved = (x, h, proj, qkv, att, lse, states, y, z1, lin, merged, yout, w_out, glu_w, scan_bwd_tab)
    return tail(x, yout, post) + (saved,)


def _even_bwd(g, saved, pre, post, w_in, late_w, glu_b, ssm_d, prep, tables, on_w, on_ssm):
    x, h, proj, qkv, att, lse, states, y, z1, lin, merged, yout, w_out, glu_w, scan_bwd_tab = saved
    lam_t, w_b, w_ct = prep
    dyout, dpost = _post_bwd(g, yout, post)
    dmerged = _mm(dyout, w_out, "nt", F32)
    dw_out = _mm(merged, dyout, "tn", BF16)

    def gate_bwd(dm_a, dm_s, att_v, ga, gs, z1v, linv, bv):
        sa, dsa = _silu_and_grad(ga)
        ss, dss = _silu_and_grad(gs)
        sig = _sigmoid(linv + bv)
        ssm_out = z1v * sig
        dssm = dm_s * ss
        dlin = dssm * z1v * sig * (1.0 - sig)
        return (dm_a * sa, dm_a * att_v * dsa, dm_s * ssm_out * dss, dssm * sig, dlin), (dlin,)

    datt, dg_att, dg_ssm, dz1a, dlin, dglu_b = _rowwise(
        gate_bwd, [(dmerged, ATT_W, 0), (dmerged, SSM_W, 2), (att, ATT_W, 0), (proj, ATT_W, 3), (proj, SSM_W, 9),
                   (z1, SSM_W, 0), (lin, SSM_W, 0), (glu_b, SSM_W, 0)],
        [(ATT_W, F32), (ATT_W, BF16), (SSM_W, BF16), (SSM_W, F32), (SSM_W, BF16)], [SSM_W], name="even_gate_bwd")
    dz1b = _mm(dlin, glu_w, "nt", F32)
    dglu_w = _mm(z1, dlin, "tn", BF16)

    def act1_bwd(da, db, yv, uv, dv):
        dpre = (da + db) * _gelu_and_grad(yv + dv * uv)[1]
        return (dpre, dpre * dv), (dpre * uv,)

    sent_late_w = on_w(dict(w_out=dw_out, glu_w=dglu_w))
    dy, du_direct, dd = _rowwise(act1_bwd, [(dz1a, SSM_W, 0), (dz1b, SSM_W, 0), (y, SSM_W, 0), (proj, SSM_W, 8),
                                            (ssm_d, SSM_W, 0)], [(SSM_W, BF16), (SSM_W, F32)], [SSM_W],
                                 name="ssm_act_bwd", after=(sent_late_w,))
    du_state, dlam_row, dw_b, dw_ct = _s5_bwd(scan_bwd_tab, dy, states, proj, U_SSM_COLS, w_b, w_ct)
    dlam = jnp.concatenate([dlam_row[0, :N_CPLX].reshape(HALF, LANES), dlam_row[0, N_CPLX:].reshape(HALF, LANES)],
                           axis=0)
    sent_ssm = on_ssm((dlam, dw_b, dw_ct))
    dq, dk, dv = _flash_bwd(qkv, att, datt, lse, tables[3], after=() if sent_ssm is None else (sent_ssm,))

    def assemble(dqv, dkv, dvv, dga, dua, dub, dgs, c, s, sw):
        rot = _rotate(jnp.concatenate([dqv, dkv], axis=1), c, s, sw, True)
        return (jnp.concatenate([(rot[:, :ATT_W] * HEAD_DIM ** -0.5).astype(BF16), rot[:, ATT_W:].astype(BF16),
                                 dvv.astype(BF16), dga, (dua + dub).astype(BF16), dgs], axis=1),), ()

    (dproj,) = _rowwise(assemble, [(dq, ATT_W, 0), (dk, ATT_W, 0), (dv, ATT_W, 0), (dg_att, ATT_W, 0),
                                   (du_state, SSM_W, 0), (du_direct, SSM_W, 0), (dg_ssm, SSM_W, 0),
                                   (tables[0], LANES, 0), (tables[1], LANES, 0), (tables[2], LANES, 0)],
                        [(EVEN_IN, BF16)], name="dproj_assemble")
    dw_in = _mm(h, dproj, "tn", BF16, out_blocks=True)
    sent = on_w(dict(w_in=dw_in))
    dh = _mm(dproj, w_in, "nt", F32, b_blocks=True, after=(sent,))
    g_prev, dpre = _pre_bwd(g, dh, x, pre)
    return g_prev, dict(pre=dpre, post=dpost, glu_b=dglu_b, ssm_d=dd)


def _odd_fwd(x, h, tail, pre, post, w_in, pool_w, pool_scale, w_out):
    proj = _mm(h, w_in, "nn", F32, b_blocks=True)
    mixed = _pool(proj, 0, False, BF16)
    ylin = _gmm(mixed, pool_w, "nn", F32)

    def gate(yl, gt, sc):
        return (yl * sc * _silu_and_grad(gt)[0],), ()

    (z,) = _rowwise(gate, [(ylin, POOL_W, 0), (proj, POOL_W, 1), (pool_scale, POOL_W, 0)], [(POOL_W, BF16)],
                    name="odd_gate_fwd")
    yout = _mm(z, w_out, "nn", F32)
    return tail(x, yout, post) + ((x, h, proj, mixed, ylin, z, yout),)


def _odd_bwd(g, saved, pre, post, w_in, pool_w, pool_scale, w_out, on_w):
    x, h, proj, mixed, ylin, z, yout = saved
    dyout, dpost = _post_bwd(g, yout, post)
    dz = _mm(dyout, w_out, "nt", F32)
    dw_out = _mm(z, dyout, "tn", BF16)

    def gate_bwd(dzv, yl, gt, sc):
        sg, dsg = _silu_and_grad(gt)
        tt = dzv * sg
        return (tt * sc, dzv * yl * sc * dsg), (tt * yl,)

    dylin, dproj_gate, dscale = _rowwise(gate_bwd, [(dz, POOL_W, 0), (ylin, POOL_W, 0), (proj, POOL_W, 1),
                                                    (pool_scale, POOL_W, 0)],
                                         [(POOL_W, BF16), (POOL_W, BF16, ODD_IN, 1)], [POOL_W], name="odd_gate_bwd")
    dmixed = _gmm(dylin, pool_w, "nt", F32)
    dpool_w = _gmm(mixed, dylin, "tn", BF16)
    dproj = _pool(dmixed, 0, True, BF16, into=dproj_gate)
    dw_in = _mm(h, dproj, "tn", BF16, out_blocks=True)
    sent = on_w(dict(w_in=dw_in, w_out=dw_out, pool_w=dpool_w))
    dh = _mm(dproj, w_in, "nt", F32, b_blocks=True, after=(sent,))
    g_prev, dpre = _pre_bwd(g, dh, x, pre)
    return g_prev, dict(pre=dpre, post=dpost, pool_scale=dscale)


def _my_index():
    return 4 * lax.axis_index("x") + 2 * lax.axis_index("y") + lax.axis_index("c")


HBM_SPEC = pl.BlockSpec(memory_space=pltpu.HBM)
SEM_SPEC = pl.BlockSpec(memory_space=pltpu.SEMAPHORE)
SPLIT_EFFECT = pltpu.SideEffectType.DATAFLOW_SIDE_EFFECTING


def _device_of(j):
    return (j // 4, (j // 2) % 2, j % 2)


def _split_copy(srcs, lands, send_sems, recv_sems, gather, i, j, dst_slot, recv_slot):
    return pltpu.make_async_remote_copy(
        src_ref=srcs[i] if gather else srcs[i].at[j], dst_ref=lands[i].at[dst_slot],
        send_sem=send_sems.at[i * N_DEV + j], recv_sem=recv_sems.at[i * N_DEV + recv_slot],
        device_id=_device_of(j), device_id_type=MESH_ID)


def _own_copy(srcs, lands, send_sems, gather, i, me):
    return pltpu.make_async_copy(srcs[i] if gather else srcs[i].at[me], lands[i].at[me], send_sems.at[i * N_DEV + me])


def _xchg_start(name, srcs, gather, after=()):
    n = len(srcs)
    n_in = n + len(after)

    def body(*refs):
        src_refs = refs[:n]
        send_sems, recv_sems, token = refs[n_in], refs[n_in + 1], refs[-1]
        land_refs = refs[n_in + 2 + n:n_in + 2 + 2 * n]
        me = _my_index()
        for j in range(N_DEV):
            @pl.when(me != j)
            def _(j=j):
                for i in range(n):
                    _split_copy(src_refs, land_refs, send_sems, recv_sems, gather, i, j, me, me).start()
        for i in range(n):
            _own_copy(src_refs, land_refs, send_sems, gather, i, me).start()
        token[...] = jnp.zeros_like(token)

    land_shapes = [((N_DEV,) + a.shape) if gather else a.shape for a in srcs]
    thru = ([pltpu.HBM(a.shape, a.dtype) for a in srcs] + [pltpu.HBM(s, a.dtype) for s, a in zip(land_shapes, srcs)])
    res = pl.pallas_call(
        body, name=name,
        out_shape=(pltpu.SemaphoreType.DMA((n * N_DEV,)), pltpu.SemaphoreType.DMA((n * N_DEV,)), *thru,
                   SDS((8, LANES), F32)),
        in_specs=[HBM_SPEC] * n + [pl.BlockSpec(memory_space=pl.ANY)] * len(after),
        out_specs=(SEM_SPEC, SEM_SPEC, *([HBM_SPEC] * (2 * n)), pl.BlockSpec(memory_space=pltpu.VMEM)),
        input_output_aliases={i: 2 + i for i in range(n)},
        compiler_params=pltpu.CompilerParams(has_side_effects=SPLIT_EFFECT),
    )(*[pltpu.with_memory_space_constraint(a, pltpu.HBM) for a in srcs], *after)
    return res[0], res[1], list(res[2:2 + n]), list(res[2 + n:2 + 2 * n]), res[-1]


def _xchg_wait(name, started, gather, after):
    send_sems, recv_sems, srcs, lands, _ = started
    n = len(srcs)

    def body(*refs):
        src_refs, land_refs = refs[:n], refs[n:2 * n]
        send_r, recv_r = refs[2 * n], refs[2 * n + 1]
        me = _my_index()
        for j in range(N_DEV):
            @pl.when(me != j)
            def _(j=j):
                for i in range(n):
                    _split_copy(src_refs, land_refs, send_r, recv_r, gather, i, j, me, me).wait_send()
                    _split_copy(src_refs, land_refs, send_r, recv_r, gather, i, j, j, j).wait_recv()
        for i in range(n):
            _own_copy(src_refs, land_refs, send_r, gather, i, me).wait()

    thru = [pltpu.HBM(a.shape, a.dtype) for a in list(srcs) + list(lands)]
    res = pl.pallas_call(
        body, name=name, out_shape=tuple(thru),
        in_specs=[HBM_SPEC] * (2 * n) + [SEM_SPEC, SEM_SPEC] + [pl.BlockSpec(memory_space=pl.ANY)] * len(after),
        out_specs=tuple([HBM_SPEC] * (2 * n)),
        input_output_aliases={i: i for i in range(2 * n)},
        compiler_params=pltpu.CompilerParams(has_side_effects=SPLIT_EFFECT),
    )(*srcs, *lands, send_sems, recv_sems, *after)
    return list(res[n:])


def _adam_layer(w, slots, m, v, layer, name, into=None):
    n_l, r, c = w.shape
    ns = slots.shape[0]
    tr = r
    while tr * c * 4 > (1 << 20) and tr % 16 == 0:
        tr //= 2
    assert r % tr == 0

    def body(w_ref, g_ref, m_ref, v_ref, *rest):
        go_ref, d_ref, mo_ref, vo_ref = rest[-4:]
        g = g_ref[0].astype(F32)
        for s in range(1, ns):
            g = g + g_ref[s].astype(F32)
        mn = ADAM_B1 * m_ref[...] + (1.0 - ADAM_B1) * g
        vn = ADAM_B2 * v_ref[...] + (1.0 - ADAM_B2) * (g * g)
        m_hat = mn / (1.0 - ADAM_B1 ** ADAM_STEP)
        v_hat = vn / (1.0 - ADAM_B2 ** ADAM_STEP)
        go_ref[...] = g
        d_ref[...] = -ADAM_LR * (m_hat / (jnp.sqrt(v_hat) + ADAM_EPS) + ADAM_WD * w_ref[...])
        mo_ref[...] = mn
        vo_ref[...] = vn

    blk = pl.BlockSpec((None, tr, c), lambda i: (layer, i, 0))
    earlier = () if into is None else tuple(into)
    return pl.pallas_call(
        body, name=name, grid=(r // tr,),
        in_specs=[blk, pl.BlockSpec((ns, tr, c), lambda i: (0, i, 0)), blk, blk]
        + [pl.BlockSpec(memory_space=pl.ANY)] * len(earlier),
        out_specs=[blk] * 4, out_shape=[SDS((n_l, r, c), F32)] * 4,
        input_output_aliases={4 + q: q for q in range(len(earlier))},
        compiler_params=_cparams(("arbitrary",)),
    )(*_in_hbm((w, slots, m, v)), *earlier)


def _adam(w, gslots, m, v, name):
    r, c = w.shape
    ns = gslots.shape[0]
    tr = r
    while tr * c * 4 > (1 << 20) and tr % 16 == 0:
        tr //= 2
    assert r % tr == 0

    def body(w_ref, g_ref, m_ref, v_ref, go_ref, d_ref, mo_ref, vo_ref):
        g = g_ref[0].astype(F32)
        for s in range(1, ns):
            g = g + g_ref[s].astype(F32)
        wv = w_ref[...]
        mn = ADAM_B1 * m_ref[...] + (1.0 - ADAM_B1) * g
        vn = ADAM_B2 * v_ref[...] + (1.0 - ADAM_B2) * (g * g)
        m_hat = mn / (1.0 - ADAM_B1 ** ADAM_STEP)
        v_hat = vn / (1.0 - ADAM_B2 ** ADAM_STEP)
        go_ref[...] = g
        d_ref[...] = -ADAM_LR * (m_hat / (jnp.sqrt(v_hat) + ADAM_EPS) + ADAM_WD * wv)
        mo_ref[...] = mn
        vo_ref[...] = vn

    blk = pl.BlockSpec((tr, c), lambda i: (i, 0))
    return pl.pallas_call(
        body, name=name, grid=(r // tr,),
        in_specs=[blk, pl.BlockSpec((ns, tr, c), lambda i: (0, i, 0)), blk, blk],
        out_specs=[blk] * 4, out_shape=[SDS((r, c), F32)] * 4,
        compiler_params=_cparams(("parallel",)),
    )(w, gslots, m, v)


def _sum_slots(slots, name):
    ns, r, c = slots.shape

    def body(g_ref, o_ref):
        g = g_ref[0]
        for s in range(1, ns):
            g = g + g_ref[s]
        o_ref[...] = g

    return pl.pallas_call(
        body, name=name, grid=(1,),
        in_specs=[pl.BlockSpec((ns, r, c), lambda i: (0, 0, 0))], out_specs=pl.BlockSpec((r, c), lambda i: (0, 0)),
        out_shape=SDS((r, c), F32), compiler_params=_cparams(("arbitrary",)),
    )(slots)


def _adam_params(params, name):
    n = len(params)

    def body(*refs):
        ins, outs = refs[:5 * n], refs[5 * n:]
        for p in range(n):
            w_ref, m_ref, v_ref, g_first, g_rest = ins[5 * p:5 * p + 5]
            go_ref, d_ref, mo_ref, vo_ref = outs[4 * p:4 * p + 4]
            for part, g_ref in ((slice(0, 1), g_first), (slice(1, w_ref.shape[0]), g_rest)):
                g = g_ref[...]
                mn = ADAM_B1 * m_ref[part] + (1.0 - ADAM_B1) * g
                vn = ADAM_B2 * v_ref[part] + (1.0 - ADAM_B2) * (g * g)
                m_hat = mn / (1.0 - ADAM_B1 ** ADAM_STEP)
                v_hat = vn / (1.0 - ADAM_B2 ** ADAM_STEP)
                go_ref[part] = g
                d_ref[part] = -ADAM_LR * (m_hat / (jnp.sqrt(v_hat) + ADAM_EPS) + ADAM_WD * w_ref[part])
                mo_ref[part] = mn
                vo_ref[part] = vn

    def whole(a):
        return pl.BlockSpec(a.shape, lambda i, nd=a.ndim: (0,) * nd)

    flat = _in_hbm([a for prm in params for a in prm])
    outs = pl.pallas_call(
        body, name=name, grid=(1,),
        in_specs=[whole(a) for a in flat],
        out_specs=[whole(prm[0]) for prm in params for _ in range(4)],
        out_shape=[SDS(prm[0].shape, F32) for prm in params for _ in range(4)],
        compiler_params=_cparams(("arbitrary",)),
    )(*flat)
    return [outs[4 * p:4 * p + 4] for p in range(n)]


SMALL_NAMES = ("pre_norm", "post_norm", "ssm_a_re", "ssm_a_im", "ssm_log_dt", "ssm_b_re", "ssm_b_im", "ssm_c_re",
               "ssm_c_im", "ssm_d", "ssm_glu_b")
SSM_NAMES = ("ssm_a_re", "ssm_a_im", "ssm_log_dt", "ssm_b_re", "ssm_b_im", "ssm_c_re", "ssm_c_im")
WEIGHT_ORDER = ("pre_norm", "post_norm", "even_w_in", "even_w_out", "ssm_a_re", "ssm_a_im", "ssm_log_dt", "ssm_b_re",
                "ssm_b_im", "ssm_c_re", "ssm_c_im", "ssm_d", "ssm_glu_w", "ssm_glu_b", "odd_w_in", "pool_w",
                "pool_scale", "odd_w_out")
PACK_ROWS_ALIGN = 8


def _pack(parts):
    flat = jnp.concatenate([p.reshape(-1).astype(F32) for p in parts])
    rows = -(-flat.shape[0] // (LANES * PACK_ROWS_ALIGN)) * PACK_ROWS_ALIGN
    return jnp.pad(flat, (0, rows * LANES - flat.shape[0])).reshape(rows, LANES)


def _unpack(packed, shapes):
    flat = packed.reshape(-1)
    out, off = [], 0
    for shp in shapes:
        size = math.prod(shp)
        out.append(flat[off:off + size].reshape(shp))
        off += size
    return out


EVEN_SHARDED = ("w_in", "w_out", "glu_w")
ODD_SHARDED = ("w_in", "pool_w", "w_out")
FAMILY = {(0, "w_in"): "even_w_in", (0, "w_out"): "even_w_out", (0, "glu_w"): "ssm_glu_w",
          (1, "w_in"): "odd_w_in", (1, "pool_w"): "pool_w", (1, "w_out"): "odd_w_out"}


def _sharded_keys(layer):
    return EVEN_SHARDED if layer % 2 == 0 else ODD_SHARDED


def _local_step(x, tgt, small, get_weights, on_w, on_ssm, on_grads, zero=0.0):
    tables = _rope_tables(zero) + (_attention_bias(zero),)
    preps, prep_vjps = [], []
    for i in range(2):
        out, vjp = jax.vjp(_ssm_prep, small["ssm_a_re"][i] + zero, small["ssm_a_im"][i], small["ssm_log_dt"][i],
                           small["ssm_b_re"][i], small["ssm_b_im"][i], small["ssm_c_re"][i], small["ssm_c_im"][i])
        preps.append(out)
        prep_vjps.append(vjp)

    def layer_args(layer, wts):
        i = layer // 2
        pre, post = _row(small["pre_norm"][layer]) + wts.get("token", 0.0), _row(small["post_norm"][layer])
        if layer % 2 == 0:
            return (pre, post, wts["w_in"], wts["late"], _row(small["ssm_glu_b"][i]), _row(small["ssm_d"][i]),
                    preps[i], tables)
        return (pre, post, wts["w_in"], wts["pool_w"], _row(wts["pool_scale"]), wts["w_out"])

    saved, args = [], []
    cur = x
    for layer in range(4):
        after = (cur,) if layer else (cur, tables[0], tables[3], preps[0][1], preps[0][2], preps[1][1], preps[1][2])
        args.append(layer_args(layer, get_weights(layer, after)))
        if layer == 0:
            h = _norm_fwd(cur, args[0][0])
        if layer < 3:
            def tail(xv, yv, post, next_gain=_row(small["pre_norm"][layer + 1])):
                return tuple(_post_fwd(xv, yv, post, next_gain))
        else:
            def tail(xv, yv, post):
                return tuple(_post_fwd_loss(xv, yv, post, tgt))
        cur, h, sv = (_even_fwd if layer % 2 == 0 else _odd_fwd)(cur, h, tail, *args[layer])
        saved.append(sv)
    g, sq = cur, h
    loss = 0.5 * jnp.sum(sq) / D

    lg = [None] * 4
    token = jnp.zeros((), F32)
    for layer in reversed(range(4)):
        largs = list(args[layer])
        largs[1] = largs[1] + token
        hooks = dict(on_w=functools.partial(on_w, layer))
        ssm_grads = []
        if layer % 2 == 0:
            def ssm_hook(cotangents, layer=layer):
                ssm_grads.append(prep_vjps[layer // 2](cotangents))
                return on_ssm(layer, ssm_grads[0])

            hooks["on_ssm"] = ssm_hook
        g, lg[layer] = (_even_bwd if layer % 2 == 0 else _odd_bwd)(g, saved[layer], *largs, **hooks)
        if ssm_grads:
            lg[layer]["ssm"] = ssm_grads[0]
        token = on_grads(layer, lg[layer])
    return loss, g, token


def _to_slots(key, gfull):
    if key == "w_in":
        return gfull
    if key in ("w_out", "glu_w"):
        rr, nn = gfull.shape
        return gfull.reshape(N_DEV, rr // N_DEV, nn)
    assert key == "pool_w"
    gg, rr, nn = gfull.shape
    return gfull.reshape(gg, N_DEV, rr // N_DEV, nn).transpose(1, 0, 2, 3)


def _from_gathered(key, gat):
    if key == "w_in":
        return gat
    if key in ("w_out", "glu_w"):
        _, rr, nn = gat.shape
        return gat.reshape(N_DEV * rr, nn)
    assert key == "pool_w"
    _, gg, rr, nn = gat.shape
    return gat.transpose(1, 0, 2, 3).reshape(gg, N_DEV * rr, nn)


def kernel(x, pre_norm, post_norm, even_w_in, even_w_out, ssm_a_re, ssm_a_im, ssm_log_dt, ssm_b_re, ssm_b_im, ssm_c_re, ssm_c_im, ssm_d, ssm_glu_w, ssm_glu_b, odd_w_in, pool_w, pool_scale, odd_w_out, loss_target, m_pre_norm, m_post_norm, m_even_w_in, m_even_w_out, m_ssm_a_re, m_ssm_a_im, m_ssm_log_dt, m_ssm_b_re, m_ssm_b_im, m_ssm_c_re, m_ssm_c_im, m_ssm_d, m_ssm_glu_w, m_ssm_glu_b, m_odd_w_in, m_pool_w, m_pool_scale, m_odd_w_out, v_pre_norm, v_post_norm, v_even_w_in, v_even_w_out, v_ssm_a_re, v_ssm_a_im, v_ssm_log_dt, v_ssm_b_re, v_ssm_b_im, v_ssm_c_re, v_ssm_c_im, v_ssm_d, v_ssm_glu_w, v_ssm_glu_b, v_odd_w_in, v_pool_w, v_pool_scale, v_odd_w_out):
    w = dict(pre_norm=pre_norm, post_norm=post_norm, even_w_in=even_w_in, even_w_out=even_w_out, ssm_a_re=ssm_a_re,
             ssm_a_im=ssm_a_im, ssm_log_dt=ssm_log_dt, ssm_b_re=ssm_b_re, ssm_b_im=ssm_b_im, ssm_c_re=ssm_c_re,
             ssm_c_im=ssm_c_im, ssm_d=ssm_d, ssm_glu_w=ssm_glu_w, ssm_glu_b=ssm_glu_b, odd_w_in=odd_w_in,
             pool_w=pool_w, pool_scale=pool_scale, odd_w_out=odd_w_out)
    mom = dict(pre_norm=m_pre_norm, post_norm=m_post_norm, even_w_in=m_even_w_in, even_w_out=m_even_w_out,
               ssm_a_re=m_ssm_a_re, ssm_a_im=m_ssm_a_im, ssm_log_dt=m_ssm_log_dt, ssm_b_re=m_ssm_b_re,
               ssm_b_im=m_ssm_b_im, ssm_c_re=m_ssm_c_re, ssm_c_im=m_ssm_c_im, ssm_d=m_ssm_d, ssm_glu_w=m_ssm_glu_w,
               ssm_glu_b=m_ssm_glu_b, odd_w_in=m_odd_w_in, pool_w=m_pool_w, pool_scale=m_pool_scale,
               odd_w_out=m_odd_w_out)
    var = dict(pre_norm=v_pre_norm, post_norm=v_post_norm, even_w_in=v_even_w_in, even_w_out=v_even_w_out,
               ssm_a_re=v_ssm_a_re, ssm_a_im=v_ssm_a_im, ssm_log_dt=v_ssm_log_dt, ssm_b_re=v_ssm_b_re,
               ssm_b_im=v_ssm_b_im, ssm_c_re=v_ssm_c_re, ssm_c_im=v_ssm_c_im, ssm_d=v_ssm_d, ssm_glu_w=v_ssm_glu_w,
               ssm_glu_b=v_ssm_glu_b, odd_w_in=v_odd_w_in, pool_w=v_pool_w, pool_scale=v_pool_scale,
               odd_w_out=v_odd_w_out)
    me = _my_index()
    scale_cols = pool_scale.shape[1]

    def shards_of(layer, keys):
        i = layer // 2
        shards = [w[FAMILY[(layer % 2, k)]][i].astype(BF16) for k in keys]
        if layer % 2 == 1:
            shards.append(jnp.pad(pool_scale[i][None], ((0, PACK_ROWS_ALIGN - 1), (0, 0))))
        return shards

    def start_gather(tag, after=()):
        return _xchg_start(f"gather_start_{tag}", shards[tag], True, after)

    shards = {0: shards_of(0, EVEN_SHARDED[:1]), "0_late": shards_of(0, EVEN_SHARDED[1:])}
    shards.update({layer: shards_of(layer, _sharded_keys(layer)) for layer in (1, 2, 3)})
    gather_started = {0: start_gather(0)}
    small = {nm: w[nm] for nm in SMALL_NAMES}
    packed_names = ("pre_norm", "post_norm") + SSM_NAMES + ("ssm_d", "ssm_glu_b")
    tails = {nm: (SSM_GROUPS, SSM_STATE * SSM_GROUP) if nm in ("ssm_b_re", "ssm_b_im") else w[nm].shape[1:]
             for nm in packed_names}
    dense = lambda nm, a: a.reshape((a.shape[0],) + tails[nm])
    small_operands = {nm: tuple(dense(nm, tree[nm]) for tree in (w, mom, var)) for nm in packed_names}
    early_work = [a for nm in ("ssm_b_re", "ssm_b_im") for a in small_operands[nm]]

    def get_weights(layer, after):
        keys = EVEN_SHARDED[:1] if layer == 0 else _sharded_keys(layer)
        if layer == 0:
            after = tuple(after) + tuple(early_work)
        lands = _xchg_wait(f"gather_wait_{layer}", gather_started[layer], True, after)
        wts = {k: _from_gathered(k, gat) for k, gat in zip(keys, lands)}
        if layer % 2 == 1:
            wts["pool_scale"] = lands[-1][:, 0, :].reshape(N_DEV * scale_cols)
        if layer == 0:
            prev = gather_started["0_late"] = start_gather("0_late", after=(lands[0],))
            for later in (1, 2, 3):
                prev = gather_started[later] = start_gather(later, after=(prev[4],))
            wts["token"] = sum(gather_started[tag][4][0, 0] for tag in ("0_late", 1, 2, 3))

            def late(after_late):
                late_lands = _xchg_wait("gather_wait_0_late", gather_started["0_late"], True, (after_late,))
                return tuple(_from_gathered(k, gat) for k, gat in zip(EVEN_SHARDED[1:], late_lands))

            wts["late"] = late
        elif layer == 2:
            wts["late"] = lambda after_late: (wts["w_out"], wts["glu_w"])
        return wts

    scatter_started = []

    def on_w(layer, gw):
        keys = tuple(k for k in _sharded_keys(layer) if k in gw)
        started = _xchg_start(f"scatter_start_{layer}_{keys[0]}", [_to_slots(k, gw[k]) for k in keys], False)
        scatter_started.append((layer, keys, started))
        return started[4]

    def wait_scatters(layers, after):
        for layer, keys, started in scatter_started:
            if layer in layers:
                lands = _xchg_wait(f"scatter_wait_{layer}_{keys[0]}", started, False, after)
                for k, land in zip(keys, lands):
                    recv[(layer, k)] = land

    layer_grads = {}
    early_started, mid_started = [], []

    def on_ssm(layer, ssm_grads):
        if layer != 0:
            return None
        mid_started.append(_xchg_start("mid_start", [_pack(list(ssm_grads))], True))
        return mid_started[0][4]

    def on_grads(layer, lg):
        layer_grads[layer] = lg
        zero = jnp.zeros((), F32)
        if layer == 1:
            lgs = layer_grads
            early = ([jnp.concatenate([lgs[l][k] for l in (1, 2, 3)], axis=0) for k in ("pre", "post")]
                     + list(lgs[2]["ssm"]) + [lgs[2]["ssm_d"], lgs[2]["glu_b"],
                                              jnp.concatenate([lgs[1]["pool_scale"], lgs[3]["pool_scale"]], axis=0)])
            early_started.append(_xchg_start("small_start", [_pack(early)], True))
            zero = zero + early_started[0][4][0, 0]
        return zero

    loss_local, grad_x, token = _local_step(x[0], loss_target[0], small, get_weights, on_w, on_ssm, on_grads,
                                            zero=gather_started[0][4][0, 0])

    lg0 = layer_grads[0]
    late_started = _xchg_start("late_start", [_pack([lg0["pre"], lg0["post"], lg0["ssm_d"], lg0["glu_b"],
                                                     loss_local.reshape(1)]) + token], True)

    def adam_family(parity, k, which, into=None):
        nm = FAMILY[(parity, k)]
        cols = w[nm].shape[-1]
        return _adam_layer(w[nm].reshape(2, -1, cols), recv[(parity + 2 * which, k)].reshape(N_DEV, -1, cols),
                           mom[nm].reshape(2, -1, cols), var[nm].reshape(2, -1, cols), which,
                           f"adam_{nm}_{which}", into)

    recv, res = {}, {}
    wait_scatters((3, 2, 1), (late_started[4],))
    for k in ODD_SHARDED:
        res[FAMILY[(1, k)]] = adam_family(1, k, 1, adam_family(1, k, 0))
    half_done = {k: adam_family(0, k, 1) for k in EVEN_SHARDED}
    odd_done = tuple(half_done[k][0] for k in EVEN_SHARDED)

    (early_slots,) = _xchg_wait("small_wait", early_started[0], True, odd_done)
    (mid_slots,) = _xchg_wait("mid_wait", mid_started[0], True, odd_done)
    early_shapes = [(w[nm].shape[0] - 1,) + tails[nm] for nm in packed_names] + [(2, N_DEV * scale_cols)]
    g_early = _unpack(_sum_slots(early_slots, "sum_small_early"), early_shapes)
    g_mid = _unpack(_sum_slots(mid_slots, "sum_small_mid"), [(1,) + tails[nm] for nm in SSM_NAMES])

    (late_slots,) = _xchg_wait("late_wait", late_started, True, (g_early[0], g_mid[0]))
    wait_scatters((0,), (late_slots,))
    for k in EVEN_SHARDED:
        res[FAMILY[(0, k)]] = adam_family(0, k, 0, half_done[k])
    for nm in FAMILY.values():
        res[nm] = [o.reshape(w[nm].shape) for o in res[nm]]

    late_names = ("pre_norm", "post_norm", "ssm_d", "ssm_glu_b")
    g_late = _unpack(_sum_slots(late_slots, "sum_small_late"), [(1,) + tails[nm] for nm in late_names] + [(1,)])
    g_first = dict(zip(late_names, g_late))
    g_first.update(zip(SSM_NAMES, g_mid))
    outs = _adam_params([small_operands[nm] + (g_first[nm], g_early[j]) for j, nm in enumerate(packed_names)],
                        "adam_small")
    for nm, four in zip(packed_names, outs):
        res[nm] = [o.reshape(w[nm].shape) for o in four]
    loss = g_late[-1].reshape(())
    g_scale = lax.dynamic_slice_in_dim(g_early[-1], me * scale_cols, scale_cols, axis=1)
    pad = ((0, PACK_ROWS_ALIGN - 2), (0, 0))
    outs = _adam(jnp.pad(pool_scale, pad), jnp.pad(g_scale, pad)[None], jnp.pad(m_pool_scale, pad),
                 jnp.pad(v_pool_scale, pad), name="adam_pool_scale")
    res["pool_scale"] = [o[:2] for o in outs]

    out = [loss, grad_x[None]]
    for kind in range(4):
        out += [res[nm][kind] for nm in WEIGHT_ORDER]
    return tuple(out)
```

```python
import functools
import math

import jax
import jax.numpy as jnp
from jax import lax
from jax.experimental import pallas as pl
from jax.experimental.pallas import tpu as pltpu

F32 = jnp.float32
BF16 = jnp.bfloat16
SDS = jax.ShapeDtypeStruct

N_DEV = 8
S = 2048
D = 1024
HEAD_DIM = 64
ROT_DIM = 16
ROPE_THETA = 500000.0
ATT_W = 1024
SSM_W = 512
SSM_GROUPS = 32
SSM_GROUP = 16
SSM_STATE = 64
N_CPLX = SSM_GROUPS * SSM_STATE
POOL_W = 2048
POOL_GROUP = 512
EVEN_IN = 5120
EVEN_OUT = 1536
ODD_IN = 4096
RMS_EPS = 1e-6
LANES = 128
VMEM_LIMIT = 48 * 1024 * 1024

ADAM_LR = 0.001
ADAM_B1 = 0.9
ADAM_B2 = 0.999
ADAM_EPS = 1e-08
ADAM_WD = 0.01
ADAM_STEP = 10

MESH_ID = pl.DeviceIdType.MESH
NN = (((1,), (0,)), ((), ()))
NT = (((1,), (1,)), ((), ()))
TN = (((0,), (0,)), ((), ()))
_DN = {"nn": NN, "nt": NT, "tn": TN}


def _cparams(sem):
    return pltpu.CompilerParams(dimension_semantics=sem, vmem_limit_bytes=VMEM_LIMIT)


def _in_hbm(arrs):
    return [pltpu.with_memory_space_constraint(a, pltpu.HBM) for a in arrs]


MM_TILES = (1024, 768, 512)


def _tile(dim):
    return next((t for t in MM_TILES if dim % t == 0), dim)


BLOCK_PAIR = 2
MAX_WHOLE_K = 2048


def _mm(a, b, mode, out_dtype, b_blocks=False, out_blocks=False, after=()):
    if b_blocks:
        nblk, rows, cb = b.shape
        b2_shape = (rows, nblk * cb)
    else:
        b2_shape = b.shape
    if mode == "nn":
        (m, k), n = a.shape, b2_shape[1]
    elif mode == "nt":
        (m, k), n = a.shape, b2_shape[0]
    else:
        (k, m), n = a.shape, b2_shape[1]
    tm, tn, tk = _tile(m), _tile(n), _tile(k)
    if k <= MAX_WHOLE_K:
        tk = k
    if b_blocks and mode == "nn":
        tn = BLOCK_PAIR * cb
        if tn <= MM_TILES[0]:
            tm = m
    if b_blocks and mode == "nt":
        tk = BLOCK_PAIR * cb
    if out_blocks:
        cb = n // N_DEV
        tn = BLOCK_PAIR * cb
        tk = k
    nk = k // tk

    def body(a_ref, b_ref, *rest):
        o_ref, acc_ref = rest[-2:]
        kk = pl.program_id(2)
        bv = jnp.concatenate([b_ref[p] for p in range(BLOCK_PAIR)], axis=1) if b_blocks else b_ref[...]
        part = lax.dot_general(a_ref[...].astype(BF16), bv.astype(BF16), _DN[mode], preferred_element_type=F32)

        def write(res):
            if out_blocks:
                for p in range(BLOCK_PAIR):
                    o_ref[p] = res[:, p * cb:(p + 1) * cb].astype(o_ref.dtype)
            else:
                o_ref[...] = res.astype(o_ref.dtype)

        if nk == 1:
            write(part)
            return

        @pl.when(kk == 0)
        def _():
            acc_ref[...] = part

        @pl.when((kk > 0) & (kk < nk - 1))
        def _():
            acc_ref[...] += part

        @pl.when(kk == nk - 1)
        def _():
            write(acc_ref[...] + part)

    if mode == "nn":
        a_spec = pl.BlockSpec((tm, tk), lambda i, j, kk: (i, kk))
        b_spec = pl.BlockSpec((tk, tn), lambda i, j, kk: (kk, j))
    elif mode == "nt":
        a_spec = pl.BlockSpec((tm, tk), lambda i, j, kk: (i, kk))
        b_spec = pl.BlockSpec((tn, tk), lambda i, j, kk: (j, kk))
    else:
        a_spec = pl.BlockSpec((tk, tm), lambda i, j, kk: (kk, i))
        b_spec = pl.BlockSpec((tk, tn), lambda i, j, kk: (kk, j))
    if b_blocks and mode == "nn":
        b_spec = pl.BlockSpec((BLOCK_PAIR, tk, cb), lambda i, j, kk: (j, kk, 0))
    if b_blocks and mode == "nt":
        b_spec = pl.BlockSpec((BLOCK_PAIR, tn, cb), lambda i, j, kk: (kk, j, 0))
    out_spec = pl.BlockSpec((tm, tn), lambda i, j, kk: (i, j))
    out_shape = SDS((m, n), out_dtype)
    if out_blocks:
        out_spec = pl.BlockSpec((BLOCK_PAIR, tm, cb), lambda i, j, kk: (j, i, 0))
        out_shape = SDS((N_DEV, m, cb), out_dtype)
    return pl.pallas_call(
        body, name=f"mm_{mode}_{m}x{k}x{n}",
        grid=(m // tm, n // tn, nk),
        in_specs=[a_spec, b_spec] + [pl.BlockSpec(memory_space=pl.ANY)] * len(after),
        out_specs=out_spec,
        out_shape=out_shape,
        scratch_shapes=[pltpu.VMEM((tm, tn) if nk > 1 else (8, LANES), F32)],
        compiler_params=_cparams(("parallel", "parallel", "arbitrary")),
    )(a, b, *after)


def _gmm(a, b, mode, out_dtype, tm=S):
    ng, gw = POOL_W // POOL_GROUP, POOL_GROUP
    ns = S // tm
    if mode in ("nn", "nt"):
        def body(a_ref, b_ref, o_ref):
            o_ref[...] = lax.dot_general(a_ref[...].astype(BF16), b_ref[...].astype(BF16), _DN[mode],
                                         preferred_element_type=F32).astype(o_ref.dtype)

        return pl.pallas_call(
            body, name=f"gmm_{mode}", grid=(ng, ns),
            in_specs=[pl.BlockSpec((tm, gw), lambda g, i: (i, g)),
                      pl.BlockSpec((None, gw, gw), lambda g, i: (g, 0, 0))],
            out_specs=pl.BlockSpec((tm, gw), lambda g, i: (i, g)),
            out_shape=SDS((S, POOL_W), out_dtype),
            compiler_params=_cparams(("parallel", "parallel")),
        )(a, b)

    def body_tn(a_ref, b_ref, o_ref, acc_ref):
        i = pl.program_id(1)

        @pl.when(i == 0)
        def _():
            acc_ref[...] = jnp.zeros_like(acc_ref)

        acc_ref[...] += lax.dot_general(a_ref[...].astype(BF16), b_ref[...].astype(BF16), TN,
                                        preferred_element_type=F32)

        @pl.when(i == ns - 1)
        def _():
            o_ref[...] = acc_ref[...].astype(o_ref.dtype)

    return pl.pallas_call(
        body_tn, name="gmm_tn", grid=(ng, ns),
        in_specs=[pl.BlockSpec((tm, gw), lambda g, i: (i, g)),
                  pl.BlockSpec((tm, gw), lambda g, i: (i, g))],
        out_specs=pl.BlockSpec((None, gw, gw), lambda g, i: (g, 0, 0)),
        out_shape=SDS((ng, gw, gw), out_dtype),
        scratch_shapes=[pltpu.VMEM((gw, gw), F32)],
        compiler_params=_cparams(("parallel", "arbitrary")),
    )(a, b)


def _rowwise(fn, inputs, out_defs, acc_defs=(), tm=512, name=None, after=()):
    n_in, n_out, n_acc = len(inputs), len(out_defs), len(acc_defs)
    n_after = len(after)
    in_specs, args = [], []
    for arr, width, cb in inputs:
        if arr.shape[0] != S:
            in_specs.append(pl.BlockSpec((arr.shape[0], width), lambda i, cb=cb: (0, cb)))
        else:
            in_specs.append(pl.BlockSpec((tm, width), lambda i, cb=cb: (i, cb)))
        args.append(arr)
    out_defs = [d if len(d) == 4 else (d[0], d[1], d[0], 0) for d in out_defs]
    out_shape = [SDS((S, ww), dt) for _, dt, ww, _ in out_defs] + [SDS((1, w), F32) for w in acc_defs]
    out_specs = ([pl.BlockSpec((tm, w), lambda i, cb=cb: (i, cb)) for w, _, _, cb in out_defs]
                 + [pl.BlockSpec((1, w), lambda i: (0, 0)) for w in acc_defs])

    def kern(*refs):
        vals = [r[...] for r in refs[:n_in]]
        outs, accs = fn(*vals)
        out_refs = refs[n_in + n_after:]
        for r, v in zip(out_refs[:n_out], outs):
            r[...] = v.astype(r.dtype)
        if n_acc:
            acc_refs = out_refs[n_out:]

            @pl.when(pl.program_id(0) == 0)
            def _():
                for r in acc_refs:
                    r[...] = jnp.zeros_like(r)

            for r, v in zip(acc_refs, accs):
                r[...] += jnp.sum(v, axis=0, keepdims=True)

    res = pl.pallas_call(
        kern, name=name, grid=(S // tm,), in_specs=in_specs + [pl.BlockSpec(memory_space=pl.ANY)] * n_after,
        out_specs=out_specs, out_shape=out_shape, compiler_params=_cparams(("arbitrary",)),
    )(*args, *after)
    return res


def _sigmoid(x):
    return 1.0 / (1.0 + jnp.exp(-x))


def _silu_and_grad(x):
    s = _sigmoid(x)
    return x * s, s * (1.0 + x * (1.0 - s))


_GELU_K = math.sqrt(2.0 / math.pi)
_GELU_C = 0.044715


def _gelu_and_grad(x):
    t = jnp.tanh(_GELU_K * (x + _GELU_C * (x * x * x)))
    cdf = 0.5 * (1.0 + t)
    grad = cdf + 0.5 * x * (1.0 - t * t) * (_GELU_K * (1.0 + 3.0 * _GELU_C * x * x))
    return x * cdf, grad


def _rms(xv, gain):
    r = lax.rsqrt(jnp.mean(xv * xv, axis=-1, keepdims=True) + RMS_EPS)
    return xv * r * gain


def _rms_bwd(dout, xv, gain):
    r = lax.rsqrt(jnp.mean(xv * xv, axis=-1, keepdims=True) + RMS_EPS)
    xhat = xv * r
    dxhat = dout * gain
    dx = r * (dxhat - xhat * jnp.mean(dxhat * xhat, axis=-1, keepdims=True))
    return dx, dout * xhat


def _norm_fwd(x, gain):
    (h,) = _rowwise(lambda xv, g: ((_rms(xv, g),), ()), [(x, D, 0), (gain, D, 0)], [(D, BF16)], name="norm_fwd")
    return h


def _post_fwd(x, y, gain, next_gain):
    def fn(xv, yv, g, gn):
        out = xv + _rms(yv, g)
        return (out, _rms(out, gn)), ()

    return _rowwise(fn, [(x, D, 0), (y, D, 0), (gain, D, 0), (next_gain, D, 0)], [(D, F32), (D, BF16)],
                    name="post_fwd")


def _post_fwd_loss(x, y, gain, tgt):
    def fn(xv, yv, g, tv):
        e = xv + _rms(yv, g) - tv
        return (e * (1.0 / D),), (e * e,)

    return _rowwise(fn, [(x, D, 0), (y, D, 0), (gain, D, 0), (tgt, D, 0)], [(D, F32)], [D], name="post_fwd_loss")


def _post_bwd(g, y, gain):
    def fn(gv, yv, gn):
        dx, dg = _rms_bwd(gv, yv, gn)
        return (dx,), (dg,)

    return _rowwise(fn, [(g, D, 0), (y, D, 0), (gain, D, 0)], [(D, BF16)], [D], name="post_bwd")


def _pre_bwd(g, dh, x, gain):
    def fn(gv, dhv, xv, gn):
        dx, dg = _rms_bwd(dhv, xv, gn)
        return (gv + dx,), (dg,)

    return _rowwise(fn, [(g, D, 0), (dh, D, 0), (x, D, 0), (gain, D, 0)], [(D, F32)], [D], name="pre_bwd")


def _pool(u_arr, col_block, transpose, out_dtype, into=None, tc=256):
    n_t = POOL_W // tc
    per_group = POOL_GROUP // tc

    def body(u_ref, *rest):
        o_ref = rest[-1]
        grp = pl.program_id(0) // per_group
        t = lax.broadcasted_iota(jnp.int32, (S, 1), 0)
        for g in range(POOL_W // POOL_GROUP):
            @pl.when(grp == g)
            def _(g=g):
                xv = u_ref[...]
                cnt = jnp.minimum(t + 1, 2 << g).astype(F32)
                cur = xv / cnt if transpose else xv
                for k in (1, 2, 4, 8)[:g + 1]:
                    if transpose:
                        cur = cur + jnp.where(t < S - k, pltpu.roll(cur, S - k, 0), 0.0)
                    else:
                        cur = cur + jnp.where(t >= k, pltpu.roll(cur, k, 0), 0.0)
                res = cur - xv if transpose else cur / cnt - xv
                o_ref[...] = res.astype(o_ref.dtype)

    in_specs = [pl.BlockSpec((S, tc), lambda c: (0, col_block * n_t + c))]
    args = [u_arr]
    if into is not None:
        in_specs.append(pl.BlockSpec(memory_space=pl.ANY))
        args.append(into)
    return pl.pallas_call(
        body, name="pool_bwd" if transpose else "pool_fwd", grid=(n_t,),
        in_specs=in_specs,
        out_specs=pl.BlockSpec((S, tc), lambda c: (0, c)),
        out_shape=SDS((S, POOL_W) if into is None else into.shape, out_dtype),
        input_output_aliases={} if into is None else {1: 0},
        compiler_params=_cparams(("parallel",)),
    )(*args)


def _rope_tables(zero):
    pos = jnp.arange(S, dtype=jnp.int32).astype(F32) + zero
    inv_freq = ROPE_THETA ** (-jnp.arange(0, ROT_DIM, 2, dtype=F32) / ROT_DIM)
    ang = pos[:, None] * inv_freq[None, :]
    cos8, sin8 = jnp.cos(ang), jnp.sin(ang)
    half = ROT_DIM // 2
    zeros = jnp.zeros((S, HEAD_DIM - ROT_DIM), F32)
    cos = jnp.concatenate([cos8, cos8, jnp.ones((S, HEAD_DIM - ROT_DIM), F32)], axis=1)
    sin = jnp.concatenate([-sin8, sin8, zeros], axis=1)
    rep = LANES // HEAD_DIM
    lane = jnp.arange(LANES)
    dim = lane % HEAD_DIM
    partner = jnp.where(dim < half, lane + half, jnp.where(dim < ROT_DIM, lane - half, -1))
    swap = (lane[:, None] == partner[None, :]).astype(BF16)
    return jnp.tile(cos, (1, rep)), jnp.tile(sin, (1, rep)), swap


def _rotate(xv, cos, sin, swap, transpose):
    rep = xv.shape[1] // LANES
    wide = lambda tab: jnp.concatenate([tab] * rep, axis=1)
    xb = xv.astype(BF16)
    partner = jnp.concatenate([lax.dot_general(xb[:, t * LANES:(t + 1) * LANES], swap, NN, preferred_element_type=F32)
                               for t in range(rep)], axis=1)
    mixed = partner * wide(sin)
    return xv * wide(cos) - mixed if transpose else xv * wide(cos) + mixed


def _qkv_prep(proj, tables):
    cos, sin, swap = tables

    def fn(x, c, s, sw):
        rot = _rotate(x[:, :2 * ATT_W], c, s, sw, False)
        return (jnp.concatenate([(rot[:, :ATT_W] * HEAD_DIM ** -0.5).astype(BF16), rot[:, ATT_W:].astype(BF16),
                                 x[:, 2 * ATT_W:].astype(BF16)], axis=1),), ()

    (qkv,) = _rowwise(fn, [(proj, 3 * ATT_W, 0), (cos, LANES, 0), (sin, LANES, 0), (swap, LANES, 0)],
                      [(3 * ATT_W, BF16)], name="qkv_prep")
    return qkv


ATT_T = 512


def _multiplicity(delta):
    ok = delta >= 0
    near = jnp.where(ok & (delta <= 128), 1.0, 0.0)
    mid = jnp.where(ok & (delta <= 512) & ((delta & 3) == 0), 1.0, 0.0)
    far = jnp.where(ok & ((delta & 15) == 0), 1.0, 0.0)
    return near + mid + far


def _attention_bias(zero):
    t = ATT_T
    pos = jnp.arange(t, dtype=jnp.int32) + jnp.asarray(zero).astype(jnp.int32)
    delta = jnp.arange(S // t, dtype=jnp.int32)[:, None, None] * t + pos[None, :, None] - pos[None, None, :]
    mult = _multiplicity(delta)
    return jnp.where(mult > 0.0, jnp.log(jnp.maximum(mult, 1.0)), -1e30).astype(F32)


def _head_split(v, first):
    zero = jnp.zeros_like(v)
    return [jnp.where(first, v, zero), jnp.where(first, zero, v)]


def _flash_fwd(qkv, bias):
    t = ATT_T
    n_hp = ATT_W // LANES

    def body(q_ref, k_ref, v_ref, b_ref, o_ref, lse_ref):
        i = pl.program_id(1)
        first = lax.broadcasted_iota(jnp.int32, (1, LANES), 1) < HEAD_DIM
        qs = _head_split(q_ref[...], first)

        def kv_step(j, carry):
            m0, l0, m1, l1, acc = carry
            off = pl.multiple_of(j * t, t)
            kb = k_ref[pl.ds(off, t), :]
            vs = _head_split(v_ref[pl.ds(off, t), :], first)
            bias_t = b_ref[i - j]
            new = []
            pv = None
            for h, (m_prev, l_prev) in enumerate(((m0, l0), (m1, l1))):
                s = lax.dot_general(qs[h], kb, NT, preferred_element_type=F32) + bias_t
                m_new = jnp.maximum(m_prev, jnp.max(s, axis=1, keepdims=True))
                p = jnp.exp(s - m_new)
                alpha = jnp.exp(m_prev - m_new)
                l_new = alpha * l_prev + jnp.sum(p, axis=1, keepdims=True)
                d = lax.dot_general(p.astype(BF16), vs[h], NN, preferred_element_type=F32)
                pv = d if pv is None else pv + d
                new.append((m_new, l_new, alpha))
            acc = acc * jnp.where(first, new[0][2], new[1][2]) + pv
            return new[0][0], new[0][1], new[1][0], new[1][1], acc

        neg = jnp.full((t, 1), -1e30, F32)
        zero = jnp.zeros((t, 1), F32)
        m0, l0, m1, l1, acc = lax.fori_loop(0, i + 1, kv_step, (neg, zero, neg, zero, jnp.zeros((t, LANES), F32)))
        o_ref[...] = acc * jnp.where(first, 1.0 / l0, 1.0 / l1)
        lse_ref[...] = jnp.where(first, m0 + jnp.log(l0), m1 + jnp.log(l1))

    blk = pl.BlockSpec((t, LANES), lambda hp, i: (i, hp))
    k_full = pl.BlockSpec((S, LANES), lambda hp, i: (0, n_hp + hp))
    v_full = pl.BlockSpec((S, LANES), lambda hp, i: (0, 2 * n_hp + hp))
    return pl.pallas_call(
        body, name="flash_fwd", grid=(n_hp, S // t),
        in_specs=[blk, k_full, v_full, pl.BlockSpec((S // t, t, t), lambda hp, i: (0, 0, 0))], out_specs=[blk, blk],
        out_shape=[SDS((S, ATT_W), F32), SDS((S, ATT_W), F32)],
        compiler_params=_cparams(("parallel", "arbitrary")),
    )(qkv, qkv, qkv, bias)


def _flash_bwd(qkv, o, do, lse, bias, after=()):
    t = ATT_T
    n_hp = ATT_W // LANES
    n_t = S // t

    def body(q_ref, k_ref, v_ref, o_ref, do_ref, lse_ref, b_ref, *rest):
        dq_ref, dk_ref, dv_ref = rest[-3:]
        j = pl.program_id(1)
        first = lax.broadcasted_iota(jnp.int32, (1, LANES), 1) < HEAD_DIM

        @pl.when(j == 0)
        def _():
            dq_ref[...] = jnp.zeros_like(dq_ref)

        kb = k_ref[...]
        vb = v_ref[...]
        ks = _head_split(kb, first)

        def q_step(i, carry):
            dk_acc, dv_acc = carry
            rows = pl.ds(pl.multiple_of(i * t, t), t)
            qs = _head_split(q_ref[rows, :], first)
            dob = do_ref[rows, :]
            prod = dob * o_ref[rows, :]
            d_all = jnp.sum(prod, axis=1, keepdims=True)
            d0 = jnp.sum(jnp.where(first, prod, 0.0), axis=1, keepdims=True)
            lse_b = lse_ref[rows, :]
            lse0 = jnp.max(jnp.where(first, lse_b, -jnp.inf), axis=1, keepdims=True)
            lse1 = jnp.max(jnp.where(first, -jnp.inf, lse_b), axis=1, keepdims=True)
            dos = _head_split(dob.astype(BF16), first)
            bias_t = b_ref[i - j]
            dq_t = jnp.zeros((t, LANES), F32)
            for h, (lse_h, d_h) in enumerate(((lse0, d0), (lse1, d_all - d0))):
                s = lax.dot_general(qs[h], kb, NT, preferred_element_type=F32)
                p = jnp.exp(s + (bias_t - lse_h))
                dp = lax.dot_general(dos[h], vb, NT, preferred_element_type=F32)
                ds = (p * (dp - d_h)).astype(BF16)
                dv_acc = dv_acc + lax.dot_general(p.astype(BF16), dos[h], TN, preferred_element_type=F32)
                dk_acc = dk_acc + lax.dot_general(ds, qs[h], TN, preferred_element_type=F32)
                dq_t = dq_t + lax.dot_general(ds, ks[h], NN, preferred_element_type=F32)
            dq_ref[rows, :] += dq_t
            return dk_acc, dv_acc

        zero = jnp.zeros((t, LANES), F32)
        dk_acc, dv_acc = lax.fori_loop(j, n_t, q_step, (zero, zero))
        dk_ref[...] = dk_acc
        dv_ref[...] = dv_acc

    blk = pl.BlockSpec((t, LANES), lambda hp, j: (j, hp))
    full = pl.BlockSpec((S, LANES), lambda hp, j: (0, hp))
    k_blk = pl.BlockSpec((t, LANES), lambda hp, j: (j, n_hp + hp))
    v_blk = pl.BlockSpec((t, LANES), lambda hp, j: (j, 2 * n_hp + hp))
    return pl.pallas_call(
        body, name="flash_bwd", grid=(n_hp, n_t),
        in_specs=([full, k_blk, v_blk, full, full, full, pl.BlockSpec((n_t, t, t), lambda hp, j: (0, 0, 0))]
                  + [pl.BlockSpec(memory_space=pl.ANY)] * len(after)),
        out_specs=[full, blk, blk],
        out_shape=[SDS((S, ATT_W), F32)] * 3,
        compiler_params=_cparams(("parallel", "arbitrary")),
    )(qkv, qkv, qkv, o, do, lse, bias, *after)


SCAN_T = 256
SCAN_GROUP = 8
SCAN_STEPS = (1, 2, 4)
ST_ROWS = 2 * N_CPLX // LANES
HALF = ST_ROWS // 2


def _scan_tables(lam_t):
    lam = lax.complex(lam_t[:HALF].reshape(N_CPLX), lam_t[HALF:].reshape(N_CPLX))
    pows = jnp.cumprod(jnp.broadcast_to(lam, (SCAN_GROUP, N_CPLX)), axis=0)
    shifts = jnp.asarray(SCAN_STEPS)
    sub = jnp.arange(SCAN_GROUP)[None, :, None]
    steps = pows[shifts - 1][:, None, :]
    fwd = jnp.concatenate([jnp.where(sub >= shifts[:, None, None], steps, 0.0), pows[None]], axis=0)
    bwd = jnp.concatenate([jnp.where(sub <= SCAN_GROUP - 1 - shifts[:, None, None], jnp.conj(steps), 0.0),
                           jnp.conj(pows)[None, ::-1]], axis=0)

    def pack(tabs):
        return jnp.concatenate([jnp.real(tabs), jnp.imag(tabs)], axis=-1).astype(F32)

    return pack(fwd), pack(bwd)


def _cmul_add(xr, xi, lr, li, sr, si):
    return xr + lr * sr - li * si, xi + lr * si + li * sr


def _group_scan(xr, xi, tab_ref, cr, ci, reverse):
    for j, k in enumerate(SCAN_STEPS):
        shift = SCAN_GROUP - k if reverse else k
        xr, xi = _cmul_add(xr, xi, tab_ref[j, :, :N_CPLX], tab_ref[j, :, N_CPLX:],
                           pltpu.roll(xr, shift, 0), pltpu.roll(xi, shift, 0))
    return _cmul_add(xr, xi, tab_ref[3, :, :N_CPLX], tab_ref[3, :, N_CPLX:],
                     jnp.broadcast_to(cr, (SCAN_GROUP, N_CPLX)), jnp.broadcast_to(ci, (SCAN_GROUP, N_CPLX)))


SSM_SUPER = 4
SB_ROWS = SSM_W // SSM_SUPER
SB_COLS = N_CPLX // SSM_SUPER


def _super_blocks():
    return [(slice(b * SB_ROWS, (b + 1) * SB_ROWS), slice(h * SB_COLS, (h + 1) * SB_COLS),
             slice(h * N_CPLX + b * SB_COLS, h * N_CPLX + (b + 1) * SB_COLS))
            for b in range(SSM_SUPER) for h in range(2)]


def _dot16(a, b, dims):
    return lax.dot_general(a.astype(BF16), b.astype(BF16), dims, preferred_element_type=F32)


def _s5_fwd(tab, u_arr, u_cols, w_b, w_ct):
    nc = N_CPLX

    def body(tab_ref, u_ref, wb_ref, wct_ref, st_ref, y_ref, carry, bu_scr):
        @pl.when(pl.program_id(0) == 0)
        def _():
            carry[...] = jnp.zeros_like(carry)

        for rows_b, cols_c, cols_s in _super_blocks():
            bu_scr[:, cols_s] = _dot16(u_ref[:, rows_b], wb_ref[rows_b, cols_c], NN)

        def group(a, c):
            rows = pl.ds(pl.multiple_of(a * SCAN_GROUP, SCAN_GROUP), SCAN_GROUP)
            xr, xi = _group_scan(bu_scr[rows, :nc], bu_scr[rows, nc:], tab_ref, c[0], c[1], False)
            st_ref[rows, :nc] = xr
            st_ref[rows, nc:] = xi
            return xr[SCAN_GROUP - 1:SCAN_GROUP, :], xi[SCAN_GROUP - 1:SCAN_GROUP, :]

        cr, ci = lax.fori_loop(0, SCAN_T // SCAN_GROUP, group, (carry[:, :nc], carry[:, nc:]), unroll=2)
        carry[:, :nc] = cr
        carry[:, nc:] = ci

        for b in range(SSM_SUPER):
            (rows_b, cols_re, st_re), (_, cols_im, st_im) = _super_blocks()[2 * b:2 * b + 2]
            y_ref[:, rows_b] = (_dot16(st_ref[:, st_re], wct_ref[rows_b, cols_re], NT)
                                + _dot16(st_ref[:, st_im], wct_ref[rows_b, cols_im], NT))

    const = lambda shape: pl.BlockSpec(shape, lambda i: (0,) * len(shape))
    return pl.pallas_call(
        body, name="s5_fwd", grid=(S // SCAN_T,),
        in_specs=[const((4, SCAN_GROUP, 2 * nc)), pl.BlockSpec((SCAN_T, SSM_W), lambda i: (i, u_cols[0] // SSM_W)),
                  const((SSM_W, 2 * SB_COLS)), const((SSM_W, 2 * SB_COLS))],
        out_specs=[pl.BlockSpec((SCAN_T, 2 * nc), lambda i: (i, 0)), pl.BlockSpec((SCAN_T, SSM_W), lambda i: (i, 0))],
        out_shape=[SDS((S, 2 * nc), F32), SDS((S, SSM_W), F32)],
        scratch_shapes=[pltpu.VMEM((1, 2 * nc), F32), pltpu.VMEM((SCAN_T, 2 * nc), F32)],
        compiler_params=_cparams(("arbitrary",)),
    )(tab, u_arr, w_b, w_ct)


def _s5_bwd(tab, dy, states, u_arr, u_cols, w_b, w_ct):
    n_blk = S // SCAN_T
    nc = N_CPLX

    def body(tab_ref, dy_ref, x_ref, u_ref, wb_ref, wct_ref, du_ref, dlam_ref, dwb_ref, dwct_ref,
             carry, acc, d_scr, g_ref):
        i = pl.program_id(0)

        @pl.when(i == 0)
        def _():
            carry[...] = jnp.zeros_like(carry)
            acc[...] = jnp.zeros_like(acc)
            dwb_ref[...] = jnp.zeros_like(dwb_ref)
            dwct_ref[...] = jnp.zeros_like(dwct_ref)

        for rows_b, cols_c, cols_s in _super_blocks():
            d_scr[:, cols_s] = _dot16(dy_ref[:, rows_b], wct_ref[rows_b, cols_c], NN)

        last_row = lax.broadcasted_iota(jnp.int32, (SCAN_GROUP, 1), 0) == SCAN_GROUP - 1
        d_ref = d_scr

        def group(j, c):
            cr, ci = c
            rows = pl.ds(pl.multiple_of((SCAN_T // SCAN_GROUP - 1 - j) * SCAN_GROUP, SCAN_GROUP), SCAN_GROUP)
            gr, gi = _group_scan(d_ref[rows, :nc], d_ref[rows, nc:], tab_ref, cr, ci, True)
            g_ref[rows, :nc] = gr
            g_ref[rows, nc:] = gi
            nr = jnp.where(last_row, jnp.broadcast_to(cr, (SCAN_GROUP, nc)), pltpu.roll(gr, SCAN_GROUP - 1, 0))
            ni = jnp.where(last_row, jnp.broadcast_to(ci, (SCAN_GROUP, nc)), pltpu.roll(gi, SCAN_GROUP - 1, 0))
            sr, si = x_ref[rows, :nc], x_ref[rows, nc:]
            acc[:, :nc] += nr * sr + ni * si
            acc[:, nc:] += ni * sr - nr * si
            return gr[0:1, :], gi[0:1, :]

        cr, ci = lax.fori_loop(0, SCAN_T // SCAN_GROUP, group, (carry[:, :nc], carry[:, nc:]), unroll=2)
        carry[:, :nc] = cr
        carry[:, nc:] = ci

        for b in range(SSM_SUPER):
            (rows_b, cols_re, st_re), (_, cols_im, st_im) = _super_blocks()[2 * b:2 * b + 2]
            du_ref[:, rows_b] = (_dot16(g_ref[:, st_re], wb_ref[rows_b, cols_re], NT)
                                 + _dot16(g_ref[:, st_im], wb_ref[rows_b, cols_im], NT))
            for cols_c, cols_s in ((cols_re, st_re), (cols_im, st_im)):
                dwb_ref[rows_b, cols_c] += _dot16(u_ref[:, rows_b], g_ref[:, cols_s], TN)
                dwct_ref[rows_b, cols_c] += _dot16(dy_ref[:, rows_b], x_ref[:, cols_s], TN)

        @pl.when(i == n_blk - 1)
        def _():
            dlam_ref[...] = jnp.sum(acc[...], axis=0, keepdims=True)

    const = lambda shape: pl.BlockSpec(shape, lambda i: (0,) * len(shape))
    rows = lambda width, col_block=0: pl.BlockSpec((SCAN_T, width), lambda i: (n_blk - 1 - i, col_block))
    maps = const((SSM_W, 2 * SB_COLS))
    return pl.pallas_call(
        body, name="s5_bwd", grid=(n_blk,),
        in_specs=[const((4, SCAN_GROUP, 2 * nc)), rows(SSM_W), rows(2 * nc), rows(SSM_W, u_cols[0] // SSM_W), maps, maps],
        out_specs=[rows(SSM_W), const((1, 2 * nc)), maps, maps],
        out_shape=[SDS((S, SSM_W), F32), SDS((1, 2 * nc), F32), SDS((SSM_W, 2 * SB_COLS), F32),
                   SDS((SSM_W, 2 * SB_COLS), F32)],
        scratch_shapes=[pltpu.VMEM((1, 2 * nc), F32), pltpu.VMEM((SCAN_GROUP, 2 * nc), F32),
                        pltpu.VMEM((SCAN_T, 2 * nc), F32), pltpu.VMEM((SCAN_T, 2 * nc), F32)],
        compiler_params=_cparams(("arbitrary",)),
    )(tab, dy, states, u_arr, w_b, w_ct)


def _ssm_prep(a_re, a_im, log_dt, b_re, b_im, c_re, c_im):
    lam = lax.complex(a_re, a_im)
    dt = jnp.exp(log_dt)[:, None]
    lam_bar = jnp.exp(lam * dt)
    b_bar = ((lam_bar - 1.0) / lam)[..., None] * lax.complex(b_re, b_im)
    lam_t = jnp.concatenate([jnp.real(lam_bar).reshape(HALF, LANES), jnp.imag(lam_bar).reshape(HALF, LANES)], axis=0)
    groups_per_super = SSM_GROUPS // SSM_SUPER
    on_diag = ((lax.broadcasted_iota(jnp.int32, (SSM_W, SB_COLS), 0) // SSM_GROUP) % groups_per_super
               == lax.broadcasted_iota(jnp.int32, (SSM_W, SB_COLS), 1) // SSM_STATE)

    repeat = (lax.broadcasted_iota(jnp.int32, (SSM_STATE, SB_COLS), 0)
              == lax.broadcasted_iota(jnp.int32, (SSM_STATE, SB_COLS), 1) % SSM_STATE).astype(F32)

    def compact(m):
        tiled = jnp.dot(m.reshape(SSM_W, SSM_STATE), repeat, precision=lax.Precision.HIGHEST)
        return jnp.where(on_diag, tiled, 0.0)

    w_b = jnp.concatenate([compact(jnp.real(b_bar).transpose(0, 2, 1)),
                           compact(jnp.imag(b_bar).transpose(0, 2, 1))], axis=1)
    w_ct = jnp.concatenate([compact(c_re), -compact(c_im)], axis=1)
    return lam_t, w_b, w_ct


U_SSM_COLS = (4 * ATT_W, SSM_W)


def _row(v):
    return v.reshape(1, -1)


def _even_fwd(x, h, tail, pre, post, w_in, late_w, glu_b, ssm_d, prep, tables):
    lam_t, w_b, w_ct, scan_fwd_tab, scan_bwd_tab = prep
    proj = _mm(h, w_in, "nn", F32, b_blocks=True)
    qkv = _qkv_prep(proj, tables[:3])
    att, lse = _flash_fwd(qkv, tables[3])
    w_out, glu_w = late_w(att)
    states, y = _s5_fwd(scan_fwd_tab, proj, U_SSM_COLS, w_b, w_ct)

    def act1(yv, uv, dv):
        return (_gelu_and_grad(yv + dv * uv)[0],), ()

    (z1,) = _rowwise(act1, [(y, SSM_W, 0), (proj, SSM_W, 8), (ssm_d, SSM_W, 0)], [(SSM_W, F32)], name="ssm_act_fwd")
    lin = _mm(z1, glu_w, "nn", F32)

    def gate(att_v, ga, gs, z1v, linv, bv):
        ssm_out = z1v * _sigmoid(linv + bv)
        return (jnp.concatenate([att_v * _silu_and_grad(ga)[0], ssm_out * _silu_and_grad(gs)[0]], axis=1),), ()

    (merged,) = _rowwise(gate, [(att, ATT_W, 0), (proj, ATT_W, 3), (proj, SSM_W, 9), (z1, SSM_W, 0),
                                (lin, SSM_W, 0), (glu_b, SSM_W, 0)], [(EVEN_OUT, BF16)], name="even_gate_fwd")
    yout = _mm(merged, w_out, "nn", F32)
    saved = (x, h, proj, qkv, att, lse, states, y, z1, lin, merged, yout, w_out, glu_w, scan_bwd_tab)
    return tail(x, yout, post) + (saved,)


def _even_bwd(g, saved, pre, post, w_in, late_w, glu_b, ssm_d, prep, tables, on_w, on_ssm):
    x, h, proj, qkv, att, lse, states, y, z1, lin, merged, yout, w_out, glu_w, scan_bwd_tab = saved
    lam_t, w_b, w_ct = prep[:3]
    dyout, dpost = _post_bwd(g, yout, post)
    dmerged = _mm(dyout, w_out, "nt", F32)
    dw_out = _mm(merged, dyout, "tn", BF16)

    def gate_bwd(dm_a, dm_s, att_v, ga, gs, z1v, linv, bv):
        sa, dsa = _silu_and_grad(ga)
        ss, dss = _silu_and_grad(gs)
        sig = _sigmoid(linv + bv)
        ssm_out = z1v * sig
        dssm = dm_s * ss
        dlin = dssm * z1v * sig * (1.0 - sig)
        return (dm_a * sa, dm_a * att_v * dsa, dm_s * ssm_out * dss, dssm * sig, dlin), (dlin,)

    datt, dg_att, dg_ssm, dz1a, dlin, dglu_b = _rowwise(
        gate_bwd, [(dmerged, ATT_W, 0), (dmerged, SSM_W, 2), (att, ATT_W, 0), (proj, ATT_W, 3), (proj, SSM_W, 9),
                   (z1, SSM_W, 0), (lin, SSM_W, 0), (glu_b, SSM_W, 0)],
        [(ATT_W, F32), (ATT_W, BF16), (SSM_W, BF16), (SSM_W, F32), (SSM_W, BF16)], [SSM_W], name="even_gate_bwd")
    dz1b = _mm(dlin, glu_w, "nt", F32)
    dglu_w = _mm(z1, dlin, "tn", BF16)

    def act1_bwd(da, db, yv, uv, dv):
        dpre = (da + db) * _gelu_and_grad(yv + dv * uv)[1]
        return (dpre, dpre * dv), (dpre * uv,)

    sent_late_w = on_w(dict(w_out=dw_out, glu_w=dglu_w))
    dy, du_direct, dd = _rowwise(act1_bwd, [(dz1a, SSM_W, 0), (dz1b, SSM_W, 0), (y, SSM_W, 0), (proj, SSM_W, 8),
                                            (ssm_d, SSM_W, 0)], [(SSM_W, BF16), (SSM_W, F32)], [SSM_W],
                                 name="ssm_act_bwd", after=(sent_late_w,))
    du_state, dlam_row, dw_b, dw_ct = _s5_bwd(scan_bwd_tab, dy, states, proj, U_SSM_COLS, w_b, w_ct)
    dlam = jnp.concatenate([dlam_row[0, :N_CPLX].reshape(HALF, LANES), dlam_row[0, N_CPLX:].reshape(HALF, LANES)],
                           axis=0)
    sent_ssm = on_ssm((dlam, dw_b, dw_ct))
    dq, dk, dv = _flash_bwd(qkv, att, datt, lse, tables[3], after=() if sent_ssm is None else (sent_ssm,))

    def assemble(dqv, dkv, dvv, dga, dua, dub, dgs, c, s, sw):
        rot = _rotate(jnp.concatenate([dqv, dkv], axis=1), c, s, sw, True)
        return (jnp.concatenate([(rot[:, :ATT_W] * HEAD_DIM ** -0.5).astype(BF16), rot[:, ATT_W:].astype(BF16),
                                 dvv.astype(BF16), dga, (dua + dub).astype(BF16), dgs], axis=1),), ()

    (dproj,) = _rowwise(assemble, [(dq, ATT_W, 0), (dk, ATT_W, 0), (dv, ATT_W, 0), (dg_att, ATT_W, 0),
                                   (du_state, SSM_W, 0), (du_direct, SSM_W, 0), (dg_ssm, SSM_W, 0),
                                   (tables[0], LANES, 0), (tables[1], LANES, 0), (tables[2], LANES, 0)],
                        [(EVEN_IN, BF16)], name="dproj_assemble")
    dw_in = _mm(h, dproj, "tn", BF16, out_blocks=True)
    sent = on_w(dict(w_in=dw_in))
    dh = _mm(dproj, w_in, "nt", F32, b_blocks=True, after=(sent,))
    g_prev, dpre = _pre_bwd(g, dh, x, pre)
    return g_prev, dict(pre=dpre, post=dpost, glu_b=dglu_b, ssm_d=dd)


def _odd_fwd(x, h, tail, pre, post, w_in, pool_w, pool_scale, w_out):
    proj = _mm(h, w_in, "nn", F32, b_blocks=True)
    mixed = _pool(proj, 0, False, BF16)
    ylin = _gmm(mixed, pool_w, "nn", F32)

    def gate(yl, gt, sc):
        return (yl * sc * _silu_and_grad(gt)[0],), ()

    (z,) = _rowwise(gate, [(ylin, POOL_W, 0), (proj, POOL_W, 1), (pool_scale, POOL_W, 0)], [(POOL_W, BF16)],
                    name="odd_gate_fwd")
    yout = _mm(z, w_out, "nn", F32)
    return tail(x, yout, post) + ((x, h, proj, mixed, ylin, z, yout),)


def _odd_bwd(g, saved, pre, post, w_in, pool_w, pool_scale, w_out, on_w):
    x, h, proj, mixed, ylin, z, yout = saved
    dyout, dpost = _post_bwd(g, yout, post)
    dz = _mm(dyout, w_out, "nt", F32)
    dw_out = _mm(z, dyout, "tn", BF16)

    def gate_bwd(dzv, yl, gt, sc):
        sg, dsg = _silu_and_grad(gt)
        tt = dzv * sg
        return (tt * sc, dzv * yl * sc * dsg), (tt * yl,)

    dylin, dproj_gate, dscale = _rowwise(gate_bwd, [(dz, POOL_W, 0), (ylin, POOL_W, 0), (proj, POOL_W, 1),
                                                    (pool_scale, POOL_W, 0)],
                                         [(POOL_W, BF16), (POOL_W, BF16, ODD_IN, 1)], [POOL_W], name="odd_gate_bwd")
    dmixed = _gmm(dylin, pool_w, "nt", F32)
    dpool_w = _gmm(mixed, dylin, "tn", BF16)
    dproj = _pool(dmixed, 0, True, BF16, into=dproj_gate)
    dw_in = _mm(h, dproj, "tn", BF16, out_blocks=True)
    sent = on_w(dict(w_in=dw_in, w_out=dw_out, pool_w=dpool_w))
    dh = _mm(dproj, w_in, "nt", F32, b_blocks=True, after=(sent,))
    g_prev, dpre = _pre_bwd(g, dh, x, pre)
    return g_prev, dict(pre=dpre, post=dpost, pool_scale=dscale)


def _my_index():
    return 4 * lax.axis_index("x") + 2 * lax.axis_index("y") + lax.axis_index("c")


HBM_SPEC = pl.BlockSpec(memory_space=pltpu.HBM)
SEM_SPEC = pl.BlockSpec(memory_space=pltpu.SEMAPHORE)
SPLIT_EFFECT = pltpu.SideEffectType.DATAFLOW_SIDE_EFFECTING


def _device_of(j):
    return (j // 4, (j // 2) % 2, j % 2)


def _split_copy(srcs, lands, send_sems, recv_sems, gather, i, j, dst_slot, recv_slot):
    return pltpu.make_async_remote_copy(
        src_ref=srcs[i] if gather else srcs[i].at[j], dst_ref=lands[i].at[dst_slot],
        send_sem=send_sems.at[i * N_DEV + j], recv_sem=recv_sems.at[i * N_DEV + recv_slot],
        device_id=_device_of(j), device_id_type=MESH_ID)


def _own_copy(srcs, lands, send_sems, gather, i, me):
    return pltpu.make_async_copy(srcs[i] if gather else srcs[i].at[me], lands[i].at[me], send_sems.at[i * N_DEV + me])


def _xchg_start(name, srcs, gather, after=()):
    n = len(srcs)
    n_in = n + len(after)

    def body(*refs):
        src_refs = refs[:n]
        send_sems, recv_sems, token = refs[n_in], refs[n_in + 1], refs[-1]
        land_refs = refs[n_in + 2 + n:n_in + 2 + 2 * n]
        me = _my_index()
        for j in range(N_DEV):
            @pl.when(me != j)
            def _(j=j):
                for i in range(n):
                    _split_copy(src_refs, land_refs, send_sems, recv_sems, gather, i, j, me, me).start()
        for i in range(n):
            _own_copy(src_refs, land_refs, send_sems, gather, i, me).start()
        token[...] = jnp.zeros_like(token)

    land_shapes = [((N_DEV,) + a.shape) if gather else a.shape for a in srcs]
    thru = ([pltpu.HBM(a.shape, a.dtype) for a in srcs] + [pltpu.HBM(s, a.dtype) for s, a in zip(land_shapes, srcs)])
    res = pl.pallas_call(
        body, name=name,
        out_shape=(pltpu.SemaphoreType.DMA((n * N_DEV,)), pltpu.SemaphoreType.DMA((n * N_DEV,)), *thru,
                   SDS((8, LANES), F32)),
        in_specs=[HBM_SPEC] * n + [pl.BlockSpec(memory_space=pl.ANY)] * len(after),
        out_specs=(SEM_SPEC, SEM_SPEC, *([HBM_SPEC] * (2 * n)), pl.BlockSpec(memory_space=pltpu.VMEM)),
        input_output_aliases={i: 2 + i for i in range(n)},
        compiler_params=pltpu.CompilerParams(has_side_effects=SPLIT_EFFECT),
    )(*[pltpu.with_memory_space_constraint(a, pltpu.HBM) for a in srcs], *after)
    return res[0], res[1], list(res[2:2 + n]), list(res[2 + n:2 + 2 * n]), res[-1]


def _xchg_wait(name, started, gather, after):
    send_sems, recv_sems, srcs, lands, _ = started
    n = len(srcs)

    def body(*refs):
        src_refs, land_refs = refs[:n], refs[n:2 * n]
        send_r, recv_r = refs[2 * n], refs[2 * n + 1]
        me = _my_index()
        for j in range(N_DEV):
            @pl.when(me != j)
            def _(j=j):
                for i in range(n):
                    _split_copy(src_refs, land_refs, send_r, recv_r, gather, i, j, me, me).wait_send()
                    _split_copy(src_refs, land_refs, send_r, recv_r, gather, i, j, j, j).wait_recv()
        for i in range(n):
            _own_copy(src_refs, land_refs, send_r, gather, i, me).wait()

    thru = [pltpu.HBM(a.shape, a.dtype) for a in list(srcs) + list(lands)]
    res = pl.pallas_call(
        body, name=name, out_shape=tuple(thru),
        in_specs=[HBM_SPEC] * (2 * n) + [SEM_SPEC, SEM_SPEC] + [pl.BlockSpec(memory_space=pl.ANY)] * len(after),
        out_specs=tuple([HBM_SPEC] * (2 * n)),
        input_output_aliases={i: i for i in range(2 * n)},
        compiler_params=pltpu.CompilerParams(has_side_effects=SPLIT_EFFECT),
    )(*srcs, *lands, send_sems, recv_sems, *after)
    return list(res[n:])


def _adam_layer(w, slots, m, v, layer, name, into=None):
    n_l, r, c = w.shape
    ns = slots.shape[0]
    tr = r
    while tr * c * 4 > (1 << 20) and tr % 16 == 0:
        tr //= 2
    assert r % tr == 0

    def body(w_ref, g_ref, m_ref, v_ref, *rest):
        go_ref, d_ref, mo_ref, vo_ref = rest[-4:]
        g = g_ref[0].astype(F32)
        for s in range(1, ns):
            g = g + g_ref[s].astype(F32)
        mn = ADAM_B1 * m_ref[...] + (1.0 - ADAM_B1) * g
        vn = ADAM_B2 * v_ref[...] + (1.0 - ADAM_B2) * (g * g)
        m_hat = mn / (1.0 - ADAM_B1 ** ADAM_STEP)
        v_hat = vn / (1.0 - ADAM_B2 ** ADAM_STEP)
        go_ref[...] = g
        d_ref[...] = -ADAM_LR * (m_hat / (jnp.sqrt(v_hat) + ADAM_EPS) + ADAM_WD * w_ref[...])
        mo_ref[...] = mn
        vo_ref[...] = vn

    blk = pl.BlockSpec((None, tr, c), lambda i: (layer, i, 0))
    earlier = () if into is None else tuple(into)
    return pl.pallas_call(
        body, name=name, grid=(r // tr,),
        in_specs=[blk, pl.BlockSpec((ns, tr, c), lambda i: (0, i, 0)), blk, blk]
        + [pl.BlockSpec(memory_space=pl.ANY)] * len(earlier),
        out_specs=[blk] * 4, out_shape=[SDS((n_l, r, c), F32)] * 4,
        input_output_aliases={4 + q: q for q in range(len(earlier))},
        compiler_params=_cparams(("arbitrary",)),
    )(*_in_hbm((w, slots, m, v)), *earlier)


def _adam(w, gslots, m, v, name):
    r, c = w.shape
    ns = gslots.shape[0]
    tr = r
    while tr * c * 4 > (1 << 20) and tr % 16 == 0:
        tr //= 2
    assert r % tr == 0

    def body(w_ref, g_ref, m_ref, v_ref, go_ref, d_ref, mo_ref, vo_ref):
        g = g_ref[0].astype(F32)
        for s in range(1, ns):
            g = g + g_ref[s].astype(F32)
        wv = w_ref[...]
        mn = ADAM_B1 * m_ref[...] + (1.0 - ADAM_B1) * g
        vn = ADAM_B2 * v_ref[...] + (1.0 - ADAM_B2) * (g * g)
        m_hat = mn / (1.0 - ADAM_B1 ** ADAM_STEP)
        v_hat = vn / (1.0 - ADAM_B2 ** ADAM_STEP)
        go_ref[...] = g
        d_ref[...] = -ADAM_LR * (m_hat / (jnp.sqrt(v_hat) + ADAM_EPS) + ADAM_WD * wv)
        mo_ref[...] = mn
        vo_ref[...] = vn

    blk = pl.BlockSpec((tr, c), lambda i: (i, 0))
    return pl.pallas_call(
        body, name=name, grid=(r // tr,),
        in_specs=[blk, pl.BlockSpec((ns, tr, c), lambda i: (0, i, 0)), blk, blk],
        out_specs=[blk] * 4, out_shape=[SDS((r, c), F32)] * 4,
        compiler_params=_cparams(("parallel",)),
    )(w, gslots, m, v)


def _sum_slots(slots, name):
    ns, r, c = slots.shape

    def body(g_ref, o_ref):
        g = g_ref[0]
        for s in range(1, ns):
            g = g + g_ref[s]
        o_ref[...] = g

    return pl.pallas_call(
        body, name=name, grid=(1,),
        in_specs=[pl.BlockSpec((ns, r, c), lambda i: (0, 0, 0))], out_specs=pl.BlockSpec((r, c), lambda i: (0, 0)),
        out_shape=SDS((r, c), F32), compiler_params=_cparams(("arbitrary",)),
    )(slots)


def _adam_params(params, name):
    n = len(params)

    def body(*refs):
        ins, outs = refs[:5 * n], refs[5 * n:]
        for p in range(n):
            w_ref, m_ref, v_ref, g_first, g_rest = ins[5 * p:5 * p + 5]
            go_ref, d_ref, mo_ref, vo_ref = outs[4 * p:4 * p + 4]
            for part, g_ref in ((slice(0, 1), g_first), (slice(1, w_ref.shape[0]), g_rest)):
                g = g_ref[...]
                mn = ADAM_B1 * m_ref[part] + (1.0 - ADAM_B1) * g
                vn = ADAM_B2 * v_ref[part] + (1.0 - ADAM_B2) * (g * g)
                m_hat = mn / (1.0 - ADAM_B1 ** ADAM_STEP)
                v_hat = vn / (1.0 - ADAM_B2 ** ADAM_STEP)
                go_ref[part] = g
                d_ref[part] = -ADAM_LR * (m_hat / (jnp.sqrt(v_hat) + ADAM_EPS) + ADAM_WD * w_ref[part])
                mo_ref[part] = mn
                vo_ref[part] = vn

    def whole(a):
        return pl.BlockSpec(a.shape, lambda i, nd=a.ndim: (0,) * nd)

    flat = _in_hbm([a for prm in params for a in prm])
    outs = pl.pallas_call(
        body, name=name, grid=(1,),
        in_specs=[whole(a) for a in flat],
        out_specs=[whole(prm[0]) for prm in params for _ in range(4)],
        out_shape=[SDS(prm[0].shape, F32) for prm in params for _ in range(4)],
        compiler_params=_cparams(("arbitrary",)),
    )(*flat)
    return [outs[4 * p:4 * p + 4] for p in range(n)]


SMALL_NAMES = ("pre_norm", "post_norm", "ssm_a_re", "ssm_a_im", "ssm_log_dt", "ssm_b_re", "ssm_b_im", "ssm_c_re",
               "ssm_c_im", "ssm_d", "ssm_glu_b")
SSM_NAMES = ("ssm_a_re", "ssm_a_im", "ssm_log_dt", "ssm_b_re", "ssm_b_im", "ssm_c_re", "ssm_c_im")
WEIGHT_ORDER = ("pre_norm", "post_norm", "even_w_in", "even_w_out", "ssm_a_re", "ssm_a_im", "ssm_log_dt", "ssm_b_re",
                "ssm_b_im", "ssm_c_re", "ssm_c_im", "ssm_d", "ssm_glu_w", "ssm_glu_b", "odd_w_in", "pool_w",
                "pool_scale", "odd_w_out")
PACK_ROWS_ALIGN = 8


def _pack(parts):
    flat = jnp.concatenate([p.reshape(-1).astype(F32) for p in parts])
    rows = -(-flat.shape[0] // (LANES * PACK_ROWS_ALIGN)) * PACK_ROWS_ALIGN
    return jnp.pad(flat, (0, rows * LANES - flat.shape[0])).reshape(rows, LANES)


def _unpack(packed, shapes):
    flat = packed.reshape(-1)
    out, off = [], 0
    for shp in shapes:
        size = math.prod(shp)
        out.append(flat[off:off + size].reshape(shp))
        off += size
    return out


EVEN_SHARDED = ("w_in", "w_out", "glu_w")
ODD_SHARDED = ("w_in", "pool_w", "w_out")
FAMILY = {(0, "w_in"): "even_w_in", (0, "w_out"): "even_w_out", (0, "glu_w"): "ssm_glu_w",
          (1, "w_in"): "odd_w_in", (1, "pool_w"): "pool_w", (1, "w_out"): "odd_w_out"}


def _sharded_keys(layer):
    return EVEN_SHARDED if layer % 2 == 0 else ODD_SHARDED


def _local_step(x, tgt, small, get_weights, on_w, on_ssm, on_grads, zero=0.0):
    tables = _rope_tables(zero) + (_attention_bias(zero),)
    preps, prep_vjps = [], []
    for i in range(2):
        out, vjp = jax.vjp(_ssm_prep, small["ssm_a_re"][i] + zero, small["ssm_a_im"][i], small["ssm_log_dt"][i],
                           small["ssm_b_re"][i], small["ssm_b_im"][i], small["ssm_c_re"][i], small["ssm_c_im"][i])
        preps.append(tuple(out) + _scan_tables(out[0]))
        prep_vjps.append(vjp)

    def layer_args(layer, wts):
        i = layer // 2
        pre, post = _row(small["pre_norm"][layer]) + wts.get("token", 0.0), _row(small["post_norm"][layer])
        if layer % 2 == 0:
            return (pre, post, wts["w_in"], wts["late"], _row(small["ssm_glu_b"][i]), _row(small["ssm_d"][i]),
                    preps[i], tables)
        return (pre, post, wts["w_in"], wts["pool_w"], _row(wts["pool_scale"]), wts["w_out"])

    saved, args = [], []
    cur = x
    for layer in range(4):
        after = (cur,) if layer else (cur, tables[0], tables[3]) + preps[0][1:] + preps[1][1:]
        args.append(layer_args(layer, get_weights(layer, after)))
        if layer == 0:
            h = _norm_fwd(cur, args[0][0])
        if layer < 3:
            def tail(xv, yv, post, next_gain=_row(small["pre_norm"][layer + 1])):
                return tuple(_post_fwd(xv, yv, post, next_gain))
        else:
            def tail(xv, yv, post):
                return tuple(_post_fwd_loss(xv, yv, post, tgt))
        cur, h, sv = (_even_fwd if layer % 2 == 0 else _odd_fwd)(cur, h, tail, *args[layer])
        saved.append(sv)
    g, sq = cur, h
    loss = 0.5 * jnp.sum(sq) / D

    lg = [None] * 4
    token = jnp.zeros((), F32)
    for layer in reversed(range(4)):
        largs = list(args[layer])
        largs[1] = largs[1] + token
        hooks = dict(on_w=functools.partial(on_w, layer))
        ssm_grads = []
        if layer % 2 == 0:
            def ssm_hook(cotangents, layer=layer):
                ssm_grads.append(prep_vjps[layer // 2](cotangents))
                return on_ssm(layer, ssm_grads[0])

            hooks["on_ssm"] = ssm_hook
        g, lg[layer] = (_even_bwd if layer % 2 == 0 else _odd_bwd)(g, saved[layer], *largs, **hooks)
        if ssm_grads:
            lg[layer]["ssm"] = ssm_grads[0]
        token = on_grads(layer, lg[layer])
    return loss, g, token


def _to_slots(key, gfull):
    if key == "w_in":
        return gfull
    if key in ("w_out", "glu_w"):
        rr, nn = gfull.shape
        return gfull.reshape(N_DEV, rr // N_DEV, nn)
    assert key == "pool_w"
    gg, rr, nn = gfull.shape
    return gfull.reshape(gg, N_DEV, rr // N_DEV, nn).transpose(1, 0, 2, 3)


def _from_gathered(key, gat):
    if key == "w_in":
        return gat
    if key in ("w_out", "glu_w"):
        _, rr, nn = gat.shape
        return gat.reshape(N_DEV * rr, nn)
    assert key == "pool_w"
    _, gg, rr, nn = gat.shape
    return gat.transpose(1, 0, 2, 3).reshape(gg, N_DEV * rr, nn)


def kernel(x, pre_norm, post_norm, even_w_in, even_w_out, ssm_a_re, ssm_a_im, ssm_log_dt, ssm_b_re, ssm_b_im, ssm_c_re, ssm_c_im, ssm_d, ssm_glu_w, ssm_glu_b, odd_w_in, pool_w, pool_scale, odd_w_out, loss_target, m_pre_norm, m_post_norm, m_even_w_in, m_even_w_out, m_ssm_a_re, m_ssm_a_im, m_ssm_log_dt, m_ssm_b_re, m_ssm_b_im, m_ssm_c_re, m_ssm_c_im, m_ssm_d, m_ssm_glu_w, m_ssm_glu_b, m_odd_w_in, m_pool_w, m_pool_scale, m_odd_w_out, v_pre_norm, v_post_norm, v_even_w_in, v_even_w_out, v_ssm_a_re, v_ssm_a_im, v_ssm_log_dt, v_ssm_b_re, v_ssm_b_im, v_ssm_c_re, v_ssm_c_im, v_ssm_d, v_ssm_glu_w, v_ssm_glu_b, v_odd_w_in, v_pool_w, v_pool_scale, v_odd_w_out):
    w = dict(pre_norm=pre_norm, post_norm=post_norm, even_w_in=even_w_in, even_w_out=even_w_out, ssm_a_re=ssm_a_re,
             ssm_a_im=ssm_a_im, ssm_log_dt=ssm_log_dt, ssm_b_re=ssm_b_re, ssm_b_im=ssm_b_im, ssm_c_re=ssm_c_re,
             ssm_c_im=ssm_c_im, ssm_d=ssm_d, ssm_glu_w=ssm_glu_w, ssm_glu_b=ssm_glu_b, odd_w_in=odd_w_in,
             pool_w=pool_w, pool_scale=pool_scale, odd_w_out=odd_w_out)
    mom = dict(pre_norm=m_pre_norm, post_norm=m_post_norm, even_w_in=m_even_w_in, even_w_out=m_even_w_out,
               ssm_a_re=m_ssm_a_re, ssm_a_im=m_ssm_a_im, ssm_log_dt=m_ssm_log_dt, ssm_b_re=m_ssm_b_re,
               ssm_b_im=m_ssm_b_im, ssm_c_re=m_ssm_c_re, ssm_c_im=m_ssm_c_im, ssm_d=m_ssm_d, ssm_glu_w=m_ssm_glu_w,
               ssm_glu_b=m_ssm_glu_b, odd_w_in=m_odd_w_in, pool_w=m_pool_w, pool_scale=m_pool_scale,
               odd_w_out=m_odd_w_out)
    var = dict(pre_norm=v_pre_norm, post_norm=v_post_norm, even_w_in=v_even_w_in, even_w_out=v_even_w_out,
               ssm_a_re=v_ssm_a_re, ssm_a_im=v_ssm_a_im, ssm_log_dt=v_ssm_log_dt, ssm_b_re=v_ssm_b_re,
               ssm_b_im=v_ssm_b_im, ssm_c_re=v_ssm_c_re, ssm_c_im=v_ssm_c_im, ssm_d=v_ssm_d, ssm_glu_w=v_ssm_glu_w,
               ssm_glu_b=v_ssm_glu_b, odd_w_in=v_odd_w_in, pool_w=v_pool_w, pool_scale=v_pool_scale,
               odd_w_out=v_odd_w_out)
    me = _my_index()
    scale_cols = pool_scale.shape[1]

    def shards_of(layer, keys):
        i = layer // 2
        shards = [w[FAMILY[(layer % 2, k)]][i].astype(BF16) for k in keys]
        if layer % 2 == 1:
            shards.append(jnp.pad(pool_scale[i][None], ((0, PACK_ROWS_ALIGN - 1), (0, 0))))
        return shards

    def start_gather(tag, after=()):
        return _xchg_start(f"gather_start_{tag}", shards[tag], True, after)

    shards = {0: shards_of(0, EVEN_SHARDED[:1]), "0_late": shards_of(0, EVEN_SHARDED[1:])}
    shards.update({layer: shards_of(layer, _sharded_keys(layer)) for layer in (1, 2, 3)})
    gather_started = {0: start_gather(0)}
    small = {nm: w[nm] for nm in SMALL_NAMES}
    packed_names = ("pre_norm", "post_norm") + SSM_NAMES + ("ssm_d", "ssm_glu_b")
    tails = {nm: (SSM_GROUPS, SSM_STATE * SSM_GROUP) if nm in ("ssm_b_re", "ssm_b_im") else w[nm].shape[1:]
             for nm in packed_names}
    dense = lambda nm, a: a.reshape((a.shape[0],) + tails[nm])
    small_operands = {nm: tuple(dense(nm, tree[nm]) for tree in (w, mom, var)) for nm in packed_names}
    early_work = [a for nm in ("ssm_b_re", "ssm_b_im") for a in small_operands[nm]]

    def get_weights(layer, after):
        keys = EVEN_SHARDED[:1] if layer == 0 else _sharded_keys(layer)
        if layer == 0:
            after = tuple(after) + tuple(early_work)
        lands = _xchg_wait(f"gather_wait_{layer}", gather_started[layer], True, after)
        wts = {k: _from_gathered(k, gat) for k, gat in zip(keys, lands)}
        if layer % 2 == 1:
            wts["pool_scale"] = lands[-1][:, 0, :].reshape(N_DEV * scale_cols)
        if layer == 0:
            prev = gather_started["0_late"] = start_gather("0_late", after=(lands[0],))
            for later in (1, 2, 3):
                prev = gather_started[later] = start_gather(later, after=(prev[4],))
            wts["token"] = prev[4][0, 0]

            def late(after_late):
                late_lands = _xchg_wait("gather_wait_0_late", gather_started["0_late"], True, (after_late,))
                return tuple(_from_gathered(k, gat) for k, gat in zip(EVEN_SHARDED[1:], late_lands))

            wts["late"] = late
        elif layer == 2:
            wts["late"] = lambda after_late: (wts["w_out"], wts["glu_w"])
        return wts

    scatter_started = []

    def on_w(layer, gw):
        keys = tuple(k for k in _sharded_keys(layer) if k in gw)
        started = _xchg_start(f"scatter_start_{layer}_{keys[0]}", [_to_slots(k, gw[k]) for k in keys], False)
        scatter_started.append((layer, keys, started))
        return started[4]

    def wait_scatters(layers, after):
        for layer, keys, started in scatter_started:
            if layer in layers:
                lands = _xchg_wait(f"scatter_wait_{layer}_{keys[0]}", started, False, after)
                for k, land in zip(keys, lands):
                    recv[(layer, k)] = land

    layer_grads = {}
    early_started, mid_started = [], []

    def on_ssm(layer, ssm_grads):
        if layer != 0:
            return None
        mid_started.append(_xchg_start("mid_start", [_pack(list(ssm_grads))], True))
        return mid_started[0][4]

    def on_grads(layer, lg):
        layer_grads[layer] = lg
        zero = jnp.zeros((), F32)
        if layer == 1:
            lgs = layer_grads
            early = ([jnp.concatenate([lgs[l][k] for l in (1, 2, 3)], axis=0) for k in ("pre", "post")]
                     + list(lgs[2]["ssm"]) + [lgs[2]["ssm_d"], lgs[2]["glu_b"],
                                              jnp.concatenate([lgs[1]["pool_scale"], lgs[3]["pool_scale"]], axis=0)])
            early_started.append(_xchg_start("small_start", [_pack(early)], True))
            zero = zero + early_started[0][4][0, 0]
        return zero

    loss_local, grad_x, token = _local_step(x[0], loss_target[0], small, get_weights, on_w, on_ssm, on_grads,
                                            zero=gather_started[0][4][0, 0])

    lg0 = layer_grads[0]
    late_started = _xchg_start("late_start", [_pack([lg0["pre"], lg0["post"], lg0["ssm_d"], lg0["glu_b"],
                                                     loss_local.reshape(1)]) + token], True)

    def adam_family(parity, k, which, into=None):
        nm = FAMILY[(parity, k)]
        cols = w[nm].shape[-1]
        return _adam_layer(w[nm].reshape(2, -1, cols), recv[(parity + 2 * which, k)].reshape(N_DEV, -1, cols),
                           mom[nm].reshape(2, -1, cols), var[nm].reshape(2, -1, cols), which,
                           f"adam_{nm}_{which}", into)

    recv, res = {}, {}
    wait_scatters((3, 2, 1), (late_started[4],))
    for k in ODD_SHARDED:
        res[FAMILY[(1, k)]] = adam_family(1, k, 1, adam_family(1, k, 0))
    half_done = {k: adam_family(0, k, 1) for k in EVEN_SHARDED}
    odd_done = tuple(half_done[k][0] for k in EVEN_SHARDED)

    (early_slots,) = _xchg_wait("small_wait", early_started[0], True, odd_done)
    (mid_slots,) = _xchg_wait("mid_wait", mid_started[0], True, odd_done)
    early_shapes = [(w[nm].shape[0] - 1,) + tails[nm] for nm in packed_names] + [(2, N_DEV * scale_cols)]
    g_early = _unpack(_sum_slots(early_slots, "sum_small_early"), early_shapes)
    g_mid = _unpack(_sum_slots(mid_slots, "sum_small_mid"), [(1,) + tails[nm] for nm in SSM_NAMES])

    (late_slots,) = _xchg_wait("late_wait", late_started, True, (g_early[0], g_mid[0]))
    wait_scatters((0,), (late_slots,))
    for k in EVEN_SHARDED:
        res[FAMILY[(0, k)]] = adam_family(0, k, 0, half_done[k])
    for nm in FAMILY.values():
        res[nm] = [o.reshape(w[nm].shape) for o in res[nm]]

    late_names = ("pre_norm", "post_norm", "ssm_d", "ssm_glu_b")
    g_late = _unpack(_sum_slots(late_slots, "sum_small_late"), [(1,) + tails[nm] for nm in late_names] + [(1,)])
    g_first = dict(zip(late_names, g_late))
    g_first.update(zip(SSM_NAMES, g_mid))
    outs = _adam_params([small_operands[nm] + (g_first[nm], g_early[j]) for j, nm in enumerate(packed_names)],
                        "adam_small")
    for nm, four in zip(packed_names, outs):
        res[nm] = [o.reshape(w[nm].shape) for o in four]
    loss = g_late[-1].reshape(())
    g_scale = lax.dynamic_slice_in_dim(g_early[-1], me * scale_cols, scale_cols, axis=1)
    pad = ((0, PACK_ROWS_ALIGN - 2), (0, 0))
    outs = _adam(jnp.pad(pool_scale, pad), jnp.pad(g_scale, pad)[None], jnp.pad(m_pool_scale, pad),
                 jnp.pad(v_pool_scale, pad), name="adam_pool_scale")
    res["pool_scale"] = [o[:2] for o in outs]

    out = [loss, grad_x[None]]
    for kind in range(4):
        out += [res[nm][kind] for nm in WEIGHT_ORDER]
    return tuple(out)
```

```python
import functools
import math

import jax
import jax.numpy as jnp
from jax import lax
from jax.experimental import pallas as pl
from jax.experimental.pallas import tpu as pltpu

F32 = jnp.float32
BF16 = jnp.bfloat16
SDS = jax.ShapeDtypeStruct

N_DEV = 8
S = 2048
D = 1024
HEAD_DIM = 64
ROT_DIM = 16
ROPE_THETA = 500000.0
ATT_W = 1024
SSM_W = 512
SSM_GROUPS = 32
SSM_GROUP = 16
SSM_STATE = 64
N_CPLX = SSM_GROUPS * SSM_STATE
POOL_W = 2048
POOL_GROUP = 512
EVEN_IN = 5120
EVEN_OUT = 1536
ODD_IN = 4096
RMS_EPS = 1e-6
LANES = 128
VMEM_LIMIT = 48 * 1024 * 1024

ADAM_LR = 0.001
ADAM_B1 = 0.9
ADAM_B2 = 0.999
ADAM_EPS = 1e-08
ADAM_WD = 0.01
ADAM_STEP = 10

MESH_ID = pl.DeviceIdType.MESH
NN = (((1,), (0,)), ((), ()))
NT = (((1,), (1,)), ((), ()))
TN = (((0,), (0,)), ((), ()))
_DN = {"nn": NN, "nt": NT, "tn": TN}


def _cparams(sem):
    return pltpu.CompilerParams(dimension_semantics=sem, vmem_limit_bytes=VMEM_LIMIT)


def _in_hbm(arrs):
    return [pltpu.with_memory_space_constraint(a, pltpu.HBM) for a in arrs]


MM_TILES = (1024, 768, 512)


def _tile(dim):
    return next((t for t in MM_TILES if dim % t == 0), dim)


BLOCK_PAIR = 2
MAX_WHOLE_K = 2048


def _mm(a, b, mode, out_dtype, b_blocks=False, out_blocks=False, after=()):
    if b_blocks:
        nblk, rows, cb = b.shape
        b2_shape = (rows, nblk * cb)
    else:
        b2_shape = b.shape
    if mode == "nn":
        (m, k), n = a.shape, b2_shape[1]
    elif mode == "nt":
        (m, k), n = a.shape, b2_shape[0]
    else:
        (k, m), n = a.shape, b2_shape[1]
    tm, tn, tk = _tile(m), _tile(n), _tile(k)
    if k <= MAX_WHOLE_K:
        tk = k
    if b_blocks and mode == "nn":
        tn = BLOCK_PAIR * cb
        if tn <= MM_TILES[0]:
            tm = m
    if b_blocks and mode == "nt":
        tk = BLOCK_PAIR * cb
    if out_blocks:
        cb = n // N_DEV
        tn = BLOCK_PAIR * cb
        tk = k
    nk = k // tk

    def body(a_ref, b_ref, *rest):
        o_ref, acc_ref = rest[-2:]
        kk = pl.program_id(2)
        bv = jnp.concatenate([b_ref[p] for p in range(BLOCK_PAIR)], axis=1) if b_blocks else b_ref[...]
        part = lax.dot_general(a_ref[...].astype(BF16), bv.astype(BF16), _DN[mode], preferred_element_type=F32)

        def write(res):
            if out_blocks:
                for p in range(BLOCK_PAIR):
                    o_ref[p] = res[:, p * cb:(p + 1) * cb].astype(o_ref.dtype)
            else:
                o_ref[...] = res.astype(o_ref.dtype)

        if nk == 1:
            write(part)
            return

        @pl.when(kk == 0)
        def _():
            acc_ref[...] = part

        @pl.when((kk > 0) & (kk < nk - 1))
        def _():
            acc_ref[...] += part

        @pl.when(kk == nk - 1)
        def _():
            write(acc_ref[...] + part)

    if mode == "nn":
        a_spec = pl.BlockSpec((tm, tk), lambda i, j, kk: (i, kk))
        b_spec = pl.BlockSpec((tk, tn), lambda i, j, kk: (kk, j))
    elif mode == "nt":
        a_spec = pl.BlockSpec((tm, tk), lambda i, j, kk: (i, kk))
        b_spec = pl.BlockSpec((tn, tk), lambda i, j, kk: (j, kk))
    else:
        a_spec = pl.BlockSpec((tk, tm), lambda i, j, kk: (kk, i))
        b_spec = pl.BlockSpec((tk, tn), lambda i, j, kk: (kk, j))
    if b_blocks and mode == "nn":
        b_spec = pl.BlockSpec((BLOCK_PAIR, tk, cb), lambda i, j, kk: (j, kk, 0))
    if b_blocks and mode == "nt":
        b_spec = pl.BlockSpec((BLOCK_PAIR, tn, cb), lambda i, j, kk: (kk, j, 0))
    out_spec = pl.BlockSpec((tm, tn), lambda i, j, kk: (i, j))
    out_shape = SDS((m, n), out_dtype)
    if out_blocks:
        out_spec = pl.BlockSpec((BLOCK_PAIR, tm, cb), lambda i, j, kk: (j, i, 0))
        out_shape = SDS((N_DEV, m, cb), out_dtype)
    return pl.pallas_call(
        body, name=f"mm_{mode}_{m}x{k}x{n}",
        grid=(m // tm, n // tn, nk),
        in_specs=[a_spec, b_spec] + [pl.BlockSpec(memory_space=pl.ANY)] * len(after),
        out_specs=out_spec,
        out_shape=out_shape,
        scratch_shapes=[pltpu.VMEM((tm, tn) if nk > 1 else (8, LANES), F32)],
        compiler_params=_cparams(("parallel", "parallel", "arbitrary")),
    )(a, b, *after)


def _gmm(a, b, mode, out_dtype, tm=S):
    ng, gw = POOL_W // POOL_GROUP, POOL_GROUP
    ns = S // tm
    if mode in ("nn", "nt"):
        def body(a_ref, b_ref, o_ref):
            o_ref[...] = lax.dot_general(a_ref[...].astype(BF16), b_ref[...].astype(BF16), _DN[mode],
                                         preferred_element_type=F32).astype(o_ref.dtype)

        return pl.pallas_call(
            body, name=f"gmm_{mode}", grid=(ng, ns),
            in_specs=[pl.BlockSpec((tm, gw), lambda g, i: (i, g)),
                      pl.BlockSpec((None, gw, gw), lambda g, i: (g, 0, 0))],
            out_specs=pl.BlockSpec((tm, gw), lambda g, i: (i, g)),
            out_shape=SDS((S, POOL_W), out_dtype),
            compiler_params=_cparams(("parallel", "parallel")),
        )(a, b)

    def body_tn(a_ref, b_ref, o_ref, acc_ref):
        i = pl.program_id(1)

        @pl.when(i == 0)
        def _():
            acc_ref[...] = jnp.zeros_like(acc_ref)

        acc_ref[...] += lax.dot_general(a_ref[...].astype(BF16), b_ref[...].astype(BF16), TN,
                                        preferred_element_type=F32)

        @pl.when(i == ns - 1)
        def _():
            o_ref[...] = acc_ref[...].astype(o_ref.dtype)

    return pl.pallas_call(
        body_tn, name="gmm_tn", grid=(ng, ns),
        in_specs=[pl.BlockSpec((tm, gw), lambda g, i: (i, g)),
                  pl.BlockSpec((tm, gw), lambda g, i: (i, g))],
        out_specs=pl.BlockSpec((None, gw, gw), lambda g, i: (g, 0, 0)),
        out_shape=SDS((ng, gw, gw), out_dtype),
        scratch_shapes=[pltpu.VMEM((gw, gw), F32)],
        compiler_params=_cparams(("parallel", "arbitrary")),
    )(a, b)


def _rowwise(fn, inputs, out_defs, acc_defs=(), tm=512, name=None, after=()):
    n_in, n_out, n_acc = len(inputs), len(out_defs), len(acc_defs)
    n_after = len(after)
    in_specs, args = [], []
    for arr, width, cb in inputs:
        if arr.shape[0] != S:
            in_specs.append(pl.BlockSpec((arr.shape[0], width), lambda i, cb=cb: (0, cb)))
        else:
            in_specs.append(pl.BlockSpec((tm, width), lambda i, cb=cb: (i, cb)))
        args.append(arr)
    out_defs = [d if len(d) == 4 else (d[0], d[1], d[0], 0) for d in out_defs]
    out_shape = [SDS((S, ww), dt) for _, dt, ww, _ in out_defs] + [SDS((1, w), F32) for w in acc_defs]
    out_specs = ([pl.BlockSpec((tm, w), lambda i, cb=cb: (i, cb)) for w, _, _, cb in out_defs]
                 + [pl.BlockSpec((1, w), lambda i: (0, 0)) for w in acc_defs])

    def kern(*refs):
        vals = [r[...] for r in refs[:n_in]]
        outs, accs = fn(*vals)
        out_refs = refs[n_in + n_after:]
        for r, v in zip(out_refs[:n_out], outs):
            r[...] = v.astype(r.dtype)
        if n_acc:
            acc_refs = out_refs[n_out:]

            @pl.when(pl.program_id(0) == 0)
            def _():
                for r in acc_refs:
                    r[...] = jnp.zeros_like(r)

            for r, v in zip(acc_refs, accs):
                r[...] += jnp.sum(v, axis=0, keepdims=True)

    res = pl.pallas_call(
        kern, name=name, grid=(S // tm,), in_specs=in_specs + [pl.BlockSpec(memory_space=pl.ANY)] * n_after,
        out_specs=out_specs, out_shape=out_shape, compiler_params=_cparams(("arbitrary",)),
    )(*args, *after)
    return res


def _sigmoid(x):
    return 1.0 / (1.0 + jnp.exp(-x))


def _silu_and_grad(x):
    s = _sigmoid(x)
    return x * s, s * (1.0 + x * (1.0 - s))


_GELU_K = math.sqrt(2.0 / math.pi)
_GELU_C = 0.044715


def _gelu_and_grad(x):
    t = jnp.tanh(_GELU_K * (x + _GELU_C * (x * x * x)))
    cdf = 0.5 * (1.0 + t)
    grad = cdf + 0.5 * x * (1.0 - t * t) * (_GELU_K * (1.0 + 3.0 * _GELU_C * x * x))
    return x * cdf, grad


def _rms(xv, gain):
    r = lax.rsqrt(jnp.mean(xv * xv, axis=-1, keepdims=True) + RMS_EPS)
    return xv * r * gain


def _rms_bwd(dout, xv, gain):
    r = lax.rsqrt(jnp.mean(xv * xv, axis=-1, keepdims=True) + RMS_EPS)
    xhat = xv * r
    dxhat = dout * gain
    dx = r * (dxhat - xhat * jnp.mean(dxhat * xhat, axis=-1, keepdims=True))
    return dx, dout * xhat


def _norm_fwd(x, gain):
    (h,) = _rowwise(lambda xv, g: ((_rms(xv, g),), ()), [(x, D, 0), (gain, D, 0)], [(D, BF16)], name="norm_fwd")
    return h


def _post_fwd(x, y, gain, next_gain):
    def fn(xv, yv, g, gn):
        out = xv + _rms(yv, g)
        return (out, _rms(out, gn)), ()

    return _rowwise(fn, [(x, D, 0), (y, D, 0), (gain, D, 0), (next_gain, D, 0)], [(D, F32), (D, BF16)],
                    name="post_fwd")


def _post_fwd_loss(x, y, gain, tgt):
    def fn(xv, yv, g, tv):
        e = xv + _rms(yv, g) - tv
        return (e * (1.0 / D),), (e * e,)

    return _rowwise(fn, [(x, D, 0), (y, D, 0), (gain, D, 0), (tgt, D, 0)], [(D, F32)], [D], name="post_fwd_loss")


def _post_bwd(g, y, gain):
    def fn(gv, yv, gn):
        dx, dg = _rms_bwd(gv, yv, gn)
        return (dx,), (dg,)

    return _rowwise(fn, [(g, D, 0), (y, D, 0), (gain, D, 0)], [(D, BF16)], [D], name="post_bwd")


def _pre_bwd(g, dh, x, gain):
    def fn(gv, dhv, xv, gn):
        dx, dg = _rms_bwd(dhv, xv, gn)
        return (gv + dx,), (dg,)

    return _rowwise(fn, [(g, D, 0), (dh, D, 0), (x, D, 0), (gain, D, 0)], [(D, F32)], [D], name="pre_bwd")


def _pool(u_arr, col_block, transpose, out_dtype, into=None, tc=256):
    n_t = POOL_W // tc
    per_group = POOL_GROUP // tc

    def body(u_ref, *rest):
        o_ref = rest[-1]
        grp = pl.program_id(0) // per_group
        t = lax.broadcasted_iota(jnp.int32, (S, 1), 0)
        for g in range(POOL_W // POOL_GROUP):
            @pl.when(grp == g)
            def _(g=g):
                xv = u_ref[...]
                cnt = jnp.minimum(t + 1, 2 << g).astype(F32)
                cur = xv / cnt if transpose else xv
                for k in (1, 2, 4, 8)[:g + 1]:
                    if transpose:
                        cur = cur + jnp.where(t < S - k, pltpu.roll(cur, S - k, 0), 0.0)
                    else:
                        cur = cur + jnp.where(t >= k, pltpu.roll(cur, k, 0), 0.0)
                res = cur - xv if transpose else cur / cnt - xv
                o_ref[...] = res.astype(o_ref.dtype)

    in_specs = [pl.BlockSpec((S, tc), lambda c: (0, col_block * n_t + c))]
    args = [u_arr]
    if into is not None:
        in_specs.append(pl.BlockSpec(memory_space=pl.ANY))
        args.append(into)
    return pl.pallas_call(
        body, name="pool_bwd" if transpose else "pool_fwd", grid=(n_t,),
        in_specs=in_specs,
        out_specs=pl.BlockSpec((S, tc), lambda c: (0, c)),
        out_shape=SDS((S, POOL_W) if into is None else into.shape, out_dtype),
        input_output_aliases={} if into is None else {1: 0},
        compiler_params=_cparams(("parallel",)),
    )(*args)


def _rope_tables(zero):
    pos = jnp.arange(S, dtype=jnp.int32).astype(F32) + zero
    inv_freq = ROPE_THETA ** (-jnp.arange(0, ROT_DIM, 2, dtype=F32) / ROT_DIM)
    ang = pos[:, None] * inv_freq[None, :]
    cos8, sin8 = jnp.cos(ang), jnp.sin(ang)
    half = ROT_DIM // 2
    zeros = jnp.zeros((S, HEAD_DIM - ROT_DIM), F32)
    cos = jnp.concatenate([cos8, cos8, jnp.ones((S, HEAD_DIM - ROT_DIM), F32)], axis=1)
    sin = jnp.concatenate([-sin8, sin8, zeros], axis=1)
    rep = LANES // HEAD_DIM
    lane = jnp.arange(LANES)
    dim = lane % HEAD_DIM
    partner = jnp.where(dim < half, lane + half, jnp.where(dim < ROT_DIM, lane - half, -1))
    swap = (lane[:, None] == partner[None, :]).astype(BF16)
    return jnp.tile(cos, (1, rep)), jnp.tile(sin, (1, rep)), swap


def _rotate(xv, cos, sin, swap, transpose):
    rep = xv.shape[1] // LANES
    wide = lambda tab: jnp.concatenate([tab] * rep, axis=1)
    xb = xv.astype(BF16)
    partner = jnp.concatenate([lax.dot_general(xb[:, t * LANES:(t + 1) * LANES], swap, NN, preferred_element_type=F32)
                               for t in range(rep)], axis=1)
    mixed = partner * wide(sin)
    return xv * wide(cos) - mixed if transpose else xv * wide(cos) + mixed


def _qkv_prep(proj, tables):
    cos, sin, swap = tables

    def fn(x, c, s, sw):
        rot = _rotate(x[:, :2 * ATT_W], c, s, sw, False)
        return (jnp.concatenate([(rot[:, :ATT_W] * HEAD_DIM ** -0.5).astype(BF16), rot[:, ATT_W:].astype(BF16),
                                 x[:, 2 * ATT_W:].astype(BF16)], axis=1),), ()

    (qkv,) = _rowwise(fn, [(proj, 3 * ATT_W, 0), (cos, LANES, 0), (sin, LANES, 0), (swap, LANES, 0)],
                      [(3 * ATT_W, BF16)], name="qkv_prep")
    return qkv


ATT_T = 512


def _multiplicity(delta):
    ok = delta >= 0
    near = jnp.where(ok & (delta <= 128), 1.0, 0.0)
    mid = jnp.where(ok & (delta <= 512) & ((delta & 3) == 0), 1.0, 0.0)
    far = jnp.where(ok & ((delta & 15) == 0), 1.0, 0.0)
    return near + mid + far


def _attention_bias(zero):
    t = ATT_T
    pos = jnp.arange(t, dtype=jnp.int32) + jnp.asarray(zero).astype(jnp.int32)
    delta = jnp.arange(S // t, dtype=jnp.int32)[:, None, None] * t + pos[None, :, None] - pos[None, None, :]
    mult = _multiplicity(delta)
    return jnp.where(mult > 0.0, jnp.log(jnp.maximum(mult, 1.0)), -1e30).astype(F32)


def _head_split(v, first):
    zero = jnp.zeros_like(v)
    return [jnp.where(first, v, zero), jnp.where(first, zero, v)]


def _flash_fwd(qkv, bias):
    t = ATT_T
    n_hp = ATT_W // LANES

    def body(q_ref, k_ref, v_ref, b_ref, o_ref, lse_ref):
        i = pl.program_id(1)
        first = lax.broadcasted_iota(jnp.int32, (1, LANES), 1) < HEAD_DIM
        qs = _head_split(q_ref[...], first)

        def kv_step(j, carry):
            m0, l0, m1, l1, acc = carry
            off = pl.multiple_of(j * t, t)
            kb = k_ref[pl.ds(off, t), :]
            vs = _head_split(v_ref[pl.ds(off, t), :], first)
            bias_t = b_ref[i - j]
            new = []
            pv = None
            for h, (m_prev, l_prev) in enumerate(((m0, l0), (m1, l1))):
                s = lax.dot_general(qs[h], kb, NT, preferred_element_type=F32) + bias_t
                m_new = jnp.maximum(m_prev, jnp.max(s, axis=1, keepdims=True))
                p = jnp.exp(s - m_new)
                alpha = jnp.exp(m_prev - m_new)
                l_new = alpha * l_prev + jnp.sum(p, axis=1, keepdims=True)
                d = lax.dot_general(p.astype(BF16), vs[h], NN, preferred_element_type=F32)
                pv = d if pv is None else pv + d
                new.append((m_new, l_new, alpha))
            acc = acc * jnp.where(first, new[0][2], new[1][2]) + pv
            return new[0][0], new[0][1], new[1][0], new[1][1], acc

        neg = jnp.full((t, 1), -1e30, F32)
        zero = jnp.zeros((t, 1), F32)
        m0, l0, m1, l1, acc = lax.fori_loop(0, i + 1, kv_step, (neg, zero, neg, zero, jnp.zeros((t, LANES), F32)))
        o_ref[...] = acc * jnp.where(first, 1.0 / l0, 1.0 / l1)
        lse_ref[...] = jnp.where(first, m0 + jnp.log(l0), m1 + jnp.log(l1))

    blk = pl.BlockSpec((t, LANES), lambda hp, i: (i, hp))
    k_full = pl.BlockSpec((S, LANES), lambda hp, i: (0, n_hp + hp))
    v_full = pl.BlockSpec((S, LANES), lambda hp, i: (0, 2 * n_hp + hp))
    return pl.pallas_call(
        body, name="flash_fwd", grid=(n_hp, S // t),
        in_specs=[blk, k_full, v_full, pl.BlockSpec((S // t, t, t), lambda hp, i: (0, 0, 0))], out_specs=[blk, blk],
        out_shape=[SDS((S, ATT_W), F32), SDS((S, ATT_W), F32)],
        compiler_params=_cparams(("parallel", "arbitrary")),
    )(qkv, qkv, qkv, bias)


def _flash_bwd(qkv, o, do, lse, bias, after=()):
    t = ATT_T
    n_hp = ATT_W // LANES
    n_t = S // t

    def body(q_ref, k_ref, v_ref, o_ref, do_ref, lse_ref, b_ref, *rest):
        dq_ref, dk_ref, dv_ref = rest[-3:]
        j = pl.program_id(1)
        first = lax.broadcasted_iota(jnp.int32, (1, LANES), 1) < HEAD_DIM

        @pl.when(j == 0)
        def _():
            dq_ref[...] = jnp.zeros_like(dq_ref)

        kb = k_ref[...]
        vb = v_ref[...]
        ks = _head_split(kb, first)

        def q_step(i, carry):
            dk_acc, dv_acc = carry
            rows = pl.ds(pl.multiple_of(i * t, t), t)
            qs = _head_split(q_ref[rows, :], first)
            dob = do_ref[rows, :]
            prod = dob * o_ref[rows, :]
            d_all = jnp.sum(prod, axis=1, keepdims=True)
            d0 = jnp.sum(jnp.where(first, prod, 0.0), axis=1, keepdims=True)
            lse_b = lse_ref[rows, :]
            lse0 = jnp.max(jnp.where(first, lse_b, -jnp.inf), axis=1, keepdims=True)
            lse1 = jnp.max(jnp.where(first, -jnp.inf, lse_b), axis=1, keepdims=True)
            dos = _head_split(dob.astype(BF16), first)
            bias_t = b_ref[i - j]
            dq_t = jnp.zeros((t, LANES), F32)
            for h, (lse_h, d_h) in enumerate(((lse0, d0), (lse1, d_all - d0))):
                s = lax.dot_general(qs[h], kb, NT, preferred_element_type=F32)
                p = jnp.exp(s + (bias_t - lse_h))
                dp = lax.dot_general(dos[h], vb, NT, preferred_element_type=F32)
                ds = (p * (dp - d_h)).astype(BF16)
                dv_acc = dv_acc + lax.dot_general(p.astype(BF16), dos[h], TN, preferred_element_type=F32)
                dk_acc = dk_acc + lax.dot_general(ds, qs[h], TN, preferred_element_type=F32)
                dq_t = dq_t + lax.dot_general(ds, ks[h], NN, preferred_element_type=F32)
            dq_ref[rows, :] += dq_t
            return dk_acc, dv_acc

        zero = jnp.zeros((t, LANES), F32)
        dk_acc, dv_acc = lax.fori_loop(j, n_t, q_step, (zero, zero))
        dk_ref[...] = dk_acc
        dv_ref[...] = dv_acc.astype(BF16)

    blk = pl.BlockSpec((t, LANES), lambda hp, j: (j, hp))
    full = pl.BlockSpec((S, LANES), lambda hp, j: (0, hp))
    k_blk = pl.BlockSpec((t, LANES), lambda hp, j: (j, n_hp + hp))
    v_blk = pl.BlockSpec((t, LANES), lambda hp, j: (j, 2 * n_hp + hp))
    return pl.pallas_call(
        body, name="flash_bwd", grid=(n_hp, n_t),
        in_specs=([full, k_blk, v_blk, full, full, full, pl.BlockSpec((n_t, t, t), lambda hp, j: (0, 0, 0))]
                  + [pl.BlockSpec(memory_space=pl.ANY)] * len(after)),
        out_specs=[full, blk, blk],
        out_shape=[SDS((S, ATT_W), F32), SDS((S, ATT_W), F32), SDS((S, ATT_W), BF16)],
        compiler_params=_cparams(("parallel", "arbitrary")),
    )(qkv, qkv, qkv, o, do, lse, bias, *after)


SCAN_T = 256
SCAN_GROUP = 8
SCAN_STEPS = (1, 2, 4)
ST_ROWS = 2 * N_CPLX // LANES
HALF = ST_ROWS // 2


def _scan_tables(lam_t):
    lam = lax.complex(lam_t[:HALF].reshape(N_CPLX), lam_t[HALF:].reshape(N_CPLX))
    pows = jnp.cumprod(jnp.broadcast_to(lam, (SCAN_GROUP, N_CPLX)), axis=0)
    shifts = jnp.asarray(SCAN_STEPS)
    sub = jnp.arange(SCAN_GROUP)[None, :, None]
    steps = pows[shifts - 1][:, None, :]
    fwd = jnp.concatenate([jnp.where(sub >= shifts[:, None, None], steps, 0.0), pows[None]], axis=0)
    bwd = jnp.concatenate([jnp.where(sub <= SCAN_GROUP - 1 - shifts[:, None, None], jnp.conj(steps), 0.0),
                           jnp.conj(pows)[None, ::-1]], axis=0)

    def pack(tabs):
        return jnp.concatenate([jnp.real(tabs), jnp.imag(tabs)], axis=-1).astype(F32)

    return pack(fwd), pack(bwd)


def _cmul_add(xr, xi, lr, li, sr, si):
    return xr + lr * sr - li * si, xi + lr * si + li * sr


def _group_scan(xr, xi, tab_ref, cr, ci, reverse):
    for j, k in enumerate(SCAN_STEPS):
        shift = SCAN_GROUP - k if reverse else k
        xr, xi = _cmul_add(xr, xi, tab_ref[j, :, :N_CPLX], tab_ref[j, :, N_CPLX:],
                           pltpu.roll(xr, shift, 0), pltpu.roll(xi, shift, 0))
    return _cmul_add(xr, xi, tab_ref[3, :, :N_CPLX], tab_ref[3, :, N_CPLX:],
                     jnp.broadcast_to(cr, (SCAN_GROUP, N_CPLX)), jnp.broadcast_to(ci, (SCAN_GROUP, N_CPLX)))


SSM_SUPER = 4
SB_ROWS = SSM_W // SSM_SUPER
SB_COLS = N_CPLX // SSM_SUPER


def _super_blocks():
    return [(slice(b * SB_ROWS, (b + 1) * SB_ROWS), slice(h * SB_COLS, (h + 1) * SB_COLS),
             slice(h * N_CPLX + b * SB_COLS, h * N_CPLX + (b + 1) * SB_COLS))
            for b in range(SSM_SUPER) for h in range(2)]


def _dot16(a, b, dims):
    return lax.dot_general(a.astype(BF16), b.astype(BF16), dims, preferred_element_type=F32)


def _s5_fwd(tab, u_arr, u_cols, w_b, w_ct):
    nc = N_CPLX

    def body(tab_ref, u_ref, wb_ref, wct_ref, st_ref, y_ref, carry, bu_scr):
        @pl.when(pl.program_id(0) == 0)
        def _():
            carry[...] = jnp.zeros_like(carry)

        for rows_b, cols_c, cols_s in _super_blocks():
            bu_scr[:, cols_s] = _dot16(u_ref[:, rows_b], wb_ref[rows_b, cols_c], NN)

        def group(a, c):
            rows = pl.ds(pl.multiple_of(a * SCAN_GROUP, SCAN_GROUP), SCAN_GROUP)
            xr, xi = _group_scan(bu_scr[rows, :nc], bu_scr[rows, nc:], tab_ref, c[0], c[1], False)
            st_ref[rows, :nc] = xr
            st_ref[rows, nc:] = xi
            return xr[SCAN_GROUP - 1:SCAN_GROUP, :], xi[SCAN_GROUP - 1:SCAN_GROUP, :]

        cr, ci = lax.fori_loop(0, SCAN_T // SCAN_GROUP, group, (carry[:, :nc], carry[:, nc:]), unroll=2)
        carry[:, :nc] = cr
        carry[:, nc:] = ci

        for b in range(SSM_SUPER):
            (rows_b, cols_re, st_re), (_, cols_im, st_im) = _super_blocks()[2 * b:2 * b + 2]
            y_ref[:, rows_b] = (_dot16(st_ref[:, st_re], wct_ref[rows_b, cols_re], NT)
                                + _dot16(st_ref[:, st_im], wct_ref[rows_b, cols_im], NT))

    const = lambda shape: pl.BlockSpec(shape, lambda i: (0,) * len(shape))
    return pl.pallas_call(
        body, name="s5_fwd", grid=(S // SCAN_T,),
        in_specs=[const((4, SCAN_GROUP, 2 * nc)), pl.BlockSpec((SCAN_T, SSM_W), lambda i: (i, u_cols[0] // SSM_W)),
                  const((SSM_W, 2 * SB_COLS)), const((SSM_W, 2 * SB_COLS))],
        out_specs=[pl.BlockSpec((SCAN_T, 2 * nc), lambda i: (i, 0)), pl.BlockSpec((SCAN_T, SSM_W), lambda i: (i, 0))],
        out_shape=[SDS((S, 2 * nc), F32), SDS((S, SSM_W), F32)],
        scratch_shapes=[pltpu.VMEM((1, 2 * nc), F32), pltpu.VMEM((SCAN_T, 2 * nc), F32)],
        compiler_params=_cparams(("arbitrary",)),
    )(tab, u_arr, w_b, w_ct)


def _s5_bwd(tab, dy, states, u_arr, u_cols, w_b, w_ct):
    n_blk = S // SCAN_T
    nc = N_CPLX

    def body(tab_ref, dy_ref, x_ref, u_ref, wb_ref, wct_ref, du_ref, dlam_ref, dwb_ref, dwct_ref,
             carry, acc, d_scr, g_ref):
        i = pl.program_id(0)

        @pl.when(i == 0)
        def _():
            carry[...] = jnp.zeros_like(carry)
            acc[...] = jnp.zeros_like(acc)
            dwb_ref[...] = jnp.zeros_like(dwb_ref)
            dwct_ref[...] = jnp.zeros_like(dwct_ref)

        for rows_b, cols_c, cols_s in _super_blocks():
            d_scr[:, cols_s] = _dot16(dy_ref[:, rows_b], wct_ref[rows_b, cols_c], NN)

        last_row = lax.broadcasted_iota(jnp.int32, (SCAN_GROUP, 1), 0) == SCAN_GROUP - 1
        d_ref = d_scr

        def group(j, c):
            cr, ci = c
            rows = pl.ds(pl.multiple_of((SCAN_T // SCAN_GROUP - 1 - j) * SCAN_GROUP, SCAN_GROUP), SCAN_GROUP)
            gr, gi = _group_scan(d_ref[rows, :nc], d_ref[rows, nc:], tab_ref, cr, ci, True)
            g_ref[rows, :nc] = gr
            g_ref[rows, nc:] = gi
            nr = jnp.where(last_row, jnp.broadcast_to(cr, (SCAN_GROUP, nc)), pltpu.roll(gr, SCAN_GROUP - 1, 0))
            ni = jnp.where(last_row, jnp.broadcast_to(ci, (SCAN_GROUP, nc)), pltpu.roll(gi, SCAN_GROUP - 1, 0))
            sr, si = x_ref[rows, :nc], x_ref[rows, nc:]
            acc[:, :nc] += nr * sr + ni * si
            acc[:, nc:] += ni * sr - nr * si
            return gr[0:1, :], gi[0:1, :]

        cr, ci = lax.fori_loop(0, SCAN_T // SCAN_GROUP, group, (carry[:, :nc], carry[:, nc:]), unroll=2)
        carry[:, :nc] = cr
        carry[:, nc:] = ci

        for b in range(SSM_SUPER):
            (rows_b, cols_re, st_re), (_, cols_im, st_im) = _super_blocks()[2 * b:2 * b + 2]
            du_ref[:, rows_b] = (_dot16(g_ref[:, st_re], wb_ref[rows_b, cols_re], NT)
                                 + _dot16(g_ref[:, st_im], wb_ref[rows_b, cols_im], NT))
            for cols_c, cols_s in ((cols_re, st_re), (cols_im, st_im)):
                dwb_ref[rows_b, cols_c] += _dot16(u_ref[:, rows_b], g_ref[:, cols_s], TN)
                dwct_ref[rows_b, cols_c] += _dot16(dy_ref[:, rows_b], x_ref[:, cols_s], TN)

        @pl.when(i == n_blk - 1)
        def _():
            dlam_ref[...] = jnp.sum(acc[...], axis=0, keepdims=True)

    const = lambda shape: pl.BlockSpec(shape, lambda i: (0,) * len(shape))
    rows = lambda width, col_block=0: pl.BlockSpec((SCAN_T, width), lambda i: (n_blk - 1 - i, col_block))
    maps = const((SSM_W, 2 * SB_COLS))
    return pl.pallas_call(
        body, name="s5_bwd", grid=(n_blk,),
        in_specs=[const((4, SCAN_GROUP, 2 * nc)), rows(SSM_W), rows(2 * nc), rows(SSM_W, u_cols[0] // SSM_W), maps, maps],
        out_specs=[rows(SSM_W), const((1, 2 * nc)), maps, maps],
        out_shape=[SDS((S, SSM_W), F32), SDS((1, 2 * nc), F32), SDS((SSM_W, 2 * SB_COLS), F32),
                   SDS((SSM_W, 2 * SB_COLS), F32)],
        scratch_shapes=[pltpu.VMEM((1, 2 * nc), F32), pltpu.VMEM((SCAN_GROUP, 2 * nc), F32),
                        pltpu.VMEM((SCAN_T, 2 * nc), F32), pltpu.VMEM((SCAN_T, 2 * nc), F32)],
        compiler_params=_cparams(("arbitrary",)),
    )(tab, dy, states, u_arr, w_b, w_ct)


def _ssm_prep(a_re, a_im, log_dt, b_re, b_im, c_re, c_im):
    lam = lax.complex(a_re, a_im)
    dt = jnp.exp(log_dt)[:, None]
    lam_bar = jnp.exp(lam * dt)
    b_bar = ((lam_bar - 1.0) / lam)[..., None] * lax.complex(b_re, b_im)
    lam_t = jnp.concatenate([jnp.real(lam_bar).reshape(HALF, LANES), jnp.imag(lam_bar).reshape(HALF, LANES)], axis=0)
    groups_per_super = SSM_GROUPS // SSM_SUPER
    on_diag = ((lax.broadcasted_iota(jnp.int32, (SSM_W, SB_COLS), 0) // SSM_GROUP) % groups_per_super
               == lax.broadcasted_iota(jnp.int32, (SSM_W, SB_COLS), 1) // SSM_STATE)

    repeat = (lax.broadcasted_iota(jnp.int32, (SSM_STATE, SB_COLS), 0)
              == lax.broadcasted_iota(jnp.int32, (SSM_STATE, SB_COLS), 1) % SSM_STATE).astype(F32)

    def compact(m):
        tiled = jnp.dot(m.reshape(SSM_W, SSM_STATE), repeat, precision=lax.Precision.HIGHEST)
        return jnp.where(on_diag, tiled, 0.0)

    w_b = jnp.concatenate([compact(jnp.real(b_bar).transpose(0, 2, 1)),
                           compact(jnp.imag(b_bar).transpose(0, 2, 1))], axis=1)
    w_ct = jnp.concatenate([compact(c_re), -compact(c_im)], axis=1)
    return lam_t, w_b, w_ct


U_SSM_COLS = (4 * ATT_W, SSM_W)


def _row(v):
    return v.reshape(1, -1)


def _even_fwd(x, h, tail, pre, post, w_in, late_w, glu_b, ssm_d, prep, tables, after=()):
    lam_t, w_b, w_ct, scan_fwd_tab, scan_bwd_tab = prep
    proj = _mm(h, w_in, "nn", F32, b_blocks=True, after=after)
    qkv = _qkv_prep(proj, tables[:3])
    att, lse = _flash_fwd(qkv, tables[3])
    w_out, glu_w = late_w(att)
    states, y = _s5_fwd(scan_fwd_tab, proj, U_SSM_COLS, w_b, w_ct)

    def act1(yv, uv, dv):
        return (_gelu_and_grad(yv + dv * uv)[0],), ()

    (z1,) = _rowwise(act1, [(y, SSM_W, 0), (proj, SSM_W, 8), (ssm_d, SSM_W, 0)], [(SSM_W, F32)], name="ssm_act_fwd")
    lin = _mm(z1, glu_w, "nn", F32)

    def gate(att_v, ga, gs, z1v, linv, bv):
        ssm_out = z1v * _sigmoid(linv + bv)
        return (jnp.concatenate([att_v * _silu_and_grad(ga)[0], ssm_out * _silu_and_grad(gs)[0]], axis=1),), ()

    (merged,) = _rowwise(gate, [(att, ATT_W, 0), (proj, ATT_W, 3), (proj, SSM_W, 9), (z1, SSM_W, 0),
                                (lin, SSM_W, 0), (glu_b, SSM_W, 0)], [(EVEN_OUT, BF16)], name="even_gate_fwd")
    yout = _mm(merged, w_out, "nn", F32)
    saved = (x, h, proj, qkv, att, lse, states, y, z1, lin, merged, yout, w_out, glu_w, scan_bwd_tab)
    return tail(x, yout, post) + (saved,)


def _even_bwd(g, saved, pre, post, w_in, late_w, glu_b, ssm_d, prep, tables, on_w, on_ssm):
    x, h, proj, qkv, att, lse, states, y, z1, lin, merged, yout, w_out, glu_w, scan_bwd_tab = saved
    lam_t, w_b, w_ct = prep[:3]
    dyout, dpost = _post_bwd(g, yout, post)
    dmerged = _mm(dyout, w_out, "nt", F32)
    dw_out = _mm(merged, dyout, "tn", BF16)

    def gate_bwd(dm_a, dm_s, att_v, ga, gs, z1v, linv, bv):
        sa, dsa = _silu_and_grad(ga)
        ss, dss = _silu_and_grad(gs)
        sig = _sigmoid(linv + bv)
        ssm_out = z1v * sig
        dssm = dm_s * ss
        dlin = dssm * z1v * sig * (1.0 - sig)
        return (dm_a * sa, dm_a * att_v * dsa, dm_s * ssm_out * dss, dssm * sig, dlin), (dlin,)

    datt, dg_att, dg_ssm, dz1a, dlin, dglu_b = _rowwise(
        gate_bwd, [(dmerged, ATT_W, 0), (dmerged, SSM_W, 2), (att, ATT_W, 0), (proj, ATT_W, 3), (proj, SSM_W, 9),
                   (z1, SSM_W, 0), (lin, SSM_W, 0), (glu_b, SSM_W, 0)],
        [(ATT_W, F32), (ATT_W, BF16), (SSM_W, BF16), (SSM_W, F32), (SSM_W, BF16)], [SSM_W], name="even_gate_bwd")
    dz1b = _mm(dlin, glu_w, "nt", F32)
    dglu_w = _mm(z1, dlin, "tn", BF16)

    def act1_bwd(da, db, yv, uv, dv):
        dpre = (da + db) * _gelu_and_grad(yv + dv * uv)[1]
        return (dpre, dpre * dv), (dpre * uv,)

    sent_late_w = on_w(dict(w_out=dw_out, glu_w=dglu_w))
    dy, du_direct, dd = _rowwise(act1_bwd, [(dz1a, SSM_W, 0), (dz1b, SSM_W, 0), (y, SSM_W, 0), (proj, SSM_W, 8),
                                            (ssm_d, SSM_W, 0)], [(SSM_W, BF16), (SSM_W, F32)], [SSM_W],
                                 name="ssm_act_bwd", after=(sent_late_w,))
    du_state, dlam_row, dw_b, dw_ct = _s5_bwd(scan_bwd_tab, dy, states, proj, U_SSM_COLS, w_b, w_ct)
    dlam = jnp.concatenate([dlam_row[0, :N_CPLX].reshape(HALF, LANES), dlam_row[0, N_CPLX:].reshape(HALF, LANES)],
                           axis=0)
    sent_ssm = on_ssm((dlam, dw_b, dw_ct))
    dq, dk, dv = _flash_bwd(qkv, att, datt, lse, tables[3], after=() if sent_ssm is None else (sent_ssm,))

    def assemble(dqv, dkv, dvv, dga, dua, dub, dgs, c, s, sw):
        rot = _rotate(jnp.concatenate([dqv, dkv], axis=1), c, s, sw, True)
        return (jnp.concatenate([(rot[:, :ATT_W] * HEAD_DIM ** -0.5).astype(BF16), rot[:, ATT_W:].astype(BF16),
                                 dvv, dga, (dua + dub).astype(BF16), dgs], axis=1),), ()

    (dproj,) = _rowwise(assemble, [(dq, ATT_W, 0), (dk, ATT_W, 0), (dv, ATT_W, 0), (dg_att, ATT_W, 0),
                                   (du_state, SSM_W, 0), (du_direct, SSM_W, 0), (dg_ssm, SSM_W, 0),
                                   (tables[0], LANES, 0), (tables[1], LANES, 0), (tables[2], LANES, 0)],
                        [(EVEN_IN, BF16)], name="dproj_assemble")
    dw_in = _mm(h, dproj, "tn", BF16, out_blocks=True)
    sent = on_w(dict(w_in=dw_in))
    dh = _mm(dproj, w_in, "nt", F32, b_blocks=True, after=(sent,))
    g_prev, dpre = _pre_bwd(g, dh, x, pre)
    return g_prev, dict(pre=dpre, post=dpost, glu_b=dglu_b, ssm_d=dd)


def _odd_fwd(x, h, tail, pre, post, w_in, pool_w, pool_scale, w_out):
    proj = _mm(h, w_in, "nn", F32, b_blocks=True)
    mixed = _pool(proj, 0, False, BF16)
    ylin = _gmm(mixed, pool_w, "nn", F32)

    def gate(yl, gt, sc):
        return (yl * sc * _silu_and_grad(gt)[0],), ()

    (z,) = _rowwise(gate, [(ylin, POOL_W, 0), (proj, POOL_W, 1), (pool_scale, POOL_W, 0)], [(POOL_W, BF16)],
                    name="odd_gate_fwd")
    yout = _mm(z, w_out, "nn", F32)
    return tail(x, yout, post) + ((x, h, proj, mixed, ylin, z, yout),)


def _odd_bwd(g, saved, pre, post, w_in, pool_w, pool_scale, w_out, on_w):
    x, h, proj, mixed, ylin, z, yout = saved
    dyout, dpost = _post_bwd(g, yout, post)
    dz = _mm(dyout, w_out, "nt", F32)
    dw_out = _mm(z, dyout, "tn", BF16)

    def gate_bwd(dzv, yl, gt, sc):
        sg, dsg = _silu_and_grad(gt)
        tt = dzv * sg
        return (tt * sc, dzv * yl * sc * dsg), (tt * yl,)

    dylin, dproj_gate, dscale = _rowwise(gate_bwd, [(dz, POOL_W, 0), (ylin, POOL_W, 0), (proj, POOL_W, 1),
                                                    (pool_scale, POOL_W, 0)],
                                         [(POOL_W, BF16), (POOL_W, BF16, ODD_IN, 1)], [POOL_W], name="odd_gate_bwd")
    dmixed = _gmm(dylin, pool_w, "nt", F32)
    dpool_w = _gmm(mixed, dylin, "tn", BF16)
    dproj = _pool(dmixed, 0, True, BF16, into=dproj_gate)
    dw_in = _mm(h, dproj, "tn", BF16, out_blocks=True)
    sent = on_w(dict(w_in=dw_in, w_out=dw_out, pool_w=dpool_w))
    dh = _mm(dproj, w_in, "nt", F32, b_blocks=True, after=(sent,))
    g_prev, dpre = _pre_bwd(g, dh, x, pre)
    return g_prev, dict(pre=dpre, post=dpost, pool_scale=dscale)


def _my_index():
    return 4 * lax.axis_index("x") + 2 * lax.axis_index("y") + lax.axis_index("c")


HBM_SPEC = pl.BlockSpec(memory_space=pltpu.HBM)
SEM_SPEC = pl.BlockSpec(memory_space=pltpu.SEMAPHORE)
SPLIT_EFFECT = pltpu.SideEffectType.DATAFLOW_SIDE_EFFECTING


def _device_of(j):
    return (j // 4, (j // 2) % 2, j % 2)


def _split_copy(srcs, lands, send_sems, recv_sems, gather, i, j, dst_slot, recv_slot):
    return pltpu.make_async_remote_copy(
        src_ref=srcs[i] if gather else srcs[i].at[j], dst_ref=lands[i].at[dst_slot],
        send_sem=send_sems.at[i * N_DEV + j], recv_sem=recv_sems.at[i * N_DEV + recv_slot],
        device_id=_device_of(j), device_id_type=MESH_ID)


def _own_copy(srcs, lands, send_sems, gather, i, me):
    return pltpu.make_async_copy(srcs[i] if gather else srcs[i].at[me], lands[i].at[me], send_sems.at[i * N_DEV + me])


def _xchg_start(name, srcs, gather, after=()):
    n = len(srcs)
    n_in = n + len(after)

    def body(*refs):
        src_refs = refs[:n]
        send_sems, recv_sems, token = refs[n_in], refs[n_in + 1], refs[-1]
        land_refs = refs[n_in + 2 + n:n_in + 2 + 2 * n]
        me = _my_index()
        for j in range(N_DEV):
            @pl.when(me != j)
            def _(j=j):
                for i in range(n):
                    _split_copy(src_refs, land_refs, send_sems, recv_sems, gather, i, j, me, me).start()
        for i in range(n):
            _own_copy(src_refs, land_refs, send_sems, gather, i, me).start()
        token[...] = jnp.zeros_like(token)

    land_shapes = [((N_DEV,) + a.shape) if gather else a.shape for a in srcs]
    thru = ([pltpu.HBM(a.shape, a.dtype) for a in srcs] + [pltpu.HBM(s, a.dtype) for s, a in zip(land_shapes, srcs)])
    res = pl.pallas_call(
        body, name=name,
        out_shape=(pltpu.SemaphoreType.DMA((n * N_DEV,)), pltpu.SemaphoreType.DMA((n * N_DEV,)), *thru,
                   SDS((8, LANES), F32)),
        in_specs=[HBM_SPEC] * n + [pl.BlockSpec(memory_space=pl.ANY)] * len(after),
        out_specs=(SEM_SPEC, SEM_SPEC, *([HBM_SPEC] * (2 * n)), pl.BlockSpec(memory_space=pltpu.VMEM)),
        input_output_aliases={i: 2 + i for i in range(n)},
        compiler_params=pltpu.CompilerParams(has_side_effects=SPLIT_EFFECT),
    )(*[pltpu.with_memory_space_constraint(a, pltpu.HBM) for a in srcs], *after)
    return res[0], res[1], list(res[2:2 + n]), list(res[2 + n:2 + 2 * n]), res[-1]


def _xchg_wait(name, started, gather, after):
    send_sems, recv_sems, srcs, lands, _ = started
    n = len(srcs)

    def body(*refs):
        src_refs, land_refs = refs[:n], refs[n:2 * n]
        send_r, recv_r = refs[2 * n], refs[2 * n + 1]
        me = _my_index()
        for j in range(N_DEV):
            @pl.when(me != j)
            def _(j=j):
                for i in range(n):
                    _split_copy(src_refs, land_refs, send_r, recv_r, gather, i, j, me, me).wait_send()
                    _split_copy(src_refs, land_refs, send_r, recv_r, gather, i, j, j, j).wait_recv()
        for i in range(n):
            _own_copy(src_refs, land_refs, send_r, gather, i, me).wait()

    thru = [pltpu.HBM(a.shape, a.dtype) for a in list(srcs) + list(lands)]
    res = pl.pallas_call(
        body, name=name, out_shape=tuple(thru),
        in_specs=[HBM_SPEC] * (2 * n) + [SEM_SPEC, SEM_SPEC] + [pl.BlockSpec(memory_space=pl.ANY)] * len(after),
        out_specs=tuple([HBM_SPEC] * (2 * n)),
        input_output_aliases={i: i for i in range(2 * n)},
        compiler_params=pltpu.CompilerParams(has_side_effects=SPLIT_EFFECT),
    )(*srcs, *lands, send_sems, recv_sems, *after)
    return list(res[n:])


def _adam_layer(w, slots, m, v, layer, name, into=None):
    n_l, r, c = w.shape
    ns = slots.shape[0]
    tr = r
    while tr * c * 4 > (1 << 20) and tr % 16 == 0:
        tr //= 2
    assert r % tr == 0

    def body(w_ref, g_ref, m_ref, v_ref, *rest):
        go_ref, d_ref, mo_ref, vo_ref = rest[-4:]
        g = g_ref[0].astype(F32)
        for s in range(1, ns):
            g = g + g_ref[s].astype(F32)
        mn = ADAM_B1 * m_ref[...] + (1.0 - ADAM_B1) * g
        vn = ADAM_B2 * v_ref[...] + (1.0 - ADAM_B2) * (g * g)
        m_hat = mn / (1.0 - ADAM_B1 ** ADAM_STEP)
        v_hat = vn / (1.0 - ADAM_B2 ** ADAM_STEP)
        go_ref[...] = g
        d_ref[...] = -ADAM_LR * (m_hat / (jnp.sqrt(v_hat) + ADAM_EPS) + ADAM_WD * w_ref[...])
        mo_ref[...] = mn
        vo_ref[...] = vn

    blk = pl.BlockSpec((None, tr, c), lambda i: (layer, i, 0))
    earlier = () if into is None else tuple(into)
    return pl.pallas_call(
        body, name=name, grid=(r // tr,),
        in_specs=[blk, pl.BlockSpec((ns, tr, c), lambda i: (0, i, 0)), blk, blk]
        + [pl.BlockSpec(memory_space=pl.ANY)] * len(earlier),
        out_specs=[blk] * 4, out_shape=[SDS((n_l, r, c), F32)] * 4,
        input_output_aliases={4 + q: q for q in range(len(earlier))},
        compiler_params=_cparams(("arbitrary",)),
    )(*_in_hbm((w, slots, m, v)), *earlier)


def _adam(w, gslots, m, v, name):
    r, c = w.shape
    ns = gslots.shape[0]
    tr = r
    while tr * c * 4 > (1 << 20) and tr % 16 == 0:
        tr //= 2
    assert r % tr == 0

    def body(w_ref, g_ref, m_ref, v_ref, go_ref, d_ref, mo_ref, vo_ref):
        g = g_ref[0].astype(F32)
        for s in range(1, ns):
            g = g + g_ref[s].astype(F32)
        wv = w_ref[...]
        mn = ADAM_B1 * m_ref[...] + (1.0 - ADAM_B1) * g
        vn = ADAM_B2 * v_ref[...] + (1.0 - ADAM_B2) * (g * g)
        m_hat = mn / (1.0 - ADAM_B1 ** ADAM_STEP)
        v_hat = vn / (1.0 - ADAM_B2 ** ADAM_STEP)
        go_ref[...] = g
        d_ref[...] = -ADAM_LR * (m_hat / (jnp.sqrt(v_hat) + ADAM_EPS) + ADAM_WD * wv)
        mo_ref[...] = mn
        vo_ref[...] = vn

    blk = pl.BlockSpec((tr, c), lambda i: (i, 0))
    return pl.pallas_call(
        body, name=name, grid=(r // tr,),
        in_specs=[blk, pl.BlockSpec((ns, tr, c), lambda i: (0, i, 0)), blk, blk],
        out_specs=[blk] * 4, out_shape=[SDS((r, c), F32)] * 4,
        compiler_params=_cparams(("parallel",)),
    )(w, gslots, m, v)


def _sum_slots(slots, name):
    ns, r, c = slots.shape

    def body(g_ref, o_ref):
        g = g_ref[0]
        for s in range(1, ns):
            g = g + g_ref[s]
        o_ref[...] = g

    return pl.pallas_call(
        body, name=name, grid=(1,),
        in_specs=[pl.BlockSpec((ns, r, c), lambda i: (0, 0, 0))], out_specs=pl.BlockSpec((r, c), lambda i: (0, 0)),
        out_shape=SDS((r, c), F32), compiler_params=_cparams(("arbitrary",)),
    )(slots)


def _adam_params(params, name):
    n = len(params)

    def body(*refs):
        ins, outs = refs[:5 * n], refs[5 * n:]
        for p in range(n):
            w_ref, m_ref, v_ref, g_first, g_rest = ins[5 * p:5 * p + 5]
            go_ref, d_ref, mo_ref, vo_ref = outs[4 * p:4 * p + 4]
            for part, g_ref in ((slice(0, 1), g_first), (slice(1, w_ref.shape[0]), g_rest)):
                g = g_ref[...]
                mn = ADAM_B1 * m_ref[part] + (1.0 - ADAM_B1) * g
                vn = ADAM_B2 * v_ref[part] + (1.0 - ADAM_B2) * (g * g)
                m_hat = mn / (1.0 - ADAM_B1 ** ADAM_STEP)
                v_hat = vn / (1.0 - ADAM_B2 ** ADAM_STEP)
                go_ref[part] = g
                d_ref[part] = -ADAM_LR * (m_hat / (jnp.sqrt(v_hat) + ADAM_EPS) + ADAM_WD * w_ref[part])
                mo_ref[part] = mn
                vo_ref[part] = vn

    def whole(a):
        return pl.BlockSpec(a.shape, lambda i, nd=a.ndim: (0,) * nd)

    flat = _in_hbm([a for prm in params for a in prm])
    outs = pl.pallas_call(
        body, name=name, grid=(1,),
        in_specs=[whole(a) for a in flat],
        out_specs=[whole(prm[0]) for prm in params for _ in range(4)],
        out_shape=[SDS(prm[0].shape, F32) for prm in params for _ in range(4)],
        compiler_params=_cparams(("arbitrary",)),
    )(*flat)
    return [outs[4 * p:4 * p + 4] for p in range(n)]


SMALL_NAMES = ("pre_norm", "post_norm", "ssm_a_re", "ssm_a_im", "ssm_log_dt", "ssm_b_re", "ssm_b_im", "ssm_c_re",
               "ssm_c_im", "ssm_d", "ssm_glu_b")
SSM_NAMES = ("ssm_a_re", "ssm_a_im", "ssm_log_dt", "ssm_b_re", "ssm_b_im", "ssm_c_re", "ssm_c_im")
WEIGHT_ORDER = ("pre_norm", "post_norm", "even_w_in", "even_w_out", "ssm_a_re", "ssm_a_im", "ssm_log_dt", "ssm_b_re",
                "ssm_b_im", "ssm_c_re", "ssm_c_im", "ssm_d", "ssm_glu_w", "ssm_glu_b", "odd_w_in", "pool_w",
                "pool_scale", "odd_w_out")
PACK_ROWS_ALIGN = 8


def _pack(parts):
    flat = jnp.concatenate([p.reshape(-1).astype(F32) for p in parts])
    rows = -(-flat.shape[0] // (LANES * PACK_ROWS_ALIGN)) * PACK_ROWS_ALIGN
    return jnp.pad(flat, (0, rows * LANES - flat.shape[0])).reshape(rows, LANES)


def _unpack(packed, shapes):
    flat = packed.reshape(-1)
    out, off = [], 0
    for shp in shapes:
        size = math.prod(shp)
        out.append(flat[off:off + size].reshape(shp))
        off += size
    return out


EVEN_SHARDED = ("w_in", "w_out", "glu_w")
ODD_SHARDED = ("w_in", "pool_w", "w_out")
FAMILY = {(0, "w_in"): "even_w_in", (0, "w_out"): "even_w_out", (0, "glu_w"): "ssm_glu_w",
          (1, "w_in"): "odd_w_in", (1, "pool_w"): "pool_w", (1, "w_out"): "odd_w_out"}


def _sharded_keys(layer):
    return EVEN_SHARDED if layer % 2 == 0 else ODD_SHARDED


def _local_step(x, tgt, small, get_weights, on_w, on_ssm, on_grads, zero=0.0):
    tables = _rope_tables(zero) + (_attention_bias(zero),)
    preps, prep_vjps = [], []
    for i in range(2):
        out, vjp = jax.vjp(_ssm_prep, small["ssm_a_re"][i] + zero, small["ssm_a_im"][i], small["ssm_log_dt"][i],
                           small["ssm_b_re"][i], small["ssm_b_im"][i], small["ssm_c_re"][i], small["ssm_c_im"][i])
        preps.append(tuple(out) + _scan_tables(out[0]))
        prep_vjps.append(vjp)

    def layer_args(layer, wts):
        i = layer // 2
        pre, post = _row(small["pre_norm"][layer]), _row(small["post_norm"][layer])
        if layer % 2 == 0:
            return (pre, post, wts["w_in"], wts["late"], _row(small["ssm_glu_b"][i]), _row(small["ssm_d"][i]),
                    preps[i], tables)
        return (pre, post, wts["w_in"], wts["pool_w"], _row(wts["pool_scale"]), wts["w_out"])

    saved, args = [], []
    cur = x
    for layer in range(4):
        if layer == 0:
            h = _norm_fwd(cur, _row(small["pre_norm"][0]))
        after = (cur,) if layer else (cur, h, tables[0], tables[3]) + preps[0][1:] + preps[1][1:]
        wts = get_weights(layer, after)
        args.append(layer_args(layer, wts))
        first_after = dict(after=(wts["token"],)) if "token" in wts else {}
        if layer < 3:
            def tail(xv, yv, post, next_gain=_row(small["pre_norm"][layer + 1])):
                return tuple(_post_fwd(xv, yv, post, next_gain))
        else:
            def tail(xv, yv, post):
                return tuple(_post_fwd_loss(xv, yv, post, tgt))
        cur, h, sv = (_even_fwd if layer % 2 == 0 else _odd_fwd)(cur, h, tail, *args[layer], **first_after)
        saved.append(sv)
    g, sq = cur, h
    loss = 0.5 * jnp.sum(sq) / D

    lg = [None] * 4
    token = jnp.zeros((), F32)
    for layer in reversed(range(4)):
        largs = list(args[layer])
        largs[1] = largs[1] + token
        hooks = dict(on_w=functools.partial(on_w, layer))
        ssm_grads = []
        if layer % 2 == 0:
            def ssm_hook(cotangents, layer=layer):
                ssm_grads.append(prep_vjps[layer // 2](cotangents))
                return on_ssm(layer, ssm_grads[0])

            hooks["on_ssm"] = ssm_hook
        g, lg[layer] = (_even_bwd if layer % 2 == 0 else _odd_bwd)(g, saved[layer], *largs, **hooks)
        if ssm_grads:
            lg[layer]["ssm"] = ssm_grads[0]
        token = on_grads(layer, lg[layer])
    return loss, g, token


def _to_slots(key, gfull):
    if key == "w_in":
        return gfull
    if key in ("w_out", "glu_w"):
        rr, nn = gfull.shape
        return gfull.reshape(N_DEV, rr // N_DEV, nn)
    assert key == "pool_w"
    gg, rr, nn = gfull.shape
    return gfull.reshape(gg, N_DEV, rr // N_DEV, nn).transpose(1, 0, 2, 3)


def _from_gathered(key, gat):
    if key == "w_in":
        return gat
    if key in ("w_out", "glu_w"):
        _, rr, nn = gat.shape
        return gat.reshape(N_DEV * rr, nn)
    assert key == "pool_w"
    _, gg, rr, nn = gat.shape
    return gat.transpose(1, 0, 2, 3).reshape(gg, N_DEV * rr, nn)


def kernel(x, pre_norm, post_norm, even_w_in, even_w_out, ssm_a_re, ssm_a_im, ssm_log_dt, ssm_b_re, ssm_b_im, ssm_c_re, ssm_c_im, ssm_d, ssm_glu_w, ssm_glu_b, odd_w_in, pool_w, pool_scale, odd_w_out, loss_target, m_pre_norm, m_post_norm, m_even_w_in, m_even_w_out, m_ssm_a_re, m_ssm_a_im, m_ssm_log_dt, m_ssm_b_re, m_ssm_b_im, m_ssm_c_re, m_ssm_c_im, m_ssm_d, m_ssm_glu_w, m_ssm_glu_b, m_odd_w_in, m_pool_w, m_pool_scale, m_odd_w_out, v_pre_norm, v_post_norm, v_even_w_in, v_even_w_out, v_ssm_a_re, v_ssm_a_im, v_ssm_log_dt, v_ssm_b_re, v_ssm_b_im, v_ssm_c_re, v_ssm_c_im, v_ssm_d, v_ssm_glu_w, v_ssm_glu_b, v_odd_w_in, v_pool_w, v_pool_scale, v_odd_w_out):
    w = dict(pre_norm=pre_norm, post_norm=post_norm, even_w_in=even_w_in, even_w_out=even_w_out, ssm_a_re=ssm_a_re,
             ssm_a_im=ssm_a_im, ssm_log_dt=ssm_log_dt, ssm_b_re=ssm_b_re, ssm_b_im=ssm_b_im, ssm_c_re=ssm_c_re,
             ssm_c_im=ssm_c_im, ssm_d=ssm_d, ssm_glu_w=ssm_glu_w, ssm_glu_b=ssm_glu_b, odd_w_in=odd_w_in,
             pool_w=pool_w, pool_scale=pool_scale, odd_w_out=odd_w_out)
    mom = dict(pre_norm=m_pre_norm, post_norm=m_post_norm, even_w_in=m_even_w_in, even_w_out=m_even_w_out,
               ssm_a_re=m_ssm_a_re, ssm_a_im=m_ssm_a_im, ssm_log_dt=m_ssm_log_dt, ssm_b_re=m_ssm_b_re,
               ssm_b_im=m_ssm_b_im, ssm_c_re=m_ssm_c_re, ssm_c_im=m_ssm_c_im, ssm_d=m_ssm_d, ssm_glu_w=m_ssm_glu_w,
               ssm_glu_b=m_ssm_glu_b, odd_w_in=m_odd_w_in, pool_w=m_pool_w, pool_scale=m_pool_scale,
               odd_w_out=m_odd_w_out)
    var = dict(pre_norm=v_pre_norm, post_norm=v_post_norm, even_w_in=v_even_w_in, even_w_out=v_even_w_out,
               ssm_a_re=v_ssm_a_re, ssm_a_im=v_ssm_a_im, ssm_log_dt=v_ssm_log_dt, ssm_b_re=v_ssm_b_re,
               ssm_b_im=v_ssm_b_im, ssm_c_re=v_ssm_c_re, ssm_c_im=v_ssm_c_im, ssm_d=v_ssm_d, ssm_glu_w=v_ssm_glu_w,
               ssm_glu_b=v_ssm_glu_b, odd_w_in=v_odd_w_in, pool_w=v_pool_w, pool_scale=v_pool_scale,
               odd_w_out=v_odd_w_out)
    me = _my_index()
    scale_cols = pool_scale.shape[1]

    def shards_of(layer, keys):
        i = layer // 2
        shards = [w[FAMILY[(layer % 2, k)]][i].astype(BF16) for k in keys]
        if layer % 2 == 1:
            shards.append(jnp.pad(pool_scale[i][None], ((0, PACK_ROWS_ALIGN - 1), (0, 0))))
        return shards

    def start_gather(tag, after=()):
        return _xchg_start(f"gather_start_{tag}", shards[tag], True, after)

    shards = {0: shards_of(0, EVEN_SHARDED[:1]), "0_late": shards_of(0, EVEN_SHARDED[1:])}
    shards.update({layer: shards_of(layer, _sharded_keys(layer)) for layer in (1, 2, 3)})
    gather_started = {0: start_gather(0)}
    small = {nm: w[nm] for nm in SMALL_NAMES}
    packed_names = ("pre_norm", "post_norm") + SSM_NAMES + ("ssm_d", "ssm_glu_b")
    tails = {nm: (SSM_GROUPS, SSM_STATE * SSM_GROUP) if nm in ("ssm_b_re", "ssm_b_im") else w[nm].shape[1:]
             for nm in packed_names}
    dense = lambda nm, a: a.reshape((a.shape[0],) + tails[nm])
    small_operands = {nm: tuple(dense(nm, tree[nm]) for tree in (w, mom, var)) for nm in packed_names}
    early_work = [a for nm in ("ssm_b_re", "ssm_b_im") for a in small_operands[nm]]

    def get_weights(layer, after):
        keys = EVEN_SHARDED[:1] if layer == 0 else _sharded_keys(layer)
        if layer == 0:
            after = tuple(after) + tuple(early_work)
        lands = _xchg_wait(f"gather_wait_{layer}", gather_started[layer], True, after)
        wts = {k: _from_gathered(k, gat) for k, gat in zip(keys, lands)}
        if layer % 2 == 1:
            wts["pool_scale"] = lands[-1][:, 0, :].reshape(N_DEV * scale_cols)
        if layer == 0:
            prev = gather_started["0_late"] = start_gather("0_late", after=(lands[0],))
            for later in (1, 2, 3):
                prev = gather_started[later] = start_gather(later, after=(prev[4],))
            wts["token"] = prev[4]

            def late(after_late):
                late_lands = _xchg_wait("gather_wait_0_late", gather_started["0_late"], True, (after_late,))
                return tuple(_from_gathered(k, gat) for k, gat in zip(EVEN_SHARDED[1:], late_lands))

            wts["late"] = late
        elif layer == 2:
            wts["late"] = lambda after_late: (wts["w_out"], wts["glu_w"])
        return wts

    scatter_started = []

    def on_w(layer, gw):
        keys = tuple(k for k in _sharded_keys(layer) if k in gw)
        started = _xchg_start(f"scatter_start_{layer}_{keys[0]}", [_to_slots(k, gw[k]) for k in keys], False)
        scatter_started.append((layer, keys, started))
        return started[4]

    def wait_scatters(layers, after):
        for layer, keys, started in scatter_started:
            if layer in layers:
                lands = _xchg_wait(f"scatter_wait_{layer}_{keys[0]}", started, False, after)
                for k, land in zip(keys, lands):
                    recv[(layer, k)] = land

    layer_grads = {}
    early_started, mid_started = [], []

    def on_ssm(layer, ssm_grads):
        if layer != 0:
            return None
        mid_started.append(_xchg_start("mid_start", [_pack(list(ssm_grads))], True))
        return mid_started[0][4]

    def on_grads(layer, lg):
        layer_grads[layer] = lg
        zero = jnp.zeros((), F32)
        if layer == 1:
            lgs = layer_grads
            early = ([jnp.concatenate([lgs[l][k] for l in (1, 2, 3)], axis=0) for k in ("pre", "post")]
                     + list(lgs[2]["ssm"]) + [lgs[2]["ssm_d"], lgs[2]["glu_b"],
                                              jnp.concatenate([lgs[1]["pool_scale"], lgs[3]["pool_scale"]], axis=0)])
            early_started.append(_xchg_start("small_start", [_pack(early)], True))
            zero = zero + early_started[0][4][0, 0]
        return zero

    loss_local, grad_x, token = _local_step(x[0], loss_target[0], small, get_weights, on_w, on_ssm, on_grads,
                                            zero=gather_started[0][4][0, 0])

    lg0 = layer_grads[0]
    late_started = _xchg_start("late_start", [_pack([lg0["pre"], lg0["post"], lg0["ssm_d"], lg0["glu_b"],
                                                     loss_local.reshape(1)]) + token], True)

    def adam_family(parity, k, which, into=None):
        nm = FAMILY[(parity, k)]
        cols = w[nm].shape[-1]
        return _adam_layer(w[nm].reshape(2, -1, cols), recv[(parity + 2 * which, k)].reshape(N_DEV, -1, cols),
                           mom[nm].reshape(2, -1, cols), var[nm].reshape(2, -1, cols), which,
                           f"adam_{nm}_{which}", into)

    recv, res = {}, {}
    wait_scatters((3, 2, 1), (late_started[4],))
    for k in ODD_SHARDED:
        res[FAMILY[(1, k)]] = adam_family(1, k, 1, adam_family(1, k, 0))
    half_done = {k: adam_family(0, k, 1) for k in EVEN_SHARDED}
    odd_done = tuple(half_done[k][0] for k in EVEN_SHARDED)

    (early_slots,) = _xchg_wait("small_wait", early_started[0], True, odd_done)
    (mid_slots,) = _xchg_wait("mid_wait", mid_started[0], True, odd_done)
    early_shapes = [(w[nm].shape[0] - 1,) + tails[nm] for nm in packed_names] + [(2, N_DEV * scale_cols)]
    g_early = _unpack(_sum_slots(early_slots, "sum_small_early"), early_shapes)
    g_mid = _unpack(_sum_slots(mid_slots, "sum_small_mid"), [(1,) + tails[nm] for nm in SSM_NAMES])

    (late_slots,) = _xchg_wait("late_wait", late_started, True, (g_early[0], g_mid[0]))
    wait_scatters((0,), (late_slots,))
    for k in EVEN_SHARDED:
        res[FAMILY[(0, k)]] = adam_family(0, k, 0, half_done[k])
    for nm in FAMILY.values():
        res[nm] = [o.reshape(w[nm].shape) for o in res[nm]]

    late_names = ("pre_norm", "post_norm", "ssm_d", "ssm_glu_b")
    g_late = _unpack(_sum_slots(late_slots, "sum_small_late"), [(1,) + tails[nm] for nm in late_names] + [(1,)])
    g_first = dict(zip(late_names, g_late))
    g_first.update(zip(SSM_NAMES, g_mid))
    outs = _adam_params([small_operands[nm] + (g_first[nm], g_early[j]) for j, nm in enumerate(packed_names)],
                        "adam_small")
    for nm, four in zip(packed_names, outs):
        res[nm] = [o.reshape(w[nm].shape) for o in four]
    loss = g_late[-1].reshape(())
    g_scale = lax.dynamic_slice_in_dim(g_early[-1], me * scale_cols, scale_cols, axis=1)
    pad = ((0, PACK_ROWS_ALIGN - 2), (0, 0))
    outs = _adam(jnp.pad(pool_scale, pad), jnp.pad(g_scale, pad)[None], jnp.pad(m_pool_scale, pad),
                 jnp.pad(v_pool_scale, pad), name="adam_pool_scale")
    res["pool_scale"] = [o[:2] for o in outs]

    out = [loss, grad_x[None]]
    for kind in range(4):
        out += [res[nm][kind] for nm in WEIGHT_ORDER]
    return tuple(out)
```

```python
import functools
import math

import jax
import jax.numpy as jnp
from jax import lax
from jax.experimental import pallas as pl
from jax.experimental.pallas import tpu as pltpu

F32 = jnp.float32
BF16 = jnp.bfloat16
SDS = jax.ShapeDtypeStruct

N_DEV = 8
S = 2048
D = 1024
HEAD_DIM = 64
ROT_DIM = 16
ROPE_THETA = 500000.0
ATT_W = 1024
SSM_W = 512
SSM_GROUPS = 32
SSM_GROUP = 16
SSM_STATE = 64
N_CPLX = SSM_GROUPS * SSM_STATE
POOL_W = 2048
POOL_GROUP = 512
EVEN_IN = 5120
EVEN_OUT = 1536
ODD_IN = 4096
RMS_EPS = 1e-6
LANES = 128
VMEM_LIMIT = 48 * 1024 * 1024

ADAM_LR = 0.001
ADAM_B1 = 0.9
ADAM_B2 = 0.999
ADAM_EPS = 1e-08
ADAM_WD = 0.01
ADAM_STEP = 10

MESH_ID = pl.DeviceIdType.MESH
NN = (((1,), (0,)), ((), ()))
NT = (((1,), (1,)), ((), ()))
TN = (((0,), (0,)), ((), ()))
_DN = {"nn": NN, "nt": NT, "tn": TN}


def _cparams(sem):
    return pltpu.CompilerParams(dimension_semantics=sem, vmem_limit_bytes=VMEM_LIMIT)


def _in_hbm(arrs):
    return [pltpu.with_memory_space_constraint(a, pltpu.HBM) for a in arrs]


MM_TILES = (1024, 768, 512)


def _tile(dim):
    return next((t for t in MM_TILES if dim % t == 0), dim)


BLOCK_PAIR = 2
MAX_WHOLE_K = 2048


def _mm(a, b, mode, out_dtype, b_blocks=False, out_blocks=False, after=()):
    if b_blocks:
        nblk, rows, cb = b.shape
        b2_shape = (rows, nblk * cb)
    else:
        b2_shape = b.shape
    if mode == "nn":
        (m, k), n = a.shape, b2_shape[1]
    elif mode == "nt":
        (m, k), n = a.shape, b2_shape[0]
    else:
        (k, m), n = a.shape, b2_shape[1]
    tm, tn, tk = _tile(m), _tile(n), _tile(k)
    if k <= MAX_WHOLE_K:
        tk = k
    if b_blocks and mode == "nn":
        tn = BLOCK_PAIR * cb
        if tn <= MM_TILES[0]:
            tm = m
    if b_blocks and mode == "nt":
        tk = BLOCK_PAIR * cb
    if out_blocks:
        cb = n // N_DEV
        tn = BLOCK_PAIR * cb
        tk = k
    nk = k // tk

    def body(a_ref, b_ref, *rest):
        o_ref, acc_ref = rest[-2:]
        kk = pl.program_id(2)
        bv = jnp.concatenate([b_ref[p] for p in range(BLOCK_PAIR)], axis=1) if b_blocks else b_ref[...]
        part = lax.dot_general(a_ref[...].astype(BF16), bv.astype(BF16), _DN[mode], preferred_element_type=F32)

        def write(res):
            if out_blocks:
                for p in range(BLOCK_PAIR):
                    o_ref[p] = res[:, p * cb:(p + 1) * cb].astype(o_ref.dtype)
            else:
                o_ref[...] = res.astype(o_ref.dtype)

        if nk == 1:
            write(part)
            return

        @pl.when(kk == 0)
        def _():
            acc_ref[...] = part

        @pl.when((kk > 0) & (kk < nk - 1))
        def _():
            acc_ref[...] += part

        @pl.when(kk == nk - 1)
        def _():
            write(acc_ref[...] + part)

    if mode == "nn":
        a_spec = pl.BlockSpec((tm, tk), lambda i, j, kk: (i, kk))
        b_spec = pl.BlockSpec((tk, tn), lambda i, j, kk: (kk, j))
    elif mode == "nt":
        a_spec = pl.BlockSpec((tm, tk), lambda i, j, kk: (i, kk))
        b_spec = pl.BlockSpec((tn, tk), lambda i, j, kk: (j, kk))
    else:
        a_spec = pl.BlockSpec((tk, tm), lambda i, j, kk: (kk, i))
        b_spec = pl.BlockSpec((tk, tn), lambda i, j, kk: (kk, j))
    if b_blocks and mode == "nn":
        b_spec = pl.BlockSpec((BLOCK_PAIR, tk, cb), lambda i, j, kk: (j, kk, 0))
    if b_blocks and mode == "nt":
        b_spec = pl.BlockSpec((BLOCK_PAIR, tn, cb), lambda i, j, kk: (kk, j, 0))
    out_spec = pl.BlockSpec((tm, tn), lambda i, j, kk: (i, j))
    out_shape = SDS((m, n), out_dtype)
    if out_blocks:
        out_spec = pl.BlockSpec((BLOCK_PAIR, tm, cb), lambda i, j, kk: (j, i, 0))
        out_shape = SDS((N_DEV, m, cb), out_dtype)
    return pl.pallas_call(
        body, name=f"mm_{mode}_{m}x{k}x{n}",
        grid=(m // tm, n // tn, nk),
        in_specs=[a_spec, b_spec] + [pl.BlockSpec(memory_space=pl.ANY)] * len(after),
        out_specs=out_spec,
        out_shape=out_shape,
        scratch_shapes=[pltpu.VMEM((tm, tn) if nk > 1 else (8, LANES), F32)],
        compiler_params=_cparams(("parallel", "parallel", "arbitrary")),
    )(a, b, *after)


def _gmm(a, b, mode, out_dtype, tm=S):
    ng, gw = POOL_W // POOL_GROUP, POOL_GROUP
    ns = S // tm
    if mode in ("nn", "nt"):
        def body(a_ref, b_ref, o_ref):
            o_ref[...] = lax.dot_general(a_ref[...].astype(BF16), b_ref[...].astype(BF16), _DN[mode],
                                         preferred_element_type=F32).astype(o_ref.dtype)

        return pl.pallas_call(
            body, name=f"gmm_{mode}", grid=(ng, ns),
            in_specs=[pl.BlockSpec((tm, gw), lambda g, i: (i, g)),
                      pl.BlockSpec((None, gw, gw), lambda g, i: (g, 0, 0))],
            out_specs=pl.BlockSpec((tm, gw), lambda g, i: (i, g)),
            out_shape=SDS((S, POOL_W), out_dtype),
            compiler_params=_cparams(("parallel", "parallel")),
        )(a, b)

    def body_tn(a_ref, b_ref, o_ref, acc_ref):
        i = pl.program_id(1)

        @pl.when(i == 0)
        def _():
            acc_ref[...] = jnp.zeros_like(acc_ref)

        acc_ref[...] += lax.dot_general(a_ref[...].astype(BF16), b_ref[...].astype(BF16), TN,
                                        preferred_element_type=F32)

        @pl.when(i == ns - 1)
        def _():
            o_ref[...] = acc_ref[...].astype(o_ref.dtype)

    return pl.pallas_call(
        body_tn, name="gmm_tn", grid=(ng, ns),
        in_specs=[pl.BlockSpec((tm, gw), lambda g, i: (i, g)),
                  pl.BlockSpec((tm, gw), lambda g, i: (i, g))],
        out_specs=pl.BlockSpec((None, gw, gw), lambda g, i: (g, 0, 0)),
        out_shape=SDS((ng, gw, gw), out_dtype),
        scratch_shapes=[pltpu.VMEM((gw, gw), F32)],
        compiler_params=_cparams(("parallel", "arbitrary")),
    )(a, b)


def _rowwise(fn, inputs, out_defs, acc_defs=(), tm=512, name=None, after=()):
    n_in, n_out, n_acc = len(inputs), len(out_defs), len(acc_defs)
    n_after = len(after)
    in_specs, args = [], []
    for arr, width, cb in inputs:
        if arr.shape[0] != S:
            in_specs.append(pl.BlockSpec((arr.shape[0], width), lambda i, cb=cb: (0, cb)))
        else:
            in_specs.append(pl.BlockSpec((tm, width), lambda i, cb=cb: (i, cb)))
        args.append(arr)
    out_defs = [d if len(d) == 4 else (d[0], d[1], d[0], 0) for d in out_defs]
    out_shape = [SDS((S, ww), dt) for _, dt, ww, _ in out_defs] + [SDS((1, w), F32) for w in acc_defs]
    out_specs = ([pl.BlockSpec((tm, w), lambda i, cb=cb: (i, cb)) for w, _, _, cb in out_defs]
                 + [pl.BlockSpec((1, w), lambda i: (0, 0)) for w in acc_defs])

    def kern(*refs):
        vals = [r[...] for r in refs[:n_in]]
        outs, accs = fn(*vals)
        out_refs = refs[n_in + n_after:]
        for r, v in zip(out_refs[:n_out], outs):
            r[...] = v.astype(r.dtype)
        if n_acc:
            acc_refs = out_refs[n_out:]

            @pl.when(pl.program_id(0) == 0)
            def _():
                for r in acc_refs:
                    r[...] = jnp.zeros_like(r)

            for r, v in zip(acc_refs, accs):
                r[...] += jnp.sum(v, axis=0, keepdims=True)

    res = pl.pallas_call(
        kern, name=name, grid=(S // tm,), in_specs=in_specs + [pl.BlockSpec(memory_space=pl.ANY)] * n_after,
        out_specs=out_specs, out_shape=out_shape, compiler_params=_cparams(("arbitrary",)),
    )(*args, *after)
    return res


def _sigmoid(x):
    return 1.0 / (1.0 + jnp.exp(-x))


def _silu_and_grad(x):
    s = _sigmoid(x)
    return x * s, s * (1.0 + x * (1.0 - s))


_GELU_K = math.sqrt(2.0 / math.pi)
_GELU_C = 0.044715


def _gelu_and_grad(x):
    t = jnp.tanh(_GELU_K * (x + _GELU_C * (x * x * x)))
    cdf = 0.5 * (1.0 + t)
    grad = cdf + 0.5 * x * (1.0 - t * t) * (_GELU_K * (1.0 + 3.0 * _GELU_C * x * x))
    return x * cdf, grad


def _rms(xv, gain):
    r = lax.rsqrt(jnp.mean(xv * xv, axis=-1, keepdims=True) + RMS_EPS)
    return xv * r * gain


def _rms_bwd(dout, xv, gain):
    r = lax.rsqrt(jnp.mean(xv * xv, axis=-1, keepdims=True) + RMS_EPS)
    xhat = xv * r
    dxhat = dout * gain
    dx = r * (dxhat - xhat * jnp.mean(dxhat * xhat, axis=-1, keepdims=True))
    return dx, dout * xhat


def _norm_fwd(x, gain):
    (h,) = _rowwise(lambda xv, g: ((_rms(xv, g),), ()), [(x, D, 0), (gain, D, 0)], [(D, BF16)], name="norm_fwd")
    return h


def _post_fwd(x, y, gain, next_gain):
    def fn(xv, yv, g, gn):
        out = xv + _rms(yv, g)
        return (out, _rms(out, gn)), ()

    return _rowwise(fn, [(x, D, 0), (y, D, 0), (gain, D, 0), (next_gain, D, 0)], [(D, F32), (D, BF16)],
                    name="post_fwd")


def _post_fwd_loss(x, y, gain, tgt):
    def fn(xv, yv, g, tv):
        e = xv + _rms(yv, g) - tv
        return (e * (1.0 / D),), (e * e,)

    return _rowwise(fn, [(x, D, 0), (y, D, 0), (gain, D, 0), (tgt, D, 0)], [(D, F32)], [D], name="post_fwd_loss")


def _post_bwd(g, y, gain):
    def fn(gv, yv, gn):
        dx, dg = _rms_bwd(gv, yv, gn)
        return (dx,), (dg,)

    return _rowwise(fn, [(g, D, 0), (y, D, 0), (gain, D, 0)], [(D, BF16)], [D], name="post_bwd")


def _pre_bwd(g, dh, x, gain):
    def fn(gv, dhv, xv, gn):
        dx, dg = _rms_bwd(dhv, xv, gn)
        return (gv + dx,), (dg,)

    return _rowwise(fn, [(g, D, 0), (dh, D, 0), (x, D, 0), (gain, D, 0)], [(D, F32)], [D], name="pre_bwd")


def _pool(u_arr, col_block, transpose, out_dtype, into=None, tc=256):
    n_t = POOL_W // tc
    per_group = POOL_GROUP // tc

    def body(u_ref, *rest):
        o_ref = rest[-1]
        grp = pl.program_id(0) // per_group
        t = lax.broadcasted_iota(jnp.int32, (S, 1), 0)
        for g in range(POOL_W // POOL_GROUP):
            @pl.when(grp == g)
            def _(g=g):
                xv = u_ref[...]
                cnt = jnp.minimum(t + 1, 2 << g).astype(F32)
                cur = xv / cnt if transpose else xv
                for k in (1, 2, 4, 8)[:g + 1]:
                    if transpose:
                        cur = cur + jnp.where(t < S - k, pltpu.roll(cur, S - k, 0), 0.0)
                    else:
                        cur = cur + jnp.where(t >= k, pltpu.roll(cur, k, 0), 0.0)
                res = cur - xv if transpose else cur / cnt - xv
                o_ref[...] = res.astype(o_ref.dtype)

    in_specs = [pl.BlockSpec((S, tc), lambda c: (0, col_block * n_t + c))]
    args = [u_arr]
    if into is not None:
        in_specs.append(pl.BlockSpec(memory_space=pl.ANY))
        args.append(into)
    return pl.pallas_call(
        body, name="pool_bwd" if transpose else "pool_fwd", grid=(n_t,),
        in_specs=in_specs,
        out_specs=pl.BlockSpec((S, tc), lambda c: (0, c)),
        out_shape=SDS((S, POOL_W) if into is None else into.shape, out_dtype),
        input_output_aliases={} if into is None else {1: 0},
        compiler_params=_cparams(("parallel",)),
    )(*args)


def _rope_tables(zero):
    pos = jnp.arange(S, dtype=jnp.int32).astype(F32) + zero
    inv_freq = ROPE_THETA ** (-jnp.arange(0, ROT_DIM, 2, dtype=F32) / ROT_DIM)
    ang = pos[:, None] * inv_freq[None, :]
    cos8, sin8 = jnp.cos(ang), jnp.sin(ang)
    half = ROT_DIM // 2
    zeros = jnp.zeros((S, HEAD_DIM - ROT_DIM), F32)
    cos = jnp.concatenate([cos8, cos8, jnp.ones((S, HEAD_DIM - ROT_DIM), F32)], axis=1)
    sin = jnp.concatenate([-sin8, sin8, zeros], axis=1)
    rep = LANES // HEAD_DIM
    lane = jnp.arange(LANES)
    dim = lane % HEAD_DIM
    partner = jnp.where(dim < half, lane + half, jnp.where(dim < ROT_DIM, lane - half, -1))
    swap = (lane[:, None] == partner[None, :]).astype(BF16)
    return jnp.tile(cos, (1, rep)), jnp.tile(sin, (1, rep)), swap


def _rotate(xv, cos, sin, swap, transpose):
    rep = xv.shape[1] // LANES
    wide = lambda tab: jnp.concatenate([tab] * rep, axis=1)
    xb = xv.astype(BF16)
    partner = jnp.concatenate([lax.dot_general(xb[:, t * LANES:(t + 1) * LANES], swap, NN, preferred_element_type=F32)
                               for t in range(rep)], axis=1)
    mixed = partner * wide(sin)
    return xv * wide(cos) - mixed if transpose else xv * wide(cos) + mixed


def _qkv_prep(proj, tables):
    cos, sin, swap = tables

    def fn(x, c, s, sw):
        rot = _rotate(x[:, :2 * ATT_W], c, s, sw, False)
        return (jnp.concatenate([(rot[:, :ATT_W] * HEAD_DIM ** -0.5).astype(BF16), rot[:, ATT_W:].astype(BF16),
                                 x[:, 2 * ATT_W:].astype(BF16)], axis=1),), ()

    (qkv,) = _rowwise(fn, [(proj, 3 * ATT_W, 0), (cos, LANES, 0), (sin, LANES, 0), (swap, LANES, 0)],
                      [(3 * ATT_W, BF16)], name="qkv_prep")
    return qkv


ATT_T = 512


def _multiplicity(delta):
    ok = delta >= 0
    near = jnp.where(ok & (delta <= 128), 1.0, 0.0)
    mid = jnp.where(ok & (delta <= 512) & ((delta & 3) == 0), 1.0, 0.0)
    far = jnp.where(ok & ((delta & 15) == 0), 1.0, 0.0)
    return near + mid + far


def _attention_bias(zero):
    t = ATT_T
    pos = jnp.arange(t, dtype=jnp.int32) + jnp.asarray(zero).astype(jnp.int32)
    delta = jnp.arange(S // t, dtype=jnp.int32)[:, None, None] * t + pos[None, :, None] - pos[None, None, :]
    mult = _multiplicity(delta)
    return jnp.where(mult > 0.0, jnp.log(jnp.maximum(mult, 1.0)), -1e30).astype(F32)


def _head_split(v, first):
    zero = jnp.zeros_like(v)
    return [jnp.where(first, v, zero), jnp.where(first, zero, v)]


def _flash_fwd(qkv, bias):
    t = ATT_T
    n_hp = ATT_W // LANES

    def body(q_ref, k_ref, v_ref, b_ref, o_ref, lse_ref):
        i = pl.program_id(1)
        first = lax.broadcasted_iota(jnp.int32, (1, LANES), 1) < HEAD_DIM
        qs = _head_split(q_ref[...], first)

        def kv_step(j, carry):
            m0, l0, m1, l1, acc = carry
            off = pl.multiple_of(j * t, t)
            kb = k_ref[pl.ds(off, t), :]
            vs = _head_split(v_ref[pl.ds(off, t), :], first)
            bias_t = b_ref[i - j]
            new = []
            pv = None
            for h, (m_prev, l_prev) in enumerate(((m0, l0), (m1, l1))):
                s = lax.dot_general(qs[h], kb, NT, preferred_element_type=F32) + bias_t
                m_new = jnp.maximum(m_prev, jnp.max(s, axis=1, keepdims=True))
                p = jnp.exp(s - m_new)
                alpha = jnp.exp(m_prev - m_new)
                l_new = alpha * l_prev + jnp.sum(p, axis=1, keepdims=True)
                d = lax.dot_general(p.astype(BF16), vs[h], NN, preferred_element_type=F32)
                pv = d if pv is None else pv + d
                new.append((m_new, l_new, alpha))
            acc = acc * jnp.where(first, new[0][2], new[1][2]) + pv
            return new[0][0], new[0][1], new[1][0], new[1][1], acc

        neg = jnp.full((t, 1), -1e30, F32)
        zero = jnp.zeros((t, 1), F32)
        m0, l0, m1, l1, acc = lax.fori_loop(0, i + 1, kv_step, (neg, zero, neg, zero, jnp.zeros((t, LANES), F32)))
        o_ref[...] = acc * jnp.where(first, 1.0 / l0, 1.0 / l1)
        lse_ref[...] = jnp.where(first, m0 + jnp.log(l0), m1 + jnp.log(l1))

    blk = pl.BlockSpec((t, LANES), lambda hp, i: (i, hp))
    k_full = pl.BlockSpec((S, LANES), lambda hp, i: (0, n_hp + hp))
    v_full = pl.BlockSpec((S, LANES), lambda hp, i: (0, 2 * n_hp + hp))
    return pl.pallas_call(
        body, name="flash_fwd", grid=(n_hp, S // t),
        in_specs=[blk, k_full, v_full, pl.BlockSpec((S // t, t, t), lambda hp, i: (0, 0, 0))], out_specs=[blk, blk],
        out_shape=[SDS((S, ATT_W), F32), SDS((S, ATT_W), F32)],
        compiler_params=_cparams(("parallel", "arbitrary")),
    )(qkv, qkv, qkv, bias)


def _flash_bwd(qkv, o, do, lse, bias, rope, after=()):
    t = ATT_T
    n_hp = ATT_W // LANES
    n_t = S // t

    def body(q_ref, k_ref, v_ref, o_ref, do_ref, lse_ref, b_ref, cos_ref, sin_ref, swap_ref, *rest):
        dq_ref, dk_ref, dv_ref, dq_acc = rest[-4:]
        j = pl.program_id(1)
        first = lax.broadcasted_iota(jnp.int32, (1, LANES), 1) < HEAD_DIM

        @pl.when(j == 0)
        def _():
            dq_acc[...] = jnp.zeros_like(dq_acc)

        kb = k_ref[...]
        vb = v_ref[...]
        ks = _head_split(kb, first)

        def q_step(i, carry):
            dk_acc, dv_acc = carry
            rows = pl.ds(pl.multiple_of(i * t, t), t)
            qs = _head_split(q_ref[rows, :], first)
            dob = do_ref[rows, :]
            prod = dob * o_ref[rows, :]
            d_all = jnp.sum(prod, axis=1, keepdims=True)
            d0 = jnp.sum(jnp.where(first, prod, 0.0), axis=1, keepdims=True)
            lse_b = lse_ref[rows, :]
            lse0 = jnp.max(jnp.where(first, lse_b, -jnp.inf), axis=1, keepdims=True)
            lse1 = jnp.max(jnp.where(first, -jnp.inf, lse_b), axis=1, keepdims=True)
            dos = _head_split(dob.astype(BF16), first)
            bias_t = b_ref[i - j]
            dq_t = jnp.zeros((t, LANES), F32)
            for h, (lse_h, d_h) in enumerate(((lse0, d0), (lse1, d_all - d0))):
                s = lax.dot_general(qs[h], kb, NT, preferred_element_type=F32)
                p = jnp.exp(s + (bias_t - lse_h))
                dp = lax.dot_general(dos[h], vb, NT, preferred_element_type=F32)
                ds = (p * (dp - d_h)).astype(BF16)
                dv_acc = dv_acc + lax.dot_general(p.astype(BF16), dos[h], TN, preferred_element_type=F32)
                dk_acc = dk_acc + lax.dot_general(ds, qs[h], TN, preferred_element_type=F32)
                dq_t = dq_t + lax.dot_general(ds, ks[h], NN, preferred_element_type=F32)
            dq_acc[rows, :] += dq_t
            return dk_acc, dv_acc

        zero = jnp.zeros((t, LANES), F32)
        dk_acc, dv_acc = lax.fori_loop(j, n_t, q_step, (zero, zero))
        swap = swap_ref[...]
        own = pl.ds(pl.multiple_of(j * t, t), t)
        dk_ref[...] = _rotate(dk_acc, cos_ref[own, :], sin_ref[own, :], swap, True).astype(BF16)
        dv_ref[...] = dv_acc.astype(BF16)

        @pl.when(j == n_t - 1)
        def _():
            for c in range(n_t):
                rows = slice(c * t, (c + 1) * t)
                rot = _rotate(dq_acc[rows, :], cos_ref[rows, :], sin_ref[rows, :], swap, True)
                dq_ref[rows, :] = (rot * HEAD_DIM ** -0.5).astype(BF16)

    blk = pl.BlockSpec((t, LANES), lambda hp, j: (j, hp))
    full = pl.BlockSpec((S, LANES), lambda hp, j: (0, hp))
    k_blk = pl.BlockSpec((t, LANES), lambda hp, j: (j, n_hp + hp))
    v_blk = pl.BlockSpec((t, LANES), lambda hp, j: (j, 2 * n_hp + hp))
    table = pl.BlockSpec((S, LANES), lambda hp, j: (0, 0))
    return pl.pallas_call(
        body, name="flash_bwd", grid=(n_hp, n_t),
        in_specs=([full, k_blk, v_blk, full, full, full, pl.BlockSpec((n_t, t, t), lambda hp, j: (0, 0, 0)),
                   table, table, pl.BlockSpec((LANES, LANES), lambda hp, j: (0, 0))]
                  + [pl.BlockSpec(memory_space=pl.ANY)] * len(after)),
        out_specs=[full, blk, blk],
        out_shape=[SDS((S, ATT_W), BF16)] * 3,
        scratch_shapes=[pltpu.VMEM((S, LANES), F32)],
        compiler_params=_cparams(("parallel", "arbitrary")),
    )(qkv, qkv, qkv, o, do, lse, bias, *rope, *after)


SCAN_T = 256
SCAN_GROUP = 8
SCAN_STEPS = (1, 2, 4)
ST_ROWS = 2 * N_CPLX // LANES
HALF = ST_ROWS // 2


def _scan_tables(lam_t):
    lam = lax.complex(lam_t[:HALF].reshape(N_CPLX), lam_t[HALF:].reshape(N_CPLX))
    pows = jnp.cumprod(jnp.broadcast_to(lam, (SCAN_GROUP, N_CPLX)), axis=0)
    shifts = jnp.asarray(SCAN_STEPS)
    sub = jnp.arange(SCAN_GROUP)[None, :, None]
    steps = pows[shifts - 1][:, None, :]
    fwd = jnp.concatenate([jnp.where(sub >= shifts[:, None, None], steps, 0.0), pows[None]], axis=0)
    bwd = jnp.concatenate([jnp.where(sub <= SCAN_GROUP - 1 - shifts[:, None, None], jnp.conj(steps), 0.0),
                           jnp.conj(pows)[None, ::-1]], axis=0)

    def pack(tabs):
        return jnp.concatenate([jnp.real(tabs), jnp.imag(tabs)], axis=-1).astype(F32)

    return pack(fwd), pack(bwd)


def _cmul_add(xr, xi, lr, li, sr, si):
    return xr + lr * sr - li * si, xi + lr * si + li * sr


def _group_scan(xr, xi, tab_ref, cr, ci, reverse):
    for j, k in enumerate(SCAN_STEPS):
        shift = SCAN_GROUP - k if reverse else k
        xr, xi = _cmul_add(xr, xi, tab_ref[j, :, :N_CPLX], tab_ref[j, :, N_CPLX:],
                           pltpu.roll(xr, shift, 0), pltpu.roll(xi, shift, 0))
    return _cmul_add(xr, xi, tab_ref[3, :, :N_CPLX], tab_ref[3, :, N_CPLX:],
                     jnp.broadcast_to(cr, (SCAN_GROUP, N_CPLX)), jnp.broadcast_to(ci, (SCAN_GROUP, N_CPLX)))


SSM_SUPER = 4
SB_ROWS = SSM_W // SSM_SUPER
SB_COLS = N_CPLX // SSM_SUPER


def _super_blocks():
    return [(slice(b * SB_ROWS, (b + 1) * SB_ROWS), slice(h * SB_COLS, (h + 1) * SB_COLS),
             slice(h * N_CPLX + b * SB_COLS, h * N_CPLX + (b + 1) * SB_COLS))
            for b in range(SSM_SUPER) for h in range(2)]


def _dot16(a, b, dims):
    return lax.dot_general(a.astype(BF16), b.astype(BF16), dims, preferred_element_type=F32)


def _s5_fwd(tab, u_arr, u_cols, w_b, w_ct):
    nc = N_CPLX

    def body(tab_ref, u_ref, wb_ref, wct_ref, st_ref, y_ref, carry, bu_scr):
        @pl.when(pl.program_id(0) == 0)
        def _():
            carry[...] = jnp.zeros_like(carry)

        for rows_b, cols_c, cols_s in _super_blocks():
            bu_scr[:, cols_s] = _dot16(u_ref[:, rows_b], wb_ref[rows_b, cols_c], NN)

        def group(a, c):
            rows = pl.ds(pl.multiple_of(a * SCAN_GROUP, SCAN_GROUP), SCAN_GROUP)
            xr, xi = _group_scan(bu_scr[rows, :nc], bu_scr[rows, nc:], tab_ref, c[0], c[1], False)
            st_ref[rows, :nc] = xr
            st_ref[rows, nc:] = xi
            return xr[SCAN_GROUP - 1:SCAN_GROUP, :], xi[SCAN_GROUP - 1:SCAN_GROUP, :]

        cr, ci = lax.fori_loop(0, SCAN_T // SCAN_GROUP, group, (carry[:, :nc], carry[:, nc:]), unroll=2)
        carry[:, :nc] = cr
        carry[:, nc:] = ci

        for b in range(SSM_SUPER):
            (rows_b, cols_re, st_re), (_, cols_im, st_im) = _super_blocks()[2 * b:2 * b + 2]
            y_ref[:, rows_b] = (_dot16(st_ref[:, st_re], wct_ref[rows_b, cols_re], NT)
                                + _dot16(st_ref[:, st_im], wct_ref[rows_b, cols_im], NT))

    const = lambda shape: pl.BlockSpec(shape, lambda i: (0,) * len(shape))
    return pl.pallas_call(
        body, name="s5_fwd", grid=(S // SCAN_T,),
        in_specs=[const((4, SCAN_GROUP, 2 * nc)), pl.BlockSpec((SCAN_T, SSM_W), lambda i: (i, u_cols[0] // SSM_W)),
                  const((SSM_W, 2 * SB_COLS)), const((SSM_W, 2 * SB_COLS))],
        out_specs=[pl.BlockSpec((SCAN_T, 2 * nc), lambda i: (i, 0)), pl.BlockSpec((SCAN_T, SSM_W), lambda i: (i, 0))],
        out_shape=[SDS((S, 2 * nc), F32), SDS((S, SSM_W), F32)],
        scratch_shapes=[pltpu.VMEM((1, 2 * nc), F32), pltpu.VMEM((SCAN_T, 2 * nc), F32)],
        compiler_params=_cparams(("arbitrary",)),
    )(tab, u_arr, w_b, w_ct)


def _s5_bwd(tab, dy, states, u_arr, u_cols, w_b, w_ct):
    n_blk = S // SCAN_T
    nc = N_CPLX

    def body(tab_ref, dy_ref, x_ref, u_ref, wb_ref, wct_ref, du_ref, dlam_ref, dwb_ref, dwct_ref,
             carry, acc, d_scr, g_ref):
        i = pl.program_id(0)

        @pl.when(i == 0)
        def _():
            carry[...] = jnp.zeros_like(carry)
            acc[...] = jnp.zeros_like(acc)
            dwb_ref[...] = jnp.zeros_like(dwb_ref)
            dwct_ref[...] = jnp.zeros_like(dwct_ref)

        for rows_b, cols_c, cols_s in _super_blocks():
            d_scr[:, cols_s] = _dot16(dy_ref[:, rows_b], wct_ref[rows_b, cols_c], NN)

        last_row = lax.broadcasted_iota(jnp.int32, (SCAN_GROUP, 1), 0) == SCAN_GROUP - 1
        d_ref = d_scr

        def group(j, c):
            cr, ci = c
            rows = pl.ds(pl.multiple_of((SCAN_T // SCAN_GROUP - 1 - j) * SCAN_GROUP, SCAN_GROUP), SCAN_GROUP)
            gr, gi = _group_scan(d_ref[rows, :nc], d_ref[rows, nc:], tab_ref, cr, ci, True)
            g_ref[rows, :nc] = gr
            g_ref[rows, nc:] = gi
            nr = jnp.where(last_row, jnp.broadcast_to(cr, (SCAN_GROUP, nc)), pltpu.roll(gr, SCAN_GROUP - 1, 0))
            ni = jnp.where(last_row, jnp.broadcast_to(ci, (SCAN_GROUP, nc)), pltpu.roll(gi, SCAN_GROUP - 1, 0))
            sr, si = x_ref[rows, :nc], x_ref[rows, nc:]
            acc[:, :nc] += nr * sr + ni * si
            acc[:, nc:] += ni * sr - nr * si
            return gr[0:1, :], gi[0:1, :]

        cr, ci = lax.fori_loop(0, SCAN_T // SCAN_GROUP, group, (carry[:, :nc], carry[:, nc:]), unroll=2)
        carry[:, :nc] = cr
        carry[:, nc:] = ci

        for b in range(SSM_SUPER):
            (rows_b, cols_re, st_re), (_, cols_im, st_im) = _super_blocks()[2 * b:2 * b + 2]
            du_ref[:, rows_b] = (_dot16(g_ref[:, st_re], wb_ref[rows_b, cols_re], NT)
                                 + _dot16(g_ref[:, st_im], wb_ref[rows_b, cols_im], NT))
            for cols_c, cols_s in ((cols_re, st_re), (cols_im, st_im)):
                dwb_ref[rows_b, cols_c] += _dot16(u_ref[:, rows_b], g_ref[:, cols_s], TN)
                dwct_ref[rows_b, cols_c] += _dot16(dy_ref[:, rows_b], x_ref[:, cols_s], TN)

        @pl.when(i == n_blk - 1)
        def _():
            dlam_ref[...] = jnp.sum(acc[...], axis=0, keepdims=True)

    const = lambda shape: pl.BlockSpec(shape, lambda i: (0,) * len(shape))
    rows = lambda width, col_block=0: pl.BlockSpec((SCAN_T, width), lambda i: (n_blk - 1 - i, col_block))
    maps = const((SSM_W, 2 * SB_COLS))
    return pl.pallas_call(
        body, name="s5_bwd", grid=(n_blk,),
        in_specs=[const((4, SCAN_GROUP, 2 * nc)), rows(SSM_W), rows(2 * nc), rows(SSM_W, u_cols[0] // SSM_W), maps, maps],
        out_specs=[rows(SSM_W), const((1, 2 * nc)), maps, maps],
        out_shape=[SDS((S, SSM_W), F32), SDS((1, 2 * nc), F32), SDS((SSM_W, 2 * SB_COLS), F32),
                   SDS((SSM_W, 2 * SB_COLS), F32)],
        scratch_shapes=[pltpu.VMEM((1, 2 * nc), F32), pltpu.VMEM((SCAN_GROUP, 2 * nc), F32),
                        pltpu.VMEM((SCAN_T, 2 * nc), F32), pltpu.VMEM((SCAN_T, 2 * nc), F32)],
        compiler_params=_cparams(("arbitrary",)),
    )(tab, dy, states, u_arr, w_b, w_ct)


def _ssm_prep(a_re, a_im, log_dt, b_re, b_im, c_re, c_im):
    lam = lax.complex(a_re, a_im)
    dt = jnp.exp(log_dt)[:, None]
    lam_bar = jnp.exp(lam * dt)
    b_bar = ((lam_bar - 1.0) / lam)[..., None] * lax.complex(b_re, b_im)
    lam_t = jnp.concatenate([jnp.real(lam_bar).reshape(HALF, LANES), jnp.imag(lam_bar).reshape(HALF, LANES)], axis=0)
    groups_per_super = SSM_GROUPS // SSM_SUPER
    on_diag = ((lax.broadcasted_iota(jnp.int32, (SSM_W, SB_COLS), 0) // SSM_GROUP) % groups_per_super
               == lax.broadcasted_iota(jnp.int32, (SSM_W, SB_COLS), 1) // SSM_STATE)

    repeat = (lax.broadcasted_iota(jnp.int32, (SSM_STATE, SB_COLS), 0)
              == lax.broadcasted_iota(jnp.int32, (SSM_STATE, SB_COLS), 1) % SSM_STATE).astype(F32)

    def compact(m):
        tiled = jnp.dot(m.reshape(SSM_W, SSM_STATE), repeat, precision=lax.Precision.HIGHEST)
        return jnp.where(on_diag, tiled, 0.0)

    w_b = jnp.concatenate([compact(jnp.real(b_bar).transpose(0, 2, 1)),
                           compact(jnp.imag(b_bar).transpose(0, 2, 1))], axis=1)
    w_ct = jnp.concatenate([compact(c_re), -compact(c_im)], axis=1)
    return lam_t, w_b, w_ct


U_SSM_COLS = (4 * ATT_W, SSM_W)


def _row(v):
    return v.reshape(1, -1)


def _even_fwd(x, h, tail, pre, post, w_in, late_w, glu_b, ssm_d, prep, tables):
    lam_t, w_b, w_ct, scan_fwd_tab, scan_bwd_tab = prep
    proj = _mm(h, w_in, "nn", F32, b_blocks=True)
    qkv = _qkv_prep(proj, tables[:3])
    att, lse = _flash_fwd(qkv, tables[3])
    w_out, glu_w = late_w(att)
    states, y = _s5_fwd(scan_fwd_tab, proj, U_SSM_COLS, w_b, w_ct)

    def act1(yv, uv, dv):
        return (_gelu_and_grad(yv + dv * uv)[0],), ()

    (z1,) = _rowwise(act1, [(y, SSM_W, 0), (proj, SSM_W, 8), (ssm_d, SSM_W, 0)], [(SSM_W, F32)], name="ssm_act_fwd")
    lin = _mm(z1, glu_w, "nn", F32)

    def gate(att_v, ga, gs, z1v, linv, bv):
        ssm_out = z1v * _sigmoid(linv + bv)
        return (jnp.concatenate([att_v * _silu_and_grad(ga)[0], ssm_out * _silu_and_grad(gs)[0]], axis=1),), ()

    (merged,) = _rowwise(gate, [(att, ATT_W, 0), (proj, ATT_W, 3), (proj, SSM_W, 9), (z1, SSM_W, 0),
                                (lin, SSM_W, 0), (glu_b, SSM_W, 0)], [(EVEN_OUT, BF16)], name="even_gate_fwd")
    yout = _mm(merged, w_out, "nn", F32)
    saved = (x, h, proj, qkv, att, lse, states, y, z1, lin, merged, yout, w_out, glu_w, scan_bwd_tab)
    return tail(x, yout, post) + (saved,)


def _even_bwd(g, saved, pre, post, w_in, late_w, glu_b, ssm_d, prep, tables, on_w, on_ssm):
    x, h, proj, qkv, att, lse, states, y, z1, lin, merged, yout, w_out, glu_w, scan_bwd_tab = saved
    lam_t, w_b, w_ct = prep[:3]
    dyout, dpost = _post_bwd(g, yout, post)
    dmerged = _mm(dyout, w_out, "nt", F32)
    dw_out = _mm(merged, dyout, "tn", BF16)

    def gate_bwd(dm_a, dm_s, att_v, ga, gs, z1v, linv, bv):
        sa, dsa = _silu_and_grad(ga)
        ss, dss = _silu_and_grad(gs)
        sig = _sigmoid(linv + bv)
        ssm_out = z1v * sig
        dssm = dm_s * ss
        dlin = dssm * z1v * sig * (1.0 - sig)
        return (dm_a * sa, dm_a * att_v * dsa, dm_s * ssm_out * dss, dssm * sig, dlin), (dlin,)

    datt, dg_att, dg_ssm, dz1a, dlin, dglu_b = _rowwise(
        gate_bwd, [(dmerged, ATT_W, 0), (dmerged, SSM_W, 2), (att, ATT_W, 0), (proj, ATT_W, 3), (proj, SSM_W, 9),
                   (z1, SSM_W, 0), (lin, SSM_W, 0), (glu_b, SSM_W, 0)],
        [(ATT_W, F32), (ATT_W, BF16), (SSM_W, BF16), (SSM_W, F32), (SSM_W, BF16)], [SSM_W], name="even_gate_bwd")
    dz1b = _mm(dlin, glu_w, "nt", F32)
    dglu_w = _mm(z1, dlin, "tn", BF16)

    def act1_bwd(da, db, yv, uv, dv):
        dpre = (da + db) * _gelu_and_grad(yv + dv * uv)[1]
        return (dpre, dpre * dv), (dpre * uv,)

    sent_late_w = on_w(dict(w_out=dw_out, glu_w=dglu_w))
    dy, du_direct, dd = _rowwise(act1_bwd, [(dz1a, SSM_W, 0), (dz1b, SSM_W, 0), (y, SSM_W, 0), (proj, SSM_W, 8),
                                            (ssm_d, SSM_W, 0)], [(SSM_W, BF16), (SSM_W, F32)], [SSM_W],
                                 name="ssm_act_bwd", after=(sent_late_w,))
    du_state, dlam_row, dw_b, dw_ct = _s5_bwd(scan_bwd_tab, dy, states, proj, U_SSM_COLS, w_b, w_ct)
    dlam = jnp.concatenate([dlam_row[0, :N_CPLX].reshape(HALF, LANES), dlam_row[0, N_CPLX:].reshape(HALF, LANES)],
                           axis=0)
    sent_ssm = on_ssm((dlam, dw_b, dw_ct))
    dq, dk, dv = _flash_bwd(qkv, att, datt, lse, tables[3], tables[:3],
                            after=() if sent_ssm is None else (sent_ssm,))

    def assemble(dqv, dkv, dvv, dga, dua, dub, dgs):
        return (jnp.concatenate([dqv, dkv, dvv, dga, (dua + dub).astype(BF16), dgs], axis=1),), ()

    (dproj,) = _rowwise(assemble, [(dq, ATT_W, 0), (dk, ATT_W, 0), (dv, ATT_W, 0), (dg_att, ATT_W, 0),
                                   (du_state, SSM_W, 0), (du_direct, SSM_W, 0), (dg_ssm, SSM_W, 0)],
                        [(EVEN_IN, BF16)], name="dproj_assemble")
    dw_in = _mm(h, dproj, "tn", BF16, out_blocks=True)
    sent = on_w(dict(w_in=dw_in))
    dh = _mm(dproj, w_in, "nt", F32, b_blocks=True, after=(sent,))
    g_prev, dpre = _pre_bwd(g, dh, x, pre)
    return g_prev, dict(pre=dpre, post=dpost, glu_b=dglu_b, ssm_d=dd)


def _odd_fwd(x, h, tail, pre, post, w_in, pool_w, pool_scale, w_out):
    proj = _mm(h, w_in, "nn", F32, b_blocks=True)
    mixed = _pool(proj, 0, False, BF16)
    ylin = _gmm(mixed, pool_w, "nn", F32)

    def gate(yl, gt, sc):
        return (yl * sc * _silu_and_grad(gt)[0],), ()

    (z,) = _rowwise(gate, [(ylin, POOL_W, 0), (proj, POOL_W, 1), (pool_scale, POOL_W, 0)], [(POOL_W, BF16)],
                    name="odd_gate_fwd")
    yout = _mm(z, w_out, "nn", F32)
    return tail(x, yout, post) + ((x, h, proj, mixed, ylin, z, yout),)


def _odd_bwd(g, saved, pre, post, w_in, pool_w, pool_scale, w_out, on_w):
    x, h, proj, mixed, ylin, z, yout = saved
    dyout, dpost = _post_bwd(g, yout, post)
    dz = _mm(dyout, w_out, "nt", F32)
    dw_out = _mm(z, dyout, "tn", BF16)

    def gate_bwd(dzv, yl, gt, sc):
        sg, dsg = _silu_and_grad(gt)
        tt = dzv * sg
        return (tt * sc, dzv * yl * sc * dsg), (tt * yl,)

    dylin, dproj_gate, dscale = _rowwise(gate_bwd, [(dz, POOL_W, 0), (ylin, POOL_W, 0), (proj, POOL_W, 1),
                                                    (pool_scale, POOL_W, 0)],
                                         [(POOL_W, BF16), (POOL_W, BF16, ODD_IN, 1)], [POOL_W], name="odd_gate_bwd")
    dmixed = _gmm(dylin, pool_w, "nt", F32)
    dpool_w = _gmm(mixed, dylin, "tn", BF16)
    dproj = _pool(dmixed, 0, True, BF16, into=dproj_gate)
    dw_in = _mm(h, dproj, "tn", BF16, out_blocks=True)
    sent = on_w(dict(w_in=dw_in, w_out=dw_out, pool_w=dpool_w))
    dh = _mm(dproj, w_in, "nt", F32, b_blocks=True, after=(sent,))
    g_prev, dpre = _pre_bwd(g, dh, x, pre)
    return g_prev, dict(pre=dpre, post=dpost, pool_scale=dscale)


def _my_index():
    return 4 * lax.axis_index("x") + 2 * lax.axis_index("y") + lax.axis_index("c")


HBM_SPEC = pl.BlockSpec(memory_space=pltpu.HBM)
SEM_SPEC = pl.BlockSpec(memory_space=pltpu.SEMAPHORE)
SPLIT_EFFECT = pltpu.SideEffectType.DATAFLOW_SIDE_EFFECTING


def _device_of(j):
    return (j // 4, (j // 2) % 2, j % 2)


def _split_copy(srcs, lands, send_sems, recv_sems, gather, i, j, dst_slot, recv_slot):
    return pltpu.make_async_remote_copy(
        src_ref=srcs[i] if gather else srcs[i].at[j], dst_ref=lands[i].at[dst_slot],
        send_sem=send_sems.at[i * N_DEV + j], recv_sem=recv_sems.at[i * N_DEV + recv_slot],
        device_id=_device_of(j), device_id_type=MESH_ID)


def _own_copy(srcs, lands, send_sems, gather, i, me):
    return pltpu.make_async_copy(srcs[i] if gather else srcs[i].at[me], lands[i].at[me], send_sems.at[i * N_DEV + me])


def _xchg_start(name, srcs, gather, after=()):
    n = len(srcs)
    n_in = n + len(after)

    def body(*refs):
        src_refs = refs[:n]
        send_sems, recv_sems, token = refs[n_in], refs[n_in + 1], refs[-1]
        land_refs = refs[n_in + 2 + n:n_in + 2 + 2 * n]
        me = _my_index()
        for j in range(N_DEV):
            @pl.when(me != j)
            def _(j=j):
                for i in range(n):
                    _split_copy(src_refs, land_refs, send_sems, recv_sems, gather, i, j, me, me).start()
        for i in range(n):
            _own_copy(src_refs, land_refs, send_sems, gather, i, me).start()
        token[...] = jnp.zeros_like(token)

    land_shapes = [((N_DEV,) + a.shape) if gather else a.shape for a in srcs]
    thru = ([pltpu.HBM(a.shape, a.dtype) for a in srcs] + [pltpu.HBM(s, a.dtype) for s, a in zip(land_shapes, srcs)])
    res = pl.pallas_call(
        body, name=name,
        out_shape=(pltpu.SemaphoreType.DMA((n * N_DEV,)), pltpu.SemaphoreType.DMA((n * N_DEV,)), *thru,
                   SDS((8, LANES), F32)),
        in_specs=[HBM_SPEC] * n + [pl.BlockSpec(memory_space=pl.ANY)] * len(after),
        out_specs=(SEM_SPEC, SEM_SPEC, *([HBM_SPEC] * (2 * n)), pl.BlockSpec(memory_space=pltpu.VMEM)),
        input_output_aliases={i: 2 + i for i in range(n)},
        compiler_params=pltpu.CompilerParams(has_side_effects=SPLIT_EFFECT),
    )(*[pltpu.with_memory_space_constraint(a, pltpu.HBM) for a in srcs], *after)
    return res[0], res[1], list(res[2:2 + n]), list(res[2 + n:2 + 2 * n]), res[-1]


def _xchg_wait(name, started, gather, after):
    send_sems, recv_sems, srcs, lands, _ = started
    n = len(srcs)

    def body(*refs):
        src_refs, land_refs = refs[:n], refs[n:2 * n]
        send_r, recv_r = refs[2 * n], refs[2 * n + 1]
        me = _my_index()
        for j in range(N_DEV):
            @pl.when(me != j)
            def _(j=j):
                for i in range(n):
                    _split_copy(src_refs, land_refs, send_r, recv_r, gather, i, j, me, me).wait_send()
                    _split_copy(src_refs, land_refs, send_r, recv_r, gather, i, j, j, j).wait_recv()
        for i in range(n):
            _own_copy(src_refs, land_refs, send_r, gather, i, me).wait()

    thru = [pltpu.HBM(a.shape, a.dtype) for a in list(srcs) + list(lands)]
    res = pl.pallas_call(
        body, name=name, out_shape=tuple(thru),
        in_specs=[HBM_SPEC] * (2 * n) + [SEM_SPEC, SEM_SPEC] + [pl.BlockSpec(memory_space=pl.ANY)] * len(after),
        out_specs=tuple([HBM_SPEC] * (2 * n)),
        input_output_aliases={i: i for i in range(2 * n)},
        compiler_params=pltpu.CompilerParams(has_side_effects=SPLIT_EFFECT),
    )(*srcs, *lands, send_sems, recv_sems, *after)
    return list(res[n:])


def _adam_layer(w, slots, m, v, layer, name, into=None):
    n_l, r, c = w.shape
    ns = slots.shape[0]
    tr = r
    while tr * c * 4 > (1 << 20) and tr % 16 == 0:
        tr //= 2
    assert r % tr == 0

    def body(w_ref, g_ref, m_ref, v_ref, *rest):
        go_ref, d_ref, mo_ref, vo_ref = rest[-4:]
        g = g_ref[0].astype(F32)
        for s in range(1, ns):
            g = g + g_ref[s].astype(F32)
        mn = ADAM_B1 * m_ref[...] + (1.0 - ADAM_B1) * g
        vn = ADAM_B2 * v_ref[...] + (1.0 - ADAM_B2) * (g * g)
        m_hat = mn / (1.0 - ADAM_B1 ** ADAM_STEP)
        v_hat = vn / (1.0 - ADAM_B2 ** ADAM_STEP)
        go_ref[...] = g
        d_ref[...] = -ADAM_LR * (m_hat / (jnp.sqrt(v_hat) + ADAM_EPS) + ADAM_WD * w_ref[...])
        mo_ref[...] = mn
        vo_ref[...] = vn

    blk = pl.BlockSpec((None, tr, c), lambda i: (layer, i, 0))
    earlier = () if into is None else tuple(into)
    return pl.pallas_call(
        body, name=name, grid=(r // tr,),
        in_specs=[blk, pl.BlockSpec((ns, tr, c), lambda i: (0, i, 0)), blk, blk]
        + [pl.BlockSpec(memory_space=pl.ANY)] * len(earlier),
        out_specs=[blk] * 4, out_shape=[SDS((n_l, r, c), F32)] * 4,
        input_output_aliases={4 + q: q for q in range(len(earlier))},
        compiler_params=_cparams(("arbitrary",)),
    )(*_in_hbm((w, slots, m, v)), *earlier)


def _adam(w, gslots, m, v, name):
    r, c = w.shape
    ns = gslots.shape[0]
    tr = r
    while tr * c * 4 > (1 << 20) and tr % 16 == 0:
        tr //= 2
    assert r % tr == 0

    def body(w_ref, g_ref, m_ref, v_ref, go_ref, d_ref, mo_ref, vo_ref):
        g = g_ref[0].astype(F32)
        for s in range(1, ns):
            g = g + g_ref[s].astype(F32)
        wv = w_ref[...]
        mn = ADAM_B1 * m_ref[...] + (1.0 - ADAM_B1) * g
        vn = ADAM_B2 * v_ref[...] + (1.0 - ADAM_B2) * (g * g)
        m_hat = mn / (1.0 - ADAM_B1 ** ADAM_STEP)
        v_hat = vn / (1.0 - ADAM_B2 ** ADAM_STEP)
        go_ref[...] = g
        d_ref[...] = -ADAM_LR * (m_hat / (jnp.sqrt(v_hat) + ADAM_EPS) + ADAM_WD * wv)
        mo_ref[...] = mn
        vo_ref[...] = vn

    blk = pl.BlockSpec((tr, c), lambda i: (i, 0))
    return pl.pallas_call(
        body, name=name, grid=(r // tr,),
        in_specs=[blk, pl.BlockSpec((ns, tr, c), lambda i: (0, i, 0)), blk, blk],
        out_specs=[blk] * 4, out_shape=[SDS((r, c), F32)] * 4,
        compiler_params=_cparams(("parallel",)),
    )(w, gslots, m, v)


def _sum_slots(slots, name):
    ns, r, c = slots.shape

    def body(g_ref, o_ref):
        g = g_ref[0]
        for s in range(1, ns):
            g = g + g_ref[s]
        o_ref[...] = g

    return pl.pallas_call(
        body, name=name, grid=(1,),
        in_specs=[pl.BlockSpec((ns, r, c), lambda i: (0, 0, 0))], out_specs=pl.BlockSpec((r, c), lambda i: (0, 0)),
        out_shape=SDS((r, c), F32), compiler_params=_cparams(("arbitrary",)),
    )(slots)


def _adam_params(params, name):
    n = len(params)

    def body(*refs):
        ins, outs = refs[:5 * n], refs[5 * n:]
        for p in range(n):
            w_ref, m_ref, v_ref, g_first, g_rest = ins[5 * p:5 * p + 5]
            go_ref, d_ref, mo_ref, vo_ref = outs[4 * p:4 * p + 4]
            for part, g_ref in ((slice(0, 1), g_first), (slice(1, w_ref.shape[0]), g_rest)):
                g = g_ref[...]
                mn = ADAM_B1 * m_ref[part] + (1.0 - ADAM_B1) * g
                vn = ADAM_B2 * v_ref[part] + (1.0 - ADAM_B2) * (g * g)
                m_hat = mn / (1.0 - ADAM_B1 ** ADAM_STEP)
                v_hat = vn / (1.0 - ADAM_B2 ** ADAM_STEP)
                go_ref[part] = g
                d_ref[part] = -ADAM_LR * (m_hat / (jnp.sqrt(v_hat) + ADAM_EPS) + ADAM_WD * w_ref[part])
                mo_ref[part] = mn
                vo_ref[part] = vn

    def whole(a):
        return pl.BlockSpec(a.shape, lambda i, nd=a.ndim: (0,) * nd)

    flat = _in_hbm([a for prm in params for a in prm])
    outs = pl.pallas_call(
        body, name=name, grid=(1,),
        in_specs=[whole(a) for a in flat],
        out_specs=[whole(prm[0]) for prm in params for _ in range(4)],
        out_shape=[SDS(prm[0].shape, F32) for prm in params for _ in range(4)],
        compiler_params=_cparams(("arbitrary",)),
    )(*flat)
    return [outs[4 * p:4 * p + 4] for p in range(n)]


SMALL_NAMES = ("pre_norm", "post_norm", "ssm_a_re", "ssm_a_im", "ssm_log_dt", "ssm_b_re", "ssm_b_im", "ssm_c_re",
               "ssm_c_im", "ssm_d", "ssm_glu_b")
SSM_NAMES = ("ssm_a_re", "ssm_a_im", "ssm_log_dt", "ssm_b_re", "ssm_b_im", "ssm_c_re", "ssm_c_im")
WEIGHT_ORDER = ("pre_norm", "post_norm", "even_w_in", "even_w_out", "ssm_a_re", "ssm_a_im", "ssm_log_dt", "ssm_b_re",
                "ssm_b_im", "ssm_c_re", "ssm_c_im", "ssm_d", "ssm_glu_w", "ssm_glu_b", "odd_w_in", "pool_w",
                "pool_scale", "odd_w_out")
PACK_ROWS_ALIGN = 8


def _pack(parts):
    flat = jnp.concatenate([p.reshape(-1).astype(F32) for p in parts])
    rows = -(-flat.shape[0] // (LANES * PACK_ROWS_ALIGN)) * PACK_ROWS_ALIGN
    return jnp.pad(flat, (0, rows * LANES - flat.shape[0])).reshape(rows, LANES)


def _unpack(packed, shapes):
    flat = packed.reshape(-1)
    out, off = [], 0
    for shp in shapes:
        size = math.prod(shp)
        out.append(flat[off:off + size].reshape(shp))
        off += size
    return out


EVEN_SHARDED = ("w_in", "w_out", "glu_w")
ODD_SHARDED = ("w_in", "pool_w", "w_out")
FAMILY = {(0, "w_in"): "even_w_in", (0, "w_out"): "even_w_out", (0, "glu_w"): "ssm_glu_w",
          (1, "w_in"): "odd_w_in", (1, "pool_w"): "pool_w", (1, "w_out"): "odd_w_out"}


def _sharded_keys(layer):
    return EVEN_SHARDED if layer % 2 == 0 else ODD_SHARDED


def _local_step(x, tgt, small, get_weights, on_w, on_ssm, on_grads, zero=0.0):
    tables = _rope_tables(zero) + (_attention_bias(zero),)
    preps, prep_vjps = [], []
    for i in range(2):
        out, vjp = jax.vjp(_ssm_prep, small["ssm_a_re"][i] + zero, small["ssm_a_im"][i], small["ssm_log_dt"][i],
                           small["ssm_b_re"][i], small["ssm_b_im"][i], small["ssm_c_re"][i], small["ssm_c_im"][i])
        preps.append(tuple(out) + _scan_tables(out[0]))
        prep_vjps.append(vjp)

    def layer_args(layer, wts):
        i = layer // 2
        pre, post = _row(small["pre_norm"][layer]) + wts.get("token", 0.0), _row(small["post_norm"][layer])
        if layer % 2 == 0:
            return (pre, post, wts["w_in"], wts["late"], _row(small["ssm_glu_b"][i]), _row(small["ssm_d"][i]),
                    preps[i], tables)
        return (pre, post, wts["w_in"], wts["pool_w"], _row(wts["pool_scale"]), wts["w_out"])

    saved, args = [], []
    cur = x
    for layer in range(4):
        after = (cur,) if layer else (cur, tables[0], tables[3]) + preps[0][1:] + preps[1][1:]
        args.append(layer_args(layer, get_weights(layer, after)))
        if layer == 0:
            h = _norm_fwd(cur, args[0][0])
        if layer < 3:
            def tail(xv, yv, post, next_gain=_row(small["pre_norm"][layer + 1])):
                return tuple(_post_fwd(xv, yv, post, next_gain))
        else:
            def tail(xv, yv, post):
                return tuple(_post_fwd_loss(xv, yv, post, tgt))
        cur, h, sv = (_even_fwd if layer % 2 == 0 else _odd_fwd)(cur, h, tail, *args[layer])
        saved.append(sv)
    g, sq = cur, h
    loss = 0.5 * jnp.sum(sq) / D

    lg = [None] * 4
    token = jnp.zeros((), F32)
    for layer in reversed(range(4)):
        largs = list(args[layer])
        largs[1] = largs[1] + token
        hooks = dict(on_w=functools.partial(on_w, layer))
        ssm_grads = []
        if layer % 2 == 0:
            def ssm_hook(cotangents, layer=layer):
                ssm_grads.append(prep_vjps[layer // 2](cotangents))
                return on_ssm(layer, ssm_grads[0])

            hooks["on_ssm"] = ssm_hook
        g, lg[layer] = (_even_bwd if layer % 2 == 0 else _odd_bwd)(g, saved[layer], *largs, **hooks)
        if ssm_grads:
            lg[layer]["ssm"] = ssm_grads[0]
        token = on_grads(layer, lg[layer])
    return loss, g, token


def _to_slots(key, gfull):
    if key == "w_in":
        return gfull
    if key in ("w_out", "glu_w"):
        rr, nn = gfull.shape
        return gfull.reshape(N_DEV, rr // N_DEV, nn)
    assert key == "pool_w"
    gg, rr, nn = gfull.shape
    return gfull.reshape(gg, N_DEV, rr // N_DEV, nn).transpose(1, 0, 2, 3)


def _from_gathered(key, gat):
    if key == "w_in":
        return gat
    if key in ("w_out", "glu_w"):
        _, rr, nn = gat.shape
        return gat.reshape(N_DEV * rr, nn)
    assert key == "pool_w"
    _, gg, rr, nn = gat.shape
    return gat.transpose(1, 0, 2, 3).reshape(gg, N_DEV * rr, nn)


def kernel(x, pre_norm, post_norm, even_w_in, even_w_out, ssm_a_re, ssm_a_im, ssm_log_dt, ssm_b_re, ssm_b_im, ssm_c_re, ssm_c_im, ssm_d, ssm_glu_w, ssm_glu_b, odd_w_in, pool_w, pool_scale, odd_w_out, loss_target, m_pre_norm, m_post_norm, m_even_w_in, m_even_w_out, m_ssm_a_re, m_ssm_a_im, m_ssm_log_dt, m_ssm_b_re, m_ssm_b_im, m_ssm_c_re, m_ssm_c_im, m_ssm_d, m_ssm_glu_w, m_ssm_glu_b, m_odd_w_in, m_pool_w, m_pool_scale, m_odd_w_out, v_pre_norm, v_post_norm, v_even_w_in, v_even_w_out, v_ssm_a_re, v_ssm_a_im, v_ssm_log_dt, v_ssm_b_re, v_ssm_b_im, v_ssm_c_re, v_ssm_c_im, v_ssm_d, v_ssm_glu_w, v_ssm_glu_b, v_odd_w_in, v_pool_w, v_pool_scale, v_odd_w_out):
    w = dict(pre_norm=pre_norm, post_norm=post_norm, even_w_in=even_w_in, even_w_out=even_w_out, ssm_a_re=ssm_a_re,
             ssm_a_im=ssm_a_im, ssm_log_dt=ssm_log_dt, ssm_b_re=ssm_b_re, ssm_b_im=ssm_b_im, ssm_c_re=ssm_c_re,
             ssm_c_im=ssm_c_im, ssm_d=ssm_d, ssm_glu_w=ssm_glu_w, ssm_glu_b=ssm_glu_b, odd_w_in=odd_w_in,
             pool_w=pool_w, pool_scale=pool_scale, odd_w_out=odd_w_out)
    mom = dict(pre_norm=m_pre_norm, post_norm=m_post_norm, even_w_in=m_even_w_in, even_w_out=m_even_w_out,
               ssm_a_re=m_ssm_a_re, ssm_a_im=m_ssm_a_im, ssm_log_dt=m_ssm_log_dt, ssm_b_re=m_ssm_b_re,
               ssm_b_im=m_ssm_b_im, ssm_c_re=m_ssm_c_re, ssm_c_im=m_ssm_c_im, ssm_d=m_ssm_d, ssm_glu_w=m_ssm_glu_w,
               ssm_glu_b=m_ssm_glu_b, odd_w_in=m_odd_w_in, pool_w=m_pool_w, pool_scale=m_pool_scale,
               odd_w_out=m_odd_w_out)
    var = dict(pre_norm=v_pre_norm, post_norm=v_post_norm, even_w_in=v_even_w_in, even_w_out=v_even_w_out,
               ssm_a_re=v_ssm_a_re, ssm_a_im=v_ssm_a_im, ssm_log_dt=v_ssm_log_dt, ssm_b_re=v_ssm_b_re,
               ssm_b_im=v_ssm_b_im, ssm_c_re=v_ssm_c_re, ssm_c_im=v_ssm_c_im, ssm_d=v_ssm_d, ssm_glu_w=v_ssm_glu_w,
               ssm_glu_b=v_ssm_glu_b, odd_w_in=v_odd_w_in, pool_w=v_pool_w, pool_scale=v_pool_scale,
               odd_w_out=v_odd_w_out)
    me = _my_index()
    scale_cols = pool_scale.shape[1]

    def shards_of(layer, keys):
        i = layer // 2
        shards = [w[FAMILY[(layer % 2, k)]][i].astype(BF16) for k in keys]
        if layer % 2 == 1:
            shards.append(jnp.pad(pool_scale[i][None], ((0, PACK_ROWS_ALIGN - 1), (0, 0))))
        return shards

    def start_gather(tag, after=()):
        return _xchg_start(f"gather_start_{tag}", shards[tag], True, after)

    shards = {0: shards_of(0, EVEN_SHARDED[:1]), "0_late": shards_of(0, EVEN_SHARDED[1:])}
    shards.update({layer: shards_of(layer, _sharded_keys(layer)) for layer in (1, 2, 3)})
    gather_started = {0: start_gather(0)}
    small = {nm: w[nm] for nm in SMALL_NAMES}
    packed_names = ("pre_norm", "post_norm") + SSM_NAMES + ("ssm_d", "ssm_glu_b")
    tails = {nm: (SSM_GROUPS, SSM_STATE * SSM_GROUP) if nm in ("ssm_b_re", "ssm_b_im") else w[nm].shape[1:]
             for nm in packed_names}
    dense = lambda nm, a: a.reshape((a.shape[0],) + tails[nm])
    small_operands = {nm: tuple(dense(nm, tree[nm]) for tree in (w, mom, var)) for nm in packed_names}
    early_work = [a for nm in ("ssm_b_re", "ssm_b_im") for a in small_operands[nm]]

    def get_weights(layer, after):
        keys = EVEN_SHARDED[:1] if layer == 0 else _sharded_keys(layer)
        if layer == 0:
            after = tuple(after) + tuple(early_work)
        lands = _xchg_wait(f"gather_wait_{layer}", gather_started[layer], True, after)
        wts = {k: _from_gathered(k, gat) for k, gat in zip(keys, lands)}
        if layer % 2 == 1:
            wts["pool_scale"] = lands[-1][:, 0, :].reshape(N_DEV * scale_cols)
        if layer == 0:
            prev = gather_started["0_late"] = start_gather("0_late", after=(lands[0],))
            for later in (1, 2, 3):
                prev = gather_started[later] = start_gather(later, after=(prev[4],))
            wts["token"] = prev[4][0, 0]

            def late(after_late):
                late_lands = _xchg_wait("gather_wait_0_late", gather_started["0_late"], True, (after_late,))
                return tuple(_from_gathered(k, gat) for k, gat in zip(EVEN_SHARDED[1:], late_lands))

            wts["late"] = late
        elif layer == 2:
            wts["late"] = lambda after_late: (wts["w_out"], wts["glu_w"])
        return wts

    scatter_started = []

    def on_w(layer, gw):
        keys = tuple(k for k in _sharded_keys(layer) if k in gw)
        started = _xchg_start(f"scatter_start_{layer}_{keys[0]}", [_to_slots(k, gw[k]) for k in keys], False)
        scatter_started.append((layer, keys, started))
        return started[4]

    def wait_scatters(layers, after):
        for layer, keys, started in scatter_started:
            if layer in layers:
                lands = _xchg_wait(f"scatter_wait_{layer}_{keys[0]}", started, False, after)
                for k, land in zip(keys, lands):
                    recv[(layer, k)] = land

    layer_grads = {}
    early_started, mid_started = [], []

    def on_ssm(layer, ssm_grads):
        if layer != 0:
            return None
        mid_started.append(_xchg_start("mid_start", [_pack(list(ssm_grads))], True))
        return mid_started[0][4]

    def on_grads(layer, lg):
        layer_grads[layer] = lg
        zero = jnp.zeros((), F32)
        if layer == 1:
            lgs = layer_grads
            early = ([jnp.concatenate([lgs[l][k] for l in (1, 2, 3)], axis=0) for k in ("pre", "post")]
                     + list(lgs[2]["ssm"]) + [lgs[2]["ssm_d"], lgs[2]["glu_b"],
                                              jnp.concatenate([lgs[1]["pool_scale"], lgs[3]["pool_scale"]], axis=0)])
            early_started.append(_xchg_start("small_start", [_pack(early)], True))
            zero = zero + early_started[0][4][0, 0]
        return zero

    loss_local, grad_x, token = _local_step(x[0], loss_target[0], small, get_weights, on_w, on_ssm, on_grads,
                                            zero=gather_started[0][4][0, 0])

    lg0 = layer_grads[0]
    late_started = _xchg_start("late_start", [_pack([lg0["pre"], lg0["post"], lg0["ssm_d"], lg0["glu_b"],
                                                     loss_local.reshape(1)]) + token], True)

    def adam_family(parity, k, which, into=None):
        nm = FAMILY[(parity, k)]
        cols = w[nm].shape[-1]
        return _adam_layer(w[nm].reshape(2, -1, cols), recv[(parity + 2 * which, k)].reshape(N_DEV, -1, cols),
                           mom[nm].reshape(2, -1, cols), var[nm].reshape(2, -1, cols), which,
                           f"adam_{nm}_{which}", into)

    recv, res = {}, {}
    wait_scatters((3, 2, 1), (late_started[4],))
    for k in ODD_SHARDED:
        res[FAMILY[(1, k)]] = adam_family(1, k, 1, adam_family(1, k, 0))
    half_done = {k: adam_family(0, k, 1) for k in EVEN_SHARDED}
    odd_done = tuple(half_done[k][0] for k in EVEN_SHARDED)

    (early_slots,) = _xchg_wait("small_wait", early_started[0], True, odd_done)
    (mid_slots,) = _xchg_wait("mid_wait", mid_started[0], True, odd_done)
    early_shapes = [(w[nm].shape[0] - 1,) + tails[nm] for nm in packed_names] + [(2, N_DEV * scale_cols)]
    g_early = _unpack(_sum_slots(early_slots, "sum_small_early"), early_shapes)
    g_mid = _unpack(_sum_slots(mid_slots, "sum_small_mid"), [(1,) + tails[nm] for nm in SSM_NAMES])

    (late_slots,) = _xchg_wait("late_wait", late_started, True, (g_early[0], g_mid[0]))
    wait_scatters((0,), (late_slots,))
    for k in EVEN_SHARDED:
        res[FAMILY[(0, k)]] = adam_family(0, k, 0, half_done[k])
    for nm in FAMILY.values():
        res[nm] = [o.reshape(w[nm].shape) for o in res[nm]]

    late_names = ("pre_norm", "post_norm", "ssm_d", "ssm_glu_b")
    g_late = _unpack(_sum_slots(late_slots, "sum_small_late"), [(1,) + tails[nm] for nm in late_names] + [(1,)])
    g_first = dict(zip(late_names, g_late))
    g_first.update(zip(SSM_NAMES, g_mid))
    outs = _adam_params([small_operands[nm] + (g_first[nm], g_early[j]) for j, nm in enumerate(packed_names)],
                        "adam_small")
    for nm, four in zip(packed_names, outs):
        res[nm] = [o.reshape(w[nm].shape) for o in four]
    loss = g_late[-1].reshape(())
    g_scale = lax.dynamic_slice_in_dim(g_early[-1], me * scale_cols, scale_cols, axis=1)
    pad = ((0, PACK_ROWS_ALIGN - 2), (0, 0))
    outs = _adam(jnp.pad(pool_scale, pad), jnp.pad(g_scale, pad)[None], jnp.pad(m_pool_scale, pad),
                 jnp.pad(v_pool_scale, pad), name="adam_pool_scale")
    res["pool_scale"] = [o[:2] for o in outs]

    out = [loss, grad_x[None]]
    for kind in range(4):
        out += [res[nm][kind] for nm in WEIGHT_ORDER]
    return tuple(out)
```

```python
import functools
import math

import jax
import jax.numpy as jnp
from jax import lax
from jax.experimental import pallas as pl
from jax.experimental.pallas import tpu as pltpu

F32 = jnp.float32
BF16 = jnp.bfloat16
SDS = jax.ShapeDtypeStruct

N_DEV = 8
S = 2048
D = 1024
HEAD_DIM = 64
ROT_DIM = 16
ROPE_THETA = 500000.0
ATT_W = 1024
SSM_W = 512
SSM_GROUPS = 32
SSM_GROUP = 16
SSM_STATE = 64
N_CPLX = SSM_GROUPS * SSM_STATE
POOL_W = 2048
POOL_GROUP = 512
EVEN_IN = 5120
EVEN_OUT = 1536
ODD_IN = 4096
RMS_EPS = 1e-6
LANES = 128
VMEM_LIMIT = 48 * 1024 * 1024

ADAM_LR = 0.001
ADAM_B1 = 0.9
ADAM_B2 = 0.999
ADAM_EPS = 1e-08
ADAM_WD = 0.01
ADAM_STEP = 10

MESH_ID = pl.DeviceIdType.MESH
NN = (((1,), (0,)), ((), ()))
NT = (((1,), (1,)), ((), ()))
TN = (((0,), (0,)), ((), ()))
_DN = {"nn": NN, "nt": NT, "tn": TN}


def _cparams(sem):
    return pltpu.CompilerParams(dimension_semantics=sem, vmem_limit_bytes=VMEM_LIMIT)


def _in_hbm(arrs):
    return [pltpu.with_memory_space_constraint(a, pltpu.HBM) for a in arrs]


MM_TILES = (1024, 768, 512)


def _tile(dim):
    return next((t for t in MM_TILES if dim % t == 0), dim)


BLOCK_PAIR = 2
MAX_WHOLE_K = 2048


def _mm(a, b, mode, out_dtype, b_blocks=False, out_blocks=False, after=()):
    if b_blocks:
        nblk, rows, cb = b.shape
        b2_shape = (rows, nblk * cb)
    else:
        b2_shape = b.shape
    if mode == "nn":
        (m, k), n = a.shape, b2_shape[1]
    elif mode == "nt":
        (m, k), n = a.shape, b2_shape[0]
    else:
        (k, m), n = a.shape, b2_shape[1]
    tm, tn, tk = _tile(m), _tile(n), _tile(k)
    if k <= MAX_WHOLE_K:
        tk = k
    if b_blocks and mode == "nn":
        tn = BLOCK_PAIR * cb
        if tn <= MM_TILES[0]:
            tm = m
    if b_blocks and mode == "nt":
        tk = BLOCK_PAIR * cb
    if out_blocks:
        cb = n // N_DEV
        tn = BLOCK_PAIR * cb
        tk = k
    nk = k // tk

    def body(a_ref, b_ref, *rest):
        o_ref, acc_ref = rest[-2:]
        kk = pl.program_id(2)
        bv = jnp.concatenate([b_ref[p] for p in range(BLOCK_PAIR)], axis=1) if b_blocks else b_ref[...]
        part = lax.dot_general(a_ref[...].astype(BF16), bv.astype(BF16), _DN[mode], preferred_element_type=F32)

        def write(res):
            if out_blocks:
                for p in range(BLOCK_PAIR):
                    o_ref[p] = res[:, p * cb:(p + 1) * cb].astype(o_ref.dtype)
            else:
                o_ref[...] = res.astype(o_ref.dtype)

        if nk == 1:
            write(part)
            return

        @pl.when(kk == 0)
        def _():
            acc_ref[...] = part

        @pl.when((kk > 0) & (kk < nk - 1))
        def _():
            acc_ref[...] += part

        @pl.when(kk == nk - 1)
        def _():
            write(acc_ref[...] + part)

    if mode == "nn":
        a_spec = pl.BlockSpec((tm, tk), lambda i, j, kk: (i, kk))
        b_spec = pl.BlockSpec((tk, tn), lambda i, j, kk: (kk, j))
    elif mode == "nt":
        a_spec = pl.BlockSpec((tm, tk), lambda i, j, kk: (i, kk))
        b_spec = pl.BlockSpec((tn, tk), lambda i, j, kk: (j, kk))
    else:
        a_spec = pl.BlockSpec((tk, tm), lambda i, j, kk: (kk, i))
        b_spec = pl.BlockSpec((tk, tn), lambda i, j, kk: (kk, j))
    if b_blocks and mode == "nn":
        b_spec = pl.BlockSpec((BLOCK_PAIR, tk, cb), lambda i, j, kk: (j, kk, 0))
    if b_blocks and mode == "nt":
        b_spec = pl.BlockSpec((BLOCK_PAIR, tn, cb), lambda i, j, kk: (kk, j, 0))
    out_spec = pl.BlockSpec((tm, tn), lambda i, j, kk: (i, j))
    out_shape = SDS((m, n), out_dtype)
    if out_blocks:
        out_spec = pl.BlockSpec((BLOCK_PAIR, tm, cb), lambda i, j, kk: (j, i, 0))
        out_shape = SDS((N_DEV, m, cb), out_dtype)
    return pl.pallas_call(
        body, name=f"mm_{mode}_{m}x{k}x{n}",
        grid=(m // tm, n // tn, nk),
        in_specs=[a_spec, b_spec] + [pl.BlockSpec(memory_space=pl.ANY)] * len(after),
        out_specs=out_spec,
        out_shape=out_shape,
        scratch_shapes=[pltpu.VMEM((tm, tn) if nk > 1 else (8, LANES), F32)],
        compiler_params=_cparams(("parallel", "parallel", "arbitrary")),
    )(a, b, *after)


def _gmm(a, b, mode, out_dtype, tm=S):
    ng, gw = POOL_W // POOL_GROUP, POOL_GROUP
    ns = S // tm
    if mode in ("nn", "nt"):
        def body(a_ref, b_ref, o_ref):
            o_ref[...] = lax.dot_general(a_ref[...].astype(BF16), b_ref[...].astype(BF16), _DN[mode],
                                         preferred_element_type=F32).astype(o_ref.dtype)

        return pl.pallas_call(
            body, name=f"gmm_{mode}", grid=(ng, ns),
            in_specs=[pl.BlockSpec((tm, gw), lambda g, i: (i, g)),
                      pl.BlockSpec((None, gw, gw), lambda g, i: (g, 0, 0))],
            out_specs=pl.BlockSpec((tm, gw), lambda g, i: (i, g)),
            out_shape=SDS((S, POOL_W), out_dtype),
            compiler_params=_cparams(("parallel", "parallel")),
        )(a, b)

    def body_tn(a_ref, b_ref, o_ref, acc_ref):
        i = pl.program_id(1)

        @pl.when(i == 0)
        def _():
            acc_ref[...] = jnp.zeros_like(acc_ref)

        acc_ref[...] += lax.dot_general(a_ref[...].astype(BF16), b_ref[...].astype(BF16), TN,
                                        preferred_element_type=F32)

        @pl.when(i == ns - 1)
        def _():
            o_ref[...] = acc_ref[...].astype(o_ref.dtype)

    return pl.pallas_call(
        body_tn, name="gmm_tn", grid=(ng, ns),
        in_specs=[pl.BlockSpec((tm, gw), lambda g, i: (i, g)),
                  pl.BlockSpec((tm, gw), lambda g, i: (i, g))],
        out_specs=pl.BlockSpec((None, gw, gw), lambda g, i: (g, 0, 0)),
        out_shape=SDS((ng, gw, gw), out_dtype),
        scratch_shapes=[pltpu.VMEM((gw, gw), F32)],
        compiler_params=_cparams(("parallel", "arbitrary")),
    )(a, b)


def _rowwise(fn, inputs, out_defs, acc_defs=(), tm=512, name=None, after=()):
    n_in, n_out, n_acc = len(inputs), len(out_defs), len(acc_defs)
    n_after = len(after)
    in_specs, args = [], []
    for arr, width, cb in inputs:
        if arr.shape[0] != S:
            in_specs.append(pl.BlockSpec((arr.shape[0], width), lambda i, cb=cb: (0, cb)))
        else:
            in_specs.append(pl.BlockSpec((tm, width), lambda i, cb=cb: (i, cb)))
        args.append(arr)
    out_defs = [d if len(d) == 4 else (d[0], d[1], d[0], 0) for d in out_defs]
    out_shape = [SDS((S, ww), dt) for _, dt, ww, _ in out_defs] + [SDS((1, w), F32) for w in acc_defs]
    out_specs = ([pl.BlockSpec((tm, w), lambda i, cb=cb: (i, cb)) for w, _, _, cb in out_defs]
                 + [pl.BlockSpec((1, w), lambda i: (0, 0)) for w in acc_defs])

    def kern(*refs):
        vals = [r[...] for r in refs[:n_in]]
        outs, accs = fn(*vals)
        out_refs = refs[n_in + n_after:]
        for r, v in zip(out_refs[:n_out], outs):
            r[...] = v.astype(r.dtype)
        if n_acc:
            acc_refs = out_refs[n_out:]

            @pl.when(pl.program_id(0) == 0)
            def _():
                for r in acc_refs:
                    r[...] = jnp.zeros_like(r)

            for r, v in zip(acc_refs, accs):
                r[...] += jnp.sum(v, axis=0, keepdims=True)

    res = pl.pallas_call(
        kern, name=name, grid=(S // tm,), in_specs=in_specs + [pl.BlockSpec(memory_space=pl.ANY)] * n_after,
        out_specs=out_specs, out_shape=out_shape, compiler_params=_cparams(("arbitrary",)),
    )(*args, *after)
    return res


def _sigmoid(x):
    return 1.0 / (1.0 + jnp.exp(-x))


def _silu_and_grad(x):
    s = _sigmoid(x)
    return x * s, s * (1.0 + x * (1.0 - s))


_GELU_K = math.sqrt(2.0 / math.pi)
_GELU_C = 0.044715


def _gelu_and_grad(x):
    t = jnp.tanh(_GELU_K * (x + _GELU_C * (x * x * x)))
    cdf = 0.5 * (1.0 + t)
    grad = cdf + 0.5 * x * (1.0 - t * t) * (_GELU_K * (1.0 + 3.0 * _GELU_C * x * x))
    return x * cdf, grad


def _rms(xv, gain):
    r = lax.rsqrt(jnp.mean(xv * xv, axis=-1, keepdims=True) + RMS_EPS)
    return xv * r * gain


def _rms_bwd(dout, xv, gain):
    r = lax.rsqrt(jnp.mean(xv * xv, axis=-1, keepdims=True) + RMS_EPS)
    xhat = xv * r
    dxhat = dout * gain
    dx = r * (dxhat - xhat * jnp.mean(dxhat * xhat, axis=-1, keepdims=True))
    return dx, dout * xhat


def _norm_fwd(x, gain):
    (h,) = _rowwise(lambda xv, g: ((_rms(xv, g),), ()), [(x, D, 0), (gain, D, 0)], [(D, BF16)], name="norm_fwd")
    return h


def _post_fwd(x, y, gain, next_gain):
    def fn(xv, yv, g, gn):
        out = xv + _rms(yv, g)
        return (out, _rms(out, gn)), ()

    return _rowwise(fn, [(x, D, 0), (y, D, 0), (gain, D, 0), (next_gain, D, 0)], [(D, F32), (D, BF16)],
                    name="post_fwd")


def _post_fwd_loss(x, y, gain, tgt):
    def fn(xv, yv, g, tv):
        e = xv + _rms(yv, g) - tv
        return (e * (1.0 / D),), (e * e,)

    return _rowwise(fn, [(x, D, 0), (y, D, 0), (gain, D, 0), (tgt, D, 0)], [(D, F32)], [D], name="post_fwd_loss")


def _post_bwd(g, y, gain):
    def fn(gv, yv, gn):
        dx, dg = _rms_bwd(gv, yv, gn)
        return (dx,), (dg,)

    return _rowwise(fn, [(g, D, 0), (y, D, 0), (gain, D, 0)], [(D, BF16)], [D], name="post_bwd")


def _pre_bwd(g, dh, x, gain):
    def fn(gv, dhv, xv, gn):
        dx, dg = _rms_bwd(dhv, xv, gn)
        return (gv + dx,), (dg,)

    return _rowwise(fn, [(g, D, 0), (dh, D, 0), (x, D, 0), (gain, D, 0)], [(D, F32)], [D], name="pre_bwd")


def _pool(u_arr, col_block, transpose, out_dtype, into=None, tc=256):
    n_t = POOL_W // tc
    per_group = POOL_GROUP // tc

    def body(u_ref, *rest):
        o_ref = rest[-1]
        grp = pl.program_id(0) // per_group
        t = lax.broadcasted_iota(jnp.int32, (S, 1), 0)
        for g in range(POOL_W // POOL_GROUP):
            @pl.when(grp == g)
            def _(g=g):
                xv = u_ref[...]
                cnt = jnp.minimum(t + 1, 2 << g).astype(F32)
                cur = xv / cnt if transpose else xv
                for k in (1, 2, 4, 8)[:g + 1]:
                    if transpose:
                        cur = cur + jnp.where(t < S - k, pltpu.roll(cur, S - k, 0), 0.0)
                    else:
                        cur = cur + jnp.where(t >= k, pltpu.roll(cur, k, 0), 0.0)
                res = cur - xv if transpose else cur / cnt - xv
                o_ref[...] = res.astype(o_ref.dtype)

    in_specs = [pl.BlockSpec((S, tc), lambda c: (0, col_block * n_t + c))]
    args = [u_arr]
    if into is not None:
        in_specs.append(pl.BlockSpec(memory_space=pl.ANY))
        args.append(into)
    return pl.pallas_call(
        body, name="pool_bwd" if transpose else "pool_fwd", grid=(n_t,),
        in_specs=in_specs,
        out_specs=pl.BlockSpec((S, tc), lambda c: (0, c)),
        out_shape=SDS((S, POOL_W) if into is None else into.shape, out_dtype),
        input_output_aliases={} if into is None else {1: 0},
        compiler_params=_cparams(("parallel",)),
    )(*args)


def _rope_tables(zero):
    pos = jnp.arange(S, dtype=jnp.int32).astype(F32) + zero
    inv_freq = ROPE_THETA ** (-jnp.arange(0, ROT_DIM, 2, dtype=F32) / ROT_DIM)
    ang = pos[:, None] * inv_freq[None, :]
    cos8, sin8 = jnp.cos(ang), jnp.sin(ang)
    half = ROT_DIM // 2
    zeros = jnp.zeros((S, HEAD_DIM - ROT_DIM), F32)
    cos = jnp.concatenate([cos8, cos8, jnp.ones((S, HEAD_DIM - ROT_DIM), F32)], axis=1)
    sin = jnp.concatenate([-sin8, sin8, zeros], axis=1)
    rep = LANES // HEAD_DIM
    lane = jnp.arange(LANES)
    dim = lane % HEAD_DIM
    partner = jnp.where(dim < half, lane + half, jnp.where(dim < ROT_DIM, lane - half, -1))
    swap = (lane[:, None] == partner[None, :]).astype(BF16)
    return jnp.tile(cos, (1, rep)), jnp.tile(sin, (1, rep)), swap


def _rotate(xv, cos, sin, swap, transpose):
    rep = xv.shape[1] // LANES
    wide = lambda tab: jnp.concatenate([tab] * rep, axis=1)
    xb = xv.astype(BF16)
    partner = jnp.concatenate([lax.dot_general(xb[:, t * LANES:(t + 1) * LANES], swap, NN, preferred_element_type=F32)
                               for t in range(rep)], axis=1)
    mixed = partner * wide(sin)
    return xv * wide(cos) - mixed if transpose else xv * wide(cos) + mixed


def _qkv_prep(proj, tables):
    cos, sin, swap = tables

    def fn(x, c, s, sw):
        rot = _rotate(x[:, :2 * ATT_W], c, s, sw, False)
        return (jnp.concatenate([(rot[:, :ATT_W] * HEAD_DIM ** -0.5).astype(BF16), rot[:, ATT_W:].astype(BF16),
                                 x[:, 2 * ATT_W:].astype(BF16)], axis=1),), ()

    (qkv,) = _rowwise(fn, [(proj, 3 * ATT_W, 0), (cos, LANES, 0), (sin, LANES, 0), (swap, LANES, 0)],
                      [(3 * ATT_W, BF16)], name="qkv_prep")
    return qkv


ATT_T = 512


def _multiplicity(delta):
    ok = delta >= 0
    near = jnp.where(ok & (delta <= 128), 1.0, 0.0)
    mid = jnp.where(ok & (delta <= 512) & ((delta & 3) == 0), 1.0, 0.0)
    far = jnp.where(ok & ((delta & 15) == 0), 1.0, 0.0)
    return near + mid + far


def _attention_bias(zero):
    t = ATT_T
    pos = jnp.arange(t, dtype=jnp.int32) + jnp.asarray(zero).astype(jnp.int32)
    delta = jnp.arange(S // t, dtype=jnp.int32)[:, None, None] * t + pos[None, :, None] - pos[None, None, :]
    mult = _multiplicity(delta)
    return jnp.where(mult > 0.0, jnp.log(jnp.maximum(mult, 1.0)), -1e30).astype(F32)


def _head_split(v, first):
    zero = jnp.zeros_like(v)
    return [jnp.where(first, v, zero), jnp.where(first, zero, v)]


def _flash_fwd(qkv, bias):
    t = ATT_T
    n_hp = ATT_W // LANES

    def body(q_ref, k_ref, v_ref, b_ref, o_ref, lse_ref):
        i = pl.program_id(1)
        first = lax.broadcasted_iota(jnp.int32, (1, LANES), 1) < HEAD_DIM
        qs = _head_split(q_ref[...], first)

        def kv_step(j, carry):
            m0, l0, m1, l1, acc = carry
            off = pl.multiple_of(j * t, t)
            kb = k_ref[pl.ds(off, t), :]
            vs = _head_split(v_ref[pl.ds(off, t), :], first)
            bias_t = b_ref[i - j]
            new = []
            pv = None
            for h, (m_prev, l_prev) in enumerate(((m0, l0), (m1, l1))):
                s = lax.dot_general(qs[h], kb, NT, preferred_element_type=F32) + bias_t
                m_new = jnp.maximum(m_prev, jnp.max(s, axis=1, keepdims=True))
                p = jnp.exp(s - m_new)
                alpha = jnp.exp(m_prev - m_new)
                l_new = alpha * l_prev + jnp.sum(p, axis=1, keepdims=True)
                d = lax.dot_general(p.astype(BF16), vs[h], NN, preferred_element_type=F32)
                pv = d if pv is None else pv + d
                new.append((m_new, l_new, alpha))
            acc = acc * jnp.where(first, new[0][2], new[1][2]) + pv
            return new[0][0], new[0][1], new[1][0], new[1][1], acc

        neg = jnp.full((t, 1), -1e30, F32)
        zero = jnp.zeros((t, 1), F32)
        m0, l0, m1, l1, acc = lax.fori_loop(0, i + 1, kv_step, (neg, zero, neg, zero, jnp.zeros((t, LANES), F32)))
        o_ref[...] = acc * jnp.where(first, 1.0 / l0, 1.0 / l1)
        lse_ref[...] = jnp.where(first, m0 + jnp.log(l0), m1 + jnp.log(l1))

    blk = pl.BlockSpec((t, LANES), lambda hp, i: (i, hp))
    k_full = pl.BlockSpec((S, LANES), lambda hp, i: (0, n_hp + hp))
    v_full = pl.BlockSpec((S, LANES), lambda hp, i: (0, 2 * n_hp + hp))
    return pl.pallas_call(
        body, name="flash_fwd", grid=(n_hp, S // t),
        in_specs=[blk, k_full, v_full, pl.BlockSpec((S // t, t, t), lambda hp, i: (0, 0, 0))], out_specs=[blk, blk],
        out_shape=[SDS((S, ATT_W), F32), SDS((S, ATT_W), F32)],
        compiler_params=_cparams(("parallel", "arbitrary")),
    )(qkv, qkv, qkv, bias)


def _flash_bwd(qkv, o, do, lse, bias, after=()):
    t = ATT_T
    n_hp = ATT_W // LANES
    n_t = S // t

    def body(q_ref, k_ref, v_ref, o_ref, do_ref, lse_ref, b_ref, *rest):
        dq_ref, dk_ref, dv_ref = rest[-3:]
        j = pl.program_id(1)
        first = lax.broadcasted_iota(jnp.int32, (1, LANES), 1) < HEAD_DIM

        @pl.when(j == 0)
        def _():
            dq_ref[...] = jnp.zeros_like(dq_ref)

        kb = k_ref[...]
        vb = v_ref[...]
        ks = _head_split(kb, first)

        def q_step(i, carry):
            dk_acc, dv_acc = carry
            rows = pl.ds(pl.multiple_of(i * t, t), t)
            qs = _head_split(q_ref[rows, :], first)
            dob = do_ref[rows, :]
            prod = dob * o_ref[rows, :]
            d_all = jnp.sum(prod, axis=1, keepdims=True)
            d0 = jnp.sum(jnp.where(first, prod, 0.0), axis=1, keepdims=True)
            lse_b = lse_ref[rows, :]
            lse0 = jnp.max(jnp.where(first, lse_b, -jnp.inf), axis=1, keepdims=True)
            lse1 = jnp.max(jnp.where(first, -jnp.inf, lse_b), axis=1, keepdims=True)
            dos = _head_split(dob.astype(BF16), first)
            bias_t = b_ref[i - j]
            dq_t = jnp.zeros((t, LANES), F32)
            for h, (lse_h, d_h) in enumerate(((lse0, d0), (lse1, d_all - d0))):
                s = lax.dot_general(qs[h], kb, NT, preferred_element_type=F32)
                p = jnp.exp(s + (bias_t - lse_h))
                dp = lax.dot_general(dos[h], vb, NT, preferred_element_type=F32)
                ds = (p * (dp - d_h)).astype(BF16)
                dv_acc = dv_acc + lax.dot_general(p.astype(BF16), dos[h], TN, preferred_element_type=F32)
                dk_acc = dk_acc + lax.dot_general(ds, qs[h], TN, preferred_element_type=F32)
                dq_t = dq_t + lax.dot_general(ds, ks[h], NN, preferred_element_type=F32)
            dq_ref[rows, :] += dq_t
            return dk_acc, dv_acc

        zero = jnp.zeros((t, LANES), F32)
        dk_acc, dv_acc = lax.fori_loop(j, n_t, q_step, (zero, zero))
        dk_ref[...] = dk_acc
        dv_ref[...] = dv_acc

    blk = pl.BlockSpec((t, LANES), lambda hp, j: (j, hp))
    full = pl.BlockSpec((S, LANES), lambda hp, j: (0, hp))
    k_blk = pl.BlockSpec((t, LANES), lambda hp, j: (j, n_hp + hp))
    v_blk = pl.BlockSpec((t, LANES), lambda hp, j: (j, 2 * n_hp + hp))
    return pl.pallas_call(
        body, name="flash_bwd", grid=(n_hp, n_t),
        in_specs=([full, k_blk, v_blk, full, full, full, pl.BlockSpec((n_t, t, t), lambda hp, j: (0, 0, 0))]
                  + [pl.BlockSpec(memory_space=pl.ANY)] * len(after)),
        out_specs=[full, blk, blk],
        out_shape=[SDS((S, ATT_W), F32)] * 3,
        compiler_params=_cparams(("parallel", "arbitrary")),
    )(qkv, qkv, qkv, o, do, lse, bias, *after)


SCAN_T = 256
SCAN_GROUP = 8
SCAN_STEPS = (1, 2, 4)
ST_ROWS = 2 * N_CPLX // LANES
HALF = ST_ROWS // 2


def _scan_tables(lam_t):
    lam = lax.complex(lam_t[:HALF].reshape(N_CPLX), lam_t[HALF:].reshape(N_CPLX))
    pows = jnp.cumprod(jnp.broadcast_to(lam, (SCAN_GROUP, N_CPLX)), axis=0)
    shifts = jnp.asarray(SCAN_STEPS)
    sub = jnp.arange(SCAN_GROUP)[None, :, None]
    steps = pows[shifts - 1][:, None, :]
    fwd = jnp.concatenate([jnp.where(sub >= shifts[:, None, None], steps, 0.0), pows[None]], axis=0)
    bwd = jnp.concatenate([jnp.where(sub <= SCAN_GROUP - 1 - shifts[:, None, None], jnp.conj(steps), 0.0),
                           jnp.conj(pows)[None, ::-1]], axis=0)

    def pack(tabs):
        return jnp.concatenate([jnp.real(tabs), jnp.imag(tabs)], axis=-1).astype(F32)

    return pack(fwd), pack(bwd)


def _cmul_add(xr, xi, lr, li, sr, si):
    return xr + lr * sr - li * si, xi + lr * si + li * sr


def _group_scan(xr, xi, tab_ref, cr, ci, reverse):
    for j, k in enumerate(SCAN_STEPS):
        shift = SCAN_GROUP - k if reverse else k
        xr, xi = _cmul_add(xr, xi, tab_ref[j, :, :N_CPLX], tab_ref[j, :, N_CPLX:],
                           pltpu.roll(xr, shift, 0), pltpu.roll(xi, shift, 0))
    return _cmul_add(xr, xi, tab_ref[3, :, :N_CPLX], tab_ref[3, :, N_CPLX:],
                     jnp.broadcast_to(cr, (SCAN_GROUP, N_CPLX)), jnp.broadcast_to(ci, (SCAN_GROUP, N_CPLX)))


SSM_SUPER = 4
SB_ROWS = SSM_W // SSM_SUPER
SB_COLS = N_CPLX // SSM_SUPER


def _super_blocks():
    return [(slice(b * SB_ROWS, (b + 1) * SB_ROWS), slice(h * SB_COLS, (h + 1) * SB_COLS),
             slice(h * N_CPLX + b * SB_COLS, h * N_CPLX + (b + 1) * SB_COLS))
            for b in range(SSM_SUPER) for h in range(2)]


def _dot16(a, b, dims):
    return lax.dot_general(a.astype(BF16), b.astype(BF16), dims, preferred_element_type=F32)


def _s5_fwd(tab, u_arr, u_cols, w_b, w_ct):
    nc = N_CPLX

    def body(tab_ref, u_ref, wb_ref, wct_ref, st_ref, y_ref, carry, bu_scr):
        @pl.when(pl.program_id(0) == 0)
        def _():
            carry[...] = jnp.zeros_like(carry)

        for rows_b, cols_c, cols_s in _super_blocks():
            bu_scr[:, cols_s] = _dot16(u_ref[:, rows_b], wb_ref[rows_b, cols_c], NN)

        def group(a, c):
            rows = pl.ds(pl.multiple_of(a * SCAN_GROUP, SCAN_GROUP), SCAN_GROUP)
            xr, xi = _group_scan(bu_scr[rows, :nc], bu_scr[rows, nc:], tab_ref, c[0], c[1], False)
            st_ref[rows, :nc] = xr
            st_ref[rows, nc:] = xi
            return xr[SCAN_GROUP - 1:SCAN_GROUP, :], xi[SCAN_GROUP - 1:SCAN_GROUP, :]

        cr, ci = lax.fori_loop(0, SCAN_T // SCAN_GROUP, group, (carry[:, :nc], carry[:, nc:]), unroll=2)
        carry[:, :nc] = cr
        carry[:, nc:] = ci

        for b in range(SSM_SUPER):
            (rows_b, cols_re, st_re), (_, cols_im, st_im) = _super_blocks()[2 * b:2 * b + 2]
            y_ref[:, rows_b] = (_dot16(st_ref[:, st_re], wct_ref[rows_b, cols_re], NT)
                                + _dot16(st_ref[:, st_im], wct_ref[rows_b, cols_im], NT))

    const = lambda shape: pl.BlockSpec(shape, lambda i: (0,) * len(shape))
    return pl.pallas_call(
        body, name="s5_fwd", grid=(S // SCAN_T,),
        in_specs=[const((4, SCAN_GROUP, 2 * nc)), pl.BlockSpec((SCAN_T, SSM_W), lambda i: (i, u_cols[0] // SSM_W)),
                  const((SSM_W, 2 * SB_COLS)), const((SSM_W, 2 * SB_COLS))],
        out_specs=[pl.BlockSpec((SCAN_T, 2 * nc), lambda i: (i, 0)), pl.BlockSpec((SCAN_T, SSM_W), lambda i: (i, 0))],
        out_shape=[SDS((S, 2 * nc), F32), SDS((S, SSM_W), F32)],
        scratch_shapes=[pltpu.VMEM((1, 2 * nc), F32), pltpu.VMEM((SCAN_T, 2 * nc), F32)],
        compiler_params=_cparams(("arbitrary",)),
    )(tab, u_arr, w_b, w_ct)


def _s5_bwd(tab, dy, states, u_arr, u_cols, w_b, w_ct):
    n_blk = S // SCAN_T
    nc = N_CPLX

    def body(tab_ref, dy_ref, x_ref, u_ref, wb_ref, wct_ref, du_ref, dlam_ref, dwb_ref, dwct_ref,
             carry, acc, d_scr, g_ref):
        i = pl.program_id(0)

        @pl.when(i == 0)
        def _():
            carry[...] = jnp.zeros_like(carry)
            acc[...] = jnp.zeros_like(acc)
            dwb_ref[...] = jnp.zeros_like(dwb_ref)
            dwct_ref[...] = jnp.zeros_like(dwct_ref)

        for rows_b, cols_c, cols_s in _super_blocks():
            d_scr[:, cols_s] = _dot16(dy_ref[:, rows_b], wct_ref[rows_b, cols_c], NN)

        last_row = lax.broadcasted_iota(jnp.int32, (SCAN_GROUP, 1), 0) == SCAN_GROUP - 1
        d_ref = d_scr

        def group(j, c):
            cr, ci = c
            rows = pl.ds(pl.multiple_of((SCAN_T // SCAN_GROUP - 1 - j) * SCAN_GROUP, SCAN_GROUP), SCAN_GROUP)
            gr, gi = _group_scan(d_ref[rows, :nc], d_ref[rows, nc:], tab_ref, cr, ci, True)
            g_ref[rows, :nc] = gr
            g_ref[rows, nc:] = gi
            nr = jnp.where(last_row, jnp.broadcast_to(cr, (SCAN_GROUP, nc)), pltpu.roll(gr, SCAN_GROUP - 1, 0))
            ni = jnp.where(last_row, jnp.broadcast_to(ci, (SCAN_GROUP, nc)), pltpu.roll(gi, SCAN_GROUP - 1, 0))
            sr, si = x_ref[rows, :nc], x_ref[rows, nc:]
            acc[:, :nc] += nr * sr + ni * si
            acc[:, nc:] += ni * sr - nr * si
            return gr[0:1, :], gi[0:1, :]

        cr, ci = lax.fori_loop(0, SCAN_T // SCAN_GROUP, group, (carry[:, :nc], carry[:, nc:]), unroll=2)
        carry[:, :nc] = cr
        carry[:, nc:] = ci

        for b in range(SSM_SUPER):
            (rows_b, cols_re, st_re), (_, cols_im, st_im) = _super_blocks()[2 * b:2 * b + 2]
            du_ref[:, rows_b] = (_dot16(g_ref[:, st_re], wb_ref[rows_b, cols_re], NT)
                                 + _dot16(g_ref[:, st_im], wb_ref[rows_b, cols_im], NT))
            for cols_c, cols_s in ((cols_re, st_re), (cols_im, st_im)):
                dwb_ref[rows_b, cols_c] += _dot16(u_ref[:, rows_b], g_ref[:, cols_s], TN)
                dwct_ref[rows_b, cols_c] += _dot16(dy_ref[:, rows_b], x_ref[:, cols_s], TN)

        @pl.when(i == n_blk - 1)
        def _():
            dlam_ref[...] = jnp.sum(acc[...], axis=0, keepdims=True)

    const = lambda shape: pl.BlockSpec(shape, lambda i: (0,) * len(shape))
    rows = lambda width, col_block=0: pl.BlockSpec((SCAN_T, width), lambda i: (n_blk - 1 - i, col_block))
    maps = const((SSM_W, 2 * SB_COLS))
    return pl.pallas_call(
        body, name="s5_bwd", grid=(n_blk,),
        in_specs=[const((4, SCAN_GROUP, 2 * nc)), rows(SSM_W), rows(2 * nc), rows(SSM_W, u_cols[0] // SSM_W), maps, maps],
        out_specs=[rows(SSM_W), const((1, 2 * nc)), maps, maps],
        out_shape=[SDS((S, SSM_W), F32), SDS((1, 2 * nc), F32), SDS((SSM_W, 2 * SB_COLS), F32),
                   SDS((SSM_W, 2 * SB_COLS), F32)],
        scratch_shapes=[pltpu.VMEM((1, 2 * nc), F32), pltpu.VMEM((SCAN_GROUP, 2 * nc), F32),
                        pltpu.VMEM((SCAN_T, 2 * nc), F32), pltpu.VMEM((SCAN_T, 2 * nc), F32)],
        compiler_params=_cparams(("arbitrary",)),
    )(tab, dy, states, u_arr, w_b, w_ct)


def _ssm_prep(a_re, a_im, log_dt, b_re, b_im, c_re, c_im):
    lam = lax.complex(a_re, a_im)
    dt = jnp.exp(log_dt)[:, None]
    lam_bar = jnp.exp(lam * dt)
    b_bar = ((lam_bar - 1.0) / lam)[..., None] * lax.complex(b_re, b_im)
    lam_t = jnp.concatenate([jnp.real(lam_bar).reshape(HALF, LANES), jnp.imag(lam_bar).reshape(HALF, LANES)], axis=0)
    groups_per_super = SSM_GROUPS // SSM_SUPER
    on_diag = ((lax.broadcasted_iota(jnp.int32, (SSM_W, SB_COLS), 0) // SSM_GROUP) % groups_per_super
               == lax.broadcasted_iota(jnp.int32, (SSM_W, SB_COLS), 1) // SSM_STATE)

    repeat = (lax.broadcasted_iota(jnp.int32, (SSM_STATE, SB_COLS), 0)
              == lax.broadcasted_iota(jnp.int32, (SSM_STATE, SB_COLS), 1) % SSM_STATE).astype(F32)

    def compact(m):
        tiled = jnp.dot(m.reshape(SSM_W, SSM_STATE), repeat, precision=lax.Precision.HIGHEST)
        return jnp.where(on_diag, tiled, 0.0)

    w_b = jnp.concatenate([compact(jnp.real(b_bar).transpose(0, 2, 1)),
                           compact(jnp.imag(b_bar).transpose(0, 2, 1))], axis=1)
    w_ct = jnp.concatenate([compact(c_re), -compact(c_im)], axis=1)
    return lam_t, w_b, w_ct


U_SSM_COLS = (4 * ATT_W, SSM_W)


def _row(v):
    return v.reshape(1, -1)


def _even_fwd(x, h, tail, pre, post, w_in, late_w, glu_b, ssm_d, prep, tables):
    lam_t, w_b, w_ct, scan_fwd_tab, scan_bwd_tab = prep
    proj = _mm(h, w_in, "nn", F32, b_blocks=True)
    qkv = _qkv_prep(proj, tables[:3])
    att, lse = _flash_fwd(qkv, tables[3])
    w_out, glu_w = late_w(att)
    states, y = _s5_fwd(scan_fwd_tab, proj, U_SSM_COLS, w_b, w_ct)

    def gate(yv, uv, dv, att_v, ga, gs, gw, bv):
        z1v = _gelu_and_grad(yv + dv * uv)[0]
        linv = _dot16(z1v, gw, NN)
        ssm_out = z1v * _sigmoid(linv + bv)
        merged_v = jnp.concatenate([att_v * _silu_and_grad(ga)[0], ssm_out * _silu_and_grad(gs)[0]], axis=1)
        return (z1v, linv, merged_v), ()

    z1, lin, merged = _rowwise(gate, [(y, SSM_W, 0), (proj, SSM_W, 8), (ssm_d, SSM_W, 0), (att, ATT_W, 0),
                                      (proj, ATT_W, 3), (proj, SSM_W, 9), (glu_w, SSM_W, 0), (glu_b, SSM_W, 0)],
                               [(SSM_W, F32), (SSM_W, F32), (EVEN_OUT, BF16)], name="even_gate_fwd")
    yout = _mm(merged, w_out, "nn", F32)
    saved = (x, h, proj, qkv, att, lse, states, y, z1, lin, merged, yout, w_out, glu_w, scan_bwd_tab)
    return tail(x, yout, post) + (saved,)


def _even_bwd(g, saved, pre, post, w_in, late_w, glu_b, ssm_d, prep, tables, on_w, on_ssm):
    x, h, proj, qkv, att, lse, states, y, z1, lin, merged, yout, w_out, glu_w, scan_bwd_tab = saved
    lam_t, w_b, w_ct = prep[:3]
    dyout, dpost = _post_bwd(g, yout, post)
    dmerged = _mm(dyout, w_out, "nt", F32)
    dw_out = _mm(merged, dyout, "tn", BF16)

    def gate_bwd(dm_a, dm_s, att_v, ga, gs, z1v, linv, bv):
        sa, dsa = _silu_and_grad(ga)
        ss, dss = _silu_and_grad(gs)
        sig = _sigmoid(linv + bv)
        ssm_out = z1v * sig
        dssm = dm_s * ss
        dlin = dssm * z1v * sig * (1.0 - sig)
        return (dm_a * sa, dm_a * att_v * dsa, dm_s * ssm_out * dss, dssm * sig, dlin), (dlin,)

    datt, dg_att, dg_ssm, dz1a, dlin, dglu_b = _rowwise(
        gate_bwd, [(dmerged, ATT_W, 0), (dmerged, SSM_W, 2), (att, ATT_W, 0), (proj, ATT_W, 3), (proj, SSM_W, 9),
                   (z1, SSM_W, 0), (lin, SSM_W, 0), (glu_b, SSM_W, 0)],
        [(ATT_W, F32), (ATT_W, BF16), (SSM_W, BF16), (SSM_W, F32), (SSM_W, BF16)], [SSM_W], name="even_gate_bwd")
    dz1b = _mm(dlin, glu_w, "nt", F32)
    dglu_w = _mm(z1, dlin, "tn", BF16)

    def act1_bwd(da, db, yv, uv, dv):
        dpre = (da + db) * _gelu_and_grad(yv + dv * uv)[1]
        return (dpre, dpre * dv), (dpre * uv,)

    sent_late_w = on_w(dict(w_out=dw_out, glu_w=dglu_w))
    dy, du_direct, dd = _rowwise(act1_bwd, [(dz1a, SSM_W, 0), (dz1b, SSM_W, 0), (y, SSM_W, 0), (proj, SSM_W, 8),
                                            (ssm_d, SSM_W, 0)], [(SSM_W, BF16), (SSM_W, F32)], [SSM_W],
                                 name="ssm_act_bwd", after=(sent_late_w,))
    du_state, dlam_row, dw_b, dw_ct = _s5_bwd(scan_bwd_tab, dy, states, proj, U_SSM_COLS, w_b, w_ct)
    dlam = jnp.concatenate([dlam_row[0, :N_CPLX].reshape(HALF, LANES), dlam_row[0, N_CPLX:].reshape(HALF, LANES)],
                           axis=0)
    sent_ssm = on_ssm((dlam, dw_b, dw_ct))
    dq, dk, dv = _flash_bwd(qkv, att, datt, lse, tables[3], after=() if sent_ssm is None else (sent_ssm,))

    def assemble(dqv, dkv, dvv, dga, dua, dub, dgs, c, s, sw):
        rot = _rotate(jnp.concatenate([dqv, dkv], axis=1), c, s, sw, True)
        return (jnp.concatenate([(rot[:, :ATT_W] * HEAD_DIM ** -0.5).astype(BF16), rot[:, ATT_W:].astype(BF16),
                                 dvv.astype(BF16), dga, (dua + dub).astype(BF16), dgs], axis=1),), ()

    (dproj,) = _rowwise(assemble, [(dq, ATT_W, 0), (dk, ATT_W, 0), (dv, ATT_W, 0), (dg_att, ATT_W, 0),
                                   (du_state, SSM_W, 0), (du_direct, SSM_W, 0), (dg_ssm, SSM_W, 0),
                                   (tables[0], LANES, 0), (tables[1], LANES, 0), (tables[2], LANES, 0)],
                        [(EVEN_IN, BF16)], name="dproj_assemble")
    dw_in = _mm(h, dproj, "tn", BF16, out_blocks=True)
    sent = on_w(dict(w_in=dw_in))
    dh = _mm(dproj, w_in, "nt", F32, b_blocks=True, after=(sent,))
    g_prev, dpre = _pre_bwd(g, dh, x, pre)
    return g_prev, dict(pre=dpre, post=dpost, glu_b=dglu_b, ssm_d=dd)


def _odd_fwd(x, h, tail, pre, post, w_in, pool_w, pool_scale, w_out):
    proj = _mm(h, w_in, "nn", F32, b_blocks=True)
    mixed = _pool(proj, 0, False, BF16)
    ylin = _gmm(mixed, pool_w, "nn", F32)

    def gate(yl, gt, sc):
        return (yl * sc * _silu_and_grad(gt)[0],), ()

    (z,) = _rowwise(gate, [(ylin, POOL_W, 0), (proj, POOL_W, 1), (pool_scale, POOL_W, 0)], [(POOL_W, BF16)],
                    name="odd_gate_fwd")
    yout = _mm(z, w_out, "nn", F32)
    return tail(x, yout, post) + ((x, h, proj, mixed, ylin, z, yout),)


def _odd_bwd(g, saved, pre, post, w_in, pool_w, pool_scale, w_out, on_w):
    x, h, proj, mixed, ylin, z, yout = saved
    dyout, dpost = _post_bwd(g, yout, post)
    dz = _mm(dyout, w_out, "nt", F32)
    dw_out = _mm(z, dyout, "tn", BF16)

    def gate_bwd(dzv, yl, gt, sc):
        sg, dsg = _silu_and_grad(gt)
        tt = dzv * sg
        return (tt * sc, dzv * yl * sc * dsg), (tt * yl,)

    dylin, dproj_gate, dscale = _rowwise(gate_bwd, [(dz, POOL_W, 0), (ylin, POOL_W, 0), (proj, POOL_W, 1),
                                                    (pool_scale, POOL_W, 0)],
                                         [(POOL_W, BF16), (POOL_W, BF16, ODD_IN, 1)], [POOL_W], name="odd_gate_bwd")
    dmixed = _gmm(dylin, pool_w, "nt", F32)
    dpool_w = _gmm(mixed, dylin, "tn", BF16)
    dproj = _pool(dmixed, 0, True, BF16, into=dproj_gate)
    dw_in = _mm(h, dproj, "tn", BF16, out_blocks=True)
    sent = on_w(dict(w_in=dw_in, w_out=dw_out, pool_w=dpool_w))
    dh = _mm(dproj, w_in, "nt", F32, b_blocks=True, after=(sent,))
    g_prev, dpre = _pre_bwd(g, dh, x, pre)
    return g_prev, dict(pre=dpre, post=dpost, pool_scale=dscale)


def _my_index():
    return 4 * lax.axis_index("x") + 2 * lax.axis_index("y") + lax.axis_index("c")


HBM_SPEC = pl.BlockSpec(memory_space=pltpu.HBM)
SEM_SPEC = pl.BlockSpec(memory_space=pltpu.SEMAPHORE)
SPLIT_EFFECT = pltpu.SideEffectType.DATAFLOW_SIDE_EFFECTING


def _device_of(j):
    return (j // 4, (j // 2) % 2, j % 2)


def _split_copy(srcs, lands, send_sems, recv_sems, gather, i, j, dst_slot, recv_slot):
    return pltpu.make_async_remote_copy(
        src_ref=srcs[i] if gather else srcs[i].at[j], dst_ref=lands[i].at[dst_slot],
        send_sem=send_sems.at[i * N_DEV + j], recv_sem=recv_sems.at[i * N_DEV + recv_slot],
        device_id=_device_of(j), device_id_type=MESH_ID)


def _own_copy(srcs, lands, send_sems, gather, i, me):
    return pltpu.make_async_copy(srcs[i] if gather else srcs[i].at[me], lands[i].at[me], send_sems.at[i * N_DEV + me])


def _xchg_start(name, srcs, gather, after=()):
    n = len(srcs)
    n_in = n + len(after)

    def body(*refs):
        src_refs = refs[:n]
        send_sems, recv_sems, token = refs[n_in], refs[n_in + 1], refs[-1]
        land_refs = refs[n_in + 2 + n:n_in + 2 + 2 * n]
        me = _my_index()
        for j in range(N_DEV):
            @pl.when(me != j)
            def _(j=j):
                for i in range(n):
                    _split_copy(src_refs, land_refs, send_sems, recv_sems, gather, i, j, me, me).start()
        for i in range(n):
            _own_copy(src_refs, land_refs, send_sems, gather, i, me).start()
        token[...] = jnp.zeros_like(token)

    land_shapes = [((N_DEV,) + a.shape) if gather else a.shape for a in srcs]
    thru = ([pltpu.HBM(a.shape, a.dtype) for a in srcs] + [pltpu.HBM(s, a.dtype) for s, a in zip(land_shapes, srcs)])
    res = pl.pallas_call(
        body, name=name,
        out_shape=(pltpu.SemaphoreType.DMA((n * N_DEV,)), pltpu.SemaphoreType.DMA((n * N_DEV,)), *thru,
                   SDS((8, LANES), F32)),
        in_specs=[HBM_SPEC] * n + [pl.BlockSpec(memory_space=pl.ANY)] * len(after),
        out_specs=(SEM_SPEC, SEM_SPEC, *([HBM_SPEC] * (2 * n)), pl.BlockSpec(memory_space=pltpu.VMEM)),
        input_output_aliases={i: 2 + i for i in range(n)},
        compiler_params=pltpu.CompilerParams(has_side_effects=SPLIT_EFFECT),
    )(*[pltpu.with_memory_space_constraint(a, pltpu.HBM) for a in srcs], *after)
    return res[0], res[1], list(res[2:2 + n]), list(res[2 + n:2 + 2 * n]), res[-1]


def _xchg_wait(name, started, gather, after):
    send_sems, recv_sems, srcs, lands, _ = started
    n = len(srcs)

    def body(*refs):
        src_refs, land_refs = refs[:n], refs[n:2 * n]
        send_r, recv_r = refs[2 * n], refs[2 * n + 1]
        me = _my_index()
        for j in range(N_DEV):
            @pl.when(me != j)
            def _(j=j):
                for i in range(n):
                    _split_copy(src_refs, land_refs, send_r, recv_r, gather, i, j, me, me).wait_send()
                    _split_copy(src_refs, land_refs, send_r, recv_r, gather, i, j, j, j).wait_recv()
        for i in range(n):
            _own_copy(src_refs, land_refs, send_r, gather, i, me).wait()

    thru = [pltpu.HBM(a.shape, a.dtype) for a in list(srcs) + list(lands)]
    res = pl.pallas_call(
        body, name=name, out_shape=tuple(thru),
        in_specs=[HBM_SPEC] * (2 * n) + [SEM_SPEC, SEM_SPEC] + [pl.BlockSpec(memory_space=pl.ANY)] * len(after),
        out_specs=tuple([HBM_SPEC] * (2 * n)),
        input_output_aliases={i: i for i in range(2 * n)},
        compiler_params=pltpu.CompilerParams(has_side_effects=SPLIT_EFFECT),
    )(*srcs, *lands, send_sems, recv_sems, *after)
    return list(res[n:])


def _adam_layer(w, slots, m, v, layer, name, into=None):
    n_l, r, c = w.shape
    ns = slots.shape[0]
    tr = r
    while tr * c * 4 > (1 << 20) and tr % 16 == 0:
        tr //= 2
    assert r % tr == 0

    def body(w_ref, g_ref, m_ref, v_ref, *rest):
        go_ref, d_ref, mo_ref, vo_ref = rest[-4:]
        g = g_ref[0].astype(F32)
        for s in range(1, ns):
            g = g + g_ref[s].astype(F32)
        mn = ADAM_B1 * m_ref[...] + (1.0 - ADAM_B1) * g
        vn = ADAM_B2 * v_ref[...] + (1.0 - ADAM_B2) * (g * g)
        m_hat = mn / (1.0 - ADAM_B1 ** ADAM_STEP)
        v_hat = vn / (1.0 - ADAM_B2 ** ADAM_STEP)
        go_ref[...] = g
        d_ref[...] = -ADAM_LR * (m_hat / (jnp.sqrt(v_hat) + ADAM_EPS) + ADAM_WD * w_ref[...])
        mo_ref[...] = mn
        vo_ref[...] = vn

    blk = pl.BlockSpec((None, tr, c), lambda i: (layer, i, 0))
    earlier = () if into is None else tuple(into)
    return pl.pallas_call(
        body, name=name, grid=(r // tr,),
        in_specs=[blk, pl.BlockSpec((ns, tr, c), lambda i: (0, i, 0)), blk, blk]
        + [pl.BlockSpec(memory_space=pl.ANY)] * len(earlier),
        out_specs=[blk] * 4, out_shape=[SDS((n_l, r, c), F32)] * 4,
        input_output_aliases={4 + q: q for q in range(len(earlier))},
        compiler_params=_cparams(("arbitrary",)),
    )(*_in_hbm((w, slots, m, v)), *earlier)


def _adam(w, gslots, m, v, name):
    r, c = w.shape
    ns = gslots.shape[0]
    tr = r
    while tr * c * 4 > (1 << 20) and tr % 16 == 0:
        tr //= 2
    assert r % tr == 0

    def body(w_ref, g_ref, m_ref, v_ref, go_ref, d_ref, mo_ref, vo_ref):
        g = g_ref[0].astype(F32)
        for s in range(1, ns):
            g = g + g_ref[s].astype(F32)
        wv = w_ref[...]
        mn = ADAM_B1 * m_ref[...] + (1.0 - ADAM_B1) * g
        vn = ADAM_B2 * v_ref[...] + (1.0 - ADAM_B2) * (g * g)
        m_hat = mn / (1.0 - ADAM_B1 ** ADAM_STEP)
        v_hat = vn / (1.0 - ADAM_B2 ** ADAM_STEP)
        go_ref[...] = g
        d_ref[...] = -ADAM_LR * (m_hat / (jnp.sqrt(v_hat) + ADAM_EPS) + ADAM_WD * wv)
        mo_ref[...] = mn
        vo_ref[...] = vn

    blk = pl.BlockSpec((tr, c), lambda i: (i, 0))
    return pl.pallas_call(
        body, name=name, grid=(r // tr,),
        in_specs=[blk, pl.BlockSpec((ns, tr, c), lambda i: (0, i, 0)), blk, blk],
        out_specs=[blk] * 4, out_shape=[SDS((r, c), F32)] * 4,
        compiler_params=_cparams(("parallel",)),
    )(w, gslots, m, v)


def _sum_slots(slots, name):
    ns, r, c = slots.shape

    def body(g_ref, o_ref):
        g = g_ref[0]
        for s in range(1, ns):
            g = g + g_ref[s]
        o_ref[...] = g

    return pl.pallas_call(
        body, name=name, grid=(1,),
        in_specs=[pl.BlockSpec((ns, r, c), lambda i: (0, 0, 0))], out_specs=pl.BlockSpec((r, c), lambda i: (0, 0)),
        out_shape=SDS((r, c), F32), compiler_params=_cparams(("arbitrary",)),
    )(slots)


def _adam_params(params, name):
    n = len(params)

    def body(*refs):
        ins, outs = refs[:5 * n], refs[5 * n:]
        for p in range(n):
            w_ref, m_ref, v_ref, g_first, g_rest = ins[5 * p:5 * p + 5]
            go_ref, d_ref, mo_ref, vo_ref = outs[4 * p:4 * p + 4]
            for part, g_ref in ((slice(0, 1), g_first), (slice(1, w_ref.shape[0]), g_rest)):
                g = g_ref[...]
                mn = ADAM_B1 * m_ref[part] + (1.0 - ADAM_B1) * g
                vn = ADAM_B2 * v_ref[part] + (1.0 - ADAM_B2) * (g * g)
                m_hat = mn / (1.0 - ADAM_B1 ** ADAM_STEP)
                v_hat = vn / (1.0 - ADAM_B2 ** ADAM_STEP)
                go_ref[part] = g
                d_ref[part] = -ADAM_LR * (m_hat / (jnp.sqrt(v_hat) + ADAM_EPS) + ADAM_WD * w_ref[part])
                mo_ref[part] = mn
                vo_ref[part] = vn

    def whole(a):
        return pl.BlockSpec(a.shape, lambda i, nd=a.ndim: (0,) * nd)

    flat = _in_hbm([a for prm in params for a in prm])
    outs = pl.pallas_call(
        body, name=name, grid=(1,),
        in_specs=[whole(a) for a in flat],
        out_specs=[whole(prm[0]) for prm in params for _ in range(4)],
        out_shape=[SDS(prm[0].shape, F32) for prm in params for _ in range(4)],
        compiler_params=_cparams(("arbitrary",)),
    )(*flat)
    return [outs[4 * p:4 * p + 4] for p in range(n)]


SMALL_NAMES = ("pre_norm", "post_norm", "ssm_a_re", "ssm_a_im", "ssm_log_dt", "ssm_b_re", "ssm_b_im", "ssm_c_re",
               "ssm_c_im", "ssm_d", "ssm_glu_b")
SSM_NAMES = ("ssm_a_re", "ssm_a_im", "ssm_log_dt", "ssm_b_re", "ssm_b_im", "ssm_c_re", "ssm_c_im")
WEIGHT_ORDER = ("pre_norm", "post_norm", "even_w_in", "even_w_out", "ssm_a_re", "ssm_a_im", "ssm_log_dt", "ssm_b_re",
                "ssm_b_im", "ssm_c_re", "ssm_c_im", "ssm_d", "ssm_glu_w", "ssm_glu_b", "odd_w_in", "pool_w",
                "pool_scale", "odd_w_out")
PACK_ROWS_ALIGN = 8


def _pack(parts):
    flat = jnp.concatenate([p.reshape(-1).astype(F32) for p in parts])
    rows = -(-flat.shape[0] // (LANES * PACK_ROWS_ALIGN)) * PACK_ROWS_ALIGN
    return jnp.pad(flat, (0, rows * LANES - flat.shape[0])).reshape(rows, LANES)


def _unpack(packed, shapes):
    flat = packed.reshape(-1)
    out, off = [], 0
    for shp in shapes:
        size = math.prod(shp)
        out.append(flat[off:off + size].reshape(shp))
        off += size
    return out


EVEN_SHARDED = ("w_in", "w_out", "glu_w")
ODD_SHARDED = ("w_in", "pool_w", "w_out")
FAMILY = {(0, "w_in"): "even_w_in", (0, "w_out"): "even_w_out", (0, "glu_w"): "ssm_glu_w",
          (1, "w_in"): "odd_w_in", (1, "pool_w"): "pool_w", (1, "w_out"): "odd_w_out"}


def _sharded_keys(layer):
    return EVEN_SHARDED if layer % 2 == 0 else ODD_SHARDED


def _local_step(x, tgt, small, get_weights, on_w, on_ssm, on_grads, zero=0.0):
    tables = _rope_tables(zero) + (_attention_bias(zero),)
    preps, prep_vjps = [], []
    for i in range(2):
        out, vjp = jax.vjp(_ssm_prep, small["ssm_a_re"][i] + zero, small["ssm_a_im"][i], small["ssm_log_dt"][i],
                           small["ssm_b_re"][i], small["ssm_b_im"][i], small["ssm_c_re"][i], small["ssm_c_im"][i])
        preps.append(tuple(out) + _scan_tables(out[0]))
        prep_vjps.append(vjp)

    def layer_args(layer, wts):
        i = layer // 2
        pre, post = _row(small["pre_norm"][layer]) + wts.get("token", 0.0), _row(small["post_norm"][layer])
        if layer % 2 == 0:
            return (pre, post, wts["w_in"], wts["late"], _row(small["ssm_glu_b"][i]), _row(small["ssm_d"][i]),
                    preps[i], tables)
        return (pre, post, wts["w_in"], wts["pool_w"], _row(wts["pool_scale"]), wts["w_out"])

    saved, args = [], []
    cur = x
    for layer in range(4):
        after = (cur,) if layer else (cur, tables[0], tables[3]) + preps[0][1:] + preps[1][1:]
        args.append(layer_args(layer, get_weights(layer, after)))
        if layer == 0:
            h = _norm_fwd(cur, args[0][0])
        if layer < 3:
            def tail(xv, yv, post, next_gain=_row(small["pre_norm"][layer + 1])):
                return tuple(_post_fwd(xv, yv, post, next_gain))
        else:
            def tail(xv, yv, post):
                return tuple(_post_fwd_loss(xv, yv, post, tgt))
        cur, h, sv = (_even_fwd if layer % 2 == 0 else _odd_fwd)(cur, h, tail, *args[layer])
        saved.append(sv)
    g, sq = cur, h
    loss = 0.5 * jnp.sum(sq) / D

    lg = [None] * 4
    token = jnp.zeros((), F32)
    for layer in reversed(range(4)):
        largs = list(args[layer])
        largs[1] = largs[1] + token
        hooks = dict(on_w=functools.partial(on_w, layer))
        ssm_grads = []
        if layer % 2 == 0:
            def ssm_hook(cotangents, layer=layer):
                ssm_grads.append(prep_vjps[layer // 2](cotangents))
                return on_ssm(layer, ssm_grads[0])

            hooks["on_ssm"] = ssm_hook
        g, lg[layer] = (_even_bwd if layer % 2 == 0 else _odd_bwd)(g, saved[layer], *largs, **hooks)
        if ssm_grads:
            lg[layer]["ssm"] = ssm_grads[0]
        token = on_grads(layer, lg[layer])
    return loss, g, token


def _to_slots(key, gfull):
    if key == "w_in":
        return gfull
    if key in ("w_out", "glu_w"):
        rr, nn = gfull.shape
        return gfull.reshape(N_DEV, rr // N_DEV, nn)
    assert key == "pool_w"
    gg, rr, nn = gfull.shape
    return gfull.reshape(gg, N_DEV, rr // N_DEV, nn).transpose(1, 0, 2, 3)


def _from_gathered(key, gat):
    if key == "w_in":
        return gat
    if key in ("w_out", "glu_w"):
        _, rr, nn = gat.shape
        return gat.reshape(N_DEV * rr, nn)
    assert key == "pool_w"
    _, gg, rr, nn = gat.shape
    return gat.transpose(1, 0, 2, 3).reshape(gg, N_DEV * rr, nn)


def kernel(x, pre_norm, post_norm, even_w_in, even_w_out, ssm_a_re, ssm_a_im, ssm_log_dt, ssm_b_re, ssm_b_im, ssm_c_re, ssm_c_im, ssm_d, ssm_glu_w, ssm_glu_b, odd_w_in, pool_w, pool_scale, odd_w_out, loss_target, m_pre_norm, m_post_norm, m_even_w_in, m_even_w_out, m_ssm_a_re, m_ssm_a_im, m_ssm_log_dt, m_ssm_b_re, m_ssm_b_im, m_ssm_c_re, m_ssm_c_im, m_ssm_d, m_ssm_glu_w, m_ssm_glu_b, m_odd_w_in, m_pool_w, m_pool_scale, m_odd_w_out, v_pre_norm, v_post_norm, v_even_w_in, v_even_w_out, v_ssm_a_re, v_ssm_a_im, v_ssm_log_dt, v_ssm_b_re, v_ssm_b_im, v_ssm_c_re, v_ssm_c_im, v_ssm_d, v_ssm_glu_w, v_ssm_glu_b, v_odd_w_in, v_pool_w, v_pool_scale, v_odd_w_out):
    w = dict(pre_norm=pre_norm, post_norm=post_norm, even_w_in=even_w_in, even_w_out=even_w_out, ssm_a_re=ssm_a_re,
             ssm_a_im=ssm_a_im, ssm_log_dt=ssm_log_dt, ssm_b_re=ssm_b_re, ssm_b_im=ssm_b_im, ssm_c_re=ssm_c_re,
             ssm_c_im=ssm_c_im, ssm_d=ssm_d, ssm_glu_w=ssm_glu_w, ssm_glu_b=ssm_glu_b, odd_w_in=odd_w_in,
             pool_w=pool_w, pool_scale=pool_scale, odd_w_out=odd_w_out)
    mom = dict(pre_norm=m_pre_norm, post_norm=m_post_norm, even_w_in=m_even_w_in, even_w_out=m_even_w_out,
               ssm_a_re=m_ssm_a_re, ssm_a_im=m_ssm_a_im, ssm_log_dt=m_ssm_log_dt, ssm_b_re=m_ssm_b_re,
               ssm_b_im=m_ssm_b_im, ssm_c_re=m_ssm_c_re, ssm_c_im=m_ssm_c_im, ssm_d=m_ssm_d, ssm_glu_w=m_ssm_glu_w,
               ssm_glu_b=m_ssm_glu_b, odd_w_in=m_odd_w_in, pool_w=m_pool_w, pool_scale=m_pool_scale,
               odd_w_out=m_odd_w_out)
    var = dict(pre_norm=v_pre_norm, post_norm=v_post_norm, even_w_in=v_even_w_in, even_w_out=v_even_w_out,
               ssm_a_re=v_ssm_a_re, ssm_a_im=v_ssm_a_im, ssm_log_dt=v_ssm_log_dt, ssm_b_re=v_ssm_b_re,
               ssm_b_im=v_ssm_b_im, ssm_c_re=v_ssm_c_re, ssm_c_im=v_ssm_c_im, ssm_d=v_ssm_d, ssm_glu_w=v_ssm_glu_w,
               ssm_glu_b=v_ssm_glu_b, odd_w_in=v_odd_w_in, pool_w=v_pool_w, pool_scale=v_pool_scale,
               odd_w_out=v_odd_w_out)
    me = _my_index()
    scale_cols = pool_scale.shape[1]

    def shards_of(layer, keys):
        i = layer // 2
        shards = [w[FAMILY[(layer % 2, k)]][i].astype(BF16) for k in keys]
        if layer % 2 == 1:
            shards.append(jnp.pad(pool_scale[i][None], ((0, PACK_ROWS_ALIGN - 1), (0, 0))))
        return shards

    def start_gather(tag, after=()):
        return _xchg_start(f"gather_start_{tag}", shards[tag], True, after)

    shards = {0: shards_of(0, EVEN_SHARDED[:1]), "0_late": shards_of(0, EVEN_SHARDED[1:])}
    shards.update({layer: shards_of(layer, _sharded_keys(layer)) for layer in (1, 2, 3)})
    gather_started = {0: start_gather(0)}
    small = {nm: w[nm] for nm in SMALL_NAMES}
    packed_names = ("pre_norm", "post_norm") + SSM_NAMES + ("ssm_d", "ssm_glu_b")
    tails = {nm: (SSM_GROUPS, SSM_STATE * SSM_GROUP) if nm in ("ssm_b_re", "ssm_b_im") else w[nm].shape[1:]
             for nm in packed_names}
    dense = lambda nm, a: a.reshape((a.shape[0],) + tails[nm])
    small_operands = {nm: tuple(dense(nm, tree[nm]) for tree in (w, mom, var)) for nm in packed_names}
    early_work = [a for nm in ("ssm_b_re", "ssm_b_im") for a in small_operands[nm]]

    def get_weights(layer, after):
        keys = EVEN_SHARDED[:1] if layer == 0 else _sharded_keys(layer)
        if layer == 0:
            after = tuple(after) + tuple(early_work)
        lands = _xchg_wait(f"gather_wait_{layer}", gather_started[layer], True, after)
        wts = {k: _from_gathered(k, gat) for k, gat in zip(keys, lands)}
        if layer % 2 == 1:
            wts["pool_scale"] = lands[-1][:, 0, :].reshape(N_DEV * scale_cols)
        if layer == 0:
            prev = gather_started["0_late"] = start_gather("0_late", after=(lands[0],))
            for later in (1, 2, 3):
                prev = gather_started[later] = start_gather(later, after=(prev[4],))
            wts["token"] = prev[4][0, 0]

            def late(after_late):
                late_lands = _xchg_wait("gather_wait_0_late", gather_started["0_late"], True, (after_late,))
                return tuple(_from_gathered(k, gat) for k, gat in zip(EVEN_SHARDED[1:], late_lands))

            wts["late"] = late
        elif layer == 2:
            wts["late"] = lambda after_late: (wts["w_out"], wts["glu_w"])
        return wts

    scatter_started = []

    def on_w(layer, gw):
        keys = tuple(k for k in _sharded_keys(layer) if k in gw)
        started = _xchg_start(f"scatter_start_{layer}_{keys[0]}", [_to_slots(k, gw[k]) for k in keys], False)
        scatter_started.append((layer, keys, started))
        return started[4]

    def wait_scatters(layers, after):
        for layer, keys, started in scatter_started:
            if layer in layers:
                lands = _xchg_wait(f"scatter_wait_{layer}_{keys[0]}", started, False, after)
                for k, land in zip(keys, lands):
                    recv[(layer, k)] = land

    layer_grads = {}
    early_started, mid_started = [], []

    def on_ssm(layer, ssm_grads):
        if layer != 0:
            return None
        mid_started.append(_xchg_start("mid_start", [_pack(list(ssm_grads))], True))
        return mid_started[0][4]

    def on_grads(layer, lg):
        layer_grads[layer] = lg
        zero = jnp.zeros((), F32)
        if layer == 1:
            lgs = layer_grads
            early = ([jnp.concatenate([lgs[l][k] for l in (1, 2, 3)], axis=0) for k in ("pre", "post")]
                     + list(lgs[2]["ssm"]) + [lgs[2]["ssm_d"], lgs[2]["glu_b"],
                                              jnp.concatenate([lgs[1]["pool_scale"], lgs[3]["pool_scale"]], axis=0)])
            early_started.append(_xchg_start("small_start", [_pack(early)], True))
            zero = zero + early_started[0][4][0, 0]
        return zero

    loss_local, grad_x, token = _local_step(x[0], loss_target[0], small, get_weights, on_w, on_ssm, on_grads,
                                            zero=gather_started[0][4][0, 0])

    lg0 = layer_grads[0]
    late_started = _xchg_start("late_start", [_pack([lg0["pre"], lg0["post"], lg0["ssm_d"], lg0["glu_b"],
                                                     loss_local.reshape(1)]) + token], True)

    def adam_family(parity, k, which, into=None):
        nm = FAMILY[(parity, k)]
        cols = w[nm].shape[-1]
        return _adam_layer(w[nm].reshape(2, -1, cols), recv[(parity + 2 * which, k)].reshape(N_DEV, -1, cols),
                           mom[nm].reshape(2, -1, cols), var[nm].reshape(2, -1, cols), which,
                           f"adam_{nm}_{which}", into)

    recv, res = {}, {}
    wait_scatters((3, 2, 1), (late_started[4],))
    for k in ODD_SHARDED:
        res[FAMILY[(1, k)]] = adam_family(1, k, 1, adam_family(1, k, 0))
    half_done = {k: adam_family(0, k, 1) for k in EVEN_SHARDED}
    odd_done = tuple(half_done[k][0] for k in EVEN_SHARDED)

    (early_slots,) = _xchg_wait("small_wait", early_started[0], True, odd_done)
    (mid_slots,) = _xchg_wait("mid_wait", mid_started[0], True, odd_done)
    early_shapes = [(w[nm].shape[0] - 1,) + tails[nm] for nm in packed_names] + [(2, N_DEV * scale_cols)]
    g_early = _unpack(_sum_slots(early_slots, "sum_small_early"), early_shapes)
    g_mid = _unpack(_sum_slots(mid_slots, "sum_small_mid"), [(1,) + tails[nm] for nm in SSM_NAMES])

    (late_slots,) = _xchg_wait("late_wait", late_started, True, (g_early[0], g_mid[0]))
    wait_scatters((0,), (late_slots,))
    for k in EVEN_SHARDED:
        res[FAMILY[(0, k)]] = adam_family(0, k, 0, half_done[k])
    for nm in FAMILY.values():
        res[nm] = [o.reshape(w[nm].shape) for o in res[nm]]

    late_names = ("pre_norm", "post_norm", "ssm_d", "ssm_glu_b")
    g_late = _unpack(_sum_slots(late_slots, "sum_small_late"), [(1,) + tails[nm] for nm in late_names] + [(1,)])
    g_first = dict(zip(late_names, g_late))
    g_first.update(zip(SSM_NAMES, g_mid))
    outs = _adam_params([small_operands[nm] + (g_first[nm], g_early[j]) for j, nm in enumerate(packed_names)],
                        "adam_small")
    for nm, four in zip(packed_names, outs):
        res[nm] = [o.reshape(w[nm].shape) for o in four]
    loss = g_late[-1].reshape(())
    g_scale = lax.dynamic_slice_in_dim(g_early[-1], me * scale_cols, scale_cols, axis=1)
    pad = ((0, PACK_ROWS_ALIGN - 2), (0, 0))
    outs = _adam(jnp.pad(pool_scale, pad), jnp.pad(g_scale, pad)[None], jnp.pad(m_pool_scale, pad),
                 jnp.pad(v_pool_scale, pad), name="adam_pool_scale")
    res["pool_scale"] = [o[:2] for o in outs]

    out = [loss, grad_x[None]]
    for kind in range(4):
        out += [res[nm][kind] for nm in WEIGHT_ORDER]
    return tuple(out)
```

```python
import functools
import math

import jax
import jax.numpy as jnp
from jax import lax
from jax.experimental import pallas as pl
from jax.experimental.pallas import tpu as pltpu

F32 = jnp.float32
BF16 = jnp.bfloat16
SDS = jax.ShapeDtypeStruct

N_DEV = 8
S = 2048
D = 1024
HEAD_DIM = 64
ROT_DIM = 16
ROPE_THETA = 500000.0
ATT_W = 1024
SSM_W = 512
SSM_GROUPS = 32
SSM_GROUP = 16
SSM_STATE = 64
N_CPLX = SSM_GROUPS * SSM_STATE
POOL_W = 2048
POOL_GROUP = 512
EVEN_IN = 5120
EVEN_OUT = 1536
ODD_IN = 4096
RMS_EPS = 1e-6
LANES = 128
VMEM_LIMIT = 48 * 1024 * 1024

ADAM_LR = 0.001
ADAM_B1 = 0.9
ADAM_B2 = 0.999
ADAM_EPS = 1e-08
ADAM_WD = 0.01
ADAM_STEP = 10

MESH_ID = pl.DeviceIdType.MESH
NN = (((1,), (0,)), ((), ()))
NT = (((1,), (1,)), ((), ()))
TN = (((0,), (0,)), ((), ()))
_DN = {"nn": NN, "nt": NT, "tn": TN}


def _cparams(sem):
    return pltpu.CompilerParams(dimension_semantics=sem, vmem_limit_bytes=VMEM_LIMIT)


def _in_hbm(arrs):
    return [pltpu.with_memory_space_constraint(a, pltpu.HBM) for a in arrs]


MM_TILES = (1024, 768, 512)


def _tile(dim):
    return next((t for t in MM_TILES if dim % t == 0), dim)


BLOCK_PAIR = 2
MAX_WHOLE_K = 2048


def _mm(a, b, mode, out_dtype, b_blocks=False, out_blocks=False, after=()):
    if b_blocks:
        nblk, rows, cb = b.shape
        b2_shape = (rows, nblk * cb)
    else:
        b2_shape = b.shape
    if mode == "nn":
        (m, k), n = a.shape, b2_shape[1]
    elif mode == "nt":
        (m, k), n = a.shape, b2_shape[0]
    else:
        (k, m), n = a.shape, b2_shape[1]
    tm, tn, tk = _tile(m), _tile(n), _tile(k)
    if k <= MAX_WHOLE_K:
        tk = k
    if b_blocks and mode == "nn":
        tn = BLOCK_PAIR * cb
        if tn <= MM_TILES[0]:
            tm = m
    if b_blocks and mode == "nt":
        tk = BLOCK_PAIR * cb
    if out_blocks:
        cb = n // N_DEV
        tn = BLOCK_PAIR * cb
        tk = k
    nk = k // tk

    def body(a_ref, b_ref, *rest):
        o_ref, acc_ref = rest[-2:]
        kk = pl.program_id(2)
        bv = jnp.concatenate([b_ref[p] for p in range(BLOCK_PAIR)], axis=1) if b_blocks else b_ref[...]
        part = lax.dot_general(a_ref[...].astype(BF16), bv.astype(BF16), _DN[mode], preferred_element_type=F32)

        def write(res):
            if out_blocks:
                for p in range(BLOCK_PAIR):
                    o_ref[p] = res[:, p * cb:(p + 1) * cb].astype(o_ref.dtype)
            else:
                o_ref[...] = res.astype(o_ref.dtype)

        if nk == 1:
            write(part)
            return

        @pl.when(kk == 0)
        def _():
            acc_ref[...] = part

        @pl.when((kk > 0) & (kk < nk - 1))
        def _():
            acc_ref[...] += part

        @pl.when(kk == nk - 1)
        def _():
            write(acc_ref[...] + part)

    if mode == "nn":
        a_spec = pl.BlockSpec((tm, tk), lambda i, j, kk: (i, kk))
        b_spec = pl.BlockSpec((tk, tn), lambda i, j, kk: (kk, j))
    elif mode == "nt":
        a_spec = pl.BlockSpec((tm, tk), lambda i, j, kk: (i, kk))
        b_spec = pl.BlockSpec((tn, tk), lambda i, j, kk: (j, kk))
    else:
        a_spec = pl.BlockSpec((tk, tm), lambda i, j, kk: (kk, i))
        b_spec = pl.BlockSpec((tk, tn), lambda i, j, kk: (kk, j))
    if b_blocks and mode == "nn":
        b_spec = pl.BlockSpec((BLOCK_PAIR, tk, cb), lambda i, j, kk: (j, kk, 0))
    if b_blocks and mode == "nt":
        b_spec = pl.BlockSpec((BLOCK_PAIR, tn, cb), lambda i, j, kk: (kk, j, 0))
    out_spec = pl.BlockSpec((tm, tn), lambda i, j, kk: (i, j))
    out_shape = SDS((m, n), out_dtype)
    if out_blocks:
        out_spec = pl.BlockSpec((BLOCK_PAIR, tm, cb), lambda i, j, kk: (j, i, 0))
        out_shape = SDS((N_DEV, m, cb), out_dtype)
    return pl.pallas_call(
        body, name=f"mm_{mode}_{m}x{k}x{n}",
        grid=(m // tm, n // tn, nk),
        in_specs=[a_spec, b_spec] + [pl.BlockSpec(memory_space=pl.ANY)] * len(after),
        out_specs=out_spec,
        out_shape=out_shape,
        scratch_shapes=[pltpu.VMEM((tm, tn) if nk > 1 else (8, LANES), F32)],
        compiler_params=_cparams(("parallel", "parallel", "arbitrary")),
    )(a, b, *after)


def _gmm(a, b, mode, out_dtype, tm=S):
    ng, gw = POOL_W // POOL_GROUP, POOL_GROUP
    ns = S // tm
    if mode in ("nn", "nt"):
        def body(a_ref, b_ref, o_ref):
            o_ref[...] = lax.dot_general(a_ref[...].astype(BF16), b_ref[...].astype(BF16), _DN[mode],
                                         preferred_element_type=F32).astype(o_ref.dtype)

        return pl.pallas_call(
            body, name=f"gmm_{mode}", grid=(ng, ns),
            in_specs=[pl.BlockSpec((tm, gw), lambda g, i: (i, g)),
                      pl.BlockSpec((None, gw, gw), lambda g, i: (g, 0, 0))],
            out_specs=pl.BlockSpec((tm, gw), lambda g, i: (i, g)),
            out_shape=SDS((S, POOL_W), out_dtype),
            compiler_params=_cparams(("parallel", "parallel")),
        )(a, b)

    def body_tn(a_ref, b_ref, o_ref, acc_ref):
        i = pl.program_id(1)

        @pl.when(i == 0)
        def _():
            acc_ref[...] = jnp.zeros_like(acc_ref)

        acc_ref[...] += lax.dot_general(a_ref[...].astype(BF16), b_ref[...].astype(BF16), TN,
                                        preferred_element_type=F32)

        @pl.when(i == ns - 1)
        def _():
            o_ref[...] = acc_ref[...].astype(o_ref.dtype)

    return pl.pallas_call(
        body_tn, name="gmm_tn", grid=(ng, ns),
        in_specs=[pl.BlockSpec((tm, gw), lambda g, i: (i, g)),
                  pl.BlockSpec((tm, gw), lambda g, i: (i, g))],
        out_specs=pl.BlockSpec((None, gw, gw), lambda g, i: (g, 0, 0)),
        out_shape=SDS((ng, gw, gw), out_dtype),
        scratch_shapes=[pltpu.VMEM((gw, gw), F32)],
        compiler_params=_cparams(("parallel", "arbitrary")),
    )(a, b)


def _rowwise(fn, inputs, out_defs, acc_defs=(), tm=512, name=None, after=()):
    n_in, n_out, n_acc = len(inputs), len(out_defs), len(acc_defs)
    n_after = len(after)
    in_specs, args = [], []
    for arr, width, cb in inputs:
        if arr.shape[0] != S:
            in_specs.append(pl.BlockSpec((arr.shape[0], width), lambda i, cb=cb: (0, cb)))
        else:
            in_specs.append(pl.BlockSpec((tm, width), lambda i, cb=cb: (i, cb)))
        args.append(arr)
    out_defs = [d if len(d) == 4 else (d[0], d[1], d[0], 0) for d in out_defs]
    out_shape = [SDS((S, ww), dt) for _, dt, ww, _ in out_defs] + [SDS((1, w), F32) for w in acc_defs]
    out_specs = ([pl.BlockSpec((tm, w), lambda i, cb=cb: (i, cb)) for w, _, _, cb in out_defs]
                 + [pl.BlockSpec((1, w), lambda i: (0, 0)) for w in acc_defs])

    def kern(*refs):
        vals = [r[...] for r in refs[:n_in]]
        outs, accs = fn(*vals)
        out_refs = refs[n_in + n_after:]
        for r, v in zip(out_refs[:n_out], outs):
            r[...] = v.astype(r.dtype)
        if n_acc:
            acc_refs = out_refs[n_out:]

            @pl.when(pl.program_id(0) == 0)
            def _():
                for r in acc_refs:
                    r[...] = jnp.zeros_like(r)

            for r, v in zip(acc_refs, accs):
                r[...] += jnp.sum(v, axis=0, keepdims=True)

    res = pl.pallas_call(
        kern, name=name, grid=(S // tm,), in_specs=in_specs + [pl.BlockSpec(memory_space=pl.ANY)] * n_after,
        out_specs=out_specs, out_shape=out_shape, compiler_params=_cparams(("arbitrary",)),
    )(*args, *after)
    return res


def _sigmoid(x):
    return 1.0 / (1.0 + jnp.exp(-x))


def _silu_and_grad(x):
    s = _sigmoid(x)
    return x * s, s * (1.0 + x * (1.0 - s))


_GELU_K = math.sqrt(2.0 / math.pi)
_GELU_C = 0.044715


def _gelu_and_grad(x):
    t = jnp.tanh(_GELU_K * (x + _GELU_C * (x * x * x)))
    cdf = 0.5 * (1.0 + t)
    grad = cdf + 0.5 * x * (1.0 - t * t) * (_GELU_K * (1.0 + 3.0 * _GELU_C * x * x))
    return x * cdf, grad


def _rms(xv, gain):
    r = lax.rsqrt(jnp.mean(xv * xv, axis=-1, keepdims=True) + RMS_EPS)
    return xv * r * gain


def _rms_bwd(dout, xv, gain):
    r = lax.rsqrt(jnp.mean(xv * xv, axis=-1, keepdims=True) + RMS_EPS)
    xhat = xv * r
    dxhat = dout * gain
    dx = r * (dxhat - xhat * jnp.mean(dxhat * xhat, axis=-1, keepdims=True))
    return dx, dout * xhat


def _norm_fwd(x, gain):
    (h,) = _rowwise(lambda xv, g: ((_rms(xv, g),), ()), [(x, D, 0), (gain, D, 0)], [(D, BF16)], name="norm_fwd")
    return h


def _post_fwd(x, y, gain, next_gain):
    def fn(xv, yv, g, gn):
        out = xv + _rms(yv, g)
        return (out, _rms(out, gn)), ()

    return _rowwise(fn, [(x, D, 0), (y, D, 0), (gain, D, 0), (next_gain, D, 0)], [(D, F32), (D, BF16)],
                    name="post_fwd")


def _post_fwd_loss(x, y, gain, tgt):
    def fn(xv, yv, g, tv):
        e = xv + _rms(yv, g) - tv
        return (e * (1.0 / D),), (e * e,)

    return _rowwise(fn, [(x, D, 0), (y, D, 0), (gain, D, 0), (tgt, D, 0)], [(D, F32)], [D], name="post_fwd_loss")


def _post_bwd(g, y, gain):
    def fn(gv, yv, gn):
        dx, dg = _rms_bwd(gv, yv, gn)
        return (dx,), (dg,)

    return _rowwise(fn, [(g, D, 0), (y, D, 0), (gain, D, 0)], [(D, BF16)], [D], name="post_bwd")


def _pre_bwd(g, dh, x, gain):
    def fn(gv, dhv, xv, gn):
        dx, dg = _rms_bwd(dhv, xv, gn)
        return (gv + dx,), (dg,)

    return _rowwise(fn, [(g, D, 0), (dh, D, 0), (x, D, 0), (gain, D, 0)], [(D, F32)], [D], name="pre_bwd")


def _pool(u_arr, col_block, transpose, out_dtype, into=None, tc=256):
    n_t = POOL_W // tc
    per_group = POOL_GROUP // tc

    def body(u_ref, *rest):
        o_ref = rest[-1]
        grp = pl.program_id(0) // per_group
        t = lax.broadcasted_iota(jnp.int32, (S, 1), 0)
        for g in range(POOL_W // POOL_GROUP):
            @pl.when(grp == g)
            def _(g=g):
                xv = u_ref[...]
                cnt = jnp.minimum(t + 1, 2 << g).astype(F32)
                cur = xv / cnt if transpose else xv
                for k in (1, 2, 4, 8)[:g + 1]:
                    if transpose:
                        cur = cur + jnp.where(t < S - k, pltpu.roll(cur, S - k, 0), 0.0)
                    else:
                        cur = cur + jnp.where(t >= k, pltpu.roll(cur, k, 0), 0.0)
                res = cur - xv if transpose else cur / cnt - xv
                o_ref[...] = res.astype(o_ref.dtype)

    in_specs = [pl.BlockSpec((S, tc), lambda c: (0, col_block * n_t + c))]
    args = [u_arr]
    if into is not None:
        in_specs.append(pl.BlockSpec(memory_space=pl.ANY))
        args.append(into)
    return pl.pallas_call(
        body, name="pool_bwd" if transpose else "pool_fwd", grid=(n_t,),
        in_specs=in_specs,
        out_specs=pl.BlockSpec((S, tc), lambda c: (0, c)),
        out_shape=SDS((S, POOL_W) if into is None else into.shape, out_dtype),
        input_output_aliases={} if into is None else {1: 0},
        compiler_params=_cparams(("parallel",)),
    )(*args)


def _rope_tables(zero):
    pos = jnp.arange(S, dtype=jnp.int32).astype(F32) + zero
    inv_freq = ROPE_THETA ** (-jnp.arange(0, ROT_DIM, 2, dtype=F32) / ROT_DIM)
    ang = pos[:, None] * inv_freq[None, :]
    cos8, sin8 = jnp.cos(ang), jnp.sin(ang)
    half = ROT_DIM // 2
    zeros = jnp.zeros((S, HEAD_DIM - ROT_DIM), F32)
    cos = jnp.concatenate([cos8, cos8, jnp.ones((S, HEAD_DIM - ROT_DIM), F32)], axis=1)
    sin = jnp.concatenate([-sin8, sin8, zeros], axis=1)
    rep = LANES // HEAD_DIM
    lane = jnp.arange(LANES)
    dim = lane % HEAD_DIM
    partner = jnp.where(dim < half, lane + half, jnp.where(dim < ROT_DIM, lane - half, -1))
    swap = (lane[:, None] == partner[None, :]).astype(BF16)
    return jnp.tile(cos, (1, rep)), jnp.tile(sin, (1, rep)), swap


def _rotate(xv, cos, sin, swap, transpose):
    rep = xv.shape[1] // LANES
    wide = lambda tab: jnp.concatenate([tab] * rep, axis=1)
    xb = xv.astype(BF16)
    partner = jnp.concatenate([lax.dot_general(xb[:, t * LANES:(t + 1) * LANES], swap, NN, preferred_element_type=F32)
                               for t in range(rep)], axis=1)
    mixed = partner * wide(sin)
    return xv * wide(cos) - mixed if transpose else xv * wide(cos) + mixed


def _qkv_prep(proj, tables):
    cos, sin, swap = tables

    def fn(x, c, s, sw):
        rot = _rotate(x[:, :2 * ATT_W], c, s, sw, False)
        return (jnp.concatenate([(rot[:, :ATT_W] * HEAD_DIM ** -0.5).astype(BF16), rot[:, ATT_W:].astype(BF16),
                                 x[:, 2 * ATT_W:].astype(BF16)], axis=1),), ()

    (qkv,) = _rowwise(fn, [(proj, 3 * ATT_W, 0), (cos, LANES, 0), (sin, LANES, 0), (swap, LANES, 0)],
                      [(3 * ATT_W, BF16)], name="qkv_prep")
    return qkv


ATT_T = 512


def _multiplicity(delta):
    ok = delta >= 0
    near = jnp.where(ok & (delta <= 128), 1.0, 0.0)
    mid = jnp.where(ok & (delta <= 512) & ((delta & 3) == 0), 1.0, 0.0)
    far = jnp.where(ok & ((delta & 15) == 0), 1.0, 0.0)
    return near + mid + far


def _attention_bias(zero):
    t = ATT_T
    pos = jnp.arange(t, dtype=jnp.int32) + jnp.asarray(zero).astype(jnp.int32)
    delta = jnp.arange(S // t, dtype=jnp.int32)[:, None, None] * t + pos[None, :, None] - pos[None, None, :]
    mult = _multiplicity(delta)
    return jnp.where(mult > 0.0, jnp.log(jnp.maximum(mult, 1.0)), -1e30).astype(F32)


def _head_split(v, first):
    zero = jnp.zeros_like(v)
    return [jnp.where(first, v, zero), jnp.where(first, zero, v)]


def _flash_fwd(qkv, bias):
    t = ATT_T
    n_hp = ATT_W // LANES

    def body(q_ref, k_ref, v_ref, b_ref, o_ref, lse_ref):
        i = pl.program_id(1)
        first = lax.broadcasted_iota(jnp.int32, (1, LANES), 1) < HEAD_DIM
        qs = _head_split(q_ref[...], first)

        def kv_step(j, carry):
            m0, l0, m1, l1, acc = carry
            off = pl.multiple_of(j * t, t)
            kb = k_ref[pl.ds(off, t), :]
            vs = _head_split(v_ref[pl.ds(off, t), :], first)
            bias_t = b_ref[i - j]
            new = []
            pv = None
            for h, (m_prev, l_prev) in enumerate(((m0, l0), (m1, l1))):
                s = lax.dot_general(qs[h], kb, NT, preferred_element_type=F32) + bias_t
                m_new = jnp.maximum(m_prev, jnp.max(s, axis=1, keepdims=True))
                p = jnp.exp(s - m_new)
                alpha = jnp.exp(m_prev - m_new)
                l_new = alpha * l_prev + jnp.sum(p, axis=1, keepdims=True)
                d = lax.dot_general(p.astype(BF16), vs[h], NN, preferred_element_type=F32)
                pv = d if pv is None else pv + d
                new.append((m_new, l_new, alpha))
            acc = acc * jnp.where(first, new[0][2], new[1][2]) + pv
            return new[0][0], new[0][1], new[1][0], new[1][1], acc

        neg = jnp.full((t, 1), -1e30, F32)
        zero = jnp.zeros((t, 1), F32)
        m0, l0, m1, l1, acc = lax.fori_loop(0, i + 1, kv_step, (neg, zero, neg, zero, jnp.zeros((t, LANES), F32)))
        o_ref[...] = acc * jnp.where(first, 1.0 / l0, 1.0 / l1)
        lse_ref[...] = jnp.where(first, m0 + jnp.log(l0), m1 + jnp.log(l1))

    blk = pl.BlockSpec((t, LANES), lambda hp, i: (i, hp))
    k_full = pl.BlockSpec((S, LANES), lambda hp, i: (0, n_hp + hp))
    v_full = pl.BlockSpec((S, LANES), lambda hp, i: (0, 2 * n_hp + hp))
    return pl.pallas_call(
        body, name="flash_fwd", grid=(n_hp, S // t),
        in_specs=[blk, k_full, v_full, pl.BlockSpec((S // t, t, t), lambda hp, i: (0, 0, 0))], out_specs=[blk, blk],
        out_shape=[SDS((S, ATT_W), F32), SDS((S, ATT_W), F32)],
        compiler_params=_cparams(("parallel", "arbitrary")),
    )(qkv, qkv, qkv, bias)


def _flash_bwd(qkv, o, do, lse, bias, after=()):
    t = ATT_T
    n_hp = ATT_W // LANES
    n_t = S // t

    def body(q_ref, k_ref, v_ref, o_ref, do_ref, lse_ref, b_ref, *rest):
        dq_ref, dk_ref, dv_ref = rest[-3:]
        j = pl.program_id(1)
        first = lax.broadcasted_iota(jnp.int32, (1, LANES), 1) < HEAD_DIM

        @pl.when(j == 0)
        def _():
            dq_ref[...] = jnp.zeros_like(dq_ref)

        kb = k_ref[...]
        vb = v_ref[...]
        ks = _head_split(kb, first)

        def q_step(i, carry):
            dk_acc, dv_acc = carry
            rows = pl.ds(pl.multiple_of(i * t, t), t)
            qs = _head_split(q_ref[rows, :], first)
            dob = do_ref[rows, :]
            prod = dob * o_ref[rows, :]
            d_all = jnp.sum(prod, axis=1, keepdims=True)
            d0 = jnp.sum(jnp.where(first, prod, 0.0), axis=1, keepdims=True)
            lse_b = lse_ref[rows, :]
            lse0 = jnp.max(jnp.where(first, lse_b, -jnp.inf), axis=1, keepdims=True)
            lse1 = jnp.max(jnp.where(first, -jnp.inf, lse_b), axis=1, keepdims=True)
            dos = _head_split(dob.astype(BF16), first)
            bias_t = b_ref[i - j]
            dq_t = jnp.zeros((t, LANES), F32)
            for h, (lse_h, d_h) in enumerate(((lse0, d0), (lse1, d_all - d0))):
                s = lax.dot_general(qs[h], kb, NT, preferred_element_type=F32)
                p = jnp.exp(s + (bias_t - lse_h))
                dp = lax.dot_general(dos[h], vb, NT, preferred_element_type=F32)
                ds = (p * (dp - d_h)).astype(BF16)
                dv_acc = dv_acc + lax.dot_general(p.astype(BF16), dos[h], TN, preferred_element_type=F32)
                dk_acc = dk_acc + lax.dot_general(ds, qs[h], TN, preferred_element_type=F32)
                dq_t = dq_t + lax.dot_general(ds, ks[h], NN, preferred_element_type=F32)
            dq_ref[rows, :] += dq_t
            return dk_acc, dv_acc

        zero = jnp.zeros((t, LANES), F32)
        dk_acc, dv_acc = lax.fori_loop(j, n_t, q_step, (zero, zero))
        dk_ref[...] = dk_acc
        dv_ref[...] = dv_acc

    blk = pl.BlockSpec((t, LANES), lambda hp, j: (j, hp))
    full = pl.BlockSpec((S, LANES), lambda hp, j: (0, hp))
    k_blk = pl.BlockSpec((t, LANES), lambda hp, j: (j, n_hp + hp))
    v_blk = pl.BlockSpec((t, LANES), lambda hp, j: (j, 2 * n_hp + hp))
    return pl.pallas_call(
        body, name="flash_bwd", grid=(n_hp, n_t),
        in_specs=([full, k_blk, v_blk, full, full, full, pl.BlockSpec((n_t, t, t), lambda hp, j: (0, 0, 0))]
                  + [pl.BlockSpec(memory_space=pl.ANY)] * len(after)),
        out_specs=[full, blk, blk],
        out_shape=[SDS((S, ATT_W), F32)] * 3,
        compiler_params=_cparams(("parallel", "arbitrary")),
    )(qkv, qkv, qkv, o, do, lse, bias, *after)


SCAN_T = 256
SCAN_GROUP = 8
SCAN_STEPS = (1, 2, 4)
ST_ROWS = 2 * N_CPLX // LANES
HALF = ST_ROWS // 2


def _scan_tables(lam_t):
    lam = lax.complex(lam_t[:HALF].reshape(N_CPLX), lam_t[HALF:].reshape(N_CPLX))
    pows = jnp.cumprod(jnp.broadcast_to(lam, (SCAN_GROUP, N_CPLX)), axis=0)
    shifts = jnp.asarray(SCAN_STEPS)
    sub = jnp.arange(SCAN_GROUP)[None, :, None]
    steps = pows[shifts - 1][:, None, :]
    fwd = jnp.concatenate([jnp.where(sub >= shifts[:, None, None], steps, 0.0), pows[None]], axis=0)
    bwd = jnp.concatenate([jnp.where(sub <= SCAN_GROUP - 1 - shifts[:, None, None], jnp.conj(steps), 0.0),
                           jnp.conj(pows)[None, ::-1]], axis=0)

    def pack(tabs):
        return jnp.concatenate([jnp.real(tabs), jnp.imag(tabs)], axis=-1).astype(F32)

    return pack(fwd), pack(bwd)


def _cmul_add(xr, xi, lr, li, sr, si):
    return xr + lr * sr - li * si, xi + lr * si + li * sr


def _group_scan(xr, xi, tab_ref, cr, ci, reverse):
    for j, k in enumerate(SCAN_STEPS):
        shift = SCAN_GROUP - k if reverse else k
        xr, xi = _cmul_add(xr, xi, tab_ref[j, :, :N_CPLX], tab_ref[j, :, N_CPLX:],
                           pltpu.roll(xr, shift, 0), pltpu.roll(xi, shift, 0))
    return _cmul_add(xr, xi, tab_ref[3, :, :N_CPLX], tab_ref[3, :, N_CPLX:],
                     jnp.broadcast_to(cr, (SCAN_GROUP, N_CPLX)), jnp.broadcast_to(ci, (SCAN_GROUP, N_CPLX)))


SSM_SUPER = 4
SB_ROWS = SSM_W // SSM_SUPER
SB_COLS = N_CPLX // SSM_SUPER


def _super_blocks():
    return [(slice(b * SB_ROWS, (b + 1) * SB_ROWS), slice(h * SB_COLS, (h + 1) * SB_COLS),
             slice(h * N_CPLX + b * SB_COLS, h * N_CPLX + (b + 1) * SB_COLS))
            for b in range(SSM_SUPER) for h in range(2)]


def _dot16(a, b, dims):
    return lax.dot_general(a.astype(BF16), b.astype(BF16), dims, preferred_element_type=F32)


def _s5_fwd(tab, u_arr, u_cols, w_b, w_ct):
    nc = N_CPLX

    def body(tab_ref, u_ref, wb_ref, wct_ref, st_ref, y_ref, carry, bu_scr):
        @pl.when(pl.program_id(0) == 0)
        def _():
            carry[...] = jnp.zeros_like(carry)

        for rows_b, cols_c, cols_s in _super_blocks():
            bu_scr[:, cols_s] = _dot16(u_ref[:, rows_b], wb_ref[rows_b, cols_c], NN)

        def group(a, c):
            rows = pl.ds(pl.multiple_of(a * SCAN_GROUP, SCAN_GROUP), SCAN_GROUP)
            xr, xi = _group_scan(bu_scr[rows, :nc], bu_scr[rows, nc:], tab_ref, c[0], c[1], False)
            st_ref[rows, :nc] = xr
            st_ref[rows, nc:] = xi
            return xr[SCAN_GROUP - 1:SCAN_GROUP, :], xi[SCAN_GROUP - 1:SCAN_GROUP, :]

        cr, ci = lax.fori_loop(0, SCAN_T // SCAN_GROUP, group, (carry[:, :nc], carry[:, nc:]), unroll=2)
        carry[:, :nc] = cr
        carry[:, nc:] = ci

        for b in range(SSM_SUPER):
            (rows_b, cols_re, st_re), (_, cols_im, st_im) = _super_blocks()[2 * b:2 * b + 2]
            y_ref[:, rows_b] = (_dot16(st_ref[:, st_re], wct_ref[rows_b, cols_re], NT)
                                + _dot16(st_ref[:, st_im], wct_ref[rows_b, cols_im], NT))

    const = lambda shape: pl.BlockSpec(shape, lambda i: (0,) * len(shape))
    return pl.pallas_call(
        body, name="s5_fwd", grid=(S // SCAN_T,),
        in_specs=[const((4, SCAN_GROUP, 2 * nc)), pl.BlockSpec((SCAN_T, SSM_W), lambda i: (i, u_cols[0] // SSM_W)),
                  const((SSM_W, 2 * SB_COLS)), const((SSM_W, 2 * SB_COLS))],
        out_specs=[pl.BlockSpec((SCAN_T, 2 * nc), lambda i: (i, 0)), pl.BlockSpec((SCAN_T, SSM_W), lambda i: (i, 0))],
        out_shape=[SDS((S, 2 * nc), F32), SDS((S, SSM_W), F32)],
        scratch_shapes=[pltpu.VMEM((1, 2 * nc), F32), pltpu.VMEM((SCAN_T, 2 * nc), F32)],
        compiler_params=_cparams(("arbitrary",)),
    )(tab, u_arr, w_b, w_ct)


def _s5_bwd(tab, dy, states, u_arr, u_cols, w_b, w_ct):
    n_blk = S // SCAN_T
    nc = N_CPLX

    def body(tab_ref, dy_ref, x_ref, u_ref, wb_ref, wct_ref, du_ref, dlam_ref, dwb_ref, dwct_ref,
             carry, acc, d_scr, g_ref):
        i = pl.program_id(0)

        @pl.when(i == 0)
        def _():
            carry[...] = jnp.zeros_like(carry)
            acc[...] = jnp.zeros_like(acc)
            dwb_ref[...] = jnp.zeros_like(dwb_ref)
            dwct_ref[...] = jnp.zeros_like(dwct_ref)

        for rows_b, cols_c, cols_s in _super_blocks():
            d_scr[:, cols_s] = _dot16(dy_ref[:, rows_b], wct_ref[rows_b, cols_c], NN)

        last_row = lax.broadcasted_iota(jnp.int32, (SCAN_GROUP, 1), 0) == SCAN_GROUP - 1
        d_ref = d_scr

        def group(j, c):
            cr, ci = c
            rows = pl.ds(pl.multiple_of((SCAN_T // SCAN_GROUP - 1 - j) * SCAN_GROUP, SCAN_GROUP), SCAN_GROUP)
            gr, gi = _group_scan(d_ref[rows, :nc], d_ref[rows, nc:], tab_ref, cr, ci, True)
            g_ref[rows, :nc] = gr
            g_ref[rows, nc:] = gi
            nr = jnp.where(last_row, jnp.broadcast_to(cr, (SCAN_GROUP, nc)), pltpu.roll(gr, SCAN_GROUP - 1, 0))
            ni = jnp.where(last_row, jnp.broadcast_to(ci, (SCAN_GROUP, nc)), pltpu.roll(gi, SCAN_GROUP - 1, 0))
            sr, si = x_ref[rows, :nc], x_ref[rows, nc:]
            acc[:, :nc] += nr * sr + ni * si
            acc[:, nc:] += ni * sr - nr * si
            return gr[0:1, :], gi[0:1, :]

        cr, ci = lax.fori_loop(0, SCAN_T // SCAN_GROUP, group, (carry[:, :nc], carry[:, nc:]), unroll=2)
        carry[:, :nc] = cr
        carry[:, nc:] = ci

        for b in range(SSM_SUPER):
            (rows_b, cols_re, st_re), (_, cols_im, st_im) = _super_blocks()[2 * b:2 * b + 2]
            du_ref[:, rows_b] = (_dot16(g_ref[:, st_re], wb_ref[rows_b, cols_re], NT)
                                 + _dot16(g_ref[:, st_im], wb_ref[rows_b, cols_im], NT))
            for cols_c, cols_s in ((cols_re, st_re), (cols_im, st_im)):
                dwb_ref[rows_b, cols_c] += _dot16(u_ref[:, rows_b], g_ref[:, cols_s], TN)
                dwct_ref[rows_b, cols_c] += _dot16(dy_ref[:, rows_b], x_ref[:, cols_s], TN)

        @pl.when(i == n_blk - 1)
        def _():
            dlam_ref[...] = jnp.sum(acc[...], axis=0, keepdims=True)

    const = lambda shape: pl.BlockSpec(shape, lambda i: (0,) * len(shape))
    rows = lambda width, col_block=0: pl.BlockSpec((SCAN_T, width), lambda i: (n_blk - 1 - i, col_block))
    maps = const((SSM_W, 2 * SB_COLS))
    return pl.pallas_call(
        body, name="s5_bwd", grid=(n_blk,),
        in_specs=[const((4, SCAN_GROUP, 2 * nc)), rows(SSM_W), rows(2 * nc), rows(SSM_W, u_cols[0] // SSM_W), maps, maps],
        out_specs=[rows(SSM_W), const((1, 2 * nc)), maps, maps],
        out_shape=[SDS((S, SSM_W), F32), SDS((1, 2 * nc), F32), SDS((SSM_W, 2 * SB_COLS), F32),
                   SDS((SSM_W, 2 * SB_COLS), F32)],
        scratch_shapes=[pltpu.VMEM((1, 2 * nc), F32), pltpu.VMEM((SCAN_GROUP, 2 * nc), F32),
                        pltpu.VMEM((SCAN_T, 2 * nc), F32), pltpu.VMEM((SCAN_T, 2 * nc), F32)],
        compiler_params=_cparams(("arbitrary",)),
    )(tab, dy, states, u_arr, w_b, w_ct)


def _ssm_prep(a_re, a_im, log_dt, b_re, b_im, c_re, c_im):
    lam = lax.complex(a_re, a_im)
    dt = jnp.exp(log_dt)[:, None]
    lam_bar = jnp.exp(lam * dt)
    b_bar = ((lam_bar - 1.0) / lam)[..., None] * lax.complex(b_re, b_im)
    lam_t = jnp.concatenate([jnp.real(lam_bar).reshape(HALF, LANES), jnp.imag(lam_bar).reshape(HALF, LANES)], axis=0)
    groups_per_super = SSM_GROUPS // SSM_SUPER
    on_diag = ((lax.broadcasted_iota(jnp.int32, (SSM_W, SB_COLS), 0) // SSM_GROUP) % groups_per_super
               == lax.broadcasted_iota(jnp.int32, (SSM_W, SB_COLS), 1) // SSM_STATE)

    repeat = (lax.broadcasted_iota(jnp.int32, (SSM_STATE, SB_COLS), 0)
              == lax.broadcasted_iota(jnp.int32, (SSM_STATE, SB_COLS), 1) % SSM_STATE).astype(F32)

    def compact(m):
        tiled = jnp.dot(m.reshape(SSM_W, SSM_STATE), repeat, precision=lax.Precision.HIGHEST)
        return jnp.where(on_diag, tiled, 0.0)

    w_b = jnp.concatenate([compact(jnp.real(b_bar).transpose(0, 2, 1)),
                           compact(jnp.imag(b_bar).transpose(0, 2, 1))], axis=1)
    w_ct = jnp.concatenate([compact(c_re), -compact(c_im)], axis=1)
    return lam_t, w_b, w_ct


U_SSM_COLS = (4 * ATT_W, SSM_W)


def _row(v):
    return v.reshape(1, -1)


def _even_fwd(x, h, tail, pre, post, w_in, late_w, glu_b, ssm_d, prep, tables):
    lam_t, w_b, w_ct, scan_fwd_tab, scan_bwd_tab = prep
    proj = _mm(h, w_in, "nn", F32, b_blocks=True)
    qkv = _qkv_prep(proj, tables[:3])
    att, lse = _flash_fwd(qkv, tables[3])
    w_out, glu_w = late_w(att)
    states, y = _s5_fwd(scan_fwd_tab, proj, U_SSM_COLS, w_b, w_ct)

    def gate(yv, uv, dv, att_v, ga, gs, gw, bv):
        z1v = _gelu_and_grad(yv + dv * uv)[0]
        linv = _dot16(z1v, gw, NN)
        ssm_out = z1v * _sigmoid(linv + bv)
        merged_v = jnp.concatenate([att_v * _silu_and_grad(ga)[0], ssm_out * _silu_and_grad(gs)[0]], axis=1)
        return (z1v, linv, merged_v), ()

    z1, lin, merged = _rowwise(gate, [(y, SSM_W, 0), (proj, SSM_W, 8), (ssm_d, SSM_W, 0), (att, ATT_W, 0),
                                      (proj, ATT_W, 3), (proj, SSM_W, 9), (glu_w, SSM_W, 0), (glu_b, SSM_W, 0)],
                               [(SSM_W, F32), (SSM_W, F32), (EVEN_OUT, BF16)], name="even_gate_fwd")
    yout = _mm(merged, w_out, "nn", F32)
    saved = (x, h, proj, qkv, att, lse, states, y, z1, lin, merged, yout, w_out, glu_w, scan_bwd_tab)
    return tail(x, yout, post) + (saved,)


def _even_bwd(g, saved, pre, post, w_in, late_w, glu_b, ssm_d, prep, tables, on_w, on_ssm):
    x, h, proj, qkv, att, lse, states, y, z1, lin, merged, yout, w_out, glu_w, scan_bwd_tab = saved
    lam_t, w_b, w_ct = prep[:3]
    dyout, dpost = _post_bwd(g, yout, post)
    dmerged = _mm(dyout, w_out, "nt", F32)
    dw_out = _mm(merged, dyout, "tn", BF16)

    def gate_bwd(dm_a, dm_s, att_v, ga, gs, z1v, linv, bv):
        sa, dsa = _silu_and_grad(ga)
        ss, dss = _silu_and_grad(gs)
        sig = _sigmoid(linv + bv)
        ssm_out = z1v * sig
        dssm = dm_s * ss
        dlin = dssm * z1v * sig * (1.0 - sig)
        return (dm_a * sa, dm_a * att_v * dsa, dm_s * ssm_out * dss, dssm * sig, dlin), (dlin,)

    datt, dg_att, dg_ssm, dz1a, dlin, dglu_b = _rowwise(
        gate_bwd, [(dmerged, ATT_W, 0), (dmerged, SSM_W, 2), (att, ATT_W, 0), (proj, ATT_W, 3), (proj, SSM_W, 9),
                   (z1, SSM_W, 0), (lin, SSM_W, 0), (glu_b, SSM_W, 0)],
        [(ATT_W, F32), (ATT_W, BF16), (SSM_W, BF16), (SSM_W, F32), (SSM_W, BF16)], [SSM_W], name="even_gate_bwd")
    dglu_w = _mm(z1, dlin, "tn", BF16)

    def act1_bwd(da, dl, gw, yv, uv, dv):
        dpre = (da + _dot16(dl, gw, NT)) * _gelu_and_grad(yv + dv * uv)[1]
        return (dpre, dpre * dv), (dpre * uv,)

    sent_late_w = on_w(dict(w_out=dw_out, glu_w=dglu_w))
    dy, du_direct, dd = _rowwise(act1_bwd, [(dz1a, SSM_W, 0), (dlin, SSM_W, 0), (glu_w, SSM_W, 0), (y, SSM_W, 0),
                                            (proj, SSM_W, 8), (ssm_d, SSM_W, 0)], [(SSM_W, BF16), (SSM_W, F32)],
                                 [SSM_W], name="ssm_act_bwd", after=(sent_late_w,))
    du_state, dlam_row, dw_b, dw_ct = _s5_bwd(scan_bwd_tab, dy, states, proj, U_SSM_COLS, w_b, w_ct)
    dlam = jnp.concatenate([dlam_row[0, :N_CPLX].reshape(HALF, LANES), dlam_row[0, N_CPLX:].reshape(HALF, LANES)],
                           axis=0)
    sent_ssm = on_ssm((dlam, dw_b, dw_ct))
    dq, dk, dv = _flash_bwd(qkv, att, datt, lse, tables[3], after=() if sent_ssm is None else (sent_ssm,))

    def assemble(dqv, dkv, dvv, dga, dua, dub, dgs, c, s, sw):
        rot = _rotate(jnp.concatenate([dqv, dkv], axis=1), c, s, sw, True)
        return (jnp.concatenate([(rot[:, :ATT_W] * HEAD_DIM ** -0.5).astype(BF16), rot[:, ATT_W:].astype(BF16),
                                 dvv.astype(BF16), dga, (dua + dub).astype(BF16), dgs], axis=1),), ()

    (dproj,) = _rowwise(assemble, [(dq, ATT_W, 0), (dk, ATT_W, 0), (dv, ATT_W, 0), (dg_att, ATT_W, 0),
                                   (du_state, SSM_W, 0), (du_direct, SSM_W, 0), (dg_ssm, SSM_W, 0),
                                   (tables[0], LANES, 0), (tables[1], LANES, 0), (tables[2], LANES, 0)],
                        [(EVEN_IN, BF16)], name="dproj_assemble")
    dw_in = _mm(h, dproj, "tn", BF16, out_blocks=True)
    sent = on_w(dict(w_in=dw_in))
    dh = _mm(dproj, w_in, "nt", F32, b_blocks=True, after=(sent,))
    g_prev, dpre = _pre_bwd(g, dh, x, pre)
    return g_prev, dict(pre=dpre, post=dpost, glu_b=dglu_b, ssm_d=dd)


def _odd_fwd(x, h, tail, pre, post, w_in, pool_w, pool_scale, w_out):
    proj = _mm(h, w_in, "nn", F32, b_blocks=True)
    mixed = _pool(proj, 0, False, BF16)
    ylin = _gmm(mixed, pool_w, "nn", F32)

    def gate(yl, gt, sc):
        return (yl * sc * _silu_and_grad(gt)[0],), ()

    (z,) = _rowwise(gate, [(ylin, POOL_W, 0), (proj, POOL_W, 1), (pool_scale, POOL_W, 0)], [(POOL_W, BF16)],
                    name="odd_gate_fwd")
    yout = _mm(z, w_out, "nn", F32)
    return tail(x, yout, post) + ((x, h, proj, mixed, ylin, z, yout),)


def _odd_bwd(g, saved, pre, post, w_in, pool_w, pool_scale, w_out, on_w):
    x, h, proj, mixed, ylin, z, yout = saved
    dyout, dpost = _post_bwd(g, yout, post)
    dz = _mm(dyout, w_out, "nt", F32)
    dw_out = _mm(z, dyout, "tn", BF16)

    def gate_bwd(dzv, yl, gt, sc):
        sg, dsg = _silu_and_grad(gt)
        tt = dzv * sg
        return (tt * sc, dzv * yl * sc * dsg), (tt * yl,)

    dylin, dproj_gate, dscale = _rowwise(gate_bwd, [(dz, POOL_W, 0), (ylin, POOL_W, 0), (proj, POOL_W, 1),
                                                    (pool_scale, POOL_W, 0)],
                                         [(POOL_W, BF16), (POOL_W, BF16, ODD_IN, 1)], [POOL_W], name="odd_gate_bwd")
    dmixed = _gmm(dylin, pool_w, "nt", F32)
    dpool_w = _gmm(mixed, dylin, "tn", BF16)
    dproj = _pool(dmixed, 0, True, BF16, into=dproj_gate)
    dw_in = _mm(h, dproj, "tn", BF16, out_blocks=True)
    sent = on_w(dict(w_in=dw_in, w_out=dw_out, pool_w=dpool_w))
    dh = _mm(dproj, w_in, "nt", F32, b_blocks=True, after=(sent,))
    g_prev, dpre = _pre_bwd(g, dh, x, pre)
    return g_prev, dict(pre=dpre, post=dpost, pool_scale=dscale)


def _my_index():
    return 4 * lax.axis_index("x") + 2 * lax.axis_index("y") + lax.axis_index("c")


HBM_SPEC = pl.BlockSpec(memory_space=pltpu.HBM)
SEM_SPEC = pl.BlockSpec(memory_space=pltpu.SEMAPHORE)
SPLIT_EFFECT = pltpu.SideEffectType.DATAFLOW_SIDE_EFFECTING


def _device_of(j):
    return (j // 4, (j // 2) % 2, j % 2)


def _split_copy(srcs, lands, send_sems, recv_sems, gather, i, j, dst_slot, recv_slot):
    return pltpu.make_async_remote_copy(
        src_ref=srcs[i] if gather else srcs[i].at[j], dst_ref=lands[i].at[dst_slot],
        send_sem=send_sems.at[i * N_DEV + j], recv_sem=recv_sems.at[i * N_DEV + recv_slot],
        device_id=_device_of(j), device_id_type=MESH_ID)


def _own_copy(srcs, lands, send_sems, gather, i, me):
    return pltpu.make_async_copy(srcs[i] if gather else srcs[i].at[me], lands[i].at[me], send_sems.at[i * N_DEV + me])


def _xchg_start(name, srcs, gather, after=()):
    n = len(srcs)
    n_in = n + len(after)

    def body(*refs):
        src_refs = refs[:n]
        send_sems, recv_sems, token = refs[n_in], refs[n_in + 1], refs[-1]
        land_refs = refs[n_in + 2 + n:n_in + 2 + 2 * n]
        me = _my_index()
        for j in range(N_DEV):
            @pl.when(me != j)
            def _(j=j):
                for i in range(n):
                    _split_copy(src_refs, land_refs, send_sems, recv_sems, gather, i, j, me, me).start()
        for i in range(n):
            _own_copy(src_refs, land_refs, send_sems, gather, i, me).start()
        token[...] = jnp.zeros_like(token)

    land_shapes = [((N_DEV,) + a.shape) if gather else a.shape for a in srcs]
    thru = ([pltpu.HBM(a.shape, a.dtype) for a in srcs] + [pltpu.HBM(s, a.dtype) for s, a in zip(land_shapes, srcs)])
    res = pl.pallas_call(
        body, name=name,
        out_shape=(pltpu.SemaphoreType.DMA((n * N_DEV,)), pltpu.SemaphoreType.DMA((n * N_DEV,)), *thru,
                   SDS((8, LANES), F32)),
        in_specs=[HBM_SPEC] * n + [pl.BlockSpec(memory_space=pl.ANY)] * len(after),
        out_specs=(SEM_SPEC, SEM_SPEC, *([HBM_SPEC] * (2 * n)), pl.BlockSpec(memory_space=pltpu.VMEM)),
        input_output_aliases={i: 2 + i for i in range(n)},
        compiler_params=pltpu.CompilerParams(has_side_effects=SPLIT_EFFECT),
    )(*[pltpu.with_memory_space_constraint(a, pltpu.HBM) for a in srcs], *after)
    return res[0], res[1], list(res[2:2 + n]), list(res[2 + n:2 + 2 * n]), res[-1]


def _xchg_wait(name, started, gather, after):
    send_sems, recv_sems, srcs, lands, _ = started
    n = len(srcs)

    def body(*refs):
        src_refs, land_refs = refs[:n], refs[n:2 * n]
        send_r, recv_r = refs[2 * n], refs[2 * n + 1]
        me = _my_index()
        for j in range(N_DEV):
            @pl.when(me != j)
            def _(j=j):
                for i in range(n):
                    _split_copy(src_refs, land_refs, send_r, recv_r, gather, i, j, me, me).wait_send()
                    _split_copy(src_refs, land_refs, send_r, recv_r, gather, i, j, j, j).wait_recv()
        for i in range(n):
            _own_copy(src_refs, land_refs, send_r, gather, i, me).wait()

    thru = [pltpu.HBM(a.shape, a.dtype) for a in list(srcs) + list(lands)]
    res = pl.pallas_call(
        body, name=name, out_shape=tuple(thru),
        in_specs=[HBM_SPEC] * (2 * n) + [SEM_SPEC, SEM_SPEC] + [pl.BlockSpec(memory_space=pl.ANY)] * len(after),
        out_specs=tuple([HBM_SPEC] * (2 * n)),
        input_output_aliases={i: i for i in range(2 * n)},
        compiler_params=pltpu.CompilerParams(has_side_effects=SPLIT_EFFECT),
    )(*srcs, *lands, send_sems, recv_sems, *after)
    return list(res[n:])


def _adam_layer(w, slots, m, v, layer, name, into=None):
    n_l, r, c = w.shape
    ns = slots.shape[0]
    tr = r
    while tr * c * 4 > (1 << 20) and tr % 16 == 0:
        tr //= 2
    assert r % tr == 0

    def body(w_ref, g_ref, m_ref, v_ref, *rest):
        go_ref, d_ref, mo_ref, vo_ref = rest[-4:]
        g = g_ref[0].astype(F32)
        for s in range(1, ns):
            g = g + g_ref[s].astype(F32)
        mn = ADAM_B1 * m_ref[...] + (1.0 - ADAM_B1) * g
        vn = ADAM_B2 * v_ref[...] + (1.0 - ADAM_B2) * (g * g)
        m_hat = mn / (1.0 - ADAM_B1 ** ADAM_STEP)
        v_hat = vn / (1.0 - ADAM_B2 ** ADAM_STEP)
        go_ref[...] = g
        d_ref[...] = -ADAM_LR * (m_hat / (jnp.sqrt(v_hat) + ADAM_EPS) + ADAM_WD * w_ref[...])
        mo_ref[...] = mn
        vo_ref[...] = vn

    blk = pl.BlockSpec((None, tr, c), lambda i: (layer, i, 0))
    earlier = () if into is None else tuple(into)
    return pl.pallas_call(
        body, name=name, grid=(r // tr,),
        in_specs=[blk, pl.BlockSpec((ns, tr, c), lambda i: (0, i, 0)), blk, blk]
        + [pl.BlockSpec(memory_space=pl.ANY)] * len(earlier),
        out_specs=[blk] * 4, out_shape=[SDS((n_l, r, c), F32)] * 4,
        input_output_aliases={4 + q: q for q in range(len(earlier))},
        compiler_params=_cparams(("arbitrary",)),
    )(*_in_hbm((w, slots, m, v)), *earlier)


def _adam(w, gslots, m, v, name):
    r, c = w.shape
    ns = gslots.shape[0]
    tr = r
    while tr * c * 4 > (1 << 20) and tr % 16 == 0:
        tr //= 2
    assert r % tr == 0

    def body(w_ref, g_ref, m_ref, v_ref, go_ref, d_ref, mo_ref, vo_ref):
        g = g_ref[0].astype(F32)
        for s in range(1, ns):
            g = g + g_ref[s].astype(F32)
        wv = w_ref[...]
        mn = ADAM_B1 * m_ref[...] + (1.0 - ADAM_B1) * g
        vn = ADAM_B2 * v_ref[...] + (1.0 - ADAM_B2) * (g * g)
        m_hat = mn / (1.0 - ADAM_B1 ** ADAM_STEP)
        v_hat = vn / (1.0 - ADAM_B2 ** ADAM_STEP)
        go_ref[...] = g
        d_ref[...] = -ADAM_LR * (m_hat / (jnp.sqrt(v_hat) + ADAM_EPS) + ADAM_WD * wv)
        mo_ref[...] = mn
        vo_ref[...] = vn

    blk = pl.BlockSpec((tr, c), lambda i: (i, 0))
    return pl.pallas_call(
        body, name=name, grid=(r // tr,),
        in_specs=[blk, pl.BlockSpec((ns, tr, c), lambda i: (0, i, 0)), blk, blk],
        out_specs=[blk] * 4, out_shape=[SDS((r, c), F32)] * 4,
        compiler_params=_cparams(("parallel",)),
    )(w, gslots, m, v)


def _sum_slots(slots, name):
    ns, r, c = slots.shape

    def body(g_ref, o_ref):
        g = g_ref[0]
        for s in range(1, ns):
            g = g + g_ref[s]
        o_ref[...] = g

    return pl.pallas_call(
        body, name=name, grid=(1,),
        in_specs=[pl.BlockSpec((ns, r, c), lambda i: (0, 0, 0))], out_specs=pl.BlockSpec((r, c), lambda i: (0, 0)),
        out_shape=SDS((r, c), F32), compiler_params=_cparams(("arbitrary",)),
    )(slots)


def _adam_params(params, name):
    n = len(params)

    def body(*refs):
        ins, outs = refs[:5 * n], refs[5 * n:]
        for p in range(n):
            w_ref, m_ref, v_ref, g_first, g_rest = ins[5 * p:5 * p + 5]
            go_ref, d_ref, mo_ref, vo_ref = outs[4 * p:4 * p + 4]
            for part, g_ref in ((slice(0, 1), g_first), (slice(1, w_ref.shape[0]), g_rest)):
                g = g_ref[...]
                mn = ADAM_B1 * m_ref[part] + (1.0 - ADAM_B1) * g
                vn = ADAM_B2 * v_ref[part] + (1.0 - ADAM_B2) * (g * g)
                m_hat = mn / (1.0 - ADAM_B1 ** ADAM_STEP)
                v_hat = vn / (1.0 - ADAM_B2 ** ADAM_STEP)
                go_ref[part] = g
                d_ref[part] = -ADAM_LR * (m_hat / (jnp.sqrt(v_hat) + ADAM_EPS) + ADAM_WD * w_ref[part])
                mo_ref[part] = mn
                vo_ref[part] = vn

    def whole(a):
        return pl.BlockSpec(a.shape, lambda i, nd=a.ndim: (0,) * nd)

    flat = _in_hbm([a for prm in params for a in prm])
    outs = pl.pallas_call(
        body, name=name, grid=(1,),
        in_specs=[whole(a) for a in flat],
        out_specs=[whole(prm[0]) for prm in params for _ in range(4)],
        out_shape=[SDS(prm[0].shape, F32) for prm in params for _ in range(4)],
        compiler_params=_cparams(("arbitrary",)),
    )(*flat)
    return [outs[4 * p:4 * p + 4] for p in range(n)]


SMALL_NAMES = ("pre_norm", "post_norm", "ssm_a_re", "ssm_a_im", "ssm_log_dt", "ssm_b_re", "ssm_b_im", "ssm_c_re",
               "ssm_c_im", "ssm_d", "ssm_glu_b")
SSM_NAMES = ("ssm_a_re", "ssm_a_im", "ssm_log_dt", "ssm_b_re", "ssm_b_im", "ssm_c_re", "ssm_c_im")
WEIGHT_ORDER = ("pre_norm", "post_norm", "even_w_in", "even_w_out", "ssm_a_re", "ssm_a_im", "ssm_log_dt", "ssm_b_re",
                "ssm_b_im", "ssm_c_re", "ssm_c_im", "ssm_d", "ssm_glu_w", "ssm_glu_b", "odd_w_in", "pool_w",
                "pool_scale", "odd_w_out")
PACK_ROWS_ALIGN = 8


def _pack(parts):
    flat = jnp.concatenate([p.reshape(-1).astype(F32) for p in parts])
    rows = -(-flat.shape[0] // (LANES * PACK_ROWS_ALIGN)) * PACK_ROWS_ALIGN
    return jnp.pad(flat, (0, rows * LANES - flat.shape[0])).reshape(rows, LANES)


def _unpack(packed, shapes):
    flat = packed.reshape(-1)
    out, off = [], 0
    for shp in shapes:
        size = math.prod(shp)
        out.append(flat[off:off + size].reshape(shp))
        off += size
    return out


EVEN_SHARDED = ("w_in", "w_out", "glu_w")
ODD_SHARDED = ("w_in", "pool_w", "w_out")
FAMILY = {(0, "w_in"): "even_w_in", (0, "w_out"): "even_w_out", (0, "glu_w"): "ssm_glu_w",
          (1, "w_in"): "odd_w_in", (1, "pool_w"): "pool_w", (1, "w_out"): "odd_w_out"}


def _sharded_keys(layer):
    return EVEN_SHARDED if layer % 2 == 0 else ODD_SHARDED


def _local_step(x, tgt, small, get_weights, on_w, on_ssm, on_grads, zero=0.0):
    tables = _rope_tables(zero) + (_attention_bias(zero),)
    preps, prep_vjps = [], []
    for i in range(2):
        out, vjp = jax.vjp(_ssm_prep, small["ssm_a_re"][i] + zero, small["ssm_a_im"][i], small["ssm_log_dt"][i],
                           small["ssm_b_re"][i], small["ssm_b_im"][i], small["ssm_c_re"][i], small["ssm_c_im"][i])
        preps.append(tuple(out) + _scan_tables(out[0]))
        prep_vjps.append(vjp)

    def layer_args(layer, wts):
        i = layer // 2
        pre, post = _row(small["pre_norm"][layer]) + wts.get("token", 0.0), _row(small["post_norm"][layer])
        if layer % 2 == 0:
            return (pre, post, wts["w_in"], wts["late"], _row(small["ssm_glu_b"][i]), _row(small["ssm_d"][i]),
                    preps[i], tables)
        return (pre, post, wts["w_in"], wts["pool_w"], _row(wts["pool_scale"]), wts["w_out"])

    saved, args = [], []
    cur = x
    for layer in range(4):
        after = (cur,) if layer else (cur, tables[0], tables[3]) + preps[0][1:] + preps[1][1:]
        args.append(layer_args(layer, get_weights(layer, after)))
        if layer == 0:
            h = _norm_fwd(cur, args[0][0])
        if layer < 3:
            def tail(xv, yv, post, next_gain=_row(small["pre_norm"][layer + 1])):
                return tuple(_post_fwd(xv, yv, post, next_gain))
        else:
            def tail(xv, yv, post):
                return tuple(_post_fwd_loss(xv, yv, post, tgt))
        cur, h, sv = (_even_fwd if layer % 2 == 0 else _odd_fwd)(cur, h, tail, *args[layer])
        saved.append(sv)
    g, sq = cur, h
    loss = 0.5 * jnp.sum(sq) / D

    lg = [None] * 4
    token = jnp.zeros((), F32)
    for layer in reversed(range(4)):
        largs = list(args[layer])
        largs[1] = largs[1] + token
        hooks = dict(on_w=functools.partial(on_w, layer))
        ssm_grads = []
        if layer % 2 == 0:
            def ssm_hook(cotangents, layer=layer):
                ssm_grads.append(prep_vjps[layer // 2](cotangents))
                return on_ssm(layer, ssm_grads[0])

            hooks["on_ssm"] = ssm_hook
        g, lg[layer] = (_even_bwd if layer % 2 == 0 else _odd_bwd)(g, saved[layer], *largs, **hooks)
        if ssm_grads:
            lg[layer]["ssm"] = ssm_grads[0]
        token = on_grads(layer, lg[layer])
    return loss, g, token


def _to_slots(key, gfull):
    if key == "w_in":
        return gfull
    if key in ("w_out", "glu_w"):
        rr, nn = gfull.shape
        return gfull.reshape(N_DEV, rr // N_DEV, nn)
    assert key == "pool_w"
    gg, rr, nn = gfull.shape
    return gfull.reshape(gg, N_DEV, rr // N_DEV, nn).transpose(1, 0, 2, 3)


def _from_gathered(key, gat):
    if key == "w_in":
        return gat
    if key in ("w_out", "glu_w"):
        _, rr, nn = gat.shape
        return gat.reshape(N_DEV * rr, nn)
    assert key == "pool_w"
    _, gg, rr, nn = gat.shape
    return gat.transpose(1, 0, 2, 3).reshape(gg, N_DEV * rr, nn)


def kernel(x, pre_norm, post_norm, even_w_in, even_w_out, ssm_a_re, ssm_a_im, ssm_log_dt, ssm_b_re, ssm_b_im, ssm_c_re, ssm_c_im, ssm_d, ssm_glu_w, ssm_glu_b, odd_w_in, pool_w, pool_scale, odd_w_out, loss_target, m_pre_norm, m_post_norm, m_even_w_in, m_even_w_out, m_ssm_a_re, m_ssm_a_im, m_ssm_log_dt, m_ssm_b_re, m_ssm_b_im, m_ssm_c_re, m_ssm_c_im, m_ssm_d, m_ssm_glu_w, m_ssm_glu_b, m_odd_w_in, m_pool_w, m_pool_scale, m_odd_w_out, v_pre_norm, v_post_norm, v_even_w_in, v_even_w_out, v_ssm_a_re, v_ssm_a_im, v_ssm_log_dt, v_ssm_b_re, v_ssm_b_im, v_ssm_c_re, v_ssm_c_im, v_ssm_d, v_ssm_glu_w, v_ssm_glu_b, v_odd_w_in, v_pool_w, v_pool_scale, v_odd_w_out):
    w = dict(pre_norm=pre_norm, post_norm=post_norm, even_w_in=even_w_in, even_w_out=even_w_out, ssm_a_re=ssm_a_re,
             ssm_a_im=ssm_a_im, ssm_log_dt=ssm_log_dt, ssm_b_re=ssm_b_re, ssm_b_im=ssm_b_im, ssm_c_re=ssm_c_re,
             ssm_c_im=ssm_c_im, ssm_d=ssm_d, ssm_glu_w=ssm_glu_w, ssm_glu_b=ssm_glu_b, odd_w_in=odd_w_in,
             pool_w=pool_w, pool_scale=pool_scale, odd_w_out=odd_w_out)
    mom = dict(pre_norm=m_pre_norm, post_norm=m_post_norm, even_w_in=m_even_w_in, even_w_out=m_even_w_out,
               ssm_a_re=m_ssm_a_re, ssm_a_im=m_ssm_a_im, ssm_log_dt=m_ssm_log_dt, ssm_b_re=m_ssm_b_re,
               ssm_b_im=m_ssm_b_im, ssm_c_re=m_ssm_c_re, ssm_c_im=m_ssm_c_im, ssm_d=m_ssm_d, ssm_glu_w=m_ssm_glu_w,
               ssm_glu_b=m_ssm_glu_b, odd_w_in=m_odd_w_in, pool_w=m_pool_w, pool_scale=m_pool_scale,
               odd_w_out=m_odd_w_out)
    var = dict(pre_norm=v_pre_norm, post_norm=v_post_norm, even_w_in=v_even_w_in, even_w_out=v_even_w_out,
               ssm_a_re=v_ssm_a_re, ssm_a_im=v_ssm_a_im, ssm_log_dt=v_ssm_log_dt, ssm_b_re=v_ssm_b_re,
               ssm_b_im=v_ssm_b_im, ssm_c_re=v_ssm_c_re, ssm_c_im=v_ssm_c_im, ssm_d=v_ssm_d, ssm_glu_w=v_ssm_glu_w,
               ssm_glu_b=v_ssm_glu_b, odd_w_in=v_odd_w_in, pool_w=v_pool_w, pool_scale=v_pool_scale,
               odd_w_out=v_odd_w_out)
    me = _my_index()
    scale_cols = pool_scale.shape[1]

    def shards_of(layer, keys):
        i = layer // 2
        shards = [w[FAMILY[(layer % 2, k)]][i].astype(BF16) for k in keys]
        if layer % 2 == 1:
            shards.append(jnp.pad(pool_scale[i][None], ((0, PACK_ROWS_ALIGN - 1), (0, 0))))
        return shards

    def start_gather(tag, after=()):
        return _xchg_start(f"gather_start_{tag}", shards[tag], True, after)

    shards = {0: shards_of(0, EVEN_SHARDED[:1]), "0_late": shards_of(0, EVEN_SHARDED[1:])}
    shards.update({layer: shards_of(layer, _sharded_keys(layer)) for layer in (1, 2, 3)})
    gather_started = {0: start_gather(0)}
    small = {nm: w[nm] for nm in SMALL_NAMES}
    packed_names = ("pre_norm", "post_norm") + SSM_NAMES + ("ssm_d", "ssm_glu_b")
    tails = {nm: (SSM_GROUPS, SSM_STATE * SSM_GROUP) if nm in ("ssm_b_re", "ssm_b_im") else w[nm].shape[1:]
             for nm in packed_names}
    dense = lambda nm, a: a.reshape((a.shape[0],) + tails[nm])
    small_operands = {nm: tuple(dense(nm, tree[nm]) for tree in (w, mom, var)) for nm in packed_names}
    early_work = [a for nm in ("ssm_b_re", "ssm_b_im") for a in small_operands[nm]]

    def get_weights(layer, after):
        keys = EVEN_SHARDED[:1] if layer == 0 else _sharded_keys(layer)
        if layer == 0:
            after = tuple(after) + tuple(early_work)
        lands = _xchg_wait(f"gather_wait_{layer}", gather_started[layer], True, after)
        wts = {k: _from_gathered(k, gat) for k, gat in zip(keys, lands)}
        if layer % 2 == 1:
            wts["pool_scale"] = lands[-1][:, 0, :].reshape(N_DEV * scale_cols)
        if layer == 0:
            prev = gather_started["0_late"] = start_gather("0_late", after=(lands[0],))
            for later in (1, 2, 3):
                prev = gather_started[later] = start_gather(later, after=(prev[4],))
            wts["token"] = prev[4][0, 0]

            def late(after_late):
                late_lands = _xchg_wait("gather_wait_0_late", gather_started["0_late"], True, (after_late,))
                return tuple(_from_gathered(k, gat) for k, gat in zip(EVEN_SHARDED[1:], late_lands))

            wts["late"] = late
        elif layer == 2:
            wts["late"] = lambda after_late: (wts["w_out"], wts["glu_w"])
        return wts

    scatter_started = []

    def on_w(layer, gw):
        keys = tuple(k for k in _sharded_keys(layer) if k in gw)
        started = _xchg_start(f"scatter_start_{layer}_{keys[0]}", [_to_slots(k, gw[k]) for k in keys], False)
        scatter_started.append((layer, keys, started))
        return started[4]

    def wait_scatters(layers, after):
        for layer, keys, started in scatter_started:
            if layer in layers:
                lands = _xchg_wait(f"scatter_wait_{layer}_{keys[0]}", started, False, after)
                for k, land in zip(keys, lands):
                    recv[(layer, k)] = land

    layer_grads = {}
    early_started, mid_started = [], []

    def on_ssm(layer, ssm_grads):
        if layer != 0:
            return None
        mid_started.append(_xchg_start("mid_start", [_pack(list(ssm_grads))], True))
        return mid_started[0][4]

    def on_grads(layer, lg):
        layer_grads[layer] = lg
        zero = jnp.zeros((), F32)
        if layer == 1:
            lgs = layer_grads
            early = ([jnp.concatenate([lgs[l][k] for l in (1, 2, 3)], axis=0) for k in ("pre", "post")]
                     + list(lgs[2]["ssm"]) + [lgs[2]["ssm_d"], lgs[2]["glu_b"],
                                              jnp.concatenate([lgs[1]["pool_scale"], lgs[3]["pool_scale"]], axis=0)])
            early_started.append(_xchg_start("small_start", [_pack(early)], True))
            zero = zero + early_started[0][4][0, 0]
        return zero

    loss_local, grad_x, token = _local_step(x[0], loss_target[0], small, get_weights, on_w, on_ssm, on_grads,
                                            zero=gather_started[0][4][0, 0])

    lg0 = layer_grads[0]
    late_started = _xchg_start("late_start", [_pack([lg0["pre"], lg0["post"], lg0["ssm_d"], lg0["glu_b"],
                                                     loss_local.reshape(1)]) + token], True)

    def adam_family(parity, k, which, into=None):
        nm = FAMILY[(parity, k)]
        cols = w[nm].shape[-1]
        return _adam_layer(w[nm].reshape(2, -1, cols), recv[(parity + 2 * which, k)].reshape(N_DEV, -1, cols),
                           mom[nm].reshape(2, -1, cols), var[nm].reshape(2, -1, cols), which,
                           f"adam_{nm}_{which}", into)

    recv, res = {}, {}
    wait_scatters((3, 2, 1), (late_started[4],))
    for k in ODD_SHARDED:
        res[FAMILY[(1, k)]] = adam_family(1, k, 1, adam_family(1, k, 0))
    half_done = {k: adam_family(0, k, 1) for k in EVEN_SHARDED}
    odd_done = tuple(half_done[k][0] for k in EVEN_SHARDED)

    (early_slots,) = _xchg_wait("small_wait", early_started[0], True, odd_done)
    (mid_slots,) = _xchg_wait("mid_wait", mid_started[0], True, odd_done)
    early_shapes = [(w[nm].shape[0] - 1,) + tails[nm] for nm in packed_names] + [(2, N_DEV * scale_cols)]
    g_early = _unpack(_sum_slots(early_slots, "sum_small_early"), early_shapes)
    g_mid = _unpack(_sum_slots(mid_slots, "sum_small_mid"), [(1,) + tails[nm] for nm in SSM_NAMES])

    (late_slots,) = _xchg_wait("late_wait", late_started, True, (g_early[0], g_mid[0]))
    wait_scatters((0,), (late_slots,))
    for k in EVEN_SHARDED:
        res[FAMILY[(0, k)]] = adam_family(0, k, 0, half_done[k])
    for nm in FAMILY.values():
        res[nm] = [o.reshape(w[nm].shape) for o in res[nm]]

    late_names = ("pre_norm", "post_norm", "ssm_d", "ssm_glu_b")
    g_late = _unpack(_sum_slots(late_slots, "sum_small_late"), [(1,) + tails[nm] for nm in late_names] + [(1,)])
    g_first = dict(zip(late_names, g_late))
    g_first.update(zip(SSM_NAMES, g_mid))
    outs = _adam_params([small_operands[nm] + (g_first[nm], g_early[j]) for j, nm in enumerate(packed_names)],
                        "adam_small")
    for nm, four in zip(packed_names, outs):
        res[nm] = [o.reshape(w[nm].shape) for o in four]
    loss = g_late[-1].reshape(())
    g_scale = lax.dynamic_slice_in_dim(g_early[-1], me * scale_cols, scale_cols, axis=1)
    pad = ((0, PACK_ROWS_ALIGN - 2), (0, 0))
    outs = _adam(jnp.pad(pool_scale, pad), jnp.pad(g_scale, pad)[None], jnp.pad(m_pool_scale, pad),
                 jnp.pad(v_pool_scale, pad), name="adam_pool_scale")
    res["pool_scale"] = [o[:2] for o in outs]

    out = [loss, grad_x[None]]
    for kind in range(4):
        out += [res[nm][kind] for nm in WEIGHT_ORDER]
    return tuple(out)
```
